```python
import jax, jax.numpy as jnp
from jax import lax
import numpy as np

D_MODEL = 1024
BATCH = 8
SEQ = 4096
DEPTH = 1

N_MEM = 256
HG_HEADS = 4
HG_DK = 128
HG_DV = 128
ML_HEADS = 4
ML_DK = 128
ML_DV = 128
D_HG = HG_HEADS * HG_DV
D_ML = ML_HEADS * ML_DV
D_MIX = D_HG + D_ML
CHUNK = 64
ML_CONV = 4
FFN_CONV = 3
D_FF = 2816
CA_HEADS = 4
CA_DH = D_MODEL // CA_HEADS
ALPHA = (2.0 * DEPTH) ** 0.25
BETA = (8.0 * DEPTH) ** -0.25
LN_EPS = 1e-5
NEG_BIG = -1e30

IN_SIZES = (HG_HEADS * HG_DK, HG_HEADS * HG_DK, D_HG, D_HG,
            ML_HEADS * ML_DK, ML_HEADS * ML_DK, D_ML, D_ML, ML_HEADS, ML_HEADS)
IN_SPLITS = tuple(int(c) for c in np.cumsum(IN_SIZES)[:-1])
D_IN = int(sum(IN_SIZES))
FG_START = int(sum(IN_SIZES[:-1]))

kernel_name = 'hybrid_hgrn2_mlstm_deepnorm'


def layer_norm(x, g, b):
    xf = x.astype(jnp.float32)
    mu = jnp.mean(xf, axis=-1, keepdims=True)
    var = jnp.mean(jnp.square(xf - mu), axis=-1, keepdims=True)
    return ((xf - mu) * lax.rsqrt(var + LN_EPS) * g + b).astype(x.dtype)


def head_rms_norm(h, w):
    y = h * lax.rsqrt(jnp.mean(h * h, axis=-1, keepdims=True) + LN_EPS)
    return y.reshape(*h.shape[:-2], -1) * w


def head_layer_norm(h, w):
    mu = jnp.mean(h, axis=-1, keepdims=True)
    var = jnp.mean(jnp.square(h - mu), axis=-1, keepdims=True)
    y = (h - mu) * lax.rsqrt(var + LN_EPS)
    return y.reshape(*h.shape[:-2], -1) * w


def causal_dwconv(x, w, b):
    k_w = w.shape[0]
    s = x.shape[1]
    xp = jnp.pad(x, ((0, 0), (k_w - 1, 0), (0, 0)))
    y = b
    for j in range(k_w):
        y = y + xp[:, j:j + s] * w[j]
    return y


def to_chunks(t):
    bsz, s, h = t.shape[:3]
    t = t.reshape(bsz, s // CHUNK, CHUNK, h, *t.shape[3:])
    return jnp.moveaxis(jnp.moveaxis(t, 1, 0), 3, 2)


def from_chunks(t):
    t = jnp.moveaxis(jnp.moveaxis(t, 2, 3), 0, 1)
    return t.reshape(t.shape[0], t.shape[1] * t.shape[2], *t.shape[3:])


def hgrn2_chunkwise(q, k, v, log_f):
    bsz, _, h, dk = q.shape
    dv = v.shape[-1]
    mask = jnp.tril(jnp.ones((CHUNK, CHUNK), dtype=bool))

    def step(state, inp):
        q_, k_, v_, lf = inp
        b = jnp.cumsum(lf, axis=2)
        b_ref = b[:, :, CHUNK // 2 - 1:CHUNK // 2]
        attn = jnp.einsum('bhtd,bhsd->bhts', q_ * jnp.exp(b - b_ref), k_ * jnp.exp(b_ref - b))
        attn = jnp.where(mask, attn, 0.0)
        o = (jnp.einsum('bhts,bhsv->bhtv', attn, v_)
             + jnp.einsum('bhtd,bhdv->bhtv', q_ * jnp.exp(b), state))
        b_last = b[:, :, -1:]
        state = (jnp.exp(b_last)[:, :, 0, :, None] * state
                 + jnp.einsum('bhsd,bhsv->bhdv', k_ * jnp.exp(b_last - b), v_))
        return state, o

    s0 = jnp.zeros((bsz, h, dk, dv), jnp.float32)
    _, o = lax.scan(step, s0, (to_chunks(q), to_chunks(k), to_chunks(v), to_chunks(log_f)))
    return from_chunks(o)


def mlstm_chunkwise(q, k, v, i_log, f_log):
    bsz, _, h, dk = q.shape
    dv = v.shape[-1]
    mask = jnp.tril(jnp.ones((CHUNK, CHUNK), dtype=bool))

    def step(carry, inp):
        c_st, n_st, m_st = carry
        q_, k_, v_, ig, lf = inp
        b = jnp.cumsum(lf, axis=-1)
        g = b[..., -1]
        d = jnp.where(mask, b[..., :, None] - b[..., None, :] + ig[..., None, :], -jnp.inf)
        inter = b + m_st[..., None]
        m_t = jnp.maximum(inter, jnp.max(d, axis=-1))
        w = jnp.exp(d - m_t[..., None])
        s = jnp.einsum('bhtd,bhsd->bhts', q_, k_) * w
        w_inter = jnp.exp(inter - m_t)
        num = (jnp.einsum('bhts,bhsv->bhtv', s, v_)
               + w_inter[..., None] * jnp.einsum('bhtd,bhdv->bhtv', q_, c_st))
        den = jnp.sum(s, axis=-1) + w_inter * jnp.einsum('bhtd,bhd->bht', q_, n_st)
        h_out = num / jnp.maximum(jnp.abs(den), jnp.exp(-m_t))[..., None]
        a = g[..., None] - b + ig
        m_new = jnp.maximum(g + m_st, jnp.max(a, axis=-1))
        decay = jnp.exp(g + m_st - m_new)
        wk = k_ * jnp.exp(a - m_new[..., None])[..., None]
        c_st = decay[..., None, None] * c_st + jnp.einsum('bhsd,bhsv->bhdv', wk, v_)
        n_st = decay[..., None] * n_st + jnp.sum(wk, axis=2)
        return (c_st, n_st, m_new), h_out

    init = (jnp.zeros((bsz, h, dk, dv), jnp.float32),
            jnp.zeros((bsz, h, dk), jnp.float32),
            jnp.full((bsz, h), NEG_BIG, jnp.float32))
    _, o = lax.scan(step, init, (to_chunks(q), to_chunks(k), to_chunks(v),
                                 to_chunks(i_log), to_chunks(f_log)))
    return from_chunks(o)


def hybrid_mixer(x, w_in, b_in, lb, hg_norm_w, ml_conv_w, ml_conv_b, ml_norm_w, w_out):
    bsz, s, _ = x.shape
    proj = x @ w_in + b_in
    hq, hf, hi, hg, mq, mk, mv, mo, mi, mf = jnp.split(proj, IN_SPLITS, axis=-1)
    f32 = lambda t: t.astype(jnp.float32)
    heads = lambda t, nh: t.reshape(bsz, s, nh, -1)
    sig = jax.nn.sigmoid(f32(hf))
    log_f = jnp.log(lb + (1.0 - lb) * sig)
    k_in = (1.0 - lb) * jax.nn.sigmoid(-f32(hf))
    o_hg = hgrn2_chunkwise(heads(jax.nn.silu(f32(hq)), HG_HEADS), heads(k_in, HG_HEADS),
                           heads(f32(hi), HG_HEADS), heads(log_f, HG_HEADS))
    o_hg = head_rms_norm(o_hg, hg_norm_w) * jax.nn.silu(f32(hg))
    qk = jax.nn.silu(f32(causal_dwconv(jnp.concatenate([mq, mk], axis=-1), ml_conv_w, ml_conv_b)))
    q_ml, k_ml = jnp.split(qk, 2, axis=-1)
    h_ml = mlstm_chunkwise(heads(q_ml, ML_HEADS) * (ML_DK ** -0.5), heads(k_ml, ML_HEADS),
                           heads(f32(mv), ML_HEADS), f32(mi), jax.nn.log_sigmoid(f32(mf)))
    o_ml = jax.nn.sigmoid(f32(mo)) * head_layer_norm(h_ml, ml_norm_w)
    y = jnp.concatenate([o_hg, o_ml], axis=-1).astype(x.dtype)
    return y @ w_out


def memory_cross_attention(x, mem, wq, wkv, wo):
    bsz, s, d = x.shape
    q = (x @ wq).reshape(bsz, s, CA_HEADS, CA_DH)
    k, v = jnp.split(mem @ wkv, 2, axis=-1)
    k = k.reshape(bsz, -1, CA_HEADS, CA_DH)
    v = v.reshape(bsz, -1, CA_HEADS, CA_DH)
    sc = jnp.einsum('bshd,bmhd->bhsm', q, k).astype(jnp.float32) * (CA_DH ** -0.5)
    p = jax.nn.softmax(sc, axis=-1).astype(v.dtype)
    o = jnp.einsum('bhsm,bmhd->bshd', p, v).reshape(bsz, s, d)
    return o @ wo


def conv_ffn(x, w_up, conv_w, conv_b, w_down):
    u = causal_dwconv(x @ w_up, conv_w, conv_b)
    gate, val = jnp.split(u, 2, axis=-1)
    return (jax.nn.gelu(gate) * val) @ w_down


def _fwd_setup_inputs(seed: int = 0) -> dict:
    key = jax.random.key(seed)
    ks = jax.random.split(key, 24)
    nrm = lambda k, shape, scale: jax.random.normal(k, shape, jnp.float32) * scale
    b_in = nrm(ks[3], (DEPTH, D_IN), 0.02)
    b_in = b_in.at[:, FG_START:].add(jnp.linspace(3.0, 6.0, ML_HEADS, dtype=jnp.float32))
    return {
        'x': nrm(ks[0], (BATCH, SEQ, D_MODEL), 1.0),
        'mem': nrm(ks[1], (BATCH, N_MEM, D_MODEL), 1.0),
        'w_in': nrm(ks[2], (DEPTH, D_MODEL, D_IN), D_MODEL ** -0.5),
        'b_in': b_in,
        'hg_lb_logits': 1.0 + nrm(ks[4], (DEPTH + 1, D_HG), 0.3),
        'hg_norm_w': 1.0 + nrm(ks[5], (DEPTH, D_HG), 0.02),
        'ml_conv_w': nrm(ks[6], (DEPTH, ML_CONV, 2 * D_ML), ML_CONV ** -0.5),
        'ml_conv_b': nrm(ks[7], (DEPTH, 2 * D_ML), 0.02),
        'ml_norm_w': 1.0 + nrm(ks[8], (DEPTH, D_ML), 0.02),
        'w_out': nrm(ks[9], (DEPTH, D_MIX, D_MODEL), BETA * D_MIX ** -0.5),
        'ln1_g': 1.0 + nrm(ks[10], (DEPTH, D_MODEL), 0.02),
        'ln1_b': nrm(ks[11], (DEPTH, D_MODEL), 0.02),
        'ca_wq': nrm(ks[12], (DEPTH, D_MODEL, D_MODEL), D_MODEL ** -0.5),
        'ca_wkv': nrm(ks[13], (DEPTH, D_MODEL, 2 * D_MODEL), D_MODEL ** -0.5),
        'ca_wo': nrm(ks[14], (DEPTH, D_MODEL, D_MODEL), BETA * D_MODEL ** -0.5),
        'ln2_g': 1.0 + nrm(ks[15], (DEPTH, D_MODEL), 0.02),
        'ln2_b': nrm(ks[16], (DEPTH, D_MODEL), 0.02),
        'ffn_w_up': nrm(ks[17], (DEPTH, D_MODEL, 2 * D_FF), D_MODEL ** -0.5),
        'ffn_conv_w': nrm(ks[18], (DEPTH, FFN_CONV, 2 * D_FF), FFN_CONV ** -0.5),
        'ffn_conv_b': nrm(ks[19], (DEPTH, 2 * D_FF), 0.02),
        'ffn_w_down': nrm(ks[20], (DEPTH, D_FF, D_MODEL), BETA * D_FF ** -0.5),
        'ln3_g': 1.0 + nrm(ks[21], (DEPTH, D_MODEL), 0.02),
        'ln3_b': nrm(ks[22], (DEPTH, D_MODEL), 0.02),
    }


def _fwd_reference(x, mem, w_in, b_in, hg_lb_logits, hg_norm_w, ml_conv_w, ml_conv_b, ml_norm_w,
              w_out, ln1_g, ln1_b, ca_wq, ca_wkv, ca_wo, ln2_g, ln2_b,
              ffn_w_up, ffn_conv_w, ffn_conv_b, ffn_w_down, ln3_g, ln3_b):
    lower_bounds = jnp.cumsum(jax.nn.softmax(hg_lb_logits.astype(jnp.float32), axis=0), axis=0)
    for l in range(DEPTH):
        mix = hybrid_mixer(x, w_in[l], b_in[l], lower_bounds[l], hg_norm_w[l],
                           ml_conv_w[l], ml_conv_b[l], ml_norm_w[l], w_out[l])
        x = layer_norm(ALPHA * x + mix, ln1_g[l], ln1_b[l])
        ca = memory_cross_attention(x, mem, ca_wq[l], ca_wkv[l], ca_wo[l])
        x = layer_norm(ALPHA * x + ca, ln2_g[l], ln2_b[l])
        ff = conv_ffn(x, ffn_w_up[l], ffn_conv_w[l], ffn_conv_b[l], ffn_w_down[l])
        x = layer_norm(ALPHA * x + ff, ln3_g[l], ln3_b[l])
    return x


import jax as _jax
import jax.numpy as _jnp

TWIN_FORMAT = 'train_step'
FWD_PARAMS = ['x', 'mem', 'w_in', 'b_in', 'hg_lb_logits', 'hg_norm_w', 'ml_conv_w', 'ml_conv_b', 'ml_norm_w', 'w_out', 'ln1_g', 'ln1_b', 'ca_wq', 'ca_wkv', 'ca_wo', 'ln2_g', 'ln2_b', 'ffn_w_up', 'ffn_conv_w', 'ffn_conv_b', 'ffn_w_down', 'ln3_g', 'ln3_b']
TWIN_WEIGHTS = ['w_in', 'b_in', 'hg_lb_logits', 'hg_norm_w', 'ml_conv_w', 'ml_conv_b', 'ml_norm_w', 'w_out', 'ln1_g', 'ln1_b', 'ca_wq', 'ca_wkv', 'ca_wo', 'ln2_g', 'ln2_b', 'ffn_w_up', 'ffn_conv_w', 'ffn_conv_b', 'ffn_w_down', 'ln3_g', 'ln3_b']
TWIN_DIFF_INPUT = 'x'
TWIN_INPUTS = ['x', 'mem', 'w_in', 'b_in', 'hg_lb_logits', 'hg_norm_w', 'ml_conv_w', 'ml_conv_b', 'ml_norm_w', 'w_out', 'ln1_g', 'ln1_b', 'ca_wq', 'ca_wkv', 'ca_wo', 'ln2_g', 'ln2_b', 'ffn_w_up', 'ffn_conv_w', 'ffn_conv_b', 'ffn_w_down', 'ln3_g', 'ln3_b', 'loss_target', 'm_w_in', 'm_b_in', 'm_hg_lb_logits', 'm_hg_norm_w', 'm_ml_conv_w', 'm_ml_conv_b', 'm_ml_norm_w', 'm_w_out', 'm_ln1_g', 'm_ln1_b', 'm_ca_wq', 'm_ca_wkv', 'm_ca_wo', 'm_ln2_g', 'm_ln2_b', 'm_ffn_w_up', 'm_ffn_conv_w', 'm_ffn_conv_b', 'm_ffn_w_down', 'm_ln3_g', 'm_ln3_b', 'v_w_in', 'v_b_in', 'v_hg_lb_logits', 'v_hg_norm_w', 'v_ml_conv_w', 'v_ml_conv_b', 'v_ml_norm_w', 'v_w_out', 'v_ln1_g', 'v_ln1_b', 'v_ca_wq', 'v_ca_wkv', 'v_ca_wo', 'v_ln2_g', 'v_ln2_b', 'v_ffn_w_up', 'v_ffn_conv_w', 'v_ffn_conv_b', 'v_ffn_w_down', 'v_ln3_g', 'v_ln3_b']
TWIN_OUTPUTS = ['loss', 'grad_x', 'grad_w_in', 'grad_b_in', 'grad_hg_lb_logits', 'grad_hg_norm_w', 'grad_ml_conv_w', 'grad_ml_conv_b', 'grad_ml_norm_w', 'grad_w_out', 'grad_ln1_g', 'grad_ln1_b', 'grad_ca_wq', 'grad_ca_wkv', 'grad_ca_wo', 'grad_ln2_g', 'grad_ln2_b', 'grad_ffn_w_up', 'grad_ffn_conv_w', 'grad_ffn_conv_b', 'grad_ffn_w_down', 'grad_ln3_g', 'grad_ln3_b', 'delta_w_in', 'delta_b_in', 'delta_hg_lb_logits', 'delta_hg_norm_w', 'delta_ml_conv_w', 'delta_ml_conv_b', 'delta_ml_norm_w', 'delta_w_out', 'delta_ln1_g', 'delta_ln1_b', 'delta_ca_wq', 'delta_ca_wkv', 'delta_ca_wo', 'delta_ln2_g', 'delta_ln2_b', 'delta_ffn_w_up', 'delta_ffn_conv_w', 'delta_ffn_conv_b', 'delta_ffn_w_down', 'delta_ln3_g', 'delta_ln3_b', 'new_m_w_in', 'new_m_b_in', 'new_m_hg_lb_logits', 'new_m_hg_norm_w', 'new_m_ml_conv_w', 'new_m_ml_conv_b', 'new_m_ml_norm_w', 'new_m_w_out', 'new_m_ln1_g', 'new_m_ln1_b', 'new_m_ca_wq', 'new_m_ca_wkv', 'new_m_ca_wo', 'new_m_ln2_g', 'new_m_ln2_b', 'new_m_ffn_w_up', 'new_m_ffn_conv_w', 'new_m_ffn_conv_b', 'new_m_ffn_w_down', 'new_m_ln3_g', 'new_m_ln3_b', 'new_v_w_in', 'new_v_b_in', 'new_v_hg_lb_logits', 'new_v_hg_norm_w', 'new_v_ml_conv_w', 'new_v_ml_conv_b', 'new_v_ml_norm_w', 'new_v_w_out', 'new_v_ln1_g', 'new_v_ln1_b', 'new_v_ca_wq', 'new_v_ca_wkv', 'new_v_ca_wo', 'new_v_ln2_g', 'new_v_ln2_b', 'new_v_ffn_w_up', 'new_v_ffn_conv_w', 'new_v_ffn_conv_b', 'new_v_ffn_w_down', 'new_v_ln3_g', 'new_v_ln3_b']
TWIN_LEAF_KINDS = {'loss': 'loss', 'grad_x': 'grad_x', 'grad_w_in': 'grad_w', 'grad_b_in': 'grad_w', 'grad_hg_lb_logits': 'grad_w', 'grad_hg_norm_w': 'grad_w', 'grad_ml_conv_w': 'grad_w', 'grad_ml_conv_b': 'grad_w', 'grad_ml_norm_w': 'grad_w', 'grad_w_out': 'grad_w', 'grad_ln1_g': 'grad_w', 'grad_ln1_b': 'grad_w', 'grad_ca_wq': 'grad_w', 'grad_ca_wkv': 'grad_w', 'grad_ca_wo': 'grad_w', 'grad_ln2_g': 'grad_w', 'grad_ln2_b': 'grad_w', 'grad_ffn_w_up': 'grad_w', 'grad_ffn_conv_w': 'grad_w', 'grad_ffn_conv_b': 'grad_w', 'grad_ffn_w_down': 'grad_w', 'grad_ln3_g': 'grad_w', 'grad_ln3_b': 'grad_w', 'delta_w_in': 'delta_w', 'delta_b_in': 'delta_w', 'delta_hg_lb_logits': 'delta_w', 'delta_hg_norm_w': 'delta_w', 'delta_ml_conv_w': 'delta_w', 'delta_ml_conv_b': 'delta_w', 'delta_ml_norm_w': 'delta_w', 'delta_w_out': 'delta_w', 'delta_ln1_g': 'delta_w', 'delta_ln1_b': 'delta_w', 'delta_ca_wq': 'delta_w', 'delta_ca_wkv': 'delta_w', 'delta_ca_wo': 'delta_w', 'delta_ln2_g': 'delta_w', 'delta_ln2_b': 'delta_w', 'delta_ffn_w_up': 'delta_w', 'delta_ffn_conv_w': 'delta_w', 'delta_ffn_conv_b': 'delta_w', 'delta_ffn_w_down': 'delta_w', 'delta_ln3_g': 'delta_w', 'delta_ln3_b': 'delta_w', 'new_m_w_in': 'new_m', 'new_m_b_in': 'new_m', 'new_m_hg_lb_logits': 'new_m', 'new_m_hg_norm_w': 'new_m', 'new_m_ml_conv_w': 'new_m', 'new_m_ml_conv_b': 'new_m', 'new_m_ml_norm_w': 'new_m', 'new_m_w_out': 'new_m', 'new_m_ln1_g': 'new_m', 'new_m_ln1_b': 'new_m', 'new_m_ca_wq': 'new_m', 'new_m_ca_wkv': 'new_m', 'new_m_ca_wo': 'new_m', 'new_m_ln2_g': 'new_m', 'new_m_ln2_b': 'new_m', 'new_m_ffn_w_up': 'new_m', 'new_m_ffn_conv_w': 'new_m', 'new_m_ffn_conv_b': 'new_m', 'new_m_ffn_w_down': 'new_m', 'new_m_ln3_g': 'new_m', 'new_m_ln3_b': 'new_m', 'new_v_w_in': 'new_v', 'new_v_b_in': 'new_v', 'new_v_hg_lb_logits': 'new_v', 'new_v_hg_norm_w': 'new_v', 'new_v_ml_conv_w': 'new_v', 'new_v_ml_conv_b': 'new_v', 'new_v_ml_norm_w': 'new_v', 'new_v_w_out': 'new_v', 'new_v_ln1_g': 'new_v', 'new_v_ln1_b': 'new_v', 'new_v_ca_wq': 'new_v', 'new_v_ca_wkv': 'new_v', 'new_v_ca_wo': 'new_v', 'new_v_ln2_g': 'new_v', 'new_v_ln2_b': 'new_v', 'new_v_ffn_w_up': 'new_v', 'new_v_ffn_conv_w': 'new_v', 'new_v_ffn_conv_b': 'new_v', 'new_v_ffn_w_down': 'new_v', 'new_v_ln3_g': 'new_v', 'new_v_ln3_b': 'new_v'}


def _forward(args):
    return _fwd_reference(*[args[k] for k in FWD_PARAMS])


def _output_shape():
    out = _jax.eval_shape(lambda: _forward(_fwd_setup_inputs(0)))
    return out.shape, out.dtype

N_MICROBATCH = 1
ADAM_LR = 0.001
ADAM_B1 = 0.9
ADAM_B2 = 0.999
ADAM_EPS = 1e-08
ADAM_WD = 0.01
ADAM_STEP = 10
PER_EXAMPLE_BATCH_AXIS = {'x': 0, 'mem': 0, 'loss_target': 0}
SHARED_INPUTS = []
_WEIGHT_DTYPES = {'w_in': _jnp.float32, 'b_in': _jnp.float32, 'hg_lb_logits': _jnp.float32, 'hg_norm_w': _jnp.float32, 'ml_conv_w': _jnp.float32, 'ml_conv_b': _jnp.float32, 'ml_norm_w': _jnp.float32, 'w_out': _jnp.float32, 'ln1_g': _jnp.float32, 'ln1_b': _jnp.float32, 'ca_wq': _jnp.float32, 'ca_wkv': _jnp.float32, 'ca_wo': _jnp.float32, 'ln2_g': _jnp.float32, 'ln2_b': _jnp.float32, 'ffn_w_up': _jnp.float32, 'ffn_conv_w': _jnp.float32, 'ffn_conv_b': _jnp.float32, 'ffn_w_down': _jnp.float32, 'ln3_g': _jnp.float32, 'ln3_b': _jnp.float32}
MOMENT_SCALE = {'w_in': 4.158375e-02, 'b_in': 3.107494e-01, 'hg_lb_logits': 4.961707e-03, 'hg_norm_w': 6.123753e-02, 'ml_conv_w': 3.368085e-02, 'ml_conv_b': 3.445159e-02, 'ml_norm_w': 5.170702e-02, 'w_out': 8.838510e-02, 'ln1_g': 9.492659e-01, 'ln1_b': 4.408481e-01, 'ca_wq': 9.759196e-03, 'ca_wkv': 1.036396e-02, 'ca_wo': 1.819813e-02, 'ln2_g': 9.483510e-01, 'ln2_b': 4.396261e-01, 'ffn_w_up': 3.337320e-02, 'ffn_conv_w': 3.463195e-02, 'ffn_conv_b': 4.015573e-02, 'ffn_w_down': 9.189070e-02, 'ln3_g': 3.205528e+01, 'ln3_b': 1.342211e+00}


def _to_microbatches(a, axis):
    t = _jnp.moveaxis(a, axis, 0)
    t = t.reshape((N_MICROBATCH, t.shape[0] // N_MICROBATCH) + t.shape[1:])
    return _jnp.moveaxis(t, 1, axis + 1)


def setup_inputs(seed: int = 0) -> dict:
    inp = _fwd_setup_inputs(seed)
    key = _jax.random.fold_in(_jax.random.key(seed), 7919)
    shape, _ = _output_shape()
    out = dict(inp)
    out["loss_target"] = _jax.random.normal(_jax.random.fold_in(key, 0), shape, _jnp.float32)
    for i, name in enumerate(TWIN_WEIGHTS):
        w = inp[name].astype(_jnp.float32)
        if MOMENT_SCALE is None:
            s = _jnp.sqrt(_jnp.mean(_jnp.square(w)) + 1e-30)
        else:
            s = MOMENT_SCALE[name]
        km, kv = _jax.random.split(_jax.random.fold_in(key, i + 1))
        out[name] = w
        out["m_" + name] = s * _jax.random.normal(km, w.shape, _jnp.float32)
        out["v_" + name] = (s * s) * _jax.random.uniform(kv, w.shape, _jnp.float32, 0.5, 1.5)
    if N_MICROBATCH > 1:
        for name, axis in PER_EXAMPLE_BATCH_AXIS.items():
            out[name] = _to_microbatches(out[name], axis)
    return {'x': out['x'], 'mem': out['mem'], 'w_in': out['w_in'], 'b_in': out['b_in'], 'hg_lb_logits': out['hg_lb_logits'], 'hg_norm_w': out['hg_norm_w'], 'ml_conv_w': out['ml_conv_w'], 'ml_conv_b': out['ml_conv_b'], 'ml_norm_w': out['ml_norm_w'], 'w_out': out['w_out'], 'ln1_g': out['ln1_g'], 'ln1_b': out['ln1_b'], 'ca_wq': out['ca_wq'], 'ca_wkv': out['ca_wkv'], 'ca_wo': out['ca_wo'], 'ln2_g': out['ln2_g'], 'ln2_b': out['ln2_b'], 'ffn_w_up': out['ffn_w_up'], 'ffn_conv_w': out['ffn_conv_w'], 'ffn_conv_b': out['ffn_conv_b'], 'ffn_w_down': out['ffn_w_down'], 'ln3_g': out['ln3_g'], 'ln3_b': out['ln3_b'], 'loss_target': out['loss_target'], 'm_w_in': out['m_w_in'], 'm_b_in': out['m_b_in'], 'm_hg_lb_logits': out['m_hg_lb_logits'], 'm_hg_norm_w': out['m_hg_norm_w'], 'm_ml_conv_w': out['m_ml_conv_w'], 'm_ml_conv_b': out['m_ml_conv_b'], 'm_ml_norm_w': out['m_ml_norm_w'], 'm_w_out': out['m_w_out'], 'm_ln1_g': out['m_ln1_g'], 'm_ln1_b': out['m_ln1_b'], 'm_ca_wq': out['m_ca_wq'], 'm_ca_wkv': out['m_ca_wkv'], 'm_ca_wo': out['m_ca_wo'], 'm_ln2_g': out['m_ln2_g'], 'm_ln2_b': out['m_ln2_b'], 'm_ffn_w_up': out['m_ffn_w_up'], 'm_ffn_conv_w': out['m_ffn_conv_w'], 'm_ffn_conv_b': out['m_ffn_conv_b'], 'm_ffn_w_down': out['m_ffn_w_down'], 'm_ln3_g': out['m_ln3_g'], 'm_ln3_b': out['m_ln3_b'], 'v_w_in': out['v_w_in'], 'v_b_in': out['v_b_in'], 'v_hg_lb_logits': out['v_hg_lb_logits'], 'v_hg_norm_w': out['v_hg_norm_w'], 'v_ml_conv_w': out['v_ml_conv_w'], 'v_ml_conv_b': out['v_ml_conv_b'], 'v_ml_norm_w': out['v_ml_norm_w'], 'v_w_out': out['v_w_out'], 'v_ln1_g': out['v_ln1_g'], 'v_ln1_b': out['v_ln1_b'], 'v_ca_wq': out['v_ca_wq'], 'v_ca_wkv': out['v_ca_wkv'], 'v_ca_wo': out['v_ca_wo'], 'v_ln2_g': out['v_ln2_g'], 'v_ln2_b': out['v_ln2_b'], 'v_ffn_w_up': out['v_ffn_w_up'], 'v_ffn_conv_w': out['v_ffn_conv_w'], 'v_ffn_conv_b': out['v_ffn_conv_b'], 'v_ffn_w_down': out['v_ffn_w_down'], 'v_ln3_g': out['v_ln3_g'], 'v_ln3_b': out['v_ln3_b']}


def _loss(weights, diff, rest, loss_target):
    with _jax.named_scope("forward"):
        args = {**rest, TWIN_DIFF_INPUT: diff, **{k: w.astype(_WEIGHT_DTYPES[k]) for k, w in weights.items()}}
        y = _forward(args)
    with _jax.named_scope("loss_head"):
        err = _jnp.square(y.astype(_jnp.float32) - loss_target)
        return 0.5 * _jnp.sum(_jnp.mean(err, axis=-1)) if err.ndim else 0.5 * err


def _adamw(w, g, m, v):
    m = ADAM_B1 * m + (1.0 - ADAM_B1) * g
    v = ADAM_B2 * v + (1.0 - ADAM_B2) * _jnp.square(g)
    m_hat = m / (1.0 - ADAM_B1 ** ADAM_STEP)
    v_hat = v / (1.0 - ADAM_B2 ** ADAM_STEP)
    delta = -ADAM_LR * (m_hat / (_jnp.sqrt(v_hat) + ADAM_EPS) + ADAM_WD * w)
    return delta, m, v


def reference(x, mem, w_in, b_in, hg_lb_logits, hg_norm_w, ml_conv_w, ml_conv_b, ml_norm_w, w_out, ln1_g, ln1_b, ca_wq, ca_wkv, ca_wo, ln2_g, ln2_b, ffn_w_up, ffn_conv_w, ffn_conv_b, ffn_w_down, ln3_g, ln3_b, loss_target, m_w_in, m_b_in, m_hg_lb_logits, m_hg_norm_w, m_ml_conv_w, m_ml_conv_b, m_ml_norm_w, m_w_out, m_ln1_g, m_ln1_b, m_ca_wq, m_ca_wkv, m_ca_wo, m_ln2_g, m_ln2_b, m_ffn_w_up, m_ffn_conv_w, m_ffn_conv_b, m_ffn_w_down, m_ln3_g, m_ln3_b, v_w_in, v_b_in, v_hg_lb_logits, v_hg_norm_w, v_ml_conv_w, v_ml_conv_b, v_ml_norm_w, v_w_out, v_ln1_g, v_ln1_b, v_ca_wq, v_ca_wkv, v_ca_wo, v_ln2_g, v_ln2_b, v_ffn_w_up, v_ffn_conv_w, v_ffn_conv_b, v_ffn_w_down, v_ln3_g, v_ln3_b):
    given = dict(x=x, mem=mem, w_in=w_in, b_in=b_in, hg_lb_logits=hg_lb_logits, hg_norm_w=hg_norm_w, ml_conv_w=ml_conv_w, ml_conv_b=ml_conv_b, ml_norm_w=ml_norm_w, w_out=w_out, ln1_g=ln1_g, ln1_b=ln1_b, ca_wq=ca_wq, ca_wkv=ca_wkv, ca_wo=ca_wo, ln2_g=ln2_g, ln2_b=ln2_b, ffn_w_up=ffn_w_up, ffn_conv_w=ffn_conv_w, ffn_conv_b=ffn_conv_b, ffn_w_down=ffn_w_down, ln3_g=ln3_g, ln3_b=ln3_b, loss_target=loss_target, m_w_in=m_w_in, m_b_in=m_b_in, m_hg_lb_logits=m_hg_lb_logits, m_hg_norm_w=m_hg_norm_w, m_ml_conv_w=m_ml_conv_w, m_ml_conv_b=m_ml_conv_b, m_ml_norm_w=m_ml_norm_w, m_w_out=m_w_out, m_ln1_g=m_ln1_g, m_ln1_b=m_ln1_b, m_ca_wq=m_ca_wq, m_ca_wkv=m_ca_wkv, m_ca_wo=m_ca_wo, m_ln2_g=m_ln2_g, m_ln2_b=m_ln2_b, m_ffn_w_up=m_ffn_w_up, m_ffn_conv_w=m_ffn_conv_w, m_ffn_conv_b=m_ffn_conv_b, m_ffn_w_down=m_ffn_w_down, m_ln3_g=m_ln3_g, m_ln3_b=m_ln3_b, v_w_in=v_w_in, v_b_in=v_b_in, v_hg_lb_logits=v_hg_lb_logits, v_hg_norm_w=v_hg_norm_w, v_ml_conv_w=v_ml_conv_w, v_ml_conv_b=v_ml_conv_b, v_ml_norm_w=v_ml_norm_w, v_w_out=v_w_out, v_ln1_g=v_ln1_g, v_ln1_b=v_ln1_b, v_ca_wq=v_ca_wq, v_ca_wkv=v_ca_wkv, v_ca_wo=v_ca_wo, v_ln2_g=v_ln2_g, v_ln2_b=v_ln2_b, v_ffn_w_up=v_ffn_w_up, v_ffn_conv_w=v_ffn_conv_w, v_ffn_conv_b=v_ffn_conv_b, v_ffn_w_down=v_ffn_w_down, v_ln3_g=v_ln3_g, v_ln3_b=v_ln3_b)
    weights = {n: given[n] for n in TWIN_WEIGHTS}
    shared = {n: given[n] for n in SHARED_INPUTS}
    per_example = {n: given[n] for n in ['x', 'mem']}
    grad_fn = _jax.value_and_grad(_loss, argnums=(0, 1))

    def one_microbatch(ex, loss_target):
        ex = dict(ex)
        diff = ex.pop(TWIN_DIFF_INPUT)
        return grad_fn(weights, diff, {**shared, **ex}, loss_target)

    if N_MICROBATCH == 1:
        loss, (grad_w, grad_x) = one_microbatch(per_example, given["loss_target"])
    else:
        def body(carry, xs):
            loss_sum, grad_sum = carry
            l_k, (gw_k, gx_k) = one_microbatch(xs[0], xs[1])
            with _jax.named_scope("update"):
                return (loss_sum + l_k, _jax.tree.map(_jnp.add, grad_sum, gw_k)), gx_k

        init = (_jnp.zeros((), _jnp.float32), _jax.tree.map(_jnp.zeros_like, weights))
        (loss, grad_w), grad_x = _jax.lax.scan(body, init, (per_example, given["loss_target"]))
    with _jax.named_scope("update"):
        delta_w, new_m, new_v = {}, {}, {}
        for n in TWIN_WEIGHTS:
            delta_w[n], new_m[n], new_v[n] = _adamw(weights[n], grad_w[n], given["m_" + n], given["v_" + n])
    return (loss, grad_x, *[grad_w[n] for n in TWIN_WEIGHTS], *[delta_w[n] for n in TWIN_WEIGHTS],
            *[new_m[n] for n in TWIN_WEIGHTS], *[new_v[n] for n in TWIN_WEIGHTS])
```

```python
import functools

import jax
import jax.numpy as jnp
from jax import lax
from jax.experimental import pallas as pl
from jax.experimental.pallas import tpu as pltpu

F32 = jnp.float32
BF16 = jnp.bfloat16

D_MODEL = 1024
HEADS = 4
DK = 128
D_GRP = HEADS * DK
CHUNK = 64
ML_CONV = 4
FFN_CONV = 3
D_FF = 2816
CA_DH = D_MODEL // HEADS
DEPTH = 1
ALPHA = (2.0 * DEPTH) ** 0.25
LN_EPS = 1e-5
NEG_BIG = -1e30
D_IN = 8 * D_GRP + 2 * HEADS
D_IN_PAD = 8 * D_GRP + 128
ADAM_LR, ADAM_B1, ADAM_B2, ADAM_EPS, ADAM_WD, ADAM_STEP = 0.001, 0.9, 0.999, 1e-08, 0.01, 10

SUBLANES = 8
LANES = 128
VMEM_BYTES = 64 * 1024 * 1024


def _pcall(body, **kw):
    return pl.pallas_call(body, **kw)


def _params(semantics, vmem_bytes):
    limit = int(min(max(2 * vmem_bytes, 16 * 1024 * 1024), VMEM_BYTES - 8 * 1024 * 1024))
    return pltpu.CompilerParams(dimension_semantics=semantics, vmem_limit_bytes=limit)


def _nbytes(shape, dtype):
    n = 1
    for s in shape:
        n *= s
    return n * jnp.dtype(dtype).itemsize


def _dg(a, b, ca, cb):
    return lax.dot_general(a.astype(BF16), b.astype(BF16), (((ca,), (cb,)), ((), ())),
                           preferred_element_type=F32)


@jax.custom_vjp
def mm_nn(a, b):
    return _dg(a, b, 1, 0)


mm_nn.defvjp(lambda a, b: (_dg(a, b, 1, 0), (a, b)),
             lambda r, g: (_dg(g, r[1], 1, 1).astype(r[0].dtype), _dg(r[0], g, 0, 0).astype(r[1].dtype)))


@jax.custom_vjp
def mm_nt(a, b):
    return _dg(a, b, 1, 1)


mm_nt.defvjp(lambda a, b: (_dg(a, b, 1, 1), (a, b)),
             lambda r, g: (_dg(g, r[1], 1, 0).astype(r[0].dtype), _dg(g, r[0], 0, 0).astype(r[1].dtype)))


@jax.custom_vjp
def mm_tn(a, b):
    return _dg(a, b, 0, 0)


mm_tn.defvjp(lambda a, b: (_dg(a, b, 0, 0), (a, b)),
             lambda r, g: (_dg(r[1], g, 1, 1).astype(r[0].dtype), _dg(r[0], g, 1, 0).astype(r[1].dtype)))


def _hdot(a, b):
    return jnp.dot(a, b, precision=lax.Precision.HIGHEST, preferred_element_type=F32)


def _tri(n, lower):
    r = lax.broadcasted_iota(jnp.int32, (n, n), 0)
    c = lax.broadcasted_iota(jnp.int32, (n, n), 1)
    return ((r >= c) if lower else (r <= c)).astype(F32)


@jax.custom_vjp
def cumsum_rows(x):
    return _hdot(_tri(x.shape[0], True), x)


cumsum_rows.defvjp(lambda x: (_hdot(_tri(x.shape[0], True), x), None),
                   lambda _, g: (_hdot(_tri(g.shape[0], False), g),))


def _shift_impl(halo, x, d):
    xx = jnp.concatenate([halo, x], axis=0)
    return pltpu.roll(xx, d, 0)[SUBLANES:]


@functools.partial(jax.custom_vjp, nondiff_argnums=(2,))
def shift_rows(halo, x, d):
    return _shift_impl(halo, x, d)


def _shift_bwd(d, _, g):
    n = g.shape[0] + SUBLANES
    gg = jnp.concatenate([jnp.zeros((SUBLANES, g.shape[1]), g.dtype), g], axis=0)
    r = pltpu.roll(gg, n - d, 0)
    return r[:SUBLANES], r[SUBLANES:]


shift_rows.defvjp(lambda halo, x, d: (_shift_impl(halo, x, d), None), _shift_bwd)


def causal_conv(halo, x, w_rows, b):
    k = len(w_rows)
    y = b + w_rows[k - 1] * x
    for d in range(1, k):
        y = y + w_rows[k - 1 - d] * shift_rows(halo, x, d)
    return y


def _sigmoid(x):
    return 1.0 / (1.0 + jnp.exp(-x))


def _silu(x):
    return x * _sigmoid(x)


def _log_sigmoid(x):
    return jnp.minimum(x, 0.0) - jnp.log(1.0 + jnp.exp(-jnp.abs(x)))


def _pick_lane(x, j):
    lane = lax.broadcasted_iota(jnp.int32, (1, x.shape[1]), 1)
    return jnp.sum(jnp.where(lane == j, x, 0.0), axis=1, keepdims=True)


def _pick_row(x, i):
    row = lax.broadcasted_iota(jnp.int32, (x.shape[0], 1), 0)
    return jnp.sum(jnp.where(row == i, x, 0.0), axis=0, keepdims=True)


def _col_to_row(e):
    n = e.shape[0]
    eye = lax.broadcasted_iota(jnp.int32, (n, n), 0) == lax.broadcasted_iota(jnp.int32, (n, n), 1)
    return jnp.sum(jnp.where(eye, e, 0.0), axis=0, keepdims=True)


def _layer_norm(z, g, b):
    mu = jnp.mean(z, axis=-1, keepdims=True)
    zc = z - mu
    var = jnp.mean(zc * zc, axis=-1, keepdims=True)
    return zc * lax.rsqrt(var + LN_EPS) * g + b


def _hg_head(st_t, hq, hf, hi, hgate, l0, l1, nw):
    n = hq.shape[0]
    lb = _sigmoid(l0 - l1)
    q = _silu(hq)
    lf = jnp.log(lb + (1.0 - lb) * _sigmoid(hf))
    k = (1.0 - lb) * _sigmoid(-hf)
    b = cumsum_rows(lf)
    b_ref = _pick_row(b, n // 2 - 1)
    b_last = _pick_row(b, n - 1)
    attn = mm_nt(q * jnp.exp(b - b_ref), k * jnp.exp(b_ref - b))
    attn = jnp.where(_tri(n, True) > 0, attn, 0.0)
    o = mm_nn(attn, hi) + mm_nt(q * jnp.exp(b), st_t)
    st_new = jnp.exp(b_last) * st_t + mm_tn(hi, k * jnp.exp(b_last - b))
    y = o * lax.rsqrt(jnp.mean(o * o, axis=-1, keepdims=True) + LN_EPS) * nw * _silu(hgate)
    return st_new, y


def _ml_head(c_st, n_st, m_st, q, k, v, gates, og, nw, h):
    n = q.shape[0]
    ig = _pick_lane(gates, h)
    fl = _log_sigmoid(_pick_lane(gates, HEADS + h))
    qs = q * (DK ** -0.5)
    b = _pick_lane(cumsum_rows(jnp.broadcast_to(fl, (n, LANES))), 0)
    g = jnp.sum(fl, axis=0, keepdims=True)
    d = jnp.where(_tri(n, True) > 0, b + _col_to_row(ig - b), -jnp.inf)
    inter = b + m_st
    m_t = jnp.maximum(inter, jnp.max(d, axis=1, keepdims=True))
    s = mm_nt(qs, k) * jnp.exp(d - m_t)
    w_inter = jnp.exp(inter - m_t)
    num = mm_nn(s, v) + w_inter * mm_nn(qs, c_st)
    den = jnp.sum(s, axis=1, keepdims=True) + w_inter * jnp.sum(qs * n_st, axis=1, keepdims=True)
    h_out = num / jnp.maximum(jnp.abs(den), jnp.exp(-m_t))
    a = g - b + ig
    m_new = jnp.maximum(g + m_st, jnp.max(a, axis=0, keepdims=True))
    decay = jnp.exp(g + m_st - m_new)
    wk = k * jnp.exp(a - m_new)
    c_new = decay * c_st + mm_tn(wk, v)
    n_new = decay * n_st + jnp.sum(wk, axis=0, keepdims=True)
    mu = jnp.mean(h_out, axis=-1, keepdims=True)
    hc = h_out - mu
    var = jnp.mean(hc * hc, axis=-1, keepdims=True)
    y = _sigmoid(og) * (hc * lax.rsqrt(var + LN_EPS) * nw)
    return c_new, n_new, m_new, y


def _qk_conv(halo, x, w0, w1, w2, w3, b):
    return _silu(causal_conv(halo, x, (w0, w1, w2, w3), b))


def _grp(i, h=None):
    if h is None:
        return pl.ds(i * D_GRP, D_GRP)
    return pl.ds(i * D_GRP + h * DK, DK)


def _mixer_specs(n_chunks, reverse):
    def chunk(c):
        return n_chunks - 1 - c if reverse else c
    row8 = CHUNK // SUBLANES
    proj_spec = pl.BlockSpec((CHUNK, D_IN_PAD), lambda c: (chunk(c), 0))
    halo_spec = pl.BlockSpec((SUBLANES, 2 * D_GRP), lambda c: (jnp.maximum(chunk(c) * row8 - 1, 0), 2))
    small = [pl.BlockSpec((2, D_GRP), lambda c: (0, 0)), pl.BlockSpec((1, D_GRP), lambda c: (0, 0)),
             pl.BlockSpec((ML_CONV, 2 * D_GRP), lambda c: (0, 0)), pl.BlockSpec((1, 2 * D_GRP), lambda c: (0, 0)),
             pl.BlockSpec((1, D_GRP), lambda c: (0, 0))]
    state_specs = [pl.BlockSpec((1, HEADS, DK, DK), lambda c: (chunk(c), 0, 0, 0)),
                   pl.BlockSpec((1, HEADS, DK, DK), lambda c: (chunk(c), 0, 0, 0)),
                   pl.BlockSpec((1, HEADS, 1, DK), lambda c: (chunk(c), 0, 0, 0)),
                   pl.BlockSpec((1, HEADS, 1, DK), lambda c: (chunk(c), 0, 0, 0))]
    y_spec = pl.BlockSpec((CHUNK, 2 * D_GRP), lambda c: (chunk(c), 0))
    return proj_spec, halo_spec, small, state_specs, y_spec, chunk


def _mixer_fwd(proj, lb_logits, hg_nw, conv_w, conv_b, ml_nw):
    seq = proj.shape[0]
    n_chunks = seq // CHUNK
    proj_spec, halo_spec, small, state_specs, y_spec, _ = _mixer_specs(n_chunks, False)

    def body(proj_ref, halo_ref, lg_ref, hnw_ref, cw_ref, cb_ref, mnw_ref,
             y_ref, hst_ref, cst_ref, nst_ref, mst_ref, hs, cs, ns, ms):
        c = pl.program_id(0)

        @pl.when(c == 0)
        def _():
            hs[...] = jnp.zeros_like(hs)
            cs[...] = jnp.zeros_like(cs)
            ns[...] = jnp.zeros_like(ns)
            ms[...] = jnp.full(ms.shape, NEG_BIG, F32)

        hst_ref[0] = hs[...]
        cst_ref[0] = cs[...]
        nst_ref[0] = ns[...]
        mst_ref[0] = ms[...]
        halo = jnp.where(c > 0, halo_ref[...], 0.0)
        qk = _qk_conv(halo, proj_ref[:, pl.ds(4 * D_GRP, 2 * D_GRP)],
                      cw_ref[0:1, :], cw_ref[1:2, :], cw_ref[2:3, :], cw_ref[3:4, :], cb_ref[...])
        gates = proj_ref[:, pl.ds(8 * D_GRP, LANES)]
        for h in range(HEADS):
            hd = pl.ds(h * DK, DK)
            st_new, y = _hg_head(hs[h], proj_ref[:, _grp(0, h)], proj_ref[:, _grp(1, h)], proj_ref[:, _grp(2, h)],
                                 proj_ref[:, _grp(3, h)], lg_ref[0:1, hd], lg_ref[1:2, hd], hnw_ref[:, hd])
            hs[h] = st_new
            y_ref[:, hd] = y
            c_new, n_new, m_new, y = _ml_head(
                cs[h], ns[h], _pick_lane(ms[h], 0), qk[:, h * DK:(h + 1) * DK],
                qk[:, D_GRP + h * DK:D_GRP + (h + 1) * DK], proj_ref[:, _grp(6, h)], gates,
                proj_ref[:, _grp(7, h)], mnw_ref[:, hd], h)
            cs[h] = c_new
            ns[h] = n_new
            ms[h] = jnp.broadcast_to(m_new, (1, DK))
            y_ref[:, pl.ds(D_GRP + h * DK, DK)] = y

    st = jax.ShapeDtypeStruct((n_chunks, HEADS, DK, DK), F32)
    vec = jax.ShapeDtypeStruct((n_chunks, HEADS, 1, DK), F32)
    vmem = 2 * (_nbytes((CHUNK, D_IN_PAD), F32) + _nbytes((CHUNK, 2 * D_GRP), F32) + 2 * _nbytes((HEADS, DK, DK), F32)) \
        + 2 * _nbytes((HEADS, DK, DK), F32)
    return _pcall(
        body, name="mixer_fwd", grid=(n_chunks,),
        in_specs=[proj_spec, halo_spec] + small,
        out_specs=[y_spec] + state_specs,
        out_shape=[jax.ShapeDtypeStruct((seq, 2 * D_GRP), F32), st, st, vec, vec],
        scratch_shapes=[pltpu.VMEM((HEADS, DK, DK), F32), pltpu.VMEM((HEADS, DK, DK), F32),
                        pltpu.VMEM((HEADS, 1, DK), F32), pltpu.VMEM((HEADS, 1, DK), F32)],
        compiler_params=_params(("arbitrary",), vmem),
    )(proj, proj, lb_logits, hg_nw, conv_w, conv_b, ml_nw)


def _mixer_bwd(proj, dy, hst, cst, nst, mst, lb_logits, hg_nw, conv_w, conv_b, ml_nw):
    seq = proj.shape[0]
    n_chunks = seq // CHUNK
    proj_spec, halo_spec, small, state_specs, y_spec, _ = _mixer_specs(n_chunks, True)

    def body(proj_ref, halo_ref, dy_ref, hst_ref, cst_ref, nst_ref, mst_ref,
             lg_ref, hnw_ref, cw_ref, cb_ref, mnw_ref,
             dproj_ref, dlg_ref, dhnw_ref, dcw_ref, dcb_ref, dmnw_ref,
             dhs, dcs, dns, dms, dhalo, dqk):
        c = pl.program_id(0)

        @pl.when(c == 0)
        def _():
            for r in (dhs, dcs, dns, dms, dhalo, dlg_ref, dhnw_ref, dcw_ref, dcb_ref, dmnw_ref):
                r[...] = jnp.zeros_like(r)

        first = c == n_chunks - 1
        halo = jnp.where(first, 0.0, halo_ref[...])
        x_qk = proj_ref[:, pl.ds(4 * D_GRP, 2 * D_GRP)]
        conv_args = (halo, x_qk, cw_ref[0:1, :], cw_ref[1:2, :], cw_ref[2:3, :], cw_ref[3:4, :], cb_ref[...])
        qk, conv_vjp = jax.vjp(_qk_conv, *conv_args)
        gates = proj_ref[:, pl.ds(8 * D_GRP, LANES)]
        dgates = jnp.zeros((CHUNK, LANES), F32)
        for h in range(HEADS):
            hd = pl.ds(h * DK, DK)
            args = (hst_ref[0, h], proj_ref[:, _grp(0, h)], proj_ref[:, _grp(1, h)], proj_ref[:, _grp(2, h)],
                    proj_ref[:, _grp(3, h)], lg_ref[0:1, hd], lg_ref[1:2, hd], hnw_ref[:, hd])
            _, vjp = jax.vjp(_hg_head, *args)
            dst, dhq, dhf, dhi, dhg, dl0, dl1, dnw = vjp((dhs[h], dy_ref[:, hd]))
            dhs[h] = dst
            dproj_ref[:, _grp(0, h)] = dhq
            dproj_ref[:, _grp(1, h)] = dhf
            dproj_ref[:, _grp(2, h)] = dhi
            dproj_ref[:, _grp(3, h)] = dhg
            dlg_ref[0:1, hd] += dl0
            dlg_ref[1:2, hd] += dl1
            dhnw_ref[:, hd] += dnw

            margs = (cst_ref[0, h], nst_ref[0, h], _pick_lane(mst_ref[0, h], 0), qk[:, h * DK:(h + 1) * DK],
                     qk[:, D_GRP + h * DK:D_GRP + (h + 1) * DK], proj_ref[:, _grp(6, h)], gates,
                     proj_ref[:, _grp(7, h)], mnw_ref[:, hd])
            _, mvjp = jax.vjp(functools.partial(_ml_head, h=h), *margs)
            dc, dn, dm, dq, dk, dv, dg, dog, dmn = mvjp(
                (dcs[h], dns[h], _pick_lane(dms[h], 0), dy_ref[:, pl.ds(D_GRP + h * DK, DK)]))
            dcs[h] = dc
            dns[h] = dn
            dms[h] = jnp.broadcast_to(dm, (1, DK))
            dqk[:, hd] = dq
            dqk[:, pl.ds(D_GRP + h * DK, DK)] = dk
            dproj_ref[:, _grp(6, h)] = dv
            dproj_ref[:, _grp(7, h)] = dog
            dmnw_ref[:, hd] += dmn
            dgates = dgates + dg
        dproj_ref[:, pl.ds(8 * D_GRP, LANES)] = dgates
        dh, dx, dw0, dw1, dw2, dw3, db = conv_vjp(dqk[...])
        tail = jnp.concatenate([jnp.zeros((CHUNK - SUBLANES, 2 * D_GRP), F32), dhalo[...]], axis=0)
        dproj_ref[:, pl.ds(4 * D_GRP, 2 * D_GRP)] = dx + tail
        dhalo[...] = dh
        dcw_ref[0:1, :] += dw0
        dcw_ref[1:2, :] += dw1
        dcw_ref[2:3, :] += dw2
        dcw_ref[3:4, :] += dw3
        dcb_ref[...] += db

    small_out = [pl.BlockSpec((2, D_GRP), lambda c: (0, 0)), pl.BlockSpec((1, D_GRP), lambda c: (0, 0)),
                 pl.BlockSpec((ML_CONV, 2 * D_GRP), lambda c: (0, 0)), pl.BlockSpec((1, 2 * D_GRP), lambda c: (0, 0)),
                 pl.BlockSpec((1, D_GRP), lambda c: (0, 0))]
    vmem = 2 * (2 * _nbytes((CHUNK, D_IN_PAD), F32) + _nbytes((CHUNK, 2 * D_GRP), F32)
                + 2 * _nbytes((HEADS, DK, DK), F32)) + 2 * _nbytes((HEADS, DK, DK), F32) + 4 * 1024 * 1024
    return _pcall(
        body, name="mixer_bwd", grid=(n_chunks,),
        in_specs=[proj_spec, halo_spec, y_spec] + state_specs + small,
        out_specs=[proj_spec] + small_out,
        out_shape=[jax.ShapeDtypeStruct((seq, D_IN_PAD), F32), jax.ShapeDtypeStruct((2, D_GRP), F32),
                   jax.ShapeDtypeStruct((1, D_GRP), F32), jax.ShapeDtypeStruct((ML_CONV, 2 * D_GRP), F32),
                   jax.ShapeDtypeStruct((1, 2 * D_GRP), F32), jax.ShapeDtypeStruct((1, D_GRP), F32)],
        scratch_shapes=[pltpu.VMEM((HEADS, DK, DK), F32), pltpu.VMEM((HEADS, DK, DK), F32),
                        pltpu.VMEM((HEADS, 1, DK), F32), pltpu.VMEM((HEADS, 1, DK), F32),
                        pltpu.VMEM((SUBLANES, 2 * D_GRP), F32), pltpu.VMEM((CHUNK, 2 * D_GRP), F32)],
        compiler_params=_params(("arbitrary",), vmem),
    )(proj, proj, dy, hst, cst, nst, mst, lb_logits, hg_nw, conv_w, conv_b, ml_nw)


def _tile(n, prefs):
    for p in prefs:
        if n % p == 0:
            return p
    return n


def _mm(name, mode, a, b, *, bias=None, res=None, res_scale=1.0, ln=None, out_dtype=F32, tm=None, tn=None, tk=None):
    if mode == "nn":
        (m, k), n = a.shape, b.shape[1]
    elif mode == "nt":
        (m, k), n = a.shape, b.shape[0]
    else:
        (k, m), n = a.shape, b.shape[1]
    kind = ln[0] if ln else None
    tm = tm or (256 if ln else _tile(m, (512, 256, 128)))
    tn = n if ln else (tn or _tile(n, (512, 384, 256, 128)))
    tk = tk or (k if k <= D_FF else _tile(k, (1024, 512, 384, 256, 128)))
    gi, gj, gk = m // tm, n // tn, k // tk
    assert gi * tm == m and gj * tn == n and gk * tk == k, (name, m, n, k, tm, tn, tk)
    ca, cb = {"nn": (1, 0), "nt": (1, 1), "tn": (0, 0)}[mode]
    a_spec = pl.BlockSpec((tk, tm), lambda i, j, kk: (kk, i)) if mode == "tn" else pl.BlockSpec((tm, tk), lambda i, j, kk: (i, kk))
    b_spec = pl.BlockSpec((tn, tk), lambda i, j, kk: (j, kk)) if mode == "nt" else pl.BlockSpec((tk, tn), lambda i, j, kk: (kk, j))
    row_spec = pl.BlockSpec((1, tn), lambda i, j, kk: (0, j))
    blk_spec = pl.BlockSpec((tm, tn), lambda i, j, kk: (i, j))
    ins, in_specs = [a, b], [a_spec, b_spec]
    if bias is not None:
        ins.append(bias), in_specs.append(row_spec)
    if res is not None:
        ins.append(res), in_specs.append(blk_spec)
    if kind == "fwd":
        ins += [ln[1], ln[2]]
        in_specs += [row_spec, row_spec]
    elif kind == "loss":
        ins += [ln[1], ln[2], ln[3]]
        in_specs += [row_spec, row_spec, blk_spec]
    elif kind == "bwd":
        ins += [ln[1], ln[2], ln[3]]
        in_specs += [blk_spec, row_spec, row_spec]
    blk_out = jax.ShapeDtypeStruct((m, n), out_dtype)
    row_out = jax.ShapeDtypeStruct((1, n), F32)
    if kind is None:
        out_shape, out_specs = [blk_out], [blk_spec]
    elif kind == "fwd":
        out_shape, out_specs = [blk_out, blk_out], [blk_spec, blk_spec]
    else:
        out_shape, out_specs = [blk_out, row_out, row_out], [blk_spec, row_spec, row_spec]
        if kind == "loss":
            out_shape.append(jax.ShapeDtypeStruct((1, LANES), F32))
            out_specs.append(pl.BlockSpec((1, LANES), lambda i, j, kk: (0, 0)))
    n_in = len(ins)

    def body(*refs):
        in_refs, out_refs, acc_ref = refs[:n_in], refs[n_in:n_in + len(out_shape)], refs[-1]
        i, kk = pl.program_id(0), pl.program_id(2)
        extra = list(in_refs[2:])

        def epilogue(acc):
            rest = list(extra)
            if bias is not None:
                acc = acc + rest.pop(0)[...]
            if res is not None:
                acc = acc + res_scale * rest.pop(0)[...]
            if kind is None:
                out_refs[0][...] = acc.astype(out_dtype)
                return
            if kind == "fwd":
                out_refs[0][...] = acc
                out_refs[1][...] = _layer_norm(acc, rest[0][...], rest[1][...])
                return
            if kind == "loss":
                y, vjp = jax.vjp(_layer_norm, acc, rest[0][...], rest[1][...])
                err = y - rest[2][...]
                part = 0.5 * jnp.sum(jnp.sum(err * err, axis=1, keepdims=True), axis=0, keepdims=True) / n
                dz, dg, db = vjp(err / n)
            else:
                _, vjp = jax.vjp(_layer_norm, rest[0][...], rest[1][...], rest[2][...])
                dz, dg, db = vjp(acc)

            @pl.when(i == 0)
            def _():
                for r in out_refs[1:]:
                    r[...] = jnp.zeros_like(r)

            out_refs[0][...] = dz
            out_refs[1][...] += dg
            out_refs[2][...] += db
            if kind == "loss":
                out_refs[3][...] += jnp.broadcast_to(part, (1, LANES))

        prod = _dg(in_refs[0][...], in_refs[1][...], ca, cb)
        if gk == 1:
            epilogue(prod)
            return

        @pl.when(kk == 0)
        def _():
            acc_ref[...] = prod

        @pl.when(kk > 0)
        def _():
            acc_ref[...] += prod

        @pl.when(kk == gk - 1)
        def _():
            epilogue(acc_ref[...])

    vmem = 2 * (_nbytes((tm, tk), a.dtype) + _nbytes((tk, tn), b.dtype)) + (2 * len(ins) + 2 * len(out_shape) + 1) * _nbytes((tm, tn), F32)
    outs = _pcall(
        body, name=name, grid=(gi, gj, gk), in_specs=in_specs, out_specs=out_specs, out_shape=out_shape,
        scratch_shapes=[pltpu.VMEM((tm, tn) if gk > 1 else (SUBLANES, LANES), F32)],
        compiler_params=_params(("arbitrary", "arbitrary", "arbitrary"), vmem),
    )(*ins)
    return outs[0] if kind is None else outs


def _colsum(name, a):
    m, n = a.shape
    tm = _tile(m, (512, 256, 128))

    def body(a_ref, o_ref):
        @pl.when(pl.program_id(0) == 0)
        def _():
            o_ref[...] = jnp.zeros_like(o_ref)

        o_ref[...] += jnp.sum(a_ref[...].astype(F32), axis=0, keepdims=True)

    return _pcall(
        body, name=name, grid=(m // tm,), in_specs=[pl.BlockSpec((tm, n), lambda i: (i, 0))],
        out_specs=pl.BlockSpec((1, n), lambda i: (0, 0)), out_shape=jax.ShapeDtypeStruct((1, n), F32),
        compiler_params=_params(("arbitrary",), 2 * _nbytes((tm, n), a.dtype)),
    )(a)


def _attn_head(q, k, v):
    sc = mm_nt(q, k) * (CA_DH ** -0.5)
    e = jnp.exp(sc - jnp.max(sc, axis=-1, keepdims=True))
    return mm_nn(e / jnp.sum(e, axis=-1, keepdims=True), v)


def _attn_fwd(q, kv):
    seq, n_mem = q.shape[0], kv.shape[0]
    tq = _tile(seq, (512, 256, 128))

    def body(q_ref, kv_ref, o_ref):
        for h in range(HEADS):
            hd = pl.ds(h * CA_DH, CA_DH)
            o_ref[:, hd] = _attn_head(q_ref[:, hd], kv_ref[:, hd], kv_ref[:, pl.ds(D_MODEL + h * CA_DH, CA_DH)])

    return _pcall(
        body, name="attn_fwd", grid=(seq // tq,),
        in_specs=[pl.BlockSpec((tq, D_MODEL), lambda i: (i, 0)), pl.BlockSpec((n_mem, 2 * D_MODEL), lambda i: (0, 0))],
        out_specs=pl.BlockSpec((tq, D_MODEL), lambda i: (i, 0)), out_shape=jax.ShapeDtypeStruct((seq, D_MODEL), F32),
        compiler_params=_params(("arbitrary",), 4 * _nbytes((tq, D_MODEL), F32) + 2 * _nbytes((n_mem, 2 * D_MODEL), F32)),
    )(q, kv)


def _attn_bwd(q, kv, do):
    seq, n_mem = q.shape[0], kv.shape[0]
    tq = _tile(seq, (512, 256, 128))

    def body(q_ref, kv_ref, do_ref, dq_ref, dkv_ref):
        @pl.when(pl.program_id(0) == 0)
        def _():
            dkv_ref[...] = jnp.zeros_like(dkv_ref)

        for h in range(HEADS):
            hd = pl.ds(h * CA_DH, CA_DH)
            vd = pl.ds(D_MODEL + h * CA_DH, CA_DH)
            _, vjp = jax.vjp(_attn_head, q_ref[:, hd], kv_ref[:, hd], kv_ref[:, vd])
            dq, dk, dv = vjp(do_ref[:, hd])
            dq_ref[:, hd] = dq
            dkv_ref[:, hd] += dk
            dkv_ref[:, vd] += dv

    return _pcall(
        body, name="attn_bwd", grid=(seq // tq,),
        in_specs=[pl.BlockSpec((tq, D_MODEL), lambda i: (i, 0)), pl.BlockSpec((n_mem, 2 * D_MODEL), lambda i: (0, 0)),
                  pl.BlockSpec((tq, D_MODEL), lambda i: (i, 0))],
        out_specs=[pl.BlockSpec((tq, D_MODEL), lambda i: (i, 0)), pl.BlockSpec((n_mem, 2 * D_MODEL), lambda i: (0, 0))],
        out_shape=[jax.ShapeDtypeStruct((seq, D_MODEL), F32), jax.ShapeDtypeStruct((n_mem, 2 * D_MODEL), F32)],
        compiler_params=_params(("arbitrary",), 6 * _nbytes((tq, D_MODEL), F32) + 4 * _nbytes((n_mem, 2 * D_MODEL), F32)),
    )(q, kv, do)


FFN_TB = 512
FFN_TC = 256


def _ffn_mid(hg, xg, hv, xv, wg0, wg1, wg2, bg, wv0, wv1, wv2, bv):
    return jax.nn.gelu(causal_conv(hg, xg, (wg0, wg1, wg2), bg)) * causal_conv(hv, xv, (wv0, wv1, wv2), bv)


def _ffn_specs(seq, reverse):
    tb = min(FFN_TB, seq)
    nt = seq // tb
    row8 = tb // SUBLANES

    def tt(t):
        return nt - 1 - t if reverse else t
    main = pl.BlockSpec((tb, FFN_TC), lambda j, t: (tt(t), j))
    halo = pl.BlockSpec((SUBLANES, FFN_TC), lambda j, t: (jnp.maximum(tt(t) * row8 - 1, 0), j))
    wsp = pl.BlockSpec((FFN_CONV, FFN_TC), lambda j, t: (0, j))
    bsp = pl.BlockSpec((1, FFN_TC), lambda j, t: (0, j))
    return tb, nt, main, halo, wsp, bsp


def _ffn_args(c_first, ug, hg, uv, hv, wg, bg, wv, bv):
    halo_g = jnp.where(c_first, 0.0, hg[...])
    halo_v = jnp.where(c_first, 0.0, hv[...])
    return (halo_g, ug[...], halo_v, uv[...], wg[0:1, :], wg[1:2, :], wg[2:3, :], bg[...],
            wv[0:1, :], wv[1:2, :], wv[2:3, :], bv[...])


def _ffn_mid_fwd(u_gate, u_val, w_gate, b_gate, w_val, b_val):
    seq = u_gate.shape[0]
    tb, nt, main, halo, wsp, bsp = _ffn_specs(seq, False)

    def body(ug, hg, uv, hv, wg, bg, wv, bv, o_ref):
        o_ref[...] = _ffn_mid(*_ffn_args(pl.program_id(1) == 0, ug, hg, uv, hv, wg, bg, wv, bv))

    return _pcall(
        body, name="ffn_mid_fwd", grid=(D_FF // FFN_TC, nt),
        in_specs=[main, halo, main, halo, wsp, bsp, wsp, bsp], out_specs=main,
        out_shape=jax.ShapeDtypeStruct((seq, D_FF), F32),
        compiler_params=_params(("arbitrary", "arbitrary"), 12 * _nbytes((tb, FFN_TC), F32)),
    )(u_gate, u_gate, u_val, u_val, w_gate, b_gate, w_val, b_val)


def _ffn_mid_bwd(u_gate, u_val, w_gate, b_gate, w_val, b_val, dh):
    seq = u_gate.shape[0]
    tb, nt, main, halo, wsp, bsp = _ffn_specs(seq, True)

    def body(ug, hg, uv, hv, wg, bg, wv, bv, dh_ref, dug, duv, dwg, dbg, dwv, dbv, carry_g, carry_v):
        t = pl.program_id(1)

        @pl.when(t == 0)
        def _():
            for r in (dwg, dbg, dwv, dbv, carry_g, carry_v):
                r[...] = jnp.zeros_like(r)

        _, vjp = jax.vjp(_ffn_mid, *_ffn_args(t == nt - 1, ug, hg, uv, hv, wg, bg, wv, bv))
        dhg, dxg, dhv, dxv, g0, g1, g2, gb, v0, v1, v2, vb = vjp(dh_ref[...])
        zeros = jnp.zeros((tb - SUBLANES, FFN_TC), F32)
        dug[...] = dxg + jnp.concatenate([zeros, carry_g[...]], axis=0)
        duv[...] = dxv + jnp.concatenate([zeros, carry_v[...]], axis=0)
        carry_g[...] = dhg
        carry_v[...] = dhv
        for r, parts in ((dwg, (g0, g1, g2)), (dwv, (v0, v1, v2))):
            for d, p in enumerate(parts):
                r[d:d + 1, :] += p
        dbg[...] += gb
        dbv[...] += vb

    blk = jax.ShapeDtypeStruct((seq, D_FF), F32)
    wsh = jax.ShapeDtypeStruct((FFN_CONV, D_FF), F32)
    bsh = jax.ShapeDtypeStruct((1, D_FF), F32)
    return _pcall(
        body, name="ffn_mid_bwd", grid=(D_FF // FFN_TC, nt),
        in_specs=[main, halo, main, halo, wsp, bsp, wsp, bsp, main],
        out_specs=[main, main, wsp, bsp, wsp, bsp], out_shape=[blk, blk, wsh, bsh, wsh, bsh],
        scratch_shapes=[pltpu.VMEM((SUBLANES, FFN_TC), F32), pltpu.VMEM((SUBLANES, FFN_TC), F32)],
        compiler_params=_params(("arbitrary", "arbitrary"), 24 * _nbytes((tb, FFN_TC), F32)),
    )(u_gate, u_gate, u_val, u_val, w_gate, b_gate, w_val, b_val, dh)


def _adamw(name, w, g, m, v):
    rows, cols = w.shape
    tr = _tile(rows, (256, 176, 128, 64, 40, 32, 16, 8))

    def body(w_ref, g_ref, m_ref, v_ref, d_ref, nm_ref, nv_ref):
        g_ = g_ref[...]
        m_new = ADAM_B1 * m_ref[...] + (1.0 - ADAM_B1) * g_
        v_new = ADAM_B2 * v_ref[...] + (1.0 - ADAM_B2) * jnp.square(g_)
        m_hat = m_new / (1.0 - ADAM_B1 ** ADAM_STEP)
        v_hat = v_new / (1.0 - ADAM_B2 ** ADAM_STEP)
        d_ref[...] = -ADAM_LR * (m_hat / (jnp.sqrt(v_hat) + ADAM_EPS) + ADAM_WD * w_ref[...])
        nm_ref[...] = m_new
        nv_ref[...] = v_new

    spec = pl.BlockSpec((tr, cols), lambda i: (i, 0))
    sh = jax.ShapeDtypeStruct((rows, cols), F32)
    return _pcall(
        body, name=name, grid=(rows // tr,), in_specs=[spec] * 4, out_specs=[spec] * 3, out_shape=[sh] * 3,
        compiler_params=_params(("arbitrary",), 14 * _nbytes((tr, -(-cols // LANES) * LANES), F32)),
    )(w, g, m, v)


MESH = pl.DeviceIdType.MESH
ANY = pl.BlockSpec(memory_space=pl.ANY)
N_CHIPS = 4
N_DEV = 8
BF16_ROWS = 16


def _me():
    return lax.axis_index("x"), lax.axis_index("y"), lax.axis_index("c")


def _other_chips(x, y):
    return [(1 - x, y), (x, 1 - y), (1 - x, 1 - y)]


def _remote(src, dst, ssem, rsem, dev):
    return pltpu.make_async_remote_copy(src_ref=src, dst_ref=dst, send_sem=ssem, recv_sem=rsem,
                                        device_id=dev, device_id_type=MESH)


def _gather_weights(wsh):
    rows = wsh.shape[0]
    half = rows // 2

    def body(w_ref, out_ref, ssem, rsem, lsem):
        x, y, c = _me()
        k_me = 2 * x + y
        sib = (x, y, 1 - c)

        def rows_of(cc):
            return pl.ds(pl.multiple_of(cc * half, BF16_ROWS), half)

        local = pltpu.make_async_copy(w_ref, out_ref.at[k_me], lsem)
        local.start()
        chips = _other_chips(x, y)
        sends = [_remote(w_ref.at[rows_of(c)], out_ref.at[k_me, rows_of(c)], ssem.at[r], rsem.at[r], (px, py, c))
                 for r, (px, py) in enumerate(chips)]
        for cp in sends:
            cp.start()
        passed = []
        for r, (px, py) in enumerate(chips):
            blk = out_ref.at[2 * px + py, rows_of(c)]
            _remote(blk, blk, ssem.at[r], rsem.at[r], (px, py, c)).wait_recv()
            cp = _remote(blk, blk, ssem.at[N_CHIPS - 1 + r], rsem.at[N_CHIPS - 1 + r], sib)
            cp.start()
            passed.append(cp)
        for r, (px, py) in enumerate(chips):
            blk = out_ref.at[2 * px + py, rows_of(1 - c)]
            _remote(blk, blk, ssem.at[N_CHIPS - 1 + r], rsem.at[N_CHIPS - 1 + r], sib).wait_recv()
        for cp in sends + passed:
            cp.wait_send()
        local.wait()

    n_sem = 2 * (N_CHIPS - 1)
    return _pcall(
        body, name="gather_weights", in_specs=[ANY], out_specs=ANY,
        out_shape=jax.ShapeDtypeStruct((N_CHIPS, rows, LANES), wsh.dtype),
        scratch_shapes=[pltpu.SemaphoreType.DMA((n_sem,)), pltpu.SemaphoreType.DMA((n_sem,)), pltpu.SemaphoreType.DMA],
    )(wsh)


def _swap_halves(g):
    rows = g.shape[1]
    half = rows // 2

    def body(g_ref, mine_ref, theirs_ref, ssem, rsem, lsem):
        x, y, c = _me()

        def rows_of(cc):
            return pl.ds(pl.multiple_of(cc * half, SUBLANES), half)

        local = pltpu.make_async_copy(g_ref.at[:, rows_of(c), :], mine_ref, lsem)
        local.start()
        cp = _remote(g_ref.at[:, rows_of(1 - c), :], theirs_ref, ssem, rsem, (x, y, 1 - c))
        cp.start()
        cp.wait()
        local.wait()

    sh = jax.ShapeDtypeStruct((N_CHIPS, half, LANES), g.dtype)
    return _pcall(
        body, name="swap_halves", in_specs=[ANY], out_specs=[ANY, ANY], out_shape=[sh, sh],
        scratch_shapes=[pltpu.SemaphoreType.DMA, pltpu.SemaphoreType.DMA, pltpu.SemaphoreType.DMA],
    )(g)


def _scatter_chips(p):
    def body(p_ref, out_ref, ssem, rsem, lsem):
        x, y, c = _me()
        k_me = 2 * x + y
        local = pltpu.make_async_copy(p_ref.at[k_me], out_ref.at[k_me], lsem)
        local.start()
        chips = _other_chips(x, y)
        sends = [_remote(p_ref.at[2 * px + py], out_ref.at[k_me], ssem.at[r], rsem.at[r], (px, py, c))
                 for r, (px, py) in enumerate(chips)]
        for cp in sends:
            cp.start()
        for r, (px, py) in enumerate(chips):
            blk = out_ref.at[2 * px + py]
            _remote(blk, blk, ssem.at[r], rsem.at[r], (px, py, c)).wait_recv()
        for cp in sends:
            cp.wait_send()
        local.wait()

    return _pcall(
        body, name="scatter_chips", in_specs=[ANY], out_specs=ANY, out_shape=jax.ShapeDtypeStruct(p.shape, p.dtype),
        scratch_shapes=[pltpu.SemaphoreType.DMA((N_CHIPS - 1,)), pltpu.SemaphoreType.DMA((N_CHIPS - 1,)),
                        pltpu.SemaphoreType.DMA],
    )(p)


def _share_halves(r):
    def body(r_ref, out_ref, ssem, rsem, lsem):
        x, y, c = _me()
        local = pltpu.make_async_copy(r_ref, out_ref.at[c], lsem)
        local.start()
        cp = _remote(r_ref, out_ref.at[c], ssem, rsem, (x, y, 1 - c))
        cp.start()
        blk = out_ref.at[1 - c]
        _remote(blk, blk, ssem, rsem, (x, y, 1 - c)).wait_recv()
        cp.wait_send()
        local.wait()

    return _pcall(
        body, name="share_halves", in_specs=[ANY], out_specs=ANY,
        out_shape=jax.ShapeDtypeStruct((2,) + r.shape, r.dtype),
        scratch_shapes=[pltpu.SemaphoreType.DMA, pltpu.SemaphoreType.DMA, pltpu.SemaphoreType.DMA],
    )(r)


def _exchange_small(v, reduce):
    rows = v.shape[0]

    def body(v_ref, out_ref, buf, ssem, rsem):
        x, y, c = _me()
        me = 4 * x + 2 * y + c
        peers = [((x + bx) % 2, (y + by) % 2, (c + bc) % 2)
                 for bx in (0, 1) for by in (0, 1) for bc in (0, 1) if (bx, by, bc) != (0, 0, 0)]
        dst = buf if reduce else out_ref
        dst[me] = v_ref[...]
        sends = [_remote(v_ref, dst.at[me], ssem.at[r], rsem.at[r], p) for r, p in enumerate(peers)]
        for cp in sends:
            cp.start()
        for r, (px, py, pc) in enumerate(peers):
            blk = dst.at[4 * px + 2 * py + pc]
            _remote(blk, blk, ssem.at[r], rsem.at[r], (px, py, pc)).wait_recv()
        if reduce:
            acc = buf[0]
            for d in range(1, N_DEV):
                acc = acc + buf[d]
            out_ref[...] = acc
        for cp in sends:
            cp.wait_send()

    vm = pl.BlockSpec(memory_space=pltpu.VMEM)
    out_shape = jax.ShapeDtypeStruct((rows, LANES) if reduce else (N_DEV, rows, LANES), F32)
    buf_shape = (N_DEV, rows, LANES) if reduce else (SUBLANES, LANES)
    return _pcall(
        body, name="reduce_small" if reduce else "gather_small", in_specs=[vm], out_specs=vm, out_shape=out_shape,
        scratch_shapes=[pltpu.VMEM(buf_shape, F32), pltpu.SemaphoreType.DMA((N_DEV - 1,)),
                        pltpu.SemaphoreType.DMA((N_DEV - 1,))],
        compiler_params=pltpu.CompilerParams(vmem_limit_bytes=32 * 1024 * 1024),
    )(v)


ADD_ROWS = 1024


def _add_pair(mine, theirs):
    _, half, _ = mine.shape
    tr = _tile(half, (ADD_ROWS, 512, 256, 128, 64, 32, 16))

    def body(a_ref, b_ref, o32_ref, o16_ref):
        s = a_ref[...] + b_ref[...]
        o32_ref[...] = s
        o16_ref[...] = s.astype(BF16)

    spec = pl.BlockSpec((N_CHIPS, tr, LANES), lambda i: (0, i, 0))
    return _pcall(
        body, name="add_pair", grid=(half // tr,), in_specs=[spec, spec], out_specs=[spec, spec],
        out_shape=[jax.ShapeDtypeStruct(mine.shape, F32), jax.ShapeDtypeStruct(mine.shape, BF16)],
        compiler_params=_params(("arbitrary",), 8 * _nbytes((N_CHIPS, tr, LANES), F32)),
    )(mine, theirs)


def _add_chips(p32, recv):
    _, half, _ = p32.shape
    tr = _tile(half, (ADD_ROWS, 512, 256, 128, 64, 32, 16))

    def body(p_ref, r_ref, o_ref):
        x, y, _ = _me()
        k_me = 2 * x + y
        acc = None
        for k in range(N_CHIPS):
            term = jnp.where(k_me == k, p_ref[k], r_ref[k].astype(F32))
            acc = term if acc is None else acc + term
        o_ref[...] = acc

    spec = pl.BlockSpec((N_CHIPS, tr, LANES), lambda i: (0, i, 0))
    return _pcall(
        body, name="add_chips", grid=(half // tr,), in_specs=[spec, spec],
        out_specs=pl.BlockSpec((tr, LANES), lambda i: (i, 0)), out_shape=jax.ShapeDtypeStruct((half, LANES), F32),
        compiler_params=_params(("arbitrary",), 6 * _nbytes((N_CHIPS, tr, LANES), F32)),
    )(p32, recv)


def kernel(x, mem, w_in, b_in, hg_lb_logits, hg_norm_w, ml_conv_w, ml_conv_b, ml_norm_w, w_out, ln1_g, ln1_b, ca_wq, ca_wkv, ca_wo, ln2_g, ln2_b, ffn_w_up, ffn_conv_w, ffn_conv_b, ffn_w_down, ln3_g, ln3_b, loss_target, m_w_in, m_b_in, m_hg_lb_logits, m_hg_norm_w, m_ml_conv_w, m_ml_conv_b, m_ml_norm_w, m_w_out, m_ln1_g, m_ln1_b, m_ca_wq, m_ca_wkv, m_ca_wo, m_ln2_g, m_ln2_b, m_ffn_w_up, m_ffn_conv_w, m_ffn_conv_b, m_ffn_w_down, m_ln3_g, m_ln3_b, v_w_in, v_b_in, v_hg_lb_logits, v_hg_norm_w, v_ml_conv_w, v_ml_conv_b, v_ml_norm_w, v_w_out, v_ln1_g, v_ln1_b, v_ca_wq, v_ca_wkv, v_ca_wo, v_ln2_g, v_ln2_b, v_ffn_w_up, v_ffn_conv_w, v_ffn_conv_b, v_ffn_w_down, v_ln3_g, v_ln3_b):
    return _train_step(dict(locals()))


WEIGHTS = ("w_in", "b_in", "hg_lb_logits", "hg_norm_w", "ml_conv_w", "ml_conv_b", "ml_norm_w", "w_out", "ln1_g",
           "ln1_b", "ca_wq", "ca_wkv", "ca_wo", "ln2_g", "ln2_b", "ffn_w_up", "ffn_conv_w", "ffn_conv_b",
           "ffn_w_down", "ln3_g", "ln3_b")
MATRICES = ("w_in", "w_out", "ca_wq", "ca_wkv", "ca_wo", "ffn_w_up", "ffn_w_down")
COL_SHARDED = ("w_in", "ca_wkv", "ffn_w_up", "ml_conv_w", "ffn_conv_w")
SMALL = tuple(n for n in WEIGHTS if n not in MATRICES)
PART_ROWS = 16


def _part_rows(shape, lead):
    n = 1
    for s in shape[lead:]:
        n *= s
    return -(-n // (LANES * PART_ROWS)) * PART_ROWS


def _pack(arrs, dtype, lead=0, rows=None):
    parts = []
    for a in arrs:
        head = a.shape[:lead]
        flat = a.reshape(head + (-1,)).astype(dtype)
        pad = _part_rows(a.shape, lead) * LANES - flat.shape[-1]
        flat = jnp.pad(flat, [(0, 0)] * lead + [(0, pad)])
        parts.append(flat.reshape(head + (-1, LANES)))
    used = sum(p.shape[lead] for p in parts)
    if rows is not None and rows > used:
        parts.append(jnp.zeros(parts[0].shape[:lead] + (rows - used, LANES), dtype))
    return jnp.concatenate(parts, axis=lead)


def _unpack(buf, shapes):
    lead = buf.shape[:-2]
    outs, r = [], 0
    for sh in shapes:
        n = 1
        for s in sh:
            n *= s
        nr = _part_rows(sh, 0)
        flat = buf[..., r:r + nr, :].reshape(lead + (nr * LANES,))
        outs.append(flat[..., :n].reshape(lead + tuple(sh)))
        r += nr
    return outs


def _cat_cols(s):
    return jnp.moveaxis(s, 0, 1).reshape(s.shape[1], -1)


def _cat_rows(s):
    return s.reshape(-1, s.shape[-1])


def _split_cols(g):
    return jnp.moveaxis(g.reshape(g.shape[0], N_CHIPS, -1), 1, 0)


def _split_rows(g):
    return g.reshape(N_CHIPS, -1, g.shape[-1])


def _train_step(a):
    xs, mems, tgt = a["x"][0], a["mem"][0], a["loss_target"][0]
    k_me = 2 * lax.axis_index("x") + lax.axis_index("y")
    shard = {n: a[n][0] for n in MATRICES}
    shard_shapes = [shard[n].shape for n in MATRICES]
    big_rows = 2 * ADD_ROWS * -(-sum(_part_rows(s, 0) for s in shard_shapes) // (2 * ADD_ROWS))

    gathered = _gather_weights(_pack([shard[n] for n in MATRICES], BF16, rows=big_rows))
    full = {}
    for n, s in zip(MATRICES, _unpack(gathered, shard_shapes)):
        full[n] = _cat_cols(s) if n in COL_SHARDED else _cat_rows(s)
    taps = _exchange_small(_pack([a["ml_conv_w"][0], a["ffn_conv_w"][0]], F32), reduce=False)
    taps = taps.reshape((N_CHIPS, 2) + taps.shape[1:])[:, 0]
    ml_cw, ffn_cw = [_cat_cols(s) for s in _unpack(taps, [a["ml_conv_w"].shape[1:], a["ffn_conv_w"].shape[1:]])]
    w_in_f = jnp.pad(full["w_in"], ((0, 0), (0, D_IN_PAD - D_IN)))
    b_in_p = jnp.pad(a["b_in"], ((0, 0), (0, D_IN_PAD - D_IN)))
    wup_g, wup_v = full["ffn_w_up"][:, :D_FF], full["ffn_w_up"][:, D_FF:]
    cw_g, cw_v = ffn_cw[:, :D_FF], ffn_cw[:, D_FF:]
    cb_g, cb_v = a["ffn_conv_b"][:, :D_FF], a["ffn_conv_b"][:, D_FF:]
    mixer_w = (a["hg_lb_logits"], a["hg_norm_w"], ml_cw, a["ml_conv_b"], a["ml_norm_w"])

    proj = _mm("proj", "nn", xs, w_in_f, bias=b_in_p)
    y, hst, cst, nst, mst = _mixer_fwd(proj, *mixer_w)
    z1, x1 = _mm("mix_out", "nn", y, full["w_out"], res=xs, res_scale=ALPHA, ln=("fwd", a["ln1_g"], a["ln1_b"]))
    q = _mm("ca_q", "nn", x1, full["ca_wq"])
    kv = _mm("ca_kv", "nn", mems, full["ca_wkv"])
    o = _attn_fwd(q, kv)
    z2, x2 = _mm("ca_out", "nn", o, full["ca_wo"], res=x1, res_scale=ALPHA, ln=("fwd", a["ln2_g"], a["ln2_b"]))
    ug = _mm("ffn_up_gate", "nn", x2, wup_g)
    uv = _mm("ffn_up_val", "nn", x2, wup_v)
    hmid = _ffn_mid_fwd(ug, uv, cw_g, cb_g, cw_v, cb_v)
    dz3, g_ln3g, g_ln3b, loss_part = _mm("ffn_down", "nn", hmid, full["ffn_w_down"], res=x2, res_scale=ALPHA,
                                         ln=("loss", a["ln3_g"], a["ln3_b"], tgt))

    grads = {"ln3_g": g_ln3g, "ln3_b": g_ln3b}
    dhmid = _mm("d_hmid", "nt", dz3, full["ffn_w_down"])
    grads["ffn_w_down"] = _mm("g_w_down", "tn", hmid, dz3)
    dug, duv, g_cw_g, g_cb_g, g_cw_v, g_cb_v = _ffn_mid_bwd(ug, uv, cw_g, cb_g, cw_v, cb_v, dhmid)
    grads["ffn_conv_w"] = jnp.concatenate([g_cw_g, g_cw_v], axis=1)
    grads["ffn_conv_b"] = jnp.concatenate([g_cb_g, g_cb_v], axis=1)
    grads["ffn_w_up"] = jnp.concatenate([_mm("g_w_up_gate", "tn", x2, dug), _mm("g_w_up_val", "tn", x2, duv)], axis=1)
    dx2 = _mm("d_x2_gate", "nt", dug, wup_g, res=dz3, res_scale=ALPHA)
    dz2, grads["ln2_g"], grads["ln2_b"] = _mm("d_x2", "nt", duv, wup_v, res=dx2,
                                              ln=("bwd", z2, a["ln2_g"], a["ln2_b"]))
    do = _mm("d_o", "nt", dz2, full["ca_wo"])
    grads["ca_wo"] = _mm("g_wo", "tn", o, dz2)
    dq, dkv = _attn_bwd(q, kv, do)
    grads["ca_wq"] = _mm("g_wq", "tn", x1, dq)
    grads["ca_wkv"] = _mm("g_wkv", "tn", mems, dkv)
    dz1, grads["ln1_g"], grads["ln1_b"] = _mm("d_x1", "nt", dq, full["ca_wq"], res=dz2, res_scale=ALPHA,
                                              ln=("bwd", z1, a["ln1_g"], a["ln1_b"]))
    dy = _mm("d_y", "nt", dz1, full["w_out"])
    grads["w_out"] = _mm("g_w_out", "tn", y, dz1)
    (dproj, grads["hg_lb_logits"], grads["hg_norm_w"], grads["ml_conv_w"], grads["ml_conv_b"],
     grads["ml_norm_w"]) = _mixer_bwd(proj, dy, hst, cst, nst, mst, *mixer_w)
    grads["w_in"] = _mm("g_w_in", "tn", xs, dproj)[:, :D_IN]
    grads["b_in"] = _colsum("g_b_in", dproj)[:, :D_IN]
    dx = _mm("d_x", "nt", dproj, w_in_f, res=dz1, res_scale=ALPHA)

    per_chip = [_split_cols(grads[n]) if n in COL_SHARDED else _split_rows(grads[n]) for n in MATRICES]
    mine, theirs = _swap_halves(_pack(per_chip, F32, lead=1, rows=big_rows))
    p32, p16 = _add_pair(mine, theirs)
    reduced = _share_halves(_add_chips(p32, _scatter_chips(p16))).reshape(big_rows, LANES)
    for n, g in zip(MATRICES, _unpack(reduced, shard_shapes)):
        grads[n] = g

    small_shapes = [grads[n].shape for n in SMALL] + [loss_part.shape]
    summed = _unpack(_exchange_small(_pack([grads[n] for n in SMALL] + [loss_part], F32), reduce=True), small_shapes)
    loss = summed[-1][0, 0]
    for n, g in zip(SMALL, summed[:-1]):
        if n in COL_SHARDED:
            cols = a[n].shape[-1]
            g = lax.dynamic_slice_in_dim(g, k_me * cols, cols, axis=1)
        grads[n] = g

    delta, new_m, new_v = {}, {}, {}
    for n in MATRICES:
        delta[n], new_m[n], new_v[n] = _adamw("adamw_" + n, shard[n], grads[n], a["m_" + n][0], a["v_" + n][0])
    small_w = [a[n][0] if a[n].ndim == 3 else a[n] for n in SMALL]
    small_m = [a["m_" + n][0] if a[n].ndim == 3 else a["m_" + n] for n in SMALL]
    small_v = [a["v_" + n][0] if a[n].ndim == 3 else a["v_" + n] for n in SMALL]
    shapes = [w.shape for w in small_w]
    packed = [_pack(l, F32) for l in (small_w, [grads[n] for n in SMALL], small_m, small_v)]
    for out, buf in zip((delta, new_m, new_v), _adamw("adamw_small", *packed)):
        for n, v in zip(SMALL, _unpack(buf, shapes)):
            out[n] = v

    def shaped(d):
        return [d[n].reshape(a[n].shape) for n in WEIGHTS]
    return (loss, dx[None], *shaped(grads), *shaped(delta), *shaped(new_m), *shaped(new_v))
```

```python
import functools

import jax
import jax.numpy as jnp
from jax import lax
from jax.experimental import pallas as pl
from jax.experimental.pallas import tpu as pltpu

F32 = jnp.float32
BF16 = jnp.bfloat16

D_MODEL = 1024
HEADS = 4
DK = 128
D_GRP = HEADS * DK
CHUNK = 64
ML_CONV = 4
FFN_CONV = 3
D_FF = 2816
CA_DH = D_MODEL // HEADS
DEPTH = 1
ALPHA = (2.0 * DEPTH) ** 0.25
LN_EPS = 1e-5
NEG_BIG = -1e30
D_IN = 8 * D_GRP + 2 * HEADS
D_IN_PAD = 8 * D_GRP + 128
ADAM_LR, ADAM_B1, ADAM_B2, ADAM_EPS, ADAM_WD, ADAM_STEP = 0.001, 0.9, 0.999, 1e-08, 0.01, 10

SUBLANES = 8
LANES = 128
VMEM_BYTES = 64 * 1024 * 1024


def _pcall(body, **kw):
    return pl.pallas_call(body, **kw)


def _params(semantics, vmem_bytes):
    limit = int(min(max(2 * vmem_bytes, 16 * 1024 * 1024), VMEM_BYTES - 8 * 1024 * 1024))
    return pltpu.CompilerParams(dimension_semantics=semantics, vmem_limit_bytes=limit)


def _nbytes(shape, dtype):
    n = 1
    for s in shape:
        n *= s
    return n * jnp.dtype(dtype).itemsize


def _dg(a, b, ca, cb):
    return lax.dot_general(a.astype(BF16), b.astype(BF16), (((ca,), (cb,)), ((), ())),
                           preferred_element_type=F32)


@jax.custom_vjp
def mm_nn(a, b):
    return _dg(a, b, 1, 0)


mm_nn.defvjp(lambda a, b: (_dg(a, b, 1, 0), (a, b)),
             lambda r, g: (_dg(g, r[1], 1, 1).astype(r[0].dtype), _dg(r[0], g, 0, 0).astype(r[1].dtype)))


@jax.custom_vjp
def mm_nt(a, b):
    return _dg(a, b, 1, 1)


mm_nt.defvjp(lambda a, b: (_dg(a, b, 1, 1), (a, b)),
             lambda r, g: (_dg(g, r[1], 1, 0).astype(r[0].dtype), _dg(g, r[0], 0, 0).astype(r[1].dtype)))


@jax.custom_vjp
def mm_tn(a, b):
    return _dg(a, b, 0, 0)


mm_tn.defvjp(lambda a, b: (_dg(a, b, 0, 0), (a, b)),
             lambda r, g: (_dg(r[1], g, 1, 1).astype(r[0].dtype), _dg(r[0], g, 1, 0).astype(r[1].dtype)))


def _hdot(a, b):
    return jnp.dot(a, b, precision=lax.Precision.HIGHEST, preferred_element_type=F32)


def _tri(n, lower):
    r = lax.broadcasted_iota(jnp.int32, (n, n), 0)
    c = lax.broadcasted_iota(jnp.int32, (n, n), 1)
    return ((r >= c) if lower else (r <= c)).astype(F32)


@jax.custom_vjp
def cumsum_rows(x):
    return _hdot(_tri(x.shape[0], True), x)


cumsum_rows.defvjp(lambda x: (_hdot(_tri(x.shape[0], True), x), None),
                   lambda _, g: (_hdot(_tri(g.shape[0], False), g),))


def _shift_impl(halo, x, d):
    xx = jnp.concatenate([halo, x], axis=0)
    return pltpu.roll(xx, d, 0)[SUBLANES:]


@functools.partial(jax.custom_vjp, nondiff_argnums=(2,))
def shift_rows(halo, x, d):
    return _shift_impl(halo, x, d)


def _shift_bwd(d, _, g):
    n = g.shape[0] + SUBLANES
    gg = jnp.concatenate([jnp.zeros((SUBLANES, g.shape[1]), g.dtype), g], axis=0)
    r = pltpu.roll(gg, n - d, 0)
    return r[:SUBLANES], r[SUBLANES:]


shift_rows.defvjp(lambda halo, x, d: (_shift_impl(halo, x, d), None), _shift_bwd)


def causal_conv(halo, x, w_rows, b):
    k = len(w_rows)
    y = b + w_rows[k - 1] * x
    for d in range(1, k):
        y = y + w_rows[k - 1 - d] * shift_rows(halo, x, d)
    return y


def _sigmoid(x):
    return 1.0 / (1.0 + jnp.exp(-x))


def _silu(x):
    return x * _sigmoid(x)


def _log_sigmoid(x):
    return jnp.minimum(x, 0.0) - jnp.log(1.0 + jnp.exp(-jnp.abs(x)))


def _pick_lane(x, j):
    lane = lax.broadcasted_iota(jnp.int32, (1, x.shape[1]), 1)
    return jnp.sum(jnp.where(lane == j, x, 0.0), axis=1, keepdims=True)


def _pick_row(x, i):
    row = lax.broadcasted_iota(jnp.int32, (x.shape[0], 1), 0)
    return jnp.sum(jnp.where(row == i, x, 0.0), axis=0, keepdims=True)


def _col_to_row(e):
    n = e.shape[0]
    eye = lax.broadcasted_iota(jnp.int32, (n, n), 0) == lax.broadcasted_iota(jnp.int32, (n, n), 1)
    return jnp.sum(jnp.where(eye, e, 0.0), axis=0, keepdims=True)


def _layer_norm(z, g, b):
    mu = jnp.mean(z, axis=-1, keepdims=True)
    zc = z - mu
    var = jnp.mean(zc * zc, axis=-1, keepdims=True)
    return zc * lax.rsqrt(var + LN_EPS) * g + b


def _hg_head(st_t, hq, hf, hi, hgate, l0, l1, nw):
    n = hq.shape[0]
    lb = _sigmoid(l0 - l1)
    q = _silu(hq)
    lf = jnp.log(lb + (1.0 - lb) * _sigmoid(hf))
    k = (1.0 - lb) * _sigmoid(-hf)
    b = cumsum_rows(lf)
    b_ref = _pick_row(b, n // 2 - 1)
    b_last = _pick_row(b, n - 1)
    attn = mm_nt(q * jnp.exp(b - b_ref), k * jnp.exp(b_ref - b))
    attn = jnp.where(_tri(n, True) > 0, attn, 0.0)
    o = mm_nn(attn, hi) + mm_nt(q * jnp.exp(b), st_t)
    st_new = jnp.exp(b_last) * st_t + mm_tn(hi, k * jnp.exp(b_last - b))
    y = o * lax.rsqrt(jnp.mean(o * o, axis=-1, keepdims=True) + LN_EPS) * nw * _silu(hgate)
    return st_new, y


def _ml_head(c_st, n_st, m_st, q, k, v, gates, og, nw, h):
    n = q.shape[0]
    ig = _pick_lane(gates, h)
    fl = _log_sigmoid(_pick_lane(gates, HEADS + h))
    qs = q * (DK ** -0.5)
    b = _pick_lane(cumsum_rows(jnp.broadcast_to(fl, (n, LANES))), 0)
    g = jnp.sum(fl, axis=0, keepdims=True)
    d = jnp.where(_tri(n, True) > 0, b + _col_to_row(ig - b), -jnp.inf)
    inter = b + m_st
    m_t = jnp.maximum(inter, jnp.max(d, axis=1, keepdims=True))
    s = mm_nt(qs, k) * jnp.exp(d - m_t)
    w_inter = jnp.exp(inter - m_t)
    num = mm_nn(s, v) + w_inter * mm_nn(qs, c_st)
    den = jnp.sum(s, axis=1, keepdims=True) + w_inter * jnp.sum(qs * n_st, axis=1, keepdims=True)
    h_out = num / jnp.maximum(jnp.abs(den), jnp.exp(-m_t))
    a = g - b + ig
    m_new = jnp.maximum(g + m_st, jnp.max(a, axis=0, keepdims=True))
    decay = jnp.exp(g + m_st - m_new)
    wk = k * jnp.exp(a - m_new)
    c_new = decay * c_st + mm_tn(wk, v)
    n_new = decay * n_st + jnp.sum(wk, axis=0, keepdims=True)
    mu = jnp.mean(h_out, axis=-1, keepdims=True)
    hc = h_out - mu
    var = jnp.mean(hc * hc, axis=-1, keepdims=True)
    y = _sigmoid(og) * (hc * lax.rsqrt(var + LN_EPS) * nw)
    return c_new, n_new, m_new, y


def _qk_conv(halo, x, w0, w1, w2, w3, b):
    return _silu(causal_conv(halo, x, (w0, w1, w2, w3), b))


def _grp(i, h=None):
    if h is None:
        return pl.ds(i * D_GRP, D_GRP)
    return pl.ds(i * D_GRP + h * DK, DK)


def _mixer_specs(n_chunks, reverse):
    def chunk(c):
        return n_chunks - 1 - c if reverse else c
    row8 = CHUNK // SUBLANES
    proj_spec = pl.BlockSpec((CHUNK, D_IN_PAD), lambda c: (chunk(c), 0))
    halo_spec = pl.BlockSpec((SUBLANES, 2 * D_GRP), lambda c: (jnp.maximum(chunk(c) * row8 - 1, 0), 2))
    small = [pl.BlockSpec((2, D_GRP), lambda c: (0, 0)), pl.BlockSpec((1, D_GRP), lambda c: (0, 0)),
             pl.BlockSpec((ML_CONV, 2 * D_GRP), lambda c: (0, 0)), pl.BlockSpec((1, 2 * D_GRP), lambda c: (0, 0)),
             pl.BlockSpec((1, D_GRP), lambda c: (0, 0))]
    state_specs = [pl.BlockSpec((1, HEADS, DK, DK), lambda c: (chunk(c), 0, 0, 0)),
                   pl.BlockSpec((1, HEADS, DK, DK), lambda c: (chunk(c), 0, 0, 0)),
                   pl.BlockSpec((1, HEADS, 1, DK), lambda c: (chunk(c), 0, 0, 0)),
                   pl.BlockSpec((1, HEADS, 1, DK), lambda c: (chunk(c), 0, 0, 0))]
    y_spec = pl.BlockSpec((CHUNK, 2 * D_GRP), lambda c: (chunk(c), 0))
    return proj_spec, halo_spec, small, state_specs, y_spec, chunk


def _mixer_fwd(proj, lb_logits, hg_nw, conv_w, conv_b, ml_nw):
    seq = proj.shape[0]
    n_chunks = seq // CHUNK
    proj_spec, halo_spec, small, state_specs, y_spec, _ = _mixer_specs(n_chunks, False)

    def body(proj_ref, halo_ref, lg_ref, hnw_ref, cw_ref, cb_ref, mnw_ref,
             y_ref, hst_ref, cst_ref, nst_ref, mst_ref, hs, cs, ns, ms):
        c = pl.program_id(0)

        @pl.when(c == 0)
        def _():
            hs[...] = jnp.zeros_like(hs)
            cs[...] = jnp.zeros_like(cs)
            ns[...] = jnp.zeros_like(ns)
            ms[...] = jnp.full(ms.shape, NEG_BIG, F32)

        hst_ref[0] = hs[...]
        cst_ref[0] = cs[...]
        nst_ref[0] = ns[...]
        mst_ref[0] = ms[...]
        halo = jnp.where(c > 0, halo_ref[...], 0.0)
        qk = _qk_conv(halo, proj_ref[:, pl.ds(4 * D_GRP, 2 * D_GRP)],
                      cw_ref[0:1, :], cw_ref[1:2, :], cw_ref[2:3, :], cw_ref[3:4, :], cb_ref[...])
        gates = proj_ref[:, pl.ds(8 * D_GRP, LANES)]
        for h in range(HEADS):
            hd = pl.ds(h * DK, DK)
            st_new, y = _hg_head(hs[h], proj_ref[:, _grp(0, h)], proj_ref[:, _grp(1, h)], proj_ref[:, _grp(2, h)],
                                 proj_ref[:, _grp(3, h)], lg_ref[0:1, hd], lg_ref[1:2, hd], hnw_ref[:, hd])
            hs[h] = st_new
            y_ref[:, hd] = y
            c_new, n_new, m_new, y = _ml_head(
                cs[h], ns[h], _pick_lane(ms[h], 0), qk[:, h * DK:(h + 1) * DK],
                qk[:, D_GRP + h * DK:D_GRP + (h + 1) * DK], proj_ref[:, _grp(6, h)], gates,
                proj_ref[:, _grp(7, h)], mnw_ref[:, hd], h)
            cs[h] = c_new
            ns[h] = n_new
            ms[h] = jnp.broadcast_to(m_new, (1, DK))
            y_ref[:, pl.ds(D_GRP + h * DK, DK)] = y

    st = jax.ShapeDtypeStruct((n_chunks, HEADS, DK, DK), F32)
    vec = jax.ShapeDtypeStruct((n_chunks, HEADS, 1, DK), F32)
    vmem = 2 * (_nbytes((CHUNK, D_IN_PAD), F32) + _nbytes((CHUNK, 2 * D_GRP), F32) + 2 * _nbytes((HEADS, DK, DK), F32)) \
        + 2 * _nbytes((HEADS, DK, DK), F32)
    return _pcall(
        body, name="mixer_fwd", grid=(n_chunks,),
        in_specs=[proj_spec, halo_spec] + small,
        out_specs=[y_spec] + state_specs,
        out_shape=[jax.ShapeDtypeStruct((seq, 2 * D_GRP), F32), st, st, vec, vec],
        scratch_shapes=[pltpu.VMEM((HEADS, DK, DK), F32), pltpu.VMEM((HEADS, DK, DK), F32),
                        pltpu.VMEM((HEADS, 1, DK), F32), pltpu.VMEM((HEADS, 1, DK), F32)],
        compiler_params=_params(("arbitrary",), vmem),
    )(proj, proj, lb_logits, hg_nw, conv_w, conv_b, ml_nw)


def _mixer_bwd(proj, dy, hst, cst, nst, mst, lb_logits, hg_nw, conv_w, conv_b, ml_nw):
    seq = proj.shape[0]
    n_chunks = seq // CHUNK
    proj_spec, halo_spec, small, state_specs, y_spec, _ = _mixer_specs(n_chunks, True)

    def body(proj_ref, halo_ref, dy_ref, hst_ref, cst_ref, nst_ref, mst_ref,
             lg_ref, hnw_ref, cw_ref, cb_ref, mnw_ref,
             dproj_ref, dlg_ref, dhnw_ref, dcw_ref, dcb_ref, dmnw_ref,
             dhs, dcs, dns, dms, dhalo, dqk):
        c = pl.program_id(0)

        @pl.when(c == 0)
        def _():
            for r in (dhs, dcs, dns, dms, dhalo, dlg_ref, dhnw_ref, dcw_ref, dcb_ref, dmnw_ref):
                r[...] = jnp.zeros_like(r)

        first = c == n_chunks - 1
        halo = jnp.where(first, 0.0, halo_ref[...])
        x_qk = proj_ref[:, pl.ds(4 * D_GRP, 2 * D_GRP)]
        conv_args = (halo, x_qk, cw_ref[0:1, :], cw_ref[1:2, :], cw_ref[2:3, :], cw_ref[3:4, :], cb_ref[...])
        qk, conv_vjp = jax.vjp(_qk_conv, *conv_args)
        gates = proj_ref[:, pl.ds(8 * D_GRP, LANES)]
        dgates = jnp.zeros((CHUNK, LANES), F32)
        for h in range(HEADS):
            hd = pl.ds(h * DK, DK)
            args = (hst_ref[0, h], proj_ref[:, _grp(0, h)], proj_ref[:, _grp(1, h)], proj_ref[:, _grp(2, h)],
                    proj_ref[:, _grp(3, h)], lg_ref[0:1, hd], lg_ref[1:2, hd], hnw_ref[:, hd])
            _, vjp = jax.vjp(_hg_head, *args)
            dst, dhq, dhf, dhi, dhg, dl0, dl1, dnw = vjp((dhs[h], dy_ref[:, hd]))
            dhs[h] = dst
            dproj_ref[:, _grp(0, h)] = dhq
            dproj_ref[:, _grp(1, h)] = dhf
            dproj_ref[:, _grp(2, h)] = dhi
            dproj_ref[:, _grp(3, h)] = dhg
            dlg_ref[0:1, hd] += dl0
            dlg_ref[1:2, hd] += dl1
            dhnw_ref[:, hd] += dnw

            margs = (cst_ref[0, h], nst_ref[0, h], _pick_lane(mst_ref[0, h], 0), qk[:, h * DK:(h + 1) * DK],
                     qk[:, D_GRP + h * DK:D_GRP + (h + 1) * DK], proj_ref[:, _grp(6, h)], gates,
                     proj_ref[:, _grp(7, h)], mnw_ref[:, hd])
            _, mvjp = jax.vjp(functools.partial(_ml_head, h=h), *margs)
            dc, dn, dm, dq, dk, dv, dg, dog, dmn = mvjp(
                (dcs[h], dns[h], _pick_lane(dms[h], 0), dy_ref[:, pl.ds(D_GRP + h * DK, DK)]))
            dcs[h] = dc
            dns[h] = dn
            dms[h] = jnp.broadcast_to(dm, (1, DK))
            dqk[:, hd] = dq
            dqk[:, pl.ds(D_GRP + h * DK, DK)] = dk
            dproj_ref[:, _grp(6, h)] = dv
            dproj_ref[:, _grp(7, h)] = dog
            dmnw_ref[:, hd] += dmn
            dgates = dgates + dg
        dproj_ref[:, pl.ds(8 * D_GRP, LANES)] = dgates
        dh, dx, dw0, dw1, dw2, dw3, db = conv_vjp(dqk[...])
        tail = jnp.concatenate([jnp.zeros((CHUNK - SUBLANES, 2 * D_GRP), F32), dhalo[...]], axis=0)
        dproj_ref[:, pl.ds(4 * D_GRP, 2 * D_GRP)] = dx + tail
        dhalo[...] = dh
        dcw_ref[0:1, :] += dw0
        dcw_ref[1:2, :] += dw1
        dcw_ref[2:3, :] += dw2
        dcw_ref[3:4, :] += dw3
        dcb_ref[...] += db

    small_out = [pl.BlockSpec((2, D_GRP), lambda c: (0, 0)), pl.BlockSpec((1, D_GRP), lambda c: (0, 0)),
                 pl.BlockSpec((ML_CONV, 2 * D_GRP), lambda c: (0, 0)), pl.BlockSpec((1, 2 * D_GRP), lambda c: (0, 0)),
                 pl.BlockSpec((1, D_GRP), lambda c: (0, 0))]
    vmem = 2 * (2 * _nbytes((CHUNK, D_IN_PAD), F32) + _nbytes((CHUNK, 2 * D_GRP), F32)
                + 2 * _nbytes((HEADS, DK, DK), F32)) + 2 * _nbytes((HEADS, DK, DK), F32) + 4 * 1024 * 1024
    return _pcall(
        body, name="mixer_bwd", grid=(n_chunks,),
        in_specs=[proj_spec, halo_spec, y_spec] + state_specs + small,
        out_specs=[proj_spec] + small_out,
        out_shape=[jax.ShapeDtypeStruct((seq, D_IN_PAD), F32), jax.ShapeDtypeStruct((2, D_GRP), F32),
                   jax.ShapeDtypeStruct((1, D_GRP), F32), jax.ShapeDtypeStruct((ML_CONV, 2 * D_GRP), F32),
                   jax.ShapeDtypeStruct((1, 2 * D_GRP), F32), jax.ShapeDtypeStruct((1, D_GRP), F32)],
        scratch_shapes=[pltpu.VMEM((HEADS, DK, DK), F32), pltpu.VMEM((HEADS, DK, DK), F32),
                        pltpu.VMEM((HEADS, 1, DK), F32), pltpu.VMEM((HEADS, 1, DK), F32),
                        pltpu.VMEM((SUBLANES, 2 * D_GRP), F32), pltpu.VMEM((CHUNK, 2 * D_GRP), F32)],
        compiler_params=_params(("arbitrary",), vmem),
    )(proj, proj, dy, hst, cst, nst, mst, lb_logits, hg_nw, conv_w, conv_b, ml_nw)


def _tile(n, prefs, unit=None):
    unit = unit or n
    for p in prefs:
        if unit % p == 0 and n % p == 0:
            return p
    return unit


def _logical(arr):
    return arr.shape if arr.ndim == 2 else (arr.shape[1], arr.shape[0] * arr.shape[2])


def _group(arr):
    return arr.shape[-1]


def _split_spec(ndim, group, tr, tc, where):
    if ndim == 2:
        return pl.BlockSpec((tr, tc), where)
    per = group // tc
    assert per * tc == group, (group, tc)

    def index(*ids):
        bi, bj = where(*ids)
        return (bj // per, bi, bj % per)
    return pl.BlockSpec((None, tr, tc), index)


def _mm(name, mode, a, b, *, bias=None, res=None, res_scale=1.0, ln=None, out_dtype=F32, out_groups=None,
        tm=None, tn=None, tk=None):
    la, lb = _logical(a), _logical(b)
    if mode == "nn":
        (m, k), n = la, lb[1]
        n_unit, k_unit = (_group(b) if b.ndim == 3 else n), (_group(a) if a.ndim == 3 else k)
    elif mode == "nt":
        (m, k), n = la, lb[0]
        n_unit, k_unit = n, min(_group(a) if a.ndim == 3 else k, _group(b) if b.ndim == 3 else k)
    else:
        (k, m), n = la, lb[1]
        n_unit, k_unit = (_group(b) if b.ndim == 3 else n), k
        assert a.ndim == 2
    if out_groups:
        n_unit = min(n_unit, n // out_groups)
    kind = ln[0] if ln else None
    tm = tm or (256 if ln else _tile(m, (512, 256, 128)))
    tn = n if ln else (tn or _tile(n, (512, 384, 256, 128), n_unit))
    tk = tk or (k if (k <= D_FF and k_unit == k) else _tile(k, (1024, 512, 384, 256, 128), k_unit))
    gi, gj, gk = m // tm, n // tn, k // tk
    assert gi * tm == m and gj * tn == n and gk * tk == k, (name, m, n, k, tm, tn, tk)
    ca, cb = {"nn": (1, 0), "nt": (1, 1), "tn": (0, 0)}[mode]
    if mode == "tn":
        a_spec = _split_spec(a.ndim, _group(a), tk, tm, lambda i, j, kk: (kk, i))
    else:
        a_spec = _split_spec(a.ndim, _group(a), tm, tk, lambda i, j, kk: (i, kk))
    if mode == "nt":
        b_spec = _split_spec(b.ndim, _group(b), tn, tk, lambda i, j, kk: (j, kk))
    else:
        b_spec = _split_spec(b.ndim, _group(b), tk, tn, lambda i, j, kk: (kk, j))
    row_spec = pl.BlockSpec((1, tn), lambda i, j, kk: (0, j))
    blk_spec = pl.BlockSpec((tm, tn), lambda i, j, kk: (i, j))
    ins, in_specs = [a, b], [a_spec, b_spec]
    if bias is not None:
        ins.append(bias), in_specs.append(row_spec)
    if res is not None:
        ins.append(res), in_specs.append(blk_spec)
    if kind == "fwd":
        ins += [ln[1], ln[2]]
        in_specs += [row_spec, row_spec]
    elif kind == "loss":
        ins += [ln[1], ln[2], ln[3]]
        in_specs += [row_spec, row_spec, blk_spec]
    elif kind == "bwd":
        ins += [ln[1], ln[2], ln[3]]
        in_specs += [blk_spec, row_spec, row_spec]
    if out_groups:
        blk_out = jax.ShapeDtypeStruct((out_groups, m, n // out_groups), out_dtype)
        out_spec = _split_spec(3, n // out_groups, tm, tn, lambda i, j, kk: (i, j))
    else:
        blk_out, out_spec = jax.ShapeDtypeStruct((m, n), out_dtype), blk_spec
    row_out = jax.ShapeDtypeStruct((1, n), F32)
    if kind is None:
        out_shape, out_specs = [blk_out], [out_spec]
    elif kind == "fwd":
        out_shape, out_specs = [blk_out, blk_out], [blk_spec, blk_spec]
    else:
        out_shape, out_specs = [blk_out, row_out, row_out], [blk_spec, row_spec, row_spec]
        if kind == "loss":
            out_shape.append(jax.ShapeDtypeStruct((1, LANES), F32))
            out_specs.append(pl.BlockSpec((1, LANES), lambda i, j, kk: (0, 0)))
    n_in = len(ins)

    def body(*refs):
        in_refs, out_refs, acc_ref = refs[:n_in], refs[n_in:n_in + len(out_shape)], refs[-1]
        i, kk = pl.program_id(0), pl.program_id(2)
        extra = list(in_refs[2:])

        def epilogue(acc):
            rest = list(extra)
            if bias is not None:
                acc = acc + rest.pop(0)[...]
            if res is not None:
                acc = acc + res_scale * rest.pop(0)[...]
            if kind is None:
                out_refs[0][...] = acc.astype(out_dtype)
                return
            if kind == "fwd":
                out_refs[0][...] = acc
                out_refs[1][...] = _layer_norm(acc, rest[0][...], rest[1][...])
                return
            if kind == "loss":
                y, vjp = jax.vjp(_layer_norm, acc, rest[0][...], rest[1][...])
                err = y - rest[2][...]
                part = 0.5 * jnp.sum(jnp.sum(err * err, axis=1, keepdims=True), axis=0, keepdims=True) / n
                dz, dg, db = vjp(err / n)
            else:
                _, vjp = jax.vjp(_layer_norm, rest[0][...], rest[1][...], rest[2][...])
                dz, dg, db = vjp(acc)

            @pl.when(i == 0)
            def _():
                for r in out_refs[1:]:
                    r[...] = jnp.zeros_like(r)

            out_refs[0][...] = dz
            out_refs[1][...] += dg
            out_refs[2][...] += db
            if kind == "loss":
                out_refs[3][...] += jnp.broadcast_to(part, (1, LANES))

        prod = _dg(in_refs[0][...], in_refs[1][...], ca, cb)
        if gk == 1:
            epilogue(prod)
            return

        @pl.when(kk == 0)
        def _():
            acc_ref[...] = prod

        @pl.when(kk > 0)
        def _():
            acc_ref[...] += prod

        @pl.when(kk == gk - 1)
        def _():
            epilogue(acc_ref[...])

    vmem = 2 * (_nbytes((tm, tk), a.dtype) + _nbytes((tk, tn), b.dtype)) + (2 * len(ins) + 2 * len(out_shape) + 1) * _nbytes((tm, tn), F32)
    outs = _pcall(
        body, name=name, grid=(gi, gj, gk), in_specs=in_specs, out_specs=out_specs, out_shape=out_shape,
        scratch_shapes=[pltpu.VMEM((tm, tn) if gk > 1 else (SUBLANES, LANES), F32)],
        compiler_params=_params(("arbitrary", "arbitrary", "arbitrary"), vmem),
    )(*ins)
    return outs[0] if kind is None else outs


def _colsum(name, a):
    m, n = a.shape
    tm = _tile(m, (512, 256, 128))

    def body(a_ref, o_ref):
        @pl.when(pl.program_id(0) == 0)
        def _():
            o_ref[...] = jnp.zeros_like(o_ref)

        o_ref[...] += jnp.sum(a_ref[...].astype(F32), axis=0, keepdims=True)

    return _pcall(
        body, name=name, grid=(m // tm,), in_specs=[pl.BlockSpec((tm, n), lambda i: (i, 0))],
        out_specs=pl.BlockSpec((1, n), lambda i: (0, 0)), out_shape=jax.ShapeDtypeStruct((1, n), F32),
        compiler_params=_params(("arbitrary",), 2 * _nbytes((tm, n), a.dtype)),
    )(a)


def _attn_head(q, k, v):
    sc = mm_nt(q, k) * (CA_DH ** -0.5)
    e = jnp.exp(sc - jnp.max(sc, axis=-1, keepdims=True))
    return mm_nn(e / jnp.sum(e, axis=-1, keepdims=True), v)


def _attn_fwd(q, kv):
    seq, n_mem = q.shape[0], kv.shape[0]
    tq = _tile(seq, (512, 256, 128))

    def body(q_ref, kv_ref, o_ref):
        for h in range(HEADS):
            hd = pl.ds(h * CA_DH, CA_DH)
            o_ref[:, hd] = _attn_head(q_ref[:, hd], kv_ref[:, hd], kv_ref[:, pl.ds(D_MODEL + h * CA_DH, CA_DH)])

    return _pcall(
        body, name="attn_fwd", grid=(seq // tq,),
        in_specs=[pl.BlockSpec((tq, D_MODEL), lambda i: (i, 0)), pl.BlockSpec((n_mem, 2 * D_MODEL), lambda i: (0, 0))],
        out_specs=pl.BlockSpec((tq, D_MODEL), lambda i: (i, 0)), out_shape=jax.ShapeDtypeStruct((seq, D_MODEL), F32),
        compiler_params=_params(("arbitrary",), 4 * _nbytes((tq, D_MODEL), F32) + 2 * _nbytes((n_mem, 2 * D_MODEL), F32)),
    )(q, kv)


def _attn_bwd(q, kv, do):
    seq, n_mem = q.shape[0], kv.shape[0]
    tq = _tile(seq, (512, 256, 128))

    def body(q_ref, kv_ref, do_ref, dq_ref, dkv_ref):
        @pl.when(pl.program_id(0) == 0)
        def _():
            dkv_ref[...] = jnp.zeros_like(dkv_ref)

        for h in range(HEADS):
            hd = pl.ds(h * CA_DH, CA_DH)
            vd = pl.ds(D_MODEL + h * CA_DH, CA_DH)
            _, vjp = jax.vjp(_attn_head, q_ref[:, hd], kv_ref[:, hd], kv_ref[:, vd])
            dq, dk, dv = vjp(do_ref[:, hd])
            dq_ref[:, hd] = dq
            dkv_ref[:, hd] += dk
            dkv_ref[:, vd] += dv

    return _pcall(
        body, name="attn_bwd", grid=(seq // tq,),
        in_specs=[pl.BlockSpec((tq, D_MODEL), lambda i: (i, 0)), pl.BlockSpec((n_mem, 2 * D_MODEL), lambda i: (0, 0)),
                  pl.BlockSpec((tq, D_MODEL), lambda i: (i, 0))],
        out_specs=[pl.BlockSpec((tq, D_MODEL), lambda i: (i, 0)), pl.BlockSpec((n_mem, 2 * D_MODEL), lambda i: (0, 0))],
        out_shape=[jax.ShapeDtypeStruct((seq, D_MODEL), F32), jax.ShapeDtypeStruct((n_mem, 2 * D_MODEL), F32)],
        compiler_params=_params(("arbitrary",), 6 * _nbytes((tq, D_MODEL), F32) + 4 * _nbytes((n_mem, 2 * D_MODEL), F32)),
    )(q, kv, do)


FFN_TB = 512
FFN_TC = 256


def _ffn_mid(hg, xg, hv, xv, wg0, wg1, wg2, bg, wv0, wv1, wv2, bv):
    return jax.nn.gelu(causal_conv(hg, xg, (wg0, wg1, wg2), bg)) * causal_conv(hv, xv, (wv0, wv1, wv2), bv)


def _ffn_specs(seq, reverse):
    tb = min(FFN_TB, seq)
    nt = seq // tb
    row8 = tb // SUBLANES

    def tt(t):
        return nt - 1 - t if reverse else t
    nj = D_FF // FFN_TC
    main = pl.BlockSpec((tb, FFN_TC), lambda j, t: (tt(t), j))
    ins = []
    for off in (0, nj):
        ins += [pl.BlockSpec((tb, FFN_TC), lambda j, t, off=off: (tt(t), j + off)),
                pl.BlockSpec((SUBLANES, FFN_TC), lambda j, t, off=off: (jnp.maximum(tt(t) * row8 - 1, 0), j + off))]
    for off in (0, nj):
        ins += [pl.BlockSpec((FFN_CONV, FFN_TC), lambda j, t, off=off: (0, j + off)),
                pl.BlockSpec((1, FFN_TC), lambda j, t, off=off: (0, j + off))]
    return tb, nt, main, ins


def _ffn_args(c_first, ug, hg, uv, hv, wg, bg, wv, bv):
    halo_g = jnp.where(c_first, 0.0, hg[...])
    halo_v = jnp.where(c_first, 0.0, hv[...])
    return (halo_g, ug[...], halo_v, uv[...], wg[0:1, :], wg[1:2, :], wg[2:3, :], bg[...],
            wv[0:1, :], wv[1:2, :], wv[2:3, :], bv[...])


def _ffn_mid_fwd(u, conv_w, conv_b):
    seq = u.shape[0]
    tb, nt, main, ins = _ffn_specs(seq, False)

    def body(ug, hg, uv, hv, wg, bg, wv, bv, o_ref):
        o_ref[...] = _ffn_mid(*_ffn_args(pl.program_id(1) == 0, ug, hg, uv, hv, wg, bg, wv, bv))

    return _pcall(
        body, name="ffn_mid_fwd", grid=(D_FF // FFN_TC, nt), in_specs=ins, out_specs=main,
        out_shape=jax.ShapeDtypeStruct((seq, D_FF), F32),
        compiler_params=_params(("arbitrary", "arbitrary"), 12 * _nbytes((tb, FFN_TC), F32)),
    )(u, u, u, u, conv_w, conv_b, conv_w, conv_b)


def _ffn_mid_bwd(u, conv_w, conv_b, dh):
    seq = u.shape[0]
    tb, nt, main, ins = _ffn_specs(seq, True)

    def body(ug, hg, uv, hv, wg, bg, wv, bv, dh_ref, du, dw, db, carry):
        t = pl.program_id(1)

        @pl.when(t == 0)
        def _():
            for r in (dw, db, carry):
                r[...] = jnp.zeros_like(r)

        _, vjp = jax.vjp(_ffn_mid, *_ffn_args(t == nt - 1, ug, hg, uv, hv, wg, bg, wv, bv))
        dhg, dxg, dhv, dxv, g0, g1, g2, gb, v0, v1, v2, vb = vjp(dh_ref[...])
        zeros = jnp.zeros((tb - SUBLANES, FFN_TC), F32)
        du[0] = dxg + jnp.concatenate([zeros, carry[0]], axis=0)
        du[1] = dxv + jnp.concatenate([zeros, carry[1]], axis=0)
        carry[0] = dhg
        carry[1] = dhv
        for half, parts in enumerate(((g0, g1, g2), (v0, v1, v2))):
            for d, p in enumerate(parts):
                dw[half, d:d + 1, :] += p
        db[0] += gb
        db[1] += vb

    def grouped(rows, index):
        return pl.BlockSpec((2, rows, FFN_TC), index)
    return _pcall(
        body, name="ffn_mid_bwd", grid=(D_FF // FFN_TC, nt), in_specs=ins + [main],
        out_specs=[grouped(tb, lambda j, t: (0, nt - 1 - t, j)), grouped(FFN_CONV, lambda j, t: (0, 0, j)),
                   grouped(1, lambda j, t: (0, 0, j))],
        out_shape=[jax.ShapeDtypeStruct((2, seq, D_FF), F32), jax.ShapeDtypeStruct((2, FFN_CONV, D_FF), F32),
                   jax.ShapeDtypeStruct((2, 1, D_FF), F32)],
        scratch_shapes=[pltpu.VMEM((2, SUBLANES, FFN_TC), F32)],
        compiler_params=_params(("arbitrary", "arbitrary"), 24 * _nbytes((tb, FFN_TC), F32)),
    )(u, u, u, u, conv_w, conv_b, conv_w, conv_b, dh)


def _adamw(name, w, g, m, v):
    rows, cols = w.shape
    tr = _tile(rows, (256, 176, 128, 64, 40, 32, 16, 8))

    def body(w_ref, g_ref, m_ref, v_ref, d_ref, nm_ref, nv_ref):
        g_ = g_ref[...]
        m_new = ADAM_B1 * m_ref[...] + (1.0 - ADAM_B1) * g_
        v_new = ADAM_B2 * v_ref[...] + (1.0 - ADAM_B2) * jnp.square(g_)
        m_hat = m_new / (1.0 - ADAM_B1 ** ADAM_STEP)
        v_hat = v_new / (1.0 - ADAM_B2 ** ADAM_STEP)
        d_ref[...] = -ADAM_LR * (m_hat / (jnp.sqrt(v_hat) + ADAM_EPS) + ADAM_WD * w_ref[...])
        nm_ref[...] = m_new
        nv_ref[...] = v_new

    spec = pl.BlockSpec((tr, cols), lambda i: (i, 0))
    sh = jax.ShapeDtypeStruct((rows, cols), F32)
    return _pcall(
        body, name=name, grid=(rows // tr,), in_specs=[spec] * 4, out_specs=[spec] * 3, out_shape=[sh] * 3,
        compiler_params=_params(("arbitrary",), 14 * _nbytes((tr, -(-cols // LANES) * LANES), F32)),
    )(w, g, m, v)


MESH = pl.DeviceIdType.MESH
ANY = pl.BlockSpec(memory_space=pl.ANY)
N_CHIPS = 4
N_DEV = 8
BF16_ROWS = 16


def _me():
    return lax.axis_index("x"), lax.axis_index("y"), lax.axis_index("c")


def _other_chips(x, y):
    return [(1 - x, y), (x, 1 - y), (1 - x, 1 - y)]


def _remote(src, dst, ssem, rsem, dev):
    return pltpu.make_async_remote_copy(src_ref=src, dst_ref=dst, send_sem=ssem, recv_sem=rsem,
                                        device_id=dev, device_id_type=MESH)


def _half_rows(ref_rows, cc):
    half = ref_rows // 2
    return pl.ds(pl.multiple_of(cc * half, BF16_ROWS), half)


def _gather_weights(shards):
    n = len(shards)
    n_ici = n * (N_CHIPS - 1)

    def body(*refs):
        ins, outs, (ssem, rsem, lsem) = refs[:n], refs[n:2 * n], refs[2 * n:]
        x, y, c = _me()
        k_me = 2 * x + y
        sib = (x, y, 1 - c)
        chips = _other_chips(x, y)
        started = []
        for i, (w_ref, o_ref) in enumerate(zip(ins, outs)):
            cp = pltpu.make_async_copy(w_ref, o_ref.at[k_me], lsem.at[i])
            cp.start()
            started.append(cp)
        for r, (px, py) in enumerate(chips):
            for i, (w_ref, o_ref) in enumerate(zip(ins, outs)):
                rows = _half_rows(w_ref.shape[0], c)
                s = r * n + i
                cp = _remote(w_ref.at[rows], o_ref.at[k_me, rows], ssem.at[s], rsem.at[s], (px, py, c))
                cp.start()
                started.append(cp)
        for r, (px, py) in enumerate(chips):
            for i, o_ref in enumerate(outs):
                blk = o_ref.at[2 * px + py, _half_rows(o_ref.shape[1], c)]
                s = r * n + i
                _remote(blk, blk, ssem.at[s], rsem.at[s], (px, py, c)).wait_recv()
                cp = _remote(blk, blk, ssem.at[n_ici + s], rsem.at[n_ici + s], sib)
                cp.start()
                started.append(cp)
        for r, (px, py) in enumerate(chips):
            for i, o_ref in enumerate(outs):
                blk = o_ref.at[2 * px + py, _half_rows(o_ref.shape[1], 1 - c)]
                s = n_ici + r * n + i
                _remote(blk, blk, ssem.at[s], rsem.at[s], sib).wait_recv()
        for cp in started[n:]:
            cp.wait_send()
        for cp in started[:n]:
            cp.wait()

    return _pcall(
        body, name="gather_weights", in_specs=[ANY] * n, out_specs=[ANY] * n,
        out_shape=[jax.ShapeDtypeStruct((N_CHIPS,) + s.shape, s.dtype) for s in shards],
        scratch_shapes=[pltpu.SemaphoreType.DMA((2 * n_ici,)), pltpu.SemaphoreType.DMA((2 * n_ici,)),
                        pltpu.SemaphoreType.DMA((n,))],
    )(*shards)


def _swap_halves(grads):
    n = len(grads)

    def body(*refs):
        ins, outs, (ssem, rsem) = refs[:n], refs[n:2 * n], refs[2 * n:]
        x, y, c = _me()
        copies = []
        for i, (g_ref, o_ref) in enumerate(zip(ins, outs)):
            for k in range(N_CHIPS):
                s = i * N_CHIPS + k
                cp = _remote(g_ref.at[k, _half_rows(g_ref.shape[1], 1 - c)], o_ref.at[k], ssem.at[s], rsem.at[s],
                             (x, y, 1 - c))
                cp.start()
                copies.append(cp)
        for cp in copies:
            cp.wait()

    return _pcall(
        body, name="swap_halves", in_specs=[ANY] * n, out_specs=[ANY] * n,
        out_shape=[jax.ShapeDtypeStruct((N_CHIPS, g.shape[1] // 2, g.shape[2]), g.dtype) for g in grads],
        scratch_shapes=[pltpu.SemaphoreType.DMA((n * N_CHIPS,)), pltpu.SemaphoreType.DMA((n * N_CHIPS,))],
    )(*grads)


def _scatter_chips(parts):
    n = len(parts)

    def body(*refs):
        ins, outs, (ssem, rsem, lsem) = refs[:n], refs[n:2 * n], refs[2 * n:]
        x, y, c = _me()
        k_me = 2 * x + y
        chips = _other_chips(x, y)
        local, sends = [], []
        for i, (p_ref, o_ref) in enumerate(zip(ins, outs)):
            cp = pltpu.make_async_copy(p_ref.at[k_me], o_ref.at[k_me], lsem.at[i])
            cp.start()
            local.append(cp)
        for r, (px, py) in enumerate(chips):
            for i, (p_ref, o_ref) in enumerate(zip(ins, outs)):
                s = r * n + i
                cp = _remote(p_ref.at[2 * px + py], o_ref.at[k_me], ssem.at[s], rsem.at[s], (px, py, c))
                cp.start()
                sends.append(cp)
        for r, (px, py) in enumerate(chips):
            for i, o_ref in enumerate(outs):
                blk = o_ref.at[2 * px + py]
                s = r * n + i
                _remote(blk, blk, ssem.at[s], rsem.at[s], (px, py, c)).wait_recv()
        for cp in sends:
            cp.wait_send()
        for cp in local:
            cp.wait()

    n_sem = n * (N_CHIPS - 1)
    return _pcall(
        body, name="scatter_chips", in_specs=[ANY] * n, out_specs=[ANY] * n,
        out_shape=[jax.ShapeDtypeStruct(p.shape, p.dtype) for p in parts],
        scratch_shapes=[pltpu.SemaphoreType.DMA((n_sem,)), pltpu.SemaphoreType.DMA((n_sem,)),
                        pltpu.SemaphoreType.DMA((n,))],
    )(*parts)


def _share_halves(halves):
    n = len(halves)

    def body(*refs):
        ins, outs, (ssem, rsem, lsem) = refs[:n], refs[n:2 * n], refs[2 * n:]
        x, y, c = _me()
        sib = (x, y, 1 - c)
        local, sends = [], []
        for i, (r_ref, o_ref) in enumerate(zip(ins, outs)):
            cp = pltpu.make_async_copy(r_ref, o_ref.at[c], lsem.at[i])
            cp.start()
            local.append(cp)
            cp = _remote(r_ref, o_ref.at[c], ssem.at[i], rsem.at[i], sib)
            cp.start()
            sends.append(cp)
        for i, o_ref in enumerate(outs):
            blk = o_ref.at[1 - c]
            _remote(blk, blk, ssem.at[i], rsem.at[i], sib).wait_recv()
        for cp in sends:
            cp.wait_send()
        for cp in local:
            cp.wait()

    return _pcall(
        body, name="share_halves", in_specs=[ANY] * n, out_specs=[ANY] * n,
        out_shape=[jax.ShapeDtypeStruct((2,) + h.shape, h.dtype) for h in halves],
        scratch_shapes=[pltpu.SemaphoreType.DMA((n,)), pltpu.SemaphoreType.DMA((n,)), pltpu.SemaphoreType.DMA((n,))],
    )(*halves)


def _exchange_small(v, reduce):
    rows = v.shape[0]

    def body(v_ref, out_ref, buf, ssem, rsem):
        x, y, c = _me()
        me = 4 * x + 2 * y + c
        peers = [((x + bx) % 2, (y + by) % 2, (c + bc) % 2)
                 for bx in (0, 1) for by in (0, 1) for bc in (0, 1) if (bx, by, bc) != (0, 0, 0)]
        dst = buf if reduce else out_ref
        dst[me] = v_ref[...]
        sends = [_remote(v_ref, dst.at[me], ssem.at[r], rsem.at[r], p) for r, p in enumerate(peers)]
        for cp in sends:
            cp.start()
        for r, (px, py, pc) in enumerate(peers):
            blk = dst.at[4 * px + 2 * py + pc]
            _remote(blk, blk, ssem.at[r], rsem.at[r], (px, py, pc)).wait_recv()
        if reduce:
            acc = buf[0]
            for d in range(1, N_DEV):
                acc = acc + buf[d]
            out_ref[...] = acc
        for cp in sends:
            cp.wait_send()

    vm = pl.BlockSpec(memory_space=pltpu.VMEM)
    out_shape = jax.ShapeDtypeStruct((rows, LANES) if reduce else (N_DEV, rows, LANES), F32)
    buf_shape = (N_DEV, rows, LANES) if reduce else (SUBLANES, LANES)
    return _pcall(
        body, name="reduce_small" if reduce else "gather_small", in_specs=[vm], out_specs=vm, out_shape=out_shape,
        scratch_shapes=[pltpu.VMEM(buf_shape, F32), pltpu.SemaphoreType.DMA((N_DEV - 1,)),
                        pltpu.SemaphoreType.DMA((N_DEV - 1,))],
        compiler_params=pltpu.CompilerParams(vmem_limit_bytes=32 * 1024 * 1024),
    )(v)


def _add_pair(name, core, g, theirs):
    _, half, cols = theirs.shape
    tr = _tile(half, (256, 176, 128))
    nb = half // tr

    def body(c_ref, g_ref, t_ref, o32_ref, o16_ref):
        s = g_ref[...] + t_ref[...]
        o32_ref[...] = s
        o16_ref[...] = s.astype(BF16)

    spec = pl.BlockSpec((None, tr, cols), lambda k, i, c_ref: (k, i, 0))
    grid_spec = pltpu.PrefetchScalarGridSpec(
        num_scalar_prefetch=1, grid=(N_CHIPS, nb),
        in_specs=[pl.BlockSpec((None, tr, cols), lambda k, i, c_ref: (k, c_ref[0] * nb + i, 0)), spec],
        out_specs=[spec, spec])
    return _pcall(
        body, name=name, grid_spec=grid_spec,
        out_shape=[jax.ShapeDtypeStruct(theirs.shape, F32), jax.ShapeDtypeStruct(theirs.shape, BF16)],
        compiler_params=_params(("arbitrary", "arbitrary"), 8 * _nbytes((tr, cols + LANES), F32)),
    )(core, g, theirs)


def _add_chips(name, chip, p32, recv):
    _, half, cols = p32.shape
    tr = _tile(half, (256, 176, 128))

    def body(k_ref, p_ref, r_ref, o_ref):
        k_me = k_ref[0]
        acc = None
        for k in range(N_CHIPS):
            term = jnp.where(k_me == k, p_ref[...], r_ref[k].astype(F32))
            acc = term if acc is None else acc + term
        o_ref[...] = acc

    grid_spec = pltpu.PrefetchScalarGridSpec(
        num_scalar_prefetch=1, grid=(half // tr,),
        in_specs=[pl.BlockSpec((None, tr, cols), lambda i, k_ref: (k_ref[0], i, 0)),
                  pl.BlockSpec((N_CHIPS, tr, cols), lambda i, k_ref: (0, i, 0))],
        out_specs=pl.BlockSpec((tr, cols), lambda i, k_ref: (i, 0)))
    return _pcall(
        body, name=name, grid_spec=grid_spec, out_shape=jax.ShapeDtypeStruct((half, cols), F32),
        compiler_params=_params(("arbitrary",), 10 * _nbytes((tr, cols + LANES), F32)),
    )(chip, p32, recv)


def kernel(x, mem, w_in, b_in, hg_lb_logits, hg_norm_w, ml_conv_w, ml_conv_b, ml_norm_w, w_out, ln1_g, ln1_b, ca_wq, ca_wkv, ca_wo, ln2_g, ln2_b, ffn_w_up, ffn_conv_w, ffn_conv_b, ffn_w_down, ln3_g, ln3_b, loss_target, m_w_in, m_b_in, m_hg_lb_logits, m_hg_norm_w, m_ml_conv_w, m_ml_conv_b, m_ml_norm_w, m_w_out, m_ln1_g, m_ln1_b, m_ca_wq, m_ca_wkv, m_ca_wo, m_ln2_g, m_ln2_b, m_ffn_w_up, m_ffn_conv_w, m_ffn_conv_b, m_ffn_w_down, m_ln3_g, m_ln3_b, v_w_in, v_b_in, v_hg_lb_logits, v_hg_norm_w, v_ml_conv_w, v_ml_conv_b, v_ml_norm_w, v_w_out, v_ln1_g, v_ln1_b, v_ca_wq, v_ca_wkv, v_ca_wo, v_ln2_g, v_ln2_b, v_ffn_w_up, v_ffn_conv_w, v_ffn_conv_b, v_ffn_w_down, v_ln3_g, v_ln3_b):
    return _train_step(dict(locals()))


WEIGHTS = ("w_in", "b_in", "hg_lb_logits", "hg_norm_w", "ml_conv_w", "ml_conv_b", "ml_norm_w", "w_out", "ln1_g",
           "ln1_b", "ca_wq", "ca_wkv", "ca_wo", "ln2_g", "ln2_b", "ffn_w_up", "ffn_conv_w", "ffn_conv_b",
           "ffn_w_down", "ln3_g", "ln3_b")
MATRICES = ("w_in", "w_out", "ca_wq", "ca_wkv", "ca_wo", "ffn_w_up", "ffn_w_down")
COL_SHARDED = ("w_in", "ca_wkv", "ffn_w_up", "ml_conv_w", "ffn_conv_w")
SMALL = tuple(n for n in WEIGHTS if n not in MATRICES)
PART_ROWS = 16


def _part_rows(shape, lead):
    n = 1
    for s in shape[lead:]:
        n *= s
    return -(-n // (LANES * PART_ROWS)) * PART_ROWS


def _pack(arrs, dtype, lead=0, rows=None):
    parts = []
    for a in arrs:
        head = a.shape[:lead]
        flat = a.reshape(head + (-1,)).astype(dtype)
        pad = _part_rows(a.shape, lead) * LANES - flat.shape[-1]
        flat = jnp.pad(flat, [(0, 0)] * lead + [(0, pad)])
        parts.append(flat.reshape(head + (-1, LANES)))
    used = sum(p.shape[lead] for p in parts)
    if rows is not None and rows > used:
        parts.append(jnp.zeros(parts[0].shape[:lead] + (rows - used, LANES), dtype))
    return jnp.concatenate(parts, axis=lead)


def _unpack(buf, shapes):
    lead = buf.shape[:-2]
    outs, r = [], 0
    for sh in shapes:
        n = 1
        for s in sh:
            n *= s
        nr = _part_rows(sh, 0)
        flat = buf[..., r:r + nr, :].reshape(lead + (nr * LANES,))
        outs.append(flat[..., :n].reshape(lead + tuple(sh)))
        r += nr
    return outs


def _cat_cols(s):
    return jnp.moveaxis(s, 0, 1).reshape(s.shape[1], -1)


def _split_cols(g):
    return jnp.moveaxis(g.reshape(g.shape[0], N_CHIPS, -1), 1, 0)


def _stack_rows(s):
    return s.reshape(-1, s.shape[-1])


def _train_step(a):
    xs, mems, tgt = a["x"][0], a["mem"][0], a["loss_target"][0]
    core = lax.axis_index("c").astype(jnp.int32).reshape(1)
    chip = (2 * lax.axis_index("x") + lax.axis_index("y")).astype(jnp.int32).reshape(1)
    k_me = chip[0]
    shard = {n: a[n][0] for n in MATRICES}

    w = dict(zip(MATRICES, _gather_weights([shard[n].astype(BF16) for n in MATRICES])))
    for n in ("w_out", "ca_wq", "ca_wo", "ffn_w_down"):
        w[n] = _stack_rows(w[n])
    w["w_in"] = jnp.pad(_cat_cols(w["w_in"]), ((0, 0), (0, D_IN_PAD - D_IN)))
    taps = _exchange_small(_pack([a["ml_conv_w"][0], a["ffn_conv_w"][0]], F32), reduce=False)
    taps = taps.reshape((N_CHIPS, 2) + taps.shape[1:])[:, 0]
    ml_cw, ffn_cw = [_cat_cols(s) for s in _unpack(taps, [a["ml_conv_w"].shape[1:], a["ffn_conv_w"].shape[1:]])]
    b_in_p = jnp.pad(a["b_in"], ((0, 0), (0, D_IN_PAD - D_IN)))
    mixer_w = (a["hg_lb_logits"], a["hg_norm_w"], ml_cw, a["ml_conv_b"], a["ml_norm_w"])
    up_cols = a["ffn_w_up"].shape[-1]

    proj = _mm("proj", "nn", xs, w["w_in"], bias=b_in_p)
    y, hst, cst, nst, mst = _mixer_fwd(proj, *mixer_w)
    z1, x1 = _mm("mix_out", "nn", y, w["w_out"], res=xs, res_scale=ALPHA, ln=("fwd", a["ln1_g"], a["ln1_b"]))
    q = _mm("ca_q", "nn", x1, w["ca_wq"])
    kv = _mm("ca_kv", "nn", mems, w["ca_wkv"])
    o = _attn_fwd(q, kv)
    z2, x2 = _mm("ca_out", "nn", o, w["ca_wo"], res=x1, res_scale=ALPHA, ln=("fwd", a["ln2_g"], a["ln2_b"]))
    u = _mm("ffn_up", "nn", x2, w["ffn_w_up"], tn=up_cols)
    hmid = _ffn_mid_fwd(u, ffn_cw, a["ffn_conv_b"])
    dz3, g_ln3g, g_ln3b, loss_part = _mm("ffn_down", "nn", hmid, w["ffn_w_down"], res=x2, res_scale=ALPHA,
                                         ln=("loss", a["ln3_g"], a["ln3_b"], tgt))

    grads = {"ln3_g": g_ln3g, "ln3_b": g_ln3b}
    dhmid = _mm("d_hmid", "nt", dz3, w["ffn_w_down"])
    grads["ffn_w_down"] = _mm("g_w_down", "tn", hmid, dz3)
    du, g_cw, g_cb = _ffn_mid_bwd(u, ffn_cw, a["ffn_conv_b"], dhmid)
    grads["ffn_conv_w"] = jnp.moveaxis(g_cw, 0, 1).reshape(FFN_CONV, 2 * D_FF)
    grads["ffn_conv_b"] = g_cb.reshape(1, 2 * D_FF)
    grads["ffn_w_up"] = _mm("g_w_up", "tn", x2, du, out_groups=N_CHIPS, tn=up_cols)
    dz2, grads["ln2_g"], grads["ln2_b"] = _mm("d_x2", "nt", du, w["ffn_w_up"], res=dz3, res_scale=ALPHA,
                                              ln=("bwd", z2, a["ln2_g"], a["ln2_b"]), tk=up_cols)
    do = _mm("d_o", "nt", dz2, w["ca_wo"])
    grads["ca_wo"] = _mm("g_wo", "tn", o, dz2)
    dq, dkv = _attn_bwd(q, kv, do)
    grads["ca_wq"] = _mm("g_wq", "tn", x1, dq)
    grads["ca_wkv"] = _mm("g_wkv", "tn", mems, dkv, out_groups=N_CHIPS)
    dz1, grads["ln1_g"], grads["ln1_b"] = _mm("d_x1", "nt", dq, w["ca_wq"], res=dz2, res_scale=ALPHA,
                                              ln=("bwd", z1, a["ln1_g"], a["ln1_b"]))
    dy = _mm("d_y", "nt", dz1, w["w_out"])
    grads["w_out"] = _mm("g_w_out", "tn", y, dz1)
    (dproj, grads["hg_lb_logits"], grads["hg_norm_w"], grads["ml_conv_w"], grads["ml_conv_b"],
     grads["ml_norm_w"]) = _mixer_bwd(proj, dy, hst, cst, nst, mst, *mixer_w)
    grads["w_in"] = _split_cols(_mm("g_w_in", "tn", xs, dproj)[:, :D_IN])
    grads["b_in"] = _colsum("g_b_in", dproj)[:, :D_IN]
    dx = _mm("d_x", "nt", dproj, w["w_in"], res=dz1, res_scale=ALPHA)
    for n in ("w_out", "ca_wq", "ca_wo", "ffn_w_down"):
        grads[n] = grads[n].reshape((N_CHIPS,) + shard[n].shape)

    per_chip = [grads[n] for n in MATRICES]
    sums = [_add_pair("add_pair_" + n, core, g, t) for n, g, t in zip(MATRICES, per_chip, _swap_halves(per_chip))]
    recv = _scatter_chips([s16 for _, s16 in sums])
    halves = [_add_chips("add_chips_" + n, chip, s32, r) for n, (s32, _), r in zip(MATRICES, sums, recv)]
    for n, both in zip(MATRICES, _share_halves(halves)):
        grads[n] = _stack_rows(both)

    small_shapes = [grads[n].shape for n in SMALL] + [loss_part.shape]
    summed = _unpack(_exchange_small(_pack([grads[n] for n in SMALL] + [loss_part], F32), reduce=True), small_shapes)
    loss = summed[-1][0, 0]
    for n, g in zip(SMALL, summed[:-1]):
        if n in COL_SHARDED:
            cols = a[n].shape[-1]
            g = lax.dynamic_slice_in_dim(g, k_me * cols, cols, axis=1)
        grads[n] = g

    delta, new_m, new_v = {}, {}, {}
    for n in MATRICES:
        delta[n], new_m[n], new_v[n] = _adamw("adamw_" + n, shard[n], grads[n], a["m_" + n][0], a["v_" + n][0])
    small_w = [a[n][0] if a[n].ndim == 3 else a[n] for n in SMALL]
    small_m = [a["m_" + n][0] if a[n].ndim == 3 else a["m_" + n] for n in SMALL]
    small_v = [a["v_" + n][0] if a[n].ndim == 3 else a["v_" + n] for n in SMALL]
    shapes = [w.shape for w in small_w]
    packed = [_pack(l, F32) for l in (small_w, [grads[n] for n in SMALL], small_m, small_v)]
    for out, buf in zip((delta, new_m, new_v), _adamw("adamw_small", *packed)):
        for n, v in zip(SMALL, _unpack(buf, shapes)):
            out[n] = v

    def shaped(d):
        return [d[n].reshape(a[n].shape) for n in WEIGHTS]
    return (loss, dx[None], *shaped(grads), *shaped(delta), *shaped(new_m), *shaped(new_v))
```

```python
import functools

import jax
import jax.numpy as jnp
from jax import lax
from jax.experimental import pallas as pl
from jax.experimental.pallas import tpu as pltpu

F32 = jnp.float32
BF16 = jnp.bfloat16

D_MODEL = 1024
HEADS = 4
DK = 128
D_GRP = HEADS * DK
CHUNK = 64
ML_CONV = 4
FFN_CONV = 3
D_FF = 2816
CA_DH = D_MODEL // HEADS
DEPTH = 1
ALPHA = (2.0 * DEPTH) ** 0.25
LN_EPS = 1e-5
NEG_BIG = -1e30
D_IN = 8 * D_GRP + 2 * HEADS
D_IN_PAD = 8 * D_GRP + 128
ADAM_LR, ADAM_B1, ADAM_B2, ADAM_EPS, ADAM_WD, ADAM_STEP = 0.001, 0.9, 0.999, 1e-08, 0.01, 10

SUBLANES = 8
LANES = 128
VMEM_BYTES = 64 * 1024 * 1024


def _pcall(body, **kw):
    return pl.pallas_call(body, **kw)


def _params(semantics, vmem_bytes):
    limit = int(min(max(2 * vmem_bytes, 16 * 1024 * 1024), VMEM_BYTES - 8 * 1024 * 1024))
    return pltpu.CompilerParams(dimension_semantics=semantics, vmem_limit_bytes=limit)


def _nbytes(shape, dtype):
    n = 1
    for s in shape:
        n *= s
    return n * jnp.dtype(dtype).itemsize


def _dg(a, b, ca, cb):
    return lax.dot_general(a.astype(BF16), b.astype(BF16), (((ca,), (cb,)), ((), ())),
                           preferred_element_type=F32)


@jax.custom_vjp
def mm_nn(a, b):
    return _dg(a, b, 1, 0)


mm_nn.defvjp(lambda a, b: (_dg(a, b, 1, 0), (a, b)),
             lambda r, g: (_dg(g, r[1], 1, 1).astype(r[0].dtype), _dg(r[0], g, 0, 0).astype(r[1].dtype)))


@jax.custom_vjp
def mm_nt(a, b):
    return _dg(a, b, 1, 1)


mm_nt.defvjp(lambda a, b: (_dg(a, b, 1, 1), (a, b)),
             lambda r, g: (_dg(g, r[1], 1, 0).astype(r[0].dtype), _dg(g, r[0], 0, 0).astype(r[1].dtype)))


@jax.custom_vjp
def mm_tn(a, b):
    return _dg(a, b, 0, 0)


mm_tn.defvjp(lambda a, b: (_dg(a, b, 0, 0), (a, b)),
             lambda r, g: (_dg(r[1], g, 1, 1).astype(r[0].dtype), _dg(r[0], g, 1, 0).astype(r[1].dtype)))


def _hdot(a, b):
    return jnp.dot(a, b, precision=lax.Precision.HIGHEST, preferred_element_type=F32)


def _tri(n, lower):
    r = lax.broadcasted_iota(jnp.int32, (n, n), 0)
    c = lax.broadcasted_iota(jnp.int32, (n, n), 1)
    return ((r >= c) if lower else (r <= c)).astype(F32)


@jax.custom_vjp
def cumsum_rows(x):
    return _hdot(_tri(x.shape[0], True), x)


cumsum_rows.defvjp(lambda x: (_hdot(_tri(x.shape[0], True), x), None),
                   lambda _, g: (_hdot(_tri(g.shape[0], False), g),))


def _shift_impl(halo, x, d):
    xx = jnp.concatenate([halo, x], axis=0)
    return pltpu.roll(xx, d, 0)[SUBLANES:]


@functools.partial(jax.custom_vjp, nondiff_argnums=(2,))
def shift_rows(halo, x, d):
    return _shift_impl(halo, x, d)


def _shift_bwd(d, _, g):
    n = g.shape[0] + SUBLANES
    gg = jnp.concatenate([jnp.zeros((SUBLANES, g.shape[1]), g.dtype), g], axis=0)
    r = pltpu.roll(gg, n - d, 0)
    return r[:SUBLANES], r[SUBLANES:]


shift_rows.defvjp(lambda halo, x, d: (_shift_impl(halo, x, d), None), _shift_bwd)


def causal_conv(halo, x, w_rows, b):
    k = len(w_rows)
    y = b + w_rows[k - 1] * x
    for d in range(1, k):
        y = y + w_rows[k - 1 - d] * shift_rows(halo, x, d)
    return y


def _sigmoid(x):
    return 1.0 / (1.0 + jnp.exp(-x))


def _silu(x):
    return x * _sigmoid(x)


def _log_sigmoid(x):
    return jnp.minimum(x, 0.0) - jnp.log(1.0 + jnp.exp(-jnp.abs(x)))


def _pick_lane(x, j):
    lane = lax.broadcasted_iota(jnp.int32, (1, x.shape[1]), 1)
    return jnp.sum(jnp.where(lane == j, x, 0.0), axis=1, keepdims=True)


def _pick_row(x, i):
    row = lax.broadcasted_iota(jnp.int32, (x.shape[0], 1), 0)
    return jnp.sum(jnp.where(row == i, x, 0.0), axis=0, keepdims=True)


def _col_to_row(e):
    n = e.shape[0]
    eye = lax.broadcasted_iota(jnp.int32, (n, n), 0) == lax.broadcasted_iota(jnp.int32, (n, n), 1)
    return jnp.sum(jnp.where(eye, e, 0.0), axis=0, keepdims=True)


def _layer_norm(z, g, b):
    mu = jnp.mean(z, axis=-1, keepdims=True)
    zc = z - mu
    var = jnp.mean(zc * zc, axis=-1, keepdims=True)
    return zc * lax.rsqrt(var + LN_EPS) * g + b


def _hg_head(st_t, hq, hf, hi, hgate, l0, l1, nw):
    n = hq.shape[0]
    lb = _sigmoid(l0 - l1)
    q = _silu(hq)
    lf = jnp.log(lb + (1.0 - lb) * _sigmoid(hf))
    k = (1.0 - lb) * _sigmoid(-hf)
    b = cumsum_rows(lf)
    b_ref = _pick_row(b, n // 2 - 1)
    b_last = _pick_row(b, n - 1)
    attn = mm_nt(q * jnp.exp(b - b_ref), k * jnp.exp(b_ref - b))
    attn = jnp.where(_tri(n, True) > 0, attn, 0.0)
    o = mm_nn(attn, hi) + mm_nt(q * jnp.exp(b), st_t)
    st_new = jnp.exp(b_last) * st_t + mm_tn(hi, k * jnp.exp(b_last - b))
    y = o * lax.rsqrt(jnp.mean(o * o, axis=-1, keepdims=True) + LN_EPS) * nw * _silu(hgate)
    return st_new, y


def _ml_head(c_st, n_st, m_st, q, k, v, gates, og, nw, h):
    n = q.shape[0]
    ig = _pick_lane(gates, h)
    fl = _log_sigmoid(_pick_lane(gates, HEADS + h))
    qs = q * (DK ** -0.5)
    b = _pick_lane(cumsum_rows(jnp.broadcast_to(fl, (n, LANES))), 0)
    g = jnp.sum(fl, axis=0, keepdims=True)
    d = jnp.where(_tri(n, True) > 0, b + _col_to_row(ig - b), -jnp.inf)
    inter = b + m_st
    m_t = jnp.maximum(inter, jnp.max(d, axis=1, keepdims=True))
    s = mm_nt(qs, k) * jnp.exp(d - m_t)
    w_inter = jnp.exp(inter - m_t)
    num = mm_nn(s, v) + w_inter * mm_nn(qs, c_st)
    den = jnp.sum(s, axis=1, keepdims=True) + w_inter * jnp.sum(qs * n_st, axis=1, keepdims=True)
    h_out = num / jnp.maximum(jnp.abs(den), jnp.exp(-m_t))
    a = g - b + ig
    m_new = jnp.maximum(g + m_st, jnp.max(a, axis=0, keepdims=True))
    decay = jnp.exp(g + m_st - m_new)
    wk = k * jnp.exp(a - m_new)
    c_new = decay * c_st + mm_tn(wk, v)
    n_new = decay * n_st + jnp.sum(wk, axis=0, keepdims=True)
    mu = jnp.mean(h_out, axis=-1, keepdims=True)
    hc = h_out - mu
    var = jnp.mean(hc * hc, axis=-1, keepdims=True)
    y = _sigmoid(og) * (hc * lax.rsqrt(var + LN_EPS) * nw)
    return c_new, n_new, m_new, y


def _qk_conv(halo, x, w0, w1, w2, w3, b):
    return _silu(causal_conv(halo, x, (w0, w1, w2, w3), b))


def _grp(i, h=None):
    if h is None:
        return pl.ds(i * D_GRP, D_GRP)
    return pl.ds(i * D_GRP + h * DK, DK)


def _mixer_specs(n_chunks, reverse):
    def chunk(c):
        return n_chunks - 1 - c if reverse else c
    row8 = CHUNK // SUBLANES
    proj_spec = pl.BlockSpec((CHUNK, D_IN_PAD), lambda c: (chunk(c), 0))
    halo_spec = pl.BlockSpec((SUBLANES, 2 * D_GRP), lambda c: (jnp.maximum(chunk(c) * row8 - 1, 0), 2))
    small = [pl.BlockSpec((2, D_GRP), lambda c: (0, 0)), pl.BlockSpec((1, D_GRP), lambda c: (0, 0)),
             pl.BlockSpec((ML_CONV, 2 * D_GRP), lambda c: (0, 0)), pl.BlockSpec((1, 2 * D_GRP), lambda c: (0, 0)),
             pl.BlockSpec((1, D_GRP), lambda c: (0, 0))]
    state_specs = [pl.BlockSpec((1, HEADS, DK, DK), lambda c: (chunk(c), 0, 0, 0)),
                   pl.BlockSpec((1, HEADS, DK, DK), lambda c: (chunk(c), 0, 0, 0)),
                   pl.BlockSpec((1, HEADS, 1, DK), lambda c: (chunk(c), 0, 0, 0)),
                   pl.BlockSpec((1, HEADS, 1, DK), lambda c: (chunk(c), 0, 0, 0))]
    y_spec = pl.BlockSpec((CHUNK, 2 * D_GRP), lambda c: (chunk(c), 0))
    return proj_spec, halo_spec, small, state_specs, y_spec, chunk


def _mixer_fwd(proj, lb_logits, hg_nw, conv_w, conv_b, ml_nw):
    seq = proj.shape[0]
    n_chunks = seq // CHUNK
    proj_spec, halo_spec, small, state_specs, y_spec, _ = _mixer_specs(n_chunks, False)

    def body(proj_ref, halo_ref, lg_ref, hnw_ref, cw_ref, cb_ref, mnw_ref,
             y_ref, hst_ref, cst_ref, nst_ref, mst_ref, hs, cs, ns, ms):
        c = pl.program_id(0)

        @pl.when(c == 0)
        def _():
            hs[...] = jnp.zeros_like(hs)
            cs[...] = jnp.zeros_like(cs)
            ns[...] = jnp.zeros_like(ns)
            ms[...] = jnp.full(ms.shape, NEG_BIG, F32)

        hst_ref[0] = hs[...]
        cst_ref[0] = cs[...]
        nst_ref[0] = ns[...]
        mst_ref[0] = ms[...]
        halo = jnp.where(c > 0, halo_ref[...], 0.0)
        qk = _qk_conv(halo, proj_ref[:, pl.ds(4 * D_GRP, 2 * D_GRP)],
                      cw_ref[0:1, :], cw_ref[1:2, :], cw_ref[2:3, :], cw_ref[3:4, :], cb_ref[...])
        gates = proj_ref[:, pl.ds(8 * D_GRP, LANES)]
        for h in range(HEADS):
            hd = pl.ds(h * DK, DK)
            st_new, y = _hg_head(hs[h], proj_ref[:, _grp(0, h)], proj_ref[:, _grp(1, h)], proj_ref[:, _grp(2, h)],
                                 proj_ref[:, _grp(3, h)], lg_ref[0:1, hd], lg_ref[1:2, hd], hnw_ref[:, hd])
            hs[h] = st_new
            y_ref[:, hd] = y
            c_new, n_new, m_new, y = _ml_head(
                cs[h], ns[h], _pick_lane(ms[h], 0), qk[:, h * DK:(h + 1) * DK],
                qk[:, D_GRP + h * DK:D_GRP + (h + 1) * DK], proj_ref[:, _grp(6, h)], gates,
                proj_ref[:, _grp(7, h)], mnw_ref[:, hd], h)
            cs[h] = c_new
            ns[h] = n_new
            ms[h] = jnp.broadcast_to(m_new, (1, DK))
            y_ref[:, pl.ds(D_GRP + h * DK, DK)] = y

    st = jax.ShapeDtypeStruct((n_chunks, HEADS, DK, DK), F32)
    vec = jax.ShapeDtypeStruct((n_chunks, HEADS, 1, DK), F32)
    vmem = 2 * (_nbytes((CHUNK, D_IN_PAD), F32) + _nbytes((CHUNK, 2 * D_GRP), F32) + 2 * _nbytes((HEADS, DK, DK), F32)) \
        + 2 * _nbytes((HEADS, DK, DK), F32)
    return _pcall(
        body, name="mixer_fwd", grid=(n_chunks,),
        in_specs=[proj_spec, halo_spec] + small,
        out_specs=[y_spec] + state_specs,
        out_shape=[jax.ShapeDtypeStruct((seq, 2 * D_GRP), F32), st, st, vec, vec],
        scratch_shapes=[pltpu.VMEM((HEADS, DK, DK), F32), pltpu.VMEM((HEADS, DK, DK), F32),
                        pltpu.VMEM((HEADS, 1, DK), F32), pltpu.VMEM((HEADS, 1, DK), F32)],
        compiler_params=_params(("arbitrary",), vmem),
    )(proj, proj, lb_logits, hg_nw, conv_w, conv_b, ml_nw)


def _mixer_bwd(proj, dy, hst, cst, nst, mst, lb_logits, hg_nw, conv_w, conv_b, ml_nw):
    seq = proj.shape[0]
    n_chunks = seq // CHUNK
    proj_spec, halo_spec, small, state_specs, y_spec, _ = _mixer_specs(n_chunks, True)

    def body(proj_ref, halo_ref, dy_ref, hst_ref, cst_ref, nst_ref, mst_ref,
             lg_ref, hnw_ref, cw_ref, cb_ref, mnw_ref,
             dproj_ref, dlg_ref, dhnw_ref, dcw_ref, dcb_ref, dmnw_ref,
             dhs, dcs, dns, dms, dhalo, dqk):
        c = pl.program_id(0)

        @pl.when(c == 0)
        def _():
            for r in (dhs, dcs, dns, dms, dhalo, dlg_ref, dhnw_ref, dcw_ref, dcb_ref, dmnw_ref):
                r[...] = jnp.zeros_like(r)

        first = c == n_chunks - 1
        halo = jnp.where(first, 0.0, halo_ref[...])
        x_qk = proj_ref[:, pl.ds(4 * D_GRP, 2 * D_GRP)]
        conv_args = (halo, x_qk, cw_ref[0:1, :], cw_ref[1:2, :], cw_ref[2:3, :], cw_ref[3:4, :], cb_ref[...])
        qk, conv_vjp = jax.vjp(_qk_conv, *conv_args)
        gates = proj_ref[:, pl.ds(8 * D_GRP, LANES)]
        dgates = jnp.zeros((CHUNK, LANES), F32)
        for h in range(HEADS):
            hd = pl.ds(h * DK, DK)
            args = (hst_ref[0, h], proj_ref[:, _grp(0, h)], proj_ref[:, _grp(1, h)], proj_ref[:, _grp(2, h)],
                    proj_ref[:, _grp(3, h)], lg_ref[0:1, hd], lg_ref[1:2, hd], hnw_ref[:, hd])
            _, vjp = jax.vjp(_hg_head, *args)
            dst, dhq, dhf, dhi, dhg, dl0, dl1, dnw = vjp((dhs[h], dy_ref[:, hd]))
            dhs[h] = dst
            dproj_ref[:, _grp(0, h)] = dhq
            dproj_ref[:, _grp(1, h)] = dhf
            dproj_ref[:, _grp(2, h)] = dhi
            dproj_ref[:, _grp(3, h)] = dhg
            dlg_ref[0:1, hd] += dl0
            dlg_ref[1:2, hd] += dl1
            dhnw_ref[:, hd] += dnw

            margs = (cst_ref[0, h], nst_ref[0, h], _pick_lane(mst_ref[0, h], 0), qk[:, h * DK:(h + 1) * DK],
                     qk[:, D_GRP + h * DK:D_GRP + (h + 1) * DK], proj_ref[:, _grp(6, h)], gates,
                     proj_ref[:, _grp(7, h)], mnw_ref[:, hd])
            _, mvjp = jax.vjp(functools.partial(_ml_head, h=h), *margs)
            dc, dn, dm, dq, dk, dv, dg, dog, dmn = mvjp(
                (dcs[h], dns[h], _pick_lane(dms[h], 0), dy_ref[:, pl.ds(D_GRP + h * DK, DK)]))
            dcs[h] = dc
            dns[h] = dn
            dms[h] = jnp.broadcast_to(dm, (1, DK))
            dqk[:, hd] = dq
            dqk[:, pl.ds(D_GRP + h * DK, DK)] = dk
            dproj_ref[:, _grp(6, h)] = dv
            dproj_ref[:, _grp(7, h)] = dog
            dmnw_ref[:, hd] += dmn
            dgates = dgates + dg
        dproj_ref[:, pl.ds(8 * D_GRP, LANES)] = dgates
        dh, dx, dw0, dw1, dw2, dw3, db = conv_vjp(dqk[...])
        tail = jnp.concatenate([jnp.zeros((CHUNK - SUBLANES, 2 * D_GRP), F32), dhalo[...]], axis=0)
        dproj_ref[:, pl.ds(4 * D_GRP, 2 * D_GRP)] = dx + tail
        dhalo[...] = dh
        dcw_ref[0:1, :] += dw0
        dcw_ref[1:2, :] += dw1
        dcw_ref[2:3, :] += dw2
        dcw_ref[3:4, :] += dw3
        dcb_ref[...] += db

    small_out = [pl.BlockSpec((2, D_GRP), lambda c: (0, 0)), pl.BlockSpec((1, D_GRP), lambda c: (0, 0)),
                 pl.BlockSpec((ML_CONV, 2 * D_GRP), lambda c: (0, 0)), pl.BlockSpec((1, 2 * D_GRP), lambda c: (0, 0)),
                 pl.BlockSpec((1, D_GRP), lambda c: (0, 0))]
    vmem = 2 * (2 * _nbytes((CHUNK, D_IN_PAD), F32) + _nbytes((CHUNK, 2 * D_GRP), F32)
                + 2 * _nbytes((HEADS, DK, DK), F32)) + 2 * _nbytes((HEADS, DK, DK), F32) + 4 * 1024 * 1024
    return _pcall(
        body, name="mixer_bwd", grid=(n_chunks,),
        in_specs=[proj_spec, halo_spec, y_spec] + state_specs + small,
        out_specs=[proj_spec] + small_out,
        out_shape=[jax.ShapeDtypeStruct((seq, D_IN_PAD), F32), jax.ShapeDtypeStruct((2, D_GRP), F32),
                   jax.ShapeDtypeStruct((1, D_GRP), F32), jax.ShapeDtypeStruct((ML_CONV, 2 * D_GRP), F32),
                   jax.ShapeDtypeStruct((1, 2 * D_GRP), F32), jax.ShapeDtypeStruct((1, D_GRP), F32)],
        scratch_shapes=[pltpu.VMEM((HEADS, DK, DK), F32), pltpu.VMEM((HEADS, DK, DK), F32),
                        pltpu.VMEM((HEADS, 1, DK), F32), pltpu.VMEM((HEADS, 1, DK), F32),
                        pltpu.VMEM((SUBLANES, 2 * D_GRP), F32), pltpu.VMEM((CHUNK, 2 * D_GRP), F32)],
        compiler_params=_params(("arbitrary",), vmem),
    )(proj, proj, dy, hst, cst, nst, mst, lb_logits, hg_nw, conv_w, conv_b, ml_nw)


def _tile(n, prefs, unit=None):
    unit = unit or n
    for p in prefs:
        if unit % p == 0 and n % p == 0:
            return p
    return unit


def _logical(arr):
    return arr.shape if arr.ndim == 2 else (arr.shape[1], arr.shape[0] * arr.shape[2])


def _group(arr):
    return arr.shape[-1]


def _split_spec(ndim, group, tr, tc, where):
    if ndim == 2:
        return pl.BlockSpec((tr, tc), where)
    per = group // tc
    assert per * tc == group, (group, tc)

    def index(*ids):
        bi, bj = where(*ids)
        return (bj // per, bi, bj % per)
    return pl.BlockSpec((None, tr, tc), index)


def _mm(name, mode, a, b, *, bias=None, res=None, res_scale=1.0, ln=None, out_dtype=F32, out_groups=None,
        tm=None, tn=None, tk=None):
    la, lb = _logical(a), _logical(b)
    if mode == "nn":
        (m, k), n = la, lb[1]
        n_unit, k_unit = (_group(b) if b.ndim == 3 else n), (_group(a) if a.ndim == 3 else k)
    elif mode == "nt":
        (m, k), n = la, lb[0]
        n_unit, k_unit = n, min(_group(a) if a.ndim == 3 else k, _group(b) if b.ndim == 3 else k)
    else:
        (k, m), n = la, lb[1]
        n_unit, k_unit = (_group(b) if b.ndim == 3 else n), k
        assert a.ndim == 2
    if out_groups:
        n_unit = min(n_unit, n // out_groups)
    kind = ln[0] if ln else None
    tm = tm or (256 if ln else _tile(m, (512, 256, 128)))
    tn = n if ln else (tn or _tile(n, (512, 384, 256, 128), n_unit))
    tk = tk or (k if (k <= D_FF and k_unit == k) else _tile(k, (1024, 512, 384, 256, 128), k_unit))
    gi, gj, gk = m // tm, n // tn, k // tk
    assert gi * tm == m and gj * tn == n and gk * tk == k, (name, m, n, k, tm, tn, tk)
    ca, cb = {"nn": (1, 0), "nt": (1, 1), "tn": (0, 0)}[mode]
    if mode == "tn":
        a_spec = _split_spec(a.ndim, _group(a), tk, tm, lambda i, j, kk: (kk, i))
    else:
        a_spec = _split_spec(a.ndim, _group(a), tm, tk, lambda i, j, kk: (i, kk))
    if mode == "nt":
        b_spec = _split_spec(b.ndim, _group(b), tn, tk, lambda i, j, kk: (j, kk))
    else:
        b_spec = _split_spec(b.ndim, _group(b), tk, tn, lambda i, j, kk: (kk, j))
    row_spec = pl.BlockSpec((1, tn), lambda i, j, kk: (0, j))
    blk_spec = pl.BlockSpec((tm, tn), lambda i, j, kk: (i, j))
    ins, in_specs = [a, b], [a_spec, b_spec]
    if bias is not None:
        ins.append(bias), in_specs.append(row_spec)
    if res is not None:
        ins.append(res), in_specs.append(blk_spec)
    if kind == "fwd":
        ins += [ln[1], ln[2]]
        in_specs += [row_spec, row_spec]
    elif kind == "loss":
        ins += [ln[1], ln[2], ln[3]]
        in_specs += [row_spec, row_spec, blk_spec]
    elif kind == "bwd":
        ins += [ln[1], ln[2], ln[3]]
        in_specs += [blk_spec, row_spec, row_spec]
    if out_groups:
        blk_out = jax.ShapeDtypeStruct((out_groups, m, n // out_groups), out_dtype)
        out_spec = _split_spec(3, n // out_groups, tm, tn, lambda i, j, kk: (i, j))
    else:
        blk_out, out_spec = jax.ShapeDtypeStruct((m, n), out_dtype), blk_spec
    row_out = jax.ShapeDtypeStruct((1, n), F32)
    if kind is None:
        out_shape, out_specs = [blk_out], [out_spec]
    elif kind == "fwd":
        out_shape, out_specs = [blk_out, blk_out], [blk_spec, blk_spec]
    else:
        out_shape, out_specs = [blk_out, row_out, row_out], [blk_spec, row_spec, row_spec]
        if kind == "loss":
            out_shape.append(jax.ShapeDtypeStruct((1, LANES), F32))
            out_specs.append(pl.BlockSpec((1, LANES), lambda i, j, kk: (0, 0)))
    n_in = len(ins)

    def body(*refs):
        in_refs, out_refs, acc_ref = refs[:n_in], refs[n_in:n_in + len(out_shape)], refs[-1]
        i, kk = pl.program_id(0), pl.program_id(2)
        extra = list(in_refs[2:])

        def epilogue(acc):
            rest = list(extra)
            if bias is not None:
                acc = acc + rest.pop(0)[...]
            if res is not None:
                acc = acc + res_scale * rest.pop(0)[...]
            if kind is None:
                out_refs[0][...] = acc.astype(out_dtype)
                return
            if kind == "fwd":
                out_refs[0][...] = acc
                out_refs[1][...] = _layer_norm(acc, rest[0][...], rest[1][...])
                return
            if kind == "loss":
                y, vjp = jax.vjp(_layer_norm, acc, rest[0][...], rest[1][...])
                err = y - rest[2][...]
                part = 0.5 * jnp.sum(jnp.sum(err * err, axis=1, keepdims=True), axis=0, keepdims=True) / n
                dz, dg, db = vjp(err / n)
            else:
                _, vjp = jax.vjp(_layer_norm, rest[0][...], rest[1][...], rest[2][...])
                dz, dg, db = vjp(acc)

            @pl.when(i == 0)
            def _():
                for r in out_refs[1:]:
                    r[...] = jnp.zeros_like(r)

            out_refs[0][...] = dz
            out_refs[1][...] += dg
            out_refs[2][...] += db
            if kind == "loss":
                out_refs[3][...] += jnp.broadcast_to(part, (1, LANES))

        prod = _dg(in_refs[0][...], in_refs[1][...], ca, cb)
        if gk == 1:
            epilogue(prod)
            return

        @pl.when(kk == 0)
        def _():
            acc_ref[...] = prod

        @pl.when(kk > 0)
        def _():
            acc_ref[...] += prod

        @pl.when(kk == gk - 1)
        def _():
            epilogue(acc_ref[...])

    vmem = 2 * (_nbytes((tm, tk), a.dtype) + _nbytes((tk, tn), b.dtype)) + (2 * len(ins) + 2 * len(out_shape) + 1) * _nbytes((tm, tn), F32)
    outs = _pcall(
        body, name=name, grid=(gi, gj, gk), in_specs=in_specs, out_specs=out_specs, out_shape=out_shape,
        scratch_shapes=[pltpu.VMEM((tm, tn) if gk > 1 else (SUBLANES, LANES), F32)],
        compiler_params=_params(("arbitrary", "arbitrary", "arbitrary"), vmem),
    )(*ins)
    return outs[0] if kind is None else outs


def _colsum(name, a):
    m, n = a.shape
    tm = _tile(m, (512, 256, 128))

    def body(a_ref, o_ref):
        @pl.when(pl.program_id(0) == 0)
        def _():
            o_ref[...] = jnp.zeros_like(o_ref)

        o_ref[...] += jnp.sum(a_ref[...].astype(F32), axis=0, keepdims=True)

    return _pcall(
        body, name=name, grid=(m // tm,), in_specs=[pl.BlockSpec((tm, n), lambda i: (i, 0))],
        out_specs=pl.BlockSpec((1, n), lambda i: (0, 0)), out_shape=jax.ShapeDtypeStruct((1, n), F32),
        compiler_params=_params(("arbitrary",), 2 * _nbytes((tm, n), a.dtype)),
    )(a)


def _attn_head(q, k, v):
    sc = mm_nt(q, k) * (CA_DH ** -0.5)
    e = jnp.exp(sc - jnp.max(sc, axis=-1, keepdims=True))
    return mm_nn(e / jnp.sum(e, axis=-1, keepdims=True), v)


def _attn_fwd(q, kv):
    seq, n_mem = q.shape[0], kv.shape[0]
    tq = _tile(seq, (512, 256, 128))

    def body(q_ref, kv_ref, o_ref):
        for h in range(HEADS):
            hd = pl.ds(h * CA_DH, CA_DH)
            o_ref[:, hd] = _attn_head(q_ref[:, hd], kv_ref[:, hd], kv_ref[:, pl.ds(D_MODEL + h * CA_DH, CA_DH)])

    return _pcall(
        body, name="attn_fwd", grid=(seq // tq,),
        in_specs=[pl.BlockSpec((tq, D_MODEL), lambda i: (i, 0)), pl.BlockSpec((n_mem, 2 * D_MODEL), lambda i: (0, 0))],
        out_specs=pl.BlockSpec((tq, D_MODEL), lambda i: (i, 0)), out_shape=jax.ShapeDtypeStruct((seq, D_MODEL), F32),
        compiler_params=_params(("arbitrary",), 4 * _nbytes((tq, D_MODEL), F32) + 2 * _nbytes((n_mem, 2 * D_MODEL), F32)),
    )(q, kv)


def _attn_bwd(q, kv, do):
    seq, n_mem = q.shape[0], kv.shape[0]
    tq = _tile(seq, (512, 256, 128))

    def body(q_ref, kv_ref, do_ref, dq_ref, dkv_ref):
        @pl.when(pl.program_id(0) == 0)
        def _():
            dkv_ref[...] = jnp.zeros_like(dkv_ref)

        for h in range(HEADS):
            hd = pl.ds(h * CA_DH, CA_DH)
            vd = pl.ds(D_MODEL + h * CA_DH, CA_DH)
            _, vjp = jax.vjp(_attn_head, q_ref[:, hd], kv_ref[:, hd], kv_ref[:, vd])
            dq, dk, dv = vjp(do_ref[:, hd])
            dq_ref[:, hd] = dq
            dkv_ref[:, hd] += dk
            dkv_ref[:, vd] += dv

    return _pcall(
        body, name="attn_bwd", grid=(seq // tq,),
        in_specs=[pl.BlockSpec((tq, D_MODEL), lambda i: (i, 0)), pl.BlockSpec((n_mem, 2 * D_MODEL), lambda i: (0, 0)),
                  pl.BlockSpec((tq, D_MODEL), lambda i: (i, 0))],
        out_specs=[pl.BlockSpec((tq, D_MODEL), lambda i: (i, 0)), pl.BlockSpec((n_mem, 2 * D_MODEL), lambda i: (0, 0))],
        out_shape=[jax.ShapeDtypeStruct((seq, D_MODEL), F32), jax.ShapeDtypeStruct((n_mem, 2 * D_MODEL), F32)],
        compiler_params=_params(("arbitrary",), 6 * _nbytes((tq, D_MODEL), F32) + 4 * _nbytes((n_mem, 2 * D_MODEL), F32)),
    )(q, kv, do)


FFN_TB = 512
FFN_TC = 256


def _ffn_mid(hg, xg, hv, xv, wg0, wg1, wg2, bg, wv0, wv1, wv2, bv):
    return jax.nn.gelu(causal_conv(hg, xg, (wg0, wg1, wg2), bg)) * causal_conv(hv, xv, (wv0, wv1, wv2), bv)


def _ffn_specs(seq, reverse):
    tb = min(FFN_TB, seq)
    nt = seq // tb
    row8 = tb // SUBLANES

    def tt(t):
        return nt - 1 - t if reverse else t
    nj = D_FF // FFN_TC
    main = pl.BlockSpec((tb, FFN_TC), lambda j, t: (tt(t), j))
    ins = []
    for off in (0, nj):
        ins += [pl.BlockSpec((tb, FFN_TC), lambda j, t, off=off: (tt(t), j + off)),
                pl.BlockSpec((SUBLANES, FFN_TC), lambda j, t, off=off: (jnp.maximum(tt(t) * row8 - 1, 0), j + off))]
    for off in (0, nj):
        ins += [pl.BlockSpec((FFN_CONV, FFN_TC), lambda j, t, off=off: (0, j + off)),
                pl.BlockSpec((1, FFN_TC), lambda j, t, off=off: (0, j + off))]
    return tb, nt, main, ins


def _ffn_args(c_first, ug, hg, uv, hv, wg, bg, wv, bv):
    halo_g = jnp.where(c_first, 0.0, hg[...])
    halo_v = jnp.where(c_first, 0.0, hv[...])
    return (halo_g, ug[...], halo_v, uv[...], wg[0:1, :], wg[1:2, :], wg[2:3, :], bg[...],
            wv[0:1, :], wv[1:2, :], wv[2:3, :], bv[...])


def _ffn_mid_fwd(u, conv_w, conv_b):
    seq = u.shape[0]
    tb, nt, main, ins = _ffn_specs(seq, False)

    def body(ug, hg, uv, hv, wg, bg, wv, bv, o_ref):
        o_ref[...] = _ffn_mid(*_ffn_args(pl.program_id(1) == 0, ug, hg, uv, hv, wg, bg, wv, bv))

    return _pcall(
        body, name="ffn_mid_fwd", grid=(D_FF // FFN_TC, nt), in_specs=ins, out_specs=main,
        out_shape=jax.ShapeDtypeStruct((seq, D_FF), F32),
        compiler_params=_params(("arbitrary", "arbitrary"), 12 * _nbytes((tb, FFN_TC), F32)),
    )(u, u, u, u, conv_w, conv_b, conv_w, conv_b)


def _ffn_mid_bwd(u, conv_w, conv_b, dh):
    seq = u.shape[0]
    tb, nt, main, ins = _ffn_specs(seq, True)

    def body(ug, hg, uv, hv, wg, bg, wv, bv, dh_ref, du, dw, db, carry):
        t = pl.program_id(1)

        @pl.when(t == 0)
        def _():
            for r in (dw, db, carry):
                r[...] = jnp.zeros_like(r)

        _, vjp = jax.vjp(_ffn_mid, *_ffn_args(t == nt - 1, ug, hg, uv, hv, wg, bg, wv, bv))
        dhg, dxg, dhv, dxv, g0, g1, g2, gb, v0, v1, v2, vb = vjp(dh_ref[...])
        zeros = jnp.zeros((tb - SUBLANES, FFN_TC), F32)
        du[0] = dxg + jnp.concatenate([zeros, carry[0]], axis=0)
        du[1] = dxv + jnp.concatenate([zeros, carry[1]], axis=0)
        carry[0] = dhg
        carry[1] = dhv
        for half, parts in enumerate(((g0, g1, g2), (v0, v1, v2))):
            for d, p in enumerate(parts):
                dw[half, d:d + 1, :] += p
        db[0] += gb
        db[1] += vb

    def grouped(rows, index):
        return pl.BlockSpec((2, rows, FFN_TC), index)
    return _pcall(
        body, name="ffn_mid_bwd", grid=(D_FF // FFN_TC, nt), in_specs=ins + [main],
        out_specs=[grouped(tb, lambda j, t: (0, nt - 1 - t, j)), grouped(FFN_CONV, lambda j, t: (0, 0, j)),
                   grouped(1, lambda j, t: (0, 0, j))],
        out_shape=[jax.ShapeDtypeStruct((2, seq, D_FF), F32), jax.ShapeDtypeStruct((2, FFN_CONV, D_FF), F32),
                   jax.ShapeDtypeStruct((2, 1, D_FF), F32)],
        scratch_shapes=[pltpu.VMEM((2, SUBLANES, FFN_TC), F32)],
        compiler_params=_params(("arbitrary", "arbitrary"), 24 * _nbytes((tb, FFN_TC), F32)),
    )(u, u, u, u, conv_w, conv_b, conv_w, conv_b, dh)


def _adamw_math(w, g, m, v):
    m_new = ADAM_B1 * m + (1.0 - ADAM_B1) * g
    v_new = ADAM_B2 * v + (1.0 - ADAM_B2) * jnp.square(g)
    m_hat = m_new / (1.0 - ADAM_B1 ** ADAM_STEP)
    v_hat = v_new / (1.0 - ADAM_B2 ** ADAM_STEP)
    return -ADAM_LR * (m_hat / (jnp.sqrt(v_hat) + ADAM_EPS) + ADAM_WD * w), m_new, v_new


def _adamw(name, w, g, m, v):
    rows, cols = w.shape
    tr = _tile(rows, (256, 176, 128, 64, 40, 32, 16, 8))

    def body(w_ref, g_ref, m_ref, v_ref, d_ref, nm_ref, nv_ref):
        d_ref[...], nm_ref[...], nv_ref[...] = _adamw_math(w_ref[...], g_ref[...], m_ref[...], v_ref[...])

    spec = pl.BlockSpec((tr, cols), lambda i: (i, 0))
    sh = jax.ShapeDtypeStruct((rows, cols), F32)
    return _pcall(
        body, name=name, grid=(rows // tr,), in_specs=[spec] * 4, out_specs=[spec] * 3, out_shape=[sh] * 3,
        compiler_params=_params(("arbitrary",), 14 * _nbytes((tr, -(-cols // LANES) * LANES), F32)),
    )(w, g, m, v)


def _adamw_halves(name, core, w, mine, theirs, m, v):
    rows, cols = w.shape
    tr = _tile(rows // 2, (256, 176, 128))
    nbh = rows // 2 // tr

    def body(c_ref, w_ref, a_ref, b_ref, m_ref, v_ref, g_ref, d_ref, nm_ref, nv_ref):
        g = jnp.where(pl.program_id(0) // nbh == c_ref[0], a_ref[...], b_ref[...])
        g_ref[...] = g
        d_ref[...], nm_ref[...], nv_ref[...] = _adamw_math(w_ref[...], g, m_ref[...], v_ref[...])

    spec = pl.BlockSpec((tr, cols), lambda i, c_ref: (i, 0))
    half = pl.BlockSpec((tr, cols), lambda i, c_ref: (i % nbh, 0))
    sh = jax.ShapeDtypeStruct((rows, cols), F32)
    grid_spec = pltpu.PrefetchScalarGridSpec(
        num_scalar_prefetch=1, grid=(rows // tr,), in_specs=[spec, half, half, spec, spec], out_specs=[spec] * 4)
    return _pcall(
        body, name=name, grid_spec=grid_spec, out_shape=[sh] * 4,
        compiler_params=_params(("arbitrary",), 18 * _nbytes((tr, -(-cols // LANES) * LANES), F32)),
    )(core, w, mine, theirs, m, v)


MESH = pl.DeviceIdType.MESH
ANY = pl.BlockSpec(memory_space=pl.ANY)
N_CHIPS = 4
N_DEV = 8
BF16_ROWS = 16


def _me():
    return lax.axis_index("x"), lax.axis_index("y"), lax.axis_index("c")


def _other_chips(x, y):
    return [(1 - x, y), (x, 1 - y), (1 - x, 1 - y)]


def _remote(src, dst, ssem, rsem, dev):
    return pltpu.make_async_remote_copy(src_ref=src, dst_ref=dst, send_sem=ssem, recv_sem=rsem,
                                        device_id=dev, device_id_type=MESH)


def _half_rows(ref_rows, cc):
    half = ref_rows // 2
    return pl.ds(pl.multiple_of(cc * half, BF16_ROWS), half)


def _gather_weights(shards):
    n = len(shards)
    n_ici = n * (N_CHIPS - 1)

    def body(*refs):
        ins, outs, (ssem, rsem, lsem, lrsem) = refs[:n], refs[n:2 * n], refs[2 * n:]
        x, y, c = _me()
        k_me = 2 * x + y
        sib = (x, y, 1 - c)
        chips = _other_chips(x, y)
        started = []
        for i, (w_ref, o_ref) in enumerate(zip(ins, outs)):
            cp = _remote(w_ref, o_ref.at[k_me], lsem.at[i], lrsem.at[i], sib)
            cp.start()
            started.append(cp)
        for r, (px, py) in enumerate(chips):
            for i, (w_ref, o_ref) in enumerate(zip(ins, outs)):
                rows = _half_rows(w_ref.shape[0], c)
                s = r * n + i
                cp = _remote(w_ref.at[rows], o_ref.at[k_me, rows], ssem.at[s], rsem.at[s], (px, py, c))
                cp.start()
                started.append(cp)
        for r, (px, py) in enumerate(chips):
            for i, o_ref in enumerate(outs):
                blk = o_ref.at[2 * px + py, _half_rows(o_ref.shape[1], c)]
                s = r * n + i
                _remote(blk, blk, ssem.at[s], rsem.at[s], (px, py, c)).wait_recv()
                cp = _remote(blk, blk, ssem.at[n_ici + s], rsem.at[n_ici + s], sib)
                cp.start()
                started.append(cp)
        for r, (px, py) in enumerate(chips):
            for i, o_ref in enumerate(outs):
                blk = o_ref.at[2 * px + py, _half_rows(o_ref.shape[1], 1 - c)]
                s = n_ici + r * n + i
                _remote(blk, blk, ssem.at[s], rsem.at[s], sib).wait_recv()
        for cp in started[n:]:
            cp.wait_send()
        for cp in started[:n]:
            cp.wait()

    return _pcall(
        body, name="gather_weights", in_specs=[ANY] * n, out_specs=[ANY] * n,
        out_shape=[jax.ShapeDtypeStruct((N_CHIPS,) + s.shape, s.dtype) for s in shards],
        scratch_shapes=[pltpu.SemaphoreType.DMA((2 * n_ici,)), pltpu.SemaphoreType.DMA((2 * n_ici,)),
                        pltpu.SemaphoreType.DMA((n,)), pltpu.SemaphoreType.DMA((n,))],
    )(*shards)


def _swap_halves(grads):
    n = len(grads)

    def body(*refs):
        ins, outs, (ssem, rsem) = refs[:n], refs[n:2 * n], refs[2 * n:]
        x, y, c = _me()
        copies = []
        for i, (g_ref, o_ref) in enumerate(zip(ins, outs)):
            for k in range(N_CHIPS):
                s = i * N_CHIPS + k
                cp = _remote(g_ref.at[k, _half_rows(g_ref.shape[1], 1 - c)], o_ref.at[k], ssem.at[s], rsem.at[s],
                             (x, y, 1 - c))
                cp.start()
                copies.append(cp)
        for cp in copies:
            cp.wait()

    return _pcall(
        body, name="swap_halves", in_specs=[ANY] * n, out_specs=[ANY] * n,
        out_shape=[jax.ShapeDtypeStruct((N_CHIPS, g.shape[1] // 2, g.shape[2]), g.dtype) for g in grads],
        scratch_shapes=[pltpu.SemaphoreType.DMA((n * N_CHIPS,)), pltpu.SemaphoreType.DMA((n * N_CHIPS,))],
    )(*grads)


def _scatter_chips(parts):
    n = len(parts)

    def body(*refs):
        ins, outs, (ssem, rsem) = refs[:n], refs[n:2 * n], refs[2 * n:]
        x, y, c = _me()
        k_me = 2 * x + y
        chips = _other_chips(x, y)
        sends = []
        for r, (px, py) in enumerate(chips):
            for i, (p_ref, o_ref) in enumerate(zip(ins, outs)):
                s = r * n + i
                cp = _remote(p_ref.at[2 * px + py], o_ref.at[k_me], ssem.at[s], rsem.at[s], (px, py, c))
                cp.start()
                sends.append(cp)
        for r, (px, py) in enumerate(chips):
            for i, o_ref in enumerate(outs):
                blk = o_ref.at[2 * px + py]
                s = r * n + i
                _remote(blk, blk, ssem.at[s], rsem.at[s], (px, py, c)).wait_recv()
        for cp in sends:
            cp.wait_send()

    n_sem = n * (N_CHIPS - 1)
    return _pcall(
        body, name="scatter_chips", in_specs=[ANY] * n, out_specs=[ANY] * n,
        out_shape=[jax.ShapeDtypeStruct(p.shape, p.dtype) for p in parts],
        scratch_shapes=[pltpu.SemaphoreType.DMA((n_sem,)), pltpu.SemaphoreType.DMA((n_sem,))],
    )(*parts)


def _share_halves(halves):
    n = len(halves)

    def body(*refs):
        ins, outs, (ssem, rsem) = refs[:n], refs[n:2 * n], refs[2 * n:]
        x, y, c = _me()
        copies = [_remote(r_ref, o_ref, ssem.at[i], rsem.at[i], (x, y, 1 - c))
                  for i, (r_ref, o_ref) in enumerate(zip(ins, outs))]
        for cp in copies:
            cp.start()
        for cp in copies:
            cp.wait()

    return _pcall(
        body, name="share_halves", in_specs=[ANY] * n, out_specs=[ANY] * n,
        out_shape=[jax.ShapeDtypeStruct(h.shape, h.dtype) for h in halves],
        scratch_shapes=[pltpu.SemaphoreType.DMA((n,)), pltpu.SemaphoreType.DMA((n,))],
    )(*halves)


def _exchange_small(v, reduce):
    rows = v.shape[0]

    def body(v_ref, out_ref, buf, ssem, rsem):
        x, y, c = _me()
        me = 4 * x + 2 * y + c
        peers = [((x + bx) % 2, (y + by) % 2, (c + bc) % 2)
                 for bx in (0, 1) for by in (0, 1) for bc in (0, 1) if (bx, by, bc) != (0, 0, 0)]
        dst = buf if reduce else out_ref
        dst[me] = v_ref[...]
        sends = [_remote(v_ref, dst.at[me], ssem.at[r], rsem.at[r], p) for r, p in enumerate(peers)]
        for cp in sends:
            cp.start()
        for r, (px, py, pc) in enumerate(peers):
            blk = dst.at[4 * px + 2 * py + pc]
            _remote(blk, blk, ssem.at[r], rsem.at[r], (px, py, pc)).wait_recv()
        if reduce:
            acc = buf[0]
            for d in range(1, N_DEV):
                acc = acc + buf[d]
            out_ref[...] = acc
        for cp in sends:
            cp.wait_send()

    vm = pl.BlockSpec(memory_space=pltpu.VMEM)
    out_shape = jax.ShapeDtypeStruct((rows, LANES) if reduce else (N_DEV, rows, LANES), F32)
    buf_shape = (N_DEV, rows, LANES) if reduce else (SUBLANES, LANES)
    return _pcall(
        body, name="reduce_small" if reduce else "gather_small", in_specs=[vm], out_specs=vm, out_shape=out_shape,
        scratch_shapes=[pltpu.VMEM(buf_shape, F32), pltpu.SemaphoreType.DMA((N_DEV - 1,)),
                        pltpu.SemaphoreType.DMA((N_DEV - 1,))],
        compiler_params=pltpu.CompilerParams(vmem_limit_bytes=32 * 1024 * 1024),
    )(v)


def _add_pair(name, core, g, theirs):
    _, half, cols = theirs.shape
    tr = _tile(half, (256, 176, 128))
    nb = half // tr

    def body(c_ref, g_ref, t_ref, o32_ref, o16_ref):
        s = g_ref[...] + t_ref[...]
        o32_ref[...] = s
        o16_ref[...] = s.astype(BF16)

    spec = pl.BlockSpec((None, tr, cols), lambda k, i, c_ref: (k, i, 0))
    grid_spec = pltpu.PrefetchScalarGridSpec(
        num_scalar_prefetch=1, grid=(N_CHIPS, nb),
        in_specs=[pl.BlockSpec((None, tr, cols), lambda k, i, c_ref: (k, c_ref[0] * nb + i, 0)), spec],
        out_specs=[spec, spec])
    return _pcall(
        body, name=name, grid_spec=grid_spec,
        out_shape=[jax.ShapeDtypeStruct(theirs.shape, F32), jax.ShapeDtypeStruct(theirs.shape, BF16)],
        compiler_params=_params(("arbitrary", "arbitrary"), 8 * _nbytes((tr, cols + LANES), F32)),
    )(core, g, theirs)


def _add_chips(name, chip, p32, recv):
    _, half, cols = p32.shape
    tr = _tile(half, (256, 176, 128))

    def body(k_ref, p_ref, r0_ref, r1_ref, r2_ref, o_ref):
        o_ref[...] = ((p_ref[...] + r0_ref[...].astype(F32)) + r1_ref[...].astype(F32)) + r2_ref[...].astype(F32)

    def other(r):
        return pl.BlockSpec((None, tr, cols), lambda i, k_ref: (r + (k_ref[0] <= r).astype(jnp.int32), i, 0))
    grid_spec = pltpu.PrefetchScalarGridSpec(
        num_scalar_prefetch=1, grid=(half // tr,),
        in_specs=[pl.BlockSpec((None, tr, cols), lambda i, k_ref: (k_ref[0], i, 0)), other(0), other(1), other(2)],
        out_specs=pl.BlockSpec((tr, cols), lambda i, k_ref: (i, 0)))
    return _pcall(
        body, name=name, grid_spec=grid_spec, out_shape=jax.ShapeDtypeStruct((half, cols), F32),
        compiler_params=_params(("arbitrary",), 10 * _nbytes((tr, cols + LANES), F32)),
    )(chip, p32, recv, recv, recv)


def kernel(x, mem, w_in, b_in, hg_lb_logits, hg_norm_w, ml_conv_w, ml_conv_b, ml_norm_w, w_out, ln1_g, ln1_b, ca_wq, ca_wkv, ca_wo, ln2_g, ln2_b, ffn_w_up, ffn_conv_w, ffn_conv_b, ffn_w_down, ln3_g, ln3_b, loss_target, m_w_in, m_b_in, m_hg_lb_logits, m_hg_norm_w, m_ml_conv_w, m_ml_conv_b, m_ml_norm_w, m_w_out, m_ln1_g, m_ln1_b, m_ca_wq, m_ca_wkv, m_ca_wo, m_ln2_g, m_ln2_b, m_ffn_w_up, m_ffn_conv_w, m_ffn_conv_b, m_ffn_w_down, m_ln3_g, m_ln3_b, v_w_in, v_b_in, v_hg_lb_logits, v_hg_norm_w, v_ml_conv_w, v_ml_conv_b, v_ml_norm_w, v_w_out, v_ln1_g, v_ln1_b, v_ca_wq, v_ca_wkv, v_ca_wo, v_ln2_g, v_ln2_b, v_ffn_w_up, v_ffn_conv_w, v_ffn_conv_b, v_ffn_w_down, v_ln3_g, v_ln3_b):
    return _train_step(dict(locals()))


WEIGHTS = ("w_in", "b_in", "hg_lb_logits", "hg_norm_w", "ml_conv_w", "ml_conv_b", "ml_norm_w", "w_out", "ln1_g",
           "ln1_b", "ca_wq", "ca_wkv", "ca_wo", "ln2_g", "ln2_b", "ffn_w_up", "ffn_conv_w", "ffn_conv_b",
           "ffn_w_down", "ln3_g", "ln3_b")
MATRICES = ("w_in", "w_out", "ca_wq", "ca_wkv", "ca_wo", "ffn_w_up", "ffn_w_down")
COL_SHARDED = ("w_in", "ca_wkv", "ffn_w_up", "ml_conv_w", "ffn_conv_w")
SMALL = tuple(n for n in WEIGHTS if n not in MATRICES)
PART_ROWS = 16


def _part_rows(shape, lead):
    n = 1
    for s in shape[lead:]:
        n *= s
    return -(-n // (LANES * PART_ROWS)) * PART_ROWS


def _pack(arrs, dtype, lead=0, rows=None):
    parts = []
    for a in arrs:
        head = a.shape[:lead]
        flat = a.reshape(head + (-1,)).astype(dtype)
        pad = _part_rows(a.shape, lead) * LANES - flat.shape[-1]
        flat = jnp.pad(flat, [(0, 0)] * lead + [(0, pad)])
        parts.append(flat.reshape(head + (-1, LANES)))
    used = sum(p.shape[lead] for p in parts)
    if rows is not None and rows > used:
        parts.append(jnp.zeros(parts[0].shape[:lead] + (rows - used, LANES), dtype))
    return jnp.concatenate(parts, axis=lead)


def _unpack(buf, shapes):
    lead = buf.shape[:-2]
    outs, r = [], 0
    for sh in shapes:
        n = 1
        for s in sh:
            n *= s
        nr = _part_rows(sh, 0)
        flat = buf[..., r:r + nr, :].reshape(lead + (nr * LANES,))
        outs.append(flat[..., :n].reshape(lead + tuple(sh)))
        r += nr
    return outs


def _cat_cols(s):
    return jnp.moveaxis(s, 0, 1).reshape(s.shape[1], -1)


def _split_cols(g):
    return jnp.moveaxis(g.reshape(g.shape[0], N_CHIPS, -1), 1, 0)


def _stack_rows(s):
    return s.reshape(-1, s.shape[-1])


def _train_step(a):
    xs, mems, tgt = a["x"][0], a["mem"][0], a["loss_target"][0]
    core = lax.axis_index("c").astype(jnp.int32).reshape(1)
    chip = (2 * lax.axis_index("x") + lax.axis_index("y")).astype(jnp.int32).reshape(1)
    k_me = chip[0]
    shard = {n: a[n][0] for n in MATRICES}

    w = dict(zip(MATRICES, _gather_weights([shard[n].astype(BF16) for n in MATRICES])))
    for n in ("w_out", "ca_wq", "ca_wo", "ffn_w_down"):
        w[n] = _stack_rows(w[n])
    w["w_in"] = jnp.pad(_cat_cols(w["w_in"]), ((0, 0), (0, D_IN_PAD - D_IN)))
    taps = _exchange_small(_pack([a["ml_conv_w"][0], a["ffn_conv_w"][0]], F32), reduce=False)
    taps = taps.reshape((N_CHIPS, 2) + taps.shape[1:])[:, 0]
    ml_cw, ffn_cw = [_cat_cols(s) for s in _unpack(taps, [a["ml_conv_w"].shape[1:], a["ffn_conv_w"].shape[1:]])]
    b_in_p = jnp.pad(a["b_in"], ((0, 0), (0, D_IN_PAD - D_IN)))
    mixer_w = (a["hg_lb_logits"], a["hg_norm_w"], ml_cw, a["ml_conv_b"], a["ml_norm_w"])
    up_cols = a["ffn_w_up"].shape[-1]

    proj = _mm("proj", "nn", xs, w["w_in"], bias=b_in_p, tm=256, tn=D_IN_PAD)
    y, hst, cst, nst, mst = _mixer_fwd(proj, *mixer_w)
    z1, x1 = _mm("mix_out", "nn", y, w["w_out"], res=xs, res_scale=ALPHA, ln=("fwd", a["ln1_g"], a["ln1_b"]))
    q = _mm("ca_q", "nn", x1, w["ca_wq"])
    kv = _mm("ca_kv", "nn", mems, w["ca_wkv"])
    o = _attn_fwd(q, kv)
    z2, x2 = _mm("ca_out", "nn", o, w["ca_wo"], res=x1, res_scale=ALPHA, ln=("fwd", a["ln2_g"], a["ln2_b"]))
    u = _mm("ffn_up", "nn", x2, w["ffn_w_up"], tn=up_cols)
    hmid = _ffn_mid_fwd(u, ffn_cw, a["ffn_conv_b"])
    dz3, g_ln3g, g_ln3b, loss_part = _mm("ffn_down", "nn", hmid, w["ffn_w_down"], res=x2, res_scale=ALPHA,
                                         ln=("loss", a["ln3_g"], a["ln3_b"], tgt))

    grads = {"ln3_g": g_ln3g, "ln3_b": g_ln3b}
    dhmid = _mm("d_hmid", "nt", dz3, w["ffn_w_down"])
    grads["ffn_w_down"] = _mm("g_w_down", "tn", hmid, dz3)
    du, g_cw, g_cb = _ffn_mid_bwd(u, ffn_cw, a["ffn_conv_b"], dhmid)
    grads["ffn_conv_w"] = jnp.moveaxis(g_cw, 0, 1).reshape(FFN_CONV, 2 * D_FF)
    grads["ffn_conv_b"] = g_cb.reshape(1, 2 * D_FF)
    grads["ffn_w_up"] = _mm("g_w_up", "tn", x2, du, out_groups=N_CHIPS, tn=up_cols)
    dz2, grads["ln2_g"], grads["ln2_b"] = _mm("d_x2", "nt", du, w["ffn_w_up"], res=dz3, res_scale=ALPHA,
                                              ln=("bwd", z2, a["ln2_g"], a["ln2_b"]), tk=up_cols)
    do = _mm("d_o", "nt", dz2, w["ca_wo"])
    grads["ca_wo"] = _mm("g_wo", "tn", o, dz2)
    dq, dkv = _attn_bwd(q, kv, do)
    grads["ca_wq"] = _mm("g_wq", "tn", x1, dq)
    grads["ca_wkv"] = _mm("g_wkv", "tn", mems, dkv, out_groups=N_CHIPS)
    dz1, grads["ln1_g"], grads["ln1_b"] = _mm("d_x1", "nt", dq, w["ca_wq"], res=dz2, res_scale=ALPHA,
                                              ln=("bwd", z1, a["ln1_g"], a["ln1_b"]))
    dy = _mm("d_y", "nt", dz1, w["w_out"])
    grads["w_out"] = _mm("g_w_out", "tn", y, dz1)
    (dproj, grads["hg_lb_logits"], grads["hg_norm_w"], grads["ml_conv_w"], grads["ml_conv_b"],
     grads["ml_norm_w"]) = _mixer_bwd(proj, dy, hst, cst, nst, mst, *mixer_w)
    grads["w_in"] = _split_cols(_mm("g_w_in", "tn", xs, dproj)[:, :D_IN])
    grads["b_in"] = _colsum("g_b_in", dproj)[:, :D_IN]
    dx = _mm("d_x", "nt", dproj, w["w_in"], res=dz1, res_scale=ALPHA, tm=256, tn=D_MODEL, tk=D_IN_PAD)
    for n in ("w_out", "ca_wq", "ca_wo", "ffn_w_down"):
        grads[n] = grads[n].reshape((N_CHIPS,) + shard[n].shape)

    per_chip = [grads[n] for n in MATRICES]
    sums = [_add_pair("add_pair_" + n, core, g, t) for n, g, t in zip(MATRICES, per_chip, _swap_halves(per_chip))]
    recv = _scatter_chips([s16 for _, s16 in sums])
    halves = [_add_chips("add_chips_" + n, chip, s32, r) for n, (s32, _), r in zip(MATRICES, sums, recv)]
    other_halves = _share_halves(halves)

    small_shapes = [grads[n].shape for n in SMALL] + [loss_part.shape]
    summed = _unpack(_exchange_small(_pack([grads[n] for n in SMALL] + [loss_part], F32), reduce=True), small_shapes)
    loss = summed[-1][0, 0]
    for n, g in zip(SMALL, summed[:-1]):
        if n in COL_SHARDED:
            cols = a[n].shape[-1]
            g = lax.dynamic_slice_in_dim(g, k_me * cols, cols, axis=1)
        grads[n] = g

    delta, new_m, new_v = {}, {}, {}
    for n, mine, theirs in zip(MATRICES, halves, other_halves):
        grads[n], delta[n], new_m[n], new_v[n] = _adamw_halves(
            "adamw_" + n, core, shard[n], mine, theirs, a["m_" + n][0], a["v_" + n][0])
    small_w = [a[n][0] if a[n].ndim == 3 else a[n] for n in SMALL]
    small_m = [a["m_" + n][0] if a[n].ndim == 3 else a["m_" + n] for n in SMALL]
    small_v = [a["v_" + n][0] if a[n].ndim == 3 else a["v_" + n] for n in SMALL]
    shapes = [w.shape for w in small_w]
    packed = [_pack(l, F32) for l in (small_w, [grads[n] for n in SMALL], small_m, small_v)]
    for out, buf in zip((delta, new_m, new_v), _adamw("adamw_small", *packed)):
        for n, v in zip(SMALL, _unpack(buf, shapes)):
            out[n] = v

    def shaped(d):
        return [d[n].reshape(a[n].shape) for n in WEIGHTS]
    return (loss, dx[None], *shaped(grads), *shaped(delta), *shaped(new_m), *shaped(new_v))
```

```python
import functools

import jax
import jax.numpy as jnp
from jax import lax
from jax.experimental import pallas as pl
from jax.experimental.pallas import tpu as pltpu

F32 = jnp.float32
BF16 = jnp.bfloat16

D_MODEL = 1024
HEADS = 4
DK = 128
D_GRP = HEADS * DK
CHUNK = 64
ML_CONV = 4
FFN_CONV = 3
D_FF = 2816
CA_DH = D_MODEL // HEADS
DEPTH = 1
ALPHA = (2.0 * DEPTH) ** 0.25
LN_EPS = 1e-5
NEG_BIG = -1e30
D_IN = 8 * D_GRP + 2 * HEADS
D_IN_PAD = 8 * D_GRP + 128
ADAM_LR, ADAM_B1, ADAM_B2, ADAM_EPS, ADAM_WD, ADAM_STEP = 0.001, 0.9, 0.999, 1e-08, 0.01, 10

SUBLANES = 8
LANES = 128
VMEM_BYTES = 64 * 1024 * 1024


def _pcall(body, pin=True, **kw):
    if not pin:
        return _call(body, **kw)
    kw["out_shape"] = jax.tree.map(lambda s: pltpu.HBM(s.shape, s.dtype), kw["out_shape"])
    call = _call(body, **kw)

    def pinned(*args):
        return call(*[pltpu.with_memory_space_constraint(x, pltpu.HBM) if jnp.issubdtype(x.dtype, jnp.floating) else x
                      for x in args])
    return pinned


def _call(body, **kw):
    return pl.pallas_call(body, **kw)


def _params(semantics, vmem_bytes):
    limit = int(min(max(2 * vmem_bytes, 16 * 1024 * 1024), VMEM_BYTES - 8 * 1024 * 1024))
    return pltpu.CompilerParams(dimension_semantics=semantics, vmem_limit_bytes=limit)


def _nbytes(shape, dtype):
    n = 1
    for s in shape:
        n *= s
    return n * jnp.dtype(dtype).itemsize


def _dg(a, b, ca, cb):
    return lax.dot_general(a.astype(BF16), b.astype(BF16), (((ca,), (cb,)), ((), ())),
                           preferred_element_type=F32)


@jax.custom_vjp
def mm_nn(a, b):
    return _dg(a, b, 1, 0)


mm_nn.defvjp(lambda a, b: (_dg(a, b, 1, 0), (a, b)),
             lambda r, g: (_dg(g, r[1], 1, 1).astype(r[0].dtype), _dg(r[0], g, 0, 0).astype(r[1].dtype)))


@jax.custom_vjp
def mm_nt(a, b):
    return _dg(a, b, 1, 1)


mm_nt.defvjp(lambda a, b: (_dg(a, b, 1, 1), (a, b)),
             lambda r, g: (_dg(g, r[1], 1, 0).astype(r[0].dtype), _dg(g, r[0], 0, 0).astype(r[1].dtype)))


@jax.custom_vjp
def mm_tn(a, b):
    return _dg(a, b, 0, 0)


mm_tn.defvjp(lambda a, b: (_dg(a, b, 0, 0), (a, b)),
             lambda r, g: (_dg(r[1], g, 1, 1).astype(r[0].dtype), _dg(r[0], g, 1, 0).astype(r[1].dtype)))


def _hdot(a, b):
    return jnp.dot(a, b, precision=lax.Precision.HIGHEST, preferred_element_type=F32)


def _tri(n, lower):
    r = lax.broadcasted_iota(jnp.int32, (n, n), 0)
    c = lax.broadcasted_iota(jnp.int32, (n, n), 1)
    return ((r >= c) if lower else (r <= c)).astype(F32)


@jax.custom_vjp
def cumsum_rows(x):
    return _hdot(_tri(x.shape[0], True), x)


cumsum_rows.defvjp(lambda x: (_hdot(_tri(x.shape[0], True), x), None),
                   lambda _, g: (_hdot(_tri(g.shape[0], False), g),))


def _shift_impl(halo, x, d):
    xx = jnp.concatenate([halo, x], axis=0)
    return pltpu.roll(xx, d, 0)[SUBLANES:]


@functools.partial(jax.custom_vjp, nondiff_argnums=(2,))
def shift_rows(halo, x, d):
    return _shift_impl(halo, x, d)


def _shift_bwd(d, _, g):
    n = g.shape[0] + SUBLANES
    gg = jnp.concatenate([jnp.zeros((SUBLANES, g.shape[1]), g.dtype), g], axis=0)
    r = pltpu.roll(gg, n - d, 0)
    return r[:SUBLANES], r[SUBLANES:]


shift_rows.defvjp(lambda halo, x, d: (_shift_impl(halo, x, d), None), _shift_bwd)


def causal_conv(halo, x, w_rows, b):
    k = len(w_rows)
    y = b + w_rows[k - 1] * x
    for d in range(1, k):
        y = y + w_rows[k - 1 - d] * shift_rows(halo, x, d)
    return y


def _sigmoid(x):
    return 1.0 / (1.0 + jnp.exp(-x))


def _silu(x):
    return x * _sigmoid(x)


def _log_sigmoid(x):
    return jnp.minimum(x, 0.0) - jnp.log(1.0 + jnp.exp(-jnp.abs(x)))


def _pick_lane(x, j):
    lane = lax.broadcasted_iota(jnp.int32, (1, x.shape[1]), 1)
    return jnp.sum(jnp.where(lane == j, x, 0.0), axis=1, keepdims=True)


def _pick_row(x, i):
    row = lax.broadcasted_iota(jnp.int32, (x.shape[0], 1), 0)
    return jnp.sum(jnp.where(row == i, x, 0.0), axis=0, keepdims=True)


def _col_to_row(e):
    n = e.shape[0]
    eye = lax.broadcasted_iota(jnp.int32, (n, n), 0) == lax.broadcasted_iota(jnp.int32, (n, n), 1)
    return jnp.sum(jnp.where(eye, e, 0.0), axis=0, keepdims=True)


def _layer_norm(z, g, b):
    mu = jnp.mean(z, axis=-1, keepdims=True)
    zc = z - mu
    var = jnp.mean(zc * zc, axis=-1, keepdims=True)
    return zc * lax.rsqrt(var + LN_EPS) * g + b


def _hg_head(st_t, hq, hf, hi, hgate, l0, l1, nw):
    n = hq.shape[0]
    lb = _sigmoid(l0 - l1)
    q = _silu(hq)
    lf = jnp.log(lb + (1.0 - lb) * _sigmoid(hf))
    k = (1.0 - lb) * _sigmoid(-hf)
    b = cumsum_rows(lf)
    b_ref = _pick_row(b, n // 2 - 1)
    b_last = _pick_row(b, n - 1)
    attn = mm_nt(q * jnp.exp(b - b_ref), k * jnp.exp(b_ref - b))
    attn = jnp.where(_tri(n, True) > 0, attn, 0.0)
    o = mm_nn(attn, hi) + mm_nt(q * jnp.exp(b), st_t)
    st_new = jnp.exp(b_last) * st_t + mm_tn(hi, k * jnp.exp(b_last - b))
    y = o * lax.rsqrt(jnp.mean(o * o, axis=-1, keepdims=True) + LN_EPS) * nw * _silu(hgate)
    return st_new, y


def _ml_head(c_st, n_st, m_st, q, k, v, gates, og, nw, h):
    n = q.shape[0]
    ig = _pick_lane(gates, h)
    fl = _log_sigmoid(_pick_lane(gates, HEADS + h))
    qs = q * (DK ** -0.5)
    b = _pick_lane(cumsum_rows(jnp.broadcast_to(fl, (n, LANES))), 0)
    g = jnp.sum(fl, axis=0, keepdims=True)
    d = jnp.where(_tri(n, True) > 0, b + _col_to_row(ig - b), -jnp.inf)
    inter = b + m_st
    m_t = jnp.maximum(inter, jnp.max(d, axis=1, keepdims=True))
    s = mm_nt(qs, k) * jnp.exp(d - m_t)
    w_inter = jnp.exp(inter - m_t)
    num = mm_nn(s, v) + w_inter * mm_nn(qs, c_st)
    den = jnp.sum(s, axis=1, keepdims=True) + w_inter * jnp.sum(qs * n_st, axis=1, keepdims=True)
    h_out = num / jnp.maximum(jnp.abs(den), jnp.exp(-m_t))
    a = g - b + ig
    m_new = jnp.maximum(g + m_st, jnp.max(a, axis=0, keepdims=True))
    decay = jnp.exp(g + m_st - m_new)
    wk = k * jnp.exp(a - m_new)
    c_new = decay * c_st + mm_tn(wk, v)
    n_new = decay * n_st + jnp.sum(wk, axis=0, keepdims=True)
    mu = jnp.mean(h_out, axis=-1, keepdims=True)
    hc = h_out - mu
    var = jnp.mean(hc * hc, axis=-1, keepdims=True)
    y = _sigmoid(og) * (hc * lax.rsqrt(var + LN_EPS) * nw)
    return c_new, n_new, m_new, y


def _qk_conv(halo, x, w0, w1, w2, w3, b):
    return _silu(causal_conv(halo, x, (w0, w1, w2, w3), b))


def _grp(i, h=None):
    if h is None:
        return pl.ds(i * D_GRP, D_GRP)
    return pl.ds(i * D_GRP + h * DK, DK)


def _mixer_specs(n_chunks, reverse):
    def chunk(c):
        return n_chunks - 1 - c if reverse else c
    row8 = CHUNK // SUBLANES
    proj_spec = pl.BlockSpec((CHUNK, D_IN_PAD), lambda c: (chunk(c), 0))
    halo_spec = pl.BlockSpec((SUBLANES, 2 * D_GRP), lambda c: (jnp.maximum(chunk(c) * row8 - 1, 0), 2))
    small = [pl.BlockSpec((2, D_GRP), lambda c: (0, 0)), pl.BlockSpec((1, D_GRP), lambda c: (0, 0)),
             pl.BlockSpec((ML_CONV, 2 * D_GRP), lambda c: (0, 0)), pl.BlockSpec((1, 2 * D_GRP), lambda c: (0, 0)),
             pl.BlockSpec((1, D_GRP), lambda c: (0, 0))]
    state_specs = [pl.BlockSpec((1, HEADS, DK, DK), lambda c: (chunk(c), 0, 0, 0)),
                   pl.BlockSpec((1, HEADS, DK, DK), lambda c: (chunk(c), 0, 0, 0)),
                   pl.BlockSpec((1, HEADS, 1, DK), lambda c: (chunk(c), 0, 0, 0)),
                   pl.BlockSpec((1, HEADS, 1, DK), lambda c: (chunk(c), 0, 0, 0))]
    y_spec = pl.BlockSpec((CHUNK, 2 * D_GRP), lambda c: (chunk(c), 0))
    return proj_spec, halo_spec, small, state_specs, y_spec, chunk


def _mixer_fwd(proj, lb_logits, hg_nw, conv_w, conv_b, ml_nw):
    seq = proj.shape[0]
    n_chunks = seq // CHUNK
    proj_spec, halo_spec, small, state_specs, y_spec, _ = _mixer_specs(n_chunks, False)

    def body(proj_ref, halo_ref, lg_ref, hnw_ref, cw_ref, cb_ref, mnw_ref,
             y_ref, hst_ref, cst_ref, nst_ref, mst_ref, hs, cs, ns, ms):
        c = pl.program_id(0)

        @pl.when(c == 0)
        def _():
            hs[...] = jnp.zeros_like(hs)
            cs[...] = jnp.zeros_like(cs)
            ns[...] = jnp.zeros_like(ns)
            ms[...] = jnp.full(ms.shape, NEG_BIG, F32)

        hst_ref[0] = hs[...]
        cst_ref[0] = cs[...]
        nst_ref[0] = ns[...]
        mst_ref[0] = ms[...]
        halo = jnp.where(c > 0, halo_ref[...], 0.0)
        qk = _qk_conv(halo, proj_ref[:, pl.ds(4 * D_GRP, 2 * D_GRP)],
                      cw_ref[0:1, :], cw_ref[1:2, :], cw_ref[2:3, :], cw_ref[3:4, :], cb_ref[...])
        gates = proj_ref[:, pl.ds(8 * D_GRP, LANES)]
        for h in range(HEADS):
            hd = pl.ds(h * DK, DK)
            st_new, y = _hg_head(hs[h], proj_ref[:, _grp(0, h)], proj_ref[:, _grp(1, h)], proj_ref[:, _grp(2, h)],
                                 proj_ref[:, _grp(3, h)], lg_ref[0:1, hd], lg_ref[1:2, hd], hnw_ref[:, hd])
            hs[h] = st_new
            y_ref[:, hd] = y
            c_new, n_new, m_new, y = _ml_head(
                cs[h], ns[h], _pick_lane(ms[h], 0), qk[:, h * DK:(h + 1) * DK],
                qk[:, D_GRP + h * DK:D_GRP + (h + 1) * DK], proj_ref[:, _grp(6, h)], gates,
                proj_ref[:, _grp(7, h)], mnw_ref[:, hd], h)
            cs[h] = c_new
            ns[h] = n_new
            ms[h] = jnp.broadcast_to(m_new, (1, DK))
            y_ref[:, pl.ds(D_GRP + h * DK, DK)] = y

    st = jax.ShapeDtypeStruct((n_chunks, HEADS, DK, DK), F32)
    vec = jax.ShapeDtypeStruct((n_chunks, HEADS, 1, DK), F32)
    vmem = 2 * (_nbytes((CHUNK, D_IN_PAD), F32) + _nbytes((CHUNK, 2 * D_GRP), F32) + 2 * _nbytes((HEADS, DK, DK), F32)) \
        + 2 * _nbytes((HEADS, DK, DK), F32)
    return _pcall(
        body, name="mixer_fwd", grid=(n_chunks,),
        in_specs=[proj_spec, halo_spec] + small,
        out_specs=[y_spec] + state_specs,
        out_shape=[jax.ShapeDtypeStruct((seq, 2 * D_GRP), F32), st, st, vec, vec],
        scratch_shapes=[pltpu.VMEM((HEADS, DK, DK), F32), pltpu.VMEM((HEADS, DK, DK), F32),
                        pltpu.VMEM((HEADS, 1, DK), F32), pltpu.VMEM((HEADS, 1, DK), F32)],
        compiler_params=_params(("arbitrary",), vmem),
    )(proj, proj, lb_logits, hg_nw, conv_w, conv_b, ml_nw)


def _mixer_bwd(proj, dy, hst, cst, nst, mst, lb_logits, hg_nw, conv_w, conv_b, ml_nw):
    seq = proj.shape[0]
    n_chunks = seq // CHUNK
    proj_spec, halo_spec, small, state_specs, y_spec, _ = _mixer_specs(n_chunks, True)

    def body(proj_ref, halo_ref, dy_ref, hst_ref, cst_ref, nst_ref, mst_ref,
             lg_ref, hnw_ref, cw_ref, cb_ref, mnw_ref,
             dproj_ref, dlg_ref, dhnw_ref, dcw_ref, dcb_ref, dmnw_ref,
             dhs, dcs, dns, dms, dhalo, dqk):
        c = pl.program_id(0)

        @pl.when(c == 0)
        def _():
            for r in (dhs, dcs, dns, dms, dhalo, dlg_ref, dhnw_ref, dcw_ref, dcb_ref, dmnw_ref):
                r[...] = jnp.zeros_like(r)

        first = c == n_chunks - 1
        halo = jnp.where(first, 0.0, halo_ref[...])
        x_qk = proj_ref[:, pl.ds(4 * D_GRP, 2 * D_GRP)]
        conv_args = (halo, x_qk, cw_ref[0:1, :], cw_ref[1:2, :], cw_ref[2:3, :], cw_ref[3:4, :], cb_ref[...])
        qk, conv_vjp = jax.vjp(_qk_conv, *conv_args)
        gates = proj_ref[:, pl.ds(8 * D_GRP, LANES)]
        dgates = jnp.zeros((CHUNK, LANES), F32)
        for h in range(HEADS):
            hd = pl.ds(h * DK, DK)
            args = (hst_ref[0, h], proj_ref[:, _grp(0, h)], proj_ref[:, _grp(1, h)], proj_ref[:, _grp(2, h)],
                    proj_ref[:, _grp(3, h)], lg_ref[0:1, hd], lg_ref[1:2, hd], hnw_ref[:, hd])
            _, vjp = jax.vjp(_hg_head, *args)
            dst, dhq, dhf, dhi, dhg, dl0, dl1, dnw = vjp((dhs[h], dy_ref[:, hd]))
            dhs[h] = dst
            dproj_ref[:, _grp(0, h)] = dhq
            dproj_ref[:, _grp(1, h)] = dhf
            dproj_ref[:, _grp(2, h)] = dhi
            dproj_ref[:, _grp(3, h)] = dhg
            dlg_ref[0:1, hd] += dl0
            dlg_ref[1:2, hd] += dl1
            dhnw_ref[:, hd] += dnw

            margs = (cst_ref[0, h], nst_ref[0, h], _pick_lane(mst_ref[0, h], 0), qk[:, h * DK:(h + 1) * DK],
                     qk[:, D_GRP + h * DK:D_GRP + (h + 1) * DK], proj_ref[:, _grp(6, h)], gates,
                     proj_ref[:, _grp(7, h)], mnw_ref[:, hd])
            _, mvjp = jax.vjp(functools.partial(_ml_head, h=h), *margs)
            dc, dn, dm, dq, dk, dv, dg, dog, dmn = mvjp(
                (dcs[h], dns[h], _pick_lane(dms[h], 0), dy_ref[:, pl.ds(D_GRP + h * DK, DK)]))
            dcs[h] = dc
            dns[h] = dn
            dms[h] = jnp.broadcast_to(dm, (1, DK))
            dqk[:, hd] = dq
            dqk[:, pl.ds(D_GRP + h * DK, DK)] = dk
            dproj_ref[:, _grp(6, h)] = dv
            dproj_ref[:, _grp(7, h)] = dog
            dmnw_ref[:, hd] += dmn
            dgates = dgates + dg
        dproj_ref[:, pl.ds(8 * D_GRP, LANES)] = dgates
        dh, dx, dw0, dw1, dw2, dw3, db = conv_vjp(dqk[...])
        tail = jnp.concatenate([jnp.zeros((CHUNK - SUBLANES, 2 * D_GRP), F32), dhalo[...]], axis=0)
        dproj_ref[:, pl.ds(4 * D_GRP, 2 * D_GRP)] = dx + tail
        dhalo[...] = dh
        dcw_ref[0:1, :] += dw0
        dcw_ref[1:2, :] += dw1
        dcw_ref[2:3, :] += dw2
        dcw_ref[3:4, :] += dw3
        dcb_ref[...] += db

    small_out = [pl.BlockSpec((2, D_GRP), lambda c: (0, 0)), pl.BlockSpec((1, D_GRP), lambda c: (0, 0)),
                 pl.BlockSpec((ML_CONV, 2 * D_GRP), lambda c: (0, 0)), pl.BlockSpec((1, 2 * D_GRP), lambda c: (0, 0)),
                 pl.BlockSpec((1, D_GRP), lambda c: (0, 0))]
    vmem = 2 * (2 * _nbytes((CHUNK, D_IN_PAD), F32) + _nbytes((CHUNK, 2 * D_GRP), F32)
                + 2 * _nbytes((HEADS, DK, DK), F32)) + 2 * _nbytes((HEADS, DK, DK), F32) + 4 * 1024 * 1024
    return _pcall(
        body, name="mixer_bwd", grid=(n_chunks,),
        in_specs=[proj_spec, halo_spec, y_spec] + state_specs + small,
        out_specs=[proj_spec] + small_out,
        out_shape=[jax.ShapeDtypeStruct((seq, D_IN_PAD), F32), jax.ShapeDtypeStruct((2, D_GRP), F32),
                   jax.ShapeDtypeStruct((1, D_GRP), F32), jax.ShapeDtypeStruct((ML_CONV, 2 * D_GRP), F32),
                   jax.ShapeDtypeStruct((1, 2 * D_GRP), F32), jax.ShapeDtypeStruct((1, D_GRP), F32)],
        scratch_shapes=[pltpu.VMEM((HEADS, DK, DK), F32), pltpu.VMEM((HEADS, DK, DK), F32),
                        pltpu.VMEM((HEADS, 1, DK), F32), pltpu.VMEM((HEADS, 1, DK), F32),
                        pltpu.VMEM((SUBLANES, 2 * D_GRP), F32), pltpu.VMEM((CHUNK, 2 * D_GRP), F32)],
        compiler_params=_params(("arbitrary",), vmem),
    )(proj, proj, dy, hst, cst, nst, mst, lb_logits, hg_nw, conv_w, conv_b, ml_nw)


def _tile(n, prefs, unit=None):
    unit = unit or n
    for p in prefs:
        if unit % p == 0 and n % p == 0:
            return p
    return unit


def _logical(arr):
    return arr.shape if arr.ndim == 2 else (arr.shape[1], arr.shape[0] * arr.shape[2])


def _group(arr):
    return arr.shape[-1]


def _split_spec(ndim, group, tr, tc, where):
    if ndim == 2:
        return pl.BlockSpec((tr, tc), where)
    per = group // tc
    assert per * tc == group, (group, tc)

    def index(*ids):
        bi, bj = where(*ids)
        return (bj // per, bi, bj % per)
    return pl.BlockSpec((None, tr, tc), index)


def _mm(name, mode, a, b, *, bias=None, res=None, res_scale=1.0, ln=None, out_dtype=F32, out_groups=None,
        tm=None, tn=None, tk=None):
    la, lb = _logical(a), _logical(b)
    if mode == "nn":
        (m, k), n = la, lb[1]
        n_unit, k_unit = (_group(b) if b.ndim == 3 else n), (_group(a) if a.ndim == 3 else k)
    elif mode == "nt":
        (m, k), n = la, lb[0]
        n_unit, k_unit = n, min(_group(a) if a.ndim == 3 else k, _group(b) if b.ndim == 3 else k)
    else:
        (k, m), n = la, lb[1]
        n_unit, k_unit = (_group(b) if b.ndim == 3 else n), k
        assert a.ndim == 2
    if out_groups:
        n_unit = min(n_unit, n // out_groups)
    kind = ln[0] if ln else None
    tm = tm or (256 if ln else _tile(m, (512, 256, 128)))
    tn = n if ln else (tn or _tile(n, (512, 384, 256, 128), n_unit))
    tk = tk or (k if (k <= D_FF and k_unit == k) else _tile(k, (1024, 512, 384, 256, 128), k_unit))
    gi, gj, gk = m // tm, n // tn, k // tk
    assert gi * tm == m and gj * tn == n and gk * tk == k, (name, m, n, k, tm, tn, tk)
    ca, cb = {"nn": (1, 0), "nt": (1, 1), "tn": (0, 0)}[mode]
    if mode == "tn":
        a_spec = _split_spec(a.ndim, _group(a), tk, tm, lambda i, j, kk: (kk, i))
    else:
        a_spec = _split_spec(a.ndim, _group(a), tm, tk, lambda i, j, kk: (i, kk))
    if mode == "nt":
        b_spec = _split_spec(b.ndim, _group(b), tn, tk, lambda i, j, kk: (j, kk))
    else:
        b_spec = _split_spec(b.ndim, _group(b), tk, tn, lambda i, j, kk: (kk, j))
    row_spec = pl.BlockSpec((1, tn), lambda i, j, kk: (0, j))
    blk_spec = pl.BlockSpec((tm, tn), lambda i, j, kk: (i, j))
    ins, in_specs = [a, b], [a_spec, b_spec]
    if bias is not None:
        ins.append(bias), in_specs.append(row_spec)
    if res is not None:
        ins.append(res), in_specs.append(blk_spec)
    if kind == "fwd":
        ins += [ln[1], ln[2]]
        in_specs += [row_spec, row_spec]
    elif kind == "loss":
        ins += [ln[1], ln[2], ln[3]]
        in_specs += [row_spec, row_spec, blk_spec]
    elif kind == "bwd":
        ins += [ln[1], ln[2], ln[3]]
        in_specs += [blk_spec, row_spec, row_spec]
    if out_groups:
        blk_out = jax.ShapeDtypeStruct((out_groups, m, n // out_groups), out_dtype)
        out_spec = _split_spec(3, n // out_groups, tm, tn, lambda i, j, kk: (i, j))
    else:
        blk_out, out_spec = jax.ShapeDtypeStruct((m, n), out_dtype), blk_spec
    row_out = jax.ShapeDtypeStruct((1, n), F32)
    if kind is None:
        out_shape, out_specs = [blk_out], [out_spec]
    elif kind == "fwd":
        out_shape, out_specs = [blk_out, blk_out], [blk_spec, blk_spec]
    else:
        out_shape, out_specs = [blk_out, row_out, row_out], [blk_spec, row_spec, row_spec]
        if kind == "loss":
            out_shape.append(jax.ShapeDtypeStruct((1, LANES), F32))
            out_specs.append(pl.BlockSpec((1, LANES), lambda i, j, kk: (0, 0)))
    n_in = len(ins)

    def body(*refs):
        in_refs, out_refs, acc_ref = refs[:n_in], refs[n_in:n_in + len(out_shape)], refs[-1]
        i, kk = pl.program_id(0), pl.program_id(2)
        extra = list(in_refs[2:])

        def epilogue(acc):
            rest = list(extra)
            if bias is not None:
                acc = acc + rest.pop(0)[...]
            if res is not None:
                acc = acc + res_scale * rest.pop(0)[...]
            if kind is None:
                out_refs[0][...] = acc.astype(out_dtype)
                return
            if kind == "fwd":
                out_refs[0][...] = acc
                out_refs[1][...] = _layer_norm(acc, rest[0][...], rest[1][...])
                return
            if kind == "loss":
                y, vjp = jax.vjp(_layer_norm, acc, rest[0][...], rest[1][...])
                err = y - rest[2][...]
                part = 0.5 * jnp.sum(jnp.sum(err * err, axis=1, keepdims=True), axis=0, keepdims=True) / n
                dz, dg, db = vjp(err / n)
            else:
                _, vjp = jax.vjp(_layer_norm, rest[0][...], rest[1][...], rest[2][...])
                dz, dg, db = vjp(acc)

            @pl.when(i == 0)
            def _():
                for r in out_refs[1:]:
                    r[...] = jnp.zeros_like(r)

            out_refs[0][...] = dz
            out_refs[1][...] += dg
            out_refs[2][...] += db
            if kind == "loss":
                out_refs[3][...] += jnp.broadcast_to(part, (1, LANES))

        prod = _dg(in_refs[0][...], in_refs[1][...], ca, cb)
        if gk == 1:
            epilogue(prod)
            return

        @pl.when(kk == 0)
        def _():
            acc_ref[...] = prod

        @pl.when(kk > 0)
        def _():
            acc_ref[...] += prod

        @pl.when(kk == gk - 1)
        def _():
            epilogue(acc_ref[...])

    vmem = 2 * (_nbytes((tm, tk), a.dtype) + _nbytes((tk, tn), b.dtype)) + (2 * len(ins) + 2 * len(out_shape) + 1) * _nbytes((tm, tn), F32)
    outs = _pcall(
        body, name=name, grid=(gi, gj, gk), in_specs=in_specs, out_specs=out_specs, out_shape=out_shape,
        scratch_shapes=[pltpu.VMEM((tm, tn) if gk > 1 else (SUBLANES, LANES), F32)],
        compiler_params=_params(("arbitrary", "arbitrary", "arbitrary"), vmem),
    )(*ins)
    return outs[0] if kind is None else outs


def _colsum(name, a):
    m, n = a.shape
    tm = _tile(m, (512, 256, 128))

    def body(a_ref, o_ref):
        @pl.when(pl.program_id(0) == 0)
        def _():
            o_ref[...] = jnp.zeros_like(o_ref)

        o_ref[...] += jnp.sum(a_ref[...].astype(F32), axis=0, keepdims=True)

    return _pcall(
        body, name=name, grid=(m // tm,), in_specs=[pl.BlockSpec((tm, n), lambda i: (i, 0))],
        out_specs=pl.BlockSpec((1, n), lambda i: (0, 0)), out_shape=jax.ShapeDtypeStruct((1, n), F32),
        compiler_params=_params(("arbitrary",), 2 * _nbytes((tm, n), a.dtype)),
    )(a)


def _attn_head(q, k, v):
    sc = mm_nt(q, k) * (CA_DH ** -0.5)
    e = jnp.exp(sc - jnp.max(sc, axis=-1, keepdims=True))
    return mm_nn(e / jnp.sum(e, axis=-1, keepdims=True), v)


def _attn_fwd(q, kv):
    seq, n_mem = q.shape[0], kv.shape[0]
    tq = _tile(seq, (512, 256, 128))

    def body(q_ref, kv_ref, o_ref):
        for h in range(HEADS):
            hd = pl.ds(h * CA_DH, CA_DH)
            o_ref[:, hd] = _attn_head(q_ref[:, hd], kv_ref[:, hd], kv_ref[:, pl.ds(D_MODEL + h * CA_DH, CA_DH)])

    return _pcall(
        body, name="attn_fwd", grid=(seq // tq,),
        in_specs=[pl.BlockSpec((tq, D_MODEL), lambda i: (i, 0)), pl.BlockSpec((n_mem, 2 * D_MODEL), lambda i: (0, 0))],
        out_specs=pl.BlockSpec((tq, D_MODEL), lambda i: (i, 0)), out_shape=jax.ShapeDtypeStruct((seq, D_MODEL), F32),
        compiler_params=_params(("arbitrary",), 4 * _nbytes((tq, D_MODEL), F32) + 2 * _nbytes((n_mem, 2 * D_MODEL), F32)),
    )(q, kv)


def _attn_bwd(q, kv, do):
    seq, n_mem = q.shape[0], kv.shape[0]
    tq = _tile(seq, (512, 256, 128))

    def body(q_ref, kv_ref, do_ref, dq_ref, dkv_ref):
        @pl.when(pl.program_id(0) == 0)
        def _():
            dkv_ref[...] = jnp.zeros_like(dkv_ref)

        for h in range(HEADS):
            hd = pl.ds(h * CA_DH, CA_DH)
            vd = pl.ds(D_MODEL + h * CA_DH, CA_DH)
            _, vjp = jax.vjp(_attn_head, q_ref[:, hd], kv_ref[:, hd], kv_ref[:, vd])
            dq, dk, dv = vjp(do_ref[:, hd])
            dq_ref[:, hd] = dq
            dkv_ref[:, hd] += dk
            dkv_ref[:, vd] += dv

    return _pcall(
        body, name="attn_bwd", grid=(seq // tq,),
        in_specs=[pl.BlockSpec((tq, D_MODEL), lambda i: (i, 0)), pl.BlockSpec((n_mem, 2 * D_MODEL), lambda i: (0, 0)),
                  pl.BlockSpec((tq, D_MODEL), lambda i: (i, 0))],
        out_specs=[pl.BlockSpec((tq, D_MODEL), lambda i: (i, 0)), pl.BlockSpec((n_mem, 2 * D_MODEL), lambda i: (0, 0))],
        out_shape=[jax.ShapeDtypeStruct((seq, D_MODEL), F32), jax.ShapeDtypeStruct((n_mem, 2 * D_MODEL), F32)],
        compiler_params=_params(("arbitrary",), 6 * _nbytes((tq, D_MODEL), F32) + 4 * _nbytes((n_mem, 2 * D_MODEL), F32)),
    )(q, kv, do)


FFN_TB = 512
FFN_TC = 256


def _ffn_mid(hg, xg, hv, xv, wg0, wg1, wg2, bg, wv0, wv1, wv2, bv):
    return jax.nn.gelu(causal_conv(hg, xg, (wg0, wg1, wg2), bg)) * causal_conv(hv, xv, (wv0, wv1, wv2), bv)


def _ffn_specs(seq, reverse):
    tb = min(FFN_TB, seq)
    nt = seq // tb
    row8 = tb // SUBLANES

    def tt(t):
        return nt - 1 - t if reverse else t
    nj = D_FF // FFN_TC
    main = pl.BlockSpec((tb, FFN_TC), lambda j, t: (tt(t), j))
    ins = []
    for off in (0, nj):
        ins += [pl.BlockSpec((tb, FFN_TC), lambda j, t, off=off: (tt(t), j + off)),
                pl.BlockSpec((SUBLANES, FFN_TC), lambda j, t, off=off: (jnp.maximum(tt(t) * row8 - 1, 0), j + off))]
    for off in (0, nj):
        ins += [pl.BlockSpec((FFN_CONV, FFN_TC), lambda j, t, off=off: (0, j + off)),
                pl.BlockSpec((1, FFN_TC), lambda j, t, off=off: (0, j + off))]
    return tb, nt, main, ins


def _ffn_args(c_first, ug, hg, uv, hv, wg, bg, wv, bv):
    halo_g = jnp.where(c_first, 0.0, hg[...])
    halo_v = jnp.where(c_first, 0.0, hv[...])
    return (halo_g, ug[...], halo_v, uv[...], wg[0:1, :], wg[1:2, :], wg[2:3, :], bg[...],
            wv[0:1, :], wv[1:2, :], wv[2:3, :], bv[...])


def _ffn_mid_fwd(u, conv_w, conv_b):
    seq = u.shape[0]
    tb, nt, main, ins = _ffn_specs(seq, False)

    def body(ug, hg, uv, hv, wg, bg, wv, bv, o_ref):
        o_ref[...] = _ffn_mid(*_ffn_args(pl.program_id(1) == 0, ug, hg, uv, hv, wg, bg, wv, bv))

    return _pcall(
        body, name="ffn_mid_fwd", grid=(D_FF // FFN_TC, nt), in_specs=ins, out_specs=main,
        out_shape=jax.ShapeDtypeStruct((seq, D_FF), F32),
        compiler_params=_params(("arbitrary", "arbitrary"), 12 * _nbytes((tb, FFN_TC), F32)),
    )(u, u, u, u, conv_w, conv_b, conv_w, conv_b)


def _ffn_mid_bwd(u, conv_w, conv_b, dh):
    seq = u.shape[0]
    tb, nt, main, ins = _ffn_specs(seq, True)

    def body(ug, hg, uv, hv, wg, bg, wv, bv, dh_ref, du, dw, db, carry):
        t = pl.program_id(1)

        @pl.when(t == 0)
        def _():
            for r in (dw, db, carry):
                r[...] = jnp.zeros_like(r)

        _, vjp = jax.vjp(_ffn_mid, *_ffn_args(t == nt - 1, ug, hg, uv, hv, wg, bg, wv, bv))
        dhg, dxg, dhv, dxv, g0, g1, g2, gb, v0, v1, v2, vb = vjp(dh_ref[...])
        zeros = jnp.zeros((tb - SUBLANES, FFN_TC), F32)
        du[0] = dxg + jnp.concatenate([zeros, carry[0]], axis=0)
        du[1] = dxv + jnp.concatenate([zeros, carry[1]], axis=0)
        carry[0] = dhg
        carry[1] = dhv
        for half, parts in enumerate(((g0, g1, g2), (v0, v1, v2))):
            for d, p in enumerate(parts):
                dw[half, d:d + 1, :] += p
        db[0] += gb
        db[1] += vb

    def grouped(rows, index):
        return pl.BlockSpec((2, rows, FFN_TC), index)
    return _pcall(
        body, name="ffn_mid_bwd", grid=(D_FF // FFN_TC, nt), in_specs=ins + [main],
        out_specs=[grouped(tb, lambda j, t: (0, nt - 1 - t, j)), grouped(FFN_CONV, lambda j, t: (0, 0, j)),
                   grouped(1, lambda j, t: (0, 0, j))],
        out_shape=[jax.ShapeDtypeStruct((2, seq, D_FF), F32), jax.ShapeDtypeStruct((2, FFN_CONV, D_FF), F32),
                   jax.ShapeDtypeStruct((2, 1, D_FF), F32)],
        scratch_shapes=[pltpu.VMEM((2, SUBLANES, FFN_TC), F32)],
        compiler_params=_params(("arbitrary", "arbitrary"), 24 * _nbytes((tb, FFN_TC), F32)),
    )(u, u, u, u, conv_w, conv_b, conv_w, conv_b, dh)


def _adamw_math(w, g, m, v):
    m_new = ADAM_B1 * m + (1.0 - ADAM_B1) * g
    v_new = ADAM_B2 * v + (1.0 - ADAM_B2) * jnp.square(g)
    m_hat = m_new / (1.0 - ADAM_B1 ** ADAM_STEP)
    v_hat = v_new / (1.0 - ADAM_B2 ** ADAM_STEP)
    return -ADAM_LR * (m_hat / (jnp.sqrt(v_hat) + ADAM_EPS) + ADAM_WD * w), m_new, v_new


def _adamw(name, w, g, m, v):
    rows, cols = w.shape
    tr = _tile(rows, (256, 176, 128, 64, 40, 32, 16, 8))

    def body(w_ref, g_ref, m_ref, v_ref, d_ref, nm_ref, nv_ref):
        d_ref[...], nm_ref[...], nv_ref[...] = _adamw_math(w_ref[...], g_ref[...], m_ref[...], v_ref[...])

    spec = pl.BlockSpec((tr, cols), lambda i: (i, 0))
    sh = jax.ShapeDtypeStruct((rows, cols), F32)
    return _pcall(
        body, name=name, grid=(rows // tr,), in_specs=[spec] * 4, out_specs=[spec] * 3, out_shape=[sh] * 3,
        compiler_params=_params(("arbitrary",), 14 * _nbytes((tr, -(-cols // LANES) * LANES), F32)),
    )(w, g, m, v)


def _adamw_halves(name, core, w, mine, theirs, m, v):
    rows, cols = w.shape
    tr = _tile(rows // 2, (256, 176, 128))
    nbh = rows // 2 // tr

    def body(c_ref, w_ref, a_ref, b_ref, m_ref, v_ref, g_ref, d_ref, nm_ref, nv_ref):
        g = jnp.where(pl.program_id(0) // nbh == c_ref[0], a_ref[...], b_ref[...])
        g_ref[...] = g
        d_ref[...], nm_ref[...], nv_ref[...] = _adamw_math(w_ref[...], g, m_ref[...], v_ref[...])

    spec = pl.BlockSpec((tr, cols), lambda i, c_ref: (i, 0))
    half = pl.BlockSpec((tr, cols), lambda i, c_ref: (i % nbh, 0))
    sh = jax.ShapeDtypeStruct((rows, cols), F32)
    grid_spec = pltpu.PrefetchScalarGridSpec(
        num_scalar_prefetch=1, grid=(rows // tr,), in_specs=[spec, half, half, spec, spec], out_specs=[spec] * 4)
    return _pcall(
        body, name=name, grid_spec=grid_spec, out_shape=[sh] * 4,
        compiler_params=_params(("arbitrary",), 18 * _nbytes((tr, -(-cols // LANES) * LANES), F32)),
    )(core, w, mine, theirs, m, v)


MESH = pl.DeviceIdType.MESH
ANY = pl.BlockSpec(memory_space=pl.ANY)
N_CHIPS = 4
N_DEV = 8
BF16_ROWS = 16


def _me():
    return lax.axis_index("x"), lax.axis_index("y"), lax.axis_index("c")


def _other_chips(x, y):
    return [(1 - x, y), (x, 1 - y), (1 - x, 1 - y)]


def _remote(src, dst, ssem, rsem, dev):
    return pltpu.make_async_remote_copy(src_ref=src, dst_ref=dst, send_sem=ssem, recv_sem=rsem,
                                        device_id=dev, device_id_type=MESH)


def _half_rows(ref_rows, cc):
    half = ref_rows // 2
    return pl.ds(pl.multiple_of(cc * half, BF16_ROWS), half)


def _gather_weights(shards):
    n = len(shards)
    n_ici = n * (N_CHIPS - 1)

    def body(*refs):
        ins, outs, (ssem, rsem, lsem, lrsem) = refs[:n], refs[n:2 * n], refs[2 * n:]
        x, y, c = _me()
        k_me = 2 * x + y
        sib = (x, y, 1 - c)
        chips = _other_chips(x, y)
        started = []
        for i, (w_ref, o_ref) in enumerate(zip(ins, outs)):
            cp = _remote(w_ref, o_ref.at[k_me], lsem.at[i], lrsem.at[i], sib)
            cp.start()
            started.append(cp)
        for r, (px, py) in enumerate(chips):
            for i, (w_ref, o_ref) in enumerate(zip(ins, outs)):
                rows = _half_rows(w_ref.shape[0], c)
                s = r * n + i
                cp = _remote(w_ref.at[rows], o_ref.at[k_me, rows], ssem.at[s], rsem.at[s], (px, py, c))
                cp.start()
                started.append(cp)
        for r, (px, py) in enumerate(chips):
            for i, o_ref in enumerate(outs):
                blk = o_ref.at[2 * px + py, _half_rows(o_ref.shape[1], c)]
                s = r * n + i
                _remote(blk, blk, ssem.at[s], rsem.at[s], (px, py, c)).wait_recv()
                cp = _remote(blk, blk, ssem.at[n_ici + s], rsem.at[n_ici + s], sib)
                cp.start()
                started.append(cp)
        for r, (px, py) in enumerate(chips):
            for i, o_ref in enumerate(outs):
                blk = o_ref.at[2 * px + py, _half_rows(o_ref.shape[1], 1 - c)]
                s = n_ici + r * n + i
                _remote(blk, blk, ssem.at[s], rsem.at[s], sib).wait_recv()
        for cp in started[n:]:
            cp.wait_send()
        for cp in started[:n]:
            cp.wait()

    return _pcall(
        body, name="gather_weights", in_specs=[ANY] * n, out_specs=[ANY] * n,
        out_shape=[jax.ShapeDtypeStruct((N_CHIPS,) + s.shape, s.dtype) for s in shards],
        scratch_shapes=[pltpu.SemaphoreType.DMA((2 * n_ici,)), pltpu.SemaphoreType.DMA((2 * n_ici,)),
                        pltpu.SemaphoreType.DMA((n,)), pltpu.SemaphoreType.DMA((n,))],
    )(*shards)


def _swap_halves(grads):
    n = len(grads)

    def body(*refs):
        ins, outs, (ssem, rsem) = refs[:n], refs[n:2 * n], refs[2 * n:]
        x, y, c = _me()
        copies = []
        for i, (g_ref, o_ref) in enumerate(zip(ins, outs)):
            for k in range(N_CHIPS):
                s = i * N_CHIPS + k
                cp = _remote(g_ref.at[k, _half_rows(g_ref.shape[1], 1 - c)], o_ref.at[k], ssem.at[s], rsem.at[s],
                             (x, y, 1 - c))
                cp.start()
                copies.append(cp)
        for cp in copies:
            cp.wait()

    return _pcall(
        body, name="swap_halves", in_specs=[ANY] * n, out_specs=[ANY] * n,
        out_shape=[jax.ShapeDtypeStruct((N_CHIPS, g.shape[1] // 2, g.shape[2]), g.dtype) for g in grads],
        scratch_shapes=[pltpu.SemaphoreType.DMA((n * N_CHIPS,)), pltpu.SemaphoreType.DMA((n * N_CHIPS,))],
    )(*grads)


def _scatter_chips(parts):
    n = len(parts)

    def body(*refs):
        ins, outs, (ssem, rsem) = refs[:n], refs[n:2 * n], refs[2 * n:]
        x, y, c = _me()
        k_me = 2 * x + y
        chips = _other_chips(x, y)
        sends = []
        for r, (px, py) in enumerate(chips):
            for i, (p_ref, o_ref) in enumerate(zip(ins, outs)):
                s = r * n + i
                cp = _remote(p_ref.at[2 * px + py], o_ref.at[k_me], ssem.at[s], rsem.at[s], (px, py, c))
                cp.start()
                sends.append(cp)
        for r, (px, py) in enumerate(chips):
            for i, o_ref in enumerate(outs):
                blk = o_ref.at[2 * px + py]
                s = r * n + i
                _remote(blk, blk, ssem.at[s], rsem.at[s], (px, py, c)).wait_recv()
        for cp in sends:
            cp.wait_send()

    n_sem = n * (N_CHIPS - 1)
    return _pcall(
        body, name="scatter_chips", in_specs=[ANY] * n, out_specs=[ANY] * n,
        out_shape=[jax.ShapeDtypeStruct(p.shape, p.dtype) for p in parts],
        scratch_shapes=[pltpu.SemaphoreType.DMA((n_sem,)), pltpu.SemaphoreType.DMA((n_sem,))],
    )(*parts)


def _share_halves(halves):
    n = len(halves)

    def body(*refs):
        ins, outs, (ssem, rsem) = refs[:n], refs[n:2 * n], refs[2 * n:]
        x, y, c = _me()
        copies = [_remote(r_ref, o_ref, ssem.at[i], rsem.at[i], (x, y, 1 - c))
                  for i, (r_ref, o_ref) in enumerate(zip(ins, outs))]
        for cp in copies:
            cp.start()
        for cp in copies:
            cp.wait()

    return _pcall(
        body, name="share_halves", in_specs=[ANY] * n, out_specs=[ANY] * n,
        out_shape=[jax.ShapeDtypeStruct(h.shape, h.dtype) for h in halves],
        scratch_shapes=[pltpu.SemaphoreType.DMA((n,)), pltpu.SemaphoreType.DMA((n,))],
    )(*halves)


def _exchange_small(v, reduce):
    rows = v.shape[0]

    def body(v_ref, out_ref, buf, ssem, rsem):
        x, y, c = _me()
        me = 4 * x + 2 * y + c
        peers = [((x + bx) % 2, (y + by) % 2, (c + bc) % 2)
                 for bx in (0, 1) for by in (0, 1) for bc in (0, 1) if (bx, by, bc) != (0, 0, 0)]
        dst = buf if reduce else out_ref
        dst[me] = v_ref[...]
        sends = [_remote(v_ref, dst.at[me], ssem.at[r], rsem.at[r], p) for r, p in enumerate(peers)]
        for cp in sends:
            cp.start()
        for r, (px, py, pc) in enumerate(peers):
            blk = dst.at[4 * px + 2 * py + pc]
            _remote(blk, blk, ssem.at[r], rsem.at[r], (px, py, pc)).wait_recv()
        if reduce:
            acc = buf[0]
            for d in range(1, N_DEV):
                acc = acc + buf[d]
            out_ref[...] = acc
        for cp in sends:
            cp.wait_send()

    vm = pl.BlockSpec(memory_space=pltpu.VMEM)
    out_shape = jax.ShapeDtypeStruct((rows, LANES) if reduce else (N_DEV, rows, LANES), F32)
    buf_shape = (N_DEV, rows, LANES) if reduce else (SUBLANES, LANES)
    return _pcall(
        body, pin=False, name="reduce_small" if reduce else "gather_small", in_specs=[vm], out_specs=vm, out_shape=out_shape,
        scratch_shapes=[pltpu.VMEM(buf_shape, F32), pltpu.SemaphoreType.DMA((N_DEV - 1,)),
                        pltpu.SemaphoreType.DMA((N_DEV - 1,))],
        compiler_params=pltpu.CompilerParams(vmem_limit_bytes=32 * 1024 * 1024),
    )(v)


def _add_pair(name, core, g, theirs):
    _, half, cols = theirs.shape
    tr = _tile(half, (256, 176, 128))
    nb = half // tr

    def body(c_ref, g_ref, t_ref, o32_ref, o16_ref):
        s = g_ref[...] + t_ref[...]
        o32_ref[...] = s
        o16_ref[...] = s.astype(BF16)

    spec = pl.BlockSpec((None, tr, cols), lambda k, i, c_ref: (k, i, 0))
    grid_spec = pltpu.PrefetchScalarGridSpec(
        num_scalar_prefetch=1, grid=(N_CHIPS, nb),
        in_specs=[pl.BlockSpec((None, tr, cols), lambda k, i, c_ref: (k, c_ref[0] * nb + i, 0)), spec],
        out_specs=[spec, spec])
    return _pcall(
        body, name=name, grid_spec=grid_spec,
        out_shape=[jax.ShapeDtypeStruct(theirs.shape, F32), jax.ShapeDtypeStruct(theirs.shape, BF16)],
        compiler_params=_params(("arbitrary", "arbitrary"), 8 * _nbytes((tr, cols + LANES), F32)),
    )(core, g, theirs)


def _add_chips(name, chip, p32, recv):
    _, half, cols = p32.shape
    tr = _tile(half, (256, 176, 128))

    def body(k_ref, p_ref, r0_ref, r1_ref, r2_ref, o_ref):
        o_ref[...] = ((p_ref[...] + r0_ref[...].astype(F32)) + r1_ref[...].astype(F32)) + r2_ref[...].astype(F32)

    def other(r):
        return pl.BlockSpec((None, tr, cols), lambda i, k_ref: (r + (k_ref[0] <= r).astype(jnp.int32), i, 0))
    grid_spec = pltpu.PrefetchScalarGridSpec(
        num_scalar_prefetch=1, grid=(half // tr,),
        in_specs=[pl.BlockSpec((None, tr, cols), lambda i, k_ref: (k_ref[0], i, 0)), other(0), other(1), other(2)],
        out_specs=pl.BlockSpec((tr, cols), lambda i, k_ref: (i, 0)))
    return _pcall(
        body, name=name, grid_spec=grid_spec, out_shape=jax.ShapeDtypeStruct((half, cols), F32),
        compiler_params=_params(("arbitrary",), 10 * _nbytes((tr, cols + LANES), F32)),
    )(chip, p32, recv, recv, recv)


def kernel(x, mem, w_in, b_in, hg_lb_logits, hg_norm_w, ml_conv_w, ml_conv_b, ml_norm_w, w_out, ln1_g, ln1_b, ca_wq, ca_wkv, ca_wo, ln2_g, ln2_b, ffn_w_up, ffn_conv_w, ffn_conv_b, ffn_w_down, ln3_g, ln3_b, loss_target, m_w_in, m_b_in, m_hg_lb_logits, m_hg_norm_w, m_ml_conv_w, m_ml_conv_b, m_ml_norm_w, m_w_out, m_ln1_g, m_ln1_b, m_ca_wq, m_ca_wkv, m_ca_wo, m_ln2_g, m_ln2_b, m_ffn_w_up, m_ffn_conv_w, m_ffn_conv_b, m_ffn_w_down, m_ln3_g, m_ln3_b, v_w_in, v_b_in, v_hg_lb_logits, v_hg_norm_w, v_ml_conv_w, v_ml_conv_b, v_ml_norm_w, v_w_out, v_ln1_g, v_ln1_b, v_ca_wq, v_ca_wkv, v_ca_wo, v_ln2_g, v_ln2_b, v_ffn_w_up, v_ffn_conv_w, v_ffn_conv_b, v_ffn_w_down, v_ln3_g, v_ln3_b):
    return _train_step(dict(locals()))


WEIGHTS = ("w_in", "b_in", "hg_lb_logits", "hg_norm_w", "ml_conv_w", "ml_conv_b", "ml_norm_w", "w_out", "ln1_g",
           "ln1_b", "ca_wq", "ca_wkv", "ca_wo", "ln2_g", "ln2_b", "ffn_w_up", "ffn_conv_w", "ffn_conv_b",
           "ffn_w_down", "ln3_g", "ln3_b")
MATRICES = ("w_in", "w_out", "ca_wq", "ca_wkv", "ca_wo", "ffn_w_up", "ffn_w_down")
COL_SHARDED = ("w_in", "ca_wkv", "ffn_w_up", "ml_conv_w", "ffn_conv_w")
SMALL = tuple(n for n in WEIGHTS if n not in MATRICES)
PART_ROWS = 16


def _part_rows(shape, lead):
    n = 1
    for s in shape[lead:]:
        n *= s
    return -(-n // (LANES * PART_ROWS)) * PART_ROWS


def _pack(arrs, dtype, lead=0, rows=None):
    parts = []
    for a in arrs:
        head = a.shape[:lead]
        flat = a.reshape(head + (-1,)).astype(dtype)
        pad = _part_rows(a.shape, lead) * LANES - flat.shape[-1]
        flat = jnp.pad(flat, [(0, 0)] * lead + [(0, pad)])
        parts.append(flat.reshape(head + (-1, LANES)))
    used = sum(p.shape[lead] for p in parts)
    if rows is not None and rows > used:
        parts.append(jnp.zeros(parts[0].shape[:lead] + (rows - used, LANES), dtype))
    return jnp.concatenate(parts, axis=lead)


def _unpack(buf, shapes):
    lead = buf.shape[:-2]
    outs, r = [], 0
    for sh in shapes:
        n = 1
        for s in sh:
            n *= s
        nr = _part_rows(sh, 0)
        flat = buf[..., r:r + nr, :].reshape(lead + (nr * LANES,))
        outs.append(flat[..., :n].reshape(lead + tuple(sh)))
        r += nr
    return outs


def _cat_cols(s):
    return jnp.moveaxis(s, 0, 1).reshape(s.shape[1], -1)


def _split_cols(g):
    return jnp.moveaxis(g.reshape(g.shape[0], N_CHIPS, -1), 1, 0)


def _stack_rows(s):
    return s.reshape(-1, s.shape[-1])


def _train_step(a):
    xs, mems, tgt = a["x"][0], a["mem"][0], a["loss_target"][0]
    core = lax.axis_index("c").astype(jnp.int32).reshape(1)
    chip = (2 * lax.axis_index("x") + lax.axis_index("y")).astype(jnp.int32).reshape(1)
    k_me = chip[0]
    shard = {n: a[n][0] for n in MATRICES}

    w = dict(zip(MATRICES, _gather_weights([shard[n].astype(BF16) for n in MATRICES])))
    for n in ("w_out", "ca_wq", "ca_wo", "ffn_w_down"):
        w[n] = _stack_rows(w[n])
    w["w_in"] = jnp.pad(_cat_cols(w["w_in"]), ((0, 0), (0, D_IN_PAD - D_IN)))
    taps = _exchange_small(_pack([a["ml_conv_w"][0], a["ffn_conv_w"][0]], F32), reduce=False)
    taps = taps.reshape((N_CHIPS, 2) + taps.shape[1:])[:, 0]
    ml_cw, ffn_cw = [_cat_cols(s) for s in _unpack(taps, [a["ml_conv_w"].shape[1:], a["ffn_conv_w"].shape[1:]])]
    b_in_p = jnp.pad(a["b_in"], ((0, 0), (0, D_IN_PAD - D_IN)))
    mixer_w = (a["hg_lb_logits"], a["hg_norm_w"], ml_cw, a["ml_conv_b"], a["ml_norm_w"])
    up_cols = a["ffn_w_up"].shape[-1]

    proj = _mm("proj", "nn", xs, w["w_in"], bias=b_in_p, tm=256, tn=D_IN_PAD)
    y, hst, cst, nst, mst = _mixer_fwd(proj, *mixer_w)
    z1, x1 = _mm("mix_out", "nn", y, w["w_out"], res=xs, res_scale=ALPHA, ln=("fwd", a["ln1_g"], a["ln1_b"]))
    q = _mm("ca_q", "nn", x1, w["ca_wq"])
    kv = _mm("ca_kv", "nn", mems, w["ca_wkv"])
    o = _attn_fwd(q, kv)
    z2, x2 = _mm("ca_out", "nn", o, w["ca_wo"], res=x1, res_scale=ALPHA, ln=("fwd", a["ln2_g"], a["ln2_b"]))
    u = _mm("ffn_up", "nn", x2, w["ffn_w_up"], tn=up_cols)
    hmid = _ffn_mid_fwd(u, ffn_cw, a["ffn_conv_b"])
    dz3, g_ln3g, g_ln3b, loss_part = _mm("ffn_down", "nn", hmid, w["ffn_w_down"], res=x2, res_scale=ALPHA,
                                         ln=("loss", a["ln3_g"], a["ln3_b"], tgt))

    grads = {"ln3_g": g_ln3g, "ln3_b": g_ln3b}
    dhmid = _mm("d_hmid", "nt", dz3, w["ffn_w_down"])
    grads["ffn_w_down"] = _mm("g_w_down", "tn", hmid, dz3)
    du, g_cw, g_cb = _ffn_mid_bwd(u, ffn_cw, a["ffn_conv_b"], dhmid)
    grads["ffn_conv_w"] = jnp.moveaxis(g_cw, 0, 1).reshape(FFN_CONV, 2 * D_FF)
    grads["ffn_conv_b"] = g_cb.reshape(1, 2 * D_FF)
    grads["ffn_w_up"] = _mm("g_w_up", "tn", x2, du, out_groups=N_CHIPS, tn=up_cols)
    dz2, grads["ln2_g"], grads["ln2_b"] = _mm("d_x2", "nt", du, w["ffn_w_up"], res=dz3, res_scale=ALPHA,
                                              ln=("bwd", z2, a["ln2_g"], a["ln2_b"]), tk=up_cols)
    do = _mm("d_o", "nt", dz2, w["ca_wo"])
    grads["ca_wo"] = _mm("g_wo", "tn", o, dz2)
    dq, dkv = _attn_bwd(q, kv, do)
    grads["ca_wq"] = _mm("g_wq", "tn", x1, dq)
    grads["ca_wkv"] = _mm("g_wkv", "tn", mems, dkv, out_groups=N_CHIPS)
    dz1, grads["ln1_g"], grads["ln1_b"] = _mm("d_x1", "nt", dq, w["ca_wq"], res=dz2, res_scale=ALPHA,
                                              ln=("bwd", z1, a["ln1_g"], a["ln1_b"]))
    dy = _mm("d_y", "nt", dz1, w["w_out"])
    grads["w_out"] = _mm("g_w_out", "tn", y, dz1)
    (dproj, grads["hg_lb_logits"], grads["hg_norm_w"], grads["ml_conv_w"], grads["ml_conv_b"],
     grads["ml_norm_w"]) = _mixer_bwd(proj, dy, hst, cst, nst, mst, *mixer_w)
    grads["w_in"] = _split_cols(_mm("g_w_in", "tn", xs, dproj)[:, :D_IN])
    grads["b_in"] = _colsum("g_b_in", dproj)[:, :D_IN]
    dx = _mm("d_x", "nt", dproj, w["w_in"], res=dz1, res_scale=ALPHA, tm=256, tn=D_MODEL, tk=D_IN_PAD)
    for n in ("w_out", "ca_wq", "ca_wo", "ffn_w_down"):
        grads[n] = grads[n].reshape((N_CHIPS,) + shard[n].shape)

    per_chip = [grads[n] for n in MATRICES]
    sums = [_add_pair("add_pair_" + n, core, g, t) for n, g, t in zip(MATRICES, per_chip, _swap_halves(per_chip))]
    recv = _scatter_chips([s16 for _, s16 in sums])
    halves = [_add_chips("add_chips_" + n, chip, s32, r) for n, (s32, _), r in zip(MATRICES, sums, recv)]
    other_halves = _share_halves(halves)

    small_shapes = [grads[n].shape for n in SMALL] + [loss_part.shape]
    summed = _unpack(_exchange_small(_pack([grads[n] for n in SMALL] + [loss_part], F32), reduce=True), small_shapes)
    loss = summed[-1][0, 0]
    for n, g in zip(SMALL, summed[:-1]):
        if n in COL_SHARDED:
            cols = a[n].shape[-1]
            g = lax.dynamic_slice_in_dim(g, k_me * cols, cols, axis=1)
        grads[n] = g

    delta, new_m, new_v = {}, {}, {}
    for n, mine, theirs in zip(MATRICES, halves, other_halves):
        grads[n], delta[n], new_m[n], new_v[n] = _adamw_halves(
            "adamw_" + n, core, shard[n], mine, theirs, a["m_" + n][0], a["v_" + n][0])
    small_w = [a[n][0] if a[n].ndim == 3 else a[n] for n in SMALL]
    small_m = [a["m_" + n][0] if a[n].ndim == 3 else a["m_" + n] for n in SMALL]
    small_v = [a["v_" + n][0] if a[n].ndim == 3 else a["v_" + n] for n in SMALL]
    shapes = [w.shape for w in small_w]
    packed = [_pack(l, F32) for l in (small_w, [grads[n] for n in SMALL], small_m, small_v)]
    for out, buf in zip((delta, new_m, new_v), _adamw("adamw_small", *packed)):
        for n, v in zip(SMALL, _unpack(buf, shapes)):
            out[n] = v

    def shaped(d):
        return [d[n].reshape(a[n].shape) for n in WEIGHTS]
    return (loss, dx[None], *shaped(grads), *shaped(delta), *shaped(new_m), *shaped(new_v))
```

```python
import functools

import jax
import jax.numpy as jnp
from jax import lax
from jax.experimental import pallas as pl
from jax.experimental.pallas import tpu as pltpu

F32 = jnp.float32
BF16 = jnp.bfloat16

D_MODEL = 1024
HEADS = 4
DK = 128
D_GRP = HEADS * DK
CHUNK = 64
ML_CONV = 4
FFN_CONV = 3
D_FF = 2816
CA_DH = D_MODEL // HEADS
DEPTH = 1
ALPHA = (2.0 * DEPTH) ** 0.25
LN_EPS = 1e-5
NEG_BIG = -1e30
D_IN = 8 * D_GRP + 2 * HEADS
D_IN_PAD = 8 * D_GRP + 128
ADAM_LR, ADAM_B1, ADAM_B2, ADAM_EPS, ADAM_WD, ADAM_STEP = 0.001, 0.9, 0.999, 1e-08, 0.01, 10

SUBLANES = 8
LANES = 128
VMEM_BYTES = 64 * 1024 * 1024


def _pcall(body, pin=True, **kw):
    if not pin:
        return _call(body, **kw)
    kw["out_shape"] = jax.tree.map(lambda s: pltpu.HBM(s.shape, s.dtype), kw["out_shape"])
    call = _call(body, **kw)

    def pinned(*args):
        return call(*[pltpu.with_memory_space_constraint(x, pltpu.HBM) if jnp.issubdtype(x.dtype, jnp.floating) else x
                      for x in args])
    return pinned


def _call(body, **kw):
    return pl.pallas_call(body, **kw)


def _params(semantics, vmem_bytes):
    limit = int(min(max(2 * vmem_bytes, 16 * 1024 * 1024), VMEM_BYTES - 8 * 1024 * 1024))
    return pltpu.CompilerParams(dimension_semantics=semantics, vmem_limit_bytes=limit)


def _nbytes(shape, dtype):
    n = 1
    for s in shape:
        n *= s
    return n * jnp.dtype(dtype).itemsize


def _dg(a, b, ca, cb):
    return lax.dot_general(a.astype(BF16), b.astype(BF16), (((ca,), (cb,)), ((), ())),
                           preferred_element_type=F32)


@jax.custom_vjp
def mm_nn(a, b):
    return _dg(a, b, 1, 0)


mm_nn.defvjp(lambda a, b: (_dg(a, b, 1, 0), (a, b)),
             lambda r, g: (_dg(g, r[1], 1, 1).astype(r[0].dtype), _dg(r[0], g, 0, 0).astype(r[1].dtype)))


@jax.custom_vjp
def mm_nt(a, b):
    return _dg(a, b, 1, 1)


mm_nt.defvjp(lambda a, b: (_dg(a, b, 1, 1), (a, b)),
             lambda r, g: (_dg(g, r[1], 1, 0).astype(r[0].dtype), _dg(g, r[0], 0, 0).astype(r[1].dtype)))


@jax.custom_vjp
def mm_tn(a, b):
    return _dg(a, b, 0, 0)


mm_tn.defvjp(lambda a, b: (_dg(a, b, 0, 0), (a, b)),
             lambda r, g: (_dg(r[1], g, 1, 1).astype(r[0].dtype), _dg(r[0], g, 1, 0).astype(r[1].dtype)))


def _hdot(a, b):
    return jnp.dot(a, b, precision=lax.Precision.HIGHEST, preferred_element_type=F32)


def _tri(n, lower):
    r = lax.broadcasted_iota(jnp.int32, (n, n), 0)
    c = lax.broadcasted_iota(jnp.int32, (n, n), 1)
    return ((r >= c) if lower else (r <= c)).astype(F32)


@jax.custom_vjp
def cumsum_rows(x):
    return _hdot(_tri(x.shape[0], True), x)


cumsum_rows.defvjp(lambda x: (_hdot(_tri(x.shape[0], True), x), None),
                   lambda _, g: (_hdot(_tri(g.shape[0], False), g),))


def _shift_impl(halo, x, d):
    xx = jnp.concatenate([halo, x], axis=0)
    return pltpu.roll(xx, d, 0)[SUBLANES:]


@functools.partial(jax.custom_vjp, nondiff_argnums=(2,))
def shift_rows(halo, x, d):
    return _shift_impl(halo, x, d)


def _shift_bwd(d, _, g):
    n = g.shape[0] + SUBLANES
    gg = jnp.concatenate([jnp.zeros((SUBLANES, g.shape[1]), g.dtype), g], axis=0)
    r = pltpu.roll(gg, n - d, 0)
    return r[:SUBLANES], r[SUBLANES:]


shift_rows.defvjp(lambda halo, x, d: (_shift_impl(halo, x, d), None), _shift_bwd)


def causal_conv(halo, x, w_rows, b):
    k = len(w_rows)
    y = b + w_rows[k - 1] * x
    for d in range(1, k):
        y = y + w_rows[k - 1 - d] * shift_rows(halo, x, d)
    return y


def _sigmoid(x):
    return 1.0 / (1.0 + jnp.exp(-x))


def _silu(x):
    return x * _sigmoid(x)


def _log_sigmoid(x):
    return jnp.minimum(x, 0.0) - jnp.log(1.0 + jnp.exp(-jnp.abs(x)))


def _pick_lane(x, j):
    lane = lax.broadcasted_iota(jnp.int32, (1, x.shape[1]), 1)
    return jnp.sum(jnp.where(lane == j, x, 0.0), axis=1, keepdims=True)


def _pick_row(x, i):
    row = lax.broadcasted_iota(jnp.int32, (x.shape[0], 1), 0)
    return jnp.sum(jnp.where(row == i, x, 0.0), axis=0, keepdims=True)


def _col_to_row(e):
    n = e.shape[0]
    eye = lax.broadcasted_iota(jnp.int32, (n, n), 0) == lax.broadcasted_iota(jnp.int32, (n, n), 1)
    return jnp.sum(jnp.where(eye, e, 0.0), axis=0, keepdims=True)


def _layer_norm(z, g, b):
    mu = jnp.mean(z, axis=-1, keepdims=True)
    zc = z - mu
    var = jnp.mean(zc * zc, axis=-1, keepdims=True)
    return zc * lax.rsqrt(var + LN_EPS) * g + b


def _hg_head(st_t, hq, hf, hi, hgate, l0, l1, nw):
    n = hq.shape[0]
    lb = _sigmoid(l0 - l1)
    q = _silu(hq)
    lf = jnp.log(lb + (1.0 - lb) * _sigmoid(hf))
    k = (1.0 - lb) * _sigmoid(-hf)
    b = cumsum_rows(lf)
    b_ref = _pick_row(b, n // 2 - 1)
    b_last = _pick_row(b, n - 1)
    attn = mm_nt(q * jnp.exp(b - b_ref), k * jnp.exp(b_ref - b))
    attn = jnp.where(_tri(n, True) > 0, attn, 0.0)
    o = mm_nn(attn, hi) + mm_nt(q * jnp.exp(b), st_t)
    st_new = jnp.exp(b_last) * st_t + mm_tn(hi, k * jnp.exp(b_last - b))
    y = o * lax.rsqrt(jnp.mean(o * o, axis=-1, keepdims=True) + LN_EPS) * nw * _silu(hgate)
    return st_new, y


def _ml_head(c_st, n_st, m_st, q, k, v, gates, og, nw, h):
    n = q.shape[0]
    ig = _pick_lane(gates, h)
    fl = _log_sigmoid(_pick_lane(gates, HEADS + h))
    qs = q * (DK ** -0.5)
    b = _pick_lane(cumsum_rows(jnp.broadcast_to(fl, (n, LANES))), 0)
    g = jnp.sum(fl, axis=0, keepdims=True)
    d = jnp.where(_tri(n, True) > 0, b + _col_to_row(ig - b), -jnp.inf)
    inter = b + m_st
    m_t = jnp.maximum(inter, jnp.max(d, axis=1, keepdims=True))
    s = mm_nt(qs, k) * jnp.exp(d - m_t)
    w_inter = jnp.exp(inter - m_t)
    num = mm_nn(s, v) + w_inter * mm_nn(qs, c_st)
    den = jnp.sum(s, axis=1, keepdims=True) + w_inter * jnp.sum(qs * n_st, axis=1, keepdims=True)
    h_out = num / jnp.maximum(jnp.abs(den), jnp.exp(-m_t))
    a = g - b + ig
    m_new = jnp.maximum(g + m_st, jnp.max(a, axis=0, keepdims=True))
    decay = jnp.exp(g + m_st - m_new)
    wk = k * jnp.exp(a - m_new)
    c_new = decay * c_st + mm_tn(wk, v)
    n_new = decay * n_st + jnp.sum(wk, axis=0, keepdims=True)
    mu = jnp.mean(h_out, axis=-1, keepdims=True)
    hc = h_out - mu
    var = jnp.mean(hc * hc, axis=-1, keepdims=True)
    y = _sigmoid(og) * (hc * lax.rsqrt(var + LN_EPS) * nw)
    return c_new, n_new, m_new, y


def _qk_conv(halo, x, w0, w1, w2, w3, b):
    return _silu(causal_conv(halo, x, (w0, w1, w2, w3), b))


def _grp(i, h=None):
    if h is None:
        return pl.ds(i * D_GRP, D_GRP)
    return pl.ds(i * D_GRP + h * DK, DK)


def _mixer_specs(n_chunks, reverse):
    def chunk(c):
        return n_chunks - 1 - c if reverse else c
    row8 = CHUNK // SUBLANES
    proj_spec = pl.BlockSpec((CHUNK, D_IN_PAD), lambda c: (chunk(c), 0))
    halo_spec = pl.BlockSpec((SUBLANES, 2 * D_GRP), lambda c: (jnp.maximum(chunk(c) * row8 - 1, 0), 2))
    small = [pl.BlockSpec((2, D_GRP), lambda c: (0, 0)), pl.BlockSpec((1, D_GRP), lambda c: (0, 0)),
             pl.BlockSpec((ML_CONV, 2 * D_GRP), lambda c: (0, 0)), pl.BlockSpec((1, 2 * D_GRP), lambda c: (0, 0)),
             pl.BlockSpec((1, D_GRP), lambda c: (0, 0))]
    state_specs = [pl.BlockSpec((1, HEADS, DK, DK), lambda c: (chunk(c), 0, 0, 0)),
                   pl.BlockSpec((1, HEADS, DK, DK), lambda c: (chunk(c), 0, 0, 0)),
                   pl.BlockSpec((1, HEADS, 1, DK), lambda c: (chunk(c), 0, 0, 0)),
                   pl.BlockSpec((1, HEADS, 1, DK), lambda c: (chunk(c), 0, 0, 0))]
    y_spec = pl.BlockSpec((CHUNK, 2 * D_GRP), lambda c: (chunk(c), 0))
    return proj_spec, halo_spec, small, state_specs, y_spec, chunk


def _mixer_fwd(proj, lb_logits, hg_nw, conv_w, conv_b, ml_nw):
    seq = proj.shape[0]
    n_chunks = seq // CHUNK
    proj_spec, halo_spec, small, state_specs, y_spec, _ = _mixer_specs(n_chunks, False)

    def body(proj_ref, halo_ref, lg_ref, hnw_ref, cw_ref, cb_ref, mnw_ref,
             y_ref, hst_ref, cst_ref, nst_ref, mst_ref, hs, cs, ns, ms):
        c = pl.program_id(0)

        @pl.when(c == 0)
        def _():
            hs[...] = jnp.zeros_like(hs)
            cs[...] = jnp.zeros_like(cs)
            ns[...] = jnp.zeros_like(ns)
            ms[...] = jnp.full(ms.shape, NEG_BIG, F32)

        hst_ref[0] = hs[...]
        cst_ref[0] = cs[...]
        nst_ref[0] = ns[...]
        mst_ref[0] = ms[...]
        halo = jnp.where(c > 0, halo_ref[...], 0.0)
        qk = _qk_conv(halo, proj_ref[:, pl.ds(4 * D_GRP, 2 * D_GRP)],
                      cw_ref[0:1, :], cw_ref[1:2, :], cw_ref[2:3, :], cw_ref[3:4, :], cb_ref[...])
        gates = proj_ref[:, pl.ds(8 * D_GRP, LANES)]
        for h in range(HEADS):
            hd = pl.ds(h * DK, DK)
            st_new, y = _hg_head(hs[h], proj_ref[:, _grp(0, h)], proj_ref[:, _grp(1, h)], proj_ref[:, _grp(2, h)],
                                 proj_ref[:, _grp(3, h)], lg_ref[0:1, hd], lg_ref[1:2, hd], hnw_ref[:, hd])
            hs[h] = st_new
            y_ref[:, hd] = y
            c_new, n_new, m_new, y = _ml_head(
                cs[h], ns[h], _pick_lane(ms[h], 0), qk[:, h * DK:(h + 1) * DK],
                qk[:, D_GRP + h * DK:D_GRP + (h + 1) * DK], proj_ref[:, _grp(6, h)], gates,
                proj_ref[:, _grp(7, h)], mnw_ref[:, hd], h)
            cs[h] = c_new
            ns[h] = n_new
            ms[h] = jnp.broadcast_to(m_new, (1, DK))
            y_ref[:, pl.ds(D_GRP + h * DK, DK)] = y

    st = jax.ShapeDtypeStruct((n_chunks, HEADS, DK, DK), F32)
    vec = jax.ShapeDtypeStruct((n_chunks, HEADS, 1, DK), F32)
    vmem = 2 * (_nbytes((CHUNK, D_IN_PAD), F32) + _nbytes((CHUNK, 2 * D_GRP), F32) + 2 * _nbytes((HEADS, DK, DK), F32)) \
        + 2 * _nbytes((HEADS, DK, DK), F32)
    return _pcall(
        body, name="mixer_fwd", grid=(n_chunks,),
        in_specs=[proj_spec, halo_spec] + small,
        out_specs=[y_spec] + state_specs,
        out_shape=[jax.ShapeDtypeStruct((seq, 2 * D_GRP), F32), st, st, vec, vec],
        scratch_shapes=[pltpu.VMEM((HEADS, DK, DK), F32), pltpu.VMEM((HEADS, DK, DK), F32),
                        pltpu.VMEM((HEADS, 1, DK), F32), pltpu.VMEM((HEADS, 1, DK), F32)],
        compiler_params=_params(("arbitrary",), vmem),
    )(proj, proj, lb_logits, hg_nw, conv_w, conv_b, ml_nw)


def _mixer_bwd(proj, dy, hst, cst, nst, mst, lb_logits, hg_nw, conv_w, conv_b, ml_nw):
    seq = proj.shape[0]
    n_chunks = seq // CHUNK
    proj_spec, halo_spec, small, state_specs, y_spec, _ = _mixer_specs(n_chunks, True)

    def body(proj_ref, halo_ref, dy_ref, hst_ref, cst_ref, nst_ref, mst_ref,
             lg_ref, hnw_ref, cw_ref, cb_ref, mnw_ref,
             dproj_ref, dlg_ref, dhnw_ref, dcw_ref, dcb_ref, dmnw_ref,
             dhs, dcs, dns, dms, dhalo, dqk):
        c = pl.program_id(0)

        @pl.when(c == 0)
        def _():
            for r in (dhs, dcs, dns, dms, dhalo, dlg_ref, dhnw_ref, dcw_ref, dcb_ref, dmnw_ref):
                r[...] = jnp.zeros_like(r)

        first = c == n_chunks - 1
        halo = jnp.where(first, 0.0, halo_ref[...])
        x_qk = proj_ref[:, pl.ds(4 * D_GRP, 2 * D_GRP)]
        conv_args = (halo, x_qk, cw_ref[0:1, :], cw_ref[1:2, :], cw_ref[2:3, :], cw_ref[3:4, :], cb_ref[...])
        qk, conv_vjp = jax.vjp(_qk_conv, *conv_args)
        gates = proj_ref[:, pl.ds(8 * D_GRP, LANES)]
        dgates = jnp.zeros((CHUNK, LANES), F32)
        for h in range(HEADS):
            hd = pl.ds(h * DK, DK)
            args = (hst_ref[0, h], proj_ref[:, _grp(0, h)], proj_ref[:, _grp(1, h)], proj_ref[:, _grp(2, h)],
                    proj_ref[:, _grp(3, h)], lg_ref[0:1, hd], lg_ref[1:2, hd], hnw_ref[:, hd])
            _, vjp = jax.vjp(_hg_head, *args)
            dst, dhq, dhf, dhi, dhg, dl0, dl1, dnw = vjp((dhs[h], dy_ref[:, hd]))
            dhs[h] = dst
            dproj_ref[:, _grp(0, h)] = dhq
            dproj_ref[:, _grp(1, h)] = dhf
            dproj_ref[:, _grp(2, h)] = dhi
            dproj_ref[:, _grp(3, h)] = dhg
            dlg_ref[0:1, hd] += dl0
            dlg_ref[1:2, hd] += dl1
            dhnw_ref[:, hd] += dnw

            margs = (cst_ref[0, h], nst_ref[0, h], _pick_lane(mst_ref[0, h], 0), qk[:, h * DK:(h + 1) * DK],
                     qk[:, D_GRP + h * DK:D_GRP + (h + 1) * DK], proj_ref[:, _grp(6, h)], gates,
                     proj_ref[:, _grp(7, h)], mnw_ref[:, hd])
            _, mvjp = jax.vjp(functools.partial(_ml_head, h=h), *margs)
            dc, dn, dm, dq, dk, dv, dg, dog, dmn = mvjp(
                (dcs[h], dns[h], _pick_lane(dms[h], 0), dy_ref[:, pl.ds(D_GRP + h * DK, DK)]))
            dcs[h] = dc
            dns[h] = dn
            dms[h] = jnp.broadcast_to(dm, (1, DK))
            dqk[:, hd] = dq
            dqk[:, pl.ds(D_GRP + h * DK, DK)] = dk
            dproj_ref[:, _grp(6, h)] = dv
            dproj_ref[:, _grp(7, h)] = dog
            dmnw_ref[:, hd] += dmn
            dgates = dgates + dg
        dproj_ref[:, pl.ds(8 * D_GRP, LANES)] = dgates
        dh, dx, dw0, dw1, dw2, dw3, db = conv_vjp(dqk[...])
        tail = jnp.concatenate([jnp.zeros((CHUNK - SUBLANES, 2 * D_GRP), F32), dhalo[...]], axis=0)
        dproj_ref[:, pl.ds(4 * D_GRP, 2 * D_GRP)] = dx + tail
        dhalo[...] = dh
        dcw_ref[0:1, :] += dw0
        dcw_ref[1:2, :] += dw1
        dcw_ref[2:3, :] += dw2
        dcw_ref[3:4, :] += dw3
        dcb_ref[...] += db

    small_out = [pl.BlockSpec((2, D_GRP), lambda c: (0, 0)), pl.BlockSpec((1, D_GRP), lambda c: (0, 0)),
                 pl.BlockSpec((ML_CONV, 2 * D_GRP), lambda c: (0, 0)), pl.BlockSpec((1, 2 * D_GRP), lambda c: (0, 0)),
                 pl.BlockSpec((1, D_GRP), lambda c: (0, 0))]
    vmem = 2 * (2 * _nbytes((CHUNK, D_IN_PAD), F32) + _nbytes((CHUNK, 2 * D_GRP), F32)
                + 2 * _nbytes((HEADS, DK, DK), F32)) + 2 * _nbytes((HEADS, DK, DK), F32) + 4 * 1024 * 1024
    return _pcall(
        body, name="mixer_bwd", grid=(n_chunks,),
        in_specs=[proj_spec, halo_spec, y_spec] + state_specs + small,
        out_specs=[proj_spec] + small_out,
        out_shape=[jax.ShapeDtypeStruct((seq, D_IN_PAD), F32), jax.ShapeDtypeStruct((2, D_GRP), F32),
                   jax.ShapeDtypeStruct((1, D_GRP), F32), jax.ShapeDtypeStruct((ML_CONV, 2 * D_GRP), F32),
                   jax.ShapeDtypeStruct((1, 2 * D_GRP), F32), jax.ShapeDtypeStruct((1, D_GRP), F32)],
        scratch_shapes=[pltpu.VMEM((HEADS, DK, DK), F32), pltpu.VMEM((HEADS, DK, DK), F32),
                        pltpu.VMEM((HEADS, 1, DK), F32), pltpu.VMEM((HEADS, 1, DK), F32),
                        pltpu.VMEM((SUBLANES, 2 * D_GRP), F32), pltpu.VMEM((CHUNK, 2 * D_GRP), F32)],
        compiler_params=_params(("arbitrary",), vmem),
    )(proj, proj, dy, hst, cst, nst, mst, lb_logits, hg_nw, conv_w, conv_b, ml_nw)


def _heads(x):
    return [x[:, h * DK:(h + 1) * DK] for h in range(HEADS)]


def _last(x, j):
    lane = lax.broadcasted_iota(jnp.int32, (1, x.shape[-1]), 1)
    return jnp.sum(jnp.where(lane == j, x, 0.0), axis=-1, keepdims=True)


def _hg_chunk(st_t, hq, hf, hi, hgate, l0, l1, nw):
    n = hq.shape[0]
    lb = _sigmoid(l0 - l1)
    q = _silu(hq)
    lf = jnp.log(lb + (1.0 - lb) * _sigmoid(hf))
    k = (1.0 - lb) * _sigmoid(-hf)
    b = cumsum_rows(lf)
    b_ref = _pick_row(b, n // 2 - 1)
    b_last = _pick_row(b, n - 1)
    qa, ka = _heads(q * jnp.exp(b - b_ref)), _heads(k * jnp.exp(b_ref - b))
    qe, kd, eb, v = _heads(q * jnp.exp(b)), _heads(k * jnp.exp(b_last - b)), _heads(jnp.exp(b_last)), _heads(hi)
    tri = _tri(n, True) > 0
    attn = [jnp.where(tri, mm_nt(qa[h], ka[h]), 0.0) for h in range(HEADS)]
    o = [mm_nn(attn[h], v[h]) + mm_nt(qe[h], st_t[h]) for h in range(HEADS)]
    st_new = jnp.stack([eb[h] * st_t[h] + mm_tn(v[h], kd[h]) for h in range(HEADS)])
    yn = [o[h] * lax.rsqrt(jnp.mean(o[h] * o[h], axis=-1, keepdims=True) + LN_EPS) for h in range(HEADS)]
    return st_new, jnp.concatenate(yn, axis=1) * nw * _silu(hgate)


def _ml_chunk(c_st, n_st, m_st, q, k, v, gates, og, nw):
    n = q.shape[0]
    ig = jnp.stack([_last(gates, h) for h in range(HEADS)])
    fl = _log_sigmoid(jnp.stack([_last(gates, HEADS + h) for h in range(HEADS)]))
    bw = cumsum_rows(jnp.concatenate([jnp.broadcast_to(fl[h], (n, DK)) for h in range(HEADS)], axis=1))
    b = jnp.stack([_last(x, 0) for x in _heads(bw)])
    g = jnp.sum(fl, axis=1, keepdims=True)
    eye = lax.broadcasted_iota(jnp.int32, (n, n), 0) == lax.broadcasted_iota(jnp.int32, (n, n), 1)
    e_row = jnp.sum(jnp.where(eye, ig - b, 0.0), axis=1, keepdims=True)
    d = jnp.where(_tri(n, True) > 0, b + e_row, -jnp.inf)
    inter = b + m_st
    m_t = jnp.maximum(inter, jnp.max(d, axis=2, keepdims=True))
    qs, kh, vh = _heads(q * (DK ** -0.5)), _heads(k), _heads(v)
    s = jnp.stack([mm_nt(qs[h], kh[h]) for h in range(HEADS)]) * jnp.exp(d - m_t)
    w_inter = jnp.exp(inter - m_t)
    num = (jnp.stack([mm_nn(s[h], vh[h]) for h in range(HEADS)])
           + w_inter * jnp.stack([mm_nn(qs[h], c_st[h]) for h in range(HEADS)]))
    den = jnp.sum(s, axis=2, keepdims=True) + w_inter * jnp.sum(jnp.stack(qs) * n_st, axis=2, keepdims=True)
    h_out = num / jnp.maximum(jnp.abs(den), jnp.exp(-m_t))
    a = g - b + ig
    m_new = jnp.maximum(g + m_st, jnp.max(a, axis=1, keepdims=True))
    decay = jnp.exp(g + m_st - m_new)
    wk = jnp.stack(kh) * jnp.exp(a - m_new)
    c_new = decay * c_st + jnp.stack([mm_tn(wk[h], vh[h]) for h in range(HEADS)])
    n_new = decay * n_st + jnp.sum(wk, axis=1, keepdims=True)
    hc = h_out - jnp.mean(h_out, axis=-1, keepdims=True)
    yn = hc * lax.rsqrt(jnp.mean(hc * hc, axis=-1, keepdims=True) + LN_EPS)
    y = _sigmoid(og) * (jnp.concatenate([yn[h] for h in range(HEADS)], axis=1) * nw)
    return c_new, n_new, m_new, y


def _mixer_inputs(proj_ref, lg_ref, hnw_ref, mnw_ref, qk):
    hg_in = (proj_ref[:, _grp(0)], proj_ref[:, _grp(1)], proj_ref[:, _grp(2)], proj_ref[:, _grp(3)],
             lg_ref[0:1, :], lg_ref[1:2, :], hnw_ref[...])
    ml_in = (qk[:, :D_GRP], qk[:, D_GRP:], proj_ref[:, _grp(6)], proj_ref[:, pl.ds(8 * D_GRP, LANES)],
             proj_ref[:, _grp(7)], mnw_ref[...])
    return hg_in, ml_in


def _mixer_fwd(proj, lb_logits, hg_nw, conv_w, conv_b, ml_nw):
    seq = proj.shape[0]
    n_chunks = seq // CHUNK
    proj_spec, halo_spec, small, state_specs, y_spec, _ = _mixer_specs(n_chunks, False)

    def body(proj_ref, halo_ref, lg_ref, hnw_ref, cw_ref, cb_ref, mnw_ref,
             y_ref, hst_ref, cst_ref, nst_ref, mst_ref, hs, cs, ns, ms):
        c = pl.program_id(0)

        @pl.when(c == 0)
        def _():
            hs[...] = jnp.zeros_like(hs)
            cs[...] = jnp.zeros_like(cs)
            ns[...] = jnp.zeros_like(ns)
            ms[...] = jnp.full(ms.shape, NEG_BIG, F32)

        hst_ref[0] = hs[...]
        cst_ref[0] = cs[...]
        nst_ref[0] = ns[...]
        mst_ref[0] = ms[...]
        halo = jnp.where(c > 0, halo_ref[...], 0.0)
        qk = _qk_conv(halo, proj_ref[:, pl.ds(4 * D_GRP, 2 * D_GRP)],
                      cw_ref[0:1, :], cw_ref[1:2, :], cw_ref[2:3, :], cw_ref[3:4, :], cb_ref[...])
        hg_in, ml_in = _mixer_inputs(proj_ref, lg_ref, hnw_ref, mnw_ref, qk)
        hs[...], y_hg = _hg_chunk(hs[...], *hg_in)
        cs[...], ns[...], m_new, y_ml = _ml_chunk(cs[...], ns[...], _last(ms[...], 0), *ml_in)
        ms[...] = jnp.broadcast_to(m_new, ms.shape)
        y_ref[:, pl.ds(0, D_GRP)] = y_hg.astype(BF16)
        y_ref[:, pl.ds(D_GRP, D_GRP)] = y_ml.astype(BF16)

    st = jax.ShapeDtypeStruct((n_chunks, HEADS, DK, DK), F32)
    vec = jax.ShapeDtypeStruct((n_chunks, HEADS, 1, DK), F32)
    vmem = 2 * (_nbytes((CHUNK, D_IN_PAD), F32) + _nbytes((CHUNK, 2 * D_GRP), F32) + 2 * _nbytes((HEADS, DK, DK), F32)) \
        + 2 * _nbytes((HEADS, DK, DK), F32)
    return _pcall(
        body, name="mixer_fwd", grid=(n_chunks,),
        in_specs=[proj_spec, halo_spec] + small,
        out_specs=[y_spec] + state_specs,
        out_shape=[jax.ShapeDtypeStruct((seq, 2 * D_GRP), BF16), st, st, vec, vec],
        scratch_shapes=[pltpu.VMEM((HEADS, DK, DK), F32), pltpu.VMEM((HEADS, DK, DK), F32),
                        pltpu.VMEM((HEADS, 1, DK), F32), pltpu.VMEM((HEADS, 1, DK), F32)],
        compiler_params=_params(("arbitrary",), vmem),
    )(proj, proj, lb_logits, hg_nw, conv_w, conv_b, ml_nw)


def _mixer_bwd(proj, dy, hst, cst, nst, mst, lb_logits, hg_nw, conv_w, conv_b, ml_nw):
    seq = proj.shape[0]
    n_chunks = seq // CHUNK
    proj_spec, halo_spec, small, state_specs, y_spec, _ = _mixer_specs(n_chunks, True)

    def body(proj_ref, halo_ref, dy_ref, hst_ref, cst_ref, nst_ref, mst_ref,
             lg_ref, hnw_ref, cw_ref, cb_ref, mnw_ref,
             dproj_ref, dbin_ref, dlg_ref, dhnw_ref, dcw_ref, dcb_ref, dmnw_ref,
             dhs, dcs, dns, dms, dhalo):
        c = pl.program_id(0)

        @pl.when(c == 0)
        def _():
            for r in (dhs, dcs, dns, dms, dhalo, dbin_ref, dlg_ref, dhnw_ref, dcw_ref, dcb_ref, dmnw_ref):
                r[...] = jnp.zeros_like(r)

        def put(cols, val):
            dproj_ref[:, cols] = val.astype(BF16)
            dbin_ref[:, cols] += jnp.sum(val, axis=0, keepdims=True)

        first = c == n_chunks - 1
        halo = jnp.where(first, 0.0, halo_ref[...])
        x_qk = proj_ref[:, pl.ds(4 * D_GRP, 2 * D_GRP)]
        conv_args = (halo, x_qk, cw_ref[0:1, :], cw_ref[1:2, :], cw_ref[2:3, :], cw_ref[3:4, :], cb_ref[...])
        qk, conv_vjp = jax.vjp(_qk_conv, *conv_args)
        hg_in, ml_in = _mixer_inputs(proj_ref, lg_ref, hnw_ref, mnw_ref, qk)
        _, hg_vjp = jax.vjp(_hg_chunk, hst_ref[0], *hg_in)
        _, ml_vjp = jax.vjp(_ml_chunk, cst_ref[0], nst_ref[0], _last(mst_ref[0], 0), *ml_in)
        dst, dhq, dhf, dhi, dhg, dl0, dl1, dnw = hg_vjp((dhs[...], dy_ref[:, pl.ds(0, D_GRP)]))
        dc, dn, dm, dq, dk, dv, dgates, dog, dmn = ml_vjp(
            (dcs[...], dns[...], _last(dms[...], 0), dy_ref[:, pl.ds(D_GRP, D_GRP)]))
        dhs[...] = dst
        dcs[...] = dc
        dns[...] = dn
        dms[...] = jnp.broadcast_to(dm, dms.shape)
        for i, val in ((0, dhq), (1, dhf), (2, dhi), (3, dhg), (6, dv), (7, dog)):
            put(_grp(i), val)
        put(pl.ds(8 * D_GRP, LANES), dgates)
        dlg_ref[0:1, :] += dl0
        dlg_ref[1:2, :] += dl1
        dhnw_ref[...] += dnw
        dmnw_ref[...] += dmn
        dh, dx, dw0, dw1, dw2, dw3, db = conv_vjp(jnp.concatenate([dq, dk], axis=1))
        tail = jnp.concatenate([jnp.zeros((CHUNK - SUBLANES, 2 * D_GRP), F32), dhalo[...]], axis=0)
        put(pl.ds(4 * D_GRP, 2 * D_GRP), dx + tail)
        dhalo[...] = dh
        for d, dw in enumerate((dw0, dw1, dw2, dw3)):
            dcw_ref[d:d + 1, :] += dw
        dcb_ref[...] += db

    row = pl.BlockSpec((1, D_GRP), lambda c: (0, 0))
    small_out = [pl.BlockSpec((1, D_IN_PAD), lambda c: (0, 0)), pl.BlockSpec((2, D_GRP), lambda c: (0, 0)), row,
                 pl.BlockSpec((ML_CONV, 2 * D_GRP), lambda c: (0, 0)), pl.BlockSpec((1, 2 * D_GRP), lambda c: (0, 0)), row]
    dy_spec = pl.BlockSpec((CHUNK, 2 * D_GRP), y_spec.index_map)
    vmem = 2 * (2 * _nbytes((CHUNK, D_IN_PAD), F32) + _nbytes((CHUNK, 2 * D_GRP), F32)
                + 2 * _nbytes((HEADS, DK, DK), F32)) + 2 * _nbytes((HEADS, DK, DK), F32) + 4 * 1024 * 1024
    return _pcall(
        body, name="mixer_bwd", grid=(n_chunks,),
        in_specs=[proj_spec, halo_spec, dy_spec] + state_specs + small,
        out_specs=[proj_spec] + small_out,
        out_shape=[jax.ShapeDtypeStruct((seq, D_IN_PAD), BF16), jax.ShapeDtypeStruct((1, D_IN_PAD), F32),
                   jax.ShapeDtypeStruct((2, D_GRP), F32), jax.ShapeDtypeStruct((1, D_GRP), F32),
                   jax.ShapeDtypeStruct((ML_CONV, 2 * D_GRP), F32), jax.ShapeDtypeStruct((1, 2 * D_GRP), F32),
                   jax.ShapeDtypeStruct((1, D_GRP), F32)],
        scratch_shapes=[pltpu.VMEM((HEADS, DK, DK), F32), pltpu.VMEM((HEADS, DK, DK), F32),
                        pltpu.VMEM((HEADS, 1, DK), F32), pltpu.VMEM((HEADS, 1, DK), F32),
                        pltpu.VMEM((SUBLANES, 2 * D_GRP), F32)],
        compiler_params=_params(("arbitrary",), vmem),
    )(proj, proj, dy, hst, cst, nst, mst, lb_logits, hg_nw, conv_w, conv_b, ml_nw)


def _tile(n, prefs, unit=None):
    unit = unit or n
    for p in prefs:
        if unit % p == 0 and n % p == 0:
            return p
    return unit


def _logical(arr):
    return arr.shape if arr.ndim == 2 else (arr.shape[1], arr.shape[0] * arr.shape[2])


def _group(arr):
    return arr.shape[-1]


def _split_spec(ndim, group, tr, tc, where):
    if ndim == 2:
        return pl.BlockSpec((tr, tc), where)
    per = group // tc
    assert per * tc == group, (group, tc)

    def index(*ids):
        bi, bj = where(*ids)
        return (bj // per, bi, bj % per)
    return pl.BlockSpec((None, tr, tc), index)


def _mm(name, mode, a, b, *, bias=None, res=None, res_scale=1.0, ln=None, out_dtype=F32, out_groups=None,
        copy_dtype=None, tm=None, tn=None, tk=None):
    la, lb = _logical(a), _logical(b)
    if mode == "nn":
        (m, k), n = la, lb[1]
        n_unit = _group(b) if b.ndim == 3 else n
        kc = _group(a) if a.ndim == 3 else k
    elif mode == "nt":
        (m, k), n = la, lb[0]
        n_unit = n
        kc = min(_group(a) if a.ndim == 3 else k, _group(b) if b.ndim == 3 else k)
    else:
        (k, m), n = la, lb[1]
        n_unit, kc = (_group(b) if b.ndim == 3 else n), k
        assert a.ndim == 2
    if out_groups:
        n_unit = min(n_unit, n // out_groups)
    kind = ln[0] if ln else None
    tm = tm or (256 if ln else _tile(m, (512, 256, 128)))
    tn = n if ln else (tn or _tile(n, (512, 384, 256, 128), n_unit))
    tk = (tk or _tile(k, (2048, 512, 256, 128))) if mode == "tn" else k
    gi, gj, gk = m // tm, n // tn, k // tk
    assert gi * tm == m and gj * tn == n and gk * tk == k and n_unit % tn == 0, (name, m, n, k, tm, tn, tk)
    ca, cb = {"nn": (1, 0), "nt": (1, 1), "tn": (0, 0)}[mode]
    i_outer = gk > 1 or (gi - 1) * _nbytes(b.shape, b.dtype) <= (gj - 1) * _nbytes(a.shape, a.dtype)

    def ij(where):
        return (lambda p, q, kk: where(p, q, kk)) if i_outer else (lambda p, q, kk: where(q, p, kk))
    if mode == "tn":
        a_spec = pl.BlockSpec((tk, tm), ij(lambda i, j, kk: (kk, i)))
    elif a.ndim == 3:
        a_spec = pl.BlockSpec((a.shape[0], tm, _group(a)), ij(lambda i, j, kk: (0, i, 0)))
    else:
        a_spec = pl.BlockSpec((tm, k), ij(lambda i, j, kk: (i, 0)))
    if mode != "nt":
        b_spec = _split_spec(b.ndim, _group(b), tk, tn, ij(lambda i, j, kk: (kk, j)))
    elif b.ndim == 3:
        b_spec = pl.BlockSpec((b.shape[0], tn, _group(b)), ij(lambda i, j, kk: (0, j, 0)))
    else:
        b_spec = pl.BlockSpec((tn, k), ij(lambda i, j, kk: (j, 0)))
    row_spec = pl.BlockSpec((1, tn), ij(lambda i, j, kk: (0, j)))
    blk_spec = pl.BlockSpec((tm, tn), ij(lambda i, j, kk: (i, j)))
    ins, in_specs = [a, b], [a_spec, b_spec]
    if bias is not None:
        ins.append(bias), in_specs.append(row_spec)
    if res is not None:
        ins.append(res), in_specs.append(blk_spec)
    if kind == "fwd":
        ins += [ln[1], ln[2]]
        in_specs += [row_spec, row_spec]
    elif kind == "loss":
        ins += [ln[1], ln[2], ln[3]]
        in_specs += [row_spec, row_spec, blk_spec]
    elif kind == "bwd":
        ins += [ln[1], ln[2], ln[3]]
        in_specs += [blk_spec, row_spec, row_spec]
    if out_groups:
        blk_out = jax.ShapeDtypeStruct((out_groups, m, n // out_groups), out_dtype)
        out_spec = _split_spec(3, n // out_groups, tm, tn, ij(lambda i, j, kk: (i, j)))
    else:
        blk_out, out_spec = jax.ShapeDtypeStruct((m, n), out_dtype), blk_spec
    row_out = jax.ShapeDtypeStruct((1, n), F32)
    if kind is None:
        out_shape, out_specs = [blk_out], [out_spec]
    elif kind == "fwd":
        out_shape, out_specs = [blk_out, blk_out], [blk_spec, blk_spec]
    else:
        out_shape, out_specs = [blk_out, row_out, row_out], [blk_spec, row_spec, row_spec]
        if kind == "loss":
            out_shape.append(jax.ShapeDtypeStruct((1, LANES), F32))
            out_specs.append(pl.BlockSpec((1, LANES), lambda p, q, kk: (0, 0)))
    if copy_dtype is not None:
        out_shape.append(jax.ShapeDtypeStruct((m, n), copy_dtype))
        out_specs.append(blk_spec)
    n_in = len(ins)

    def body(*refs):
        in_refs, out_refs, acc_ref = refs[:n_in], refs[n_in:n_in + len(out_shape)], refs[-1]
        i, kk = pl.program_id(0 if i_outer else 1), pl.program_id(2)
        a_ref, b_ref = in_refs[:2]
        extra = list(in_refs[2:])

        def epilogue(acc):
            rest = list(extra)
            if bias is not None:
                acc = acc + rest.pop(0)[...]
            if res is not None:
                acc = acc + res_scale * rest.pop(0)[...]
            if kind is None:
                out_refs[0][...] = acc.astype(out_dtype)
                return
            if kind == "fwd":
                out_refs[0][...] = acc
                y = _layer_norm(acc, rest[0][...], rest[1][...])
                out_refs[1][...] = y
                if copy_dtype is not None:
                    out_refs[-1][...] = y.astype(copy_dtype)
                return
            if kind == "loss":
                y, vjp = jax.vjp(_layer_norm, acc, rest[0][...], rest[1][...])
                err = y - rest[2][...]
                part = 0.5 * jnp.sum(jnp.sum(err * err, axis=1, keepdims=True), axis=0, keepdims=True) / n
                dz, dg, db = vjp(err / n)
            else:
                _, vjp = jax.vjp(_layer_norm, rest[0][...], rest[1][...], rest[2][...])
                dz, dg, db = vjp(acc)

            @pl.when(i == 0)
            def _():
                for r in out_refs[1:3 + (kind == "loss")]:
                    r[...] = jnp.zeros_like(r)

            out_refs[0][...] = dz
            out_refs[1][...] += dg
            out_refs[2][...] += db
            if kind == "loss":
                out_refs[3][...] += jnp.broadcast_to(part, (1, LANES))
            if copy_dtype is not None:
                out_refs[-1][...] = dz.astype(copy_dtype)

        def chunk(ref, c0, last):
            if ref.ndim == 3:
                g = ref.shape[2]
                return ref[c0 // g, :, pl.ds(c0 % g, kc)]
            return ref[:, pl.ds(c0, kc)] if last else ref[pl.ds(c0, kc), :]

        if mode == "tn" or kc == k:
            prod = _dg(a_ref[...], b_ref[...], ca, cb)
        else:
            prod = None
            for c0 in range(0, k, kc):
                part = _dg(chunk(a_ref, c0, True), chunk(b_ref, c0, mode == "nt"), ca, cb)
                prod = part if prod is None else prod + part
        if gk == 1:
            epilogue(prod)
            return

        @pl.when(kk == 0)
        def _():
            acc_ref[...] = prod

        @pl.when(kk > 0)
        def _():
            acc_ref[...] += prod

        @pl.when(kk == gk - 1)
        def _():
            epilogue(acc_ref[...])

    vmem = (2 * (_nbytes((tm, tk), a.dtype) + _nbytes((tk, tn), b.dtype))
            + (2 * len(ins) + 2 * len(out_shape) + 1) * _nbytes((tm, tn), F32))
    outs = _pcall(
        body, name=name, grid=(gi, gj, gk) if i_outer else (gj, gi, gk), in_specs=in_specs, out_specs=out_specs,
        out_shape=out_shape, scratch_shapes=[pltpu.VMEM((tm, tn) if gk > 1 else (SUBLANES, LANES), F32)],
        compiler_params=_params(("arbitrary", "arbitrary", "arbitrary"), vmem),
    )(*ins)
    return outs[0] if (kind is None and copy_dtype is None) else outs


def _colsum(name, a):
    m, n = a.shape
    tm = _tile(m, (512, 256, 128))

    def body(a_ref, o_ref):
        @pl.when(pl.program_id(0) == 0)
        def _():
            o_ref[...] = jnp.zeros_like(o_ref)

        o_ref[...] += jnp.sum(a_ref[...].astype(F32), axis=0, keepdims=True)

    return _pcall(
        body, name=name, grid=(m // tm,), in_specs=[pl.BlockSpec((tm, n), lambda i: (i, 0))],
        out_specs=pl.BlockSpec((1, n), lambda i: (0, 0)), out_shape=jax.ShapeDtypeStruct((1, n), F32),
        compiler_params=_params(("arbitrary",), 2 * _nbytes((tm, n), a.dtype)),
    )(a)


def _attn_head(q, k, v):
    sc = mm_nt(q, k) * (CA_DH ** -0.5)
    e = jnp.exp(sc - jnp.max(sc, axis=-1, keepdims=True))
    return mm_nn(e / jnp.sum(e, axis=-1, keepdims=True), v)


def _attn_fwd(q, kv):
    seq, n_mem = q.shape[0], kv.shape[0]
    tq = _tile(seq, (512, 256, 128))

    def body(q_ref, kv_ref, o_ref):
        for h in range(HEADS):
            hd = pl.ds(h * CA_DH, CA_DH)
            o = _attn_head(q_ref[:, hd], kv_ref[:, hd], kv_ref[:, pl.ds(D_MODEL + h * CA_DH, CA_DH)])
            o_ref[:, hd] = o.astype(BF16)

    return _pcall(
        body, name="attn_fwd", grid=(seq // tq,),
        in_specs=[pl.BlockSpec((tq, D_MODEL), lambda i: (i, 0)), pl.BlockSpec((n_mem, 2 * D_MODEL), lambda i: (0, 0))],
        out_specs=pl.BlockSpec((tq, D_MODEL), lambda i: (i, 0)), out_shape=jax.ShapeDtypeStruct((seq, D_MODEL), BF16),
        compiler_params=_params(("arbitrary",), 4 * _nbytes((tq, D_MODEL), F32) + 2 * _nbytes((n_mem, 2 * D_MODEL), F32)),
    )(q, kv)


def _attn_bwd(q, kv, do):
    seq, n_mem = q.shape[0], kv.shape[0]
    tq = _tile(seq, (512, 256, 128))

    def body(q_ref, kv_ref, do_ref, dq_ref, dkv_ref):
        @pl.when(pl.program_id(0) == 0)
        def _():
            dkv_ref[...] = jnp.zeros_like(dkv_ref)

        for h in range(HEADS):
            hd = pl.ds(h * CA_DH, CA_DH)
            vd = pl.ds(D_MODEL + h * CA_DH, CA_DH)
            _, vjp = jax.vjp(_attn_head, q_ref[:, hd], kv_ref[:, hd], kv_ref[:, vd])
            dq, dk, dv = vjp(do_ref[:, hd].astype(F32))
            dq_ref[:, hd] = dq.astype(BF16)
            dkv_ref[:, hd] += dk
            dkv_ref[:, vd] += dv

    return _pcall(
        body, name="attn_bwd", grid=(seq // tq,),
        in_specs=[pl.BlockSpec((tq, D_MODEL), lambda i: (i, 0)), pl.BlockSpec((n_mem, 2 * D_MODEL), lambda i: (0, 0)),
                  pl.BlockSpec((tq, D_MODEL), lambda i: (i, 0))],
        out_specs=[pl.BlockSpec((tq, D_MODEL), lambda i: (i, 0)), pl.BlockSpec((n_mem, 2 * D_MODEL), lambda i: (0, 0))],
        out_shape=[jax.ShapeDtypeStruct((seq, D_MODEL), BF16), jax.ShapeDtypeStruct((n_mem, 2 * D_MODEL), F32)],
        compiler_params=_params(("arbitrary",), 6 * _nbytes((tq, D_MODEL), F32) + 4 * _nbytes((n_mem, 2 * D_MODEL), F32)),
    )(q, kv, do)


FFN_TB = 512
FFN_TC = 256


def _ffn_mid(hg, xg, hv, xv, wg0, wg1, wg2, bg, wv0, wv1, wv2, bv):
    return jax.nn.gelu(causal_conv(hg, xg, (wg0, wg1, wg2), bg)) * causal_conv(hv, xv, (wv0, wv1, wv2), bv)


def _ffn_specs(seq, reverse):
    tb = min(FFN_TB, seq)
    nt = seq // tb
    row8 = tb // SUBLANES

    def tt(t):
        return nt - 1 - t if reverse else t
    nj = D_FF // FFN_TC
    main = pl.BlockSpec((tb, FFN_TC), lambda j, t: (tt(t), j))
    ins = []
    for off in (0, nj):
        ins += [pl.BlockSpec((tb, FFN_TC), lambda j, t, off=off: (tt(t), j + off)),
                pl.BlockSpec((SUBLANES, FFN_TC), lambda j, t, off=off: (jnp.maximum(tt(t) * row8 - 1, 0), j + off))]
    for off in (0, nj):
        ins += [pl.BlockSpec((FFN_CONV, FFN_TC), lambda j, t, off=off: (0, j + off)),
                pl.BlockSpec((1, FFN_TC), lambda j, t, off=off: (0, j + off))]
    return tb, nt, main, ins


def _ffn_args(c_first, ug, hg, uv, hv, wg, bg, wv, bv):
    halo_g = jnp.where(c_first, 0.0, hg[...])
    halo_v = jnp.where(c_first, 0.0, hv[...])
    return (halo_g, ug[...], halo_v, uv[...], wg[0:1, :], wg[1:2, :], wg[2:3, :], bg[...],
            wv[0:1, :], wv[1:2, :], wv[2:3, :], bv[...])


def _ffn_mid_fwd(u, conv_w, conv_b):
    seq = u.shape[0]
    tb, nt, main, ins = _ffn_specs(seq, False)

    def body(ug, hg, uv, hv, wg, bg, wv, bv, o_ref):
        o_ref[...] = _ffn_mid(*_ffn_args(pl.program_id(1) == 0, ug, hg, uv, hv, wg, bg, wv, bv)).astype(BF16)

    return _pcall(
        body, name="ffn_mid_fwd", grid=(D_FF // FFN_TC, nt), in_specs=ins, out_specs=main,
        out_shape=jax.ShapeDtypeStruct((seq, D_FF), BF16),
        compiler_params=_params(("arbitrary", "arbitrary"), 12 * _nbytes((tb, FFN_TC), F32)),
    )(u, u, u, u, conv_w, conv_b, conv_w, conv_b)


def _ffn_mid_bwd(u, conv_w, conv_b, dh):
    seq = u.shape[0]
    tb, nt, main, ins = _ffn_specs(seq, True)

    def body(ug, hg, uv, hv, wg, bg, wv, bv, dh_ref, du, dw, db, carry):
        t = pl.program_id(1)

        @pl.when(t == 0)
        def _():
            for r in (dw, db, carry):
                r[...] = jnp.zeros_like(r)

        _, vjp = jax.vjp(_ffn_mid, *_ffn_args(t == nt - 1, ug, hg, uv, hv, wg, bg, wv, bv))
        dhg, dxg, dhv, dxv, g0, g1, g2, gb, v0, v1, v2, vb = vjp(dh_ref[...])
        zeros = jnp.zeros((tb - SUBLANES, FFN_TC), F32)
        du[0] = (dxg + jnp.concatenate([zeros, carry[0]], axis=0)).astype(BF16)
        du[1] = (dxv + jnp.concatenate([zeros, carry[1]], axis=0)).astype(BF16)
        carry[0] = dhg
        carry[1] = dhv
        for half, parts in enumerate(((g0, g1, g2), (v0, v1, v2))):
            for d, p in enumerate(parts):
                dw[half, d:d + 1, :] += p
        db[0] += gb
        db[1] += vb

    def grouped(rows, index):
        return pl.BlockSpec((2, rows, FFN_TC), index)
    return _pcall(
        body, name="ffn_mid_bwd", grid=(D_FF // FFN_TC, nt), in_specs=ins + [main],
        out_specs=[grouped(tb, lambda j, t: (0, nt - 1 - t, j)), grouped(FFN_CONV, lambda j, t: (0, 0, j)),
                   grouped(1, lambda j, t: (0, 0, j))],
        out_shape=[jax.ShapeDtypeStruct((2, seq, D_FF), BF16), jax.ShapeDtypeStruct((2, FFN_CONV, D_FF), F32),
                   jax.ShapeDtypeStruct((2, 1, D_FF), F32)],
        scratch_shapes=[pltpu.VMEM((2, SUBLANES, FFN_TC), F32)],
        compiler_params=_params(("arbitrary", "arbitrary"), 24 * _nbytes((tb, FFN_TC), F32)),
    )(u, u, u, u, conv_w, conv_b, conv_w, conv_b, dh)


def _adamw_math(w, g, m, v):
    m_new = ADAM_B1 * m + (1.0 - ADAM_B1) * g
    v_new = ADAM_B2 * v + (1.0 - ADAM_B2) * jnp.square(g)
    m_hat = m_new / (1.0 - ADAM_B1 ** ADAM_STEP)
    v_hat = v_new / (1.0 - ADAM_B2 ** ADAM_STEP)
    return -ADAM_LR * (m_hat / (jnp.sqrt(v_hat) + ADAM_EPS) + ADAM_WD * w), m_new, v_new


def _adamw(name, w, g, m, v):
    rows, cols = w.shape
    tr = _tile(rows, (256, 176, 128, 64, 40, 32, 16, 8))

    def body(w_ref, g_ref, m_ref, v_ref, d_ref, nm_ref, nv_ref):
        d_ref[...], nm_ref[...], nv_ref[...] = _adamw_math(w_ref[...], g_ref[...], m_ref[...], v_ref[...])

    spec = pl.BlockSpec((tr, cols), lambda i: (i, 0))
    sh = jax.ShapeDtypeStruct((rows, cols), F32)
    return _pcall(
        body, name=name, grid=(rows // tr,), in_specs=[spec] * 4, out_specs=[spec] * 3, out_shape=[sh] * 3,
        compiler_params=_params(("arbitrary",), 14 * _nbytes((tr, -(-cols // LANES) * LANES), F32)),
    )(w, g, m, v)


def _adamw_halves(name, core, w, mine, theirs, m, v):
    rows, cols = w.shape
    tr = _tile(rows // 2, (256, 176, 128))
    nbh = rows // 2 // tr

    def body(c_ref, w_ref, a_ref, b_ref, m_ref, v_ref, g_ref, d_ref, nm_ref, nv_ref):
        g = jnp.where(pl.program_id(0) // nbh == c_ref[0], a_ref[...], b_ref[...])
        g_ref[...] = g
        d_ref[...], nm_ref[...], nv_ref[...] = _adamw_math(w_ref[...], g, m_ref[...], v_ref[...])

    spec = pl.BlockSpec((tr, cols), lambda i, c_ref: (i, 0))
    half = pl.BlockSpec((tr, cols), lambda i, c_ref: (i % nbh, 0))
    sh = jax.ShapeDtypeStruct((rows, cols), F32)
    grid_spec = pltpu.PrefetchScalarGridSpec(
        num_scalar_prefetch=1, grid=(rows // tr,), in_specs=[spec, half, half, spec, spec], out_specs=[spec] * 4)
    return _pcall(
        body, name=name, grid_spec=grid_spec, out_shape=[sh] * 4,
        compiler_params=_params(("arbitrary",), 18 * _nbytes((tr, -(-cols // LANES) * LANES), F32)),
    )(core, w, mine, theirs, m, v)


MESH = pl.DeviceIdType.MESH
ANY = pl.BlockSpec(memory_space=pl.ANY)
N_CHIPS = 4
N_DEV = 8
BF16_ROWS = 16


def _me():
    return lax.axis_index("x"), lax.axis_index("y"), lax.axis_index("c")


def _other_chips(x, y):
    return [(1 - x, y), (x, 1 - y), (1 - x, 1 - y)]


def _remote(src, dst, ssem, rsem, dev):
    return pltpu.make_async_remote_copy(src_ref=src, dst_ref=dst, send_sem=ssem, recv_sem=rsem,
                                        device_id=dev, device_id_type=MESH)


def _half_rows(ref_rows, cc):
    half = ref_rows // 2
    return pl.ds(pl.multiple_of(cc * half, BF16_ROWS), half)


def _gather_weights(shards):
    n = len(shards)
    n_ici = n * (N_CHIPS - 1)

    def body(*refs):
        ins, outs, (ssem, rsem, lsem, lrsem) = refs[:n], refs[n:2 * n], refs[2 * n:]
        x, y, c = _me()
        k_me = 2 * x + y
        sib = (x, y, 1 - c)
        chips = _other_chips(x, y)
        started = []
        for i, (w_ref, o_ref) in enumerate(zip(ins, outs)):
            cp = _remote(w_ref, o_ref.at[k_me], lsem.at[i], lrsem.at[i], sib)
            cp.start()
            started.append(cp)
        for r, (px, py) in enumerate(chips):
            for i, (w_ref, o_ref) in enumerate(zip(ins, outs)):
                rows = _half_rows(w_ref.shape[0], c)
                s = r * n + i
                cp = _remote(w_ref.at[rows], o_ref.at[k_me, rows], ssem.at[s], rsem.at[s], (px, py, c))
                cp.start()
                started.append(cp)
        for r, (px, py) in enumerate(chips):
            for i, o_ref in enumerate(outs):
                blk = o_ref.at[2 * px + py, _half_rows(o_ref.shape[1], c)]
                s = r * n + i
                _remote(blk, blk, ssem.at[s], rsem.at[s], (px, py, c)).wait_recv()
                cp = _remote(blk, blk, ssem.at[n_ici + s], rsem.at[n_ici + s], sib)
                cp.start()
                started.append(cp)
        for r, (px, py) in enumerate(chips):
            for i, o_ref in enumerate(outs):
                blk = o_ref.at[2 * px + py, _half_rows(o_ref.shape[1], 1 - c)]
                s = n_ici + r * n + i
                _remote(blk, blk, ssem.at[s], rsem.at[s], sib).wait_recv()
        for cp in started[n:]:
            cp.wait_send()
        for cp in started[:n]:
            cp.wait()

    return _pcall(
        body, name="gather_weights", in_specs=[ANY] * n, out_specs=[ANY] * n,
        out_shape=[jax.ShapeDtypeStruct((N_CHIPS,) + s.shape, s.dtype) for s in shards],
        scratch_shapes=[pltpu.SemaphoreType.DMA((2 * n_ici,)), pltpu.SemaphoreType.DMA((2 * n_ici,)),
                        pltpu.SemaphoreType.DMA((n,)), pltpu.SemaphoreType.DMA((n,))],
    )(*shards)


def _swap_halves(grads):
    n = len(grads)

    def body(*refs):
        ins, outs, (ssem, rsem) = refs[:n], refs[n:2 * n], refs[2 * n:]
        x, y, c = _me()
        copies = []
        for i, (g_ref, o_ref) in enumerate(zip(ins, outs)):
            for k in range(N_CHIPS):
                s = i * N_CHIPS + k
                cp = _remote(g_ref.at[k, _half_rows(g_ref.shape[1], 1 - c)], o_ref.at[k], ssem.at[s], rsem.at[s],
                             (x, y, 1 - c))
                cp.start()
                copies.append(cp)
        for cp in copies:
            cp.wait()

    return _pcall(
        body, name="swap_halves", in_specs=[ANY] * n, out_specs=[ANY] * n,
        out_shape=[jax.ShapeDtypeStruct((N_CHIPS, g.shape[1] // 2, g.shape[2]), g.dtype) for g in grads],
        scratch_shapes=[pltpu.SemaphoreType.DMA((n * N_CHIPS,)), pltpu.SemaphoreType.DMA((n * N_CHIPS,))],
    )(*grads)


def _scatter_chips(parts):
    n = len(parts)

    def body(*refs):
        ins, outs, (ssem, rsem) = refs[:n], refs[n:2 * n], refs[2 * n:]
        x, y, c = _me()
        k_me = 2 * x + y
        chips = _other_chips(x, y)
        sends = []
        for r, (px, py) in enumerate(chips):
            for i, (p_ref, o_ref) in enumerate(zip(ins, outs)):
                s = r * n + i
                cp = _remote(p_ref.at[2 * px + py], o_ref.at[k_me], ssem.at[s], rsem.at[s], (px, py, c))
                cp.start()
                sends.append(cp)
        for r, (px, py) in enumerate(chips):
            for i, o_ref in enumerate(outs):
                blk = o_ref.at[2 * px + py]
                s = r * n + i
                _remote(blk, blk, ssem.at[s], rsem.at[s], (px, py, c)).wait_recv()
        for cp in sends:
            cp.wait_send()

    n_sem = n * (N_CHIPS - 1)
    return _pcall(
        body, name="scatter_chips", in_specs=[ANY] * n, out_specs=[ANY] * n,
        out_shape=[jax.ShapeDtypeStruct(p.shape, p.dtype) for p in parts],
        scratch_shapes=[pltpu.SemaphoreType.DMA((n_sem,)), pltpu.SemaphoreType.DMA((n_sem,))],
    )(*parts)


def _share_halves(halves):
    n = len(halves)

    def body(*refs):
        ins, outs, (ssem, rsem) = refs[:n], refs[n:2 * n], refs[2 * n:]
        x, y, c = _me()
        copies = [_remote(r_ref, o_ref, ssem.at[i], rsem.at[i], (x, y, 1 - c))
                  for i, (r_ref, o_ref) in enumerate(zip(ins, outs))]
        for cp in copies:
            cp.start()
        for cp in copies:
            cp.wait()

    return _pcall(
        body, name="share_halves", in_specs=[ANY] * n, out_specs=[ANY] * n,
        out_shape=[jax.ShapeDtypeStruct(h.shape, h.dtype) for h in halves],
        scratch_shapes=[pltpu.SemaphoreType.DMA((n,)), pltpu.SemaphoreType.DMA((n,))],
    )(*halves)


def _exchange_small(v, reduce):
    rows = v.shape[0]

    def body(v_ref, out_ref, buf, ssem, rsem):
        x, y, c = _me()
        me = 4 * x + 2 * y + c
        peers = [((x + bx) % 2, (y + by) % 2, (c + bc) % 2)
                 for bx in (0, 1) for by in (0, 1) for bc in (0, 1) if (bx, by, bc) != (0, 0, 0)]
        dst = buf if reduce else out_ref
        dst[me] = v_ref[...]
        sends = [_remote(v_ref, dst.at[me], ssem.at[r], rsem.at[r], p) for r, p in enumerate(peers)]
        for cp in sends:
            cp.start()
        for r, (px, py, pc) in enumerate(peers):
            blk = dst.at[4 * px + 2 * py + pc]
            _remote(blk, blk, ssem.at[r], rsem.at[r], (px, py, pc)).wait_recv()
        if reduce:
            acc = buf[0]
            for d in range(1, N_DEV):
                acc = acc + buf[d]
            out_ref[...] = acc
        for cp in sends:
            cp.wait_send()

    vm = pl.BlockSpec(memory_space=pltpu.VMEM)
    out_shape = jax.ShapeDtypeStruct((rows, LANES) if reduce else (N_DEV, rows, LANES), F32)
    buf_shape = (N_DEV, rows, LANES) if reduce else (SUBLANES, LANES)
    return _pcall(
        body, pin=False, name="reduce_small" if reduce else "gather_small", in_specs=[vm], out_specs=vm, out_shape=out_shape,
        scratch_shapes=[pltpu.VMEM(buf_shape, F32), pltpu.SemaphoreType.DMA((N_DEV - 1,)),
                        pltpu.SemaphoreType.DMA((N_DEV - 1,))],
        compiler_params=pltpu.CompilerParams(vmem_limit_bytes=32 * 1024 * 1024),
    )(v)


def _add_pair(name, core, g, theirs):
    _, half, cols = theirs.shape
    tr = _tile(half, (256, 176, 128))
    nb = half // tr

    def body(c_ref, g_ref, t_ref, o32_ref, o16_ref):
        s = g_ref[...] + t_ref[...]
        o32_ref[...] = s
        o16_ref[...] = s.astype(BF16)

    spec = pl.BlockSpec((None, tr, cols), lambda k, i, c_ref: (k, i, 0))
    grid_spec = pltpu.PrefetchScalarGridSpec(
        num_scalar_prefetch=1, grid=(N_CHIPS, nb),
        in_specs=[pl.BlockSpec((None, tr, cols), lambda k, i, c_ref: (k, c_ref[0] * nb + i, 0)), spec],
        out_specs=[spec, spec])
    return _pcall(
        body, name=name, grid_spec=grid_spec,
        out_shape=[jax.ShapeDtypeStruct(theirs.shape, F32), jax.ShapeDtypeStruct(theirs.shape, BF16)],
        compiler_params=_params(("arbitrary", "arbitrary"), 8 * _nbytes((tr, cols + LANES), F32)),
    )(core, g, theirs)


def _add_chips(name, chip, p32, recv):
    _, half, cols = p32.shape
    tr = _tile(half, (256, 176, 128))

    def body(k_ref, p_ref, r0_ref, r1_ref, r2_ref, o_ref):
        o_ref[...] = ((p_ref[...] + r0_ref[...].astype(F32)) + r1_ref[...].astype(F32)) + r2_ref[...].astype(F32)

    def other(r):
        return pl.BlockSpec((None, tr, cols), lambda i, k_ref: (r + (k_ref[0] <= r).astype(jnp.int32), i, 0))
    grid_spec = pltpu.PrefetchScalarGridSpec(
        num_scalar_prefetch=1, grid=(half // tr,),
        in_specs=[pl.BlockSpec((None, tr, cols), lambda i, k_ref: (k_ref[0], i, 0)), other(0), other(1), other(2)],
        out_specs=pl.BlockSpec((tr, cols), lambda i, k_ref: (i, 0)))
    return _pcall(
        body, name=name, grid_spec=grid_spec, out_shape=jax.ShapeDtypeStruct((half, cols), F32),
        compiler_params=_params(("arbitrary",), 10 * _nbytes((tr, cols + LANES), F32)),
    )(chip, p32, recv, recv, recv)


def kernel(x, mem, w_in, b_in, hg_lb_logits, hg_norm_w, ml_conv_w, ml_conv_b, ml_norm_w, w_out, ln1_g, ln1_b, ca_wq, ca_wkv, ca_wo, ln2_g, ln2_b, ffn_w_up, ffn_conv_w, ffn_conv_b, ffn_w_down, ln3_g, ln3_b, loss_target, m_w_in, m_b_in, m_hg_lb_logits, m_hg_norm_w, m_ml_conv_w, m_ml_conv_b, m_ml_norm_w, m_w_out, m_ln1_g, m_ln1_b, m_ca_wq, m_ca_wkv, m_ca_wo, m_ln2_g, m_ln2_b, m_ffn_w_up, m_ffn_conv_w, m_ffn_conv_b, m_ffn_w_down, m_ln3_g, m_ln3_b, v_w_in, v_b_in, v_hg_lb_logits, v_hg_norm_w, v_ml_conv_w, v_ml_conv_b, v_ml_norm_w, v_w_out, v_ln1_g, v_ln1_b, v_ca_wq, v_ca_wkv, v_ca_wo, v_ln2_g, v_ln2_b, v_ffn_w_up, v_ffn_conv_w, v_ffn_conv_b, v_ffn_w_down, v_ln3_g, v_ln3_b):
    return _train_step(dict(locals()))


WEIGHTS = ("w_in", "b_in", "hg_lb_logits", "hg_norm_w", "ml_conv_w", "ml_conv_b", "ml_norm_w", "w_out", "ln1_g",
           "ln1_b", "ca_wq", "ca_wkv", "ca_wo", "ln2_g", "ln2_b", "ffn_w_up", "ffn_conv_w", "ffn_conv_b",
           "ffn_w_down", "ln3_g", "ln3_b")
MATRICES = ("w_in", "w_out", "ca_wq", "ca_wkv", "ca_wo", "ffn_w_up", "ffn_w_down")
COL_SHARDED = ("w_in", "ca_wkv", "ffn_w_up", "ml_conv_w", "ffn_conv_w")
SMALL = tuple(n for n in WEIGHTS if n not in MATRICES)
PART_ROWS = 16


def _part_rows(shape, lead):
    n = 1
    for s in shape[lead:]:
        n *= s
    return -(-n // (LANES * PART_ROWS)) * PART_ROWS


def _pack(arrs, dtype, lead=0, rows=None):
    parts = []
    for a in arrs:
        head = a.shape[:lead]
        flat = a.reshape(head + (-1,)).astype(dtype)
        pad = _part_rows(a.shape, lead) * LANES - flat.shape[-1]
        flat = jnp.pad(flat, [(0, 0)] * lead + [(0, pad)])
        parts.append(flat.reshape(head + (-1, LANES)))
    used = sum(p.shape[lead] for p in parts)
    if rows is not None and rows > used:
        parts.append(jnp.zeros(parts[0].shape[:lead] + (rows - used, LANES), dtype))
    return jnp.concatenate(parts, axis=lead)


def _unpack(buf, shapes):
    lead = buf.shape[:-2]
    outs, r = [], 0
    for sh in shapes:
        n = 1
        for s in sh:
            n *= s
        nr = _part_rows(sh, 0)
        flat = buf[..., r:r + nr, :].reshape(lead + (nr * LANES,))
        outs.append(flat[..., :n].reshape(lead + tuple(sh)))
        r += nr
    return outs


def _cat_cols(s):
    return jnp.moveaxis(s, 0, 1).reshape(s.shape[1], -1)


def _split_cols(g):
    return jnp.moveaxis(g.reshape(g.shape[0], N_CHIPS, -1), 1, 0)


def _stack_rows(s):
    return s.reshape(-1, s.shape[-1])


def _train_step(a):
    xs, mems, tgt = a["x"][0], a["mem"][0], a["loss_target"][0]
    core = lax.axis_index("c").astype(jnp.int32).reshape(1)
    chip = (2 * lax.axis_index("x") + lax.axis_index("y")).astype(jnp.int32).reshape(1)
    k_me = chip[0]
    shard = {n: a[n][0] for n in MATRICES}

    w = dict(zip(MATRICES, _gather_weights([shard[n].astype(BF16) for n in MATRICES])))
    for n in ("w_out", "ca_wq", "ca_wo", "ffn_w_down"):
        w[n] = _stack_rows(w[n])
    w["w_in"] = jnp.pad(_cat_cols(w["w_in"]), ((0, 0), (0, D_IN_PAD - D_IN)))
    taps = _exchange_small(_pack([a["ml_conv_w"][0], a["ffn_conv_w"][0]], F32), reduce=False)
    taps = taps.reshape((N_CHIPS, 2) + taps.shape[1:])[:, 0]
    ml_cw, ffn_cw = [_cat_cols(s) for s in _unpack(taps, [a["ml_conv_w"].shape[1:], a["ffn_conv_w"].shape[1:]])]
    b_in_p = jnp.pad(a["b_in"], ((0, 0), (0, D_IN_PAD - D_IN)))
    mixer_w = (a["hg_lb_logits"], a["hg_norm_w"], ml_cw, a["ml_conv_b"], a["ml_norm_w"])
    up_cols = a["ffn_w_up"].shape[-1]

    xb = xs.astype(BF16)
    proj = _mm("proj", "nn", xb, w["w_in"], bias=b_in_p, tm=256, tn=D_IN_PAD)
    y, hst, cst, nst, mst = _mixer_fwd(proj, *mixer_w)
    z1, x1, x1b = _mm("mix_out", "nn", y, w["w_out"], res=xs, res_scale=ALPHA, ln=("fwd", a["ln1_g"], a["ln1_b"]),
                      copy_dtype=BF16)
    q = _mm("ca_q", "nn", x1b, w["ca_wq"], out_dtype=BF16, tn=D_MODEL)
    kv = _mm("ca_kv", "nn", mems, w["ca_wkv"])
    o = _attn_fwd(q, kv)
    z2, x2, x2b = _mm("ca_out", "nn", o, w["ca_wo"], res=x1, res_scale=ALPHA, ln=("fwd", a["ln2_g"], a["ln2_b"]),
                      copy_dtype=BF16)
    u = _mm("ffn_up", "nn", x2b, w["ffn_w_up"], tn=up_cols)
    hmid = _ffn_mid_fwd(u, ffn_cw, a["ffn_conv_b"])
    dz3, g_ln3g, g_ln3b, loss_part, dz3b = _mm("ffn_down", "nn", hmid, w["ffn_w_down"], res=x2, res_scale=ALPHA,
                                               ln=("loss", a["ln3_g"], a["ln3_b"], tgt), copy_dtype=BF16)

    grads = {"ln3_g": g_ln3g, "ln3_b": g_ln3b}
    dhmid = _mm("d_hmid", "nt", dz3b, w["ffn_w_down"], tn=D_FF)
    grads["ffn_w_down"] = _mm("g_w_down", "tn", hmid, dz3b, tm=D_FF // 2, tn=D_MODEL)
    du, g_cw, g_cb = _ffn_mid_bwd(u, ffn_cw, a["ffn_conv_b"], dhmid)
    grads["ffn_conv_w"] = jnp.moveaxis(g_cw, 0, 1).reshape(FFN_CONV, 2 * D_FF)
    grads["ffn_conv_b"] = g_cb.reshape(1, 2 * D_FF)
    grads["ffn_w_up"] = _mm("g_w_up", "tn", x2b, du, out_groups=N_CHIPS, tm=D_MODEL, tn=up_cols)
    dz2, grads["ln2_g"], grads["ln2_b"], dz2b = _mm("d_x2", "nt", du, w["ffn_w_up"], res=dz3, res_scale=ALPHA,
                                                    ln=("bwd", z2, a["ln2_g"], a["ln2_b"]), copy_dtype=BF16)
    do = _mm("d_o", "nt", dz2b, w["ca_wo"], out_dtype=BF16, tn=D_MODEL)
    grads["ca_wo"] = _mm("g_wo", "tn", o, dz2b, tm=D_MODEL, tn=D_MODEL)
    dq, dkv = _attn_bwd(q, kv, do)
    grads["ca_wq"] = _mm("g_wq", "tn", x1b, dq, tm=D_MODEL, tn=D_MODEL)
    grads["ca_wkv"] = _mm("g_wkv", "tn", mems, dkv, out_groups=N_CHIPS, tm=D_MODEL)
    dz1, grads["ln1_g"], grads["ln1_b"], dz1b = _mm("d_x1", "nt", dq, w["ca_wq"], res=dz2, res_scale=ALPHA,
                                                    ln=("bwd", z1, a["ln1_g"], a["ln1_b"]), copy_dtype=BF16)
    dy = _mm("d_y", "nt", dz1b, w["w_out"], tn=D_MODEL)
    grads["w_out"] = _mm("g_w_out", "tn", y, dz1b, tm=D_MODEL, tn=D_MODEL)
    (dproj, g_b_in, grads["hg_lb_logits"], grads["hg_norm_w"], grads["ml_conv_w"], grads["ml_conv_b"],
     grads["ml_norm_w"]) = _mixer_bwd(proj, dy, hst, cst, nst, mst, *mixer_w)
    grads["w_in"] = _split_cols(_mm("g_w_in", "tn", xb, dproj, tm=D_MODEL, tn=up_cols)[:, :D_IN])
    grads["b_in"] = g_b_in[:, :D_IN]
    dx = _mm("d_x", "nt", dproj, w["w_in"], res=dz1, res_scale=ALPHA, tm=256, tn=D_MODEL)
    for n in ("w_out", "ca_wq", "ca_wo", "ffn_w_down"):
        grads[n] = grads[n].reshape((N_CHIPS,) + shard[n].shape)

    per_chip = [grads[n] for n in MATRICES]
    sums = [_add_pair("add_pair_" + n, core, g, t) for n, g, t in zip(MATRICES, per_chip, _swap_halves(per_chip))]
    recv = _scatter_chips([s16 for _, s16 in sums])
    halves = [_add_chips("add_chips_" + n, chip, s32, r) for n, (s32, _), r in zip(MATRICES, sums, recv)]
    other_halves = _share_halves(halves)

    small_shapes = [grads[n].shape for n in SMALL] + [loss_part.shape]
    summed = _unpack(_exchange_small(_pack([grads[n] for n in SMALL] + [loss_part], F32), reduce=True), small_shapes)
    loss = summed[-1][0, 0]
    for n, g in zip(SMALL, summed[:-1]):
        if n in COL_SHARDED:
            cols = a[n].shape[-1]
            g = lax.dynamic_slice_in_dim(g, k_me * cols, cols, axis=1)
        grads[n] = g

    delta, new_m, new_v = {}, {}, {}
    for n, mine, theirs in zip(MATRICES, halves, other_halves):
        grads[n], delta[n], new_m[n], new_v[n] = _adamw_halves(
            "adamw_" + n, core, shard[n], mine, theirs, a["m_" + n][0], a["v_" + n][0])
    small_w = [a[n][0] if a[n].ndim == 3 else a[n] for n in SMALL]
    small_m = [a["m_" + n][0] if a[n].ndim == 3 else a["m_" + n] for n in SMALL]
    small_v = [a["v_" + n][0] if a[n].ndim == 3 else a["v_" + n] for n in SMALL]
    shapes = [w.shape for w in small_w]
    packed = [_pack(l, F32) for l in (small_w, [grads[n] for n in SMALL], small_m, small_v)]
    for out, buf in zip((delta, new_m, new_v), _adamw("adamw_small", *packed)):
        for n, v in zip(SMALL, _unpack(buf, shapes)):
            out[n] = v

    def shaped(d):
        return [d[n].reshape(a[n].shape) for n in WEIGHTS]
    return (loss, dx[None], *shaped(grads), *shaped(delta), *shaped(new_m), *shaped(new_v))
```

```python
import functools

import jax
import jax.numpy as jnp
from jax import lax
from jax.experimental import pallas as pl
from jax.experimental.pallas import tpu as pltpu

F32 = jnp.float32
BF16 = jnp.bfloat16

D_MODEL = 1024
HEADS = 4
DK = 128
D_GRP = HEADS * DK
CHUNK = 64
ML_CONV = 4
FFN_CONV = 3
D_FF = 2816
CA_DH = D_MODEL // HEADS
DEPTH = 1
ALPHA = (2.0 * DEPTH) ** 0.25
LN_EPS = 1e-5
NEG_BIG = -1e30
D_IN = 8 * D_GRP + 2 * HEADS
D_IN_PAD = 8 * D_GRP + 128
ADAM_LR, ADAM_B1, ADAM_B2, ADAM_EPS, ADAM_WD, ADAM_STEP = 0.001, 0.9, 0.999, 1e-08, 0.01, 10

SUBLANES = 8
LANES = 128
VMEM_BYTES = 64 * 1024 * 1024


def _pcall(body, pin=True, **kw):
    if not pin:
        return _call(body, **kw)
    kw["out_shape"] = jax.tree.map(lambda s: pltpu.HBM(s.shape, s.dtype), kw["out_shape"])
    call = _call(body, **kw)

    def pinned(*args):
        return call(*[pltpu.with_memory_space_constraint(x, pltpu.HBM) if jnp.issubdtype(x.dtype, jnp.floating) else x
                      for x in args])
    return pinned


def _call(body, **kw):
    return pl.pallas_call(body, **kw)


def _params(semantics, vmem_bytes):
    limit = int(min(max(2 * vmem_bytes, 16 * 1024 * 1024), VMEM_BYTES - 8 * 1024 * 1024))
    return pltpu.CompilerParams(dimension_semantics=semantics, vmem_limit_bytes=limit)


def _nbytes(shape, dtype):
    n = 1
    for s in shape:
        n *= s
    return n * jnp.dtype(dtype).itemsize


def _dg(a, b, ca, cb):
    return lax.dot_general(a.astype(BF16), b.astype(BF16), (((ca,), (cb,)), ((), ())),
                           preferred_element_type=F32)


@jax.custom_vjp
def mm_nn(a, b):
    return _dg(a, b, 1, 0)


mm_nn.defvjp(lambda a, b: (_dg(a, b, 1, 0), (a, b)),
             lambda r, g: (_dg(g, r[1], 1, 1).astype(r[0].dtype), _dg(r[0], g, 0, 0).astype(r[1].dtype)))


@jax.custom_vjp
def mm_nt(a, b):
    return _dg(a, b, 1, 1)


mm_nt.defvjp(lambda a, b: (_dg(a, b, 1, 1), (a, b)),
             lambda r, g: (_dg(g, r[1], 1, 0).astype(r[0].dtype), _dg(g, r[0], 0, 0).astype(r[1].dtype)))


@jax.custom_vjp
def mm_tn(a, b):
    return _dg(a, b, 0, 0)


mm_tn.defvjp(lambda a, b: (_dg(a, b, 0, 0), (a, b)),
             lambda r, g: (_dg(r[1], g, 1, 1).astype(r[0].dtype), _dg(r[0], g, 1, 0).astype(r[1].dtype)))


def _hdot(a, b):
    return jnp.dot(a, b, precision=lax.Precision.HIGHEST, preferred_element_type=F32)


def _tri(n, lower):
    r = lax.broadcasted_iota(jnp.int32, (n, n), 0)
    c = lax.broadcasted_iota(jnp.int32, (n, n), 1)
    return ((r >= c) if lower else (r <= c)).astype(F32)


@jax.custom_vjp
def cumsum_rows(x):
    return _hdot(_tri(x.shape[0], True), x)


cumsum_rows.defvjp(lambda x: (_hdot(_tri(x.shape[0], True), x), None),
                   lambda _, g: (_hdot(_tri(g.shape[0], False), g),))


def _shift_impl(halo, x, d):
    xx = jnp.concatenate([halo, x], axis=0)
    return pltpu.roll(xx, d, 0)[SUBLANES:]


@functools.partial(jax.custom_vjp, nondiff_argnums=(2,))
def shift_rows(halo, x, d):
    return _shift_impl(halo, x, d)


def _shift_bwd(d, _, g):
    n = g.shape[0] + SUBLANES
    gg = jnp.concatenate([jnp.zeros((SUBLANES, g.shape[1]), g.dtype), g], axis=0)
    r = pltpu.roll(gg, n - d, 0)
    return r[:SUBLANES], r[SUBLANES:]


shift_rows.defvjp(lambda halo, x, d: (_shift_impl(halo, x, d), None), _shift_bwd)


def causal_conv(halo, x, w_rows, b):
    k = len(w_rows)
    y = b + w_rows[k - 1] * x
    for d in range(1, k):
        y = y + w_rows[k - 1 - d] * shift_rows(halo, x, d)
    return y


def _sigmoid(x):
    return 1.0 / (1.0 + jnp.exp(-x))


def _silu(x):
    return x * _sigmoid(x)


def _log_sigmoid(x):
    return jnp.minimum(x, 0.0) - jnp.log(1.0 + jnp.exp(-jnp.abs(x)))


def _pick_lane(x, j):
    lane = lax.broadcasted_iota(jnp.int32, (1, x.shape[1]), 1)
    return jnp.sum(jnp.where(lane == j, x, 0.0), axis=1, keepdims=True)


def _pick_row(x, i):
    row = lax.broadcasted_iota(jnp.int32, (x.shape[0], 1), 0)
    return jnp.sum(jnp.where(row == i, x, 0.0), axis=0, keepdims=True)


def _col_to_row(e):
    n = e.shape[0]
    eye = lax.broadcasted_iota(jnp.int32, (n, n), 0) == lax.broadcasted_iota(jnp.int32, (n, n), 1)
    return jnp.sum(jnp.where(eye, e, 0.0), axis=0, keepdims=True)


def _layer_norm(z, g, b):
    mu = jnp.mean(z, axis=-1, keepdims=True)
    zc = z - mu
    var = jnp.mean(zc * zc, axis=-1, keepdims=True)
    return zc * lax.rsqrt(var + LN_EPS) * g + b


def _hg_head(st_t, hq, hf, hi, hgate, l0, l1, nw):
    n = hq.shape[0]
    lb = _sigmoid(l0 - l1)
    q = _silu(hq)
    lf = jnp.log(lb + (1.0 - lb) * _sigmoid(hf))
    k = (1.0 - lb) * _sigmoid(-hf)
    b = cumsum_rows(lf)
    b_ref = _pick_row(b, n // 2 - 1)
    b_last = _pick_row(b, n - 1)
    attn = mm_nt(q * jnp.exp(b - b_ref), k * jnp.exp(b_ref - b))
    attn = jnp.where(_tri(n, True) > 0, attn, 0.0)
    o = mm_nn(attn, hi) + mm_nt(q * jnp.exp(b), st_t)
    st_new = jnp.exp(b_last) * st_t + mm_tn(hi, k * jnp.exp(b_last - b))
    y = o * lax.rsqrt(jnp.mean(o * o, axis=-1, keepdims=True) + LN_EPS) * nw * _silu(hgate)
    return st_new, y


def _ml_head(c_st, n_st, m_st, q, k, v, gates, og, nw, h):
    n = q.shape[0]
    ig = _pick_lane(gates, h)
    fl = _log_sigmoid(_pick_lane(gates, HEADS + h))
    qs = q * (DK ** -0.5)
    b = _pick_lane(cumsum_rows(jnp.broadcast_to(fl, (n, LANES))), 0)
    g = jnp.sum(fl, axis=0, keepdims=True)
    d = jnp.where(_tri(n, True) > 0, b + _col_to_row(ig - b), -jnp.inf)
    inter = b + m_st
    m_t = jnp.maximum(inter, jnp.max(d, axis=1, keepdims=True))
    s = mm_nt(qs, k) * jnp.exp(d - m_t)
    w_inter = jnp.exp(inter - m_t)
    num = mm_nn(s, v) + w_inter * mm_nn(qs, c_st)
    den = jnp.sum(s, axis=1, keepdims=True) + w_inter * jnp.sum(qs * n_st, axis=1, keepdims=True)
    h_out = num / jnp.maximum(jnp.abs(den), jnp.exp(-m_t))
    a = g - b + ig
    m_new = jnp.maximum(g + m_st, jnp.max(a, axis=0, keepdims=True))
    decay = jnp.exp(g + m_st - m_new)
    wk = k * jnp.exp(a - m_new)
    c_new = decay * c_st + mm_tn(wk, v)
    n_new = decay * n_st + jnp.sum(wk, axis=0, keepdims=True)
    mu = jnp.mean(h_out, axis=-1, keepdims=True)
    hc = h_out - mu
    var = jnp.mean(hc * hc, axis=-1, keepdims=True)
    y = _sigmoid(og) * (hc * lax.rsqrt(var + LN_EPS) * nw)
    return c_new, n_new, m_new, y


def _qk_conv(halo, x, w0, w1, w2, w3, b):
    return _silu(causal_conv(halo, x, (w0, w1, w2, w3), b))


def _grp(i, h=None):
    if h is None:
        return pl.ds(i * D_GRP, D_GRP)
    return pl.ds(i * D_GRP + h * DK, DK)


def _mixer_specs(n_chunks, reverse):
    def chunk(c):
        return n_chunks - 1 - c if reverse else c
    row8 = CHUNK // SUBLANES
    proj_spec = pl.BlockSpec((CHUNK, D_IN_PAD), lambda c: (chunk(c), 0))
    halo_spec = pl.BlockSpec((SUBLANES, 2 * D_GRP), lambda c: (jnp.maximum(chunk(c) * row8 - 1, 0), 2))
    small = [pl.BlockSpec((2, D_GRP), lambda c: (0, 0)), pl.BlockSpec((1, D_GRP), lambda c: (0, 0)),
             pl.BlockSpec((ML_CONV, 2 * D_GRP), lambda c: (0, 0)), pl.BlockSpec((1, 2 * D_GRP), lambda c: (0, 0)),
             pl.BlockSpec((1, D_GRP), lambda c: (0, 0))]
    state_specs = [pl.BlockSpec((1, HEADS, DK, DK), lambda c: (chunk(c), 0, 0, 0)),
                   pl.BlockSpec((1, HEADS, DK, DK), lambda c: (chunk(c), 0, 0, 0)),
                   pl.BlockSpec((1, HEADS, 1, DK), lambda c: (chunk(c), 0, 0, 0)),
                   pl.BlockSpec((1, HEADS, 1, DK), lambda c: (chunk(c), 0, 0, 0))]
    y_spec = pl.BlockSpec((CHUNK, 2 * D_GRP), lambda c: (chunk(c), 0))
    return proj_spec, halo_spec, small, state_specs, y_spec, chunk


def _mixer_fwd(proj, lb_logits, hg_nw, conv_w, conv_b, ml_nw):
    seq = proj.shape[0]
    n_chunks = seq // CHUNK
    proj_spec, halo_spec, small, state_specs, y_spec, _ = _mixer_specs(n_chunks, False)

    def body(proj_ref, halo_ref, lg_ref, hnw_ref, cw_ref, cb_ref, mnw_ref,
             y_ref, hst_ref, cst_ref, nst_ref, mst_ref, hs, cs, ns, ms):
        c = pl.program_id(0)

        @pl.when(c == 0)
        def _():
            hs[...] = jnp.zeros_like(hs)
            cs[...] = jnp.zeros_like(cs)
            ns[...] = jnp.zeros_like(ns)
            ms[...] = jnp.full(ms.shape, NEG_BIG, F32)

        hst_ref[0] = hs[...]
        cst_ref[0] = cs[...]
        nst_ref[0] = ns[...]
        mst_ref[0] = ms[...]
        halo = jnp.where(c > 0, halo_ref[...], 0.0)
        qk = _qk_conv(halo, proj_ref[:, pl.ds(4 * D_GRP, 2 * D_GRP)],
                      cw_ref[0:1, :], cw_ref[1:2, :], cw_ref[2:3, :], cw_ref[3:4, :], cb_ref[...])
        gates = proj_ref[:, pl.ds(8 * D_GRP, LANES)]
        for h in range(HEADS):
            hd = pl.ds(h * DK, DK)
            st_new, y = _hg_head(hs[h], proj_ref[:, _grp(0, h)], proj_ref[:, _grp(1, h)], proj_ref[:, _grp(2, h)],
                                 proj_ref[:, _grp(3, h)], lg_ref[0:1, hd], lg_ref[1:2, hd], hnw_ref[:, hd])
            hs[h] = st_new
            y_ref[:, hd] = y
            c_new, n_new, m_new, y = _ml_head(
                cs[h], ns[h], _pick_lane(ms[h], 0), qk[:, h * DK:(h + 1) * DK],
                qk[:, D_GRP + h * DK:D_GRP + (h + 1) * DK], proj_ref[:, _grp(6, h)], gates,
                proj_ref[:, _grp(7, h)], mnw_ref[:, hd], h)
            cs[h] = c_new
            ns[h] = n_new
            ms[h] = jnp.broadcast_to(m_new, (1, DK))
            y_ref[:, pl.ds(D_GRP + h * DK, DK)] = y

    st = jax.ShapeDtypeStruct((n_chunks, HEADS, DK, DK), F32)
    vec = jax.ShapeDtypeStruct((n_chunks, HEADS, 1, DK), F32)
    vmem = 2 * (_nbytes((CHUNK, D_IN_PAD), F32) + _nbytes((CHUNK, 2 * D_GRP), F32) + 2 * _nbytes((HEADS, DK, DK), F32)) \
        + 2 * _nbytes((HEADS, DK, DK), F32)
    return _pcall(
        body, name="mixer_fwd", grid=(n_chunks,),
        in_specs=[proj_spec, halo_spec] + small,
        out_specs=[y_spec] + state_specs,
        out_shape=[jax.ShapeDtypeStruct((seq, 2 * D_GRP), F32), st, st, vec, vec],
        scratch_shapes=[pltpu.VMEM((HEADS, DK, DK), F32), pltpu.VMEM((HEADS, DK, DK), F32),
                        pltpu.VMEM((HEADS, 1, DK), F32), pltpu.VMEM((HEADS, 1, DK), F32)],
        compiler_params=_params(("arbitrary",), vmem),
    )(proj, proj, lb_logits, hg_nw, conv_w, conv_b, ml_nw)


def _mixer_bwd(proj, dy, hst, cst, nst, mst, lb_logits, hg_nw, conv_w, conv_b, ml_nw):
    seq = proj.shape[0]
    n_chunks = seq // CHUNK
    proj_spec, halo_spec, small, state_specs, y_spec, _ = _mixer_specs(n_chunks, True)

    def body(proj_ref, halo_ref, dy_ref, hst_ref, cst_ref, nst_ref, mst_ref,
             lg_ref, hnw_ref, cw_ref, cb_ref, mnw_ref,
             dproj_ref, dlg_ref, dhnw_ref, dcw_ref, dcb_ref, dmnw_ref,
             dhs, dcs, dns, dms, dhalo, dqk):
        c = pl.program_id(0)

        @pl.when(c == 0)
        def _():
            for r in (dhs, dcs, dns, dms, dhalo, dlg_ref, dhnw_ref, dcw_ref, dcb_ref, dmnw_ref):
                r[...] = jnp.zeros_like(r)

        first = c == n_chunks - 1
        halo = jnp.where(first, 0.0, halo_ref[...])
        x_qk = proj_ref[:, pl.ds(4 * D_GRP, 2 * D_GRP)]
        conv_args = (halo, x_qk, cw_ref[0:1, :], cw_ref[1:2, :], cw_ref[2:3, :], cw_ref[3:4, :], cb_ref[...])
        qk, conv_vjp = jax.vjp(_qk_conv, *conv_args)
        gates = proj_ref[:, pl.ds(8 * D_GRP, LANES)]
        dgates = jnp.zeros((CHUNK, LANES), F32)
        for h in range(HEADS):
            hd = pl.ds(h * DK, DK)
            args = (hst_ref[0, h], proj_ref[:, _grp(0, h)], proj_ref[:, _grp(1, h)], proj_ref[:, _grp(2, h)],
                    proj_ref[:, _grp(3, h)], lg_ref[0:1, hd], lg_ref[1:2, hd], hnw_ref[:, hd])
            _, vjp = jax.vjp(_hg_head, *args)
            dst, dhq, dhf, dhi, dhg, dl0, dl1, dnw = vjp((dhs[h], dy_ref[:, hd]))
            dhs[h] = dst
            dproj_ref[:, _grp(0, h)] = dhq
            dproj_ref[:, _grp(1, h)] = dhf
            dproj_ref[:, _grp(2, h)] = dhi
            dproj_ref[:, _grp(3, h)] = dhg
            dlg_ref[0:1, hd] += dl0
            dlg_ref[1:2, hd] += dl1
            dhnw_ref[:, hd] += dnw

            margs = (cst_ref[0, h], nst_ref[0, h], _pick_lane(mst_ref[0, h], 0), qk[:, h * DK:(h + 1) * DK],
                     qk[:, D_GRP + h * DK:D_GRP + (h + 1) * DK], proj_ref[:, _grp(6, h)], gates,
                     proj_ref[:, _grp(7, h)], mnw_ref[:, hd])
            _, mvjp = jax.vjp(functools.partial(_ml_head, h=h), *margs)
            dc, dn, dm, dq, dk, dv, dg, dog, dmn = mvjp(
                (dcs[h], dns[h], _pick_lane(dms[h], 0), dy_ref[:, pl.ds(D_GRP + h * DK, DK)]))
            dcs[h] = dc
            dns[h] = dn
            dms[h] = jnp.broadcast_to(dm, (1, DK))
            dqk[:, hd] = dq
            dqk[:, pl.ds(D_GRP + h * DK, DK)] = dk
            dproj_ref[:, _grp(6, h)] = dv
            dproj_ref[:, _grp(7, h)] = dog
            dmnw_ref[:, hd] += dmn
            dgates = dgates + dg
        dproj_ref[:, pl.ds(8 * D_GRP, LANES)] = dgates
        dh, dx, dw0, dw1, dw2, dw3, db = conv_vjp(dqk[...])
        tail = jnp.concatenate([jnp.zeros((CHUNK - SUBLANES, 2 * D_GRP), F32), dhalo[...]], axis=0)
        dproj_ref[:, pl.ds(4 * D_GRP, 2 * D_GRP)] = dx + tail
        dhalo[...] = dh
        dcw_ref[0:1, :] += dw0
        dcw_ref[1:2, :] += dw1
        dcw_ref[2:3, :] += dw2
        dcw_ref[3:4, :] += dw3
        dcb_ref[...] += db

    small_out = [pl.BlockSpec((2, D_GRP), lambda c: (0, 0)), pl.BlockSpec((1, D_GRP), lambda c: (0, 0)),
                 pl.BlockSpec((ML_CONV, 2 * D_GRP), lambda c: (0, 0)), pl.BlockSpec((1, 2 * D_GRP), lambda c: (0, 0)),
                 pl.BlockSpec((1, D_GRP), lambda c: (0, 0))]
    vmem = 2 * (2 * _nbytes((CHUNK, D_IN_PAD), F32) + _nbytes((CHUNK, 2 * D_GRP), F32)
                + 2 * _nbytes((HEADS, DK, DK), F32)) + 2 * _nbytes((HEADS, DK, DK), F32) + 4 * 1024 * 1024
    return _pcall(
        body, name="mixer_bwd", grid=(n_chunks,),
        in_specs=[proj_spec, halo_spec, y_spec] + state_specs + small,
        out_specs=[proj_spec] + small_out,
        out_shape=[jax.ShapeDtypeStruct((seq, D_IN_PAD), F32), jax.ShapeDtypeStruct((2, D_GRP), F32),
                   jax.ShapeDtypeStruct((1, D_GRP), F32), jax.ShapeDtypeStruct((ML_CONV, 2 * D_GRP), F32),
                   jax.ShapeDtypeStruct((1, 2 * D_GRP), F32), jax.ShapeDtypeStruct((1, D_GRP), F32)],
        scratch_shapes=[pltpu.VMEM((HEADS, DK, DK), F32), pltpu.VMEM((HEADS, DK, DK), F32),
                        pltpu.VMEM((HEADS, 1, DK), F32), pltpu.VMEM((HEADS, 1, DK), F32),
                        pltpu.VMEM((SUBLANES, 2 * D_GRP), F32), pltpu.VMEM((CHUNK, 2 * D_GRP), F32)],
        compiler_params=_params(("arbitrary",), vmem),
    )(proj, proj, dy, hst, cst, nst, mst, lb_logits, hg_nw, conv_w, conv_b, ml_nw)


def _heads(x):
    return [x[:, h * DK:(h + 1) * DK] for h in range(HEADS)]


def _last(x, j):
    lane = lax.broadcasted_iota(jnp.int32, (1, x.shape[-1]), 1)
    return jnp.sum(jnp.where(lane == j, x, 0.0), axis=-1, keepdims=True)


def _hg_chunk(st_t, hq, hf, hi, hgate, l0, l1, nw):
    n = hq.shape[0]
    lb = _sigmoid(l0 - l1)
    q = _silu(hq)
    lf = jnp.log(lb + (1.0 - lb) * _sigmoid(hf))
    k = (1.0 - lb) * _sigmoid(-hf)
    b = cumsum_rows(lf)
    b_ref = _pick_row(b, n // 2 - 1)
    b_last = _pick_row(b, n - 1)
    qa, ka = _heads(q * jnp.exp(b - b_ref)), _heads(k * jnp.exp(b_ref - b))
    qe, kd, eb, v = _heads(q * jnp.exp(b)), _heads(k * jnp.exp(b_last - b)), _heads(jnp.exp(b_last)), _heads(hi)
    tri = _tri(n, True) > 0
    attn = [jnp.where(tri, mm_nt(qa[h], ka[h]), 0.0) for h in range(HEADS)]
    o = [mm_nn(attn[h], v[h]) + mm_nt(qe[h], st_t[h]) for h in range(HEADS)]
    st_new = jnp.stack([eb[h] * st_t[h] + mm_tn(v[h], kd[h]) for h in range(HEADS)])
    yn = [o[h] * lax.rsqrt(jnp.mean(o[h] * o[h], axis=-1, keepdims=True) + LN_EPS) for h in range(HEADS)]
    return st_new, jnp.concatenate(yn, axis=1) * nw * _silu(hgate)


def _ml_chunk(c_st, n_st, m_st, q, k, v, gates, og, nw):
    n = q.shape[0]
    ig = jnp.stack([_last(gates, h) for h in range(HEADS)])
    fl = _log_sigmoid(jnp.stack([_last(gates, HEADS + h) for h in range(HEADS)]))
    bw = cumsum_rows(jnp.concatenate([jnp.broadcast_to(fl[h], (n, DK)) for h in range(HEADS)], axis=1))
    b = jnp.stack([_last(x, 0) for x in _heads(bw)])
    g = jnp.sum(fl, axis=1, keepdims=True)
    eye = lax.broadcasted_iota(jnp.int32, (n, n), 0) == lax.broadcasted_iota(jnp.int32, (n, n), 1)
    e_row = jnp.sum(jnp.where(eye, ig - b, 0.0), axis=1, keepdims=True)
    d = jnp.where(_tri(n, True) > 0, b + e_row, -jnp.inf)
    inter = b + m_st
    m_t = jnp.maximum(inter, jnp.max(d, axis=2, keepdims=True))
    qs, kh, vh = _heads(q * (DK ** -0.5)), _heads(k), _heads(v)
    s = jnp.stack([mm_nt(qs[h], kh[h]) for h in range(HEADS)]) * jnp.exp(d - m_t)
    w_inter = jnp.exp(inter - m_t)
    num = (jnp.stack([mm_nn(s[h], vh[h]) for h in range(HEADS)])
           + w_inter * jnp.stack([mm_nn(qs[h], c_st[h]) for h in range(HEADS)]))
    den = jnp.sum(s, axis=2, keepdims=True) + w_inter * jnp.sum(jnp.stack(qs) * n_st, axis=2, keepdims=True)
    h_out = num / jnp.maximum(jnp.abs(den), jnp.exp(-m_t))
    a = g - b + ig
    m_new = jnp.maximum(g + m_st, jnp.max(a, axis=1, keepdims=True))
    decay = jnp.exp(g + m_st - m_new)
    wk = jnp.stack(kh) * jnp.exp(a - m_new)
    c_new = decay * c_st + jnp.stack([mm_tn(wk[h], vh[h]) for h in range(HEADS)])
    n_new = decay * n_st + jnp.sum(wk, axis=1, keepdims=True)
    hc = h_out - jnp.mean(h_out, axis=-1, keepdims=True)
    yn = hc * lax.rsqrt(jnp.mean(hc * hc, axis=-1, keepdims=True) + LN_EPS)
    y = _sigmoid(og) * (jnp.concatenate([yn[h] for h in range(HEADS)], axis=1) * nw)
    return c_new, n_new, m_new, y


def _mixer_inputs(proj_ref, lg_ref, hnw_ref, mnw_ref, qk):
    hg_in = (proj_ref[:, _grp(0)], proj_ref[:, _grp(1)], proj_ref[:, _grp(2)], proj_ref[:, _grp(3)],
             lg_ref[0:1, :], lg_ref[1:2, :], hnw_ref[...])
    ml_in = (qk[:, :D_GRP], qk[:, D_GRP:], proj_ref[:, _grp(6)], proj_ref[:, pl.ds(8 * D_GRP, LANES)],
             proj_ref[:, _grp(7)], mnw_ref[...])
    return hg_in, ml_in


def _mixer_fwd(proj, lb_logits, hg_nw, conv_w, conv_b, ml_nw):
    seq = proj.shape[0]
    n_chunks = seq // CHUNK
    proj_spec, halo_spec, small, state_specs, y_spec, _ = _mixer_specs(n_chunks, False)

    def body(proj_ref, halo_ref, lg_ref, hnw_ref, cw_ref, cb_ref, mnw_ref,
             y_ref, hst_ref, cst_ref, nst_ref, mst_ref, hs, cs, ns, ms):
        c = pl.program_id(0)

        @pl.when(c == 0)
        def _():
            hs[...] = jnp.zeros_like(hs)
            cs[...] = jnp.zeros_like(cs)
            ns[...] = jnp.zeros_like(ns)
            ms[...] = jnp.full(ms.shape, NEG_BIG, F32)

        hst_ref[0] = hs[...]
        cst_ref[0] = cs[...]
        nst_ref[0] = ns[...]
        mst_ref[0] = ms[...]
        halo = jnp.where(c > 0, halo_ref[...], 0.0)
        qk = _qk_conv(halo, proj_ref[:, pl.ds(4 * D_GRP, 2 * D_GRP)],
                      cw_ref[0:1, :], cw_ref[1:2, :], cw_ref[2:3, :], cw_ref[3:4, :], cb_ref[...])
        hg_in, ml_in = _mixer_inputs(proj_ref, lg_ref, hnw_ref, mnw_ref, qk)
        hs[...], y_hg = _hg_chunk(hs[...], *hg_in)
        cs[...], ns[...], m_new, y_ml = _ml_chunk(cs[...], ns[...], _last(ms[...], 0), *ml_in)
        ms[...] = jnp.broadcast_to(m_new, ms.shape)
        y_ref[:, pl.ds(0, D_GRP)] = y_hg.astype(BF16)
        y_ref[:, pl.ds(D_GRP, D_GRP)] = y_ml.astype(BF16)

    st = jax.ShapeDtypeStruct((n_chunks, HEADS, DK, DK), F32)
    vec = jax.ShapeDtypeStruct((n_chunks, HEADS, 1, DK), F32)
    vmem = 2 * (_nbytes((CHUNK, D_IN_PAD), F32) + _nbytes((CHUNK, 2 * D_GRP), F32) + 2 * _nbytes((HEADS, DK, DK), F32)) \
        + 2 * _nbytes((HEADS, DK, DK), F32)
    return _pcall(
        body, name="mixer_fwd", grid=(n_chunks,),
        in_specs=[proj_spec, halo_spec] + small,
        out_specs=[y_spec] + state_specs,
        out_shape=[jax.ShapeDtypeStruct((seq, 2 * D_GRP), BF16), st, st, vec, vec],
        scratch_shapes=[pltpu.VMEM((HEADS, DK, DK), F32), pltpu.VMEM((HEADS, DK, DK), F32),
                        pltpu.VMEM((HEADS, 1, DK), F32), pltpu.VMEM((HEADS, 1, DK), F32)],
        compiler_params=_params(("arbitrary",), vmem),
    )(proj, proj, lb_logits, hg_nw, conv_w, conv_b, ml_nw)


def _mixer_bwd(proj, dy, hst, cst, nst, mst, lb_logits, hg_nw, conv_w, conv_b, ml_nw):
    seq = proj.shape[0]
    n_chunks = seq // CHUNK
    proj_spec, halo_spec, small, state_specs, y_spec, _ = _mixer_specs(n_chunks, True)

    def body(proj_ref, halo_ref, dy_ref, hst_ref, cst_ref, nst_ref, mst_ref,
             lg_ref, hnw_ref, cw_ref, cb_ref, mnw_ref,
             dproj_ref, dbin_ref, dlg_ref, dhnw_ref, dcw_ref, dcb_ref, dmnw_ref,
             dhs, dcs, dns, dms, dhalo):
        c = pl.program_id(0)

        @pl.when(c == 0)
        def _():
            for r in (dhs, dcs, dns, dms, dhalo, dbin_ref, dlg_ref, dhnw_ref, dcw_ref, dcb_ref, dmnw_ref):
                r[...] = jnp.zeros_like(r)

        def put(cols, val):
            dproj_ref[:, cols] = val.astype(BF16)
            dbin_ref[:, cols] += jnp.sum(val, axis=0, keepdims=True)

        first = c == n_chunks - 1
        halo = jnp.where(first, 0.0, halo_ref[...])
        x_qk = proj_ref[:, pl.ds(4 * D_GRP, 2 * D_GRP)]
        conv_args = (halo, x_qk, cw_ref[0:1, :], cw_ref[1:2, :], cw_ref[2:3, :], cw_ref[3:4, :], cb_ref[...])
        qk, conv_vjp = jax.vjp(_qk_conv, *conv_args)
        hg_in, ml_in = _mixer_inputs(proj_ref, lg_ref, hnw_ref, mnw_ref, qk)
        _, hg_vjp = jax.vjp(_hg_chunk, hst_ref[0], *hg_in)
        _, ml_vjp = jax.vjp(_ml_chunk, cst_ref[0], nst_ref[0], _last(mst_ref[0], 0), *ml_in)
        dst, dhq, dhf, dhi, dhg, dl0, dl1, dnw = hg_vjp((dhs[...], dy_ref[:, pl.ds(0, D_GRP)]))
        dc, dn, dm, dq, dk, dv, dgates, dog, dmn = ml_vjp(
            (dcs[...], dns[...], _last(dms[...], 0), dy_ref[:, pl.ds(D_GRP, D_GRP)]))
        dhs[...] = dst
        dcs[...] = dc
        dns[...] = dn
        dms[...] = jnp.broadcast_to(dm, dms.shape)
        for i, val in ((0, dhq), (1, dhf), (2, dhi), (3, dhg), (6, dv), (7, dog)):
            put(_grp(i), val)
        put(pl.ds(8 * D_GRP, LANES), dgates)
        dlg_ref[0:1, :] += dl0
        dlg_ref[1:2, :] += dl1
        dhnw_ref[...] += dnw
        dmnw_ref[...] += dmn
        dh, dx, dw0, dw1, dw2, dw3, db = conv_vjp(jnp.concatenate([dq, dk], axis=1))
        tail = jnp.concatenate([jnp.zeros((CHUNK - SUBLANES, 2 * D_GRP), F32), dhalo[...]], axis=0)
        put(pl.ds(4 * D_GRP, 2 * D_GRP), dx + tail)
        dhalo[...] = dh
        for d, dw in enumerate((dw0, dw1, dw2, dw3)):
            dcw_ref[d:d + 1, :] += dw
        dcb_ref[...] += db

    row = pl.BlockSpec((1, D_GRP), lambda c: (0, 0))
    small_out = [pl.BlockSpec((1, D_IN_PAD), lambda c: (0, 0)), pl.BlockSpec((2, D_GRP), lambda c: (0, 0)), row,
                 pl.BlockSpec((ML_CONV, 2 * D_GRP), lambda c: (0, 0)), pl.BlockSpec((1, 2 * D_GRP), lambda c: (0, 0)), row]
    dy_spec = pl.BlockSpec((CHUNK, 2 * D_GRP), y_spec.index_map)
    vmem = 2 * (2 * _nbytes((CHUNK, D_IN_PAD), F32) + _nbytes((CHUNK, 2 * D_GRP), F32)
                + 2 * _nbytes((HEADS, DK, DK), F32)) + 2 * _nbytes((HEADS, DK, DK), F32) + 4 * 1024 * 1024
    return _pcall(
        body, name="mixer_bwd", grid=(n_chunks,),
        in_specs=[proj_spec, halo_spec, dy_spec] + state_specs + small,
        out_specs=[proj_spec] + small_out,
        out_shape=[jax.ShapeDtypeStruct((seq, D_IN_PAD), BF16), jax.ShapeDtypeStruct((1, D_IN_PAD), F32),
                   jax.ShapeDtypeStruct((2, D_GRP), F32), jax.ShapeDtypeStruct((1, D_GRP), F32),
                   jax.ShapeDtypeStruct((ML_CONV, 2 * D_GRP), F32), jax.ShapeDtypeStruct((1, 2 * D_GRP), F32),
                   jax.ShapeDtypeStruct((1, D_GRP), F32)],
        scratch_shapes=[pltpu.VMEM((HEADS, DK, DK), F32), pltpu.VMEM((HEADS, DK, DK), F32),
                        pltpu.VMEM((HEADS, 1, DK), F32), pltpu.VMEM((HEADS, 1, DK), F32),
                        pltpu.VMEM((SUBLANES, 2 * D_GRP), F32)],
        compiler_params=_params(("arbitrary",), vmem),
    )(proj, proj, dy, hst, cst, nst, mst, lb_logits, hg_nw, conv_w, conv_b, ml_nw)


def _tile(n, prefs, unit=None):
    unit = unit or n
    for p in prefs:
        if unit % p == 0 and n % p == 0:
            return p
    return unit


def _logical(arr):
    return arr.shape if arr.ndim == 2 else (arr.shape[1], arr.shape[0] * arr.shape[2])


def _group(arr):
    return arr.shape[-1]


def _split_spec(ndim, group, tr, tc, where):
    if ndim == 2:
        return pl.BlockSpec((tr, tc), where)
    per = group // tc
    assert per * tc == group, (group, tc)

    def index(*ids):
        bi, bj = where(*ids)
        return (bj // per, bi, bj % per)
    return pl.BlockSpec((None, tr, tc), index)


def _mm(name, mode, a, b, *, bias=None, res=None, res_scale=1.0, ln=None, out_dtype=F32, out_groups=None,
        copy_dtype=None, tm=None, tn=None, tk=None):
    la, lb = _logical(a), _logical(b)
    if mode == "nn":
        (m, k), n = la, lb[1]
        n_unit = _group(b) if b.ndim == 3 else n
        kc = _group(a) if a.ndim == 3 else k
    elif mode == "nt":
        (m, k), n = la, lb[0]
        n_unit = n
        kc = min(_group(a) if a.ndim == 3 else k, _group(b) if b.ndim == 3 else k)
    else:
        (k, m), n = la, lb[1]
        n_unit, kc = (_group(b) if b.ndim == 3 else n), k
        assert a.ndim == 2
    if out_groups:
        n_unit = min(n_unit, n // out_groups)
    kind = ln[0] if ln else None
    tm = tm or (256 if ln else _tile(m, (512, 256, 128)))
    tn = n if ln else (tn or _tile(n, (512, 384, 256, 128), n_unit))
    tk = (tk or _tile(k, (2048, 512, 256, 128))) if mode == "tn" else k
    gi, gj, gk = m // tm, n // tn, k // tk
    assert gi * tm == m and gj * tn == n and gk * tk == k and n_unit % tn == 0, (name, m, n, k, tm, tn, tk)
    ca, cb = {"nn": (1, 0), "nt": (1, 1), "tn": (0, 0)}[mode]
    i_outer = gk > 1 or (gi - 1) * _nbytes(b.shape, b.dtype) <= (gj - 1) * _nbytes(a.shape, a.dtype)

    def ij(where):
        return (lambda p, q, kk: where(p, q, kk)) if i_outer else (lambda p, q, kk: where(q, p, kk))
    if mode == "tn":
        a_spec = pl.BlockSpec((tk, tm), ij(lambda i, j, kk: (kk, i)))
    elif a.ndim == 3:
        a_spec = pl.BlockSpec((a.shape[0], tm, _group(a)), ij(lambda i, j, kk: (0, i, 0)))
    else:
        a_spec = pl.BlockSpec((tm, k), ij(lambda i, j, kk: (i, 0)))
    if mode != "nt":
        b_spec = _split_spec(b.ndim, _group(b), tk, tn, ij(lambda i, j, kk: (kk, j)))
    elif b.ndim == 3:
        b_spec = pl.BlockSpec((b.shape[0], tn, _group(b)), ij(lambda i, j, kk: (0, j, 0)))
    else:
        b_spec = pl.BlockSpec((tn, k), ij(lambda i, j, kk: (j, 0)))
    row_spec = pl.BlockSpec((1, tn), ij(lambda i, j, kk: (0, j)))
    blk_spec = pl.BlockSpec((tm, tn), ij(lambda i, j, kk: (i, j)))
    ins, in_specs = [a, b], [a_spec, b_spec]
    if bias is not None:
        ins.append(bias), in_specs.append(row_spec)
    if res is not None:
        ins.append(res), in_specs.append(blk_spec)
    if kind == "fwd":
        ins += [ln[1], ln[2]]
        in_specs += [row_spec, row_spec]
    elif kind == "loss":
        ins += [ln[1], ln[2], ln[3]]
        in_specs += [row_spec, row_spec, blk_spec]
    elif kind == "bwd":
        ins += [ln[1], ln[2], ln[3]]
        in_specs += [blk_spec, row_spec, row_spec]
    if out_groups:
        blk_out = jax.ShapeDtypeStruct((out_groups, m, n // out_groups), out_dtype)
        out_spec = _split_spec(3, n // out_groups, tm, tn, ij(lambda i, j, kk: (i, j)))
    else:
        blk_out, out_spec = jax.ShapeDtypeStruct((m, n), out_dtype), blk_spec
    row_out = jax.ShapeDtypeStruct((1, n), F32)
    if kind is None:
        out_shape, out_specs = [blk_out], [out_spec]
    elif kind == "fwd":
        out_shape, out_specs = [blk_out, blk_out], [blk_spec, blk_spec]
    else:
        out_shape, out_specs = [blk_out, row_out, row_out], [blk_spec, row_spec, row_spec]
        if kind == "loss":
            out_shape.append(jax.ShapeDtypeStruct((1, LANES), F32))
            out_specs.append(pl.BlockSpec((1, LANES), lambda p, q, kk: (0, 0)))
    if copy_dtype is not None:
        out_shape.append(jax.ShapeDtypeStruct((m, n), copy_dtype))
        out_specs.append(blk_spec)
    n_in = len(ins)

    def body(*refs):
        in_refs, out_refs, acc_ref = refs[:n_in], refs[n_in:n_in + len(out_shape)], refs[-1]
        i, kk = pl.program_id(0 if i_outer else 1), pl.program_id(2)
        a_ref, b_ref = in_refs[:2]
        extra = list(in_refs[2:])

        def epilogue(acc):
            rest = list(extra)
            if bias is not None:
                acc = acc + rest.pop(0)[...]
            if res is not None:
                acc = acc + res_scale * rest.pop(0)[...]
            if kind is None:
                out_refs[0][...] = acc.astype(out_dtype)
                return
            if kind == "fwd":
                out_refs[0][...] = acc
                y = _layer_norm(acc, rest[0][...], rest[1][...])
                out_refs[1][...] = y
                if copy_dtype is not None:
                    out_refs[-1][...] = y.astype(copy_dtype)
                return
            if kind == "loss":
                y, vjp = jax.vjp(_layer_norm, acc, rest[0][...], rest[1][...])
                err = y - rest[2][...]
                part = 0.5 * jnp.sum(jnp.sum(err * err, axis=1, keepdims=True), axis=0, keepdims=True) / n
                dz, dg, db = vjp(err / n)
            else:
                _, vjp = jax.vjp(_layer_norm, rest[0][...], rest[1][...], rest[2][...])
                dz, dg, db = vjp(acc)

            @pl.when(i == 0)
            def _():
                for r in out_refs[1:3 + (kind == "loss")]:
                    r[...] = jnp.zeros_like(r)

            out_refs[0][...] = dz
            out_refs[1][...] += dg
            out_refs[2][...] += db
            if kind == "loss":
                out_refs[3][...] += jnp.broadcast_to(part, (1, LANES))
            if copy_dtype is not None:
                out_refs[-1][...] = dz.astype(copy_dtype)

        def chunk(ref, c0, last):
            if ref.ndim == 3:
                g = ref.shape[2]
                return ref[c0 // g, :, pl.ds(c0 % g, kc)]
            return ref[:, pl.ds(c0, kc)] if last else ref[pl.ds(c0, kc), :]

        if mode == "tn" or kc == k:
            prod = _dg(a_ref[...], b_ref[...], ca, cb)
        else:
            prod = None
            for c0 in range(0, k, kc):
                part = _dg(chunk(a_ref, c0, True), chunk(b_ref, c0, mode == "nt"), ca, cb)
                prod = part if prod is None else prod + part
        if gk == 1:
            epilogue(prod)
            return

        @pl.when(kk == 0)
        def _():
            acc_ref[...] = prod

        @pl.when(kk > 0)
        def _():
            acc_ref[...] += prod

        @pl.when(kk == gk - 1)
        def _():
            epilogue(acc_ref[...])

    vmem = (2 * (_nbytes((tm, tk), a.dtype) + _nbytes((tk, tn), b.dtype))
            + (2 * len(ins) + 2 * len(out_shape) + 1) * _nbytes((tm, tn), F32))
    outs = _pcall(
        body, name=name, grid=(gi, gj, gk) if i_outer else (gj, gi, gk), in_specs=in_specs, out_specs=out_specs,
        out_shape=out_shape, scratch_shapes=[pltpu.VMEM((tm, tn) if gk > 1 else (SUBLANES, LANES), F32)],
        compiler_params=_params(("arbitrary", "arbitrary", "arbitrary"), vmem),
    )(*ins)
    return outs[0] if (kind is None and copy_dtype is None) else outs


def _colsum(name, a):
    m, n = a.shape
    tm = _tile(m, (512, 256, 128))

    def body(a_ref, o_ref):
        @pl.when(pl.program_id(0) == 0)
        def _():
            o_ref[...] = jnp.zeros_like(o_ref)

        o_ref[...] += jnp.sum(a_ref[...].astype(F32), axis=0, keepdims=True)

    return _pcall(
        body, name=name, grid=(m // tm,), in_specs=[pl.BlockSpec((tm, n), lambda i: (i, 0))],
        out_specs=pl.BlockSpec((1, n), lambda i: (0, 0)), out_shape=jax.ShapeDtypeStruct((1, n), F32),
        compiler_params=_params(("arbitrary",), 2 * _nbytes((tm, n), a.dtype)),
    )(a)


def _attn_head(q, k, v):
    sc = mm_nt(q, k) * (CA_DH ** -0.5)
    e = jnp.exp(sc - jnp.max(sc, axis=-1, keepdims=True))
    return mm_nn(e / jnp.sum(e, axis=-1, keepdims=True), v)


def _attn_fwd(q, kv):
    seq, n_mem = q.shape[0], kv.shape[0]
    tq = _tile(seq, (512, 256, 128))

    def body(q_ref, kv_ref, o_ref):
        for h in range(HEADS):
            hd = pl.ds(h * CA_DH, CA_DH)
            o = _attn_head(q_ref[:, hd], kv_ref[:, hd], kv_ref[:, pl.ds(D_MODEL + h * CA_DH, CA_DH)])
            o_ref[:, hd] = o.astype(BF16)

    return _pcall(
        body, name="attn_fwd", grid=(seq // tq,),
        in_specs=[pl.BlockSpec((tq, D_MODEL), lambda i: (i, 0)), pl.BlockSpec((n_mem, 2 * D_MODEL), lambda i: (0, 0))],
        out_specs=pl.BlockSpec((tq, D_MODEL), lambda i: (i, 0)), out_shape=jax.ShapeDtypeStruct((seq, D_MODEL), BF16),
        compiler_params=_params(("arbitrary",), 4 * _nbytes((tq, D_MODEL), F32) + 2 * _nbytes((n_mem, 2 * D_MODEL), F32)),
    )(q, kv)


def _attn_bwd(q, kv, do):
    seq, n_mem = q.shape[0], kv.shape[0]
    tq = _tile(seq, (512, 256, 128))

    def body(q_ref, kv_ref, do_ref, dq_ref, dkv_ref):
        @pl.when(pl.program_id(0) == 0)
        def _():
            dkv_ref[...] = jnp.zeros_like(dkv_ref)

        for h in range(HEADS):
            hd = pl.ds(h * CA_DH, CA_DH)
            vd = pl.ds(D_MODEL + h * CA_DH, CA_DH)
            _, vjp = jax.vjp(_attn_head, q_ref[:, hd], kv_ref[:, hd], kv_ref[:, vd])
            dq, dk, dv = vjp(do_ref[:, hd].astype(F32))
            dq_ref[:, hd] = dq.astype(BF16)
            dkv_ref[:, hd] += dk
            dkv_ref[:, vd] += dv

    return _pcall(
        body, name="attn_bwd", grid=(seq // tq,),
        in_specs=[pl.BlockSpec((tq, D_MODEL), lambda i: (i, 0)), pl.BlockSpec((n_mem, 2 * D_MODEL), lambda i: (0, 0)),
                  pl.BlockSpec((tq, D_MODEL), lambda i: (i, 0))],
        out_specs=[pl.BlockSpec((tq, D_MODEL), lambda i: (i, 0)), pl.BlockSpec((n_mem, 2 * D_MODEL), lambda i: (0, 0))],
        out_shape=[jax.ShapeDtypeStruct((seq, D_MODEL), BF16), jax.ShapeDtypeStruct((n_mem, 2 * D_MODEL), F32)],
        compiler_params=_params(("arbitrary",), 6 * _nbytes((tq, D_MODEL), F32) + 4 * _nbytes((n_mem, 2 * D_MODEL), F32)),
    )(q, kv, do)


FFN_TB = 512
FFN_TC = 256


def _ffn_mid(hg, xg, hv, xv, wg0, wg1, wg2, bg, wv0, wv1, wv2, bv):
    return jax.nn.gelu(causal_conv(hg, xg, (wg0, wg1, wg2), bg)) * causal_conv(hv, xv, (wv0, wv1, wv2), bv)


def _ffn_specs(seq, reverse):
    tb = min(FFN_TB, seq)
    nt = seq // tb
    row8 = tb // SUBLANES

    def tt(t):
        return nt - 1 - t if reverse else t
    nj = D_FF // FFN_TC
    main = pl.BlockSpec((tb, FFN_TC), lambda j, t: (tt(t), j))
    ins = []
    for off in (0, nj):
        ins += [pl.BlockSpec((tb, FFN_TC), lambda j, t, off=off: (tt(t), j + off)),
                pl.BlockSpec((SUBLANES, FFN_TC), lambda j, t, off=off: (jnp.maximum(tt(t) * row8 - 1, 0), j + off))]
    for off in (0, nj):
        ins += [pl.BlockSpec((FFN_CONV, FFN_TC), lambda j, t, off=off: (0, j + off)),
                pl.BlockSpec((1, FFN_TC), lambda j, t, off=off: (0, j + off))]
    return tb, nt, main, ins


def _ffn_args(c_first, ug, hg, uv, hv, wg, bg, wv, bv):
    halo_g = jnp.where(c_first, 0.0, hg[...])
    halo_v = jnp.where(c_first, 0.0, hv[...])
    return (halo_g, ug[...], halo_v, uv[...], wg[0:1, :], wg[1:2, :], wg[2:3, :], bg[...],
            wv[0:1, :], wv[1:2, :], wv[2:3, :], bv[...])


def _ffn_mid_fwd(u, conv_w, conv_b):
    seq = u.shape[0]
    tb, nt, main, ins = _ffn_specs(seq, False)

    def body(ug, hg, uv, hv, wg, bg, wv, bv, o_ref):
        o_ref[...] = _ffn_mid(*_ffn_args(pl.program_id(1) == 0, ug, hg, uv, hv, wg, bg, wv, bv)).astype(BF16)

    return _pcall(
        body, name="ffn_mid_fwd", grid=(D_FF // FFN_TC, nt), in_specs=ins, out_specs=main,
        out_shape=jax.ShapeDtypeStruct((seq, D_FF), BF16),
        compiler_params=_params(("arbitrary", "arbitrary"), 12 * _nbytes((tb, FFN_TC), F32)),
    )(u, u, u, u, conv_w, conv_b, conv_w, conv_b)


def _ffn_mid_bwd(u, conv_w, conv_b, dh):
    seq = u.shape[0]
    tb, nt, main, ins = _ffn_specs(seq, True)

    def body(ug, hg, uv, hv, wg, bg, wv, bv, dh_ref, du, dw, db, carry):
        t = pl.program_id(1)

        @pl.when(t == 0)
        def _():
            for r in (dw, db, carry):
                r[...] = jnp.zeros_like(r)

        _, vjp = jax.vjp(_ffn_mid, *_ffn_args(t == nt - 1, ug, hg, uv, hv, wg, bg, wv, bv))
        dhg, dxg, dhv, dxv, g0, g1, g2, gb, v0, v1, v2, vb = vjp(dh_ref[...])
        zeros = jnp.zeros((tb - SUBLANES, FFN_TC), F32)
        du[0] = (dxg + jnp.concatenate([zeros, carry[0]], axis=0)).astype(BF16)
        du[1] = (dxv + jnp.concatenate([zeros, carry[1]], axis=0)).astype(BF16)
        carry[0] = dhg
        carry[1] = dhv
        for half, parts in enumerate(((g0, g1, g2), (v0, v1, v2))):
            for d, p in enumerate(parts):
                dw[half, d:d + 1, :] += p
        db[0] += gb
        db[1] += vb

    def grouped(rows, index):
        return pl.BlockSpec((2, rows, FFN_TC), index)
    return _pcall(
        body, name="ffn_mid_bwd", grid=(D_FF // FFN_TC, nt), in_specs=ins + [main],
        out_specs=[grouped(tb, lambda j, t: (0, nt - 1 - t, j)), grouped(FFN_CONV, lambda j, t: (0, 0, j)),
                   grouped(1, lambda j, t: (0, 0, j))],
        out_shape=[jax.ShapeDtypeStruct((2, seq, D_FF), BF16), jax.ShapeDtypeStruct((2, FFN_CONV, D_FF), F32),
                   jax.ShapeDtypeStruct((2, 1, D_FF), F32)],
        scratch_shapes=[pltpu.VMEM((2, SUBLANES, FFN_TC), F32)],
        compiler_params=_params(("arbitrary", "arbitrary"), 24 * _nbytes((tb, FFN_TC), F32)),
    )(u, u, u, u, conv_w, conv_b, conv_w, conv_b, dh)


def _adamw_math(w, g, m, v):
    m_new = ADAM_B1 * m + (1.0 - ADAM_B1) * g
    v_new = ADAM_B2 * v + (1.0 - ADAM_B2) * jnp.square(g)
    m_hat = m_new / (1.0 - ADAM_B1 ** ADAM_STEP)
    v_hat = v_new / (1.0 - ADAM_B2 ** ADAM_STEP)
    return -ADAM_LR * (m_hat / (jnp.sqrt(v_hat) + ADAM_EPS) + ADAM_WD * w), m_new, v_new


def _adamw(name, w, g, m, v):
    rows, cols = w.shape
    tr = _tile(rows, (256, 176, 128, 64, 40, 32, 16, 8))

    def body(w_ref, g_ref, m_ref, v_ref, d_ref, nm_ref, nv_ref):
        d_ref[...], nm_ref[...], nv_ref[...] = _adamw_math(w_ref[...], g_ref[...], m_ref[...], v_ref[...])

    spec = pl.BlockSpec((tr, cols), lambda i: (i, 0))
    sh = jax.ShapeDtypeStruct((rows, cols), F32)
    return _pcall(
        body, name=name, grid=(rows // tr,), in_specs=[spec] * 4, out_specs=[spec] * 3, out_shape=[sh] * 3,
        compiler_params=_params(("arbitrary",), 14 * _nbytes((tr, -(-cols // LANES) * LANES), F32)),
    )(w, g, m, v)


def _adamw_halves(name, core, w, mine, theirs, m, v):
    rows, cols = w.shape
    tr = _tile(rows // 2, (256, 176, 128))
    nbh = rows // 2 // tr

    def body(c_ref, w_ref, a_ref, b_ref, m_ref, v_ref, g_ref, d_ref, nm_ref, nv_ref):
        g = jnp.where(pl.program_id(0) // nbh == c_ref[0], a_ref[...], b_ref[...])
        g_ref[...] = g
        d_ref[...], nm_ref[...], nv_ref[...] = _adamw_math(w_ref[...], g, m_ref[...], v_ref[...])

    spec = pl.BlockSpec((tr, cols), lambda i, c_ref: (i, 0))
    half = pl.BlockSpec((tr, cols), lambda i, c_ref: (i % nbh, 0))
    sh = jax.ShapeDtypeStruct((rows, cols), F32)
    grid_spec = pltpu.PrefetchScalarGridSpec(
        num_scalar_prefetch=1, grid=(rows // tr,), in_specs=[spec, half, half, spec, spec], out_specs=[spec] * 4)
    return _pcall(
        body, name=name, grid_spec=grid_spec, out_shape=[sh] * 4,
        compiler_params=_params(("arbitrary",), 18 * _nbytes((tr, -(-cols // LANES) * LANES), F32)),
    )(core, w, mine, theirs, m, v)


MESH = pl.DeviceIdType.MESH
ANY = pl.BlockSpec(memory_space=pl.ANY)
N_CHIPS = 4
N_DEV = 8
BF16_ROWS = 16


def _me():
    return lax.axis_index("x"), lax.axis_index("y"), lax.axis_index("c")


def _other_chips(x, y):
    return [(1 - x, y), (x, 1 - y), (1 - x, 1 - y)]


def _remote(src, dst, ssem, rsem, dev):
    return pltpu.make_async_remote_copy(src_ref=src, dst_ref=dst, send_sem=ssem, recv_sem=rsem,
                                        device_id=dev, device_id_type=MESH)


def _half_rows(ref_rows, cc):
    half = ref_rows // 2
    return pl.ds(pl.multiple_of(cc * half, BF16_ROWS), half)


def _gather_weights(shards):
    n = len(shards)
    n_ici = n * (N_CHIPS - 1)

    def body(*refs):
        ins, outs, (ssem, rsem, lsem, lrsem) = refs[:n], refs[n:2 * n], refs[2 * n:]
        x, y, c = _me()
        k_me = 2 * x + y
        sib = (x, y, 1 - c)
        chips = _other_chips(x, y)
        started = []
        for i, (w_ref, o_ref) in enumerate(zip(ins, outs)):
            cp = _remote(w_ref, o_ref.at[k_me], lsem.at[i], lrsem.at[i], sib)
            cp.start()
            started.append(cp)
        for r, (px, py) in enumerate(chips):
            for i, (w_ref, o_ref) in enumerate(zip(ins, outs)):
                rows = _half_rows(w_ref.shape[0], c)
                s = r * n + i
                cp = _remote(w_ref.at[rows], o_ref.at[k_me, rows], ssem.at[s], rsem.at[s], (px, py, c))
                cp.start()
                started.append(cp)
        for r, (px, py) in enumerate(chips):
            for i, o_ref in enumerate(outs):
                blk = o_ref.at[2 * px + py, _half_rows(o_ref.shape[1], c)]
                s = r * n + i
                _remote(blk, blk, ssem.at[s], rsem.at[s], (px, py, c)).wait_recv()
                cp = _remote(blk, blk, ssem.at[n_ici + s], rsem.at[n_ici + s], sib)
                cp.start()
                started.append(cp)
        for r, (px, py) in enumerate(chips):
            for i, o_ref in enumerate(outs):
                blk = o_ref.at[2 * px + py, _half_rows(o_ref.shape[1], 1 - c)]
                s = n_ici + r * n + i
                _remote(blk, blk, ssem.at[s], rsem.at[s], sib).wait_recv()
        for cp in started[n:]:
            cp.wait_send()
        for cp in started[:n]:
            cp.wait()

    return _pcall(
        body, name="gather_weights", in_specs=[ANY] * n, out_specs=[ANY] * n,
        out_shape=[jax.ShapeDtypeStruct((N_CHIPS,) + s.shape, s.dtype) for s in shards],
        scratch_shapes=[pltpu.SemaphoreType.DMA((2 * n_ici,)), pltpu.SemaphoreType.DMA((2 * n_ici,)),
                        pltpu.SemaphoreType.DMA((n,)), pltpu.SemaphoreType.DMA((n,))],
    )(*shards)


def _swap_halves(name, grads):
    n = len(grads)

    def body(*refs):
        ins, outs, (ssem, rsem) = refs[:n], refs[n:2 * n], refs[2 * n:]
        x, y, c = _me()
        copies = []
        for i, (g_ref, o_ref) in enumerate(zip(ins, outs)):
            for k in range(N_CHIPS):
                s = i * N_CHIPS + k
                cp = _remote(g_ref.at[k, _half_rows(g_ref.shape[1], 1 - c)], o_ref.at[k], ssem.at[s], rsem.at[s],
                             (x, y, 1 - c))
                cp.start()
                copies.append(cp)
        for cp in copies:
            cp.wait()

    return _pcall(
        body, name=name, in_specs=[ANY] * n, out_specs=[ANY] * n,
        out_shape=[jax.ShapeDtypeStruct((N_CHIPS, g.shape[1] // 2, g.shape[2]), g.dtype) for g in grads],
        scratch_shapes=[pltpu.SemaphoreType.DMA((n * N_CHIPS,)), pltpu.SemaphoreType.DMA((n * N_CHIPS,))],
    )(*grads)


SEM = pl.BlockSpec(memory_space=pltpu.SEMAPHORE)
IN_HBM = pl.BlockSpec(memory_space=pltpu.HBM)
SPLIT_PARAMS = dict(compiler_params=pltpu.CompilerParams(has_side_effects=pltpu.SideEffectType.DATAFLOW_SIDE_EFFECTING))


def _scatter_start(name, parts):
    n = len(parts)
    n_sem = n * (N_CHIPS - 1)

    def body(*refs):
        ins, lands, (ssem, rsem), token = refs[:n], refs[n:2 * n], refs[2 * n:2 * n + 2], refs[-1]
        x, y, c = _me()
        k_me = 2 * x + y
        for r, (px, py) in enumerate(_other_chips(x, y)):
            for i, (p_ref, l_ref) in enumerate(zip(ins, lands)):
                s = r * n + i
                _remote(p_ref.at[2 * px + py], l_ref.at[k_me], ssem.at[s], rsem.at[s], (px, py, c)).start()
        token[...] = jnp.zeros_like(token)

    hbm = [pltpu.HBM(p.shape, p.dtype) for p in parts]
    outs = _call(
        body, name=name, in_specs=[IN_HBM] * (2 * n),
        out_specs=[SEM, SEM] + [IN_HBM] * (2 * n) + [pl.BlockSpec(memory_space=pltpu.VMEM)],
        out_shape=[pltpu.SemaphoreType.DMA((n_sem,)), pltpu.SemaphoreType.DMA((n_sem,))] + hbm + hbm
        + [jax.ShapeDtypeStruct((SUBLANES, LANES), F32)],
        input_output_aliases={i: 2 + i for i in range(2 * n)}, **SPLIT_PARAMS,
    )(*[pltpu.with_memory_space_constraint(p, pltpu.HBM) for p in parts],
      *[pltpu.with_memory_space_constraint(lax.empty(p.shape, p.dtype), pltpu.HBM) for p in parts])
    return outs[:-1], outs[-1]


def _scatter_wait(name, handle, after):
    ssem, rsem, thru = handle[0], handle[1], handle[2:]
    n = len(thru) // 2

    def body(*refs):
        ins, lands, (ssem_ref, rsem_ref) = refs[:n], refs[n:2 * n], refs[2 * n:2 * n + 2]
        x, y, c = _me()
        for r, (px, py) in enumerate(_other_chips(x, y)):
            for i, (p_ref, l_ref) in enumerate(zip(ins, lands)):
                s = r * n + i
                cp = _remote(p_ref.at[2 * px + py], l_ref.at[2 * px + py], ssem_ref.at[s], rsem_ref.at[s], (px, py, c))
                cp.wait_send()
                cp.wait_recv()

    outs = _call(
        body, name=name, in_specs=[IN_HBM] * (2 * n) + [SEM, SEM, ANY], out_specs=[IN_HBM] * (2 * n),
        out_shape=[pltpu.HBM(t.shape, t.dtype) for t in thru],
        input_output_aliases={i: i for i in range(2 * n)}, **SPLIT_PARAMS,
    )(*thru, ssem, rsem, after)
    return outs[n:]


def _share_halves(halves):
    n = len(halves)

    def body(*refs):
        ins, outs, (ssem, rsem) = refs[:n], refs[n:2 * n], refs[2 * n:]
        x, y, c = _me()
        copies = [_remote(r_ref, o_ref, ssem.at[i], rsem.at[i], (x, y, 1 - c))
                  for i, (r_ref, o_ref) in enumerate(zip(ins, outs))]
        for cp in copies:
            cp.start()
        for cp in copies:
            cp.wait()

    return _pcall(
        body, name="share_halves", in_specs=[ANY] * n, out_specs=[ANY] * n,
        out_shape=[jax.ShapeDtypeStruct(h.shape, h.dtype) for h in halves],
        scratch_shapes=[pltpu.SemaphoreType.DMA((n,)), pltpu.SemaphoreType.DMA((n,))],
    )(*halves)


def _exchange_small(v, reduce):
    rows = v.shape[0]

    def body(v_ref, out_ref, buf, ssem, rsem):
        x, y, c = _me()
        me = 4 * x + 2 * y + c
        peers = [((x + bx) % 2, (y + by) % 2, (c + bc) % 2)
                 for bx in (0, 1) for by in (0, 1) for bc in (0, 1) if (bx, by, bc) != (0, 0, 0)]
        dst = buf if reduce else out_ref
        dst[me] = v_ref[...]
        sends = [_remote(v_ref, dst.at[me], ssem.at[r], rsem.at[r], p) for r, p in enumerate(peers)]
        for cp in sends:
            cp.start()
        for r, (px, py, pc) in enumerate(peers):
            blk = dst.at[4 * px + 2 * py + pc]
            _remote(blk, blk, ssem.at[r], rsem.at[r], (px, py, pc)).wait_recv()
        if reduce:
            acc = buf[0]
            for d in range(1, N_DEV):
                acc = acc + buf[d]
            out_ref[...] = acc
        for cp in sends:
            cp.wait_send()

    vm = pl.BlockSpec(memory_space=pltpu.VMEM)
    out_shape = jax.ShapeDtypeStruct((rows, LANES) if reduce else (N_DEV, rows, LANES), F32)
    buf_shape = (N_DEV, rows, LANES) if reduce else (SUBLANES, LANES)
    return _pcall(
        body, pin=False, name="reduce_small" if reduce else "gather_small", in_specs=[vm], out_specs=vm, out_shape=out_shape,
        scratch_shapes=[pltpu.VMEM(buf_shape, F32), pltpu.SemaphoreType.DMA((N_DEV - 1,)),
                        pltpu.SemaphoreType.DMA((N_DEV - 1,))],
        compiler_params=pltpu.CompilerParams(vmem_limit_bytes=32 * 1024 * 1024),
    )(v)


def _add_pair(name, core, g, theirs):
    _, half, cols = theirs.shape
    tr = _tile(half, (256, 176, 128))
    nb = half // tr

    def body(c_ref, g_ref, t_ref, o32_ref, o16_ref):
        s = g_ref[...] + t_ref[...]
        o32_ref[...] = s
        o16_ref[...] = s.astype(BF16)

    spec = pl.BlockSpec((None, tr, cols), lambda k, i, c_ref: (k, i, 0))
    grid_spec = pltpu.PrefetchScalarGridSpec(
        num_scalar_prefetch=1, grid=(N_CHIPS, nb),
        in_specs=[pl.BlockSpec((None, tr, cols), lambda k, i, c_ref: (k, c_ref[0] * nb + i, 0)), spec],
        out_specs=[spec, spec])
    return _pcall(
        body, name=name, grid_spec=grid_spec,
        out_shape=[jax.ShapeDtypeStruct(theirs.shape, F32), jax.ShapeDtypeStruct(theirs.shape, BF16)],
        compiler_params=_params(("arbitrary", "arbitrary"), 8 * _nbytes((tr, cols + LANES), F32)),
    )(core, g, theirs)


def _add_chips(name, chip, p32, recv):
    _, half, cols = p32.shape
    tr = _tile(half, (256, 176, 128))

    def body(k_ref, p_ref, r0_ref, r1_ref, r2_ref, o_ref):
        o_ref[...] = ((p_ref[...] + r0_ref[...].astype(F32)) + r1_ref[...].astype(F32)) + r2_ref[...].astype(F32)

    def other(r):
        return pl.BlockSpec((None, tr, cols), lambda i, k_ref: (r + (k_ref[0] <= r).astype(jnp.int32), i, 0))
    grid_spec = pltpu.PrefetchScalarGridSpec(
        num_scalar_prefetch=1, grid=(half // tr,),
        in_specs=[pl.BlockSpec((None, tr, cols), lambda i, k_ref: (k_ref[0], i, 0)), other(0), other(1), other(2)],
        out_specs=pl.BlockSpec((tr, cols), lambda i, k_ref: (i, 0)))
    return _pcall(
        body, name=name, grid_spec=grid_spec, out_shape=jax.ShapeDtypeStruct((half, cols), F32),
        compiler_params=_params(("arbitrary",), 10 * _nbytes((tr, cols + LANES), F32)),
    )(chip, p32, recv, recv, recv)


def kernel(x, mem, w_in, b_in, hg_lb_logits, hg_norm_w, ml_conv_w, ml_conv_b, ml_norm_w, w_out, ln1_g, ln1_b, ca_wq, ca_wkv, ca_wo, ln2_g, ln2_b, ffn_w_up, ffn_conv_w, ffn_conv_b, ffn_w_down, ln3_g, ln3_b, loss_target, m_w_in, m_b_in, m_hg_lb_logits, m_hg_norm_w, m_ml_conv_w, m_ml_conv_b, m_ml_norm_w, m_w_out, m_ln1_g, m_ln1_b, m_ca_wq, m_ca_wkv, m_ca_wo, m_ln2_g, m_ln2_b, m_ffn_w_up, m_ffn_conv_w, m_ffn_conv_b, m_ffn_w_down, m_ln3_g, m_ln3_b, v_w_in, v_b_in, v_hg_lb_logits, v_hg_norm_w, v_ml_conv_w, v_ml_conv_b, v_ml_norm_w, v_w_out, v_ln1_g, v_ln1_b, v_ca_wq, v_ca_wkv, v_ca_wo, v_ln2_g, v_ln2_b, v_ffn_w_up, v_ffn_conv_w, v_ffn_conv_b, v_ffn_w_down, v_ln3_g, v_ln3_b):
    return _train_step(dict(locals()))


WEIGHTS = ("w_in", "b_in", "hg_lb_logits", "hg_norm_w", "ml_conv_w", "ml_conv_b", "ml_norm_w", "w_out", "ln1_g",
           "ln1_b", "ca_wq", "ca_wkv", "ca_wo", "ln2_g", "ln2_b", "ffn_w_up", "ffn_conv_w", "ffn_conv_b",
           "ffn_w_down", "ln3_g", "ln3_b")
MATRICES = ("w_in", "w_out", "ca_wq", "ca_wkv", "ca_wo", "ffn_w_up", "ffn_w_down")
COL_SHARDED = ("w_in", "ca_wkv", "ffn_w_up", "ml_conv_w", "ffn_conv_w")
SMALL = tuple(n for n in WEIGHTS if n not in MATRICES)
PART_ROWS = 16


def _part_rows(shape, lead):
    n = 1
    for s in shape[lead:]:
        n *= s
    return -(-n // (LANES * PART_ROWS)) * PART_ROWS


def _pack(arrs, dtype, lead=0, rows=None):
    parts = []
    for a in arrs:
        head = a.shape[:lead]
        flat = a.reshape(head + (-1,)).astype(dtype)
        pad = _part_rows(a.shape, lead) * LANES - flat.shape[-1]
        flat = jnp.pad(flat, [(0, 0)] * lead + [(0, pad)])
        parts.append(flat.reshape(head + (-1, LANES)))
    used = sum(p.shape[lead] for p in parts)
    if rows is not None and rows > used:
        parts.append(jnp.zeros(parts[0].shape[:lead] + (rows - used, LANES), dtype))
    return jnp.concatenate(parts, axis=lead)


def _unpack(buf, shapes):
    lead = buf.shape[:-2]
    outs, r = [], 0
    for sh in shapes:
        n = 1
        for s in sh:
            n *= s
        nr = _part_rows(sh, 0)
        flat = buf[..., r:r + nr, :].reshape(lead + (nr * LANES,))
        outs.append(flat[..., :n].reshape(lead + tuple(sh)))
        r += nr
    return outs


def _cat_cols(s):
    return jnp.moveaxis(s, 0, 1).reshape(s.shape[1], -1)


def _split_cols(g):
    return jnp.moveaxis(g.reshape(g.shape[0], N_CHIPS, -1), 1, 0)


def _stack_rows(s):
    return s.reshape(-1, s.shape[-1])


def _train_step(a):
    xs, mems, tgt = a["x"][0], a["mem"][0], a["loss_target"][0]
    core = lax.axis_index("c").astype(jnp.int32).reshape(1)
    chip = (2 * lax.axis_index("x") + lax.axis_index("y")).astype(jnp.int32).reshape(1)
    k_me = chip[0]
    shard = {n: a[n][0] for n in MATRICES}

    w = dict(zip(MATRICES, _gather_weights([shard[n].astype(BF16) for n in MATRICES])))
    for n in ("w_out", "ca_wq", "ca_wo", "ffn_w_down"):
        w[n] = _stack_rows(w[n])
    w["w_in"] = jnp.pad(_cat_cols(w["w_in"]), ((0, 0), (0, D_IN_PAD - D_IN)))
    taps = _exchange_small(_pack([a["ml_conv_w"][0], a["ffn_conv_w"][0]], F32), reduce=False)
    taps = taps.reshape((N_CHIPS, 2) + taps.shape[1:])[:, 0]
    ml_cw, ffn_cw = [_cat_cols(s) for s in _unpack(taps, [a["ml_conv_w"].shape[1:], a["ffn_conv_w"].shape[1:]])]
    b_in_p = jnp.pad(a["b_in"], ((0, 0), (0, D_IN_PAD - D_IN)))
    mixer_w = (a["hg_lb_logits"], a["hg_norm_w"], ml_cw, a["ml_conv_b"], a["ml_norm_w"])
    up_cols = a["ffn_w_up"].shape[-1]

    xb = xs.astype(BF16)
    proj = _mm("proj", "nn", xb, w["w_in"], bias=b_in_p, tm=256, tn=D_IN_PAD)
    y, hst, cst, nst, mst = _mixer_fwd(proj, *mixer_w)
    z1, x1, x1b = _mm("mix_out", "nn", y, w["w_out"], res=xs, res_scale=ALPHA, ln=("fwd", a["ln1_g"], a["ln1_b"]),
                      copy_dtype=BF16)
    q = _mm("ca_q", "nn", x1b, w["ca_wq"], out_dtype=BF16, tn=D_MODEL)
    kv = _mm("ca_kv", "nn", mems, w["ca_wkv"])
    o = _attn_fwd(q, kv)
    z2, x2, x2b = _mm("ca_out", "nn", o, w["ca_wo"], res=x1, res_scale=ALPHA, ln=("fwd", a["ln2_g"], a["ln2_b"]),
                      copy_dtype=BF16)
    u = _mm("ffn_up", "nn", x2b, w["ffn_w_up"], tn=up_cols)
    hmid = _ffn_mid_fwd(u, ffn_cw, a["ffn_conv_b"])
    dz3, g_ln3g, g_ln3b, loss_part, dz3b = _mm("ffn_down", "nn", hmid, w["ffn_w_down"], res=x2, res_scale=ALPHA,
                                               ln=("loss", a["ln3_g"], a["ln3_b"], tgt), copy_dtype=BF16)

    grads = {"ln3_g": g_ln3g, "ln3_b": g_ln3b}
    dhmid = _mm("d_hmid", "nt", dz3b, w["ffn_w_down"], tn=D_FF)
    grads["ffn_w_down"] = _mm("g_w_down", "tn", hmid, dz3b, tm=D_FF // 2, tn=D_MODEL)
    du, g_cw, g_cb = _ffn_mid_bwd(u, ffn_cw, a["ffn_conv_b"], dhmid)
    grads["ffn_conv_w"] = jnp.moveaxis(g_cw, 0, 1).reshape(FFN_CONV, 2 * D_FF)
    grads["ffn_conv_b"] = g_cb.reshape(1, 2 * D_FF)
    grads["ffn_w_up"] = _mm("g_w_up", "tn", x2b, du, out_groups=N_CHIPS, tm=D_MODEL, tn=up_cols)
    grads["ffn_w_down"] = grads["ffn_w_down"].reshape((N_CHIPS,) + shard["ffn_w_down"].shape)
    pending = {}

    def reduce_start(tag, names):
        group = [grads[n] for n in names]
        sums = [_add_pair("add_pair_" + n, core, g, t)
                for n, g, t in zip(names, group, _swap_halves("swap_halves_" + tag, group))]
        handle, token = _scatter_start("scatter_start_" + tag, [s16 for _, s16 in sums])
        pending[tag] = (names, [s32 for s32, _ in sums], handle)
        return token[0:1, 0:1]

    zero = reduce_start("ffn", ("ffn_w_up", "ffn_w_down"))
    dz2, grads["ln2_g"], grads["ln2_b"], dz2b = _mm("d_x2", "nt", du, w["ffn_w_up"], res=dz3, res_scale=ALPHA,
                                                    ln=("bwd", z2, a["ln2_g"] + zero, a["ln2_b"]), copy_dtype=BF16)
    do = _mm("d_o", "nt", dz2b, w["ca_wo"], out_dtype=BF16, tn=D_MODEL)
    grads["ca_wo"] = _mm("g_wo", "tn", o, dz2b, tm=D_MODEL, tn=D_MODEL)
    dq, dkv = _attn_bwd(q, kv, do)
    grads["ca_wq"] = _mm("g_wq", "tn", x1b, dq, tm=D_MODEL, tn=D_MODEL)
    grads["ca_wkv"] = _mm("g_wkv", "tn", mems, dkv, out_groups=N_CHIPS, tm=D_MODEL)
    dz1, grads["ln1_g"], grads["ln1_b"], dz1b = _mm("d_x1", "nt", dq, w["ca_wq"], res=dz2, res_scale=ALPHA,
                                                    ln=("bwd", z1, a["ln1_g"], a["ln1_b"]), copy_dtype=BF16)
    dy = _mm("d_y", "nt", dz1b, w["w_out"], tn=D_MODEL)
    grads["w_out"] = _mm("g_w_out", "tn", y, dz1b, tm=D_MODEL, tn=D_MODEL)
    for n in ("w_out", "ca_wq", "ca_wo"):
        grads[n] = grads[n].reshape((N_CHIPS,) + shard[n].shape)
    zero = reduce_start("attn", ("w_out", "ca_wq", "ca_wkv", "ca_wo"))
    (dproj, g_b_in, grads["hg_lb_logits"], grads["hg_norm_w"], grads["ml_conv_w"], grads["ml_conv_b"],
     grads["ml_norm_w"]) = _mixer_bwd(proj, dy, hst, cst, nst, mst, mixer_w[0], mixer_w[1] + zero, *mixer_w[2:])
    grads["w_in"] = _split_cols(_mm("g_w_in", "tn", xb, dproj, tm=D_MODEL, tn=up_cols)[:, :D_IN])
    grads["b_in"] = g_b_in[:, :D_IN]
    zero = reduce_start("in", ("w_in",))
    dx = _mm("d_x", "nt", dproj, w["w_in"], bias=jnp.zeros((1, D_MODEL), F32) + zero, res=dz1, res_scale=ALPHA,
             tm=256, tn=D_MODEL)

    halves = {}
    for tag, (names, sums32, handle) in pending.items():
        for n, s32, r in zip(names, sums32, _scatter_wait("scatter_wait_" + tag, handle, dx)):
            halves[n] = _add_chips("add_chips_" + n, chip, s32, r)
    halves = [halves[n] for n in MATRICES]
    other_halves = _share_halves(halves)

    small_shapes = [grads[n].shape for n in SMALL] + [loss_part.shape]
    summed = _unpack(_exchange_small(_pack([grads[n] for n in SMALL] + [loss_part], F32), reduce=True), small_shapes)
    loss = summed[-1][0, 0]
    for n, g in zip(SMALL, summed[:-1]):
        if n in COL_SHARDED:
            cols = a[n].shape[-1]
            g = lax.dynamic_slice_in_dim(g, k_me * cols, cols, axis=1)
        grads[n] = g

    delta, new_m, new_v = {}, {}, {}
    for n, mine, theirs in zip(MATRICES, halves, other_halves):
        grads[n], delta[n], new_m[n], new_v[n] = _adamw_halves(
            "adamw_" + n, core, shard[n], mine, theirs, a["m_" + n][0], a["v_" + n][0])
    small_w = [a[n][0] if a[n].ndim == 3 else a[n] for n in SMALL]
    small_m = [a["m_" + n][0] if a[n].ndim == 3 else a["m_" + n] for n in SMALL]
    small_v = [a["v_" + n][0] if a[n].ndim == 3 else a["v_" + n] for n in SMALL]
    shapes = [w.shape for w in small_w]
    packed = [_pack(l, F32) for l in (small_w, [grads[n] for n in SMALL], small_m, small_v)]
    for out, buf in zip((delta, new_m, new_v), _adamw("adamw_small", *packed)):
        for n, v in zip(SMALL, _unpack(buf, shapes)):
            out[n] = v

    def shaped(d):
        return [d[n].reshape(a[n].shape) for n in WEIGHTS]
    return (loss, dx[None], *shaped(grads), *shaped(delta), *shaped(new_m), *shaped(new_v))
```

```python
import functools

import jax
import jax.numpy as jnp
from jax import lax
from jax.experimental import pallas as pl
from jax.experimental.pallas import tpu as pltpu

F32 = jnp.float32
BF16 = jnp.bfloat16

D_MODEL = 1024
HEADS = 4
DK = 128
D_GRP = HEADS * DK
CHUNK = 64
ML_CONV = 4
FFN_CONV = 3
D_FF = 2816
CA_DH = D_MODEL // HEADS
DEPTH = 1
ALPHA = (2.0 * DEPTH) ** 0.25
LN_EPS = 1e-5
NEG_BIG = -1e30
D_IN = 8 * D_GRP + 2 * HEADS
D_IN_PAD = 8 * D_GRP + 128
ADAM_LR, ADAM_B1, ADAM_B2, ADAM_EPS, ADAM_WD, ADAM_STEP = 0.001, 0.9, 0.999, 1e-08, 0.01, 10

SUBLANES = 8
LANES = 128
VMEM_BYTES = 64 * 1024 * 1024


def _pcall(body, pin=True, **kw):
    if not pin:
        return _call(body, **kw)
    kw["out_shape"] = jax.tree.map(lambda s: pltpu.HBM(s.shape, s.dtype), kw["out_shape"])
    call = _call(body, **kw)

    def pinned(*args):
        return call(*[pltpu.with_memory_space_constraint(x, pltpu.HBM) if jnp.issubdtype(x.dtype, jnp.floating) else x
                      for x in args])
    return pinned


def _call(body, **kw):
    return pl.pallas_call(body, **kw)


def _params(semantics, vmem_bytes):
    limit = int(min(max(2 * vmem_bytes, 16 * 1024 * 1024), VMEM_BYTES - 8 * 1024 * 1024))
    return pltpu.CompilerParams(dimension_semantics=semantics, vmem_limit_bytes=limit)


def _nbytes(shape, dtype):
    n = 1
    for s in shape:
        n *= s
    return n * jnp.dtype(dtype).itemsize


def _dg(a, b, ca, cb):
    return lax.dot_general(a.astype(BF16), b.astype(BF16), (((ca,), (cb,)), ((), ())),
                           preferred_element_type=F32)


@jax.custom_vjp
def mm_nn(a, b):
    return _dg(a, b, 1, 0)


mm_nn.defvjp(lambda a, b: (_dg(a, b, 1, 0), (a, b)),
             lambda r, g: (_dg(g, r[1], 1, 1).astype(r[0].dtype), _dg(r[0], g, 0, 0).astype(r[1].dtype)))


@jax.custom_vjp
def mm_nt(a, b):
    return _dg(a, b, 1, 1)


mm_nt.defvjp(lambda a, b: (_dg(a, b, 1, 1), (a, b)),
             lambda r, g: (_dg(g, r[1], 1, 0).astype(r[0].dtype), _dg(g, r[0], 0, 0).astype(r[1].dtype)))


@jax.custom_vjp
def mm_tn(a, b):
    return _dg(a, b, 0, 0)


mm_tn.defvjp(lambda a, b: (_dg(a, b, 0, 0), (a, b)),
             lambda r, g: (_dg(r[1], g, 1, 1).astype(r[0].dtype), _dg(r[0], g, 1, 0).astype(r[1].dtype)))


def _hdot(a, b):
    return jnp.dot(a, b, precision=lax.Precision.HIGHEST, preferred_element_type=F32)


def _tri(n, lower):
    r = lax.broadcasted_iota(jnp.int32, (n, n), 0)
    c = lax.broadcasted_iota(jnp.int32, (n, n), 1)
    return ((r >= c) if lower else (r <= c)).astype(F32)


@jax.custom_vjp
def cumsum_rows(x):
    return _hdot(_tri(x.shape[0], True), x)


cumsum_rows.defvjp(lambda x: (_hdot(_tri(x.shape[0], True), x), None),
                   lambda _, g: (_hdot(_tri(g.shape[0], False), g),))


def _shift_impl(halo, x, d):
    xx = jnp.concatenate([halo, x], axis=0)
    return pltpu.roll(xx, d, 0)[SUBLANES:]


@functools.partial(jax.custom_vjp, nondiff_argnums=(2,))
def shift_rows(halo, x, d):
    return _shift_impl(halo, x, d)


def _shift_bwd(d, _, g):
    n = g.shape[0] + SUBLANES
    gg = jnp.concatenate([jnp.zeros((SUBLANES, g.shape[1]), g.dtype), g], axis=0)
    r = pltpu.roll(gg, n - d, 0)
    return r[:SUBLANES], r[SUBLANES:]


shift_rows.defvjp(lambda halo, x, d: (_shift_impl(halo, x, d), None), _shift_bwd)


def causal_conv(halo, x, w_rows, b):
    k = len(w_rows)
    y = b + w_rows[k - 1] * x
    for d in range(1, k):
        y = y + w_rows[k - 1 - d] * shift_rows(halo, x, d)
    return y


def _sigmoid(x):
    return 1.0 / (1.0 + jnp.exp(-x))


def _silu(x):
    return x * _sigmoid(x)


def _log_sigmoid(x):
    return jnp.minimum(x, 0.0) - jnp.log(1.0 + jnp.exp(-jnp.abs(x)))


def _pick_lane(x, j):
    lane = lax.broadcasted_iota(jnp.int32, (1, x.shape[1]), 1)
    return jnp.sum(jnp.where(lane == j, x, 0.0), axis=1, keepdims=True)


def _pick_row(x, i):
    row = lax.broadcasted_iota(jnp.int32, (x.shape[0], 1), 0)
    return jnp.sum(jnp.where(row == i, x, 0.0), axis=0, keepdims=True)


def _col_to_row(e):
    n = e.shape[0]
    eye = lax.broadcasted_iota(jnp.int32, (n, n), 0) == lax.broadcasted_iota(jnp.int32, (n, n), 1)
    return jnp.sum(jnp.where(eye, e, 0.0), axis=0, keepdims=True)


def _layer_norm(z, g, b):
    mu = jnp.mean(z, axis=-1, keepdims=True)
    zc = z - mu
    var = jnp.mean(zc * zc, axis=-1, keepdims=True)
    return zc * lax.rsqrt(var + LN_EPS) * g + b


def _hg_head(st_t, hq, hf, hi, hgate, l0, l1, nw):
    n = hq.shape[0]
    lb = _sigmoid(l0 - l1)
    q = _silu(hq)
    lf = jnp.log(lb + (1.0 - lb) * _sigmoid(hf))
    k = (1.0 - lb) * _sigmoid(-hf)
    b = cumsum_rows(lf)
    b_ref = _pick_row(b, n // 2 - 1)
    b_last = _pick_row(b, n - 1)
    attn = mm_nt(q * jnp.exp(b - b_ref), k * jnp.exp(b_ref - b))
    attn = jnp.where(_tri(n, True) > 0, attn, 0.0)
    o = mm_nn(attn, hi) + mm_nt(q * jnp.exp(b), st_t)
    st_new = jnp.exp(b_last) * st_t + mm_tn(hi, k * jnp.exp(b_last - b))
    y = o * lax.rsqrt(jnp.mean(o * o, axis=-1, keepdims=True) + LN_EPS) * nw * _silu(hgate)
    return st_new, y


def _ml_head(c_st, n_st, m_st, q, k, v, gates, og, nw, h):
    n = q.shape[0]
    ig = _pick_lane(gates, h)
    fl = _log_sigmoid(_pick_lane(gates, HEADS + h))
    qs = q * (DK ** -0.5)
    b = _pick_lane(cumsum_rows(jnp.broadcast_to(fl, (n, LANES))), 0)
    g = jnp.sum(fl, axis=0, keepdims=True)
    d = jnp.where(_tri(n, True) > 0, b + _col_to_row(ig - b), -jnp.inf)
    inter = b + m_st
    m_t = jnp.maximum(inter, jnp.max(d, axis=1, keepdims=True))
    s = mm_nt(qs, k) * jnp.exp(d - m_t)
    w_inter = jnp.exp(inter - m_t)
    num = mm_nn(s, v) + w_inter * mm_nn(qs, c_st)
    den = jnp.sum(s, axis=1, keepdims=True) + w_inter * jnp.sum(qs * n_st, axis=1, keepdims=True)
    h_out = num / jnp.maximum(jnp.abs(den), jnp.exp(-m_t))
    a = g - b + ig
    m_new = jnp.maximum(g + m_st, jnp.max(a, axis=0, keepdims=True))
    decay = jnp.exp(g + m_st - m_new)
    wk = k * jnp.exp(a - m_new)
    c_new = decay * c_st + mm_tn(wk, v)
    n_new = decay * n_st + jnp.sum(wk, axis=0, keepdims=True)
    mu = jnp.mean(h_out, axis=-1, keepdims=True)
    hc = h_out - mu
    var = jnp.mean(hc * hc, axis=-1, keepdims=True)
    y = _sigmoid(og) * (hc * lax.rsqrt(var + LN_EPS) * nw)
    return c_new, n_new, m_new, y


def _qk_conv(halo, x, w0, w1, w2, w3, b):
    return _silu(causal_conv(halo, x, (w0, w1, w2, w3), b))


def _grp(i, h=None):
    if h is None:
        return pl.ds(i * D_GRP, D_GRP)
    return pl.ds(i * D_GRP + h * DK, DK)


def _mixer_specs(n_chunks, reverse):
    def chunk(c):
        return n_chunks - 1 - c if reverse else c
    row8 = CHUNK // SUBLANES
    proj_spec = pl.BlockSpec((CHUNK, D_IN_PAD), lambda c: (chunk(c), 0))
    halo_spec = pl.BlockSpec((SUBLANES, 2 * D_GRP), lambda c: (jnp.maximum(chunk(c) * row8 - 1, 0), 2))
    small = [pl.BlockSpec((2, D_GRP), lambda c: (0, 0)), pl.BlockSpec((1, D_GRP), lambda c: (0, 0)),
             pl.BlockSpec((ML_CONV, 2 * D_GRP), lambda c: (0, 0)), pl.BlockSpec((1, 2 * D_GRP), lambda c: (0, 0)),
             pl.BlockSpec((1, D_GRP), lambda c: (0, 0))]
    state_specs = [pl.BlockSpec((1, HEADS, DK, DK), lambda c: (chunk(c), 0, 0, 0)),
                   pl.BlockSpec((1, HEADS, DK, DK), lambda c: (chunk(c), 0, 0, 0)),
                   pl.BlockSpec((1, HEADS, 1, DK), lambda c: (chunk(c), 0, 0, 0)),
                   pl.BlockSpec((1, HEADS, 1, DK), lambda c: (chunk(c), 0, 0, 0))]
    y_spec = pl.BlockSpec((CHUNK, 2 * D_GRP), lambda c: (chunk(c), 0))
    return proj_spec, halo_spec, small, state_specs, y_spec, chunk


def _mixer_fwd(proj, lb_logits, hg_nw, conv_w, conv_b, ml_nw):
    seq = proj.shape[0]
    n_chunks = seq // CHUNK
    proj_spec, halo_spec, small, state_specs, y_spec, _ = _mixer_specs(n_chunks, False)

    def body(proj_ref, halo_ref, lg_ref, hnw_ref, cw_ref, cb_ref, mnw_ref,
             y_ref, hst_ref, cst_ref, nst_ref, mst_ref, hs, cs, ns, ms):
        c = pl.program_id(0)

        @pl.when(c == 0)
        def _():
            hs[...] = jnp.zeros_like(hs)
            cs[...] = jnp.zeros_like(cs)
            ns[...] = jnp.zeros_like(ns)
            ms[...] = jnp.full(ms.shape, NEG_BIG, F32)

        hst_ref[0] = hs[...]
        cst_ref[0] = cs[...]
        nst_ref[0] = ns[...]
        mst_ref[0] = ms[...]
        halo = jnp.where(c > 0, halo_ref[...], 0.0)
        qk = _qk_conv(halo, proj_ref[:, pl.ds(4 * D_GRP, 2 * D_GRP)],
                      cw_ref[0:1, :], cw_ref[1:2, :], cw_ref[2:3, :], cw_ref[3:4, :], cb_ref[...])
        gates = proj_ref[:, pl.ds(8 * D_GRP, LANES)]
        for h in range(HEADS):
            hd = pl.ds(h * DK, DK)
            st_new, y = _hg_head(hs[h], proj_ref[:, _grp(0, h)], proj_ref[:, _grp(1, h)], proj_ref[:, _grp(2, h)],
                                 proj_ref[:, _grp(3, h)], lg_ref[0:1, hd], lg_ref[1:2, hd], hnw_ref[:, hd])
            hs[h] = st_new
            y_ref[:, hd] = y
            c_new, n_new, m_new, y = _ml_head(
                cs[h], ns[h], _pick_lane(ms[h], 0), qk[:, h * DK:(h + 1) * DK],
                qk[:, D_GRP + h * DK:D_GRP + (h + 1) * DK], proj_ref[:, _grp(6, h)], gates,
                proj_ref[:, _grp(7, h)], mnw_ref[:, hd], h)
            cs[h] = c_new
            ns[h] = n_new
            ms[h] = jnp.broadcast_to(m_new, (1, DK))
            y_ref[:, pl.ds(D_GRP + h * DK, DK)] = y

    st = jax.ShapeDtypeStruct((n_chunks, HEADS, DK, DK), F32)
    vec = jax.ShapeDtypeStruct((n_chunks, HEADS, 1, DK), F32)
    vmem = 2 * (_nbytes((CHUNK, D_IN_PAD), F32) + _nbytes((CHUNK, 2 * D_GRP), F32) + 2 * _nbytes((HEADS, DK, DK), F32)) \
        + 2 * _nbytes((HEADS, DK, DK), F32)
    return _pcall(
        body, name="mixer_fwd", grid=(n_chunks,),
        in_specs=[proj_spec, halo_spec] + small,
        out_specs=[y_spec] + state_specs,
        out_shape=[jax.ShapeDtypeStruct((seq, 2 * D_GRP), F32), st, st, vec, vec],
        scratch_shapes=[pltpu.VMEM((HEADS, DK, DK), F32), pltpu.VMEM((HEADS, DK, DK), F32),
                        pltpu.VMEM((HEADS, 1, DK), F32), pltpu.VMEM((HEADS, 1, DK), F32)],
        compiler_params=_params(("arbitrary",), vmem),
    )(proj, proj, lb_logits, hg_nw, conv_w, conv_b, ml_nw)


def _mixer_bwd(proj, dy, hst, cst, nst, mst, lb_logits, hg_nw, conv_w, conv_b, ml_nw):
    seq = proj.shape[0]
    n_chunks = seq // CHUNK
    proj_spec, halo_spec, small, state_specs, y_spec, _ = _mixer_specs(n_chunks, True)

    def body(proj_ref, halo_ref, dy_ref, hst_ref, cst_ref, nst_ref, mst_ref,
             lg_ref, hnw_ref, cw_ref, cb_ref, mnw_ref,
             dproj_ref, dlg_ref, dhnw_ref, dcw_ref, dcb_ref, dmnw_ref,
             dhs, dcs, dns, dms, dhalo, dqk):
        c = pl.program_id(0)

        @pl.when(c == 0)
        def _():
            for r in (dhs, dcs, dns, dms, dhalo, dlg_ref, dhnw_ref, dcw_ref, dcb_ref, dmnw_ref):
                r[...] = jnp.zeros_like(r)

        first = c == n_chunks - 1
        halo = jnp.where(first, 0.0, halo_ref[...])
        x_qk = proj_ref[:, pl.ds(4 * D_GRP, 2 * D_GRP)]
        conv_args = (halo, x_qk, cw_ref[0:1, :], cw_ref[1:2, :], cw_ref[2:3, :], cw_ref[3:4, :], cb_ref[...])
        qk, conv_vjp = jax.vjp(_qk_conv, *conv_args)
        gates = proj_ref[:, pl.ds(8 * D_GRP, LANES)]
        dgates = jnp.zeros((CHUNK, LANES), F32)
        for h in range(HEADS):
            hd = pl.ds(h * DK, DK)
            args = (hst_ref[0, h], proj_ref[:, _grp(0, h)], proj_ref[:, _grp(1, h)], proj_ref[:, _grp(2, h)],
                    proj_ref[:, _grp(3, h)], lg_ref[0:1, hd], lg_ref[1:2, hd], hnw_ref[:, hd])
            _, vjp = jax.vjp(_hg_head, *args)
            dst, dhq, dhf, dhi, dhg, dl0, dl1, dnw = vjp((dhs[h], dy_ref[:, hd]))
            dhs[h] = dst
            dproj_ref[:, _grp(0, h)] = dhq
            dproj_ref[:, _grp(1, h)] = dhf
            dproj_ref[:, _grp(2, h)] = dhi
            dproj_ref[:, _grp(3, h)] = dhg
            dlg_ref[0:1, hd] += dl0
            dlg_ref[1:2, hd] += dl1
            dhnw_ref[:, hd] += dnw

            margs = (cst_ref[0, h], nst_ref[0, h], _pick_lane(mst_ref[0, h], 0), qk[:, h * DK:(h + 1) * DK],
                     qk[:, D_GRP + h * DK:D_GRP + (h + 1) * DK], proj_ref[:, _grp(6, h)], gates,
                     proj_ref[:, _grp(7, h)], mnw_ref[:, hd])
            _, mvjp = jax.vjp(functools.partial(_ml_head, h=h), *margs)
            dc, dn, dm, dq, dk, dv, dg, dog, dmn = mvjp(
                (dcs[h], dns[h], _pick_lane(dms[h], 0), dy_ref[:, pl.ds(D_GRP + h * DK, DK)]))
            dcs[h] = dc
            dns[h] = dn
            dms[h] = jnp.broadcast_to(dm, (1, DK))
            dqk[:, hd] = dq
            dqk[:, pl.ds(D_GRP + h * DK, DK)] = dk
            dproj_ref[:, _grp(6, h)] = dv
            dproj_ref[:, _grp(7, h)] = dog
            dmnw_ref[:, hd] += dmn
            dgates = dgates + dg
        dproj_ref[:, pl.ds(8 * D_GRP, LANES)] = dgates
        dh, dx, dw0, dw1, dw2, dw3, db = conv_vjp(dqk[...])
        tail = jnp.concatenate([jnp.zeros((CHUNK - SUBLANES, 2 * D_GRP), F32), dhalo[...]], axis=0)
        dproj_ref[:, pl.ds(4 * D_GRP, 2 * D_GRP)] = dx + tail
        dhalo[...] = dh
        dcw_ref[0:1, :] += dw0
        dcw_ref[1:2, :] += dw1
        dcw_ref[2:3, :] += dw2
        dcw_ref[3:4, :] += dw3
        dcb_ref[...] += db

    small_out = [pl.BlockSpec((2, D_GRP), lambda c: (0, 0)), pl.BlockSpec((1, D_GRP), lambda c: (0, 0)),
                 pl.BlockSpec((ML_CONV, 2 * D_GRP), lambda c: (0, 0)), pl.BlockSpec((1, 2 * D_GRP), lambda c: (0, 0)),
                 pl.BlockSpec((1, D_GRP), lambda c: (0, 0))]
    vmem = 2 * (2 * _nbytes((CHUNK, D_IN_PAD), F32) + _nbytes((CHUNK, 2 * D_GRP), F32)
                + 2 * _nbytes((HEADS, DK, DK), F32)) + 2 * _nbytes((HEADS, DK, DK), F32) + 4 * 1024 * 1024
    return _pcall(
        body, name="mixer_bwd", grid=(n_chunks,),
        in_specs=[proj_spec, halo_spec, y_spec] + state_specs + small,
        out_specs=[proj_spec] + small_out,
        out_shape=[jax.ShapeDtypeStruct((seq, D_IN_PAD), F32), jax.ShapeDtypeStruct((2, D_GRP), F32),
                   jax.ShapeDtypeStruct((1, D_GRP), F32), jax.ShapeDtypeStruct((ML_CONV, 2 * D_GRP), F32),
                   jax.ShapeDtypeStruct((1, 2 * D_GRP), F32), jax.ShapeDtypeStruct((1, D_GRP), F32)],
        scratch_shapes=[pltpu.VMEM((HEADS, DK, DK), F32), pltpu.VMEM((HEADS, DK, DK), F32),
                        pltpu.VMEM((HEADS, 1, DK), F32), pltpu.VMEM((HEADS, 1, DK), F32),
                        pltpu.VMEM((SUBLANES, 2 * D_GRP), F32), pltpu.VMEM((CHUNK, 2 * D_GRP), F32)],
        compiler_params=_params(("arbitrary",), vmem),
    )(proj, proj, dy, hst, cst, nst, mst, lb_logits, hg_nw, conv_w, conv_b, ml_nw)


def _heads(x):
    return [x[:, h * DK:(h + 1) * DK] for h in range(HEADS)]


def _last(x, j):
    lane = lax.broadcasted_iota(jnp.int32, (1, x.shape[-1]), 1)
    return jnp.sum(jnp.where(lane == j, x, 0.0), axis=-1, keepdims=True)


def _hg_chunk(st_t, hq, hf, hi, hgate, l0, l1, nw):
    n = hq.shape[0]
    lb = _sigmoid(l0 - l1)
    q = _silu(hq)
    lf = jnp.log(lb + (1.0 - lb) * _sigmoid(hf))
    k = (1.0 - lb) * _sigmoid(-hf)
    b = cumsum_rows(lf)
    b_ref = _pick_row(b, n // 2 - 1)
    b_last = _pick_row(b, n - 1)
    qa, ka = _heads(q * jnp.exp(b - b_ref)), _heads(k * jnp.exp(b_ref - b))
    qe, kd, eb, v = _heads(q * jnp.exp(b)), _heads(k * jnp.exp(b_last - b)), _heads(jnp.exp(b_last)), _heads(hi)
    tri = _tri(n, True) > 0
    attn = [jnp.where(tri, mm_nt(qa[h], ka[h]), 0.0) for h in range(HEADS)]
    o = [mm_nn(attn[h], v[h]) + mm_nt(qe[h], st_t[h]) for h in range(HEADS)]
    st_new = jnp.stack([eb[h] * st_t[h] + mm_tn(v[h], kd[h]) for h in range(HEADS)])
    yn = [o[h] * lax.rsqrt(jnp.mean(o[h] * o[h], axis=-1, keepdims=True) + LN_EPS) for h in range(HEADS)]
    return st_new, jnp.concatenate(yn, axis=1) * nw * _silu(hgate)


def _ml_chunk(c_st, n_st, m_st, q, k, v, gates, og, nw):
    n = q.shape[0]
    ig = jnp.stack([_last(gates, h) for h in range(HEADS)])
    fl = _log_sigmoid(jnp.stack([_last(gates, HEADS + h) for h in range(HEADS)]))
    bw = cumsum_rows(jnp.concatenate([jnp.broadcast_to(fl[h], (n, DK)) for h in range(HEADS)], axis=1))
    b = jnp.stack([_last(x, 0) for x in _heads(bw)])
    g = jnp.sum(fl, axis=1, keepdims=True)
    eye = lax.broadcasted_iota(jnp.int32, (n, n), 0) == lax.broadcasted_iota(jnp.int32, (n, n), 1)
    e_row = jnp.sum(jnp.where(eye, ig - b, 0.0), axis=1, keepdims=True)
    d = jnp.where(_tri(n, True) > 0, b + e_row, -jnp.inf)
    inter = b + m_st
    m_t = jnp.maximum(inter, jnp.max(d, axis=2, keepdims=True))
    qs, kh, vh = _heads(q * (DK ** -0.5)), _heads(k), _heads(v)
    s = jnp.stack([mm_nt(qs[h], kh[h]) for h in range(HEADS)]) * jnp.exp(d - m_t)
    w_inter = jnp.exp(inter - m_t)
    num = (jnp.stack([mm_nn(s[h], vh[h]) for h in range(HEADS)])
           + w_inter * jnp.stack([mm_nn(qs[h], c_st[h]) for h in range(HEADS)]))
    den = jnp.sum(s, axis=2, keepdims=True) + w_inter * jnp.sum(jnp.stack(qs) * n_st, axis=2, keepdims=True)
    h_out = num / jnp.maximum(jnp.abs(den), jnp.exp(-m_t))
    a = g - b + ig
    m_new = jnp.maximum(g + m_st, jnp.max(a, axis=1, keepdims=True))
    decay = jnp.exp(g + m_st - m_new)
    wk = jnp.stack(kh) * jnp.exp(a - m_new)
    c_new = decay * c_st + jnp.stack([mm_tn(wk[h], vh[h]) for h in range(HEADS)])
    n_new = decay * n_st + jnp.sum(wk, axis=1, keepdims=True)
    hc = h_out - jnp.mean(h_out, axis=-1, keepdims=True)
    yn = hc * lax.rsqrt(jnp.mean(hc * hc, axis=-1, keepdims=True) + LN_EPS)
    y = _sigmoid(og) * (jnp.concatenate([yn[h] for h in range(HEADS)], axis=1) * nw)
    return c_new, n_new, m_new, y


def _mixer_inputs(proj_ref, lg_ref, hnw_ref, mnw_ref, qk):
    hg_in = (proj_ref[:, _grp(0)], proj_ref[:, _grp(1)], proj_ref[:, _grp(2)], proj_ref[:, _grp(3)],
             lg_ref[0:1, :], lg_ref[1:2, :], hnw_ref[...])
    ml_in = (qk[:, :D_GRP], qk[:, D_GRP:], proj_ref[:, _grp(6)], proj_ref[:, pl.ds(8 * D_GRP, LANES)],
             proj_ref[:, _grp(7)], mnw_ref[...])
    return hg_in, ml_in


def _mixer_fwd(proj, lb_logits, hg_nw, conv_w, conv_b, ml_nw):
    seq = proj.shape[0]
    n_chunks = seq // CHUNK
    proj_spec, halo_spec, small, state_specs, y_spec, _ = _mixer_specs(n_chunks, False)

    def body(proj_ref, halo_ref, lg_ref, hnw_ref, cw_ref, cb_ref, mnw_ref,
             y_ref, hst_ref, cst_ref, nst_ref, mst_ref, hs, cs, ns, ms):
        c = pl.program_id(0)

        @pl.when(c == 0)
        def _():
            hs[...] = jnp.zeros_like(hs)
            cs[...] = jnp.zeros_like(cs)
            ns[...] = jnp.zeros_like(ns)
            ms[...] = jnp.full(ms.shape, NEG_BIG, F32)

        hst_ref[0] = hs[...]
        cst_ref[0] = cs[...]
        nst_ref[0] = ns[...]
        mst_ref[0] = ms[...]
        halo = jnp.where(c > 0, halo_ref[...], 0.0)
        qk = _qk_conv(halo, proj_ref[:, pl.ds(4 * D_GRP, 2 * D_GRP)],
                      cw_ref[0:1, :], cw_ref[1:2, :], cw_ref[2:3, :], cw_ref[3:4, :], cb_ref[...])
        hg_in, ml_in = _mixer_inputs(proj_ref, lg_ref, hnw_ref, mnw_ref, qk)
        hs[...], y_hg = _hg_chunk(hs[...], *hg_in)
        cs[...], ns[...], m_new, y_ml = _ml_chunk(cs[...], ns[...], _last(ms[...], 0), *ml_in)
        ms[...] = jnp.broadcast_to(m_new, ms.shape)
        y_ref[:, pl.ds(0, D_GRP)] = y_hg.astype(BF16)
        y_ref[:, pl.ds(D_GRP, D_GRP)] = y_ml.astype(BF16)

    st = jax.ShapeDtypeStruct((n_chunks, HEADS, DK, DK), F32)
    vec = jax.ShapeDtypeStruct((n_chunks, HEADS, 1, DK), F32)
    vmem = 2 * (_nbytes((CHUNK, D_IN_PAD), F32) + _nbytes((CHUNK, 2 * D_GRP), F32) + 2 * _nbytes((HEADS, DK, DK), F32)) \
        + 2 * _nbytes((HEADS, DK, DK), F32)
    return _pcall(
        body, name="mixer_fwd", grid=(n_chunks,),
        in_specs=[proj_spec, halo_spec] + small,
        out_specs=[y_spec] + state_specs,
        out_shape=[jax.ShapeDtypeStruct((seq, 2 * D_GRP), BF16), st, st, vec, vec],
        scratch_shapes=[pltpu.VMEM((HEADS, DK, DK), F32), pltpu.VMEM((HEADS, DK, DK), F32),
                        pltpu.VMEM((HEADS, 1, DK), F32), pltpu.VMEM((HEADS, 1, DK), F32)],
        compiler_params=_params(("arbitrary",), vmem),
    )(proj, proj, lb_logits, hg_nw, conv_w, conv_b, ml_nw)


def _mixer_bwd(proj, dy, hst, cst, nst, mst, lb_logits, hg_nw, conv_w, conv_b, ml_nw):
    seq = proj.shape[0]
    n_chunks = seq // CHUNK
    proj_spec, halo_spec, small, state_specs, y_spec, _ = _mixer_specs(n_chunks, True)

    def body(proj_ref, halo_ref, dy_ref, hst_ref, cst_ref, nst_ref, mst_ref,
             lg_ref, hnw_ref, cw_ref, cb_ref, mnw_ref,
             dproj_ref, dbin_ref, dlg_ref, dhnw_ref, dcw_ref, dcb_ref, dmnw_ref,
             dhs, dcs, dns, dms, dhalo):
        c = pl.program_id(0)

        @pl.when(c == 0)
        def _():
            for r in (dhs, dcs, dns, dms, dhalo, dbin_ref, dlg_ref, dhnw_ref, dcw_ref, dcb_ref, dmnw_ref):
                r[...] = jnp.zeros_like(r)

        def put(cols, val):
            dproj_ref[:, cols] = val.astype(BF16)
            dbin_ref[:, cols] += jnp.sum(val, axis=0, keepdims=True)

        first = c == n_chunks - 1
        halo = jnp.where(first, 0.0, halo_ref[...])
        x_qk = proj_ref[:, pl.ds(4 * D_GRP, 2 * D_GRP)]
        conv_args = (halo, x_qk, cw_ref[0:1, :], cw_ref[1:2, :], cw_ref[2:3, :], cw_ref[3:4, :], cb_ref[...])
        qk, conv_vjp = jax.vjp(_qk_conv, *conv_args)
        hg_in, ml_in = _mixer_inputs(proj_ref, lg_ref, hnw_ref, mnw_ref, qk)
        _, hg_vjp = jax.vjp(_hg_chunk, hst_ref[0], *hg_in)
        _, ml_vjp = jax.vjp(_ml_chunk, cst_ref[0], nst_ref[0], _last(mst_ref[0], 0), *ml_in)
        dst, dhq, dhf, dhi, dhg, dl0, dl1, dnw = hg_vjp((dhs[...], dy_ref[:, pl.ds(0, D_GRP)]))
        dc, dn, dm, dq, dk, dv, dgates, dog, dmn = ml_vjp(
            (dcs[...], dns[...], _last(dms[...], 0), dy_ref[:, pl.ds(D_GRP, D_GRP)]))
        dhs[...] = dst
        dcs[...] = dc
        dns[...] = dn
        dms[...] = jnp.broadcast_to(dm, dms.shape)
        for i, val in ((0, dhq), (1, dhf), (2, dhi), (3, dhg), (6, dv), (7, dog)):
            put(_grp(i), val)
        put(pl.ds(8 * D_GRP, LANES), dgates)
        dlg_ref[0:1, :] += dl0
        dlg_ref[1:2, :] += dl1
        dhnw_ref[...] += dnw
        dmnw_ref[...] += dmn
        dh, dx, dw0, dw1, dw2, dw3, db = conv_vjp(jnp.concatenate([dq, dk], axis=1))
        tail = jnp.concatenate([jnp.zeros((CHUNK - SUBLANES, 2 * D_GRP), F32), dhalo[...]], axis=0)
        put(pl.ds(4 * D_GRP, 2 * D_GRP), dx + tail)
        dhalo[...] = dh
        for d, dw in enumerate((dw0, dw1, dw2, dw3)):
            dcw_ref[d:d + 1, :] += dw
        dcb_ref[...] += db

    row = pl.BlockSpec((1, D_GRP), lambda c: (0, 0))
    small_out = [pl.BlockSpec((1, D_IN_PAD), lambda c: (0, 0)), pl.BlockSpec((2, D_GRP), lambda c: (0, 0)), row,
                 pl.BlockSpec((ML_CONV, 2 * D_GRP), lambda c: (0, 0)), pl.BlockSpec((1, 2 * D_GRP), lambda c: (0, 0)), row]
    dy_spec = pl.BlockSpec((CHUNK, 2 * D_GRP), y_spec.index_map)
    vmem = 2 * (2 * _nbytes((CHUNK, D_IN_PAD), F32) + _nbytes((CHUNK, 2 * D_GRP), F32)
                + 2 * _nbytes((HEADS, DK, DK), F32)) + 2 * _nbytes((HEADS, DK, DK), F32) + 4 * 1024 * 1024
    return _pcall(
        body, name="mixer_bwd", grid=(n_chunks,),
        in_specs=[proj_spec, halo_spec, dy_spec] + state_specs + small,
        out_specs=[proj_spec] + small_out,
        out_shape=[jax.ShapeDtypeStruct((seq, D_IN_PAD), BF16), jax.ShapeDtypeStruct((1, D_IN_PAD), F32),
                   jax.ShapeDtypeStruct((2, D_GRP), F32), jax.ShapeDtypeStruct((1, D_GRP), F32),
                   jax.ShapeDtypeStruct((ML_CONV, 2 * D_GRP), F32), jax.ShapeDtypeStruct((1, 2 * D_GRP), F32),
                   jax.ShapeDtypeStruct((1, D_GRP), F32)],
        scratch_shapes=[pltpu.VMEM((HEADS, DK, DK), F32), pltpu.VMEM((HEADS, DK, DK), F32),
                        pltpu.VMEM((HEADS, 1, DK), F32), pltpu.VMEM((HEADS, 1, DK), F32),
                        pltpu.VMEM((SUBLANES, 2 * D_GRP), F32)],
        compiler_params=_params(("arbitrary",), vmem),
    )(proj, proj, dy, hst, cst, nst, mst, lb_logits, hg_nw, conv_w, conv_b, ml_nw)


def _tile(n, prefs, unit=None):
    unit = unit or n
    for p in prefs:
        if unit % p == 0 and n % p == 0:
            return p
    return unit


def _logical(arr):
    return arr.shape if arr.ndim == 2 else (arr.shape[1], arr.shape[0] * arr.shape[2])


def _group(arr):
    return arr.shape[-1]


def _split_spec(ndim, group, tr, tc, where):
    if ndim == 2:
        return pl.BlockSpec((tr, tc), where)
    per = group // tc
    assert per * tc == group, (group, tc)

    def index(*ids):
        bi, bj = where(*ids)
        return (bj // per, bi, bj % per)
    return pl.BlockSpec((None, tr, tc), index)


def _mm(name, mode, a, b, *, bias=None, res=None, res_scale=1.0, ln=None, out_dtype=F32, out_groups=None,
        copy_dtype=None, tm=None, tn=None, tk=None):
    la, lb = _logical(a), _logical(b)
    if mode == "nn":
        (m, k), n = la, lb[1]
        n_unit = _group(b) if b.ndim == 3 else n
        kc = _group(a) if a.ndim == 3 else k
    elif mode == "nt":
        (m, k), n = la, lb[0]
        n_unit = n
        kc = min(_group(a) if a.ndim == 3 else k, _group(b) if b.ndim == 3 else k)
    else:
        (k, m), n = la, lb[1]
        n_unit, kc = (_group(b) if b.ndim == 3 else n), k
        assert a.ndim == 2
    if out_groups:
        n_unit = min(n_unit, n // out_groups)
    kind = ln[0] if ln else None
    tm = tm or (256 if ln else _tile(m, (512, 256, 128)))
    tn = n if ln else (tn or _tile(n, (512, 384, 256, 128), n_unit))
    tk = (tk or _tile(k, (2048, 512, 256, 128))) if mode == "tn" else k
    gi, gj, gk = m // tm, n // tn, k // tk
    assert gi * tm == m and gj * tn == n and gk * tk == k and n_unit % tn == 0, (name, m, n, k, tm, tn, tk)
    ca, cb = {"nn": (1, 0), "nt": (1, 1), "tn": (0, 0)}[mode]
    i_outer = gk > 1 or (gi - 1) * _nbytes(b.shape, b.dtype) <= (gj - 1) * _nbytes(a.shape, a.dtype)

    def ij(where):
        return (lambda p, q, kk: where(p, q, kk)) if i_outer else (lambda p, q, kk: where(q, p, kk))
    if mode == "tn":
        a_spec = pl.BlockSpec((tk, tm), ij(lambda i, j, kk: (kk, i)))
    elif a.ndim == 3:
        a_spec = pl.BlockSpec((a.shape[0], tm, _group(a)), ij(lambda i, j, kk: (0, i, 0)))
    else:
        a_spec = pl.BlockSpec((tm, k), ij(lambda i, j, kk: (i, 0)))
    if mode != "nt":
        b_spec = _split_spec(b.ndim, _group(b), tk, tn, ij(lambda i, j, kk: (kk, j)))
    elif b.ndim == 3:
        b_spec = pl.BlockSpec((b.shape[0], tn, _group(b)), ij(lambda i, j, kk: (0, j, 0)))
    else:
        b_spec = pl.BlockSpec((tn, k), ij(lambda i, j, kk: (j, 0)))
    row_spec = pl.BlockSpec((1, tn), ij(lambda i, j, kk: (0, j)))
    blk_spec = pl.BlockSpec((tm, tn), ij(lambda i, j, kk: (i, j)))
    ins, in_specs = [a, b], [a_spec, b_spec]
    if bias is not None:
        ins.append(bias), in_specs.append(row_spec)
    if res is not None:
        ins.append(res), in_specs.append(blk_spec)
    if kind == "fwd":
        ins += [ln[1], ln[2]]
        in_specs += [row_spec, row_spec]
    elif kind == "loss":
        ins += [ln[1], ln[2], ln[3]]
        in_specs += [row_spec, row_spec, blk_spec]
    elif kind == "bwd":
        ins += [ln[1], ln[2], ln[3]]
        in_specs += [blk_spec, row_spec, row_spec]
    if out_groups:
        blk_out = jax.ShapeDtypeStruct((out_groups, m, n // out_groups), out_dtype)
        out_spec = _split_spec(3, n // out_groups, tm, tn, ij(lambda i, j, kk: (i, j)))
    else:
        blk_out, out_spec = jax.ShapeDtypeStruct((m, n), out_dtype), blk_spec
    row_out = jax.ShapeDtypeStruct((1, n), F32)
    if kind is None:
        out_shape, out_specs = [blk_out], [out_spec]
    elif kind == "fwd":
        out_shape, out_specs = [blk_out, blk_out], [blk_spec, blk_spec]
    else:
        out_shape, out_specs = [blk_out, row_out, row_out], [blk_spec, row_spec, row_spec]
        if kind == "loss":
            out_shape.append(jax.ShapeDtypeStruct((1, LANES), F32))
            out_specs.append(pl.BlockSpec((1, LANES), lambda p, q, kk: (0, 0)))
    if copy_dtype is not None:
        out_shape.append(jax.ShapeDtypeStruct((m, n), copy_dtype))
        out_specs.append(blk_spec)
    n_in = len(ins)

    def body(*refs):
        in_refs, out_refs, acc_ref = refs[:n_in], refs[n_in:n_in + len(out_shape)], refs[-1]
        i, kk = pl.program_id(0 if i_outer else 1), pl.program_id(2)
        a_ref, b_ref = in_refs[:2]
        extra = list(in_refs[2:])

        def epilogue(acc):
            rest = list(extra)
            if bias is not None:
                acc = acc + rest.pop(0)[...]
            if res is not None:
                acc = acc + res_scale * rest.pop(0)[...]
            if kind is None:
                out_refs[0][...] = acc.astype(out_dtype)
                return
            if kind == "fwd":
                out_refs[0][...] = acc
                y = _layer_norm(acc, rest[0][...], rest[1][...])
                out_refs[1][...] = y
                if copy_dtype is not None:
                    out_refs[-1][...] = y.astype(copy_dtype)
                return
            if kind == "loss":
                y, vjp = jax.vjp(_layer_norm, acc, rest[0][...], rest[1][...])
                err = y - rest[2][...]
                part = 0.5 * jnp.sum(jnp.sum(err * err, axis=1, keepdims=True), axis=0, keepdims=True) / n
                dz, dg, db = vjp(err / n)
            else:
                _, vjp = jax.vjp(_layer_norm, rest[0][...], rest[1][...], rest[2][...])
                dz, dg, db = vjp(acc)

            @pl.when(i == 0)
            def _():
                for r in out_refs[1:3 + (kind == "loss")]:
                    r[...] = jnp.zeros_like(r)

            out_refs[0][...] = dz
            out_refs[1][...] += dg
            out_refs[2][...] += db
            if kind == "loss":
                out_refs[3][...] += jnp.broadcast_to(part, (1, LANES))
            if copy_dtype is not None:
                out_refs[-1][...] = dz.astype(copy_dtype)

        def chunk(ref, c0, last):
            if ref.ndim == 3:
                g = ref.shape[2]
                return ref[c0 // g, :, pl.ds(c0 % g, kc)]
            return ref[:, pl.ds(c0, kc)] if last else ref[pl.ds(c0, kc), :]

        if mode == "tn" or kc == k:
            prod = _dg(a_ref[...], b_ref[...], ca, cb)
        else:
            prod = None
            for c0 in range(0, k, kc):
                part = _dg(chunk(a_ref, c0, True), chunk(b_ref, c0, mode == "nt"), ca, cb)
                prod = part if prod is None else prod + part
        if gk == 1:
            epilogue(prod)
            return

        @pl.when(kk == 0)
        def _():
            acc_ref[...] = prod

        @pl.when(kk > 0)
        def _():
            acc_ref[...] += prod

        @pl.when(kk == gk - 1)
        def _():
            epilogue(acc_ref[...])

    vmem = (2 * (_nbytes((tm, tk), a.dtype) + _nbytes((tk, tn), b.dtype))
            + (2 * len(ins) + 2 * len(out_shape) + 1) * _nbytes((tm, tn), F32))
    outs = _pcall(
        body, name=name, grid=(gi, gj, gk) if i_outer else (gj, gi, gk), in_specs=in_specs, out_specs=out_specs,
        out_shape=out_shape, scratch_shapes=[pltpu.VMEM((tm, tn) if gk > 1 else (SUBLANES, LANES), F32)],
        compiler_params=_params(("arbitrary", "arbitrary", "arbitrary"), vmem),
    )(*ins)
    return outs[0] if (kind is None and copy_dtype is None) else outs


def _colsum(name, a):
    m, n = a.shape
    tm = _tile(m, (512, 256, 128))

    def body(a_ref, o_ref):
        @pl.when(pl.program_id(0) == 0)
        def _():
            o_ref[...] = jnp.zeros_like(o_ref)

        o_ref[...] += jnp.sum(a_ref[...].astype(F32), axis=0, keepdims=True)

    return _pcall(
        body, name=name, grid=(m // tm,), in_specs=[pl.BlockSpec((tm, n), lambda i: (i, 0))],
        out_specs=pl.BlockSpec((1, n), lambda i: (0, 0)), out_shape=jax.ShapeDtypeStruct((1, n), F32),
        compiler_params=_params(("arbitrary",), 2 * _nbytes((tm, n), a.dtype)),
    )(a)


def _attn_head(q, k, v):
    sc = mm_nt(q, k) * (CA_DH ** -0.5)
    e = jnp.exp(sc - jnp.max(sc, axis=-1, keepdims=True))
    return mm_nn(e / jnp.sum(e, axis=-1, keepdims=True), v)


def _attn_fwd(q, kv):
    seq, n_mem = q.shape[0], kv.shape[0]
    tq = _tile(seq, (512, 256, 128))

    def body(q_ref, kv_ref, o_ref):
        for h in range(HEADS):
            hd = pl.ds(h * CA_DH, CA_DH)
            o = _attn_head(q_ref[:, hd], kv_ref[:, hd], kv_ref[:, pl.ds(D_MODEL + h * CA_DH, CA_DH)])
            o_ref[:, hd] = o.astype(BF16)

    return _pcall(
        body, name="attn_fwd", grid=(seq // tq,),
        in_specs=[pl.BlockSpec((tq, D_MODEL), lambda i: (i, 0)), pl.BlockSpec((n_mem, 2 * D_MODEL), lambda i: (0, 0))],
        out_specs=pl.BlockSpec((tq, D_MODEL), lambda i: (i, 0)), out_shape=jax.ShapeDtypeStruct((seq, D_MODEL), BF16),
        compiler_params=_params(("arbitrary",), 4 * _nbytes((tq, D_MODEL), F32) + 2 * _nbytes((n_mem, 2 * D_MODEL), F32)),
    )(q, kv)


def _attn_bwd(q, kv, do):
    seq, n_mem = q.shape[0], kv.shape[0]
    tq = _tile(seq, (512, 256, 128))

    def body(q_ref, kv_ref, do_ref, dq_ref, dkv_ref):
        @pl.when(pl.program_id(0) == 0)
        def _():
            dkv_ref[...] = jnp.zeros_like(dkv_ref)

        for h in range(HEADS):
            hd = pl.ds(h * CA_DH, CA_DH)
            vd = pl.ds(D_MODEL + h * CA_DH, CA_DH)
            _, vjp = jax.vjp(_attn_head, q_ref[:, hd], kv_ref[:, hd], kv_ref[:, vd])
            dq, dk, dv = vjp(do_ref[:, hd].astype(F32))
            dq_ref[:, hd] = dq.astype(BF16)
            dkv_ref[:, hd] += dk
            dkv_ref[:, vd] += dv

    return _pcall(
        body, name="attn_bwd", grid=(seq // tq,),
        in_specs=[pl.BlockSpec((tq, D_MODEL), lambda i: (i, 0)), pl.BlockSpec((n_mem, 2 * D_MODEL), lambda i: (0, 0)),
                  pl.BlockSpec((tq, D_MODEL), lambda i: (i, 0))],
        out_specs=[pl.BlockSpec((tq, D_MODEL), lambda i: (i, 0)), pl.BlockSpec((n_mem, 2 * D_MODEL), lambda i: (0, 0))],
        out_shape=[jax.ShapeDtypeStruct((seq, D_MODEL), BF16), jax.ShapeDtypeStruct((n_mem, 2 * D_MODEL), F32)],
        compiler_params=_params(("arbitrary",), 6 * _nbytes((tq, D_MODEL), F32) + 4 * _nbytes((n_mem, 2 * D_MODEL), F32)),
    )(q, kv, do)


FFN_TB = 512
FFN_TC = 256


def _ffn_mid(hg, xg, hv, xv, wg0, wg1, wg2, bg, wv0, wv1, wv2, bv):
    return jax.nn.gelu(causal_conv(hg, xg, (wg0, wg1, wg2), bg)) * causal_conv(hv, xv, (wv0, wv1, wv2), bv)


def _ffn_specs(seq, reverse):
    tb = min(FFN_TB, seq)
    nt = seq // tb
    row8 = tb // SUBLANES

    def tt(t):
        return nt - 1 - t if reverse else t
    nj = D_FF // FFN_TC
    main = pl.BlockSpec((tb, FFN_TC), lambda j, t: (tt(t), j))
    ins = []
    for off in (0, nj):
        ins += [pl.BlockSpec((tb, FFN_TC), lambda j, t, off=off: (tt(t), j + off)),
                pl.BlockSpec((SUBLANES, FFN_TC), lambda j, t, off=off: (jnp.maximum(tt(t) * row8 - 1, 0), j + off))]
    for off in (0, nj):
        ins += [pl.BlockSpec((FFN_CONV, FFN_TC), lambda j, t, off=off: (0, j + off)),
                pl.BlockSpec((1, FFN_TC), lambda j, t, off=off: (0, j + off))]
    return tb, nt, main, ins


def _ffn_args(c_first, ug, hg, uv, hv, wg, bg, wv, bv):
    halo_g = jnp.where(c_first, 0.0, hg[...])
    halo_v = jnp.where(c_first, 0.0, hv[...])
    return (halo_g, ug[...], halo_v, uv[...], wg[0:1, :], wg[1:2, :], wg[2:3, :], bg[...],
            wv[0:1, :], wv[1:2, :], wv[2:3, :], bv[...])


def _ffn_mid_fwd(u, conv_w, conv_b):
    seq = u.shape[0]
    tb, nt, main, ins = _ffn_specs(seq, False)

    def body(ug, hg, uv, hv, wg, bg, wv, bv, o_ref):
        o_ref[...] = _ffn_mid(*_ffn_args(pl.program_id(1) == 0, ug, hg, uv, hv, wg, bg, wv, bv)).astype(BF16)

    return _pcall(
        body, name="ffn_mid_fwd", grid=(D_FF // FFN_TC, nt), in_specs=ins, out_specs=main,
        out_shape=jax.ShapeDtypeStruct((seq, D_FF), BF16),
        compiler_params=_params(("arbitrary", "arbitrary"), 12 * _nbytes((tb, FFN_TC), F32)),
    )(u, u, u, u, conv_w, conv_b, conv_w, conv_b)


def _ffn_mid_bwd(u, conv_w, conv_b, dh):
    seq = u.shape[0]
    tb, nt, main, ins = _ffn_specs(seq, True)

    def body(ug, hg, uv, hv, wg, bg, wv, bv, dh_ref, du, dw, db, carry):
        t = pl.program_id(1)

        @pl.when(t == 0)
        def _():
            for r in (dw, db, carry):
                r[...] = jnp.zeros_like(r)

        _, vjp = jax.vjp(_ffn_mid, *_ffn_args(t == nt - 1, ug, hg, uv, hv, wg, bg, wv, bv))
        dhg, dxg, dhv, dxv, g0, g1, g2, gb, v0, v1, v2, vb = vjp(dh_ref[...])
        zeros = jnp.zeros((tb - SUBLANES, FFN_TC), F32)
        du[0] = (dxg + jnp.concatenate([zeros, carry[0]], axis=0)).astype(BF16)
        du[1] = (dxv + jnp.concatenate([zeros, carry[1]], axis=0)).astype(BF16)
        carry[0] = dhg
        carry[1] = dhv
        for half, parts in enumerate(((g0, g1, g2), (v0, v1, v2))):
            for d, p in enumerate(parts):
                dw[half, d:d + 1, :] += p
        db[0] += gb
        db[1] += vb

    def grouped(rows, index):
        return pl.BlockSpec((2, rows, FFN_TC), index)
    return _pcall(
        body, name="ffn_mid_bwd", grid=(D_FF // FFN_TC, nt), in_specs=ins + [main],
        out_specs=[grouped(tb, lambda j, t: (0, nt - 1 - t, j)), grouped(FFN_CONV, lambda j, t: (0, 0, j)),
                   grouped(1, lambda j, t: (0, 0, j))],
        out_shape=[jax.ShapeDtypeStruct((2, seq, D_FF), BF16), jax.ShapeDtypeStruct((2, FFN_CONV, D_FF), F32),
                   jax.ShapeDtypeStruct((2, 1, D_FF), F32)],
        scratch_shapes=[pltpu.VMEM((2, SUBLANES, FFN_TC), F32)],
        compiler_params=_params(("arbitrary", "arbitrary"), 24 * _nbytes((tb, FFN_TC), F32)),
    )(u, u, u, u, conv_w, conv_b, conv_w, conv_b, dh)


def _adamw_math(w, g, m, v):
    m_new = ADAM_B1 * m + (1.0 - ADAM_B1) * g
    v_new = ADAM_B2 * v + (1.0 - ADAM_B2) * jnp.square(g)
    m_hat = m_new / (1.0 - ADAM_B1 ** ADAM_STEP)
    v_hat = v_new / (1.0 - ADAM_B2 ** ADAM_STEP)
    return -ADAM_LR * (m_hat / (jnp.sqrt(v_hat) + ADAM_EPS) + ADAM_WD * w), m_new, v_new


def _adamw(name, w, g, m, v):
    rows, cols = w.shape
    tr = _tile(rows, (256, 176, 128, 64, 40, 32, 16, 8))

    def body(w_ref, g_ref, m_ref, v_ref, d_ref, nm_ref, nv_ref):
        d_ref[...], nm_ref[...], nv_ref[...] = _adamw_math(w_ref[...], g_ref[...], m_ref[...], v_ref[...])

    spec = pl.BlockSpec((tr, cols), lambda i: (i, 0))
    sh = jax.ShapeDtypeStruct((rows, cols), F32)
    return _pcall(
        body, name=name, grid=(rows // tr,), in_specs=[spec] * 4, out_specs=[spec] * 3, out_shape=[sh] * 3,
        compiler_params=_params(("arbitrary",), 14 * _nbytes((tr, -(-cols // LANES) * LANES), F32)),
    )(w, g, m, v)


def _adamw_halves(name, core, w, mine, theirs, m, v):
    rows, cols = w.shape
    tr = _tile(rows // 2, (256, 176, 128))
    nbh = rows // 2 // tr

    def body(c_ref, w_ref, a_ref, b_ref, m_ref, v_ref, g_ref, d_ref, nm_ref, nv_ref):
        g = jnp.where(pl.program_id(0) // nbh == c_ref[0], a_ref[...], b_ref[...])
        g_ref[...] = g
        d_ref[...], nm_ref[...], nv_ref[...] = _adamw_math(w_ref[...], g, m_ref[...], v_ref[...])

    spec = pl.BlockSpec((tr, cols), lambda i, c_ref: (i, 0))
    half = pl.BlockSpec((tr, cols), lambda i, c_ref: (i % nbh, 0))
    sh = jax.ShapeDtypeStruct((rows, cols), F32)
    grid_spec = pltpu.PrefetchScalarGridSpec(
        num_scalar_prefetch=1, grid=(rows // tr,), in_specs=[spec, half, half, spec, spec], out_specs=[spec] * 4)
    return _pcall(
        body, name=name, grid_spec=grid_spec, out_shape=[sh] * 4,
        compiler_params=_params(("arbitrary",), 18 * _nbytes((tr, -(-cols // LANES) * LANES), F32)),
    )(core, w, mine, theirs, m, v)


MESH = pl.DeviceIdType.MESH
ANY = pl.BlockSpec(memory_space=pl.ANY)
N_CHIPS = 4
N_DEV = 8
BF16_ROWS = 16


def _me():
    return lax.axis_index("x"), lax.axis_index("y"), lax.axis_index("c")


def _other_chips(x, y):
    return [(1 - x, y), (x, 1 - y), (1 - x, 1 - y)]


def _remote(src, dst, ssem, rsem, dev):
    return pltpu.make_async_remote_copy(src_ref=src, dst_ref=dst, send_sem=ssem, recv_sem=rsem,
                                        device_id=dev, device_id_type=MESH)


def _half_rows(ref_rows, cc):
    half = ref_rows // 2
    return pl.ds(pl.multiple_of(cc * half, BF16_ROWS), half)


def _gather_weights(shards):
    n = len(shards)
    n_ici = n * (N_CHIPS - 1)

    def body(*refs):
        ins, outs, (ssem, rsem, lsem, lrsem) = refs[:n], refs[n:2 * n], refs[2 * n:]
        x, y, c = _me()
        k_me = 2 * x + y
        sib = (x, y, 1 - c)
        chips = _other_chips(x, y)
        started = []
        for i, (w_ref, o_ref) in enumerate(zip(ins, outs)):
            cp = _remote(w_ref, o_ref.at[k_me], lsem.at[i], lrsem.at[i], sib)
            cp.start()
            started.append(cp)
        for r, (px, py) in enumerate(chips):
            for i, (w_ref, o_ref) in enumerate(zip(ins, outs)):
                rows = _half_rows(w_ref.shape[0], c)
                s = r * n + i
                cp = _remote(w_ref.at[rows], o_ref.at[k_me, rows], ssem.at[s], rsem.at[s], (px, py, c))
                cp.start()
                started.append(cp)
        for r, (px, py) in enumerate(chips):
            for i, o_ref in enumerate(outs):
                blk = o_ref.at[2 * px + py, _half_rows(o_ref.shape[1], c)]
                s = r * n + i
                _remote(blk, blk, ssem.at[s], rsem.at[s], (px, py, c)).wait_recv()
                cp = _remote(blk, blk, ssem.at[n_ici + s], rsem.at[n_ici + s], sib)
                cp.start()
                started.append(cp)
        for r, (px, py) in enumerate(chips):
            for i, o_ref in enumerate(outs):
                blk = o_ref.at[2 * px + py, _half_rows(o_ref.shape[1], 1 - c)]
                s = n_ici + r * n + i
                _remote(blk, blk, ssem.at[s], rsem.at[s], sib).wait_recv()
        for cp in started[n:]:
            cp.wait_send()
        for cp in started[:n]:
            cp.wait()

    return _pcall(
        body, name="gather_weights", in_specs=[ANY] * n, out_specs=[ANY] * n,
        out_shape=[jax.ShapeDtypeStruct((N_CHIPS,) + s.shape, s.dtype) for s in shards],
        scratch_shapes=[pltpu.SemaphoreType.DMA((2 * n_ici,)), pltpu.SemaphoreType.DMA((2 * n_ici,)),
                        pltpu.SemaphoreType.DMA((n,)), pltpu.SemaphoreType.DMA((n,))],
    )(*shards)


def _swap_halves(name, grads):
    n = len(grads)

    def body(*refs):
        ins, outs, (ssem, rsem) = refs[:n], refs[n:2 * n], refs[2 * n:]
        x, y, c = _me()
        copies = []
        for i, (g_ref, o_ref) in enumerate(zip(ins, outs)):
            for k in range(N_CHIPS):
                s = i * N_CHIPS + k
                cp = _remote(g_ref.at[k, _half_rows(g_ref.shape[1], 1 - c)], o_ref.at[k], ssem.at[s], rsem.at[s],
                             (x, y, 1 - c))
                cp.start()
                copies.append(cp)
        for cp in copies:
            cp.wait()

    return _pcall(
        body, name=name, in_specs=[ANY] * n, out_specs=[ANY] * n,
        out_shape=[jax.ShapeDtypeStruct((N_CHIPS, g.shape[1] // 2, g.shape[2]), g.dtype) for g in grads],
        scratch_shapes=[pltpu.SemaphoreType.DMA((n * N_CHIPS,)), pltpu.SemaphoreType.DMA((n * N_CHIPS,))],
    )(*grads)


SEM = pl.BlockSpec(memory_space=pltpu.SEMAPHORE)
IN_HBM = pl.BlockSpec(memory_space=pltpu.HBM)
SPLIT_PARAMS = dict(compiler_params=pltpu.CompilerParams(has_side_effects=pltpu.SideEffectType.DATAFLOW_SIDE_EFFECTING))


def _gather_start(name, shards):
    n = len(shards)
    n_sem = n * N_CHIPS

    def body(*refs):
        ins, lands, (ssem, rsem), token = refs[:n], refs[n:2 * n], refs[2 * n:2 * n + 2], refs[-1]
        x, y, c = _me()
        k_me = 2 * x + y
        for i, (w_ref, l_ref) in enumerate(zip(ins, lands)):
            _remote(w_ref, l_ref.at[k_me], ssem.at[i], rsem.at[i], (x, y, 1 - c)).start()
        for r, (px, py) in enumerate(_other_chips(x, y)):
            for i, (w_ref, l_ref) in enumerate(zip(ins, lands)):
                rows = _half_rows(w_ref.shape[0], c)
                s = (r + 1) * n + i
                _remote(w_ref.at[rows], l_ref.at[k_me, rows], ssem.at[s], rsem.at[s], (px, py, c)).start()
        token[...] = jnp.zeros_like(token)

    src = [pltpu.HBM(s.shape, s.dtype) for s in shards]
    dst = [pltpu.HBM((N_CHIPS,) + s.shape, s.dtype) for s in shards]
    outs = _call(
        body, name=name, in_specs=[IN_HBM] * (2 * n),
        out_specs=[SEM, SEM] + [IN_HBM] * (2 * n) + [pl.BlockSpec(memory_space=pltpu.VMEM)],
        out_shape=[pltpu.SemaphoreType.DMA((n_sem,)), pltpu.SemaphoreType.DMA((n_sem,))] + src + dst
        + [jax.ShapeDtypeStruct((SUBLANES, LANES), F32)],
        input_output_aliases={i: 2 + i for i in range(2 * n)}, **SPLIT_PARAMS,
    )(*[pltpu.with_memory_space_constraint(s, pltpu.HBM) for s in shards],
      *[pltpu.with_memory_space_constraint(lax.empty(d.shape, d.dtype), pltpu.HBM) for d in dst])
    return outs[:-1], outs[-1]


def _gather_wait(name, handle, after):
    ssem, rsem, thru = handle[0], handle[1], handle[2:]
    n = len(thru) // 2

    def body(*refs):
        ins, lands, (ssem_ref, rsem_ref) = refs[:n], refs[n:2 * n], refs[2 * n:2 * n + 2]
        x, y, c = _me()
        k_me = 2 * x + y
        for i, (w_ref, l_ref) in enumerate(zip(ins, lands)):
            cp = _remote(w_ref, l_ref.at[k_me], ssem_ref.at[i], rsem_ref.at[i], (x, y, 1 - c))
            cp.wait_send()
            cp.wait_recv()
        for r, (px, py) in enumerate(_other_chips(x, y)):
            for i, (w_ref, l_ref) in enumerate(zip(ins, lands)):
                rows = _half_rows(w_ref.shape[0], c)
                s = (r + 1) * n + i
                cp = _remote(w_ref.at[rows], l_ref.at[2 * px + py, rows], ssem_ref.at[s], rsem_ref.at[s], (px, py, c))
                cp.wait_send()
                cp.wait_recv()

    outs = _call(
        body, name=name, in_specs=[IN_HBM] * (2 * n) + [SEM, SEM, ANY], out_specs=[IN_HBM] * (2 * n),
        out_shape=[pltpu.HBM(t.shape, t.dtype) for t in thru],
        input_output_aliases={i: i for i in range(2 * n)}, **SPLIT_PARAMS,
    )(*thru, ssem, rsem, after)
    return outs[n:]


def _forward_halves(name, blocks):
    n = len(blocks)
    n_sem = n * (N_CHIPS - 1)

    def body(*refs):
        outs, (ssem, rsem) = refs[n:2 * n], refs[2 * n:]
        x, y, c = _me()
        sib = (x, y, 1 - c)
        chips = _other_chips(x, y)
        sends = []
        for r, (px, py) in enumerate(chips):
            for i, o_ref in enumerate(outs):
                blk = o_ref.at[2 * px + py, _half_rows(o_ref.shape[1], c)]
                cp = _remote(blk, blk, ssem.at[r * n + i], rsem.at[r * n + i], sib)
                cp.start()
                sends.append(cp)
        for r, (px, py) in enumerate(chips):
            for i, o_ref in enumerate(outs):
                blk = o_ref.at[2 * px + py, _half_rows(o_ref.shape[1], 1 - c)]
                _remote(blk, blk, ssem.at[r * n + i], rsem.at[r * n + i], sib).wait_recv()
        for cp in sends:
            cp.wait_send()

    return _pcall(
        body, name=name, in_specs=[ANY] * n, out_specs=[ANY] * n,
        out_shape=[jax.ShapeDtypeStruct(b.shape, b.dtype) for b in blocks],
        input_output_aliases={i: i for i in range(n)},
        scratch_shapes=[pltpu.SemaphoreType.DMA((n_sem,)), pltpu.SemaphoreType.DMA((n_sem,))],
    )(*blocks)


def _scatter_start(name, parts):
    n = len(parts)
    n_sem = n * (N_CHIPS - 1)

    def body(*refs):
        ins, lands, (ssem, rsem), token = refs[:n], refs[n:2 * n], refs[2 * n:2 * n + 2], refs[-1]
        x, y, c = _me()
        k_me = 2 * x + y
        for r, (px, py) in enumerate(_other_chips(x, y)):
            for i, (p_ref, l_ref) in enumerate(zip(ins, lands)):
                s = r * n + i
                _remote(p_ref.at[2 * px + py], l_ref.at[k_me], ssem.at[s], rsem.at[s], (px, py, c)).start()
        token[...] = jnp.zeros_like(token)

    hbm = [pltpu.HBM(p.shape, p.dtype) for p in parts]
    outs = _call(
        body, name=name, in_specs=[IN_HBM] * (2 * n),
        out_specs=[SEM, SEM] + [IN_HBM] * (2 * n) + [pl.BlockSpec(memory_space=pltpu.VMEM)],
        out_shape=[pltpu.SemaphoreType.DMA((n_sem,)), pltpu.SemaphoreType.DMA((n_sem,))] + hbm + hbm
        + [jax.ShapeDtypeStruct((SUBLANES, LANES), F32)],
        input_output_aliases={i: 2 + i for i in range(2 * n)}, **SPLIT_PARAMS,
    )(*[pltpu.with_memory_space_constraint(p, pltpu.HBM) for p in parts],
      *[pltpu.with_memory_space_constraint(lax.empty(p.shape, p.dtype), pltpu.HBM) for p in parts])
    return outs[:-1], outs[-1]


def _scatter_wait(name, handle, after):
    ssem, rsem, thru = handle[0], handle[1], handle[2:]
    n = len(thru) // 2

    def body(*refs):
        ins, lands, (ssem_ref, rsem_ref) = refs[:n], refs[n:2 * n], refs[2 * n:2 * n + 2]
        x, y, c = _me()
        for r, (px, py) in enumerate(_other_chips(x, y)):
            for i, (p_ref, l_ref) in enumerate(zip(ins, lands)):
                s = r * n + i
                cp = _remote(p_ref.at[2 * px + py], l_ref.at[2 * px + py], ssem_ref.at[s], rsem_ref.at[s], (px, py, c))
                cp.wait_send()
                cp.wait_recv()

    outs = _call(
        body, name=name, in_specs=[IN_HBM] * (2 * n) + [SEM, SEM, ANY], out_specs=[IN_HBM] * (2 * n),
        out_shape=[pltpu.HBM(t.shape, t.dtype) for t in thru],
        input_output_aliases={i: i for i in range(2 * n)}, **SPLIT_PARAMS,
    )(*thru, ssem, rsem, after)
    return outs[n:]


def _share_halves(halves):
    n = len(halves)

    def body(*refs):
        ins, outs, (ssem, rsem) = refs[:n], refs[n:2 * n], refs[2 * n:]
        x, y, c = _me()
        copies = [_remote(r_ref, o_ref, ssem.at[i], rsem.at[i], (x, y, 1 - c))
                  for i, (r_ref, o_ref) in enumerate(zip(ins, outs))]
        for cp in copies:
            cp.start()
        for cp in copies:
            cp.wait()

    return _pcall(
        body, name="share_halves", in_specs=[ANY] * n, out_specs=[ANY] * n,
        out_shape=[jax.ShapeDtypeStruct(h.shape, h.dtype) for h in halves],
        scratch_shapes=[pltpu.SemaphoreType.DMA((n,)), pltpu.SemaphoreType.DMA((n,))],
    )(*halves)


def _exchange_small(v, reduce):
    rows = v.shape[0]

    def body(v_ref, out_ref, buf, ssem, rsem):
        x, y, c = _me()
        me = 4 * x + 2 * y + c
        peers = [((x + bx) % 2, (y + by) % 2, (c + bc) % 2)
                 for bx in (0, 1) for by in (0, 1) for bc in (0, 1) if (bx, by, bc) != (0, 0, 0)]
        dst = buf if reduce else out_ref
        dst[me] = v_ref[...]
        sends = [_remote(v_ref, dst.at[me], ssem.at[r], rsem.at[r], p) for r, p in enumerate(peers)]
        for cp in sends:
            cp.start()
        for r, (px, py, pc) in enumerate(peers):
            blk = dst.at[4 * px + 2 * py + pc]
            _remote(blk, blk, ssem.at[r], rsem.at[r], (px, py, pc)).wait_recv()
        if reduce:
            acc = buf[0]
            for d in range(1, N_DEV):
                acc = acc + buf[d]
            out_ref[...] = acc
        for cp in sends:
            cp.wait_send()

    vm = pl.BlockSpec(memory_space=pltpu.VMEM)
    out_shape = jax.ShapeDtypeStruct((rows, LANES) if reduce else (N_DEV, rows, LANES), F32)
    buf_shape = (N_DEV, rows, LANES) if reduce else (SUBLANES, LANES)
    return _pcall(
        body, pin=False, name="reduce_small" if reduce else "gather_small", in_specs=[vm], out_specs=vm, out_shape=out_shape,
        scratch_shapes=[pltpu.VMEM(buf_shape, F32), pltpu.SemaphoreType.DMA((N_DEV - 1,)),
                        pltpu.SemaphoreType.DMA((N_DEV - 1,))],
        compiler_params=pltpu.CompilerParams(vmem_limit_bytes=32 * 1024 * 1024),
    )(v)


def _add_pair(name, core, g, theirs):
    _, half, cols = theirs.shape
    tr = _tile(half, (256, 176, 128))
    nb = half // tr

    def body(c_ref, g_ref, t_ref, o32_ref, o16_ref):
        s = g_ref[...] + t_ref[...]
        o32_ref[...] = s
        o16_ref[...] = s.astype(BF16)

    spec = pl.BlockSpec((None, tr, cols), lambda k, i, c_ref: (k, i, 0))
    grid_spec = pltpu.PrefetchScalarGridSpec(
        num_scalar_prefetch=1, grid=(N_CHIPS, nb),
        in_specs=[pl.BlockSpec((None, tr, cols), lambda k, i, c_ref: (k, c_ref[0] * nb + i, 0)), spec],
        out_specs=[spec, spec])
    return _pcall(
        body, name=name, grid_spec=grid_spec,
        out_shape=[jax.ShapeDtypeStruct(theirs.shape, F32), jax.ShapeDtypeStruct(theirs.shape, BF16)],
        compiler_params=_params(("arbitrary", "arbitrary"), 8 * _nbytes((tr, cols + LANES), F32)),
    )(core, g, theirs)


def _add_chips(name, chip, p32, recv):
    _, half, cols = p32.shape
    tr = _tile(half, (256, 176, 128))

    def body(k_ref, p_ref, r0_ref, r1_ref, r2_ref, o_ref):
        o_ref[...] = ((p_ref[...] + r0_ref[...].astype(F32)) + r1_ref[...].astype(F32)) + r2_ref[...].astype(F32)

    def other(r):
        return pl.BlockSpec((None, tr, cols), lambda i, k_ref: (r + (k_ref[0] <= r).astype(jnp.int32), i, 0))
    grid_spec = pltpu.PrefetchScalarGridSpec(
        num_scalar_prefetch=1, grid=(half // tr,),
        in_specs=[pl.BlockSpec((None, tr, cols), lambda i, k_ref: (k_ref[0], i, 0)), other(0), other(1), other(2)],
        out_specs=pl.BlockSpec((tr, cols), lambda i, k_ref: (i, 0)))
    return _pcall(
        body, name=name, grid_spec=grid_spec, out_shape=jax.ShapeDtypeStruct((half, cols), F32),
        compiler_params=_params(("arbitrary",), 10 * _nbytes((tr, cols + LANES), F32)),
    )(chip, p32, recv, recv, recv)


def kernel(x, mem, w_in, b_in, hg_lb_logits, hg_norm_w, ml_conv_w, ml_conv_b, ml_norm_w, w_out, ln1_g, ln1_b, ca_wq, ca_wkv, ca_wo, ln2_g, ln2_b, ffn_w_up, ffn_conv_w, ffn_conv_b, ffn_w_down, ln3_g, ln3_b, loss_target, m_w_in, m_b_in, m_hg_lb_logits, m_hg_norm_w, m_ml_conv_w, m_ml_conv_b, m_ml_norm_w, m_w_out, m_ln1_g, m_ln1_b, m_ca_wq, m_ca_wkv, m_ca_wo, m_ln2_g, m_ln2_b, m_ffn_w_up, m_ffn_conv_w, m_ffn_conv_b, m_ffn_w_down, m_ln3_g, m_ln3_b, v_w_in, v_b_in, v_hg_lb_logits, v_hg_norm_w, v_ml_conv_w, v_ml_conv_b, v_ml_norm_w, v_w_out, v_ln1_g, v_ln1_b, v_ca_wq, v_ca_wkv, v_ca_wo, v_ln2_g, v_ln2_b, v_ffn_w_up, v_ffn_conv_w, v_ffn_conv_b, v_ffn_w_down, v_ln3_g, v_ln3_b):
    return _train_step(dict(locals()))


WEIGHTS = ("w_in", "b_in", "hg_lb_logits", "hg_norm_w", "ml_conv_w", "ml_conv_b", "ml_norm_w", "w_out", "ln1_g",
           "ln1_b", "ca_wq", "ca_wkv", "ca_wo", "ln2_g", "ln2_b", "ffn_w_up", "ffn_conv_w", "ffn_conv_b",
           "ffn_w_down", "ln3_g", "ln3_b")
MATRICES = ("w_in", "w_out", "ca_wq", "ca_wkv", "ca_wo", "ffn_w_up", "ffn_w_down")
COL_SHARDED = ("w_in", "ca_wkv", "ffn_w_up", "ml_conv_w", "ffn_conv_w")
SMALL = tuple(n for n in WEIGHTS if n not in MATRICES)
PART_ROWS = 16


def _part_rows(shape, lead):
    n = 1
    for s in shape[lead:]:
        n *= s
    return -(-n // (LANES * PART_ROWS)) * PART_ROWS


def _pack(arrs, dtype, lead=0, rows=None):
    parts = []
    for a in arrs:
        head = a.shape[:lead]
        flat = a.reshape(head + (-1,)).astype(dtype)
        pad = _part_rows(a.shape, lead) * LANES - flat.shape[-1]
        flat = jnp.pad(flat, [(0, 0)] * lead + [(0, pad)])
        parts.append(flat.reshape(head + (-1, LANES)))
    used = sum(p.shape[lead] for p in parts)
    if rows is not None and rows > used:
        parts.append(jnp.zeros(parts[0].shape[:lead] + (rows - used, LANES), dtype))
    return jnp.concatenate(parts, axis=lead)


def _unpack(buf, shapes):
    lead = buf.shape[:-2]
    outs, r = [], 0
    for sh in shapes:
        n = 1
        for s in sh:
            n *= s
        nr = _part_rows(sh, 0)
        flat = buf[..., r:r + nr, :].reshape(lead + (nr * LANES,))
        outs.append(flat[..., :n].reshape(lead + tuple(sh)))
        r += nr
    return outs


def _cat_cols(s):
    return jnp.moveaxis(s, 0, 1).reshape(s.shape[1], -1)


def _split_cols(g):
    return jnp.moveaxis(g.reshape(g.shape[0], N_CHIPS, -1), 1, 0)


def _stack_rows(s):
    return s.reshape(-1, s.shape[-1])


def _train_step(a):
    xs, mems, tgt = a["x"][0], a["mem"][0], a["loss_target"][0]
    core = lax.axis_index("c").astype(jnp.int32).reshape(1)
    chip = (2 * lax.axis_index("x") + lax.axis_index("y")).astype(jnp.int32).reshape(1)
    k_me = chip[0]
    shard = {n: a[n][0] for n in MATRICES}

    later = [n for n in MATRICES if n != "w_in"]
    taps = _exchange_small(_pack([a["ml_conv_w"][0], a["ffn_conv_w"][0]], F32), reduce=False)
    w = {"w_in": jnp.pad(_cat_cols(_gather_weights([shard["w_in"].astype(BF16)])[0]), ((0, 0), (0, D_IN_PAD - D_IN)))}
    gathering, token = _gather_start("gather_start", [shard[n].astype(BF16) for n in later])
    taps = taps.reshape((N_CHIPS, 2) + taps.shape[1:])[:, 0]
    ml_cw, ffn_cw = [_cat_cols(s) for s in _unpack(taps, [a["ml_conv_w"].shape[1:], a["ffn_conv_w"].shape[1:]])]
    b_in_p = jnp.pad(a["b_in"], ((0, 0), (0, D_IN_PAD - D_IN))) + token[0:1, 0:1]
    mixer_w = (a["hg_lb_logits"], a["hg_norm_w"], ml_cw, a["ml_conv_b"], a["ml_norm_w"])
    up_cols = a["ffn_w_up"].shape[-1]

    xb = xs.astype(BF16)
    proj = _mm("proj", "nn", xb, w["w_in"], bias=b_in_p, tm=256, tn=D_IN_PAD)
    y, hst, cst, nst, mst = _mixer_fwd(proj, *mixer_w)
    w.update(zip(later, _forward_halves("forward_halves", _gather_wait("gather_wait", gathering, y))))
    for n in ("w_out", "ca_wq", "ca_wo", "ffn_w_down"):
        w[n] = _stack_rows(w[n])
    z1, x1, x1b = _mm("mix_out", "nn", y, w["w_out"], res=xs, res_scale=ALPHA, ln=("fwd", a["ln1_g"], a["ln1_b"]),
                      copy_dtype=BF16)
    q = _mm("ca_q", "nn", x1b, w["ca_wq"], out_dtype=BF16, tn=D_MODEL)
    kv = _mm("ca_kv", "nn", mems, w["ca_wkv"])
    o = _attn_fwd(q, kv)
    z2, x2, x2b = _mm("ca_out", "nn", o, w["ca_wo"], res=x1, res_scale=ALPHA, ln=("fwd", a["ln2_g"], a["ln2_b"]),
                      copy_dtype=BF16)
    u = _mm("ffn_up", "nn", x2b, w["ffn_w_up"], tn=up_cols)
    hmid = _ffn_mid_fwd(u, ffn_cw, a["ffn_conv_b"])
    dz3, g_ln3g, g_ln3b, loss_part, dz3b = _mm("ffn_down", "nn", hmid, w["ffn_w_down"], res=x2, res_scale=ALPHA,
                                               ln=("loss", a["ln3_g"], a["ln3_b"], tgt), copy_dtype=BF16)

    grads = {"ln3_g": g_ln3g, "ln3_b": g_ln3b}
    dhmid = _mm("d_hmid", "nt", dz3b, w["ffn_w_down"], tn=D_FF)
    grads["ffn_w_down"] = _mm("g_w_down", "tn", hmid, dz3b, tm=D_FF // 2, tn=D_MODEL)
    du, g_cw, g_cb = _ffn_mid_bwd(u, ffn_cw, a["ffn_conv_b"], dhmid)
    grads["ffn_conv_w"] = jnp.moveaxis(g_cw, 0, 1).reshape(FFN_CONV, 2 * D_FF)
    grads["ffn_conv_b"] = g_cb.reshape(1, 2 * D_FF)
    grads["ffn_w_up"] = _mm("g_w_up", "tn", x2b, du, out_groups=N_CHIPS, tm=D_MODEL, tn=up_cols)
    grads["ffn_w_down"] = grads["ffn_w_down"].reshape((N_CHIPS,) + shard["ffn_w_down"].shape)
    pending = {}

    def reduce_start(tag, names):
        group = [grads[n] for n in names]
        sums = [_add_pair("add_pair_" + n, core, g, t)
                for n, g, t in zip(names, group, _swap_halves("swap_halves_" + tag, group))]
        handle, token = _scatter_start("scatter_start_" + tag, [s16 for _, s16 in sums])
        pending[tag] = (names, [s32 for s32, _ in sums], handle)
        return token[0:1, 0:1]

    zero = reduce_start("ffn", ("ffn_w_up", "ffn_w_down"))
    dz2, grads["ln2_g"], grads["ln2_b"], dz2b = _mm("d_x2", "nt", du, w["ffn_w_up"], res=dz3, res_scale=ALPHA,
                                                    ln=("bwd", z2, a["ln2_g"] + zero, a["ln2_b"]), copy_dtype=BF16)
    do = _mm("d_o", "nt", dz2b, w["ca_wo"], out_dtype=BF16, tn=D_MODEL)
    grads["ca_wo"] = _mm("g_wo", "tn", o, dz2b, tm=D_MODEL, tn=D_MODEL)
    dq, dkv = _attn_bwd(q, kv, do)
    grads["ca_wq"] = _mm("g_wq", "tn", x1b, dq, tm=D_MODEL, tn=D_MODEL)
    grads["ca_wkv"] = _mm("g_wkv", "tn", mems, dkv, out_groups=N_CHIPS, tm=D_MODEL)
    dz1, grads["ln1_g"], grads["ln1_b"], dz1b = _mm("d_x1", "nt", dq, w["ca_wq"], res=dz2, res_scale=ALPHA,
                                                    ln=("bwd", z1, a["ln1_g"], a["ln1_b"]), copy_dtype=BF16)
    dy = _mm("d_y", "nt", dz1b, w["w_out"], tn=D_MODEL)
    grads["w_out"] = _mm("g_w_out", "tn", y, dz1b, tm=D_MODEL, tn=D_MODEL)
    for n in ("w_out", "ca_wq", "ca_wo"):
        grads[n] = grads[n].reshape((N_CHIPS,) + shard[n].shape)
    zero = reduce_start("attn", ("w_out", "ca_wq", "ca_wkv", "ca_wo"))
    (dproj, g_b_in, grads["hg_lb_logits"], grads["hg_norm_w"], grads["ml_conv_w"], grads["ml_conv_b"],
     grads["ml_norm_w"]) = _mixer_bwd(proj, dy, hst, cst, nst, mst, mixer_w[0], mixer_w[1] + zero, *mixer_w[2:])
    grads["w_in"] = _split_cols(_mm("g_w_in", "tn", xb, dproj, tm=D_MODEL, tn=up_cols)[:, :D_IN])
    grads["b_in"] = g_b_in[:, :D_IN]
    zero = reduce_start("in", ("w_in",))
    dx = _mm("d_x", "nt", dproj, w["w_in"], bias=jnp.zeros((1, D_MODEL), F32) + zero, res=dz1, res_scale=ALPHA,
             tm=256, tn=D_MODEL)

    halves = {}
    for tag, (names, sums32, handle) in pending.items():
        for n, s32, r in zip(names, sums32, _scatter_wait("scatter_wait_" + tag, handle, dx)):
            halves[n] = _add_chips("add_chips_" + n, chip, s32, r)
    halves = [halves[n] for n in MATRICES]
    other_halves = _share_halves(halves)

    small_shapes = [grads[n].shape for n in SMALL] + [loss_part.shape]
    summed = _unpack(_exchange_small(_pack([grads[n] for n in SMALL] + [loss_part], F32), reduce=True), small_shapes)
    loss = summed[-1][0, 0]
    for n, g in zip(SMALL, summed[:-1]):
        if n in COL_SHARDED:
            cols = a[n].shape[-1]
            g = lax.dynamic_slice_in_dim(g, k_me * cols, cols, axis=1)
        grads[n] = g

    delta, new_m, new_v = {}, {}, {}
    for n, mine, theirs in zip(MATRICES, halves, other_halves):
        grads[n], delta[n], new_m[n], new_v[n] = _adamw_halves(
            "adamw_" + n, core, shard[n], mine, theirs, a["m_" + n][0], a["v_" + n][0])
    small_w = [a[n][0] if a[n].ndim == 3 else a[n] for n in SMALL]
    small_m = [a["m_" + n][0] if a[n].ndim == 3 else a["m_" + n] for n in SMALL]
    small_v = [a["v_" + n][0] if a[n].ndim == 3 else a["v_" + n] for n in SMALL]
    shapes = [w.shape for w in small_w]
    packed = [_pack(l, F32) for l in (small_w, [grads[n] for n in SMALL], small_m, small_v)]
    for out, buf in zip((delta, new_m, new_v), _adamw("adamw_small", *packed)):
        for n, v in zip(SMALL, _unpack(buf, shapes)):
            out[n] = v

    def shaped(d):
        return [d[n].reshape(a[n].shape) for n in WEIGHTS]
    return (loss, dx[None], *shaped(grads), *shaped(delta), *shaped(new_m), *shaped(new_v))
```

```python
import functools

import jax
import jax.numpy as jnp
from jax import lax
from jax.experimental import pallas as pl
from jax.experimental.pallas import tpu as pltpu

F32 = jnp.float32
BF16 = jnp.bfloat16

D_MODEL = 1024
HEADS = 4
DK = 128
D_GRP = HEADS * DK
CHUNK = 64
ML_CONV = 4
FFN_CONV = 3
D_FF = 2816
CA_DH = D_MODEL // HEADS
DEPTH = 1
ALPHA = (2.0 * DEPTH) ** 0.25
LN_EPS = 1e-5
NEG_BIG = -1e30
D_IN = 8 * D_GRP + 2 * HEADS
D_IN_PAD = 8 * D_GRP + 128
ADAM_LR, ADAM_B1, ADAM_B2, ADAM_EPS, ADAM_WD, ADAM_STEP = 0.001, 0.9, 0.999, 1e-08, 0.01, 10

SUBLANES = 8
LANES = 128
VMEM_BYTES = 64 * 1024 * 1024


def _pcall(body, pin=True, **kw):
    if not pin:
        return _call(body, **kw)
    kw["out_shape"] = jax.tree.map(lambda s: pltpu.HBM(s.shape, s.dtype), kw["out_shape"])
    call = _call(body, **kw)

    def pinned(*args):
        return call(*[pltpu.with_memory_space_constraint(x, pltpu.HBM) if jnp.issubdtype(x.dtype, jnp.floating) else x
                      for x in args])
    return pinned


def _call(body, **kw):
    return pl.pallas_call(body, **kw)


def _params(semantics, vmem_bytes):
    limit = int(min(max(2 * vmem_bytes, 16 * 1024 * 1024), VMEM_BYTES - 8 * 1024 * 1024))
    return pltpu.CompilerParams(dimension_semantics=semantics, vmem_limit_bytes=limit)


def _nbytes(shape, dtype):
    n = 1
    for s in shape:
        n *= s
    return n * jnp.dtype(dtype).itemsize


def _dg(a, b, ca, cb):
    return lax.dot_general(a.astype(BF16), b.astype(BF16), (((ca,), (cb,)), ((), ())),
                           preferred_element_type=F32)


@jax.custom_vjp
def mm_nn(a, b):
    return _dg(a, b, 1, 0)


mm_nn.defvjp(lambda a, b: (_dg(a, b, 1, 0), (a, b)),
             lambda r, g: (_dg(g, r[1], 1, 1).astype(r[0].dtype), _dg(r[0], g, 0, 0).astype(r[1].dtype)))


@jax.custom_vjp
def mm_nt(a, b):
    return _dg(a, b, 1, 1)


mm_nt.defvjp(lambda a, b: (_dg(a, b, 1, 1), (a, b)),
             lambda r, g: (_dg(g, r[1], 1, 0).astype(r[0].dtype), _dg(g, r[0], 0, 0).astype(r[1].dtype)))


@jax.custom_vjp
def mm_tn(a, b):
    return _dg(a, b, 0, 0)


mm_tn.defvjp(lambda a, b: (_dg(a, b, 0, 0), (a, b)),
             lambda r, g: (_dg(r[1], g, 1, 1).astype(r[0].dtype), _dg(r[0], g, 1, 0).astype(r[1].dtype)))


def _hdot(a, b):
    return jnp.dot(a, b, precision=lax.Precision.HIGHEST, preferred_element_type=F32)


def _tri(n, lower):
    r = lax.broadcasted_iota(jnp.int32, (n, n), 0)
    c = lax.broadcasted_iota(jnp.int32, (n, n), 1)
    return ((r >= c) if lower else (r <= c)).astype(F32)


@jax.custom_vjp
def cumsum_rows(x):
    return _hdot(_tri(x.shape[0], True), x)


cumsum_rows.defvjp(lambda x: (_hdot(_tri(x.shape[0], True), x), None),
                   lambda _, g: (_hdot(_tri(g.shape[0], False), g),))


def _shift_impl(halo, x, d):
    xx = jnp.concatenate([halo, x], axis=0)
    return pltpu.roll(xx, d, 0)[SUBLANES:]


@functools.partial(jax.custom_vjp, nondiff_argnums=(2,))
def shift_rows(halo, x, d):
    return _shift_impl(halo, x, d)


def _shift_bwd(d, _, g):
    n = g.shape[0] + SUBLANES
    gg = jnp.concatenate([jnp.zeros((SUBLANES, g.shape[1]), g.dtype), g], axis=0)
    r = pltpu.roll(gg, n - d, 0)
    return r[:SUBLANES], r[SUBLANES:]


shift_rows.defvjp(lambda halo, x, d: (_shift_impl(halo, x, d), None), _shift_bwd)


def causal_conv(halo, x, w_rows, b):
    k = len(w_rows)
    y = b + w_rows[k - 1] * x
    for d in range(1, k):
        y = y + w_rows[k - 1 - d] * shift_rows(halo, x, d)
    return y


def _sigmoid(x):
    return 1.0 / (1.0 + jnp.exp(-x))


def _silu(x):
    return x * _sigmoid(x)


def _log_sigmoid(x):
    return jnp.minimum(x, 0.0) - jnp.log(1.0 + jnp.exp(-jnp.abs(x)))


def _pick_lane(x, j):
    lane = lax.broadcasted_iota(jnp.int32, (1, x.shape[1]), 1)
    return jnp.sum(jnp.where(lane == j, x, 0.0), axis=1, keepdims=True)


def _pick_row(x, i):
    row = lax.broadcasted_iota(jnp.int32, (x.shape[0], 1), 0)
    return jnp.sum(jnp.where(row == i, x, 0.0), axis=0, keepdims=True)


def _col_to_row(e):
    n = e.shape[0]
    eye = lax.broadcasted_iota(jnp.int32, (n, n), 0) == lax.broadcasted_iota(jnp.int32, (n, n), 1)
    return jnp.sum(jnp.where(eye, e, 0.0), axis=0, keepdims=True)


def _layer_norm(z, g, b):
    mu = jnp.mean(z, axis=-1, keepdims=True)
    zc = z - mu
    var = jnp.mean(zc * zc, axis=-1, keepdims=True)
    return zc * lax.rsqrt(var + LN_EPS) * g + b


def _hg_head(st_t, hq, hf, hi, hgate, l0, l1, nw):
    n = hq.shape[0]
    lb = _sigmoid(l0 - l1)
    q = _silu(hq)
    lf = jnp.log(lb + (1.0 - lb) * _sigmoid(hf))
    k = (1.0 - lb) * _sigmoid(-hf)
    b = cumsum_rows(lf)
    b_ref = _pick_row(b, n // 2 - 1)
    b_last = _pick_row(b, n - 1)
    attn = mm_nt(q * jnp.exp(b - b_ref), k * jnp.exp(b_ref - b))
    attn = jnp.where(_tri(n, True) > 0, attn, 0.0)
    o = mm_nn(attn, hi) + mm_nt(q * jnp.exp(b), st_t)
    st_new = jnp.exp(b_last) * st_t + mm_tn(hi, k * jnp.exp(b_last - b))
    y = o * lax.rsqrt(jnp.mean(o * o, axis=-1, keepdims=True) + LN_EPS) * nw * _silu(hgate)
    return st_new, y


def _ml_head(c_st, n_st, m_st, q, k, v, gates, og, nw, h):
    n = q.shape[0]
    ig = _pick_lane(gates, h)
    fl = _log_sigmoid(_pick_lane(gates, HEADS + h))
    qs = q * (DK ** -0.5)
    b = _pick_lane(cumsum_rows(jnp.broadcast_to(fl, (n, LANES))), 0)
    g = jnp.sum(fl, axis=0, keepdims=True)
    d = jnp.where(_tri(n, True) > 0, b + _col_to_row(ig - b), -jnp.inf)
    inter = b + m_st
    m_t = jnp.maximum(inter, jnp.max(d, axis=1, keepdims=True))
    s = mm_nt(qs, k) * jnp.exp(d - m_t)
    w_inter = jnp.exp(inter - m_t)
    num = mm_nn(s, v) + w_inter * mm_nn(qs, c_st)
    den = jnp.sum(s, axis=1, keepdims=True) + w_inter * jnp.sum(qs * n_st, axis=1, keepdims=True)
    h_out = num / jnp.maximum(jnp.abs(den), jnp.exp(-m_t))
    a = g - b + ig
    m_new = jnp.maximum(g + m_st, jnp.max(a, axis=0, keepdims=True))
    decay = jnp.exp(g + m_st - m_new)
    wk = k * jnp.exp(a - m_new)
    c_new = decay * c_st + mm_tn(wk, v)
    n_new = decay * n_st + jnp.sum(wk, axis=0, keepdims=True)
    mu = jnp.mean(h_out, axis=-1, keepdims=True)
    hc = h_out - mu
    var = jnp.mean(hc * hc, axis=-1, keepdims=True)
    y = _sigmoid(og) * (hc * lax.rsqrt(var + LN_EPS) * nw)
    return c_new, n_new, m_new, y


def _qk_conv(halo, x, w0, w1, w2, w3, b):
    return _silu(causal_conv(halo, x, (w0, w1, w2, w3), b))


def _grp(i, h=None):
    if h is None:
        return pl.ds(i * D_GRP, D_GRP)
    return pl.ds(i * D_GRP + h * DK, DK)


def _mixer_specs(n_chunks, reverse):
    def chunk(c):
        return n_chunks - 1 - c if reverse else c
    row8 = CHUNK // SUBLANES
    proj_spec = pl.BlockSpec((CHUNK, D_IN_PAD), lambda c: (chunk(c), 0))
    halo_spec = pl.BlockSpec((SUBLANES, 2 * D_GRP), lambda c: (jnp.maximum(chunk(c) * row8 - 1, 0), 2))
    small = [pl.BlockSpec((2, D_GRP), lambda c: (0, 0)), pl.BlockSpec((1, D_GRP), lambda c: (0, 0)),
             pl.BlockSpec((ML_CONV, 2 * D_GRP), lambda c: (0, 0)), pl.BlockSpec((1, 2 * D_GRP), lambda c: (0, 0)),
             pl.BlockSpec((1, D_GRP), lambda c: (0, 0))]
    state_specs = [pl.BlockSpec((1, HEADS, DK, DK), lambda c: (chunk(c), 0, 0, 0)),
                   pl.BlockSpec((1, HEADS, DK, DK), lambda c: (chunk(c), 0, 0, 0)),
                   pl.BlockSpec((1, HEADS, 1, DK), lambda c: (chunk(c), 0, 0, 0)),
                   pl.BlockSpec((1, HEADS, 1, DK), lambda c: (chunk(c), 0, 0, 0))]
    y_spec = pl.BlockSpec((CHUNK, 2 * D_GRP), lambda c: (chunk(c), 0))
    return proj_spec, halo_spec, small, state_specs, y_spec, chunk


def _mixer_fwd(proj, lb_logits, hg_nw, conv_w, conv_b, ml_nw):
    seq = proj.shape[0]
    n_chunks = seq // CHUNK
    proj_spec, halo_spec, small, state_specs, y_spec, _ = _mixer_specs(n_chunks, False)

    def body(proj_ref, halo_ref, lg_ref, hnw_ref, cw_ref, cb_ref, mnw_ref,
             y_ref, hst_ref, cst_ref, nst_ref, mst_ref, hs, cs, ns, ms):
        c = pl.program_id(0)

        @pl.when(c == 0)
        def _():
            hs[...] = jnp.zeros_like(hs)
            cs[...] = jnp.zeros_like(cs)
            ns[...] = jnp.zeros_like(ns)
            ms[...] = jnp.full(ms.shape, NEG_BIG, F32)

        hst_ref[0] = hs[...]
        cst_ref[0] = cs[...]
        nst_ref[0] = ns[...]
        mst_ref[0] = ms[...]
        halo = jnp.where(c > 0, halo_ref[...], 0.0)
        qk = _qk_conv(halo, proj_ref[:, pl.ds(4 * D_GRP, 2 * D_GRP)],
                      cw_ref[0:1, :], cw_ref[1:2, :], cw_ref[2:3, :], cw_ref[3:4, :], cb_ref[...])
        gates = proj_ref[:, pl.ds(8 * D_GRP, LANES)]
        for h in range(HEADS):
            hd = pl.ds(h * DK, DK)
            st_new, y = _hg_head(hs[h], proj_ref[:, _grp(0, h)], proj_ref[:, _grp(1, h)], proj_ref[:, _grp(2, h)],
                                 proj_ref[:, _grp(3, h)], lg_ref[0:1, hd], lg_ref[1:2, hd], hnw_ref[:, hd])
            hs[h] = st_new
            y_ref[:, hd] = y
            c_new, n_new, m_new, y = _ml_head(
                cs[h], ns[h], _pick_lane(ms[h], 0), qk[:, h * DK:(h + 1) * DK],
                qk[:, D_GRP + h * DK:D_GRP + (h + 1) * DK], proj_ref[:, _grp(6, h)], gates,
                proj_ref[:, _grp(7, h)], mnw_ref[:, hd], h)
            cs[h] = c_new
            ns[h] = n_new
            ms[h] = jnp.broadcast_to(m_new, (1, DK))
            y_ref[:, pl.ds(D_GRP + h * DK, DK)] = y

    st = jax.ShapeDtypeStruct((n_chunks, HEADS, DK, DK), F32)
    vec = jax.ShapeDtypeStruct((n_chunks, HEADS, 1, DK), F32)
    vmem = 2 * (_nbytes((CHUNK, D_IN_PAD), F32) + _nbytes((CHUNK, 2 * D_GRP), F32) + 2 * _nbytes((HEADS, DK, DK), F32)) \
        + 2 * _nbytes((HEADS, DK, DK), F32)
    return _pcall(
        body, name="mixer_fwd", grid=(n_chunks,),
        in_specs=[proj_spec, halo_spec] + small,
        out_specs=[y_spec] + state_specs,
        out_shape=[jax.ShapeDtypeStruct((seq, 2 * D_GRP), F32), st, st, vec, vec],
        scratch_shapes=[pltpu.VMEM((HEADS, DK, DK), F32), pltpu.VMEM((HEADS, DK, DK), F32),
                        pltpu.VMEM((HEADS, 1, DK), F32), pltpu.VMEM((HEADS, 1, DK), F32)],
        compiler_params=_params(("arbitrary",), vmem),
    )(proj, proj, lb_logits, hg_nw, conv_w, conv_b, ml_nw)


def _mixer_bwd(proj, dy, hst, cst, nst, mst, lb_logits, hg_nw, conv_w, conv_b, ml_nw):
    seq = proj.shape[0]
    n_chunks = seq // CHUNK
    proj_spec, halo_spec, small, state_specs, y_spec, _ = _mixer_specs(n_chunks, True)

    def body(proj_ref, halo_ref, dy_ref, hst_ref, cst_ref, nst_ref, mst_ref,
             lg_ref, hnw_ref, cw_ref, cb_ref, mnw_ref,
             dproj_ref, dlg_ref, dhnw_ref, dcw_ref, dcb_ref, dmnw_ref,
             dhs, dcs, dns, dms, dhalo, dqk):
        c = pl.program_id(0)

        @pl.when(c == 0)
        def _():
            for r in (dhs, dcs, dns, dms, dhalo, dlg_ref, dhnw_ref, dcw_ref, dcb_ref, dmnw_ref):
                r[...] = jnp.zeros_like(r)

        first = c == n_chunks - 1
        halo = jnp.where(first, 0.0, halo_ref[...])
        x_qk = proj_ref[:, pl.ds(4 * D_GRP, 2 * D_GRP)]
        conv_args = (halo, x_qk, cw_ref[0:1, :], cw_ref[1:2, :], cw_ref[2:3, :], cw_ref[3:4, :], cb_ref[...])
        qk, conv_vjp = jax.vjp(_qk_conv, *conv_args)
        gates = proj_ref[:, pl.ds(8 * D_GRP, LANES)]
        dgates = jnp.zeros((CHUNK, LANES), F32)
        for h in range(HEADS):
            hd = pl.ds(h * DK, DK)
            args = (hst_ref[0, h], proj_ref[:, _grp(0, h)], proj_ref[:, _grp(1, h)], proj_ref[:, _grp(2, h)],
                    proj_ref[:, _grp(3, h)], lg_ref[0:1, hd], lg_ref[1:2, hd], hnw_ref[:, hd])
            _, vjp = jax.vjp(_hg_head, *args)
            dst, dhq, dhf, dhi, dhg, dl0, dl1, dnw = vjp((dhs[h], dy_ref[:, hd]))
            dhs[h] = dst
            dproj_ref[:, _grp(0, h)] = dhq
            dproj_ref[:, _grp(1, h)] = dhf
            dproj_ref[:, _grp(2, h)] = dhi
            dproj_ref[:, _grp(3, h)] = dhg
            dlg_ref[0:1, hd] += dl0
            dlg_ref[1:2, hd] += dl1
            dhnw_ref[:, hd] += dnw

            margs = (cst_ref[0, h], nst_ref[0, h], _pick_lane(mst_ref[0, h], 0), qk[:, h * DK:(h + 1) * DK],
                     qk[:, D_GRP + h * DK:D_GRP + (h + 1) * DK], proj_ref[:, _grp(6, h)], gates,
                     proj_ref[:, _grp(7, h)], mnw_ref[:, hd])
            _, mvjp = jax.vjp(functools.partial(_ml_head, h=h), *margs)
            dc, dn, dm, dq, dk, dv, dg, dog, dmn = mvjp(
                (dcs[h], dns[h], _pick_lane(dms[h], 0), dy_ref[:, pl.ds(D_GRP + h * DK, DK)]))
            dcs[h] = dc
            dns[h] = dn
            dms[h] = jnp.broadcast_to(dm, (1, DK))
            dqk[:, hd] = dq
            dqk[:, pl.ds(D_GRP + h * DK, DK)] = dk
            dproj_ref[:, _grp(6, h)] = dv
            dproj_ref[:, _grp(7, h)] = dog
            dmnw_ref[:, hd] += dmn
            dgates = dgates + dg
        dproj_ref[:, pl.ds(8 * D_GRP, LANES)] = dgates
        dh, dx, dw0, dw1, dw2, dw3, db = conv_vjp(dqk[...])
        tail = jnp.concatenate([jnp.zeros((CHUNK - SUBLANES, 2 * D_GRP), F32), dhalo[...]], axis=0)
        dproj_ref[:, pl.ds(4 * D_GRP, 2 * D_GRP)] = dx + tail
        dhalo[...] = dh
        dcw_ref[0:1, :] += dw0
        dcw_ref[1:2, :] += dw1
        dcw_ref[2:3, :] += dw2
        dcw_ref[3:4, :] += dw3
        dcb_ref[...] += db

    small_out = [pl.BlockSpec((2, D_GRP), lambda c: (0, 0)), pl.BlockSpec((1, D_GRP), lambda c: (0, 0)),
                 pl.BlockSpec((ML_CONV, 2 * D_GRP), lambda c: (0, 0)), pl.BlockSpec((1, 2 * D_GRP), lambda c: (0, 0)),
                 pl.BlockSpec((1, D_GRP), lambda c: (0, 0))]
    vmem = 2 * (2 * _nbytes((CHUNK, D_IN_PAD), F32) + _nbytes((CHUNK, 2 * D_GRP), F32)
                + 2 * _nbytes((HEADS, DK, DK), F32)) + 2 * _nbytes((HEADS, DK, DK), F32) + 4 * 1024 * 1024
    return _pcall(
        body, name="mixer_bwd", grid=(n_chunks,),
        in_specs=[proj_spec, halo_spec, y_spec] + state_specs + small,
        out_specs=[proj_spec] + small_out,
        out_shape=[jax.ShapeDtypeStruct((seq, D_IN_PAD), F32), jax.ShapeDtypeStruct((2, D_GRP), F32),
                   jax.ShapeDtypeStruct((1, D_GRP), F32), jax.ShapeDtypeStruct((ML_CONV, 2 * D_GRP), F32),
                   jax.ShapeDtypeStruct((1, 2 * D_GRP), F32), jax.ShapeDtypeStruct((1, D_GRP), F32)],
        scratch_shapes=[pltpu.VMEM((HEADS, DK, DK), F32), pltpu.VMEM((HEADS, DK, DK), F32),
                        pltpu.VMEM((HEADS, 1, DK), F32), pltpu.VMEM((HEADS, 1, DK), F32),
                        pltpu.VMEM((SUBLANES, 2 * D_GRP), F32), pltpu.VMEM((CHUNK, 2 * D_GRP), F32)],
        compiler_params=_params(("arbitrary",), vmem),
    )(proj, proj, dy, hst, cst, nst, mst, lb_logits, hg_nw, conv_w, conv_b, ml_nw)


def _heads(x):
    return [x[:, h * DK:(h + 1) * DK] for h in range(HEADS)]


def _last(x, j):
    lane = lax.broadcasted_iota(jnp.int32, (1, x.shape[-1]), 1)
    return jnp.sum(jnp.where(lane == j, x, 0.0), axis=-1, keepdims=True)


def _hg_chunk(st_t, hq, hf, hi, hgate, l0, l1, nw):
    n = hq.shape[0]
    lb = _sigmoid(l0 - l1)
    q = _silu(hq)
    lf = jnp.log(lb + (1.0 - lb) * _sigmoid(hf))
    k = (1.0 - lb) * _sigmoid(-hf)
    b = cumsum_rows(lf)
    b_ref = _pick_row(b, n // 2 - 1)
    b_last = _pick_row(b, n - 1)
    qa, ka = _heads(q * jnp.exp(b - b_ref)), _heads(k * jnp.exp(b_ref - b))
    qe, kd, eb, v = _heads(q * jnp.exp(b)), _heads(k * jnp.exp(b_last - b)), _heads(jnp.exp(b_last)), _heads(hi)
    tri = _tri(n, True) > 0
    attn = [jnp.where(tri, mm_nt(qa[h], ka[h]), 0.0) for h in range(HEADS)]
    o = [mm_nn(attn[h], v[h]) + mm_nt(qe[h], st_t[h]) for h in range(HEADS)]
    st_new = jnp.stack([eb[h] * st_t[h] + mm_tn(v[h], kd[h]) for h in range(HEADS)])
    yn = [o[h] * lax.rsqrt(jnp.mean(o[h] * o[h], axis=-1, keepdims=True) + LN_EPS) for h in range(HEADS)]
    return st_new, jnp.concatenate(yn, axis=1) * nw * _silu(hgate)


def _ml_chunk(c_st, n_st, m_st, q, k, v, gates, og, nw):
    n = q.shape[0]
    ig = jnp.stack([_last(gates, h) for h in range(HEADS)])
    fl = _log_sigmoid(jnp.stack([_last(gates, HEADS + h) for h in range(HEADS)]))
    bw = cumsum_rows(jnp.concatenate([jnp.broadcast_to(fl[h], (n, DK)) for h in range(HEADS)], axis=1))
    b = jnp.stack([_last(x, 0) for x in _heads(bw)])
    g = jnp.sum(fl, axis=1, keepdims=True)
    eye = lax.broadcasted_iota(jnp.int32, (n, n), 0) == lax.broadcasted_iota(jnp.int32, (n, n), 1)
    e_row = jnp.sum(jnp.where(eye, ig - b, 0.0), axis=1, keepdims=True)
    d = jnp.where(_tri(n, True) > 0, b + e_row, -jnp.inf)
    inter = b + m_st
    m_t = jnp.maximum(inter, jnp.max(d, axis=2, keepdims=True))
    qs, kh, vh = _heads(q * (DK ** -0.5)), _heads(k), _heads(v)
    s = jnp.stack([mm_nt(qs[h], kh[h]) for h in range(HEADS)]) * jnp.exp(d - m_t)
    w_inter = jnp.exp(inter - m_t)
    num = (jnp.stack([mm_nn(s[h], vh[h]) for h in range(HEADS)])
           + w_inter * jnp.stack([mm_nn(qs[h], c_st[h]) for h in range(HEADS)]))
    den = jnp.sum(s, axis=2, keepdims=True) + w_inter * jnp.sum(jnp.stack(qs) * n_st, axis=2, keepdims=True)
    h_out = num / jnp.maximum(jnp.abs(den), jnp.exp(-m_t))
    a = g - b + ig
    m_new = jnp.maximum(g + m_st, jnp.max(a, axis=1, keepdims=True))
    decay = jnp.exp(g + m_st - m_new)
    wk = jnp.stack(kh) * jnp.exp(a - m_new)
    c_new = decay * c_st + jnp.stack([mm_tn(wk[h], vh[h]) for h in range(HEADS)])
    n_new = decay * n_st + jnp.sum(wk, axis=1, keepdims=True)
    hc = h_out - jnp.mean(h_out, axis=-1, keepdims=True)
    yn = hc * lax.rsqrt(jnp.mean(hc * hc, axis=-1, keepdims=True) + LN_EPS)
    y = _sigmoid(og) * (jnp.concatenate([yn[h] for h in range(HEADS)], axis=1) * nw)
    return c_new, n_new, m_new, y


def _mixer_inputs(proj_ref, lg_ref, hnw_ref, mnw_ref, qk):
    hg_in = (proj_ref[:, _grp(0)], proj_ref[:, _grp(1)], proj_ref[:, _grp(2)], proj_ref[:, _grp(3)],
             lg_ref[0:1, :], lg_ref[1:2, :], hnw_ref[...])
    ml_in = (qk[:, :D_GRP], qk[:, D_GRP:], proj_ref[:, _grp(6)], proj_ref[:, pl.ds(8 * D_GRP, LANES)],
             proj_ref[:, _grp(7)], mnw_ref[...])
    return hg_in, ml_in


def _mixer_fwd(proj, lb_logits, hg_nw, conv_w, conv_b, ml_nw):
    seq = proj.shape[0]
    n_chunks = seq // CHUNK
    proj_spec, halo_spec, small, state_specs, y_spec, _ = _mixer_specs(n_chunks, False)

    def body(proj_ref, halo_ref, lg_ref, hnw_ref, cw_ref, cb_ref, mnw_ref,
             y_ref, hst_ref, cst_ref, nst_ref, mst_ref, hs, cs, ns, ms):
        c = pl.program_id(0)

        @pl.when(c == 0)
        def _():
            hs[...] = jnp.zeros_like(hs)
            cs[...] = jnp.zeros_like(cs)
            ns[...] = jnp.zeros_like(ns)
            ms[...] = jnp.full(ms.shape, NEG_BIG, F32)

        hst_ref[0] = hs[...]
        cst_ref[0] = cs[...]
        nst_ref[0] = ns[...]
        mst_ref[0] = ms[...]
        halo = jnp.where(c > 0, halo_ref[...], 0.0)
        qk = _qk_conv(halo, proj_ref[:, pl.ds(4 * D_GRP, 2 * D_GRP)],
                      cw_ref[0:1, :], cw_ref[1:2, :], cw_ref[2:3, :], cw_ref[3:4, :], cb_ref[...])
        hg_in, ml_in = _mixer_inputs(proj_ref, lg_ref, hnw_ref, mnw_ref, qk)
        hs[...], y_hg = _hg_chunk(hs[...], *hg_in)
        cs[...], ns[...], m_new, y_ml = _ml_chunk(cs[...], ns[...], _last(ms[...], 0), *ml_in)
        ms[...] = jnp.broadcast_to(m_new, ms.shape)
        y_ref[:, pl.ds(0, D_GRP)] = y_hg.astype(BF16)
        y_ref[:, pl.ds(D_GRP, D_GRP)] = y_ml.astype(BF16)

    st = jax.ShapeDtypeStruct((n_chunks, HEADS, DK, DK), F32)
    vec = jax.ShapeDtypeStruct((n_chunks, HEADS, 1, DK), F32)
    vmem = 2 * (_nbytes((CHUNK, D_IN_PAD), F32) + _nbytes((CHUNK, 2 * D_GRP), F32) + 2 * _nbytes((HEADS, DK, DK), F32)) \
        + 2 * _nbytes((HEADS, DK, DK), F32)
    return _pcall(
        body, name="mixer_fwd", grid=(n_chunks,),
        in_specs=[proj_spec, halo_spec] + small,
        out_specs=[y_spec] + state_specs,
        out_shape=[jax.ShapeDtypeStruct((seq, 2 * D_GRP), BF16), st, st, vec, vec],
        scratch_shapes=[pltpu.VMEM((HEADS, DK, DK), F32), pltpu.VMEM((HEADS, DK, DK), F32),
                        pltpu.VMEM((HEADS, 1, DK), F32), pltpu.VMEM((HEADS, 1, DK), F32)],
        compiler_params=_params(("arbitrary",), vmem),
    )(proj, proj, lb_logits, hg_nw, conv_w, conv_b, ml_nw)


def _mixer_bwd(proj, dy, hst, cst, nst, mst, lb_logits, hg_nw, conv_w, conv_b, ml_nw):
    seq = proj.shape[0]
    n_chunks = seq // CHUNK
    proj_spec, halo_spec, small, state_specs, y_spec, _ = _mixer_specs(n_chunks, True)

    def body(proj_ref, halo_ref, dy_ref, hst_ref, cst_ref, nst_ref, mst_ref,
             lg_ref, hnw_ref, cw_ref, cb_ref, mnw_ref,
             dproj_ref, dbin_ref, dlg_ref, dhnw_ref, dcw_ref, dcb_ref, dmnw_ref,
             dhs, dcs, dns, dms, dhalo):
        c = pl.program_id(0)

        @pl.when(c == 0)
        def _():
            for r in (dhs, dcs, dns, dms, dhalo, dbin_ref, dlg_ref, dhnw_ref, dcw_ref, dcb_ref, dmnw_ref):
                r[...] = jnp.zeros_like(r)

        def put(cols, val):
            dproj_ref[:, cols] = val.astype(BF16)
            dbin_ref[:, cols] += jnp.sum(val, axis=0, keepdims=True)

        first = c == n_chunks - 1
        halo = jnp.where(first, 0.0, halo_ref[...])
        x_qk = proj_ref[:, pl.ds(4 * D_GRP, 2 * D_GRP)]
        conv_args = (halo, x_qk, cw_ref[0:1, :], cw_ref[1:2, :], cw_ref[2:3, :], cw_ref[3:4, :], cb_ref[...])
        qk, conv_vjp = jax.vjp(_qk_conv, *conv_args)
        hg_in, ml_in = _mixer_inputs(proj_ref, lg_ref, hnw_ref, mnw_ref, qk)
        _, hg_vjp = jax.vjp(_hg_chunk, hst_ref[0], *hg_in)
        _, ml_vjp = jax.vjp(_ml_chunk, cst_ref[0], nst_ref[0], _last(mst_ref[0], 0), *ml_in)
        dst, dhq, dhf, dhi, dhg, dl0, dl1, dnw = hg_vjp((dhs[...], dy_ref[:, pl.ds(0, D_GRP)]))
        dc, dn, dm, dq, dk, dv, dgates, dog, dmn = ml_vjp(
            (dcs[...], dns[...], _last(dms[...], 0), dy_ref[:, pl.ds(D_GRP, D_GRP)]))
        dhs[...] = dst
        dcs[...] = dc
        dns[...] = dn
        dms[...] = jnp.broadcast_to(dm, dms.shape)
        for i, val in ((0, dhq), (1, dhf), (2, dhi), (3, dhg), (6, dv), (7, dog)):
            put(_grp(i), val)
        put(pl.ds(8 * D_GRP, LANES), dgates)
        dlg_ref[0:1, :] += dl0
        dlg_ref[1:2, :] += dl1
        dhnw_ref[...] += dnw
        dmnw_ref[...] += dmn
        dh, dx, dw0, dw1, dw2, dw3, db = conv_vjp(jnp.concatenate([dq, dk], axis=1))
        tail = jnp.concatenate([jnp.zeros((CHUNK - SUBLANES, 2 * D_GRP), F32), dhalo[...]], axis=0)
        put(pl.ds(4 * D_GRP, 2 * D_GRP), dx + tail)
        dhalo[...] = dh
        for d, dw in enumerate((dw0, dw1, dw2, dw3)):
            dcw_ref[d:d + 1, :] += dw
        dcb_ref[...] += db

    row = pl.BlockSpec((1, D_GRP), lambda c: (0, 0))
    small_out = [pl.BlockSpec((1, D_IN_PAD), lambda c: (0, 0)), pl.BlockSpec((2, D_GRP), lambda c: (0, 0)), row,
                 pl.BlockSpec((ML_CONV, 2 * D_GRP), lambda c: (0, 0)), pl.BlockSpec((1, 2 * D_GRP), lambda c: (0, 0)), row]
    dy_spec = pl.BlockSpec((CHUNK, 2 * D_GRP), y_spec.index_map)
    vmem = 2 * (2 * _nbytes((CHUNK, D_IN_PAD), F32) + _nbytes((CHUNK, 2 * D_GRP), F32)
                + 2 * _nbytes((HEADS, DK, DK), F32)) + 2 * _nbytes((HEADS, DK, DK), F32) + 4 * 1024 * 1024
    return _pcall(
        body, name="mixer_bwd", grid=(n_chunks,),
        in_specs=[proj_spec, halo_spec, dy_spec] + state_specs + small,
        out_specs=[proj_spec] + small_out,
        out_shape=[jax.ShapeDtypeStruct((seq, D_IN_PAD), BF16), jax.ShapeDtypeStruct((1, D_IN_PAD), F32),
                   jax.ShapeDtypeStruct((2, D_GRP), F32), jax.ShapeDtypeStruct((1, D_GRP), F32),
                   jax.ShapeDtypeStruct((ML_CONV, 2 * D_GRP), F32), jax.ShapeDtypeStruct((1, 2 * D_GRP), F32),
                   jax.ShapeDtypeStruct((1, D_GRP), F32)],
        scratch_shapes=[pltpu.VMEM((HEADS, DK, DK), F32), pltpu.VMEM((HEADS, DK, DK), F32),
                        pltpu.VMEM((HEADS, 1, DK), F32), pltpu.VMEM((HEADS, 1, DK), F32),
                        pltpu.VMEM((SUBLANES, 2 * D_GRP), F32)],
        compiler_params=_params(("arbitrary",), vmem),
    )(proj, proj, dy, hst, cst, nst, mst, lb_logits, hg_nw, conv_w, conv_b, ml_nw)


def _tile(n, prefs, unit=None):
    unit = unit or n
    for p in prefs:
        if unit % p == 0 and n % p == 0:
            return p
    return unit


def _logical(arr):
    return arr.shape if arr.ndim == 2 else (arr.shape[1], arr.shape[0] * arr.shape[2])


def _group(arr):
    return arr.shape[-1]


def _split_spec(ndim, group, tr, tc, where):
    if ndim == 2:
        return pl.BlockSpec((tr, tc), where)
    per = group // tc
    assert per * tc == group, (group, tc)

    def index(*ids):
        bi, bj = where(*ids)
        return (bj // per, bi, bj % per)
    return pl.BlockSpec((None, tr, tc), index)


def _mm(name, mode, a, b, *, bias=None, res=None, res_scale=1.0, ln=None, out_dtype=F32, out_groups=None,
        copy_dtype=None, tm=None, tn=None, tk=None):
    la, lb = _logical(a), _logical(b)
    if mode == "nn":
        (m, k), n = la, lb[1]
        n_unit = _group(b) if b.ndim == 3 else n
        kc = _group(a) if a.ndim == 3 else k
    elif mode == "nt":
        (m, k), n = la, lb[0]
        n_unit = n
        kc = min(_group(a) if a.ndim == 3 else k, _group(b) if b.ndim == 3 else k)
    else:
        (k, m), n = la, lb[1]
        n_unit, kc = (_group(b) if b.ndim == 3 else n), k
        assert a.ndim == 2
    if out_groups:
        n_unit = min(n_unit, n // out_groups)
    kind = ln[0] if ln else None
    tm = tm or (256 if ln else _tile(m, (512, 256, 128)))
    tn = n if ln else (tn or _tile(n, (512, 384, 256, 128), n_unit))
    tk = (tk or _tile(k, (2048, 512, 256, 128))) if mode == "tn" else k
    gi, gj, gk = m // tm, n // tn, k // tk
    assert gi * tm == m and gj * tn == n and gk * tk == k and n_unit % tn == 0, (name, m, n, k, tm, tn, tk)
    ca, cb = {"nn": (1, 0), "nt": (1, 1), "tn": (0, 0)}[mode]
    i_outer = gk > 1 or (gi - 1) * _nbytes(b.shape, b.dtype) <= (gj - 1) * _nbytes(a.shape, a.dtype)

    def ij(where):
        return (lambda p, q, kk: where(p, q, kk)) if i_outer else (lambda p, q, kk: where(q, p, kk))
    if mode == "tn":
        a_spec = pl.BlockSpec((tk, tm), ij(lambda i, j, kk: (kk, i)))
    elif a.ndim == 3:
        a_spec = pl.BlockSpec((a.shape[0], tm, _group(a)), ij(lambda i, j, kk: (0, i, 0)))
    else:
        a_spec = pl.BlockSpec((tm, k), ij(lambda i, j, kk: (i, 0)))
    if mode != "nt":
        b_spec = _split_spec(b.ndim, _group(b), tk, tn, ij(lambda i, j, kk: (kk, j)))
    elif b.ndim == 3:
        b_spec = pl.BlockSpec((b.shape[0], tn, _group(b)), ij(lambda i, j, kk: (0, j, 0)))
    else:
        b_spec = pl.BlockSpec((tn, k), ij(lambda i, j, kk: (j, 0)))
    row_spec = pl.BlockSpec((1, tn), ij(lambda i, j, kk: (0, j)))
    blk_spec = pl.BlockSpec((tm, tn), ij(lambda i, j, kk: (i, j)))
    ins, in_specs = [a, b], [a_spec, b_spec]
    if bias is not None:
        ins.append(bias), in_specs.append(row_spec)
    if res is not None:
        ins.append(res), in_specs.append(blk_spec)
    if kind == "fwd":
        ins += [ln[1], ln[2]]
        in_specs += [row_spec, row_spec]
    elif kind == "loss":
        ins += [ln[1], ln[2], ln[3]]
        in_specs += [row_spec, row_spec, blk_spec]
    elif kind == "bwd":
        ins += [ln[1], ln[2], ln[3]]
        in_specs += [blk_spec, row_spec, row_spec]
    if out_groups:
        blk_out = jax.ShapeDtypeStruct((out_groups, m, n // out_groups), out_dtype)
        out_spec = _split_spec(3, n // out_groups, tm, tn, ij(lambda i, j, kk: (i, j)))
    else:
        blk_out, out_spec = jax.ShapeDtypeStruct((m, n), out_dtype), blk_spec
    row_out = jax.ShapeDtypeStruct((1, n), F32)
    if kind is None:
        out_shape, out_specs = [blk_out], [out_spec]
    elif kind == "fwd":
        out_shape, out_specs = [blk_out, blk_out], [blk_spec, blk_spec]
    else:
        out_shape, out_specs = [blk_out, row_out, row_out], [blk_spec, row_spec, row_spec]
        if kind == "loss":
            out_shape.append(jax.ShapeDtypeStruct((1, LANES), F32))
            out_specs.append(pl.BlockSpec((1, LANES), lambda p, q, kk: (0, 0)))
    if copy_dtype is not None:
        out_shape.append(jax.ShapeDtypeStruct((m, n), copy_dtype))
        out_specs.append(blk_spec)
    n_in = len(ins)

    def body(*refs):
        in_refs, out_refs, acc_ref = refs[:n_in], refs[n_in:n_in + len(out_shape)], refs[-1]
        i, kk = pl.program_id(0 if i_outer else 1), pl.program_id(2)
        a_ref, b_ref = in_refs[:2]
        extra = list(in_refs[2:])

        def epilogue(acc):
            rest = list(extra)
            if bias is not None:
                acc = acc + rest.pop(0)[...]
            if res is not None:
                acc = acc + res_scale * rest.pop(0)[...]
            if kind is None:
                out_refs[0][...] = acc.astype(out_dtype)
                return
            if kind == "fwd":
                out_refs[0][...] = acc
                y = _layer_norm(acc, rest[0][...], rest[1][...])
                out_refs[1][...] = y
                if copy_dtype is not None:
                    out_refs[-1][...] = y.astype(copy_dtype)
                return
            if kind == "loss":
                y, vjp = jax.vjp(_layer_norm, acc, rest[0][...], rest[1][...])
                err = y - rest[2][...]
                part = 0.5 * jnp.sum(jnp.sum(err * err, axis=1, keepdims=True), axis=0, keepdims=True) / n
                dz, dg, db = vjp(err / n)
            else:
                _, vjp = jax.vjp(_layer_norm, rest[0][...], rest[1][...], rest[2][...])
                dz, dg, db = vjp(acc)

            @pl.when(i == 0)
            def _():
                for r in out_refs[1:3 + (kind == "loss")]:
                    r[...] = jnp.zeros_like(r)

            out_refs[0][...] = dz
            out_refs[1][...] += dg
            out_refs[2][...] += db
            if kind == "loss":
                out_refs[3][...] += jnp.broadcast_to(part, (1, LANES))
            if copy_dtype is not None:
                out_refs[-1][...] = dz.astype(copy_dtype)

        def chunk(ref, c0, last):
            if ref.ndim == 3:
                g = ref.shape[2]
                return ref[c0 // g, :, pl.ds(c0 % g, kc)]
            return ref[:, pl.ds(c0, kc)] if last else ref[pl.ds(c0, kc), :]

        if mode == "tn" or kc == k:
            prod = _dg(a_ref[...], b_ref[...], ca, cb)
        else:
            prod = None
            for c0 in range(0, k, kc):
                part = _dg(chunk(a_ref, c0, True), chunk(b_ref, c0, mode == "nt"), ca, cb)
                prod = part if prod is None else prod + part
        if gk == 1:
            epilogue(prod)
            return

        @pl.when(kk == 0)
        def _():
            acc_ref[...] = prod

        @pl.when(kk > 0)
        def _():
            acc_ref[...] += prod

        @pl.when(kk == gk - 1)
        def _():
            epilogue(acc_ref[...])

    vmem = (2 * (_nbytes((tm, tk), a.dtype) + _nbytes((tk, tn), b.dtype))
            + (2 * len(ins) + 2 * len(out_shape) + 1) * _nbytes((tm, tn), F32))
    outs = _pcall(
        body, name=name, grid=(gi, gj, gk) if i_outer else (gj, gi, gk), in_specs=in_specs, out_specs=out_specs,
        out_shape=out_shape, scratch_shapes=[pltpu.VMEM((tm, tn) if gk > 1 else (SUBLANES, LANES), F32)],
        compiler_params=_params(("arbitrary", "arbitrary", "arbitrary"), vmem),
    )(*ins)
    return outs[0] if (kind is None and copy_dtype is None) else outs


def _colsum(name, a):
    m, n = a.shape
    tm = _tile(m, (512, 256, 128))

    def body(a_ref, o_ref):
        @pl.when(pl.program_id(0) == 0)
        def _():
            o_ref[...] = jnp.zeros_like(o_ref)

        o_ref[...] += jnp.sum(a_ref[...].astype(F32), axis=0, keepdims=True)

    return _pcall(
        body, name=name, grid=(m // tm,), in_specs=[pl.BlockSpec((tm, n), lambda i: (i, 0))],
        out_specs=pl.BlockSpec((1, n), lambda i: (0, 0)), out_shape=jax.ShapeDtypeStruct((1, n), F32),
        compiler_params=_params(("arbitrary",), 2 * _nbytes((tm, n), a.dtype)),
    )(a)


def _attn_head(q, k, v):
    sc = mm_nt(q, k) * (CA_DH ** -0.5)
    e = jnp.exp(sc - jnp.max(sc, axis=-1, keepdims=True))
    return mm_nn(e / jnp.sum(e, axis=-1, keepdims=True), v)


def _attn_fwd(q, kv):
    seq, n_mem = q.shape[0], kv.shape[0]
    tq = _tile(seq, (512, 256, 128))

    def body(q_ref, kv_ref, o_ref):
        for h in range(HEADS):
            hd = pl.ds(h * CA_DH, CA_DH)
            o = _attn_head(q_ref[:, hd], kv_ref[:, hd], kv_ref[:, pl.ds(D_MODEL + h * CA_DH, CA_DH)])
            o_ref[:, hd] = o.astype(BF16)

    return _pcall(
        body, name="attn_fwd", grid=(seq // tq,),
        in_specs=[pl.BlockSpec((tq, D_MODEL), lambda i: (i, 0)), pl.BlockSpec((n_mem, 2 * D_MODEL), lambda i: (0, 0))],
        out_specs=pl.BlockSpec((tq, D_MODEL), lambda i: (i, 0)), out_shape=jax.ShapeDtypeStruct((seq, D_MODEL), BF16),
        compiler_params=_params(("arbitrary",), 4 * _nbytes((tq, D_MODEL), F32) + 2 * _nbytes((n_mem, 2 * D_MODEL), F32)),
    )(q, kv)


def _attn_bwd(q, kv, do):
    seq, n_mem = q.shape[0], kv.shape[0]
    tq = _tile(seq, (512, 256, 128))

    def body(q_ref, kv_ref, do_ref, dq_ref, dkv_ref):
        @pl.when(pl.program_id(0) == 0)
        def _():
            dkv_ref[...] = jnp.zeros_like(dkv_ref)

        for h in range(HEADS):
            hd = pl.ds(h * CA_DH, CA_DH)
            vd = pl.ds(D_MODEL + h * CA_DH, CA_DH)
            _, vjp = jax.vjp(_attn_head, q_ref[:, hd], kv_ref[:, hd], kv_ref[:, vd])
            dq, dk, dv = vjp(do_ref[:, hd].astype(F32))
            dq_ref[:, hd] = dq.astype(BF16)
            dkv_ref[:, hd] += dk
            dkv_ref[:, vd] += dv

    return _pcall(
        body, name="attn_bwd", grid=(seq // tq,),
        in_specs=[pl.BlockSpec((tq, D_MODEL), lambda i: (i, 0)), pl.BlockSpec((n_mem, 2 * D_MODEL), lambda i: (0, 0)),
                  pl.BlockSpec((tq, D_MODEL), lambda i: (i, 0))],
        out_specs=[pl.BlockSpec((tq, D_MODEL), lambda i: (i, 0)), pl.BlockSpec((n_mem, 2 * D_MODEL), lambda i: (0, 0))],
        out_shape=[jax.ShapeDtypeStruct((seq, D_MODEL), BF16), jax.ShapeDtypeStruct((n_mem, 2 * D_MODEL), F32)],
        compiler_params=_params(("arbitrary",), 6 * _nbytes((tq, D_MODEL), F32) + 4 * _nbytes((n_mem, 2 * D_MODEL), F32)),
    )(q, kv, do)


FFN_TB = 512
FFN_TC = 256


def _ffn_mid(hg, xg, hv, xv, wg0, wg1, wg2, bg, wv0, wv1, wv2, bv):
    return jax.nn.gelu(causal_conv(hg, xg, (wg0, wg1, wg2), bg)) * causal_conv(hv, xv, (wv0, wv1, wv2), bv)


def _ffn_specs(seq, reverse):
    tb = min(FFN_TB, seq)
    nt = seq // tb
    row8 = tb // SUBLANES

    def tt(t):
        return nt - 1 - t if reverse else t
    nj = D_FF // FFN_TC
    main = pl.BlockSpec((tb, FFN_TC), lambda j, t: (tt(t), j))
    ins = []
    for off in (0, nj):
        ins += [pl.BlockSpec((tb, FFN_TC), lambda j, t, off=off: (tt(t), j + off)),
                pl.BlockSpec((SUBLANES, FFN_TC), lambda j, t, off=off: (jnp.maximum(tt(t) * row8 - 1, 0), j + off))]
    for off in (0, nj):
        ins += [pl.BlockSpec((FFN_CONV, FFN_TC), lambda j, t, off=off: (0, j + off)),
                pl.BlockSpec((1, FFN_TC), lambda j, t, off=off: (0, j + off))]
    return tb, nt, main, ins


def _ffn_args(c_first, ug, hg, uv, hv, wg, bg, wv, bv):
    halo_g = jnp.where(c_first, 0.0, hg[...])
    halo_v = jnp.where(c_first, 0.0, hv[...])
    return (halo_g, ug[...], halo_v, uv[...], wg[0:1, :], wg[1:2, :], wg[2:3, :], bg[...],
            wv[0:1, :], wv[1:2, :], wv[2:3, :], bv[...])


def _ffn_mid_fwd(u, conv_w, conv_b):
    seq = u.shape[0]
    tb, nt, main, ins = _ffn_specs(seq, False)

    def body(ug, hg, uv, hv, wg, bg, wv, bv, o_ref):
        o_ref[...] = _ffn_mid(*_ffn_args(pl.program_id(1) == 0, ug, hg, uv, hv, wg, bg, wv, bv)).astype(BF16)

    return _pcall(
        body, name="ffn_mid_fwd", grid=(D_FF // FFN_TC, nt), in_specs=ins, out_specs=main,
        out_shape=jax.ShapeDtypeStruct((seq, D_FF), BF16),
        compiler_params=_params(("arbitrary", "arbitrary"), 12 * _nbytes((tb, FFN_TC), F32)),
    )(u, u, u, u, conv_w, conv_b, conv_w, conv_b)


def _ffn_mid_bwd(u, conv_w, conv_b, dh):
    seq = u.shape[0]
    tb, nt, main, ins = _ffn_specs(seq, True)

    def body(ug, hg, uv, hv, wg, bg, wv, bv, dh_ref, du, dw, db, carry):
        t = pl.program_id(1)

        @pl.when(t == 0)
        def _():
            for r in (dw, db, carry):
                r[...] = jnp.zeros_like(r)

        _, vjp = jax.vjp(_ffn_mid, *_ffn_args(t == nt - 1, ug, hg, uv, hv, wg, bg, wv, bv))
        dhg, dxg, dhv, dxv, g0, g1, g2, gb, v0, v1, v2, vb = vjp(dh_ref[...])
        zeros = jnp.zeros((tb - SUBLANES, FFN_TC), F32)
        du[0] = (dxg + jnp.concatenate([zeros, carry[0]], axis=0)).astype(BF16)
        du[1] = (dxv + jnp.concatenate([zeros, carry[1]], axis=0)).astype(BF16)
        carry[0] = dhg
        carry[1] = dhv
        for half, parts in enumerate(((g0, g1, g2), (v0, v1, v2))):
            for d, p in enumerate(parts):
                dw[half, d:d + 1, :] += p
        db[0] += gb
        db[1] += vb

    def grouped(rows, index):
        return pl.BlockSpec((2, rows, FFN_TC), index)
    return _pcall(
        body, name="ffn_mid_bwd", grid=(D_FF // FFN_TC, nt), in_specs=ins + [main],
        out_specs=[grouped(tb, lambda j, t: (0, nt - 1 - t, j)), grouped(FFN_CONV, lambda j, t: (0, 0, j)),
                   grouped(1, lambda j, t: (0, 0, j))],
        out_shape=[jax.ShapeDtypeStruct((2, seq, D_FF), BF16), jax.ShapeDtypeStruct((2, FFN_CONV, D_FF), F32),
                   jax.ShapeDtypeStruct((2, 1, D_FF), F32)],
        scratch_shapes=[pltpu.VMEM((2, SUBLANES, FFN_TC), F32)],
        compiler_params=_params(("arbitrary", "arbitrary"), 24 * _nbytes((tb, FFN_TC), F32)),
    )(u, u, u, u, conv_w, conv_b, conv_w, conv_b, dh)


FFN_J = D_FF // FFN_TC


def _ffn_common_specs(seq, row):
    tb = min(FFN_TB, seq)
    full = pl.BlockSpec((tb, D_MODEL), lambda t, j: (row(t), 0))
    vec = pl.BlockSpec((1, D_MODEL), lambda t, j: (0, 0))
    halves = []
    for off in (0, FFN_J):
        halves.append(dict(
            w_up=pl.BlockSpec((D_MODEL, FFN_TC), lambda t, j, off=off: (0, j + off)),
            taps=pl.BlockSpec((FFN_CONV, FFN_TC), lambda t, j, off=off: (0, j + off)),
            bias=pl.BlockSpec((1, FFN_TC), lambda t, j, off=off: (0, j + off))))
    w_down = pl.BlockSpec((FFN_TC, D_MODEL), lambda t, j: (j, 0))
    u_blk = pl.BlockSpec((2, tb, FFN_TC), lambda t, j: (0, row(t), j))
    return tb, full, vec, halves, w_down, u_blk


def _ffn_fwd(x2b, x2, w_up, conv_w, conv_b, w_down, ln_g, ln_b, target):
    seq = x2.shape[0]
    tb, full, vec, halves, wd_spec, u_blk = _ffn_common_specs(seq, lambda t: t)
    nt = seq // tb

    def body(xb_ref, wg_ref, wv_ref, tg_ref, tv_ref, bg_ref, bv_ref, wd_ref, x_ref, g_ref, b_ref, tgt_ref,
             u_ref, h_ref, dz_ref, dg_ref, db_ref, loss_ref, dzb_ref, acc, carry):
        t, j = pl.program_id(0), pl.program_id(1)
        xb = xb_ref[...]
        ug, uv = _dg(xb, wg_ref[...], 1, 0), _dg(xb, wv_ref[...], 1, 0)
        u_ref[0] = ug
        u_ref[1] = uv
        halo_g = jnp.where(t == 0, 0.0, carry[j, 0])
        halo_v = jnp.where(t == 0, 0.0, carry[j, 1])
        h = _ffn_mid(halo_g, ug, halo_v, uv, tg_ref[0:1, :], tg_ref[1:2, :], tg_ref[2:3, :], bg_ref[...],
                     tv_ref[0:1, :], tv_ref[1:2, :], tv_ref[2:3, :], bv_ref[...]).astype(BF16)
        carry[j, 0] = ug[tb - SUBLANES:, :]
        carry[j, 1] = uv[tb - SUBLANES:, :]
        h_ref[...] = h
        part = _dg(h, wd_ref[...], 1, 0)

        @pl.when(j == 0)
        def _():
            acc[...] = part

        @pl.when(j > 0)
        def _():
            acc[...] += part

        @pl.when(j == FFN_J - 1)
        def _():
            y, vjp = jax.vjp(_layer_norm, acc[...] + ALPHA * x_ref[...], g_ref[...], b_ref[...])
            err = y - tgt_ref[...]
            part_loss = 0.5 * jnp.sum(jnp.sum(err * err, axis=1, keepdims=True), axis=0, keepdims=True) / D_MODEL
            dz, dg, db = vjp(err / D_MODEL)

            @pl.when(t == 0)
            def _():
                for r in (dg_ref, db_ref, loss_ref):
                    r[...] = jnp.zeros_like(r)

            dz_ref[...] = dz
            dzb_ref[...] = dz.astype(BF16)
            dg_ref[...] += dg
            db_ref[...] += db
            loss_ref[...] += jnp.broadcast_to(part_loss, (1, LANES))

    h0, h1 = halves
    row = jax.ShapeDtypeStruct((1, D_MODEL), F32)
    return _pcall(
        body, name="ffn_fwd", grid=(nt, FFN_J),
        in_specs=[full, h0["w_up"], h1["w_up"], h0["taps"], h1["taps"], h0["bias"], h1["bias"], wd_spec, full, vec, vec,
                  full],
        out_specs=[u_blk, pl.BlockSpec((tb, FFN_TC), lambda t, j: (t, j)), full, vec, vec,
                   pl.BlockSpec((1, LANES), lambda t, j: (0, 0)), full],
        out_shape=[jax.ShapeDtypeStruct((2, seq, D_FF), F32), jax.ShapeDtypeStruct((seq, D_FF), BF16),
                   jax.ShapeDtypeStruct((seq, D_MODEL), F32), row, row, jax.ShapeDtypeStruct((1, LANES), F32),
                   jax.ShapeDtypeStruct((seq, D_MODEL), BF16)],
        scratch_shapes=[pltpu.VMEM((tb, D_MODEL), F32), pltpu.VMEM((FFN_J, 2, SUBLANES, FFN_TC), F32)],
        compiler_params=_params(("arbitrary", "arbitrary"), 14 * _nbytes((tb, D_MODEL), F32)),
    )(x2b, w_up, w_up, conv_w, conv_w, conv_b, conv_b, w_down, x2, ln_g, ln_b, target)


def _ffn_bwd(u, conv_w, conv_b, dz3b, dz3, w_down, w_up, z2, ln_g, ln_b):
    seq = dz3.shape[0]
    tb = min(FFN_TB, seq)
    nt = seq // tb
    row8 = tb // SUBLANES
    tb, full, vec, halves, wd_spec, u_blk = _ffn_common_specs(seq, lambda t: nt - 1 - t)
    halo = pl.BlockSpec((2, SUBLANES, FFN_TC), lambda t, j: (0, jnp.maximum((nt - 1 - t) * row8 - 1, 0), j))

    def body(u_ref, halo_ref, tg_ref, tv_ref, bg_ref, bv_ref, dzb_ref, wd_ref, wg_ref, wv_ref, dz3_ref, z_ref, g_ref,
             b_ref, du_ref, dw_ref, dbias_ref, dz_ref, dg_ref, db_ref, dz2b_ref, acc, carry):
        t, j = pl.program_id(0), pl.program_id(1)

        @pl.when((t == 0) & (j == 0))
        def _():
            for r in (dw_ref, dbias_ref, dg_ref, db_ref):
                r[...] = jnp.zeros_like(r)

        dh = _dg(dzb_ref[...], wd_ref[...], 1, 1)
        first = t == nt - 1
        args = (jnp.where(first, 0.0, halo_ref[0]), u_ref[0], jnp.where(first, 0.0, halo_ref[1]), u_ref[1],
                tg_ref[0:1, :], tg_ref[1:2, :], tg_ref[2:3, :], bg_ref[...],
                tv_ref[0:1, :], tv_ref[1:2, :], tv_ref[2:3, :], bv_ref[...])
        _, vjp = jax.vjp(_ffn_mid, *args)
        dhg, dxg, dhv, dxv, g0, g1, g2, gb, v0, v1, v2, vb = vjp(dh)
        zeros = jnp.zeros((tb - SUBLANES, FFN_TC), F32)
        dug = (dxg + jnp.concatenate([zeros, jnp.where(t == 0, 0.0, carry[j, 0])], axis=0)).astype(BF16)
        duv = (dxv + jnp.concatenate([zeros, jnp.where(t == 0, 0.0, carry[j, 1])], axis=0)).astype(BF16)
        carry[j, 0] = dhg
        carry[j, 1] = dhv
        du_ref[0] = dug
        du_ref[1] = duv
        for half, parts in enumerate(((g0, g1, g2), (v0, v1, v2))):
            for d, p in enumerate(parts):
                dw_ref[j, half, d:d + 1, :] += p
        dbias_ref[j, 0] += gb
        dbias_ref[j, 1] += vb
        part = _dg(dug, wg_ref[...], 1, 1) + _dg(duv, wv_ref[...], 1, 1)

        @pl.when(j == 0)
        def _():
            acc[...] = part

        @pl.when(j > 0)
        def _():
            acc[...] += part

        @pl.when(j == FFN_J - 1)
        def _():
            _, ln_vjp = jax.vjp(_layer_norm, z_ref[...], g_ref[...], b_ref[...])
            dz, dg, db = ln_vjp(acc[...] + ALPHA * dz3_ref[...])
            dz_ref[...] = dz
            dz2b_ref[...] = dz.astype(BF16)
            dg_ref[...] += dg
            db_ref[...] += db

    h0, h1 = halves
    row = jax.ShapeDtypeStruct((1, D_MODEL), F32)
    whole = lambda *shape: pl.BlockSpec(shape, lambda t, j: (0,) * len(shape))
    return _pcall(
        body, name="ffn_bwd", grid=(nt, FFN_J),
        in_specs=[u_blk, halo, h0["taps"], h1["taps"], h0["bias"], h1["bias"], full, wd_spec, h0["w_up"], h1["w_up"],
                  full, full, vec, vec],
        out_specs=[u_blk, whole(FFN_J, 2, FFN_CONV, FFN_TC), whole(FFN_J, 2, 1, FFN_TC), full, vec, vec, full],
        out_shape=[jax.ShapeDtypeStruct((2, seq, D_FF), BF16), jax.ShapeDtypeStruct((FFN_J, 2, FFN_CONV, FFN_TC), F32),
                   jax.ShapeDtypeStruct((FFN_J, 2, 1, FFN_TC), F32), jax.ShapeDtypeStruct((seq, D_MODEL), F32), row, row,
                   jax.ShapeDtypeStruct((seq, D_MODEL), BF16)],
        scratch_shapes=[pltpu.VMEM((tb, D_MODEL), F32), pltpu.VMEM((FFN_J, 2, SUBLANES, FFN_TC), F32)],
        compiler_params=_params(("arbitrary", "arbitrary"), 14 * _nbytes((tb, D_MODEL), F32)),
    )(u, u, conv_w, conv_w, conv_b, conv_b, dz3b, w_down, w_up, w_up, dz3, z2, ln_g, ln_b)


def _adamw_math(w, g, m, v):
    m_new = ADAM_B1 * m + (1.0 - ADAM_B1) * g
    v_new = ADAM_B2 * v + (1.0 - ADAM_B2) * jnp.square(g)
    m_hat = m_new / (1.0 - ADAM_B1 ** ADAM_STEP)
    v_hat = v_new / (1.0 - ADAM_B2 ** ADAM_STEP)
    return -ADAM_LR * (m_hat / (jnp.sqrt(v_hat) + ADAM_EPS) + ADAM_WD * w), m_new, v_new


def _adamw(name, w, g, m, v):
    rows, cols = w.shape
    tr = _tile(rows, (256, 176, 128, 64, 40, 32, 16, 8))

    def body(w_ref, g_ref, m_ref, v_ref, d_ref, nm_ref, nv_ref):
        d_ref[...], nm_ref[...], nv_ref[...] = _adamw_math(w_ref[...], g_ref[...], m_ref[...], v_ref[...])

    spec = pl.BlockSpec((tr, cols), lambda i: (i, 0))
    sh = jax.ShapeDtypeStruct((rows, cols), F32)
    return _pcall(
        body, name=name, grid=(rows // tr,), in_specs=[spec] * 4, out_specs=[spec] * 3, out_shape=[sh] * 3,
        compiler_params=_params(("arbitrary",), 14 * _nbytes((tr, -(-cols // LANES) * LANES), F32)),
    )(w, g, m, v)


def _adamw_halves(name, core, w, mine, theirs, m, v):
    rows, cols = w.shape
    tr = _tile(rows // 2, (256, 176, 128))
    nbh = rows // 2 // tr

    def body(c_ref, w_ref, a_ref, b_ref, m_ref, v_ref, g_ref, d_ref, nm_ref, nv_ref):
        g = jnp.where(pl.program_id(0) // nbh == c_ref[0], a_ref[...], b_ref[...])
        g_ref[...] = g
        d_ref[...], nm_ref[...], nv_ref[...] = _adamw_math(w_ref[...], g, m_ref[...], v_ref[...])

    spec = pl.BlockSpec((tr, cols), lambda i, c_ref: (i, 0))
    half = pl.BlockSpec((tr, cols), lambda i, c_ref: (i % nbh, 0))
    sh = jax.ShapeDtypeStruct((rows, cols), F32)
    grid_spec = pltpu.PrefetchScalarGridSpec(
        num_scalar_prefetch=1, grid=(rows // tr,), in_specs=[spec, half, half, spec, spec], out_specs=[spec] * 4)
    return _pcall(
        body, name=name, grid_spec=grid_spec, out_shape=[sh] * 4,
        compiler_params=_params(("arbitrary",), 18 * _nbytes((tr, -(-cols // LANES) * LANES), F32)),
    )(core, w, mine, theirs, m, v)


MESH = pl.DeviceIdType.MESH
ANY = pl.BlockSpec(memory_space=pl.ANY)
N_CHIPS = 4
N_DEV = 8
BF16_ROWS = 16


def _me():
    return lax.axis_index("x"), lax.axis_index("y"), lax.axis_index("c")


def _other_chips(x, y):
    return [(1 - x, y), (x, 1 - y), (1 - x, 1 - y)]


def _remote(src, dst, ssem, rsem, dev):
    return pltpu.make_async_remote_copy(src_ref=src, dst_ref=dst, send_sem=ssem, recv_sem=rsem,
                                        device_id=dev, device_id_type=MESH)


def _half_rows(ref_rows, cc):
    half = ref_rows // 2
    return pl.ds(pl.multiple_of(cc * half, BF16_ROWS), half)


def _gather_weights(shards):
    n = len(shards)
    n_ici = n * (N_CHIPS - 1)

    def body(*refs):
        ins, outs, (ssem, rsem, lsem, lrsem) = refs[:n], refs[n:2 * n], refs[2 * n:]
        x, y, c = _me()
        k_me = 2 * x + y
        sib = (x, y, 1 - c)
        chips = _other_chips(x, y)
        started = []
        for i, (w_ref, o_ref) in enumerate(zip(ins, outs)):
            cp = _remote(w_ref, o_ref.at[k_me], lsem.at[i], lrsem.at[i], sib)
            cp.start()
            started.append(cp)
        for r, (px, py) in enumerate(chips):
            for i, (w_ref, o_ref) in enumerate(zip(ins, outs)):
                rows = _half_rows(w_ref.shape[0], c)
                s = r * n + i
                cp = _remote(w_ref.at[rows], o_ref.at[k_me, rows], ssem.at[s], rsem.at[s], (px, py, c))
                cp.start()
                started.append(cp)
        for r, (px, py) in enumerate(chips):
            for i, o_ref in enumerate(outs):
                blk = o_ref.at[2 * px + py, _half_rows(o_ref.shape[1], c)]
                s = r * n + i
                _remote(blk, blk, ssem.at[s], rsem.at[s], (px, py, c)).wait_recv()
                cp = _remote(blk, blk, ssem.at[n_ici + s], rsem.at[n_ici + s], sib)
                cp.start()
                started.append(cp)
        for r, (px, py) in enumerate(chips):
            for i, o_ref in enumerate(outs):
                blk = o_ref.at[2 * px + py, _half_rows(o_ref.shape[1], 1 - c)]
                s = n_ici + r * n + i
                _remote(blk, blk, ssem.at[s], rsem.at[s], sib).wait_recv()
        for cp in started[n:]:
            cp.wait_send()
        for cp in started[:n]:
            cp.wait()

    return _pcall(
        body, name="gather_weights", in_specs=[ANY] * n, out_specs=[ANY] * n,
        out_shape=[jax.ShapeDtypeStruct((N_CHIPS,) + s.shape, s.dtype) for s in shards],
        scratch_shapes=[pltpu.SemaphoreType.DMA((2 * n_ici,)), pltpu.SemaphoreType.DMA((2 * n_ici,)),
                        pltpu.SemaphoreType.DMA((n,)), pltpu.SemaphoreType.DMA((n,))],
    )(*shards)


def _swap_halves(name, grads):
    n = len(grads)

    def body(*refs):
        ins, outs, (ssem, rsem) = refs[:n], refs[n:2 * n], refs[2 * n:]
        x, y, c = _me()
        copies = []
        for i, (g_ref, o_ref) in enumerate(zip(ins, outs)):
            for k in range(N_CHIPS):
                s = i * N_CHIPS + k
                cp = _remote(g_ref.at[k, _half_rows(g_ref.shape[1], 1 - c)], o_ref.at[k], ssem.at[s], rsem.at[s],
                             (x, y, 1 - c))
                cp.start()
                copies.append(cp)
        for cp in copies:
            cp.wait()

    return _pcall(
        body, name=name, in_specs=[ANY] * n, out_specs=[ANY] * n,
        out_shape=[jax.ShapeDtypeStruct((N_CHIPS, g.shape[1] // 2, g.shape[2]), g.dtype) for g in grads],
        scratch_shapes=[pltpu.SemaphoreType.DMA((n * N_CHIPS,)), pltpu.SemaphoreType.DMA((n * N_CHIPS,))],
    )(*grads)


SEM = pl.BlockSpec(memory_space=pltpu.SEMAPHORE)
IN_HBM = pl.BlockSpec(memory_space=pltpu.HBM)
SPLIT_PARAMS = dict(compiler_params=pltpu.CompilerParams(has_side_effects=pltpu.SideEffectType.DATAFLOW_SIDE_EFFECTING))


def _gather_start(name, shards):
    n = len(shards)
    n_sem = n * N_CHIPS

    def body(*refs):
        ins, lands, (ssem, rsem), token = refs[:n], refs[n:2 * n], refs[2 * n:2 * n + 2], refs[-1]
        x, y, c = _me()
        k_me = 2 * x + y
        for i, (w_ref, l_ref) in enumerate(zip(ins, lands)):
            _remote(w_ref, l_ref.at[k_me], ssem.at[i], rsem.at[i], (x, y, 1 - c)).start()
        for r, (px, py) in enumerate(_other_chips(x, y)):
            for i, (w_ref, l_ref) in enumerate(zip(ins, lands)):
                rows = _half_rows(w_ref.shape[0], c)
                s = (r + 1) * n + i
                _remote(w_ref.at[rows], l_ref.at[k_me, rows], ssem.at[s], rsem.at[s], (px, py, c)).start()
        token[...] = jnp.zeros_like(token)

    src = [pltpu.HBM(s.shape, s.dtype) for s in shards]
    dst = [pltpu.HBM((N_CHIPS,) + s.shape, s.dtype) for s in shards]
    outs = _call(
        body, name=name, in_specs=[IN_HBM] * (2 * n),
        out_specs=[SEM, SEM] + [IN_HBM] * (2 * n) + [pl.BlockSpec(memory_space=pltpu.VMEM)],
        out_shape=[pltpu.SemaphoreType.DMA((n_sem,)), pltpu.SemaphoreType.DMA((n_sem,))] + src + dst
        + [jax.ShapeDtypeStruct((SUBLANES, LANES), F32)],
        input_output_aliases={i: 2 + i for i in range(2 * n)}, **SPLIT_PARAMS,
    )(*[pltpu.with_memory_space_constraint(s, pltpu.HBM) for s in shards],
      *[pltpu.with_memory_space_constraint(lax.empty(d.shape, d.dtype), pltpu.HBM) for d in dst])
    return outs[:-1], outs[-1]


def _gather_wait(name, handle, after):
    ssem, rsem, thru = handle[0], handle[1], handle[2:]
    n = len(thru) // 2

    def body(*refs):
        ins, lands, (ssem_ref, rsem_ref) = refs[:n], refs[n:2 * n], refs[2 * n:2 * n + 2]
        x, y, c = _me()
        k_me = 2 * x + y
        for i, (w_ref, l_ref) in enumerate(zip(ins, lands)):
            cp = _remote(w_ref, l_ref.at[k_me], ssem_ref.at[i], rsem_ref.at[i], (x, y, 1 - c))
            cp.wait_send()
            cp.wait_recv()
        for r, (px, py) in enumerate(_other_chips(x, y)):
            for i, (w_ref, l_ref) in enumerate(zip(ins, lands)):
                rows = _half_rows(w_ref.shape[0], c)
                s = (r + 1) * n + i
                cp = _remote(w_ref.at[rows], l_ref.at[2 * px + py, rows], ssem_ref.at[s], rsem_ref.at[s], (px, py, c))
                cp.wait_send()
                cp.wait_recv()

    outs = _call(
        body, name=name, in_specs=[IN_HBM] * (2 * n) + [SEM, SEM, ANY], out_specs=[IN_HBM] * (2 * n),
        out_shape=[pltpu.HBM(t.shape, t.dtype) for t in thru],
        input_output_aliases={i: i for i in range(2 * n)}, **SPLIT_PARAMS,
    )(*thru, ssem, rsem, after)
    return outs[n:]


def _forward_halves(name, blocks):
    n = len(blocks)
    n_sem = n * (N_CHIPS - 1)

    def body(*refs):
        outs, (ssem, rsem) = refs[n:2 * n], refs[2 * n:]
        x, y, c = _me()
        sib = (x, y, 1 - c)
        chips = _other_chips(x, y)
        sends = []
        for r, (px, py) in enumerate(chips):
            for i, o_ref in enumerate(outs):
                blk = o_ref.at[2 * px + py, _half_rows(o_ref.shape[1], c)]
                cp = _remote(blk, blk, ssem.at[r * n + i], rsem.at[r * n + i], sib)
                cp.start()
                sends.append(cp)
        for r, (px, py) in enumerate(chips):
            for i, o_ref in enumerate(outs):
                blk = o_ref.at[2 * px + py, _half_rows(o_ref.shape[1], 1 - c)]
                _remote(blk, blk, ssem.at[r * n + i], rsem.at[r * n + i], sib).wait_recv()
        for cp in sends:
            cp.wait_send()

    return _pcall(
        body, name=name, in_specs=[ANY] * n, out_specs=[ANY] * n,
        out_shape=[jax.ShapeDtypeStruct(b.shape, b.dtype) for b in blocks],
        input_output_aliases={i: i for i in range(n)},
        scratch_shapes=[pltpu.SemaphoreType.DMA((n_sem,)), pltpu.SemaphoreType.DMA((n_sem,))],
    )(*blocks)


def _scatter_start(name, parts):
    n = len(parts)
    n_sem = n * (N_CHIPS - 1)

    def body(*refs):
        ins, lands, (ssem, rsem), token = refs[:n], refs[n:2 * n], refs[2 * n:2 * n + 2], refs[-1]
        x, y, c = _me()
        k_me = 2 * x + y
        for r, (px, py) in enumerate(_other_chips(x, y)):
            for i, (p_ref, l_ref) in enumerate(zip(ins, lands)):
                s = r * n + i
                _remote(p_ref.at[2 * px + py], l_ref.at[k_me], ssem.at[s], rsem.at[s], (px, py, c)).start()
        token[...] = jnp.zeros_like(token)

    hbm = [pltpu.HBM(p.shape, p.dtype) for p in parts]
    outs = _call(
        body, name=name, in_specs=[IN_HBM] * (2 * n),
        out_specs=[SEM, SEM] + [IN_HBM] * (2 * n) + [pl.BlockSpec(memory_space=pltpu.VMEM)],
        out_shape=[pltpu.SemaphoreType.DMA((n_sem,)), pltpu.SemaphoreType.DMA((n_sem,))] + hbm + hbm
        + [jax.ShapeDtypeStruct((SUBLANES, LANES), F32)],
        input_output_aliases={i: 2 + i for i in range(2 * n)}, **SPLIT_PARAMS,
    )(*[pltpu.with_memory_space_constraint(p, pltpu.HBM) for p in parts],
      *[pltpu.with_memory_space_constraint(lax.empty(p.shape, p.dtype), pltpu.HBM) for p in parts])
    return outs[:-1], outs[-1]


def _scatter_wait(name, handle, after):
    ssem, rsem, thru = handle[0], handle[1], handle[2:]
    n = len(thru) // 2

    def body(*refs):
        ins, lands, (ssem_ref, rsem_ref) = refs[:n], refs[n:2 * n], refs[2 * n:2 * n + 2]
        x, y, c = _me()
        for r, (px, py) in enumerate(_other_chips(x, y)):
            for i, (p_ref, l_ref) in enumerate(zip(ins, lands)):
                s = r * n + i
                cp = _remote(p_ref.at[2 * px + py], l_ref.at[2 * px + py], ssem_ref.at[s], rsem_ref.at[s], (px, py, c))
                cp.wait_send()
                cp.wait_recv()

    outs = _call(
        body, name=name, in_specs=[IN_HBM] * (2 * n) + [SEM, SEM, ANY], out_specs=[IN_HBM] * (2 * n),
        out_shape=[pltpu.HBM(t.shape, t.dtype) for t in thru],
        input_output_aliases={i: i for i in range(2 * n)}, **SPLIT_PARAMS,
    )(*thru, ssem, rsem, after)
    return outs[n:]


def _share_halves(halves):
    n = len(halves)

    def body(*refs):
        ins, outs, (ssem, rsem) = refs[:n], refs[n:2 * n], refs[2 * n:]
        x, y, c = _me()
        copies = [_remote(r_ref, o_ref, ssem.at[i], rsem.at[i], (x, y, 1 - c))
                  for i, (r_ref, o_ref) in enumerate(zip(ins, outs))]
        for cp in copies:
            cp.start()
        for cp in copies:
            cp.wait()

    return _pcall(
        body, name="share_halves", in_specs=[ANY] * n, out_specs=[ANY] * n,
        out_shape=[jax.ShapeDtypeStruct(h.shape, h.dtype) for h in halves],
        scratch_shapes=[pltpu.SemaphoreType.DMA((n,)), pltpu.SemaphoreType.DMA((n,))],
    )(*halves)


def _exchange_small(v, reduce):
    rows = v.shape[0]

    def body(v_ref, out_ref, buf, ssem, rsem):
        x, y, c = _me()
        me = 4 * x + 2 * y + c
        peers = [((x + bx) % 2, (y + by) % 2, (c + bc) % 2)
                 for bx in (0, 1) for by in (0, 1) for bc in (0, 1) if (bx, by, bc) != (0, 0, 0)]
        dst = buf if reduce else out_ref
        dst[me] = v_ref[...]
        sends = [_remote(v_ref, dst.at[me], ssem.at[r], rsem.at[r], p) for r, p in enumerate(peers)]
        for cp in sends:
            cp.start()
        for r, (px, py, pc) in enumerate(peers):
            blk = dst.at[4 * px + 2 * py + pc]
            _remote(blk, blk, ssem.at[r], rsem.at[r], (px, py, pc)).wait_recv()
        if reduce:
            acc = buf[0]
            for d in range(1, N_DEV):
                acc = acc + buf[d]
            out_ref[...] = acc
        for cp in sends:
            cp.wait_send()

    vm = pl.BlockSpec(memory_space=pltpu.VMEM)
    out_shape = jax.ShapeDtypeStruct((rows, LANES) if reduce else (N_DEV, rows, LANES), F32)
    buf_shape = (N_DEV, rows, LANES) if reduce else (SUBLANES, LANES)
    return _pcall(
        body, pin=False, name="reduce_small" if reduce else "gather_small", in_specs=[vm], out_specs=vm, out_shape=out_shape,
        scratch_shapes=[pltpu.VMEM(buf_shape, F32), pltpu.SemaphoreType.DMA((N_DEV - 1,)),
                        pltpu.SemaphoreType.DMA((N_DEV - 1,))],
        compiler_params=pltpu.CompilerParams(vmem_limit_bytes=32 * 1024 * 1024),
    )(v)


def _add_pair(name, core, g, theirs):
    _, half, cols = theirs.shape
    tr = _tile(half, (256, 176, 128))
    nb = half // tr

    def body(c_ref, g_ref, t_ref, o32_ref, o16_ref):
        s = g_ref[...] + t_ref[...]
        o32_ref[...] = s
        o16_ref[...] = s.astype(BF16)

    spec = pl.BlockSpec((None, tr, cols), lambda k, i, c_ref: (k, i, 0))
    grid_spec = pltpu.PrefetchScalarGridSpec(
        num_scalar_prefetch=1, grid=(N_CHIPS, nb),
        in_specs=[pl.BlockSpec((None, tr, cols), lambda k, i, c_ref: (k, c_ref[0] * nb + i, 0)), spec],
        out_specs=[spec, spec])
    return _pcall(
        body, name=name, grid_spec=grid_spec,
        out_shape=[jax.ShapeDtypeStruct(theirs.shape, F32), jax.ShapeDtypeStruct(theirs.shape, BF16)],
        compiler_params=_params(("arbitrary", "arbitrary"), 8 * _nbytes((tr, cols + LANES), F32)),
    )(core, g, theirs)


def _add_chips(name, chip, p32, recv):
    _, half, cols = p32.shape
    tr = _tile(half, (256, 176, 128))

    def body(k_ref, p_ref, r0_ref, r1_ref, r2_ref, o_ref):
        o_ref[...] = ((p_ref[...] + r0_ref[...].astype(F32)) + r1_ref[...].astype(F32)) + r2_ref[...].astype(F32)

    def other(r):
        return pl.BlockSpec((None, tr, cols), lambda i, k_ref: (r + (k_ref[0] <= r).astype(jnp.int32), i, 0))
    grid_spec = pltpu.PrefetchScalarGridSpec(
        num_scalar_prefetch=1, grid=(half // tr,),
        in_specs=[pl.BlockSpec((None, tr, cols), lambda i, k_ref: (k_ref[0], i, 0)), other(0), other(1), other(2)],
        out_specs=pl.BlockSpec((tr, cols), lambda i, k_ref: (i, 0)))
    return _pcall(
        body, name=name, grid_spec=grid_spec, out_shape=jax.ShapeDtypeStruct((half, cols), F32),
        compiler_params=_params(("arbitrary",), 10 * _nbytes((tr, cols + LANES), F32)),
    )(chip, p32, recv, recv, recv)


def kernel(x, mem, w_in, b_in, hg_lb_logits, hg_norm_w, ml_conv_w, ml_conv_b, ml_norm_w, w_out, ln1_g, ln1_b, ca_wq, ca_wkv, ca_wo, ln2_g, ln2_b, ffn_w_up, ffn_conv_w, ffn_conv_b, ffn_w_down, ln3_g, ln3_b, loss_target, m_w_in, m_b_in, m_hg_lb_logits, m_hg_norm_w, m_ml_conv_w, m_ml_conv_b, m_ml_norm_w, m_w_out, m_ln1_g, m_ln1_b, m_ca_wq, m_ca_wkv, m_ca_wo, m_ln2_g, m_ln2_b, m_ffn_w_up, m_ffn_conv_w, m_ffn_conv_b, m_ffn_w_down, m_ln3_g, m_ln3_b, v_w_in, v_b_in, v_hg_lb_logits, v_hg_norm_w, v_ml_conv_w, v_ml_conv_b, v_ml_norm_w, v_w_out, v_ln1_g, v_ln1_b, v_ca_wq, v_ca_wkv, v_ca_wo, v_ln2_g, v_ln2_b, v_ffn_w_up, v_ffn_conv_w, v_ffn_conv_b, v_ffn_w_down, v_ln3_g, v_ln3_b):
    return _train_step(dict(locals()))


WEIGHTS = ("w_in", "b_in", "hg_lb_logits", "hg_norm_w", "ml_conv_w", "ml_conv_b", "ml_norm_w", "w_out", "ln1_g",
           "ln1_b", "ca_wq", "ca_wkv", "ca_wo", "ln2_g", "ln2_b", "ffn_w_up", "ffn_conv_w", "ffn_conv_b",
           "ffn_w_down", "ln3_g", "ln3_b")
MATRICES = ("w_in", "w_out", "ca_wq", "ca_wkv", "ca_wo", "ffn_w_up", "ffn_w_down")
COL_SHARDED = ("w_in", "ca_wkv", "ffn_w_up", "ml_conv_w", "ffn_conv_w")
SMALL = tuple(n for n in WEIGHTS if n not in MATRICES)
PART_ROWS = 16


def _part_rows(shape, lead):
    n = 1
    for s in shape[lead:]:
        n *= s
    return -(-n // (LANES * PART_ROWS)) * PART_ROWS


def _pack(arrs, dtype, lead=0, rows=None):
    parts = []
    for a in arrs:
        head = a.shape[:lead]
        flat = a.reshape(head + (-1,)).astype(dtype)
        pad = _part_rows(a.shape, lead) * LANES - flat.shape[-1]
        flat = jnp.pad(flat, [(0, 0)] * lead + [(0, pad)])
        parts.append(flat.reshape(head + (-1, LANES)))
    used = sum(p.shape[lead] for p in parts)
    if rows is not None and rows > used:
        parts.append(jnp.zeros(parts[0].shape[:lead] + (rows - used, LANES), dtype))
    return jnp.concatenate(parts, axis=lead)


def _unpack(buf, shapes):
    lead = buf.shape[:-2]
    outs, r = [], 0
    for sh in shapes:
        n = 1
        for s in sh:
            n *= s
        nr = _part_rows(sh, 0)
        flat = buf[..., r:r + nr, :].reshape(lead + (nr * LANES,))
        outs.append(flat[..., :n].reshape(lead + tuple(sh)))
        r += nr
    return outs


def _cat_cols(s):
    return jnp.moveaxis(s, 0, 1).reshape(s.shape[1], -1)


def _split_cols(g):
    return jnp.moveaxis(g.reshape(g.shape[0], N_CHIPS, -1), 1, 0)


def _stack_rows(s):
    return s.reshape(-1, s.shape[-1])


def _train_step(a):
    xs, mems, tgt = a["x"][0], a["mem"][0], a["loss_target"][0]
    core = lax.axis_index("c").astype(jnp.int32).reshape(1)
    chip = (2 * lax.axis_index("x") + lax.axis_index("y")).astype(jnp.int32).reshape(1)
    k_me = chip[0]
    shard = {n: a[n][0] for n in MATRICES}

    later = [n for n in MATRICES if n != "w_in"]
    taps = _exchange_small(_pack([a["ml_conv_w"][0], a["ffn_conv_w"][0]], F32), reduce=False)
    w = {"w_in": jnp.pad(_cat_cols(_gather_weights([shard["w_in"].astype(BF16)])[0]), ((0, 0), (0, D_IN_PAD - D_IN)))}
    gathering, token = _gather_start("gather_start", [shard[n].astype(BF16) for n in later])
    taps = taps.reshape((N_CHIPS, 2) + taps.shape[1:])[:, 0]
    ml_cw, ffn_cw = [_cat_cols(s) for s in _unpack(taps, [a["ml_conv_w"].shape[1:], a["ffn_conv_w"].shape[1:]])]
    b_in_p = jnp.pad(a["b_in"], ((0, 0), (0, D_IN_PAD - D_IN))) + token[0:1, 0:1]
    mixer_w = (a["hg_lb_logits"], a["hg_norm_w"], ml_cw, a["ml_conv_b"], a["ml_norm_w"])
    up_cols = a["ffn_w_up"].shape[-1]

    xb = xs.astype(BF16)
    proj = _mm("proj", "nn", xb, w["w_in"], bias=b_in_p, tm=256, tn=D_IN_PAD)
    y, hst, cst, nst, mst = _mixer_fwd(proj, *mixer_w)
    w.update(zip(later, _forward_halves("forward_halves", _gather_wait("gather_wait", gathering, y))))
    for n in ("w_out", "ca_wq", "ca_wo", "ffn_w_down"):
        w[n] = _stack_rows(w[n])
    z1, x1, x1b = _mm("mix_out", "nn", y, w["w_out"], res=xs, res_scale=ALPHA, ln=("fwd", a["ln1_g"], a["ln1_b"]),
                      copy_dtype=BF16)
    q = _mm("ca_q", "nn", x1b, w["ca_wq"], out_dtype=BF16, tn=D_MODEL)
    kv = _mm("ca_kv", "nn", mems, w["ca_wkv"])
    o = _attn_fwd(q, kv)
    z2, x2, x2b = _mm("ca_out", "nn", o, w["ca_wo"], res=x1, res_scale=ALPHA, ln=("fwd", a["ln2_g"], a["ln2_b"]),
                      copy_dtype=BF16)
    w_up = _cat_cols(w["ffn_w_up"])
    u, hmid, dz3, g_ln3g, g_ln3b, loss_part, dz3b = _ffn_fwd(
        x2b, x2, w_up, ffn_cw, a["ffn_conv_b"], w["ffn_w_down"], a["ln3_g"], a["ln3_b"], tgt)

    grads = {"ln3_g": g_ln3g, "ln3_b": g_ln3b}
    grads["ffn_w_down"] = _mm("g_w_down", "tn", hmid, dz3b, tm=D_FF // 2, tn=D_MODEL)
    du, g_cw, g_cb, dz2, grads["ln2_g"], grads["ln2_b"], dz2b = _ffn_bwd(
        u, ffn_cw, a["ffn_conv_b"], dz3b, dz3, w["ffn_w_down"], w_up, z2, a["ln2_g"], a["ln2_b"])
    grads["ffn_conv_w"] = jnp.transpose(g_cw, (2, 1, 0, 3)).reshape(FFN_CONV, 2 * D_FF)
    grads["ffn_conv_b"] = jnp.transpose(g_cb, (2, 1, 0, 3)).reshape(1, 2 * D_FF)
    grads["ffn_w_up"] = _mm("g_w_up", "tn", x2b, du, out_groups=N_CHIPS, tm=D_MODEL, tn=up_cols)
    grads["ffn_w_down"] = grads["ffn_w_down"].reshape((N_CHIPS,) + shard["ffn_w_down"].shape)
    pending = {}

    def reduce_start(tag, names):
        group = [grads[n] for n in names]
        sums = [_add_pair("add_pair_" + n, core, g, t)
                for n, g, t in zip(names, group, _swap_halves("swap_halves_" + tag, group))]
        handle, token = _scatter_start("scatter_start_" + tag, [s16 for _, s16 in sums])
        pending[tag] = (names, [s32 for s32, _ in sums], handle)
        return token[0:1, 0:1]

    zero = reduce_start("ffn", ("ffn_w_up", "ffn_w_down"))
    do = _mm("d_o", "nt", dz2b, w["ca_wo"], bias=jnp.zeros((1, D_MODEL), F32) + zero, out_dtype=BF16, tn=D_MODEL)
    grads["ca_wo"] = _mm("g_wo", "tn", o, dz2b, tm=D_MODEL, tn=D_MODEL)
    dq, dkv = _attn_bwd(q, kv, do)
    grads["ca_wq"] = _mm("g_wq", "tn", x1b, dq, tm=D_MODEL, tn=D_MODEL)
    grads["ca_wkv"] = _mm("g_wkv", "tn", mems, dkv, out_groups=N_CHIPS, tm=D_MODEL)
    dz1, grads["ln1_g"], grads["ln1_b"], dz1b = _mm("d_x1", "nt", dq, w["ca_wq"], res=dz2, res_scale=ALPHA,
                                                    ln=("bwd", z1, a["ln1_g"], a["ln1_b"]), copy_dtype=BF16)
    dy = _mm("d_y", "nt", dz1b, w["w_out"], tn=D_MODEL)
    grads["w_out"] = _mm("g_w_out", "tn", y, dz1b, tm=D_MODEL, tn=D_MODEL)
    for n in ("w_out", "ca_wq", "ca_wo"):
        grads[n] = grads[n].reshape((N_CHIPS,) + shard[n].shape)
    zero = reduce_start("attn", ("w_out", "ca_wq", "ca_wkv", "ca_wo"))
    (dproj, g_b_in, grads["hg_lb_logits"], grads["hg_norm_w"], grads["ml_conv_w"], grads["ml_conv_b"],
     grads["ml_norm_w"]) = _mixer_bwd(proj, dy, hst, cst, nst, mst, mixer_w[0], mixer_w[1] + zero, *mixer_w[2:])
    grads["w_in"] = _split_cols(_mm("g_w_in", "tn", xb, dproj, tm=D_MODEL, tn=up_cols)[:, :D_IN])
    grads["b_in"] = g_b_in[:, :D_IN]
    zero = reduce_start("in", ("w_in",))
    dx = _mm("d_x", "nt", dproj, w["w_in"], bias=jnp.zeros((1, D_MODEL), F32) + zero, res=dz1, res_scale=ALPHA,
             tm=256, tn=D_MODEL)

    halves = {}
    for tag, (names, sums32, handle) in pending.items():
        for n, s32, r in zip(names, sums32, _scatter_wait("scatter_wait_" + tag, handle, dx)):
            halves[n] = _add_chips("add_chips_" + n, chip, s32, r)
    halves = [halves[n] for n in MATRICES]
    other_halves = _share_halves(halves)

    small_shapes = [grads[n].shape for n in SMALL] + [loss_part.shape]
    summed = _unpack(_exchange_small(_pack([grads[n] for n in SMALL] + [loss_part], F32), reduce=True), small_shapes)
    loss = summed[-1][0, 0]
    for n, g in zip(SMALL, summed[:-1]):
        if n in COL_SHARDED:
            cols = a[n].shape[-1]
            g = lax.dynamic_slice_in_dim(g, k_me * cols, cols, axis=1)
        grads[n] = g

    delta, new_m, new_v = {}, {}, {}
    for n, mine, theirs in zip(MATRICES, halves, other_halves):
        grads[n], delta[n], new_m[n], new_v[n] = _adamw_halves(
            "adamw_" + n, core, shard[n], mine, theirs, a["m_" + n][0], a["v_" + n][0])
    small_w = [a[n][0] if a[n].ndim == 3 else a[n] for n in SMALL]
    small_m = [a["m_" + n][0] if a[n].ndim == 3 else a["m_" + n] for n in SMALL]
    small_v = [a["v_" + n][0] if a[n].ndim == 3 else a["v_" + n] for n in SMALL]
    shapes = [w.shape for w in small_w]
    packed = [_pack(l, F32) for l in (small_w, [grads[n] for n in SMALL], small_m, small_v)]
    for out, buf in zip((delta, new_m, new_v), _adamw("adamw_small", *packed)):
        for n, v in zip(SMALL, _unpack(buf, shapes)):
            out[n] = v

    def shaped(d):
        return [d[n].reshape(a[n].shape) for n in WEIGHTS]
    return (loss, dx[None], *shaped(grads), *shaped(delta), *shaped(new_m), *shaped(new_v))
```

```python
import functools

import jax
import jax.numpy as jnp
from jax import lax
from jax.experimental import pallas as pl
from jax.experimental.pallas import tpu as pltpu

F32 = jnp.float32
BF16 = jnp.bfloat16

D_MODEL = 1024
HEADS = 4
DK = 128
D_GRP = HEADS * DK
CHUNK = 64
ML_CONV = 4
FFN_CONV = 3
D_FF = 2816
CA_DH = D_MODEL // HEADS
DEPTH = 1
ALPHA = (2.0 * DEPTH) ** 0.25
LN_EPS = 1e-5
NEG_BIG = -1e30
D_IN = 8 * D_GRP + 2 * HEADS
D_IN_PAD = 8 * D_GRP + 128
ADAM_LR, ADAM_B1, ADAM_B2, ADAM_EPS, ADAM_WD, ADAM_STEP = 0.001, 0.9, 0.999, 1e-08, 0.01, 10

SUBLANES = 8
LANES = 128
VMEM_BYTES = 64 * 1024 * 1024


def _pcall(body, pin=True, **kw):
    if not pin:
        return _call(body, **kw)
    kw["out_shape"] = jax.tree.map(lambda s: pltpu.HBM(s.shape, s.dtype), kw["out_shape"])
    call = _call(body, **kw)

    def pinned(*args):
        return call(*[pltpu.with_memory_space_constraint(x, pltpu.HBM) if jnp.issubdtype(x.dtype, jnp.floating) else x
                      for x in args])
    return pinned


def _call(body, **kw):
    return pl.pallas_call(body, **kw)


def _params(semantics, vmem_bytes):
    limit = int(min(max(2 * vmem_bytes, 16 * 1024 * 1024), VMEM_BYTES - 8 * 1024 * 1024))
    return pltpu.CompilerParams(dimension_semantics=semantics, vmem_limit_bytes=limit)


def _nbytes(shape, dtype):
    n = 1
    for s in shape:
        n *= s
    return n * jnp.dtype(dtype).itemsize


def _dg(a, b, ca, cb):
    return lax.dot_general(a.astype(BF16), b.astype(BF16), (((ca,), (cb,)), ((), ())),
                           preferred_element_type=F32)


@jax.custom_vjp
def mm_nn(a, b):
    return _dg(a, b, 1, 0)


mm_nn.defvjp(lambda a, b: (_dg(a, b, 1, 0), (a, b)),
             lambda r, g: (_dg(g, r[1], 1, 1).astype(r[0].dtype), _dg(r[0], g, 0, 0).astype(r[1].dtype)))


@jax.custom_vjp
def mm_nt(a, b):
    return _dg(a, b, 1, 1)


mm_nt.defvjp(lambda a, b: (_dg(a, b, 1, 1), (a, b)),
             lambda r, g: (_dg(g, r[1], 1, 0).astype(r[0].dtype), _dg(g, r[0], 0, 0).astype(r[1].dtype)))


@jax.custom_vjp
def mm_tn(a, b):
    return _dg(a, b, 0, 0)


mm_tn.defvjp(lambda a, b: (_dg(a, b, 0, 0), (a, b)),
             lambda r, g: (_dg(r[1], g, 1, 1).astype(r[0].dtype), _dg(r[0], g, 1, 0).astype(r[1].dtype)))


def _hdot(a, b):
    return jnp.dot(a, b, precision=lax.Precision.HIGHEST, preferred_element_type=F32)


def _tri(n, lower):
    r = lax.broadcasted_iota(jnp.int32, (n, n), 0)
    c = lax.broadcasted_iota(jnp.int32, (n, n), 1)
    return ((r >= c) if lower else (r <= c)).astype(F32)


@jax.custom_vjp
def cumsum_rows(x):
    return _hdot(_tri(x.shape[0], True), x)


cumsum_rows.defvjp(lambda x: (_hdot(_tri(x.shape[0], True), x), None),
                   lambda _, g: (_hdot(_tri(g.shape[0], False), g),))


def _shift_impl(halo, x, d):
    xx = jnp.concatenate([halo, x], axis=0)
    return pltpu.roll(xx, d, 0)[SUBLANES:]


@functools.partial(jax.custom_vjp, nondiff_argnums=(2,))
def shift_rows(halo, x, d):
    return _shift_impl(halo, x, d)


def _shift_bwd(d, _, g):
    n = g.shape[0] + SUBLANES
    gg = jnp.concatenate([jnp.zeros((SUBLANES, g.shape[1]), g.dtype), g], axis=0)
    r = pltpu.roll(gg, n - d, 0)
    return r[:SUBLANES], r[SUBLANES:]


shift_rows.defvjp(lambda halo, x, d: (_shift_impl(halo, x, d), None), _shift_bwd)


def causal_conv(halo, x, w_rows, b):
    k = len(w_rows)
    y = b + w_rows[k - 1] * x
    for d in range(1, k):
        y = y + w_rows[k - 1 - d] * shift_rows(halo, x, d)
    return y


def _sigmoid(x):
    return 1.0 / (1.0 + jnp.exp(-x))


def _silu(x):
    return x * _sigmoid(x)


def _log_sigmoid(x):
    return jnp.minimum(x, 0.0) - jnp.log(1.0 + jnp.exp(-jnp.abs(x)))


def _pick_lane(x, j):
    lane = lax.broadcasted_iota(jnp.int32, (1, x.shape[1]), 1)
    return jnp.sum(jnp.where(lane == j, x, 0.0), axis=1, keepdims=True)


def _pick_row(x, i):
    row = lax.broadcasted_iota(jnp.int32, (x.shape[0], 1), 0)
    return jnp.sum(jnp.where(row == i, x, 0.0), axis=0, keepdims=True)


def _col_to_row(e):
    n = e.shape[0]
    eye = lax.broadcasted_iota(jnp.int32, (n, n), 0) == lax.broadcasted_iota(jnp.int32, (n, n), 1)
    return jnp.sum(jnp.where(eye, e, 0.0), axis=0, keepdims=True)


def _layer_norm(z, g, b):
    mu = jnp.mean(z, axis=-1, keepdims=True)
    zc = z - mu
    var = jnp.mean(zc * zc, axis=-1, keepdims=True)
    return zc * lax.rsqrt(var + LN_EPS) * g + b


def _hg_head(st_t, hq, hf, hi, hgate, l0, l1, nw):
    n = hq.shape[0]
    lb = _sigmoid(l0 - l1)
    q = _silu(hq)
    lf = jnp.log(lb + (1.0 - lb) * _sigmoid(hf))
    k = (1.0 - lb) * _sigmoid(-hf)
    b = cumsum_rows(lf)
    b_ref = _pick_row(b, n // 2 - 1)
    b_last = _pick_row(b, n - 1)
    attn = mm_nt(q * jnp.exp(b - b_ref), k * jnp.exp(b_ref - b))
    attn = jnp.where(_tri(n, True) > 0, attn, 0.0)
    o = mm_nn(attn, hi) + mm_nt(q * jnp.exp(b), st_t)
    st_new = jnp.exp(b_last) * st_t + mm_tn(hi, k * jnp.exp(b_last - b))
    y = o * lax.rsqrt(jnp.mean(o * o, axis=-1, keepdims=True) + LN_EPS) * nw * _silu(hgate)
    return st_new, y


def _ml_head(c_st, n_st, m_st, q, k, v, gates, og, nw, h):
    n = q.shape[0]
    ig = _pick_lane(gates, h)
    fl = _log_sigmoid(_pick_lane(gates, HEADS + h))
    qs = q * (DK ** -0.5)
    b = _pick_lane(cumsum_rows(jnp.broadcast_to(fl, (n, LANES))), 0)
    g = jnp.sum(fl, axis=0, keepdims=True)
    d = jnp.where(_tri(n, True) > 0, b + _col_to_row(ig - b), -jnp.inf)
    inter = b + m_st
    m_t = jnp.maximum(inter, jnp.max(d, axis=1, keepdims=True))
    s = mm_nt(qs, k) * jnp.exp(d - m_t)
    w_inter = jnp.exp(inter - m_t)
    num = mm_nn(s, v) + w_inter * mm_nn(qs, c_st)
    den = jnp.sum(s, axis=1, keepdims=True) + w_inter * jnp.sum(qs * n_st, axis=1, keepdims=True)
    h_out = num / jnp.maximum(jnp.abs(den), jnp.exp(-m_t))
    a = g - b + ig
    m_new = jnp.maximum(g + m_st, jnp.max(a, axis=0, keepdims=True))
    decay = jnp.exp(g + m_st - m_new)
    wk = k * jnp.exp(a - m_new)
    c_new = decay * c_st + mm_tn(wk, v)
    n_new = decay * n_st + jnp.sum(wk, axis=0, keepdims=True)
    mu = jnp.mean(h_out, axis=-1, keepdims=True)
    hc = h_out - mu
    var = jnp.mean(hc * hc, axis=-1, keepdims=True)
    y = _sigmoid(og) * (hc * lax.rsqrt(var + LN_EPS) * nw)
    return c_new, n_new, m_new, y


def _qk_conv(halo, x, w0, w1, w2, w3, b):
    return _silu(causal_conv(halo, x, (w0, w1, w2, w3), b))


def _grp(i, h=None):
    if h is None:
        return pl.ds(i * D_GRP, D_GRP)
    return pl.ds(i * D_GRP + h * DK, DK)


def _mixer_specs(n_chunks, reverse):
    def chunk(c):
        return n_chunks - 1 - c if reverse else c
    row8 = CHUNK // SUBLANES
    proj_spec = pl.BlockSpec((CHUNK, D_IN_PAD), lambda c: (chunk(c), 0))
    halo_spec = pl.BlockSpec((SUBLANES, 2 * D_GRP), lambda c: (jnp.maximum(chunk(c) * row8 - 1, 0), 2))
    small = [pl.BlockSpec((2, D_GRP), lambda c: (0, 0)), pl.BlockSpec((1, D_GRP), lambda c: (0, 0)),
             pl.BlockSpec((ML_CONV, 2 * D_GRP), lambda c: (0, 0)), pl.BlockSpec((1, 2 * D_GRP), lambda c: (0, 0)),
             pl.BlockSpec((1, D_GRP), lambda c: (0, 0))]
    state_specs = [pl.BlockSpec((1, HEADS, DK, DK), lambda c: (chunk(c), 0, 0, 0)),
                   pl.BlockSpec((1, HEADS, DK, DK), lambda c: (chunk(c), 0, 0, 0)),
                   pl.BlockSpec((1, HEADS, 1, DK), lambda c: (chunk(c), 0, 0, 0)),
                   pl.BlockSpec((1, HEADS, 1, DK), lambda c: (chunk(c), 0, 0, 0))]
    y_spec = pl.BlockSpec((CHUNK, 2 * D_GRP), lambda c: (chunk(c), 0))
    return proj_spec, halo_spec, small, state_specs, y_spec, chunk


def _mixer_fwd(proj, lb_logits, hg_nw, conv_w, conv_b, ml_nw):
    seq = proj.shape[0]
    n_chunks = seq // CHUNK
    proj_spec, halo_spec, small, state_specs, y_spec, _ = _mixer_specs(n_chunks, False)

    def body(proj_ref, halo_ref, lg_ref, hnw_ref, cw_ref, cb_ref, mnw_ref,
             y_ref, hst_ref, cst_ref, nst_ref, mst_ref, hs, cs, ns, ms):
        c = pl.program_id(0)

        @pl.when(c == 0)
        def _():
            hs[...] = jnp.zeros_like(hs)
            cs[...] = jnp.zeros_like(cs)
            ns[...] = jnp.zeros_like(ns)
            ms[...] = jnp.full(ms.shape, NEG_BIG, F32)

        hst_ref[0] = hs[...]
        cst_ref[0] = cs[...]
        nst_ref[0] = ns[...]
        mst_ref[0] = ms[...]
        halo = jnp.where(c > 0, halo_ref[...], 0.0)
        qk = _qk_conv(halo, proj_ref[:, pl.ds(4 * D_GRP, 2 * D_GRP)],
                      cw_ref[0:1, :], cw_ref[1:2, :], cw_ref[2:3, :], cw_ref[3:4, :], cb_ref[...])
        gates = proj_ref[:, pl.ds(8 * D_GRP, LANES)]
        for h in range(HEADS):
            hd = pl.ds(h * DK, DK)
            st_new, y = _hg_head(hs[h], proj_ref[:, _grp(0, h)], proj_ref[:, _grp(1, h)], proj_ref[:, _grp(2, h)],
                                 proj_ref[:, _grp(3, h)], lg_ref[0:1, hd], lg_ref[1:2, hd], hnw_ref[:, hd])
            hs[h] = st_new
            y_ref[:, hd] = y
            c_new, n_new, m_new, y = _ml_head(
                cs[h], ns[h], _pick_lane(ms[h], 0), qk[:, h * DK:(h + 1) * DK],
                qk[:, D_GRP + h * DK:D_GRP + (h + 1) * DK], proj_ref[:, _grp(6, h)], gates,
                proj_ref[:, _grp(7, h)], mnw_ref[:, hd], h)
            cs[h] = c_new
            ns[h] = n_new
            ms[h] = jnp.broadcast_to(m_new, (1, DK))
            y_ref[:, pl.ds(D_GRP + h * DK, DK)] = y

    st = jax.ShapeDtypeStruct((n_chunks, HEADS, DK, DK), F32)
    vec = jax.ShapeDtypeStruct((n_chunks, HEADS, 1, DK), F32)
    vmem = 2 * (_nbytes((CHUNK, D_IN_PAD), F32) + _nbytes((CHUNK, 2 * D_GRP), F32) + 2 * _nbytes((HEADS, DK, DK), F32)) \
        + 2 * _nbytes((HEADS, DK, DK), F32)
    return _pcall(
        body, name="mixer_fwd", grid=(n_chunks,),
        in_specs=[proj_spec, halo_spec] + small,
        out_specs=[y_spec] + state_specs,
        out_shape=[jax.ShapeDtypeStruct((seq, 2 * D_GRP), F32), st, st, vec, vec],
        scratch_shapes=[pltpu.VMEM((HEADS, DK, DK), F32), pltpu.VMEM((HEADS, DK, DK), F32),
                        pltpu.VMEM((HEADS, 1, DK), F32), pltpu.VMEM((HEADS, 1, DK), F32)],
        compiler_params=_params(("arbitrary",), vmem),
    )(proj, proj, lb_logits, hg_nw, conv_w, conv_b, ml_nw)


def _mixer_bwd(proj, dy, hst, cst, nst, mst, lb_logits, hg_nw, conv_w, conv_b, ml_nw):
    seq = proj.shape[0]
    n_chunks = seq // CHUNK
    proj_spec, halo_spec, small, state_specs, y_spec, _ = _mixer_specs(n_chunks, True)

    def body(proj_ref, halo_ref, dy_ref, hst_ref, cst_ref, nst_ref, mst_ref,
             lg_ref, hnw_ref, cw_ref, cb_ref, mnw_ref,
             dproj_ref, dlg_ref, dhnw_ref, dcw_ref, dcb_ref, dmnw_ref,
             dhs, dcs, dns, dms, dhalo, dqk):
        c = pl.program_id(0)

        @pl.when(c == 0)
        def _():
            for r in (dhs, dcs, dns, dms, dhalo, dlg_ref, dhnw_ref, dcw_ref, dcb_ref, dmnw_ref):
                r[...] = jnp.zeros_like(r)

        first = c == n_chunks - 1
        halo = jnp.where(first, 0.0, halo_ref[...])
        x_qk = proj_ref[:, pl.ds(4 * D_GRP, 2 * D_GRP)]
        conv_args = (halo, x_qk, cw_ref[0:1, :], cw_ref[1:2, :], cw_ref[2:3, :], cw_ref[3:4, :], cb_ref[...])
        qk, conv_vjp = jax.vjp(_qk_conv, *conv_args)
        gates = proj_ref[:, pl.ds(8 * D_GRP, LANES)]
        dgates = jnp.zeros((CHUNK, LANES), F32)
        for h in range(HEADS):
            hd = pl.ds(h * DK, DK)
            args = (hst_ref[0, h], proj_ref[:, _grp(0, h)], proj_ref[:, _grp(1, h)], proj_ref[:, _grp(2, h)],
                    proj_ref[:, _grp(3, h)], lg_ref[0:1, hd], lg_ref[1:2, hd], hnw_ref[:, hd])
            _, vjp = jax.vjp(_hg_head, *args)
            dst, dhq, dhf, dhi, dhg, dl0, dl1, dnw = vjp((dhs[h], dy_ref[:, hd]))
            dhs[h] = dst
            dproj_ref[:, _grp(0, h)] = dhq
            dproj_ref[:, _grp(1, h)] = dhf
            dproj_ref[:, _grp(2, h)] = dhi
            dproj_ref[:, _grp(3, h)] = dhg
            dlg_ref[0:1, hd] += dl0
            dlg_ref[1:2, hd] += dl1
            dhnw_ref[:, hd] += dnw

            margs = (cst_ref[0, h], nst_ref[0, h], _pick_lane(mst_ref[0, h], 0), qk[:, h * DK:(h + 1) * DK],
                     qk[:, D_GRP + h * DK:D_GRP + (h + 1) * DK], proj_ref[:, _grp(6, h)], gates,
                     proj_ref[:, _grp(7, h)], mnw_ref[:, hd])
            _, mvjp = jax.vjp(functools.partial(_ml_head, h=h), *margs)
            dc, dn, dm, dq, dk, dv, dg, dog, dmn = mvjp(
                (dcs[h], dns[h], _pick_lane(dms[h], 0), dy_ref[:, pl.ds(D_GRP + h * DK, DK)]))
            dcs[h] = dc
            dns[h] = dn
            dms[h] = jnp.broadcast_to(dm, (1, DK))
            dqk[:, hd] = dq
            dqk[:, pl.ds(D_GRP + h * DK, DK)] = dk
            dproj_ref[:, _grp(6, h)] = dv
            dproj_ref[:, _grp(7, h)] = dog
            dmnw_ref[:, hd] += dmn
            dgates = dgates + dg
        dproj_ref[:, pl.ds(8 * D_GRP, LANES)] = dgates
        dh, dx, dw0, dw1, dw2, dw3, db = conv_vjp(dqk[...])
        tail = jnp.concatenate([jnp.zeros((CHUNK - SUBLANES, 2 * D_GRP), F32), dhalo[...]], axis=0)
        dproj_ref[:, pl.ds(4 * D_GRP, 2 * D_GRP)] = dx + tail
        dhalo[...] = dh
        dcw_ref[0:1, :] += dw0
        dcw_ref[1:2, :] += dw1
        dcw_ref[2:3, :] += dw2
        dcw_ref[3:4, :] += dw3
        dcb_ref[...] += db

    small_out = [pl.BlockSpec((2, D_GRP), lambda c: (0, 0)), pl.BlockSpec((1, D_GRP), lambda c: (0, 0)),
                 pl.BlockSpec((ML_CONV, 2 * D_GRP), lambda c: (0, 0)), pl.BlockSpec((1, 2 * D_GRP), lambda c: (0, 0)),
                 pl.BlockSpec((1, D_GRP), lambda c: (0, 0))]
    vmem = 2 * (2 * _nbytes((CHUNK, D_IN_PAD), F32) + _nbytes((CHUNK, 2 * D_GRP), F32)
                + 2 * _nbytes((HEADS, DK, DK), F32)) + 2 * _nbytes((HEADS, DK, DK), F32) + 4 * 1024 * 1024
    return _pcall(
        body, name="mixer_bwd", grid=(n_chunks,),
        in_specs=[proj_spec, halo_spec, y_spec] + state_specs + small,
        out_specs=[proj_spec] + small_out,
        out_shape=[jax.ShapeDtypeStruct((seq, D_IN_PAD), F32), jax.ShapeDtypeStruct((2, D_GRP), F32),
                   jax.ShapeDtypeStruct((1, D_GRP), F32), jax.ShapeDtypeStruct((ML_CONV, 2 * D_GRP), F32),
                   jax.ShapeDtypeStruct((1, 2 * D_GRP), F32), jax.ShapeDtypeStruct((1, D_GRP), F32)],
        scratch_shapes=[pltpu.VMEM((HEADS, DK, DK), F32), pltpu.VMEM((HEADS, DK, DK), F32),
                        pltpu.VMEM((HEADS, 1, DK), F32), pltpu.VMEM((HEADS, 1, DK), F32),
                        pltpu.VMEM((SUBLANES, 2 * D_GRP), F32), pltpu.VMEM((CHUNK, 2 * D_GRP), F32)],
        compiler_params=_params(("arbitrary",), vmem),
    )(proj, proj, dy, hst, cst, nst, mst, lb_logits, hg_nw, conv_w, conv_b, ml_nw)


def _heads(x):
    return [x[:, h * DK:(h + 1) * DK] for h in range(HEADS)]


def _last(x, j):
    lane = lax.broadcasted_iota(jnp.int32, (1, x.shape[-1]), 1)
    return jnp.sum(jnp.where(lane == j, x, 0.0), axis=-1, keepdims=True)


def _hg_chunk(st_t, hq, hf, hi, hgate, l0, l1, nw):
    n = hq.shape[0]
    lb = _sigmoid(l0 - l1)
    q = _silu(hq)
    lf = jnp.log(lb + (1.0 - lb) * _sigmoid(hf))
    k = (1.0 - lb) * _sigmoid(-hf)
    b = cumsum_rows(lf)
    b_ref = _pick_row(b, n // 2 - 1)
    b_last = _pick_row(b, n - 1)
    qa, ka = _heads(q * jnp.exp(b - b_ref)), _heads(k * jnp.exp(b_ref - b))
    qe, kd, eb, v = _heads(q * jnp.exp(b)), _heads(k * jnp.exp(b_last - b)), _heads(jnp.exp(b_last)), _heads(hi)
    tri = _tri(n, True) > 0
    attn = [jnp.where(tri, mm_nt(qa[h], ka[h]), 0.0) for h in range(HEADS)]
    o = [mm_nn(attn[h], v[h]) + mm_nt(qe[h], st_t[h]) for h in range(HEADS)]
    st_new = jnp.stack([eb[h] * st_t[h] + mm_tn(v[h], kd[h]) for h in range(HEADS)])
    yn = [o[h] * lax.rsqrt(jnp.mean(o[h] * o[h], axis=-1, keepdims=True) + LN_EPS) for h in range(HEADS)]
    return st_new, jnp.concatenate(yn, axis=1) * nw * _silu(hgate)


def _ml_chunk(c_st, n_st, m_st, q, k, v, gates, og, nw):
    n = q.shape[0]
    ig = jnp.stack([_last(gates, h) for h in range(HEADS)])
    fl = _log_sigmoid(jnp.stack([_last(gates, HEADS + h) for h in range(HEADS)]))
    bw = cumsum_rows(jnp.concatenate([jnp.broadcast_to(fl[h], (n, DK)) for h in range(HEADS)], axis=1))
    b = jnp.stack([_last(x, 0) for x in _heads(bw)])
    g = jnp.sum(fl, axis=1, keepdims=True)
    eye = lax.broadcasted_iota(jnp.int32, (n, n), 0) == lax.broadcasted_iota(jnp.int32, (n, n), 1)
    e_row = jnp.sum(jnp.where(eye, ig - b, 0.0), axis=1, keepdims=True)
    d = jnp.where(_tri(n, True) > 0, b + e_row, -jnp.inf)
    inter = b + m_st
    m_t = jnp.maximum(inter, jnp.max(d, axis=2, keepdims=True))
    qs, kh, vh = _heads(q * (DK ** -0.5)), _heads(k), _heads(v)
    s = jnp.stack([mm_nt(qs[h], kh[h]) for h in range(HEADS)]) * jnp.exp(d - m_t)
    w_inter = jnp.exp(inter - m_t)
    num = (jnp.stack([mm_nn(s[h], vh[h]) for h in range(HEADS)])
           + w_inter * jnp.stack([mm_nn(qs[h], c_st[h]) for h in range(HEADS)]))
    den = jnp.sum(s, axis=2, keepdims=True) + w_inter * jnp.sum(jnp.stack(qs) * n_st, axis=2, keepdims=True)
    h_out = num / jnp.maximum(jnp.abs(den), jnp.exp(-m_t))
    a = g - b + ig
    m_new = jnp.maximum(g + m_st, jnp.max(a, axis=1, keepdims=True))
    decay = jnp.exp(g + m_st - m_new)
    wk = jnp.stack(kh) * jnp.exp(a - m_new)
    c_new = decay * c_st + jnp.stack([mm_tn(wk[h], vh[h]) for h in range(HEADS)])
    n_new = decay * n_st + jnp.sum(wk, axis=1, keepdims=True)
    hc = h_out - jnp.mean(h_out, axis=-1, keepdims=True)
    yn = hc * lax.rsqrt(jnp.mean(hc * hc, axis=-1, keepdims=True) + LN_EPS)
    y = _sigmoid(og) * (jnp.concatenate([yn[h] for h in range(HEADS)], axis=1) * nw)
    return c_new, n_new, m_new, y


def _mixer_inputs(proj_ref, lg_ref, hnw_ref, mnw_ref, qk):
    hg_in = (proj_ref[:, _grp(0)], proj_ref[:, _grp(1)], proj_ref[:, _grp(2)], proj_ref[:, _grp(3)],
             lg_ref[0:1, :], lg_ref[1:2, :], hnw_ref[...])
    ml_in = (qk[:, :D_GRP], qk[:, D_GRP:], proj_ref[:, _grp(6)], proj_ref[:, pl.ds(8 * D_GRP, LANES)],
             proj_ref[:, _grp(7)], mnw_ref[...])
    return hg_in, ml_in


def _mixer_fwd(proj, lb_logits, hg_nw, conv_w, conv_b, ml_nw):
    seq = proj.shape[0]
    n_chunks = seq // CHUNK
    proj_spec, halo_spec, small, state_specs, y_spec, _ = _mixer_specs(n_chunks, False)

    def body(proj_ref, halo_ref, lg_ref, hnw_ref, cw_ref, cb_ref, mnw_ref,
             y_ref, hst_ref, cst_ref, nst_ref, mst_ref, hs, cs, ns, ms):
        c = pl.program_id(0)

        @pl.when(c == 0)
        def _():
            hs[...] = jnp.zeros_like(hs)
            cs[...] = jnp.zeros_like(cs)
            ns[...] = jnp.zeros_like(ns)
            ms[...] = jnp.full(ms.shape, NEG_BIG, F32)

        hst_ref[0] = hs[...]
        cst_ref[0] = cs[...]
        nst_ref[0] = ns[...]
        mst_ref[0] = ms[...]
        halo = jnp.where(c > 0, halo_ref[...], 0.0)
        qk = _qk_conv(halo, proj_ref[:, pl.ds(4 * D_GRP, 2 * D_GRP)],
                      cw_ref[0:1, :], cw_ref[1:2, :], cw_ref[2:3, :], cw_ref[3:4, :], cb_ref[...])
        hg_in, ml_in = _mixer_inputs(proj_ref, lg_ref, hnw_ref, mnw_ref, qk)
        hs[...], y_hg = _hg_chunk(hs[...], *hg_in)
        cs[...], ns[...], m_new, y_ml = _ml_chunk(cs[...], ns[...], _last(ms[...], 0), *ml_in)
        ms[...] = jnp.broadcast_to(m_new, ms.shape)
        y_ref[:, pl.ds(0, D_GRP)] = y_hg.astype(BF16)
        y_ref[:, pl.ds(D_GRP, D_GRP)] = y_ml.astype(BF16)

    st = jax.ShapeDtypeStruct((n_chunks, HEADS, DK, DK), F32)
    vec = jax.ShapeDtypeStruct((n_chunks, HEADS, 1, DK), F32)
    vmem = 2 * (_nbytes((CHUNK, D_IN_PAD), F32) + _nbytes((CHUNK, 2 * D_GRP), F32) + 2 * _nbytes((HEADS, DK, DK), F32)) \
        + 2 * _nbytes((HEADS, DK, DK), F32)
    return _pcall(
        body, name="mixer_fwd", grid=(n_chunks,),
        in_specs=[proj_spec, halo_spec] + small,
        out_specs=[y_spec] + state_specs,
        out_shape=[jax.ShapeDtypeStruct((seq, 2 * D_GRP), BF16), st, st, vec, vec],
        scratch_shapes=[pltpu.VMEM((HEADS, DK, DK), F32), pltpu.VMEM((HEADS, DK, DK), F32),
                        pltpu.VMEM((HEADS, 1, DK), F32), pltpu.VMEM((HEADS, 1, DK), F32)],
        compiler_params=_params(("arbitrary",), vmem),
    )(proj, proj, lb_logits, hg_nw, conv_w, conv_b, ml_nw)


def _mixer_bwd(proj, dy, hst, cst, nst, mst, lb_logits, hg_nw, conv_w, conv_b, ml_nw):
    seq = proj.shape[0]
    n_chunks = seq // CHUNK
    proj_spec, halo_spec, small, state_specs, y_spec, _ = _mixer_specs(n_chunks, True)

    def body(proj_ref, halo_ref, dy_ref, hst_ref, cst_ref, nst_ref, mst_ref,
             lg_ref, hnw_ref, cw_ref, cb_ref, mnw_ref,
             dproj_ref, dbin_ref, dlg_ref, dhnw_ref, dcw_ref, dcb_ref, dmnw_ref,
             dhs, dcs, dns, dms, dhalo):
        c = pl.program_id(0)

        @pl.when(c == 0)
        def _():
            for r in (dhs, dcs, dns, dms, dhalo, dbin_ref, dlg_ref, dhnw_ref, dcw_ref, dcb_ref, dmnw_ref):
                r[...] = jnp.zeros_like(r)

        def put(cols, val):
            dproj_ref[:, cols] = val.astype(BF16)
            dbin_ref[:, cols] += jnp.sum(val, axis=0, keepdims=True)

        first = c == n_chunks - 1
        halo = jnp.where(first, 0.0, halo_ref[...])
        x_qk = proj_ref[:, pl.ds(4 * D_GRP, 2 * D_GRP)]
        conv_args = (halo, x_qk, cw_ref[0:1, :], cw_ref[1:2, :], cw_ref[2:3, :], cw_ref[3:4, :], cb_ref[...])
        qk, conv_vjp = jax.vjp(_qk_conv, *conv_args)
        hg_in, ml_in = _mixer_inputs(proj_ref, lg_ref, hnw_ref, mnw_ref, qk)
        _, hg_vjp = jax.vjp(_hg_chunk, hst_ref[0], *hg_in)
        _, ml_vjp = jax.vjp(_ml_chunk, cst_ref[0], nst_ref[0], _last(mst_ref[0], 0), *ml_in)
        dst, dhq, dhf, dhi, dhg, dl0, dl1, dnw = hg_vjp((dhs[...], dy_ref[:, pl.ds(0, D_GRP)]))
        dc, dn, dm, dq, dk, dv, dgates, dog, dmn = ml_vjp(
            (dcs[...], dns[...], _last(dms[...], 0), dy_ref[:, pl.ds(D_GRP, D_GRP)]))
        dhs[...] = dst
        dcs[...] = dc
        dns[...] = dn
        dms[...] = jnp.broadcast_to(dm, dms.shape)
        for i, val in ((0, dhq), (1, dhf), (2, dhi), (3, dhg), (6, dv), (7, dog)):
            put(_grp(i), val)
        put(pl.ds(8 * D_GRP, LANES), dgates)
        dlg_ref[0:1, :] += dl0
        dlg_ref[1:2, :] += dl1
        dhnw_ref[...] += dnw
        dmnw_ref[...] += dmn
        dh, dx, dw0, dw1, dw2, dw3, db = conv_vjp(jnp.concatenate([dq, dk], axis=1))
        tail = jnp.concatenate([jnp.zeros((CHUNK - SUBLANES, 2 * D_GRP), F32), dhalo[...]], axis=0)
        put(pl.ds(4 * D_GRP, 2 * D_GRP), dx + tail)
        dhalo[...] = dh
        for d, dw in enumerate((dw0, dw1, dw2, dw3)):
            dcw_ref[d:d + 1, :] += dw
        dcb_ref[...] += db

    row = pl.BlockSpec((1, D_GRP), lambda c: (0, 0))
    small_out = [pl.BlockSpec((1, D_IN_PAD), lambda c: (0, 0)), pl.BlockSpec((2, D_GRP), lambda c: (0, 0)), row,
                 pl.BlockSpec((ML_CONV, 2 * D_GRP), lambda c: (0, 0)), pl.BlockSpec((1, 2 * D_GRP), lambda c: (0, 0)), row]
    dy_spec = pl.BlockSpec((CHUNK, 2 * D_GRP), y_spec.index_map)
    vmem = 2 * (2 * _nbytes((CHUNK, D_IN_PAD), F32) + _nbytes((CHUNK, 2 * D_GRP), F32)
                + 2 * _nbytes((HEADS, DK, DK), F32)) + 2 * _nbytes((HEADS, DK, DK), F32) + 4 * 1024 * 1024
    return _pcall(
        body, name="mixer_bwd", grid=(n_chunks,),
        in_specs=[proj_spec, halo_spec, dy_spec] + state_specs + small,
        out_specs=[proj_spec] + small_out,
        out_shape=[jax.ShapeDtypeStruct((seq, D_IN_PAD), BF16), jax.ShapeDtypeStruct((1, D_IN_PAD), F32),
                   jax.ShapeDtypeStruct((2, D_GRP), F32), jax.ShapeDtypeStruct((1, D_GRP), F32),
                   jax.ShapeDtypeStruct((ML_CONV, 2 * D_GRP), F32), jax.ShapeDtypeStruct((1, 2 * D_GRP), F32),
                   jax.ShapeDtypeStruct((1, D_GRP), F32)],
        scratch_shapes=[pltpu.VMEM((HEADS, DK, DK), F32), pltpu.VMEM((HEADS, DK, DK), F32),
                        pltpu.VMEM((HEADS, 1, DK), F32), pltpu.VMEM((HEADS, 1, DK), F32),
                        pltpu.VMEM((SUBLANES, 2 * D_GRP), F32)],
        compiler_params=_params(("arbitrary",), vmem),
    )(proj, proj, dy, hst, cst, nst, mst, lb_logits, hg_nw, conv_w, conv_b, ml_nw)


def _tile(n, prefs, unit=None):
    unit = unit or n
    for p in prefs:
        if unit % p == 0 and n % p == 0:
            return p
    return unit


def _logical(arr):
    return arr.shape if arr.ndim == 2 else (arr.shape[1], arr.shape[0] * arr.shape[2])


def _group(arr):
    return arr.shape[-1]


def _split_spec(ndim, group, tr, tc, where):
    if ndim == 2:
        return pl.BlockSpec((tr, tc), where)
    per = group // tc
    assert per * tc == group, (group, tc)

    def index(*ids):
        bi, bj = where(*ids)
        return (bj // per, bi, bj % per)
    return pl.BlockSpec((None, tr, tc), index)


def _mm(name, mode, a, b, *, bias=None, res=None, res_scale=1.0, ln=None, out_dtype=F32, out_groups=None,
        copy_dtype=None, tm=None, tn=None, tk=None):
    la, lb = _logical(a), _logical(b)
    if mode == "nn":
        (m, k), n = la, lb[1]
        n_unit = _group(b) if b.ndim == 3 else n
        kc = _group(a) if a.ndim == 3 else k
    elif mode == "nt":
        (m, k), n = la, lb[0]
        n_unit = n
        kc = min(_group(a) if a.ndim == 3 else k, _group(b) if b.ndim == 3 else k)
    else:
        (k, m), n = la, lb[1]
        n_unit, kc = (_group(b) if b.ndim == 3 else n), k
        assert a.ndim == 2
    if out_groups:
        n_unit = min(n_unit, n // out_groups)
    kind = ln[0] if ln else None
    tm = tm or (256 if ln else _tile(m, (512, 256, 128)))
    tn = n if ln else (tn or _tile(n, (512, 384, 256, 128), n_unit))
    tk = (tk or _tile(k, (2048, 512, 256, 128))) if mode == "tn" else k
    gi, gj, gk = m // tm, n // tn, k // tk
    assert gi * tm == m and gj * tn == n and gk * tk == k and n_unit % tn == 0, (name, m, n, k, tm, tn, tk)
    ca, cb = {"nn": (1, 0), "nt": (1, 1), "tn": (0, 0)}[mode]
    i_outer = gk > 1 or (gi - 1) * _nbytes(b.shape, b.dtype) <= (gj - 1) * _nbytes(a.shape, a.dtype)

    def ij(where):
        return (lambda p, q, kk: where(p, q, kk)) if i_outer else (lambda p, q, kk: where(q, p, kk))
    if mode == "tn":
        a_spec = pl.BlockSpec((tk, tm), ij(lambda i, j, kk: (kk, i)))
    elif a.ndim == 3:
        a_spec = pl.BlockSpec((a.shape[0], tm, _group(a)), ij(lambda i, j, kk: (0, i, 0)))
    else:
        a_spec = pl.BlockSpec((tm, k), ij(lambda i, j, kk: (i, 0)))
    if mode != "nt":
        b_spec = _split_spec(b.ndim, _group(b), tk, tn, ij(lambda i, j, kk: (kk, j)))
    elif b.ndim == 3:
        b_spec = pl.BlockSpec((b.shape[0], tn, _group(b)), ij(lambda i, j, kk: (0, j, 0)))
    else:
        b_spec = pl.BlockSpec((tn, k), ij(lambda i, j, kk: (j, 0)))
    row_spec = pl.BlockSpec((1, tn), ij(lambda i, j, kk: (0, j)))
    blk_spec = pl.BlockSpec((tm, tn), ij(lambda i, j, kk: (i, j)))
    ins, in_specs = [a, b], [a_spec, b_spec]
    if bias is not None:
        ins.append(bias), in_specs.append(row_spec)
    if res is not None:
        ins.append(res), in_specs.append(blk_spec)
    if kind == "fwd":
        ins += [ln[1], ln[2]]
        in_specs += [row_spec, row_spec]
    elif kind == "loss":
        ins += [ln[1], ln[2], ln[3]]
        in_specs += [row_spec, row_spec, blk_spec]
    elif kind == "bwd":
        ins += [ln[1], ln[2], ln[3]]
        in_specs += [blk_spec, row_spec, row_spec]
    if out_groups:
        blk_out = jax.ShapeDtypeStruct((out_groups, m, n // out_groups), out_dtype)
        out_spec = _split_spec(3, n // out_groups, tm, tn, ij(lambda i, j, kk: (i, j)))
    else:
        blk_out, out_spec = jax.ShapeDtypeStruct((m, n), out_dtype), blk_spec
    row_out = jax.ShapeDtypeStruct((1, n), F32)
    if kind is None:
        out_shape, out_specs = [blk_out], [out_spec]
    elif kind == "fwd":
        out_shape, out_specs = [blk_out, blk_out], [blk_spec, blk_spec]
    else:
        out_shape, out_specs = [blk_out, row_out, row_out], [blk_spec, row_spec, row_spec]
        if kind == "loss":
            out_shape.append(jax.ShapeDtypeStruct((1, LANES), F32))
            out_specs.append(pl.BlockSpec((1, LANES), lambda p, q, kk: (0, 0)))
    if copy_dtype is not None:
        out_shape.append(jax.ShapeDtypeStruct((m, n), copy_dtype))
        out_specs.append(blk_spec)
    n_in = len(ins)

    def body(*refs):
        in_refs, out_refs, acc_ref = refs[:n_in], refs[n_in:n_in + len(out_shape)], refs[-1]
        i, kk = pl.program_id(0 if i_outer else 1), pl.program_id(2)
        a_ref, b_ref = in_refs[:2]
        extra = list(in_refs[2:])

        def epilogue(acc):
            rest = list(extra)
            if bias is not None:
                acc = acc + rest.pop(0)[...]
            if res is not None:
                acc = acc + res_scale * rest.pop(0)[...]
            if kind is None:
                out_refs[0][...] = acc.astype(out_dtype)
                return
            if kind == "fwd":
                out_refs[0][...] = acc
                y = _layer_norm(acc, rest[0][...], rest[1][...])
                out_refs[1][...] = y
                if copy_dtype is not None:
                    out_refs[-1][...] = y.astype(copy_dtype)
                return
            if kind == "loss":
                y, vjp = jax.vjp(_layer_norm, acc, rest[0][...], rest[1][...])
                err = y - rest[2][...]
                part = 0.5 * jnp.sum(jnp.sum(err * err, axis=1, keepdims=True), axis=0, keepdims=True) / n
                dz, dg, db = vjp(err / n)
            else:
                _, vjp = jax.vjp(_layer_norm, rest[0][...], rest[1][...], rest[2][...])
                dz, dg, db = vjp(acc)

            @pl.when(i == 0)
            def _():
                for r in out_refs[1:3 + (kind == "loss")]:
                    r[...] = jnp.zeros_like(r)

            out_refs[0][...] = dz
            out_refs[1][...] += dg
            out_refs[2][...] += db
            if kind == "loss":
                out_refs[3][...] += jnp.broadcast_to(part, (1, LANES))
            if copy_dtype is not None:
                out_refs[-1][...] = dz.astype(copy_dtype)

        def chunk(ref, c0, last):
            if ref.ndim == 3:
                g = ref.shape[2]
                return ref[c0 // g, :, pl.ds(c0 % g, kc)]
            return ref[:, pl.ds(c0, kc)] if last else ref[pl.ds(c0, kc), :]

        if mode == "tn" or kc == k:
            prod = _dg(a_ref[...], b_ref[...], ca, cb)
        else:
            prod = None
            for c0 in range(0, k, kc):
                part = _dg(chunk(a_ref, c0, True), chunk(b_ref, c0, mode == "nt"), ca, cb)
                prod = part if prod is None else prod + part
        if gk == 1:
            epilogue(prod)
            return

        @pl.when(kk == 0)
        def _():
            acc_ref[...] = prod

        @pl.when(kk > 0)
        def _():
            acc_ref[...] += prod

        @pl.when(kk == gk - 1)
        def _():
            epilogue(acc_ref[...])

    vmem = (2 * (_nbytes((tm, tk), a.dtype) + _nbytes((tk, tn), b.dtype))
            + (2 * len(ins) + 2 * len(out_shape) + 1) * _nbytes((tm, tn), F32))
    outs = _pcall(
        body, name=name, grid=(gi, gj, gk) if i_outer else (gj, gi, gk), in_specs=in_specs, out_specs=out_specs,
        out_shape=out_shape, scratch_shapes=[pltpu.VMEM((tm, tn) if gk > 1 else (SUBLANES, LANES), F32)],
        compiler_params=_params(("arbitrary", "arbitrary", "arbitrary"), vmem),
    )(*ins)
    return outs[0] if (kind is None and copy_dtype is None) else outs


def _colsum(name, a):
    m, n = a.shape
    tm = _tile(m, (512, 256, 128))

    def body(a_ref, o_ref):
        @pl.when(pl.program_id(0) == 0)
        def _():
            o_ref[...] = jnp.zeros_like(o_ref)

        o_ref[...] += jnp.sum(a_ref[...].astype(F32), axis=0, keepdims=True)

    return _pcall(
        body, name=name, grid=(m // tm,), in_specs=[pl.BlockSpec((tm, n), lambda i: (i, 0))],
        out_specs=pl.BlockSpec((1, n), lambda i: (0, 0)), out_shape=jax.ShapeDtypeStruct((1, n), F32),
        compiler_params=_params(("arbitrary",), 2 * _nbytes((tm, n), a.dtype)),
    )(a)


def _attn_head(q, k, v):
    sc = mm_nt(q, k) * (CA_DH ** -0.5)
    e = jnp.exp(sc - jnp.max(sc, axis=-1, keepdims=True))
    return mm_nn(e / jnp.sum(e, axis=-1, keepdims=True), v)


def _attn_fwd(q, kv):
    seq, n_mem = q.shape[0], kv.shape[0]
    tq = _tile(seq, (512, 256, 128))

    def body(q_ref, kv_ref, o_ref):
        for h in range(HEADS):
            hd = pl.ds(h * CA_DH, CA_DH)
            o = _attn_head(q_ref[:, hd], kv_ref[:, hd], kv_ref[:, pl.ds(D_MODEL + h * CA_DH, CA_DH)])
            o_ref[:, hd] = o.astype(BF16)

    return _pcall(
        body, name="attn_fwd", grid=(seq // tq,),
        in_specs=[pl.BlockSpec((tq, D_MODEL), lambda i: (i, 0)), pl.BlockSpec((n_mem, 2 * D_MODEL), lambda i: (0, 0))],
        out_specs=pl.BlockSpec((tq, D_MODEL), lambda i: (i, 0)), out_shape=jax.ShapeDtypeStruct((seq, D_MODEL), BF16),
        compiler_params=_params(("arbitrary",), 4 * _nbytes((tq, D_MODEL), F32) + 2 * _nbytes((n_mem, 2 * D_MODEL), F32)),
    )(q, kv)


def _attn_bwd(q, kv, do):
    seq, n_mem = q.shape[0], kv.shape[0]
    tq = _tile(seq, (512, 256, 128))

    def body(q_ref, kv_ref, do_ref, dq_ref, dkv_ref):
        @pl.when(pl.program_id(0) == 0)
        def _():
            dkv_ref[...] = jnp.zeros_like(dkv_ref)

        for h in range(HEADS):
            hd = pl.ds(h * CA_DH, CA_DH)
            vd = pl.ds(D_MODEL + h * CA_DH, CA_DH)
            _, vjp = jax.vjp(_attn_head, q_ref[:, hd], kv_ref[:, hd], kv_ref[:, vd])
            dq, dk, dv = vjp(do_ref[:, hd].astype(F32))
            dq_ref[:, hd] = dq.astype(BF16)
            dkv_ref[:, hd] += dk
            dkv_ref[:, vd] += dv

    return _pcall(
        body, name="attn_bwd", grid=(seq // tq,),
        in_specs=[pl.BlockSpec((tq, D_MODEL), lambda i: (i, 0)), pl.BlockSpec((n_mem, 2 * D_MODEL), lambda i: (0, 0)),
                  pl.BlockSpec((tq, D_MODEL), lambda i: (i, 0))],
        out_specs=[pl.BlockSpec((tq, D_MODEL), lambda i: (i, 0)), pl.BlockSpec((n_mem, 2 * D_MODEL), lambda i: (0, 0))],
        out_shape=[jax.ShapeDtypeStruct((seq, D_MODEL), BF16), jax.ShapeDtypeStruct((n_mem, 2 * D_MODEL), F32)],
        compiler_params=_params(("arbitrary",), 6 * _nbytes((tq, D_MODEL), F32) + 4 * _nbytes((n_mem, 2 * D_MODEL), F32)),
    )(q, kv, do)


FFN_TB = 256
FFN_TC = 256


def _ffn_mid(hg, xg, hv, xv, wg0, wg1, wg2, bg, wv0, wv1, wv2, bv):
    return jax.nn.gelu(causal_conv(hg, xg, (wg0, wg1, wg2), bg)) * causal_conv(hv, xv, (wv0, wv1, wv2), bv)


def _ffn_specs(seq, reverse):
    tb = min(FFN_TB, seq)
    nt = seq // tb
    row8 = tb // SUBLANES

    def tt(t):
        return nt - 1 - t if reverse else t
    nj = D_FF // FFN_TC
    main = pl.BlockSpec((tb, FFN_TC), lambda j, t: (tt(t), j))
    ins = []
    for off in (0, nj):
        ins += [pl.BlockSpec((tb, FFN_TC), lambda j, t, off=off: (tt(t), j + off)),
                pl.BlockSpec((SUBLANES, FFN_TC), lambda j, t, off=off: (jnp.maximum(tt(t) * row8 - 1, 0), j + off))]
    for off in (0, nj):
        ins += [pl.BlockSpec((FFN_CONV, FFN_TC), lambda j, t, off=off: (0, j + off)),
                pl.BlockSpec((1, FFN_TC), lambda j, t, off=off: (0, j + off))]
    return tb, nt, main, ins


def _ffn_args(c_first, ug, hg, uv, hv, wg, bg, wv, bv):
    halo_g = jnp.where(c_first, 0.0, hg[...])
    halo_v = jnp.where(c_first, 0.0, hv[...])
    return (halo_g, ug[...], halo_v, uv[...], wg[0:1, :], wg[1:2, :], wg[2:3, :], bg[...],
            wv[0:1, :], wv[1:2, :], wv[2:3, :], bv[...])


def _ffn_mid_fwd(u, conv_w, conv_b):
    seq = u.shape[0]
    tb, nt, main, ins = _ffn_specs(seq, False)

    def body(ug, hg, uv, hv, wg, bg, wv, bv, o_ref):
        o_ref[...] = _ffn_mid(*_ffn_args(pl.program_id(1) == 0, ug, hg, uv, hv, wg, bg, wv, bv)).astype(BF16)

    return _pcall(
        body, name="ffn_mid_fwd", grid=(D_FF // FFN_TC, nt), in_specs=ins, out_specs=main,
        out_shape=jax.ShapeDtypeStruct((seq, D_FF), BF16),
        compiler_params=_params(("arbitrary", "arbitrary"), 12 * _nbytes((tb, FFN_TC), F32)),
    )(u, u, u, u, conv_w, conv_b, conv_w, conv_b)


def _ffn_mid_bwd(u, conv_w, conv_b, dh):
    seq = u.shape[0]
    tb, nt, main, ins = _ffn_specs(seq, True)

    def body(ug, hg, uv, hv, wg, bg, wv, bv, dh_ref, du, dw, db, carry):
        t = pl.program_id(1)

        @pl.when(t == 0)
        def _():
            for r in (dw, db, carry):
                r[...] = jnp.zeros_like(r)

        _, vjp = jax.vjp(_ffn_mid, *_ffn_args(t == nt - 1, ug, hg, uv, hv, wg, bg, wv, bv))
        dhg, dxg, dhv, dxv, g0, g1, g2, gb, v0, v1, v2, vb = vjp(dh_ref[...])
        zeros = jnp.zeros((tb - SUBLANES, FFN_TC), F32)
        du[0] = (dxg + jnp.concatenate([zeros, carry[0]], axis=0)).astype(BF16)
        du[1] = (dxv + jnp.concatenate([zeros, carry[1]], axis=0)).astype(BF16)
        carry[0] = dhg
        carry[1] = dhv
        for half, parts in enumerate(((g0, g1, g2), (v0, v1, v2))):
            for d, p in enumerate(parts):
                dw[half, d:d + 1, :] += p
        db[0] += gb
        db[1] += vb

    def grouped(rows, index):
        return pl.BlockSpec((2, rows, FFN_TC), index)
    return _pcall(
        body, name="ffn_mid_bwd", grid=(D_FF // FFN_TC, nt), in_specs=ins + [main],
        out_specs=[grouped(tb, lambda j, t: (0, nt - 1 - t, j)), grouped(FFN_CONV, lambda j, t: (0, 0, j)),
                   grouped(1, lambda j, t: (0, 0, j))],
        out_shape=[jax.ShapeDtypeStruct((2, seq, D_FF), BF16), jax.ShapeDtypeStruct((2, FFN_CONV, D_FF), F32),
                   jax.ShapeDtypeStruct((2, 1, D_FF), F32)],
        scratch_shapes=[pltpu.VMEM((2, SUBLANES, FFN_TC), F32)],
        compiler_params=_params(("arbitrary", "arbitrary"), 24 * _nbytes((tb, FFN_TC), F32)),
    )(u, u, u, u, conv_w, conv_b, conv_w, conv_b, dh)


FFN_W = D_FF // 2
FFN_J = D_FF // FFN_W
MXU_COLS = 256
FFN_PIECES = tuple((off, min(MXU_COLS, FFN_W - off)) for off in range(0, FFN_W, MXU_COLS))


def _ffn_common_specs(seq, row):
    tb = min(FFN_TB, seq)
    full = pl.BlockSpec((tb, D_MODEL), lambda t, j: (row(t), 0))
    vec = pl.BlockSpec((1, D_MODEL), lambda t, j: (0, 0))
    halves = []
    for off in (0, FFN_J):
        halves.append(dict(
            w_up=pl.BlockSpec((D_MODEL, FFN_W), lambda t, j, off=off: (0, j + off)),
            taps=pl.BlockSpec((FFN_CONV, FFN_W), lambda t, j, off=off: (0, j + off)),
            bias=pl.BlockSpec((1, FFN_W), lambda t, j, off=off: (0, j + off))))
    w_down = pl.BlockSpec((FFN_W, D_MODEL), lambda t, j: (j, 0))
    u_blk = pl.BlockSpec((2, tb, FFN_W), lambda t, j: (0, row(t), j))
    return tb, full, vec, halves, w_down, u_blk


def _ffn_vmem(tb):
    return (_nbytes((2, tb, FFN_W), F32) + _nbytes((2, tb, FFN_W), BF16) + 3 * _nbytes((D_MODEL, FFN_W), BF16)
            + 10 * _nbytes((tb, D_MODEL), F32))


def _conv_params(taps_ref, bias_ref, cols):
    return taps_ref[0:1, cols], taps_ref[1:2, cols], taps_ref[2:3, cols], bias_ref[:, cols]


def _ffn_fwd(x2b, x2, w_up, conv_w, conv_b, w_down, ln_g, ln_b, target):
    seq = x2.shape[0]
    tb, full, vec, halves, wd_spec, u_blk = _ffn_common_specs(seq, lambda t: t)
    nt = seq // tb

    def body(xb_ref, wg_ref, wv_ref, tg_ref, tv_ref, bg_ref, bv_ref, wd_ref, x_ref, g_ref, b_ref, tgt_ref,
             u_ref, h_ref, dz_ref, dg_ref, db_ref, loss_ref, dzb_ref, acc, carry):
        t, j = pl.program_id(0), pl.program_id(1)
        xb = xb_ref[...]
        pieces = [pl.ds(off, width) for off, width in FFN_PIECES]
        ug = [_dg(xb, wg_ref[:, cols], 1, 0) for cols in pieces]
        uv = [_dg(xb, wv_ref[:, cols], 1, 0) for cols in pieces]
        hs = []
        for cols, g, v in zip(pieces, ug, uv):
            u_ref[0, :, cols] = g
            u_ref[1, :, cols] = v
            halo_g = jnp.where(t == 0, 0.0, carry[j, 0, :, cols])
            halo_v = jnp.where(t == 0, 0.0, carry[j, 1, :, cols])
            h = _ffn_mid(halo_g, g, halo_v, v, *_conv_params(tg_ref, bg_ref, cols),
                         *_conv_params(tv_ref, bv_ref, cols)).astype(BF16)
            carry[j, 0, :, cols] = g[tb - SUBLANES:, :]
            carry[j, 1, :, cols] = v[tb - SUBLANES:, :]
            h_ref[:, cols] = h
            hs.append(h)
        part = None
        for cols, h in zip(pieces, hs):
            p = _dg(h, wd_ref[cols, :], 1, 0)
            part = p if part is None else part + p

        @pl.when(j == 0)
        def _():
            acc[...] = part

        @pl.when(j > 0)
        def _():
            acc[...] += part

        @pl.when(j == FFN_J - 1)
        def _():
            y, vjp = jax.vjp(_layer_norm, acc[...] + ALPHA * x_ref[...], g_ref[...], b_ref[...])
            err = y - tgt_ref[...]
            part_loss = 0.5 * jnp.sum(jnp.sum(err * err, axis=1, keepdims=True), axis=0, keepdims=True) / D_MODEL
            dz, dg, db = vjp(err / D_MODEL)

            @pl.when(t == 0)
            def _():
                for r in (dg_ref, db_ref, loss_ref):
                    r[...] = jnp.zeros_like(r)

            dz_ref[...] = dz
            dzb_ref[...] = dz.astype(BF16)
            dg_ref[...] += dg
            db_ref[...] += db
            loss_ref[...] += jnp.broadcast_to(part_loss, (1, LANES))

    h0, h1 = halves
    row = jax.ShapeDtypeStruct((1, D_MODEL), F32)
    return _pcall(
        body, name="ffn_fwd", grid=(nt, FFN_J),
        in_specs=[full, h0["w_up"], h1["w_up"], h0["taps"], h1["taps"], h0["bias"], h1["bias"], wd_spec, full, vec, vec,
                  full],
        out_specs=[u_blk, pl.BlockSpec((tb, FFN_W), lambda t, j: (t, j)), full, vec, vec,
                   pl.BlockSpec((1, LANES), lambda t, j: (0, 0)), full],
        out_shape=[jax.ShapeDtypeStruct((2, seq, D_FF), F32), jax.ShapeDtypeStruct((seq, D_FF), BF16),
                   jax.ShapeDtypeStruct((seq, D_MODEL), F32), row, row, jax.ShapeDtypeStruct((1, LANES), F32),
                   jax.ShapeDtypeStruct((seq, D_MODEL), BF16)],
        scratch_shapes=[pltpu.VMEM((tb, D_MODEL), F32), pltpu.VMEM((FFN_J, 2, SUBLANES, FFN_W), F32)],
        compiler_params=_params(("arbitrary", "arbitrary"), _ffn_vmem(tb)),
    )(x2b, w_up, w_up, conv_w, conv_w, conv_b, conv_b, w_down, x2, ln_g, ln_b, target)


def _ffn_bwd(u, conv_w, conv_b, dz3b, dz3, w_down, w_up, z2, ln_g, ln_b):
    seq = dz3.shape[0]
    tb = min(FFN_TB, seq)
    nt = seq // tb
    row8 = tb // SUBLANES
    tb, full, vec, halves, wd_spec, u_blk = _ffn_common_specs(seq, lambda t: nt - 1 - t)
    halo = pl.BlockSpec((2, SUBLANES, FFN_W), lambda t, j: (0, jnp.maximum((nt - 1 - t) * row8 - 1, 0), j))

    def body(u_ref, halo_ref, tg_ref, tv_ref, bg_ref, bv_ref, dzb_ref, wd_ref, wg_ref, wv_ref, dz3_ref, z_ref, g_ref,
             b_ref, du_ref, dw_ref, dbias_ref, dz_ref, dg_ref, db_ref, dz2b_ref, acc, carry):
        t, j = pl.program_id(0), pl.program_id(1)

        @pl.when((t == 0) & (j == 0))
        def _():
            for r in (dw_ref, dbias_ref, dg_ref, db_ref):
                r[...] = jnp.zeros_like(r)

        pieces = [pl.ds(off, width) for off, width in FFN_PIECES]
        dzb = dzb_ref[...]
        dhs = [_dg(dzb, wd_ref[cols, :], 1, 1) for cols in pieces]
        first = t == nt - 1
        dus = []
        for cols, dh in zip(pieces, dhs):
            args = (jnp.where(first, 0.0, halo_ref[0, :, cols]), u_ref[0, :, cols],
                    jnp.where(first, 0.0, halo_ref[1, :, cols]), u_ref[1, :, cols],
                    *_conv_params(tg_ref, bg_ref, cols), *_conv_params(tv_ref, bv_ref, cols))
            _, vjp = jax.vjp(_ffn_mid, *args)
            dhg, dxg, dhv, dxv, g0, g1, g2, gb, v0, v1, v2, vb = vjp(dh)
            zeros = jnp.zeros((tb - SUBLANES, dh.shape[1]), F32)
            dug = (dxg + jnp.concatenate([zeros, jnp.where(t == 0, 0.0, carry[j, 0, :, cols])], axis=0)).astype(BF16)
            duv = (dxv + jnp.concatenate([zeros, jnp.where(t == 0, 0.0, carry[j, 1, :, cols])], axis=0)).astype(BF16)
            carry[j, 0, :, cols] = dhg
            carry[j, 1, :, cols] = dhv
            du_ref[0, :, cols] = dug
            du_ref[1, :, cols] = duv
            for half, parts in enumerate(((g0, g1, g2), (v0, v1, v2))):
                for d, p in enumerate(parts):
                    dw_ref[j, half, d:d + 1, cols] += p
            dbias_ref[j, 0, :, cols] += gb
            dbias_ref[j, 1, :, cols] += vb
            dus.append((dug, duv))
        part = None
        for cols, (dug, duv) in zip(pieces, dus):
            p = _dg(dug, wg_ref[:, cols], 1, 1) + _dg(duv, wv_ref[:, cols], 1, 1)
            part = p if part is None else part + p

        @pl.when(j == 0)
        def _():
            acc[...] = part

        @pl.when(j > 0)
        def _():
            acc[...] += part

        @pl.when(j == FFN_J - 1)
        def _():
            _, ln_vjp = jax.vjp(_layer_norm, z_ref[...], g_ref[...], b_ref[...])
            dz, dg, db = ln_vjp(acc[...] + ALPHA * dz3_ref[...])
            dz_ref[...] = dz
            dz2b_ref[...] = dz.astype(BF16)
            dg_ref[...] += dg
            db_ref[...] += db

    h0, h1 = halves
    row = jax.ShapeDtypeStruct((1, D_MODEL), F32)
    whole = lambda *shape: pl.BlockSpec(shape, lambda t, j: (0,) * len(shape))
    return _pcall(
        body, name="ffn_bwd", grid=(nt, FFN_J),
        in_specs=[u_blk, halo, h0["taps"], h1["taps"], h0["bias"], h1["bias"], full, wd_spec, h0["w_up"], h1["w_up"],
                  full, full, vec, vec],
        out_specs=[u_blk, whole(FFN_J, 2, FFN_CONV, FFN_W), whole(FFN_J, 2, 1, FFN_W), full, vec, vec, full],
        out_shape=[jax.ShapeDtypeStruct((2, seq, D_FF), BF16), jax.ShapeDtypeStruct((FFN_J, 2, FFN_CONV, FFN_W), F32),
                   jax.ShapeDtypeStruct((FFN_J, 2, 1, FFN_W), F32), jax.ShapeDtypeStruct((seq, D_MODEL), F32), row, row,
                   jax.ShapeDtypeStruct((seq, D_MODEL), BF16)],
        scratch_shapes=[pltpu.VMEM((tb, D_MODEL), F32), pltpu.VMEM((FFN_J, 2, SUBLANES, FFN_W), F32)],
        compiler_params=_params(("arbitrary", "arbitrary"), _ffn_vmem(tb)),
    )(u, u, conv_w, conv_w, conv_b, conv_b, dz3b, w_down, w_up, w_up, dz3, z2, ln_g, ln_b)


def _adamw_math(w, g, m, v):
    m_new = ADAM_B1 * m + (1.0 - ADAM_B1) * g
    v_new = ADAM_B2 * v + (1.0 - ADAM_B2) * jnp.square(g)
    m_hat = m_new / (1.0 - ADAM_B1 ** ADAM_STEP)
    v_hat = v_new / (1.0 - ADAM_B2 ** ADAM_STEP)
    return -ADAM_LR * (m_hat / (jnp.sqrt(v_hat) + ADAM_EPS) + ADAM_WD * w), m_new, v_new


def _adamw(name, w, g, m, v):
    rows, cols = w.shape
    tr = _tile(rows, (256, 176, 128, 64, 40, 32, 16, 8))

    def body(w_ref, g_ref, m_ref, v_ref, d_ref, nm_ref, nv_ref):
        d_ref[...], nm_ref[...], nv_ref[...] = _adamw_math(w_ref[...], g_ref[...], m_ref[...], v_ref[...])

    spec = pl.BlockSpec((tr, cols), lambda i: (i, 0))
    sh = jax.ShapeDtypeStruct((rows, cols), F32)
    return _pcall(
        body, name=name, grid=(rows // tr,), in_specs=[spec] * 4, out_specs=[spec] * 3, out_shape=[sh] * 3,
        compiler_params=_params(("arbitrary",), 14 * _nbytes((tr, -(-cols // LANES) * LANES), F32)),
    )(w, g, m, v)


def _adamw_halves(name, core, w, mine, theirs, m, v):
    rows, cols = w.shape
    tr = _tile(rows // 2, (256, 176, 128))
    nbh = rows // 2 // tr

    def body(c_ref, w_ref, a_ref, b_ref, m_ref, v_ref, g_ref, d_ref, nm_ref, nv_ref):
        g = jnp.where(pl.program_id(0) // nbh == c_ref[0], a_ref[...], b_ref[...])
        g_ref[...] = g
        d_ref[...], nm_ref[...], nv_ref[...] = _adamw_math(w_ref[...], g, m_ref[...], v_ref[...])

    spec = pl.BlockSpec((tr, cols), lambda i, c_ref: (i, 0))
    half = pl.BlockSpec((tr, cols), lambda i, c_ref: (i % nbh, 0))
    sh = jax.ShapeDtypeStruct((rows, cols), F32)
    grid_spec = pltpu.PrefetchScalarGridSpec(
        num_scalar_prefetch=1, grid=(rows // tr,), in_specs=[spec, half, half, spec, spec], out_specs=[spec] * 4)
    return _pcall(
        body, name=name, grid_spec=grid_spec, out_shape=[sh] * 4,
        compiler_params=_params(("arbitrary",), 18 * _nbytes((tr, -(-cols // LANES) * LANES), F32)),
    )(core, w, mine, theirs, m, v)


MESH = pl.DeviceIdType.MESH
ANY = pl.BlockSpec(memory_space=pl.ANY)
N_CHIPS = 4
N_DEV = 8
BF16_ROWS = 16


def _me():
    return lax.axis_index("x"), lax.axis_index("y"), lax.axis_index("c")


def _other_chips(x, y):
    return [(1 - x, y), (x, 1 - y), (1 - x, 1 - y)]


def _remote(src, dst, ssem, rsem, dev):
    return pltpu.make_async_remote_copy(src_ref=src, dst_ref=dst, send_sem=ssem, recv_sem=rsem,
                                        device_id=dev, device_id_type=MESH)


def _half_rows(ref_rows, cc):
    half = ref_rows // 2
    return pl.ds(pl.multiple_of(cc * half, BF16_ROWS), half)


def _gather_weights(shards):
    n = len(shards)
    n_ici = n * (N_CHIPS - 1)

    def body(*refs):
        ins, outs, (ssem, rsem, lsem, lrsem) = refs[:n], refs[n:2 * n], refs[2 * n:]
        x, y, c = _me()
        k_me = 2 * x + y
        sib = (x, y, 1 - c)
        chips = _other_chips(x, y)
        started = []
        for i, (w_ref, o_ref) in enumerate(zip(ins, outs)):
            cp = _remote(w_ref, o_ref.at[k_me], lsem.at[i], lrsem.at[i], sib)
            cp.start()
            started.append(cp)
        for r, (px, py) in enumerate(chips):
            for i, (w_ref, o_ref) in enumerate(zip(ins, outs)):
                rows = _half_rows(w_ref.shape[0], c)
                s = r * n + i
                cp = _remote(w_ref.at[rows], o_ref.at[k_me, rows], ssem.at[s], rsem.at[s], (px, py, c))
                cp.start()
                started.append(cp)
        for r, (px, py) in enumerate(chips):
            for i, o_ref in enumerate(outs):
                blk = o_ref.at[2 * px + py, _half_rows(o_ref.shape[1], c)]
                s = r * n + i
                _remote(blk, blk, ssem.at[s], rsem.at[s], (px, py, c)).wait_recv()
                cp = _remote(blk, blk, ssem.at[n_ici + s], rsem.at[n_ici + s], sib)
                cp.start()
                started.append(cp)
        for r, (px, py) in enumerate(chips):
            for i, o_ref in enumerate(outs):
                blk = o_ref.at[2 * px + py, _half_rows(o_ref.shape[1], 1 - c)]
                s = n_ici + r * n + i
                _remote(blk, blk, ssem.at[s], rsem.at[s], sib).wait_recv()
        for cp in started[n:]:
            cp.wait_send()
        for cp in started[:n]:
            cp.wait()

    return _pcall(
        body, name="gather_weights", in_specs=[ANY] * n, out_specs=[ANY] * n,
        out_shape=[jax.ShapeDtypeStruct((N_CHIPS,) + s.shape, s.dtype) for s in shards],
        scratch_shapes=[pltpu.SemaphoreType.DMA((2 * n_ici,)), pltpu.SemaphoreType.DMA((2 * n_ici,)),
                        pltpu.SemaphoreType.DMA((n,)), pltpu.SemaphoreType.DMA((n,))],
    )(*shards)


def _swap_halves(name, grads):
    n = len(grads)

    def body(*refs):
        ins, outs, (ssem, rsem) = refs[:n], refs[n:2 * n], refs[2 * n:]
        x, y, c = _me()
        copies = []
        for i, (g_ref, o_ref) in enumerate(zip(ins, outs)):
            for k in range(N_CHIPS):
                s = i * N_CHIPS + k
                cp = _remote(g_ref.at[k, _half_rows(g_ref.shape[1], 1 - c)], o_ref.at[k], ssem.at[s], rsem.at[s],
                             (x, y, 1 - c))
                cp.start()
                copies.append(cp)
        for cp in copies:
            cp.wait()

    return _pcall(
        body, name=name, in_specs=[ANY] * n, out_specs=[ANY] * n,
        out_shape=[jax.ShapeDtypeStruct((N_CHIPS, g.shape[1] // 2, g.shape[2]), g.dtype) for g in grads],
        scratch_shapes=[pltpu.SemaphoreType.DMA((n * N_CHIPS,)), pltpu.SemaphoreType.DMA((n * N_CHIPS,))],
    )(*grads)


SEM = pl.BlockSpec(memory_space=pltpu.SEMAPHORE)
IN_HBM = pl.BlockSpec(memory_space=pltpu.HBM)
SPLIT_PARAMS = dict(compiler_params=pltpu.CompilerParams(has_side_effects=pltpu.SideEffectType.DATAFLOW_SIDE_EFFECTING))


def _gather_start(name, shards):
    n = len(shards)
    n_sem = n * N_CHIPS

    def body(*refs):
        ins, lands, (ssem, rsem), token = refs[:n], refs[n:2 * n], refs[2 * n:2 * n + 2], refs[-1]
        x, y, c = _me()
        k_me = 2 * x + y
        for i, (w_ref, l_ref) in enumerate(zip(ins, lands)):
            _remote(w_ref, l_ref.at[k_me], ssem.at[i], rsem.at[i], (x, y, 1 - c)).start()
        for r, (px, py) in enumerate(_other_chips(x, y)):
            for i, (w_ref, l_ref) in enumerate(zip(ins, lands)):
                rows = _half_rows(w_ref.shape[0], c)
                s = (r + 1) * n + i
                _remote(w_ref.at[rows], l_ref.at[k_me, rows], ssem.at[s], rsem.at[s], (px, py, c)).start()
        token[...] = jnp.zeros_like(token)

    src = [pltpu.HBM(s.shape, s.dtype) for s in shards]
    dst = [pltpu.HBM((N_CHIPS,) + s.shape, s.dtype) for s in shards]
    outs = _call(
        body, name=name, in_specs=[IN_HBM] * (2 * n),
        out_specs=[SEM, SEM] + [IN_HBM] * (2 * n) + [pl.BlockSpec(memory_space=pltpu.VMEM)],
        out_shape=[pltpu.SemaphoreType.DMA((n_sem,)), pltpu.SemaphoreType.DMA((n_sem,))] + src + dst
        + [jax.ShapeDtypeStruct((SUBLANES, LANES), F32)],
        input_output_aliases={i: 2 + i for i in range(2 * n)}, **SPLIT_PARAMS,
    )(*[pltpu.with_memory_space_constraint(s, pltpu.HBM) for s in shards],
      *[pltpu.with_memory_space_constraint(lax.empty(d.shape, d.dtype), pltpu.HBM) for d in dst])
    return outs[:-1], outs[-1]


def _gather_wait(name, handle, after):
    ssem, rsem, thru = handle[0], handle[1], handle[2:]
    n = len(thru) // 2

    def body(*refs):
        ins, lands, (ssem_ref, rsem_ref) = refs[:n], refs[n:2 * n], refs[2 * n:2 * n + 2]
        x, y, c = _me()
        k_me = 2 * x + y
        for i, (w_ref, l_ref) in enumerate(zip(ins, lands)):
            cp = _remote(w_ref, l_ref.at[k_me], ssem_ref.at[i], rsem_ref.at[i], (x, y, 1 - c))
            cp.wait_send()
            cp.wait_recv()
        for r, (px, py) in enumerate(_other_chips(x, y)):
            for i, (w_ref, l_ref) in enumerate(zip(ins, lands)):
                rows = _half_rows(w_ref.shape[0], c)
                s = (r + 1) * n + i
                cp = _remote(w_ref.at[rows], l_ref.at[2 * px + py, rows], ssem_ref.at[s], rsem_ref.at[s], (px, py, c))
                cp.wait_send()
                cp.wait_recv()

    outs = _call(
        body, name=name, in_specs=[IN_HBM] * (2 * n) + [SEM, SEM, ANY], out_specs=[IN_HBM] * (2 * n),
        out_shape=[pltpu.HBM(t.shape, t.dtype) for t in thru],
        input_output_aliases={i: i for i in range(2 * n)}, **SPLIT_PARAMS,
    )(*thru, ssem, rsem, after)
    return outs[n:]


def _forward_halves(name, blocks):
    n = len(blocks)
    n_sem = n * (N_CHIPS - 1)

    def body(*refs):
        outs, (ssem, rsem) = refs[n:2 * n], refs[2 * n:]
        x, y, c = _me()
        sib = (x, y, 1 - c)
        chips = _other_chips(x, y)
        sends = []
        for r, (px, py) in enumerate(chips):
            for i, o_ref in enumerate(outs):
                blk = o_ref.at[2 * px + py, _half_rows(o_ref.shape[1], c)]
                cp = _remote(blk, blk, ssem.at[r * n + i], rsem.at[r * n + i], sib)
                cp.start()
                sends.append(cp)
        for r, (px, py) in enumerate(chips):
            for i, o_ref in enumerate(outs):
                blk = o_ref.at[2 * px + py, _half_rows(o_ref.shape[1], 1 - c)]
                _remote(blk, blk, ssem.at[r * n + i], rsem.at[r * n + i], sib).wait_recv()
        for cp in sends:
            cp.wait_send()

    return _pcall(
        body, name=name, in_specs=[ANY] * n, out_specs=[ANY] * n,
        out_shape=[jax.ShapeDtypeStruct(b.shape, b.dtype) for b in blocks],
        input_output_aliases={i: i for i in range(n)},
        scratch_shapes=[pltpu.SemaphoreType.DMA((n_sem,)), pltpu.SemaphoreType.DMA((n_sem,))],
    )(*blocks)


def _scatter_start(name, parts):
    n = len(parts)
    n_sem = n * (N_CHIPS - 1)

    def body(*refs):
        ins, lands, (ssem, rsem), token = refs[:n], refs[n:2 * n], refs[2 * n:2 * n + 2], refs[-1]
        x, y, c = _me()
        k_me = 2 * x + y
        for r, (px, py) in enumerate(_other_chips(x, y)):
            for i, (p_ref, l_ref) in enumerate(zip(ins, lands)):
                s = r * n + i
                _remote(p_ref.at[2 * px + py], l_ref.at[k_me], ssem.at[s], rsem.at[s], (px, py, c)).start()
        token[...] = jnp.zeros_like(token)

    hbm = [pltpu.HBM(p.shape, p.dtype) for p in parts]
    outs = _call(
        body, name=name, in_specs=[IN_HBM] * (2 * n),
        out_specs=[SEM, SEM] + [IN_HBM] * (2 * n) + [pl.BlockSpec(memory_space=pltpu.VMEM)],
        out_shape=[pltpu.SemaphoreType.DMA((n_sem,)), pltpu.SemaphoreType.DMA((n_sem,))] + hbm + hbm
        + [jax.ShapeDtypeStruct((SUBLANES, LANES), F32)],
        input_output_aliases={i: 2 + i for i in range(2 * n)}, **SPLIT_PARAMS,
    )(*[pltpu.with_memory_space_constraint(p, pltpu.HBM) for p in parts],
      *[pltpu.with_memory_space_constraint(lax.empty(p.shape, p.dtype), pltpu.HBM) for p in parts])
    return outs[:-1], outs[-1]


def _scatter_wait(name, handle, after):
    ssem, rsem, thru = handle[0], handle[1], handle[2:]
    n = len(thru) // 2

    def body(*refs):
        ins, lands, (ssem_ref, rsem_ref) = refs[:n], refs[n:2 * n], refs[2 * n:2 * n + 2]
        x, y, c = _me()
        for r, (px, py) in enumerate(_other_chips(x, y)):
            for i, (p_ref, l_ref) in enumerate(zip(ins, lands)):
                s = r * n + i
                cp = _remote(p_ref.at[2 * px + py], l_ref.at[2 * px + py], ssem_ref.at[s], rsem_ref.at[s], (px, py, c))
                cp.wait_send()
                cp.wait_recv()

    outs = _call(
        body, name=name, in_specs=[IN_HBM] * (2 * n) + [SEM, SEM, ANY], out_specs=[IN_HBM] * (2 * n),
        out_shape=[pltpu.HBM(t.shape, t.dtype) for t in thru],
        input_output_aliases={i: i for i in range(2 * n)}, **SPLIT_PARAMS,
    )(*thru, ssem, rsem, after)
    return outs[n:]


def _share_halves(halves):
    n = len(halves)

    def body(*refs):
        ins, outs, (ssem, rsem) = refs[:n], refs[n:2 * n], refs[2 * n:]
        x, y, c = _me()
        copies = [_remote(r_ref, o_ref, ssem.at[i], rsem.at[i], (x, y, 1 - c))
                  for i, (r_ref, o_ref) in enumerate(zip(ins, outs))]
        for cp in copies:
            cp.start()
        for cp in copies:
            cp.wait()

    return _pcall(
        body, name="share_halves", in_specs=[ANY] * n, out_specs=[ANY] * n,
        out_shape=[jax.ShapeDtypeStruct(h.shape, h.dtype) for h in halves],
        scratch_shapes=[pltpu.SemaphoreType.DMA((n,)), pltpu.SemaphoreType.DMA((n,))],
    )(*halves)


def _exchange_small(v, reduce):
    rows = v.shape[0]

    def body(v_ref, out_ref, buf, ssem, rsem):
        x, y, c = _me()
        me = 4 * x + 2 * y + c
        peers = [((x + bx) % 2, (y + by) % 2, (c + bc) % 2)
                 for bx in (0, 1) for by in (0, 1) for bc in (0, 1) if (bx, by, bc) != (0, 0, 0)]
        dst = buf if reduce else out_ref
        dst[me] = v_ref[...]
        sends = [_remote(v_ref, dst.at[me], ssem.at[r], rsem.at[r], p) for r, p in enumerate(peers)]
        for cp in sends:
            cp.start()
        for r, (px, py, pc) in enumerate(peers):
            blk = dst.at[4 * px + 2 * py + pc]
            _remote(blk, blk, ssem.at[r], rsem.at[r], (px, py, pc)).wait_recv()
        if reduce:
            acc = buf[0]
            for d in range(1, N_DEV):
                acc = acc + buf[d]
            out_ref[...] = acc
        for cp in sends:
            cp.wait_send()

    vm = pl.BlockSpec(memory_space=pltpu.VMEM)
    out_shape = jax.ShapeDtypeStruct((rows, LANES) if reduce else (N_DEV, rows, LANES), F32)
    buf_shape = (N_DEV, rows, LANES) if reduce else (SUBLANES, LANES)
    return _pcall(
        body, pin=False, name="reduce_small" if reduce else "gather_small", in_specs=[vm], out_specs=vm, out_shape=out_shape,
        scratch_shapes=[pltpu.VMEM(buf_shape, F32), pltpu.SemaphoreType.DMA((N_DEV - 1,)),
                        pltpu.SemaphoreType.DMA((N_DEV - 1,))],
        compiler_params=pltpu.CompilerParams(vmem_limit_bytes=32 * 1024 * 1024),
    )(v)


def _add_pair(name, core, g, theirs):
    _, half, cols = theirs.shape
    tr = _tile(half, (256, 176, 128))
    nb = half // tr

    def body(c_ref, g_ref, t_ref, o32_ref, o16_ref):
        s = g_ref[...] + t_ref[...]
        o32_ref[...] = s
        o16_ref[...] = s.astype(BF16)

    spec = pl.BlockSpec((None, tr, cols), lambda k, i, c_ref: (k, i, 0))
    grid_spec = pltpu.PrefetchScalarGridSpec(
        num_scalar_prefetch=1, grid=(N_CHIPS, nb),
        in_specs=[pl.BlockSpec((None, tr, cols), lambda k, i, c_ref: (k, c_ref[0] * nb + i, 0)), spec],
        out_specs=[spec, spec])
    return _pcall(
        body, name=name, grid_spec=grid_spec,
        out_shape=[jax.ShapeDtypeStruct(theirs.shape, F32), jax.ShapeDtypeStruct(theirs.shape, BF16)],
        compiler_params=_params(("arbitrary", "arbitrary"), 8 * _nbytes((tr, cols + LANES), F32)),
    )(core, g, theirs)


def _add_chips(name, chip, p32, recv):
    _, half, cols = p32.shape
    tr = _tile(half, (256, 176, 128))

    def body(k_ref, p_ref, r0_ref, r1_ref, r2_ref, o_ref):
        o_ref[...] = ((p_ref[...] + r0_ref[...].astype(F32)) + r1_ref[...].astype(F32)) + r2_ref[...].astype(F32)

    def other(r):
        return pl.BlockSpec((None, tr, cols), lambda i, k_ref: (r + (k_ref[0] <= r).astype(jnp.int32), i, 0))
    grid_spec = pltpu.PrefetchScalarGridSpec(
        num_scalar_prefetch=1, grid=(half // tr,),
        in_specs=[pl.BlockSpec((None, tr, cols), lambda i, k_ref: (k_ref[0], i, 0)), other(0), other(1), other(2)],
        out_specs=pl.BlockSpec((tr, cols), lambda i, k_ref: (i, 0)))
    return _pcall(
        body, name=name, grid_spec=grid_spec, out_shape=jax.ShapeDtypeStruct((half, cols), F32),
        compiler_params=_params(("arbitrary",), 10 * _nbytes((tr, cols + LANES), F32)),
    )(chip, p32, recv, recv, recv)


def kernel(x, mem, w_in, b_in, hg_lb_logits, hg_norm_w, ml_conv_w, ml_conv_b, ml_norm_w, w_out, ln1_g, ln1_b, ca_wq, ca_wkv, ca_wo, ln2_g, ln2_b, ffn_w_up, ffn_conv_w, ffn_conv_b, ffn_w_down, ln3_g, ln3_b, loss_target, m_w_in, m_b_in, m_hg_lb_logits, m_hg_norm_w, m_ml_conv_w, m_ml_conv_b, m_ml_norm_w, m_w_out, m_ln1_g, m_ln1_b, m_ca_wq, m_ca_wkv, m_ca_wo, m_ln2_g, m_ln2_b, m_ffn_w_up, m_ffn_conv_w, m_ffn_conv_b, m_ffn_w_down, m_ln3_g, m_ln3_b, v_w_in, v_b_in, v_hg_lb_logits, v_hg_norm_w, v_ml_conv_w, v_ml_conv_b, v_ml_norm_w, v_w_out, v_ln1_g, v_ln1_b, v_ca_wq, v_ca_wkv, v_ca_wo, v_ln2_g, v_ln2_b, v_ffn_w_up, v_ffn_conv_w, v_ffn_conv_b, v_ffn_w_down, v_ln3_g, v_ln3_b):
    return _train_step(dict(locals()))


WEIGHTS = ("w_in", "b_in", "hg_lb_logits", "hg_norm_w", "ml_conv_w", "ml_conv_b", "ml_norm_w", "w_out", "ln1_g",
           "ln1_b", "ca_wq", "ca_wkv", "ca_wo", "ln2_g", "ln2_b", "ffn_w_up", "ffn_conv_w", "ffn_conv_b",
           "ffn_w_down", "ln3_g", "ln3_b")
MATRICES = ("w_in", "w_out", "ca_wq", "ca_wkv", "ca_wo", "ffn_w_up", "ffn_w_down")
COL_SHARDED = ("w_in", "ca_wkv", "ffn_w_up", "ml_conv_w", "ffn_conv_w")
SMALL = tuple(n for n in WEIGHTS if n not in MATRICES)
PART_ROWS = 16


def _part_rows(shape, lead):
    n = 1
    for s in shape[lead:]:
        n *= s
    return -(-n // (LANES * PART_ROWS)) * PART_ROWS


def _pack(arrs, dtype, lead=0, rows=None):
    parts = []
    for a in arrs:
        head = a.shape[:lead]
        flat = a.reshape(head + (-1,)).astype(dtype)
        pad = _part_rows(a.shape, lead) * LANES - flat.shape[-1]
        flat = jnp.pad(flat, [(0, 0)] * lead + [(0, pad)])
        parts.append(flat.reshape(head + (-1, LANES)))
    used = sum(p.shape[lead] for p in parts)
    if rows is not None and rows > used:
        parts.append(jnp.zeros(parts[0].shape[:lead] + (rows - used, LANES), dtype))
    return jnp.concatenate(parts, axis=lead)


def _unpack(buf, shapes):
    lead = buf.shape[:-2]
    outs, r = [], 0
    for sh in shapes:
        n = 1
        for s in sh:
            n *= s
        nr = _part_rows(sh, 0)
        flat = buf[..., r:r + nr, :].reshape(lead + (nr * LANES,))
        outs.append(flat[..., :n].reshape(lead + tuple(sh)))
        r += nr
    return outs


def _cat_cols(s):
    return jnp.moveaxis(s, 0, 1).reshape(s.shape[1], -1)


def _split_cols(g):
    return jnp.moveaxis(g.reshape(g.shape[0], N_CHIPS, -1), 1, 0)


def _stack_rows(s):
    return s.reshape(-1, s.shape[-1])


def _train_step(a):
    xs, mems, tgt = a["x"][0], a["mem"][0], a["loss_target"][0]
    core = lax.axis_index("c").astype(jnp.int32).reshape(1)
    chip = (2 * lax.axis_index("x") + lax.axis_index("y")).astype(jnp.int32).reshape(1)
    k_me = chip[0]
    shard = {n: a[n][0] for n in MATRICES}

    later = [n for n in MATRICES if n != "w_in"]
    taps = _exchange_small(_pack([a["ml_conv_w"][0], a["ffn_conv_w"][0]], F32), reduce=False)
    w = {"w_in": jnp.pad(_cat_cols(_gather_weights([shard["w_in"].astype(BF16)])[0]), ((0, 0), (0, D_IN_PAD - D_IN)))}
    gathering, token = _gather_start("gather_start", [shard[n].astype(BF16) for n in later])
    taps = taps.reshape((N_CHIPS, 2) + taps.shape[1:])[:, 0]
    ml_cw, ffn_cw = [_cat_cols(s) for s in _unpack(taps, [a["ml_conv_w"].shape[1:], a["ffn_conv_w"].shape[1:]])]
    b_in_p = jnp.pad(a["b_in"], ((0, 0), (0, D_IN_PAD - D_IN))) + token[0:1, 0:1]
    mixer_w = (a["hg_lb_logits"], a["hg_norm_w"], ml_cw, a["ml_conv_b"], a["ml_norm_w"])
    up_cols = a["ffn_w_up"].shape[-1]

    xb = xs.astype(BF16)
    proj = _mm("proj", "nn", xb, w["w_in"], bias=b_in_p, tm=256, tn=D_IN_PAD)
    y, hst, cst, nst, mst = _mixer_fwd(proj, *mixer_w)
    w.update(zip(later, _forward_halves("forward_halves", _gather_wait("gather_wait", gathering, y))))
    for n in ("w_out", "ca_wq", "ca_wo", "ffn_w_down"):
        w[n] = _stack_rows(w[n])
    z1, x1, x1b = _mm("mix_out", "nn", y, w["w_out"], res=xs, res_scale=ALPHA, ln=("fwd", a["ln1_g"], a["ln1_b"]),
                      copy_dtype=BF16)
    q = _mm("ca_q", "nn", x1b, w["ca_wq"], out_dtype=BF16, tn=D_MODEL)
    kv = _mm("ca_kv", "nn", mems, w["ca_wkv"])
    o = _attn_fwd(q, kv)
    z2, x2, x2b = _mm("ca_out", "nn", o, w["ca_wo"], res=x1, res_scale=ALPHA, ln=("fwd", a["ln2_g"], a["ln2_b"]),
                      copy_dtype=BF16)
    w_up = _cat_cols(w["ffn_w_up"])
    u, hmid, dz3, g_ln3g, g_ln3b, loss_part, dz3b = _ffn_fwd(
        x2b, x2, w_up, ffn_cw, a["ffn_conv_b"], w["ffn_w_down"], a["ln3_g"], a["ln3_b"], tgt)

    grads = {"ln3_g": g_ln3g, "ln3_b": g_ln3b}
    grads["ffn_w_down"] = _mm("g_w_down", "tn", hmid, dz3b, tm=D_FF // 2, tn=D_MODEL)
    du, g_cw, g_cb, dz2, grads["ln2_g"], grads["ln2_b"], dz2b = _ffn_bwd(
        u, ffn_cw, a["ffn_conv_b"], dz3b, dz3, w["ffn_w_down"], w_up, z2, a["ln2_g"], a["ln2_b"])
    grads["ffn_conv_w"] = jnp.transpose(g_cw, (2, 1, 0, 3)).reshape(FFN_CONV, 2 * D_FF)
    grads["ffn_conv_b"] = jnp.transpose(g_cb, (2, 1, 0, 3)).reshape(1, 2 * D_FF)
    grads["ffn_w_up"] = _mm("g_w_up", "tn", x2b, du, out_groups=N_CHIPS, tm=D_MODEL, tn=up_cols)
    grads["ffn_w_down"] = grads["ffn_w_down"].reshape((N_CHIPS,) + shard["ffn_w_down"].shape)
    pending = {}

    def reduce_start(tag, names):
        group = [grads[n] for n in names]
        sums = [_add_pair("add_pair_" + n, core, g, t)
                for n, g, t in zip(names, group, _swap_halves("swap_halves_" + tag, group))]
        handle, token = _scatter_start("scatter_start_" + tag, [s16 for _, s16 in sums])
        pending[tag] = (names, [s32 for s32, _ in sums], handle)
        return token[0:1, 0:1]

    zero = reduce_start("ffn", ("ffn_w_up", "ffn_w_down"))
    do = _mm("d_o", "nt", dz2b, w["ca_wo"], bias=jnp.zeros((1, D_MODEL), F32) + zero, out_dtype=BF16, tn=D_MODEL)
    grads["ca_wo"] = _mm("g_wo", "tn", o, dz2b, tm=D_MODEL, tn=D_MODEL)
    dq, dkv = _attn_bwd(q, kv, do)
    grads["ca_wq"] = _mm("g_wq", "tn", x1b, dq, tm=D_MODEL, tn=D_MODEL)
    grads["ca_wkv"] = _mm("g_wkv", "tn", mems, dkv, out_groups=N_CHIPS, tm=D_MODEL)
    dz1, grads["ln1_g"], grads["ln1_b"], dz1b = _mm("d_x1", "nt", dq, w["ca_wq"], res=dz2, res_scale=ALPHA,
                                                    ln=("bwd", z1, a["ln1_g"], a["ln1_b"]), copy_dtype=BF16)
    dy = _mm("d_y", "nt", dz1b, w["w_out"], tn=D_MODEL)
    grads["w_out"] = _mm("g_w_out", "tn", y, dz1b, tm=D_MODEL, tn=D_MODEL)
    for n in ("w_out", "ca_wq", "ca_wo"):
        grads[n] = grads[n].reshape((N_CHIPS,) + shard[n].shape)
    zero = reduce_start("attn", ("w_out", "ca_wq", "ca_wkv", "ca_wo"))
    (dproj, g_b_in, grads["hg_lb_logits"], grads["hg_norm_w"], grads["ml_conv_w"], grads["ml_conv_b"],
     grads["ml_norm_w"]) = _mixer_bwd(proj, dy, hst, cst, nst, mst, mixer_w[0], mixer_w[1] + zero, *mixer_w[2:])
    grads["w_in"] = _split_cols(_mm("g_w_in", "tn", xb, dproj, tm=D_MODEL, tn=up_cols)[:, :D_IN])
    grads["b_in"] = g_b_in[:, :D_IN]
    zero = reduce_start("in", ("w_in",))
    dx = _mm("d_x", "nt", dproj, w["w_in"], bias=jnp.zeros((1, D_MODEL), F32) + zero, res=dz1, res_scale=ALPHA,
             tm=256, tn=D_MODEL)

    halves = {}
    for tag, (names, sums32, handle) in pending.items():
        for n, s32, r in zip(names, sums32, _scatter_wait("scatter_wait_" + tag, handle, dx)):
            halves[n] = _add_chips("add_chips_" + n, chip, s32, r)
    halves = [halves[n] for n in MATRICES]
    other_halves = _share_halves(halves)

    small_shapes = [grads[n].shape for n in SMALL] + [loss_part.shape]
    summed = _unpack(_exchange_small(_pack([grads[n] for n in SMALL] + [loss_part], F32), reduce=True), small_shapes)
    loss = summed[-1][0, 0]
    for n, g in zip(SMALL, summed[:-1]):
        if n in COL_SHARDED:
            cols = a[n].shape[-1]
            g = lax.dynamic_slice_in_dim(g, k_me * cols, cols, axis=1)
        grads[n] = g

    delta, new_m, new_v = {}, {}, {}
    for n, mine, theirs in zip(MATRICES, halves, other_halves):
        grads[n], delta[n], new_m[n], new_v[n] = _adamw_halves(
            "adamw_" + n, core, shard[n], mine, theirs, a["m_" + n][0], a["v_" + n][0])
    small_w = [a[n][0] if a[n].ndim == 3 else a[n] for n in SMALL]
    small_m = [a["m_" + n][0] if a[n].ndim == 3 else a["m_" + n] for n in SMALL]
    small_v = [a["v_" + n][0] if a[n].ndim == 3 else a["v_" + n] for n in SMALL]
    shapes = [w.shape for w in small_w]
    packed = [_pack(l, F32) for l in (small_w, [grads[n] for n in SMALL], small_m, small_v)]
    for out, buf in zip((delta, new_m, new_v), _adamw("adamw_small", *packed)):
        for n, v in zip(SMALL, _unpack(buf, shapes)):
            out[n] = v

    def shaped(d):
        return [d[n].reshape(a[n].shape) for n in WEIGHTS]
    return (loss, dx[None], *shaped(grads), *shaped(delta), *shaped(new_m), *shaped(new_v))
```

```python
import functools

import jax
import jax.numpy as jnp
from jax import lax
from jax.experimental import pallas as pl
from jax.experimental.pallas import tpu as pltpu

F32 = jnp.float32
BF16 = jnp.bfloat16

D_MODEL = 1024
HEADS = 4
DK = 128
D_GRP = HEADS * DK
CHUNK = 64
ML_CONV = 4
FFN_CONV = 3
D_FF = 2816
CA_DH = D_MODEL // HEADS
DEPTH = 1
ALPHA = (2.0 * DEPTH) ** 0.25
LN_EPS = 1e-5
NEG_BIG = -1e30
D_IN = 8 * D_GRP + 2 * HEADS
D_IN_PAD = 8 * D_GRP + 128
ADAM_LR, ADAM_B1, ADAM_B2, ADAM_EPS, ADAM_WD, ADAM_STEP = 0.001, 0.9, 0.999, 1e-08, 0.01, 10

SUBLANES = 8
LANES = 128
VMEM_BYTES = 64 * 1024 * 1024


def _pcall(body, pin=True, **kw):
    if not pin:
        return _call(body, **kw)
    kw["out_shape"] = jax.tree.map(lambda s: pltpu.HBM(s.shape, s.dtype), kw["out_shape"])
    call = _call(body, **kw)

    def pinned(*args):
        return call(*[pltpu.with_memory_space_constraint(x, pltpu.HBM) if jnp.issubdtype(x.dtype, jnp.floating) else x
                      for x in args])
    return pinned


def _call(body, **kw):
    return pl.pallas_call(body, **kw)


def _params(semantics, vmem_bytes):
    limit = int(min(max(2 * vmem_bytes, 16 * 1024 * 1024), VMEM_BYTES - 8 * 1024 * 1024))
    return pltpu.CompilerParams(dimension_semantics=semantics, vmem_limit_bytes=limit)


def _nbytes(shape, dtype):
    n = 1
    for s in shape:
        n *= s
    return n * jnp.dtype(dtype).itemsize


def _dg(a, b, ca, cb):
    return lax.dot_general(a.astype(BF16), b.astype(BF16), (((ca,), (cb,)), ((), ())),
                           preferred_element_type=F32)


@jax.custom_vjp
def mm_nn(a, b):
    return _dg(a, b, 1, 0)


mm_nn.defvjp(lambda a, b: (_dg(a, b, 1, 0), (a, b)),
             lambda r, g: (_dg(g, r[1], 1, 1).astype(r[0].dtype), _dg(r[0], g, 0, 0).astype(r[1].dtype)))


@jax.custom_vjp
def mm_nt(a, b):
    return _dg(a, b, 1, 1)


mm_nt.defvjp(lambda a, b: (_dg(a, b, 1, 1), (a, b)),
             lambda r, g: (_dg(g, r[1], 1, 0).astype(r[0].dtype), _dg(g, r[0], 0, 0).astype(r[1].dtype)))


@jax.custom_vjp
def mm_tn(a, b):
    return _dg(a, b, 0, 0)


mm_tn.defvjp(lambda a, b: (_dg(a, b, 0, 0), (a, b)),
             lambda r, g: (_dg(r[1], g, 1, 1).astype(r[0].dtype), _dg(r[0], g, 1, 0).astype(r[1].dtype)))


def _tri(n, lower):
    r = lax.broadcasted_iota(jnp.int32, (n, n), 0)
    c = lax.broadcasted_iota(jnp.int32, (n, n), 1)
    return ((r >= c) if lower else (r <= c)).astype(F32)


def _tri_dot(lower, x):
    t = _tri(x.shape[0], lower).astype(BF16)
    hi = x.astype(BF16)
    rest = x - hi.astype(F32)
    mid = rest.astype(BF16)
    lo = (rest - mid.astype(F32)).astype(BF16)
    return sum(lax.dot_general(t, p, (((1,), (0,)), ((), ())), preferred_element_type=F32) for p in (hi, mid, lo))


@jax.custom_vjp
def cumsum_rows(x):
    return _tri_dot(True, x)


cumsum_rows.defvjp(lambda x: (_tri_dot(True, x), None), lambda _, g: (_tri_dot(False, g),))


def _shift_impl(halo, x, d):
    xx = jnp.concatenate([halo, x], axis=0)
    return pltpu.roll(xx, d, 0)[SUBLANES:]


@functools.partial(jax.custom_vjp, nondiff_argnums=(2,))
def shift_rows(halo, x, d):
    return _shift_impl(halo, x, d)


def _shift_bwd(d, _, g):
    n = g.shape[0] + SUBLANES
    gg = jnp.concatenate([jnp.zeros((SUBLANES, g.shape[1]), g.dtype), g], axis=0)
    r = pltpu.roll(gg, n - d, 0)
    return r[:SUBLANES], r[SUBLANES:]


shift_rows.defvjp(lambda halo, x, d: (_shift_impl(halo, x, d), None), _shift_bwd)


def causal_conv(halo, x, w_rows, b):
    k = len(w_rows)
    y = b + w_rows[k - 1] * x
    for d in range(1, k):
        y = y + w_rows[k - 1 - d] * shift_rows(halo, x, d)
    return y


def _sigmoid(x):
    return 1.0 / (1.0 + jnp.exp(-x))


def _silu(x):
    return x * _sigmoid(x)


def _log_sigmoid(x):
    return jnp.minimum(x, 0.0) - jnp.log(1.0 + jnp.exp(-jnp.abs(x)))


def _pick_row(x, i):
    row = lax.broadcasted_iota(jnp.int32, (x.shape[0], 1), 0)
    return jnp.sum(jnp.where(row == i, x, 0.0), axis=0, keepdims=True)


def _layer_norm(z, g, b):
    mu = jnp.mean(z, axis=-1, keepdims=True)
    zc = z - mu
    var = jnp.mean(zc * zc, axis=-1, keepdims=True)
    return zc * lax.rsqrt(var + LN_EPS) * g + b


def _qk_conv(halo, x, w0, w1, w2, w3, b):
    return _silu(causal_conv(halo, x, (w0, w1, w2, w3), b))


def _grp(i, h=None):
    if h is None:
        return pl.ds(i * D_GRP, D_GRP)
    return pl.ds(i * D_GRP + h * DK, DK)


def _mixer_specs(n_chunks, reverse):
    def chunk(c):
        return n_chunks - 1 - c if reverse else c
    row8 = CHUNK // SUBLANES
    proj_spec = pl.BlockSpec((CHUNK, D_IN_PAD), lambda c: (chunk(c), 0))
    halo_spec = pl.BlockSpec((SUBLANES, 2 * D_GRP), lambda c: (jnp.maximum(chunk(c) * row8 - 1, 0), 2))
    small = [pl.BlockSpec((2, D_GRP), lambda c: (0, 0)), pl.BlockSpec((1, D_GRP), lambda c: (0, 0)),
             pl.BlockSpec((ML_CONV, 2 * D_GRP), lambda c: (0, 0)), pl.BlockSpec((1, 2 * D_GRP), lambda c: (0, 0)),
             pl.BlockSpec((1, D_GRP), lambda c: (0, 0))]
    state_specs = [pl.BlockSpec((1, HEADS, DK, DK), lambda c: (chunk(c), 0, 0, 0)),
                   pl.BlockSpec((1, HEADS, DK, DK), lambda c: (chunk(c), 0, 0, 0)),
                   pl.BlockSpec((1, HEADS, 1, DK), lambda c: (chunk(c), 0, 0, 0)),
                   pl.BlockSpec((1, HEADS, 1, DK), lambda c: (chunk(c), 0, 0, 0))]
    y_spec = pl.BlockSpec((CHUNK, 2 * D_GRP), lambda c: (chunk(c), 0))
    return proj_spec, halo_spec, small, state_specs, y_spec, chunk


def _heads(x):
    return [x[:, h * DK:(h + 1) * DK] for h in range(HEADS)]


def _last(x, j):
    lane = lax.broadcasted_iota(jnp.int32, (1, x.shape[-1]), 1)
    return jnp.sum(jnp.where(lane == j, x, 0.0), axis=-1, keepdims=True)


def _hg_chunk(st_t, hq, hf, hi, hgate, l0, l1, nw):
    n = hq.shape[0]
    lb = _sigmoid(l0 - l1)
    q = _silu(hq)
    lf = jnp.log(lb + (1.0 - lb) * _sigmoid(hf))
    k = (1.0 - lb) * _sigmoid(-hf)
    b = cumsum_rows(lf)
    b_ref = _pick_row(b, n // 2 - 1)
    b_last = _pick_row(b, n - 1)
    qa, ka = _heads(q * jnp.exp(b - b_ref)), _heads(k * jnp.exp(b_ref - b))
    qe, kd, eb, v = _heads(q * jnp.exp(b)), _heads(k * jnp.exp(b_last - b)), _heads(jnp.exp(b_last)), _heads(hi)
    tri = _tri(n, True) > 0
    attn = [jnp.where(tri, mm_nt(qa[h], ka[h]), 0.0) for h in range(HEADS)]
    o = [mm_nn(attn[h], v[h]) + mm_nt(qe[h], st_t[h]) for h in range(HEADS)]
    st_new = jnp.stack([eb[h] * st_t[h] + mm_tn(v[h], kd[h]) for h in range(HEADS)])
    yn = [o[h] * lax.rsqrt(jnp.mean(o[h] * o[h], axis=-1, keepdims=True) + LN_EPS) for h in range(HEADS)]
    return st_new, jnp.concatenate(yn, axis=1) * nw * _silu(hgate)


def _ml_chunk(c_st, n_st, m_st, q, k, v, gates, og, nw):
    n = q.shape[0]
    ig = jnp.stack([_last(gates, h) for h in range(HEADS)])
    log_f = _log_sigmoid(gates)
    fl = jnp.stack([_last(log_f, HEADS + h) for h in range(HEADS)])
    bw = cumsum_rows(jnp.concatenate([jnp.broadcast_to(fl[h], (n, DK)) for h in range(HEADS)], axis=1))
    b = jnp.stack([_last(x, 0) for x in _heads(bw)])
    g = jnp.sum(fl, axis=1, keepdims=True)
    eye = lax.broadcasted_iota(jnp.int32, (n, n), 0) == lax.broadcasted_iota(jnp.int32, (n, n), 1)
    e_row = jnp.sum(jnp.where(eye, ig - b, 0.0), axis=1, keepdims=True)
    d = jnp.where(_tri(n, True) > 0, b + e_row, -jnp.inf)
    inter = b + m_st
    m_t = jnp.maximum(inter, jnp.max(d, axis=2, keepdims=True))
    qs, kh, vh = _heads(q * (DK ** -0.5)), _heads(k), _heads(v)
    s = jnp.stack([mm_nt(qs[h], kh[h]) for h in range(HEADS)]) * jnp.exp(d - m_t)
    w_inter = jnp.exp(inter - m_t)
    num = (jnp.stack([mm_nn(s[h], vh[h]) for h in range(HEADS)])
           + w_inter * jnp.stack([mm_nn(qs[h], c_st[h]) for h in range(HEADS)]))
    den = jnp.sum(s, axis=2, keepdims=True) + w_inter * jnp.sum(jnp.stack(qs) * n_st, axis=2, keepdims=True)
    h_out = num / jnp.maximum(jnp.abs(den), jnp.exp(-m_t))
    a = g - b + ig
    m_new = jnp.maximum(g + m_st, jnp.max(a, axis=1, keepdims=True))
    decay = jnp.exp(g + m_st - m_new)
    wk = jnp.stack(kh) * jnp.exp(a - m_new)
    c_new = decay * c_st + jnp.stack([mm_tn(wk[h], vh[h]) for h in range(HEADS)])
    n_new = decay * n_st + jnp.sum(wk, axis=1, keepdims=True)
    hc = h_out - jnp.mean(h_out, axis=-1, keepdims=True)
    yn = hc * lax.rsqrt(jnp.mean(hc * hc, axis=-1, keepdims=True) + LN_EPS)
    y = _sigmoid(og) * (jnp.concatenate([yn[h] for h in range(HEADS)], axis=1) * nw)
    return c_new, n_new, m_new, y


def _mixer_inputs(proj_ref, lg_ref, hnw_ref, mnw_ref, qk):
    hg_in = (proj_ref[:, _grp(0)], proj_ref[:, _grp(1)], proj_ref[:, _grp(2)], proj_ref[:, _grp(3)],
             lg_ref[0:1, :], lg_ref[1:2, :], hnw_ref[...])
    ml_in = (qk[:, :D_GRP], qk[:, D_GRP:], proj_ref[:, _grp(6)], proj_ref[:, pl.ds(8 * D_GRP, LANES)],
             proj_ref[:, _grp(7)], mnw_ref[...])
    return hg_in, ml_in


def _mixer_fwd(proj, lb_logits, hg_nw, conv_w, conv_b, ml_nw):
    seq = proj.shape[0]
    n_chunks = seq // CHUNK
    proj_spec, halo_spec, small, state_specs, y_spec, _ = _mixer_specs(n_chunks, False)

    def body(proj_ref, halo_ref, lg_ref, hnw_ref, cw_ref, cb_ref, mnw_ref,
             y_ref, hst_ref, cst_ref, nst_ref, mst_ref, hs, cs, ns, ms):
        c = pl.program_id(0)

        @pl.when(c == 0)
        def _():
            hs[...] = jnp.zeros_like(hs)
            cs[...] = jnp.zeros_like(cs)
            ns[...] = jnp.zeros_like(ns)
            ms[...] = jnp.full(ms.shape, NEG_BIG, F32)

        hst_ref[0] = hs[...]
        cst_ref[0] = cs[...]
        nst_ref[0] = ns[...]
        mst_ref[0] = ms[...]
        halo = jnp.where(c > 0, halo_ref[...], 0.0)
        qk = _qk_conv(halo, proj_ref[:, pl.ds(4 * D_GRP, 2 * D_GRP)],
                      cw_ref[0:1, :], cw_ref[1:2, :], cw_ref[2:3, :], cw_ref[3:4, :], cb_ref[...])
        hg_in, ml_in = _mixer_inputs(proj_ref, lg_ref, hnw_ref, mnw_ref, qk)
        hs[...], y_hg = _hg_chunk(hs[...], *hg_in)
        cs[...], ns[...], m_new, y_ml = _ml_chunk(cs[...], ns[...], _last(ms[...], 0), *ml_in)
        ms[...] = jnp.broadcast_to(m_new, ms.shape)
        y_ref[:, pl.ds(0, D_GRP)] = y_hg.astype(BF16)
        y_ref[:, pl.ds(D_GRP, D_GRP)] = y_ml.astype(BF16)

    st = jax.ShapeDtypeStruct((n_chunks, HEADS, DK, DK), F32)
    vec = jax.ShapeDtypeStruct((n_chunks, HEADS, 1, DK), F32)
    vmem = 2 * (_nbytes((CHUNK, D_IN_PAD), F32) + _nbytes((CHUNK, 2 * D_GRP), F32) + 2 * _nbytes((HEADS, DK, DK), F32)) \
        + 2 * _nbytes((HEADS, DK, DK), F32)
    return _pcall(
        body, name="mixer_fwd", grid=(n_chunks,),
        in_specs=[proj_spec, halo_spec] + small,
        out_specs=[y_spec] + state_specs,
        out_shape=[jax.ShapeDtypeStruct((seq, 2 * D_GRP), BF16), st, st, vec, vec],
        scratch_shapes=[pltpu.VMEM((HEADS, DK, DK), F32), pltpu.VMEM((HEADS, DK, DK), F32),
                        pltpu.VMEM((HEADS, 1, DK), F32), pltpu.VMEM((HEADS, 1, DK), F32)],
        compiler_params=_params(("arbitrary",), vmem),
    )(proj, proj, lb_logits, hg_nw, conv_w, conv_b, ml_nw)


def _mixer_bwd(proj, dy, hst, cst, nst, mst, lb_logits, hg_nw, conv_w, conv_b, ml_nw):
    seq = proj.shape[0]
    n_chunks = seq // CHUNK
    proj_spec, halo_spec, small, state_specs, y_spec, _ = _mixer_specs(n_chunks, True)

    def body(proj_ref, halo_ref, dy_ref, hst_ref, cst_ref, nst_ref, mst_ref,
             lg_ref, hnw_ref, cw_ref, cb_ref, mnw_ref,
             dproj_ref, dbin_ref, dlg_ref, dhnw_ref, dcw_ref, dcb_ref, dmnw_ref,
             dhs, dcs, dns, dms, dhalo):
        c = pl.program_id(0)

        @pl.when(c == 0)
        def _():
            for r in (dhs, dcs, dns, dms, dhalo, dbin_ref, dlg_ref, dhnw_ref, dcw_ref, dcb_ref, dmnw_ref):
                r[...] = jnp.zeros_like(r)

        def put(cols, val):
            dproj_ref[:, cols] = val.astype(BF16)
            dbin_ref[:, cols] += jnp.sum(val, axis=0, keepdims=True)

        first = c == n_chunks - 1
        halo = jnp.where(first, 0.0, halo_ref[...])
        x_qk = proj_ref[:, pl.ds(4 * D_GRP, 2 * D_GRP)]
        conv_args = (halo, x_qk, cw_ref[0:1, :], cw_ref[1:2, :], cw_ref[2:3, :], cw_ref[3:4, :], cb_ref[...])
        qk, conv_vjp = jax.vjp(_qk_conv, *conv_args)
        hg_in, ml_in = _mixer_inputs(proj_ref, lg_ref, hnw_ref, mnw_ref, qk)
        _, hg_vjp = jax.vjp(_hg_chunk, hst_ref[0], *hg_in)
        _, ml_vjp = jax.vjp(_ml_chunk, cst_ref[0], nst_ref[0], _last(mst_ref[0], 0), *ml_in)
        dst, dhq, dhf, dhi, dhg, dl0, dl1, dnw = hg_vjp((dhs[...], dy_ref[:, pl.ds(0, D_GRP)]))
        dc, dn, dm, dq, dk, dv, dgates, dog, dmn = ml_vjp(
            (dcs[...], dns[...], _last(dms[...], 0), dy_ref[:, pl.ds(D_GRP, D_GRP)]))
        dhs[...] = dst
        dcs[...] = dc
        dns[...] = dn
        dms[...] = jnp.broadcast_to(dm, dms.shape)
        for i, val in ((0, dhq), (1, dhf), (2, dhi), (3, dhg), (6, dv), (7, dog)):
            put(_grp(i), val)
        put(pl.ds(8 * D_GRP, LANES), dgates)
        dlg_ref[0:1, :] += dl0
        dlg_ref[1:2, :] += dl1
        dhnw_ref[...] += dnw
        dmnw_ref[...] += dmn
        dh, dx, dw0, dw1, dw2, dw3, db = conv_vjp(jnp.concatenate([dq, dk], axis=1))
        tail = jnp.concatenate([jnp.zeros((CHUNK - SUBLANES, 2 * D_GRP), F32), dhalo[...]], axis=0)
        put(pl.ds(4 * D_GRP, 2 * D_GRP), dx + tail)
        dhalo[...] = dh
        for d, dw in enumerate((dw0, dw1, dw2, dw3)):
            dcw_ref[d:d + 1, :] += dw
        dcb_ref[...] += db

    row = pl.BlockSpec((1, D_GRP), lambda c: (0, 0))
    small_out = [pl.BlockSpec((1, D_IN_PAD), lambda c: (0, 0)), pl.BlockSpec((2, D_GRP), lambda c: (0, 0)), row,
                 pl.BlockSpec((ML_CONV, 2 * D_GRP), lambda c: (0, 0)), pl.BlockSpec((1, 2 * D_GRP), lambda c: (0, 0)), row]
    dy_spec = pl.BlockSpec((CHUNK, 2 * D_GRP), y_spec.index_map)
    vmem = 2 * (2 * _nbytes((CHUNK, D_IN_PAD), F32) + _nbytes((CHUNK, 2 * D_GRP), F32)
                + 2 * _nbytes((HEADS, DK, DK), F32)) + 2 * _nbytes((HEADS, DK, DK), F32) + 4 * 1024 * 1024
    return _pcall(
        body, name="mixer_bwd", grid=(n_chunks,),
        in_specs=[proj_spec, halo_spec, dy_spec] + state_specs + small,
        out_specs=[proj_spec] + small_out,
        out_shape=[jax.ShapeDtypeStruct((seq, D_IN_PAD), BF16), jax.ShapeDtypeStruct((1, D_IN_PAD), F32),
                   jax.ShapeDtypeStruct((2, D_GRP), F32), jax.ShapeDtypeStruct((1, D_GRP), F32),
                   jax.ShapeDtypeStruct((ML_CONV, 2 * D_GRP), F32), jax.ShapeDtypeStruct((1, 2 * D_GRP), F32),
                   jax.ShapeDtypeStruct((1, D_GRP), F32)],
        scratch_shapes=[pltpu.VMEM((HEADS, DK, DK), F32), pltpu.VMEM((HEADS, DK, DK), F32),
                        pltpu.VMEM((HEADS, 1, DK), F32), pltpu.VMEM((HEADS, 1, DK), F32),
                        pltpu.VMEM((SUBLANES, 2 * D_GRP), F32)],
        compiler_params=_params(("arbitrary",), vmem),
    )(proj, proj, dy, hst, cst, nst, mst, lb_logits, hg_nw, conv_w, conv_b, ml_nw)


def _tile(n, prefs, unit=None):
    unit = unit or n
    for p in prefs:
        if unit % p == 0 and n % p == 0:
            return p
    return unit


def _logical(arr):
    return arr.shape if arr.ndim == 2 else (arr.shape[1], arr.shape[0] * arr.shape[2])


def _group(arr):
    return arr.shape[-1]


def _split_spec(ndim, group, tr, tc, where):
    if ndim == 2:
        return pl.BlockSpec((tr, tc), where)
    per = group // tc
    assert per * tc == group, (group, tc)

    def index(*ids):
        bi, bj = where(*ids)
        return (bj // per, bi, bj % per)
    return pl.BlockSpec((None, tr, tc), index)


def _mm(name, mode, a, b, *, bias=None, res=None, res_scale=1.0, ln=None, out_dtype=F32, out_groups=None,
        copy_dtype=None, tm=None, tn=None, tk=None):
    la, lb = _logical(a), _logical(b)
    if mode == "nn":
        (m, k), n = la, lb[1]
        n_unit = _group(b) if b.ndim == 3 else n
        kc = _group(a) if a.ndim == 3 else k
    elif mode == "nt":
        (m, k), n = la, lb[0]
        n_unit = n
        kc = min(_group(a) if a.ndim == 3 else k, _group(b) if b.ndim == 3 else k)
    else:
        (k, m), n = la, lb[1]
        n_unit, kc = (_group(b) if b.ndim == 3 else n), k
        assert a.ndim == 2
    if out_groups:
        n_unit = min(n_unit, n // out_groups)
    kind = ln[0] if ln else None
    tm = tm or (256 if ln else _tile(m, (512, 256, 128)))
    tn = n if ln else (tn or _tile(n, (512, 384, 256, 128), n_unit))
    tk = (tk or _tile(k, (2048, 512, 256, 128))) if mode == "tn" else k
    gi, gj, gk = m // tm, n // tn, k // tk
    assert gi * tm == m and gj * tn == n and gk * tk == k and n_unit % tn == 0, (name, m, n, k, tm, tn, tk)
    ca, cb = {"nn": (1, 0), "nt": (1, 1), "tn": (0, 0)}[mode]
    i_outer = gk > 1 or (gi - 1) * _nbytes(b.shape, b.dtype) <= (gj - 1) * _nbytes(a.shape, a.dtype)

    def ij(where):
        return (lambda p, q, kk: where(p, q, kk)) if i_outer else (lambda p, q, kk: where(q, p, kk))
    if mode == "tn":
        a_spec = pl.BlockSpec((tk, tm), ij(lambda i, j, kk: (kk, i)))
    elif a.ndim == 3:
        a_spec = pl.BlockSpec((a.shape[0], tm, _group(a)), ij(lambda i, j, kk: (0, i, 0)))
    else:
        a_spec = pl.BlockSpec((tm, k), ij(lambda i, j, kk: (i, 0)))
    if mode != "nt":
        b_spec = _split_spec(b.ndim, _group(b), tk, tn, ij(lambda i, j, kk: (kk, j)))
    elif b.ndim == 3:
        b_spec = pl.BlockSpec((b.shape[0], tn, _group(b)), ij(lambda i, j, kk: (0, j, 0)))
    else:
        b_spec = pl.BlockSpec((tn, k), ij(lambda i, j, kk: (j, 0)))
    row_spec = pl.BlockSpec((1, tn), ij(lambda i, j, kk: (0, j)))
    blk_spec = pl.BlockSpec((tm, tn), ij(lambda i, j, kk: (i, j)))
    ins, in_specs = [a, b], [a_spec, b_spec]
    if bias is not None:
        ins.append(bias), in_specs.append(row_spec)
    if res is not None:
        ins.append(res), in_specs.append(blk_spec)
    if kind == "fwd":
        ins += [ln[1], ln[2]]
        in_specs += [row_spec, row_spec]
    elif kind == "loss":
        ins += [ln[1], ln[2], ln[3]]
        in_specs += [row_spec, row_spec, blk_spec]
    elif kind == "bwd":
        ins += [ln[1], ln[2], ln[3]]
        in_specs += [blk_spec, row_spec, row_spec]
    if out_groups:
        blk_out = jax.ShapeDtypeStruct((out_groups, m, n // out_groups), out_dtype)
        out_spec = _split_spec(3, n // out_groups, tm, tn, ij(lambda i, j, kk: (i, j)))
    else:
        blk_out, out_spec = jax.ShapeDtypeStruct((m, n), out_dtype), blk_spec
    row_out = jax.ShapeDtypeStruct((1, n), F32)
    if kind is None:
        out_shape, out_specs = [blk_out], [out_spec]
    elif kind == "fwd":
        out_shape, out_specs = [blk_out, blk_out], [blk_spec, blk_spec]
    else:
        out_shape, out_specs = [blk_out, row_out, row_out], [blk_spec, row_spec, row_spec]
        if kind == "loss":
            out_shape.append(jax.ShapeDtypeStruct((1, LANES), F32))
            out_specs.append(pl.BlockSpec((1, LANES), lambda p, q, kk: (0, 0)))
    if copy_dtype is not None:
        out_shape.append(jax.ShapeDtypeStruct((m, n), copy_dtype))
        out_specs.append(blk_spec)
    n_in = len(ins)

    def body(*refs):
        in_refs, out_refs, acc_ref = refs[:n_in], refs[n_in:n_in + len(out_shape)], refs[-1]
        i, kk = pl.program_id(0 if i_outer else 1), pl.program_id(2)
        a_ref, b_ref = in_refs[:2]
        extra = list(in_refs[2:])

        def epilogue(acc):
            rest = list(extra)
            if bias is not None:
                acc = acc + rest.pop(0)[...]
            if res is not None:
                acc = acc + res_scale * rest.pop(0)[...]
            if kind is None:
                out_refs[0][...] = acc.astype(out_dtype)
                return
            if kind == "fwd":
                out_refs[0][...] = acc
                y = _layer_norm(acc, rest[0][...], rest[1][...])
                out_refs[1][...] = y
                if copy_dtype is not None:
                    out_refs[-1][...] = y.astype(copy_dtype)
                return
            if kind == "loss":
                y, vjp = jax.vjp(_layer_norm, acc, rest[0][...], rest[1][...])
                err = y - rest[2][...]
                part = 0.5 * jnp.sum(jnp.sum(err * err, axis=1, keepdims=True), axis=0, keepdims=True) / n
                dz, dg, db = vjp(err / n)
            else:
                _, vjp = jax.vjp(_layer_norm, rest[0][...], rest[1][...], rest[2][...])
                dz, dg, db = vjp(acc)

            @pl.when(i == 0)
            def _():
                for r in out_refs[1:3 + (kind == "loss")]:
                    r[...] = jnp.zeros_like(r)

            out_refs[0][...] = dz
            out_refs[1][...] += dg
            out_refs[2][...] += db
            if kind == "loss":
                out_refs[3][...] += jnp.broadcast_to(part, (1, LANES))
            if copy_dtype is not None:
                out_refs[-1][...] = dz.astype(copy_dtype)

        def chunk(ref, c0, last):
            if ref.ndim == 3:
                g = ref.shape[2]
                return ref[c0 // g, :, pl.ds(c0 % g, kc)]
            return ref[:, pl.ds(c0, kc)] if last else ref[pl.ds(c0, kc), :]

        if mode == "tn" or kc == k:
            prod = _dg(a_ref[...], b_ref[...], ca, cb)
        else:
            prod = None
            for c0 in range(0, k, kc):
                part = _dg(chunk(a_ref, c0, True), chunk(b_ref, c0, mode == "nt"), ca, cb)
                prod = part if prod is None else prod + part
        if gk == 1:
            epilogue(prod)
            return

        @pl.when(kk == 0)
        def _():
            acc_ref[...] = prod

        @pl.when(kk > 0)
        def _():
            acc_ref[...] += prod

        @pl.when(kk == gk - 1)
        def _():
            epilogue(acc_ref[...])

    vmem = (2 * (_nbytes((tm, tk), a.dtype) + _nbytes((tk, tn), b.dtype))
            + (2 * len(ins) + 2 * len(out_shape) + 1) * _nbytes((tm, tn), F32))
    outs = _pcall(
        body, name=name, grid=(gi, gj, gk) if i_outer else (gj, gi, gk), in_specs=in_specs, out_specs=out_specs,
        out_shape=out_shape, scratch_shapes=[pltpu.VMEM((tm, tn) if gk > 1 else (SUBLANES, LANES), F32)],
        compiler_params=_params(("arbitrary", "arbitrary", "arbitrary"), vmem),
    )(*ins)
    return outs[0] if (kind is None and copy_dtype is None) else outs


def _attn_head(q, k, v):
    sc = mm_nt(q, k) * (CA_DH ** -0.5)
    e = jnp.exp(sc - jnp.max(sc, axis=-1, keepdims=True))
    return mm_nn(e / jnp.sum(e, axis=-1, keepdims=True), v)


def _attn_fwd(q, kv):
    seq, n_mem = q.shape[0], kv.shape[0]
    tq = _tile(seq, (512, 256, 128))

    def body(q_ref, kv_ref, o_ref):
        for h in range(HEADS):
            hd = pl.ds(h * CA_DH, CA_DH)
            o = _attn_head(q_ref[:, hd], kv_ref[:, hd], kv_ref[:, pl.ds(D_MODEL + h * CA_DH, CA_DH)])
            o_ref[:, hd] = o.astype(BF16)

    return _pcall(
        body, name="attn_fwd", grid=(seq // tq,),
        in_specs=[pl.BlockSpec((tq, D_MODEL), lambda i: (i, 0)), pl.BlockSpec((n_mem, 2 * D_MODEL), lambda i: (0, 0))],
        out_specs=pl.BlockSpec((tq, D_MODEL), lambda i: (i, 0)), out_shape=jax.ShapeDtypeStruct((seq, D_MODEL), BF16),
        compiler_params=_params(("arbitrary",), 4 * _nbytes((tq, D_MODEL), F32) + 2 * _nbytes((n_mem, 2 * D_MODEL), F32)),
    )(q, kv)


def _attn_bwd(q, kv, do):
    seq, n_mem = q.shape[0], kv.shape[0]
    tq = _tile(seq, (512, 256, 128))

    def body(q_ref, kv_ref, do_ref, dq_ref, dkv_ref):
        @pl.when(pl.program_id(0) == 0)
        def _():
            dkv_ref[...] = jnp.zeros_like(dkv_ref)

        for h in range(HEADS):
            hd = pl.ds(h * CA_DH, CA_DH)
            vd = pl.ds(D_MODEL + h * CA_DH, CA_DH)
            _, vjp = jax.vjp(_attn_head, q_ref[:, hd], kv_ref[:, hd], kv_ref[:, vd])
            dq, dk, dv = vjp(do_ref[:, hd].astype(F32))
            dq_ref[:, hd] = dq.astype(BF16)
            dkv_ref[:, hd] += dk
            dkv_ref[:, vd] += dv

    return _pcall(
        body, name="attn_bwd", grid=(seq // tq,),
        in_specs=[pl.BlockSpec((tq, D_MODEL), lambda i: (i, 0)), pl.BlockSpec((n_mem, 2 * D_MODEL), lambda i: (0, 0)),
                  pl.BlockSpec((tq, D_MODEL), lambda i: (i, 0))],
        out_specs=[pl.BlockSpec((tq, D_MODEL), lambda i: (i, 0)), pl.BlockSpec((n_mem, 2 * D_MODEL), lambda i: (0, 0))],
        out_shape=[jax.ShapeDtypeStruct((seq, D_MODEL), BF16), jax.ShapeDtypeStruct((n_mem, 2 * D_MODEL), F32)],
        compiler_params=_params(("arbitrary",), 6 * _nbytes((tq, D_MODEL), F32) + 4 * _nbytes((n_mem, 2 * D_MODEL), F32)),
    )(q, kv, do)


def _ffn_mid(hg, xg, hv, xv, wg0, wg1, wg2, bg, wv0, wv1, wv2, bv):
    return jax.nn.gelu(causal_conv(hg, xg, (wg0, wg1, wg2), bg)) * causal_conv(hv, xv, (wv0, wv1, wv2), bv)


FFN_TB = 256
FFN_W = D_FF // 2
FFN_J = D_FF // FFN_W
MXU_COLS = 256
FFN_PIECES = tuple((off, min(MXU_COLS, FFN_W - off)) for off in range(0, FFN_W, MXU_COLS))


def _ffn_common_specs(seq, row):
    tb = min(FFN_TB, seq)
    full = pl.BlockSpec((tb, D_MODEL), lambda t, j: (row(t), 0))
    vec = pl.BlockSpec((1, D_MODEL), lambda t, j: (0, 0))
    halves = []
    for off in (0, FFN_J):
        halves.append(dict(
            w_up=pl.BlockSpec((D_MODEL, FFN_W), lambda t, j, off=off: (0, j + off)),
            taps=pl.BlockSpec((FFN_CONV, FFN_W), lambda t, j, off=off: (0, j + off)),
            bias=pl.BlockSpec((1, FFN_W), lambda t, j, off=off: (0, j + off))))
    w_down = pl.BlockSpec((FFN_W, D_MODEL), lambda t, j: (j, 0))
    u_blk = pl.BlockSpec((2, tb, FFN_W), lambda t, j: (0, row(t), j))
    return tb, full, vec, halves, w_down, u_blk


def _ffn_vmem(tb):
    return (_nbytes((2, tb, FFN_W), F32) + _nbytes((2, tb, FFN_W), BF16) + 3 * _nbytes((D_MODEL, FFN_W), BF16)
            + 10 * _nbytes((tb, D_MODEL), F32))


def _conv_params(taps_ref, bias_ref, cols):
    return taps_ref[0:1, cols], taps_ref[1:2, cols], taps_ref[2:3, cols], bias_ref[:, cols]


def _ffn_fwd(x2b, x2, w_up, conv_w, conv_b, w_down, ln_g, ln_b, target):
    seq = x2.shape[0]
    tb, full, vec, halves, wd_spec, u_blk = _ffn_common_specs(seq, lambda t: t)
    nt = seq // tb

    def body(xb_ref, wg_ref, wv_ref, tg_ref, tv_ref, bg_ref, bv_ref, wd_ref, x_ref, g_ref, b_ref, tgt_ref,
             u_ref, h_ref, dz_ref, dg_ref, db_ref, loss_ref, dzb_ref, acc, carry):
        t, j = pl.program_id(0), pl.program_id(1)
        xb = xb_ref[...]
        pieces = [pl.ds(off, width) for off, width in FFN_PIECES]
        ug = [_dg(xb, wg_ref[:, cols], 1, 0) for cols in pieces]
        uv = [_dg(xb, wv_ref[:, cols], 1, 0) for cols in pieces]
        hs = []
        for cols, g, v in zip(pieces, ug, uv):
            u_ref[0, :, cols] = g
            u_ref[1, :, cols] = v
            halo_g = jnp.where(t == 0, 0.0, carry[j, 0, :, cols])
            halo_v = jnp.where(t == 0, 0.0, carry[j, 1, :, cols])
            h = _ffn_mid(halo_g, g, halo_v, v, *_conv_params(tg_ref, bg_ref, cols),
                         *_conv_params(tv_ref, bv_ref, cols)).astype(BF16)
            carry[j, 0, :, cols] = g[tb - SUBLANES:, :]
            carry[j, 1, :, cols] = v[tb - SUBLANES:, :]
            h_ref[:, cols] = h
            hs.append(h)
        part = None
        for cols, h in zip(pieces, hs):
            p = _dg(h, wd_ref[cols, :], 1, 0)
            part = p if part is None else part + p

        @pl.when(j == 0)
        def _():
            acc[...] = part

        @pl.when(j > 0)
        def _():
            acc[...] += part

        @pl.when(j == FFN_J - 1)
        def _():
            y, vjp = jax.vjp(_layer_norm, acc[...] + ALPHA * x_ref[...], g_ref[...], b_ref[...])
            err = y - tgt_ref[...]
            part_loss = 0.5 * jnp.sum(jnp.sum(err * err, axis=1, keepdims=True), axis=0, keepdims=True) / D_MODEL
            dz, dg, db = vjp(err / D_MODEL)

            @pl.when(t == 0)
            def _():
                for r in (dg_ref, db_ref, loss_ref):
                    r[...] = jnp.zeros_like(r)

            dz_ref[...] = dz
            dzb_ref[...] = dz.astype(BF16)
            dg_ref[...] += dg
            db_ref[...] += db
            loss_ref[...] += jnp.broadcast_to(part_loss, (1, LANES))

    h0, h1 = halves
    row = jax.ShapeDtypeStruct((1, D_MODEL), F32)
    return _pcall(
        body, name="ffn_fwd", grid=(nt, FFN_J),
        in_specs=[full, h0["w_up"], h1["w_up"], h0["taps"], h1["taps"], h0["bias"], h1["bias"], wd_spec, full, vec, vec,
                  full],
        out_specs=[u_blk, pl.BlockSpec((tb, FFN_W), lambda t, j: (t, j)), full, vec, vec,
                   pl.BlockSpec((1, LANES), lambda t, j: (0, 0)), full],
        out_shape=[jax.ShapeDtypeStruct((2, seq, D_FF), F32), jax.ShapeDtypeStruct((seq, D_FF), BF16),
                   jax.ShapeDtypeStruct((seq, D_MODEL), F32), row, row, jax.ShapeDtypeStruct((1, LANES), F32),
                   jax.ShapeDtypeStruct((seq, D_MODEL), BF16)],
        scratch_shapes=[pltpu.VMEM((tb, D_MODEL), F32), pltpu.VMEM((FFN_J, 2, SUBLANES, FFN_W), F32)],
        compiler_params=_params(("arbitrary", "arbitrary"), _ffn_vmem(tb)),
    )(x2b, w_up, w_up, conv_w, conv_w, conv_b, conv_b, w_down, x2, ln_g, ln_b, target)


def _ffn_bwd(u, conv_w, conv_b, dz3b, dz3, w_down, w_up, z2, ln_g, ln_b):
    seq = dz3.shape[0]
    tb = min(FFN_TB, seq)
    nt = seq // tb
    row8 = tb // SUBLANES
    tb, full, vec, halves, wd_spec, u_blk = _ffn_common_specs(seq, lambda t: nt - 1 - t)
    halo = pl.BlockSpec((2, SUBLANES, FFN_W), lambda t, j: (0, jnp.maximum((nt - 1 - t) * row8 - 1, 0), j))

    def body(u_ref, halo_ref, tg_ref, tv_ref, bg_ref, bv_ref, dzb_ref, wd_ref, wg_ref, wv_ref, dz3_ref, z_ref, g_ref,
             b_ref, du_ref, dw_ref, dbias_ref, dz_ref, dg_ref, db_ref, dz2b_ref, acc, carry):
        t, j = pl.program_id(0), pl.program_id(1)

        @pl.when((t == 0) & (j == 0))
        def _():
            for r in (dw_ref, dbias_ref, dg_ref, db_ref):
                r[...] = jnp.zeros_like(r)

        pieces = [pl.ds(off, width) for off, width in FFN_PIECES]
        dzb = dzb_ref[...]
        dhs = [_dg(dzb, wd_ref[cols, :], 1, 1) for cols in pieces]
        first = t == nt - 1
        dus = []
        for cols, dh in zip(pieces, dhs):
            args = (jnp.where(first, 0.0, halo_ref[0, :, cols]), u_ref[0, :, cols],
                    jnp.where(first, 0.0, halo_ref[1, :, cols]), u_ref[1, :, cols],
                    *_conv_params(tg_ref, bg_ref, cols), *_conv_params(tv_ref, bv_ref, cols))
            _, vjp = jax.vjp(_ffn_mid, *args)
            dhg, dxg, dhv, dxv, g0, g1, g2, gb, v0, v1, v2, vb = vjp(dh)
            zeros = jnp.zeros((tb - SUBLANES, dh.shape[1]), F32)
            dug = (dxg + jnp.concatenate([zeros, jnp.where(t == 0, 0.0, carry[j, 0, :, cols])], axis=0)).astype(BF16)
            duv = (dxv + jnp.concatenate([zeros, jnp.where(t == 0, 0.0, carry[j, 1, :, cols])], axis=0)).astype(BF16)
            carry[j, 0, :, cols] = dhg
            carry[j, 1, :, cols] = dhv
            du_ref[0, :, cols] = dug
            du_ref[1, :, cols] = duv
            for half, parts in enumerate(((g0, g1, g2), (v0, v1, v2))):
                for d, p in enumerate(parts):
                    dw_ref[j, half, d:d + 1, cols] += p
            dbias_ref[j, 0, :, cols] += gb
            dbias_ref[j, 1, :, cols] += vb
            dus.append((dug, duv))
        part = None
        for cols, (dug, duv) in zip(pieces, dus):
            p = _dg(dug, wg_ref[:, cols], 1, 1) + _dg(duv, wv_ref[:, cols], 1, 1)
            part = p if part is None else part + p

        @pl.when(j == 0)
        def _():
            acc[...] = part

        @pl.when(j > 0)
        def _():
            acc[...] += part

        @pl.when(j == FFN_J - 1)
        def _():
            _, ln_vjp = jax.vjp(_layer_norm, z_ref[...], g_ref[...], b_ref[...])
            dz, dg, db = ln_vjp(acc[...] + ALPHA * dz3_ref[...])
            dz_ref[...] = dz
            dz2b_ref[...] = dz.astype(BF16)
            dg_ref[...] += dg
            db_ref[...] += db

    h0, h1 = halves
    row = jax.ShapeDtypeStruct((1, D_MODEL), F32)
    whole = lambda *shape: pl.BlockSpec(shape, lambda t, j: (0,) * len(shape))
    return _pcall(
        body, name="ffn_bwd", grid=(nt, FFN_J),
        in_specs=[u_blk, halo, h0["taps"], h1["taps"], h0["bias"], h1["bias"], full, wd_spec, h0["w_up"], h1["w_up"],
                  full, full, vec, vec],
        out_specs=[u_blk, whole(FFN_J, 2, FFN_CONV, FFN_W), whole(FFN_J, 2, 1, FFN_W), full, vec, vec, full],
        out_shape=[jax.ShapeDtypeStruct((2, seq, D_FF), BF16), jax.ShapeDtypeStruct((FFN_J, 2, FFN_CONV, FFN_W), F32),
                   jax.ShapeDtypeStruct((FFN_J, 2, 1, FFN_W), F32), jax.ShapeDtypeStruct((seq, D_MODEL), F32), row, row,
                   jax.ShapeDtypeStruct((seq, D_MODEL), BF16)],
        scratch_shapes=[pltpu.VMEM((tb, D_MODEL), F32), pltpu.VMEM((FFN_J, 2, SUBLANES, FFN_W), F32)],
        compiler_params=_params(("arbitrary", "arbitrary"), _ffn_vmem(tb)),
    )(u, u, conv_w, conv_w, conv_b, conv_b, dz3b, w_down, w_up, w_up, dz3, z2, ln_g, ln_b)


def _adamw_math(w, g, m, v):
    m_new = ADAM_B1 * m + (1.0 - ADAM_B1) * g
    v_new = ADAM_B2 * v + (1.0 - ADAM_B2) * jnp.square(g)
    m_hat = m_new / (1.0 - ADAM_B1 ** ADAM_STEP)
    v_hat = v_new / (1.0 - ADAM_B2 ** ADAM_STEP)
    return -ADAM_LR * (m_hat / (jnp.sqrt(v_hat) + ADAM_EPS) + ADAM_WD * w), m_new, v_new


def _adamw(name, w, g, m, v):
    rows, cols = w.shape
    tr = _tile(rows, (256, 176, 128, 64, 40, 32, 16, 8))

    def body(w_ref, g_ref, m_ref, v_ref, d_ref, nm_ref, nv_ref):
        d_ref[...], nm_ref[...], nv_ref[...] = _adamw_math(w_ref[...], g_ref[...], m_ref[...], v_ref[...])

    spec = pl.BlockSpec((tr, cols), lambda i: (i, 0))
    sh = jax.ShapeDtypeStruct((rows, cols), F32)
    return _pcall(
        body, name=name, grid=(rows // tr,), in_specs=[spec] * 4, out_specs=[spec] * 3, out_shape=[sh] * 3,
        compiler_params=_params(("arbitrary",), 14 * _nbytes((tr, -(-cols // LANES) * LANES), F32)),
    )(w, g, m, v)


def _adamw_halves(name, core, w, mine, theirs, m, v):
    rows, cols = w.shape
    tr = _tile(rows // 2, (256, 176, 128))
    nbh = rows // 2 // tr

    def body(c_ref, w_ref, a_ref, b_ref, m_ref, v_ref, g_ref, d_ref, nm_ref, nv_ref):
        g = jnp.where(pl.program_id(0) // nbh == c_ref[0], a_ref[...], b_ref[...])
        g_ref[...] = g
        d_ref[...], nm_ref[...], nv_ref[...] = _adamw_math(w_ref[...], g, m_ref[...], v_ref[...])

    spec = pl.BlockSpec((tr, cols), lambda i, c_ref: (i, 0))
    half = pl.BlockSpec((tr, cols), lambda i, c_ref: (i % nbh, 0))
    sh = jax.ShapeDtypeStruct((rows, cols), F32)
    grid_spec = pltpu.PrefetchScalarGridSpec(
        num_scalar_prefetch=1, grid=(rows // tr,), in_specs=[spec, half, half, spec, spec], out_specs=[spec] * 4)
    return _pcall(
        body, name=name, grid_spec=grid_spec, out_shape=[sh] * 4,
        compiler_params=_params(("arbitrary",), 18 * _nbytes((tr, -(-cols // LANES) * LANES), F32)),
    )(core, w, mine, theirs, m, v)


MESH = pl.DeviceIdType.MESH
ANY = pl.BlockSpec(memory_space=pl.ANY)
N_CHIPS = 4
N_DEV = 8
BF16_ROWS = 16


def _me():
    return lax.axis_index("x"), lax.axis_index("y"), lax.axis_index("c")


def _other_chips(x, y):
    return [(1 - x, y), (x, 1 - y), (1 - x, 1 - y)]


def _remote(src, dst, ssem, rsem, dev):
    return pltpu.make_async_remote_copy(src_ref=src, dst_ref=dst, send_sem=ssem, recv_sem=rsem,
                                        device_id=dev, device_id_type=MESH)


def _half_rows(ref_rows, cc):
    half = ref_rows // 2
    return pl.ds(pl.multiple_of(cc * half, BF16_ROWS), half)


def _gather_weights(shards):
    n = len(shards)
    n_ici = n * (N_CHIPS - 1)

    def body(*refs):
        ins, outs, (ssem, rsem, lsem, lrsem) = refs[:n], refs[n:2 * n], refs[2 * n:]
        x, y, c = _me()
        k_me = 2 * x + y
        sib = (x, y, 1 - c)
        chips = _other_chips(x, y)
        started = []
        for i, (w_ref, o_ref) in enumerate(zip(ins, outs)):
            cp = _remote(w_ref, o_ref.at[k_me], lsem.at[i], lrsem.at[i], sib)
            cp.start()
            started.append(cp)
        for r, (px, py) in enumerate(chips):
            for i, (w_ref, o_ref) in enumerate(zip(ins, outs)):
                rows = _half_rows(w_ref.shape[0], c)
                s = r * n + i
                cp = _remote(w_ref.at[rows], o_ref.at[k_me, rows], ssem.at[s], rsem.at[s], (px, py, c))
                cp.start()
                started.append(cp)
        for r, (px, py) in enumerate(chips):
            for i, o_ref in enumerate(outs):
                blk = o_ref.at[2 * px + py, _half_rows(o_ref.shape[1], c)]
                s = r * n + i
                _remote(blk, blk, ssem.at[s], rsem.at[s], (px, py, c)).wait_recv()
                cp = _remote(blk, blk, ssem.at[n_ici + s], rsem.at[n_ici + s], sib)
                cp.start()
                started.append(cp)
        for r, (px, py) in enumerate(chips):
            for i, o_ref in enumerate(outs):
                blk = o_ref.at[2 * px + py, _half_rows(o_ref.shape[1], 1 - c)]
                s = n_ici + r * n + i
                _remote(blk, blk, ssem.at[s], rsem.at[s], sib).wait_recv()
        for cp in started[n:]:
            cp.wait_send()
        for cp in started[:n]:
            cp.wait()

    return _pcall(
        body, name="gather_weights", in_specs=[ANY] * n, out_specs=[ANY] * n,
        out_shape=[jax.ShapeDtypeStruct((N_CHIPS,) + s.shape, s.dtype) for s in shards],
        scratch_shapes=[pltpu.SemaphoreType.DMA((2 * n_ici,)), pltpu.SemaphoreType.DMA((2 * n_ici,)),
                        pltpu.SemaphoreType.DMA((n,)), pltpu.SemaphoreType.DMA((n,))],
    )(*shards)


def _swap_halves(name, grads):
    n = len(grads)

    def body(*refs):
        ins, outs, (ssem, rsem) = refs[:n], refs[n:2 * n], refs[2 * n:]
        x, y, c = _me()
        copies = []
        for i, (g_ref, o_ref) in enumerate(zip(ins, outs)):
            for k in range(N_CHIPS):
                s = i * N_CHIPS + k
                cp = _remote(g_ref.at[k, _half_rows(g_ref.shape[1], 1 - c)], o_ref.at[k], ssem.at[s], rsem.at[s],
                             (x, y, 1 - c))
                cp.start()
                copies.append(cp)
        for cp in copies:
            cp.wait()

    return _pcall(
        body, name=name, in_specs=[ANY] * n, out_specs=[ANY] * n,
        out_shape=[jax.ShapeDtypeStruct((N_CHIPS, g.shape[1] // 2, g.shape[2]), g.dtype) for g in grads],
        scratch_shapes=[pltpu.SemaphoreType.DMA((n * N_CHIPS,)), pltpu.SemaphoreType.DMA((n * N_CHIPS,))],
    )(*grads)


SEM = pl.BlockSpec(memory_space=pltpu.SEMAPHORE)
IN_HBM = pl.BlockSpec(memory_space=pltpu.HBM)
SPLIT_PARAMS = dict(compiler_params=pltpu.CompilerParams(has_side_effects=pltpu.SideEffectType.DATAFLOW_SIDE_EFFECTING))


def _gather_start(name, shards):
    n = len(shards)
    n_sem = n * N_CHIPS

    def body(*refs):
        ins, lands, (ssem, rsem), token = refs[:n], refs[n:2 * n], refs[2 * n:2 * n + 2], refs[-1]
        x, y, c = _me()
        k_me = 2 * x + y
        for i, (w_ref, l_ref) in enumerate(zip(ins, lands)):
            _remote(w_ref, l_ref.at[k_me], ssem.at[i], rsem.at[i], (x, y, 1 - c)).start()
        for r, (px, py) in enumerate(_other_chips(x, y)):
            for i, (w_ref, l_ref) in enumerate(zip(ins, lands)):
                rows = _half_rows(w_ref.shape[0], c)
                s = (r + 1) * n + i
                _remote(w_ref.at[rows], l_ref.at[k_me, rows], ssem.at[s], rsem.at[s], (px, py, c)).start()
        token[...] = jnp.zeros_like(token)

    src = [pltpu.HBM(s.shape, s.dtype) for s in shards]
    dst = [pltpu.HBM((N_CHIPS,) + s.shape, s.dtype) for s in shards]
    outs = _call(
        body, name=name, in_specs=[IN_HBM] * (2 * n),
        out_specs=[SEM, SEM] + [IN_HBM] * (2 * n) + [pl.BlockSpec(memory_space=pltpu.VMEM)],
        out_shape=[pltpu.SemaphoreType.DMA((n_sem,)), pltpu.SemaphoreType.DMA((n_sem,))] + src + dst
        + [jax.ShapeDtypeStruct((SUBLANES, LANES), F32)],
        input_output_aliases={i: 2 + i for i in range(2 * n)}, **SPLIT_PARAMS,
    )(*[pltpu.with_memory_space_constraint(s, pltpu.HBM) for s in shards],
      *[pltpu.with_memory_space_constraint(lax.empty(d.shape, d.dtype), pltpu.HBM) for d in dst])
    return outs[:-1], outs[-1]


def _gather_wait(name, handle, after):
    ssem, rsem, thru = handle[0], handle[1], handle[2:]
    n = len(thru) // 2

    def body(*refs):
        ins, lands, (ssem_ref, rsem_ref) = refs[:n], refs[n:2 * n], refs[2 * n:2 * n + 2]
        x, y, c = _me()
        k_me = 2 * x + y
        for i, (w_ref, l_ref) in enumerate(zip(ins, lands)):
            cp = _remote(w_ref, l_ref.at[k_me], ssem_ref.at[i], rsem_ref.at[i], (x, y, 1 - c))
            cp.wait_send()
            cp.wait_recv()
        for r, (px, py) in enumerate(_other_chips(x, y)):
            for i, (w_ref, l_ref) in enumerate(zip(ins, lands)):
                rows = _half_rows(w_ref.shape[0], c)
                s = (r + 1) * n + i
                cp = _remote(w_ref.at[rows], l_ref.at[2 * px + py, rows], ssem_ref.at[s], rsem_ref.at[s], (px, py, c))
                cp.wait_send()
                cp.wait_recv()

    outs = _call(
        body, name=name, in_specs=[IN_HBM] * (2 * n) + [SEM, SEM, ANY], out_specs=[IN_HBM] * (2 * n),
        out_shape=[pltpu.HBM(t.shape, t.dtype) for t in thru],
        input_output_aliases={i: i for i in range(2 * n)}, **SPLIT_PARAMS,
    )(*thru, ssem, rsem, after)
    return outs[n:]


def _forward_halves(name, blocks):
    n = len(blocks)
    n_sem = n * (N_CHIPS - 1)

    def body(*refs):
        outs, (ssem, rsem) = refs[n:2 * n], refs[2 * n:]
        x, y, c = _me()
        sib = (x, y, 1 - c)
        chips = _other_chips(x, y)
        sends = []
        for r, (px, py) in enumerate(chips):
            for i, o_ref in enumerate(outs):
                blk = o_ref.at[2 * px + py, _half_rows(o_ref.shape[1], c)]
                cp = _remote(blk, blk, ssem.at[r * n + i], rsem.at[r * n + i], sib)
                cp.start()
                sends.append(cp)
        for r, (px, py) in enumerate(chips):
            for i, o_ref in enumerate(outs):
                blk = o_ref.at[2 * px + py, _half_rows(o_ref.shape[1], 1 - c)]
                _remote(blk, blk, ssem.at[r * n + i], rsem.at[r * n + i], sib).wait_recv()
        for cp in sends:
            cp.wait_send()

    return _pcall(
        body, name=name, in_specs=[ANY] * n, out_specs=[ANY] * n,
        out_shape=[jax.ShapeDtypeStruct(b.shape, b.dtype) for b in blocks],
        input_output_aliases={i: i for i in range(n)},
        scratch_shapes=[pltpu.SemaphoreType.DMA((n_sem,)), pltpu.SemaphoreType.DMA((n_sem,))],
    )(*blocks)


def _scatter_start(name, parts):
    n = len(parts)
    n_sem = n * (N_CHIPS - 1)

    def body(*refs):
        ins, lands, (ssem, rsem), token = refs[:n], refs[n:2 * n], refs[2 * n:2 * n + 2], refs[-1]
        x, y, c = _me()
        k_me = 2 * x + y
        for r, (px, py) in enumerate(_other_chips(x, y)):
            for i, (p_ref, l_ref) in enumerate(zip(ins, lands)):
                s = r * n + i
                _remote(p_ref.at[2 * px + py], l_ref.at[k_me], ssem.at[s], rsem.at[s], (px, py, c)).start()
        token[...] = jnp.zeros_like(token)

    hbm = [pltpu.HBM(p.shape, p.dtype) for p in parts]
    outs = _call(
        body, name=name, in_specs=[IN_HBM] * (2 * n),
        out_specs=[SEM, SEM] + [IN_HBM] * (2 * n) + [pl.BlockSpec(memory_space=pltpu.VMEM)],
        out_shape=[pltpu.SemaphoreType.DMA((n_sem,)), pltpu.SemaphoreType.DMA((n_sem,))] + hbm + hbm
        + [jax.ShapeDtypeStruct((SUBLANES, LANES), F32)],
        input_output_aliases={i: 2 + i for i in range(2 * n)}, **SPLIT_PARAMS,
    )(*[pltpu.with_memory_space_constraint(p, pltpu.HBM) for p in parts],
      *[pltpu.with_memory_space_constraint(lax.empty(p.shape, p.dtype), pltpu.HBM) for p in parts])
    return outs[:-1], outs[-1]


def _scatter_wait(name, handle, after):
    ssem, rsem, thru = handle[0], handle[1], handle[2:]
    n = len(thru) // 2

    def body(*refs):
        ins, lands, (ssem_ref, rsem_ref) = refs[:n], refs[n:2 * n], refs[2 * n:2 * n + 2]
        x, y, c = _me()
        for r, (px, py) in enumerate(_other_chips(x, y)):
            for i, (p_ref, l_ref) in enumerate(zip(ins, lands)):
                s = r * n + i
                cp = _remote(p_ref.at[2 * px + py], l_ref.at[2 * px + py], ssem_ref.at[s], rsem_ref.at[s], (px, py, c))
                cp.wait_send()
                cp.wait_recv()

    outs = _call(
        body, name=name, in_specs=[IN_HBM] * (2 * n) + [SEM, SEM, ANY], out_specs=[IN_HBM] * (2 * n),
        out_shape=[pltpu.HBM(t.shape, t.dtype) for t in thru],
        input_output_aliases={i: i for i in range(2 * n)}, **SPLIT_PARAMS,
    )(*thru, ssem, rsem, after)
    return outs[n:]


def _share_halves(halves):
    n = len(halves)

    def body(*refs):
        ins, outs, (ssem, rsem) = refs[:n], refs[n:2 * n], refs[2 * n:]
        x, y, c = _me()
        copies = [_remote(r_ref, o_ref, ssem.at[i], rsem.at[i], (x, y, 1 - c))
                  for i, (r_ref, o_ref) in enumerate(zip(ins, outs))]
        for cp in copies:
            cp.start()
        for cp in copies:
            cp.wait()

    return _pcall(
        body, name="share_halves", in_specs=[ANY] * n, out_specs=[ANY] * n,
        out_shape=[jax.ShapeDtypeStruct(h.shape, h.dtype) for h in halves],
        scratch_shapes=[pltpu.SemaphoreType.DMA((n,)), pltpu.SemaphoreType.DMA((n,))],
    )(*halves)


def _exchange_small(v, reduce):
    rows = v.shape[0]

    def body(v_ref, out_ref, buf, ssem, rsem):
        x, y, c = _me()
        me = 4 * x + 2 * y + c
        peers = [((x + bx) % 2, (y + by) % 2, (c + bc) % 2)
                 for bx in (0, 1) for by in (0, 1) for bc in (0, 1) if (bx, by, bc) != (0, 0, 0)]
        dst = buf if reduce else out_ref
        dst[me] = v_ref[...]
        sends = [_remote(v_ref, dst.at[me], ssem.at[r], rsem.at[r], p) for r, p in enumerate(peers)]
        for cp in sends:
            cp.start()
        for r, (px, py, pc) in enumerate(peers):
            blk = dst.at[4 * px + 2 * py + pc]
            _remote(blk, blk, ssem.at[r], rsem.at[r], (px, py, pc)).wait_recv()
        if reduce:
            acc = buf[0]
            for d in range(1, N_DEV):
                acc = acc + buf[d]
            out_ref[...] = acc
        for cp in sends:
            cp.wait_send()

    vm = pl.BlockSpec(memory_space=pltpu.VMEM)
    out_shape = jax.ShapeDtypeStruct((rows, LANES) if reduce else (N_DEV, rows, LANES), F32)
    buf_shape = (N_DEV, rows, LANES) if reduce else (SUBLANES, LANES)
    return _pcall(
        body, pin=False, name="reduce_small" if reduce else "gather_small", in_specs=[vm], out_specs=vm, out_shape=out_shape,
        scratch_shapes=[pltpu.VMEM(buf_shape, F32), pltpu.SemaphoreType.DMA((N_DEV - 1,)),
                        pltpu.SemaphoreType.DMA((N_DEV - 1,))],
        compiler_params=pltpu.CompilerParams(vmem_limit_bytes=32 * 1024 * 1024),
    )(v)


def _add_pair(name, core, g, theirs):
    _, half, cols = theirs.shape
    tr = _tile(half, (256, 176, 128))
    nb = half // tr

    def body(c_ref, g_ref, t_ref, o32_ref, o16_ref):
        s = g_ref[...] + t_ref[...]
        o32_ref[...] = s
        o16_ref[...] = s.astype(BF16)

    spec = pl.BlockSpec((None, tr, cols), lambda k, i, c_ref: (k, i, 0))
    grid_spec = pltpu.PrefetchScalarGridSpec(
        num_scalar_prefetch=1, grid=(N_CHIPS, nb),
        in_specs=[pl.BlockSpec((None, tr, cols), lambda k, i, c_ref: (k, c_ref[0] * nb + i, 0)), spec],
        out_specs=[spec, spec])
    return _pcall(
        body, name=name, grid_spec=grid_spec,
        out_shape=[jax.ShapeDtypeStruct(theirs.shape, F32), jax.ShapeDtypeStruct(theirs.shape, BF16)],
        compiler_params=_params(("arbitrary", "arbitrary"), 8 * _nbytes((tr, cols + LANES), F32)),
    )(core, g, theirs)


def _add_chips(name, chip, p32, recv):
    _, half, cols = p32.shape
    tr = _tile(half, (256, 176, 128))

    def body(k_ref, p_ref, r0_ref, r1_ref, r2_ref, o_ref):
        o_ref[...] = ((p_ref[...] + r0_ref[...].astype(F32)) + r1_ref[...].astype(F32)) + r2_ref[...].astype(F32)

    def other(r):
        return pl.BlockSpec((None, tr, cols), lambda i, k_ref: (r + (k_ref[0] <= r).astype(jnp.int32), i, 0))
    grid_spec = pltpu.PrefetchScalarGridSpec(
        num_scalar_prefetch=1, grid=(half // tr,),
        in_specs=[pl.BlockSpec((None, tr, cols), lambda i, k_ref: (k_ref[0], i, 0)), other(0), other(1), other(2)],
        out_specs=pl.BlockSpec((tr, cols), lambda i, k_ref: (i, 0)))
    return _pcall(
        body, name=name, grid_spec=grid_spec, out_shape=jax.ShapeDtypeStruct((half, cols), F32),
        compiler_params=_params(("arbitrary",), 10 * _nbytes((tr, cols + LANES), F32)),
    )(chip, p32, recv, recv, recv)


def kernel(x, mem, w_in, b_in, hg_lb_logits, hg_norm_w, ml_conv_w, ml_conv_b, ml_norm_w, w_out, ln1_g, ln1_b, ca_wq, ca_wkv, ca_wo, ln2_g, ln2_b, ffn_w_up, ffn_conv_w, ffn_conv_b, ffn_w_down, ln3_g, ln3_b, loss_target, m_w_in, m_b_in, m_hg_lb_logits, m_hg_norm_w, m_ml_conv_w, m_ml_conv_b, m_ml_norm_w, m_w_out, m_ln1_g, m_ln1_b, m_ca_wq, m_ca_wkv, m_ca_wo, m_ln2_g, m_ln2_b, m_ffn_w_up, m_ffn_conv_w, m_ffn_conv_b, m_ffn_w_down, m_ln3_g, m_ln3_b, v_w_in, v_b_in, v_hg_lb_logits, v_hg_norm_w, v_ml_conv_w, v_ml_conv_b, v_ml_norm_w, v_w_out, v_ln1_g, v_ln1_b, v_ca_wq, v_ca_wkv, v_ca_wo, v_ln2_g, v_ln2_b, v_ffn_w_up, v_ffn_conv_w, v_ffn_conv_b, v_ffn_w_down, v_ln3_g, v_ln3_b):
    return _train_step(dict(locals()))


WEIGHTS = ("w_in", "b_in", "hg_lb_logits", "hg_norm_w", "ml_conv_w", "ml_conv_b", "ml_norm_w", "w_out", "ln1_g",
           "ln1_b", "ca_wq", "ca_wkv", "ca_wo", "ln2_g", "ln2_b", "ffn_w_up", "ffn_conv_w", "ffn_conv_b",
           "ffn_w_down", "ln3_g", "ln3_b")
MATRICES = ("w_in", "w_out", "ca_wq", "ca_wkv", "ca_wo", "ffn_w_up", "ffn_w_down")
COL_SHARDED = ("w_in", "ca_wkv", "ffn_w_up", "ml_conv_w", "ffn_conv_w")
SMALL = tuple(n for n in WEIGHTS if n not in MATRICES)
PART_ROWS = 16


def _part_rows(shape, lead):
    n = 1
    for s in shape[lead:]:
        n *= s
    return -(-n // (LANES * PART_ROWS)) * PART_ROWS


def _pack(arrs, dtype, lead=0, rows=None):
    parts = []
    for a in arrs:
        head = a.shape[:lead]
        flat = a.reshape(head + (-1,)).astype(dtype)
        pad = _part_rows(a.shape, lead) * LANES - flat.shape[-1]
        flat = jnp.pad(flat, [(0, 0)] * lead + [(0, pad)])
        parts.append(flat.reshape(head + (-1, LANES)))
    used = sum(p.shape[lead] for p in parts)
    if rows is not None and rows > used:
        parts.append(jnp.zeros(parts[0].shape[:lead] + (rows - used, LANES), dtype))
    return jnp.concatenate(parts, axis=lead)


def _unpack(buf, shapes):
    lead = buf.shape[:-2]
    outs, r = [], 0
    for sh in shapes:
        n = 1
        for s in sh:
            n *= s
        nr = _part_rows(sh, 0)
        flat = buf[..., r:r + nr, :].reshape(lead + (nr * LANES,))
        outs.append(flat[..., :n].reshape(lead + tuple(sh)))
        r += nr
    return outs


def _cat_cols(s):
    return jnp.moveaxis(s, 0, 1).reshape(s.shape[1], -1)


def _split_cols(g):
    return jnp.moveaxis(g.reshape(g.shape[0], N_CHIPS, -1), 1, 0)


def _stack_rows(s):
    return s.reshape(-1, s.shape[-1])


def _train_step(a):
    xs, mems, tgt = a["x"][0], a["mem"][0], a["loss_target"][0]
    core = lax.axis_index("c").astype(jnp.int32).reshape(1)
    chip = (2 * lax.axis_index("x") + lax.axis_index("y")).astype(jnp.int32).reshape(1)
    k_me = chip[0]
    shard = {n: a[n][0] for n in MATRICES}

    later = [n for n in MATRICES if n != "w_in"]
    taps = _exchange_small(_pack([a["ml_conv_w"][0], a["ffn_conv_w"][0]], F32), reduce=False)
    w = {"w_in": jnp.pad(_cat_cols(_gather_weights([shard["w_in"].astype(BF16)])[0]), ((0, 0), (0, D_IN_PAD - D_IN)))}
    gathering, token = _gather_start("gather_start", [shard[n].astype(BF16) for n in later])
    taps = taps.reshape((N_CHIPS, 2) + taps.shape[1:])[:, 0]
    ml_cw, ffn_cw = [_cat_cols(s) for s in _unpack(taps, [a["ml_conv_w"].shape[1:], a["ffn_conv_w"].shape[1:]])]
    b_in_p = jnp.pad(a["b_in"], ((0, 0), (0, D_IN_PAD - D_IN))) + token[0:1, 0:1]
    mixer_w = (a["hg_lb_logits"], a["hg_norm_w"], ml_cw, a["ml_conv_b"], a["ml_norm_w"])
    up_cols = a["ffn_w_up"].shape[-1]

    xb = xs.astype(BF16)
    proj = _mm("proj", "nn", xb, w["w_in"], bias=b_in_p, tm=256, tn=D_IN_PAD)
    y, hst, cst, nst, mst = _mixer_fwd(proj, *mixer_w)
    w.update(zip(later, _forward_halves("forward_halves", _gather_wait("gather_wait", gathering, y))))
    for n in ("w_out", "ca_wq", "ca_wo", "ffn_w_down"):
        w[n] = _stack_rows(w[n])
    z1, x1, x1b = _mm("mix_out", "nn", y, w["w_out"], res=xs, res_scale=ALPHA, ln=("fwd", a["ln1_g"], a["ln1_b"]),
                      copy_dtype=BF16)
    q = _mm("ca_q", "nn", x1b, w["ca_wq"], out_dtype=BF16, tn=D_MODEL)
    kv = _mm("ca_kv", "nn", mems, w["ca_wkv"])
    o = _attn_fwd(q, kv)
    z2, x2, x2b = _mm("ca_out", "nn", o, w["ca_wo"], res=x1, res_scale=ALPHA, ln=("fwd", a["ln2_g"], a["ln2_b"]),
                      copy_dtype=BF16)
    w_up = _cat_cols(w["ffn_w_up"])
    u, hmid, dz3, g_ln3g, g_ln3b, loss_part, dz3b = _ffn_fwd(
        x2b, x2, w_up, ffn_cw, a["ffn_conv_b"], w["ffn_w_down"], a["ln3_g"], a["ln3_b"], tgt)

    grads = {"ln3_g": g_ln3g, "ln3_b": g_ln3b}
    grads["ffn_w_down"] = _mm("g_w_down", "tn", hmid, dz3b, tm=D_FF // 2, tn=D_MODEL)
    du, g_cw, g_cb, dz2, grads["ln2_g"], grads["ln2_b"], dz2b = _ffn_bwd(
        u, ffn_cw, a["ffn_conv_b"], dz3b, dz3, w["ffn_w_down"], w_up, z2, a["ln2_g"], a["ln2_b"])
    grads["ffn_conv_w"] = jnp.transpose(g_cw, (2, 1, 0, 3)).reshape(FFN_CONV, 2 * D_FF)
    grads["ffn_conv_b"] = jnp.transpose(g_cb, (2, 1, 0, 3)).reshape(1, 2 * D_FF)
    grads["ffn_w_up"] = _mm("g_w_up", "tn", x2b, du, out_groups=N_CHIPS, tm=D_MODEL, tn=up_cols)
    grads["ffn_w_down"] = grads["ffn_w_down"].reshape((N_CHIPS,) + shard["ffn_w_down"].shape)
    pending = {}

    def reduce_start(tag, names):
        group = [grads[n] for n in names]
        sums = [_add_pair("add_pair_" + n, core, g, t)
                for n, g, t in zip(names, group, _swap_halves("swap_halves_" + tag, group))]
        handle, token = _scatter_start("scatter_start_" + tag, [s16 for _, s16 in sums])
        pending[tag] = (names, [s32 for s32, _ in sums], handle)
        return token[0:1, 0:1]

    zero = reduce_start("ffn", ("ffn_w_up", "ffn_w_down"))
    do = _mm("d_o", "nt", dz2b, w["ca_wo"], bias=jnp.zeros((1, D_MODEL), F32) + zero, out_dtype=BF16, tn=D_MODEL)
    grads["ca_wo"] = _mm("g_wo", "tn", o, dz2b, tm=D_MODEL, tn=D_MODEL)
    dq, dkv = _attn_bwd(q, kv, do)
    grads["ca_wq"] = _mm("g_wq", "tn", x1b, dq, tm=D_MODEL, tn=D_MODEL)
    grads["ca_wkv"] = _mm("g_wkv", "tn", mems, dkv, out_groups=N_CHIPS, tm=D_MODEL)
    dz1, grads["ln1_g"], grads["ln1_b"], dz1b = _mm("d_x1", "nt", dq, w["ca_wq"], res=dz2, res_scale=ALPHA,
                                                    ln=("bwd", z1, a["ln1_g"], a["ln1_b"]), copy_dtype=BF16)
    dy = _mm("d_y", "nt", dz1b, w["w_out"], tn=D_MODEL)
    grads["w_out"] = _mm("g_w_out", "tn", y, dz1b, tm=D_MODEL, tn=D_MODEL)
    for n in ("w_out", "ca_wq", "ca_wo"):
        grads[n] = grads[n].reshape((N_CHIPS,) + shard[n].shape)
    zero = reduce_start("attn", ("w_out", "ca_wq", "ca_wkv", "ca_wo"))
    (dproj, g_b_in, grads["hg_lb_logits"], grads["hg_norm_w"], grads["ml_conv_w"], grads["ml_conv_b"],
     grads["ml_norm_w"]) = _mixer_bwd(proj, dy, hst, cst, nst, mst, mixer_w[0], mixer_w[1] + zero, *mixer_w[2:])
    grads["w_in"] = _split_cols(_mm("g_w_in", "tn", xb, dproj, tm=D_MODEL, tn=up_cols)[:, :D_IN])
    grads["b_in"] = g_b_in[:, :D_IN]
    zero = reduce_start("in", ("w_in",))
    dx = _mm("d_x", "nt", dproj, w["w_in"], bias=jnp.zeros((1, D_MODEL), F32) + zero, res=dz1, res_scale=ALPHA,
             tm=256, tn=D_MODEL)

    halves = {}
    for tag, (names, sums32, handle) in pending.items():
        for n, s32, r in zip(names, sums32, _scatter_wait("scatter_wait_" + tag, handle, dx)):
            halves[n] = _add_chips("add_chips_" + n, chip, s32, r)
    halves = [halves[n] for n in MATRICES]
    other_halves = _share_halves(halves)

    small_shapes = [grads[n].shape for n in SMALL] + [loss_part.shape]
    summed = _unpack(_exchange_small(_pack([grads[n] for n in SMALL] + [loss_part], F32), reduce=True), small_shapes)
    loss = summed[-1][0, 0]
    for n, g in zip(SMALL, summed[:-1]):
        if n in COL_SHARDED:
            cols = a[n].shape[-1]
            g = lax.dynamic_slice_in_dim(g, k_me * cols, cols, axis=1)
        grads[n] = g

    delta, new_m, new_v = {}, {}, {}
    for n, mine, theirs in zip(MATRICES, halves, other_halves):
        grads[n], delta[n], new_m[n], new_v[n] = _adamw_halves(
            "adamw_" + n, core, shard[n], mine, theirs, a["m_" + n][0], a["v_" + n][0])
    small_w = [a[n][0] if a[n].ndim == 3 else a[n] for n in SMALL]
    small_m = [a["m_" + n][0] if a[n].ndim == 3 else a["m_" + n] for n in SMALL]
    small_v = [a["v_" + n][0] if a[n].ndim == 3 else a["v_" + n] for n in SMALL]
    shapes = [w.shape for w in small_w]
    packed = [_pack(l, F32) for l in (small_w, [grads[n] for n in SMALL], small_m, small_v)]
    for out, buf in zip((delta, new_m, new_v), _adamw("adamw_small", *packed)):
        for n, v in zip(SMALL, _unpack(buf, shapes)):
            out[n] = v

    def shaped(d):
        return [d[n].reshape(a[n].shape) for n in WEIGHTS]
    return (loss, dx[None], *shaped(grads), *shaped(delta), *shaped(new_m), *shaped(new_v))
```

```python
import functools

import jax
import jax.numpy as jnp
from jax import lax
from jax.experimental import pallas as pl
from jax.experimental.pallas import tpu as pltpu

F32 = jnp.float32
BF16 = jnp.bfloat16

D_MODEL = 1024
HEADS = 4
DK = 128
D_GRP = HEADS * DK
CHUNK = 64
ML_CONV = 4
FFN_CONV = 3
D_FF = 2816
CA_DH = D_MODEL // HEADS
DEPTH = 1
ALPHA = (2.0 * DEPTH) ** 0.25
LN_EPS = 1e-5
NEG_BIG = -1e30
D_IN = 8 * D_GRP + 2 * HEADS
D_IN_PAD = 8 * D_GRP + 128
ADAM_LR, ADAM_B1, ADAM_B2, ADAM_EPS, ADAM_WD, ADAM_STEP = 0.001, 0.9, 0.999, 1e-08, 0.01, 10

SUBLANES = 8
LANES = 128
VMEM_BYTES = 64 * 1024 * 1024


def _pcall(body, pin=True, **kw):
    if not pin:
        return _call(body, **kw)
    kw["out_shape"] = jax.tree.map(lambda s: pltpu.HBM(s.shape, s.dtype), kw["out_shape"])
    call = _call(body, **kw)

    def pinned(*args):
        return call(*[pltpu.with_memory_space_constraint(x, pltpu.HBM) if jnp.issubdtype(x.dtype, jnp.floating) else x
                      for x in args])
    return pinned


def _call(body, **kw):
    return pl.pallas_call(body, **kw)


def _params(semantics, vmem_bytes):
    limit = int(min(max(2 * vmem_bytes, 16 * 1024 * 1024), VMEM_BYTES - 8 * 1024 * 1024))
    return pltpu.CompilerParams(dimension_semantics=semantics, vmem_limit_bytes=limit)


def _nbytes(shape, dtype):
    n = 1
    for s in shape:
        n *= s
    return n * jnp.dtype(dtype).itemsize


def _dg(a, b, ca, cb):
    return lax.dot_general(a.astype(BF16), b.astype(BF16), (((ca,), (cb,)), ((), ())),
                           preferred_element_type=F32)


@jax.custom_vjp
def mm_nn(a, b):
    return _dg(a, b, 1, 0)


mm_nn.defvjp(lambda a, b: (_dg(a, b, 1, 0), (a, b)),
             lambda r, g: (_dg(g, r[1], 1, 1).astype(r[0].dtype), _dg(r[0], g, 0, 0).astype(r[1].dtype)))


@jax.custom_vjp
def mm_nt(a, b):
    return _dg(a, b, 1, 1)


mm_nt.defvjp(lambda a, b: (_dg(a, b, 1, 1), (a, b)),
             lambda r, g: (_dg(g, r[1], 1, 0).astype(r[0].dtype), _dg(g, r[0], 0, 0).astype(r[1].dtype)))


@jax.custom_vjp
def mm_tn(a, b):
    return _dg(a, b, 0, 0)


mm_tn.defvjp(lambda a, b: (_dg(a, b, 0, 0), (a, b)),
             lambda r, g: (_dg(r[1], g, 1, 1).astype(r[0].dtype), _dg(r[0], g, 1, 0).astype(r[1].dtype)))


def _tri(n, lower):
    r = lax.broadcasted_iota(jnp.int32, (n, n), 0)
    c = lax.broadcasted_iota(jnp.int32, (n, n), 1)
    return ((r >= c) if lower else (r <= c)).astype(F32)


def _tri_dot(lower, x):
    t = _tri(x.shape[0], lower).astype(BF16)
    hi = x.astype(BF16)
    rest = x - hi.astype(F32)
    mid = rest.astype(BF16)
    lo = (rest - mid.astype(F32)).astype(BF16)
    return sum(lax.dot_general(t, p, (((1,), (0,)), ((), ())), preferred_element_type=F32) for p in (hi, mid, lo))


@jax.custom_vjp
def cumsum_rows(x):
    return _tri_dot(True, x)


cumsum_rows.defvjp(lambda x: (_tri_dot(True, x), None), lambda _, g: (_tri_dot(False, g),))


def _shift_impl(halo, x, d):
    xx = jnp.concatenate([halo, x], axis=0)
    return pltpu.roll(xx, d, 0)[SUBLANES:]


@functools.partial(jax.custom_vjp, nondiff_argnums=(2,))
def shift_rows(halo, x, d):
    return _shift_impl(halo, x, d)


def _shift_bwd(d, _, g):
    n = g.shape[0] + SUBLANES
    gg = jnp.concatenate([jnp.zeros((SUBLANES, g.shape[1]), g.dtype), g], axis=0)
    r = pltpu.roll(gg, n - d, 0)
    return r[:SUBLANES], r[SUBLANES:]


shift_rows.defvjp(lambda halo, x, d: (_shift_impl(halo, x, d), None), _shift_bwd)


def causal_conv(halo, x, w_rows, b):
    k = len(w_rows)
    y = b + w_rows[k - 1] * x
    for d in range(1, k):
        y = y + w_rows[k - 1 - d] * shift_rows(halo, x, d)
    return y


def _sigmoid(x):
    return 0.5 * jnp.tanh(0.5 * x) + 0.5


def _sigmoid_tail(x):
    return 1.0 / (1.0 + jnp.exp(-x))


def _silu(x):
    return x * _sigmoid(x)


def _log_sigmoid(x):
    return jnp.minimum(x, 0.0) - jnp.log(1.0 + jnp.exp(-jnp.abs(x)))


def _pick_row(x, i):
    row = lax.broadcasted_iota(jnp.int32, (x.shape[0], 1), 0)
    return jnp.sum(jnp.where(row == i, x, 0.0), axis=0, keepdims=True)


def _layer_norm(z, g, b):
    mu = jnp.mean(z, axis=-1, keepdims=True)
    zc = z - mu
    var = jnp.mean(zc * zc, axis=-1, keepdims=True)
    return zc * lax.rsqrt(var + LN_EPS) * g + b


def _qk_conv(halo, x, w0, w1, w2, w3, b):
    return _silu(causal_conv(halo, x, (w0, w1, w2, w3), b))


def _grp(i, h=None):
    if h is None:
        return pl.ds(i * D_GRP, D_GRP)
    return pl.ds(i * D_GRP + h * DK, DK)


def _mixer_specs(n_chunks, reverse):
    def chunk(c):
        return n_chunks - 1 - c if reverse else c
    row8 = CHUNK // SUBLANES
    proj_spec = pl.BlockSpec((CHUNK, D_IN_PAD), lambda c: (chunk(c), 0))
    halo_spec = pl.BlockSpec((SUBLANES, 2 * D_GRP), lambda c: (jnp.maximum(chunk(c) * row8 - 1, 0), 2))
    small = [pl.BlockSpec((2, D_GRP), lambda c: (0, 0)), pl.BlockSpec((1, D_GRP), lambda c: (0, 0)),
             pl.BlockSpec((ML_CONV, 2 * D_GRP), lambda c: (0, 0)), pl.BlockSpec((1, 2 * D_GRP), lambda c: (0, 0)),
             pl.BlockSpec((1, D_GRP), lambda c: (0, 0))]
    state_specs = [pl.BlockSpec((1, HEADS, DK, DK), lambda c: (chunk(c), 0, 0, 0)),
                   pl.BlockSpec((1, HEADS, DK, DK), lambda c: (chunk(c), 0, 0, 0)),
                   pl.BlockSpec((1, HEADS, 1, DK), lambda c: (chunk(c), 0, 0, 0)),
                   pl.BlockSpec((1, HEADS, 1, DK), lambda c: (chunk(c), 0, 0, 0))]
    y_spec = pl.BlockSpec((CHUNK, 2 * D_GRP), lambda c: (chunk(c), 0))
    return proj_spec, halo_spec, small, state_specs, y_spec, chunk


def _heads(x):
    return [x[:, h * DK:(h + 1) * DK] for h in range(HEADS)]


def _last(x, j):
    lane = lax.broadcasted_iota(jnp.int32, (1, x.shape[-1]), 1)
    return jnp.sum(jnp.where(lane == j, x, 0.0), axis=-1, keepdims=True)


def _hg_chunk(st_t, hq, hf, hi, hgate, l0, l1, nw):
    n = hq.shape[0]
    lb = _sigmoid(l0 - l1)
    q = _silu(hq)
    lf = jnp.log(lb + (1.0 - lb) * _sigmoid(hf))
    k = (1.0 - lb) * _sigmoid_tail(-hf)
    b = cumsum_rows(lf)
    b_ref = _pick_row(b, n // 2 - 1)
    b_last = _pick_row(b, n - 1)
    qa, ka =_heads(q * jnp.exp(b - b_ref)), _heads(k * jnp.exp(b_ref - b))
    qe, kd, eb, v = _heads(q * jnp.exp(b)), _heads(k * jnp.exp(b_last - b)), _heads(jnp.exp(b_last)), _heads(hi)
    tri = _tri(n, True) > 0
    attn = [jnp.where(tri, mm_nt(qa[h], ka[h]), 0.0) for h in range(HEADS)]
    o = [mm_nn(attn[h], v[h]) + mm_nt(qe[h], st_t[h]) for h in range(HEADS)]
    st_new = jnp.stack([eb[h] * st_t[h] + mm_tn(v[h], kd[h]) for h in range(HEADS)])
    yn = [o[h] * lax.rsqrt(jnp.mean(o[h] * o[h], axis=-1, keepdims=True) + LN_EPS) for h in range(HEADS)]
    return st_new, jnp.concatenate(yn, axis=1) * nw * _silu(hgate)


def _ml_chunk(c_st, n_st, m_st, q, k, v, gates, og, nw):
    n = q.shape[0]
    ig = jnp.stack([_last(gates, h) for h in range(HEADS)])
    log_f = _log_sigmoid(gates)
    fl = jnp.stack([_last(log_f, HEADS + h) for h in range(HEADS)])
    bw = cumsum_rows(jnp.concatenate([jnp.broadcast_to(fl[h], (n, DK)) for h in range(HEADS)], axis=1))
    b = jnp.stack([_last(x, 0) for x in _heads(bw)])
    g = jnp.sum(fl, axis=1, keepdims=True)
    eye = lax.broadcasted_iota(jnp.int32, (n, n), 0) == lax.broadcasted_iota(jnp.int32, (n, n), 1)
    e_row = jnp.sum(jnp.where(eye, ig - b, 0.0), axis=1, keepdims=True)
    d = jnp.where(_tri(n, True) > 0, b + e_row, -jnp.inf)
    inter = b + m_st
    m_t = jnp.maximum(inter, jnp.max(d, axis=2, keepdims=True))
    qs, kh, vh = _heads(q * (DK ** -0.5)), _heads(k), _heads(v)
    s = jnp.stack([mm_nt(qs[h], kh[h]) for h in range(HEADS)]) * jnp.exp(d - m_t)
    w_inter = jnp.exp(inter - m_t)
    num = (jnp.stack([mm_nn(s[h], vh[h]) for h in range(HEADS)])
           + w_inter * jnp.stack([mm_nn(qs[h], c_st[h]) for h in range(HEADS)]))
    den = jnp.sum(s, axis=2, keepdims=True) + w_inter * jnp.sum(jnp.stack(qs) * n_st, axis=2, keepdims=True)
    h_out = num / jnp.maximum(jnp.abs(den), jnp.exp(-m_t))
    a = g - b + ig
    m_new = jnp.maximum(g + m_st, jnp.max(a, axis=1, keepdims=True))
    decay = jnp.exp(g + m_st - m_new)
    wk = jnp.stack(kh) * jnp.exp(a - m_new)
    c_new = decay * c_st + jnp.stack([mm_tn(wk[h], vh[h]) for h in range(HEADS)])
    n_new = decay * n_st + jnp.sum(wk, axis=1, keepdims=True)
    hc = h_out - jnp.mean(h_out, axis=-1, keepdims=True)
    yn = hc * lax.rsqrt(jnp.mean(hc * hc, axis=-1, keepdims=True) + LN_EPS)
    y = _sigmoid(og) * (jnp.concatenate([yn[h] for h in range(HEADS)], axis=1) * nw)
    return c_new, n_new, m_new, y


def _mixer_inputs(proj_ref, lg_ref, hnw_ref, mnw_ref, qk):
    hg_in = (proj_ref[:, _grp(0)], proj_ref[:, _grp(1)], proj_ref[:, _grp(2)], proj_ref[:, _grp(3)],
             lg_ref[0:1, :], lg_ref[1:2, :], hnw_ref[...])
    ml_in = (qk[:, :D_GRP], qk[:, D_GRP:], proj_ref[:, _grp(6)], proj_ref[:, pl.ds(8 * D_GRP, LANES)],
             proj_ref[:, _grp(7)], mnw_ref[...])
    return hg_in, ml_in


def _mixer_fwd(proj, lb_logits, hg_nw, conv_w, conv_b, ml_nw):
    seq = proj.shape[0]
    n_chunks = seq // CHUNK
    proj_spec, halo_spec, small, state_specs, y_spec, _ = _mixer_specs(n_chunks, False)

    def body(proj_ref, halo_ref, lg_ref, hnw_ref, cw_ref, cb_ref, mnw_ref,
             y_ref, hst_ref, cst_ref, nst_ref, mst_ref, hs, cs, ns, ms):
        c = pl.program_id(0)

        @pl.when(c == 0)
        def _():
            hs[...] = jnp.zeros_like(hs)
            cs[...] = jnp.zeros_like(cs)
            ns[...] = jnp.zeros_like(ns)
            ms[...] = jnp.full(ms.shape, NEG_BIG, F32)

        hst_ref[0] = hs[...]
        cst_ref[0] = cs[...]
        nst_ref[0] = ns[...]
        mst_ref[0] = ms[...]
        halo = jnp.where(c > 0, halo_ref[...], 0.0)
        qk = _qk_conv(halo, proj_ref[:, pl.ds(4 * D_GRP, 2 * D_GRP)],
                      cw_ref[0:1, :], cw_ref[1:2, :], cw_ref[2:3, :], cw_ref[3:4, :], cb_ref[...])
        hg_in, ml_in = _mixer_inputs(proj_ref, lg_ref, hnw_ref, mnw_ref, qk)
        hs[...], y_hg = _hg_chunk(hs[...], *hg_in)
        cs[...], ns[...], m_new, y_ml = _ml_chunk(cs[...], ns[...], _last(ms[...], 0), *ml_in)
        ms[...] = jnp.broadcast_to(m_new, ms.shape)
        y_ref[:, pl.ds(0, D_GRP)] = y_hg.astype(BF16)
        y_ref[:, pl.ds(D_GRP, D_GRP)] = y_ml.astype(BF16)

    st = jax.ShapeDtypeStruct((n_chunks, HEADS, DK, DK), F32)
    vec = jax.ShapeDtypeStruct((n_chunks, HEADS, 1, DK), F32)
    vmem = 2 * (_nbytes((CHUNK, D_IN_PAD), F32) + _nbytes((CHUNK, 2 * D_GRP), F32) + 2 * _nbytes((HEADS, DK, DK), F32)) \
        + 2 * _nbytes((HEADS, DK, DK), F32)
    return _pcall(
        body, name="mixer_fwd", grid=(n_chunks,),
        in_specs=[proj_spec, halo_spec] + small,
        out_specs=[y_spec] + state_specs,
        out_shape=[jax.ShapeDtypeStruct((seq, 2 * D_GRP), BF16), st, st, vec, vec],
        scratch_shapes=[pltpu.VMEM((HEADS, DK, DK), F32), pltpu.VMEM((HEADS, DK, DK), F32),
                        pltpu.VMEM((HEADS, 1, DK), F32), pltpu.VMEM((HEADS, 1, DK), F32)],
        compiler_params=_params(("arbitrary",), vmem),
    )(proj, proj, lb_logits, hg_nw, conv_w, conv_b, ml_nw)


def _mixer_bwd(proj, dy, hst, cst, nst, mst, lb_logits, hg_nw, conv_w, conv_b, ml_nw):
    seq = proj.shape[0]
    n_chunks = seq // CHUNK
    proj_spec, halo_spec, small, state_specs, y_spec, _ = _mixer_specs(n_chunks, True)

    def body(proj_ref, halo_ref, dy_ref, hst_ref, cst_ref, nst_ref, mst_ref,
             lg_ref, hnw_ref, cw_ref, cb_ref, mnw_ref,
             dproj_ref, dbin_ref, dlg_ref, dhnw_ref, dcw_ref, dcb_ref, dmnw_ref,
             dhs, dcs, dns, dms, dhalo):
        c = pl.program_id(0)

        @pl.when(c == 0)
        def _():
            for r in (dhs, dcs, dns, dms, dhalo, dbin_ref, dlg_ref, dhnw_ref, dcw_ref, dcb_ref, dmnw_ref):
                r[...] = jnp.zeros_like(r)

        def put(cols, val):
            dproj_ref[:, cols] = val.astype(BF16)
            dbin_ref[:, cols] += jnp.sum(val, axis=0, keepdims=True)

        first = c == n_chunks - 1
        halo = jnp.where(first, 0.0, halo_ref[...])
        x_qk = proj_ref[:, pl.ds(4 * D_GRP, 2 * D_GRP)]
        conv_args = (halo, x_qk, cw_ref[0:1, :], cw_ref[1:2, :], cw_ref[2:3, :], cw_ref[3:4, :], cb_ref[...])
        qk, conv_vjp = jax.vjp(_qk_conv, *conv_args)
        hg_in, ml_in = _mixer_inputs(proj_ref, lg_ref, hnw_ref, mnw_ref, qk)
        _, hg_vjp = jax.vjp(_hg_chunk, hst_ref[0], *hg_in)
        _, ml_vjp = jax.vjp(_ml_chunk, cst_ref[0], nst_ref[0], _last(mst_ref[0], 0), *ml_in)
        dst, dhq, dhf, dhi, dhg, dl0, dl1, dnw = hg_vjp((dhs[...], dy_ref[:, pl.ds(0, D_GRP)]))
        dc, dn, dm, dq, dk, dv, dgates, dog, dmn = ml_vjp(
            (dcs[...], dns[...], _last(dms[...], 0), dy_ref[:, pl.ds(D_GRP, D_GRP)]))
        dhs[...] = dst
        dcs[...] = dc
        dns[...] = dn
        dms[...] = jnp.broadcast_to(dm, dms.shape)
        for i, val in ((0, dhq), (1, dhf), (2, dhi), (3, dhg), (6, dv), (7, dog)):
            put(_grp(i), val)
        put(pl.ds(8 * D_GRP, LANES), dgates)
        dlg_ref[0:1, :] += dl0
        dlg_ref[1:2, :] += dl1
        dhnw_ref[...] += dnw
        dmnw_ref[...] += dmn
        dh, dx, dw0, dw1, dw2, dw3, db = conv_vjp(jnp.concatenate([dq, dk], axis=1))
        tail = jnp.concatenate([jnp.zeros((CHUNK - SUBLANES, 2 * D_GRP), F32), dhalo[...]], axis=0)
        put(pl.ds(4 * D_GRP, 2 * D_GRP), dx + tail)
        dhalo[...] = dh
        for d, dw in enumerate((dw0, dw1, dw2, dw3)):
            dcw_ref[d:d + 1, :] += dw
        dcb_ref[...] += db

    row = pl.BlockSpec((1, D_GRP), lambda c: (0, 0))
    small_out = [pl.BlockSpec((1, D_IN_PAD), lambda c: (0, 0)), pl.BlockSpec((2, D_GRP), lambda c: (0, 0)), row,
                 pl.BlockSpec((ML_CONV, 2 * D_GRP), lambda c: (0, 0)), pl.BlockSpec((1, 2 * D_GRP), lambda c: (0, 0)), row]
    dy_spec = pl.BlockSpec((CHUNK, 2 * D_GRP), y_spec.index_map)
    vmem = 2 * (2 * _nbytes((CHUNK, D_IN_PAD), F32) + _nbytes((CHUNK, 2 * D_GRP), F32)
                + 2 * _nbytes((HEADS, DK, DK), F32)) + 2 * _nbytes((HEADS, DK, DK), F32) + 4 * 1024 * 1024
    return _pcall(
        body, name="mixer_bwd", grid=(n_chunks,),
        in_specs=[proj_spec, halo_spec, dy_spec] + state_specs + small,
        out_specs=[proj_spec] + small_out,
        out_shape=[jax.ShapeDtypeStruct((seq, D_IN_PAD), BF16), jax.ShapeDtypeStruct((1, D_IN_PAD), F32),
                   jax.ShapeDtypeStruct((2, D_GRP), F32), jax.ShapeDtypeStruct((1, D_GRP), F32),
                   jax.ShapeDtypeStruct((ML_CONV, 2 * D_GRP), F32), jax.ShapeDtypeStruct((1, 2 * D_GRP), F32),
                   jax.ShapeDtypeStruct((1, D_GRP), F32)],
        scratch_shapes=[pltpu.VMEM((HEADS, DK, DK), F32), pltpu.VMEM((HEADS, DK, DK), F32),
                        pltpu.VMEM((HEADS, 1, DK), F32), pltpu.VMEM((HEADS, 1, DK), F32),
                        pltpu.VMEM((SUBLANES, 2 * D_GRP), F32)],
        compiler_params=_params(("arbitrary",), vmem),
    )(proj, proj, dy, hst, cst, nst, mst, lb_logits, hg_nw, conv_w, conv_b, ml_nw)


def _tile(n, prefs, unit=None):
    unit = unit or n
    for p in prefs:
        if unit % p == 0 and n % p == 0:
            return p
    return unit


def _logical(arr):
    return arr.shape if arr.ndim == 2 else (arr.shape[1], arr.shape[0] * arr.shape[2])


def _group(arr):
    return arr.shape[-1]


def _split_spec(ndim, group, tr, tc, where):
    if ndim == 2:
        return pl.BlockSpec((tr, tc), where)
    per = group // tc
    assert per * tc == group, (group, tc)

    def index(*ids):
        bi, bj = where(*ids)
        return (bj // per, bi, bj % per)
    return pl.BlockSpec((None, tr, tc), index)


def _mm(name, mode, a, b, *, bias=None, res=None, res_scale=1.0, ln=None, out_dtype=F32, out_groups=None,
        copy_dtype=None, tm=None, tn=None, tk=None):
    la, lb = _logical(a), _logical(b)
    if mode == "nn":
        (m, k), n = la, lb[1]
        n_unit = _group(b) if b.ndim == 3 else n
        kc = _group(a) if a.ndim == 3 else k
    elif mode == "nt":
        (m, k), n = la, lb[0]
        n_unit = n
        kc = min(_group(a) if a.ndim == 3 else k, _group(b) if b.ndim == 3 else k)
    else:
        (k, m), n = la, lb[1]
        n_unit, kc = (_group(b) if b.ndim == 3 else n), k
        assert a.ndim == 2
    if out_groups:
        n_unit = min(n_unit, n // out_groups)
    kind = ln[0] if ln else None
    tm = tm or (256 if ln else _tile(m, (512, 256, 128)))
    tn = n if ln else (tn or _tile(n, (512, 384, 256, 128), n_unit))
    tk = (tk or _tile(k, (2048, 512, 256, 128))) if mode == "tn" else k
    gi, gj, gk = m // tm, n // tn, k // tk
    assert gi * tm == m and gj * tn == n and gk * tk == k and n_unit % tn == 0, (name, m, n, k, tm, tn, tk)
    ca, cb = {"nn": (1, 0), "nt": (1, 1), "tn": (0, 0)}[mode]
    i_outer = gk > 1 or (gi - 1) * _nbytes(b.shape, b.dtype) <= (gj - 1) * _nbytes(a.shape, a.dtype)

    def ij(where):
        return (lambda p, q, kk: where(p, q, kk)) if i_outer else (lambda p, q, kk: where(q, p, kk))
    if mode == "tn":
        a_spec = pl.BlockSpec((tk, tm), ij(lambda i, j, kk: (kk, i)))
    elif a.ndim == 3:
        a_spec = pl.BlockSpec((a.shape[0], tm, _group(a)), ij(lambda i, j, kk: (0, i, 0)))
    else:
        a_spec = pl.BlockSpec((tm, k), ij(lambda i, j, kk: (i, 0)))
    if mode != "nt":
        b_spec = _split_spec(b.ndim, _group(b), tk, tn, ij(lambda i, j, kk: (kk, j)))
    elif b.ndim == 3:
        b_spec = pl.BlockSpec((b.shape[0], tn, _group(b)), ij(lambda i, j, kk: (0, j, 0)))
    else:
        b_spec = pl.BlockSpec((tn, k), ij(lambda i, j, kk: (j, 0)))
    row_spec = pl.BlockSpec((1, tn), ij(lambda i, j, kk: (0, j)))
    blk_spec = pl.BlockSpec((tm, tn), ij(lambda i, j, kk: (i, j)))
    ins, in_specs = [a, b], [a_spec, b_spec]
    if bias is not None:
        ins.append(bias), in_specs.append(row_spec)
    if res is not None:
        ins.append(res), in_specs.append(blk_spec)
    if kind == "fwd":
        ins += [ln[1], ln[2]]
        in_specs += [row_spec, row_spec]
    elif kind == "loss":
        ins += [ln[1], ln[2], ln[3]]
        in_specs += [row_spec, row_spec, blk_spec]
    elif kind == "bwd":
        ins += [ln[1], ln[2], ln[3]]
        in_specs += [blk_spec, row_spec, row_spec]
    if out_groups:
        blk_out = jax.ShapeDtypeStruct((out_groups, m, n // out_groups), out_dtype)
        out_spec = _split_spec(3, n // out_groups, tm, tn, ij(lambda i, j, kk: (i, j)))
    else:
        blk_out, out_spec = jax.ShapeDtypeStruct((m, n), out_dtype), blk_spec
    row_out = jax.ShapeDtypeStruct((1, n), F32)
    if kind is None:
        out_shape, out_specs = [blk_out], [out_spec]
    elif kind == "fwd":
        out_shape, out_specs = [blk_out, blk_out], [blk_spec, blk_spec]
    else:
        out_shape, out_specs = [blk_out, row_out, row_out], [blk_spec, row_spec, row_spec]
        if kind == "loss":
            out_shape.append(jax.ShapeDtypeStruct((1, LANES), F32))
            out_specs.append(pl.BlockSpec((1, LANES), lambda p, q, kk: (0, 0)))
    if copy_dtype is not None:
        out_shape.append(jax.ShapeDtypeStruct((m, n), copy_dtype))
        out_specs.append(blk_spec)
    n_in = len(ins)

    def body(*refs):
        in_refs, out_refs, acc_ref = refs[:n_in], refs[n_in:n_in + len(out_shape)], refs[-1]
        i, kk = pl.program_id(0 if i_outer else 1), pl.program_id(2)
        a_ref, b_ref = in_refs[:2]
        extra = list(in_refs[2:])

        def epilogue(acc, rows=slice(None)):
            rest = list(extra)
            if bias is not None:
                acc = acc + rest.pop(0)[...]
            if res is not None:
                acc = acc + res_scale * rest.pop(0)[rows, :]
            if kind is None:
                out_refs[0][...] = acc.astype(out_dtype)
                return
            if kind == "fwd":
                out_refs[0][rows, :] = acc
                y = _layer_norm(acc, rest[0][...], rest[1][...])
                out_refs[1][rows, :] = y
                if copy_dtype is not None:
                    out_refs[-1][rows, :] = y.astype(copy_dtype)
                return
            if kind == "loss":
                y, vjp = jax.vjp(_layer_norm, acc, rest[0][...], rest[1][...])
                err = y - rest[2][rows, :]
                part = 0.5 * jnp.sum(jnp.sum(err * err, axis=1, keepdims=True), axis=0, keepdims=True) / n
                dz, dg, db = vjp(err / n)
            else:
                _, vjp = jax.vjp(_layer_norm, rest[0][rows, :], rest[1][...], rest[2][...])
                dz, dg, db = vjp(acc)
            out_refs[0][rows, :] = dz
            out_refs[1][...] += dg
            out_refs[2][...] += db
            if kind == "loss":
                out_refs[3][...] += jnp.broadcast_to(part, (1, LANES))
            if copy_dtype is not None:
                out_refs[-1][rows, :] = dz.astype(copy_dtype)

        if kind in ("loss", "bwd"):
            @pl.when((i == 0) & (kk == 0))
            def _():
                for r in out_refs[1:3 + (kind == "loss")]:
                    r[...] = jnp.zeros_like(r)

        if kind is not None and a.ndim == 2 and kc == k and tm % (2 * BF16_ROWS) == 0:
            halves = [pl.ds(p * (tm // 2), tm // 2) for p in range(2)]
            prods = [_dg(a_ref[rows, :], b_ref[...], ca, cb) for rows in halves]
            for rows, prod in zip(halves, prods):
                epilogue(prod, rows)
            return

        def chunk(ref, c0, last):
            if ref.ndim == 3:
                g = ref.shape[2]
                return ref[c0 // g, :, pl.ds(c0 % g, kc)]
            return ref[:, pl.ds(c0, kc)] if last else ref[pl.ds(c0, kc), :]

        if mode == "tn" or kc == k:
            prod = _dg(a_ref[...], b_ref[...], ca, cb)
        else:
            prod = None
            for c0 in range(0, k, kc):
                part = _dg(chunk(a_ref, c0, True), chunk(b_ref, c0, mode == "nt"), ca, cb)
                prod = part if prod is None else prod + part
        if gk == 1:
            epilogue(prod)
            return

        @pl.when(kk == 0)
        def _():
            acc_ref[...] = prod

        @pl.when(kk > 0)
        def _():
            acc_ref[...] += prod

        @pl.when(kk == gk - 1)
        def _():
            epilogue(acc_ref[...])

    vmem = (2 * (_nbytes((tm, tk), a.dtype) + _nbytes((tk, tn), b.dtype))
            + (2 * len(ins) + 2 * len(out_shape) + 1) * _nbytes((tm, tn), F32))
    outs = _pcall(
        body, name=name, grid=(gi, gj, gk) if i_outer else (gj, gi, gk), in_specs=in_specs, out_specs=out_specs,
        out_shape=out_shape, scratch_shapes=[pltpu.VMEM((tm, tn) if gk > 1 else (SUBLANES, LANES), F32)],
        compiler_params=_params(("arbitrary", "arbitrary", "arbitrary"), vmem),
    )(*ins)
    return outs[0] if (kind is None and copy_dtype is None) else outs


def _attn_head(q, k, v):
    sc = mm_nt(q, k) * (CA_DH ** -0.5)
    e = jnp.exp(sc - jnp.max(sc, axis=-1, keepdims=True))
    return mm_nn(e / jnp.sum(e, axis=-1, keepdims=True), v)


def _attn_fwd(q, kv):
    seq, n_mem = q.shape[0], kv.shape[0]
    tq = _tile(seq, (512, 256, 128))

    def body(q_ref, kv_ref, o_ref):
        for h in range(HEADS):
            hd = pl.ds(h * CA_DH, CA_DH)
            o = _attn_head(q_ref[:, hd], kv_ref[:, hd], kv_ref[:, pl.ds(D_MODEL + h * CA_DH, CA_DH)])
            o_ref[:, hd] = o.astype(BF16)

    return _pcall(
        body, name="attn_fwd", grid=(seq // tq,),
        in_specs=[pl.BlockSpec((tq, D_MODEL), lambda i: (i, 0)), pl.BlockSpec((n_mem, 2 * D_MODEL), lambda i: (0, 0))],
        out_specs=pl.BlockSpec((tq, D_MODEL), lambda i: (i, 0)), out_shape=jax.ShapeDtypeStruct((seq, D_MODEL), BF16),
        compiler_params=_params(("arbitrary",), 4 * _nbytes((tq, D_MODEL), F32) + 2 * _nbytes((n_mem, 2 * D_MODEL), F32)),
    )(q, kv)


def _attn_bwd(q, kv, do):
    seq, n_mem = q.shape[0], kv.shape[0]
    tq = _tile(seq, (512, 256, 128))

    def body(q_ref, kv_ref, do_ref, dq_ref, dkv_ref):
        @pl.when(pl.program_id(0) == 0)
        def _():
            dkv_ref[...] = jnp.zeros_like(dkv_ref)

        for h in range(HEADS):
            hd = pl.ds(h * CA_DH, CA_DH)
            vd = pl.ds(D_MODEL + h * CA_DH, CA_DH)
            _, vjp = jax.vjp(_attn_head, q_ref[:, hd], kv_ref[:, hd], kv_ref[:, vd])
            dq, dk, dv = vjp(do_ref[:, hd].astype(F32))
            dq_ref[:, hd] = dq.astype(BF16)
            dkv_ref[:, hd] += dk
            dkv_ref[:, vd] += dv

    return _pcall(
        body, name="attn_bwd", grid=(seq // tq,),
        in_specs=[pl.BlockSpec((tq, D_MODEL), lambda i: (i, 0)), pl.BlockSpec((n_mem, 2 * D_MODEL), lambda i: (0, 0)),
                  pl.BlockSpec((tq, D_MODEL), lambda i: (i, 0))],
        out_specs=[pl.BlockSpec((tq, D_MODEL), lambda i: (i, 0)), pl.BlockSpec((n_mem, 2 * D_MODEL), lambda i: (0, 0))],
        out_shape=[jax.ShapeDtypeStruct((seq, D_MODEL), BF16), jax.ShapeDtypeStruct((n_mem, 2 * D_MODEL), F32)],
        compiler_params=_params(("arbitrary",), 6 * _nbytes((tq, D_MODEL), F32) + 4 * _nbytes((n_mem, 2 * D_MODEL), F32)),
    )(q, kv, do)


def _ffn_mid(hg, xg, hv, xv, wg0, wg1, wg2, bg, wv0, wv1, wv2, bv):
    return jax.nn.gelu(causal_conv(hg, xg, (wg0, wg1, wg2), bg)) * causal_conv(hv, xv, (wv0, wv1, wv2), bv)


FFN_TB = 256
FFN_W = D_FF // 2
FFN_J = D_FF // FFN_W
MXU_COLS = 256
FFN_PIECES = tuple((off, min(MXU_COLS, FFN_W - off)) for off in range(0, FFN_W, MXU_COLS))


def _ffn_common_specs(seq, row):
    tb = min(FFN_TB, seq)
    full = pl.BlockSpec((tb, D_MODEL), lambda t, j: (row(t), 0))
    vec = pl.BlockSpec((1, D_MODEL), lambda t, j: (0, 0))
    halves = []
    for off in (0, FFN_J):
        halves.append(dict(
            w_up=pl.BlockSpec((D_MODEL, FFN_W), lambda t, j, off=off: (0, j + off)),
            taps=pl.BlockSpec((FFN_CONV, FFN_W), lambda t, j, off=off: (0, j + off)),
            bias=pl.BlockSpec((1, FFN_W), lambda t, j, off=off: (0, j + off))))
    w_down = pl.BlockSpec((FFN_W, D_MODEL), lambda t, j: (j, 0))
    u_blk = pl.BlockSpec((2, tb, FFN_W), lambda t, j: (0, row(t), j))
    return tb, full, vec, halves, w_down, u_blk


def _ffn_vmem(tb):
    return (_nbytes((2, tb, FFN_W), F32) + _nbytes((2, tb, FFN_W), BF16) + 3 * _nbytes((D_MODEL, FFN_W), BF16)
            + 10 * _nbytes((tb, D_MODEL), F32))


def _conv_params(taps_ref, bias_ref, cols):
    return taps_ref[0:1, cols], taps_ref[1:2, cols], taps_ref[2:3, cols], bias_ref[:, cols]


def _ffn_fwd(x2b, x2, w_up, conv_w, conv_b, w_down, ln_g, ln_b, target):
    seq = x2.shape[0]
    tb, full, vec, halves, wd_spec, u_blk = _ffn_common_specs(seq, lambda t: t)
    nt = seq // tb

    def body(xb_ref, wg_ref, wv_ref, tg_ref, tv_ref, bg_ref, bv_ref, wd_ref, x_ref, g_ref, b_ref, tgt_ref,
             u_ref, h_ref, dz_ref, dg_ref, db_ref, loss_ref, dzb_ref, acc, carry):
        t, j = pl.program_id(0), pl.program_id(1)
        xb = xb_ref[...]
        pieces = [pl.ds(off, width) for off, width in FFN_PIECES]
        ug = [_dg(xb, wg_ref[:, cols], 1, 0) for cols in pieces]
        uv = [_dg(xb, wv_ref[:, cols], 1, 0) for cols in pieces]
        hs = []
        for cols, g, v in zip(pieces, ug, uv):
            u_ref[0, :, cols] = g
            u_ref[1, :, cols] = v
            halo_g = jnp.where(t == 0, 0.0, carry[j, 0, :, cols])
            halo_v = jnp.where(t == 0, 0.0, carry[j, 1, :, cols])
            h = _ffn_mid(halo_g, g, halo_v, v, *_conv_params(tg_ref, bg_ref, cols),
                         *_conv_params(tv_ref, bv_ref, cols)).astype(BF16)
            carry[j, 0, :, cols] = g[tb - SUBLANES:, :]
            carry[j, 1, :, cols] = v[tb - SUBLANES:, :]
            h_ref[:, cols] = h
            hs.append(h)
        part = None
        for cols, h in zip(pieces, hs):
            p = _dg(h, wd_ref[cols, :], 1, 0)
            part = p if part is None else part + p

        @pl.when(j == 0)
        def _():
            acc[...] = part

        @pl.when(j > 0)
        def _():
            acc[...] += part

        @pl.when(j == FFN_J - 1)
        def _():
            y, vjp = jax.vjp(_layer_norm, acc[...] + ALPHA * x_ref[...], g_ref[...], b_ref[...])
            err = y - tgt_ref[...]
            part_loss = 0.5 * jnp.sum(jnp.sum(err * err, axis=1, keepdims=True), axis=0, keepdims=True) / D_MODEL
            dz, dg, db = vjp(err / D_MODEL)

            @pl.when(t == 0)
            def _():
                for r in (dg_ref, db_ref, loss_ref):
                    r[...] = jnp.zeros_like(r)

            dz_ref[...] = dz
            dzb_ref[...] = dz.astype(BF16)
            dg_ref[...] += dg
            db_ref[...] += db
            loss_ref[...] += jnp.broadcast_to(part_loss, (1, LANES))

    h0, h1 = halves
    row = jax.ShapeDtypeStruct((1, D_MODEL), F32)
    return _pcall(
        body, name="ffn_fwd", grid=(nt, FFN_J),
        in_specs=[full, h0["w_up"], h1["w_up"], h0["taps"], h1["taps"], h0["bias"], h1["bias"], wd_spec, full, vec, vec,
                  full],
        out_specs=[u_blk, pl.BlockSpec((tb, FFN_W), lambda t, j: (t, j)), full, vec, vec,
                   pl.BlockSpec((1, LANES), lambda t, j: (0, 0)), full],
        out_shape=[jax.ShapeDtypeStruct((2, seq, D_FF), F32), jax.ShapeDtypeStruct((seq, D_FF), BF16),
                   jax.ShapeDtypeStruct((seq, D_MODEL), F32), row, row, jax.ShapeDtypeStruct((1, LANES), F32),
                   jax.ShapeDtypeStruct((seq, D_MODEL), BF16)],
        scratch_shapes=[pltpu.VMEM((tb, D_MODEL), F32), pltpu.VMEM((FFN_J, 2, SUBLANES, FFN_W), F32)],
        compiler_params=_params(("arbitrary", "arbitrary"), _ffn_vmem(tb)),
    )(x2b, w_up, w_up, conv_w, conv_w, conv_b, conv_b, w_down, x2, ln_g, ln_b, target)


def _ffn_bwd(u, conv_w, conv_b, dz3b, dz3, w_down, w_up, z2, ln_g, ln_b):
    seq = dz3.shape[0]
    tb = min(FFN_TB, seq)
    nt = seq // tb
    row8 = tb // SUBLANES
    tb, full, vec, halves, wd_spec, u_blk = _ffn_common_specs(seq, lambda t: nt - 1 - t)
    halo = pl.BlockSpec((2, SUBLANES, FFN_W), lambda t, j: (0, jnp.maximum((nt - 1 - t) * row8 - 1, 0), j))

    def body(u_ref, halo_ref, tg_ref, tv_ref, bg_ref, bv_ref, dzb_ref, wd_ref, wg_ref, wv_ref, dz3_ref, z_ref, g_ref,
             b_ref, du_ref, dw_ref, dbias_ref, dz_ref, dg_ref, db_ref, dz2b_ref, acc, carry):
        t, j = pl.program_id(0), pl.program_id(1)

        @pl.when((t == 0) & (j == 0))
        def _():
            for r in (dw_ref, dbias_ref, dg_ref, db_ref):
                r[...] = jnp.zeros_like(r)

        pieces = [pl.ds(off, width) for off, width in FFN_PIECES]
        dzb = dzb_ref[...]
        dhs = [_dg(dzb, wd_ref[cols, :], 1, 1) for cols in pieces]
        first = t == nt - 1
        dus = []
        for cols, dh in zip(pieces, dhs):
            args = (jnp.where(first, 0.0, halo_ref[0, :, cols]), u_ref[0, :, cols],
                    jnp.where(first, 0.0, halo_ref[1, :, cols]), u_ref[1, :, cols],
                    *_conv_params(tg_ref, bg_ref, cols), *_conv_params(tv_ref, bv_ref, cols))
            _, vjp = jax.vjp(_ffn_mid, *args)
            dhg, dxg, dhv, dxv, g0, g1, g2, gb, v0, v1, v2, vb = vjp(dh)
            zeros = jnp.zeros((tb - SUBLANES, dh.shape[1]), F32)
            dug = (dxg + jnp.concatenate([zeros, jnp.where(t == 0, 0.0, carry[j, 0, :, cols])], axis=0)).astype(BF16)
            duv = (dxv + jnp.concatenate([zeros, jnp.where(t == 0, 0.0, carry[j, 1, :, cols])], axis=0)).astype(BF16)
            carry[j, 0, :, cols] = dhg
            carry[j, 1, :, cols] = dhv
            du_ref[0, :, cols] = dug
            du_ref[1, :, cols] = duv
            for half, parts in enumerate(((g0, g1, g2), (v0, v1, v2))):
                for d, p in enumerate(parts):
                    dw_ref[j, half, d:d + 1, cols] += p
            dbias_ref[j, 0, :, cols] += gb
            dbias_ref[j, 1, :, cols] += vb
            dus.append((dug, duv))
        part = None
        for cols, (dug, duv) in zip(pieces, dus):
            p = _dg(dug, wg_ref[:, cols], 1, 1) + _dg(duv, wv_ref[:, cols], 1, 1)
            part = p if part is None else part + p

        @pl.when(j == 0)
        def _():
            acc[...] = part

        @pl.when(j > 0)
        def _():
            acc[...] += part

        @pl.when(j == FFN_J - 1)
        def _():
            _, ln_vjp = jax.vjp(_layer_norm, z_ref[...], g_ref[...], b_ref[...])
            dz, dg, db = ln_vjp(acc[...] + ALPHA * dz3_ref[...])
            dz_ref[...] = dz
            dz2b_ref[...] = dz.astype(BF16)
            dg_ref[...] += dg
            db_ref[...] += db

    h0, h1 = halves
    row = jax.ShapeDtypeStruct((1, D_MODEL), F32)
    whole = lambda *shape: pl.BlockSpec(shape, lambda t, j: (0,) * len(shape))
    return _pcall(
        body, name="ffn_bwd", grid=(nt, FFN_J),
        in_specs=[u_blk, halo, h0["taps"], h1["taps"], h0["bias"], h1["bias"], full, wd_spec, h0["w_up"], h1["w_up"],
                  full, full, vec, vec],
        out_specs=[u_blk, whole(FFN_J, 2, FFN_CONV, FFN_W), whole(FFN_J, 2, 1, FFN_W), full, vec, vec, full],
        out_shape=[jax.ShapeDtypeStruct((2, seq, D_FF), BF16), jax.ShapeDtypeStruct((FFN_J, 2, FFN_CONV, FFN_W), F32),
                   jax.ShapeDtypeStruct((FFN_J, 2, 1, FFN_W), F32), jax.ShapeDtypeStruct((seq, D_MODEL), F32), row, row,
                   jax.ShapeDtypeStruct((seq, D_MODEL), BF16)],
        scratch_shapes=[pltpu.VMEM((tb, D_MODEL), F32), pltpu.VMEM((FFN_J, 2, SUBLANES, FFN_W), F32)],
        compiler_params=_params(("arbitrary", "arbitrary"), _ffn_vmem(tb)),
    )(u, u, conv_w, conv_w, conv_b, conv_b, dz3b, w_down, w_up, w_up, dz3, z2, ln_g, ln_b)


def _adamw_math(w, g, m, v):
    m_new = ADAM_B1 * m + (1.0 - ADAM_B1) * g
    v_new = ADAM_B2 * v + (1.0 - ADAM_B2) * jnp.square(g)
    m_hat = m_new / (1.0 - ADAM_B1 ** ADAM_STEP)
    v_hat = v_new / (1.0 - ADAM_B2 ** ADAM_STEP)
    return -ADAM_LR * (m_hat / (jnp.sqrt(v_hat) + ADAM_EPS) + ADAM_WD * w), m_new, v_new


def _adamw(name, w, g, m, v):
    rows, cols = w.shape
    tr = _tile(rows, (256, 176, 128, 64, 40, 32, 16, 8))

    def body(w_ref, g_ref, m_ref, v_ref, d_ref, nm_ref, nv_ref):
        d_ref[...], nm_ref[...], nv_ref[...] = _adamw_math(w_ref[...], g_ref[...], m_ref[...], v_ref[...])

    spec = pl.BlockSpec((tr, cols), lambda i: (i, 0))
    sh = jax.ShapeDtypeStruct((rows, cols), F32)
    return _pcall(
        body, name=name, grid=(rows // tr,), in_specs=[spec] * 4, out_specs=[spec] * 3, out_shape=[sh] * 3,
        compiler_params=_params(("arbitrary",), 14 * _nbytes((tr, -(-cols // LANES) * LANES), F32)),
    )(w, g, m, v)


def _adamw_halves(name, core, w, mine, theirs, m, v):
    rows, cols = w.shape
    tr = _tile(rows // 2, (256, 176, 128))
    nbh = rows // 2 // tr

    def body(c_ref, w_ref, a_ref, b_ref, m_ref, v_ref, g_ref, d_ref, nm_ref, nv_ref):
        g = jnp.where(pl.program_id(0) // nbh == c_ref[0], a_ref[...], b_ref[...])
        g_ref[...] = g
        d_ref[...], nm_ref[...], nv_ref[...] = _adamw_math(w_ref[...], g, m_ref[...], v_ref[...])

    spec = pl.BlockSpec((tr, cols), lambda i, c_ref: (i, 0))
    half = pl.BlockSpec((tr, cols), lambda i, c_ref: (i % nbh, 0))
    sh = jax.ShapeDtypeStruct((rows, cols), F32)
    grid_spec = pltpu.PrefetchScalarGridSpec(
        num_scalar_prefetch=1, grid=(rows // tr,), in_specs=[spec, half, half, spec, spec], out_specs=[spec] * 4)
    return _pcall(
        body, name=name, grid_spec=grid_spec, out_shape=[sh] * 4,
        compiler_params=_params(("arbitrary",), 18 * _nbytes((tr, -(-cols // LANES) * LANES), F32)),
    )(core, w, mine, theirs, m, v)


MESH = pl.DeviceIdType.MESH
ANY = pl.BlockSpec(memory_space=pl.ANY)
N_CHIPS = 4
N_DEV = 8
BF16_ROWS = 16


def _me():
    return lax.axis_index("x"), lax.axis_index("y"), lax.axis_index("c")


def _other_chips(x, y):
    return [(1 - x, y), (x, 1 - y), (1 - x, 1 - y)]


def _remote(src, dst, ssem, rsem, dev):
    return pltpu.make_async_remote_copy(src_ref=src, dst_ref=dst, send_sem=ssem, recv_sem=rsem,
                                        device_id=dev, device_id_type=MESH)


def _half_rows(ref_rows, cc):
    half = ref_rows // 2
    return pl.ds(pl.multiple_of(cc * half, BF16_ROWS), half)


def _gather_weights(shards):
    n = len(shards)
    n_ici = n * (N_CHIPS - 1)

    def body(*refs):
        ins, outs, (ssem, rsem, lsem, lrsem) = refs[:n], refs[n:2 * n], refs[2 * n:]
        x, y, c = _me()
        k_me = 2 * x + y
        sib = (x, y, 1 - c)
        chips = _other_chips(x, y)
        started = []
        for i, (w_ref, o_ref) in enumerate(zip(ins, outs)):
            cp = _remote(w_ref, o_ref.at[k_me], lsem.at[i], lrsem.at[i], sib)
            cp.start()
            started.append(cp)
        for r, (px, py) in enumerate(chips):
            for i, (w_ref, o_ref) in enumerate(zip(ins, outs)):
                rows = _half_rows(w_ref.shape[0], c)
                s = r * n + i
                cp = _remote(w_ref.at[rows], o_ref.at[k_me, rows], ssem.at[s], rsem.at[s], (px, py, c))
                cp.start()
                started.append(cp)
        for r, (px, py) in enumerate(chips):
            for i, o_ref in enumerate(outs):
                blk = o_ref.at[2 * px + py, _half_rows(o_ref.shape[1], c)]
                s = r * n + i
                _remote(blk, blk, ssem.at[s], rsem.at[s], (px, py, c)).wait_recv()
                cp = _remote(blk, blk, ssem.at[n_ici + s], rsem.at[n_ici + s], sib)
                cp.start()
                started.append(cp)
        for r, (px, py) in enumerate(chips):
            for i, o_ref in enumerate(outs):
                blk = o_ref.at[2 * px + py, _half_rows(o_ref.shape[1], 1 - c)]
                s = n_ici + r * n + i
                _remote(blk, blk, ssem.at[s], rsem.at[s], sib).wait_recv()
        for cp in started[n:]:
            cp.wait_send()
        for cp in started[:n]:
            cp.wait()

    return _pcall(
        body, name="gather_weights", in_specs=[ANY] * n, out_specs=[ANY] * n,
        out_shape=[jax.ShapeDtypeStruct((N_CHIPS,) + s.shape, s.dtype) for s in shards],
        scratch_shapes=[pltpu.SemaphoreType.DMA((2 * n_ici,)), pltpu.SemaphoreType.DMA((2 * n_ici,)),
                        pltpu.SemaphoreType.DMA((n,)), pltpu.SemaphoreType.DMA((n,))],
    )(*shards)


def _swap_halves(name, grads):
    n = len(grads)

    def body(*refs):
        ins, outs, (ssem, rsem) = refs[:n], refs[n:2 * n], refs[2 * n:]
        x, y, c = _me()
        copies = []
        for i, (g_ref, o_ref) in enumerate(zip(ins, outs)):
            for k in range(N_CHIPS):
                s = i * N_CHIPS + k
                cp = _remote(g_ref.at[k, _half_rows(g_ref.shape[1], 1 - c)], o_ref.at[k], ssem.at[s], rsem.at[s],
                             (x, y, 1 - c))
                cp.start()
                copies.append(cp)
        for cp in copies:
            cp.wait()

    return _pcall(
        body, name=name, in_specs=[ANY] * n, out_specs=[ANY] * n,
        out_shape=[jax.ShapeDtypeStruct((N_CHIPS, g.shape[1] // 2, g.shape[2]), g.dtype) for g in grads],
        scratch_shapes=[pltpu.SemaphoreType.DMA((n * N_CHIPS,)), pltpu.SemaphoreType.DMA((n * N_CHIPS,))],
    )(*grads)


SEM = pl.BlockSpec(memory_space=pltpu.SEMAPHORE)
IN_HBM = pl.BlockSpec(memory_space=pltpu.HBM)
SPLIT_PARAMS = dict(compiler_params=pltpu.CompilerParams(has_side_effects=pltpu.SideEffectType.DATAFLOW_SIDE_EFFECTING))


def _gather_start(name, shards):
    n = len(shards)
    n_sem = n * N_CHIPS

    def body(*refs):
        ins, lands, (ssem, rsem), token = refs[:n], refs[n:2 * n], refs[2 * n:2 * n + 2], refs[-1]
        x, y, c = _me()
        k_me = 2 * x + y
        for i, (w_ref, l_ref) in enumerate(zip(ins, lands)):
            _remote(w_ref, l_ref.at[k_me], ssem.at[i], rsem.at[i], (x, y, 1 - c)).start()
        for r, (px, py) in enumerate(_other_chips(x, y)):
            for i, (w_ref, l_ref) in enumerate(zip(ins, lands)):
                rows = _half_rows(w_ref.shape[0], c)
                s = (r + 1) * n + i
                _remote(w_ref.at[rows], l_ref.at[k_me, rows], ssem.at[s], rsem.at[s], (px, py, c)).start()
        token[...] = jnp.zeros_like(token)

    src = [pltpu.HBM(s.shape, s.dtype) for s in shards]
    dst = [pltpu.HBM((N_CHIPS,) + s.shape, s.dtype) for s in shards]
    outs = _call(
        body, name=name, in_specs=[IN_HBM] * (2 * n),
        out_specs=[SEM, SEM] + [IN_HBM] * (2 * n) + [pl.BlockSpec(memory_space=pltpu.VMEM)],
        out_shape=[pltpu.SemaphoreType.DMA((n_sem,)), pltpu.SemaphoreType.DMA((n_sem,))] + src + dst
        + [jax.ShapeDtypeStruct((SUBLANES, LANES), F32)],
        input_output_aliases={i: 2 + i for i in range(2 * n)}, **SPLIT_PARAMS,
    )(*[pltpu.with_memory_space_constraint(s, pltpu.HBM) for s in shards],
      *[pltpu.with_memory_space_constraint(lax.empty(d.shape, d.dtype), pltpu.HBM) for d in dst])
    return outs[:-1], outs[-1]


def _gather_wait(name, handle, after):
    ssem, rsem, thru = handle[0], handle[1], handle[2:]
    n = len(thru) // 2

    def body(*refs):
        ins, lands, (ssem_ref, rsem_ref) = refs[:n], refs[n:2 * n], refs[2 * n:2 * n + 2]
        x, y, c = _me()
        k_me = 2 * x + y
        for i, (w_ref, l_ref) in enumerate(zip(ins, lands)):
            cp = _remote(w_ref, l_ref.at[k_me], ssem_ref.at[i], rsem_ref.at[i], (x, y, 1 - c))
            cp.wait_send()
            cp.wait_recv()
        for r, (px, py) in enumerate(_other_chips(x, y)):
            for i, (w_ref, l_ref) in enumerate(zip(ins, lands)):
                rows = _half_rows(w_ref.shape[0], c)
                s = (r + 1) * n + i
                cp = _remote(w_ref.at[rows], l_ref.at[2 * px + py, rows], ssem_ref.at[s], rsem_ref.at[s], (px, py, c))
                cp.wait_send()
                cp.wait_recv()

    outs = _call(
        body, name=name, in_specs=[IN_HBM] * (2 * n) + [SEM, SEM, ANY], out_specs=[IN_HBM] * (2 * n),
        out_shape=[pltpu.HBM(t.shape, t.dtype) for t in thru],
        input_output_aliases={i: i for i in range(2 * n)}, **SPLIT_PARAMS,
    )(*thru, ssem, rsem, after)
    return outs[n:]


def _forward_halves(name, blocks):
    n = len(blocks)
    n_sem = n * (N_CHIPS - 1)

    def body(*refs):
        outs, (ssem, rsem) = refs[n:2 * n], refs[2 * n:]
        x, y, c = _me()
        sib = (x, y, 1 - c)
        chips = _other_chips(x, y)
        sends = []
        for r, (px, py) in enumerate(chips):
            for i, o_ref in enumerate(outs):
                blk = o_ref.at[2 * px + py, _half_rows(o_ref.shape[1], c)]
                cp = _remote(blk, blk, ssem.at[r * n + i], rsem.at[r * n + i], sib)
                cp.start()
                sends.append(cp)
        for r, (px, py) in enumerate(chips):
            for i, o_ref in enumerate(outs):
                blk = o_ref.at[2 * px + py, _half_rows(o_ref.shape[1], 1 - c)]
                _remote(blk, blk, ssem.at[r * n + i], rsem.at[r * n + i], sib).wait_recv()
        for cp in sends:
            cp.wait_send()

    return _pcall(
        body, name=name, in_specs=[ANY] * n, out_specs=[ANY] * n,
        out_shape=[jax.ShapeDtypeStruct(b.shape, b.dtype) for b in blocks],
        input_output_aliases={i: i for i in range(n)},
        scratch_shapes=[pltpu.SemaphoreType.DMA((n_sem,)), pltpu.SemaphoreType.DMA((n_sem,))],
    )(*blocks)


def _scatter_start(name, parts):
    n = len(parts)
    n_sem = n * (N_CHIPS - 1)

    def body(*refs):
        ins, lands, (ssem, rsem), token = refs[:n], refs[n:2 * n], refs[2 * n:2 * n + 2], refs[-1]
        x, y, c = _me()
        k_me = 2 * x + y
        for r, (px, py) in enumerate(_other_chips(x, y)):
            for i, (p_ref, l_ref) in enumerate(zip(ins, lands)):
                s = r * n + i
                _remote(p_ref.at[2 * px + py], l_ref.at[k_me], ssem.at[s], rsem.at[s], (px, py, c)).start()
        token[...] = jnp.zeros_like(token)

    hbm = [pltpu.HBM(p.shape, p.dtype) for p in parts]
    outs = _call(
        body, name=name, in_specs=[IN_HBM] * (2 * n),
        out_specs=[SEM, SEM] + [IN_HBM] * (2 * n) + [pl.BlockSpec(memory_space=pltpu.VMEM)],
        out_shape=[pltpu.SemaphoreType.DMA((n_sem,)), pltpu.SemaphoreType.DMA((n_sem,))] + hbm + hbm
        + [jax.ShapeDtypeStruct((SUBLANES, LANES), F32)],
        input_output_aliases={i: 2 + i for i in range(2 * n)}, **SPLIT_PARAMS,
    )(*[pltpu.with_memory_space_constraint(p, pltpu.HBM) for p in parts],
      *[pltpu.with_memory_space_constraint(lax.empty(p.shape, p.dtype), pltpu.HBM) for p in parts])
    return outs[:-1], outs[-1]


def _scatter_wait(name, handle, after):
    ssem, rsem, thru = handle[0], handle[1], handle[2:]
    n = len(thru) // 2

    def body(*refs):
        ins, lands, (ssem_ref, rsem_ref) = refs[:n], refs[n:2 * n], refs[2 * n:2 * n + 2]
        x, y, c = _me()
        for r, (px, py) in enumerate(_other_chips(x, y)):
            for i, (p_ref, l_ref) in enumerate(zip(ins, lands)):
                s = r * n + i
                cp = _remote(p_ref.at[2 * px + py], l_ref.at[2 * px + py], ssem_ref.at[s], rsem_ref.at[s], (px, py, c))
                cp.wait_send()
                cp.wait_recv()

    outs = _call(
        body, name=name, in_specs=[IN_HBM] * (2 * n) + [SEM, SEM, ANY], out_specs=[IN_HBM] * (2 * n),
        out_shape=[pltpu.HBM(t.shape, t.dtype) for t in thru],
        input_output_aliases={i: i for i in range(2 * n)}, **SPLIT_PARAMS,
    )(*thru, ssem, rsem, after)
    return outs[n:]


def _share_halves(halves):
    n = len(halves)

    def body(*refs):
        ins, outs, (ssem, rsem) = refs[:n], refs[n:2 * n], refs[2 * n:]
        x, y, c = _me()
        copies = [_remote(r_ref, o_ref, ssem.at[i], rsem.at[i], (x, y, 1 - c))
                  for i, (r_ref, o_ref) in enumerate(zip(ins, outs))]
        for cp in copies:
            cp.start()
        for cp in copies:
            cp.wait()

    return _pcall(
        body, name="share_halves", in_specs=[ANY] * n, out_specs=[ANY] * n,
        out_shape=[jax.ShapeDtypeStruct(h.shape, h.dtype) for h in halves],
        scratch_shapes=[pltpu.SemaphoreType.DMA((n,)), pltpu.SemaphoreType.DMA((n,))],
    )(*halves)


def _exchange_small(v, reduce):
    rows = v.shape[0]

    def body(v_ref, out_ref, buf, ssem, rsem):
        x, y, c = _me()
        me = 4 * x + 2 * y + c
        peers = [((x + bx) % 2, (y + by) % 2, (c + bc) % 2)
                 for bx in (0, 1) for by in (0, 1) for bc in (0, 1) if (bx, by, bc) != (0, 0, 0)]
        dst = buf if reduce else out_ref
        dst[me] = v_ref[...]
        sends = [_remote(v_ref, dst.at[me], ssem.at[r], rsem.at[r], p) for r, p in enumerate(peers)]
        for cp in sends:
            cp.start()
        for r, (px, py, pc) in enumerate(peers):
            blk = dst.at[4 * px + 2 * py + pc]
            _remote(blk, blk, ssem.at[r], rsem.at[r], (px, py, pc)).wait_recv()
        if reduce:
            acc = buf[0]
            for d in range(1, N_DEV):
                acc = acc + buf[d]
            out_ref[...] = acc
        for cp in sends:
            cp.wait_send()

    vm = pl.BlockSpec(memory_space=pltpu.VMEM)
    out_shape = jax.ShapeDtypeStruct((rows, LANES) if reduce else (N_DEV, rows, LANES), F32)
    buf_shape = (N_DEV, rows, LANES) if reduce else (SUBLANES, LANES)
    return _pcall(
        body, pin=False, name="reduce_small" if reduce else "gather_small", in_specs=[vm], out_specs=vm, out_shape=out_shape,
        scratch_shapes=[pltpu.VMEM(buf_shape, F32), pltpu.SemaphoreType.DMA((N_DEV - 1,)),
                        pltpu.SemaphoreType.DMA((N_DEV - 1,))],
        compiler_params=pltpu.CompilerParams(vmem_limit_bytes=32 * 1024 * 1024),
    )(v)


def _add_pair(name, core, g, theirs):
    _, half, cols = theirs.shape
    tr = _tile(half, (256, 176, 128))
    nb = half // tr

    def body(c_ref, g_ref, t_ref, o32_ref, o16_ref):
        s = g_ref[...] + t_ref[...]
        o32_ref[...] = s
        o16_ref[...] = s.astype(BF16)

    spec = pl.BlockSpec((None, tr, cols), lambda k, i, c_ref: (k, i, 0))
    grid_spec = pltpu.PrefetchScalarGridSpec(
        num_scalar_prefetch=1, grid=(N_CHIPS, nb),
        in_specs=[pl.BlockSpec((None, tr, cols), lambda k, i, c_ref: (k, c_ref[0] * nb + i, 0)), spec],
        out_specs=[spec, spec])
    return _pcall(
        body, name=name, grid_spec=grid_spec,
        out_shape=[jax.ShapeDtypeStruct(theirs.shape, F32), jax.ShapeDtypeStruct(theirs.shape, BF16)],
        compiler_params=_params(("arbitrary", "arbitrary"), 8 * _nbytes((tr, cols + LANES), F32)),
    )(core, g, theirs)


def _add_chips(name, chip, p32, recv):
    _, half, cols = p32.shape
    tr = _tile(half, (256, 176, 128))

    def body(k_ref, p_ref, r0_ref, r1_ref, r2_ref, o_ref):
        o_ref[...] = ((p_ref[...] + r0_ref[...].astype(F32)) + r1_ref[...].astype(F32)) + r2_ref[...].astype(F32)

    def other(r):
        return pl.BlockSpec((None, tr, cols), lambda i, k_ref: (r + (k_ref[0] <= r).astype(jnp.int32), i, 0))
    grid_spec = pltpu.PrefetchScalarGridSpec(
        num_scalar_prefetch=1, grid=(half // tr,),
        in_specs=[pl.BlockSpec((None, tr, cols), lambda i, k_ref: (k_ref[0], i, 0)), other(0), other(1), other(2)],
        out_specs=pl.BlockSpec((tr, cols), lambda i, k_ref: (i, 0)))
    return _pcall(
        body, name=name, grid_spec=grid_spec, out_shape=jax.ShapeDtypeStruct((half, cols), F32),
        compiler_params=_params(("arbitrary",), 10 * _nbytes((tr, cols + LANES), F32)),
    )(chip, p32, recv, recv, recv)


def kernel(x, mem, w_in, b_in, hg_lb_logits, hg_norm_w, ml_conv_w, ml_conv_b, ml_norm_w, w_out, ln1_g, ln1_b, ca_wq, ca_wkv, ca_wo, ln2_g, ln2_b, ffn_w_up, ffn_conv_w, ffn_conv_b, ffn_w_down, ln3_g, ln3_b, loss_target, m_w_in, m_b_in, m_hg_lb_logits, m_hg_norm_w, m_ml_conv_w, m_ml_conv_b, m_ml_norm_w, m_w_out, m_ln1_g, m_ln1_b, m_ca_wq, m_ca_wkv, m_ca_wo, m_ln2_g, m_ln2_b, m_ffn_w_up, m_ffn_conv_w, m_ffn_conv_b, m_ffn_w_down, m_ln3_g, m_ln3_b, v_w_in, v_b_in, v_hg_lb_logits, v_hg_norm_w, v_ml_conv_w, v_ml_conv_b, v_ml_norm_w, v_w_out, v_ln1_g, v_ln1_b, v_ca_wq, v_ca_wkv, v_ca_wo, v_ln2_g, v_ln2_b, v_ffn_w_up, v_ffn_conv_w, v_ffn_conv_b, v_ffn_w_down, v_ln3_g, v_ln3_b):
    return _train_step(dict(locals()))


WEIGHTS = ("w_in", "b_in", "hg_lb_logits", "hg_norm_w", "ml_conv_w", "ml_conv_b", "ml_norm_w", "w_out", "ln1_g",
           "ln1_b", "ca_wq", "ca_wkv", "ca_wo", "ln2_g", "ln2_b", "ffn_w_up", "ffn_conv_w", "ffn_conv_b",
           "ffn_w_down", "ln3_g", "ln3_b")
MATRICES = ("w_in", "w_out", "ca_wq", "ca_wkv", "ca_wo", "ffn_w_up", "ffn_w_down")
COL_SHARDED = ("w_in", "ca_wkv", "ffn_w_up", "ml_conv_w", "ffn_conv_w")
SMALL = tuple(n for n in WEIGHTS if n not in MATRICES)
PART_ROWS = 16


def _part_rows(shape, lead):
    n = 1
    for s in shape[lead:]:
        n *= s
    return -(-n // (LANES * PART_ROWS)) * PART_ROWS


def _pack(arrs, dtype, lead=0, rows=None):
    parts = []
    for a in arrs:
        head = a.shape[:lead]
        flat = a.reshape(head + (-1,)).astype(dtype)
        pad = _part_rows(a.shape, lead) * LANES - flat.shape[-1]
        flat = jnp.pad(flat, [(0, 0)] * lead + [(0, pad)])
        parts.append(flat.reshape(head + (-1, LANES)))
    used = sum(p.shape[lead] for p in parts)
    if rows is not None and rows > used:
        parts.append(jnp.zeros(parts[0].shape[:lead] + (rows - used, LANES), dtype))
    return jnp.concatenate(parts, axis=lead)


def _unpack(buf, shapes):
    lead = buf.shape[:-2]
    outs, r = [], 0
    for sh in shapes:
        n = 1
        for s in sh:
            n *= s
        nr = _part_rows(sh, 0)
        flat = buf[..., r:r + nr, :].reshape(lead + (nr * LANES,))
        outs.append(flat[..., :n].reshape(lead + tuple(sh)))
        r += nr
    return outs


def _cat_cols(s):
    return jnp.moveaxis(s, 0, 1).reshape(s.shape[1], -1)


def _split_cols(g):
    return jnp.moveaxis(g.reshape(g.shape[0], N_CHIPS, -1), 1, 0)


def _stack_rows(s):
    return s.reshape(-1, s.shape[-1])


def _train_step(a):
    xs, mems, tgt = a["x"][0], a["mem"][0], a["loss_target"][0]
    core = lax.axis_index("c").astype(jnp.int32).reshape(1)
    chip = (2 * lax.axis_index("x") + lax.axis_index("y")).astype(jnp.int32).reshape(1)
    k_me = chip[0]
    shard = {n: a[n][0] for n in MATRICES}

    later = [n for n in MATRICES if n != "w_in"]
    taps = _exchange_small(_pack([a["ml_conv_w"][0], a["ffn_conv_w"][0]], F32), reduce=False)
    w = {"w_in": jnp.pad(_cat_cols(_gather_weights([shard["w_in"].astype(BF16)])[0]), ((0, 0), (0, D_IN_PAD - D_IN)))}
    gathering, token = _gather_start("gather_start", [shard[n].astype(BF16) for n in later])
    taps = taps.reshape((N_CHIPS, 2) + taps.shape[1:])[:, 0]
    ml_cw, ffn_cw = [_cat_cols(s) for s in _unpack(taps, [a["ml_conv_w"].shape[1:], a["ffn_conv_w"].shape[1:]])]
    b_in_p = jnp.pad(a["b_in"], ((0, 0), (0, D_IN_PAD - D_IN))) + token[0:1, 0:1]
    mixer_w = (a["hg_lb_logits"], a["hg_norm_w"], ml_cw, a["ml_conv_b"], a["ml_norm_w"])
    up_cols = a["ffn_w_up"].shape[-1]

    xb = xs.astype(BF16)
    proj = _mm("proj", "nn", xb, w["w_in"], bias=b_in_p, tm=256, tn=D_IN_PAD)
    y, hst, cst, nst, mst = _mixer_fwd(proj, *mixer_w)
    w.update(zip(later, _forward_halves("forward_halves", _gather_wait("gather_wait", gathering, y))))
    for n in ("w_out", "ca_wq", "ca_wo", "ffn_w_down"):
        w[n] = _stack_rows(w[n])
    z1, x1, x1b = _mm("mix_out", "nn", y, w["w_out"], res=xs, res_scale=ALPHA, ln=("fwd", a["ln1_g"], a["ln1_b"]),
                      copy_dtype=BF16)
    q = _mm("ca_q", "nn", x1b, w["ca_wq"], out_dtype=BF16, tn=D_MODEL)
    kv = _mm("ca_kv", "nn", mems, w["ca_wkv"])
    o = _attn_fwd(q, kv)
    z2, x2, x2b = _mm("ca_out", "nn", o, w["ca_wo"], res=x1, res_scale=ALPHA, ln=("fwd", a["ln2_g"], a["ln2_b"]),
                      copy_dtype=BF16)
    w_up = _cat_cols(w["ffn_w_up"])
    u, hmid, dz3, g_ln3g, g_ln3b, loss_part, dz3b = _ffn_fwd(
        x2b, x2, w_up, ffn_cw, a["ffn_conv_b"], w["ffn_w_down"], a["ln3_g"], a["ln3_b"], tgt)

    grads = {"ln3_g": g_ln3g, "ln3_b": g_ln3b}
    grads["ffn_w_down"] = _mm("g_w_down", "tn", hmid, dz3b, tm=D_FF // 2, tn=D_MODEL)
    du, g_cw, g_cb, dz2, grads["ln2_g"], grads["ln2_b"], dz2b = _ffn_bwd(
        u, ffn_cw, a["ffn_conv_b"], dz3b, dz3, w["ffn_w_down"], w_up, z2, a["ln2_g"], a["ln2_b"])
    grads["ffn_conv_w"] = jnp.transpose(g_cw, (2, 1, 0, 3)).reshape(FFN_CONV, 2 * D_FF)
    grads["ffn_conv_b"] = jnp.transpose(g_cb, (2, 1, 0, 3)).reshape(1, 2 * D_FF)
    grads["ffn_w_up"] = _mm("g_w_up", "tn", x2b, du, out_groups=N_CHIPS, tm=D_MODEL, tn=up_cols)
    grads["ffn_w_down"] = grads["ffn_w_down"].reshape((N_CHIPS,) + shard["ffn_w_down"].shape)
    pending = {}

    def reduce_start(tag, names):
        group = [grads[n] for n in names]
        sums = [_add_pair("add_pair_" + n, core, g, t)
                for n, g, t in zip(names, group, _swap_halves("swap_halves_" + tag, group))]
        handle, token = _scatter_start("scatter_start_" + tag, [s16 for _, s16 in sums])
        pending[tag] = (names, [s32 for s32, _ in sums], handle)
        return token[0:1, 0:1]

    zero = reduce_start("ffn", ("ffn_w_up", "ffn_w_down"))
    do = _mm("d_o", "nt", dz2b, w["ca_wo"], bias=jnp.zeros((1, D_MODEL), F32) + zero, out_dtype=BF16, tn=D_MODEL)
    grads["ca_wo"] = _mm("g_wo", "tn", o, dz2b, tm=D_MODEL, tn=D_MODEL)
    dq, dkv = _attn_bwd(q, kv, do)
    grads["ca_wq"] = _mm("g_wq", "tn", x1b, dq, tm=D_MODEL, tn=D_MODEL)
    grads["ca_wkv"] = _mm("g_wkv", "tn", mems, dkv, out_groups=N_CHIPS, tm=D_MODEL)
    dz1, grads["ln1_g"], grads["ln1_b"], dz1b = _mm("d_x1", "nt", dq, w["ca_wq"], res=dz2, res_scale=ALPHA,
                                                    ln=("bwd", z1, a["ln1_g"], a["ln1_b"]), copy_dtype=BF16)
    dy = _mm("d_y", "nt", dz1b, w["w_out"], tn=D_MODEL)
    grads["w_out"] = _mm("g_w_out", "tn", y, dz1b, tm=D_MODEL, tn=D_MODEL)
    for n in ("w_out", "ca_wq", "ca_wo"):
        grads[n] = grads[n].reshape((N_CHIPS,) + shard[n].shape)
    zero = reduce_start("attn", ("w_out", "ca_wq", "ca_wkv", "ca_wo"))
    (dproj, g_b_in, grads["hg_lb_logits"], grads["hg_norm_w"], grads["ml_conv_w"], grads["ml_conv_b"],
     grads["ml_norm_w"]) = _mixer_bwd(proj, dy, hst, cst, nst, mst, mixer_w[0], mixer_w[1] + zero, *mixer_w[2:])
    grads["w_in"] = _split_cols(_mm("g_w_in", "tn", xb, dproj, tm=D_MODEL, tn=up_cols)[:, :D_IN])
    grads["b_in"] = g_b_in[:, :D_IN]
    zero = reduce_start("in", ("w_in",))
    dx = _mm("d_x", "nt", dproj, w["w_in"], bias=jnp.zeros((1, D_MODEL), F32) + zero, res=dz1, res_scale=ALPHA,
             tm=256, tn=D_MODEL)

    halves = {}
    for tag, (names, sums32, handle) in pending.items():
        for n, s32, r in zip(names, sums32, _scatter_wait("scatter_wait_" + tag, handle, dx)):
            halves[n] = _add_chips("add_chips_" + n, chip, s32, r)
    halves = [halves[n] for n in MATRICES]
    other_halves = _share_halves(halves)

    small_shapes = [grads[n].shape for n in SMALL] + [loss_part.shape]
    summed = _unpack(_exchange_small(_pack([grads[n] for n in SMALL] + [loss_part], F32), reduce=True), small_shapes)
    loss = summed[-1][0, 0]
    for n, g in zip(SMALL, summed[:-1]):
        if n in COL_SHARDED:
            cols = a[n].shape[-1]
            g = lax.dynamic_slice_in_dim(g, k_me * cols, cols, axis=1)
        grads[n] = g

    delta, new_m, new_v = {}, {}, {}
    for n, mine, theirs in zip(MATRICES, halves, other_halves):
        grads[n], delta[n], new_m[n], new_v[n] = _adamw_halves(
            "adamw_" + n, core, shard[n], mine, theirs, a["m_" + n][0], a["v_" + n][0])
    small_w = [a[n][0] if a[n].ndim == 3 else a[n] for n in SMALL]
    small_m = [a["m_" + n][0] if a[n].ndim == 3 else a["m_" + n] for n in SMALL]
    small_v = [a["v_" + n][0] if a[n].ndim == 3 else a["v_" + n] for n in SMALL]
    shapes = [w.shape for w in small_w]
    packed = [_pack(l, F32) for l in (small_w, [grads[n] for n in SMALL], small_m, small_v)]
    for out, buf in zip((delta, new_m, new_v), _adamw("adamw_small", *packed)):
        for n, v in zip(SMALL, _unpack(buf, shapes)):
            out[n] = v

    def shaped(d):
        return [d[n].reshape(a[n].shape) for n in WEIGHTS]
    return (loss, dx[None], *shaped(grads), *shaped(delta), *shaped(new_m), *shaped(new_v))
```

```python
import functools

import jax
import jax.numpy as jnp
from jax import lax
from jax.experimental import pallas as pl
from jax.experimental.pallas import tpu as pltpu

F32 = jnp.float32
BF16 = jnp.bfloat16

D_MODEL = 1024
HEADS = 4
DK = 128
D_GRP = HEADS * DK
CHUNK = 64
ML_CONV = 4
FFN_CONV = 3
D_FF = 2816
CA_DH = D_MODEL // HEADS
DEPTH = 1
ALPHA = (2.0 * DEPTH) ** 0.25
LN_EPS = 1e-5
NEG_BIG = -1e30
D_IN = 8 * D_GRP + 2 * HEADS
D_IN_PAD = 8 * D_GRP + 128
ADAM_LR, ADAM_B1, ADAM_B2, ADAM_EPS, ADAM_WD, ADAM_STEP = 0.001, 0.9, 0.999, 1e-08, 0.01, 10

SUBLANES = 8
LANES = 128
VMEM_BYTES = 64 * 1024 * 1024


def _pcall(body, pin=True, **kw):
    if not pin:
        return _call(body, **kw)
    kw["out_shape"] = jax.tree.map(lambda s: pltpu.HBM(s.shape, s.dtype), kw["out_shape"])
    call = _call(body, **kw)

    def pinned(*args):
        return call(*[pltpu.with_memory_space_constraint(x, pltpu.HBM) if jnp.issubdtype(x.dtype, jnp.floating) else x
                      for x in args])
    return pinned


def _call(body, **kw):
    return pl.pallas_call(body, **kw)


def _params(semantics, vmem_bytes):
    limit = int(min(max(2 * vmem_bytes, 16 * 1024 * 1024), VMEM_BYTES - 8 * 1024 * 1024))
    return pltpu.CompilerParams(dimension_semantics=semantics, vmem_limit_bytes=limit)


def _nbytes(shape, dtype):
    n = 1
    for s in shape:
        n *= s
    return n * jnp.dtype(dtype).itemsize


def _dg(a, b, ca, cb):
    return lax.dot_general(a.astype(BF16), b.astype(BF16), (((ca,), (cb,)), ((), ())),
                           preferred_element_type=F32)


@jax.custom_vjp
def mm_nn(a, b):
    return _dg(a, b, 1, 0)


mm_nn.defvjp(lambda a, b: (_dg(a, b, 1, 0), (a, b)),
             lambda r, g: (_dg(g, r[1], 1, 1).astype(r[0].dtype), _dg(r[0], g, 0, 0).astype(r[1].dtype)))


@jax.custom_vjp
def mm_nt(a, b):
    return _dg(a, b, 1, 1)


mm_nt.defvjp(lambda a, b: (_dg(a, b, 1, 1), (a, b)),
             lambda r, g: (_dg(g, r[1], 1, 0).astype(r[0].dtype), _dg(g, r[0], 0, 0).astype(r[1].dtype)))


@jax.custom_vjp
def mm_tn(a, b):
    return _dg(a, b, 0, 0)


mm_tn.defvjp(lambda a, b: (_dg(a, b, 0, 0), (a, b)),
             lambda r, g: (_dg(r[1], g, 1, 1).astype(r[0].dtype), _dg(r[0], g, 1, 0).astype(r[1].dtype)))


def _tri(n, lower):
    r = lax.broadcasted_iota(jnp.int32, (n, n), 0)
    c = lax.broadcasted_iota(jnp.int32, (n, n), 1)
    return ((r >= c) if lower else (r <= c)).astype(F32)


def _tri_dot(lower, x):
    t = _tri(x.shape[0], lower).astype(BF16)
    hi = x.astype(BF16)
    rest = x - hi.astype(F32)
    mid = rest.astype(BF16)
    lo = (rest - mid.astype(F32)).astype(BF16)
    return sum(lax.dot_general(t, p, (((1,), (0,)), ((), ())), preferred_element_type=F32) for p in (hi, mid, lo))


@jax.custom_vjp
def cumsum_rows(x):
    return _tri_dot(True, x)


cumsum_rows.defvjp(lambda x: (_tri_dot(True, x), None), lambda _, g: (_tri_dot(False, g),))


def _shift_impl(halo, x, d):
    xx = jnp.concatenate([halo, x], axis=0)
    return pltpu.roll(xx, d, 0)[SUBLANES:]


@functools.partial(jax.custom_vjp, nondiff_argnums=(2,))
def shift_rows(halo, x, d):
    return _shift_impl(halo, x, d)


def _shift_bwd(d, _, g):
    n = g.shape[0] + SUBLANES
    gg = jnp.concatenate([jnp.zeros((SUBLANES, g.shape[1]), g.dtype), g], axis=0)
    r = pltpu.roll(gg, n - d, 0)
    return r[:SUBLANES], r[SUBLANES:]


shift_rows.defvjp(lambda halo, x, d: (_shift_impl(halo, x, d), None), _shift_bwd)


def causal_conv(halo, x, w_rows, b):
    k = len(w_rows)
    y = b + w_rows[k - 1] * x
    for d in range(1, k):
        y = y + w_rows[k - 1 - d] * shift_rows(halo, x, d)
    return y


def _sigmoid(x):
    return 1.0 / (1.0 + jnp.exp(-x))


def _silu(x):
    return x * _sigmoid(x)


def _log_sigmoid(x):
    return jnp.minimum(x, 0.0) - jnp.log(1.0 + jnp.exp(-jnp.abs(x)))


def _pick_row(x, i):
    row = lax.broadcasted_iota(jnp.int32, (x.shape[0], 1), 0)
    return jnp.sum(jnp.where(row == i, x, 0.0), axis=0, keepdims=True)


def _layer_norm(z, g, b):
    mu = jnp.mean(z, axis=-1, keepdims=True)
    zc = z - mu
    var = jnp.mean(zc * zc, axis=-1, keepdims=True)
    return zc * lax.rsqrt(var + LN_EPS) * g + b


def _qk_conv(halo, x, w0, w1, w2, w3, b):
    return _silu(causal_conv(halo, x, (w0, w1, w2, w3), b))


def _grp(i, h=None):
    if h is None:
        return pl.ds(i * D_GRP, D_GRP)
    return pl.ds(i * D_GRP + h * DK, DK)


def _mixer_specs(n_chunks, reverse):
    def chunk(c):
        return n_chunks - 1 - c if reverse else c
    row8 = CHUNK // SUBLANES
    proj_spec = pl.BlockSpec((CHUNK, D_IN_PAD), lambda c: (chunk(c), 0))
    halo_spec = pl.BlockSpec((SUBLANES, 2 * D_GRP), lambda c: (jnp.maximum(chunk(c) * row8 - 1, 0), 2))
    small = [pl.BlockSpec((2, D_GRP), lambda c: (0, 0)), pl.BlockSpec((1, D_GRP), lambda c: (0, 0)),
             pl.BlockSpec((ML_CONV, 2 * D_GRP), lambda c: (0, 0)), pl.BlockSpec((1, 2 * D_GRP), lambda c: (0, 0)),
             pl.BlockSpec((1, D_GRP), lambda c: (0, 0))]
    state_specs = [pl.BlockSpec((1, HEADS, DK, DK), lambda c: (chunk(c), 0, 0, 0)),
                   pl.BlockSpec((1, HEADS, DK, DK), lambda c: (chunk(c), 0, 0, 0)),
                   pl.BlockSpec((1, HEADS, 1, DK), lambda c: (chunk(c), 0, 0, 0)),
                   pl.BlockSpec((1, HEADS, 1, DK), lambda c: (chunk(c), 0, 0, 0))]
    y_spec = pl.BlockSpec((CHUNK, 2 * D_GRP), lambda c: (chunk(c), 0))
    return proj_spec, halo_spec, small, state_specs, y_spec, chunk


def _heads(x):
    return [x[:, h * DK:(h + 1) * DK] for h in range(HEADS)]


def _last(x, j):
    lane = lax.broadcasted_iota(jnp.int32, (1, x.shape[-1]), 1)
    return jnp.sum(jnp.where(lane == j, x, 0.0), axis=-1, keepdims=True)


def _hg_chunk(st_t, hq, hf, hi, hgate, l0, l1, nw):
    n = hq.shape[0]
    lb = _sigmoid(l0 - l1)
    q = _silu(hq)
    lf = jnp.log(lb + (1.0 - lb) * _sigmoid(hf))
    k = (1.0 - lb) * _sigmoid(-hf)
    b = cumsum_rows(lf)
    b_ref = _pick_row(b, n // 2 - 1)
    b_last = _pick_row(b, n - 1)
    qa, ka =_heads(q * jnp.exp(b - b_ref)), _heads(k * jnp.exp(b_ref - b))
    qe, kd, eb, v = _heads(q * jnp.exp(b)), _heads(k * jnp.exp(b_last - b)), _heads(jnp.exp(b_last)), _heads(hi)
    tri = _tri(n, True) > 0
    attn = [jnp.where(tri, mm_nt(qa[h], ka[h]), 0.0) for h in range(HEADS)]
    o = [mm_nn(attn[h], v[h]) + mm_nt(qe[h], st_t[h]) for h in range(HEADS)]
    st_new = jnp.stack([eb[h] * st_t[h] + mm_tn(v[h], kd[h]) for h in range(HEADS)])
    yn = [o[h] * lax.rsqrt(jnp.mean(o[h] * o[h], axis=-1, keepdims=True) + LN_EPS) for h in range(HEADS)]
    return st_new, jnp.concatenate(yn, axis=1) * nw * _silu(hgate)


def _ml_chunk(c_st, n_st, m_st, q, k, v, gates, og, nw):
    n = q.shape[0]
    ig = jnp.stack([_last(gates, h) for h in range(HEADS)])
    log_f = _log_sigmoid(gates)
    fl = jnp.stack([_last(log_f, HEADS + h) for h in range(HEADS)])
    bw = cumsum_rows(jnp.concatenate([jnp.broadcast_to(fl[h], (n, DK)) for h in range(HEADS)], axis=1))
    b = jnp.stack([_last(x, 0) for x in _heads(bw)])
    g = jnp.sum(fl, axis=1, keepdims=True)
    eye = lax.broadcasted_iota(jnp.int32, (n, n), 0) == lax.broadcasted_iota(jnp.int32, (n, n), 1)
    e_row = jnp.sum(jnp.where(eye, ig - b, 0.0), axis=1, keepdims=True)
    d = jnp.where(_tri(n, True) > 0, b + e_row, -jnp.inf)
    inter = b + m_st
    m_t = jnp.maximum(inter, jnp.max(d, axis=2, keepdims=True))
    qs, kh, vh = _heads(q * (DK ** -0.5)), _heads(k), _heads(v)
    s = jnp.stack([mm_nt(qs[h], kh[h]) for h in range(HEADS)]) * jnp.exp(d - m_t)
    w_inter = jnp.exp(inter - m_t)
    num = (jnp.stack([mm_nn(s[h], vh[h]) for h in range(HEADS)])
           + w_inter * jnp.stack([mm_nn(qs[h], c_st[h]) for h in range(HEADS)]))
    den = jnp.sum(s, axis=2, keepdims=True) + w_inter * jnp.sum(jnp.stack(qs) * n_st, axis=2, keepdims=True)
    h_out = num / jnp.maximum(jnp.abs(den), jnp.exp(-m_t))
    a = g - b + ig
    m_new = jnp.maximum(g + m_st, jnp.max(a, axis=1, keepdims=True))
    decay = jnp.exp(g + m_st - m_new)
    wk = jnp.stack(kh) * jnp.exp(a - m_new)
    c_new = decay * c_st + jnp.stack([mm_tn(wk[h], vh[h]) for h in range(HEADS)])
    n_new = decay * n_st + jnp.sum(wk, axis=1, keepdims=True)
    hc = h_out - jnp.mean(h_out, axis=-1, keepdims=True)
    yn = hc * lax.rsqrt(jnp.mean(hc * hc, axis=-1, keepdims=True) + LN_EPS)
    y = _sigmoid(og) * (jnp.concatenate([yn[h] for h in range(HEADS)], axis=1) * nw)
    return c_new, n_new, m_new, y


def _mixer_inputs(proj_ref, lg_ref, hnw_ref, mnw_ref, qk):
    hg_in = (proj_ref[:, _grp(0)], proj_ref[:, _grp(1)], proj_ref[:, _grp(2)], proj_ref[:, _grp(3)],
             lg_ref[0:1, :], lg_ref[1:2, :], hnw_ref[...])
    ml_in = (qk[:, :D_GRP], qk[:, D_GRP:], proj_ref[:, _grp(6)], proj_ref[:, pl.ds(8 * D_GRP, LANES)],
             proj_ref[:, _grp(7)], mnw_ref[...])
    return hg_in, ml_in


def _mixer_fwd(proj, lb_logits, hg_nw, conv_w, conv_b, ml_nw):
    seq = proj.shape[0]
    n_chunks = seq // CHUNK
    proj_spec, halo_spec, small, state_specs, y_spec, _ = _mixer_specs(n_chunks, False)

    def body(proj_ref, halo_ref, lg_ref, hnw_ref, cw_ref, cb_ref, mnw_ref,
             y_ref, hst_ref, cst_ref, nst_ref, mst_ref, hs, cs, ns, ms):
        c = pl.program_id(0)

        @pl.when(c == 0)
        def _():
            hs[...] = jnp.zeros_like(hs)
            cs[...] = jnp.zeros_like(cs)
            ns[...] = jnp.zeros_like(ns)
            ms[...] = jnp.full(ms.shape, NEG_BIG, F32)

        hst_ref[0] = hs[...]
        cst_ref[0] = cs[...]
        nst_ref[0] = ns[...]
        mst_ref[0] = ms[...]
        halo = jnp.where(c > 0, halo_ref[...], 0.0)
        qk = _qk_conv(halo, proj_ref[:, pl.ds(4 * D_GRP, 2 * D_GRP)],
                      cw_ref[0:1, :], cw_ref[1:2, :], cw_ref[2:3, :], cw_ref[3:4, :], cb_ref[...])
        hg_in, ml_in = _mixer_inputs(proj_ref, lg_ref, hnw_ref, mnw_ref, qk)
        hs[...], y_hg = _hg_chunk(hs[...], *hg_in)
        cs[...], ns[...], m_new, y_ml = _ml_chunk(cs[...], ns[...], _last(ms[...], 0), *ml_in)
        ms[...] = jnp.broadcast_to(m_new, ms.shape)
        y_ref[:, pl.ds(0, D_GRP)] = y_hg.astype(BF16)
        y_ref[:, pl.ds(D_GRP, D_GRP)] = y_ml.astype(BF16)

    st = jax.ShapeDtypeStruct((n_chunks, HEADS, DK, DK), F32)
    vec = jax.ShapeDtypeStruct((n_chunks, HEADS, 1, DK), F32)
    vmem = 2 * (_nbytes((CHUNK, D_IN_PAD), F32) + _nbytes((CHUNK, 2 * D_GRP), F32) + 2 * _nbytes((HEADS, DK, DK), F32)) \
        + 2 * _nbytes((HEADS, DK, DK), F32)
    return _pcall(
        body, name="mixer_fwd", grid=(n_chunks,),
        in_specs=[proj_spec, halo_spec] + small,
        out_specs=[y_spec] + state_specs,
        out_shape=[jax.ShapeDtypeStruct((seq, 2 * D_GRP), BF16), st, st, vec, vec],
        scratch_shapes=[pltpu.VMEM((HEADS, DK, DK), F32), pltpu.VMEM((HEADS, DK, DK), F32),
                        pltpu.VMEM((HEADS, 1, DK), F32), pltpu.VMEM((HEADS, 1, DK), F32)],
        compiler_params=_params(("arbitrary",), vmem),
    )(proj, proj, lb_logits, hg_nw, conv_w, conv_b, ml_nw)


def _mixer_bwd(proj, dy, hst, cst, nst, mst, lb_logits, hg_nw, conv_w, conv_b, ml_nw):
    seq = proj.shape[0]
    n_chunks = seq // CHUNK
    proj_spec, halo_spec, small, state_specs, y_spec, _ = _mixer_specs(n_chunks, True)

    def body(proj_ref, halo_ref, dy_ref, hst_ref, cst_ref, nst_ref, mst_ref,
             lg_ref, hnw_ref, cw_ref, cb_ref, mnw_ref,
             dproj_ref, dbin_ref, dlg_ref, dhnw_ref, dcw_ref, dcb_ref, dmnw_ref,
             dhs, dcs, dns, dms, dhalo):
        c = pl.program_id(0)

        @pl.when(c == 0)
        def _():
            for r in (dhs, dcs, dns, dms, dhalo, dbin_ref, dlg_ref, dhnw_ref, dcw_ref, dcb_ref, dmnw_ref):
                r[...] = jnp.zeros_like(r)

        def put(cols, val):
            dproj_ref[:, cols] = val.astype(BF16)
            dbin_ref[:, cols] += jnp.sum(val, axis=0, keepdims=True)

        first = c == n_chunks - 1
        halo = jnp.where(first, 0.0, halo_ref[...])
        x_qk = proj_ref[:, pl.ds(4 * D_GRP, 2 * D_GRP)]
        conv_args = (halo, x_qk, cw_ref[0:1, :], cw_ref[1:2, :], cw_ref[2:3, :], cw_ref[3:4, :], cb_ref[...])
        qk, conv_vjp = jax.vjp(_qk_conv, *conv_args)
        hg_in, ml_in = _mixer_inputs(proj_ref, lg_ref, hnw_ref, mnw_ref, qk)
        _, hg_vjp = jax.vjp(_hg_chunk, hst_ref[0], *hg_in)
        _, ml_vjp = jax.vjp(_ml_chunk, cst_ref[0], nst_ref[0], _last(mst_ref[0], 0), *ml_in)
        dst, dhq, dhf, dhi, dhg, dl0, dl1, dnw = hg_vjp((dhs[...], dy_ref[:, pl.ds(0, D_GRP)]))
        dc, dn, dm, dq, dk, dv, dgates, dog, dmn = ml_vjp(
            (dcs[...], dns[...], _last(dms[...], 0), dy_ref[:, pl.ds(D_GRP, D_GRP)]))
        dhs[...] = dst
        dcs[...] = dc
        dns[...] = dn
        dms[...] = jnp.broadcast_to(dm, dms.shape)
        for i, val in ((0, dhq), (1, dhf), (2, dhi), (3, dhg), (6, dv), (7, dog)):
            put(_grp(i), val)
        put(pl.ds(8 * D_GRP, LANES), dgates)
        dlg_ref[0:1, :] += dl0
        dlg_ref[1:2, :] += dl1
        dhnw_ref[...] += dnw
        dmnw_ref[...] += dmn
        dh, dx, dw0, dw1, dw2, dw3, db = conv_vjp(jnp.concatenate([dq, dk], axis=1))
        tail = jnp.concatenate([jnp.zeros((CHUNK - SUBLANES, 2 * D_GRP), F32), dhalo[...]], axis=0)
        put(pl.ds(4 * D_GRP, 2 * D_GRP), dx + tail)
        dhalo[...] = dh
        for d, dw in enumerate((dw0, dw1, dw2, dw3)):
            dcw_ref[d:d + 1, :] += dw
        dcb_ref[...] += db

    row = pl.BlockSpec((1, D_GRP), lambda c: (0, 0))
    small_out = [pl.BlockSpec((1, D_IN_PAD), lambda c: (0, 0)), pl.BlockSpec((2, D_GRP), lambda c: (0, 0)), row,
                 pl.BlockSpec((ML_CONV, 2 * D_GRP), lambda c: (0, 0)), pl.BlockSpec((1, 2 * D_GRP), lambda c: (0, 0)), row]
    dy_spec = pl.BlockSpec((CHUNK, 2 * D_GRP), y_spec.index_map)
    vmem = 2 * (2 * _nbytes((CHUNK, D_IN_PAD), F32) + _nbytes((CHUNK, 2 * D_GRP), F32)
                + 2 * _nbytes((HEADS, DK, DK), F32)) + 2 * _nbytes((HEADS, DK, DK), F32) + 4 * 1024 * 1024
    return _pcall(
        body, name="mixer_bwd", grid=(n_chunks,),
        in_specs=[proj_spec, halo_spec, dy_spec] + state_specs + small,
        out_specs=[proj_spec] + small_out,
        out_shape=[jax.ShapeDtypeStruct((seq, D_IN_PAD), BF16), jax.ShapeDtypeStruct((1, D_IN_PAD), F32),
                   jax.ShapeDtypeStruct((2, D_GRP), F32), jax.ShapeDtypeStruct((1, D_GRP), F32),
                   jax.ShapeDtypeStruct((ML_CONV, 2 * D_GRP), F32), jax.ShapeDtypeStruct((1, 2 * D_GRP), F32),
                   jax.ShapeDtypeStruct((1, D_GRP), F32)],
        scratch_shapes=[pltpu.VMEM((HEADS, DK, DK), F32), pltpu.VMEM((HEADS, DK, DK), F32),
                        pltpu.VMEM((HEADS, 1, DK), F32), pltpu.VMEM((HEADS, 1, DK), F32),
                        pltpu.VMEM((SUBLANES, 2 * D_GRP), F32)],
        compiler_params=_params(("arbitrary",), vmem),
    )(proj, proj, dy, hst, cst, nst, mst, lb_logits, hg_nw, conv_w, conv_b, ml_nw)


def _tile(n, prefs, unit=None):
    unit = unit or n
    for p in prefs:
        if unit % p == 0 and n % p == 0:
            return p
    return unit


def _logical(arr):
    return arr.shape if arr.ndim == 2 else (arr.shape[1], arr.shape[0] * arr.shape[2])


def _group(arr):
    return arr.shape[-1]


def _split_spec(ndim, group, tr, tc, where):
    if ndim == 2:
        return pl.BlockSpec((tr, tc), where)
    per = group // tc
    assert per * tc == group, (group, tc)

    def index(*ids):
        bi, bj = where(*ids)
        return (bj // per, bi, bj % per)
    return pl.BlockSpec((None, tr, tc), index)


def _mm(name, mode, a, b, *, bias=None, res=None, res_scale=1.0, ln=None, out_dtype=F32, out_groups=None,
        copy_dtype=None, tm=None, tn=None, tk=None):
    la, lb = _logical(a), _logical(b)
    if mode == "nn":
        (m, k), n = la, lb[1]
        n_unit = _group(b) if b.ndim == 3 else n
        kc = _group(a) if a.ndim == 3 else k
    elif mode == "nt":
        (m, k), n = la, lb[0]
        n_unit = n
        kc = min(_group(a) if a.ndim == 3 else k, _group(b) if b.ndim == 3 else k)
    else:
        (k, m), n = la, lb[1]
        n_unit, kc = (_group(b) if b.ndim == 3 else n), k
        assert a.ndim == 2
    if out_groups:
        n_unit = min(n_unit, n // out_groups)
    kind = ln[0] if ln else None
    tm = tm or (256 if ln else _tile(m, (512, 256, 128)))
    tn = n if ln else (tn or _tile(n, (512, 384, 256, 128), n_unit))
    tk = (tk or _tile(k, (2048, 512, 256, 128))) if mode == "tn" else k
    gi, gj, gk = m // tm, n // tn, k // tk
    assert gi * tm == m and gj * tn == n and gk * tk == k and n_unit % tn == 0, (name, m, n, k, tm, tn, tk)
    ca, cb = {"nn": (1, 0), "nt": (1, 1), "tn": (0, 0)}[mode]
    i_outer = gk > 1 or (gi - 1) * _nbytes(b.shape, b.dtype) <= (gj - 1) * _nbytes(a.shape, a.dtype)

    def ij(where):
        return (lambda p, q, kk: where(p, q, kk)) if i_outer else (lambda p, q, kk: where(q, p, kk))
    if mode == "tn":
        a_spec = pl.BlockSpec((tk, tm), ij(lambda i, j, kk: (kk, i)))
    elif a.ndim == 3:
        a_spec = pl.BlockSpec((a.shape[0], tm, _group(a)), ij(lambda i, j, kk: (0, i, 0)))
    else:
        a_spec = pl.BlockSpec((tm, k), ij(lambda i, j, kk: (i, 0)))
    if mode != "nt":
        b_spec = _split_spec(b.ndim, _group(b), tk, tn, ij(lambda i, j, kk: (kk, j)))
    elif b.ndim == 3:
        b_spec = pl.BlockSpec((b.shape[0], tn, _group(b)), ij(lambda i, j, kk: (0, j, 0)))
    else:
        b_spec = pl.BlockSpec((tn, k), ij(lambda i, j, kk: (j, 0)))
    row_spec = pl.BlockSpec((1, tn), ij(lambda i, j, kk: (0, j)))
    blk_spec = pl.BlockSpec((tm, tn), ij(lambda i, j, kk: (i, j)))
    ins, in_specs = [a, b], [a_spec, b_spec]
    if bias is not None:
        ins.append(bias), in_specs.append(row_spec)
    if res is not None:
        ins.append(res), in_specs.append(blk_spec)
    if kind == "fwd":
        ins += [ln[1], ln[2]]
        in_specs += [row_spec, row_spec]
    elif kind == "loss":
        ins += [ln[1], ln[2], ln[3]]
        in_specs += [row_spec, row_spec, blk_spec]
    elif kind == "bwd":
        ins += [ln[1], ln[2], ln[3]]
        in_specs += [blk_spec, row_spec, row_spec]
    if out_groups:
        blk_out = jax.ShapeDtypeStruct((out_groups, m, n // out_groups), out_dtype)
        out_spec = _split_spec(3, n // out_groups, tm, tn, ij(lambda i, j, kk: (i, j)))
    else:
        blk_out, out_spec = jax.ShapeDtypeStruct((m, n), out_dtype), blk_spec
    row_out = jax.ShapeDtypeStruct((1, n), F32)
    if kind is None:
        out_shape, out_specs = [blk_out], [out_spec]
    elif kind == "fwd":
        out_shape, out_specs = [blk_out, blk_out], [blk_spec, blk_spec]
    else:
        out_shape, out_specs = [blk_out, row_out, row_out], [blk_spec, row_spec, row_spec]
        if kind == "loss":
            out_shape.append(jax.ShapeDtypeStruct((1, LANES), F32))
            out_specs.append(pl.BlockSpec((1, LANES), lambda p, q, kk: (0, 0)))
    if copy_dtype is not None:
        out_shape.append(jax.ShapeDtypeStruct((m, n), copy_dtype))
        out_specs.append(blk_spec)
    n_in = len(ins)

    def body(*refs):
        in_refs, out_refs, acc_ref = refs[:n_in], refs[n_in:n_in + len(out_shape)], refs[-1]
        i, kk = pl.program_id(0 if i_outer else 1), pl.program_id(2)
        a_ref, b_ref = in_refs[:2]
        extra = list(in_refs[2:])

        def epilogue(acc, rows=slice(None)):
            rest = list(extra)
            if bias is not None:
                acc = acc + rest.pop(0)[...]
            if res is not None:
                acc = acc + res_scale * rest.pop(0)[rows, :]
            if kind is None:
                out_refs[0][...] = acc.astype(out_dtype)
                return
            if kind == "fwd":
                out_refs[0][rows, :] = acc
                y = _layer_norm(acc, rest[0][...], rest[1][...])
                out_refs[1][rows, :] = y
                if copy_dtype is not None:
                    out_refs[-1][rows, :] = y.astype(copy_dtype)
                return
            if kind == "loss":
                y, vjp = jax.vjp(_layer_norm, acc, rest[0][...], rest[1][...])
                err = y - rest[2][rows, :]
                part = 0.5 * jnp.sum(jnp.sum(err * err, axis=1, keepdims=True), axis=0, keepdims=True) / n
                dz, dg, db = vjp(err / n)
            else:
                _, vjp = jax.vjp(_layer_norm, rest[0][rows, :], rest[1][...], rest[2][...])
                dz, dg, db = vjp(acc)
            out_refs[0][rows, :] = dz
            out_refs[1][...] += dg
            out_refs[2][...] += db
            if kind == "loss":
                out_refs[3][...] += jnp.broadcast_to(part, (1, LANES))
            if copy_dtype is not None:
                out_refs[-1][rows, :] = dz.astype(copy_dtype)

        if kind in ("loss", "bwd"):
            @pl.when((i == 0) & (kk == 0))
            def _():
                for r in out_refs[1:3 + (kind == "loss")]:
                    r[...] = jnp.zeros_like(r)

        def chunk(ref, c0, last):
            if ref.ndim == 3:
                g = ref.shape[2]
                return ref[c0 // g, :, pl.ds(c0 % g, kc)]
            return ref[:, pl.ds(c0, kc)] if last else ref[pl.ds(c0, kc), :]

        if mode == "tn" or kc == k:
            prod = _dg(a_ref[...], b_ref[...], ca, cb)
        else:
            prod = None
            for c0 in range(0, k, kc):
                part = _dg(chunk(a_ref, c0, True), chunk(b_ref, c0, mode == "nt"), ca, cb)
                prod = part if prod is None else prod + part
        if gk == 1:
            epilogue(prod)
            return

        @pl.when(kk == 0)
        def _():
            acc_ref[...] = prod

        @pl.when(kk > 0)
        def _():
            acc_ref[...] += prod

        @pl.when(kk == gk - 1)
        def _():
            epilogue(acc_ref[...])

    vmem = (2 * (_nbytes((tm, tk), a.dtype) + _nbytes((tk, tn), b.dtype))
            + (2 * len(ins) + 2 * len(out_shape) + 1) * _nbytes((tm, tn), F32))
    outs = _pcall(
        body, name=name, grid=(gi, gj, gk) if i_outer else (gj, gi, gk), in_specs=in_specs, out_specs=out_specs,
        out_shape=out_shape, scratch_shapes=[pltpu.VMEM((tm, tn) if gk > 1 else (SUBLANES, LANES), F32)],
        compiler_params=_params(("arbitrary", "arbitrary", "arbitrary"), vmem),
    )(*ins)
    return outs[0] if (kind is None and copy_dtype is None) else outs


def _attn_head(q, k, v):
    sc = mm_nt(q, k) * (CA_DH ** -0.5)
    e = jnp.exp(sc - jnp.max(sc, axis=-1, keepdims=True))
    return mm_nn(e / jnp.sum(e, axis=-1, keepdims=True), v)


def _attn_fwd(q, kv):
    seq, n_mem = q.shape[0], kv.shape[0]
    tq = _tile(seq, (512, 256, 128))

    def body(q_ref, kv_ref, o_ref):
        for h in range(HEADS):
            hd = pl.ds(h * CA_DH, CA_DH)
            o = _attn_head(q_ref[:, hd], kv_ref[:, hd], kv_ref[:, pl.ds(D_MODEL + h * CA_DH, CA_DH)])
            o_ref[:, hd] = o.astype(BF16)

    return _pcall(
        body, name="attn_fwd", grid=(seq // tq,),
        in_specs=[pl.BlockSpec((tq, D_MODEL), lambda i: (i, 0)), pl.BlockSpec((n_mem, 2 * D_MODEL), lambda i: (0, 0))],
        out_specs=pl.BlockSpec((tq, D_MODEL), lambda i: (i, 0)), out_shape=jax.ShapeDtypeStruct((seq, D_MODEL), BF16),
        compiler_params=_params(("arbitrary",), 4 * _nbytes((tq, D_MODEL), F32) + 2 * _nbytes((n_mem, 2 * D_MODEL), F32)),
    )(q, kv)


def _attn_bwd(q, kv, do):
    seq, n_mem = q.shape[0], kv.shape[0]
    tq = _tile(seq, (512, 256, 128))

    def body(q_ref, kv_ref, do_ref, dq_ref, dkv_ref):
        @pl.when(pl.program_id(0) == 0)
        def _():
            dkv_ref[...] = jnp.zeros_like(dkv_ref)

        for h in range(HEADS):
            hd = pl.ds(h * CA_DH, CA_DH)
            vd = pl.ds(D_MODEL + h * CA_DH, CA_DH)
            _, vjp = jax.vjp(_attn_head, q_ref[:, hd], kv_ref[:, hd], kv_ref[:, vd])
            dq, dk, dv = vjp(do_ref[:, hd].astype(F32))
            dq_ref[:, hd] = dq.astype(BF16)
            dkv_ref[:, hd] += dk
            dkv_ref[:, vd] += dv

    return _pcall(
        body, name="attn_bwd", grid=(seq // tq,),
        in_specs=[pl.BlockSpec((tq, D_MODEL), lambda i: (i, 0)), pl.BlockSpec((n_mem, 2 * D_MODEL), lambda i: (0, 0)),
                  pl.BlockSpec((tq, D_MODEL), lambda i: (i, 0))],
        out_specs=[pl.BlockSpec((tq, D_MODEL), lambda i: (i, 0)), pl.BlockSpec((n_mem, 2 * D_MODEL), lambda i: (0, 0))],
        out_shape=[jax.ShapeDtypeStruct((seq, D_MODEL), BF16), jax.ShapeDtypeStruct((n_mem, 2 * D_MODEL), F32)],
        compiler_params=_params(("arbitrary",), 6 * _nbytes((tq, D_MODEL), F32) + 4 * _nbytes((n_mem, 2 * D_MODEL), F32)),
    )(q, kv, do)


def _ffn_mid(hg, xg, hv, xv, wg0, wg1, wg2, bg, wv0, wv1, wv2, bv):
    return jax.nn.gelu(causal_conv(hg, xg, (wg0, wg1, wg2), bg)) * causal_conv(hv, xv, (wv0, wv1, wv2), bv)


FFN_TB = 256
FFN_W = D_FF // 2
FFN_J = D_FF // FFN_W
MXU_COLS = 256
FFN_PIECES = tuple((off, min(MXU_COLS, FFN_W - off)) for off in range(0, FFN_W, MXU_COLS))


def _ffn_common_specs(seq, row):
    tb = min(FFN_TB, seq)
    full = pl.BlockSpec((tb, D_MODEL), lambda t, j: (row(t), 0))
    vec = pl.BlockSpec((1, D_MODEL), lambda t, j: (0, 0))
    halves = []
    for off in (0, FFN_J):
        halves.append(dict(
            w_up=pl.BlockSpec((D_MODEL, FFN_W), lambda t, j, off=off: (0, j + off)),
            taps=pl.BlockSpec((FFN_CONV, FFN_W), lambda t, j, off=off: (0, j + off)),
            bias=pl.BlockSpec((1, FFN_W), lambda t, j, off=off: (0, j + off))))
    w_down = pl.BlockSpec((FFN_W, D_MODEL), lambda t, j: (j, 0))
    u_blk = pl.BlockSpec((2, tb, FFN_W), lambda t, j: (0, row(t), j))
    return tb, full, vec, halves, w_down, u_blk


def _ffn_vmem(tb):
    return (_nbytes((2, tb, FFN_W), F32) + _nbytes((2, tb, FFN_W), BF16) + 3 * _nbytes((D_MODEL, FFN_W), BF16)
            + 10 * _nbytes((tb, D_MODEL), F32))


def _conv_params(taps_ref, bias_ref, cols):
    return taps_ref[0:1, cols], taps_ref[1:2, cols], taps_ref[2:3, cols], bias_ref[:, cols]


def _ffn_fwd(x2b, x2, w_up, conv_w, conv_b, w_down, ln_g, ln_b, target):
    seq = x2.shape[0]
    tb, full, vec, halves, wd_spec, u_blk = _ffn_common_specs(seq, lambda t: t)
    nt = seq // tb

    def body(xb_ref, wg_ref, wv_ref, tg_ref, tv_ref, bg_ref, bv_ref, wd_ref, x_ref, g_ref, b_ref, tgt_ref,
             u_ref, h_ref, dz_ref, dg_ref, db_ref, loss_ref, dzb_ref, acc, carry):
        t, j = pl.program_id(0), pl.program_id(1)
        xb = xb_ref[...]
        pieces = [pl.ds(off, width) for off, width in FFN_PIECES]
        ug = [_dg(xb, wg_ref[:, cols], 1, 0) for cols in pieces]
        uv = [_dg(xb, wv_ref[:, cols], 1, 0) for cols in pieces]
        hs = []
        for cols, g, v in zip(pieces, ug, uv):
            u_ref[0, :, cols] = g
            u_ref[1, :, cols] = v
            halo_g = jnp.where(t == 0, 0.0, carry[j, 0, :, cols])
            halo_v = jnp.where(t == 0, 0.0, carry[j, 1, :, cols])
            h = _ffn_mid(halo_g, g, halo_v, v, *_conv_params(tg_ref, bg_ref, cols),
                         *_conv_params(tv_ref, bv_ref, cols)).astype(BF16)
            carry[j, 0, :, cols] = g[tb - SUBLANES:, :]
            carry[j, 1, :, cols] = v[tb - SUBLANES:, :]
            h_ref[:, cols] = h
            hs.append(h)
        part = None
        for cols, h in zip(pieces, hs):
            p = _dg(h, wd_ref[cols, :], 1, 0)
            part = p if part is None else part + p

        @pl.when(j == 0)
        def _():
            acc[...] = part

        @pl.when(j > 0)
        def _():
            acc[...] += part

        @pl.when(j == FFN_J - 1)
        def _():
            y, vjp = jax.vjp(_layer_norm, acc[...] + ALPHA * x_ref[...], g_ref[...], b_ref[...])
            err = y - tgt_ref[...]
            part_loss = 0.5 * jnp.sum(jnp.sum(err * err, axis=1, keepdims=True), axis=0, keepdims=True) / D_MODEL
            dz, dg, db = vjp(err / D_MODEL)

            @pl.when(t == 0)
            def _():
                for r in (dg_ref, db_ref, loss_ref):
                    r[...] = jnp.zeros_like(r)

            dz_ref[...] = dz
            dzb_ref[...] = dz.astype(BF16)
            dg_ref[...] += dg
            db_ref[...] += db
            loss_ref[...] += jnp.broadcast_to(part_loss, (1, LANES))

    h0, h1 = halves
    row = jax.ShapeDtypeStruct((1, D_MODEL), F32)
    return _pcall(
        body, name="ffn_fwd", grid=(nt, FFN_J),
        in_specs=[full, h0["w_up"], h1["w_up"], h0["taps"], h1["taps"], h0["bias"], h1["bias"], wd_spec, full, vec, vec,
                  full],
        out_specs=[u_blk, pl.BlockSpec((tb, FFN_W), lambda t, j: (t, j)), full, vec, vec,
                   pl.BlockSpec((1, LANES), lambda t, j: (0, 0)), full],
        out_shape=[jax.ShapeDtypeStruct((2, seq, D_FF), F32), jax.ShapeDtypeStruct((seq, D_FF), BF16),
                   jax.ShapeDtypeStruct((seq, D_MODEL), F32), row, row, jax.ShapeDtypeStruct((1, LANES), F32),
                   jax.ShapeDtypeStruct((seq, D_MODEL), BF16)],
        scratch_shapes=[pltpu.VMEM((tb, D_MODEL), F32), pltpu.VMEM((FFN_J, 2, SUBLANES, FFN_W), F32)],
        compiler_params=_params(("arbitrary", "arbitrary"), _ffn_vmem(tb)),
    )(x2b, w_up, w_up, conv_w, conv_w, conv_b, conv_b, w_down, x2, ln_g, ln_b, target)


def _ffn_bwd(u, conv_w, conv_b, dz3b, dz3, w_down, w_up, z2, ln_g, ln_b):
    seq = dz3.shape[0]
    tb = min(FFN_TB, seq)
    nt = seq // tb
    row8 = tb // SUBLANES
    tb, full, vec, halves, wd_spec, u_blk = _ffn_common_specs(seq, lambda t: nt - 1 - t)
    halo = pl.BlockSpec((2, SUBLANES, FFN_W), lambda t, j: (0, jnp.maximum((nt - 1 - t) * row8 - 1, 0), j))

    def body(u_ref, halo_ref, tg_ref, tv_ref, bg_ref, bv_ref, dzb_ref, wd_ref, wg_ref, wv_ref, dz3_ref, z_ref, g_ref,
             b_ref, du_ref, dw_ref, dbias_ref, dz_ref, dg_ref, db_ref, dz2b_ref, acc, carry):
        t, j = pl.program_id(0), pl.program_id(1)

        @pl.when((t == 0) & (j == 0))
        def _():
            for r in (dw_ref, dbias_ref, dg_ref, db_ref):
                r[...] = jnp.zeros_like(r)

        pieces = [pl.ds(off, width) for off, width in FFN_PIECES]
        dzb = dzb_ref[...]
        dhs = [_dg(dzb, wd_ref[cols, :], 1, 1) for cols in pieces]
        first = t == nt - 1
        dus = []
        for cols, dh in zip(pieces, dhs):
            args = (jnp.where(first, 0.0, halo_ref[0, :, cols]), u_ref[0, :, cols],
                    jnp.where(first, 0.0, halo_ref[1, :, cols]), u_ref[1, :, cols],
                    *_conv_params(tg_ref, bg_ref, cols), *_conv_params(tv_ref, bv_ref, cols))
            _, vjp = jax.vjp(_ffn_mid, *args)
            dhg, dxg, dhv, dxv, g0, g1, g2, gb, v0, v1, v2, vb = vjp(dh)
            zeros = jnp.zeros((tb - SUBLANES, dh.shape[1]), F32)
            dug = (dxg + jnp.concatenate([zeros, jnp.where(t == 0, 0.0, carry[j, 0, :, cols])], axis=0)).astype(BF16)
            duv = (dxv + jnp.concatenate([zeros, jnp.where(t == 0, 0.0, carry[j, 1, :, cols])], axis=0)).astype(BF16)
            carry[j, 0, :, cols] = dhg
            carry[j, 1, :, cols] = dhv
            du_ref[0, :, cols] = dug
            du_ref[1, :, cols] = duv
            for half, parts in enumerate(((g0, g1, g2), (v0, v1, v2))):
                for d, p in enumerate(parts):
                    dw_ref[j, half, d:d + 1, cols] += p
            dbias_ref[j, 0, :, cols] += gb
            dbias_ref[j, 1, :, cols] += vb
            dus.append((dug, duv))
        part = None
        for cols, (dug, duv) in zip(pieces, dus):
            p = _dg(dug, wg_ref[:, cols], 1, 1) + _dg(duv, wv_ref[:, cols], 1, 1)
            part = p if part is None else part + p

        @pl.when(j == 0)
        def _():
            acc[...] = part

        @pl.when(j > 0)
        def _():
            acc[...] += part

        @pl.when(j == FFN_J - 1)
        def _():
            _, ln_vjp = jax.vjp(_layer_norm, z_ref[...], g_ref[...], b_ref[...])
            dz, dg, db = ln_vjp(acc[...] + ALPHA * dz3_ref[...])
            dz_ref[...] = dz
            dz2b_ref[...] = dz.astype(BF16)
            dg_ref[...] += dg
            db_ref[...] += db

    h0, h1 = halves
    row = jax.ShapeDtypeStruct((1, D_MODEL), F32)
    whole = lambda *shape: pl.BlockSpec(shape, lambda t, j: (0,) * len(shape))
    return _pcall(
        body, name="ffn_bwd", grid=(nt, FFN_J),
        in_specs=[u_blk, halo, h0["taps"], h1["taps"], h0["bias"], h1["bias"], full, wd_spec, h0["w_up"], h1["w_up"],
                  full, full, vec, vec],
        out_specs=[u_blk, whole(FFN_J, 2, FFN_CONV, FFN_W), whole(FFN_J, 2, 1, FFN_W), full, vec, vec, full],
        out_shape=[jax.ShapeDtypeStruct((2, seq, D_FF), BF16), jax.ShapeDtypeStruct((FFN_J, 2, FFN_CONV, FFN_W), F32),
                   jax.ShapeDtypeStruct((FFN_J, 2, 1, FFN_W), F32), jax.ShapeDtypeStruct((seq, D_MODEL), F32), row, row,
                   jax.ShapeDtypeStruct((seq, D_MODEL), BF16)],
        scratch_shapes=[pltpu.VMEM((tb, D_MODEL), F32), pltpu.VMEM((FFN_J, 2, SUBLANES, FFN_W), F32)],
        compiler_params=_params(("arbitrary", "arbitrary"), _ffn_vmem(tb)),
    )(u, u, conv_w, conv_w, conv_b, conv_b, dz3b, w_down, w_up, w_up, dz3, z2, ln_g, ln_b)


def _adamw_math(w, g, m, v):
    m_new = ADAM_B1 * m + (1.0 - ADAM_B1) * g
    v_new = ADAM_B2 * v + (1.0 - ADAM_B2) * jnp.square(g)
    m_hat = m_new / (1.0 - ADAM_B1 ** ADAM_STEP)
    v_hat = v_new / (1.0 - ADAM_B2 ** ADAM_STEP)
    return -ADAM_LR * (m_hat / (jnp.sqrt(v_hat) + ADAM_EPS) + ADAM_WD * w), m_new, v_new


def _adamw(name, w, g, m, v):
    rows, cols = w.shape
    tr = _tile(rows, (256, 176, 128, 64, 40, 32, 16, 8))

    def body(w_ref, g_ref, m_ref, v_ref, d_ref, nm_ref, nv_ref):
        d_ref[...], nm_ref[...], nv_ref[...] = _adamw_math(w_ref[...], g_ref[...], m_ref[...], v_ref[...])

    spec = pl.BlockSpec((tr, cols), lambda i: (i, 0))
    sh = jax.ShapeDtypeStruct((rows, cols), F32)
    return _pcall(
        body, name=name, grid=(rows // tr,), in_specs=[spec] * 4, out_specs=[spec] * 3, out_shape=[sh] * 3,
        compiler_params=_params(("arbitrary",), 14 * _nbytes((tr, -(-cols // LANES) * LANES), F32)),
    )(w, g, m, v)


def _adamw_halves(name, core, w, mine, theirs, m, v):
    rows, cols = w.shape
    half_rows = mine.shape[0]
    tr = _tile(half_rows, (1056, 256, 176, 128))
    nbh = half_rows // tr
    assert 2 * half_rows >= rows and (rows // 2 == half_rows or cols == LANES)

    def body(c_ref, w_ref, a_ref, b_ref, m_ref, v_ref, g_ref, d_ref, nm_ref, nv_ref):
        g = jnp.where(pl.program_id(0) // nbh == c_ref[0], a_ref[...], b_ref[...])
        g_ref[...] = g
        d_ref[...], nm_ref[...], nv_ref[...] = _adamw_math(w_ref[...], g, m_ref[...], v_ref[...])

    spec = pl.BlockSpec((tr, cols), lambda i, c_ref: (i, 0))
    half = pl.BlockSpec((tr, cols), lambda i, c_ref: (i % nbh, 0))
    sh = jax.ShapeDtypeStruct((rows, cols), F32)
    grid_spec = pltpu.PrefetchScalarGridSpec(
        num_scalar_prefetch=1, grid=(-(-rows // tr),), in_specs=[spec, half, half, spec, spec], out_specs=[spec] * 4)
    return _pcall(
        body, name=name, grid_spec=grid_spec, out_shape=[sh] * 4,
        compiler_params=_params(("arbitrary",), 18 * _nbytes((tr, -(-cols // LANES) * LANES), F32)),
    )(core, w, mine, theirs, m, v)


MESH = pl.DeviceIdType.MESH
ANY = pl.BlockSpec(memory_space=pl.ANY)
N_CHIPS = 4
N_DEV = 8
BF16_ROWS = 16


def _me():
    return lax.axis_index("x"), lax.axis_index("y"), lax.axis_index("c")


def _other_chips(x, y):
    return [(1 - x, y), (x, 1 - y), (1 - x, 1 - y)]


def _remote(src, dst, ssem, rsem, dev):
    return pltpu.make_async_remote_copy(src_ref=src, dst_ref=dst, send_sem=ssem, recv_sem=rsem,
                                        device_id=dev, device_id_type=MESH)


def _half_rows(ref_rows, cc):
    half = ref_rows // 2
    return pl.ds(pl.multiple_of(cc * half, BF16_ROWS), half)


def _gather_weights(shards):
    n = len(shards)
    n_ici = n * (N_CHIPS - 1)

    def body(*refs):
        ins, outs, (ssem, rsem, lsem, lrsem) = refs[:n], refs[n:2 * n], refs[2 * n:]
        x, y, c = _me()
        k_me = 2 * x + y
        sib = (x, y, 1 - c)
        chips = _other_chips(x, y)
        started = []
        for i, (w_ref, o_ref) in enumerate(zip(ins, outs)):
            cp = _remote(w_ref, o_ref.at[k_me], lsem.at[i], lrsem.at[i], sib)
            cp.start()
            started.append(cp)
        for r, (px, py) in enumerate(chips):
            for i, (w_ref, o_ref) in enumerate(zip(ins, outs)):
                rows = _half_rows(w_ref.shape[0], c)
                s = r * n + i
                cp = _remote(w_ref.at[rows], o_ref.at[k_me, rows], ssem.at[s], rsem.at[s], (px, py, c))
                cp.start()
                started.append(cp)
        for r, (px, py) in enumerate(chips):
            for i, o_ref in enumerate(outs):
                blk = o_ref.at[2 * px + py, _half_rows(o_ref.shape[1], c)]
                s = r * n + i
                _remote(blk, blk, ssem.at[s], rsem.at[s], (px, py, c)).wait_recv()
                cp = _remote(blk, blk, ssem.at[n_ici + s], rsem.at[n_ici + s], sib)
                cp.start()
                started.append(cp)
        for r, (px, py) in enumerate(chips):
            for i, o_ref in enumerate(outs):
                blk = o_ref.at[2 * px + py, _half_rows(o_ref.shape[1], 1 - c)]
                s = n_ici + r * n + i
                _remote(blk, blk, ssem.at[s], rsem.at[s], sib).wait_recv()
        for cp in started[n:]:
            cp.wait_send()
        for cp in started[:n]:
            cp.wait()

    return _pcall(
        body, name="gather_weights", in_specs=[ANY] * n, out_specs=[ANY] * n,
        out_shape=[jax.ShapeDtypeStruct((N_CHIPS,) + s.shape, s.dtype) for s in shards],
        scratch_shapes=[pltpu.SemaphoreType.DMA((2 * n_ici,)), pltpu.SemaphoreType.DMA((2 * n_ici,)),
                        pltpu.SemaphoreType.DMA((n,)), pltpu.SemaphoreType.DMA((n,))],
    )(*shards)


def _swap_halves(name, grads):
    n = len(grads)

    def body(*refs):
        ins, outs, (ssem, rsem) = refs[:n], refs[n:2 * n], refs[2 * n:]
        x, y, c = _me()
        copies = []
        for i, (g_ref, o_ref) in enumerate(zip(ins, outs)):
            for k in range(N_CHIPS):
                s = i * N_CHIPS + k
                cp = _remote(g_ref.at[k, _half_rows(g_ref.shape[1], 1 - c)], o_ref.at[k], ssem.at[s], rsem.at[s],
                             (x, y, 1 - c))
                cp.start()
                copies.append(cp)
        for cp in copies:
            cp.wait()

    return _pcall(
        body, name=name, in_specs=[ANY] * n, out_specs=[ANY] * n,
        out_shape=[jax.ShapeDtypeStruct((N_CHIPS, g.shape[1] // 2, g.shape[2]), g.dtype) for g in grads],
        scratch_shapes=[pltpu.SemaphoreType.DMA((n * N_CHIPS,)), pltpu.SemaphoreType.DMA((n * N_CHIPS,))],
    )(*grads)


SEM = pl.BlockSpec(memory_space=pltpu.SEMAPHORE)
IN_HBM = pl.BlockSpec(memory_space=pltpu.HBM)
SPLIT_PARAMS = dict(compiler_params=pltpu.CompilerParams(has_side_effects=pltpu.SideEffectType.DATAFLOW_SIDE_EFFECTING))


def _gather_start(name, shards):
    n = len(shards)
    n_sem = n * N_CHIPS

    def body(*refs):
        ins, lands, (ssem, rsem), token = refs[:n], refs[n:2 * n], refs[2 * n:2 * n + 2], refs[-1]
        x, y, c = _me()
        k_me = 2 * x + y
        for i, (w_ref, l_ref) in enumerate(zip(ins, lands)):
            _remote(w_ref, l_ref.at[k_me], ssem.at[i], rsem.at[i], (x, y, 1 - c)).start()
        for r, (px, py) in enumerate(_other_chips(x, y)):
            for i, (w_ref, l_ref) in enumerate(zip(ins, lands)):
                rows = _half_rows(w_ref.shape[0], c)
                s = (r + 1) * n + i
                _remote(w_ref.at[rows], l_ref.at[k_me, rows], ssem.at[s], rsem.at[s], (px, py, c)).start()
        token[...] = jnp.zeros_like(token)

    src = [pltpu.HBM(s.shape, s.dtype) for s in shards]
    dst = [pltpu.HBM((N_CHIPS,) + s.shape, s.dtype) for s in shards]
    outs = _call(
        body, name=name, in_specs=[IN_HBM] * (2 * n),
        out_specs=[SEM, SEM] + [IN_HBM] * (2 * n) + [pl.BlockSpec(memory_space=pltpu.VMEM)],
        out_shape=[pltpu.SemaphoreType.DMA((n_sem,)), pltpu.SemaphoreType.DMA((n_sem,))] + src + dst
        + [jax.ShapeDtypeStruct((SUBLANES, LANES), F32)],
        input_output_aliases={i: 2 + i for i in range(2 * n)}, **SPLIT_PARAMS,
    )(*[pltpu.with_memory_space_constraint(s, pltpu.HBM) for s in shards],
      *[pltpu.with_memory_space_constraint(lax.empty(d.shape, d.dtype), pltpu.HBM) for d in dst])
    return outs[:-1], outs[-1]


def _gather_wait(name, handle, after):
    ssem, rsem, thru = handle[0], handle[1], handle[2:]
    n = len(thru) // 2

    def body(*refs):
        ins, lands, (ssem_ref, rsem_ref) = refs[:n], refs[n:2 * n], refs[2 * n:2 * n + 2]
        x, y, c = _me()
        k_me = 2 * x + y
        for i, (w_ref, l_ref) in enumerate(zip(ins, lands)):
            cp = _remote(w_ref, l_ref.at[k_me], ssem_ref.at[i], rsem_ref.at[i], (x, y, 1 - c))
            cp.wait_send()
            cp.wait_recv()
        for r, (px, py) in enumerate(_other_chips(x, y)):
            for i, (w_ref, l_ref) in enumerate(zip(ins, lands)):
                rows = _half_rows(w_ref.shape[0], c)
                s = (r + 1) * n + i
                cp = _remote(w_ref.at[rows], l_ref.at[2 * px + py, rows], ssem_ref.at[s], rsem_ref.at[s], (px, py, c))
                cp.wait_send()
                cp.wait_recv()

    outs = _call(
        body, name=name, in_specs=[IN_HBM] * (2 * n) + [SEM, SEM, ANY], out_specs=[IN_HBM] * (2 * n),
        out_shape=[pltpu.HBM(t.shape, t.dtype) for t in thru],
        input_output_aliases={i: i for i in range(2 * n)}, **SPLIT_PARAMS,
    )(*thru, ssem, rsem, after)
    return outs[n:]


def _forward_halves(name, blocks):
    n = len(blocks)
    n_sem = n * (N_CHIPS - 1)

    def body(*refs):
        outs, (ssem, rsem) = refs[n:2 * n], refs[2 * n:]
        x, y, c = _me()
        sib = (x, y, 1 - c)
        chips = _other_chips(x, y)
        sends = []
        for r, (px, py) in enumerate(chips):
            for i, o_ref in enumerate(outs):
                blk = o_ref.at[2 * px + py, _half_rows(o_ref.shape[1], c)]
                cp = _remote(blk, blk, ssem.at[r * n + i], rsem.at[r * n + i], sib)
                cp.start()
                sends.append(cp)
        for r, (px, py) in enumerate(chips):
            for i, o_ref in enumerate(outs):
                blk = o_ref.at[2 * px + py, _half_rows(o_ref.shape[1], 1 - c)]
                _remote(blk, blk, ssem.at[r * n + i], rsem.at[r * n + i], sib).wait_recv()
        for cp in sends:
            cp.wait_send()

    return _pcall(
        body, name=name, in_specs=[ANY] * n, out_specs=[ANY] * n,
        out_shape=[jax.ShapeDtypeStruct(b.shape, b.dtype) for b in blocks],
        input_output_aliases={i: i for i in range(n)},
        scratch_shapes=[pltpu.SemaphoreType.DMA((n_sem,)), pltpu.SemaphoreType.DMA((n_sem,))],
    )(*blocks)


def _scatter_start(name, parts):
    n = len(parts)
    n_sem = n * (N_CHIPS - 1)

    def body(*refs):
        ins, lands, (ssem, rsem), token = refs[:n], refs[n:2 * n], refs[2 * n:2 * n + 2], refs[-1]
        x, y, c = _me()
        k_me = 2 * x + y
        for r, (px, py) in enumerate(_other_chips(x, y)):
            for i, (p_ref, l_ref) in enumerate(zip(ins, lands)):
                s = r * n + i
                _remote(p_ref.at[2 * px + py], l_ref.at[k_me], ssem.at[s], rsem.at[s], (px, py, c)).start()
        token[...] = jnp.zeros_like(token)

    hbm = [pltpu.HBM(p.shape, p.dtype) for p in parts]
    outs = _call(
        body, name=name, in_specs=[IN_HBM] * (2 * n),
        out_specs=[SEM, SEM] + [IN_HBM] * (2 * n) + [pl.BlockSpec(memory_space=pltpu.VMEM)],
        out_shape=[pltpu.SemaphoreType.DMA((n_sem,)), pltpu.SemaphoreType.DMA((n_sem,))] + hbm + hbm
        + [jax.ShapeDtypeStruct((SUBLANES, LANES), F32)],
        input_output_aliases={i: 2 + i for i in range(2 * n)}, **SPLIT_PARAMS,
    )(*[pltpu.with_memory_space_constraint(p, pltpu.HBM) for p in parts],
      *[pltpu.with_memory_space_constraint(lax.empty(p.shape, p.dtype), pltpu.HBM) for p in parts])
    return outs[:-1], outs[-1]


def _scatter_wait(name, handle, after):
    ssem, rsem, thru = handle[0], handle[1], handle[2:]
    n = len(thru) // 2

    def body(*refs):
        ins, lands, (ssem_ref, rsem_ref) = refs[:n], refs[n:2 * n], refs[2 * n:2 * n + 2]
        x, y, c = _me()
        for r, (px, py) in enumerate(_other_chips(x, y)):
            for i, (p_ref, l_ref) in enumerate(zip(ins, lands)):
                s = r * n + i
                cp = _remote(p_ref.at[2 * px + py], l_ref.at[2 * px + py], ssem_ref.at[s], rsem_ref.at[s], (px, py, c))
                cp.wait_send()
                cp.wait_recv()

    outs = _call(
        body, name=name, in_specs=[IN_HBM] * (2 * n) + [SEM, SEM, ANY], out_specs=[IN_HBM] * (2 * n),
        out_shape=[pltpu.HBM(t.shape, t.dtype) for t in thru],
        input_output_aliases={i: i for i in range(2 * n)}, **SPLIT_PARAMS,
    )(*thru, ssem, rsem, after)
    return outs[n:]


def _share_halves(halves):
    n = len(halves)

    def body(*refs):
        ins, outs, (ssem, rsem) = refs[:n], refs[n:2 * n], refs[2 * n:]
        x, y, c = _me()
        copies = [_remote(r_ref, o_ref, ssem.at[i], rsem.at[i], (x, y, 1 - c))
                  for i, (r_ref, o_ref) in enumerate(zip(ins, outs))]
        for cp in copies:
            cp.start()
        for cp in copies:
            cp.wait()

    return _pcall(
        body, name="share_halves", in_specs=[ANY] * n, out_specs=[ANY] * n,
        out_shape=[jax.ShapeDtypeStruct(h.shape, h.dtype) for h in halves],
        scratch_shapes=[pltpu.SemaphoreType.DMA((n,)), pltpu.SemaphoreType.DMA((n,))],
    )(*halves)


def _exchange_small(v, reduce):
    rows = v.shape[0]

    def body(v_ref, out_ref, buf, ssem, rsem):
        x, y, c = _me()
        me = 4 * x + 2 * y + c
        peers = [((x + bx) % 2, (y + by) % 2, (c + bc) % 2)
                 for bx in (0, 1) for by in (0, 1) for bc in (0, 1) if (bx, by, bc) != (0, 0, 0)]
        dst = buf if reduce else out_ref
        dst[me] = v_ref[...]
        sends = [_remote(v_ref, dst.at[me], ssem.at[r], rsem.at[r], p) for r, p in enumerate(peers)]
        for cp in sends:
            cp.start()
        for r, (px, py, pc) in enumerate(peers):
            blk = dst.at[4 * px + 2 * py + pc]
            _remote(blk, blk, ssem.at[r], rsem.at[r], (px, py, pc)).wait_recv()
        if reduce:
            acc = buf[0]
            for d in range(1, N_DEV):
                acc = acc + buf[d]
            out_ref[...] = acc
        for cp in sends:
            cp.wait_send()

    vm = pl.BlockSpec(memory_space=pltpu.VMEM)
    out_shape = jax.ShapeDtypeStruct((rows, LANES) if reduce else (N_DEV, rows, LANES), F32)
    buf_shape = (N_DEV, rows, LANES) if reduce else (SUBLANES, LANES)
    return _pcall(
        body, pin=False, name="reduce_small" if reduce else "gather_small", in_specs=[vm], out_specs=vm, out_shape=out_shape,
        scratch_shapes=[pltpu.VMEM(buf_shape, F32), pltpu.SemaphoreType.DMA((N_DEV - 1,)),
                        pltpu.SemaphoreType.DMA((N_DEV - 1,))],
        compiler_params=pltpu.CompilerParams(vmem_limit_bytes=32 * 1024 * 1024),
    )(v)


def _add_pair(name, core, g, theirs):
    _, half, cols = theirs.shape
    tr = _tile(half, (256, 176, 128))
    nb = half // tr

    def body(c_ref, g_ref, t_ref, o32_ref, o16_ref):
        s = g_ref[...] + t_ref[...]
        o32_ref[...] = s
        o16_ref[...] = s.astype(BF16)

    spec = pl.BlockSpec((None, tr, cols), lambda k, i, c_ref: (k, i, 0))
    grid_spec = pltpu.PrefetchScalarGridSpec(
        num_scalar_prefetch=1, grid=(N_CHIPS, nb),
        in_specs=[pl.BlockSpec((None, tr, cols), lambda k, i, c_ref: (k, c_ref[0] * nb + i, 0)), spec],
        out_specs=[spec, spec])
    return _pcall(
        body, name=name, grid_spec=grid_spec,
        out_shape=[jax.ShapeDtypeStruct(theirs.shape, F32), jax.ShapeDtypeStruct(theirs.shape, BF16)],
        compiler_params=_params(("arbitrary", "arbitrary"), 8 * _nbytes((tr, cols + LANES), F32)),
    )(core, g, theirs)


def _add_chips(name, chip, p32, recv):
    _, half, cols = p32.shape
    tr = _tile(half, (256, 176, 128))

    def body(k_ref, p_ref, r0_ref, r1_ref, r2_ref, o_ref):
        o_ref[...] = ((p_ref[...] + r0_ref[...].astype(F32)) + r1_ref[...].astype(F32)) + r2_ref[...].astype(F32)

    def other(r):
        return pl.BlockSpec((None, tr, cols), lambda i, k_ref: (r + (k_ref[0] <= r).astype(jnp.int32), i, 0))
    grid_spec = pltpu.PrefetchScalarGridSpec(
        num_scalar_prefetch=1, grid=(half // tr,),
        in_specs=[pl.BlockSpec((None, tr, cols), lambda i, k_ref: (k_ref[0], i, 0)), other(0), other(1), other(2)],
        out_specs=pl.BlockSpec((tr, cols), lambda i, k_ref: (i, 0)))
    return _pcall(
        body, name=name, grid_spec=grid_spec, out_shape=jax.ShapeDtypeStruct((half, cols), F32),
        compiler_params=_params(("arbitrary",), 10 * _nbytes((tr, cols + LANES), F32)),
    )(chip, p32, recv, recv, recv)


def kernel(x, mem, w_in, b_in, hg_lb_logits, hg_norm_w, ml_conv_w, ml_conv_b, ml_norm_w, w_out, ln1_g, ln1_b, ca_wq, ca_wkv, ca_wo, ln2_g, ln2_b, ffn_w_up, ffn_conv_w, ffn_conv_b, ffn_w_down, ln3_g, ln3_b, loss_target, m_w_in, m_b_in, m_hg_lb_logits, m_hg_norm_w, m_ml_conv_w, m_ml_conv_b, m_ml_norm_w, m_w_out, m_ln1_g, m_ln1_b, m_ca_wq, m_ca_wkv, m_ca_wo, m_ln2_g, m_ln2_b, m_ffn_w_up, m_ffn_conv_w, m_ffn_conv_b, m_ffn_w_down, m_ln3_g, m_ln3_b, v_w_in, v_b_in, v_hg_lb_logits, v_hg_norm_w, v_ml_conv_w, v_ml_conv_b, v_ml_norm_w, v_w_out, v_ln1_g, v_ln1_b, v_ca_wq, v_ca_wkv, v_ca_wo, v_ln2_g, v_ln2_b, v_ffn_w_up, v_ffn_conv_w, v_ffn_conv_b, v_ffn_w_down, v_ln3_g, v_ln3_b):
    return _train_step(dict(locals()))


WEIGHTS = ("w_in", "b_in", "hg_lb_logits", "hg_norm_w", "ml_conv_w", "ml_conv_b", "ml_norm_w", "w_out", "ln1_g",
           "ln1_b", "ca_wq", "ca_wkv", "ca_wo", "ln2_g", "ln2_b", "ffn_w_up", "ffn_conv_w", "ffn_conv_b",
           "ffn_w_down", "ln3_g", "ln3_b")
MATRICES = ("w_in", "w_out", "ca_wq", "ca_wkv", "ca_wo", "ffn_w_up", "ffn_w_down")
COL_SHARDED = ("w_in", "ca_wkv", "ffn_w_up", "ml_conv_w", "ffn_conv_w")
SMALL = tuple(n for n in WEIGHTS if n not in MATRICES)
PART_ROWS = 16


def _part_rows(shape, lead):
    n = 1
    for s in shape[lead:]:
        n *= s
    return -(-n // (LANES * PART_ROWS)) * PART_ROWS


def _pack(arrs, dtype, lead=0, rows=None):
    parts = []
    for a in arrs:
        head = a.shape[:lead]
        flat = a.reshape(head + (-1,)).astype(dtype)
        pad = _part_rows(a.shape, lead) * LANES - flat.shape[-1]
        flat = jnp.pad(flat, [(0, 0)] * lead + [(0, pad)])
        parts.append(flat.reshape(head + (-1, LANES)))
    used = sum(p.shape[lead] for p in parts)
    if rows is not None and rows > used:
        parts.append(jnp.zeros(parts[0].shape[:lead] + (rows - used, LANES), dtype))
    return jnp.concatenate(parts, axis=lead)


def _unpack(buf, shapes):
    lead = buf.shape[:-2]
    outs, r = [], 0
    for sh in shapes:
        n = 1
        for s in sh:
            n *= s
        nr = _part_rows(sh, 0)
        flat = buf[..., r:r + nr, :].reshape(lead + (nr * LANES,))
        outs.append(flat[..., :n].reshape(lead + tuple(sh)))
        r += nr
    return outs


def _cat_cols(s):
    return jnp.moveaxis(s, 0, 1).reshape(s.shape[1], -1)


def _stack_rows(s):
    return s.reshape(-1, s.shape[-1])


def _train_step(a):
    xs, mems, tgt = a["x"][0], a["mem"][0], a["loss_target"][0]
    core = lax.axis_index("c").astype(jnp.int32).reshape(1)
    chip = (2 * lax.axis_index("x") + lax.axis_index("y")).astype(jnp.int32).reshape(1)
    k_me = chip[0]
    shard = {n: a[n][0] for n in MATRICES}

    later = [n for n in MATRICES if n != "w_in"]
    taps = _exchange_small(_pack([a["ml_conv_w"][0], a["ffn_conv_w"][0]], F32), reduce=False)
    in_cols = a["w_in"].shape[-1]
    in_pad = -(-in_cols // (2 * BF16_ROWS)) * 2 * BF16_ROWS - in_cols
    w_in_t = jnp.pad(jnp.swapaxes(shard["w_in"], 0, 1).astype(BF16), ((0, in_pad), (0, 0)))
    w_in_t = _gather_weights([w_in_t])[0][:, :in_cols].reshape(D_IN, D_MODEL)
    w = {"w_in": jnp.pad(w_in_t, ((0, D_IN_PAD - D_IN), (0, 0)))}
    gathering, token = _gather_start("gather_start", [shard[n].astype(BF16) for n in later])
    taps = taps.reshape((N_CHIPS, 2) + taps.shape[1:])[:, 0]
    ml_cw, ffn_cw = [_cat_cols(s) for s in _unpack(taps, [a["ml_conv_w"].shape[1:], a["ffn_conv_w"].shape[1:]])]
    b_in_p = jnp.pad(a["b_in"], ((0, 0), (0, D_IN_PAD - D_IN))) + token[0:1, 0:1]
    mixer_w = (a["hg_lb_logits"], a["hg_norm_w"], ml_cw, a["ml_conv_b"], a["ml_norm_w"])
    up_cols = a["ffn_w_up"].shape[-1]

    xb = xs.astype(BF16)
    proj = _mm("proj", "nt", xb, w["w_in"], bias=b_in_p, tm=256, tn=D_IN_PAD)
    y, hst, cst, nst, mst = _mixer_fwd(proj, *mixer_w)
    w.update(zip(later, _forward_halves("forward_halves", _gather_wait("gather_wait", gathering, y))))
    for n in ("w_out", "ca_wq", "ca_wo", "ffn_w_down"):
        w[n] = _stack_rows(w[n])
    z1, x1, x1b = _mm("mix_out", "nn", y, w["w_out"], res=xs, res_scale=ALPHA, ln=("fwd", a["ln1_g"], a["ln1_b"]),
                      copy_dtype=BF16)
    q = _mm("ca_q", "nn", x1b, w["ca_wq"], out_dtype=BF16, tn=D_MODEL)
    kv = _mm("ca_kv", "nn", mems, w["ca_wkv"])
    o = _attn_fwd(q, kv)
    z2, x2, x2b = _mm("ca_out", "nn", o, w["ca_wo"], res=x1, res_scale=ALPHA, ln=("fwd", a["ln2_g"], a["ln2_b"]),
                      copy_dtype=BF16)
    w_up = _cat_cols(w["ffn_w_up"])
    u, hmid, dz3, g_ln3g, g_ln3b, loss_part, dz3b = _ffn_fwd(
        x2b, x2, w_up, ffn_cw, a["ffn_conv_b"], w["ffn_w_down"], a["ln3_g"], a["ln3_b"], tgt)

    grads = {"ln3_g": g_ln3g, "ln3_b": g_ln3b}
    grads["ffn_w_down"] = _mm("g_w_down", "tn", hmid, dz3b, tm=D_FF // 2, tn=D_MODEL)
    du, g_cw, g_cb, dz2, grads["ln2_g"], grads["ln2_b"], dz2b = _ffn_bwd(
        u, ffn_cw, a["ffn_conv_b"], dz3b, dz3, w["ffn_w_down"], w_up, z2, a["ln2_g"], a["ln2_b"])
    grads["ffn_conv_w"] = jnp.transpose(g_cw, (2, 1, 0, 3)).reshape(FFN_CONV, 2 * D_FF)
    grads["ffn_conv_b"] = jnp.transpose(g_cb, (2, 1, 0, 3)).reshape(1, 2 * D_FF)
    grads["ffn_w_up"] = _mm("g_w_up", "tn", x2b, du, out_groups=N_CHIPS, tm=D_MODEL, tn=up_cols)
    grads["ffn_w_down"] = grads["ffn_w_down"].reshape((N_CHIPS,) + shard["ffn_w_down"].shape)
    pending = {}

    def reduce_start(tag, names):
        group = [grads[n] for n in names]
        sums = [_add_pair("add_pair_" + n, core, g, t)
                for n, g, t in zip(names, group, _swap_halves("swap_halves_" + tag, group))]
        handle, token = _scatter_start("scatter_start_" + tag, [s16 for _, s16 in sums])
        pending[tag] = (names, [s32 for s32, _ in sums], handle)
        return token[0:1, 0:1]

    zero = reduce_start("ffn", ("ffn_w_up", "ffn_w_down"))
    do = _mm("d_o", "nt", dz2b, w["ca_wo"], bias=jnp.zeros((1, D_MODEL), F32) + zero, out_dtype=BF16, tn=D_MODEL)
    grads["ca_wo"] = _mm("g_wo", "tn", o, dz2b, tm=D_MODEL, tn=D_MODEL)
    dq, dkv = _attn_bwd(q, kv, do)
    grads["ca_wq"] = _mm("g_wq", "tn", x1b, dq, tm=D_MODEL, tn=D_MODEL)
    grads["ca_wkv"] = _mm("g_wkv", "tn", mems, dkv, out_groups=N_CHIPS, tm=D_MODEL)
    dz1, grads["ln1_g"], grads["ln1_b"], dz1b = _mm("d_x1", "nt", dq, w["ca_wq"], res=dz2, res_scale=ALPHA,
                                                    ln=("bwd", z1, a["ln1_g"], a["ln1_b"]), copy_dtype=BF16)
    dy = _mm("d_y", "nt", dz1b, w["w_out"], tn=D_MODEL)
    grads["w_out"] = _mm("g_w_out", "tn", y, dz1b, tm=D_MODEL, tn=D_MODEL)
    for n in ("w_out", "ca_wq", "ca_wo"):
        grads[n] = grads[n].reshape((N_CHIPS,) + shard[n].shape)
    zero = reduce_start("attn", ("w_out", "ca_wq", "ca_wkv", "ca_wo"))
    (dproj, g_b_in, grads["hg_lb_logits"], grads["hg_norm_w"], grads["ml_conv_w"], grads["ml_conv_b"],
     grads["ml_norm_w"]) = _mixer_bwd(proj, dy, hst, cst, nst, mst, mixer_w[0], mixer_w[1] + zero, *mixer_w[2:])
    g_in_t = _mm("g_w_in", "tn", dproj, xb, tm=up_cols, tn=D_MODEL)[:D_IN].reshape(N_CHIPS, in_cols, D_MODEL)
    grads["w_in"] = jnp.pad(g_in_t, ((0, 0), (0, in_pad), (0, 0)))
    grads["b_in"] = g_b_in[:, :D_IN]
    zero = reduce_start("in", ("w_in",))
    dx = _mm("d_x", "nn", dproj, w["w_in"], bias=jnp.zeros((1, D_MODEL), F32) + zero, res=dz1, res_scale=ALPHA,
             tm=256, tn=D_MODEL)

    halves = {}
    for tag, (names, sums32, handle) in pending.items():
        for n, s32, r in zip(names, sums32, _scatter_wait("scatter_wait_" + tag, handle, dx)):
            halves[n] = _add_chips("add_chips_" + n, chip, s32, r)
    halves = [halves[n] for n in MATRICES]
    other_halves = _share_halves(halves)

    small_shapes = [grads[n].shape for n in SMALL] + [loss_part.shape]
    summed = _unpack(_exchange_small(_pack([grads[n] for n in SMALL] + [loss_part], F32), reduce=True), small_shapes)
    loss = summed[-1][0, 0]
    for n, g in zip(SMALL, summed[:-1]):
        if n in COL_SHARDED:
            cols = a[n].shape[-1]
            g = lax.dynamic_slice_in_dim(g, k_me * cols, cols, axis=1)
        grads[n] = g

    delta, new_m, new_v = {}, {}, {}
    def lanes_t(p):
        return jnp.swapaxes(p, 0, 1).reshape(-1, LANES)

    for n, mine, theirs in zip(MATRICES, halves, other_halves):
        if n == "w_in":
            outs = _adamw_halves("adamw_" + n, core, lanes_t(shard[n]), mine.reshape(-1, LANES),
                                 theirs.reshape(-1, LANES), lanes_t(a["m_" + n][0]), lanes_t(a["v_" + n][0]))
            outs = [jnp.swapaxes(o.reshape(in_cols, D_MODEL), 0, 1) for o in outs]
        else:
            outs = _adamw_halves("adamw_" + n, core, shard[n], mine, theirs, a["m_" + n][0], a["v_" + n][0])
        grads[n], delta[n], new_m[n], new_v[n] = outs
    small_w = [a[n][0] if a[n].ndim == 3 else a[n] for n in SMALL]
    small_m = [a["m_" + n][0] if a[n].ndim == 3 else a["m_" + n] for n in SMALL]
    small_v = [a["v_" + n][0] if a[n].ndim == 3 else a["v_" + n] for n in SMALL]
    shapes = [w.shape for w in small_w]
    packed = [_pack(l, F32) for l in (small_w, [grads[n] for n in SMALL], small_m, small_v)]
    for out, buf in zip((delta, new_m, new_v), _adamw("adamw_small", *packed)):
        for n, v in zip(SMALL, _unpack(buf, shapes)):
            out[n] = v

    def shaped(d):
        return [d[n].reshape(a[n].shape) for n in WEIGHTS]
    return (loss, dx[None], *shaped(grads), *shaped(delta), *shaped(new_m), *shaped(new_v))
```

```python
import functools

import jax
import jax.numpy as jnp
from jax import lax
from jax.experimental import pallas as pl
from jax.experimental.pallas import tpu as pltpu

F32 = jnp.float32
BF16 = jnp.bfloat16

D_MODEL = 1024
HEADS = 4
DK = 128
D_GRP = HEADS * DK
CHUNK = 64
ML_CONV = 4
FFN_CONV = 3
D_FF = 2816
CA_DH = D_MODEL // HEADS
DEPTH = 1
ALPHA = (2.0 * DEPTH) ** 0.25
LN_EPS = 1e-5
NEG_BIG = -1e30
D_IN = 8 * D_GRP + 2 * HEADS
D_IN_PAD = 8 * D_GRP + 128
ADAM_LR, ADAM_B1, ADAM_B2, ADAM_EPS, ADAM_WD, ADAM_STEP = 0.001, 0.9, 0.999, 1e-08, 0.01, 10

SUBLANES = 8
LANES = 128
VMEM_BYTES = 64 * 1024 * 1024


def _pcall(body, pin=True, **kw):
    if not pin:
        return _call(body, **kw)
    kw["out_shape"] = jax.tree.map(lambda s: pltpu.HBM(s.shape, s.dtype), kw["out_shape"])
    call = _call(body, **kw)

    def pinned(*args):
        return call(*[pltpu.with_memory_space_constraint(x, pltpu.HBM) if jnp.issubdtype(x.dtype, jnp.floating) else x
                      for x in args])
    return pinned


def _call(body, **kw):
    return pl.pallas_call(body, **kw)


def _params(semantics, vmem_bytes):
    limit = int(min(max(2 * vmem_bytes, 16 * 1024 * 1024), VMEM_BYTES - 8 * 1024 * 1024))
    return pltpu.CompilerParams(dimension_semantics=semantics, vmem_limit_bytes=limit)


def _nbytes(shape, dtype):
    n = 1
    for s in shape:
        n *= s
    return n * jnp.dtype(dtype).itemsize


def _dg(a, b, ca, cb):
    return lax.dot_general(a.astype(BF16), b.astype(BF16), (((ca,), (cb,)), ((), ())),
                           preferred_element_type=F32)


@jax.custom_vjp
def mm_nn(a, b):
    return _dg(a, b, 1, 0)


mm_nn.defvjp(lambda a, b: (_dg(a, b, 1, 0), (a, b)),
             lambda r, g: (_dg(g, r[1], 1, 1).astype(r[0].dtype), _dg(r[0], g, 0, 0).astype(r[1].dtype)))


@jax.custom_vjp
def mm_nt(a, b):
    return _dg(a, b, 1, 1)


mm_nt.defvjp(lambda a, b: (_dg(a, b, 1, 1), (a, b)),
             lambda r, g: (_dg(g, r[1], 1, 0).astype(r[0].dtype), _dg(g, r[0], 0, 0).astype(r[1].dtype)))


@jax.custom_vjp
def mm_tn(a, b):
    return _dg(a, b, 0, 0)


mm_tn.defvjp(lambda a, b: (_dg(a, b, 0, 0), (a, b)),
             lambda r, g: (_dg(r[1], g, 1, 1).astype(r[0].dtype), _dg(r[0], g, 1, 0).astype(r[1].dtype)))


def _tri(n, lower):
    r = lax.broadcasted_iota(jnp.int32, (n, n), 0)
    c = lax.broadcasted_iota(jnp.int32, (n, n), 1)
    return ((r >= c) if lower else (r <= c)).astype(F32)


def _tri_dot(lower, x):
    t = _tri(x.shape[0], lower).astype(BF16)
    hi = x.astype(BF16)
    rest = x - hi.astype(F32)
    mid = rest.astype(BF16)
    lo = (rest - mid.astype(F32)).astype(BF16)
    return sum(lax.dot_general(t, p, (((1,), (0,)), ((), ())), preferred_element_type=F32) for p in (hi, mid, lo))


@jax.custom_vjp
def cumsum_rows(x):
    return _tri_dot(True, x)


cumsum_rows.defvjp(lambda x: (_tri_dot(True, x), None), lambda _, g: (_tri_dot(False, g),))


def _shift_impl(halo, x, d):
    xx = jnp.concatenate([halo, x], axis=0)
    return pltpu.roll(xx, d, 0)[SUBLANES:]


@functools.partial(jax.custom_vjp, nondiff_argnums=(2,))
def shift_rows(halo, x, d):
    return _shift_impl(halo, x, d)


def _shift_bwd(d, _, g):
    n = g.shape[0] + SUBLANES
    gg = jnp.concatenate([jnp.zeros((SUBLANES, g.shape[1]), g.dtype), g], axis=0)
    r = pltpu.roll(gg, n - d, 0)
    return r[:SUBLANES], r[SUBLANES:]


shift_rows.defvjp(lambda halo, x, d: (_shift_impl(halo, x, d), None), _shift_bwd)


def causal_conv(halo, x, w_rows, b):
    k = len(w_rows)
    y = b + w_rows[k - 1] * x
    for d in range(1, k):
        y = y + w_rows[k - 1 - d] * shift_rows(halo, x, d)
    return y


def _sigmoid(x):
    return 1.0 / (1.0 + jnp.exp(-x))


def _silu(x):
    return x * _sigmoid(x)


def _log_sigmoid(x):
    return jnp.minimum(x, 0.0) - jnp.log(1.0 + jnp.exp(-jnp.abs(x)))


def _pick_row(x, i):
    row = lax.broadcasted_iota(jnp.int32, (x.shape[0], 1), 0)
    return jnp.sum(jnp.where(row == i, x, 0.0), axis=0, keepdims=True)


def _layer_norm(z, g, b):
    mu = jnp.mean(z, axis=-1, keepdims=True)
    zc = z - mu
    var = jnp.mean(zc * zc, axis=-1, keepdims=True)
    return zc * lax.rsqrt(var + LN_EPS) * g + b


def _qk_conv(halo, x, w0, w1, w2, w3, b):
    return _silu(causal_conv(halo, x, (w0, w1, w2, w3), b))


def _grp(i, h=None):
    if h is None:
        return pl.ds(i * D_GRP, D_GRP)
    return pl.ds(i * D_GRP + h * DK, DK)


def _mixer_specs(n_chunks, reverse):
    def chunk(c):
        return n_chunks - 1 - c if reverse else c
    row8 = CHUNK // SUBLANES
    proj_spec = pl.BlockSpec((CHUNK, D_IN_PAD), lambda c: (chunk(c), 0))
    halo_spec = pl.BlockSpec((SUBLANES, 2 * D_GRP), lambda c: (jnp.maximum(chunk(c) * row8 - 1, 0), 2))
    small = [pl.BlockSpec((2, D_GRP), lambda c: (0, 0)), pl.BlockSpec((1, D_GRP), lambda c: (0, 0)),
             pl.BlockSpec((ML_CONV, 2 * D_GRP), lambda c: (0, 0)), pl.BlockSpec((1, 2 * D_GRP), lambda c: (0, 0)),
             pl.BlockSpec((1, D_GRP), lambda c: (0, 0))]
    state_specs = [pl.BlockSpec((1, HEADS, DK, DK), lambda c: (chunk(c), 0, 0, 0)),
                   pl.BlockSpec((1, HEADS, DK, DK), lambda c: (chunk(c), 0, 0, 0)),
                   pl.BlockSpec((1, HEADS, 1, DK), lambda c: (chunk(c), 0, 0, 0)),
                   pl.BlockSpec((1, HEADS, 1, DK), lambda c: (chunk(c), 0, 0, 0))]
    y_spec = pl.BlockSpec((CHUNK, 2 * D_GRP), lambda c: (chunk(c), 0))
    return proj_spec, halo_spec, small, state_specs, y_spec, chunk


def _heads(x):
    return [x[:, h * DK:(h + 1) * DK] for h in range(HEADS)]


def _last(x, j):
    lane = lax.broadcasted_iota(jnp.int32, (1, x.shape[-1]), 1)
    return jnp.sum(jnp.where(lane == j, x, 0.0), axis=-1, keepdims=True)


def _hg_chunk(st_t, hq, hf, hi, hgate, l0, l1, nw):
    n = hq.shape[0]
    lb = _sigmoid(l0 - l1)
    q = _silu(hq)
    lf = jnp.log(lb + (1.0 - lb) * _sigmoid(hf))
    k = (1.0 - lb) * _sigmoid(-hf)
    b = cumsum_rows(lf)
    b_ref = _pick_row(b, n // 2 - 1)
    b_last = _pick_row(b, n - 1)
    qa, ka =_heads(q * jnp.exp(b - b_ref)), _heads(k * jnp.exp(b_ref - b))
    qe, kd, eb, v = _heads(q * jnp.exp(b)), _heads(k * jnp.exp(b_last - b)), _heads(jnp.exp(b_last)), _heads(hi)
    tri = _tri(n, True) > 0
    attn = [jnp.where(tri, mm_nt(qa[h], ka[h]), 0.0) for h in range(HEADS)]
    o = [mm_nn(attn[h], v[h]) + mm_nt(qe[h], st_t[h]) for h in range(HEADS)]
    st_new = jnp.stack([eb[h] * st_t[h] + mm_tn(v[h], kd[h]) for h in range(HEADS)])
    yn = [o[h] * lax.rsqrt(jnp.mean(o[h] * o[h], axis=-1, keepdims=True) + LN_EPS) for h in range(HEADS)]
    return st_new, jnp.concatenate(yn, axis=1) * nw * _silu(hgate)


def _ml_chunk(c_st, n_st, m_st, q, k, v, gates, og, nw):
    n = q.shape[0]
    ig = jnp.stack([_last(gates, h) for h in range(HEADS)])
    log_f = _log_sigmoid(gates)
    fl = jnp.stack([_last(log_f, HEADS + h) for h in range(HEADS)])
    bw = cumsum_rows(jnp.concatenate([jnp.broadcast_to(fl[h], (n, DK)) for h in range(HEADS)], axis=1))
    b = jnp.stack([_last(x, 0) for x in _heads(bw)])
    g = jnp.sum(fl, axis=1, keepdims=True)
    eye = lax.broadcasted_iota(jnp.int32, (n, n), 0) == lax.broadcasted_iota(jnp.int32, (n, n), 1)
    e_row = jnp.sum(jnp.where(eye, ig - b, 0.0), axis=1, keepdims=True)
    d = jnp.where(_tri(n, True) > 0, b + e_row, -jnp.inf)
    inter = b + m_st
    m_t = jnp.maximum(inter, jnp.max(d, axis=2, keepdims=True))
    qs, kh, vh = _heads(q * (DK ** -0.5)), _heads(k), _heads(v)
    s = jnp.stack([mm_nt(qs[h], kh[h]) for h in range(HEADS)]) * jnp.exp(d - m_t)
    w_inter = jnp.exp(inter - m_t)
    num = (jnp.stack([mm_nn(s[h], vh[h]) for h in range(HEADS)])
           + w_inter * jnp.stack([mm_nn(qs[h], c_st[h]) for h in range(HEADS)]))
    den = jnp.sum(s, axis=2, keepdims=True) + w_inter * jnp.sum(jnp.stack(qs) * n_st, axis=2, keepdims=True)
    h_out = num / jnp.maximum(jnp.abs(den), jnp.exp(-m_t))
    a = g - b + ig
    m_new = jnp.maximum(g + m_st, jnp.max(a, axis=1, keepdims=True))
    decay = jnp.exp(g + m_st - m_new)
    wk = jnp.stack(kh) * jnp.exp(a - m_new)
    c_new = decay * c_st + jnp.stack([mm_tn(wk[h], vh[h]) for h in range(HEADS)])
    n_new = decay * n_st + jnp.sum(wk, axis=1, keepdims=True)
    hc = h_out - jnp.mean(h_out, axis=-1, keepdims=True)
    yn = hc * lax.rsqrt(jnp.mean(hc * hc, axis=-1, keepdims=True) + LN_EPS)
    y = _sigmoid(og) * (jnp.concatenate([yn[h] for h in range(HEADS)], axis=1) * nw)
    return c_new, n_new, m_new, y


def _mixer_inputs(proj_ref, lg_ref, hnw_ref, mnw_ref, qk):
    hg_in = (proj_ref[:, _grp(0)], proj_ref[:, _grp(1)], proj_ref[:, _grp(2)], proj_ref[:, _grp(3)],
             lg_ref[0:1, :], lg_ref[1:2, :], hnw_ref[...])
    ml_in = (qk[:, :D_GRP], qk[:, D_GRP:], proj_ref[:, _grp(6)], proj_ref[:, pl.ds(8 * D_GRP, LANES)],
             proj_ref[:, _grp(7)], mnw_ref[...])
    return hg_in, ml_in


def _mixer_fwd(proj, lb_logits, hg_nw, conv_w, conv_b, ml_nw):
    seq = proj.shape[0]
    n_chunks = seq // CHUNK
    proj_spec, halo_spec, small, state_specs, y_spec, _ = _mixer_specs(n_chunks, False)

    def body(proj_ref, halo_ref, lg_ref, hnw_ref, cw_ref, cb_ref, mnw_ref,
             y_ref, hst_ref, cst_ref, nst_ref, mst_ref, hs, cs, ns, ms):
        c = pl.program_id(0)

        @pl.when(c == 0)
        def _():
            hs[...] = jnp.zeros_like(hs)
            cs[...] = jnp.zeros_like(cs)
            ns[...] = jnp.zeros_like(ns)
            ms[...] = jnp.full(ms.shape, NEG_BIG, F32)

        hst_ref[0] = hs[...]
        cst_ref[0] = cs[...]
        nst_ref[0] = ns[...]
        mst_ref[0] = ms[...]
        halo = jnp.where(c > 0, halo_ref[...], 0.0)
        qk = _qk_conv(halo, proj_ref[:, pl.ds(4 * D_GRP, 2 * D_GRP)],
                      cw_ref[0:1, :], cw_ref[1:2, :], cw_ref[2:3, :], cw_ref[3:4, :], cb_ref[...])
        hg_in, ml_in = _mixer_inputs(proj_ref, lg_ref, hnw_ref, mnw_ref, qk)
        hs[...], y_hg = _hg_chunk(hs[...], *hg_in)
        cs[...], ns[...], m_new, y_ml = _ml_chunk(cs[...], ns[...], _last(ms[...], 0), *ml_in)
        ms[...] = jnp.broadcast_to(m_new, ms.shape)
        y_ref[:, pl.ds(0, D_GRP)] = y_hg.astype(BF16)
        y_ref[:, pl.ds(D_GRP, D_GRP)] = y_ml.astype(BF16)

    st = jax.ShapeDtypeStruct((n_chunks, HEADS, DK, DK), F32)
    vec = jax.ShapeDtypeStruct((n_chunks, HEADS, 1, DK), F32)
    vmem = 2 * (_nbytes((CHUNK, D_IN_PAD), F32) + _nbytes((CHUNK, 2 * D_GRP), F32) + 2 * _nbytes((HEADS, DK, DK), F32)) \
        + 2 * _nbytes((HEADS, DK, DK), F32)
    return _pcall(
        body, name="mixer_fwd", grid=(n_chunks,),
        in_specs=[proj_spec, halo_spec] + small,
        out_specs=[y_spec] + state_specs,
        out_shape=[jax.ShapeDtypeStruct((seq, 2 * D_GRP), BF16), st, st, vec, vec],
        scratch_shapes=[pltpu.VMEM((HEADS, DK, DK), F32), pltpu.VMEM((HEADS, DK, DK), F32),
                        pltpu.VMEM((HEADS, 1, DK), F32), pltpu.VMEM((HEADS, 1, DK), F32)],
        compiler_params=_params(("arbitrary",), vmem),
    )(proj, proj, lb_logits, hg_nw, conv_w, conv_b, ml_nw)


def _mixer_bwd(proj, dy, hst, cst, nst, mst, lb_logits, hg_nw, conv_w, conv_b, ml_nw):
    seq = proj.shape[0]
    n_chunks = seq // CHUNK
    proj_spec, halo_spec, small, state_specs, y_spec, _ = _mixer_specs(n_chunks, True)

    def body(proj_ref, halo_ref, dy_ref, hst_ref, cst_ref, nst_ref, mst_ref,
             lg_ref, hnw_ref, cw_ref, cb_ref, mnw_ref,
             dproj_ref, dbin_ref, dlg_ref, dhnw_ref, dcw_ref, dcb_ref, dmnw_ref,
             dhs, dcs, dns, dms, dhalo):
        c = pl.program_id(0)

        @pl.when(c == 0)
        def _():
            for r in (dhs, dcs, dns, dms, dhalo, dbin_ref, dlg_ref, dhnw_ref, dcw_ref, dcb_ref, dmnw_ref):
                r[...] = jnp.zeros_like(r)

        def put(cols, val):
            dproj_ref[:, cols] = val.astype(BF16)
            dbin_ref[:, cols] += jnp.sum(val, axis=0, keepdims=True)

        first = c == n_chunks - 1
        halo = jnp.where(first, 0.0, halo_ref[...])
        x_qk = proj_ref[:, pl.ds(4 * D_GRP, 2 * D_GRP)]
        conv_args = (halo, x_qk, cw_ref[0:1, :], cw_ref[1:2, :], cw_ref[2:3, :], cw_ref[3:4, :], cb_ref[...])
        qk, conv_vjp = jax.vjp(_qk_conv, *conv_args)
        hg_in, ml_in = _mixer_inputs(proj_ref, lg_ref, hnw_ref, mnw_ref, qk)
        _, hg_vjp = jax.vjp(_hg_chunk, hst_ref[0], *hg_in)
        _, ml_vjp = jax.vjp(_ml_chunk, cst_ref[0], nst_ref[0], _last(mst_ref[0], 0), *ml_in)
        dst, dhq, dhf, dhi, dhg, dl0, dl1, dnw = hg_vjp((dhs[...], dy_ref[:, pl.ds(0, D_GRP)]))
        dc, dn, dm, dq, dk, dv, dgates, dog, dmn = ml_vjp(
            (dcs[...], dns[...], _last(dms[...], 0), dy_ref[:, pl.ds(D_GRP, D_GRP)]))
        dhs[...] = dst
        dcs[...] = dc
        dns[...] = dn
        dms[...] = jnp.broadcast_to(dm, dms.shape)
        for i, val in ((0, dhq), (1, dhf), (2, dhi), (3, dhg), (6, dv), (7, dog)):
            put(_grp(i), val)
        put(pl.ds(8 * D_GRP, LANES), dgates)
        dlg_ref[0:1, :] += dl0
        dlg_ref[1:2, :] += dl1
        dhnw_ref[...] += dnw
        dmnw_ref[...] += dmn
        dh, dx, dw0, dw1, dw2, dw3, db = conv_vjp(jnp.concatenate([dq, dk], axis=1))
        tail = jnp.concatenate([jnp.zeros((CHUNK - SUBLANES, 2 * D_GRP), F32), dhalo[...]], axis=0)
        put(pl.ds(4 * D_GRP, 2 * D_GRP), dx + tail)
        dhalo[...] = dh
        for d, dw in enumerate((dw0, dw1, dw2, dw3)):
            dcw_ref[d:d + 1, :] += dw
        dcb_ref[...] += db

    row = pl.BlockSpec((1, D_GRP), lambda c: (0, 0))
    small_out = [pl.BlockSpec((1, D_IN_PAD), lambda c: (0, 0)), pl.BlockSpec((2, D_GRP), lambda c: (0, 0)), row,
                 pl.BlockSpec((ML_CONV, 2 * D_GRP), lambda c: (0, 0)), pl.BlockSpec((1, 2 * D_GRP), lambda c: (0, 0)), row]
    dy_spec = pl.BlockSpec((CHUNK, 2 * D_GRP), y_spec.index_map)
    vmem = 2 * (2 * _nbytes((CHUNK, D_IN_PAD), F32) + _nbytes((CHUNK, 2 * D_GRP), F32)
                + 2 * _nbytes((HEADS, DK, DK), F32)) + 2 * _nbytes((HEADS, DK, DK), F32) + 4 * 1024 * 1024
    return _pcall(
        body, name="mixer_bwd", grid=(n_chunks,),
        in_specs=[proj_spec, halo_spec, dy_spec] + state_specs + small,
        out_specs=[proj_spec] + small_out,
        out_shape=[jax.ShapeDtypeStruct((seq, D_IN_PAD), BF16), jax.ShapeDtypeStruct((1, D_IN_PAD), F32),
                   jax.ShapeDtypeStruct((2, D_GRP), F32), jax.ShapeDtypeStruct((1, D_GRP), F32),
                   jax.ShapeDtypeStruct((ML_CONV, 2 * D_GRP), F32), jax.ShapeDtypeStruct((1, 2 * D_GRP), F32),
                   jax.ShapeDtypeStruct((1, D_GRP), F32)],
        scratch_shapes=[pltpu.VMEM((HEADS, DK, DK), F32), pltpu.VMEM((HEADS, DK, DK), F32),
                        pltpu.VMEM((HEADS, 1, DK), F32), pltpu.VMEM((HEADS, 1, DK), F32),
                        pltpu.VMEM((SUBLANES, 2 * D_GRP), F32)],
        compiler_params=_params(("arbitrary",), vmem),
    )(proj, proj, dy, hst, cst, nst, mst, lb_logits, hg_nw, conv_w, conv_b, ml_nw)


def _tile(n, prefs, unit=None):
    unit = unit or n
    for p in prefs:
        if unit % p == 0 and n % p == 0:
            return p
    return unit


def _logical(arr):
    return arr.shape if arr.ndim == 2 else (arr.shape[1], arr.shape[0] * arr.shape[2])


def _group(arr):
    return arr.shape[-1]


def _split_spec(ndim, group, tr, tc, where):
    if ndim == 2:
        return pl.BlockSpec((tr, tc), where)
    per = group // tc
    assert per * tc == group, (group, tc)

    def index(*ids):
        bi, bj = where(*ids)
        return (bj // per, bi, bj % per)
    return pl.BlockSpec((None, tr, tc), index)


def _mm(name, mode, a, b, *, bias=None, res=None, res_scale=1.0, ln=None, out_dtype=F32, out_groups=None,
        copy_dtype=None, tm=None, tn=None, tk=None):
    la, lb = _logical(a), _logical(b)
    if mode == "nn":
        (m, k), n = la, lb[1]
        n_unit = _group(b) if b.ndim == 3 else n
        kc = _group(a) if a.ndim == 3 else k
    elif mode == "nt":
        (m, k), n = la, lb[0]
        n_unit = n
        kc = min(_group(a) if a.ndim == 3 else k, _group(b) if b.ndim == 3 else k)
    else:
        (k, m), n = la, lb[1]
        n_unit, kc = (_group(b) if b.ndim == 3 else n), k
        assert a.ndim == 2
    if out_groups:
        n_unit = min(n_unit, n // out_groups)
    kind = ln[0] if ln else None
    tm = tm or (256 if ln else _tile(m, (512, 256, 128)))
    tn = n if ln else (tn or _tile(n, (512, 384, 256, 128), n_unit))
    tk = (tk or _tile(k, (2048, 512, 256, 128))) if mode == "tn" else k
    gi, gj, gk = m // tm, n // tn, k // tk
    assert gi * tm == m and gj * tn == n and gk * tk == k and n_unit % tn == 0, (name, m, n, k, tm, tn, tk)
    ca, cb = {"nn": (1, 0), "nt": (1, 1), "tn": (0, 0)}[mode]
    i_outer = gk > 1 or (gi - 1) * _nbytes(b.shape, b.dtype) <= (gj - 1) * _nbytes(a.shape, a.dtype)

    def ij(where):
        return (lambda p, q, kk: where(p, q, kk)) if i_outer else (lambda p, q, kk: where(q, p, kk))
    if mode == "tn":
        a_spec = pl.BlockSpec((tk, tm), ij(lambda i, j, kk: (kk, i)))
    elif a.ndim == 3:
        a_spec = pl.BlockSpec((a.shape[0], tm, _group(a)), ij(lambda i, j, kk: (0, i, 0)))
    else:
        a_spec = pl.BlockSpec((tm, k), ij(lambda i, j, kk: (i, 0)))
    if mode != "nt":
        b_spec = _split_spec(b.ndim, _group(b), tk, tn, ij(lambda i, j, kk: (kk, j)))
    elif b.ndim == 3:
        b_spec = pl.BlockSpec((b.shape[0], tn, _group(b)), ij(lambda i, j, kk: (0, j, 0)))
    else:
        b_spec = pl.BlockSpec((tn, k), ij(lambda i, j, kk: (j, 0)))
    row_spec = pl.BlockSpec((1, tn), ij(lambda i, j, kk: (0, j)))
    blk_spec = pl.BlockSpec((tm, tn), ij(lambda i, j, kk: (i, j)))
    ins, in_specs = [a, b], [a_spec, b_spec]
    if bias is not None:
        ins.append(bias), in_specs.append(row_spec)
    if res is not None:
        ins.append(res), in_specs.append(blk_spec)
    if kind == "fwd":
        ins += [ln[1], ln[2]]
        in_specs += [row_spec, row_spec]
    elif kind == "loss":
        ins += [ln[1], ln[2], ln[3]]
        in_specs += [row_spec, row_spec, blk_spec]
    elif kind == "bwd":
        ins += [ln[1], ln[2], ln[3]]
        in_specs += [blk_spec, row_spec, row_spec]
    if out_groups:
        blk_out = jax.ShapeDtypeStruct((out_groups, m, n // out_groups), out_dtype)
        out_spec = _split_spec(3, n // out_groups, tm, tn, ij(lambda i, j, kk: (i, j)))
    else:
        blk_out, out_spec = jax.ShapeDtypeStruct((m, n), out_dtype), blk_spec
    row_out = jax.ShapeDtypeStruct((1, n), F32)
    if kind is None:
        out_shape, out_specs = [blk_out], [out_spec]
    elif kind == "fwd":
        out_shape, out_specs = [blk_out, blk_out], [blk_spec, blk_spec]
    else:
        out_shape, out_specs = [blk_out, row_out, row_out], [blk_spec, row_spec, row_spec]
        if kind == "loss":
            out_shape.append(jax.ShapeDtypeStruct((1, LANES), F32))
            out_specs.append(pl.BlockSpec((1, LANES), lambda p, q, kk: (0, 0)))
    if copy_dtype is not None:
        out_shape.append(jax.ShapeDtypeStruct((m, n), copy_dtype))
        out_specs.append(blk_spec)
    n_in = len(ins)

    def body(*refs):
        in_refs, out_refs, acc_ref = refs[:n_in], refs[n_in:n_in + len(out_shape)], refs[-1]
        i, kk = pl.program_id(0 if i_outer else 1), pl.program_id(2)
        a_ref, b_ref = in_refs[:2]
        extra = list(in_refs[2:])

        def epilogue(acc, rows=slice(None)):
            rest = list(extra)
            if bias is not None:
                acc = acc + rest.pop(0)[...]
            if res is not None:
                acc = acc + res_scale * rest.pop(0)[rows, :]
            if kind is None:
                out_refs[0][...] = acc.astype(out_dtype)
                return
            if kind == "fwd":
                out_refs[0][rows, :] = acc
                y = _layer_norm(acc, rest[0][...], rest[1][...])
                out_refs[1][rows, :] = y
                if copy_dtype is not None:
                    out_refs[-1][rows, :] = y.astype(copy_dtype)
                return
            if kind == "loss":
                y, vjp = jax.vjp(_layer_norm, acc, rest[0][...], rest[1][...])
                err = y - rest[2][rows, :]
                part = 0.5 * jnp.sum(jnp.sum(err * err, axis=1, keepdims=True), axis=0, keepdims=True) / n
                dz, dg, db = vjp(err / n)
            else:
                _, vjp = jax.vjp(_layer_norm, rest[0][rows, :], rest[1][...], rest[2][...])
                dz, dg, db = vjp(acc)
            out_refs[0][rows, :] = dz
            out_refs[1][...] += dg
            out_refs[2][...] += db
            if kind == "loss":
                out_refs[3][...] += jnp.broadcast_to(part, (1, LANES))
            if copy_dtype is not None:
                out_refs[-1][rows, :] = dz.astype(copy_dtype)

        if kind in ("loss", "bwd"):
            @pl.when((i == 0) & (kk == 0))
            def _():
                for r in out_refs[1:3 + (kind == "loss")]:
                    r[...] = jnp.zeros_like(r)

        def chunk(ref, c0, last):
            if ref.ndim == 3:
                g = ref.shape[2]
                return ref[c0 // g, :, pl.ds(c0 % g, kc)]
            return ref[:, pl.ds(c0, kc)] if last else ref[pl.ds(c0, kc), :]

        if mode == "tn" or kc == k:
            prod = _dg(a_ref[...], b_ref[...], ca, cb)
        else:
            prod = None
            for c0 in range(0, k, kc):
                part = _dg(chunk(a_ref, c0, True), chunk(b_ref, c0, mode == "nt"), ca, cb)
                prod = part if prod is None else prod + part
        if gk == 1:
            epilogue(prod)
            return

        @pl.when(kk == 0)
        def _():
            acc_ref[...] = prod

        @pl.when(kk > 0)
        def _():
            acc_ref[...] += prod

        @pl.when(kk == gk - 1)
        def _():
            epilogue(acc_ref[...])

    vmem = (2 * (_nbytes((tm, tk), a.dtype) + _nbytes((tk, tn), b.dtype))
            + (2 * len(ins) + 2 * len(out_shape) + 1) * _nbytes((tm, tn), F32))
    outs = _pcall(
        body, name=name, grid=(gi, gj, gk) if i_outer else (gj, gi, gk), in_specs=in_specs, out_specs=out_specs,
        out_shape=out_shape, scratch_shapes=[pltpu.VMEM((tm, tn) if gk > 1 else (SUBLANES, LANES), F32)],
        compiler_params=_params(("arbitrary", "arbitrary", "arbitrary"), vmem),
    )(*ins)
    return outs[0] if (kind is None and copy_dtype is None) else outs


def _attn_head(q, k, v):
    sc = mm_nt(q, k) * (CA_DH ** -0.5)
    e = jnp.exp(sc - jnp.max(sc, axis=-1, keepdims=True))
    return mm_nn(e / jnp.sum(e, axis=-1, keepdims=True), v)


def _attn_fwd(q, kv):
    seq, n_mem = q.shape[0], kv.shape[0]
    tq = _tile(seq, (512, 256, 128))

    def body(q_ref, kv_ref, o_ref):
        for h in range(HEADS):
            hd = pl.ds(h * CA_DH, CA_DH)
            o = _attn_head(q_ref[:, hd], kv_ref[:, hd], kv_ref[:, pl.ds(D_MODEL + h * CA_DH, CA_DH)])
            o_ref[:, hd] = o.astype(BF16)

    return _pcall(
        body, name="attn_fwd", grid=(seq // tq,),
        in_specs=[pl.BlockSpec((tq, D_MODEL), lambda i: (i, 0)), pl.BlockSpec((n_mem, 2 * D_MODEL), lambda i: (0, 0))],
        out_specs=pl.BlockSpec((tq, D_MODEL), lambda i: (i, 0)), out_shape=jax.ShapeDtypeStruct((seq, D_MODEL), BF16),
        compiler_params=_params(("arbitrary",), 4 * _nbytes((tq, D_MODEL), F32) + 2 * _nbytes((n_mem, 2 * D_MODEL), F32)),
    )(q, kv)


def _attn_bwd(q, kv, do):
    seq, n_mem = q.shape[0], kv.shape[0]
    tq = _tile(seq, (512, 256, 128))

    def body(q_ref, kv_ref, do_ref, dq_ref, dkv_ref):
        @pl.when(pl.program_id(0) == 0)
        def _():
            dkv_ref[...] = jnp.zeros_like(dkv_ref)

        for h in range(HEADS):
            hd = pl.ds(h * CA_DH, CA_DH)
            vd = pl.ds(D_MODEL + h * CA_DH, CA_DH)
            _, vjp = jax.vjp(_attn_head, q_ref[:, hd], kv_ref[:, hd], kv_ref[:, vd])
            dq, dk, dv = vjp(do_ref[:, hd].astype(F32))
            dq_ref[:, hd] = dq.astype(BF16)
            dkv_ref[:, hd] += dk
            dkv_ref[:, vd] += dv

    return _pcall(
        body, name="attn_bwd", grid=(seq // tq,),
        in_specs=[pl.BlockSpec((tq, D_MODEL), lambda i: (i, 0)), pl.BlockSpec((n_mem, 2 * D_MODEL), lambda i: (0, 0)),
                  pl.BlockSpec((tq, D_MODEL), lambda i: (i, 0))],
        out_specs=[pl.BlockSpec((tq, D_MODEL), lambda i: (i, 0)), pl.BlockSpec((n_mem, 2 * D_MODEL), lambda i: (0, 0))],
        out_shape=[jax.ShapeDtypeStruct((seq, D_MODEL), BF16), jax.ShapeDtypeStruct((n_mem, 2 * D_MODEL), F32)],
        compiler_params=_params(("arbitrary",), 6 * _nbytes((tq, D_MODEL), F32) + 4 * _nbytes((n_mem, 2 * D_MODEL), F32)),
    )(q, kv, do)


def _ffn_mid(hg, xg, hv, xv, wg0, wg1, wg2, bg, wv0, wv1, wv2, bv):
    return jax.nn.gelu(causal_conv(hg, xg, (wg0, wg1, wg2), bg)) * causal_conv(hv, xv, (wv0, wv1, wv2), bv)


FFN_TB = 256
FFN_W = D_FF // 2
FFN_J = D_FF // FFN_W
MXU_COLS = 256
FFN_PIECES = tuple((off, min(MXU_COLS, FFN_W - off)) for off in range(0, FFN_W, MXU_COLS))


def _ffn_common_specs(seq, row):
    tb = min(FFN_TB, seq)
    full = pl.BlockSpec((tb, D_MODEL), lambda t, j: (row(t), 0))
    vec = pl.BlockSpec((1, D_MODEL), lambda t, j: (0, 0))
    halves = []
    for off in (0, FFN_J):
        halves.append(dict(
            w_up=pl.BlockSpec((D_MODEL, FFN_W), lambda t, j, off=off: (0, j + off)),
            taps=pl.BlockSpec((FFN_CONV, FFN_W), lambda t, j, off=off: (0, j + off)),
            bias=pl.BlockSpec((1, FFN_W), lambda t, j, off=off: (0, j + off))))
    w_down = pl.BlockSpec((FFN_W, D_MODEL), lambda t, j: (j, 0))
    u_blk = pl.BlockSpec((2, tb, FFN_W), lambda t, j: (0, row(t), j))
    return tb, full, vec, halves, w_down, u_blk


def _ffn_vmem(tb):
    return (_nbytes((2, tb, FFN_W), F32) + _nbytes((2, tb, FFN_W), BF16) + 3 * _nbytes((D_MODEL, FFN_W), BF16)
            + 10 * _nbytes((tb, D_MODEL), F32))


def _conv_params(taps_ref, bias_ref, cols):
    return taps_ref[0:1, cols], taps_ref[1:2, cols], taps_ref[2:3, cols], bias_ref[:, cols]


def _ffn_fwd(x2b, x2, w_up, conv_w, conv_b, w_down, ln_g, ln_b, target):
    seq = x2.shape[0]
    tb, full, vec, halves, wd_spec, u_blk = _ffn_common_specs(seq, lambda t: t)
    nt = seq // tb

    def body(xb_ref, wg_ref, wv_ref, tg_ref, tv_ref, bg_ref, bv_ref, wd_ref, x_ref, g_ref, b_ref, tgt_ref,
             u_ref, h_ref, dz_ref, dg_ref, db_ref, loss_ref, dzb_ref, acc, carry):
        t, j = pl.program_id(0), pl.program_id(1)
        xb = xb_ref[...]
        pieces = [pl.ds(off, width) for off, width in FFN_PIECES]
        ug = [_dg(xb, wg_ref[:, cols], 1, 0) for cols in pieces]
        uv = [_dg(xb, wv_ref[:, cols], 1, 0) for cols in pieces]
        hs = []
        for cols, g, v in zip(pieces, ug, uv):
            u_ref[0, :, cols] = g
            u_ref[1, :, cols] = v
            halo_g = jnp.where(t == 0, 0.0, carry[j, 0, :, cols])
            halo_v = jnp.where(t == 0, 0.0, carry[j, 1, :, cols])
            h = _ffn_mid(halo_g, g, halo_v, v, *_conv_params(tg_ref, bg_ref, cols),
                         *_conv_params(tv_ref, bv_ref, cols)).astype(BF16)
            carry[j, 0, :, cols] = g[tb - SUBLANES:, :]
            carry[j, 1, :, cols] = v[tb - SUBLANES:, :]
            h_ref[:, cols] = h
            hs.append(h)
        part = None
        for cols, h in zip(pieces, hs):
            p = _dg(h, wd_ref[cols, :], 1, 0)
            part = p if part is None else part + p

        @pl.when(j == 0)
        def _():
            acc[...] = part

        @pl.when(j > 0)
        def _():
            acc[...] += part

        @pl.when(j == FFN_J - 1)
        def _():
            y, vjp = jax.vjp(_layer_norm, acc[...] + ALPHA * x_ref[...], g_ref[...], b_ref[...])
            err = y - tgt_ref[...]
            part_loss = 0.5 * jnp.sum(jnp.sum(err * err, axis=1, keepdims=True), axis=0, keepdims=True) / D_MODEL
            dz, dg, db = vjp(err / D_MODEL)

            @pl.when(t == 0)
            def _():
                for r in (dg_ref, db_ref, loss_ref):
                    r[...] = jnp.zeros_like(r)

            dz_ref[...] = dz
            dzb_ref[...] = dz.astype(BF16)
            dg_ref[...] += dg
            db_ref[...] += db
            loss_ref[...] += jnp.broadcast_to(part_loss, (1, LANES))

    h0, h1 = halves
    row = jax.ShapeDtypeStruct((1, D_MODEL), F32)
    return _pcall(
        body, name="ffn_fwd", grid=(nt, FFN_J),
        in_specs=[full, h0["w_up"], h1["w_up"], h0["taps"], h1["taps"], h0["bias"], h1["bias"], wd_spec, full, vec, vec,
                  full],
        out_specs=[u_blk, pl.BlockSpec((tb, FFN_W), lambda t, j: (t, j)), full, vec, vec,
                   pl.BlockSpec((1, LANES), lambda t, j: (0, 0)), full],
        out_shape=[jax.ShapeDtypeStruct((2, seq, D_FF), F32), jax.ShapeDtypeStruct((seq, D_FF), BF16),
                   jax.ShapeDtypeStruct((seq, D_MODEL), F32), row, row, jax.ShapeDtypeStruct((1, LANES), F32),
                   jax.ShapeDtypeStruct((seq, D_MODEL), BF16)],
        scratch_shapes=[pltpu.VMEM((tb, D_MODEL), F32), pltpu.VMEM((FFN_J, 2, SUBLANES, FFN_W), F32)],
        compiler_params=_params(("arbitrary", "arbitrary"), _ffn_vmem(tb)),
    )(x2b, w_up, w_up, conv_w, conv_w, conv_b, conv_b, w_down, x2, ln_g, ln_b, target)


def _ffn_bwd(u, conv_w, conv_b, dz3b, dz3, w_down, w_up, z2, ln_g, ln_b):
    seq = dz3.shape[0]
    tb = min(FFN_TB, seq)
    nt = seq // tb
    row8 = tb // SUBLANES
    tb, full, vec, halves, wd_spec, u_blk = _ffn_common_specs(seq, lambda t: nt - 1 - t)
    halo = pl.BlockSpec((2, SUBLANES, FFN_W), lambda t, j: (0, jnp.maximum((nt - 1 - t) * row8 - 1, 0), j))

    def body(u_ref, halo_ref, tg_ref, tv_ref, bg_ref, bv_ref, dzb_ref, wd_ref, wg_ref, wv_ref, dz3_ref, z_ref, g_ref,
             b_ref, du_ref, dw_ref, dbias_ref, dz_ref, dg_ref, db_ref, dz2b_ref, acc, carry):
        t, j = pl.program_id(0), pl.program_id(1)

        @pl.when((t == 0) & (j == 0))
        def _():
            for r in (dw_ref, dbias_ref, dg_ref, db_ref):
                r[...] = jnp.zeros_like(r)

        pieces = [pl.ds(off, width) for off, width in FFN_PIECES]
        dzb = dzb_ref[...]
        dhs = [_dg(dzb, wd_ref[cols, :], 1, 1) for cols in pieces]
        first = t == nt - 1
        dus = []
        for cols, dh in zip(pieces, dhs):
            args = (jnp.where(first, 0.0, halo_ref[0, :, cols]), u_ref[0, :, cols],
                    jnp.where(first, 0.0, halo_ref[1, :, cols]), u_ref[1, :, cols],
                    *_conv_params(tg_ref, bg_ref, cols), *_conv_params(tv_ref, bv_ref, cols))
            _, vjp = jax.vjp(_ffn_mid, *args)
            dhg, dxg, dhv, dxv, g0, g1, g2, gb, v0, v1, v2, vb = vjp(dh)
            zeros = jnp.zeros((tb - SUBLANES, dh.shape[1]), F32)
            dug = (dxg + jnp.concatenate([zeros, jnp.where(t == 0, 0.0, carry[j, 0, :, cols])], axis=0)).astype(BF16)
            duv = (dxv + jnp.concatenate([zeros, jnp.where(t == 0, 0.0, carry[j, 1, :, cols])], axis=0)).astype(BF16)
            carry[j, 0, :, cols] = dhg
            carry[j, 1, :, cols] = dhv
            du_ref[0, :, cols] = dug
            du_ref[1, :, cols] = duv
            for half, parts in enumerate(((g0, g1, g2), (v0, v1, v2))):
                for d, p in enumerate(parts):
                    dw_ref[j, half, d:d + 1, cols] += p
            dbias_ref[j, 0, :, cols] += gb
            dbias_ref[j, 1, :, cols] += vb
            dus.append((dug, duv))
        part = None
        for cols, (dug, duv) in zip(pieces, dus):
            p = _dg(dug, wg_ref[:, cols], 1, 1) + _dg(duv, wv_ref[:, cols], 1, 1)
            part = p if part is None else part + p

        @pl.when(j == 0)
        def _():
            acc[...] = part

        @pl.when(j > 0)
        def _():
            acc[...] += part

        @pl.when(j == FFN_J - 1)
        def _():
            _, ln_vjp = jax.vjp(_layer_norm, z_ref[...], g_ref[...], b_ref[...])
            dz, dg, db = ln_vjp(acc[...] + ALPHA * dz3_ref[...])
            dz_ref[...] = dz
            dz2b_ref[...] = dz.astype(BF16)
            dg_ref[...] += dg
            db_ref[...] += db

    h0, h1 = halves
    row = jax.ShapeDtypeStruct((1, D_MODEL), F32)
    whole = lambda *shape: pl.BlockSpec(shape, lambda t, j: (0,) * len(shape))
    return _pcall(
        body, name="ffn_bwd", grid=(nt, FFN_J),
        in_specs=[u_blk, halo, h0["taps"], h1["taps"], h0["bias"], h1["bias"], full, wd_spec, h0["w_up"], h1["w_up"],
                  full, full, vec, vec],
        out_specs=[u_blk, whole(FFN_J, 2, FFN_CONV, FFN_W), whole(FFN_J, 2, 1, FFN_W), full, vec, vec, full],
        out_shape=[jax.ShapeDtypeStruct((2, seq, D_FF), BF16), jax.ShapeDtypeStruct((FFN_J, 2, FFN_CONV, FFN_W), F32),
                   jax.ShapeDtypeStruct((FFN_J, 2, 1, FFN_W), F32), jax.ShapeDtypeStruct((seq, D_MODEL), F32), row, row,
                   jax.ShapeDtypeStruct((seq, D_MODEL), BF16)],
        scratch_shapes=[pltpu.VMEM((tb, D_MODEL), F32), pltpu.VMEM((FFN_J, 2, SUBLANES, FFN_W), F32)],
        compiler_params=_params(("arbitrary", "arbitrary"), _ffn_vmem(tb)),
    )(u, u, conv_w, conv_w, conv_b, conv_b, dz3b, w_down, w_up, w_up, dz3, z2, ln_g, ln_b)


def _adamw_math(w, g, m, v):
    m_new = ADAM_B1 * m + (1.0 - ADAM_B1) * g
    v_new = ADAM_B2 * v + (1.0 - ADAM_B2) * jnp.square(g)
    m_hat = m_new / (1.0 - ADAM_B1 ** ADAM_STEP)
    v_hat = v_new / (1.0 - ADAM_B2 ** ADAM_STEP)
    return -ADAM_LR * (m_hat / (jnp.sqrt(v_hat) + ADAM_EPS) + ADAM_WD * w), m_new, v_new


def _adamw(name, w, g, m, v):
    rows, cols = w.shape
    tr = _tile(rows, (256, 176, 128, 64, 40, 32, 16, 8))

    def body(w_ref, g_ref, m_ref, v_ref, d_ref, nm_ref, nv_ref):
        d_ref[...], nm_ref[...], nv_ref[...] = _adamw_math(w_ref[...], g_ref[...], m_ref[...], v_ref[...])

    spec = pl.BlockSpec((tr, cols), lambda i: (i, 0))
    sh = jax.ShapeDtypeStruct((rows, cols), F32)
    return _pcall(
        body, name=name, grid=(rows // tr,), in_specs=[spec] * 4, out_specs=[spec] * 3, out_shape=[sh] * 3,
        compiler_params=_params(("arbitrary",), 14 * _nbytes((tr, -(-cols // LANES) * LANES), F32)),
    )(w, g, m, v)


def _adamw_halves(name, core, w, mine, theirs, m, v):
    rows, cols = w.shape
    half_rows = mine.shape[0]
    tr = _tile(half_rows, (256, 176, 128))
    nbh = half_rows // tr
    assert 2 * half_rows == rows

    def body(c_ref, w_ref, a_ref, b_ref, m_ref, v_ref, g_ref, d_ref, nm_ref, nv_ref):
        g = jnp.where(pl.program_id(0) // nbh == c_ref[0], a_ref[...], b_ref[...])
        g_ref[...] = g
        d_ref[...], nm_ref[...], nv_ref[...] = _adamw_math(w_ref[...], g, m_ref[...], v_ref[...])

    spec = pl.BlockSpec((tr, cols), lambda i, c_ref: (i, 0))
    half = pl.BlockSpec((tr, cols), lambda i, c_ref: (i % nbh, 0))
    sh = jax.ShapeDtypeStruct((rows, cols), F32)
    grid_spec = pltpu.PrefetchScalarGridSpec(
        num_scalar_prefetch=1, grid=(rows // tr,), in_specs=[spec, half, half, spec, spec], out_specs=[spec] * 4)
    return _pcall(
        body, name=name, grid_spec=grid_spec, out_shape=[sh] * 4,
        compiler_params=_params(("arbitrary",), 18 * _nbytes((tr, -(-cols // LANES) * LANES), F32)),
    )(core, w, mine, theirs, m, v)


MESH = pl.DeviceIdType.MESH
ANY = pl.BlockSpec(memory_space=pl.ANY)
N_CHIPS = 4
N_DEV = 8
BF16_ROWS = 16


def _me():
    return lax.axis_index("x"), lax.axis_index("y"), lax.axis_index("c")


def _other_chips(x, y):
    return [(1 - x, y), (x, 1 - y), (1 - x, 1 - y)]


def _remote(src, dst, ssem, rsem, dev):
    return pltpu.make_async_remote_copy(src_ref=src, dst_ref=dst, send_sem=ssem, recv_sem=rsem,
                                        device_id=dev, device_id_type=MESH)


def _half_rows(ref_rows, cc):
    half = ref_rows // 2
    return pl.ds(pl.multiple_of(cc * half, BF16_ROWS), half)


def _gather_weights(shards):
    n = len(shards)
    n_ici = n * (N_CHIPS - 1)

    def body(*refs):
        ins, outs, (ssem, rsem, lsem, lrsem) = refs[:n], refs[n:2 * n], refs[2 * n:]
        x, y, c = _me()
        k_me = 2 * x + y
        sib = (x, y, 1 - c)
        chips = _other_chips(x, y)
        started = []
        for i, (w_ref, o_ref) in enumerate(zip(ins, outs)):
            cp = _remote(w_ref, o_ref.at[k_me], lsem.at[i], lrsem.at[i], sib)
            cp.start()
            started.append(cp)
        for r, (px, py) in enumerate(chips):
            for i, (w_ref, o_ref) in enumerate(zip(ins, outs)):
                rows = _half_rows(w_ref.shape[0], c)
                s = r * n + i
                cp = _remote(w_ref.at[rows], o_ref.at[k_me, rows], ssem.at[s], rsem.at[s], (px, py, c))
                cp.start()
                started.append(cp)
        for r, (px, py) in enumerate(chips):
            for i, o_ref in enumerate(outs):
                blk = o_ref.at[2 * px + py, _half_rows(o_ref.shape[1], c)]
                s = r * n + i
                _remote(blk, blk, ssem.at[s], rsem.at[s], (px, py, c)).wait_recv()
                cp = _remote(blk, blk, ssem.at[n_ici + s], rsem.at[n_ici + s], sib)
                cp.start()
                started.append(cp)
        for r, (px, py) in enumerate(chips):
            for i, o_ref in enumerate(outs):
                blk = o_ref.at[2 * px + py, _half_rows(o_ref.shape[1], 1 - c)]
                s = n_ici + r * n + i
                _remote(blk, blk, ssem.at[s], rsem.at[s], sib).wait_recv()
        for cp in started[n:]:
            cp.wait_send()
        for cp in started[:n]:
            cp.wait()

    return _pcall(
        body, name="gather_weights", in_specs=[ANY] * n, out_specs=[ANY] * n,
        out_shape=[jax.ShapeDtypeStruct((N_CHIPS,) + s.shape, s.dtype) for s in shards],
        scratch_shapes=[pltpu.SemaphoreType.DMA((2 * n_ici,)), pltpu.SemaphoreType.DMA((2 * n_ici,)),
                        pltpu.SemaphoreType.DMA((n,)), pltpu.SemaphoreType.DMA((n,))],
    )(*shards)


def _swap_halves(name, grads):
    n = len(grads)

    def body(*refs):
        ins, outs, (ssem, rsem) = refs[:n], refs[n:2 * n], refs[2 * n:]
        x, y, c = _me()
        copies = []
        for i, (g_ref, o_ref) in enumerate(zip(ins, outs)):
            for k in range(N_CHIPS):
                s = i * N_CHIPS + k
                cp = _remote(g_ref.at[k, _half_rows(g_ref.shape[1], 1 - c)], o_ref.at[k], ssem.at[s], rsem.at[s],
                             (x, y, 1 - c))
                cp.start()
                copies.append(cp)
        for cp in copies:
            cp.wait()

    return _pcall(
        body, name=name, in_specs=[ANY] * n, out_specs=[ANY] * n,
        out_shape=[jax.ShapeDtypeStruct((N_CHIPS, g.shape[1] // 2, g.shape[2]), g.dtype) for g in grads],
        scratch_shapes=[pltpu.SemaphoreType.DMA((n * N_CHIPS,)), pltpu.SemaphoreType.DMA((n * N_CHIPS,))],
    )(*grads)


SEM = pl.BlockSpec(memory_space=pltpu.SEMAPHORE)
IN_HBM = pl.BlockSpec(memory_space=pltpu.HBM)
SPLIT_PARAMS = dict(compiler_params=pltpu.CompilerParams(has_side_effects=pltpu.SideEffectType.DATAFLOW_SIDE_EFFECTING))


def _swap_start(name, grads):
    n = len(grads)
    n_sem = n * N_CHIPS

    def body(*refs):
        ins, lands, (ssem, rsem), token = refs[:n], refs[n:2 * n], refs[2 * n:2 * n + 2], refs[-1]
        x, y, c = _me()
        for i, (g_ref, l_ref) in enumerate(zip(ins, lands)):
            for k in range(N_CHIPS):
                s = i * N_CHIPS + k
                _remote(g_ref.at[k, _half_rows(g_ref.shape[1], 1 - c)], l_ref.at[k], ssem.at[s], rsem.at[s],
                        (x, y, 1 - c)).start()
        token[...] = jnp.zeros_like(token)

    src = [pltpu.HBM(g.shape, g.dtype) for g in grads]
    dst = [pltpu.HBM((N_CHIPS, g.shape[1] // 2, g.shape[2]), g.dtype) for g in grads]
    outs = _call(
        body, name=name, in_specs=[IN_HBM] * (2 * n),
        out_specs=[SEM, SEM] + [IN_HBM] * (2 * n) + [pl.BlockSpec(memory_space=pltpu.VMEM)],
        out_shape=[pltpu.SemaphoreType.DMA((n_sem,)), pltpu.SemaphoreType.DMA((n_sem,))] + src + dst
        + [jax.ShapeDtypeStruct((SUBLANES, LANES), F32)],
        input_output_aliases={i: 2 + i for i in range(2 * n)}, **SPLIT_PARAMS,
    )(*[pltpu.with_memory_space_constraint(g, pltpu.HBM) for g in grads],
      *[pltpu.with_memory_space_constraint(lax.empty(d.shape, d.dtype), pltpu.HBM) for d in dst])
    return outs[:-1], outs[-1]


def _swap_wait(name, handle, after):
    ssem, rsem, thru = handle[0], handle[1], handle[2:]
    n = len(thru) // 2

    def body(*refs):
        ins, lands, (ssem_ref, rsem_ref) = refs[:n], refs[n:2 * n], refs[2 * n:2 * n + 2]
        x, y, c = _me()
        for i, (g_ref, l_ref) in enumerate(zip(ins, lands)):
            for k in range(N_CHIPS):
                s = i * N_CHIPS + k
                cp = _remote(g_ref.at[k, _half_rows(g_ref.shape[1], 1 - c)], l_ref.at[k], ssem_ref.at[s],
                             rsem_ref.at[s], (x, y, 1 - c))
                cp.wait_send()
                cp.wait_recv()

    outs = _call(
        body, name=name, in_specs=[IN_HBM] * (2 * n) + [SEM, SEM, ANY], out_specs=[IN_HBM] * (2 * n),
        out_shape=[pltpu.HBM(t.shape, t.dtype) for t in thru],
        input_output_aliases={i: i for i in range(2 * n)}, **SPLIT_PARAMS,
    )(*thru, ssem, rsem, after)
    return outs[:n], outs[n:]


def _gather_start(name, shards):
    n = len(shards)
    n_sem = n * N_CHIPS

    def body(*refs):
        ins, lands, (ssem, rsem), token = refs[:n], refs[n:2 * n], refs[2 * n:2 * n + 2], refs[-1]
        x, y, c = _me()
        k_me = 2 * x + y
        for i, (w_ref, l_ref) in enumerate(zip(ins, lands)):
            _remote(w_ref, l_ref.at[k_me], ssem.at[i], rsem.at[i], (x, y, 1 - c)).start()
        for r, (px, py) in enumerate(_other_chips(x, y)):
            for i, (w_ref, l_ref) in enumerate(zip(ins, lands)):
                rows = _half_rows(w_ref.shape[0], c)
                s = (r + 1) * n + i
                _remote(w_ref.at[rows], l_ref.at[k_me, rows], ssem.at[s], rsem.at[s], (px, py, c)).start()
        token[...] = jnp.zeros_like(token)

    src = [pltpu.HBM(s.shape, s.dtype) for s in shards]
    dst = [pltpu.HBM((N_CHIPS,) + s.shape, s.dtype) for s in shards]
    outs = _call(
        body, name=name, in_specs=[IN_HBM] * (2 * n),
        out_specs=[SEM, SEM] + [IN_HBM] * (2 * n) + [pl.BlockSpec(memory_space=pltpu.VMEM)],
        out_shape=[pltpu.SemaphoreType.DMA((n_sem,)), pltpu.SemaphoreType.DMA((n_sem,))] + src + dst
        + [jax.ShapeDtypeStruct((SUBLANES, LANES), F32)],
        input_output_aliases={i: 2 + i for i in range(2 * n)}, **SPLIT_PARAMS,
    )(*[pltpu.with_memory_space_constraint(s, pltpu.HBM) for s in shards],
      *[pltpu.with_memory_space_constraint(lax.empty(d.shape, d.dtype), pltpu.HBM) for d in dst])
    return outs[:-1], outs[-1]


def _gather_wait(name, handle, after):
    ssem, rsem, thru = handle[0], handle[1], handle[2:]
    n = len(thru) // 2

    def body(*refs):
        ins, lands, (ssem_ref, rsem_ref) = refs[:n], refs[n:2 * n], refs[2 * n:2 * n + 2]
        x, y, c = _me()
        k_me = 2 * x + y
        for i, (w_ref, l_ref) in enumerate(zip(ins, lands)):
            cp = _remote(w_ref, l_ref.at[k_me], ssem_ref.at[i], rsem_ref.at[i], (x, y, 1 - c))
            cp.wait_send()
            cp.wait_recv()
        for r, (px, py) in enumerate(_other_chips(x, y)):
            for i, (w_ref, l_ref) in enumerate(zip(ins, lands)):
                rows = _half_rows(w_ref.shape[0], c)
                s = (r + 1) * n + i
                cp = _remote(w_ref.at[rows], l_ref.at[2 * px + py, rows], ssem_ref.at[s], rsem_ref.at[s], (px, py, c))
                cp.wait_send()
                cp.wait_recv()

    outs = _call(
        body, name=name, in_specs=[IN_HBM] * (2 * n) + [SEM, SEM, ANY], out_specs=[IN_HBM] * (2 * n),
        out_shape=[pltpu.HBM(t.shape, t.dtype) for t in thru],
        input_output_aliases={i: i for i in range(2 * n)}, **SPLIT_PARAMS,
    )(*thru, ssem, rsem, after)
    return outs[n:]


def _forward_halves(name, blocks):
    n = len(blocks)
    n_sem = n * (N_CHIPS - 1)

    def body(*refs):
        outs, (ssem, rsem) = refs[n:2 * n], refs[2 * n:]
        x, y, c = _me()
        sib = (x, y, 1 - c)
        chips = _other_chips(x, y)
        sends = []
        for r, (px, py) in enumerate(chips):
            for i, o_ref in enumerate(outs):
                blk = o_ref.at[2 * px + py, _half_rows(o_ref.shape[1], c)]
                cp = _remote(blk, blk, ssem.at[r * n + i], rsem.at[r * n + i], sib)
                cp.start()
                sends.append(cp)
        for r, (px, py) in enumerate(chips):
            for i, o_ref in enumerate(outs):
                blk = o_ref.at[2 * px + py, _half_rows(o_ref.shape[1], 1 - c)]
                _remote(blk, blk, ssem.at[r * n + i], rsem.at[r * n + i], sib).wait_recv()
        for cp in sends:
            cp.wait_send()

    return _pcall(
        body, name=name, in_specs=[ANY] * n, out_specs=[ANY] * n,
        out_shape=[jax.ShapeDtypeStruct(b.shape, b.dtype) for b in blocks],
        input_output_aliases={i: i for i in range(n)},
        scratch_shapes=[pltpu.SemaphoreType.DMA((n_sem,)), pltpu.SemaphoreType.DMA((n_sem,))],
    )(*blocks)


def _scatter_start(name, parts):
    n = len(parts)
    n_sem = n * (N_CHIPS - 1)

    def body(*refs):
        ins, lands, (ssem, rsem), token = refs[:n], refs[n:2 * n], refs[2 * n:2 * n + 2], refs[-1]
        x, y, c = _me()
        k_me = 2 * x + y
        for r, (px, py) in enumerate(_other_chips(x, y)):
            for i, (p_ref, l_ref) in enumerate(zip(ins, lands)):
                s = r * n + i
                _remote(p_ref.at[2 * px + py], l_ref.at[k_me], ssem.at[s], rsem.at[s], (px, py, c)).start()
        token[...] = jnp.zeros_like(token)

    hbm = [pltpu.HBM(p.shape, p.dtype) for p in parts]
    outs = _call(
        body, name=name, in_specs=[IN_HBM] * (2 * n),
        out_specs=[SEM, SEM] + [IN_HBM] * (2 * n) + [pl.BlockSpec(memory_space=pltpu.VMEM)],
        out_shape=[pltpu.SemaphoreType.DMA((n_sem,)), pltpu.SemaphoreType.DMA((n_sem,))] + hbm + hbm
        + [jax.ShapeDtypeStruct((SUBLANES, LANES), F32)],
        input_output_aliases={i: 2 + i for i in range(2 * n)}, **SPLIT_PARAMS,
    )(*[pltpu.with_memory_space_constraint(p, pltpu.HBM) for p in parts],
      *[pltpu.with_memory_space_constraint(lax.empty(p.shape, p.dtype), pltpu.HBM) for p in parts])
    return outs[:-1], outs[-1]


def _scatter_wait(name, handle, after):
    ssem, rsem, thru = handle[0], handle[1], handle[2:]
    n = len(thru) // 2

    def body(*refs):
        ins, lands, (ssem_ref, rsem_ref) = refs[:n], refs[n:2 * n], refs[2 * n:2 * n + 2]
        x, y, c = _me()
        for r, (px, py) in enumerate(_other_chips(x, y)):
            for i, (p_ref, l_ref) in enumerate(zip(ins, lands)):
                s = r * n + i
                cp = _remote(p_ref.at[2 * px + py], l_ref.at[2 * px + py], ssem_ref.at[s], rsem_ref.at[s], (px, py, c))
                cp.wait_send()
                cp.wait_recv()

    outs = _call(
        body, name=name, in_specs=[IN_HBM] * (2 * n) + [SEM, SEM, ANY], out_specs=[IN_HBM] * (2 * n),
        out_shape=[pltpu.HBM(t.shape, t.dtype) for t in thru],
        input_output_aliases={i: i for i in range(2 * n)}, **SPLIT_PARAMS,
    )(*thru, ssem, rsem, after)
    return outs[n:]


def _share_halves(halves):
    n = len(halves)

    def body(*refs):
        ins, outs, (ssem, rsem) = refs[:n], refs[n:2 * n], refs[2 * n:]
        x, y, c = _me()
        copies = [_remote(r_ref, o_ref, ssem.at[i], rsem.at[i], (x, y, 1 - c))
                  for i, (r_ref, o_ref) in enumerate(zip(ins, outs))]
        for cp in copies:
            cp.start()
        for cp in copies:
            cp.wait()

    return _pcall(
        body, name="share_halves", in_specs=[ANY] * n, out_specs=[ANY] * n,
        out_shape=[jax.ShapeDtypeStruct(h.shape, h.dtype) for h in halves],
        scratch_shapes=[pltpu.SemaphoreType.DMA((n,)), pltpu.SemaphoreType.DMA((n,))],
    )(*halves)


def _exchange_small(v, reduce):
    rows = v.shape[0]

    def body(v_ref, out_ref, buf, ssem, rsem):
        x, y, c = _me()
        me = 4 * x + 2 * y + c
        peers = [((x + bx) % 2, (y + by) % 2, (c + bc) % 2)
                 for bx in (0, 1) for by in (0, 1) for bc in (0, 1) if (bx, by, bc) != (0, 0, 0)]
        dst = buf if reduce else out_ref
        dst[me] = v_ref[...]
        sends = [_remote(v_ref, dst.at[me], ssem.at[r], rsem.at[r], p) for r, p in enumerate(peers)]
        for cp in sends:
            cp.start()
        for r, (px, py, pc) in enumerate(peers):
            blk = dst.at[4 * px + 2 * py + pc]
            _remote(blk, blk, ssem.at[r], rsem.at[r], (px, py, pc)).wait_recv()
        if reduce:
            acc = buf[0]
            for d in range(1, N_DEV):
                acc = acc + buf[d]
            out_ref[...] = acc
        for cp in sends:
            cp.wait_send()

    vm = pl.BlockSpec(memory_space=pltpu.VMEM)
    out_shape = jax.ShapeDtypeStruct((rows, LANES) if reduce else (N_DEV, rows, LANES), F32)
    buf_shape = (N_DEV, rows, LANES) if reduce else (SUBLANES, LANES)
    return _pcall(
        body, pin=False, name="reduce_small" if reduce else "gather_small", in_specs=[vm], out_specs=vm, out_shape=out_shape,
        scratch_shapes=[pltpu.VMEM(buf_shape, F32), pltpu.SemaphoreType.DMA((N_DEV - 1,)),
                        pltpu.SemaphoreType.DMA((N_DEV - 1,))],
        compiler_params=pltpu.CompilerParams(vmem_limit_bytes=32 * 1024 * 1024),
    )(v)


def _add_pair(name, core, g, theirs):
    _, half, cols = theirs.shape
    tr = _tile(half, (256, 176, 128))
    nb = half // tr

    def body(c_ref, g_ref, t_ref, o32_ref, o16_ref):
        s = g_ref[...] + t_ref[...]
        o32_ref[...] = s
        o16_ref[...] = s.astype(BF16)

    spec = pl.BlockSpec((None, tr, cols), lambda k, i, c_ref: (k, i, 0))
    grid_spec = pltpu.PrefetchScalarGridSpec(
        num_scalar_prefetch=1, grid=(N_CHIPS, nb),
        in_specs=[pl.BlockSpec((None, tr, cols), lambda k, i, c_ref: (k, c_ref[0] * nb + i, 0)), spec],
        out_specs=[spec, spec])
    return _pcall(
        body, name=name, grid_spec=grid_spec,
        out_shape=[jax.ShapeDtypeStruct(theirs.shape, F32), jax.ShapeDtypeStruct(theirs.shape, BF16)],
        compiler_params=_params(("arbitrary", "arbitrary"), 8 * _nbytes((tr, cols + LANES), F32)),
    )(core, g, theirs)


def _add_chips(name, chip, p32, recv):
    _, half, cols = p32.shape
    tr = _tile(half, (256, 176, 128))

    def body(k_ref, p_ref, r0_ref, r1_ref, r2_ref, o_ref):
        o_ref[...] = ((p_ref[...] + r0_ref[...].astype(F32)) + r1_ref[...].astype(F32)) + r2_ref[...].astype(F32)

    def other(r):
        return pl.BlockSpec((None, tr, cols), lambda i, k_ref: (r + (k_ref[0] <= r).astype(jnp.int32), i, 0))
    grid_spec = pltpu.PrefetchScalarGridSpec(
        num_scalar_prefetch=1, grid=(half // tr,),
        in_specs=[pl.BlockSpec((None, tr, cols), lambda i, k_ref: (k_ref[0], i, 0)), other(0), other(1), other(2)],
        out_specs=pl.BlockSpec((tr, cols), lambda i, k_ref: (i, 0)))
    return _pcall(
        body, name=name, grid_spec=grid_spec, out_shape=jax.ShapeDtypeStruct((half, cols), F32),
        compiler_params=_params(("arbitrary",), 10 * _nbytes((tr, cols + LANES), F32)),
    )(chip, p32, recv, recv, recv)


def kernel(x, mem, w_in, b_in, hg_lb_logits, hg_norm_w, ml_conv_w, ml_conv_b, ml_norm_w, w_out, ln1_g, ln1_b, ca_wq, ca_wkv, ca_wo, ln2_g, ln2_b, ffn_w_up, ffn_conv_w, ffn_conv_b, ffn_w_down, ln3_g, ln3_b, loss_target, m_w_in, m_b_in, m_hg_lb_logits, m_hg_norm_w, m_ml_conv_w, m_ml_conv_b, m_ml_norm_w, m_w_out, m_ln1_g, m_ln1_b, m_ca_wq, m_ca_wkv, m_ca_wo, m_ln2_g, m_ln2_b, m_ffn_w_up, m_ffn_conv_w, m_ffn_conv_b, m_ffn_w_down, m_ln3_g, m_ln3_b, v_w_in, v_b_in, v_hg_lb_logits, v_hg_norm_w, v_ml_conv_w, v_ml_conv_b, v_ml_norm_w, v_w_out, v_ln1_g, v_ln1_b, v_ca_wq, v_ca_wkv, v_ca_wo, v_ln2_g, v_ln2_b, v_ffn_w_up, v_ffn_conv_w, v_ffn_conv_b, v_ffn_w_down, v_ln3_g, v_ln3_b):
    return _train_step(dict(locals()))


WEIGHTS = ("w_in", "b_in", "hg_lb_logits", "hg_norm_w", "ml_conv_w", "ml_conv_b", "ml_norm_w", "w_out", "ln1_g",
           "ln1_b", "ca_wq", "ca_wkv", "ca_wo", "ln2_g", "ln2_b", "ffn_w_up", "ffn_conv_w", "ffn_conv_b",
           "ffn_w_down", "ln3_g", "ln3_b")
MATRICES = ("w_in", "w_out", "ca_wq", "ca_wkv", "ca_wo", "ffn_w_up", "ffn_w_down")
COL_SHARDED = ("w_in", "ca_wkv", "ffn_w_up", "ml_conv_w", "ffn_conv_w")
SMALL = tuple(n for n in WEIGHTS if n not in MATRICES)
PART_ROWS = 16


def _part_rows(shape, lead):
    n = 1
    for s in shape[lead:]:
        n *= s
    return -(-n // (LANES * PART_ROWS)) * PART_ROWS


def _pack(arrs, dtype, lead=0, rows=None):
    parts = []
    for a in arrs:
        head = a.shape[:lead]
        flat = a.reshape(head + (-1,)).astype(dtype)
        pad = _part_rows(a.shape, lead) * LANES - flat.shape[-1]
        flat = jnp.pad(flat, [(0, 0)] * lead + [(0, pad)])
        parts.append(flat.reshape(head + (-1, LANES)))
    used = sum(p.shape[lead] for p in parts)
    if rows is not None and rows > used:
        parts.append(jnp.zeros(parts[0].shape[:lead] + (rows - used, LANES), dtype))
    return jnp.concatenate(parts, axis=lead)


def _unpack(buf, shapes):
    lead = buf.shape[:-2]
    outs, r = [], 0
    for sh in shapes:
        n = 1
        for s in sh:
            n *= s
        nr = _part_rows(sh, 0)
        flat = buf[..., r:r + nr, :].reshape(lead + (nr * LANES,))
        outs.append(flat[..., :n].reshape(lead + tuple(sh)))
        r += nr
    return outs


def _cat_cols(s):
    return jnp.moveaxis(s, 0, 1).reshape(s.shape[1], -1)


def _stack_rows(s):
    return s.reshape(-1, s.shape[-1])


def _train_step(a):
    xs, mems, tgt = a["x"][0], a["mem"][0], a["loss_target"][0]
    core = lax.axis_index("c").astype(jnp.int32).reshape(1)
    chip = (2 * lax.axis_index("x") + lax.axis_index("y")).astype(jnp.int32).reshape(1)
    k_me = chip[0]
    shard = {n: a[n][0] for n in MATRICES}

    later = [n for n in MATRICES if n != "w_in"]
    taps = _exchange_small(_pack([a["ml_conv_w"][0], a["ffn_conv_w"][0]], F32), reduce=False)
    w = {"w_in": jnp.pad(_cat_cols(_gather_weights([shard["w_in"].astype(BF16)])[0]), ((0, 0), (0, D_IN_PAD - D_IN)))}
    gathering, token = _gather_start("gather_start", [shard[n].astype(BF16) for n in later])
    taps = taps.reshape((N_CHIPS, 2) + taps.shape[1:])[:, 0]
    ml_cw, ffn_cw = [_cat_cols(s) for s in _unpack(taps, [a["ml_conv_w"].shape[1:], a["ffn_conv_w"].shape[1:]])]
    b_in_p = jnp.pad(a["b_in"], ((0, 0), (0, D_IN_PAD - D_IN))) + token[0:1, 0:1]
    mixer_w = (a["hg_lb_logits"], a["hg_norm_w"], ml_cw, a["ml_conv_b"], a["ml_norm_w"])
    up_cols = a["ffn_w_up"].shape[-1]

    xb = xs.astype(BF16)
    proj = _mm("proj", "nn", xb, w["w_in"], bias=b_in_p, tm=256, tn=D_IN_PAD)
    y, hst, cst, nst, mst = _mixer_fwd(proj, *mixer_w)
    w.update(zip(later, _forward_halves("forward_halves", _gather_wait("gather_wait", gathering, y))))
    for n in ("w_out", "ca_wq", "ca_wo", "ffn_w_down"):
        w[n] = _stack_rows(w[n])
    z1, x1, x1b = _mm("mix_out", "nn", y, w["w_out"], res=xs, res_scale=ALPHA, ln=("fwd", a["ln1_g"], a["ln1_b"]),
                      copy_dtype=BF16)
    q = _mm("ca_q", "nn", x1b, w["ca_wq"], out_dtype=BF16, tn=D_MODEL)
    kv = _mm("ca_kv", "nn", mems, w["ca_wkv"])
    o = _attn_fwd(q, kv)
    z2, x2, x2b = _mm("ca_out", "nn", o, w["ca_wo"], res=x1, res_scale=ALPHA, ln=("fwd", a["ln2_g"], a["ln2_b"]),
                      copy_dtype=BF16)
    w_up = _cat_cols(w["ffn_w_up"])
    u, hmid, dz3, g_ln3g, g_ln3b, loss_part, dz3b = _ffn_fwd(
        x2b, x2, w_up, ffn_cw, a["ffn_conv_b"], w["ffn_w_down"], a["ln3_g"], a["ln3_b"], tgt)

    grads = {"ln3_g": g_ln3g, "ln3_b": g_ln3b}
    grads["ffn_w_down"] = _mm("g_w_down", "tn", hmid, dz3b, tm=D_FF // 2, tn=D_MODEL)
    du, g_cw, g_cb, dz2, grads["ln2_g"], grads["ln2_b"], dz2b = _ffn_bwd(
        u, ffn_cw, a["ffn_conv_b"], dz3b, dz3, w["ffn_w_down"], w_up, z2, a["ln2_g"], a["ln2_b"])
    grads["ffn_conv_w"] = jnp.transpose(g_cw, (2, 1, 0, 3)).reshape(FFN_CONV, 2 * D_FF)
    grads["ffn_conv_b"] = jnp.transpose(g_cb, (2, 1, 0, 3)).reshape(1, 2 * D_FF)
    grads["ffn_w_up"] = _mm("g_w_up", "tn", x2b, du, out_groups=N_CHIPS, tm=D_MODEL, tn=up_cols)
    grads["ffn_w_down"] = grads["ffn_w_down"].reshape((N_CHIPS,) + shard["ffn_w_down"].shape)
    pending = {}

    def reduce_start(tag, names, swapped=None):
        group = [grads[n] for n in names]
        group, theirs = swapped or (group, _swap_halves("swap_halves_" + tag, group))
        sums = [_add_pair("add_pair_" + n, core, g, t) for n, g, t in zip(names, group, theirs)]
        handle, token = _scatter_start("scatter_start_" + tag, [s16 for _, s16 in sums])
        pending[tag] = (names, [s32 for s32, _ in sums], handle)
        return token[0:1, 0:1]

    ffn = ("ffn_w_up", "ffn_w_down")
    swapping, token = _swap_start("swap_start_ffn", [grads[n] for n in ffn])
    do = _mm("d_o", "nt", dz2b, w["ca_wo"], bias=jnp.zeros((1, D_MODEL), F32) + token[0:1, 0:1], out_dtype=BF16,
             tn=D_MODEL)
    grads["ca_wo"] = _mm("g_wo", "tn", o, dz2b, tm=D_MODEL, tn=D_MODEL)
    zero = reduce_start("ffn", ffn, _swap_wait("swap_wait_ffn", swapping, grads["ca_wo"]))
    dq, dkv = _attn_bwd(q, kv + zero, do)
    grads["ca_wq"] = _mm("g_wq", "tn", x1b, dq, tm=D_MODEL, tn=D_MODEL)
    grads["ca_wkv"] = _mm("g_wkv", "tn", mems, dkv, out_groups=N_CHIPS, tm=D_MODEL)
    dz1, grads["ln1_g"], grads["ln1_b"], dz1b = _mm("d_x1", "nt", dq, w["ca_wq"], res=dz2, res_scale=ALPHA,
                                                    ln=("bwd", z1, a["ln1_g"], a["ln1_b"]), copy_dtype=BF16)
    dy = _mm("d_y", "nt", dz1b, w["w_out"], tn=D_MODEL)
    grads["w_out"] = _mm("g_w_out", "tn", y, dz1b, tm=D_MODEL, tn=D_MODEL)
    for n in ("w_out", "ca_wq", "ca_wo"):
        grads[n] = grads[n].reshape((N_CHIPS,) + shard[n].shape)
    zero = reduce_start("attn", ("w_out", "ca_wq", "ca_wkv", "ca_wo"))
    (dproj, g_b_in, grads["hg_lb_logits"], grads["hg_norm_w"], grads["ml_conv_w"], grads["ml_conv_b"],
     grads["ml_norm_w"]) = _mixer_bwd(proj, dy, hst, cst, nst, mst, mixer_w[0], mixer_w[1] + zero, *mixer_w[2:])
    g_in = _mm("g_w_in", "tn", xb, dproj, tm=D_MODEL, tn=up_cols)[:, :D_IN]
    grads["w_in"] = jnp.moveaxis(g_in.reshape(D_MODEL, N_CHIPS, -1), 1, 0)
    grads["b_in"] = g_b_in[:, :D_IN]
    zero = reduce_start("in", ("w_in",))
    dx = _mm("d_x", "nt", dproj, w["w_in"], bias=jnp.zeros((1, D_MODEL), F32) + zero, res=dz1, res_scale=ALPHA,
             tm=256, tn=D_MODEL)

    halves = {}
    for tag, (names, sums32, handle) in pending.items():
        for n, s32, r in zip(names, sums32, _scatter_wait("scatter_wait_" + tag, handle, dx)):
            halves[n] = _add_chips("add_chips_" + n, chip, s32, r)
    halves = [halves[n] for n in MATRICES]
    other_halves = _share_halves(halves)

    small_shapes = [grads[n].shape for n in SMALL] + [loss_part.shape]
    summed = _unpack(_exchange_small(_pack([grads[n] for n in SMALL] + [loss_part], F32), reduce=True), small_shapes)
    loss = summed[-1][0, 0]
    for n, g in zip(SMALL, summed[:-1]):
        if n in COL_SHARDED:
            cols = a[n].shape[-1]
            g = lax.dynamic_slice_in_dim(g, k_me * cols, cols, axis=1)
        grads[n] = g

    delta, new_m, new_v = {}, {}, {}
    for n, mine, theirs in zip(MATRICES, halves, other_halves):
        grads[n], delta[n], new_m[n], new_v[n] = _adamw_halves(
            "adamw_" + n, core, shard[n], mine, theirs, a["m_" + n][0], a["v_" + n][0])
    small_w = [a[n][0] if a[n].ndim == 3 else a[n] for n in SMALL]
    small_m = [a["m_" + n][0] if a[n].ndim == 3 else a["m_" + n] for n in SMALL]
    small_v = [a["v_" + n][0] if a[n].ndim == 3 else a["v_" + n] for n in SMALL]
    shapes = [w.shape for w in small_w]
    packed = [_pack(l, F32) for l in (small_w, [grads[n] for n in SMALL], small_m, small_v)]
    for out, buf in zip((delta, new_m, new_v), _adamw("adamw_small", *packed)):
        for n, v in zip(SMALL, _unpack(buf, shapes)):
            out[n] = v

    def shaped(d):
        return [d[n].reshape(a[n].shape) for n in WEIGHTS]
    return (loss, dx[None], *shaped(grads), *shaped(delta), *shaped(new_m), *shaped(new_v))
```

```python
import functools

import jax
import jax.numpy as jnp
from jax import lax
from jax.experimental import pallas as pl
from jax.experimental.pallas import tpu as pltpu

F32 = jnp.float32
BF16 = jnp.bfloat16

D_MODEL = 1024
HEADS = 4
DK = 128
D_GRP = HEADS * DK
CHUNK = 64
ML_CONV = 4
FFN_CONV = 3
D_FF = 2816
CA_DH = D_MODEL // HEADS
DEPTH = 1
ALPHA = (2.0 * DEPTH) ** 0.25
LN_EPS = 1e-5
NEG_BIG = -1e30
D_IN = 8 * D_GRP + 2 * HEADS
D_IN_PAD = 8 * D_GRP + 128
ADAM_LR, ADAM_B1, ADAM_B2, ADAM_EPS, ADAM_WD, ADAM_STEP = 0.001, 0.9, 0.999, 1e-08, 0.01, 10

SUBLANES = 8
LANES = 128
VMEM_BYTES = 64 * 1024 * 1024


def _pcall(body, pin=True, **kw):
    if not pin:
        return _call(body, **kw)
    kw["out_shape"] = jax.tree.map(lambda s: pltpu.HBM(s.shape, s.dtype), kw["out_shape"])
    call = _call(body, **kw)

    def pinned(*args):
        return call(*[pltpu.with_memory_space_constraint(x, pltpu.HBM) if jnp.issubdtype(x.dtype, jnp.floating) else x
                      for x in args])
    return pinned


def _call(body, **kw):
    return pl.pallas_call(body, **kw)


def _params(semantics, vmem_bytes):
    limit = int(min(max(2 * vmem_bytes, 16 * 1024 * 1024), VMEM_BYTES - 8 * 1024 * 1024))
    return pltpu.CompilerParams(dimension_semantics=semantics, vmem_limit_bytes=limit)


def _nbytes(shape, dtype):
    n = 1
    for s in shape:
        n *= s
    return n * jnp.dtype(dtype).itemsize


def _dg(a, b, ca, cb):
    return lax.dot_general(a.astype(BF16), b.astype(BF16), (((ca,), (cb,)), ((), ())),
                           preferred_element_type=F32)


@jax.custom_vjp
def mm_nn(a, b):
    return _dg(a, b, 1, 0)


mm_nn.defvjp(lambda a, b: (_dg(a, b, 1, 0), (a, b)),
             lambda r, g: (_dg(g, r[1], 1, 1).astype(r[0].dtype), _dg(r[0], g, 0, 0).astype(r[1].dtype)))


@jax.custom_vjp
def mm_nt(a, b):
    return _dg(a, b, 1, 1)


mm_nt.defvjp(lambda a, b: (_dg(a, b, 1, 1), (a, b)),
             lambda r, g: (_dg(g, r[1], 1, 0).astype(r[0].dtype), _dg(g, r[0], 0, 0).astype(r[1].dtype)))


@jax.custom_vjp
def mm_tn(a, b):
    return _dg(a, b, 0, 0)


mm_tn.defvjp(lambda a, b: (_dg(a, b, 0, 0), (a, b)),
             lambda r, g: (_dg(r[1], g, 1, 1).astype(r[0].dtype), _dg(r[0], g, 1, 0).astype(r[1].dtype)))


def _tri(n, lower):
    r = lax.broadcasted_iota(jnp.int32, (n, n), 0)
    c = lax.broadcasted_iota(jnp.int32, (n, n), 1)
    return ((r >= c) if lower else (r <= c)).astype(F32)


def _tri_dot(lower, x):
    t = _tri(x.shape[0], lower).astype(BF16)
    hi = x.astype(BF16)
    rest = x - hi.astype(F32)
    mid = rest.astype(BF16)
    lo = (rest - mid.astype(F32)).astype(BF16)
    return sum(lax.dot_general(t, p, (((1,), (0,)), ((), ())), preferred_element_type=F32) for p in (hi, mid, lo))


@jax.custom_vjp
def cumsum_rows(x):
    return _tri_dot(True, x)


cumsum_rows.defvjp(lambda x: (_tri_dot(True, x), None), lambda _, g: (_tri_dot(False, g),))


def _shift_impl(halo, x, d):
    xx = jnp.concatenate([halo, x], axis=0)
    return pltpu.roll(xx, d, 0)[SUBLANES:]


@functools.partial(jax.custom_vjp, nondiff_argnums=(2,))
def shift_rows(halo, x, d):
    return _shift_impl(halo, x, d)


def _shift_bwd(d, _, g):
    n = g.shape[0] + SUBLANES
    gg = jnp.concatenate([jnp.zeros((SUBLANES, g.shape[1]), g.dtype), g], axis=0)
    r = pltpu.roll(gg, n - d, 0)
    return r[:SUBLANES], r[SUBLANES:]


shift_rows.defvjp(lambda halo, x, d: (_shift_impl(halo, x, d), None), _shift_bwd)


def causal_conv(halo, x, w_rows, b):
    k = len(w_rows)
    y = b + w_rows[k - 1] * x
    for d in range(1, k):
        y = y + w_rows[k - 1 - d] * shift_rows(halo, x, d)
    return y


def _sigmoid(x):
    return 1.0 / (1.0 + jnp.exp(-x))


def _silu(x):
    return x * _sigmoid(x)


def _log_sigmoid(x):
    return jnp.minimum(x, 0.0) - jnp.log(1.0 + jnp.exp(-jnp.abs(x)))


def _pick_row(x, i):
    row = lax.broadcasted_iota(jnp.int32, (x.shape[0], 1), 0)
    return jnp.sum(jnp.where(row == i, x, 0.0), axis=0, keepdims=True)


def _layer_norm(z, g, b):
    mu = jnp.mean(z, axis=-1, keepdims=True)
    zc = z - mu
    var = jnp.mean(zc * zc, axis=-1, keepdims=True)
    return zc * lax.rsqrt(var + LN_EPS) * g + b


def _qk_conv(halo, x, w0, w1, w2, w3, b):
    return _silu(causal_conv(halo, x, (w0, w1, w2, w3), b))


def _grp(i, h=None):
    if h is None:
        return pl.ds(i * D_GRP, D_GRP)
    return pl.ds(i * D_GRP + h * DK, DK)


def _mixer_specs(n_chunks, reverse):
    def chunk(c):
        return n_chunks - 1 - c if reverse else c
    row8 = CHUNK // SUBLANES
    proj_spec = pl.BlockSpec((CHUNK, D_IN_PAD), lambda c: (chunk(c), 0))
    halo_spec = pl.BlockSpec((SUBLANES, 2 * D_GRP), lambda c: (jnp.maximum(chunk(c) * row8 - 1, 0), 2))
    small = [pl.BlockSpec((2, D_GRP), lambda c: (0, 0)), pl.BlockSpec((1, D_GRP), lambda c: (0, 0)),
             pl.BlockSpec((ML_CONV, 2 * D_GRP), lambda c: (0, 0)), pl.BlockSpec((1, 2 * D_GRP), lambda c: (0, 0)),
             pl.BlockSpec((1, D_GRP), lambda c: (0, 0))]
    state_specs = [pl.BlockSpec((1, HEADS, DK, DK), lambda c: (chunk(c), 0, 0, 0)),
                   pl.BlockSpec((1, HEADS, DK, DK), lambda c: (chunk(c), 0, 0, 0)),
                   pl.BlockSpec((1, HEADS, 1, DK), lambda c: (chunk(c), 0, 0, 0)),
                   pl.BlockSpec((1, HEADS, 1, DK), lambda c: (chunk(c), 0, 0, 0))]
    y_spec = pl.BlockSpec((CHUNK, 2 * D_GRP), lambda c: (chunk(c), 0))
    return proj_spec, halo_spec, small, state_specs, y_spec, chunk


def _heads(x):
    return [x[:, h * DK:(h + 1) * DK] for h in range(HEADS)]


def _last(x, j):
    lane = lax.broadcasted_iota(jnp.int32, (1, x.shape[-1]), 1)
    return jnp.sum(jnp.where(lane == j, x, 0.0), axis=-1, keepdims=True)


def _hg_chunk(st_t, hq, hf, hi, hgate, l0, l1, nw):
    n = hq.shape[0]
    lb = _sigmoid(l0 - l1)
    q = _silu(hq)
    lf = jnp.log(lb + (1.0 - lb) * _sigmoid(hf))
    k = (1.0 - lb) * _sigmoid(-hf)
    b = cumsum_rows(lf)
    b_ref = _pick_row(b, n // 2 - 1)
    b_last = _pick_row(b, n - 1)
    qa, ka =_heads(q * jnp.exp(b - b_ref)), _heads(k * jnp.exp(b_ref - b))
    qe, kd, eb, v = _heads(q * jnp.exp(b)), _heads(k * jnp.exp(b_last - b)), _heads(jnp.exp(b_last)), _heads(hi)
    tri = _tri(n, True) > 0
    attn = [jnp.where(tri, mm_nt(qa[h], ka[h]), 0.0) for h in range(HEADS)]
    o = [mm_nn(attn[h], v[h]) + mm_nt(qe[h], st_t[h]) for h in range(HEADS)]
    st_new = jnp.stack([eb[h] * st_t[h] + mm_tn(v[h], kd[h]) for h in range(HEADS)])
    yn = [o[h] * lax.rsqrt(jnp.mean(o[h] * o[h], axis=-1, keepdims=True) + LN_EPS) for h in range(HEADS)]
    return st_new, jnp.concatenate(yn, axis=1) * nw * _silu(hgate)


def _ml_chunk(c_st, n_st, m_st, q, k, v, gates, og, nw):
    n = q.shape[0]
    ig = jnp.stack([_last(gates, h) for h in range(HEADS)])
    log_f = _log_sigmoid(gates)
    fl = jnp.stack([_last(log_f, HEADS + h) for h in range(HEADS)])
    bw = cumsum_rows(jnp.concatenate([jnp.broadcast_to(fl[h], (n, DK)) for h in range(HEADS)], axis=1))
    b = jnp.stack([_last(x, 0) for x in _heads(bw)])
    g = jnp.sum(fl, axis=1, keepdims=True)
    eye = lax.broadcasted_iota(jnp.int32, (n, n), 0) == lax.broadcasted_iota(jnp.int32, (n, n), 1)
    e_row = jnp.sum(jnp.where(eye, ig - b, 0.0), axis=1, keepdims=True)
    d = jnp.where(_tri(n, True) > 0, b + e_row, -jnp.inf)
    inter = b + m_st
    m_t = jnp.maximum(inter, jnp.max(d, axis=2, keepdims=True))
    qs, kh, vh = _heads(q * (DK ** -0.5)), _heads(k), _heads(v)
    s = jnp.stack([mm_nt(qs[h], kh[h]) for h in range(HEADS)]) * jnp.exp(d - m_t)
    w_inter = jnp.exp(inter - m_t)
    num = (jnp.stack([mm_nn(s[h], vh[h]) for h in range(HEADS)])
           + w_inter * jnp.stack([mm_nn(qs[h], c_st[h]) for h in range(HEADS)]))
    den = jnp.sum(s, axis=2, keepdims=True) + w_inter * jnp.sum(jnp.stack(qs) * n_st, axis=2, keepdims=True)
    h_out = num / jnp.maximum(jnp.abs(den), jnp.exp(-m_t))
    a = g - b + ig
    m_new = jnp.maximum(g + m_st, jnp.max(a, axis=1, keepdims=True))
    decay = jnp.exp(g + m_st - m_new)
    wk = jnp.stack(kh) * jnp.exp(a - m_new)
    c_new = decay * c_st + jnp.stack([mm_tn(wk[h], vh[h]) for h in range(HEADS)])
    n_new = decay * n_st + jnp.sum(wk, axis=1, keepdims=True)
    hc = h_out - jnp.mean(h_out, axis=-1, keepdims=True)
    yn = hc * lax.rsqrt(jnp.mean(hc * hc, axis=-1, keepdims=True) + LN_EPS)
    y = _sigmoid(og) * (jnp.concatenate([yn[h] for h in range(HEADS)], axis=1) * nw)
    return c_new, n_new, m_new, y


def _mixer_inputs(proj_ref, lg_ref, hnw_ref, mnw_ref, qk):
    hg_in = (proj_ref[:, _grp(0)], proj_ref[:, _grp(1)], proj_ref[:, _grp(2)], proj_ref[:, _grp(3)],
             lg_ref[0:1, :], lg_ref[1:2, :], hnw_ref[...])
    ml_in = (qk[:, :D_GRP], qk[:, D_GRP:], proj_ref[:, _grp(6)], proj_ref[:, pl.ds(8 * D_GRP, LANES)],
             proj_ref[:, _grp(7)], mnw_ref[...])
    return hg_in, ml_in


def _mixer_fwd(proj, lb_logits, hg_nw, conv_w, conv_b, ml_nw):
    seq = proj.shape[0]
    n_chunks = seq // CHUNK
    proj_spec, halo_spec, small, state_specs, y_spec, _ = _mixer_specs(n_chunks, False)

    def body(proj_ref, halo_ref, lg_ref, hnw_ref, cw_ref, cb_ref, mnw_ref,
             y_ref, hst_ref, cst_ref, nst_ref, mst_ref, hs, cs, ns, ms):
        c = pl.program_id(0)

        @pl.when(c == 0)
        def _():
            hs[...] = jnp.zeros_like(hs)
            cs[...] = jnp.zeros_like(cs)
            ns[...] = jnp.zeros_like(ns)
            ms[...] = jnp.full(ms.shape, NEG_BIG, F32)

        hst_ref[0] = hs[...]
        cst_ref[0] = cs[...]
        nst_ref[0] = ns[...]
        mst_ref[0] = ms[...]
        halo = jnp.where(c > 0, halo_ref[...], 0.0)
        qk = _qk_conv(halo, proj_ref[:, pl.ds(4 * D_GRP, 2 * D_GRP)],
                      cw_ref[0:1, :], cw_ref[1:2, :], cw_ref[2:3, :], cw_ref[3:4, :], cb_ref[...])
        hg_in, ml_in = _mixer_inputs(proj_ref, lg_ref, hnw_ref, mnw_ref, qk)
        hs[...], y_hg = _hg_chunk(hs[...], *hg_in)
        cs[...], ns[...], m_new, y_ml = _ml_chunk(cs[...], ns[...], _last(ms[...], 0), *ml_in)
        ms[...] = jnp.broadcast_to(m_new, ms.shape)
        y_ref[:, pl.ds(0, D_GRP)] = y_hg.astype(BF16)
        y_ref[:, pl.ds(D_GRP, D_GRP)] = y_ml.astype(BF16)

    st = jax.ShapeDtypeStruct((n_chunks, HEADS, DK, DK), F32)
    vec = jax.ShapeDtypeStruct((n_chunks, HEADS, 1, DK), F32)
    vmem = 2 * (_nbytes((CHUNK, D_IN_PAD), F32) + _nbytes((CHUNK, 2 * D_GRP), F32) + 2 * _nbytes((HEADS, DK, DK), F32)) \
        + 2 * _nbytes((HEADS, DK, DK), F32)
    return _pcall(
        body, name="mixer_fwd", grid=(n_chunks,),
        in_specs=[proj_spec, halo_spec] + small,
        out_specs=[y_spec] + state_specs,
        out_shape=[jax.ShapeDtypeStruct((seq, 2 * D_GRP), BF16), st, st, vec, vec],
        scratch_shapes=[pltpu.VMEM((HEADS, DK, DK), F32), pltpu.VMEM((HEADS, DK, DK), F32),
                        pltpu.VMEM((HEADS, 1, DK), F32), pltpu.VMEM((HEADS, 1, DK), F32)],
        compiler_params=_params(("arbitrary",), vmem),
    )(proj, proj, lb_logits, hg_nw, conv_w, conv_b, ml_nw)


def _mixer_bwd(proj, dy, hst, cst, nst, mst, lb_logits, hg_nw, conv_w, conv_b, ml_nw):
    seq = proj.shape[0]
    n_chunks = seq // CHUNK
    proj_spec, halo_spec, small, state_specs, y_spec, _ = _mixer_specs(n_chunks, True)

    def body(proj_ref, halo_ref, dy_ref, hst_ref, cst_ref, nst_ref, mst_ref,
             lg_ref, hnw_ref, cw_ref, cb_ref, mnw_ref,
             dproj_ref, dbin_ref, dlg_ref, dhnw_ref, dcw_ref, dcb_ref, dmnw_ref,
             dhs, dcs, dns, dms, dhalo):
        c = pl.program_id(0)

        @pl.when(c == 0)
        def _():
            for r in (dhs, dcs, dns, dms, dhalo, dbin_ref, dlg_ref, dhnw_ref, dcw_ref, dcb_ref, dmnw_ref):
                r[...] = jnp.zeros_like(r)

        def put(cols, val):
            dproj_ref[:, cols] = val.astype(BF16)
            dbin_ref[:, cols] += jnp.sum(val, axis=0, keepdims=True)

        first = c == n_chunks - 1
        halo = jnp.where(first, 0.0, halo_ref[...])
        x_qk = proj_ref[:, pl.ds(4 * D_GRP, 2 * D_GRP)]
        conv_args = (halo, x_qk, cw_ref[0:1, :], cw_ref[1:2, :], cw_ref[2:3, :], cw_ref[3:4, :], cb_ref[...])
        qk, conv_vjp = jax.vjp(_qk_conv, *conv_args)
        hg_in, ml_in = _mixer_inputs(proj_ref, lg_ref, hnw_ref, mnw_ref, qk)
        _, hg_vjp = jax.vjp(_hg_chunk, hst_ref[0], *hg_in)
        _, ml_vjp = jax.vjp(_ml_chunk, cst_ref[0], nst_ref[0], _last(mst_ref[0], 0), *ml_in)
        dst, dhq, dhf, dhi, dhg, dl0, dl1, dnw = hg_vjp((dhs[...], dy_ref[:, pl.ds(0, D_GRP)]))
        dc, dn, dm, dq, dk, dv, dgates, dog, dmn = ml_vjp(
            (dcs[...], dns[...], _last(dms[...], 0), dy_ref[:, pl.ds(D_GRP, D_GRP)]))
        dhs[...] = dst
        dcs[...] = dc
        dns[...] = dn
        dms[...] = jnp.broadcast_to(dm, dms.shape)
        for i, val in ((0, dhq), (1, dhf), (2, dhi), (3, dhg), (6, dv), (7, dog)):
            put(_grp(i), val)
        put(pl.ds(8 * D_GRP, LANES), dgates)
        dlg_ref[0:1, :] += dl0
        dlg_ref[1:2, :] += dl1
        dhnw_ref[...] += dnw
        dmnw_ref[...] += dmn
        dh, dx, dw0, dw1, dw2, dw3, db = conv_vjp(jnp.concatenate([dq, dk], axis=1))
        tail = jnp.concatenate([jnp.zeros((CHUNK - SUBLANES, 2 * D_GRP), F32), dhalo[...]], axis=0)
        put(pl.ds(4 * D_GRP, 2 * D_GRP), dx + tail)
        dhalo[...] = dh
        for d, dw in enumerate((dw0, dw1, dw2, dw3)):
            dcw_ref[d:d + 1, :] += dw
        dcb_ref[...] += db

    row = pl.BlockSpec((1, D_GRP), lambda c: (0, 0))
    small_out = [pl.BlockSpec((1, D_IN_PAD), lambda c: (0, 0)), pl.BlockSpec((2, D_GRP), lambda c: (0, 0)), row,
                 pl.BlockSpec((ML_CONV, 2 * D_GRP), lambda c: (0, 0)), pl.BlockSpec((1, 2 * D_GRP), lambda c: (0, 0)), row]
    dy_spec = pl.BlockSpec((CHUNK, 2 * D_GRP), y_spec.index_map)
    vmem = 2 * (2 * _nbytes((CHUNK, D_IN_PAD), F32) + _nbytes((CHUNK, 2 * D_GRP), F32)
                + 2 * _nbytes((HEADS, DK, DK), F32)) + 2 * _nbytes((HEADS, DK, DK), F32) + 4 * 1024 * 1024
    return _pcall(
        body, name="mixer_bwd", grid=(n_chunks,),
        in_specs=[proj_spec, halo_spec, dy_spec] + state_specs + small,
        out_specs=[proj_spec] + small_out,
        out_shape=[jax.ShapeDtypeStruct((seq, D_IN_PAD), BF16), jax.ShapeDtypeStruct((1, D_IN_PAD), F32),
                   jax.ShapeDtypeStruct((2, D_GRP), F32), jax.ShapeDtypeStruct((1, D_GRP), F32),
                   jax.ShapeDtypeStruct((ML_CONV, 2 * D_GRP), F32), jax.ShapeDtypeStruct((1, 2 * D_GRP), F32),
                   jax.ShapeDtypeStruct((1, D_GRP), F32)],
        scratch_shapes=[pltpu.VMEM((HEADS, DK, DK), F32), pltpu.VMEM((HEADS, DK, DK), F32),
                        pltpu.VMEM((HEADS, 1, DK), F32), pltpu.VMEM((HEADS, 1, DK), F32),
                        pltpu.VMEM((SUBLANES, 2 * D_GRP), F32)],
        compiler_params=_params(("arbitrary",), vmem),
    )(proj, proj, dy, hst, cst, nst, mst, lb_logits, hg_nw, conv_w, conv_b, ml_nw)


def _tile(n, prefs, unit=None):
    unit = unit or n
    for p in prefs:
        if unit % p == 0 and n % p == 0:
            return p
    return unit


def _logical(arr):
    return arr.shape if arr.ndim == 2 else (arr.shape[1], arr.shape[0] * arr.shape[2])


def _group(arr):
    return arr.shape[-1]


def _split_spec(ndim, group, tr, tc, where):
    if ndim == 2:
        return pl.BlockSpec((tr, tc), where)
    per = group // tc
    assert per * tc == group, (group, tc)

    def index(*ids):
        bi, bj = where(*ids)
        return (bj // per, bi, bj % per)
    return pl.BlockSpec((None, tr, tc), index)


def _mm(name, mode, a, b, *, bias=None, res=None, res_scale=1.0, ln=None, out_dtype=F32, out_groups=None,
        copy_dtype=None, tm=None, tn=None, tk=None):
    la, lb = _logical(a), _logical(b)
    if mode == "nn":
        (m, k), n = la, lb[1]
        n_unit = _group(b) if b.ndim == 3 else n
        kc = _group(a) if a.ndim == 3 else k
    elif mode == "nt":
        (m, k), n = la, lb[0]
        n_unit = n
        kc = min(_group(a) if a.ndim == 3 else k, _group(b) if b.ndim == 3 else k)
    else:
        (k, m), n = la, lb[1]
        n_unit, kc = (_group(b) if b.ndim == 3 else n), k
        assert a.ndim == 2
    if out_groups:
        n_unit = min(n_unit, n // out_groups)
    kind = ln[0] if ln else None
    tm = tm or (256 if ln else _tile(m, (512, 256, 128)))
    tn = n if ln else (tn or _tile(n, (512, 384, 256, 128), n_unit))
    tk = (tk or _tile(k, (2048, 512, 256, 128))) if mode == "tn" else k
    gi, gj, gk = m // tm, n // tn, k // tk
    assert gi * tm == m and gj * tn == n and gk * tk == k and n_unit % tn == 0, (name, m, n, k, tm, tn, tk)
    ca, cb = {"nn": (1, 0), "nt": (1, 1), "tn": (0, 0)}[mode]
    i_outer = gk > 1 or (gi - 1) * _nbytes(b.shape, b.dtype) <= (gj - 1) * _nbytes(a.shape, a.dtype)

    def ij(where):
        return (lambda p, q, kk: where(p, q, kk)) if i_outer else (lambda p, q, kk: where(q, p, kk))
    if mode == "tn":
        a_spec = pl.BlockSpec((tk, tm), ij(lambda i, j, kk: (kk, i)))
    elif a.ndim == 3:
        a_spec = pl.BlockSpec((a.shape[0], tm, _group(a)), ij(lambda i, j, kk: (0, i, 0)))
    else:
        a_spec = pl.BlockSpec((tm, k), ij(lambda i, j, kk: (i, 0)))
    if mode != "nt":
        b_spec = _split_spec(b.ndim, _group(b), tk, tn, ij(lambda i, j, kk: (kk, j)))
    elif b.ndim == 3:
        b_spec = pl.BlockSpec((b.shape[0], tn, _group(b)), ij(lambda i, j, kk: (0, j, 0)))
    else:
        b_spec = pl.BlockSpec((tn, k), ij(lambda i, j, kk: (j, 0)))
    row_spec = pl.BlockSpec((1, tn), ij(lambda i, j, kk: (0, j)))
    blk_spec = pl.BlockSpec((tm, tn), ij(lambda i, j, kk: (i, j)))
    ins, in_specs = [a, b], [a_spec, b_spec]
    if bias is not None:
        ins.append(bias), in_specs.append(row_spec)
    if res is not None:
        ins.append(res), in_specs.append(blk_spec)
    if kind == "fwd":
        ins += [ln[1], ln[2]]
        in_specs += [row_spec, row_spec]
    elif kind == "loss":
        ins += [ln[1], ln[2], ln[3]]
        in_specs += [row_spec, row_spec, blk_spec]
    elif kind == "bwd":
        ins += [ln[1], ln[2], ln[3]]
        in_specs += [blk_spec, row_spec, row_spec]
    if out_groups:
        blk_out = jax.ShapeDtypeStruct((out_groups, m, n // out_groups), out_dtype)
        out_spec = _split_spec(3, n // out_groups, tm, tn, ij(lambda i, j, kk: (i, j)))
    else:
        blk_out, out_spec = jax.ShapeDtypeStruct((m, n), out_dtype), blk_spec
    row_out = jax.ShapeDtypeStruct((1, n), F32)
    if kind is None:
        out_shape, out_specs = [blk_out], [out_spec]
    elif kind == "fwd":
        out_shape, out_specs = [blk_out, blk_out], [blk_spec, blk_spec]
    else:
        out_shape, out_specs = [blk_out, row_out, row_out], [blk_spec, row_spec, row_spec]
        if kind == "loss":
            out_shape.append(jax.ShapeDtypeStruct((1, LANES), F32))
            out_specs.append(pl.BlockSpec((1, LANES), lambda p, q, kk: (0, 0)))
    if copy_dtype is not None:
        out_shape.append(jax.ShapeDtypeStruct((m, n), copy_dtype))
        out_specs.append(blk_spec)
    n_in = len(ins)

    def body(*refs):
        in_refs, out_refs, acc_ref = refs[:n_in], refs[n_in:n_in + len(out_shape)], refs[-1]
        i, kk = pl.program_id(0 if i_outer else 1), pl.program_id(2)
        a_ref, b_ref = in_refs[:2]
        extra = list(in_refs[2:])

        def epilogue(acc, rows=slice(None)):
            rest = list(extra)
            if bias is not None:
                acc = acc + rest.pop(0)[...]
            if res is not None:
                acc = acc + res_scale * rest.pop(0)[rows, :]
            if kind is None:
                out_refs[0][...] = acc.astype(out_dtype)
                return
            if kind == "fwd":
                out_refs[0][rows, :] = acc
                y = _layer_norm(acc, rest[0][...], rest[1][...])
                out_refs[1][rows, :] = y
                if copy_dtype is not None:
                    out_refs[-1][rows, :] = y.astype(copy_dtype)
                return
            if kind == "loss":
                y, vjp = jax.vjp(_layer_norm, acc, rest[0][...], rest[1][...])
                err = y - rest[2][rows, :]
                part = 0.5 * jnp.sum(jnp.sum(err * err, axis=1, keepdims=True), axis=0, keepdims=True) / n
                dz, dg, db = vjp(err / n)
            else:
                _, vjp = jax.vjp(_layer_norm, rest[0][rows, :], rest[1][...], rest[2][...])
                dz, dg, db = vjp(acc)
            out_refs[0][rows, :] = dz
            out_refs[1][...] += dg
            out_refs[2][...] += db
            if kind == "loss":
                out_refs[3][...] += jnp.broadcast_to(part, (1, LANES))
            if copy_dtype is not None:
                out_refs[-1][rows, :] = dz.astype(copy_dtype)

        if kind in ("loss", "bwd"):
            @pl.when((i == 0) & (kk == 0))
            def _():
                for r in out_refs[1:3 + (kind == "loss")]:
                    r[...] = jnp.zeros_like(r)

        def chunk(ref, c0, last):
            if ref.ndim == 3:
                g = ref.shape[2]
                return ref[c0 // g, :, pl.ds(c0 % g, kc)]
            return ref[:, pl.ds(c0, kc)] if last else ref[pl.ds(c0, kc), :]

        if mode == "tn" or kc == k:
            prod = _dg(a_ref[...], b_ref[...], ca, cb)
        else:
            prod = None
            for c0 in range(0, k, kc):
                part = _dg(chunk(a_ref, c0, True), chunk(b_ref, c0, mode == "nt"), ca, cb)
                prod = part if prod is None else prod + part
        if gk == 1:
            epilogue(prod)
            return

        @pl.when(kk == 0)
        def _():
            acc_ref[...] = prod

        @pl.when(kk > 0)
        def _():
            acc_ref[...] += prod

        @pl.when(kk == gk - 1)
        def _():
            epilogue(acc_ref[...])

    vmem = (2 * (_nbytes((tm, tk), a.dtype) + _nbytes((tk, tn), b.dtype))
            + (2 * len(ins) + 2 * len(out_shape) + 1) * _nbytes((tm, tn), F32))
    outs = _pcall(
        body, name=name, grid=(gi, gj, gk) if i_outer else (gj, gi, gk), in_specs=in_specs, out_specs=out_specs,
        out_shape=out_shape, scratch_shapes=[pltpu.VMEM((tm, tn) if gk > 1 else (SUBLANES, LANES), F32)],
        compiler_params=_params(("arbitrary", "arbitrary", "arbitrary"), vmem),
    )(*ins)
    return outs[0] if (kind is None and copy_dtype is None) else outs


def _attn_head(q, k, v):
    sc = mm_nt(q, k) * (CA_DH ** -0.5)
    e = jnp.exp(sc - jnp.max(sc, axis=-1, keepdims=True))
    return mm_nn(e / jnp.sum(e, axis=-1, keepdims=True), v)


def _attn_fwd(q, kv):
    seq, n_mem = q.shape[0], kv.shape[0]
    tq = _tile(seq, (512, 256, 128))

    def body(q_ref, kv_ref, o_ref):
        for h in range(HEADS):
            hd = pl.ds(h * CA_DH, CA_DH)
            o = _attn_head(q_ref[:, hd], kv_ref[:, hd], kv_ref[:, pl.ds(D_MODEL + h * CA_DH, CA_DH)])
            o_ref[:, hd] = o.astype(BF16)

    return _pcall(
        body, name="attn_fwd", grid=(seq // tq,),
        in_specs=[pl.BlockSpec((tq, D_MODEL), lambda i: (i, 0)), pl.BlockSpec((n_mem, 2 * D_MODEL), lambda i: (0, 0))],
        out_specs=pl.BlockSpec((tq, D_MODEL), lambda i: (i, 0)), out_shape=jax.ShapeDtypeStruct((seq, D_MODEL), BF16),
        compiler_params=_params(("arbitrary",), 4 * _nbytes((tq, D_MODEL), F32) + 2 * _nbytes((n_mem, 2 * D_MODEL), F32)),
    )(q, kv)


def _attn_bwd(q, kv, do):
    seq, n_mem = q.shape[0], kv.shape[0]
    tq = _tile(seq, (512, 256, 128))

    def body(q_ref, kv_ref, do_ref, dq_ref, dkv_ref):
        @pl.when(pl.program_id(0) == 0)
        def _():
            dkv_ref[...] = jnp.zeros_like(dkv_ref)

        for h in range(HEADS):
            hd = pl.ds(h * CA_DH, CA_DH)
            vd = pl.ds(D_MODEL + h * CA_DH, CA_DH)
            _, vjp = jax.vjp(_attn_head, q_ref[:, hd], kv_ref[:, hd], kv_ref[:, vd])
            dq, dk, dv = vjp(do_ref[:, hd].astype(F32))
            dq_ref[:, hd] = dq.astype(BF16)
            dkv_ref[:, hd] += dk
            dkv_ref[:, vd] += dv

    return _pcall(
        body, name="attn_bwd", grid=(seq // tq,),
        in_specs=[pl.BlockSpec((tq, D_MODEL), lambda i: (i, 0)), pl.BlockSpec((n_mem, 2 * D_MODEL), lambda i: (0, 0)),
                  pl.BlockSpec((tq, D_MODEL), lambda i: (i, 0))],
        out_specs=[pl.BlockSpec((tq, D_MODEL), lambda i: (i, 0)), pl.BlockSpec((n_mem, 2 * D_MODEL), lambda i: (0, 0))],
        out_shape=[jax.ShapeDtypeStruct((seq, D_MODEL), BF16), jax.ShapeDtypeStruct((n_mem, 2 * D_MODEL), F32)],
        compiler_params=_params(("arbitrary",), 6 * _nbytes((tq, D_MODEL), F32) + 4 * _nbytes((n_mem, 2 * D_MODEL), F32)),
    )(q, kv, do)


def _ffn_mid(hg, xg, hv, xv, wg0, wg1, wg2, bg, wv0, wv1, wv2, bv):
    return jax.nn.gelu(causal_conv(hg, xg, (wg0, wg1, wg2), bg)) * causal_conv(hv, xv, (wv0, wv1, wv2), bv)


FFN_TB = 256
FFN_W = D_FF // 2
FFN_J = D_FF // FFN_W
MXU_COLS = 256
FFN_PIECES = tuple((off, min(MXU_COLS, FFN_W - off)) for off in range(0, FFN_W, MXU_COLS))


def _ffn_common_specs(seq, row):
    tb = min(FFN_TB, seq)
    full = pl.BlockSpec((tb, D_MODEL), lambda t, j: (row(t), 0))
    vec = pl.BlockSpec((1, D_MODEL), lambda t, j: (0, 0))
    halves = []
    for off in (0, FFN_J):
        halves.append(dict(
            w_up=pl.BlockSpec((D_MODEL, FFN_W), lambda t, j, off=off: (0, j + off)),
            taps=pl.BlockSpec((FFN_CONV, FFN_W), lambda t, j, off=off: (0, j + off)),
            bias=pl.BlockSpec((1, FFN_W), lambda t, j, off=off: (0, j + off))))
    w_down = pl.BlockSpec((FFN_W, D_MODEL), lambda t, j: (j, 0))
    u_blk = pl.BlockSpec((2, tb, FFN_W), lambda t, j: (0, row(t), j))
    return tb, full, vec, halves, w_down, u_blk


def _ffn_vmem(tb):
    return (_nbytes((2, tb, FFN_W), F32) + _nbytes((2, tb, FFN_W), BF16) + 3 * _nbytes((D_MODEL, FFN_W), BF16)
            + 10 * _nbytes((tb, D_MODEL), F32))


def _conv_params(taps_ref, bias_ref, cols):
    return taps_ref[0:1, cols], taps_ref[1:2, cols], taps_ref[2:3, cols], bias_ref[:, cols]


def _ffn_fwd(x2b, x2, w_up, conv_w, conv_b, w_down, ln_g, ln_b, target):
    seq = x2.shape[0]
    tb, full, vec, halves, wd_spec, u_blk = _ffn_common_specs(seq, lambda t: t)
    nt = seq // tb

    def body(xb_ref, wg_ref, wv_ref, tg_ref, tv_ref, bg_ref, bv_ref, wd_ref, x_ref, g_ref, b_ref, tgt_ref,
             u_ref, h_ref, dz_ref, dg_ref, db_ref, loss_ref, dzb_ref, acc, carry):
        t, j = pl.program_id(0), pl.program_id(1)
        xb = xb_ref[...]
        pieces = [pl.ds(off, width) for off, width in FFN_PIECES]
        ug = [_dg(xb, wg_ref[:, cols], 1, 0) for cols in pieces]
        uv = [_dg(xb, wv_ref[:, cols], 1, 0) for cols in pieces]
        hs = []
        for cols, g, v in zip(pieces, ug, uv):
            u_ref[0, :, cols] = g
            u_ref[1, :, cols] = v
            halo_g = jnp.where(t == 0, 0.0, carry[j, 0, :, cols])
            halo_v = jnp.where(t == 0, 0.0, carry[j, 1, :, cols])
            h = _ffn_mid(halo_g, g, halo_v, v, *_conv_params(tg_ref, bg_ref, cols),
                         *_conv_params(tv_ref, bv_ref, cols)).astype(BF16)
            carry[j, 0, :, cols] = g[tb - SUBLANES:, :]
            carry[j, 1, :, cols] = v[tb - SUBLANES:, :]
            h_ref[:, cols] = h
            hs.append(h)
        part = None
        for cols, h in zip(pieces, hs):
            p = _dg(h, wd_ref[cols, :], 1, 0)
            part = p if part is None else part + p

        @pl.when(j == 0)
        def _():
            acc[...] = part

        @pl.when(j > 0)
        def _():
            acc[...] += part

        @pl.when(j == FFN_J - 1)
        def _():
            y, vjp = jax.vjp(_layer_norm, acc[...] + ALPHA * x_ref[...], g_ref[...], b_ref[...])
            err = y - tgt_ref[...]
            part_loss = 0.5 * jnp.sum(jnp.sum(err * err, axis=1, keepdims=True), axis=0, keepdims=True) / D_MODEL
            dz, dg, db = vjp(err / D_MODEL)

            @pl.when(t == 0)
            def _():
                for r in (dg_ref, db_ref, loss_ref):
                    r[...] = jnp.zeros_like(r)

            dz_ref[...] = dz
            dzb_ref[...] = dz.astype(BF16)
            dg_ref[...] += dg
            db_ref[...] += db
            loss_ref[...] += jnp.broadcast_to(part_loss, (1, LANES))

    h0, h1 = halves
    row = jax.ShapeDtypeStruct((1, D_MODEL), F32)
    return _pcall(
        body, name="ffn_fwd", grid=(nt, FFN_J),
        in_specs=[full, h0["w_up"], h1["w_up"], h0["taps"], h1["taps"], h0["bias"], h1["bias"], wd_spec, full, vec, vec,
                  full],
        out_specs=[u_blk, pl.BlockSpec((tb, FFN_W), lambda t, j: (t, j)), full, vec, vec,
                   pl.BlockSpec((1, LANES), lambda t, j: (0, 0)), full],
        out_shape=[jax.ShapeDtypeStruct((2, seq, D_FF), F32), jax.ShapeDtypeStruct((seq, D_FF), BF16),
                   jax.ShapeDtypeStruct((seq, D_MODEL), F32), row, row, jax.ShapeDtypeStruct((1, LANES), F32),
                   jax.ShapeDtypeStruct((seq, D_MODEL), BF16)],
        scratch_shapes=[pltpu.VMEM((tb, D_MODEL), F32), pltpu.VMEM((FFN_J, 2, SUBLANES, FFN_W), F32)],
        compiler_params=_params(("arbitrary", "arbitrary"), _ffn_vmem(tb)),
    )(x2b, w_up, w_up, conv_w, conv_w, conv_b, conv_b, w_down, x2, ln_g, ln_b, target)


def _ffn_bwd(u, conv_w, conv_b, dz3b, dz3, w_down, w_up, z2, ln_g, ln_b):
    seq = dz3.shape[0]
    tb = min(FFN_TB, seq)
    nt = seq // tb
    row8 = tb // SUBLANES
    tb, full, vec, halves, wd_spec, u_blk = _ffn_common_specs(seq, lambda t: nt - 1 - t)
    halo = pl.BlockSpec((2, SUBLANES, FFN_W), lambda t, j: (0, jnp.maximum((nt - 1 - t) * row8 - 1, 0), j))

    def body(u_ref, halo_ref, tg_ref, tv_ref, bg_ref, bv_ref, dzb_ref, wd_ref, wg_ref, wv_ref, dz3_ref, z_ref, g_ref,
             b_ref, du_ref, dw_ref, dbias_ref, dz_ref, dg_ref, db_ref, dz2b_ref, acc, carry):
        t, j = pl.program_id(0), pl.program_id(1)

        @pl.when((t == 0) & (j == 0))
        def _():
            for r in (dw_ref, dbias_ref, dg_ref, db_ref):
                r[...] = jnp.zeros_like(r)

        pieces = [pl.ds(off, width) for off, width in FFN_PIECES]
        dzb = dzb_ref[...]
        dhs = [_dg(dzb, wd_ref[cols, :], 1, 1) for cols in pieces]
        first = t == nt - 1
        dus = []
        for cols, dh in zip(pieces, dhs):
            args = (jnp.where(first, 0.0, halo_ref[0, :, cols]), u_ref[0, :, cols],
                    jnp.where(first, 0.0, halo_ref[1, :, cols]), u_ref[1, :, cols],
                    *_conv_params(tg_ref, bg_ref, cols), *_conv_params(tv_ref, bv_ref, cols))
            _, vjp = jax.vjp(_ffn_mid, *args)
            dhg, dxg, dhv, dxv, g0, g1, g2, gb, v0, v1, v2, vb = vjp(dh)
            zeros = jnp.zeros((tb - SUBLANES, dh.shape[1]), F32)
            dug = (dxg + jnp.concatenate([zeros, jnp.where(t == 0, 0.0, carry[j, 0, :, cols])], axis=0)).astype(BF16)
            duv = (dxv + jnp.concatenate([zeros, jnp.where(t == 0, 0.0, carry[j, 1, :, cols])], axis=0)).astype(BF16)
            carry[j, 0, :, cols] = dhg
            carry[j, 1, :, cols] = dhv
            du_ref[0, :, cols] = dug
            du_ref[1, :, cols] = duv
            for half, parts in enumerate(((g0, g1, g2), (v0, v1, v2))):
                for d, p in enumerate(parts):
                    dw_ref[j, half, d:d + 1, cols] += p
            dbias_ref[j, 0, :, cols] += gb
            dbias_ref[j, 1, :, cols] += vb
            dus.append((dug, duv))
        part = None
        for cols, (dug, duv) in zip(pieces, dus):
            p = _dg(dug, wg_ref[:, cols], 1, 1) + _dg(duv, wv_ref[:, cols], 1, 1)
            part = p if part is None else part + p

        @pl.when(j == 0)
        def _():
            acc[...] = part

        @pl.when(j > 0)
        def _():
            acc[...] += part

        @pl.when(j == FFN_J - 1)
        def _():
            _, ln_vjp = jax.vjp(_layer_norm, z_ref[...], g_ref[...], b_ref[...])
            dz, dg, db = ln_vjp(acc[...] + ALPHA * dz3_ref[...])
            dz_ref[...] = dz
            dz2b_ref[...] = dz.astype(BF16)
            dg_ref[...] += dg
            db_ref[...] += db

    h0, h1 = halves
    row = jax.ShapeDtypeStruct((1, D_MODEL), F32)
    whole = lambda *shape: pl.BlockSpec(shape, lambda t, j: (0,) * len(shape))
    return _pcall(
        body, name="ffn_bwd", grid=(nt, FFN_J),
        in_specs=[u_blk, halo, h0["taps"], h1["taps"], h0["bias"], h1["bias"], full, wd_spec, h0["w_up"], h1["w_up"],
                  full, full, vec, vec],
        out_specs=[u_blk, whole(FFN_J, 2, FFN_CONV, FFN_W), whole(FFN_J, 2, 1, FFN_W), full, vec, vec, full],
        out_shape=[jax.ShapeDtypeStruct((2, seq, D_FF), BF16), jax.ShapeDtypeStruct((FFN_J, 2, FFN_CONV, FFN_W), F32),
                   jax.ShapeDtypeStruct((FFN_J, 2, 1, FFN_W), F32), jax.ShapeDtypeStruct((seq, D_MODEL), F32), row, row,
                   jax.ShapeDtypeStruct((seq, D_MODEL), BF16)],
        scratch_shapes=[pltpu.VMEM((tb, D_MODEL), F32), pltpu.VMEM((FFN_J, 2, SUBLANES, FFN_W), F32)],
        compiler_params=_params(("arbitrary", "arbitrary"), _ffn_vmem(tb)),
    )(u, u, conv_w, conv_w, conv_b, conv_b, dz3b, w_down, w_up, w_up, dz3, z2, ln_g, ln_b)


def _adamw_math(w, g, m, v):
    m_new = ADAM_B1 * m + (1.0 - ADAM_B1) * g
    v_new = ADAM_B2 * v + (1.0 - ADAM_B2) * jnp.square(g)
    m_hat = m_new / (1.0 - ADAM_B1 ** ADAM_STEP)
    v_hat = v_new / (1.0 - ADAM_B2 ** ADAM_STEP)
    return -ADAM_LR * (m_hat / (jnp.sqrt(v_hat) + ADAM_EPS) + ADAM_WD * w), m_new, v_new


def _adamw(name, w, g, m, v):
    rows, cols = w.shape
    tr = _tile(rows, (256, 176, 128, 64, 40, 32, 16, 8))

    def body(w_ref, g_ref, m_ref, v_ref, d_ref, nm_ref, nv_ref):
        d_ref[...], nm_ref[...], nv_ref[...] = _adamw_math(w_ref[...], g_ref[...], m_ref[...], v_ref[...])

    spec = pl.BlockSpec((tr, cols), lambda i: (i, 0))
    sh = jax.ShapeDtypeStruct((rows, cols), F32)
    return _pcall(
        body, name=name, grid=(rows // tr,), in_specs=[spec] * 4, out_specs=[spec] * 3, out_shape=[sh] * 3,
        compiler_params=_params(("arbitrary",), 14 * _nbytes((tr, -(-cols // LANES) * LANES), F32)),
    )(w, g, m, v)


def _adamw_halves(name, core, w, mine, theirs, m, v):
    rows, cols = w.shape
    half_rows = mine.shape[0]
    tr = _tile(half_rows, (256, 176, 128))
    nbh = half_rows // tr
    assert 2 * half_rows == rows

    def body(c_ref, w_ref, a_ref, b_ref, m_ref, v_ref, g_ref, d_ref, nm_ref, nv_ref):
        g = jnp.where(pl.program_id(0) // nbh == c_ref[0], a_ref[...], b_ref[...])
        g_ref[...] = g
        d_ref[...], nm_ref[...], nv_ref[...] = _adamw_math(w_ref[...], g, m_ref[...], v_ref[...])

    spec = pl.BlockSpec((tr, cols), lambda i, c_ref: (i, 0))
    half = pl.BlockSpec((tr, cols), lambda i, c_ref: (i % nbh, 0))
    sh = jax.ShapeDtypeStruct((rows, cols), F32)
    grid_spec = pltpu.PrefetchScalarGridSpec(
        num_scalar_prefetch=1, grid=(rows // tr,), in_specs=[spec, half, half, spec, spec], out_specs=[spec] * 4)
    return _pcall(
        body, name=name, grid_spec=grid_spec, out_shape=[sh] * 4,
        compiler_params=_params(("arbitrary",), 18 * _nbytes((tr, -(-cols // LANES) * LANES), F32)),
    )(core, w, mine, theirs, m, v)


MESH = pl.DeviceIdType.MESH
ANY = pl.BlockSpec(memory_space=pl.ANY)
N_CHIPS = 4
N_DEV = 8
BF16_ROWS = 16


def _me():
    return lax.axis_index("x"), lax.axis_index("y"), lax.axis_index("c")


def _other_chips(x, y):
    return [(1 - x, y), (x, 1 - y), (1 - x, 1 - y)]


def _remote(src, dst, ssem, rsem, dev):
    return pltpu.make_async_remote_copy(src_ref=src, dst_ref=dst, send_sem=ssem, recv_sem=rsem,
                                        device_id=dev, device_id_type=MESH)


def _half_rows(ref_rows, cc):
    half = ref_rows // 2
    return pl.ds(pl.multiple_of(cc * half, BF16_ROWS), half)


def _gather_weights(shards):
    n = len(shards)
    n_ici = n * (N_CHIPS - 1)

    def body(*refs):
        ins, outs, (ssem, rsem, lsem, lrsem) = refs[:n], refs[n:2 * n], refs[2 * n:]
        x, y, c = _me()
        k_me = 2 * x + y
        sib = (x, y, 1 - c)
        chips = _other_chips(x, y)
        started = []
        for i, (w_ref, o_ref) in enumerate(zip(ins, outs)):
            cp = _remote(w_ref, o_ref.at[k_me], lsem.at[i], lrsem.at[i], sib)
            cp.start()
            started.append(cp)
        for r, (px, py) in enumerate(chips):
            for i, (w_ref, o_ref) in enumerate(zip(ins, outs)):
                rows = _half_rows(w_ref.shape[0], c)
                s = r * n + i
                cp = _remote(w_ref.at[rows], o_ref.at[k_me, rows], ssem.at[s], rsem.at[s], (px, py, c))
                cp.start()
                started.append(cp)
        for r, (px, py) in enumerate(chips):
            for i, o_ref in enumerate(outs):
                blk = o_ref.at[2 * px + py, _half_rows(o_ref.shape[1], c)]
                s = r * n + i
                _remote(blk, blk, ssem.at[s], rsem.at[s], (px, py, c)).wait_recv()
                cp = _remote(blk, blk, ssem.at[n_ici + s], rsem.at[n_ici + s], sib)
                cp.start()
                started.append(cp)
        for r, (px, py) in enumerate(chips):
            for i, o_ref in enumerate(outs):
                blk = o_ref.at[2 * px + py, _half_rows(o_ref.shape[1], 1 - c)]
                s = n_ici + r * n + i
                _remote(blk, blk, ssem.at[s], rsem.at[s], sib).wait_recv()
        for cp in started[n:]:
            cp.wait_send()
        for cp in started[:n]:
            cp.wait()

    return _pcall(
        body, name="gather_weights", in_specs=[ANY] * n, out_specs=[ANY] * n,
        out_shape=[jax.ShapeDtypeStruct((N_CHIPS,) + s.shape, s.dtype) for s in shards],
        scratch_shapes=[pltpu.SemaphoreType.DMA((2 * n_ici,)), pltpu.SemaphoreType.DMA((2 * n_ici,)),
                        pltpu.SemaphoreType.DMA((n,)), pltpu.SemaphoreType.DMA((n,))],
    )(*shards)


def _swap_halves(name, grads):
    n = len(grads)

    def body(*refs):
        ins, outs, (ssem, rsem) = refs[:n], refs[n:2 * n], refs[2 * n:]
        x, y, c = _me()
        copies = []
        for i, (g_ref, o_ref) in enumerate(zip(ins, outs)):
            for k in range(N_CHIPS):
                s = i * N_CHIPS + k
                cp = _remote(g_ref.at[k, _half_rows(g_ref.shape[1], 1 - c)], o_ref.at[k], ssem.at[s], rsem.at[s],
                             (x, y, 1 - c))
                cp.start()
                copies.append(cp)
        for cp in copies:
            cp.wait()

    return _pcall(
        body, name=name, in_specs=[ANY] * n, out_specs=[ANY] * n,
        out_shape=[jax.ShapeDtypeStruct((N_CHIPS, g.shape[1] // 2, g.shape[2]), g.dtype) for g in grads],
        scratch_shapes=[pltpu.SemaphoreType.DMA((n * N_CHIPS,)), pltpu.SemaphoreType.DMA((n * N_CHIPS,))],
    )(*grads)


SEM = pl.BlockSpec(memory_space=pltpu.SEMAPHORE)
IN_HBM = pl.BlockSpec(memory_space=pltpu.HBM)
SPLIT_PARAMS = dict(compiler_params=pltpu.CompilerParams(has_side_effects=pltpu.SideEffectType.DATAFLOW_SIDE_EFFECTING))


def _split_start(name, sources, landings, n_copies, plan):
    ns, nl = len(sources), len(landings)

    def body(*refs):
        ins, lands, (ssem, rsem), token = refs[:ns], refs[ns:ns + nl], refs[ns + nl:ns + nl + 2], refs[-1]
        for s, (src, dst, _, dev) in enumerate(plan(ins, lands)):
            _remote(src, dst, ssem.at[s], rsem.at[s], dev).start()
        token[...] = jnp.zeros_like(token)

    arrays = list(sources) + list(landings)
    outs = _call(
        body, name=name, in_specs=[IN_HBM] * (ns + nl),
        out_specs=[SEM, SEM] + [IN_HBM] * (ns + nl) + [pl.BlockSpec(memory_space=pltpu.VMEM)],
        out_shape=[pltpu.SemaphoreType.DMA((n_copies,)), pltpu.SemaphoreType.DMA((n_copies,))]
        + [pltpu.HBM(a.shape, a.dtype) for a in arrays] + [jax.ShapeDtypeStruct((SUBLANES, LANES), F32)],
        input_output_aliases={i: 2 + i for i in range(ns + nl)}, **SPLIT_PARAMS,
    )(*[pltpu.with_memory_space_constraint(a, pltpu.HBM) for a in arrays])
    return (outs[:-1], ns), outs[-1]


def _split_wait(name, handle, after, plan):
    (ssem, rsem, *thru), ns = handle
    nl = len(thru) - ns

    def body(*refs):
        ins, lands, (ssem_ref, rsem_ref) = refs[:ns], refs[ns:ns + nl], refs[ns + nl:ns + nl + 2]
        for s, (src, _, dst, dev) in enumerate(plan(ins, lands)):
            cp = _remote(src, dst, ssem_ref.at[s], rsem_ref.at[s], dev)
            cp.wait_send()
            cp.wait_recv()

    outs = _call(
        body, name=name, in_specs=[IN_HBM] * (ns + nl) + [SEM, SEM, ANY], out_specs=[IN_HBM] * (ns + nl),
        out_shape=[pltpu.HBM(t.shape, t.dtype) for t in thru],
        input_output_aliases={i: i for i in range(ns + nl)}, **SPLIT_PARAMS,
    )(*thru, ssem, rsem, after)
    return outs[:ns], outs[ns:]


def _share_plan(ins, lands):
    x, y, c = _me()
    return [(r_ref, l_ref, l_ref, (x, y, 1 - c)) for r_ref, l_ref in zip(ins, lands)]


def _small_plan(ins, lands):
    x, y, c = _me()
    me = 4 * x + 2 * y + c
    plan = []
    for bx in (0, 1):
        for by in (0, 1):
            for bc in (0, 1):
                if (bx, by, bc) != (0, 0, 0):
                    px, py, pc = (x + bx) % 2, (y + by) % 2, (c + bc) % 2
                    plan.append((ins[0], lands[0].at[me], lands[0].at[4 * px + 2 * py + pc], (px, py, pc)))
    return plan


def _sum_devices(blocks):
    _, rows, _ = blocks.shape

    def body(b_ref, o_ref):
        acc = b_ref[0]
        for d in range(1, N_DEV):
            acc = acc + b_ref[d]
        o_ref[...] = acc

    return _pcall(
        body, name="sum_devices", grid=(1,), in_specs=[pl.BlockSpec(blocks.shape, lambda i: (0, 0, 0))],
        out_specs=pl.BlockSpec((rows, LANES), lambda i: (0, 0)), out_shape=jax.ShapeDtypeStruct((rows, LANES), F32),
        compiler_params=_params(("arbitrary",), 3 * _nbytes(blocks.shape, F32)),
    )(blocks)


def _swap_start(name, grads):
    n = len(grads)
    n_sem = n * N_CHIPS

    def body(*refs):
        ins, lands, (ssem, rsem), token = refs[:n], refs[n:2 * n], refs[2 * n:2 * n + 2], refs[-1]
        x, y, c = _me()
        for i, (g_ref, l_ref) in enumerate(zip(ins, lands)):
            for k in range(N_CHIPS):
                s = i * N_CHIPS + k
                _remote(g_ref.at[k, _half_rows(g_ref.shape[1], 1 - c)], l_ref.at[k], ssem.at[s], rsem.at[s],
                        (x, y, 1 - c)).start()
        token[...] = jnp.zeros_like(token)

    src = [pltpu.HBM(g.shape, g.dtype) for g in grads]
    dst = [pltpu.HBM((N_CHIPS, g.shape[1] // 2, g.shape[2]), g.dtype) for g in grads]
    outs = _call(
        body, name=name, in_specs=[IN_HBM] * (2 * n),
        out_specs=[SEM, SEM] + [IN_HBM] * (2 * n) + [pl.BlockSpec(memory_space=pltpu.VMEM)],
        out_shape=[pltpu.SemaphoreType.DMA((n_sem,)), pltpu.SemaphoreType.DMA((n_sem,))] + src + dst
        + [jax.ShapeDtypeStruct((SUBLANES, LANES), F32)],
        input_output_aliases={i: 2 + i for i in range(2 * n)}, **SPLIT_PARAMS,
    )(*[pltpu.with_memory_space_constraint(g, pltpu.HBM) for g in grads],
      *[pltpu.with_memory_space_constraint(lax.empty(d.shape, d.dtype), pltpu.HBM) for d in dst])
    return outs[:-1], outs[-1]


def _swap_wait(name, handle, after):
    ssem, rsem, thru = handle[0], handle[1], handle[2:]
    n = len(thru) // 2

    def body(*refs):
        ins, lands, (ssem_ref, rsem_ref) = refs[:n], refs[n:2 * n], refs[2 * n:2 * n + 2]
        x, y, c = _me()
        for i, (g_ref, l_ref) in enumerate(zip(ins, lands)):
            for k in range(N_CHIPS):
                s = i * N_CHIPS + k
                cp = _remote(g_ref.at[k, _half_rows(g_ref.shape[1], 1 - c)], l_ref.at[k], ssem_ref.at[s],
                             rsem_ref.at[s], (x, y, 1 - c))
                cp.wait_send()
                cp.wait_recv()

    outs = _call(
        body, name=name, in_specs=[IN_HBM] * (2 * n) + [SEM, SEM, ANY], out_specs=[IN_HBM] * (2 * n),
        out_shape=[pltpu.HBM(t.shape, t.dtype) for t in thru],
        input_output_aliases={i: i for i in range(2 * n)}, **SPLIT_PARAMS,
    )(*thru, ssem, rsem, after)
    return outs[:n], outs[n:]


def _gather_start(name, shards):
    n = len(shards)
    n_sem = n * N_CHIPS

    def body(*refs):
        ins, lands, (ssem, rsem), token = refs[:n], refs[n:2 * n], refs[2 * n:2 * n + 2], refs[-1]
        x, y, c = _me()
        k_me = 2 * x + y
        for i, (w_ref, l_ref) in enumerate(zip(ins, lands)):
            _remote(w_ref, l_ref.at[k_me], ssem.at[i], rsem.at[i], (x, y, 1 - c)).start()
        for r, (px, py) in enumerate(_other_chips(x, y)):
            for i, (w_ref, l_ref) in enumerate(zip(ins, lands)):
                rows = _half_rows(w_ref.shape[0], c)
                s = (r + 1) * n + i
                _remote(w_ref.at[rows], l_ref.at[k_me, rows], ssem.at[s], rsem.at[s], (px, py, c)).start()
        token[...] = jnp.zeros_like(token)

    src = [pltpu.HBM(s.shape, s.dtype) for s in shards]
    dst = [pltpu.HBM((N_CHIPS,) + s.shape, s.dtype) for s in shards]
    outs = _call(
        body, name=name, in_specs=[IN_HBM] * (2 * n),
        out_specs=[SEM, SEM] + [IN_HBM] * (2 * n) + [pl.BlockSpec(memory_space=pltpu.VMEM)],
        out_shape=[pltpu.SemaphoreType.DMA((n_sem,)), pltpu.SemaphoreType.DMA((n_sem,))] + src + dst
        + [jax.ShapeDtypeStruct((SUBLANES, LANES), F32)],
        input_output_aliases={i: 2 + i for i in range(2 * n)}, **SPLIT_PARAMS,
    )(*[pltpu.with_memory_space_constraint(s, pltpu.HBM) for s in shards],
      *[pltpu.with_memory_space_constraint(lax.empty(d.shape, d.dtype), pltpu.HBM) for d in dst])
    return outs[:-1], outs[-1]


def _gather_wait(name, handle, after):
    ssem, rsem, thru = handle[0], handle[1], handle[2:]
    n = len(thru) // 2

    def body(*refs):
        ins, lands, (ssem_ref, rsem_ref) = refs[:n], refs[n:2 * n], refs[2 * n:2 * n + 2]
        x, y, c = _me()
        k_me = 2 * x + y
        for i, (w_ref, l_ref) in enumerate(zip(ins, lands)):
            cp = _remote(w_ref, l_ref.at[k_me], ssem_ref.at[i], rsem_ref.at[i], (x, y, 1 - c))
            cp.wait_send()
            cp.wait_recv()
        for r, (px, py) in enumerate(_other_chips(x, y)):
            for i, (w_ref, l_ref) in enumerate(zip(ins, lands)):
                rows = _half_rows(w_ref.shape[0], c)
                s = (r + 1) * n + i
                cp = _remote(w_ref.at[rows], l_ref.at[2 * px + py, rows], ssem_ref.at[s], rsem_ref.at[s], (px, py, c))
                cp.wait_send()
                cp.wait_recv()

    outs = _call(
        body, name=name, in_specs=[IN_HBM] * (2 * n) + [SEM, SEM, ANY], out_specs=[IN_HBM] * (2 * n),
        out_shape=[pltpu.HBM(t.shape, t.dtype) for t in thru],
        input_output_aliases={i: i for i in range(2 * n)}, **SPLIT_PARAMS,
    )(*thru, ssem, rsem, after)
    return outs[n:]


def _forward_halves(name, blocks):
    n = len(blocks)
    n_sem = n * (N_CHIPS - 1)

    def body(*refs):
        outs, (ssem, rsem) = refs[n:2 * n], refs[2 * n:]
        x, y, c = _me()
        sib = (x, y, 1 - c)
        chips = _other_chips(x, y)
        sends = []
        for r, (px, py) in enumerate(chips):
            for i, o_ref in enumerate(outs):
                blk = o_ref.at[2 * px + py, _half_rows(o_ref.shape[1], c)]
                cp = _remote(blk, blk, ssem.at[r * n + i], rsem.at[r * n + i], sib)
                cp.start()
                sends.append(cp)
        for r, (px, py) in enumerate(chips):
            for i, o_ref in enumerate(outs):
                blk = o_ref.at[2 * px + py, _half_rows(o_ref.shape[1], 1 - c)]
                _remote(blk, blk, ssem.at[r * n + i], rsem.at[r * n + i], sib).wait_recv()
        for cp in sends:
            cp.wait_send()

    return _pcall(
        body, name=name, in_specs=[ANY] * n, out_specs=[ANY] * n,
        out_shape=[jax.ShapeDtypeStruct(b.shape, b.dtype) for b in blocks],
        input_output_aliases={i: i for i in range(n)},
        scratch_shapes=[pltpu.SemaphoreType.DMA((n_sem,)), pltpu.SemaphoreType.DMA((n_sem,))],
    )(*blocks)


def _scatter_start(name, parts):
    n = len(parts)
    n_sem = n * (N_CHIPS - 1)

    def body(*refs):
        ins, lands, (ssem, rsem), token = refs[:n], refs[n:2 * n], refs[2 * n:2 * n + 2], refs[-1]
        x, y, c = _me()
        k_me = 2 * x + y
        for r, (px, py) in enumerate(_other_chips(x, y)):
            for i, (p_ref, l_ref) in enumerate(zip(ins, lands)):
                s = r * n + i
                _remote(p_ref.at[2 * px + py], l_ref.at[k_me], ssem.at[s], rsem.at[s], (px, py, c)).start()
        token[...] = jnp.zeros_like(token)

    hbm = [pltpu.HBM(p.shape, p.dtype) for p in parts]
    outs = _call(
        body, name=name, in_specs=[IN_HBM] * (2 * n),
        out_specs=[SEM, SEM] + [IN_HBM] * (2 * n) + [pl.BlockSpec(memory_space=pltpu.VMEM)],
        out_shape=[pltpu.SemaphoreType.DMA((n_sem,)), pltpu.SemaphoreType.DMA((n_sem,))] + hbm + hbm
        + [jax.ShapeDtypeStruct((SUBLANES, LANES), F32)],
        input_output_aliases={i: 2 + i for i in range(2 * n)}, **SPLIT_PARAMS,
    )(*[pltpu.with_memory_space_constraint(p, pltpu.HBM) for p in parts],
      *[pltpu.with_memory_space_constraint(lax.empty(p.shape, p.dtype), pltpu.HBM) for p in parts])
    return outs[:-1], outs[-1]


def _scatter_wait(name, handle, after):
    ssem, rsem, thru = handle[0], handle[1], handle[2:]
    n = len(thru) // 2

    def body(*refs):
        ins, lands, (ssem_ref, rsem_ref) = refs[:n], refs[n:2 * n], refs[2 * n:2 * n + 2]
        x, y, c = _me()
        for r, (px, py) in enumerate(_other_chips(x, y)):
            for i, (p_ref, l_ref) in enumerate(zip(ins, lands)):
                s = r * n + i
                cp = _remote(p_ref.at[2 * px + py], l_ref.at[2 * px + py], ssem_ref.at[s], rsem_ref.at[s], (px, py, c))
                cp.wait_send()
                cp.wait_recv()

    outs = _call(
        body, name=name, in_specs=[IN_HBM] * (2 * n) + [SEM, SEM, ANY], out_specs=[IN_HBM] * (2 * n),
        out_shape=[pltpu.HBM(t.shape, t.dtype) for t in thru],
        input_output_aliases={i: i for i in range(2 * n)}, **SPLIT_PARAMS,
    )(*thru, ssem, rsem, after)
    return outs[n:]


def _share_halves(halves):
    n = len(halves)

    def body(*refs):
        ins, outs, (ssem, rsem) = refs[:n], refs[n:2 * n], refs[2 * n:]
        x, y, c = _me()
        copies = [_remote(r_ref, o_ref, ssem.at[i], rsem.at[i], (x, y, 1 - c))
                  for i, (r_ref, o_ref) in enumerate(zip(ins, outs))]
        for cp in copies:
            cp.start()
        for cp in copies:
            cp.wait()

    return _pcall(
        body, name="share_halves", in_specs=[ANY] * n, out_specs=[ANY] * n,
        out_shape=[jax.ShapeDtypeStruct(h.shape, h.dtype) for h in halves],
        scratch_shapes=[pltpu.SemaphoreType.DMA((n,)), pltpu.SemaphoreType.DMA((n,))],
    )(*halves)


def _exchange_small(v, reduce):
    rows = v.shape[0]

    def body(v_ref, out_ref, buf, ssem, rsem):
        x, y, c = _me()
        me = 4 * x + 2 * y + c
        peers = [((x + bx) % 2, (y + by) % 2, (c + bc) % 2)
                 for bx in (0, 1) for by in (0, 1) for bc in (0, 1) if (bx, by, bc) != (0, 0, 0)]
        dst = buf if reduce else out_ref
        dst[me] = v_ref[...]
        sends = [_remote(v_ref, dst.at[me], ssem.at[r], rsem.at[r], p) for r, p in enumerate(peers)]
        for cp in sends:
            cp.start()
        for r, (px, py, pc) in enumerate(peers):
            blk = dst.at[4 * px + 2 * py + pc]
            _remote(blk, blk, ssem.at[r], rsem.at[r], (px, py, pc)).wait_recv()
        if reduce:
            acc = buf[0]
            for d in range(1, N_DEV):
                acc = acc + buf[d]
            out_ref[...] = acc
        for cp in sends:
            cp.wait_send()

    vm = pl.BlockSpec(memory_space=pltpu.VMEM)
    out_shape = jax.ShapeDtypeStruct((rows, LANES) if reduce else (N_DEV, rows, LANES), F32)
    buf_shape = (N_DEV, rows, LANES) if reduce else (SUBLANES, LANES)
    return _pcall(
        body, pin=False, name="reduce_small" if reduce else "gather_small", in_specs=[vm], out_specs=vm, out_shape=out_shape,
        scratch_shapes=[pltpu.VMEM(buf_shape, F32), pltpu.SemaphoreType.DMA((N_DEV - 1,)),
                        pltpu.SemaphoreType.DMA((N_DEV - 1,))],
        compiler_params=pltpu.CompilerParams(vmem_limit_bytes=32 * 1024 * 1024),
    )(v)


def _add_pair(name, core, g, theirs):
    _, half, cols = theirs.shape
    tr = _tile(half, (256, 176, 128))
    nb = half // tr

    def body(c_ref, g_ref, t_ref, o32_ref, o16_ref):
        s = g_ref[...] + t_ref[...]
        o32_ref[...] = s
        o16_ref[...] = s.astype(BF16)

    spec = pl.BlockSpec((None, tr, cols), lambda k, i, c_ref: (k, i, 0))
    grid_spec = pltpu.PrefetchScalarGridSpec(
        num_scalar_prefetch=1, grid=(N_CHIPS, nb),
        in_specs=[pl.BlockSpec((None, tr, cols), lambda k, i, c_ref: (k, c_ref[0] * nb + i, 0)), spec],
        out_specs=[spec, spec])
    return _pcall(
        body, name=name, grid_spec=grid_spec,
        out_shape=[jax.ShapeDtypeStruct(theirs.shape, F32), jax.ShapeDtypeStruct(theirs.shape, BF16)],
        compiler_params=_params(("arbitrary", "arbitrary"), 8 * _nbytes((tr, cols + LANES), F32)),
    )(core, g, theirs)


def _add_chips(name, chip, p32, recv):
    _, half, cols = p32.shape
    tr = _tile(half, (256, 176, 128))

    def body(k_ref, p_ref, r0_ref, r1_ref, r2_ref, o_ref):
        o_ref[...] = ((p_ref[...] + r0_ref[...].astype(F32)) + r1_ref[...].astype(F32)) + r2_ref[...].astype(F32)

    def other(r):
        return pl.BlockSpec((None, tr, cols), lambda i, k_ref: (r + (k_ref[0] <= r).astype(jnp.int32), i, 0))
    grid_spec = pltpu.PrefetchScalarGridSpec(
        num_scalar_prefetch=1, grid=(half // tr,),
        in_specs=[pl.BlockSpec((None, tr, cols), lambda i, k_ref: (k_ref[0], i, 0)), other(0), other(1), other(2)],
        out_specs=pl.BlockSpec((tr, cols), lambda i, k_ref: (i, 0)))
    return _pcall(
        body, name=name, grid_spec=grid_spec, out_shape=jax.ShapeDtypeStruct((half, cols), F32),
        compiler_params=_params(("arbitrary",), 10 * _nbytes((tr, cols + LANES), F32)),
    )(chip, p32, recv, recv, recv)


def kernel(x, mem, w_in, b_in, hg_lb_logits, hg_norm_w, ml_conv_w, ml_conv_b, ml_norm_w, w_out, ln1_g, ln1_b, ca_wq, ca_wkv, ca_wo, ln2_g, ln2_b, ffn_w_up, ffn_conv_w, ffn_conv_b, ffn_w_down, ln3_g, ln3_b, loss_target, m_w_in, m_b_in, m_hg_lb_logits, m_hg_norm_w, m_ml_conv_w, m_ml_conv_b, m_ml_norm_w, m_w_out, m_ln1_g, m_ln1_b, m_ca_wq, m_ca_wkv, m_ca_wo, m_ln2_g, m_ln2_b, m_ffn_w_up, m_ffn_conv_w, m_ffn_conv_b, m_ffn_w_down, m_ln3_g, m_ln3_b, v_w_in, v_b_in, v_hg_lb_logits, v_hg_norm_w, v_ml_conv_w, v_ml_conv_b, v_ml_norm_w, v_w_out, v_ln1_g, v_ln1_b, v_ca_wq, v_ca_wkv, v_ca_wo, v_ln2_g, v_ln2_b, v_ffn_w_up, v_ffn_conv_w, v_ffn_conv_b, v_ffn_w_down, v_ln3_g, v_ln3_b):
    return _train_step(dict(locals()))


WEIGHTS = ("w_in", "b_in", "hg_lb_logits", "hg_norm_w", "ml_conv_w", "ml_conv_b", "ml_norm_w", "w_out", "ln1_g",
           "ln1_b", "ca_wq", "ca_wkv", "ca_wo", "ln2_g", "ln2_b", "ffn_w_up", "ffn_conv_w", "ffn_conv_b",
           "ffn_w_down", "ln3_g", "ln3_b")
MATRICES = ("w_in", "w_out", "ca_wq", "ca_wkv", "ca_wo", "ffn_w_up", "ffn_w_down")
COL_SHARDED = ("w_in", "ca_wkv", "ffn_w_up", "ml_conv_w", "ffn_conv_w")
SMALL = tuple(n for n in WEIGHTS if n not in MATRICES)
PART_ROWS = 16


def _part_rows(shape, lead):
    n = 1
    for s in shape[lead:]:
        n *= s
    return -(-n // (LANES * PART_ROWS)) * PART_ROWS


def _pack(arrs, dtype, lead=0, rows=None):
    parts = []
    for a in arrs:
        head = a.shape[:lead]
        flat = a.reshape(head + (-1,)).astype(dtype)
        pad = _part_rows(a.shape, lead) * LANES - flat.shape[-1]
        flat = jnp.pad(flat, [(0, 0)] * lead + [(0, pad)])
        parts.append(flat.reshape(head + (-1, LANES)))
    used = sum(p.shape[lead] for p in parts)
    if rows is not None and rows > used:
        parts.append(jnp.zeros(parts[0].shape[:lead] + (rows - used, LANES), dtype))
    return jnp.concatenate(parts, axis=lead)


def _unpack(buf, shapes):
    lead = buf.shape[:-2]
    outs, r = [], 0
    for sh in shapes:
        n = 1
        for s in sh:
            n *= s
        nr = _part_rows(sh, 0)
        flat = buf[..., r:r + nr, :].reshape(lead + (nr * LANES,))
        outs.append(flat[..., :n].reshape(lead + tuple(sh)))
        r += nr
    return outs


def _cat_cols(s):
    return jnp.moveaxis(s, 0, 1).reshape(s.shape[1], -1)


def _stack_rows(s):
    return s.reshape(-1, s.shape[-1])


def _train_step(a):
    xs, mems, tgt = a["x"][0], a["mem"][0], a["loss_target"][0]
    core = lax.axis_index("c").astype(jnp.int32).reshape(1)
    chip = (2 * lax.axis_index("x") + lax.axis_index("y")).astype(jnp.int32).reshape(1)
    k_me = chip[0]
    shard = {n: a[n][0] for n in MATRICES}

    later = [n for n in MATRICES if n != "w_in"]
    taps = _exchange_small(_pack([a["ml_conv_w"][0], a["ffn_conv_w"][0]], F32), reduce=False)
    w = {"w_in": jnp.pad(_cat_cols(_gather_weights([shard["w_in"].astype(BF16)])[0]), ((0, 0), (0, D_IN_PAD - D_IN)))}
    gathering, token = _gather_start("gather_start", [shard[n].astype(BF16) for n in later])
    taps = taps.reshape((N_CHIPS, 2) + taps.shape[1:])[:, 0]
    ml_cw, ffn_cw = [_cat_cols(s) for s in _unpack(taps, [a["ml_conv_w"].shape[1:], a["ffn_conv_w"].shape[1:]])]
    b_in_p = jnp.pad(a["b_in"], ((0, 0), (0, D_IN_PAD - D_IN))) + token[0:1, 0:1]
    mixer_w = (a["hg_lb_logits"], a["hg_norm_w"], ml_cw, a["ml_conv_b"], a["ml_norm_w"])
    up_cols = a["ffn_w_up"].shape[-1]

    xb = xs.astype(BF16)
    proj = _mm("proj", "nn", xb, w["w_in"], bias=b_in_p, tm=256, tn=D_IN_PAD)
    y, hst, cst, nst, mst = _mixer_fwd(proj, *mixer_w)
    w.update(zip(later, _forward_halves("forward_halves", _gather_wait("gather_wait", gathering, y))))
    for n in ("w_out", "ca_wq", "ca_wo", "ffn_w_down"):
        w[n] = _stack_rows(w[n])
    z1, x1, x1b = _mm("mix_out", "nn", y, w["w_out"], res=xs, res_scale=ALPHA, ln=("fwd", a["ln1_g"], a["ln1_b"]),
                      copy_dtype=BF16)
    q = _mm("ca_q", "nn", x1b, w["ca_wq"], out_dtype=BF16, tn=D_MODEL)
    kv = _mm("ca_kv", "nn", mems, w["ca_wkv"])
    o = _attn_fwd(q, kv)
    z2, x2, x2b = _mm("ca_out", "nn", o, w["ca_wo"], res=x1, res_scale=ALPHA, ln=("fwd", a["ln2_g"], a["ln2_b"]),
                      copy_dtype=BF16)
    w_up = _cat_cols(w["ffn_w_up"])
    u, hmid, dz3, g_ln3g, g_ln3b, loss_part, dz3b = _ffn_fwd(
        x2b, x2, w_up, ffn_cw, a["ffn_conv_b"], w["ffn_w_down"], a["ln3_g"], a["ln3_b"], tgt)

    grads = {"ln3_g": g_ln3g, "ln3_b": g_ln3b}
    grads["ffn_w_down"] = _mm("g_w_down", "tn", hmid, dz3b, tm=D_FF // 2, tn=D_MODEL)
    du, g_cw, g_cb, dz2, grads["ln2_g"], grads["ln2_b"], dz2b = _ffn_bwd(
        u, ffn_cw, a["ffn_conv_b"], dz3b, dz3, w["ffn_w_down"], w_up, z2, a["ln2_g"], a["ln2_b"])
    grads["ffn_conv_w"] = jnp.transpose(g_cw, (2, 1, 0, 3)).reshape(FFN_CONV, 2 * D_FF)
    grads["ffn_conv_b"] = jnp.transpose(g_cb, (2, 1, 0, 3)).reshape(1, 2 * D_FF)
    grads["ffn_w_up"] = _mm("g_w_up", "tn", x2b, du, out_groups=N_CHIPS, tm=D_MODEL, tn=up_cols)
    grads["ffn_w_down"] = grads["ffn_w_down"].reshape((N_CHIPS,) + shard["ffn_w_down"].shape)
    pending = {}

    def reduce_start(tag, names, swapped=None):
        group = [grads[n] for n in names]
        group, theirs = swapped or (group, _swap_halves("swap_halves_" + tag, group))
        sums = [_add_pair("add_pair_" + n, core, g, t) for n, g, t in zip(names, group, theirs)]
        handle, token = _scatter_start("scatter_start_" + tag, [s16 for _, s16 in sums])
        pending[tag] = (names, [s32 for s32, _ in sums], handle)
        return token[0:1, 0:1]

    ffn = ("ffn_w_up", "ffn_w_down")
    swapping, token = _swap_start("swap_start_ffn", [grads[n] for n in ffn])
    do = _mm("d_o", "nt", dz2b, w["ca_wo"], bias=jnp.zeros((1, D_MODEL), F32) + token[0:1, 0:1], out_dtype=BF16,
             tn=D_MODEL)
    grads["ca_wo"] = _mm("g_wo", "tn", o, dz2b, tm=D_MODEL, tn=D_MODEL)
    zero = reduce_start("ffn", ffn, _swap_wait("swap_wait_ffn", swapping, grads["ca_wo"]))
    dq, dkv = _attn_bwd(q, kv + zero, do)
    grads["ca_wq"] = _mm("g_wq", "tn", x1b, dq, tm=D_MODEL, tn=D_MODEL)
    grads["ca_wkv"] = _mm("g_wkv", "tn", mems, dkv, out_groups=N_CHIPS, tm=D_MODEL)
    dz1, grads["ln1_g"], grads["ln1_b"], dz1b = _mm("d_x1", "nt", dq, w["ca_wq"], res=dz2, res_scale=ALPHA,
                                                    ln=("bwd", z1, a["ln1_g"], a["ln1_b"]), copy_dtype=BF16)
    dy = _mm("d_y", "nt", dz1b, w["w_out"], tn=D_MODEL)
    grads["w_out"] = _mm("g_w_out", "tn", y, dz1b, tm=D_MODEL, tn=D_MODEL)
    for n in ("w_out", "ca_wq", "ca_wo"):
        grads[n] = grads[n].reshape((N_CHIPS,) + shard[n].shape)
    zero = reduce_start("attn", ("w_out", "ca_wq", "ca_wkv", "ca_wo"))
    (dproj, g_b_in, grads["hg_lb_logits"], grads["hg_norm_w"], grads["ml_conv_w"], grads["ml_conv_b"],
     grads["ml_norm_w"]) = _mixer_bwd(proj, dy, hst, cst, nst, mst, mixer_w[0], mixer_w[1] + zero, *mixer_w[2:])
    grads["b_in"] = g_b_in[:, :D_IN]

    small_shapes = [grads[n].shape for n in SMALL] + [loss_part.shape]
    mine_small = _pack([grads[n] for n in SMALL] + [loss_part], F32)
    everyone = lax.dynamic_update_slice(lax.empty((N_DEV,) + mine_small.shape, F32), mine_small[None],
                                        (4 * lax.axis_index("x") + 2 * lax.axis_index("y") + lax.axis_index("c"), 0, 0))
    small_going, token = _split_start("small_start", [mine_small], [everyone], N_DEV - 1, _small_plan)
    g_in = _mm("g_w_in", "tn", xb, dproj, bias=jnp.zeros((1, D_IN_PAD), F32) + token[0:1, 0:1],
               tm=D_MODEL, tn=up_cols)[:, :D_IN]
    grads["w_in"] = jnp.moveaxis(g_in.reshape(D_MODEL, N_CHIPS, -1), 1, 0)
    zero = reduce_start("in", ("w_in",))

    def finish(tag, after):
        names, sums32, handle = pending[tag]
        return {n: _add_chips("add_chips_" + n, chip, s32, r)
                for n, s32, r in zip(names, sums32, _scatter_wait("scatter_wait_" + tag, handle, after))}

    halves = {**finish("ffn", dproj), **finish("attn", dproj)}
    early = [n for n in MATRICES if n in halves]
    sharing, token = _split_start("share_start", [halves[n] for n in early],
                                  [lax.empty(halves[n].shape, F32) for n in early], len(early), _share_plan)
    dx = _mm("d_x", "nt", dproj, w["w_in"], bias=jnp.zeros((1, D_MODEL), F32) + zero + token[0:1, 0:1], res=dz1,
             res_scale=ALPHA, tm=256, tn=D_MODEL)
    halves.update(finish("in", dx))
    other_halves = dict(zip(early, _split_wait("share_wait", sharing, dx, _share_plan)[1]))
    other_halves["w_in"] = _share_halves([halves["w_in"]])[0]
    halves, other_halves = [halves[n] for n in MATRICES], [other_halves[n] for n in MATRICES]

    summed = _unpack(_sum_devices(_split_wait("small_wait", small_going, dx, _small_plan)[1][0]), small_shapes)
    loss = summed[-1][0, 0]
    for n, g in zip(SMALL, summed[:-1]):
        if n in COL_SHARDED:
            cols = a[n].shape[-1]
            g = lax.dynamic_slice_in_dim(g, k_me * cols, cols, axis=1)
        grads[n] = g

    delta, new_m, new_v = {}, {}, {}
    for n, mine, theirs in zip(MATRICES, halves, other_halves):
        grads[n], delta[n], new_m[n], new_v[n] = _adamw_halves(
            "adamw_" + n, core, shard[n], mine, theirs, a["m_" + n][0], a["v_" + n][0])
    small_w = [a[n][0] if a[n].ndim == 3 else a[n] for n in SMALL]
    small_m = [a["m_" + n][0] if a[n].ndim == 3 else a["m_" + n] for n in SMALL]
    small_v = [a["v_" + n][0] if a[n].ndim == 3 else a["v_" + n] for n in SMALL]
    shapes = [w.shape for w in small_w]
    packed = [_pack(l, F32) for l in (small_w, [grads[n] for n in SMALL], small_m, small_v)]
    for out, buf in zip((delta, new_m, new_v), _adamw("adamw_small", *packed)):
        for n, v in zip(SMALL, _unpack(buf, shapes)):
            out[n] = v

    def shaped(d):
        return [d[n].reshape(a[n].shape) for n in WEIGHTS]
    return (loss, dx[None], *shaped(grads), *shaped(delta), *shaped(new_m), *shaped(new_v))
```

```python
import functools

import jax
import jax.numpy as jnp
from jax import lax
from jax.experimental import pallas as pl
from jax.experimental.pallas import tpu as pltpu

F32 = jnp.float32
BF16 = jnp.bfloat16

D_MODEL = 1024
HEADS = 4
DK = 128
D_GRP = HEADS * DK
CHUNK = 64
ML_CONV = 4
FFN_CONV = 3
D_FF = 2816
CA_DH = D_MODEL // HEADS
DEPTH = 1
ALPHA = (2.0 * DEPTH) ** 0.25
LN_EPS = 1e-5
NEG_BIG = -1e30
D_IN = 8 * D_GRP + 2 * HEADS
D_IN_PAD = 8 * D_GRP + 128
ADAM_LR, ADAM_B1, ADAM_B2, ADAM_EPS, ADAM_WD, ADAM_STEP = 0.001, 0.9, 0.999, 1e-08, 0.01, 10

SUBLANES = 8
LANES = 128
VMEM_BYTES = 64 * 1024 * 1024


def _pcall(body, pin=True, **kw):
    if not pin:
        return _call(body, **kw)
    kw["out_shape"] = jax.tree.map(lambda s: pltpu.HBM(s.shape, s.dtype), kw["out_shape"])
    call = _call(body, **kw)

    def pinned(*args):
        return call(*[pltpu.with_memory_space_constraint(x, pltpu.HBM) if jnp.issubdtype(x.dtype, jnp.floating) else x
                      for x in args])
    return pinned


def _call(body, **kw):
    return pl.pallas_call(body, **kw)


def _params(semantics, vmem_bytes):
    limit = int(min(max(2 * vmem_bytes, 16 * 1024 * 1024), VMEM_BYTES - 8 * 1024 * 1024))
    return pltpu.CompilerParams(dimension_semantics=semantics, vmem_limit_bytes=limit)


def _nbytes(shape, dtype):
    n = 1
    for s in shape:
        n *= s
    return n * jnp.dtype(dtype).itemsize


def _dg(a, b, ca, cb):
    return lax.dot_general(a.astype(BF16), b.astype(BF16), (((ca,), (cb,)), ((), ())),
                           preferred_element_type=F32)


@jax.custom_vjp
def mm_nn(a, b):
    return _dg(a, b, 1, 0)


mm_nn.defvjp(lambda a, b: (_dg(a, b, 1, 0), (a, b)),
             lambda r, g: (_dg(g, r[1], 1, 1).astype(r[0].dtype), _dg(r[0], g, 0, 0).astype(r[1].dtype)))


@jax.custom_vjp
def mm_nt(a, b):
    return _dg(a, b, 1, 1)


mm_nt.defvjp(lambda a, b: (_dg(a, b, 1, 1), (a, b)),
             lambda r, g: (_dg(g, r[1], 1, 0).astype(r[0].dtype), _dg(g, r[0], 0, 0).astype(r[1].dtype)))


@jax.custom_vjp
def mm_tn(a, b):
    return _dg(a, b, 0, 0)


mm_tn.defvjp(lambda a, b: (_dg(a, b, 0, 0), (a, b)),
             lambda r, g: (_dg(r[1], g, 1, 1).astype(r[0].dtype), _dg(r[0], g, 1, 0).astype(r[1].dtype)))


def _tri(n, lower):
    r = lax.broadcasted_iota(jnp.int32, (n, n), 0)
    c = lax.broadcasted_iota(jnp.int32, (n, n), 1)
    return ((r >= c) if lower else (r <= c)).astype(F32)


def _tri_dot(lower, x):
    t = _tri(x.shape[0], lower).astype(BF16)
    hi = x.astype(BF16)
    rest = x - hi.astype(F32)
    mid = rest.astype(BF16)
    lo = (rest - mid.astype(F32)).astype(BF16)
    return sum(lax.dot_general(t, p, (((1,), (0,)), ((), ())), preferred_element_type=F32) for p in (hi, mid, lo))


@jax.custom_vjp
def cumsum_rows(x):
    return _tri_dot(True, x)


cumsum_rows.defvjp(lambda x: (_tri_dot(True, x), None), lambda _, g: (_tri_dot(False, g),))


def _shift_impl(halo, x, d):
    xx = jnp.concatenate([halo, x], axis=0)
    return pltpu.roll(xx, d, 0)[SUBLANES:]


@functools.partial(jax.custom_vjp, nondiff_argnums=(2,))
def shift_rows(halo, x, d):
    return _shift_impl(halo, x, d)


def _shift_bwd(d, _, g):
    n = g.shape[0] + SUBLANES
    gg = jnp.concatenate([jnp.zeros((SUBLANES, g.shape[1]), g.dtype), g], axis=0)
    r = pltpu.roll(gg, n - d, 0)
    return r[:SUBLANES], r[SUBLANES:]


shift_rows.defvjp(lambda halo, x, d: (_shift_impl(halo, x, d), None), _shift_bwd)


def causal_conv(halo, x, w_rows, b):
    k = len(w_rows)
    y = b + w_rows[k - 1] * x
    for d in range(1, k):
        y = y + w_rows[k - 1 - d] * shift_rows(halo, x, d)
    return y


def _sigmoid(x):
    return 1.0 / (1.0 + jnp.exp(-x))


def _silu(x):
    return x * _sigmoid(x)


def _log_sigmoid(x):
    return jnp.minimum(x, 0.0) - jnp.log(1.0 + jnp.exp(-jnp.abs(x)))


def _pick_row(x, i):
    row = lax.broadcasted_iota(jnp.int32, (x.shape[0], 1), 0)
    return jnp.sum(jnp.where(row == i, x, 0.0), axis=0, keepdims=True)


def _layer_norm(z, g, b):
    mu = jnp.mean(z, axis=-1, keepdims=True)
    zc = z - mu
    var = jnp.mean(zc * zc, axis=-1, keepdims=True)
    return zc * lax.rsqrt(var + LN_EPS) * g + b


def _qk_conv(halo, x, w0, w1, w2, w3, b):
    return _silu(causal_conv(halo, x, (w0, w1, w2, w3), b))


def _grp(i, h=None):
    if h is None:
        return pl.ds(i * D_GRP, D_GRP)
    return pl.ds(i * D_GRP + h * DK, DK)


def _mixer_specs(n_chunks, reverse):
    def chunk(c):
        return n_chunks - 1 - c if reverse else c
    row8 = CHUNK // SUBLANES
    proj_spec = pl.BlockSpec((CHUNK, D_IN_PAD), lambda c: (chunk(c), 0))
    halo_spec = pl.BlockSpec((SUBLANES, 2 * D_GRP), lambda c: (jnp.maximum(chunk(c) * row8 - 1, 0), 2))
    small = [pl.BlockSpec((2, D_GRP), lambda c: (0, 0)), pl.BlockSpec((1, D_GRP), lambda c: (0, 0)),
             pl.BlockSpec((ML_CONV, 2 * D_GRP), lambda c: (0, 0)), pl.BlockSpec((1, 2 * D_GRP), lambda c: (0, 0)),
             pl.BlockSpec((1, D_GRP), lambda c: (0, 0))]
    state_specs = [pl.BlockSpec((1, HEADS, DK, DK), lambda c: (chunk(c), 0, 0, 0)),
                   pl.BlockSpec((1, HEADS, DK, DK), lambda c: (chunk(c), 0, 0, 0)),
                   pl.BlockSpec((1, HEADS, 1, DK), lambda c: (chunk(c), 0, 0, 0)),
                   pl.BlockSpec((1, HEADS, 1, DK), lambda c: (chunk(c), 0, 0, 0))]
    y_spec = pl.BlockSpec((CHUNK, 2 * D_GRP), lambda c: (chunk(c), 0))
    return proj_spec, halo_spec, small, state_specs, y_spec, chunk


def _heads(x):
    return [x[:, h * DK:(h + 1) * DK] for h in range(HEADS)]


def _last(x, j):
    lane = lax.broadcasted_iota(jnp.int32, (1, x.shape[-1]), 1)
    return jnp.sum(jnp.where(lane == j, x, 0.0), axis=-1, keepdims=True)


def _hg_chunk(st_t, hq, hf, hi, hgate, l0, l1, nw):
    n = hq.shape[0]
    lb = _sigmoid(l0 - l1)
    q = _silu(hq)
    lf = jnp.log(lb + (1.0 - lb) * _sigmoid(hf))
    k = (1.0 - lb) * _sigmoid(-hf)
    b = cumsum_rows(lf)
    b_ref = _pick_row(b, n // 2 - 1)
    b_last = _pick_row(b, n - 1)
    qa, ka =_heads(q * jnp.exp(b - b_ref)), _heads(k * jnp.exp(b_ref - b))
    qe, kd, eb, v = _heads(q * jnp.exp(b)), _heads(k * jnp.exp(b_last - b)), _heads(jnp.exp(b_last)), _heads(hi)
    tri = _tri(n, True) > 0
    attn = [jnp.where(tri, mm_nt(qa[h], ka[h]), 0.0) for h in range(HEADS)]
    o = [mm_nn(attn[h], v[h]) + mm_nt(qe[h], st_t[h]) for h in range(HEADS)]
    st_new = jnp.stack([eb[h] * st_t[h] + mm_tn(v[h], kd[h]) for h in range(HEADS)])
    yn = [o[h] * lax.rsqrt(jnp.mean(o[h] * o[h], axis=-1, keepdims=True) + LN_EPS) for h in range(HEADS)]
    return st_new, jnp.concatenate(yn, axis=1) * nw * _silu(hgate)


def _ml_chunk(c_st, n_st, m_st, q, k, v, gates, og, nw):
    n = q.shape[0]
    ig = jnp.stack([_last(gates, h) for h in range(HEADS)])
    log_f = _log_sigmoid(gates)
    fl = jnp.stack([_last(log_f, HEADS + h) for h in range(HEADS)])
    bw = cumsum_rows(jnp.concatenate([jnp.broadcast_to(fl[h], (n, DK)) for h in range(HEADS)], axis=1))
    b = jnp.stack([_last(x, 0) for x in _heads(bw)])
    g = jnp.sum(fl, axis=1, keepdims=True)
    eye = lax.broadcasted_iota(jnp.int32, (n, n), 0) == lax.broadcasted_iota(jnp.int32, (n, n), 1)
    e_row = jnp.sum(jnp.where(eye, ig - b, 0.0), axis=1, keepdims=True)
    d = jnp.where(_tri(n, True) > 0, b + e_row, -jnp.inf)
    inter = b + m_st
    m_t = jnp.maximum(inter, jnp.max(d, axis=2, keepdims=True))
    qs, kh, vh = _heads(q * (DK ** -0.5)), _heads(k), _heads(v)
    s = jnp.stack([mm_nt(qs[h], kh[h]) for h in range(HEADS)]) * jnp.exp(d - m_t)
    w_inter = jnp.exp(inter - m_t)
    num = (jnp.stack([mm_nn(s[h], vh[h]) for h in range(HEADS)])
           + w_inter * jnp.stack([mm_nn(qs[h], c_st[h]) for h in range(HEADS)]))
    den = jnp.sum(s, axis=2, keepdims=True) + w_inter * jnp.sum(jnp.stack(qs) * n_st, axis=2, keepdims=True)
    h_out = num / jnp.maximum(jnp.abs(den), jnp.exp(-m_t))
    a = g - b + ig
    m_new = jnp.maximum(g + m_st, jnp.max(a, axis=1, keepdims=True))
    decay = jnp.exp(g + m_st - m_new)
    wk = jnp.stack(kh) * jnp.exp(a - m_new)
    c_new = decay * c_st + jnp.stack([mm_tn(wk[h], vh[h]) for h in range(HEADS)])
    n_new = decay * n_st + jnp.sum(wk, axis=1, keepdims=True)
    hc = h_out - jnp.mean(h_out, axis=-1, keepdims=True)
    yn = hc * lax.rsqrt(jnp.mean(hc * hc, axis=-1, keepdims=True) + LN_EPS)
    y = _sigmoid(og) * (jnp.concatenate([yn[h] for h in range(HEADS)], axis=1) * nw)
    return c_new, n_new, m_new, y


def _mixer_inputs(proj_ref, lg_ref, hnw_ref, mnw_ref, qk):
    hg_in = (proj_ref[:, _grp(0)], proj_ref[:, _grp(1)], proj_ref[:, _grp(2)], proj_ref[:, _grp(3)],
             lg_ref[0:1, :], lg_ref[1:2, :], hnw_ref[...])
    ml_in = (qk[:, :D_GRP], qk[:, D_GRP:], proj_ref[:, _grp(6)], proj_ref[:, pl.ds(8 * D_GRP, LANES)],
             proj_ref[:, _grp(7)], mnw_ref[...])
    return hg_in, ml_in


def _mixer_fwd(proj, lb_logits, hg_nw, conv_w, conv_b, ml_nw):
    seq = proj.shape[0]
    n_chunks = seq // CHUNK
    proj_spec, halo_spec, small, state_specs, y_spec, _ = _mixer_specs(n_chunks, False)

    def body(proj_ref, halo_ref, lg_ref, hnw_ref, cw_ref, cb_ref, mnw_ref,
             y_ref, hst_ref, cst_ref, nst_ref, mst_ref, hs, cs, ns, ms):
        c = pl.program_id(0)

        @pl.when(c == 0)
        def _():
            hs[...] = jnp.zeros_like(hs)
            cs[...] = jnp.zeros_like(cs)
            ns[...] = jnp.zeros_like(ns)
            ms[...] = jnp.full(ms.shape, NEG_BIG, F32)

        hst_ref[0] = hs[...]
        cst_ref[0] = cs[...]
        nst_ref[0] = ns[...]
        mst_ref[0] = ms[...]
        halo = jnp.where(c > 0, halo_ref[...], 0.0)
        qk = _qk_conv(halo, proj_ref[:, pl.ds(4 * D_GRP, 2 * D_GRP)],
                      cw_ref[0:1, :], cw_ref[1:2, :], cw_ref[2:3, :], cw_ref[3:4, :], cb_ref[...])
        hg_in, ml_in = _mixer_inputs(proj_ref, lg_ref, hnw_ref, mnw_ref, qk)
        hs[...], y_hg = _hg_chunk(hs[...], *hg_in)
        cs[...], ns[...], m_new, y_ml = _ml_chunk(cs[...], ns[...], _last(ms[...], 0), *ml_in)
        ms[...] = jnp.broadcast_to(m_new, ms.shape)
        y_ref[:, pl.ds(0, D_GRP)] = y_hg.astype(BF16)
        y_ref[:, pl.ds(D_GRP, D_GRP)] = y_ml.astype(BF16)

    st = jax.ShapeDtypeStruct((n_chunks, HEADS, DK, DK), F32)
    vec = jax.ShapeDtypeStruct((n_chunks, HEADS, 1, DK), F32)
    vmem = 2 * (_nbytes((CHUNK, D_IN_PAD), F32) + _nbytes((CHUNK, 2 * D_GRP), F32) + 2 * _nbytes((HEADS, DK, DK), F32)) \
        + 2 * _nbytes((HEADS, DK, DK), F32)
    return _pcall(
        body, name="mixer_fwd", grid=(n_chunks,),
        in_specs=[proj_spec, halo_spec] + small,
        out_specs=[y_spec] + state_specs,
        out_shape=[jax.ShapeDtypeStruct((seq, 2 * D_GRP), BF16), st, st, vec, vec],
        scratch_shapes=[pltpu.VMEM((HEADS, DK, DK), F32), pltpu.VMEM((HEADS, DK, DK), F32),
                        pltpu.VMEM((HEADS, 1, DK), F32), pltpu.VMEM((HEADS, 1, DK), F32)],
        compiler_params=_params(("arbitrary",), vmem),
    )(proj, proj, lb_logits, hg_nw, conv_w, conv_b, ml_nw)


def _mixer_bwd(proj, dy, hst, cst, nst, mst, lb_logits, hg_nw, conv_w, conv_b, ml_nw):
    seq = proj.shape[0]
    n_chunks = seq // CHUNK
    proj_spec, halo_spec, small, state_specs, y_spec, _ = _mixer_specs(n_chunks, True)

    def body(proj_ref, halo_ref, dy_ref, hst_ref, cst_ref, nst_ref, mst_ref,
             lg_ref, hnw_ref, cw_ref, cb_ref, mnw_ref,
             dproj_ref, dbin_ref, dlg_ref, dhnw_ref, dcw_ref, dcb_ref, dmnw_ref,
             dhs, dcs, dns, dms, dhalo):
        c = pl.program_id(0)

        @pl.when(c == 0)
        def _():
            for r in (dhs, dcs, dns, dms, dhalo, dbin_ref, dlg_ref, dhnw_ref, dcw_ref, dcb_ref, dmnw_ref):
                r[...] = jnp.zeros_like(r)

        def put(cols, val):
            dproj_ref[:, cols] = val.astype(BF16)
            dbin_ref[:, cols] += jnp.sum(val, axis=0, keepdims=True)

        first = c == n_chunks - 1
        halo = jnp.where(first, 0.0, halo_ref[...])
        x_qk = proj_ref[:, pl.ds(4 * D_GRP, 2 * D_GRP)]
        conv_args = (halo, x_qk, cw_ref[0:1, :], cw_ref[1:2, :], cw_ref[2:3, :], cw_ref[3:4, :], cb_ref[...])
        qk, conv_vjp = jax.vjp(_qk_conv, *conv_args)
        hg_in, ml_in = _mixer_inputs(proj_ref, lg_ref, hnw_ref, mnw_ref, qk)
        _, hg_vjp = jax.vjp(_hg_chunk, hst_ref[0], *hg_in)
        _, ml_vjp = jax.vjp(_ml_chunk, cst_ref[0], nst_ref[0], _last(mst_ref[0], 0), *ml_in)
        dst, dhq, dhf, dhi, dhg, dl0, dl1, dnw = hg_vjp((dhs[...], dy_ref[:, pl.ds(0, D_GRP)]))
        dc, dn, dm, dq, dk, dv, dgates, dog, dmn = ml_vjp(
            (dcs[...], dns[...], _last(dms[...], 0), dy_ref[:, pl.ds(D_GRP, D_GRP)]))
        dhs[...] = dst
        dcs[...] = dc
        dns[...] = dn
        dms[...] = jnp.broadcast_to(dm, dms.shape)
        for i, val in ((0, dhq), (1, dhf), (2, dhi), (3, dhg), (6, dv), (7, dog)):
            put(_grp(i), val)
        put(pl.ds(8 * D_GRP, LANES), dgates)
        dlg_ref[0:1, :] += dl0
        dlg_ref[1:2, :] += dl1
        dhnw_ref[...] += dnw
        dmnw_ref[...] += dmn
        dh, dx, dw0, dw1, dw2, dw3, db = conv_vjp(jnp.concatenate([dq, dk], axis=1))
        tail = jnp.concatenate([jnp.zeros((CHUNK - SUBLANES, 2 * D_GRP), F32), dhalo[...]], axis=0)
        put(pl.ds(4 * D_GRP, 2 * D_GRP), dx + tail)
        dhalo[...] = dh
        for d, dw in enumerate((dw0, dw1, dw2, dw3)):
            dcw_ref[d:d + 1, :] += dw
        dcb_ref[...] += db

    row = pl.BlockSpec((1, D_GRP), lambda c: (0, 0))
    small_out = [pl.BlockSpec((1, D_IN_PAD), lambda c: (0, 0)), pl.BlockSpec((2, D_GRP), lambda c: (0, 0)), row,
                 pl.BlockSpec((ML_CONV, 2 * D_GRP), lambda c: (0, 0)), pl.BlockSpec((1, 2 * D_GRP), lambda c: (0, 0)), row]
    dy_spec = pl.BlockSpec((CHUNK, 2 * D_GRP), y_spec.index_map)
    vmem = 2 * (2 * _nbytes((CHUNK, D_IN_PAD), F32) + _nbytes((CHUNK, 2 * D_GRP), F32)
                + 2 * _nbytes((HEADS, DK, DK), F32)) + 2 * _nbytes((HEADS, DK, DK), F32) + 4 * 1024 * 1024
    return _pcall(
        body, name="mixer_bwd", grid=(n_chunks,),
        in_specs=[proj_spec, halo_spec, dy_spec] + state_specs + small,
        out_specs=[proj_spec] + small_out,
        out_shape=[jax.ShapeDtypeStruct((seq, D_IN_PAD), BF16), jax.ShapeDtypeStruct((1, D_IN_PAD), F32),
                   jax.ShapeDtypeStruct((2, D_GRP), F32), jax.ShapeDtypeStruct((1, D_GRP), F32),
                   jax.ShapeDtypeStruct((ML_CONV, 2 * D_GRP), F32), jax.ShapeDtypeStruct((1, 2 * D_GRP), F32),
                   jax.ShapeDtypeStruct((1, D_GRP), F32)],
        scratch_shapes=[pltpu.VMEM((HEADS, DK, DK), F32), pltpu.VMEM((HEADS, DK, DK), F32),
                        pltpu.VMEM((HEADS, 1, DK), F32), pltpu.VMEM((HEADS, 1, DK), F32),
                        pltpu.VMEM((SUBLANES, 2 * D_GRP), F32)],
        compiler_params=_params(("arbitrary",), vmem),
    )(proj, proj, dy, hst, cst, nst, mst, lb_logits, hg_nw, conv_w, conv_b, ml_nw)


def _tile(n, prefs, unit=None):
    unit = unit or n
    for p in prefs:
        if unit % p == 0 and n % p == 0:
            return p
    return unit


def _logical(arr):
    return arr.shape if arr.ndim == 2 else (arr.shape[1], arr.shape[0] * arr.shape[2])


def _group(arr):
    return arr.shape[-1]


def _split_spec(ndim, group, tr, tc, where):
    if ndim == 2:
        return pl.BlockSpec((tr, tc), where)
    per = group // tc
    assert per * tc == group, (group, tc)

    def index(*ids):
        bi, bj = where(*ids)
        return (bj // per, bi, bj % per)
    return pl.BlockSpec((None, tr, tc), index)


def _mm(name, mode, a, b, *, bias=None, res=None, res_scale=1.0, ln=None, out_dtype=F32, out_groups=None,
        copy_dtype=None, tm=None, tn=None, tk=None):
    la, lb = _logical(a), _logical(b)
    if mode == "nn":
        (m, k), n = la, lb[1]
        n_unit = _group(b) if b.ndim == 3 else n
        kc = _group(a) if a.ndim == 3 else k
    elif mode == "nt":
        (m, k), n = la, lb[0]
        n_unit = n
        kc = min(_group(a) if a.ndim == 3 else k, _group(b) if b.ndim == 3 else k)
    else:
        (k, m), n = la, lb[1]
        n_unit, kc = (_group(b) if b.ndim == 3 else n), k
        assert a.ndim == 2
    if out_groups:
        n_unit = min(n_unit, n // out_groups)
    kind = ln[0] if ln else None
    tm = tm or (256 if ln else _tile(m, (512, 256, 128)))
    tn = n if ln else (tn or _tile(n, (512, 384, 256, 128), n_unit))
    tk = (tk or _tile(k, (2048, 512, 256, 128))) if mode == "tn" else k
    gi, gj, gk = m // tm, n // tn, k // tk
    assert gi * tm == m and gj * tn == n and gk * tk == k and n_unit % tn == 0, (name, m, n, k, tm, tn, tk)
    ca, cb = {"nn": (1, 0), "nt": (1, 1), "tn": (0, 0)}[mode]
    i_outer = gk > 1 or (gi - 1) * _nbytes(b.shape, b.dtype) <= (gj - 1) * _nbytes(a.shape, a.dtype)

    def ij(where):
        return (lambda p, q, kk: where(p, q, kk)) if i_outer else (lambda p, q, kk: where(q, p, kk))
    if mode == "tn":
        a_spec = pl.BlockSpec((tk, tm), ij(lambda i, j, kk: (kk, i)))
    elif a.ndim == 3:
        a_spec = pl.BlockSpec((a.shape[0], tm, _group(a)), ij(lambda i, j, kk: (0, i, 0)))
    else:
        a_spec = pl.BlockSpec((tm, k), ij(lambda i, j, kk: (i, 0)))
    if mode != "nt":
        b_spec = _split_spec(b.ndim, _group(b), tk, tn, ij(lambda i, j, kk: (kk, j)))
    elif b.ndim == 3:
        b_spec = pl.BlockSpec((b.shape[0], tn, _group(b)), ij(lambda i, j, kk: (0, j, 0)))
    else:
        b_spec = pl.BlockSpec((tn, k), ij(lambda i, j, kk: (j, 0)))
    row_spec = pl.BlockSpec((1, tn), ij(lambda i, j, kk: (0, j)))
    blk_spec = pl.BlockSpec((tm, tn), ij(lambda i, j, kk: (i, j)))
    ins, in_specs = [a, b], [a_spec, b_spec]
    if bias is not None:
        ins.append(bias), in_specs.append(row_spec)
    if res is not None:
        ins.append(res), in_specs.append(blk_spec)
    if kind == "fwd":
        ins += [ln[1], ln[2]]
        in_specs += [row_spec, row_spec]
    elif kind == "loss":
        ins += [ln[1], ln[2], ln[3]]
        in_specs += [row_spec, row_spec, blk_spec]
    elif kind == "bwd":
        ins += [ln[1], ln[2], ln[3]]
        in_specs += [blk_spec, row_spec, row_spec]
    if out_groups:
        blk_out = jax.ShapeDtypeStruct((out_groups, m, n // out_groups), out_dtype)
        out_spec = _split_spec(3, n // out_groups, tm, tn, ij(lambda i, j, kk: (i, j)))
    else:
        blk_out, out_spec = jax.ShapeDtypeStruct((m, n), out_dtype), blk_spec
    row_out = jax.ShapeDtypeStruct((1, n), F32)
    if kind is None:
        out_shape, out_specs = [blk_out], [out_spec]
    elif kind == "fwd":
        out_shape, out_specs = [blk_out, blk_out], [blk_spec, blk_spec]
    else:
        out_shape, out_specs = [blk_out, row_out, row_out], [blk_spec, row_spec, row_spec]
        if kind == "loss":
            out_shape.append(jax.ShapeDtypeStruct((1, LANES), F32))
            out_specs.append(pl.BlockSpec((1, LANES), lambda p, q, kk: (0, 0)))
    if copy_dtype is not None:
        out_shape.append(jax.ShapeDtypeStruct((m, n), copy_dtype))
        out_specs.append(blk_spec)
    n_in = len(ins)

    def body(*refs):
        in_refs, out_refs, acc_ref = refs[:n_in], refs[n_in:n_in + len(out_shape)], refs[-1]
        i, kk = pl.program_id(0 if i_outer else 1), pl.program_id(2)
        a_ref, b_ref = in_refs[:2]
        extra = list(in_refs[2:])

        def epilogue(acc, rows=slice(None)):
            rest = list(extra)
            if bias is not None:
                acc = acc + rest.pop(0)[...]
            if res is not None:
                acc = acc + res_scale * rest.pop(0)[rows, :]
            if kind is None:
                out_refs[0][...] = acc.astype(out_dtype)
                return
            if kind == "fwd":
                out_refs[0][rows, :] = acc
                y = _layer_norm(acc, rest[0][...], rest[1][...])
                out_refs[1][rows, :] = y
                if copy_dtype is not None:
                    out_refs[-1][rows, :] = y.astype(copy_dtype)
                return
            if kind == "loss":
                y, vjp = jax.vjp(_layer_norm, acc, rest[0][...], rest[1][...])
                err = y - rest[2][rows, :]
                part = 0.5 * jnp.sum(jnp.sum(err * err, axis=1, keepdims=True), axis=0, keepdims=True) / n
                dz, dg, db = vjp(err / n)
            else:
                _, vjp = jax.vjp(_layer_norm, rest[0][rows, :], rest[1][...], rest[2][...])
                dz, dg, db = vjp(acc)
            out_refs[0][rows, :] = dz
            out_refs[1][...] += dg
            out_refs[2][...] += db
            if kind == "loss":
                out_refs[3][...] += jnp.broadcast_to(part, (1, LANES))
            if copy_dtype is not None:
                out_refs[-1][rows, :] = dz.astype(copy_dtype)

        if kind in ("loss", "bwd"):
            @pl.when((i == 0) & (kk == 0))
            def _():
                for r in out_refs[1:3 + (kind == "loss")]:
                    r[...] = jnp.zeros_like(r)

        def chunk(ref, c0, last):
            if ref.ndim == 3:
                g = ref.shape[2]
                return ref[c0 // g, :, pl.ds(c0 % g, kc)]
            return ref[:, pl.ds(c0, kc)] if last else ref[pl.ds(c0, kc), :]

        if mode == "tn" or kc == k:
            prod = _dg(a_ref[...], b_ref[...], ca, cb)
        else:
            prod = None
            for c0 in range(0, k, kc):
                part = _dg(chunk(a_ref, c0, True), chunk(b_ref, c0, mode == "nt"), ca, cb)
                prod = part if prod is None else prod + part
        if gk == 1:
            epilogue(prod)
            return

        @pl.when(kk == 0)
        def _():
            acc_ref[...] = prod

        @pl.when(kk > 0)
        def _():
            acc_ref[...] += prod

        @pl.when(kk == gk - 1)
        def _():
            epilogue(acc_ref[...])

    vmem = (2 * (_nbytes((tm, tk), a.dtype) + _nbytes((tk, tn), b.dtype))
            + (2 * len(ins) + 2 * len(out_shape) + 1) * _nbytes((tm, tn), F32))
    outs = _pcall(
        body, name=name, grid=(gi, gj, gk) if i_outer else (gj, gi, gk), in_specs=in_specs, out_specs=out_specs,
        out_shape=out_shape, scratch_shapes=[pltpu.VMEM((tm, tn) if gk > 1 else (SUBLANES, LANES), F32)],
        compiler_params=_params(("arbitrary", "arbitrary", "arbitrary"), vmem),
    )(*ins)
    return outs[0] if (kind is None and copy_dtype is None) else outs


def _attn_head(q, k, v):
    sc = mm_nt(q, k) * (CA_DH ** -0.5)
    e = jnp.exp(sc - jnp.max(sc, axis=-1, keepdims=True))
    return mm_nn(e / jnp.sum(e, axis=-1, keepdims=True), v)


def _attn_fwd(q, kv):
    seq, n_mem = q.shape[0], kv.shape[0]
    tq = _tile(seq, (512, 256, 128))

    def body(q_ref, kv_ref, o_ref):
        for h in range(HEADS):
            hd = pl.ds(h * CA_DH, CA_DH)
            o = _attn_head(q_ref[:, hd], kv_ref[:, hd], kv_ref[:, pl.ds(D_MODEL + h * CA_DH, CA_DH)])
            o_ref[:, hd] = o.astype(BF16)

    return _pcall(
        body, name="attn_fwd", grid=(seq // tq,),
        in_specs=[pl.BlockSpec((tq, D_MODEL), lambda i: (i, 0)), pl.BlockSpec((n_mem, 2 * D_MODEL), lambda i: (0, 0))],
        out_specs=pl.BlockSpec((tq, D_MODEL), lambda i: (i, 0)), out_shape=jax.ShapeDtypeStruct((seq, D_MODEL), BF16),
        compiler_params=_params(("arbitrary",), 4 * _nbytes((tq, D_MODEL), F32) + 2 * _nbytes((n_mem, 2 * D_MODEL), F32)),
    )(q, kv)


def _attn_bwd(q, kv, do):
    seq, n_mem = q.shape[0], kv.shape[0]
    tq = _tile(seq, (512, 256, 128))

    def body(q_ref, kv_ref, do_ref, dq_ref, dkv_ref):
        @pl.when(pl.program_id(0) == 0)
        def _():
            dkv_ref[...] = jnp.zeros_like(dkv_ref)

        for h in range(HEADS):
            hd = pl.ds(h * CA_DH, CA_DH)
            vd = pl.ds(D_MODEL + h * CA_DH, CA_DH)
            _, vjp = jax.vjp(_attn_head, q_ref[:, hd], kv_ref[:, hd], kv_ref[:, vd])
            dq, dk, dv = vjp(do_ref[:, hd].astype(F32))
            dq_ref[:, hd] = dq.astype(BF16)
            dkv_ref[:, hd] += dk
            dkv_ref[:, vd] += dv

    return _pcall(
        body, name="attn_bwd", grid=(seq // tq,),
        in_specs=[pl.BlockSpec((tq, D_MODEL), lambda i: (i, 0)), pl.BlockSpec((n_mem, 2 * D_MODEL), lambda i: (0, 0)),
                  pl.BlockSpec((tq, D_MODEL), lambda i: (i, 0))],
        out_specs=[pl.BlockSpec((tq, D_MODEL), lambda i: (i, 0)), pl.BlockSpec((n_mem, 2 * D_MODEL), lambda i: (0, 0))],
        out_shape=[jax.ShapeDtypeStruct((seq, D_MODEL), BF16), jax.ShapeDtypeStruct((n_mem, 2 * D_MODEL), F32)],
        compiler_params=_params(("arbitrary",), 6 * _nbytes((tq, D_MODEL), F32) + 4 * _nbytes((n_mem, 2 * D_MODEL), F32)),
    )(q, kv, do)


def _ffn_mid(hg, xg, hv, xv, wg0, wg1, wg2, bg, wv0, wv1, wv2, bv):
    return jax.nn.gelu(causal_conv(hg, xg, (wg0, wg1, wg2), bg)) * causal_conv(hv, xv, (wv0, wv1, wv2), bv)


FFN_TB = 256
FFN_W = D_FF // 2
FFN_J = D_FF // FFN_W
MXU_COLS = 256
FFN_PIECES = tuple((off, min(MXU_COLS, FFN_W - off)) for off in range(0, FFN_W, MXU_COLS))


def _ffn_common_specs(seq, row):
    tb = min(FFN_TB, seq)
    full = pl.BlockSpec((tb, D_MODEL), lambda t, j: (row(t), 0))
    vec = pl.BlockSpec((1, D_MODEL), lambda t, j: (0, 0))
    halves = []
    for off in (0, FFN_J):
        halves.append(dict(
            w_up=pl.BlockSpec((D_MODEL, FFN_W), lambda t, j, off=off: (0, j + off)),
            taps=pl.BlockSpec((FFN_CONV, FFN_W), lambda t, j, off=off: (0, j + off)),
            bias=pl.BlockSpec((1, FFN_W), lambda t, j, off=off: (0, j + off))))
    w_down = pl.BlockSpec((FFN_W, D_MODEL), lambda t, j: (j, 0))
    u_blk = pl.BlockSpec((2, tb, FFN_W), lambda t, j: (0, row(t), j))
    return tb, full, vec, halves, w_down, u_blk


def _ffn_vmem(tb):
    return (_nbytes((2, tb, FFN_W), F32) + _nbytes((2, tb, FFN_W), BF16) + 3 * _nbytes((D_MODEL, FFN_W), BF16)
            + 10 * _nbytes((tb, D_MODEL), F32))


def _conv_params(taps_ref, bias_ref, cols):
    return taps_ref[0:1, cols], taps_ref[1:2, cols], taps_ref[2:3, cols], bias_ref[:, cols]


def _ffn_fwd(x2b, x2, w_up, conv_w, conv_b, w_down, ln_g, ln_b, target):
    seq = x2.shape[0]
    tb, full, vec, halves, wd_spec, u_blk = _ffn_common_specs(seq, lambda t: t)
    nt = seq // tb

    def body(xb_ref, wg_ref, wv_ref, tg_ref, tv_ref, bg_ref, bv_ref, wd_ref, x_ref, g_ref, b_ref, tgt_ref,
             u_ref, h_ref, dz_ref, dg_ref, db_ref, loss_ref, dzb_ref, acc, carry):
        t, j = pl.program_id(0), pl.program_id(1)
        xb = xb_ref[...]
        pieces = [pl.ds(off, width) for off, width in FFN_PIECES]
        ug = [_dg(xb, wg_ref[:, cols], 1, 0) for cols in pieces]
        uv = [_dg(xb, wv_ref[:, cols], 1, 0) for cols in pieces]
        hs = []
        for cols, g, v in zip(pieces, ug, uv):
            u_ref[0, :, cols] = g
            u_ref[1, :, cols] = v
            halo_g = jnp.where(t == 0, 0.0, carry[j, 0, :, cols])
            halo_v = jnp.where(t == 0, 0.0, carry[j, 1, :, cols])
            h = _ffn_mid(halo_g, g, halo_v, v, *_conv_params(tg_ref, bg_ref, cols),
                         *_conv_params(tv_ref, bv_ref, cols)).astype(BF16)
            carry[j, 0, :, cols] = g[tb - SUBLANES:, :]
            carry[j, 1, :, cols] = v[tb - SUBLANES:, :]
            h_ref[:, cols] = h
            hs.append(h)
        part = None
        for cols, h in zip(pieces, hs):
            p = _dg(h, wd_ref[cols, :], 1, 0)
            part = p if part is None else part + p

        @pl.when(j == 0)
        def _():
            acc[...] = part

        @pl.when(j > 0)
        def _():
            acc[...] += part

        @pl.when(j == FFN_J - 1)
        def _():
            y, vjp = jax.vjp(_layer_norm, acc[...] + ALPHA * x_ref[...], g_ref[...], b_ref[...])
            err = y - tgt_ref[...]
            part_loss = 0.5 * jnp.sum(jnp.sum(err * err, axis=1, keepdims=True), axis=0, keepdims=True) / D_MODEL
            dz, dg, db = vjp(err / D_MODEL)

            @pl.when(t == 0)
            def _():
                for r in (dg_ref, db_ref, loss_ref):
                    r[...] = jnp.zeros_like(r)

            dz_ref[...] = dz
            dzb_ref[...] = dz.astype(BF16)
            dg_ref[...] += dg
            db_ref[...] += db
            loss_ref[...] += jnp.broadcast_to(part_loss, (1, LANES))

    h0, h1 = halves
    row = jax.ShapeDtypeStruct((1, D_MODEL), F32)
    return _pcall(
        body, name="ffn_fwd", grid=(nt, FFN_J),
        in_specs=[full, h0["w_up"], h1["w_up"], h0["taps"], h1["taps"], h0["bias"], h1["bias"], wd_spec, full, vec, vec,
                  full],
        out_specs=[u_blk, pl.BlockSpec((tb, FFN_W), lambda t, j: (t, j)), full, vec, vec,
                   pl.BlockSpec((1, LANES), lambda t, j: (0, 0)), full],
        out_shape=[jax.ShapeDtypeStruct((2, seq, D_FF), F32), jax.ShapeDtypeStruct((seq, D_FF), BF16),
                   jax.ShapeDtypeStruct((seq, D_MODEL), F32), row, row, jax.ShapeDtypeStruct((1, LANES), F32),
                   jax.ShapeDtypeStruct((seq, D_MODEL), BF16)],
        scratch_shapes=[pltpu.VMEM((tb, D_MODEL), F32), pltpu.VMEM((FFN_J, 2, SUBLANES, FFN_W), F32)],
        compiler_params=_params(("arbitrary", "arbitrary"), _ffn_vmem(tb)),
    )(x2b, w_up, w_up, conv_w, conv_w, conv_b, conv_b, w_down, x2, ln_g, ln_b, target)


def _ffn_bwd(u, conv_w, conv_b, dz3b, dz3, w_down, w_up, z2, ln_g, ln_b):
    seq = dz3.shape[0]
    tb = min(FFN_TB, seq)
    nt = seq // tb
    row8 = tb // SUBLANES
    tb, full, vec, halves, wd_spec, u_blk = _ffn_common_specs(seq, lambda t: nt - 1 - t)
    halo = pl.BlockSpec((2, SUBLANES, FFN_W), lambda t, j: (0, jnp.maximum((nt - 1 - t) * row8 - 1, 0), j))

    def body(u_ref, halo_ref, tg_ref, tv_ref, bg_ref, bv_ref, dzb_ref, wd_ref, wg_ref, wv_ref, dz3_ref, z_ref, g_ref,
             b_ref, du_ref, dw_ref, dbias_ref, dz_ref, dg_ref, db_ref, dz2b_ref, acc, carry):
        t, j = pl.program_id(0), pl.program_id(1)

        @pl.when((t == 0) & (j == 0))
        def _():
            for r in (dw_ref, dbias_ref, dg_ref, db_ref):
                r[...] = jnp.zeros_like(r)

        pieces = [pl.ds(off, width) for off, width in FFN_PIECES]
        dzb = dzb_ref[...]
        dhs = [_dg(dzb, wd_ref[cols, :], 1, 1) for cols in pieces]
        first = t == nt - 1
        dus = []
        for cols, dh in zip(pieces, dhs):
            args = (jnp.where(first, 0.0, halo_ref[0, :, cols]), u_ref[0, :, cols],
                    jnp.where(first, 0.0, halo_ref[1, :, cols]), u_ref[1, :, cols],
                    *_conv_params(tg_ref, bg_ref, cols), *_conv_params(tv_ref, bv_ref, cols))
            _, vjp = jax.vjp(_ffn_mid, *args)
            dhg, dxg, dhv, dxv, g0, g1, g2, gb, v0, v1, v2, vb = vjp(dh)
            zeros = jnp.zeros((tb - SUBLANES, dh.shape[1]), F32)
            dug = (dxg + jnp.concatenate([zeros, jnp.where(t == 0, 0.0, carry[j, 0, :, cols])], axis=0)).astype(BF16)
            duv = (dxv + jnp.concatenate([zeros, jnp.where(t == 0, 0.0, carry[j, 1, :, cols])], axis=0)).astype(BF16)
            carry[j, 0, :, cols] = dhg
            carry[j, 1, :, cols] = dhv
            du_ref[0, :, cols] = dug
            du_ref[1, :, cols] = duv
            for half, parts in enumerate(((g0, g1, g2), (v0, v1, v2))):
                for d, p in enumerate(parts):
                    dw_ref[j, half, d:d + 1, cols] += p
            dbias_ref[j, 0, :, cols] += gb
            dbias_ref[j, 1, :, cols] += vb
            dus.append((dug, duv))
        part = None
        for cols, (dug, duv) in zip(pieces, dus):
            p = _dg(dug, wg_ref[:, cols], 1, 1) + _dg(duv, wv_ref[:, cols], 1, 1)
            part = p if part is None else part + p

        @pl.when(j == 0)
        def _():
            acc[...] = part

        @pl.when(j > 0)
        def _():
            acc[...] += part

        @pl.when(j == FFN_J - 1)
        def _():
            _, ln_vjp = jax.vjp(_layer_norm, z_ref[...], g_ref[...], b_ref[...])
            dz, dg, db = ln_vjp(acc[...] + ALPHA * dz3_ref[...])
            dz_ref[...] = dz
            dz2b_ref[...] = dz.astype(BF16)
            dg_ref[...] += dg
            db_ref[...] += db

    h0, h1 = halves
    row = jax.ShapeDtypeStruct((1, D_MODEL), F32)
    whole = lambda *shape: pl.BlockSpec(shape, lambda t, j: (0,) * len(shape))
    return _pcall(
        body, name="ffn_bwd", grid=(nt, FFN_J),
        in_specs=[u_blk, halo, h0["taps"], h1["taps"], h0["bias"], h1["bias"], full, wd_spec, h0["w_up"], h1["w_up"],
                  full, full, vec, vec],
        out_specs=[u_blk, whole(FFN_J, 2, FFN_CONV, FFN_W), whole(FFN_J, 2, 1, FFN_W), full, vec, vec, full],
        out_shape=[jax.ShapeDtypeStruct((2, seq, D_FF), BF16), jax.ShapeDtypeStruct((FFN_J, 2, FFN_CONV, FFN_W), F32),
                   jax.ShapeDtypeStruct((FFN_J, 2, 1, FFN_W), F32), jax.ShapeDtypeStruct((seq, D_MODEL), F32), row, row,
                   jax.ShapeDtypeStruct((seq, D_MODEL), BF16)],
        scratch_shapes=[pltpu.VMEM((tb, D_MODEL), F32), pltpu.VMEM((FFN_J, 2, SUBLANES, FFN_W), F32)],
        compiler_params=_params(("arbitrary", "arbitrary"), _ffn_vmem(tb)),
    )(u, u, conv_w, conv_w, conv_b, conv_b, dz3b, w_down, w_up, w_up, dz3, z2, ln_g, ln_b)


def _adamw_math(w, g, m, v):
    m_new = ADAM_B1 * m + (1.0 - ADAM_B1) * g
    v_new = ADAM_B2 * v + (1.0 - ADAM_B2) * jnp.square(g)
    m_hat = m_new / (1.0 - ADAM_B1 ** ADAM_STEP)
    v_hat = v_new / (1.0 - ADAM_B2 ** ADAM_STEP)
    return -ADAM_LR * (m_hat / (jnp.sqrt(v_hat) + ADAM_EPS) + ADAM_WD * w), m_new, v_new


def _adamw(name, w, g, m, v):
    rows, cols = w.shape
    tr = _tile(rows, (256, 176, 128, 64, 40, 32, 16, 8))

    def body(w_ref, g_ref, m_ref, v_ref, d_ref, nm_ref, nv_ref):
        d_ref[...], nm_ref[...], nv_ref[...] = _adamw_math(w_ref[...], g_ref[...], m_ref[...], v_ref[...])

    spec = pl.BlockSpec((tr, cols), lambda i: (i, 0))
    sh = jax.ShapeDtypeStruct((rows, cols), F32)
    return _pcall(
        body, name=name, grid=(rows // tr,), in_specs=[spec] * 4, out_specs=[spec] * 3, out_shape=[sh] * 3,
        compiler_params=_params(("arbitrary",), 14 * _nbytes((tr, -(-cols // LANES) * LANES), F32)),
    )(w, g, m, v)


def _adamw_halves(name, core, w, mine, theirs, m, v):
    rows, cols = w.shape
    half_rows = mine.shape[0]
    tr = _tile(half_rows, (256, 176, 128))
    nbh = half_rows // tr
    assert 2 * half_rows == rows

    def body(c_ref, w_ref, a_ref, b_ref, m_ref, v_ref, g_ref, d_ref, nm_ref, nv_ref):
        g = jnp.where(pl.program_id(0) // nbh == c_ref[0], a_ref[...], b_ref[...])
        g_ref[...] = g
        d_ref[...], nm_ref[...], nv_ref[...] = _adamw_math(w_ref[...], g, m_ref[...], v_ref[...])

    spec = pl.BlockSpec((tr, cols), lambda i, c_ref: (i, 0))
    half = pl.BlockSpec((tr, cols), lambda i, c_ref: (i % nbh, 0))
    sh = jax.ShapeDtypeStruct((rows, cols), F32)
    grid_spec = pltpu.PrefetchScalarGridSpec(
        num_scalar_prefetch=1, grid=(rows // tr,), in_specs=[spec, half, half, spec, spec], out_specs=[spec] * 4)
    return _pcall(
        body, name=name, grid_spec=grid_spec, out_shape=[sh] * 4,
        compiler_params=_params(("arbitrary",), 18 * _nbytes((tr, -(-cols // LANES) * LANES), F32)),
    )(core, w, mine, theirs, m, v)


MESH = pl.DeviceIdType.MESH
ANY = pl.BlockSpec(memory_space=pl.ANY)
N_CHIPS = 4
N_DEV = 8
BF16_ROWS = 16


def _me():
    return lax.axis_index("x"), lax.axis_index("y"), lax.axis_index("c")


def _other_chips(x, y):
    return [(1 - x, y), (x, 1 - y), (1 - x, 1 - y)]


def _remote(src, dst, ssem, rsem, dev):
    return pltpu.make_async_remote_copy(src_ref=src, dst_ref=dst, send_sem=ssem, recv_sem=rsem,
                                        device_id=dev, device_id_type=MESH)


def _half_rows(ref_rows, cc):
    half = ref_rows // 2
    return pl.ds(pl.multiple_of(cc * half, BF16_ROWS), half)


def _gather_weights(shards):
    n = len(shards)
    n_ici = n * (N_CHIPS - 1)

    def body(*refs):
        ins, outs, (ssem, rsem, lsem, lrsem) = refs[:n], refs[n:2 * n], refs[2 * n:]
        x, y, c = _me()
        k_me = 2 * x + y
        sib = (x, y, 1 - c)
        chips = _other_chips(x, y)
        started = []
        for i, (w_ref, o_ref) in enumerate(zip(ins, outs)):
            cp = _remote(w_ref, o_ref.at[k_me], lsem.at[i], lrsem.at[i], sib)
            cp.start()
            started.append(cp)
        for r, (px, py) in enumerate(chips):
            for i, (w_ref, o_ref) in enumerate(zip(ins, outs)):
                rows = _half_rows(w_ref.shape[0], c)
                s = r * n + i
                cp = _remote(w_ref.at[rows], o_ref.at[k_me, rows], ssem.at[s], rsem.at[s], (px, py, c))
                cp.start()
                started.append(cp)
        for r, (px, py) in enumerate(chips):
            for i, o_ref in enumerate(outs):
                blk = o_ref.at[2 * px + py, _half_rows(o_ref.shape[1], c)]
                s = r * n + i
                _remote(blk, blk, ssem.at[s], rsem.at[s], (px, py, c)).wait_recv()
                cp = _remote(blk, blk, ssem.at[n_ici + s], rsem.at[n_ici + s], sib)
                cp.start()
                started.append(cp)
        for r, (px, py) in enumerate(chips):
            for i, o_ref in enumerate(outs):
                blk = o_ref.at[2 * px + py, _half_rows(o_ref.shape[1], 1 - c)]
                s = n_ici + r * n + i
                _remote(blk, blk, ssem.at[s], rsem.at[s], sib).wait_recv()
        for cp in started[n:]:
            cp.wait_send()
        for cp in started[:n]:
            cp.wait()

    return _pcall(
        body, name="gather_weights", in_specs=[ANY] * n, out_specs=[ANY] * n,
        out_shape=[jax.ShapeDtypeStruct((N_CHIPS,) + s.shape, s.dtype) for s in shards],
        scratch_shapes=[pltpu.SemaphoreType.DMA((2 * n_ici,)), pltpu.SemaphoreType.DMA((2 * n_ici,)),
                        pltpu.SemaphoreType.DMA((n,)), pltpu.SemaphoreType.DMA((n,))],
    )(*shards)


def _swap_halves(name, grads):
    n = len(grads)

    def body(*refs):
        ins, outs, (ssem, rsem) = refs[:n], refs[n:2 * n], refs[2 * n:]
        x, y, c = _me()
        copies = []
        for i, (g_ref, o_ref) in enumerate(zip(ins, outs)):
            for k in range(N_CHIPS):
                s = i * N_CHIPS + k
                cp = _remote(g_ref.at[k, _half_rows(g_ref.shape[1], 1 - c)], o_ref.at[k], ssem.at[s], rsem.at[s],
                             (x, y, 1 - c))
                cp.start()
                copies.append(cp)
        for cp in copies:
            cp.wait()

    return _pcall(
        body, name=name, in_specs=[ANY] * n, out_specs=[ANY] * n,
        out_shape=[jax.ShapeDtypeStruct((N_CHIPS, g.shape[1] // 2, g.shape[2]), g.dtype) for g in grads],
        scratch_shapes=[pltpu.SemaphoreType.DMA((n * N_CHIPS,)), pltpu.SemaphoreType.DMA((n * N_CHIPS,))],
    )(*grads)


SEM = pl.BlockSpec(memory_space=pltpu.SEMAPHORE)
IN_HBM = pl.BlockSpec(memory_space=pltpu.HBM)
SPLIT_PARAMS = dict(compiler_params=pltpu.CompilerParams(has_side_effects=pltpu.SideEffectType.DATAFLOW_SIDE_EFFECTING))


def _split_start(name, sources, landings, n_copies, plan):
    ns, nl = len(sources), len(landings)

    def body(*refs):
        ins, lands, (ssem, rsem), token = refs[:ns], refs[ns:ns + nl], refs[ns + nl:ns + nl + 2], refs[-1]
        for s, (src, dst, _, dev) in enumerate(plan(ins, lands)):
            _remote(src, dst, ssem.at[s], rsem.at[s], dev).start()
        token[...] = jnp.zeros_like(token)

    arrays = list(sources) + list(landings)
    outs = _call(
        body, name=name, in_specs=[IN_HBM] * (ns + nl),
        out_specs=[SEM, SEM] + [IN_HBM] * (ns + nl) + [pl.BlockSpec(memory_space=pltpu.VMEM)],
        out_shape=[pltpu.SemaphoreType.DMA((n_copies,)), pltpu.SemaphoreType.DMA((n_copies,))]
        + [pltpu.HBM(a.shape, a.dtype) for a in arrays] + [jax.ShapeDtypeStruct((SUBLANES, LANES), F32)],
        input_output_aliases={i: 2 + i for i in range(ns + nl)}, **SPLIT_PARAMS,
    )(*[pltpu.with_memory_space_constraint(a, pltpu.HBM) for a in arrays])
    return (outs[:-1], ns), outs[-1]


def _split_wait(name, handle, after, plan):
    (ssem, rsem, *thru), ns = handle
    nl = len(thru) - ns

    def body(*refs):
        ins, lands, (ssem_ref, rsem_ref) = refs[:ns], refs[ns:ns + nl], refs[ns + nl:ns + nl + 2]
        for s, (src, _, dst, dev) in enumerate(plan(ins, lands)):
            cp = _remote(src, dst, ssem_ref.at[s], rsem_ref.at[s], dev)
            cp.wait_send()
            cp.wait_recv()

    outs = _call(
        body, name=name, in_specs=[IN_HBM] * (ns + nl) + [SEM, SEM, ANY], out_specs=[IN_HBM] * (ns + nl),
        out_shape=[pltpu.HBM(t.shape, t.dtype) for t in thru],
        input_output_aliases={i: i for i in range(ns + nl)}, **SPLIT_PARAMS,
    )(*thru, ssem, rsem, after)
    return outs[:ns], outs[ns:]


def _swap_plan(ins, lands):
    x, y, c = _me()
    return [(g_ref.at[k, _half_rows(g_ref.shape[1], 1 - c)], l_ref.at[k], l_ref.at[k], (x, y, 1 - c))
            for g_ref, l_ref in zip(ins, lands) for k in range(N_CHIPS)]


def _swap_start(name, grads):
    lands = [lax.empty((N_CHIPS, g.shape[1] // 2, g.shape[2]), g.dtype) for g in grads]
    return _split_start(name, grads, lands, len(grads) * N_CHIPS, _swap_plan)


def _swap_wait(name, handle, after):
    return _split_wait(name, handle, after, _swap_plan)


def _gather_plan(ins, lands):
    x, y, c = _me()
    k_me = 2 * x + y
    plan = [(w_ref, l_ref.at[k_me], l_ref.at[k_me], (x, y, 1 - c)) for w_ref, l_ref in zip(ins, lands)]
    for px, py in _other_chips(x, y):
        for w_ref, l_ref in zip(ins, lands):
            rows = _half_rows(w_ref.shape[0], c)
            plan.append((w_ref.at[rows], l_ref.at[k_me, rows], l_ref.at[2 * px + py, rows], (px, py, c)))
    return plan


def _gather_start(name, shards):
    lands = [lax.empty((N_CHIPS,) + s.shape, s.dtype) for s in shards]
    return _split_start(name, shards, lands, len(shards) * N_CHIPS, _gather_plan)


def _gather_wait(name, handle, after):
    return _split_wait(name, handle, after, _gather_plan)[1]


def _forward_halves(name, blocks):
    n = len(blocks)
    n_sem = n * (N_CHIPS - 1)

    def body(*refs):
        outs, (ssem, rsem) = refs[n:2 * n], refs[2 * n:]
        x, y, c = _me()
        sib = (x, y, 1 - c)
        chips = _other_chips(x, y)
        sends = []
        for r, (px, py) in enumerate(chips):
            for i, o_ref in enumerate(outs):
                blk = o_ref.at[2 * px + py, _half_rows(o_ref.shape[1], c)]
                cp = _remote(blk, blk, ssem.at[r * n + i], rsem.at[r * n + i], sib)
                cp.start()
                sends.append(cp)
        for r, (px, py) in enumerate(chips):
            for i, o_ref in enumerate(outs):
                blk = o_ref.at[2 * px + py, _half_rows(o_ref.shape[1], 1 - c)]
                _remote(blk, blk, ssem.at[r * n + i], rsem.at[r * n + i], sib).wait_recv()
        for cp in sends:
            cp.wait_send()

    return _pcall(
        body, name=name, in_specs=[ANY] * n, out_specs=[ANY] * n,
        out_shape=[jax.ShapeDtypeStruct(b.shape, b.dtype) for b in blocks],
        input_output_aliases={i: i for i in range(n)},
        scratch_shapes=[pltpu.SemaphoreType.DMA((n_sem,)), pltpu.SemaphoreType.DMA((n_sem,))],
    )(*blocks)


def _scatter_plan(ins, lands):
    x, y, c = _me()
    k_me = 2 * x + y
    return [(p_ref.at[2 * px + py], l_ref.at[k_me], l_ref.at[2 * px + py], (px, py, c))
            for px, py in _other_chips(x, y) for p_ref, l_ref in zip(ins, lands)]


def _scatter_start(name, parts):
    lands = [lax.empty(p.shape, p.dtype) for p in parts]
    return _split_start(name, parts, lands, len(parts) * (N_CHIPS - 1), _scatter_plan)


def _scatter_wait(name, handle, after):
    return _split_wait(name, handle, after, _scatter_plan)[1]


def _share_halves(halves):
    n = len(halves)

    def body(*refs):
        ins, outs, (ssem, rsem) = refs[:n], refs[n:2 * n], refs[2 * n:]
        x, y, c = _me()
        copies = [_remote(r_ref, o_ref, ssem.at[i], rsem.at[i], (x, y, 1 - c))
                  for i, (r_ref, o_ref) in enumerate(zip(ins, outs))]
        for cp in copies:
            cp.start()
        for cp in copies:
            cp.wait()

    return _pcall(
        body, name="share_halves", in_specs=[ANY] * n, out_specs=[ANY] * n,
        out_shape=[jax.ShapeDtypeStruct(h.shape, h.dtype) for h in halves],
        scratch_shapes=[pltpu.SemaphoreType.DMA((n,)), pltpu.SemaphoreType.DMA((n,))],
    )(*halves)


def _exchange_small(v, reduce):
    rows = v.shape[0]

    def body(v_ref, out_ref, buf, ssem, rsem):
        x, y, c = _me()
        me = 4 * x + 2 * y + c
        peers = [((x + bx) % 2, (y + by) % 2, (c + bc) % 2)
                 for bx in (0, 1) for by in (0, 1) for bc in (0, 1) if (bx, by, bc) != (0, 0, 0)]
        dst = buf if reduce else out_ref
        dst[me] = v_ref[...]
        sends = [_remote(v_ref, dst.at[me], ssem.at[r], rsem.at[r], p) for r, p in enumerate(peers)]
        for cp in sends:
            cp.start()
        for r, (px, py, pc) in enumerate(peers):
            blk = dst.at[4 * px + 2 * py + pc]
            _remote(blk, blk, ssem.at[r], rsem.at[r], (px, py, pc)).wait_recv()
        if reduce:
            acc = buf[0]
            for d in range(1, N_DEV):
                acc = acc + buf[d]
            out_ref[...] = acc
        for cp in sends:
            cp.wait_send()

    vm = pl.BlockSpec(memory_space=pltpu.VMEM)
    out_shape = jax.ShapeDtypeStruct((rows, LANES) if reduce else (N_DEV, rows, LANES), F32)
    buf_shape = (N_DEV, rows, LANES) if reduce else (SUBLANES, LANES)
    return _pcall(
        body, pin=False, name="reduce_small" if reduce else "gather_small", in_specs=[vm], out_specs=vm, out_shape=out_shape,
        scratch_shapes=[pltpu.VMEM(buf_shape, F32), pltpu.SemaphoreType.DMA((N_DEV - 1,)),
                        pltpu.SemaphoreType.DMA((N_DEV - 1,))],
        compiler_params=pltpu.CompilerParams(vmem_limit_bytes=32 * 1024 * 1024),
    )(v)


def _add_pair(name, core, g, theirs):
    _, half, cols = theirs.shape
    tr = _tile(half, (256, 176, 128))
    nb = half // tr

    def body(c_ref, g_ref, t_ref, o32_ref, o16_ref):
        s = g_ref[...] + t_ref[...]
        o32_ref[...] = s
        o16_ref[...] = s.astype(BF16)

    spec = pl.BlockSpec((None, tr, cols), lambda k, i, c_ref: (k, i, 0))
    grid_spec = pltpu.PrefetchScalarGridSpec(
        num_scalar_prefetch=1, grid=(N_CHIPS, nb),
        in_specs=[pl.BlockSpec((None, tr, cols), lambda k, i, c_ref: (k, c_ref[0] * nb + i, 0)), spec],
        out_specs=[spec, spec])
    return _pcall(
        body, name=name, grid_spec=grid_spec,
        out_shape=[jax.ShapeDtypeStruct(theirs.shape, F32), jax.ShapeDtypeStruct(theirs.shape, BF16)],
        compiler_params=_params(("arbitrary", "arbitrary"), 8 * _nbytes((tr, cols + LANES), F32)),
    )(core, g, theirs)


def _add_chips(name, chip, p32, recv):
    _, half, cols = p32.shape
    tr = _tile(half, (256, 176, 128))

    def body(k_ref, p_ref, r0_ref, r1_ref, r2_ref, o_ref):
        o_ref[...] = ((p_ref[...] + r0_ref[...].astype(F32)) + r1_ref[...].astype(F32)) + r2_ref[...].astype(F32)

    def other(r):
        return pl.BlockSpec((None, tr, cols), lambda i, k_ref: (r + (k_ref[0] <= r).astype(jnp.int32), i, 0))
    grid_spec = pltpu.PrefetchScalarGridSpec(
        num_scalar_prefetch=1, grid=(half // tr,),
        in_specs=[pl.BlockSpec((None, tr, cols), lambda i, k_ref: (k_ref[0], i, 0)), other(0), other(1), other(2)],
        out_specs=pl.BlockSpec((tr, cols), lambda i, k_ref: (i, 0)))
    return _pcall(
        body, name=name, grid_spec=grid_spec, out_shape=jax.ShapeDtypeStruct((half, cols), F32),
        compiler_params=_params(("arbitrary",), 10 * _nbytes((tr, cols + LANES), F32)),
    )(chip, p32, recv, recv, recv)


def kernel(x, mem, w_in, b_in, hg_lb_logits, hg_norm_w, ml_conv_w, ml_conv_b, ml_norm_w, w_out, ln1_g, ln1_b, ca_wq, ca_wkv, ca_wo, ln2_g, ln2_b, ffn_w_up, ffn_conv_w, ffn_conv_b, ffn_w_down, ln3_g, ln3_b, loss_target, m_w_in, m_b_in, m_hg_lb_logits, m_hg_norm_w, m_ml_conv_w, m_ml_conv_b, m_ml_norm_w, m_w_out, m_ln1_g, m_ln1_b, m_ca_wq, m_ca_wkv, m_ca_wo, m_ln2_g, m_ln2_b, m_ffn_w_up, m_ffn_conv_w, m_ffn_conv_b, m_ffn_w_down, m_ln3_g, m_ln3_b, v_w_in, v_b_in, v_hg_lb_logits, v_hg_norm_w, v_ml_conv_w, v_ml_conv_b, v_ml_norm_w, v_w_out, v_ln1_g, v_ln1_b, v_ca_wq, v_ca_wkv, v_ca_wo, v_ln2_g, v_ln2_b, v_ffn_w_up, v_ffn_conv_w, v_ffn_conv_b, v_ffn_w_down, v_ln3_g, v_ln3_b):
    return _train_step(dict(locals()))


WEIGHTS = ("w_in", "b_in", "hg_lb_logits", "hg_norm_w", "ml_conv_w", "ml_conv_b", "ml_norm_w", "w_out", "ln1_g",
           "ln1_b", "ca_wq", "ca_wkv", "ca_wo", "ln2_g", "ln2_b", "ffn_w_up", "ffn_conv_w", "ffn_conv_b",
           "ffn_w_down", "ln3_g", "ln3_b")
MATRICES = ("w_in", "w_out", "ca_wq", "ca_wkv", "ca_wo", "ffn_w_up", "ffn_w_down")
COL_SHARDED = ("w_in", "ca_wkv", "ffn_w_up", "ml_conv_w", "ffn_conv_w")
SMALL = tuple(n for n in WEIGHTS if n not in MATRICES)
PART_ROWS = 16


def _part_rows(shape, lead):
    n = 1
    for s in shape[lead:]:
        n *= s
    return -(-n // (LANES * PART_ROWS)) * PART_ROWS


def _pack(arrs, dtype, lead=0, rows=None):
    parts = []
    for a in arrs:
        head = a.shape[:lead]
        flat = a.reshape(head + (-1,)).astype(dtype)
        pad = _part_rows(a.shape, lead) * LANES - flat.shape[-1]
        flat = jnp.pad(flat, [(0, 0)] * lead + [(0, pad)])
        parts.append(flat.reshape(head + (-1, LANES)))
    used = sum(p.shape[lead] for p in parts)
    if rows is not None and rows > used:
        parts.append(jnp.zeros(parts[0].shape[:lead] + (rows - used, LANES), dtype))
    return jnp.concatenate(parts, axis=lead)


def _unpack(buf, shapes):
    lead = buf.shape[:-2]
    outs, r = [], 0
    for sh in shapes:
        n = 1
        for s in sh:
            n *= s
        nr = _part_rows(sh, 0)
        flat = buf[..., r:r + nr, :].reshape(lead + (nr * LANES,))
        outs.append(flat[..., :n].reshape(lead + tuple(sh)))
        r += nr
    return outs


def _cat_cols(s):
    return jnp.moveaxis(s, 0, 1).reshape(s.shape[1], -1)


def _stack_rows(s):
    return s.reshape(-1, s.shape[-1])


def _train_step(a):
    xs, mems, tgt = a["x"][0], a["mem"][0], a["loss_target"][0]
    core = lax.axis_index("c").astype(jnp.int32).reshape(1)
    chip = (2 * lax.axis_index("x") + lax.axis_index("y")).astype(jnp.int32).reshape(1)
    k_me = chip[0]
    shard = {n: a[n][0] for n in MATRICES}

    later = [n for n in MATRICES if n != "w_in"]
    taps = _exchange_small(_pack([a["ml_conv_w"][0], a["ffn_conv_w"][0]], F32), reduce=False)
    w = {"w_in": jnp.pad(_cat_cols(_gather_weights([shard["w_in"].astype(BF16)])[0]), ((0, 0), (0, D_IN_PAD - D_IN)))}
    gathering, token = _gather_start("gather_start", [shard[n].astype(BF16) for n in later])
    taps = taps.reshape((N_CHIPS, 2) + taps.shape[1:])[:, 0]
    ml_cw, ffn_cw = [_cat_cols(s) for s in _unpack(taps, [a["ml_conv_w"].shape[1:], a["ffn_conv_w"].shape[1:]])]
    b_in_p = jnp.pad(a["b_in"], ((0, 0), (0, D_IN_PAD - D_IN))) + token[0:1, 0:1]
    mixer_w = (a["hg_lb_logits"], a["hg_norm_w"], ml_cw, a["ml_conv_b"], a["ml_norm_w"])
    up_cols = a["ffn_w_up"].shape[-1]

    xb = xs.astype(BF16)
    proj = _mm("proj", "nn", xb, w["w_in"], bias=b_in_p, tm=256, tn=D_IN_PAD)
    y, hst, cst, nst, mst = _mixer_fwd(proj, *mixer_w)
    w.update(zip(later, _forward_halves("forward_halves", _gather_wait("gather_wait", gathering, y))))
    for n in ("w_out", "ca_wq", "ca_wo", "ffn_w_down"):
        w[n] = _stack_rows(w[n])
    z1, x1, x1b = _mm("mix_out", "nn", y, w["w_out"], res=xs, res_scale=ALPHA, ln=("fwd", a["ln1_g"], a["ln1_b"]),
                      copy_dtype=BF16)
    q = _mm("ca_q", "nn", x1b, w["ca_wq"], out_dtype=BF16, tn=D_MODEL)
    kv = _mm("ca_kv", "nn", mems, w["ca_wkv"])
    o = _attn_fwd(q, kv)
    z2, x2, x2b = _mm("ca_out", "nn", o, w["ca_wo"], res=x1, res_scale=ALPHA, ln=("fwd", a["ln2_g"], a["ln2_b"]),
                      copy_dtype=BF16)
    w_up = _cat_cols(w["ffn_w_up"])
    u, hmid, dz3, g_ln3g, g_ln3b, loss_part, dz3b = _ffn_fwd(
        x2b, x2, w_up, ffn_cw, a["ffn_conv_b"], w["ffn_w_down"], a["ln3_g"], a["ln3_b"], tgt)

    grads = {"ln3_g": g_ln3g, "ln3_b": g_ln3b}
    grads["ffn_w_down"] = _mm("g_w_down", "tn", hmid, dz3b, tm=D_FF // 2, tn=D_MODEL)
    du, g_cw, g_cb, dz2, grads["ln2_g"], grads["ln2_b"], dz2b = _ffn_bwd(
        u, ffn_cw, a["ffn_conv_b"], dz3b, dz3, w["ffn_w_down"], w_up, z2, a["ln2_g"], a["ln2_b"])
    grads["ffn_conv_w"] = jnp.transpose(g_cw, (2, 1, 0, 3)).reshape(FFN_CONV, 2 * D_FF)
    grads["ffn_conv_b"] = jnp.transpose(g_cb, (2, 1, 0, 3)).reshape(1, 2 * D_FF)
    grads["ffn_w_up"] = _mm("g_w_up", "tn", x2b, du, out_groups=N_CHIPS, tm=D_MODEL, tn=up_cols)
    grads["ffn_w_down"] = grads["ffn_w_down"].reshape((N_CHIPS,) + shard["ffn_w_down"].shape)
    pending = {}

    def reduce_start(tag, names, swapped=None):
        group = [grads[n] for n in names]
        group, theirs = swapped or (group, _swap_halves("swap_halves_" + tag, group))
        sums = [_add_pair("add_pair_" + n, core, g, t) for n, g, t in zip(names, group, theirs)]
        handle, token = _scatter_start("scatter_start_" + tag, [s16 for _, s16 in sums])
        pending[tag] = (names, [s32 for s32, _ in sums], handle)
        return token[0:1, 0:1]

    ffn = ("ffn_w_up", "ffn_w_down")
    swapping, token = _swap_start("swap_start_ffn", [grads[n] for n in ffn])
    do = _mm("d_o", "nt", dz2b, w["ca_wo"], bias=jnp.zeros((1, D_MODEL), F32) + token[0:1, 0:1], out_dtype=BF16,
             tn=D_MODEL)
    grads["ca_wo"] = _mm("g_wo", "tn", o, dz2b, tm=D_MODEL, tn=D_MODEL)
    zero = reduce_start("ffn", ffn, _swap_wait("swap_wait_ffn", swapping, grads["ca_wo"]))
    dq, dkv = _attn_bwd(q, kv + zero, do)
    grads["ca_wq"] = _mm("g_wq", "tn", x1b, dq, tm=D_MODEL, tn=D_MODEL)
    grads["ca_wkv"] = _mm("g_wkv", "tn", mems, dkv, out_groups=N_CHIPS, tm=D_MODEL)
    dz1, grads["ln1_g"], grads["ln1_b"], dz1b = _mm("d_x1", "nt", dq, w["ca_wq"], res=dz2, res_scale=ALPHA,
                                                    ln=("bwd", z1, a["ln1_g"], a["ln1_b"]), copy_dtype=BF16)
    dy = _mm("d_y", "nt", dz1b, w["w_out"], tn=D_MODEL)
    grads["w_out"] = _mm("g_w_out", "tn", y, dz1b, tm=D_MODEL, tn=D_MODEL)
    for n in ("w_out", "ca_wq", "ca_wo"):
        grads[n] = grads[n].reshape((N_CHIPS,) + shard[n].shape)
    zero = reduce_start("attn", ("w_out", "ca_wq", "ca_wkv", "ca_wo"))
    (dproj, g_b_in, grads["hg_lb_logits"], grads["hg_norm_w"], grads["ml_conv_w"], grads["ml_conv_b"],
     grads["ml_norm_w"]) = _mixer_bwd(proj, dy, hst, cst, nst, mst, mixer_w[0], mixer_w[1] + zero, *mixer_w[2:])
    g_in = _mm("g_w_in", "tn", xb, dproj, tm=D_MODEL, tn=up_cols)[:, :D_IN]
    grads["w_in"] = jnp.moveaxis(g_in.reshape(D_MODEL, N_CHIPS, -1), 1, 0)
    grads["b_in"] = g_b_in[:, :D_IN]
    zero = reduce_start("in", ("w_in",))
    dx = _mm("d_x", "nt", dproj, w["w_in"], bias=jnp.zeros((1, D_MODEL), F32) + zero, res=dz1, res_scale=ALPHA,
             tm=256, tn=D_MODEL)

    halves = {}
    for tag, (names, sums32, handle) in pending.items():
        for n, s32, r in zip(names, sums32, _scatter_wait("scatter_wait_" + tag, handle, dx)):
            halves[n] = _add_chips("add_chips_" + n, chip, s32, r)
    halves = [halves[n] for n in MATRICES]
    other_halves = _share_halves(halves)

    small_shapes = [grads[n].shape for n in SMALL] + [loss_part.shape]
    summed = _unpack(_exchange_small(_pack([grads[n] for n in SMALL] + [loss_part], F32), reduce=True), small_shapes)
    loss = summed[-1][0, 0]
    for n, g in zip(SMALL, summed[:-1]):
        if n in COL_SHARDED:
            cols = a[n].shape[-1]
            g = lax.dynamic_slice_in_dim(g, k_me * cols, cols, axis=1)
        grads[n] = g

    delta, new_m, new_v = {}, {}, {}
    for n, mine, theirs in zip(MATRICES, halves, other_halves):
        grads[n], delta[n], new_m[n], new_v[n] = _adamw_halves(
            "adamw_" + n, core, shard[n], mine, theirs, a["m_" + n][0], a["v_" + n][0])
    small_w = [a[n][0] if a[n].ndim == 3 else a[n] for n in SMALL]
    small_m = [a["m_" + n][0] if a[n].ndim == 3 else a["m_" + n] for n in SMALL]
    small_v = [a["v_" + n][0] if a[n].ndim == 3 else a["v_" + n] for n in SMALL]
    shapes = [w.shape for w in small_w]
    packed = [_pack(l, F32) for l in (small_w, [grads[n] for n in SMALL], small_m, small_v)]
    for out, buf in zip((delta, new_m, new_v), _adamw("adamw_small", *packed)):
        for n, v in zip(SMALL, _unpack(buf, shapes)):
            out[n] = v

    def shaped(d):
        return [d[n].reshape(a[n].shape) for n in WEIGHTS]
    return (loss, dx[None], *shaped(grads), *shaped(delta), *shaped(new_m), *shaped(new_v))
```

```python
import functools

import jax
import jax.numpy as jnp
from jax import lax
from jax.experimental import pallas as pl
from jax.experimental.pallas import tpu as pltpu

F32 = jnp.float32
BF16 = jnp.bfloat16

D_MODEL = 1024
HEADS = 4
DK = 128
D_GRP = HEADS * DK
CHUNK = 64
ML_CONV = 4
FFN_CONV = 3
D_FF = 2816
CA_DH = D_MODEL // HEADS
DEPTH = 1
ALPHA = (2.0 * DEPTH) ** 0.25
LN_EPS = 1e-5
NEG_BIG = -1e30
D_IN = 8 * D_GRP + 2 * HEADS
D_IN_PAD = 8 * D_GRP + 128
ADAM_LR, ADAM_B1, ADAM_B2, ADAM_EPS, ADAM_WD, ADAM_STEP = 0.001, 0.9, 0.999, 1e-08, 0.01, 10

SUBLANES = 8
LANES = 128
VMEM_BYTES = 64 * 1024 * 1024


def _pcall(body, pin=True, **kw):
    if not pin:
        return _call(body, **kw)
    kw["out_shape"] = jax.tree.map(lambda s: pltpu.HBM(s.shape, s.dtype), kw["out_shape"])
    call = _call(body, **kw)

    def pinned(*args):
        return call(*[pltpu.with_memory_space_constraint(x, pltpu.HBM) if jnp.issubdtype(x.dtype, jnp.floating) else x
                      for x in args])
    return pinned


def _call(body, **kw):
    return pl.pallas_call(body, **kw)


def _params(semantics, vmem_bytes):
    limit = int(min(max(2 * vmem_bytes, 16 * 1024 * 1024), VMEM_BYTES - 8 * 1024 * 1024))
    return pltpu.CompilerParams(dimension_semantics=semantics, vmem_limit_bytes=limit)


def _nbytes(shape, dtype):
    n = 1
    for s in shape:
        n *= s
    return n * jnp.dtype(dtype).itemsize


def _dg(a, b, ca, cb):
    return lax.dot_general(a.astype(BF16), b.astype(BF16), (((ca,), (cb,)), ((), ())),
                           preferred_element_type=F32)


@jax.custom_vjp
def mm_nn(a, b):
    return _dg(a, b, 1, 0)


mm_nn.defvjp(lambda a, b: (_dg(a, b, 1, 0), (a, b)),
             lambda r, g: (_dg(g, r[1], 1, 1).astype(r[0].dtype), _dg(r[0], g, 0, 0).astype(r[1].dtype)))


@jax.custom_vjp
def mm_nt(a, b):
    return _dg(a, b, 1, 1)


mm_nt.defvjp(lambda a, b: (_dg(a, b, 1, 1), (a, b)),
             lambda r, g: (_dg(g, r[1], 1, 0).astype(r[0].dtype), _dg(g, r[0], 0, 0).astype(r[1].dtype)))


@jax.custom_vjp
def mm_tn(a, b):
    return _dg(a, b, 0, 0)


mm_tn.defvjp(lambda a, b: (_dg(a, b, 0, 0), (a, b)),
             lambda r, g: (_dg(r[1], g, 1, 1).astype(r[0].dtype), _dg(r[0], g, 1, 0).astype(r[1].dtype)))


def _tri(n, lower):
    r = lax.broadcasted_iota(jnp.int32, (n, n), 0)
    c = lax.broadcasted_iota(jnp.int32, (n, n), 1)
    return ((r >= c) if lower else (r <= c)).astype(F32)


def _tri_dot(lower, x):
    t = _tri(x.shape[0], lower).astype(BF16)
    hi = x.astype(BF16)
    rest = x - hi.astype(F32)
    mid = rest.astype(BF16)
    lo = (rest - mid.astype(F32)).astype(BF16)
    return sum(lax.dot_general(t, p, (((1,), (0,)), ((), ())), preferred_element_type=F32) for p in (hi, mid, lo))


@jax.custom_vjp
def cumsum_rows(x):
    return _tri_dot(True, x)


cumsum_rows.defvjp(lambda x: (_tri_dot(True, x), None), lambda _, g: (_tri_dot(False, g),))


def _shift_impl(halo, x, d):
    xx = jnp.concatenate([halo, x], axis=0)
    return pltpu.roll(xx, d, 0)[SUBLANES:]


@functools.partial(jax.custom_vjp, nondiff_argnums=(2,))
def shift_rows(halo, x, d):
    return _shift_impl(halo, x, d)


def _shift_bwd(d, _, g):
    n = g.shape[0] + SUBLANES
    gg = jnp.concatenate([jnp.zeros((SUBLANES, g.shape[1]), g.dtype), g], axis=0)
    r = pltpu.roll(gg, n - d, 0)
    return r[:SUBLANES], r[SUBLANES:]


shift_rows.defvjp(lambda halo, x, d: (_shift_impl(halo, x, d), None), _shift_bwd)


def causal_conv(halo, x, w_rows, b):
    k = len(w_rows)
    y = b + w_rows[k - 1] * x
    for d in range(1, k):
        y = y + w_rows[k - 1 - d] * shift_rows(halo, x, d)
    return y


def _sigmoid(x):
    return 1.0 / (1.0 + jnp.exp(-x))


def _silu(x):
    return x * _sigmoid(x)


def _log_sigmoid(x):
    return jnp.minimum(x, 0.0) - jnp.log(1.0 + jnp.exp(-jnp.abs(x)))


def _pick_row(x, i):
    row = lax.broadcasted_iota(jnp.int32, (x.shape[0], 1), 0)
    return jnp.sum(jnp.where(row == i, x, 0.0), axis=0, keepdims=True)


def _layer_norm(z, g, b):
    mu = jnp.mean(z, axis=-1, keepdims=True)
    zc = z - mu
    var = jnp.mean(zc * zc, axis=-1, keepdims=True)
    return zc * lax.rsqrt(var + LN_EPS) * g + b


def _qk_conv(halo, x, w0, w1, w2, w3, b):
    return _silu(causal_conv(halo, x, (w0, w1, w2, w3), b))


def _grp(i, h=None):
    if h is None:
        return pl.ds(i * D_GRP, D_GRP)
    return pl.ds(i * D_GRP + h * DK, DK)


def _mixer_specs(n_chunks, reverse):
    def chunk(c):
        return n_chunks - 1 - c if reverse else c
    row8 = CHUNK // SUBLANES
    proj_spec = pl.BlockSpec((CHUNK, D_IN_PAD), lambda c: (chunk(c), 0))
    halo_spec = pl.BlockSpec((SUBLANES, 2 * D_GRP), lambda c: (jnp.maximum(chunk(c) * row8 - 1, 0), 2))
    small = [pl.BlockSpec((2, D_GRP), lambda c: (0, 0)), pl.BlockSpec((1, D_GRP), lambda c: (0, 0)),
             pl.BlockSpec((ML_CONV, 2 * D_GRP), lambda c: (0, 0)), pl.BlockSpec((1, 2 * D_GRP), lambda c: (0, 0)),
             pl.BlockSpec((1, D_GRP), lambda c: (0, 0))]
    state_specs = [pl.BlockSpec((1, HEADS, DK, DK), lambda c: (chunk(c), 0, 0, 0)),
                   pl.BlockSpec((1, HEADS, DK, DK), lambda c: (chunk(c), 0, 0, 0)),
                   pl.BlockSpec((1, HEADS, 1, DK), lambda c: (chunk(c), 0, 0, 0)),
                   pl.BlockSpec((1, HEADS, 1, DK), lambda c: (chunk(c), 0, 0, 0))]
    y_spec = pl.BlockSpec((CHUNK, 2 * D_GRP), lambda c: (chunk(c), 0))
    return proj_spec, halo_spec, small, state_specs, y_spec, chunk


def _heads(x):
    return [x[:, h * DK:(h + 1) * DK] for h in range(HEADS)]


def _last(x, j):
    lane = lax.broadcasted_iota(jnp.int32, (1, x.shape[-1]), 1)
    return jnp.sum(jnp.where(lane == j, x, 0.0), axis=-1, keepdims=True)


def _hg_chunk(st_t, hq, hf, hi, hgate, l0, l1, nw):
    n = hq.shape[0]
    lb = _sigmoid(l0 - l1)
    q = _silu(hq)
    lf = jnp.log(lb + (1.0 - lb) * _sigmoid(hf))
    k = (1.0 - lb) * _sigmoid(-hf)
    b = cumsum_rows(lf)
    b_ref = _pick_row(b, n // 2 - 1)
    b_last = _pick_row(b, n - 1)
    qa, ka =_heads(q * jnp.exp(b - b_ref)), _heads(k * jnp.exp(b_ref - b))
    qe, kd, eb, v = _heads(q * jnp.exp(b)), _heads(k * jnp.exp(b_last - b)), _heads(jnp.exp(b_last)), _heads(hi)
    tri = _tri(n, True) > 0
    attn = [jnp.where(tri, mm_nt(qa[h], ka[h]), 0.0) for h in range(HEADS)]
    o = [mm_nn(attn[h], v[h]) + mm_nt(qe[h], st_t[h]) for h in range(HEADS)]
    st_new = jnp.stack([eb[h] * st_t[h] + mm_tn(v[h], kd[h]) for h in range(HEADS)])
    yn = [o[h] * lax.rsqrt(jnp.mean(o[h] * o[h], axis=-1, keepdims=True) + LN_EPS) for h in range(HEADS)]
    return st_new, jnp.concatenate(yn, axis=1) * nw * _silu(hgate)


def _ml_chunk(c_st, n_st, m_st, q, k, v, gates, og, nw):
    n = q.shape[0]
    ig = jnp.stack([_last(gates, h) for h in range(HEADS)])
    log_f = _log_sigmoid(gates)
    fl = jnp.stack([_last(log_f, HEADS + h) for h in range(HEADS)])
    bw = cumsum_rows(jnp.concatenate([jnp.broadcast_to(fl[h], (n, DK)) for h in range(HEADS)], axis=1))
    b = jnp.stack([_last(x, 0) for x in _heads(bw)])
    g = jnp.sum(fl, axis=1, keepdims=True)
    eye = lax.broadcasted_iota(jnp.int32, (n, n), 0) == lax.broadcasted_iota(jnp.int32, (n, n), 1)
    e_row = jnp.sum(jnp.where(eye, ig - b, 0.0), axis=1, keepdims=True)
    d = jnp.where(_tri(n, True) > 0, b + e_row, -jnp.inf)
    inter = b + m_st
    m_t = jnp.maximum(inter, jnp.max(d, axis=2, keepdims=True))
    qs, kh, vh = _heads(q * (DK ** -0.5)), _heads(k), _heads(v)
    s = jnp.stack([mm_nt(qs[h], kh[h]) for h in range(HEADS)]) * jnp.exp(d - m_t)
    w_inter = jnp.exp(inter - m_t)
    num = (jnp.stack([mm_nn(s[h], vh[h]) for h in range(HEADS)])
           + w_inter * jnp.stack([mm_nn(qs[h], c_st[h]) for h in range(HEADS)]))
    den = jnp.sum(s, axis=2, keepdims=True) + w_inter * jnp.sum(jnp.stack(qs) * n_st, axis=2, keepdims=True)
    h_out = num / jnp.maximum(jnp.abs(den), jnp.exp(-m_t))
    a = g - b + ig
    m_new = jnp.maximum(g + m_st, jnp.max(a, axis=1, keepdims=True))
    decay = jnp.exp(g + m_st - m_new)
    wk = jnp.stack(kh) * jnp.exp(a - m_new)
    c_new = decay * c_st + jnp.stack([mm_tn(wk[h], vh[h]) for h in range(HEADS)])
    n_new = decay * n_st + jnp.sum(wk, axis=1, keepdims=True)
    hc = h_out - jnp.mean(h_out, axis=-1, keepdims=True)
    yn = hc * lax.rsqrt(jnp.mean(hc * hc, axis=-1, keepdims=True) + LN_EPS)
    y = _sigmoid(og) * (jnp.concatenate([yn[h] for h in range(HEADS)], axis=1) * nw)
    return c_new, n_new, m_new, y


def _mixer_inputs(proj_ref, lg_ref, hnw_ref, mnw_ref, qk):
    hg_in = (proj_ref[:, _grp(0)], proj_ref[:, _grp(1)], proj_ref[:, _grp(2)], proj_ref[:, _grp(3)],
             lg_ref[0:1, :], lg_ref[1:2, :], hnw_ref[...])
    ml_in = (qk[:, :D_GRP], qk[:, D_GRP:], proj_ref[:, _grp(6)], proj_ref[:, pl.ds(8 * D_GRP, LANES)],
             proj_ref[:, _grp(7)], mnw_ref[...])
    return hg_in, ml_in


def _mixer_fwd(proj, lb_logits, hg_nw, conv_w, conv_b, ml_nw):
    seq = proj.shape[0]
    n_chunks = seq // CHUNK
    proj_spec, halo_spec, small, state_specs, y_spec, _ = _mixer_specs(n_chunks, False)

    def body(proj_ref, halo_ref, lg_ref, hnw_ref, cw_ref, cb_ref, mnw_ref,
             y_ref, hst_ref, cst_ref, nst_ref, mst_ref, hs, cs, ns, ms):
        c = pl.program_id(0)

        @pl.when(c == 0)
        def _():
            hs[...] = jnp.zeros_like(hs)
            cs[...] = jnp.zeros_like(cs)
            ns[...] = jnp.zeros_like(ns)
            ms[...] = jnp.full(ms.shape, NEG_BIG, F32)

        hst_ref[0] = hs[...]
        cst_ref[0] = cs[...]
        nst_ref[0] = ns[...]
        mst_ref[0] = ms[...]
        halo = jnp.where(c > 0, halo_ref[...], 0.0)
        qk = _qk_conv(halo, proj_ref[:, pl.ds(4 * D_GRP, 2 * D_GRP)],
                      cw_ref[0:1, :], cw_ref[1:2, :], cw_ref[2:3, :], cw_ref[3:4, :], cb_ref[...])
        hg_in, ml_in = _mixer_inputs(proj_ref, lg_ref, hnw_ref, mnw_ref, qk)
        hs[...], y_hg = _hg_chunk(hs[...], *hg_in)
        cs[...], ns[...], m_new, y_ml = _ml_chunk(cs[...], ns[...], _last(ms[...], 0), *ml_in)
        ms[...] = jnp.broadcast_to(m_new, ms.shape)
        y_ref[:, pl.ds(0, D_GRP)] = y_hg.astype(BF16)
        y_ref[:, pl.ds(D_GRP, D_GRP)] = y_ml.astype(BF16)

    st = jax.ShapeDtypeStruct((n_chunks, HEADS, DK, DK), F32)
    vec = jax.ShapeDtypeStruct((n_chunks, HEADS, 1, DK), F32)
    vmem = 2 * (_nbytes((CHUNK, D_IN_PAD), F32) + _nbytes((CHUNK, 2 * D_GRP), F32) + 2 * _nbytes((HEADS, DK, DK), F32)) \
        + 2 * _nbytes((HEADS, DK, DK), F32)
    return _pcall(
        body, name="mixer_fwd", grid=(n_chunks,),
        in_specs=[proj_spec, halo_spec] + small,
        out_specs=[y_spec] + state_specs,
        out_shape=[jax.ShapeDtypeStruct((seq, 2 * D_GRP), BF16), st, st, vec, vec],
        scratch_shapes=[pltpu.VMEM((HEADS, DK, DK), F32), pltpu.VMEM((HEADS, DK, DK), F32),
                        pltpu.VMEM((HEADS, 1, DK), F32), pltpu.VMEM((HEADS, 1, DK), F32)],
        compiler_params=_params(("arbitrary",), vmem),
    )(proj, proj, lb_logits, hg_nw, conv_w, conv_b, ml_nw)


def _mixer_bwd(proj, dy, hst, cst, nst, mst, lb_logits, hg_nw, conv_w, conv_b, ml_nw):
    seq = proj.shape[0]
    n_chunks = seq // CHUNK
    proj_spec, halo_spec, small, state_specs, y_spec, _ = _mixer_specs(n_chunks, True)

    def body(proj_ref, halo_ref, dy_ref, hst_ref, cst_ref, nst_ref, mst_ref,
             lg_ref, hnw_ref, cw_ref, cb_ref, mnw_ref,
             dproj_ref, dbin_ref, dlg_ref, dhnw_ref, dcw_ref, dcb_ref, dmnw_ref,
             dhs, dcs, dns, dms, dhalo):
        c = pl.program_id(0)

        @pl.when(c == 0)
        def _():
            for r in (dhs, dcs, dns, dms, dhalo, dbin_ref, dlg_ref, dhnw_ref, dcw_ref, dcb_ref, dmnw_ref):
                r[...] = jnp.zeros_like(r)

        def put(cols, val):
            dproj_ref[:, cols] = val.astype(BF16)
            dbin_ref[:, cols] += jnp.sum(val, axis=0, keepdims=True)

        first = c == n_chunks - 1
        halo = jnp.where(first, 0.0, halo_ref[...])
        x_qk = proj_ref[:, pl.ds(4 * D_GRP, 2 * D_GRP)]
        conv_args = (halo, x_qk, cw_ref[0:1, :], cw_ref[1:2, :], cw_ref[2:3, :], cw_ref[3:4, :], cb_ref[...])
        qk, conv_vjp = jax.vjp(_qk_conv, *conv_args)
        hg_in, ml_in = _mixer_inputs(proj_ref, lg_ref, hnw_ref, mnw_ref, qk)
        _, hg_vjp = jax.vjp(_hg_chunk, hst_ref[0], *hg_in)
        _, ml_vjp = jax.vjp(_ml_chunk, cst_ref[0], nst_ref[0], _last(mst_ref[0], 0), *ml_in)
        dst, dhq, dhf, dhi, dhg, dl0, dl1, dnw = hg_vjp((dhs[...], dy_ref[:, pl.ds(0, D_GRP)]))
        dc, dn, dm, dq, dk, dv, dgates, dog, dmn = ml_vjp(
            (dcs[...], dns[...], _last(dms[...], 0), dy_ref[:, pl.ds(D_GRP, D_GRP)]))
        dhs[...] = dst
        dcs[...] = dc
        dns[...] = dn
        dms[...] = jnp.broadcast_to(dm, dms.shape)
        for i, val in ((0, dhq), (1, dhf), (2, dhi), (3, dhg), (6, dv), (7, dog)):
            put(_grp(i), val)
        put(pl.ds(8 * D_GRP, LANES), dgates)
        dlg_ref[0:1, :] += dl0
        dlg_ref[1:2, :] += dl1
        dhnw_ref[...] += dnw
        dmnw_ref[...] += dmn
        dh, dx, dw0, dw1, dw2, dw3, db = conv_vjp(jnp.concatenate([dq, dk], axis=1))
        tail = jnp.concatenate([jnp.zeros((CHUNK - SUBLANES, 2 * D_GRP), F32), dhalo[...]], axis=0)
        put(pl.ds(4 * D_GRP, 2 * D_GRP), dx + tail)
        dhalo[...] = dh
        for d, dw in enumerate((dw0, dw1, dw2, dw3)):
            dcw_ref[d:d + 1, :] += dw
        dcb_ref[...] += db

    row = pl.BlockSpec((1, D_GRP), lambda c: (0, 0))
    small_out = [pl.BlockSpec((1, D_IN_PAD), lambda c: (0, 0)), pl.BlockSpec((2, D_GRP), lambda c: (0, 0)), row,
                 pl.BlockSpec((ML_CONV, 2 * D_GRP), lambda c: (0, 0)), pl.BlockSpec((1, 2 * D_GRP), lambda c: (0, 0)), row]
    dy_spec = pl.BlockSpec((CHUNK, 2 * D_GRP), y_spec.index_map)
    vmem = 2 * (2 * _nbytes((CHUNK, D_IN_PAD), F32) + _nbytes((CHUNK, 2 * D_GRP), F32)
                + 2 * _nbytes((HEADS, DK, DK), F32)) + 2 * _nbytes((HEADS, DK, DK), F32) + 4 * 1024 * 1024
    return _pcall(
        body, name="mixer_bwd", grid=(n_chunks,),
        in_specs=[proj_spec, halo_spec, dy_spec] + state_specs + small,
        out_specs=[proj_spec] + small_out,
        out_shape=[jax.ShapeDtypeStruct((seq, D_IN_PAD), BF16), jax.ShapeDtypeStruct((1, D_IN_PAD), F32),
                   jax.ShapeDtypeStruct((2, D_GRP), F32), jax.ShapeDtypeStruct((1, D_GRP), F32),
                   jax.ShapeDtypeStruct((ML_CONV, 2 * D_GRP), F32), jax.ShapeDtypeStruct((1, 2 * D_GRP), F32),
                   jax.ShapeDtypeStruct((1, D_GRP), F32)],
        scratch_shapes=[pltpu.VMEM((HEADS, DK, DK), F32), pltpu.VMEM((HEADS, DK, DK), F32),
                        pltpu.VMEM((HEADS, 1, DK), F32), pltpu.VMEM((HEADS, 1, DK), F32),
                        pltpu.VMEM((SUBLANES, 2 * D_GRP), F32)],
        compiler_params=_params(("arbitrary",), vmem),
    )(proj, proj, dy, hst, cst, nst, mst, lb_logits, hg_nw, conv_w, conv_b, ml_nw)


def _tile(n, prefs, unit=None):
    unit = unit or n
    for p in prefs:
        if unit % p == 0 and n % p == 0:
            return p
    return unit


def _logical(arr):
    return arr.shape if arr.ndim == 2 else (arr.shape[1], arr.shape[0] * arr.shape[2])


def _group(arr):
    return arr.shape[-1]


def _split_spec(ndim, group, tr, tc, where):
    if ndim == 2:
        return pl.BlockSpec((tr, tc), where)
    per = group // tc
    assert per * tc == group, (group, tc)

    def index(*ids):
        bi, bj = where(*ids)
        return (bj // per, bi, bj % per)
    return pl.BlockSpec((None, tr, tc), index)


def _mm(name, mode, a, b, *, bias=None, res=None, res_scale=1.0, ln=None, out_dtype=F32, out_groups=None,
        copy_dtype=None, tm=None, tn=None, tk=None):
    la, lb = _logical(a), _logical(b)
    if mode == "nn":
        (m, k), n = la, lb[1]
        n_unit = _group(b) if b.ndim == 3 else n
        kc = _group(a) if a.ndim == 3 else k
    elif mode == "nt":
        (m, k), n = la, lb[0]
        n_unit = n
        kc = min(_group(a) if a.ndim == 3 else k, _group(b) if b.ndim == 3 else k)
    else:
        (k, m), n = la, lb[1]
        n_unit, kc = (_group(b) if b.ndim == 3 else n), k
        assert a.ndim == 2
    if out_groups:
        n_unit = min(n_unit, n // out_groups)
    kind = ln[0] if ln else None
    tm = tm or (256 if ln else _tile(m, (512, 256, 128)))
    tn = n if ln else (tn or _tile(n, (512, 384, 256, 128), n_unit))
    tk = (tk or _tile(k, (2048, 512, 256, 128))) if mode == "tn" else k
    gi, gj, gk = m // tm, n // tn, k // tk
    assert gi * tm == m and gj * tn == n and gk * tk == k and n_unit % tn == 0, (name, m, n, k, tm, tn, tk)
    ca, cb = {"nn": (1, 0), "nt": (1, 1), "tn": (0, 0)}[mode]
    i_outer = gk > 1 or (gi - 1) * _nbytes(b.shape, b.dtype) <= (gj - 1) * _nbytes(a.shape, a.dtype)

    def ij(where):
        return (lambda p, q, kk: where(p, q, kk)) if i_outer else (lambda p, q, kk: where(q, p, kk))
    if mode == "tn":
        a_spec = pl.BlockSpec((tk, tm), ij(lambda i, j, kk: (kk, i)))
    elif a.ndim == 3:
        a_spec = pl.BlockSpec((a.shape[0], tm, _group(a)), ij(lambda i, j, kk: (0, i, 0)))
    else:
        a_spec = pl.BlockSpec((tm, k), ij(lambda i, j, kk: (i, 0)))
    if mode != "nt":
        b_spec = _split_spec(b.ndim, _group(b), tk, tn, ij(lambda i, j, kk: (kk, j)))
    elif b.ndim == 3:
        b_spec = pl.BlockSpec((b.shape[0], tn, _group(b)), ij(lambda i, j, kk: (0, j, 0)))
    else:
        b_spec = pl.BlockSpec((tn, k), ij(lambda i, j, kk: (j, 0)))
    row_spec = pl.BlockSpec((1, tn), ij(lambda i, j, kk: (0, j)))
    blk_spec = pl.BlockSpec((tm, tn), ij(lambda i, j, kk: (i, j)))
    ins, in_specs = [a, b], [a_spec, b_spec]
    if bias is not None:
        ins.append(bias), in_specs.append(row_spec)
    if res is not None:
        ins.append(res), in_specs.append(blk_spec)
    if kind == "fwd":
        ins += [ln[1], ln[2]]
        in_specs += [row_spec, row_spec]
    elif kind == "loss":
        ins += [ln[1], ln[2], ln[3]]
        in_specs += [row_spec, row_spec, blk_spec]
    elif kind == "bwd":
        ins += [ln[1], ln[2], ln[3]]
        in_specs += [blk_spec, row_spec, row_spec]
    if out_groups:
        blk_out = jax.ShapeDtypeStruct((out_groups, m, n // out_groups), out_dtype)
        out_spec = _split_spec(3, n // out_groups, tm, tn, ij(lambda i, j, kk: (i, j)))
    else:
        blk_out, out_spec = jax.ShapeDtypeStruct((m, n), out_dtype), blk_spec
    row_out = jax.ShapeDtypeStruct((1, n), F32)
    if kind is None:
        out_shape, out_specs = [blk_out], [out_spec]
    elif kind == "fwd":
        out_shape, out_specs = [blk_out, blk_out], [blk_spec, blk_spec]
    else:
        out_shape, out_specs = [blk_out, row_out, row_out], [blk_spec, row_spec, row_spec]
        if kind == "loss":
            out_shape.append(jax.ShapeDtypeStruct((1, LANES), F32))
            out_specs.append(pl.BlockSpec((1, LANES), lambda p, q, kk: (0, 0)))
    if copy_dtype is not None:
        out_shape.append(jax.ShapeDtypeStruct((m, n), copy_dtype))
        out_specs.append(blk_spec)
    n_in = len(ins)

    def body(*refs):
        in_refs, out_refs, acc_ref = refs[:n_in], refs[n_in:n_in + len(out_shape)], refs[-1]
        i, kk = pl.program_id(0 if i_outer else 1), pl.program_id(2)
        a_ref, b_ref = in_refs[:2]
        extra = list(in_refs[2:])

        def epilogue(acc, rows=slice(None)):
            rest = list(extra)
            if bias is not None:
                acc = acc + rest.pop(0)[...]
            if res is not None:
                acc = acc + res_scale * rest.pop(0)[rows, :]
            if kind is None:
                out_refs[0][...] = acc.astype(out_dtype)
                return
            if kind == "fwd":
                out_refs[0][rows, :] = acc
                y = _layer_norm(acc, rest[0][...], rest[1][...])
                out_refs[1][rows, :] = y
                if copy_dtype is not None:
                    out_refs[-1][rows, :] = y.astype(copy_dtype)
                return
            if kind == "loss":
                y, vjp = jax.vjp(_layer_norm, acc, rest[0][...], rest[1][...])
                err = y - rest[2][rows, :]
                part = 0.5 * jnp.sum(jnp.sum(err * err, axis=1, keepdims=True), axis=0, keepdims=True) / n
                dz, dg, db = vjp(err / n)
            else:
                _, vjp = jax.vjp(_layer_norm, rest[0][rows, :], rest[1][...], rest[2][...])
                dz, dg, db = vjp(acc)
            out_refs[0][rows, :] = dz
            out_refs[1][...] += dg
            out_refs[2][...] += db
            if kind == "loss":
                out_refs[3][...] += jnp.broadcast_to(part, (1, LANES))
            if copy_dtype is not None:
                out_refs[-1][rows, :] = dz.astype(copy_dtype)

        if kind in ("loss", "bwd"):
            @pl.when((i == 0) & (kk == 0))
            def _():
                for r in out_refs[1:3 + (kind == "loss")]:
                    r[...] = jnp.zeros_like(r)

        def chunk(ref, c0, last):
            if ref.ndim == 3:
                g = ref.shape[2]
                return ref[c0 // g, :, pl.ds(c0 % g, kc)]
            return ref[:, pl.ds(c0, kc)] if last else ref[pl.ds(c0, kc), :]

        if mode == "tn" or kc == k:
            prod = _dg(a_ref[...], b_ref[...], ca, cb)
        else:
            prod = None
            for c0 in range(0, k, kc):
                part = _dg(chunk(a_ref, c0, True), chunk(b_ref, c0, mode == "nt"), ca, cb)
                prod = part if prod is None else prod + part
        if gk == 1:
            epilogue(prod)
            return

        @pl.when(kk == 0)
        def _():
            acc_ref[...] = prod

        @pl.when(kk > 0)
        def _():
            acc_ref[...] += prod

        @pl.when(kk == gk - 1)
        def _():
            epilogue(acc_ref[...])

    vmem = (2 * (_nbytes((tm, tk), a.dtype) + _nbytes((tk, tn), b.dtype))
            + (2 * len(ins) + 2 * len(out_shape) + 1) * _nbytes((tm, tn), F32))
    outs = _pcall(
        body, name=name, grid=(gi, gj, gk) if i_outer else (gj, gi, gk), in_specs=in_specs, out_specs=out_specs,
        out_shape=out_shape, scratch_shapes=[pltpu.VMEM((tm, tn) if gk > 1 else (SUBLANES, LANES), F32)],
        compiler_params=_params(("arbitrary", "arbitrary", "arbitrary"), vmem),
    )(*ins)
    return outs[0] if (kind is None and copy_dtype is None) else outs


def _attn_head(q, k, v):
    sc = mm_nt(q, k) * (CA_DH ** -0.5)
    e = jnp.exp(sc - jnp.max(sc, axis=-1, keepdims=True))
    return mm_nn(e / jnp.sum(e, axis=-1, keepdims=True), v)


def _attn_fwd(q, kv):
    seq, n_mem = q.shape[0], kv.shape[0]
    tq = _tile(seq, (512, 256, 128))

    def body(q_ref, kv_ref, o_ref):
        for h in range(HEADS):
            hd = pl.ds(h * CA_DH, CA_DH)
            o = _attn_head(q_ref[:, hd], kv_ref[:, hd], kv_ref[:, pl.ds(D_MODEL + h * CA_DH, CA_DH)])
            o_ref[:, hd] = o.astype(BF16)

    return _pcall(
        body, name="attn_fwd", grid=(seq // tq,),
        in_specs=[pl.BlockSpec((tq, D_MODEL), lambda i: (i, 0)), pl.BlockSpec((n_mem, 2 * D_MODEL), lambda i: (0, 0))],
        out_specs=pl.BlockSpec((tq, D_MODEL), lambda i: (i, 0)), out_shape=jax.ShapeDtypeStruct((seq, D_MODEL), BF16),
        compiler_params=_params(("arbitrary",), 4 * _nbytes((tq, D_MODEL), F32) + 2 * _nbytes((n_mem, 2 * D_MODEL), F32)),
    )(q, kv)


def _attn_bwd(q, kv, do):
    seq, n_mem = q.shape[0], kv.shape[0]
    tq = _tile(seq, (512, 256, 128))

    def body(q_ref, kv_ref, do_ref, dq_ref, dkv_ref):
        @pl.when(pl.program_id(0) == 0)
        def _():
            dkv_ref[...] = jnp.zeros_like(dkv_ref)

        for h in range(HEADS):
            hd = pl.ds(h * CA_DH, CA_DH)
            vd = pl.ds(D_MODEL + h * CA_DH, CA_DH)
            _, vjp = jax.vjp(_attn_head, q_ref[:, hd], kv_ref[:, hd], kv_ref[:, vd])
            dq, dk, dv = vjp(do_ref[:, hd].astype(F32))
            dq_ref[:, hd] = dq.astype(BF16)
            dkv_ref[:, hd] += dk
            dkv_ref[:, vd] += dv

    return _pcall(
        body, name="attn_bwd", grid=(seq // tq,),
        in_specs=[pl.BlockSpec((tq, D_MODEL), lambda i: (i, 0)), pl.BlockSpec((n_mem, 2 * D_MODEL), lambda i: (0, 0)),
                  pl.BlockSpec((tq, D_MODEL), lambda i: (i, 0))],
        out_specs=[pl.BlockSpec((tq, D_MODEL), lambda i: (i, 0)), pl.BlockSpec((n_mem, 2 * D_MODEL), lambda i: (0, 0))],
        out_shape=[jax.ShapeDtypeStruct((seq, D_MODEL), BF16), jax.ShapeDtypeStruct((n_mem, 2 * D_MODEL), F32)],
        compiler_params=_params(("arbitrary",), 6 * _nbytes((tq, D_MODEL), F32) + 4 * _nbytes((n_mem, 2 * D_MODEL), F32)),
    )(q, kv, do)


def _conv3(halo, x, w0, w1, w2, b):
    return causal_conv(halo, x, (w0, w1, w2), b)


def _gated(pg, pv):
    return jax.nn.gelu(pg) * pv


FFN_TB = 256
FFN_W = D_FF // 2
FFN_J = D_FF // FFN_W
MXU_COLS = 256
FFN_PIECES = tuple((off, min(MXU_COLS, FFN_W - off)) for off in range(0, FFN_W, MXU_COLS))


def _ffn_common_specs(seq, row):
    tb = min(FFN_TB, seq)
    full = pl.BlockSpec((tb, D_MODEL), lambda t, j: (row(t), 0))
    vec = pl.BlockSpec((1, D_MODEL), lambda t, j: (0, 0))
    halves = []
    for off in (0, FFN_J):
        halves.append(dict(
            w_up=pl.BlockSpec((D_MODEL, FFN_W), lambda t, j, off=off: (0, j + off)),
            taps=pl.BlockSpec((FFN_CONV, FFN_W), lambda t, j, off=off: (0, j + off)),
            bias=pl.BlockSpec((1, FFN_W), lambda t, j, off=off: (0, j + off))))
    w_down = pl.BlockSpec((FFN_W, D_MODEL), lambda t, j: (j, 0))
    u_blk = pl.BlockSpec((2, tb, FFN_W), lambda t, j: (0, row(t), j))
    return tb, full, vec, halves, w_down, u_blk


def _ffn_vmem(tb):
    return (2 * _nbytes((2, tb, FFN_W), F32) + _nbytes((2, tb, FFN_W), BF16) + 3 * _nbytes((D_MODEL, FFN_W), BF16)
            + 9 * _nbytes((tb, D_MODEL), F32))


def _conv_params(taps_ref, bias_ref, cols):
    return taps_ref[0:1, cols], taps_ref[1:2, cols], taps_ref[2:3, cols], bias_ref[:, cols]


def _ffn_fwd(x2b, x2, w_up, conv_w, conv_b, w_down, ln_g, ln_b, target):
    seq = x2.shape[0]
    tb, full, vec, halves, wd_spec, u_blk = _ffn_common_specs(seq, lambda t: t)
    nt = seq // tb

    def body(xb_ref, wg_ref, wv_ref, tg_ref, tv_ref, bg_ref, bv_ref, wd_ref, x_ref, g_ref, b_ref, tgt_ref,
             u_ref, pre_ref, h_ref, dz_ref, dg_ref, db_ref, loss_ref, dzb_ref, acc, carry):
        t, j = pl.program_id(0), pl.program_id(1)
        xb = xb_ref[...]
        pieces = [pl.ds(off, width) for off, width in FFN_PIECES]
        ug = [_dg(xb, wg_ref[:, cols], 1, 0) for cols in pieces]
        uv = [_dg(xb, wv_ref[:, cols], 1, 0) for cols in pieces]
        hs = []
        for cols, g, v in zip(pieces, ug, uv):
            u_ref[0, :, cols] = g
            u_ref[1, :, cols] = v
            halo_g = jnp.where(t == 0, 0.0, carry[j, 0, :, cols])
            halo_v = jnp.where(t == 0, 0.0, carry[j, 1, :, cols])
            pg = _conv3(halo_g, g, *_conv_params(tg_ref, bg_ref, cols))
            pv = _conv3(halo_v, v, *_conv_params(tv_ref, bv_ref, cols))
            pre_ref[0, :, cols] = pg
            pre_ref[1, :, cols] = pv
            h = _gated(pg, pv).astype(BF16)
            carry[j, 0, :, cols] = g[tb - SUBLANES:, :]
            carry[j, 1, :, cols] = v[tb - SUBLANES:, :]
            h_ref[:, cols] = h
            hs.append(h)
        part = None
        for cols, h in zip(pieces, hs):
            p = _dg(h, wd_ref[cols, :], 1, 0)
            part = p if part is None else part + p

        @pl.when(j == 0)
        def _():
            acc[...] = part

        @pl.when(j > 0)
        def _():
            acc[...] += part

        @pl.when(j == FFN_J - 1)
        def _():
            y, vjp = jax.vjp(_layer_norm, acc[...] + ALPHA * x_ref[...], g_ref[...], b_ref[...])
            err = y - tgt_ref[...]
            part_loss = 0.5 * jnp.sum(jnp.sum(err * err, axis=1, keepdims=True), axis=0, keepdims=True) / D_MODEL
            dz, dg, db = vjp(err / D_MODEL)

            @pl.when(t == 0)
            def _():
                for r in (dg_ref, db_ref, loss_ref):
                    r[...] = jnp.zeros_like(r)

            dz_ref[...] = dz
            dzb_ref[...] = dz.astype(BF16)
            dg_ref[...] += dg
            db_ref[...] += db
            loss_ref[...] += jnp.broadcast_to(part_loss, (1, LANES))

    h0, h1 = halves
    row = jax.ShapeDtypeStruct((1, D_MODEL), F32)
    return _pcall(
        body, name="ffn_fwd", grid=(nt, FFN_J),
        in_specs=[full, h0["w_up"], h1["w_up"], h0["taps"], h1["taps"], h0["bias"], h1["bias"], wd_spec, full, vec, vec,
                  full],
        out_specs=[u_blk, u_blk, pl.BlockSpec((tb, FFN_W), lambda t, j: (t, j)), full, vec, vec,
                   pl.BlockSpec((1, LANES), lambda t, j: (0, 0)), full],
        out_shape=[jax.ShapeDtypeStruct((2, seq, D_FF), F32), jax.ShapeDtypeStruct((2, seq, D_FF), F32),
                   jax.ShapeDtypeStruct((seq, D_FF), BF16),
                   jax.ShapeDtypeStruct((seq, D_MODEL), F32), row, row, jax.ShapeDtypeStruct((1, LANES), F32),
                   jax.ShapeDtypeStruct((seq, D_MODEL), BF16)],
        scratch_shapes=[pltpu.VMEM((tb, D_MODEL), F32), pltpu.VMEM((FFN_J, 2, SUBLANES, FFN_W), F32)],
        compiler_params=_params(("arbitrary", "arbitrary"), _ffn_vmem(tb)),
    )(x2b, w_up, w_up, conv_w, conv_w, conv_b, conv_b, w_down, x2, ln_g, ln_b, target)


def _ffn_bwd(u, pre, conv_w, conv_b, dz3b, dz3, w_down, w_up, z2, ln_g, ln_b):
    seq = dz3.shape[0]
    tb = min(FFN_TB, seq)
    nt = seq // tb
    row8 = tb // SUBLANES
    tb, full, vec, halves, wd_spec, u_blk = _ffn_common_specs(seq, lambda t: nt - 1 - t)
    halo = pl.BlockSpec((2, SUBLANES, FFN_W), lambda t, j: (0, jnp.maximum((nt - 1 - t) * row8 - 1, 0), j))

    def body(u_ref, halo_ref, pre_ref, tg_ref, tv_ref, bg_ref, bv_ref, dzb_ref, wd_ref, wg_ref, wv_ref, dz3_ref, z_ref,
             g_ref, b_ref, du_ref, dw_ref, dbias_ref, dz_ref, dg_ref, db_ref, dz2b_ref, acc, carry):
        t, j = pl.program_id(0), pl.program_id(1)

        @pl.when((t == 0) & (j == 0))
        def _():
            for r in (dw_ref, dbias_ref, dg_ref, db_ref):
                r[...] = jnp.zeros_like(r)

        pieces = [pl.ds(off, width) for off, width in FFN_PIECES]
        dzb = dzb_ref[...]
        dhs = [_dg(dzb, wd_ref[cols, :], 1, 1) for cols in pieces]
        first = t == nt - 1
        dus = []
        for cols, dh in zip(pieces, dhs):
            _, act_vjp = jax.vjp(_gated, pre_ref[0, :, cols], pre_ref[1, :, cols])
            dpg, dpv = act_vjp(dh)
            _, g_vjp = jax.vjp(_conv3, jnp.where(first, 0.0, halo_ref[0, :, cols]), u_ref[0, :, cols],
                               *_conv_params(tg_ref, bg_ref, cols))
            _, v_vjp = jax.vjp(_conv3, jnp.where(first, 0.0, halo_ref[1, :, cols]), u_ref[1, :, cols],
                               *_conv_params(tv_ref, bv_ref, cols))
            dhg, dxg, g0, g1, g2, gb = g_vjp(dpg)
            dhv, dxv, v0, v1, v2, vb = v_vjp(dpv)
            zeros = jnp.zeros((tb - SUBLANES, dh.shape[1]), F32)
            dug = (dxg + jnp.concatenate([zeros, jnp.where(t == 0, 0.0, carry[j, 0, :, cols])], axis=0)).astype(BF16)
            duv = (dxv + jnp.concatenate([zeros, jnp.where(t == 0, 0.0, carry[j, 1, :, cols])], axis=0)).astype(BF16)
            carry[j, 0, :, cols] = dhg
            carry[j, 1, :, cols] = dhv
            du_ref[0, :, cols] = dug
            du_ref[1, :, cols] = duv
            for half, parts in enumerate(((g0, g1, g2), (v0, v1, v2))):
                for d, p in enumerate(parts):
                    dw_ref[j, half, d:d + 1, cols] += p
            dbias_ref[j, 0, :, cols] += gb
            dbias_ref[j, 1, :, cols] += vb
            dus.append((dug, duv))
        part = None
        for cols, (dug, duv) in zip(pieces, dus):
            p = _dg(dug, wg_ref[:, cols], 1, 1) + _dg(duv, wv_ref[:, cols], 1, 1)
            part = p if part is None else part + p

        @pl.when(j == 0)
        def _():
            acc[...] = part

        @pl.when(j > 0)
        def _():
            acc[...] += part

        @pl.when(j == FFN_J - 1)
        def _():
            _, ln_vjp = jax.vjp(_layer_norm, z_ref[...], g_ref[...], b_ref[...])
            dz, dg, db = ln_vjp(acc[...] + ALPHA * dz3_ref[...])
            dz_ref[...] = dz
            dz2b_ref[...] = dz.astype(BF16)
            dg_ref[...] += dg
            db_ref[...] += db

    h0, h1 = halves
    row = jax.ShapeDtypeStruct((1, D_MODEL), F32)
    whole = lambda *shape: pl.BlockSpec(shape, lambda t, j: (0,) * len(shape))
    return _pcall(
        body, name="ffn_bwd", grid=(nt, FFN_J),
        in_specs=[u_blk, halo, u_blk, h0["taps"], h1["taps"], h0["bias"], h1["bias"], full, wd_spec, h0["w_up"],
                  h1["w_up"], full, full, vec, vec],
        out_specs=[u_blk, whole(FFN_J, 2, FFN_CONV, FFN_W), whole(FFN_J, 2, 1, FFN_W), full, vec, vec, full],
        out_shape=[jax.ShapeDtypeStruct((2, seq, D_FF), BF16), jax.ShapeDtypeStruct((FFN_J, 2, FFN_CONV, FFN_W), F32),
                   jax.ShapeDtypeStruct((FFN_J, 2, 1, FFN_W), F32), jax.ShapeDtypeStruct((seq, D_MODEL), F32), row, row,
                   jax.ShapeDtypeStruct((seq, D_MODEL), BF16)],
        scratch_shapes=[pltpu.VMEM((tb, D_MODEL), F32), pltpu.VMEM((FFN_J, 2, SUBLANES, FFN_W), F32)],
        compiler_params=_params(("arbitrary", "arbitrary"), _ffn_vmem(tb)),
    )(u, u, pre, conv_w, conv_w, conv_b, conv_b, dz3b, w_down, w_up, w_up, dz3, z2, ln_g, ln_b)


def _adamw_math(w, g, m, v):
    m_new = ADAM_B1 * m + (1.0 - ADAM_B1) * g
    v_new = ADAM_B2 * v + (1.0 - ADAM_B2) * jnp.square(g)
    m_hat = m_new / (1.0 - ADAM_B1 ** ADAM_STEP)
    v_hat = v_new / (1.0 - ADAM_B2 ** ADAM_STEP)
    return -ADAM_LR * (m_hat / (jnp.sqrt(v_hat) + ADAM_EPS) + ADAM_WD * w), m_new, v_new


def _adamw(name, w, g, m, v):
    rows, cols = w.shape
    tr = _tile(rows, (256, 176, 128, 64, 40, 32, 16, 8))

    def body(w_ref, g_ref, m_ref, v_ref, d_ref, nm_ref, nv_ref):
        d_ref[...], nm_ref[...], nv_ref[...] = _adamw_math(w_ref[...], g_ref[...], m_ref[...], v_ref[...])

    spec = pl.BlockSpec((tr, cols), lambda i: (i, 0))
    sh = jax.ShapeDtypeStruct((rows, cols), F32)
    return _pcall(
        body, name=name, grid=(rows // tr,), in_specs=[spec] * 4, out_specs=[spec] * 3, out_shape=[sh] * 3,
        compiler_params=_params(("arbitrary",), 14 * _nbytes((tr, -(-cols // LANES) * LANES), F32)),
    )(w, g, m, v)


def _adamw_halves(name, core, w, mine, theirs, m, v):
    rows, cols = w.shape
    half_rows = mine.shape[0]
    tr = _tile(half_rows, (256, 176, 128))
    nbh = half_rows // tr
    assert 2 * half_rows == rows

    def body(c_ref, w_ref, a_ref, b_ref, m_ref, v_ref, g_ref, d_ref, nm_ref, nv_ref):
        g = jnp.where(pl.program_id(0) // nbh == c_ref[0], a_ref[...], b_ref[...])
        g_ref[...] = g
        d_ref[...], nm_ref[...], nv_ref[...] = _adamw_math(w_ref[...], g, m_ref[...], v_ref[...])

    spec = pl.BlockSpec((tr, cols), lambda i, c_ref: (i, 0))
    half = pl.BlockSpec((tr, cols), lambda i, c_ref: (i % nbh, 0))
    sh = jax.ShapeDtypeStruct((rows, cols), F32)
    grid_spec = pltpu.PrefetchScalarGridSpec(
        num_scalar_prefetch=1, grid=(rows // tr,), in_specs=[spec, half, half, spec, spec], out_specs=[spec] * 4)
    return _pcall(
        body, name=name, grid_spec=grid_spec, out_shape=[sh] * 4,
        compiler_params=_params(("arbitrary",), 18 * _nbytes((tr, -(-cols // LANES) * LANES), F32)),
    )(core, w, mine, theirs, m, v)


MESH = pl.DeviceIdType.MESH
ANY = pl.BlockSpec(memory_space=pl.ANY)
N_CHIPS = 4
N_DEV = 8
BF16_ROWS = 16


def _me():
    return lax.axis_index("x"), lax.axis_index("y"), lax.axis_index("c")


def _other_chips(x, y):
    return [(1 - x, y), (x, 1 - y), (1 - x, 1 - y)]


def _remote(src, dst, ssem, rsem, dev):
    return pltpu.make_async_remote_copy(src_ref=src, dst_ref=dst, send_sem=ssem, recv_sem=rsem,
                                        device_id=dev, device_id_type=MESH)


def _half_rows(ref_rows, cc):
    half = ref_rows // 2
    return pl.ds(pl.multiple_of(cc * half, BF16_ROWS), half)


def _gather_weights(shards):
    n = len(shards)
    n_ici = n * (N_CHIPS - 1)

    def body(*refs):
        ins, outs, (ssem, rsem, lsem, lrsem) = refs[:n], refs[n:2 * n], refs[2 * n:]
        x, y, c = _me()
        k_me = 2 * x + y
        sib = (x, y, 1 - c)
        chips = _other_chips(x, y)
        started = []
        for i, (w_ref, o_ref) in enumerate(zip(ins, outs)):
            cp = _remote(w_ref, o_ref.at[k_me], lsem.at[i], lrsem.at[i], sib)
            cp.start()
            started.append(cp)
        for r, (px, py) in enumerate(chips):
            for i, (w_ref, o_ref) in enumerate(zip(ins, outs)):
                rows = _half_rows(w_ref.shape[0], c)
                s = r * n + i
                cp = _remote(w_ref.at[rows], o_ref.at[k_me, rows], ssem.at[s], rsem.at[s], (px, py, c))
                cp.start()
                started.append(cp)
        for r, (px, py) in enumerate(chips):
            for i, o_ref in enumerate(outs):
                blk = o_ref.at[2 * px + py, _half_rows(o_ref.shape[1], c)]
                s = r * n + i
                _remote(blk, blk, ssem.at[s], rsem.at[s], (px, py, c)).wait_recv()
                cp = _remote(blk, blk, ssem.at[n_ici + s], rsem.at[n_ici + s], sib)
                cp.start()
                started.append(cp)
        for r, (px, py) in enumerate(chips):
            for i, o_ref in enumerate(outs):
                blk = o_ref.at[2 * px + py, _half_rows(o_ref.shape[1], 1 - c)]
                s = n_ici + r * n + i
                _remote(blk, blk, ssem.at[s], rsem.at[s], sib).wait_recv()
        for cp in started[n:]:
            cp.wait_send()
        for cp in started[:n]:
            cp.wait()

    return _pcall(
        body, name="gather_weights", in_specs=[ANY] * n, out_specs=[ANY] * n,
        out_shape=[jax.ShapeDtypeStruct((N_CHIPS,) + s.shape, s.dtype) for s in shards],
        scratch_shapes=[pltpu.SemaphoreType.DMA((2 * n_ici,)), pltpu.SemaphoreType.DMA((2 * n_ici,)),
                        pltpu.SemaphoreType.DMA((n,)), pltpu.SemaphoreType.DMA((n,))],
    )(*shards)


def _swap_halves(name, grads):
    n = len(grads)

    def body(*refs):
        ins, outs, (ssem, rsem) = refs[:n], refs[n:2 * n], refs[2 * n:]
        x, y, c = _me()
        copies = []
        for i, (g_ref, o_ref) in enumerate(zip(ins, outs)):
            for k in range(N_CHIPS):
                s = i * N_CHIPS + k
                cp = _remote(g_ref.at[k, _half_rows(g_ref.shape[1], 1 - c)], o_ref.at[k], ssem.at[s], rsem.at[s],
                             (x, y, 1 - c))
                cp.start()
                copies.append(cp)
        for cp in copies:
            cp.wait()

    return _pcall(
        body, name=name, in_specs=[ANY] * n, out_specs=[ANY] * n,
        out_shape=[jax.ShapeDtypeStruct((N_CHIPS, g.shape[1] // 2, g.shape[2]), g.dtype) for g in grads],
        scratch_shapes=[pltpu.SemaphoreType.DMA((n * N_CHIPS,)), pltpu.SemaphoreType.DMA((n * N_CHIPS,))],
    )(*grads)


SEM = pl.BlockSpec(memory_space=pltpu.SEMAPHORE)
IN_HBM = pl.BlockSpec(memory_space=pltpu.HBM)
SPLIT_PARAMS = dict(compiler_params=pltpu.CompilerParams(has_side_effects=pltpu.SideEffectType.DATAFLOW_SIDE_EFFECTING))


def _split_start(name, sources, landings, n_copies, plan):
    ns, nl = len(sources), len(landings)

    def body(*refs):
        ins, lands, (ssem, rsem), token = refs[:ns], refs[ns:ns + nl], refs[ns + nl:ns + nl + 2], refs[-1]
        for s, (src, dst, _, dev) in enumerate(plan(ins, lands)):
            _remote(src, dst, ssem.at[s], rsem.at[s], dev).start()
        token[...] = jnp.zeros_like(token)

    arrays = list(sources) + list(landings)
    outs = _call(
        body, name=name, in_specs=[IN_HBM] * (ns + nl),
        out_specs=[SEM, SEM] + [IN_HBM] * (ns + nl) + [pl.BlockSpec(memory_space=pltpu.VMEM)],
        out_shape=[pltpu.SemaphoreType.DMA((n_copies,)), pltpu.SemaphoreType.DMA((n_copies,))]
        + [pltpu.HBM(a.shape, a.dtype) for a in arrays] + [jax.ShapeDtypeStruct((SUBLANES, LANES), F32)],
        input_output_aliases={i: 2 + i for i in range(ns + nl)}, **SPLIT_PARAMS,
    )(*[pltpu.with_memory_space_constraint(a, pltpu.HBM) for a in arrays])
    return (outs[:-1], ns), outs[-1]


def _split_wait(name, handle, after, plan):
    (ssem, rsem, *thru), ns = handle
    nl = len(thru) - ns

    def body(*refs):
        ins, lands, (ssem_ref, rsem_ref) = refs[:ns], refs[ns:ns + nl], refs[ns + nl:ns + nl + 2]
        for s, (src, _, dst, dev) in enumerate(plan(ins, lands)):
            cp = _remote(src, dst, ssem_ref.at[s], rsem_ref.at[s], dev)
            cp.wait_send()
            cp.wait_recv()

    outs = _call(
        body, name=name, in_specs=[IN_HBM] * (ns + nl) + [SEM, SEM, ANY], out_specs=[IN_HBM] * (ns + nl),
        out_shape=[pltpu.HBM(t.shape, t.dtype) for t in thru],
        input_output_aliases={i: i for i in range(ns + nl)}, **SPLIT_PARAMS,
    )(*thru, ssem, rsem, after)
    return outs[:ns], outs[ns:]


def _swap_plan(ins, lands):
    x, y, c = _me()
    return [(g_ref.at[k, _half_rows(g_ref.shape[1], 1 - c)], l_ref.at[k], l_ref.at[k], (x, y, 1 - c))
            for g_ref, l_ref in zip(ins, lands) for k in range(N_CHIPS)]


def _swap_start(name, grads):
    lands = [lax.empty((N_CHIPS, g.shape[1] // 2, g.shape[2]), g.dtype) for g in grads]
    return _split_start(name, grads, lands, len(grads) * N_CHIPS, _swap_plan)


def _swap_wait(name, handle, after):
    return _split_wait(name, handle, after, _swap_plan)


def _gather_plan(ins, lands):
    x, y, c = _me()
    k_me = 2 * x + y
    plan = [(w_ref, l_ref.at[k_me], l_ref.at[k_me], (x, y, 1 - c)) for w_ref, l_ref in zip(ins, lands)]
    for px, py in _other_chips(x, y):
        for w_ref, l_ref in zip(ins, lands):
            rows = _half_rows(w_ref.shape[0], c)
            plan.append((w_ref.at[rows], l_ref.at[k_me, rows], l_ref.at[2 * px + py, rows], (px, py, c)))
    return plan


def _gather_start(name, shards):
    lands = [lax.empty((N_CHIPS,) + s.shape, s.dtype) for s in shards]
    return _split_start(name, shards, lands, len(shards) * N_CHIPS, _gather_plan)


def _gather_wait(name, handle, after):
    return _split_wait(name, handle, after, _gather_plan)[1]


def _forward_halves(name, blocks):
    n = len(blocks)
    n_sem = n * (N_CHIPS - 1)

    def body(*refs):
        outs, (ssem, rsem) = refs[n:2 * n], refs[2 * n:]
        x, y, c = _me()
        sib = (x, y, 1 - c)
        chips = _other_chips(x, y)
        sends = []
        for r, (px, py) in enumerate(chips):
            for i, o_ref in enumerate(outs):
                blk = o_ref.at[2 * px + py, _half_rows(o_ref.shape[1], c)]
                cp = _remote(blk, blk, ssem.at[r * n + i], rsem.at[r * n + i], sib)
                cp.start()
                sends.append(cp)
        for r, (px, py) in enumerate(chips):
            for i, o_ref in enumerate(outs):
                blk = o_ref.at[2 * px + py, _half_rows(o_ref.shape[1], 1 - c)]
                _remote(blk, blk, ssem.at[r * n + i], rsem.at[r * n + i], sib).wait_recv()
        for cp in sends:
            cp.wait_send()

    return _pcall(
        body, name=name, in_specs=[ANY] * n, out_specs=[ANY] * n,
        out_shape=[jax.ShapeDtypeStruct(b.shape, b.dtype) for b in blocks],
        input_output_aliases={i: i for i in range(n)},
        scratch_shapes=[pltpu.SemaphoreType.DMA((n_sem,)), pltpu.SemaphoreType.DMA((n_sem,))],
    )(*blocks)


def _scatter_plan(ins, lands):
    x, y, c = _me()
    k_me = 2 * x + y
    return [(p_ref.at[2 * px + py], l_ref.at[k_me], l_ref.at[2 * px + py], (px, py, c))
            for px, py in _other_chips(x, y) for p_ref, l_ref in zip(ins, lands)]


def _scatter_start(name, parts):
    lands = [lax.empty(p.shape, p.dtype) for p in parts]
    return _split_start(name, parts, lands, len(parts) * (N_CHIPS - 1), _scatter_plan)


def _scatter_wait(name, handle, after):
    return _split_wait(name, handle, after, _scatter_plan)[1]


def _share_halves(halves):
    n = len(halves)

    def body(*refs):
        ins, outs, (ssem, rsem) = refs[:n], refs[n:2 * n], refs[2 * n:]
        x, y, c = _me()
        copies = [_remote(r_ref, o_ref, ssem.at[i], rsem.at[i], (x, y, 1 - c))
                  for i, (r_ref, o_ref) in enumerate(zip(ins, outs))]
        for cp in copies:
            cp.start()
        for cp in copies:
            cp.wait()

    return _pcall(
        body, name="share_halves", in_specs=[ANY] * n, out_specs=[ANY] * n,
        out_shape=[jax.ShapeDtypeStruct(h.shape, h.dtype) for h in halves],
        scratch_shapes=[pltpu.SemaphoreType.DMA((n,)), pltpu.SemaphoreType.DMA((n,))],
    )(*halves)


def _exchange_small(v, reduce):
    rows = v.shape[0]

    def body(v_ref, out_ref, buf, ssem, rsem):
        x, y, c = _me()
        me = 4 * x + 2 * y + c
        peers = [((x + bx) % 2, (y + by) % 2, (c + bc) % 2)
                 for bx in (0, 1) for by in (0, 1) for bc in (0, 1) if (bx, by, bc) != (0, 0, 0)]
        dst = buf if reduce else out_ref
        dst[me] = v_ref[...]
        sends = [_remote(v_ref, dst.at[me], ssem.at[r], rsem.at[r], p) for r, p in enumerate(peers)]
        for cp in sends:
            cp.start()
        for r, (px, py, pc) in enumerate(peers):
            blk = dst.at[4 * px + 2 * py + pc]
            _remote(blk, blk, ssem.at[r], rsem.at[r], (px, py, pc)).wait_recv()
        if reduce:
            acc = buf[0]
            for d in range(1, N_DEV):
                acc = acc + buf[d]
            out_ref[...] = acc
        for cp in sends:
            cp.wait_send()

    vm = pl.BlockSpec(memory_space=pltpu.VMEM)
    out_shape = jax.ShapeDtypeStruct((rows, LANES) if reduce else (N_DEV, rows, LANES), F32)
    buf_shape = (N_DEV, rows, LANES) if reduce else (SUBLANES, LANES)
    return _pcall(
        body, pin=False, name="reduce_small" if reduce else "gather_small", in_specs=[vm], out_specs=vm, out_shape=out_shape,
        scratch_shapes=[pltpu.VMEM(buf_shape, F32), pltpu.SemaphoreType.DMA((N_DEV - 1,)),
                        pltpu.SemaphoreType.DMA((N_DEV - 1,))],
        compiler_params=pltpu.CompilerParams(vmem_limit_bytes=32 * 1024 * 1024),
    )(v)


def _add_pair(name, core, g, theirs):
    _, half, cols = theirs.shape
    tr = _tile(half, (256, 176, 128))
    nb = half // tr

    def body(c_ref, g_ref, t_ref, o32_ref, o16_ref):
        s = g_ref[...] + t_ref[...]
        o32_ref[...] = s
        o16_ref[...] = s.astype(BF16)

    spec = pl.BlockSpec((None, tr, cols), lambda k, i, c_ref: (k, i, 0))
    grid_spec = pltpu.PrefetchScalarGridSpec(
        num_scalar_prefetch=1, grid=(N_CHIPS, nb),
        in_specs=[pl.BlockSpec((None, tr, cols), lambda k, i, c_ref: (k, c_ref[0] * nb + i, 0)), spec],
        out_specs=[spec, spec])
    return _pcall(
        body, name=name, grid_spec=grid_spec,
        out_shape=[jax.ShapeDtypeStruct(theirs.shape, F32), jax.ShapeDtypeStruct(theirs.shape, BF16)],
        compiler_params=_params(("arbitrary", "arbitrary"), 8 * _nbytes((tr, cols + LANES), F32)),
    )(core, g, theirs)


def _add_chips(name, chip, p32, recv):
    _, half, cols = p32.shape
    tr = _tile(half, (256, 176, 128))

    def body(k_ref, p_ref, r0_ref, r1_ref, r2_ref, o_ref):
        o_ref[...] = ((p_ref[...] + r0_ref[...].astype(F32)) + r1_ref[...].astype(F32)) + r2_ref[...].astype(F32)

    def other(r):
        return pl.BlockSpec((None, tr, cols), lambda i, k_ref: (r + (k_ref[0] <= r).astype(jnp.int32), i, 0))
    grid_spec = pltpu.PrefetchScalarGridSpec(
        num_scalar_prefetch=1, grid=(half // tr,),
        in_specs=[pl.BlockSpec((None, tr, cols), lambda i, k_ref: (k_ref[0], i, 0)), other(0), other(1), other(2)],
        out_specs=pl.BlockSpec((tr, cols), lambda i, k_ref: (i, 0)))
    return _pcall(
        body, name=name, grid_spec=grid_spec, out_shape=jax.ShapeDtypeStruct((half, cols), F32),
        compiler_params=_params(("arbitrary",), 10 * _nbytes((tr, cols + LANES), F32)),
    )(chip, p32, recv, recv, recv)


def kernel(x, mem, w_in, b_in, hg_lb_logits, hg_norm_w, ml_conv_w, ml_conv_b, ml_norm_w, w_out, ln1_g, ln1_b, ca_wq, ca_wkv, ca_wo, ln2_g, ln2_b, ffn_w_up, ffn_conv_w, ffn_conv_b, ffn_w_down, ln3_g, ln3_b, loss_target, m_w_in, m_b_in, m_hg_lb_logits, m_hg_norm_w, m_ml_conv_w, m_ml_conv_b, m_ml_norm_w, m_w_out, m_ln1_g, m_ln1_b, m_ca_wq, m_ca_wkv, m_ca_wo, m_ln2_g, m_ln2_b, m_ffn_w_up, m_ffn_conv_w, m_ffn_conv_b, m_ffn_w_down, m_ln3_g, m_ln3_b, v_w_in, v_b_in, v_hg_lb_logits, v_hg_norm_w, v_ml_conv_w, v_ml_conv_b, v_ml_norm_w, v_w_out, v_ln1_g, v_ln1_b, v_ca_wq, v_ca_wkv, v_ca_wo, v_ln2_g, v_ln2_b, v_ffn_w_up, v_ffn_conv_w, v_ffn_conv_b, v_ffn_w_down, v_ln3_g, v_ln3_b):
    return _train_step(dict(locals()))


WEIGHTS = ("w_in", "b_in", "hg_lb_logits", "hg_norm_w", "ml_conv_w", "ml_conv_b", "ml_norm_w", "w_out", "ln1_g",
           "ln1_b", "ca_wq", "ca_wkv", "ca_wo", "ln2_g", "ln2_b", "ffn_w_up", "ffn_conv_w", "ffn_conv_b",
           "ffn_w_down", "ln3_g", "ln3_b")
MATRICES = ("w_in", "w_out", "ca_wq", "ca_wkv", "ca_wo", "ffn_w_up", "ffn_w_down")
COL_SHARDED = ("w_in", "ca_wkv", "ffn_w_up", "ml_conv_w", "ffn_conv_w")
SMALL = tuple(n for n in WEIGHTS if n not in MATRICES)
PART_ROWS = 16


def _part_rows(shape, lead):
    n = 1
    for s in shape[lead:]:
        n *= s
    return -(-n // (LANES * PART_ROWS)) * PART_ROWS


def _pack(arrs, dtype, lead=0, rows=None):
    parts = []
    for a in arrs:
        head = a.shape[:lead]
        flat = a.reshape(head + (-1,)).astype(dtype)
        pad = _part_rows(a.shape, lead) * LANES - flat.shape[-1]
        flat = jnp.pad(flat, [(0, 0)] * lead + [(0, pad)])
        parts.append(flat.reshape(head + (-1, LANES)))
    used = sum(p.shape[lead] for p in parts)
    if rows is not None and rows > used:
        parts.append(jnp.zeros(parts[0].shape[:lead] + (rows - used, LANES), dtype))
    return jnp.concatenate(parts, axis=lead)


def _unpack(buf, shapes):
    lead = buf.shape[:-2]
    outs, r = [], 0
    for sh in shapes:
        n = 1
        for s in sh:
            n *= s
        nr = _part_rows(sh, 0)
        flat = buf[..., r:r + nr, :].reshape(lead + (nr * LANES,))
        outs.append(flat[..., :n].reshape(lead + tuple(sh)))
        r += nr
    return outs


def _cat_cols(s):
    return jnp.moveaxis(s, 0, 1).reshape(s.shape[1], -1)


def _stack_rows(s):
    return s.reshape(-1, s.shape[-1])


def _train_step(a):
    xs, mems, tgt = a["x"][0], a["mem"][0], a["loss_target"][0]
    core = lax.axis_index("c").astype(jnp.int32).reshape(1)
    chip = (2 * lax.axis_index("x") + lax.axis_index("y")).astype(jnp.int32).reshape(1)
    k_me = chip[0]
    shard = {n: a[n][0] for n in MATRICES}

    later = [n for n in MATRICES if n != "w_in"]
    taps = _exchange_small(_pack([a["ml_conv_w"][0], a["ffn_conv_w"][0]], F32), reduce=False)
    w = {"w_in": jnp.pad(_cat_cols(_gather_weights([shard["w_in"].astype(BF16)])[0]), ((0, 0), (0, D_IN_PAD - D_IN)))}
    gathering, token = _gather_start("gather_start", [shard[n].astype(BF16) for n in later])
    taps = taps.reshape((N_CHIPS, 2) + taps.shape[1:])[:, 0]
    ml_cw, ffn_cw = [_cat_cols(s) for s in _unpack(taps, [a["ml_conv_w"].shape[1:], a["ffn_conv_w"].shape[1:]])]
    b_in_p = jnp.pad(a["b_in"], ((0, 0), (0, D_IN_PAD - D_IN))) + token[0:1, 0:1]
    mixer_w = (a["hg_lb_logits"], a["hg_norm_w"], ml_cw, a["ml_conv_b"], a["ml_norm_w"])
    up_cols = a["ffn_w_up"].shape[-1]

    xb = xs.astype(BF16)
    proj = _mm("proj", "nn", xb, w["w_in"], bias=b_in_p, tm=256, tn=D_IN_PAD)
    y, hst, cst, nst, mst = _mixer_fwd(proj, *mixer_w)
    w.update(zip(later, _forward_halves("forward_halves", _gather_wait("gather_wait", gathering, y))))
    for n in ("w_out", "ca_wq", "ca_wo", "ffn_w_down"):
        w[n] = _stack_rows(w[n])
    z1, x1, x1b = _mm("mix_out", "nn", y, w["w_out"], res=xs, res_scale=ALPHA, ln=("fwd", a["ln1_g"], a["ln1_b"]),
                      copy_dtype=BF16)
    q = _mm("ca_q", "nn", x1b, w["ca_wq"], out_dtype=BF16, tn=D_MODEL)
    kv = _mm("ca_kv", "nn", mems, w["ca_wkv"])
    o = _attn_fwd(q, kv)
    z2, x2, x2b = _mm("ca_out", "nn", o, w["ca_wo"], res=x1, res_scale=ALPHA, ln=("fwd", a["ln2_g"], a["ln2_b"]),
                      copy_dtype=BF16)
    w_up = _cat_cols(w["ffn_w_up"])
    u, pre, hmid, dz3, g_ln3g, g_ln3b, loss_part, dz3b = _ffn_fwd(
        x2b, x2, w_up, ffn_cw, a["ffn_conv_b"], w["ffn_w_down"], a["ln3_g"], a["ln3_b"], tgt)

    grads = {"ln3_g": g_ln3g, "ln3_b": g_ln3b}
    grads["ffn_w_down"] = _mm("g_w_down", "tn", hmid, dz3b, tm=D_FF // 2, tn=D_MODEL)
    du, g_cw, g_cb, dz2, grads["ln2_g"], grads["ln2_b"], dz2b = _ffn_bwd(
        u, pre, ffn_cw, a["ffn_conv_b"], dz3b, dz3, w["ffn_w_down"], w_up, z2, a["ln2_g"], a["ln2_b"])
    grads["ffn_conv_w"] = jnp.transpose(g_cw, (2, 1, 0, 3)).reshape(FFN_CONV, 2 * D_FF)
    grads["ffn_conv_b"] = jnp.transpose(g_cb, (2, 1, 0, 3)).reshape(1, 2 * D_FF)
    grads["ffn_w_up"] = _mm("g_w_up", "tn", x2b, du, out_groups=N_CHIPS, tm=D_MODEL, tn=up_cols)
    grads["ffn_w_down"] = grads["ffn_w_down"].reshape((N_CHIPS,) + shard["ffn_w_down"].shape)
    pending = {}

    def reduce_start(tag, names, swapped=None):
        group = [grads[n] for n in names]
        group, theirs = swapped or (group, _swap_halves("swap_halves_" + tag, group))
        sums = [_add_pair("add_pair_" + n, core, g, t) for n, g, t in zip(names, group, theirs)]
        handle, token = _scatter_start("scatter_start_" + tag, [s16 for _, s16 in sums])
        pending[tag] = (names, [s32 for s32, _ in sums], handle)
        return token[0:1, 0:1]

    ffn = ("ffn_w_up", "ffn_w_down")
    swapping, token = _swap_start("swap_start_ffn", [grads[n] for n in ffn])
    do = _mm("d_o", "nt", dz2b, w["ca_wo"], bias=jnp.zeros((1, D_MODEL), F32) + token[0:1, 0:1], out_dtype=BF16,
             tn=D_MODEL)
    grads["ca_wo"] = _mm("g_wo", "tn", o, dz2b, tm=D_MODEL, tn=D_MODEL)
    zero = reduce_start("ffn", ffn, _swap_wait("swap_wait_ffn", swapping, grads["ca_wo"]))
    dq, dkv = _attn_bwd(q, kv + zero, do)
    grads["ca_wq"] = _mm("g_wq", "tn", x1b, dq, tm=D_MODEL, tn=D_MODEL)
    grads["ca_wkv"] = _mm("g_wkv", "tn", mems, dkv, out_groups=N_CHIPS, tm=D_MODEL)
    dz1, grads["ln1_g"], grads["ln1_b"], dz1b = _mm("d_x1", "nt", dq, w["ca_wq"], res=dz2, res_scale=ALPHA,
                                                    ln=("bwd", z1, a["ln1_g"], a["ln1_b"]), copy_dtype=BF16)
    dy = _mm("d_y", "nt", dz1b, w["w_out"], tn=D_MODEL)
    grads["w_out"] = _mm("g_w_out", "tn", y, dz1b, tm=D_MODEL, tn=D_MODEL)
    for n in ("w_out", "ca_wq", "ca_wo"):
        grads[n] = grads[n].reshape((N_CHIPS,) + shard[n].shape)
    zero = reduce_start("attn", ("w_out", "ca_wq", "ca_wkv", "ca_wo"))
    (dproj, g_b_in, grads["hg_lb_logits"], grads["hg_norm_w"], grads["ml_conv_w"], grads["ml_conv_b"],
     grads["ml_norm_w"]) = _mixer_bwd(proj, dy, hst, cst, nst, mst, mixer_w[0], mixer_w[1] + zero, *mixer_w[2:])
    g_in = _mm("g_w_in", "tn", xb, dproj, tm=D_MODEL, tn=up_cols)[:, :D_IN]
    grads["w_in"] = jnp.moveaxis(g_in.reshape(D_MODEL, N_CHIPS, -1), 1, 0)
    grads["b_in"] = g_b_in[:, :D_IN]
    zero = reduce_start("in", ("w_in",))
    dx = _mm("d_x", "nt", dproj, w["w_in"], bias=jnp.zeros((1, D_MODEL), F32) + zero, res=dz1, res_scale=ALPHA,
             tm=256, tn=D_MODEL)

    halves = {}
    for tag, (names, sums32, handle) in pending.items():
        for n, s32, r in zip(names, sums32, _scatter_wait("scatter_wait_" + tag, handle, dx)):
            halves[n] = _add_chips("add_chips_" + n, chip, s32, r)
    halves = [halves[n] for n in MATRICES]
    other_halves = _share_halves(halves)

    small_shapes = [grads[n].shape for n in SMALL] + [loss_part.shape]
    summed = _unpack(_exchange_small(_pack([grads[n] for n in SMALL] + [loss_part], F32), reduce=True), small_shapes)
    loss = summed[-1][0, 0]
    for n, g in zip(SMALL, summed[:-1]):
        if n in COL_SHARDED:
            cols = a[n].shape[-1]
            g = lax.dynamic_slice_in_dim(g, k_me * cols, cols, axis=1)
        grads[n] = g

    delta, new_m, new_v = {}, {}, {}
    for n, mine, theirs in zip(MATRICES, halves, other_halves):
        grads[n], delta[n], new_m[n], new_v[n] = _adamw_halves(
            "adamw_" + n, core, shard[n], mine, theirs, a["m_" + n][0], a["v_" + n][0])
    small_w = [a[n][0] if a[n].ndim == 3 else a[n] for n in SMALL]
    small_m = [a["m_" + n][0] if a[n].ndim == 3 else a["m_" + n] for n in SMALL]
    small_v = [a["v_" + n][0] if a[n].ndim == 3 else a["v_" + n] for n in SMALL]
    shapes = [w.shape for w in small_w]
    packed = [_pack(l, F32) for l in (small_w, [grads[n] for n in SMALL], small_m, small_v)]
    for out, buf in zip((delta, new_m, new_v), _adamw("adamw_small", *packed)):
        for n, v in zip(SMALL, _unpack(buf, shapes)):
            out[n] = v

    def shaped(d):
        return [d[n].reshape(a[n].shape) for n in WEIGHTS]
    return (loss, dx[None], *shaped(grads), *shaped(delta), *shaped(new_m), *shaped(new_v))
```

```python
import functools

import jax
import jax.numpy as jnp
from jax import lax
from jax.experimental import pallas as pl
from jax.experimental.pallas import tpu as pltpu

F32 = jnp.float32
BF16 = jnp.bfloat16

D_MODEL = 1024
HEADS = 4
DK = 128
D_GRP = HEADS * DK
CHUNK = 64
ML_CONV = 4
FFN_CONV = 3
D_FF = 2816
CA_DH = D_MODEL // HEADS
DEPTH = 1
ALPHA = (2.0 * DEPTH) ** 0.25
LN_EPS = 1e-5
NEG_BIG = -1e30
D_IN = 8 * D_GRP + 2 * HEADS
D_IN_PAD = 8 * D_GRP + 128
ADAM_LR, ADAM_B1, ADAM_B2, ADAM_EPS, ADAM_WD, ADAM_STEP = 0.001, 0.9, 0.999, 1e-08, 0.01, 10

SUBLANES = 8
LANES = 128
VMEM_BYTES = 64 * 1024 * 1024


def _pcall(body, pin=True, **kw):
    if not pin:
        return _call(body, **kw)
    kw["out_shape"] = jax.tree.map(lambda s: pltpu.HBM(s.shape, s.dtype), kw["out_shape"])
    call = _call(body, **kw)

    def pinned(*args):
        return call(*[pltpu.with_memory_space_constraint(x, pltpu.HBM) if jnp.issubdtype(x.dtype, jnp.floating) else x
                      for x in args])
    return pinned


def _call(body, **kw):
    return pl.pallas_call(body, **kw)


def _params(semantics, vmem_bytes):
    limit = int(min(max(2 * vmem_bytes, 16 * 1024 * 1024), VMEM_BYTES - 8 * 1024 * 1024))
    return pltpu.CompilerParams(dimension_semantics=semantics, vmem_limit_bytes=limit)


def _nbytes(shape, dtype):
    n = 1
    for s in shape:
        n *= s
    return n * jnp.dtype(dtype).itemsize


def _dg(a, b, ca, cb):
    return lax.dot_general(a.astype(BF16), b.astype(BF16), (((ca,), (cb,)), ((), ())),
                           preferred_element_type=F32)


@jax.custom_vjp
def mm_nn(a, b):
    return _dg(a, b, 1, 0)


mm_nn.defvjp(lambda a, b: (_dg(a, b, 1, 0), (a, b)),
             lambda r, g: (_dg(g, r[1], 1, 1).astype(r[0].dtype), _dg(r[0], g, 0, 0).astype(r[1].dtype)))


@jax.custom_vjp
def mm_nt(a, b):
    return _dg(a, b, 1, 1)


mm_nt.defvjp(lambda a, b: (_dg(a, b, 1, 1), (a, b)),
             lambda r, g: (_dg(g, r[1], 1, 0).astype(r[0].dtype), _dg(g, r[0], 0, 0).astype(r[1].dtype)))


@jax.custom_vjp
def mm_tn(a, b):
    return _dg(a, b, 0, 0)


mm_tn.defvjp(lambda a, b: (_dg(a, b, 0, 0), (a, b)),
             lambda r, g: (_dg(r[1], g, 1, 1).astype(r[0].dtype), _dg(r[0], g, 1, 0).astype(r[1].dtype)))


def _tri(n, lower):
    r = lax.broadcasted_iota(jnp.int32, (n, n), 0)
    c = lax.broadcasted_iota(jnp.int32, (n, n), 1)
    return ((r >= c) if lower else (r <= c)).astype(F32)


def _tri_dot(lower, x):
    t = _tri(x.shape[0], lower).astype(BF16)
    hi = x.astype(BF16)
    rest = x - hi.astype(F32)
    mid = rest.astype(BF16)
    lo = (rest - mid.astype(F32)).astype(BF16)
    return sum(lax.dot_general(t, p, (((1,), (0,)), ((), ())), preferred_element_type=F32) for p in (hi, mid, lo))


@jax.custom_vjp
def cumsum_rows(x):
    return _tri_dot(True, x)


cumsum_rows.defvjp(lambda x: (_tri_dot(True, x), None), lambda _, g: (_tri_dot(False, g),))


def _shift_impl(halo, x, d):
    xx = jnp.concatenate([halo, x], axis=0)
    return pltpu.roll(xx, d, 0)[SUBLANES:]


@functools.partial(jax.custom_vjp, nondiff_argnums=(2,))
def shift_rows(halo, x, d):
    return _shift_impl(halo, x, d)


def _shift_bwd(d, _, g):
    n = g.shape[0] + SUBLANES
    gg = jnp.concatenate([jnp.zeros((SUBLANES, g.shape[1]), g.dtype), g], axis=0)
    r = pltpu.roll(gg, n - d, 0)
    return r[:SUBLANES], r[SUBLANES:]


shift_rows.defvjp(lambda halo, x, d: (_shift_impl(halo, x, d), None), _shift_bwd)


def causal_conv(halo, x, w_rows, b):
    k = len(w_rows)
    y = b + w_rows[k - 1] * x
    for d in range(1, k):
        y = y + w_rows[k - 1 - d] * shift_rows(halo, x, d)
    return y


def _sigmoid(x):
    return 1.0 / (1.0 + jnp.exp(-x))


def _silu(x):
    return x * _sigmoid(x)


def _log_sigmoid(x):
    return jnp.minimum(x, 0.0) - jnp.log(1.0 + jnp.exp(-jnp.abs(x)))


def _pick_row(x, i):
    row = lax.broadcasted_iota(jnp.int32, (x.shape[0], 1), 0)
    return jnp.sum(jnp.where(row == i, x, 0.0), axis=0, keepdims=True)


def _layer_norm(z, g, b):
    mu = jnp.mean(z, axis=-1, keepdims=True)
    zc = z - mu
    var = jnp.mean(zc * zc, axis=-1, keepdims=True)
    return zc * lax.rsqrt(var + LN_EPS) * g + b


def _qk_conv(halo, x, w0, w1, w2, w3, b):
    return _silu(causal_conv(halo, x, (w0, w1, w2, w3), b))


def _grp(i, h=None):
    if h is None:
        return pl.ds(i * D_GRP, D_GRP)
    return pl.ds(i * D_GRP + h * DK, DK)


def _mixer_specs(n_chunks, reverse):
    def chunk(c):
        return n_chunks - 1 - c if reverse else c
    row8 = CHUNK // SUBLANES
    proj_spec = pl.BlockSpec((CHUNK, D_IN_PAD), lambda c: (chunk(c), 0))
    halo_spec = pl.BlockSpec((SUBLANES, 2 * D_GRP), lambda c: (jnp.maximum(chunk(c) * row8 - 1, 0), 2))
    small = [pl.BlockSpec((2, D_GRP), lambda c: (0, 0)), pl.BlockSpec((1, D_GRP), lambda c: (0, 0)),
             pl.BlockSpec((ML_CONV, 2 * D_GRP), lambda c: (0, 0)), pl.BlockSpec((1, 2 * D_GRP), lambda c: (0, 0)),
             pl.BlockSpec((1, D_GRP), lambda c: (0, 0))]
    state_specs = [pl.BlockSpec((1, HEADS, DK, DK), lambda c: (chunk(c), 0, 0, 0)),
                   pl.BlockSpec((1, HEADS, DK, DK), lambda c: (chunk(c), 0, 0, 0)),
                   pl.BlockSpec((1, HEADS, 1, DK), lambda c: (chunk(c), 0, 0, 0)),
                   pl.BlockSpec((1, HEADS, 1, DK), lambda c: (chunk(c), 0, 0, 0))]
    y_spec = pl.BlockSpec((CHUNK, 2 * D_GRP), lambda c: (chunk(c), 0))
    return proj_spec, halo_spec, small, state_specs, y_spec, chunk


def _heads(x):
    return [x[:, h * DK:(h + 1) * DK] for h in range(HEADS)]


def _last(x, j):
    lane = lax.broadcasted_iota(jnp.int32, (1, x.shape[-1]), 1)
    return jnp.sum(jnp.where(lane == j, x, 0.0), axis=-1, keepdims=True)


def _hg_chunk(st_t, hq, hf, hi, hgate, l0, l1, nw):
    n = hq.shape[0]
    lb = _sigmoid(l0 - l1)
    q = _silu(hq)
    lf = jnp.log(lb + (1.0 - lb) * _sigmoid(hf))
    k = (1.0 - lb) * _sigmoid(-hf)
    b = cumsum_rows(lf)
    b_ref = _pick_row(b, n // 2 - 1)
    b_last = _pick_row(b, n - 1)
    qa, ka =_heads(q * jnp.exp(b - b_ref)), _heads(k * jnp.exp(b_ref - b))
    qe, kd, eb, v = _heads(q * jnp.exp(b)), _heads(k * jnp.exp(b_last - b)), _heads(jnp.exp(b_last)), _heads(hi)
    tri = _tri(n, True) > 0
    attn = [jnp.where(tri, mm_nt(qa[h], ka[h]), 0.0) for h in range(HEADS)]
    o = [mm_nn(attn[h], v[h]) + mm_nt(qe[h], st_t[h]) for h in range(HEADS)]
    st_new = jnp.stack([eb[h] * st_t[h] + mm_tn(v[h], kd[h]) for h in range(HEADS)])
    yn = [o[h] * lax.rsqrt(jnp.mean(o[h] * o[h], axis=-1, keepdims=True) + LN_EPS) for h in range(HEADS)]
    return st_new, jnp.concatenate(yn, axis=1) * nw * _silu(hgate)


def _ml_chunk(c_st, n_st, m_st, q, k, v, gates, og, nw):
    n = q.shape[0]
    ig = jnp.stack([_last(gates, h) for h in range(HEADS)])
    log_f = _log_sigmoid(gates)
    fl = jnp.stack([_last(log_f, HEADS + h) for h in range(HEADS)])
    bw = cumsum_rows(jnp.concatenate([jnp.broadcast_to(fl[h], (n, DK)) for h in range(HEADS)], axis=1))
    b = jnp.stack([_last(x, 0) for x in _heads(bw)])
    g = jnp.sum(fl, axis=1, keepdims=True)
    eye = lax.broadcasted_iota(jnp.int32, (n, n), 0) == lax.broadcasted_iota(jnp.int32, (n, n), 1)
    e_row = jnp.sum(jnp.where(eye, ig - b, 0.0), axis=1, keepdims=True)
    d = jnp.where(_tri(n, True) > 0, b + e_row, -jnp.inf)
    inter = b + m_st
    m_t = jnp.maximum(inter, jnp.max(d, axis=2, keepdims=True))
    qs, kh, vh = _heads(q * (DK ** -0.5)), _heads(k), _heads(v)
    s = jnp.stack([mm_nt(qs[h], kh[h]) for h in range(HEADS)]) * jnp.exp(d - m_t)
    w_inter = jnp.exp(inter - m_t)
    num = (jnp.stack([mm_nn(s[h], vh[h]) for h in range(HEADS)])
           + w_inter * jnp.stack([mm_nn(qs[h], c_st[h]) for h in range(HEADS)]))
    den = jnp.sum(s, axis=2, keepdims=True) + w_inter * jnp.sum(jnp.stack(qs) * n_st, axis=2, keepdims=True)
    h_out = num / jnp.maximum(jnp.abs(den), jnp.exp(-m_t))
    a = g - b + ig
    m_new = jnp.maximum(g + m_st, jnp.max(a, axis=1, keepdims=True))
    decay = jnp.exp(g + m_st - m_new)
    wk = jnp.stack(kh) * jnp.exp(a - m_new)
    c_new = decay * c_st + jnp.stack([mm_tn(wk[h], vh[h]) for h in range(HEADS)])
    n_new = decay * n_st + jnp.sum(wk, axis=1, keepdims=True)
    hc = h_out - jnp.mean(h_out, axis=-1, keepdims=True)
    yn = hc * lax.rsqrt(jnp.mean(hc * hc, axis=-1, keepdims=True) + LN_EPS)
    y = _sigmoid(og) * (jnp.concatenate([yn[h] for h in range(HEADS)], axis=1) * nw)
    return c_new, n_new, m_new, y


def _mixer_inputs(proj_ref, lg_ref, hnw_ref, mnw_ref, qk):
    hg_in = (proj_ref[:, _grp(0)], proj_ref[:, _grp(1)], proj_ref[:, _grp(2)], proj_ref[:, _grp(3)],
             lg_ref[0:1, :], lg_ref[1:2, :], hnw_ref[...])
    ml_in = (qk[:, :D_GRP], qk[:, D_GRP:], proj_ref[:, _grp(6)], proj_ref[:, pl.ds(8 * D_GRP, LANES)],
             proj_ref[:, _grp(7)], mnw_ref[...])
    return hg_in, ml_in


def _mixer_fwd(proj, lb_logits, hg_nw, conv_w, conv_b, ml_nw):
    seq = proj.shape[0]
    n_chunks = seq // CHUNK
    proj_spec, halo_spec, small, state_specs, y_spec, _ = _mixer_specs(n_chunks, False)

    def body(proj_ref, halo_ref, lg_ref, hnw_ref, cw_ref, cb_ref, mnw_ref,
             y_ref, hst_ref, cst_ref, nst_ref, mst_ref, hs, cs, ns, ms):
        c = pl.program_id(0)

        @pl.when(c == 0)
        def _():
            hs[...] = jnp.zeros_like(hs)
            cs[...] = jnp.zeros_like(cs)
            ns[...] = jnp.zeros_like(ns)
            ms[...] = jnp.full(ms.shape, NEG_BIG, F32)

        hst_ref[0] = hs[...]
        cst_ref[0] = cs[...]
        nst_ref[0] = ns[...]
        mst_ref[0] = ms[...]
        halo = jnp.where(c > 0, halo_ref[...], 0.0)
        qk = _qk_conv(halo, proj_ref[:, pl.ds(4 * D_GRP, 2 * D_GRP)],
                      cw_ref[0:1, :], cw_ref[1:2, :], cw_ref[2:3, :], cw_ref[3:4, :], cb_ref[...])
        hg_in, ml_in = _mixer_inputs(proj_ref, lg_ref, hnw_ref, mnw_ref, qk)
        hs[...], y_hg = _hg_chunk(hs[...], *hg_in)
        cs[...], ns[...], m_new, y_ml = _ml_chunk(cs[...], ns[...], _last(ms[...], 0), *ml_in)
        ms[...] = jnp.broadcast_to(m_new, ms.shape)
        y_ref[:, pl.ds(0, D_GRP)] = y_hg.astype(BF16)
        y_ref[:, pl.ds(D_GRP, D_GRP)] = y_ml.astype(BF16)

    st = jax.ShapeDtypeStruct((n_chunks, HEADS, DK, DK), F32)
    vec = jax.ShapeDtypeStruct((n_chunks, HEADS, 1, DK), F32)
    vmem = 2 * (_nbytes((CHUNK, D_IN_PAD), F32) + _nbytes((CHUNK, 2 * D_GRP), F32) + 2 * _nbytes((HEADS, DK, DK), F32)) \
        + 2 * _nbytes((HEADS, DK, DK), F32)
    return _pcall(
        body, name="mixer_fwd", grid=(n_chunks,),
        in_specs=[proj_spec, halo_spec] + small,
        out_specs=[y_spec] + state_specs,
        out_shape=[jax.ShapeDtypeStruct((seq, 2 * D_GRP), BF16), st, st, vec, vec],
        scratch_shapes=[pltpu.VMEM((HEADS, DK, DK), F32), pltpu.VMEM((HEADS, DK, DK), F32),
                        pltpu.VMEM((HEADS, 1, DK), F32), pltpu.VMEM((HEADS, 1, DK), F32)],
        compiler_params=_params(("arbitrary",), vmem),
    )(proj, proj, lb_logits, hg_nw, conv_w, conv_b, ml_nw)


def _mixer_bwd(proj, dy, hst, cst, nst, mst, lb_logits, hg_nw, conv_w, conv_b, ml_nw):
    seq = proj.shape[0]
    n_chunks = seq // CHUNK
    proj_spec, halo_spec, small, state_specs, y_spec, _ = _mixer_specs(n_chunks, True)

    def body(proj_ref, halo_ref, dy_ref, hst_ref, cst_ref, nst_ref, mst_ref,
             lg_ref, hnw_ref, cw_ref, cb_ref, mnw_ref,
             dproj_ref, dbin_ref, dlg_ref, dhnw_ref, dcw_ref, dcb_ref, dmnw_ref,
             dhs, dcs, dns, dms, dhalo):
        c = pl.program_id(0)

        @pl.when(c == 0)
        def _():
            for r in (dhs, dcs, dns, dms, dhalo, dbin_ref, dlg_ref, dhnw_ref, dcw_ref, dcb_ref, dmnw_ref):
                r[...] = jnp.zeros_like(r)

        def put(cols, val):
            dproj_ref[:, cols] = val.astype(BF16)
            dbin_ref[:, cols] += jnp.sum(val, axis=0, keepdims=True)

        first = c == n_chunks - 1
        halo = jnp.where(first, 0.0, halo_ref[...])
        x_qk = proj_ref[:, pl.ds(4 * D_GRP, 2 * D_GRP)]
        conv_args = (halo, x_qk, cw_ref[0:1, :], cw_ref[1:2, :], cw_ref[2:3, :], cw_ref[3:4, :], cb_ref[...])
        qk, conv_vjp = jax.vjp(_qk_conv, *conv_args)
        hg_in, ml_in = _mixer_inputs(proj_ref, lg_ref, hnw_ref, mnw_ref, qk)
        _, hg_vjp = jax.vjp(_hg_chunk, hst_ref[0], *hg_in)
        _, ml_vjp = jax.vjp(_ml_chunk, cst_ref[0], nst_ref[0], _last(mst_ref[0], 0), *ml_in)
        dst, dhq, dhf, dhi, dhg, dl0, dl1, dnw = hg_vjp((dhs[...], dy_ref[:, pl.ds(0, D_GRP)]))
        dc, dn, dm, dq, dk, dv, dgates, dog, dmn = ml_vjp(
            (dcs[...], dns[...], _last(dms[...], 0), dy_ref[:, pl.ds(D_GRP, D_GRP)]))
        dhs[...] = dst
        dcs[...] = dc
        dns[...] = dn
        dms[...] = jnp.broadcast_to(dm, dms.shape)
        for i, val in ((0, dhq), (1, dhf), (2, dhi), (3, dhg), (6, dv), (7, dog)):
            put(_grp(i), val)
        put(pl.ds(8 * D_GRP, LANES), dgates)
        dlg_ref[0:1, :] += dl0
        dlg_ref[1:2, :] += dl1
        dhnw_ref[...] += dnw
        dmnw_ref[...] += dmn
        dh, dx, dw0, dw1, dw2, dw3, db = conv_vjp(jnp.concatenate([dq, dk], axis=1))
        tail = jnp.concatenate([jnp.zeros((CHUNK - SUBLANES, 2 * D_GRP), F32), dhalo[...]], axis=0)
        put(pl.ds(4 * D_GRP, 2 * D_GRP), dx + tail)
        dhalo[...] = dh
        for d, dw in enumerate((dw0, dw1, dw2, dw3)):
            dcw_ref[d:d + 1, :] += dw
        dcb_ref[...] += db

    row = pl.BlockSpec((1, D_GRP), lambda c: (0, 0))
    small_out = [pl.BlockSpec((1, D_IN_PAD), lambda c: (0, 0)), pl.BlockSpec((2, D_GRP), lambda c: (0, 0)), row,
                 pl.BlockSpec((ML_CONV, 2 * D_GRP), lambda c: (0, 0)), pl.BlockSpec((1, 2 * D_GRP), lambda c: (0, 0)), row]
    dy_spec = pl.BlockSpec((CHUNK, 2 * D_GRP), y_spec.index_map)
    vmem = 2 * (2 * _nbytes((CHUNK, D_IN_PAD), F32) + _nbytes((CHUNK, 2 * D_GRP), F32)
                + 2 * _nbytes((HEADS, DK, DK), F32)) + 2 * _nbytes((HEADS, DK, DK), F32) + 4 * 1024 * 1024
    return _pcall(
        body, name="mixer_bwd", grid=(n_chunks,),
        in_specs=[proj_spec, halo_spec, dy_spec] + state_specs + small,
        out_specs=[proj_spec] + small_out,
        out_shape=[jax.ShapeDtypeStruct((seq, D_IN_PAD), BF16), jax.ShapeDtypeStruct((1, D_IN_PAD), F32),
                   jax.ShapeDtypeStruct((2, D_GRP), F32), jax.ShapeDtypeStruct((1, D_GRP), F32),
                   jax.ShapeDtypeStruct((ML_CONV, 2 * D_GRP), F32), jax.ShapeDtypeStruct((1, 2 * D_GRP), F32),
                   jax.ShapeDtypeStruct((1, D_GRP), F32)],
        scratch_shapes=[pltpu.VMEM((HEADS, DK, DK), F32), pltpu.VMEM((HEADS, DK, DK), F32),
                        pltpu.VMEM((HEADS, 1, DK), F32), pltpu.VMEM((HEADS, 1, DK), F32),
                        pltpu.VMEM((SUBLANES, 2 * D_GRP), F32)],
        compiler_params=_params(("arbitrary",), vmem),
    )(proj, proj, dy, hst, cst, nst, mst, lb_logits, hg_nw, conv_w, conv_b, ml_nw)


def _tile(n, prefs, unit=None):
    unit = unit or n
    for p in prefs:
        if unit % p == 0 and n % p == 0:
            return p
    return unit


def _logical(arr):
    return arr.shape if arr.ndim == 2 else (arr.shape[1], arr.shape[0] * arr.shape[2])


def _group(arr):
    return arr.shape[-1]


def _split_spec(ndim, group, tr, tc, where):
    if ndim == 2:
        return pl.BlockSpec((tr, tc), where)
    per = group // tc
    assert per * tc == group, (group, tc)

    def index(*ids):
        bi, bj = where(*ids)
        return (bj // per, bi, bj % per)
    return pl.BlockSpec((None, tr, tc), index)


def _mm(name, mode, a, b, *, bias=None, res=None, res_scale=1.0, ln=None, out_dtype=F32, out_groups=None,
        copy_dtype=None, tm=None, tn=None, tk=None):
    la, lb = _logical(a), _logical(b)
    if mode == "nn":
        (m, k), n = la, lb[1]
        n_unit = _group(b) if b.ndim == 3 else n
        kc = _group(a) if a.ndim == 3 else k
    elif mode == "nt":
        (m, k), n = la, lb[0]
        n_unit = n
        kc = min(_group(a) if a.ndim == 3 else k, _group(b) if b.ndim == 3 else k)
    else:
        (k, m), n = la, lb[1]
        n_unit, kc = (_group(b) if b.ndim == 3 else n), k
        assert a.ndim == 2
    if out_groups:
        n_unit = min(n_unit, n // out_groups)
    kind = ln[0] if ln else None
    tm = tm or (256 if ln else _tile(m, (512, 256, 128)))
    tn = n if ln else (tn or _tile(n, (512, 384, 256, 128), n_unit))
    tk = (tk or _tile(k, (2048, 512, 256, 128))) if mode == "tn" else k
    gi, gj, gk = m // tm, n // tn, k // tk
    assert gi * tm == m and gj * tn == n and gk * tk == k and n_unit % tn == 0, (name, m, n, k, tm, tn, tk)
    ca, cb = {"nn": (1, 0), "nt": (1, 1), "tn": (0, 0)}[mode]
    i_outer = gk > 1 or (gi - 1) * _nbytes(b.shape, b.dtype) <= (gj - 1) * _nbytes(a.shape, a.dtype)

    def ij(where):
        return (lambda p, q, kk: where(p, q, kk)) if i_outer else (lambda p, q, kk: where(q, p, kk))
    if mode == "tn":
        a_spec = pl.BlockSpec((tk, tm), ij(lambda i, j, kk: (kk, i)))
    elif a.ndim == 3:
        a_spec = pl.BlockSpec((a.shape[0], tm, _group(a)), ij(lambda i, j, kk: (0, i, 0)))
    else:
        a_spec = pl.BlockSpec((tm, k), ij(lambda i, j, kk: (i, 0)))
    if mode != "nt":
        b_spec = _split_spec(b.ndim, _group(b), tk, tn, ij(lambda i, j, kk: (kk, j)))
    elif b.ndim == 3:
        b_spec = pl.BlockSpec((b.shape[0], tn, _group(b)), ij(lambda i, j, kk: (0, j, 0)))
    else:
        b_spec = pl.BlockSpec((tn, k), ij(lambda i, j, kk: (j, 0)))
    row_spec = pl.BlockSpec((1, tn), ij(lambda i, j, kk: (0, j)))
    blk_spec = pl.BlockSpec((tm, tn), ij(lambda i, j, kk: (i, j)))
    ins, in_specs = [a, b], [a_spec, b_spec]
    if bias is not None:
        ins.append(bias), in_specs.append(row_spec)
    if res is not None:
        ins.append(res), in_specs.append(blk_spec)
    if kind == "fwd":
        ins += [ln[1], ln[2]]
        in_specs += [row_spec, row_spec]
    elif kind == "loss":
        ins += [ln[1], ln[2], ln[3]]
        in_specs += [row_spec, row_spec, blk_spec]
    elif kind == "bwd":
        ins += [ln[1], ln[2], ln[3]]
        in_specs += [blk_spec, row_spec, row_spec]
    if out_groups:
        blk_out = jax.ShapeDtypeStruct((out_groups, m, n // out_groups), out_dtype)
        out_spec = _split_spec(3, n // out_groups, tm, tn, ij(lambda i, j, kk: (i, j)))
    else:
        blk_out, out_spec = jax.ShapeDtypeStruct((m, n), out_dtype), blk_spec
    row_out = jax.ShapeDtypeStruct((1, n), F32)
    if kind is None:
        out_shape, out_specs = [blk_out], [out_spec]
    elif kind == "fwd":
        out_shape, out_specs = [blk_out, blk_out], [blk_spec, blk_spec]
    else:
        out_shape, out_specs = [blk_out, row_out, row_out], [blk_spec, row_spec, row_spec]
        if kind == "loss":
            out_shape.append(jax.ShapeDtypeStruct((1, LANES), F32))
            out_specs.append(pl.BlockSpec((1, LANES), lambda p, q, kk: (0, 0)))
    if copy_dtype is not None:
        out_shape.append(jax.ShapeDtypeStruct((m, n), copy_dtype))
        out_specs.append(blk_spec)
    n_in = len(ins)

    def body(*refs):
        in_refs, out_refs, acc_ref = refs[:n_in], refs[n_in:n_in + len(out_shape)], refs[-1]
        i, kk = pl.program_id(0 if i_outer else 1), pl.program_id(2)
        a_ref, b_ref = in_refs[:2]
        extra = list(in_refs[2:])

        def epilogue(acc, rows=slice(None)):
            rest = list(extra)
            if bias is not None:
                acc = acc + rest.pop(0)[...]
            if res is not None:
                acc = acc + res_scale * rest.pop(0)[rows, :]
            if kind is None:
                out_refs[0][...] = acc.astype(out_dtype)
                return
            if kind == "fwd":
                out_refs[0][rows, :] = acc
                y = _layer_norm(acc, rest[0][...], rest[1][...])
                out_refs[1][rows, :] = y
                if copy_dtype is not None:
                    out_refs[-1][rows, :] = y.astype(copy_dtype)
                return
            if kind == "loss":
                y, vjp = jax.vjp(_layer_norm, acc, rest[0][...], rest[1][...])
                err = y - rest[2][rows, :]
                part = 0.5 * jnp.sum(jnp.sum(err * err, axis=1, keepdims=True), axis=0, keepdims=True) / n
                dz, dg, db = vjp(err / n)
            else:
                _, vjp = jax.vjp(_layer_norm, rest[0][rows, :], rest[1][...], rest[2][...])
                dz, dg, db = vjp(acc)
            out_refs[0][rows, :] = dz
            out_refs[1][...] += dg
            out_refs[2][...] += db
            if kind == "loss":
                out_refs[3][...] += jnp.broadcast_to(part, (1, LANES))
            if copy_dtype is not None:
                out_refs[-1][rows, :] = dz.astype(copy_dtype)

        if kind in ("loss", "bwd"):
            @pl.when((i == 0) & (kk == 0))
            def _():
                for r in out_refs[1:3 + (kind == "loss")]:
                    r[...] = jnp.zeros_like(r)

        def chunk(ref, c0, last):
            if ref.ndim == 3:
                g = ref.shape[2]
                return ref[c0 // g, :, pl.ds(c0 % g, kc)]
            return ref[:, pl.ds(c0, kc)] if last else ref[pl.ds(c0, kc), :]

        if mode == "tn" or kc == k:
            prod = _dg(a_ref[...], b_ref[...], ca, cb)
        else:
            prod = None
            for c0 in range(0, k, kc):
                part = _dg(chunk(a_ref, c0, True), chunk(b_ref, c0, mode == "nt"), ca, cb)
                prod = part if prod is None else prod + part
        if gk == 1:
            epilogue(prod)
            return

        @pl.when(kk == 0)
        def _():
            acc_ref[...] = prod

        @pl.when(kk > 0)
        def _():
            acc_ref[...] += prod

        @pl.when(kk == gk - 1)
        def _():
            epilogue(acc_ref[...])

    vmem = (2 * (_nbytes((tm, tk), a.dtype) + _nbytes((tk, tn), b.dtype))
            + (2 * len(ins) + 2 * len(out_shape) + 1) * _nbytes((tm, tn), F32))
    outs = _pcall(
        body, name=name, grid=(gi, gj, gk) if i_outer else (gj, gi, gk), in_specs=in_specs, out_specs=out_specs,
        out_shape=out_shape, scratch_shapes=[pltpu.VMEM((tm, tn) if gk > 1 else (SUBLANES, LANES), F32)],
        compiler_params=_params(("arbitrary", "arbitrary", "arbitrary"), vmem),
    )(*ins)
    return outs[0] if (kind is None and copy_dtype is None) else outs


def _attn_head(q, k, v):
    sc = mm_nt(q, k) * (CA_DH ** -0.5)
    e = jnp.exp(sc - jnp.max(sc, axis=-1, keepdims=True))
    return mm_nn(e / jnp.sum(e, axis=-1, keepdims=True), v)


def _attn_fwd(q, kv):
    seq, n_mem = q.shape[0], kv.shape[0]
    tq = _tile(seq, (512, 256, 128))

    def body(q_ref, kv_ref, o_ref):
        for h in range(HEADS):
            hd = pl.ds(h * CA_DH, CA_DH)
            o = _attn_head(q_ref[:, hd], kv_ref[:, hd], kv_ref[:, pl.ds(D_MODEL + h * CA_DH, CA_DH)])
            o_ref[:, hd] = o.astype(BF16)

    return _pcall(
        body, name="attn_fwd", grid=(seq // tq,),
        in_specs=[pl.BlockSpec((tq, D_MODEL), lambda i: (i, 0)), pl.BlockSpec((n_mem, 2 * D_MODEL), lambda i: (0, 0))],
        out_specs=pl.BlockSpec((tq, D_MODEL), lambda i: (i, 0)), out_shape=jax.ShapeDtypeStruct((seq, D_MODEL), BF16),
        compiler_params=_params(("arbitrary",), 4 * _nbytes((tq, D_MODEL), F32) + 2 * _nbytes((n_mem, 2 * D_MODEL), F32)),
    )(q, kv)


def _attn_bwd(q, kv, do):
    seq, n_mem = q.shape[0], kv.shape[0]
    tq = _tile(seq, (512, 256, 128))

    def body(q_ref, kv_ref, do_ref, dq_ref, dkv_ref):
        @pl.when(pl.program_id(0) == 0)
        def _():
            dkv_ref[...] = jnp.zeros_like(dkv_ref)

        for h in range(HEADS):
            hd = pl.ds(h * CA_DH, CA_DH)
            vd = pl.ds(D_MODEL + h * CA_DH, CA_DH)
            _, vjp = jax.vjp(_attn_head, q_ref[:, hd], kv_ref[:, hd], kv_ref[:, vd])
            dq, dk, dv = vjp(do_ref[:, hd].astype(F32))
            dq_ref[:, hd] = dq.astype(BF16)
            dkv_ref[:, hd] += dk
            dkv_ref[:, vd] += dv

    return _pcall(
        body, name="attn_bwd", grid=(seq // tq,),
        in_specs=[pl.BlockSpec((tq, D_MODEL), lambda i: (i, 0)), pl.BlockSpec((n_mem, 2 * D_MODEL), lambda i: (0, 0)),
                  pl.BlockSpec((tq, D_MODEL), lambda i: (i, 0))],
        out_specs=[pl.BlockSpec((tq, D_MODEL), lambda i: (i, 0)), pl.BlockSpec((n_mem, 2 * D_MODEL), lambda i: (0, 0))],
        out_shape=[jax.ShapeDtypeStruct((seq, D_MODEL), BF16), jax.ShapeDtypeStruct((n_mem, 2 * D_MODEL), F32)],
        compiler_params=_params(("arbitrary",), 6 * _nbytes((tq, D_MODEL), F32) + 4 * _nbytes((n_mem, 2 * D_MODEL), F32)),
    )(q, kv, do)


def _ffn_mid(hg, xg, hv, xv, wg0, wg1, wg2, bg, wv0, wv1, wv2, bv):
    return jax.nn.gelu(causal_conv(hg, xg, (wg0, wg1, wg2), bg)) * causal_conv(hv, xv, (wv0, wv1, wv2), bv)


FFN_TB = 256
FFN_W = D_FF // 2
FFN_J = D_FF // FFN_W
MXU_COLS = 256
FFN_PIECES = tuple((off, min(MXU_COLS, FFN_W - off)) for off in range(0, FFN_W, MXU_COLS))


def _ffn_common_specs(seq, row):
    tb = min(FFN_TB, seq)
    full = pl.BlockSpec((tb, D_MODEL), lambda t, j: (row(t), 0))
    vec = pl.BlockSpec((1, D_MODEL), lambda t, j: (0, 0))
    halves = []
    for off in (0, FFN_J):
        halves.append(dict(
            w_up=pl.BlockSpec((None, D_MODEL, FFN_W), lambda t, j, off=off: (j + off, 0, 0)),
            taps=pl.BlockSpec((FFN_CONV, FFN_W), lambda t, j, off=off: (0, j + off)),
            bias=pl.BlockSpec((1, FFN_W), lambda t, j, off=off: (0, j + off))))
    w_down = pl.BlockSpec((FFN_W, D_MODEL), lambda t, j: (j, 0))
    u_blk = pl.BlockSpec((2, tb, FFN_W), lambda t, j: (0, row(t), j))
    return tb, full, vec, halves, w_down, u_blk


def _ffn_vmem(tb):
    return (_nbytes((2, tb, FFN_W), F32) + _nbytes((2, tb, FFN_W), BF16) + 3 * _nbytes((D_MODEL, FFN_W), BF16)
            + 10 * _nbytes((tb, D_MODEL), F32))


def _conv_params(taps_ref, bias_ref, cols):
    return taps_ref[0:1, cols], taps_ref[1:2, cols], taps_ref[2:3, cols], bias_ref[:, cols]


def _ffn_fwd(x2b, x2, w_up, conv_w, conv_b, w_down, ln_g, ln_b, target):
    seq = x2.shape[0]
    tb, full, vec, halves, wd_spec, u_blk = _ffn_common_specs(seq, lambda t: t)
    nt = seq // tb

    def body(xb_ref, wg_ref, wv_ref, tg_ref, tv_ref, bg_ref, bv_ref, wd_ref, x_ref, g_ref, b_ref, tgt_ref,
             u_ref, h_ref, dz_ref, dg_ref, db_ref, loss_ref, dzb_ref, acc, carry):
        t, j = pl.program_id(0), pl.program_id(1)
        xb = xb_ref[...]
        pieces = [pl.ds(off, width) for off, width in FFN_PIECES]
        ug = [_dg(xb, wg_ref[:, cols], 1, 0) for cols in pieces]
        uv = [_dg(xb, wv_ref[:, cols], 1, 0) for cols in pieces]
        hs = []
        for cols, g, v in zip(pieces, ug, uv):
            u_ref[0, :, cols] = g
            u_ref[1, :, cols] = v
            halo_g = jnp.where(t == 0, 0.0, carry[j, 0, :, cols])
            halo_v = jnp.where(t == 0, 0.0, carry[j, 1, :, cols])
            h = _ffn_mid(halo_g, g, halo_v, v, *_conv_params(tg_ref, bg_ref, cols),
                         *_conv_params(tv_ref, bv_ref, cols)).astype(BF16)
            carry[j, 0, :, cols] = g[tb - SUBLANES:, :]
            carry[j, 1, :, cols] = v[tb - SUBLANES:, :]
            h_ref[:, cols] = h
            hs.append(h)
        part = None
        for cols, h in zip(pieces, hs):
            p = _dg(h, wd_ref[cols, :], 1, 0)
            part = p if part is None else part + p

        @pl.when(j == 0)
        def _():
            acc[...] = part

        @pl.when(j > 0)
        def _():
            acc[...] += part

        @pl.when(j == FFN_J - 1)
        def _():
            y, vjp = jax.vjp(_layer_norm, acc[...] + ALPHA * x_ref[...], g_ref[...], b_ref[...])
            err = y - tgt_ref[...]
            part_loss = 0.5 * jnp.sum(jnp.sum(err * err, axis=1, keepdims=True), axis=0, keepdims=True) / D_MODEL
            dz, dg, db = vjp(err / D_MODEL)

            @pl.when(t == 0)
            def _():
                for r in (dg_ref, db_ref, loss_ref):
                    r[...] = jnp.zeros_like(r)

            dz_ref[...] = dz
            dzb_ref[...] = dz.astype(BF16)
            dg_ref[...] += dg
            db_ref[...] += db
            loss_ref[...] += jnp.broadcast_to(part_loss, (1, LANES))

    h0, h1 = halves
    row = jax.ShapeDtypeStruct((1, D_MODEL), F32)
    return _pcall(
        body, name="ffn_fwd", grid=(nt, FFN_J),
        in_specs=[full, h0["w_up"], h1["w_up"], h0["taps"], h1["taps"], h0["bias"], h1["bias"], wd_spec, full, vec, vec,
                  full],
        out_specs=[u_blk, pl.BlockSpec((tb, FFN_W), lambda t, j: (t, j)), full, vec, vec,
                   pl.BlockSpec((1, LANES), lambda t, j: (0, 0)), full],
        out_shape=[jax.ShapeDtypeStruct((2, seq, D_FF), F32), jax.ShapeDtypeStruct((seq, D_FF), BF16),
                   jax.ShapeDtypeStruct((seq, D_MODEL), F32), row, row, jax.ShapeDtypeStruct((1, LANES), F32),
                   jax.ShapeDtypeStruct((seq, D_MODEL), BF16)],
        scratch_shapes=[pltpu.VMEM((tb, D_MODEL), F32), pltpu.VMEM((FFN_J, 2, SUBLANES, FFN_W), F32)],
        compiler_params=_params(("arbitrary", "arbitrary"), _ffn_vmem(tb)),
    )(x2b, w_up, w_up, conv_w, conv_w, conv_b, conv_b, w_down, x2, ln_g, ln_b, target)


def _ffn_bwd(u, conv_w, conv_b, dz3b, dz3, w_down, w_up, z2, ln_g, ln_b):
    seq = dz3.shape[0]
    tb = min(FFN_TB, seq)
    nt = seq // tb
    row8 = tb // SUBLANES
    tb, full, vec, halves, wd_spec, u_blk = _ffn_common_specs(seq, lambda t: nt - 1 - t)
    halo = pl.BlockSpec((2, SUBLANES, FFN_W), lambda t, j: (0, jnp.maximum((nt - 1 - t) * row8 - 1, 0), j))

    def body(u_ref, halo_ref, tg_ref, tv_ref, bg_ref, bv_ref, dzb_ref, wd_ref, wg_ref, wv_ref, dz3_ref, z_ref, g_ref,
             b_ref, du_ref, dw_ref, dbias_ref, dz_ref, dg_ref, db_ref, dz2b_ref, acc, carry):
        t, j = pl.program_id(0), pl.program_id(1)

        @pl.when((t == 0) & (j == 0))
        def _():
            for r in (dw_ref, dbias_ref, dg_ref, db_ref):
                r[...] = jnp.zeros_like(r)

        pieces = [pl.ds(off, width) for off, width in FFN_PIECES]
        dzb = dzb_ref[...]
        dhs = [_dg(dzb, wd_ref[cols, :], 1, 1) for cols in pieces]
        first = t == nt - 1
        dus = []
        for cols, dh in zip(pieces, dhs):
            args = (jnp.where(first, 0.0, halo_ref[0, :, cols]), u_ref[0, :, cols],
                    jnp.where(first, 0.0, halo_ref[1, :, cols]), u_ref[1, :, cols],
                    *_conv_params(tg_ref, bg_ref, cols), *_conv_params(tv_ref, bv_ref, cols))
            _, vjp = jax.vjp(_ffn_mid, *args)
            dhg, dxg, dhv, dxv, g0, g1, g2, gb, v0, v1, v2, vb = vjp(dh)
            zeros = jnp.zeros((tb - SUBLANES, dh.shape[1]), F32)
            dug = (dxg + jnp.concatenate([zeros, jnp.where(t == 0, 0.0, carry[j, 0, :, cols])], axis=0)).astype(BF16)
            duv = (dxv + jnp.concatenate([zeros, jnp.where(t == 0, 0.0, carry[j, 1, :, cols])], axis=0)).astype(BF16)
            carry[j, 0, :, cols] = dhg
            carry[j, 1, :, cols] = dhv
            du_ref[0, :, cols] = dug
            du_ref[1, :, cols] = duv
            for half, parts in enumerate(((g0, g1, g2), (v0, v1, v2))):
                for d, p in enumerate(parts):
                    dw_ref[j, half, d:d + 1, cols] += p
            dbias_ref[j, 0, :, cols] += gb
            dbias_ref[j, 1, :, cols] += vb
            dus.append((dug, duv))
        part = None
        for cols, (dug, duv) in zip(pieces, dus):
            p = _dg(dug, wg_ref[:, cols], 1, 1) + _dg(duv, wv_ref[:, cols], 1, 1)
            part = p if part is None else part + p

        @pl.when(j == 0)
        def _():
            acc[...] = part

        @pl.when(j > 0)
        def _():
            acc[...] += part

        @pl.when(j == FFN_J - 1)
        def _():
            _, ln_vjp = jax.vjp(_layer_norm, z_ref[...], g_ref[...], b_ref[...])
            dz, dg, db = ln_vjp(acc[...] + ALPHA * dz3_ref[...])
            dz_ref[...] = dz
            dz2b_ref[...] = dz.astype(BF16)
            dg_ref[...] += dg
            db_ref[...] += db

    h0, h1 = halves
    row = jax.ShapeDtypeStruct((1, D_MODEL), F32)
    whole = lambda *shape: pl.BlockSpec(shape, lambda t, j: (0,) * len(shape))
    return _pcall(
        body, name="ffn_bwd", grid=(nt, FFN_J),
        in_specs=[u_blk, halo, h0["taps"], h1["taps"], h0["bias"], h1["bias"], full, wd_spec, h0["w_up"], h1["w_up"],
                  full, full, vec, vec],
        out_specs=[u_blk, whole(FFN_J, 2, FFN_CONV, FFN_W), whole(FFN_J, 2, 1, FFN_W), full, vec, vec, full],
        out_shape=[jax.ShapeDtypeStruct((2, seq, D_FF), BF16), jax.ShapeDtypeStruct((FFN_J, 2, FFN_CONV, FFN_W), F32),
                   jax.ShapeDtypeStruct((FFN_J, 2, 1, FFN_W), F32), jax.ShapeDtypeStruct((seq, D_MODEL), F32), row, row,
                   jax.ShapeDtypeStruct((seq, D_MODEL), BF16)],
        scratch_shapes=[pltpu.VMEM((tb, D_MODEL), F32), pltpu.VMEM((FFN_J, 2, SUBLANES, FFN_W), F32)],
        compiler_params=_params(("arbitrary", "arbitrary"), _ffn_vmem(tb)),
    )(u, u, conv_w, conv_w, conv_b, conv_b, dz3b, w_down, w_up, w_up, dz3, z2, ln_g, ln_b)


def _adamw_math(w, g, m, v):
    m_new = ADAM_B1 * m + (1.0 - ADAM_B1) * g
    v_new = ADAM_B2 * v + (1.0 - ADAM_B2) * jnp.square(g)
    m_hat = m_new / (1.0 - ADAM_B1 ** ADAM_STEP)
    v_hat = v_new / (1.0 - ADAM_B2 ** ADAM_STEP)
    return -ADAM_LR * (m_hat / (jnp.sqrt(v_hat) + ADAM_EPS) + ADAM_WD * w), m_new, v_new


def _adamw(name, w, g, m, v):
    rows, cols = w.shape
    tr = _tile(rows, (256, 176, 128, 64, 40, 32, 16, 8))

    def body(w_ref, g_ref, m_ref, v_ref, d_ref, nm_ref, nv_ref):
        d_ref[...], nm_ref[...], nv_ref[...] = _adamw_math(w_ref[...], g_ref[...], m_ref[...], v_ref[...])

    spec = pl.BlockSpec((tr, cols), lambda i: (i, 0))
    sh = jax.ShapeDtypeStruct((rows, cols), F32)
    return _pcall(
        body, name=name, grid=(rows // tr,), in_specs=[spec] * 4, out_specs=[spec] * 3, out_shape=[sh] * 3,
        compiler_params=_params(("arbitrary",), 14 * _nbytes((tr, -(-cols // LANES) * LANES), F32)),
    )(w, g, m, v)


def _adamw_halves(name, core, w, mine, theirs, m, v):
    rows, cols = w.shape
    half_rows = mine.shape[0]
    tr = _tile(half_rows, (256, 176, 128))
    nbh = half_rows // tr
    assert 2 * half_rows == rows

    def body(c_ref, w_ref, a_ref, b_ref, m_ref, v_ref, g_ref, d_ref, nm_ref, nv_ref):
        g = jnp.where(pl.program_id(0) // nbh == c_ref[0], a_ref[...], b_ref[...])
        g_ref[...] = g
        d_ref[...], nm_ref[...], nv_ref[...] = _adamw_math(w_ref[...], g, m_ref[...], v_ref[...])

    spec = pl.BlockSpec((tr, cols), lambda i, c_ref: (i, 0))
    half = pl.BlockSpec((tr, cols), lambda i, c_ref: (i % nbh, 0))
    sh = jax.ShapeDtypeStruct((rows, cols), F32)
    grid_spec = pltpu.PrefetchScalarGridSpec(
        num_scalar_prefetch=1, grid=(rows // tr,), in_specs=[spec, half, half, spec, spec], out_specs=[spec] * 4)
    return _pcall(
        body, name=name, grid_spec=grid_spec, out_shape=[sh] * 4,
        compiler_params=_params(("arbitrary",), 18 * _nbytes((tr, -(-cols // LANES) * LANES), F32)),
    )(core, w, mine, theirs, m, v)


MESH = pl.DeviceIdType.MESH
ANY = pl.BlockSpec(memory_space=pl.ANY)
N_CHIPS = 4
N_DEV = 8
BF16_ROWS = 16


def _me():
    return lax.axis_index("x"), lax.axis_index("y"), lax.axis_index("c")


def _other_chips(x, y):
    return [(1 - x, y), (x, 1 - y), (1 - x, 1 - y)]


def _remote(src, dst, ssem, rsem, dev):
    return pltpu.make_async_remote_copy(src_ref=src, dst_ref=dst, send_sem=ssem, recv_sem=rsem,
                                        device_id=dev, device_id_type=MESH)


def _half_rows(ref_rows, cc):
    half = ref_rows // 2
    return pl.ds(pl.multiple_of(cc * half, BF16_ROWS), half)


def _gather_weights(shards):
    n = len(shards)
    n_ici = n * (N_CHIPS - 1)

    def body(*refs):
        ins, outs, (ssem, rsem, lsem, lrsem) = refs[:n], refs[n:2 * n], refs[2 * n:]
        x, y, c = _me()
        k_me = 2 * x + y
        sib = (x, y, 1 - c)
        chips = _other_chips(x, y)
        started = []
        for i, (w_ref, o_ref) in enumerate(zip(ins, outs)):
            cp = _remote(w_ref, o_ref.at[k_me], lsem.at[i], lrsem.at[i], sib)
            cp.start()
            started.append(cp)
        for r, (px, py) in enumerate(chips):
            for i, (w_ref, o_ref) in enumerate(zip(ins, outs)):
                rows = _half_rows(w_ref.shape[0], c)
                s = r * n + i
                cp = _remote(w_ref.at[rows], o_ref.at[k_me, rows], ssem.at[s], rsem.at[s], (px, py, c))
                cp.start()
                started.append(cp)
        for r, (px, py) in enumerate(chips):
            for i, o_ref in enumerate(outs):
                blk = o_ref.at[2 * px + py, _half_rows(o_ref.shape[1], c)]
                s = r * n + i
                _remote(blk, blk, ssem.at[s], rsem.at[s], (px, py, c)).wait_recv()
                cp = _remote(blk, blk, ssem.at[n_ici + s], rsem.at[n_ici + s], sib)
                cp.start()
                started.append(cp)
        for r, (px, py) in enumerate(chips):
            for i, o_ref in enumerate(outs):
                blk = o_ref.at[2 * px + py, _half_rows(o_ref.shape[1], 1 - c)]
                s = n_ici + r * n + i
                _remote(blk, blk, ssem.at[s], rsem.at[s], sib).wait_recv()
        for cp in started[n:]:
            cp.wait_send()
        for cp in started[:n]:
            cp.wait()

    return _pcall(
        body, name="gather_weights", in_specs=[ANY] * n, out_specs=[ANY] * n,
        out_shape=[jax.ShapeDtypeStruct((N_CHIPS,) + s.shape, s.dtype) for s in shards],
        scratch_shapes=[pltpu.SemaphoreType.DMA((2 * n_ici,)), pltpu.SemaphoreType.DMA((2 * n_ici,)),
                        pltpu.SemaphoreType.DMA((n,)), pltpu.SemaphoreType.DMA((n,))],
    )(*shards)


def _swap_halves(name, grads):
    n = len(grads)

    def body(*refs):
        ins, outs, (ssem, rsem) = refs[:n], refs[n:2 * n], refs[2 * n:]
        x, y, c = _me()
        copies = []
        for i, (g_ref, o_ref) in enumerate(zip(ins, outs)):
            for k in range(N_CHIPS):
                s = i * N_CHIPS + k
                cp = _remote(g_ref.at[k, _half_rows(g_ref.shape[1], 1 - c)], o_ref.at[k], ssem.at[s], rsem.at[s],
                             (x, y, 1 - c))
                cp.start()
                copies.append(cp)
        for cp in copies:
            cp.wait()

    return _pcall(
        body, name=name, in_specs=[ANY] * n, out_specs=[ANY] * n,
        out_shape=[jax.ShapeDtypeStruct((N_CHIPS, g.shape[1] // 2, g.shape[2]), g.dtype) for g in grads],
        scratch_shapes=[pltpu.SemaphoreType.DMA((n * N_CHIPS,)), pltpu.SemaphoreType.DMA((n * N_CHIPS,))],
    )(*grads)


SEM = pl.BlockSpec(memory_space=pltpu.SEMAPHORE)
IN_HBM = pl.BlockSpec(memory_space=pltpu.HBM)
SPLIT_PARAMS = dict(compiler_params=pltpu.CompilerParams(has_side_effects=pltpu.SideEffectType.DATAFLOW_SIDE_EFFECTING))


def _split_start(name, sources, landings, n_copies, plan):
    ns, nl = len(sources), len(landings)

    def body(*refs):
        ins, lands, (ssem, rsem), token = refs[:ns], refs[ns:ns + nl], refs[ns + nl:ns + nl + 2], refs[-1]
        for s, (src, dst, _, dev) in enumerate(plan(ins, lands)):
            _remote(src, dst, ssem.at[s], rsem.at[s], dev).start()
        token[...] = jnp.zeros_like(token)

    arrays = list(sources) + list(landings)
    outs = _call(
        body, name=name, in_specs=[IN_HBM] * (ns + nl),
        out_specs=[SEM, SEM] + [IN_HBM] * (ns + nl) + [pl.BlockSpec(memory_space=pltpu.VMEM)],
        out_shape=[pltpu.SemaphoreType.DMA((n_copies,)), pltpu.SemaphoreType.DMA((n_copies,))]
        + [pltpu.HBM(a.shape, a.dtype) for a in arrays] + [jax.ShapeDtypeStruct((SUBLANES, LANES), F32)],
        input_output_aliases={i: 2 + i for i in range(ns + nl)}, **SPLIT_PARAMS,
    )(*[pltpu.with_memory_space_constraint(a, pltpu.HBM) for a in arrays])
    return (outs[:-1], ns), outs[-1]


def _split_wait(name, handle, after, plan):
    (ssem, rsem, *thru), ns = handle
    nl = len(thru) - ns

    def body(*refs):
        ins, lands, (ssem_ref, rsem_ref) = refs[:ns], refs[ns:ns + nl], refs[ns + nl:ns + nl + 2]
        for s, (src, _, dst, dev) in enumerate(plan(ins, lands)):
            cp = _remote(src, dst, ssem_ref.at[s], rsem_ref.at[s], dev)
            cp.wait_send()
            cp.wait_recv()

    outs = _call(
        body, name=name, in_specs=[IN_HBM] * (ns + nl) + [SEM, SEM, ANY], out_specs=[IN_HBM] * (ns + nl),
        out_shape=[pltpu.HBM(t.shape, t.dtype) for t in thru],
        input_output_aliases={i: i for i in range(ns + nl)}, **SPLIT_PARAMS,
    )(*thru, ssem, rsem, after)
    return outs[:ns], outs[ns:]


def _swap_plan(ins, lands):
    x, y, c = _me()
    return [(g_ref.at[k, _half_rows(g_ref.shape[1], 1 - c)], l_ref.at[k], l_ref.at[k], (x, y, 1 - c))
            for g_ref, l_ref in zip(ins, lands) for k in range(N_CHIPS)]


def _swap_start(name, grads):
    lands = [lax.empty((N_CHIPS, g.shape[1] // 2, g.shape[2]), g.dtype) for g in grads]
    return _split_start(name, grads, lands, len(grads) * N_CHIPS, _swap_plan)


def _swap_wait(name, handle, after):
    return _split_wait(name, handle, after, _swap_plan)


def _gather_plan(ins, lands):
    x, y, c = _me()
    k_me = 2 * x + y
    plan = [(w_ref, l_ref.at[k_me], l_ref.at[k_me], (x, y, 1 - c)) for w_ref, l_ref in zip(ins, lands)]
    for px, py in _other_chips(x, y):
        for w_ref, l_ref in zip(ins, lands):
            rows = _half_rows(w_ref.shape[0], c)
            plan.append((w_ref.at[rows], l_ref.at[k_me, rows], l_ref.at[2 * px + py, rows], (px, py, c)))
    return plan


def _gather_start(name, shards):
    lands = [lax.empty((N_CHIPS,) + s.shape, s.dtype) for s in shards]
    return _split_start(name, shards, lands, len(shards) * N_CHIPS, _gather_plan)


def _gather_wait(name, handle, after):
    return _split_wait(name, handle, after, _gather_plan)[1]


def _forward_halves(name, blocks):
    n = len(blocks)
    n_sem = n * (N_CHIPS - 1)

    def body(*refs):
        outs, (ssem, rsem) = refs[n:2 * n], refs[2 * n:]
        x, y, c = _me()
        sib = (x, y, 1 - c)
        chips = _other_chips(x, y)
        sends = []
        for r, (px, py) in enumerate(chips):
            for i, o_ref in enumerate(outs):
                blk = o_ref.at[2 * px + py, _half_rows(o_ref.shape[1], c)]
                cp = _remote(blk, blk, ssem.at[r * n + i], rsem.at[r * n + i], sib)
                cp.start()
                sends.append(cp)
        for r, (px, py) in enumerate(chips):
            for i, o_ref in enumerate(outs):
                blk = o_ref.at[2 * px + py, _half_rows(o_ref.shape[1], 1 - c)]
                _remote(blk, blk, ssem.at[r * n + i], rsem.at[r * n + i], sib).wait_recv()
        for cp in sends:
            cp.wait_send()

    return _pcall(
        body, name=name, in_specs=[ANY] * n, out_specs=[ANY] * n,
        out_shape=[jax.ShapeDtypeStruct(b.shape, b.dtype) for b in blocks],
        input_output_aliases={i: i for i in range(n)},
        scratch_shapes=[pltpu.SemaphoreType.DMA((n_sem,)), pltpu.SemaphoreType.DMA((n_sem,))],
    )(*blocks)


def _scatter_plan(ins, lands):
    x, y, c = _me()
    k_me = 2 * x + y
    return [(p_ref.at[2 * px + py], l_ref.at[k_me], l_ref.at[2 * px + py], (px, py, c))
            for px, py in _other_chips(x, y) for p_ref, l_ref in zip(ins, lands)]


def _scatter_start(name, parts):
    lands = [lax.empty(p.shape, p.dtype) for p in parts]
    return _split_start(name, parts, lands, len(parts) * (N_CHIPS - 1), _scatter_plan)


def _scatter_wait(name, handle, after):
    return _split_wait(name, handle, after, _scatter_plan)[1]


def _share_halves(halves):
    n = len(halves)

    def body(*refs):
        ins, outs, (ssem, rsem) = refs[:n], refs[n:2 * n], refs[2 * n:]
        x, y, c = _me()
        copies = [_remote(r_ref, o_ref, ssem.at[i], rsem.at[i], (x, y, 1 - c))
                  for i, (r_ref, o_ref) in enumerate(zip(ins, outs))]
        for cp in copies:
            cp.start()
        for cp in copies:
            cp.wait()

    return _pcall(
        body, name="share_halves", in_specs=[ANY] * n, out_specs=[ANY] * n,
        out_shape=[jax.ShapeDtypeStruct(h.shape, h.dtype) for h in halves],
        scratch_shapes=[pltpu.SemaphoreType.DMA((n,)), pltpu.SemaphoreType.DMA((n,))],
    )(*halves)


def _exchange_small(v, reduce):
    rows = v.shape[0]

    def body(v_ref, out_ref, buf, ssem, rsem):
        x, y, c = _me()
        me = 4 * x + 2 * y + c
        peers = [((x + bx) % 2, (y + by) % 2, (c + bc) % 2)
                 for bx in (0, 1) for by in (0, 1) for bc in (0, 1) if (bx, by, bc) != (0, 0, 0)]
        dst = buf if reduce else out_ref
        dst[me] = v_ref[...]
        sends = [_remote(v_ref, dst.at[me], ssem.at[r], rsem.at[r], p) for r, p in enumerate(peers)]
        for cp in sends:
            cp.start()
        for r, (px, py, pc) in enumerate(peers):
            blk = dst.at[4 * px + 2 * py + pc]
            _remote(blk, blk, ssem.at[r], rsem.at[r], (px, py, pc)).wait_recv()
        if reduce:
            acc = buf[0]
            for d in range(1, N_DEV):
                acc = acc + buf[d]
            out_ref[...] = acc
        for cp in sends:
            cp.wait_send()

    vm = pl.BlockSpec(memory_space=pltpu.VMEM)
    out_shape = jax.ShapeDtypeStruct((rows, LANES) if reduce else (N_DEV, rows, LANES), F32)
    buf_shape = (N_DEV, rows, LANES) if reduce else (SUBLANES, LANES)
    return _pcall(
        body, pin=False, name="reduce_small" if reduce else "gather_small", in_specs=[vm], out_specs=vm, out_shape=out_shape,
        scratch_shapes=[pltpu.VMEM(buf_shape, F32), pltpu.SemaphoreType.DMA((N_DEV - 1,)),
                        pltpu.SemaphoreType.DMA((N_DEV - 1,))],
        compiler_params=pltpu.CompilerParams(vmem_limit_bytes=32 * 1024 * 1024),
    )(v)


def _add_pair(name, core, g, theirs):
    _, half, cols = theirs.shape
    tr = _tile(half, (256, 176, 128))
    nb = half // tr

    def body(c_ref, g_ref, t_ref, o32_ref, o16_ref):
        s = g_ref[...] + t_ref[...]
        o32_ref[...] = s
        o16_ref[...] = s.astype(BF16)

    spec = pl.BlockSpec((None, tr, cols), lambda k, i, c_ref: (k, i, 0))
    grid_spec = pltpu.PrefetchScalarGridSpec(
        num_scalar_prefetch=1, grid=(N_CHIPS, nb),
        in_specs=[pl.BlockSpec((None, tr, cols), lambda k, i, c_ref: (k, c_ref[0] * nb + i, 0)), spec],
        out_specs=[spec, spec])
    return _pcall(
        body, name=name, grid_spec=grid_spec,
        out_shape=[jax.ShapeDtypeStruct(theirs.shape, F32), jax.ShapeDtypeStruct(theirs.shape, BF16)],
        compiler_params=_params(("arbitrary", "arbitrary"), 8 * _nbytes((tr, cols + LANES), F32)),
    )(core, g, theirs)


def _add_chips(name, chip, p32, recv):
    _, half, cols = p32.shape
    tr = _tile(half, (256, 176, 128))

    def body(k_ref, p_ref, r0_ref, r1_ref, r2_ref, o_ref):
        o_ref[...] = ((p_ref[...] + r0_ref[...].astype(F32)) + r1_ref[...].astype(F32)) + r2_ref[...].astype(F32)

    def other(r):
        return pl.BlockSpec((None, tr, cols), lambda i, k_ref: (r + (k_ref[0] <= r).astype(jnp.int32), i, 0))
    grid_spec = pltpu.PrefetchScalarGridSpec(
        num_scalar_prefetch=1, grid=(half // tr,),
        in_specs=[pl.BlockSpec((None, tr, cols), lambda i, k_ref: (k_ref[0], i, 0)), other(0), other(1), other(2)],
        out_specs=pl.BlockSpec((tr, cols), lambda i, k_ref: (i, 0)))
    return _pcall(
        body, name=name, grid_spec=grid_spec, out_shape=jax.ShapeDtypeStruct((half, cols), F32),
        compiler_params=_params(("arbitrary",), 10 * _nbytes((tr, cols + LANES), F32)),
    )(chip, p32, recv, recv, recv)


def kernel(x, mem, w_in, b_in, hg_lb_logits, hg_norm_w, ml_conv_w, ml_conv_b, ml_norm_w, w_out, ln1_g, ln1_b, ca_wq, ca_wkv, ca_wo, ln2_g, ln2_b, ffn_w_up, ffn_conv_w, ffn_conv_b, ffn_w_down, ln3_g, ln3_b, loss_target, m_w_in, m_b_in, m_hg_lb_logits, m_hg_norm_w, m_ml_conv_w, m_ml_conv_b, m_ml_norm_w, m_w_out, m_ln1_g, m_ln1_b, m_ca_wq, m_ca_wkv, m_ca_wo, m_ln2_g, m_ln2_b, m_ffn_w_up, m_ffn_conv_w, m_ffn_conv_b, m_ffn_w_down, m_ln3_g, m_ln3_b, v_w_in, v_b_in, v_hg_lb_logits, v_hg_norm_w, v_ml_conv_w, v_ml_conv_b, v_ml_norm_w, v_w_out, v_ln1_g, v_ln1_b, v_ca_wq, v_ca_wkv, v_ca_wo, v_ln2_g, v_ln2_b, v_ffn_w_up, v_ffn_conv_w, v_ffn_conv_b, v_ffn_w_down, v_ln3_g, v_ln3_b):
    return _train_step(dict(locals()))


WEIGHTS = ("w_in", "b_in", "hg_lb_logits", "hg_norm_w", "ml_conv_w", "ml_conv_b", "ml_norm_w", "w_out", "ln1_g",
           "ln1_b", "ca_wq", "ca_wkv", "ca_wo", "ln2_g", "ln2_b", "ffn_w_up", "ffn_conv_w", "ffn_conv_b",
           "ffn_w_down", "ln3_g", "ln3_b")
MATRICES = ("w_in", "w_out", "ca_wq", "ca_wkv", "ca_wo", "ffn_w_up", "ffn_w_down")
COL_SHARDED = ("w_in", "ca_wkv", "ffn_w_up", "ml_conv_w", "ffn_conv_w")
SMALL = tuple(n for n in WEIGHTS if n not in MATRICES)
PART_ROWS = 16


def _part_rows(shape, lead):
    n = 1
    for s in shape[lead:]:
        n *= s
    return -(-n // (LANES * PART_ROWS)) * PART_ROWS


def _pack(arrs, dtype, lead=0, rows=None):
    parts = []
    for a in arrs:
        head = a.shape[:lead]
        flat = a.reshape(head + (-1,)).astype(dtype)
        pad = _part_rows(a.shape, lead) * LANES - flat.shape[-1]
        flat = jnp.pad(flat, [(0, 0)] * lead + [(0, pad)])
        parts.append(flat.reshape(head + (-1, LANES)))
    used = sum(p.shape[lead] for p in parts)
    if rows is not None and rows > used:
        parts.append(jnp.zeros(parts[0].shape[:lead] + (rows - used, LANES), dtype))
    return jnp.concatenate(parts, axis=lead)


def _unpack(buf, shapes):
    lead = buf.shape[:-2]
    outs, r = [], 0
    for sh in shapes:
        n = 1
        for s in sh:
            n *= s
        nr = _part_rows(sh, 0)
        flat = buf[..., r:r + nr, :].reshape(lead + (nr * LANES,))
        outs.append(flat[..., :n].reshape(lead + tuple(sh)))
        r += nr
    return outs


def _cat_cols(s):
    return jnp.moveaxis(s, 0, 1).reshape(s.shape[1], -1)


def _stack_rows(s):
    return s.reshape(-1, s.shape[-1])


def _train_step(a):
    xs, mems, tgt = a["x"][0], a["mem"][0], a["loss_target"][0]
    core = lax.axis_index("c").astype(jnp.int32).reshape(1)
    chip = (2 * lax.axis_index("x") + lax.axis_index("y")).astype(jnp.int32).reshape(1)
    k_me = chip[0]
    shard = {n: a[n][0] for n in MATRICES}

    later = [n for n in MATRICES if n != "w_in"]
    taps = _exchange_small(_pack([a["ml_conv_w"][0], a["ffn_conv_w"][0]], F32), reduce=False)
    w = {"w_in": jnp.pad(_cat_cols(_gather_weights([shard["w_in"].astype(BF16)])[0]), ((0, 0), (0, D_IN_PAD - D_IN)))}
    gathering, token = _gather_start("gather_start", [shard[n].astype(BF16) for n in later])
    taps = taps.reshape((N_CHIPS, 2) + taps.shape[1:])[:, 0]
    ml_cw, ffn_cw = [_cat_cols(s) for s in _unpack(taps, [a["ml_conv_w"].shape[1:], a["ffn_conv_w"].shape[1:]])]
    b_in_p = jnp.pad(a["b_in"], ((0, 0), (0, D_IN_PAD - D_IN))) + token[0:1, 0:1]
    mixer_w = (a["hg_lb_logits"], a["hg_norm_w"], ml_cw, a["ml_conv_b"], a["ml_norm_w"])
    up_cols = a["ffn_w_up"].shape[-1]

    xb = xs.astype(BF16)
    proj = _mm("proj", "nn", xb, w["w_in"], bias=b_in_p, tm=256, tn=D_IN_PAD)
    y, hst, cst, nst, mst = _mixer_fwd(proj, *mixer_w)
    w.update(zip(later, _forward_halves("forward_halves", _gather_wait("gather_wait", gathering, y))))
    for n in ("w_out", "ca_wq", "ca_wo", "ffn_w_down"):
        w[n] = _stack_rows(w[n])
    z1, x1, x1b = _mm("mix_out", "nn", y, w["w_out"], res=xs, res_scale=ALPHA, ln=("fwd", a["ln1_g"], a["ln1_b"]),
                      copy_dtype=BF16)
    q = _mm("ca_q", "nn", x1b, w["ca_wq"], out_dtype=BF16, tn=D_MODEL)
    kv = _mm("ca_kv", "nn", mems, w["ca_wkv"])
    o = _attn_fwd(q, kv)
    z2, x2, x2b = _mm("ca_out", "nn", o, w["ca_wo"], res=x1, res_scale=ALPHA, ln=("fwd", a["ln2_g"], a["ln2_b"]),
                      copy_dtype=BF16)
    w_up = w["ffn_w_up"]
    assert w_up.shape == (2 * FFN_J, D_MODEL, FFN_W)
    u, hmid, dz3, g_ln3g, g_ln3b, loss_part, dz3b = _ffn_fwd(
        x2b, x2, w_up, ffn_cw, a["ffn_conv_b"], w["ffn_w_down"], a["ln3_g"], a["ln3_b"], tgt)

    grads = {"ln3_g": g_ln3g, "ln3_b": g_ln3b}
    grads["ffn_w_down"] = _mm("g_w_down", "tn", hmid, dz3b, tm=D_FF // 2, tn=D_MODEL)
    du, g_cw, g_cb, dz2, grads["ln2_g"], grads["ln2_b"], dz2b = _ffn_bwd(
        u, ffn_cw, a["ffn_conv_b"], dz3b, dz3, w["ffn_w_down"], w_up, z2, a["ln2_g"], a["ln2_b"])
    grads["ffn_conv_w"] = jnp.transpose(g_cw, (2, 1, 0, 3)).reshape(FFN_CONV, 2 * D_FF)
    grads["ffn_conv_b"] = jnp.transpose(g_cb, (2, 1, 0, 3)).reshape(1, 2 * D_FF)
    grads["ffn_w_up"] = _mm("g_w_up", "tn", x2b, du, out_groups=N_CHIPS, tm=D_MODEL, tn=up_cols)
    grads["ffn_w_down"] = grads["ffn_w_down"].reshape((N_CHIPS,) + shard["ffn_w_down"].shape)
    pending = {}

    def reduce_start(tag, names, swapped=None):
        group = [grads[n] for n in names]
        group, theirs = swapped or (group, _swap_halves("swap_halves_" + tag, group))
        sums = [_add_pair("add_pair_" + n, core, g, t) for n, g, t in zip(names, group, theirs)]
        handle, token = _scatter_start("scatter_start_" + tag, [s16 for _, s16 in sums])
        pending[tag] = (names, [s32 for s32, _ in sums], handle)
        return token[0:1, 0:1]

    ffn = ("ffn_w_up", "ffn_w_down")
    swapping, token = _swap_start("swap_start_ffn", [grads[n] for n in ffn])
    do = _mm("d_o", "nt", dz2b, w["ca_wo"], bias=jnp.zeros((1, D_MODEL), F32) + token[0:1, 0:1], out_dtype=BF16,
             tn=D_MODEL)
    grads["ca_wo"] = _mm("g_wo", "tn", o, dz2b, tm=D_MODEL, tn=D_MODEL)
    zero = reduce_start("ffn", ffn, _swap_wait("swap_wait_ffn", swapping, grads["ca_wo"]))
    dq, dkv = _attn_bwd(q, kv + zero, do)
    grads["ca_wq"] = _mm("g_wq", "tn", x1b, dq, tm=D_MODEL, tn=D_MODEL)
    grads["ca_wkv"] = _mm("g_wkv", "tn", mems, dkv, out_groups=N_CHIPS, tm=D_MODEL)
    dz1, grads["ln1_g"], grads["ln1_b"], dz1b = _mm("d_x1", "nt", dq, w["ca_wq"], res=dz2, res_scale=ALPHA,
                                                    ln=("bwd", z1, a["ln1_g"], a["ln1_b"]), copy_dtype=BF16)
    grads["w_out"] = _mm("g_w_out", "tn", y, dz1b, tm=D_MODEL, tn=D_MODEL)
    for n in ("w_out", "ca_wq", "ca_wo"):
        grads[n] = grads[n].reshape((N_CHIPS,) + shard[n].shape)
    attn = ("w_out", "ca_wq", "ca_wkv", "ca_wo")
    swapping, token = _swap_start("swap_start_attn", [grads[n] for n in attn])
    dy = _mm("d_y", "nt", dz1b, w["w_out"], bias=jnp.zeros((1, D_MODEL), F32) + token[0:1, 0:1], tn=D_MODEL)
    zero = reduce_start("attn", attn, _swap_wait("swap_wait_attn", swapping, dy))
    (dproj, g_b_in, grads["hg_lb_logits"], grads["hg_norm_w"], grads["ml_conv_w"], grads["ml_conv_b"],
     grads["ml_norm_w"]) = _mixer_bwd(proj, dy, hst, cst, nst, mst, mixer_w[0], mixer_w[1] + zero, *mixer_w[2:])
    g_in = _mm("g_w_in", "tn", xb, dproj, tm=D_MODEL, tn=up_cols)[:, :D_IN]
    grads["w_in"] = jnp.moveaxis(g_in.reshape(D_MODEL, N_CHIPS, -1), 1, 0)
    grads["b_in"] = g_b_in[:, :D_IN]
    zero = reduce_start("in", ("w_in",))
    dx = _mm("d_x", "nt", dproj, w["w_in"], bias=jnp.zeros((1, D_MODEL), F32) + zero, res=dz1, res_scale=ALPHA,
             tm=256, tn=D_MODEL)

    halves = {}
    for tag, (names, sums32, handle) in pending.items():
        for n, s32, r in zip(names, sums32, _scatter_wait("scatter_wait_" + tag, handle, dx)):
            halves[n] = _add_chips("add_chips_" + n, chip, s32, r)
    halves = [halves[n] for n in MATRICES]
    other_halves = _share_halves(halves)

    small_shapes = [grads[n].shape for n in SMALL] + [loss_part.shape]
    summed = _unpack(_exchange_small(_pack([grads[n] for n in SMALL] + [loss_part], F32), reduce=True), small_shapes)
    loss = summed[-1][0, 0]
    for n, g in zip(SMALL, summed[:-1]):
        if n in COL_SHARDED:
            cols = a[n].shape[-1]
            g = lax.dynamic_slice_in_dim(g, k_me * cols, cols, axis=1)
        grads[n] = g

    delta, new_m, new_v = {}, {}, {}
    for n, mine, theirs in zip(MATRICES, halves, other_halves):
        grads[n], delta[n], new_m[n], new_v[n] = _adamw_halves(
            "adamw_" + n, core, shard[n], mine, theirs, a["m_" + n][0], a["v_" + n][0])
    small_w = [a[n][0] if a[n].ndim == 3 else a[n] for n in SMALL]
    small_m = [a["m_" + n][0] if a[n].ndim == 3 else a["m_" + n] for n in SMALL]
    small_v = [a["v_" + n][0] if a[n].ndim == 3 else a["v_" + n] for n in SMALL]
    shapes = [w.shape for w in small_w]
    packed = [_pack(l, F32) for l in (small_w, [grads[n] for n in SMALL], small_m, small_v)]
    for out, buf in zip((delta, new_m, new_v), _adamw("adamw_small", *packed)):
        for n, v in zip(SMALL, _unpack(buf, shapes)):
            out[n] = v

    def shaped(d):
        return [d[n].reshape(a[n].shape) for n in WEIGHTS]
    return (loss, dx[None], *shaped(grads), *shaped(delta), *shaped(new_m), *shaped(new_v))
```

```python
import functools

import jax
import jax.numpy as jnp
from jax import lax
from jax.experimental import pallas as pl
from jax.experimental.pallas import tpu as pltpu

F32 = jnp.float32
BF16 = jnp.bfloat16

D_MODEL = 1024
HEADS = 4
DK = 128
D_GRP = HEADS * DK
CHUNK = 64
ML_CONV = 4
FFN_CONV = 3
D_FF = 2816
CA_DH = D_MODEL // HEADS
DEPTH = 1
ALPHA = (2.0 * DEPTH) ** 0.25
LN_EPS = 1e-5
NEG_BIG = -1e30
D_IN = 8 * D_GRP + 2 * HEADS
D_IN_PAD = 8 * D_GRP + 128
ADAM_LR, ADAM_B1, ADAM_B2, ADAM_EPS, ADAM_WD, ADAM_STEP = 0.001, 0.9, 0.999, 1e-08, 0.01, 10

SUBLANES = 8
LANES = 128
VMEM_BYTES = 64 * 1024 * 1024


def _pcall(body, pin=True, **kw):
    if not pin:
        return _call(body, **kw)
    kw["out_shape"] = jax.tree.map(lambda s: pltpu.HBM(s.shape, s.dtype), kw["out_shape"])
    call = _call(body, **kw)

    def pinned(*args):
        return call(*[pltpu.with_memory_space_constraint(x, pltpu.HBM) if jnp.issubdtype(x.dtype, jnp.floating) else x
                      for x in args])
    return pinned


def _call(body, **kw):
    return pl.pallas_call(body, **kw)


def _params(semantics, vmem_bytes):
    limit = int(min(max(2 * vmem_bytes, 16 * 1024 * 1024), VMEM_BYTES - 8 * 1024 * 1024))
    return pltpu.CompilerParams(dimension_semantics=semantics, vmem_limit_bytes=limit)


def _nbytes(shape, dtype):
    n = 1
    for s in shape:
        n *= s
    return n * jnp.dtype(dtype).itemsize


def _dg(a, b, ca, cb):
    return lax.dot_general(a.astype(BF16), b.astype(BF16), (((ca,), (cb,)), ((), ())),
                           preferred_element_type=F32)


@jax.custom_vjp
def mm_nn(a, b):
    return _dg(a, b, 1, 0)


mm_nn.defvjp(lambda a, b: (_dg(a, b, 1, 0), (a, b)),
             lambda r, g: (_dg(g, r[1], 1, 1).astype(r[0].dtype), _dg(r[0], g, 0, 0).astype(r[1].dtype)))


@jax.custom_vjp
def mm_nt(a, b):
    return _dg(a, b, 1, 1)


mm_nt.defvjp(lambda a, b: (_dg(a, b, 1, 1), (a, b)),
             lambda r, g: (_dg(g, r[1], 1, 0).astype(r[0].dtype), _dg(g, r[0], 0, 0).astype(r[1].dtype)))


@jax.custom_vjp
def mm_tn(a, b):
    return _dg(a, b, 0, 0)


mm_tn.defvjp(lambda a, b: (_dg(a, b, 0, 0), (a, b)),
             lambda r, g: (_dg(r[1], g, 1, 1).astype(r[0].dtype), _dg(r[0], g, 1, 0).astype(r[1].dtype)))


def _tri(n, lower):
    r = lax.broadcasted_iota(jnp.int32, (n, n), 0)
    c = lax.broadcasted_iota(jnp.int32, (n, n), 1)
    return ((r >= c) if lower else (r <= c)).astype(F32)


def _tri_dot(lower, x):
    t = _tri(x.shape[0], lower).astype(BF16)
    hi = x.astype(BF16)
    rest = x - hi.astype(F32)
    mid = rest.astype(BF16)
    lo = (rest - mid.astype(F32)).astype(BF16)
    return sum(lax.dot_general(t, p, (((1,), (0,)), ((), ())), preferred_element_type=F32) for p in (hi, mid, lo))


@jax.custom_vjp
def cumsum_rows(x):
    return _tri_dot(True, x)


cumsum_rows.defvjp(lambda x: (_tri_dot(True, x), None), lambda _, g: (_tri_dot(False, g),))


def _shift_impl(halo, x, d):
    xx = jnp.concatenate([halo, x], axis=0)
    return pltpu.roll(xx, d, 0)[SUBLANES:]


@functools.partial(jax.custom_vjp, nondiff_argnums=(2,))
def shift_rows(halo, x, d):
    return _shift_impl(halo, x, d)


def _shift_bwd(d, _, g):
    n = g.shape[0] + SUBLANES
    gg = jnp.concatenate([jnp.zeros((SUBLANES, g.shape[1]), g.dtype), g], axis=0)
    r = pltpu.roll(gg, n - d, 0)
    return r[:SUBLANES], r[SUBLANES:]


shift_rows.defvjp(lambda halo, x, d: (_shift_impl(halo, x, d), None), _shift_bwd)


def causal_conv(halo, x, w_rows, b):
    k = len(w_rows)
    y = b + w_rows[k - 1] * x
    for d in range(1, k):
        y = y + w_rows[k - 1 - d] * shift_rows(halo, x, d)
    return y


def _sigmoid(x):
    return 1.0 / (1.0 + jnp.exp(-x))


def _silu(x):
    return x * _sigmoid(x)


def _log_sigmoid(x):
    return jnp.minimum(x, 0.0) - jnp.log(1.0 + jnp.exp(-jnp.abs(x)))


def _pick_row(x, i):
    row = lax.broadcasted_iota(jnp.int32, (x.shape[0], 1), 0)
    return jnp.sum(jnp.where(row == i, x, 0.0), axis=0, keepdims=True)


def _layer_norm(z, g, b):
    mu = jnp.mean(z, axis=-1, keepdims=True)
    zc = z - mu
    var = jnp.mean(zc * zc, axis=-1, keepdims=True)
    return zc * lax.rsqrt(var + LN_EPS) * g + b


def _qk_conv(halo, x, w0, w1, w2, w3, b):
    return _silu(causal_conv(halo, x, (w0, w1, w2, w3), b))


def _grp(i, h=None):
    if h is None:
        return pl.ds(i * D_GRP, D_GRP)
    return pl.ds(i * D_GRP + h * DK, DK)


def _mixer_specs(n_chunks, reverse):
    def chunk(c):
        return n_chunks - 1 - c if reverse else c
    row8 = CHUNK // SUBLANES
    proj_spec = pl.BlockSpec((CHUNK, D_IN_PAD), lambda c: (chunk(c), 0))
    halo_spec = pl.BlockSpec((SUBLANES, 2 * D_GRP), lambda c: (jnp.maximum(chunk(c) * row8 - 1, 0), 2))
    small = [pl.BlockSpec((2, D_GRP), lambda c: (0, 0)), pl.BlockSpec((1, D_GRP), lambda c: (0, 0)),
             pl.BlockSpec((ML_CONV, 2 * D_GRP), lambda c: (0, 0)), pl.BlockSpec((1, 2 * D_GRP), lambda c: (0, 0)),
             pl.BlockSpec((1, D_GRP), lambda c: (0, 0))]
    state_specs = [pl.BlockSpec((1, HEADS, DK, DK), lambda c: (chunk(c), 0, 0, 0)),
                   pl.BlockSpec((1, HEADS, DK, DK), lambda c: (chunk(c), 0, 0, 0)),
                   pl.BlockSpec((1, HEADS, 1, DK), lambda c: (chunk(c), 0, 0, 0)),
                   pl.BlockSpec((1, HEADS, 1, DK), lambda c: (chunk(c), 0, 0, 0))]
    y_spec = pl.BlockSpec((CHUNK, 2 * D_GRP), lambda c: (chunk(c), 0))
    return proj_spec, halo_spec, small, state_specs, y_spec, chunk


def _heads(x):
    return [x[:, h * DK:(h + 1) * DK] for h in range(HEADS)]


def _last(x, j):
    lane = lax.broadcasted_iota(jnp.int32, (1, x.shape[-1]), 1)
    return jnp.sum(jnp.where(lane == j, x, 0.0), axis=-1, keepdims=True)


def _hg_chunk(st_t, hq, hf, hi, hgate, l0, l1, nw):
    n = hq.shape[0]
    lb = _sigmoid(l0 - l1)
    q = _silu(hq)
    lf = jnp.log(lb + (1.0 - lb) * _sigmoid(hf))
    k = (1.0 - lb) * _sigmoid(-hf)
    b = cumsum_rows(lf)
    b_ref = _pick_row(b, n // 2 - 1)
    b_last = _pick_row(b, n - 1)
    qa, ka =_heads(q * jnp.exp(b - b_ref)), _heads(k * jnp.exp(b_ref - b))
    qe, kd, eb, v = _heads(q * jnp.exp(b)), _heads(k * jnp.exp(b_last - b)), _heads(jnp.exp(b_last)), _heads(hi)
    tri = _tri(n, True) > 0
    attn = [jnp.where(tri, mm_nt(qa[h], ka[h]), 0.0) for h in range(HEADS)]
    o = [mm_nn(attn[h], v[h]) + mm_nt(qe[h], st_t[h]) for h in range(HEADS)]
    st_new = jnp.stack([eb[h] * st_t[h] + mm_tn(v[h], kd[h]) for h in range(HEADS)])
    yn = [o[h] * lax.rsqrt(jnp.mean(o[h] * o[h], axis=-1, keepdims=True) + LN_EPS) for h in range(HEADS)]
    return st_new, jnp.concatenate(yn, axis=1) * nw * _silu(hgate)


def _ml_chunk(c_st, n_st, m_st, q, k, v, gates, og, nw):
    n = q.shape[0]
    ig = jnp.stack([_last(gates, h) for h in range(HEADS)])
    log_f = _log_sigmoid(gates)
    fl = jnp.stack([_last(log_f, HEADS + h) for h in range(HEADS)])
    bw = cumsum_rows(jnp.concatenate([jnp.broadcast_to(fl[h], (n, DK)) for h in range(HEADS)], axis=1))
    b = jnp.stack([_last(x, 0) for x in _heads(bw)])
    g = jnp.sum(fl, axis=1, keepdims=True)
    eye = lax.broadcasted_iota(jnp.int32, (n, n), 0) == lax.broadcasted_iota(jnp.int32, (n, n), 1)
    e_row = jnp.sum(jnp.where(eye, ig - b, 0.0), axis=1, keepdims=True)
    d = jnp.where(_tri(n, True) > 0, b + e_row, -jnp.inf)
    inter = b + m_st
    m_t = jnp.maximum(inter, jnp.max(d, axis=2, keepdims=True))
    qs, kh, vh = _heads(q * (DK ** -0.5)), _heads(k), _heads(v)
    s = jnp.stack([mm_nt(qs[h], kh[h]) for h in range(HEADS)]) * jnp.exp(d - m_t)
    w_inter = jnp.exp(inter - m_t)
    num = (jnp.stack([mm_nn(s[h], vh[h]) for h in range(HEADS)])
           + w_inter * jnp.stack([mm_nn(qs[h], c_st[h]) for h in range(HEADS)]))
    den = jnp.sum(s, axis=2, keepdims=True) + w_inter * jnp.sum(jnp.stack(qs) * n_st, axis=2, keepdims=True)
    h_out = num / jnp.maximum(jnp.abs(den), jnp.exp(-m_t))
    a = g - b + ig
    m_new = jnp.maximum(g + m_st, jnp.max(a, axis=1, keepdims=True))
    decay = jnp.exp(g + m_st - m_new)
    wk = jnp.stack(kh) * jnp.exp(a - m_new)
    c_new = decay * c_st + jnp.stack([mm_tn(wk[h], vh[h]) for h in range(HEADS)])
    n_new = decay * n_st + jnp.sum(wk, axis=1, keepdims=True)
    hc = h_out - jnp.mean(h_out, axis=-1, keepdims=True)
    yn = hc * lax.rsqrt(jnp.mean(hc * hc, axis=-1, keepdims=True) + LN_EPS)
    y = _sigmoid(og) * (jnp.concatenate([yn[h] for h in range(HEADS)], axis=1) * nw)
    return c_new, n_new, m_new, y


def _mixer_inputs(proj_ref, lg_ref, hnw_ref, mnw_ref, qk):
    hg_in = (proj_ref[:, _grp(0)], proj_ref[:, _grp(1)], proj_ref[:, _grp(2)], proj_ref[:, _grp(3)],
             lg_ref[0:1, :], lg_ref[1:2, :], hnw_ref[...])
    ml_in = (qk[:, :D_GRP], qk[:, D_GRP:], proj_ref[:, _grp(6)], proj_ref[:, pl.ds(8 * D_GRP, LANES)],
             proj_ref[:, _grp(7)], mnw_ref[...])
    return hg_in, ml_in


def _mixer_fwd(proj, lb_logits, hg_nw, conv_w, conv_b, ml_nw):
    seq = proj.shape[0]
    n_chunks = seq // CHUNK
    proj_spec, halo_spec, small, state_specs, y_spec, _ = _mixer_specs(n_chunks, False)

    def body(proj_ref, halo_ref, lg_ref, hnw_ref, cw_ref, cb_ref, mnw_ref,
             y_ref, hst_ref, cst_ref, nst_ref, mst_ref, hs, cs, ns, ms):
        c = pl.program_id(0)

        @pl.when(c == 0)
        def _():
            hs[...] = jnp.zeros_like(hs)
            cs[...] = jnp.zeros_like(cs)
            ns[...] = jnp.zeros_like(ns)
            ms[...] = jnp.full(ms.shape, NEG_BIG, F32)

        hst_ref[0] = hs[...]
        cst_ref[0] = cs[...]
        nst_ref[0] = ns[...]
        mst_ref[0] = ms[...]
        halo = jnp.where(c > 0, halo_ref[...], 0.0)
        qk = _qk_conv(halo, proj_ref[:, pl.ds(4 * D_GRP, 2 * D_GRP)],
                      cw_ref[0:1, :], cw_ref[1:2, :], cw_ref[2:3, :], cw_ref[3:4, :], cb_ref[...])
        hg_in, ml_in = _mixer_inputs(proj_ref, lg_ref, hnw_ref, mnw_ref, qk)
        hs[...], y_hg = _hg_chunk(hs[...], *hg_in)
        cs[...], ns[...], m_new, y_ml = _ml_chunk(cs[...], ns[...], _last(ms[...], 0), *ml_in)
        ms[...] = jnp.broadcast_to(m_new, ms.shape)
        y_ref[:, pl.ds(0, D_GRP)] = y_hg.astype(BF16)
        y_ref[:, pl.ds(D_GRP, D_GRP)] = y_ml.astype(BF16)

    st = jax.ShapeDtypeStruct((n_chunks, HEADS, DK, DK), F32)
    vec = jax.ShapeDtypeStruct((n_chunks, HEADS, 1, DK), F32)
    vmem = 2 * (_nbytes((CHUNK, D_IN_PAD), F32) + _nbytes((CHUNK, 2 * D_GRP), F32) + 2 * _nbytes((HEADS, DK, DK), F32)) \
        + 2 * _nbytes((HEADS, DK, DK), F32)
    return _pcall(
        body, name="mixer_fwd", grid=(n_chunks,),
        in_specs=[proj_spec, halo_spec] + small,
        out_specs=[y_spec] + state_specs,
        out_shape=[jax.ShapeDtypeStruct((seq, 2 * D_GRP), BF16), st, st, vec, vec],
        scratch_shapes=[pltpu.VMEM((HEADS, DK, DK), F32), pltpu.VMEM((HEADS, DK, DK), F32),
                        pltpu.VMEM((HEADS, 1, DK), F32), pltpu.VMEM((HEADS, 1, DK), F32)],
        compiler_params=_params(("arbitrary",), vmem),
    )(proj, proj, lb_logits, hg_nw, conv_w, conv_b, ml_nw)


def _mixer_bwd(proj, dy, hst, cst, nst, mst, lb_logits, hg_nw, conv_w, conv_b, ml_nw):
    seq = proj.shape[0]
    n_chunks = seq // CHUNK
    proj_spec, halo_spec, small, state_specs, y_spec, _ = _mixer_specs(n_chunks, True)

    def body(proj_ref, halo_ref, dy_ref, hst_ref, cst_ref, nst_ref, mst_ref,
             lg_ref, hnw_ref, cw_ref, cb_ref, mnw_ref,
             dproj_ref, dbin_ref, dlg_ref, dhnw_ref, dcw_ref, dcb_ref, dmnw_ref,
             dhs, dcs, dns, dms, dhalo):
        c = pl.program_id(0)

        @pl.when(c == 0)
        def _():
            for r in (dhs, dcs, dns, dms, dhalo, dbin_ref, dlg_ref, dhnw_ref, dcw_ref, dcb_ref, dmnw_ref):
                r[...] = jnp.zeros_like(r)

        def put(cols, val):
            dproj_ref[:, cols] = val.astype(BF16)
            dbin_ref[:, cols] += jnp.sum(val, axis=0, keepdims=True)

        first = c == n_chunks - 1
        halo = jnp.where(first, 0.0, halo_ref[...])
        x_qk = proj_ref[:, pl.ds(4 * D_GRP, 2 * D_GRP)]
        conv_args = (halo, x_qk, cw_ref[0:1, :], cw_ref[1:2, :], cw_ref[2:3, :], cw_ref[3:4, :], cb_ref[...])
        qk, conv_vjp = jax.vjp(_qk_conv, *conv_args)
        hg_in, ml_in = _mixer_inputs(proj_ref, lg_ref, hnw_ref, mnw_ref, qk)
        _, hg_vjp = jax.vjp(_hg_chunk, hst_ref[0], *hg_in)
        _, ml_vjp = jax.vjp(_ml_chunk, cst_ref[0], nst_ref[0], _last(mst_ref[0], 0), *ml_in)
        dst, dhq, dhf, dhi, dhg, dl0, dl1, dnw = hg_vjp((dhs[...], dy_ref[:, pl.ds(0, D_GRP)]))
        dc, dn, dm, dq, dk, dv, dgates, dog, dmn = ml_vjp(
            (dcs[...], dns[...], _last(dms[...], 0), dy_ref[:, pl.ds(D_GRP, D_GRP)]))
        dhs[...] = dst
        dcs[...] = dc
        dns[...] = dn
        dms[...] = jnp.broadcast_to(dm, dms.shape)
        for i, val in ((0, dhq), (1, dhf), (2, dhi), (3, dhg), (6, dv), (7, dog)):
            put(_grp(i), val)
        put(pl.ds(8 * D_GRP, LANES), dgates)
        dlg_ref[0:1, :] += dl0
        dlg_ref[1:2, :] += dl1
        dhnw_ref[...] += dnw
        dmnw_ref[...] += dmn
        dh, dx, dw0, dw1, dw2, dw3, db = conv_vjp(jnp.concatenate([dq, dk], axis=1))
        tail = jnp.concatenate([jnp.zeros((CHUNK - SUBLANES, 2 * D_GRP), F32), dhalo[...]], axis=0)
        put(pl.ds(4 * D_GRP, 2 * D_GRP), dx + tail)
        dhalo[...] = dh
        for d, dw in enumerate((dw0, dw1, dw2, dw3)):
            dcw_ref[d:d + 1, :] += dw
        dcb_ref[...] += db

    row = pl.BlockSpec((1, D_GRP), lambda c: (0, 0))
    small_out = [pl.BlockSpec((1, D_IN_PAD), lambda c: (0, 0)), pl.BlockSpec((2, D_GRP), lambda c: (0, 0)), row,
                 pl.BlockSpec((ML_CONV, 2 * D_GRP), lambda c: (0, 0)), pl.BlockSpec((1, 2 * D_GRP), lambda c: (0, 0)), row]
    dy_spec = pl.BlockSpec((CHUNK, 2 * D_GRP), y_spec.index_map)
    vmem = 2 * (2 * _nbytes((CHUNK, D_IN_PAD), F32) + _nbytes((CHUNK, 2 * D_GRP), F32)
                + 2 * _nbytes((HEADS, DK, DK), F32)) + 2 * _nbytes((HEADS, DK, DK), F32) + 4 * 1024 * 1024
    return _pcall(
        body, name="mixer_bwd", grid=(n_chunks,),
        in_specs=[proj_spec, halo_spec, dy_spec] + state_specs + small,
        out_specs=[proj_spec] + small_out,
        out_shape=[jax.ShapeDtypeStruct((seq, D_IN_PAD), BF16), jax.ShapeDtypeStruct((1, D_IN_PAD), F32),
                   jax.ShapeDtypeStruct((2, D_GRP), F32), jax.ShapeDtypeStruct((1, D_GRP), F32),
                   jax.ShapeDtypeStruct((ML_CONV, 2 * D_GRP), F32), jax.ShapeDtypeStruct((1, 2 * D_GRP), F32),
                   jax.ShapeDtypeStruct((1, D_GRP), F32)],
        scratch_shapes=[pltpu.VMEM((HEADS, DK, DK), F32), pltpu.VMEM((HEADS, DK, DK), F32),
                        pltpu.VMEM((HEADS, 1, DK), F32), pltpu.VMEM((HEADS, 1, DK), F32),
                        pltpu.VMEM((SUBLANES, 2 * D_GRP), F32)],
        compiler_params=_params(("arbitrary",), vmem),
    )(proj, proj, dy, hst, cst, nst, mst, lb_logits, hg_nw, conv_w, conv_b, ml_nw)


def _tile(n, prefs, unit=None):
    unit = unit or n
    for p in prefs:
        if unit % p == 0 and n % p == 0:
            return p
    return unit


def _logical(arr):
    return arr.shape if arr.ndim == 2 else (arr.shape[1], arr.shape[0] * arr.shape[2])


def _group(arr):
    return arr.shape[-1]


def _split_spec(ndim, group, tr, tc, where):
    if ndim == 2:
        return pl.BlockSpec((tr, tc), where)
    per = group // tc
    assert per * tc == group, (group, tc)

    def index(*ids):
        bi, bj = where(*ids)
        return (bj // per, bi, bj % per)
    return pl.BlockSpec((None, tr, tc), index)


def _mm(name, mode, a, b, *, bias=None, res=None, res_scale=1.0, ln=None, out_dtype=F32, out_groups=None,
        copy_dtype=None, tm=None, tn=None, tk=None):
    la, lb = _logical(a), _logical(b)
    if mode == "nn":
        (m, k), n = la, lb[1]
        n_unit = _group(b) if b.ndim == 3 else n
        kc = _group(a) if a.ndim == 3 else k
    elif mode == "nt":
        (m, k), n = la, lb[0]
        n_unit = n
        kc = min(_group(a) if a.ndim == 3 else k, _group(b) if b.ndim == 3 else k)
    else:
        (k, m), n = la, lb[1]
        n_unit, kc = (_group(b) if b.ndim == 3 else n), k
        assert a.ndim == 2
    if out_groups:
        n_unit = min(n_unit, n // out_groups)
    kind = ln[0] if ln else None
    tm = tm or (256 if ln else _tile(m, (512, 256, 128)))
    tn = n if ln else (tn or _tile(n, (512, 384, 256, 128), n_unit))
    tk = (tk or _tile(k, (2048, 512, 256, 128))) if mode == "tn" else k
    gi, gj, gk = m // tm, n // tn, k // tk
    assert gi * tm == m and gj * tn == n and gk * tk == k and n_unit % tn == 0, (name, m, n, k, tm, tn, tk)
    ca, cb = {"nn": (1, 0), "nt": (1, 1), "tn": (0, 0)}[mode]
    i_outer = gk > 1 or (gi - 1) * _nbytes(b.shape, b.dtype) <= (gj - 1) * _nbytes(a.shape, a.dtype)

    def ij(where):
        return (lambda p, q, kk: where(p, q, kk)) if i_outer else (lambda p, q, kk: where(q, p, kk))
    if mode == "tn":
        a_spec = pl.BlockSpec((tk, tm), ij(lambda i, j, kk: (kk, i)))
    elif a.ndim == 3:
        a_spec = pl.BlockSpec((a.shape[0], tm, _group(a)), ij(lambda i, j, kk: (0, i, 0)))
    else:
        a_spec = pl.BlockSpec((tm, k), ij(lambda i, j, kk: (i, 0)))
    if mode != "nt":
        b_spec = _split_spec(b.ndim, _group(b), tk, tn, ij(lambda i, j, kk: (kk, j)))
    elif b.ndim == 3:
        b_spec = pl.BlockSpec((b.shape[0], tn, _group(b)), ij(lambda i, j, kk: (0, j, 0)))
    else:
        b_spec = pl.BlockSpec((tn, k), ij(lambda i, j, kk: (j, 0)))
    row_spec = pl.BlockSpec((1, tn), ij(lambda i, j, kk: (0, j)))
    blk_spec = pl.BlockSpec((tm, tn), ij(lambda i, j, kk: (i, j)))
    ins, in_specs = [a, b], [a_spec, b_spec]
    if bias is not None:
        ins.append(bias), in_specs.append(row_spec)
    if res is not None:
        ins.append(res), in_specs.append(blk_spec)
    if kind == "fwd":
        ins += [ln[1], ln[2]]
        in_specs += [row_spec, row_spec]
    elif kind == "loss":
        ins += [ln[1], ln[2], ln[3]]
        in_specs += [row_spec, row_spec, blk_spec]
    elif kind == "bwd":
        ins += [ln[1], ln[2], ln[3]]
        in_specs += [blk_spec, row_spec, row_spec]
    if out_groups:
        blk_out = jax.ShapeDtypeStruct((out_groups, m, n // out_groups), out_dtype)
        out_spec = _split_spec(3, n // out_groups, tm, tn, ij(lambda i, j, kk: (i, j)))
    else:
        blk_out, out_spec = jax.ShapeDtypeStruct((m, n), out_dtype), blk_spec
    row_out = jax.ShapeDtypeStruct((1, n), F32)
    if kind is None:
        out_shape, out_specs = [blk_out], [out_spec]
    elif kind == "fwd":
        out_shape, out_specs = [blk_out, blk_out], [blk_spec, blk_spec]
    else:
        out_shape, out_specs = [blk_out, row_out, row_out], [blk_spec, row_spec, row_spec]
        if kind == "loss":
            out_shape.append(jax.ShapeDtypeStruct((1, LANES), F32))
            out_specs.append(pl.BlockSpec((1, LANES), lambda p, q, kk: (0, 0)))
    if copy_dtype is not None:
        out_shape.append(jax.ShapeDtypeStruct((m, n), copy_dtype))
        out_specs.append(blk_spec)
    n_in = len(ins)

    def body(*refs):
        in_refs, out_refs, acc_ref = refs[:n_in], refs[n_in:n_in + len(out_shape)], refs[-1]
        i, kk = pl.program_id(0 if i_outer else 1), pl.program_id(2)
        a_ref, b_ref = in_refs[:2]
        extra = list(in_refs[2:])

        def epilogue(acc, rows=slice(None)):
            rest = list(extra)
            if bias is not None:
                acc = acc + rest.pop(0)[...]
            if res is not None:
                acc = acc + res_scale * rest.pop(0)[rows, :]
            if kind is None:
                out_refs[0][...] = acc.astype(out_dtype)
                return
            if kind == "fwd":
                out_refs[0][rows, :] = acc
                y = _layer_norm(acc, rest[0][...], rest[1][...])
                out_refs[1][rows, :] = y
                if copy_dtype is not None:
                    out_refs[-1][rows, :] = y.astype(copy_dtype)
                return
            if kind == "loss":
                y, vjp = jax.vjp(_layer_norm, acc, rest[0][...], rest[1][...])
                err = y - rest[2][rows, :]
                part = 0.5 * jnp.sum(jnp.sum(err * err, axis=1, keepdims=True), axis=0, keepdims=True) / n
                dz, dg, db = vjp(err / n)
            else:
                _, vjp = jax.vjp(_layer_norm, rest[0][rows, :], rest[1][...], rest[2][...])
                dz, dg, db = vjp(acc)
            out_refs[0][rows, :] = dz
            out_refs[1][...] += dg
            out_refs[2][...] += db
            if kind == "loss":
                out_refs[3][...] += jnp.broadcast_to(part, (1, LANES))
            if copy_dtype is not None:
                out_refs[-1][rows, :] = dz.astype(copy_dtype)

        if kind in ("loss", "bwd"):
            @pl.when((i == 0) & (kk == 0))
            def _():
                for r in out_refs[1:3 + (kind == "loss")]:
                    r[...] = jnp.zeros_like(r)

        def chunk(ref, c0, last):
            if ref.ndim == 3:
                g = ref.shape[2]
                return ref[c0 // g, :, pl.ds(c0 % g, kc)]
            return ref[:, pl.ds(c0, kc)] if last else ref[pl.ds(c0, kc), :]

        if mode == "tn" or kc == k:
            prod = _dg(a_ref[...], b_ref[...], ca, cb)
        else:
            prod = None
            for c0 in range(0, k, kc):
                part = _dg(chunk(a_ref, c0, True), chunk(b_ref, c0, mode == "nt"), ca, cb)
                prod = part if prod is None else prod + part
        if gk == 1:
            epilogue(prod)
            return

        @pl.when(kk == 0)
        def _():
            acc_ref[...] = prod

        @pl.when(kk > 0)
        def _():
            acc_ref[...] += prod

        @pl.when(kk == gk - 1)
        def _():
            epilogue(acc_ref[...])

    vmem = (2 * (_nbytes((tm, tk), a.dtype) + _nbytes((tk, tn), b.dtype))
            + (2 * len(ins) + 2 * len(out_shape) + 1) * _nbytes((tm, tn), F32))
    outs = _pcall(
        body, name=name, grid=(gi, gj, gk) if i_outer else (gj, gi, gk), in_specs=in_specs, out_specs=out_specs,
        out_shape=out_shape, scratch_shapes=[pltpu.VMEM((tm, tn) if gk > 1 else (SUBLANES, LANES), F32)],
        compiler_params=_params(("arbitrary", "arbitrary", "arbitrary"), vmem),
    )(*ins)
    return outs[0] if (kind is None and copy_dtype is None) else outs


def _attn_head(q, k, v):
    sc = mm_nt(q, k) * (CA_DH ** -0.5)
    e = jnp.exp(sc - jnp.max(sc, axis=-1, keepdims=True))
    return mm_nn(e / jnp.sum(e, axis=-1, keepdims=True), v)


def _attn_fwd(q, kv):
    seq, n_mem = q.shape[0], kv.shape[0]
    tq = _tile(seq, (512, 256, 128))

    def body(q_ref, kv_ref, o_ref):
        for h in range(HEADS):
            hd = pl.ds(h * CA_DH, CA_DH)
            o = _attn_head(q_ref[:, hd], kv_ref[:, hd], kv_ref[:, pl.ds(D_MODEL + h * CA_DH, CA_DH)])
            o_ref[:, hd] = o.astype(BF16)

    return _pcall(
        body, name="attn_fwd", grid=(seq // tq,),
        in_specs=[pl.BlockSpec((tq, D_MODEL), lambda i: (i, 0)), pl.BlockSpec((n_mem, 2 * D_MODEL), lambda i: (0, 0))],
        out_specs=pl.BlockSpec((tq, D_MODEL), lambda i: (i, 0)), out_shape=jax.ShapeDtypeStruct((seq, D_MODEL), BF16),
        compiler_params=_params(("arbitrary",), 4 * _nbytes((tq, D_MODEL), F32) + 2 * _nbytes((n_mem, 2 * D_MODEL), F32)),
    )(q, kv)


def _attn_bwd(q, kv, do):
    seq, n_mem = q.shape[0], kv.shape[0]
    tq = _tile(seq, (512, 256, 128))

    def body(q_ref, kv_ref, do_ref, dq_ref, dkv_ref):
        @pl.when(pl.program_id(0) == 0)
        def _():
            dkv_ref[...] = jnp.zeros_like(dkv_ref)

        for h in range(HEADS):
            hd = pl.ds(h * CA_DH, CA_DH)
            vd = pl.ds(D_MODEL + h * CA_DH, CA_DH)
            _, vjp = jax.vjp(_attn_head, q_ref[:, hd], kv_ref[:, hd], kv_ref[:, vd])
            dq, dk, dv = vjp(do_ref[:, hd].astype(F32))
            dq_ref[:, hd] = dq.astype(BF16)
            dkv_ref[:, hd] += dk
            dkv_ref[:, vd] += dv

    return _pcall(
        body, name="attn_bwd", grid=(seq // tq,),
        in_specs=[pl.BlockSpec((tq, D_MODEL), lambda i: (i, 0)), pl.BlockSpec((n_mem, 2 * D_MODEL), lambda i: (0, 0)),
                  pl.BlockSpec((tq, D_MODEL), lambda i: (i, 0))],
        out_specs=[pl.BlockSpec((tq, D_MODEL), lambda i: (i, 0)), pl.BlockSpec((n_mem, 2 * D_MODEL), lambda i: (0, 0))],
        out_shape=[jax.ShapeDtypeStruct((seq, D_MODEL), BF16), jax.ShapeDtypeStruct((n_mem, 2 * D_MODEL), F32)],
        compiler_params=_params(("arbitrary",), 6 * _nbytes((tq, D_MODEL), F32) + 4 * _nbytes((n_mem, 2 * D_MODEL), F32)),
    )(q, kv, do)


def _ffn_mid(hg, xg, hv, xv, wg0, wg1, wg2, bg, wv0, wv1, wv2, bv):
    return jax.nn.gelu(causal_conv(hg, xg, (wg0, wg1, wg2), bg)) * causal_conv(hv, xv, (wv0, wv1, wv2), bv)


FFN_TB = 256
FFN_W = D_FF // 2
FFN_J = D_FF // FFN_W
MXU_COLS = 256
FFN_PIECES = tuple((off, min(MXU_COLS, FFN_W - off)) for off in range(0, FFN_W, MXU_COLS))


def _ffn_common_specs(seq, row):
    tb = min(FFN_TB, seq)
    full = pl.BlockSpec((tb, D_MODEL), lambda t, j: (row(t), 0))
    vec = pl.BlockSpec((1, D_MODEL), lambda t, j: (0, 0))
    halves = []
    for off in (0, FFN_J):
        halves.append(dict(
            w_up=pl.BlockSpec((None, D_MODEL, FFN_W), lambda t, j, off=off: (j + off, 0, 0)),
            taps=pl.BlockSpec((FFN_CONV, FFN_W), lambda t, j, off=off: (0, j + off)),
            bias=pl.BlockSpec((1, FFN_W), lambda t, j, off=off: (0, j + off))))
    w_down = pl.BlockSpec((FFN_W, D_MODEL), lambda t, j: (j, 0))
    u_blk = pl.BlockSpec((2, tb, FFN_W), lambda t, j: (0, row(t), j))
    return tb, full, vec, halves, w_down, u_blk


def _ffn_vmem(tb):
    return (_nbytes((2, tb, FFN_W), F32) + _nbytes((2, tb, FFN_W), BF16) + 3 * _nbytes((D_MODEL, FFN_W), BF16)
            + 10 * _nbytes((tb, D_MODEL), F32))


def _conv_params(taps_ref, bias_ref, cols):
    return taps_ref[0:1, cols], taps_ref[1:2, cols], taps_ref[2:3, cols], bias_ref[:, cols]


def _ffn_fwd(x2b, x2, w_up, conv_w, conv_b, w_down, ln_g, ln_b, target):
    seq = x2.shape[0]
    tb, full, vec, halves, wd_spec, u_blk = _ffn_common_specs(seq, lambda t: t)
    nt = seq // tb

    def body(xb_ref, wg_ref, wv_ref, tg_ref, tv_ref, bg_ref, bv_ref, wd_ref, x_ref, g_ref, b_ref, tgt_ref,
             u_ref, h_ref, dz_ref, dg_ref, db_ref, loss_ref, dzb_ref, acc, carry):
        t, j = pl.program_id(0), pl.program_id(1)
        xb = xb_ref[...]
        pieces = [pl.ds(off, width) for off, width in FFN_PIECES]
        ug = [_dg(xb, wg_ref[:, cols], 1, 0) for cols in pieces]
        uv = [_dg(xb, wv_ref[:, cols], 1, 0) for cols in pieces]
        hs = []
        for cols, g, v in zip(pieces, ug, uv):
            u_ref[0, :, cols] = g
            u_ref[1, :, cols] = v
            halo_g = jnp.where(t == 0, 0.0, carry[j, 0, :, cols])
            halo_v = jnp.where(t == 0, 0.0, carry[j, 1, :, cols])
            h = _ffn_mid(halo_g, g, halo_v, v, *_conv_params(tg_ref, bg_ref, cols),
                         *_conv_params(tv_ref, bv_ref, cols)).astype(BF16)
            carry[j, 0, :, cols] = g[tb - SUBLANES:, :]
            carry[j, 1, :, cols] = v[tb - SUBLANES:, :]
            h_ref[:, cols] = h
            hs.append(h)
        part = None
        for cols, h in zip(pieces, hs):
            p = _dg(h, wd_ref[cols, :], 1, 0)
            part = p if part is None else part + p

        @pl.when(j == 0)
        def _():
            acc[...] = part

        @pl.when(j > 0)
        def _():
            acc[...] += part

        @pl.when(j == FFN_J - 1)
        def _():
            y, vjp = jax.vjp(_layer_norm, acc[...] + ALPHA * x_ref[...], g_ref[...], b_ref[...])
            err = y - tgt_ref[...]
            part_loss = 0.5 * jnp.sum(jnp.sum(err * err, axis=1, keepdims=True), axis=0, keepdims=True) / D_MODEL
            dz, dg, db = vjp(err / D_MODEL)

            @pl.when(t == 0)
            def _():
                for r in (dg_ref, db_ref, loss_ref):
                    r[...] = jnp.zeros_like(r)

            dz_ref[...] = dz
            dzb_ref[...] = dz.astype(BF16)
            dg_ref[...] += dg
            db_ref[...] += db
            loss_ref[...] += jnp.broadcast_to(part_loss, (1, LANES))

    h0, h1 = halves
    row = jax.ShapeDtypeStruct((1, D_MODEL), F32)
    return _pcall(
        body, name="ffn_fwd", grid=(nt, FFN_J),
        in_specs=[full, h0["w_up"], h1["w_up"], h0["taps"], h1["taps"], h0["bias"], h1["bias"], wd_spec, full, vec, vec,
                  full],
        out_specs=[u_blk, pl.BlockSpec((tb, FFN_W), lambda t, j: (t, j)), full, vec, vec,
                   pl.BlockSpec((1, LANES), lambda t, j: (0, 0)), full],
        out_shape=[jax.ShapeDtypeStruct((2, seq, D_FF), F32), jax.ShapeDtypeStruct((seq, D_FF), BF16),
                   jax.ShapeDtypeStruct((seq, D_MODEL), F32), row, row, jax.ShapeDtypeStruct((1, LANES), F32),
                   jax.ShapeDtypeStruct((seq, D_MODEL), BF16)],
        scratch_shapes=[pltpu.VMEM((tb, D_MODEL), F32), pltpu.VMEM((FFN_J, 2, SUBLANES, FFN_W), F32)],
        compiler_params=_params(("arbitrary", "arbitrary"), _ffn_vmem(tb)),
    )(x2b, w_up, w_up, conv_w, conv_w, conv_b, conv_b, w_down, x2, ln_g, ln_b, target)


def _ffn_bwd(u, conv_w, conv_b, dz3b, dz3, w_down, w_up, z2, ln_g, ln_b):
    seq = dz3.shape[0]
    tb = min(FFN_TB, seq)
    nt = seq // tb
    row8 = tb // SUBLANES
    tb, full, vec, halves, wd_spec, u_blk = _ffn_common_specs(seq, lambda t: nt - 1 - t)
    halo = pl.BlockSpec((2, SUBLANES, FFN_W), lambda t, j: (0, jnp.maximum((nt - 1 - t) * row8 - 1, 0), j))

    def body(u_ref, halo_ref, tg_ref, tv_ref, bg_ref, bv_ref, dzb_ref, wd_ref, wg_ref, wv_ref, dz3_ref, z_ref, g_ref,
             b_ref, du_ref, dw_ref, dbias_ref, dz_ref, dg_ref, db_ref, dz2b_ref, acc, carry):
        t, j = pl.program_id(0), pl.program_id(1)

        @pl.when((t == 0) & (j == 0))
        def _():
            for r in (dw_ref, dbias_ref, dg_ref, db_ref):
                r[...] = jnp.zeros_like(r)

        pieces = [pl.ds(off, width) for off, width in FFN_PIECES]
        dzb = dzb_ref[...]
        dhs = [_dg(dzb, wd_ref[cols, :], 1, 1) for cols in pieces]
        first = t == nt - 1
        dus = []
        for cols, dh in zip(pieces, dhs):
            args = (jnp.where(first, 0.0, halo_ref[0, :, cols]), u_ref[0, :, cols],
                    jnp.where(first, 0.0, halo_ref[1, :, cols]), u_ref[1, :, cols],
                    *_conv_params(tg_ref, bg_ref, cols), *_conv_params(tv_ref, bv_ref, cols))
            _, vjp = jax.vjp(_ffn_mid, *args)
            dhg, dxg, dhv, dxv, g0, g1, g2, gb, v0, v1, v2, vb = vjp(dh)
            zeros = jnp.zeros((tb - SUBLANES, dh.shape[1]), F32)
            dug = (dxg + jnp.concatenate([zeros, jnp.where(t == 0, 0.0, carry[j, 0, :, cols])], axis=0)).astype(BF16)
            duv = (dxv + jnp.concatenate([zeros, jnp.where(t == 0, 0.0, carry[j, 1, :, cols])], axis=0)).astype(BF16)
            carry[j, 0, :, cols] = dhg
            carry[j, 1, :, cols] = dhv
            du_ref[0, :, cols] = dug
            du_ref[1, :, cols] = duv
            for half, parts in enumerate(((g0, g1, g2), (v0, v1, v2))):
                for d, p in enumerate(parts):
                    dw_ref[j, half, d:d + 1, cols] += p
            dbias_ref[j, 0, :, cols] += gb
            dbias_ref[j, 1, :, cols] += vb
            dus.append((dug, duv))
        part = None
        for cols, (dug, duv) in zip(pieces, dus):
            p = _dg(dug, wg_ref[:, cols], 1, 1) + _dg(duv, wv_ref[:, cols], 1, 1)
            part = p if part is None else part + p

        @pl.when(j == 0)
        def _():
            acc[...] = part

        @pl.when(j > 0)
        def _():
            acc[...] += part

        @pl.when(j == FFN_J - 1)
        def _():
            _, ln_vjp = jax.vjp(_layer_norm, z_ref[...], g_ref[...], b_ref[...])
            dz, dg, db = ln_vjp(acc[...] + ALPHA * dz3_ref[...])
            dz_ref[...] = dz
            dz2b_ref[...] = dz.astype(BF16)
            dg_ref[...] += dg
            db_ref[...] += db

    h0, h1 = halves
    row = jax.ShapeDtypeStruct((1, D_MODEL), F32)
    whole = lambda *shape: pl.BlockSpec(shape, lambda t, j: (0,) * len(shape))
    return _pcall(
        body, name="ffn_bwd", grid=(nt, FFN_J),
        in_specs=[u_blk, halo, h0["taps"], h1["taps"], h0["bias"], h1["bias"], full, wd_spec, h0["w_up"], h1["w_up"],
                  full, full, vec, vec],
        out_specs=[u_blk, whole(FFN_J, 2, FFN_CONV, FFN_W), whole(FFN_J, 2, 1, FFN_W), full, vec, vec, full],
        out_shape=[jax.ShapeDtypeStruct((2, seq, D_FF), BF16), jax.ShapeDtypeStruct((FFN_J, 2, FFN_CONV, FFN_W), F32),
                   jax.ShapeDtypeStruct((FFN_J, 2, 1, FFN_W), F32), jax.ShapeDtypeStruct((seq, D_MODEL), F32), row, row,
                   jax.ShapeDtypeStruct((seq, D_MODEL), BF16)],
        scratch_shapes=[pltpu.VMEM((tb, D_MODEL), F32), pltpu.VMEM((FFN_J, 2, SUBLANES, FFN_W), F32)],
        compiler_params=_params(("arbitrary", "arbitrary"), _ffn_vmem(tb)),
    )(u, u, conv_w, conv_w, conv_b, conv_b, dz3b, w_down, w_up, w_up, dz3, z2, ln_g, ln_b)


def _adamw_math(w, g, m, v):
    m_new = ADAM_B1 * m + (1.0 - ADAM_B1) * g
    v_new = ADAM_B2 * v + (1.0 - ADAM_B2) * jnp.square(g)
    m_hat = m_new / (1.0 - ADAM_B1 ** ADAM_STEP)
    v_hat = v_new / (1.0 - ADAM_B2 ** ADAM_STEP)
    return -ADAM_LR * (m_hat / (jnp.sqrt(v_hat) + ADAM_EPS) + ADAM_WD * w), m_new, v_new


def _adamw(name, w, g, m, v):
    rows, cols = w.shape
    tr = _tile(rows, (256, 176, 128, 64, 40, 32, 16, 8))

    def body(w_ref, g_ref, m_ref, v_ref, d_ref, nm_ref, nv_ref):
        d_ref[...], nm_ref[...], nv_ref[...] = _adamw_math(w_ref[...], g_ref[...], m_ref[...], v_ref[...])

    spec = pl.BlockSpec((tr, cols), lambda i: (i, 0))
    sh = jax.ShapeDtypeStruct((rows, cols), F32)
    return _pcall(
        body, name=name, grid=(rows // tr,), in_specs=[spec] * 4, out_specs=[spec] * 3, out_shape=[sh] * 3,
        compiler_params=_params(("arbitrary",), 14 * _nbytes((tr, -(-cols // LANES) * LANES), F32)),
    )(w, g, m, v)


def _adamw_halves(name, core, w, mine, theirs, m, v):
    rows, cols = w.shape
    half_rows = mine.shape[0]
    tr = _tile(half_rows, (256, 176, 128))
    nbh = half_rows // tr
    assert 2 * half_rows == rows

    def body(c_ref, w_ref, a_ref, b_ref, m_ref, v_ref, g_ref, d_ref, nm_ref, nv_ref):
        g = jnp.where(pl.program_id(0) // nbh == c_ref[0], a_ref[...], b_ref[...])
        g_ref[...] = g
        d_ref[...], nm_ref[...], nv_ref[...] = _adamw_math(w_ref[...], g, m_ref[...], v_ref[...])

    spec = pl.BlockSpec((tr, cols), lambda i, c_ref: (i, 0))
    half = pl.BlockSpec((tr, cols), lambda i, c_ref: (i % nbh, 0))
    sh = jax.ShapeDtypeStruct((rows, cols), F32)
    grid_spec = pltpu.PrefetchScalarGridSpec(
        num_scalar_prefetch=1, grid=(rows // tr,), in_specs=[spec, half, half, spec, spec], out_specs=[spec] * 4)
    return _pcall(
        body, name=name, grid_spec=grid_spec, out_shape=[sh] * 4,
        compiler_params=_params(("arbitrary",), 18 * _nbytes((tr, -(-cols // LANES) * LANES), F32)),
    )(core, w, mine, theirs, m, v)


MESH = pl.DeviceIdType.MESH
ANY = pl.BlockSpec(memory_space=pl.ANY)
N_CHIPS = 4
N_DEV = 8
BF16_ROWS = 16


def _me():
    return lax.axis_index("x"), lax.axis_index("y"), lax.axis_index("c")


def _other_chips(x, y):
    return [(1 - x, y), (x, 1 - y), (1 - x, 1 - y)]


def _remote(src, dst, ssem, rsem, dev):
    return pltpu.make_async_remote_copy(src_ref=src, dst_ref=dst, send_sem=ssem, recv_sem=rsem,
                                        device_id=dev, device_id_type=MESH)


def _half_rows(ref_rows, cc):
    half = ref_rows // 2
    return pl.ds(pl.multiple_of(cc * half, BF16_ROWS), half)


def _gather_weights(shards):
    n = len(shards)
    n_ici = n * (N_CHIPS - 1)

    def body(*refs):
        ins, outs, (ssem, rsem, lsem, lrsem) = refs[:n], refs[n:2 * n], refs[2 * n:]
        x, y, c = _me()
        k_me = 2 * x + y
        sib = (x, y, 1 - c)
        chips = _other_chips(x, y)
        started = []
        for i, (w_ref, o_ref) in enumerate(zip(ins, outs)):
            cp = _remote(w_ref, o_ref.at[k_me], lsem.at[i], lrsem.at[i], sib)
            cp.start()
            started.append(cp)
        for r, (px, py) in enumerate(chips):
            for i, (w_ref, o_ref) in enumerate(zip(ins, outs)):
                rows = _half_rows(w_ref.shape[0], c)
                s = r * n + i
                cp = _remote(w_ref.at[rows], o_ref.at[k_me, rows], ssem.at[s], rsem.at[s], (px, py, c))
                cp.start()
                started.append(cp)
        for r, (px, py) in enumerate(chips):
            for i, o_ref in enumerate(outs):
                blk = o_ref.at[2 * px + py, _half_rows(o_ref.shape[1], c)]
                s = r * n + i
                _remote(blk, blk, ssem.at[s], rsem.at[s], (px, py, c)).wait_recv()
                cp = _remote(blk, blk, ssem.at[n_ici + s], rsem.at[n_ici + s], sib)
                cp.start()
                started.append(cp)
        for r, (px, py) in enumerate(chips):
            for i, o_ref in enumerate(outs):
                blk = o_ref.at[2 * px + py, _half_rows(o_ref.shape[1], 1 - c)]
                s = n_ici + r * n + i
                _remote(blk, blk, ssem.at[s], rsem.at[s], sib).wait_recv()
        for cp in started[n:]:
            cp.wait_send()
        for cp in started[:n]:
            cp.wait()

    return _pcall(
        body, name="gather_weights", in_specs=[ANY] * n, out_specs=[ANY] * n,
        out_shape=[jax.ShapeDtypeStruct((N_CHIPS,) + s.shape, s.dtype) for s in shards],
        scratch_shapes=[pltpu.SemaphoreType.DMA((2 * n_ici,)), pltpu.SemaphoreType.DMA((2 * n_ici,)),
                        pltpu.SemaphoreType.DMA((n,)), pltpu.SemaphoreType.DMA((n,))],
    )(*shards)


def _swap_halves(name, grads):
    n = len(grads)

    def body(*refs):
        ins, outs, (ssem, rsem) = refs[:n], refs[n:2 * n], refs[2 * n:]
        x, y, c = _me()
        copies = []
        for i, (g_ref, o_ref) in enumerate(zip(ins, outs)):
            for k in range(N_CHIPS):
                s = i * N_CHIPS + k
                cp = _remote(g_ref.at[k, _half_rows(g_ref.shape[1], 1 - c)], o_ref.at[k], ssem.at[s], rsem.at[s],
                             (x, y, 1 - c))
                cp.start()
                copies.append(cp)
        for cp in copies:
            cp.wait()

    return _pcall(
        body, name=name, in_specs=[ANY] * n, out_specs=[ANY] * n,
        out_shape=[jax.ShapeDtypeStruct((N_CHIPS, g.shape[1] // 2, g.shape[2]), g.dtype) for g in grads],
        scratch_shapes=[pltpu.SemaphoreType.DMA((n * N_CHIPS,)), pltpu.SemaphoreType.DMA((n * N_CHIPS,))],
    )(*grads)


SEM = pl.BlockSpec(memory_space=pltpu.SEMAPHORE)
IN_HBM = pl.BlockSpec(memory_space=pltpu.HBM)
SPLIT_PARAMS = dict(compiler_params=pltpu.CompilerParams(has_side_effects=pltpu.SideEffectType.DATAFLOW_SIDE_EFFECTING))


def _split_start(name, sources, landings, n_copies, plan):
    ns, nl = len(sources), len(landings)

    def body(*refs):
        ins, lands, (ssem, rsem), token = refs[:ns], refs[ns:ns + nl], refs[ns + nl:ns + nl + 2], refs[-1]
        for s, (src, dst, _, dev) in enumerate(plan(ins, lands)):
            _remote(src, dst, ssem.at[s], rsem.at[s], dev).start()
        token[...] = jnp.zeros_like(token)

    arrays = list(sources) + list(landings)
    outs = _call(
        body, name=name, in_specs=[IN_HBM] * (ns + nl),
        out_specs=[SEM, SEM] + [IN_HBM] * (ns + nl) + [pl.BlockSpec(memory_space=pltpu.VMEM)],
        out_shape=[pltpu.SemaphoreType.DMA((n_copies,)), pltpu.SemaphoreType.DMA((n_copies,))]
        + [pltpu.HBM(a.shape, a.dtype) for a in arrays] + [jax.ShapeDtypeStruct((SUBLANES, LANES), F32)],
        input_output_aliases={i: 2 + i for i in range(ns + nl)}, **SPLIT_PARAMS,
    )(*[pltpu.with_memory_space_constraint(a, pltpu.HBM) for a in arrays])
    return (outs[:-1], ns), outs[-1]


def _split_wait(name, handle, after, plan):
    (ssem, rsem, *thru), ns = handle
    nl = len(thru) - ns

    def body(*refs):
        ins, lands, (ssem_ref, rsem_ref) = refs[:ns], refs[ns:ns + nl], refs[ns + nl:ns + nl + 2]
        for s, (src, _, dst, dev) in enumerate(plan(ins, lands)):
            cp = _remote(src, dst, ssem_ref.at[s], rsem_ref.at[s], dev)
            cp.wait_send()
            cp.wait_recv()

    outs = _call(
        body, name=name, in_specs=[IN_HBM] * (ns + nl) + [SEM, SEM, ANY], out_specs=[IN_HBM] * (ns + nl),
        out_shape=[pltpu.HBM(t.shape, t.dtype) for t in thru],
        input_output_aliases={i: i for i in range(ns + nl)}, **SPLIT_PARAMS,
    )(*thru, ssem, rsem, after)
    return outs[:ns], outs[ns:]


def _swap_plan(ins, lands):
    x, y, c = _me()
    return [(g_ref.at[k, _half_rows(g_ref.shape[1], 1 - c)], l_ref.at[k], l_ref.at[k], (x, y, 1 - c))
            for g_ref, l_ref in zip(ins, lands) for k in range(N_CHIPS)]


def _swap_start(name, grads):
    lands = [lax.empty((N_CHIPS, g.shape[1] // 2, g.shape[2]), g.dtype) for g in grads]
    return _split_start(name, grads, lands, len(grads) * N_CHIPS, _swap_plan)


def _swap_wait(name, handle, after):
    return _split_wait(name, handle, after, _swap_plan)


def _gather_plan(ins, lands):
    x, y, c = _me()
    k_me = 2 * x + y
    plan = [(w_ref, l_ref.at[k_me], l_ref.at[k_me], (x, y, 1 - c)) for w_ref, l_ref in zip(ins, lands)]
    for px, py in _other_chips(x, y):
        for w_ref, l_ref in zip(ins, lands):
            rows = _half_rows(w_ref.shape[0], c)
            plan.append((w_ref.at[rows], l_ref.at[k_me, rows], l_ref.at[2 * px + py, rows], (px, py, c)))
    return plan


def _gather_start(name, shards):
    lands = [lax.empty((N_CHIPS,) + s.shape, s.dtype) for s in shards]
    return _split_start(name, shards, lands, len(shards) * N_CHIPS, _gather_plan)


def _gather_wait(name, handle, after):
    return _split_wait(name, handle, after, _gather_plan)[1]


def _forward_halves(name, blocks):
    n = len(blocks)
    n_sem = n * (N_CHIPS - 1)

    def body(*refs):
        outs, (ssem, rsem) = refs[n:2 * n], refs[2 * n:]
        x, y, c = _me()
        sib = (x, y, 1 - c)
        chips = _other_chips(x, y)
        sends = []
        for r, (px, py) in enumerate(chips):
            for i, o_ref in enumerate(outs):
                blk = o_ref.at[2 * px + py, _half_rows(o_ref.shape[1], c)]
                cp = _remote(blk, blk, ssem.at[r * n + i], rsem.at[r * n + i], sib)
                cp.start()
                sends.append(cp)
        for r, (px, py) in enumerate(chips):
            for i, o_ref in enumerate(outs):
                blk = o_ref.at[2 * px + py, _half_rows(o_ref.shape[1], 1 - c)]
                _remote(blk, blk, ssem.at[r * n + i], rsem.at[r * n + i], sib).wait_recv()
        for cp in sends:
            cp.wait_send()

    return _pcall(
        body, name=name, in_specs=[ANY] * n, out_specs=[ANY] * n,
        out_shape=[jax.ShapeDtypeStruct(b.shape, b.dtype) for b in blocks],
        input_output_aliases={i: i for i in range(n)},
        scratch_shapes=[pltpu.SemaphoreType.DMA((n_sem,)), pltpu.SemaphoreType.DMA((n_sem,))],
    )(*blocks)


def _scatter_plan(ins, lands):
    x, y, c = _me()
    k_me = 2 * x + y
    return [(p_ref.at[2 * px + py], l_ref.at[k_me], l_ref.at[2 * px + py], (px, py, c))
            for px, py in _other_chips(x, y) for p_ref, l_ref in zip(ins, lands)]


def _scatter_start(name, parts):
    lands = [lax.empty(p.shape, p.dtype) for p in parts]
    return _split_start(name, parts, lands, len(parts) * (N_CHIPS - 1), _scatter_plan)


def _scatter_wait(name, handle, after):
    return _split_wait(name, handle, after, _scatter_plan)[1]


def _share_halves(halves):
    n = len(halves)

    def body(*refs):
        ins, outs, (ssem, rsem) = refs[:n], refs[n:2 * n], refs[2 * n:]
        x, y, c = _me()
        copies = [_remote(r_ref, o_ref, ssem.at[i], rsem.at[i], (x, y, 1 - c))
                  for i, (r_ref, o_ref) in enumerate(zip(ins, outs))]
        for cp in copies:
            cp.start()
        for cp in copies:
            cp.wait()

    return _pcall(
        body, name="share_halves", in_specs=[ANY] * n, out_specs=[ANY] * n,
        out_shape=[jax.ShapeDtypeStruct(h.shape, h.dtype) for h in halves],
        scratch_shapes=[pltpu.SemaphoreType.DMA((n,)), pltpu.SemaphoreType.DMA((n,))],
    )(*halves)


def _exchange_small(v, reduce):
    rows = v.shape[0]

    def body(v_ref, out_ref, buf, ssem, rsem):
        x, y, c = _me()
        me = 4 * x + 2 * y + c
        peers = [((x + bx) % 2, (y + by) % 2, (c + bc) % 2)
                 for bx in (0, 1) for by in (0, 1) for bc in (0, 1) if (bx, by, bc) != (0, 0, 0)]
        dst = buf if reduce else out_ref
        dst[me] = v_ref[...]
        sends = [_remote(v_ref, dst.at[me], ssem.at[r], rsem.at[r], p) for r, p in enumerate(peers)]
        for cp in sends:
            cp.start()
        for r, (px, py, pc) in enumerate(peers):
            blk = dst.at[4 * px + 2 * py + pc]
            _remote(blk, blk, ssem.at[r], rsem.at[r], (px, py, pc)).wait_recv()
        if reduce:
            acc = buf[0]
            for d in range(1, N_DEV):
                acc = acc + buf[d]
            out_ref[...] = acc
        for cp in sends:
            cp.wait_send()

    vm = pl.BlockSpec(memory_space=pltpu.VMEM)
    out_shape = jax.ShapeDtypeStruct((rows, LANES) if reduce else (N_DEV, rows, LANES), F32)
    buf_shape = (N_DEV, rows, LANES) if reduce else (SUBLANES, LANES)
    return _pcall(
        body, pin=False, name="reduce_small" if reduce else "gather_small", in_specs=[vm], out_specs=vm, out_shape=out_shape,
        scratch_shapes=[pltpu.VMEM(buf_shape, F32), pltpu.SemaphoreType.DMA((N_DEV - 1,)),
                        pltpu.SemaphoreType.DMA((N_DEV - 1,))],
        compiler_params=pltpu.CompilerParams(vmem_limit_bytes=32 * 1024 * 1024),
    )(v)


def _add_pair(name, core, g, theirs):
    _, half, cols = theirs.shape
    tr = _tile(half, (256, 176, 128))
    nb = half // tr

    def body(c_ref, g_ref, t_ref, o32_ref, o16_ref):
        s = g_ref[...] + t_ref[...]
        o32_ref[...] = s
        o16_ref[...] = s.astype(BF16)

    spec = pl.BlockSpec((None, tr, cols), lambda k, i, c_ref: (k, i, 0))
    grid_spec = pltpu.PrefetchScalarGridSpec(
        num_scalar_prefetch=1, grid=(N_CHIPS, nb),
        in_specs=[pl.BlockSpec((None, tr, cols), lambda k, i, c_ref: (k, c_ref[0] * nb + i, 0)), spec],
        out_specs=[spec, spec])
    return _pcall(
        body, name=name, grid_spec=grid_spec,
        out_shape=[jax.ShapeDtypeStruct(theirs.shape, F32), jax.ShapeDtypeStruct(theirs.shape, BF16)],
        compiler_params=_params(("arbitrary", "arbitrary"), 8 * _nbytes((tr, cols + LANES), F32)),
    )(core, g, theirs)


def _add_chips(name, chip, p32, recv):
    _, half, cols = p32.shape
    tr = _tile(half, (256, 176, 128))

    def body(k_ref, p_ref, r0_ref, r1_ref, r2_ref, o_ref):
        o_ref[...] = ((p_ref[...] + r0_ref[...].astype(F32)) + r1_ref[...].astype(F32)) + r2_ref[...].astype(F32)

    def other(r):
        return pl.BlockSpec((None, tr, cols), lambda i, k_ref: (r + (k_ref[0] <= r).astype(jnp.int32), i, 0))
    grid_spec = pltpu.PrefetchScalarGridSpec(
        num_scalar_prefetch=1, grid=(half // tr,),
        in_specs=[pl.BlockSpec((None, tr, cols), lambda i, k_ref: (k_ref[0], i, 0)), other(0), other(1), other(2)],
        out_specs=pl.BlockSpec((tr, cols), lambda i, k_ref: (i, 0)))
    return _pcall(
        body, name=name, grid_spec=grid_spec, out_shape=jax.ShapeDtypeStruct((half, cols), F32),
        compiler_params=_params(("arbitrary",), 10 * _nbytes((tr, cols + LANES), F32)),
    )(chip, p32, recv, recv, recv)


def kernel(x, mem, w_in, b_in, hg_lb_logits, hg_norm_w, ml_conv_w, ml_conv_b, ml_norm_w, w_out, ln1_g, ln1_b, ca_wq, ca_wkv, ca_wo, ln2_g, ln2_b, ffn_w_up, ffn_conv_w, ffn_conv_b, ffn_w_down, ln3_g, ln3_b, loss_target, m_w_in, m_b_in, m_hg_lb_logits, m_hg_norm_w, m_ml_conv_w, m_ml_conv_b, m_ml_norm_w, m_w_out, m_ln1_g, m_ln1_b, m_ca_wq, m_ca_wkv, m_ca_wo, m_ln2_g, m_ln2_b, m_ffn_w_up, m_ffn_conv_w, m_ffn_conv_b, m_ffn_w_down, m_ln3_g, m_ln3_b, v_w_in, v_b_in, v_hg_lb_logits, v_hg_norm_w, v_ml_conv_w, v_ml_conv_b, v_ml_norm_w, v_w_out, v_ln1_g, v_ln1_b, v_ca_wq, v_ca_wkv, v_ca_wo, v_ln2_g, v_ln2_b, v_ffn_w_up, v_ffn_conv_w, v_ffn_conv_b, v_ffn_w_down, v_ln3_g, v_ln3_b):
    return _train_step(dict(locals()))


WEIGHTS = ("w_in", "b_in", "hg_lb_logits", "hg_norm_w", "ml_conv_w", "ml_conv_b", "ml_norm_w", "w_out", "ln1_g",
           "ln1_b", "ca_wq", "ca_wkv", "ca_wo", "ln2_g", "ln2_b", "ffn_w_up", "ffn_conv_w", "ffn_conv_b",
           "ffn_w_down", "ln3_g", "ln3_b")
MATRICES = ("w_in", "w_out", "ca_wq", "ca_wkv", "ca_wo", "ffn_w_up", "ffn_w_down")
COL_SHARDED = ("w_in", "ca_wkv", "ffn_w_up", "ml_conv_w", "ffn_conv_w")
SMALL = tuple(n for n in WEIGHTS if n not in MATRICES)
PART_ROWS = 16


def _part_rows(shape, lead):
    n = 1
    for s in shape[lead:]:
        n *= s
    return -(-n // (LANES * PART_ROWS)) * PART_ROWS


def _pack(arrs, dtype, lead=0, rows=None):
    parts = []
    for a in arrs:
        head = a.shape[:lead]
        flat = a.reshape(head + (-1,)).astype(dtype)
        pad = _part_rows(a.shape, lead) * LANES - flat.shape[-1]
        flat = jnp.pad(flat, [(0, 0)] * lead + [(0, pad)])
        parts.append(flat.reshape(head + (-1, LANES)))
    used = sum(p.shape[lead] for p in parts)
    if rows is not None and rows > used:
        parts.append(jnp.zeros(parts[0].shape[:lead] + (rows - used, LANES), dtype))
    return jnp.concatenate(parts, axis=lead)


def _unpack(buf, shapes):
    lead = buf.shape[:-2]
    outs, r = [], 0
    for sh in shapes:
        n = 1
        for s in sh:
            n *= s
        nr = _part_rows(sh, 0)
        flat = buf[..., r:r + nr, :].reshape(lead + (nr * LANES,))
        outs.append(flat[..., :n].reshape(lead + tuple(sh)))
        r += nr
    return outs


def _cat_cols(s):
    return jnp.moveaxis(s, 0, 1).reshape(s.shape[1], -1)


def _stack_rows(s):
    return s.reshape(-1, s.shape[-1])


def _train_step(a):
    xs, mems, tgt = a["x"][0], a["mem"][0], a["loss_target"][0]
    core = lax.axis_index("c").astype(jnp.int32).reshape(1)
    chip = (2 * lax.axis_index("x") + lax.axis_index("y")).astype(jnp.int32).reshape(1)
    k_me = chip[0]
    shard = {n: a[n][0] for n in MATRICES}

    later = [n for n in MATRICES if n != "w_in"]
    gathering_in, token = _gather_start("gather_start_in", [shard["w_in"].astype(BF16)])
    zero = token[0:1, 0:1]
    xb = (xs + zero).astype(BF16)
    later_bf = [(shard[n] + zero).astype(BF16) for n in later]
    taps = _exchange_small(_pack([a["ml_conv_w"][0], a["ffn_conv_w"][0]], F32) + zero, reduce=False)
    done = xb[:SUBLANES, :LANES].astype(F32) + taps[0, :SUBLANES] + sum(l[:SUBLANES, :LANES].astype(F32) for l in later_bf)
    w_in = _forward_halves("forward_halves_in", _gather_wait("gather_wait_in", gathering_in, done))[0]
    w = {"w_in": jnp.pad(_cat_cols(w_in), ((0, 0), (0, D_IN_PAD - D_IN)))}
    gathering, token = _gather_start("gather_start", later_bf)
    taps = taps.reshape((N_CHIPS, 2) + taps.shape[1:])[:, 0]
    ml_cw, ffn_cw = [_cat_cols(s) for s in _unpack(taps, [a["ml_conv_w"].shape[1:], a["ffn_conv_w"].shape[1:]])]
    b_in_p = jnp.pad(a["b_in"], ((0, 0), (0, D_IN_PAD - D_IN))) + token[0:1, 0:1]
    mixer_w = (a["hg_lb_logits"], a["hg_norm_w"], ml_cw, a["ml_conv_b"], a["ml_norm_w"])
    up_cols = a["ffn_w_up"].shape[-1]

    proj =_mm("proj", "nn", xb, w["w_in"], bias=b_in_p, tm=256, tn=D_IN_PAD)
    y, hst, cst, nst, mst = _mixer_fwd(proj, *mixer_w)
    w.update(zip(later, _forward_halves("forward_halves", _gather_wait("gather_wait", gathering, y))))
    for n in ("w_out", "ca_wq", "ca_wo", "ffn_w_down"):
        w[n] = _stack_rows(w[n])
    z1, x1, x1b = _mm("mix_out", "nn", y, w["w_out"], res=xs, res_scale=ALPHA, ln=("fwd", a["ln1_g"], a["ln1_b"]),
                      copy_dtype=BF16)
    q = _mm("ca_q", "nn", x1b, w["ca_wq"], out_dtype=BF16, tn=D_MODEL)
    kv = _mm("ca_kv", "nn", mems, w["ca_wkv"])
    o = _attn_fwd(q, kv)
    z2, x2, x2b = _mm("ca_out", "nn", o, w["ca_wo"], res=x1, res_scale=ALPHA, ln=("fwd", a["ln2_g"], a["ln2_b"]),
                      copy_dtype=BF16)
    w_up = w["ffn_w_up"]
    assert w_up.shape == (2 * FFN_J, D_MODEL, FFN_W)
    u, hmid, dz3, g_ln3g, g_ln3b, loss_part, dz3b = _ffn_fwd(
        x2b, x2, w_up, ffn_cw, a["ffn_conv_b"], w["ffn_w_down"], a["ln3_g"], a["ln3_b"], tgt)

    grads = {"ln3_g": g_ln3g, "ln3_b": g_ln3b}
    grads["ffn_w_down"] = _mm("g_w_down", "tn", hmid, dz3b, tm=D_FF // 2, tn=D_MODEL)
    du, g_cw, g_cb, dz2, grads["ln2_g"], grads["ln2_b"], dz2b = _ffn_bwd(
        u, ffn_cw, a["ffn_conv_b"], dz3b, dz3, w["ffn_w_down"], w_up, z2, a["ln2_g"], a["ln2_b"])
    grads["ffn_conv_w"] = jnp.transpose(g_cw, (2, 1, 0, 3)).reshape(FFN_CONV, 2 * D_FF)
    grads["ffn_conv_b"] = jnp.transpose(g_cb, (2, 1, 0, 3)).reshape(1, 2 * D_FF)
    grads["ffn_w_up"] = _mm("g_w_up", "tn", x2b, du, out_groups=N_CHIPS, tm=D_MODEL, tn=up_cols)
    grads["ffn_w_down"] = grads["ffn_w_down"].reshape((N_CHIPS,) + shard["ffn_w_down"].shape)
    pending = {}

    def reduce_start(tag, names, swapped=None):
        group = [grads[n] for n in names]
        group, theirs = swapped or (group, _swap_halves("swap_halves_" + tag, group))
        sums = [_add_pair("add_pair_" + n, core, g, t) for n, g, t in zip(names, group, theirs)]
        handle, token = _scatter_start("scatter_start_" + tag, [s16 for _, s16 in sums])
        pending[tag] = (names, [s32 for s32, _ in sums], handle)
        return token[0:1, 0:1]

    ffn = ("ffn_w_up", "ffn_w_down")
    swapping, token = _swap_start("swap_start_ffn", [grads[n] for n in ffn])
    do = _mm("d_o", "nt", dz2b, w["ca_wo"], bias=jnp.zeros((1, D_MODEL), F32) + token[0:1, 0:1], out_dtype=BF16,
             tn=D_MODEL)
    grads["ca_wo"] = _mm("g_wo", "tn", o, dz2b, tm=D_MODEL, tn=D_MODEL)
    zero = reduce_start("ffn", ffn, _swap_wait("swap_wait_ffn", swapping, grads["ca_wo"]))
    dq, dkv = _attn_bwd(q, kv + zero, do)
    grads["ca_wq"] = _mm("g_wq", "tn", x1b, dq, tm=D_MODEL, tn=D_MODEL)
    grads["ca_wkv"] = _mm("g_wkv", "tn", mems, dkv, out_groups=N_CHIPS, tm=D_MODEL)
    dz1, grads["ln1_g"], grads["ln1_b"], dz1b = _mm("d_x1", "nt", dq, w["ca_wq"], res=dz2, res_scale=ALPHA,
                                                    ln=("bwd", z1, a["ln1_g"], a["ln1_b"]), copy_dtype=BF16)
    grads["w_out"] = _mm("g_w_out", "tn", y, dz1b, tm=D_MODEL, tn=D_MODEL)
    for n in ("w_out", "ca_wq", "ca_wo"):
        grads[n] = grads[n].reshape((N_CHIPS,) + shard[n].shape)
    attn = ("w_out", "ca_wq", "ca_wkv", "ca_wo")
    swapping, token = _swap_start("swap_start_attn", [grads[n] for n in attn])
    dy = _mm("d_y", "nt", dz1b, w["w_out"], bias=jnp.zeros((1, D_MODEL), F32) + token[0:1, 0:1], tn=D_MODEL)
    zero = reduce_start("attn", attn, _swap_wait("swap_wait_attn", swapping, dy))
    (dproj, g_b_in, grads["hg_lb_logits"], grads["hg_norm_w"], grads["ml_conv_w"], grads["ml_conv_b"],
     grads["ml_norm_w"]) = _mixer_bwd(proj, dy, hst, cst, nst, mst, mixer_w[0], mixer_w[1] + zero, *mixer_w[2:])
    g_in = _mm("g_w_in", "tn", xb, dproj, tm=D_MODEL, tn=up_cols)[:, :D_IN]
    grads["w_in"] = jnp.moveaxis(g_in.reshape(D_MODEL, N_CHIPS, -1), 1, 0)
    grads["b_in"] = g_b_in[:, :D_IN]
    zero = reduce_start("in", ("w_in",))
    dx = _mm("d_x", "nt", dproj, w["w_in"], bias=jnp.zeros((1, D_MODEL), F32) + zero, res=dz1, res_scale=ALPHA,
             tm=256, tn=D_MODEL)

    halves = {}
    for tag, (names, sums32, handle) in pending.items():
        for n, s32, r in zip(names, sums32, _scatter_wait("scatter_wait_" + tag, handle, dx)):
            halves[n] = _add_chips("add_chips_" + n, chip, s32, r)
    halves = [halves[n] for n in MATRICES]
    other_halves = _share_halves(halves)

    small_shapes = [grads[n].shape for n in SMALL] + [loss_part.shape]
    summed = _unpack(_exchange_small(_pack([grads[n] for n in SMALL] + [loss_part], F32), reduce=True), small_shapes)
    loss = summed[-1][0, 0]
    for n, g in zip(SMALL, summed[:-1]):
        if n in COL_SHARDED:
            cols = a[n].shape[-1]
            g = lax.dynamic_slice_in_dim(g, k_me * cols, cols, axis=1)
        grads[n] = g

    delta, new_m, new_v = {}, {}, {}
    for n, mine, theirs in zip(MATRICES, halves, other_halves):
        grads[n], delta[n], new_m[n], new_v[n] = _adamw_halves(
            "adamw_" + n, core, shard[n], mine, theirs, a["m_" + n][0], a["v_" + n][0])
    small_w = [a[n][0] if a[n].ndim == 3 else a[n] for n in SMALL]
    small_m = [a["m_" + n][0] if a[n].ndim == 3 else a["m_" + n] for n in SMALL]
    small_v = [a["v_" + n][0] if a[n].ndim == 3 else a["v_" + n] for n in SMALL]
    shapes = [w.shape for w in small_w]
    packed = [_pack(l, F32) for l in (small_w, [grads[n] for n in SMALL], small_m, small_v)]
    for out, buf in zip((delta, new_m, new_v), _adamw("adamw_small", *packed)):
        for n, v in zip(SMALL, _unpack(buf, shapes)):
            out[n] = v

    def shaped(d):
        return [d[n].reshape(a[n].shape) for n in WEIGHTS]
    return (loss, dx[None], *shaped(grads), *shaped(delta), *shaped(new_m), *shaped(new_v))
```

```python
import functools

import jax
import jax.numpy as jnp
from jax import lax
from jax.experimental import pallas as pl
from jax.experimental.pallas import tpu as pltpu

F32 = jnp.float32
BF16 = jnp.bfloat16

D_MODEL = 1024
HEADS = 4
DK = 128
D_GRP = HEADS * DK
CHUNK = 64
ML_CONV = 4
FFN_CONV = 3
D_FF = 2816
CA_DH = D_MODEL // HEADS
DEPTH = 1
ALPHA = (2.0 * DEPTH) ** 0.25
LN_EPS = 1e-5
NEG_BIG = -1e30
D_IN = 8 * D_GRP + 2 * HEADS
D_IN_PAD = 8 * D_GRP + 128
ADAM_LR, ADAM_B1, ADAM_B2, ADAM_EPS, ADAM_WD, ADAM_STEP = 0.001, 0.9, 0.999, 1e-08, 0.01, 10

SUBLANES = 8
LANES = 128
VMEM_BYTES = 64 * 1024 * 1024


def _pcall(body, pin=True, **kw):
    if not pin:
        return _call(body, **kw)
    kw["out_shape"] = jax.tree.map(lambda s: pltpu.HBM(s.shape, s.dtype), kw["out_shape"])
    call = _call(body, **kw)

    def pinned(*args):
        return call(*[pltpu.with_memory_space_constraint(x, pltpu.HBM) if jnp.issubdtype(x.dtype, jnp.floating) else x
                      for x in args])
    return pinned


def _call(body, **kw):
    return pl.pallas_call(body, **kw)


def _params(semantics, vmem_bytes):
    limit = int(min(max(2 * vmem_bytes, 16 * 1024 * 1024), VMEM_BYTES - 8 * 1024 * 1024))
    return pltpu.CompilerParams(dimension_semantics=semantics, vmem_limit_bytes=limit)


def _nbytes(shape, dtype):
    n = 1
    for s in shape:
        n *= s
    return n * jnp.dtype(dtype).itemsize


def _dg(a, b, ca, cb):
    return lax.dot_general(a.astype(BF16), b.astype(BF16), (((ca,), (cb,)), ((), ())),
                           preferred_element_type=F32)


@jax.custom_vjp
def mm_nn(a, b):
    return _dg(a, b, 1, 0)


mm_nn.defvjp(lambda a, b: (_dg(a, b, 1, 0), (a, b)),
             lambda r, g: (_dg(g, r[1], 1, 1).astype(r[0].dtype), _dg(r[0], g, 0, 0).astype(r[1].dtype)))


@jax.custom_vjp
def mm_nt(a, b):
    return _dg(a, b, 1, 1)


mm_nt.defvjp(lambda a, b: (_dg(a, b, 1, 1), (a, b)),
             lambda r, g: (_dg(g, r[1], 1, 0).astype(r[0].dtype), _dg(g, r[0], 0, 0).astype(r[1].dtype)))


@jax.custom_vjp
def mm_tn(a, b):
    return _dg(a, b, 0, 0)


mm_tn.defvjp(lambda a, b: (_dg(a, b, 0, 0), (a, b)),
             lambda r, g: (_dg(r[1], g, 1, 1).astype(r[0].dtype), _dg(r[0], g, 1, 0).astype(r[1].dtype)))


def _tri(n, lower):
    r = lax.broadcasted_iota(jnp.int32, (n, n), 0)
    c = lax.broadcasted_iota(jnp.int32, (n, n), 1)
    return ((r >= c) if lower else (r <= c)).astype(F32)


def _tri_dot(lower, x):
    t = _tri(x.shape[0], lower).astype(BF16)
    hi = x.astype(BF16)
    rest = x - hi.astype(F32)
    mid = rest.astype(BF16)
    lo = (rest - mid.astype(F32)).astype(BF16)
    return sum(lax.dot_general(t, p, (((1,), (0,)), ((), ())), preferred_element_type=F32) for p in (hi, mid, lo))


@jax.custom_vjp
def cumsum_rows(x):
    return _tri_dot(True, x)


cumsum_rows.defvjp(lambda x: (_tri_dot(True, x), None), lambda _, g: (_tri_dot(False, g),))


def _shift_impl(halo, x, d):
    xx = jnp.concatenate([halo, x], axis=0)
    return pltpu.roll(xx, d, 0)[SUBLANES:]


@functools.partial(jax.custom_vjp, nondiff_argnums=(2,))
def shift_rows(halo, x, d):
    return _shift_impl(halo, x, d)


def _shift_bwd(d, _, g):
    n = g.shape[0] + SUBLANES
    gg = jnp.concatenate([jnp.zeros((SUBLANES, g.shape[1]), g.dtype), g], axis=0)
    r = pltpu.roll(gg, n - d, 0)
    return r[:SUBLANES], r[SUBLANES:]


shift_rows.defvjp(lambda halo, x, d: (_shift_impl(halo, x, d), None), _shift_bwd)


def causal_conv(halo, x, w_rows, b):
    k = len(w_rows)
    y = b + w_rows[k - 1] * x
    for d in range(1, k):
        y = y + w_rows[k - 1 - d] * shift_rows(halo, x, d)
    return y


def _sigmoid(x):
    return 1.0 / (1.0 + jnp.exp(-x))


def _silu(x):
    return x * _sigmoid(x)


def _log_sigmoid(x):
    return jnp.minimum(x, 0.0) - jnp.log(1.0 + jnp.exp(-jnp.abs(x)))


def _pick_row(x, i):
    row = lax.broadcasted_iota(jnp.int32, (x.shape[0], 1), 0)
    return jnp.sum(jnp.where(row == i, x, 0.0), axis=0, keepdims=True)


def _layer_norm(z, g, b):
    mu = jnp.mean(z, axis=-1, keepdims=True)
    zc = z - mu
    var = jnp.mean(zc * zc, axis=-1, keepdims=True)
    return zc * lax.rsqrt(var + LN_EPS) * g + b


def _qk_conv(halo, x, w0, w1, w2, w3, b):
    return _silu(causal_conv(halo, x, (w0, w1, w2, w3), b))


def _grp(i, h=None):
    if h is None:
        return pl.ds(i * D_GRP, D_GRP)
    return pl.ds(i * D_GRP + h * DK, DK)


def _mixer_specs(n_chunks, reverse):
    def chunk(c):
        return n_chunks - 1 - c if reverse else c
    row8 = CHUNK // SUBLANES
    proj_spec = pl.BlockSpec((CHUNK, D_IN_PAD), lambda c: (chunk(c), 0))
    halo_spec = pl.BlockSpec((SUBLANES, 2 * D_GRP), lambda c: (jnp.maximum(chunk(c) * row8 - 1, 0), 2))
    small = [pl.BlockSpec((2, D_GRP), lambda c: (0, 0)), pl.BlockSpec((1, D_GRP), lambda c: (0, 0)),
             pl.BlockSpec((ML_CONV, 2 * D_GRP), lambda c: (0, 0)), pl.BlockSpec((1, 2 * D_GRP), lambda c: (0, 0)),
             pl.BlockSpec((1, D_GRP), lambda c: (0, 0))]
    state_specs = [pl.BlockSpec((1, HEADS, DK, DK), lambda c: (chunk(c), 0, 0, 0)),
                   pl.BlockSpec((1, HEADS, DK, DK), lambda c: (chunk(c), 0, 0, 0)),
                   pl.BlockSpec((1, HEADS, 1, DK), lambda c: (chunk(c), 0, 0, 0)),
                   pl.BlockSpec((1, HEADS, 1, DK), lambda c: (chunk(c), 0, 0, 0))]
    y_spec = pl.BlockSpec((CHUNK, 2 * D_GRP), lambda c: (chunk(c), 0))
    return proj_spec, halo_spec, small, state_specs, y_spec, chunk


def _heads(x):
    return [x[:, h * DK:(h + 1) * DK] for h in range(HEADS)]


def _last(x, j):
    lane = lax.broadcasted_iota(jnp.int32, (1, x.shape[-1]), 1)
    return jnp.sum(jnp.where(lane == j, x, 0.0), axis=-1, keepdims=True)


def _hg_chunk(st_t, hq, hf, hi, hgate, l0, l1, nw):
    n = hq.shape[0]
    lb = _sigmoid(l0 - l1)
    q = _silu(hq)
    lf = jnp.log(lb + (1.0 - lb) * _sigmoid(hf))
    k = (1.0 - lb) * _sigmoid(-hf)
    b = cumsum_rows(lf)
    b_ref = _pick_row(b, n // 2 - 1)
    b_last = _pick_row(b, n - 1)
    qa, ka =_heads(q * jnp.exp(b - b_ref)), _heads(k * jnp.exp(b_ref - b))
    qe, kd, eb, v = _heads(q * jnp.exp(b)), _heads(k * jnp.exp(b_last - b)), _heads(jnp.exp(b_last)), _heads(hi)
    tri = _tri(n, True) > 0
    attn = [jnp.where(tri, mm_nt(qa[h], ka[h]), 0.0) for h in range(HEADS)]
    o = [mm_nn(attn[h], v[h]) + mm_nt(qe[h], st_t[h]) for h in range(HEADS)]
    st_new = jnp.stack([eb[h] * st_t[h] + mm_tn(v[h], kd[h]) for h in range(HEADS)])
    yn = [o[h] * lax.rsqrt(jnp.mean(o[h] * o[h], axis=-1, keepdims=True) + LN_EPS) for h in range(HEADS)]
    return st_new, jnp.concatenate(yn, axis=1) * nw * _silu(hgate)


def _ml_chunk(c_st, n_st, m_st, q, k, v, gates, og, nw):
    n = q.shape[0]
    ig = jnp.stack([_last(gates, h) for h in range(HEADS)])
    log_f = _log_sigmoid(gates)
    fl = jnp.stack([_last(log_f, HEADS + h) for h in range(HEADS)])
    bw = cumsum_rows(jnp.concatenate([jnp.broadcast_to(fl[h], (n, DK)) for h in range(HEADS)], axis=1))
    b = jnp.stack([_last(x, 0) for x in _heads(bw)])
    g = jnp.sum(fl, axis=1, keepdims=True)
    eye = lax.broadcasted_iota(jnp.int32, (n, n), 0) == lax.broadcasted_iota(jnp.int32, (n, n), 1)
    e_row = jnp.sum(jnp.where(eye, ig - b, 0.0), axis=1, keepdims=True)
    d = jnp.where(_tri(n, True) > 0, b + e_row, -jnp.inf)
    inter = b + m_st
    m_t = jnp.maximum(inter, jnp.max(d, axis=2, keepdims=True))
    qs, kh, vh = _heads(q * (DK ** -0.5)), _heads(k), _heads(v)
    s = jnp.stack([mm_nt(qs[h], kh[h]) for h in range(HEADS)]) * jnp.exp(d - m_t)
    w_inter = jnp.exp(inter - m_t)
    num = (jnp.stack([mm_nn(s[h], vh[h]) for h in range(HEADS)])
           + w_inter * jnp.stack([mm_nn(qs[h], c_st[h]) for h in range(HEADS)]))
    den = jnp.sum(s, axis=2, keepdims=True) + w_inter * jnp.sum(jnp.stack(qs) * n_st, axis=2, keepdims=True)
    h_out = num / jnp.maximum(jnp.abs(den), jnp.exp(-m_t))
    a = g - b + ig
    m_new = jnp.maximum(g + m_st, jnp.max(a, axis=1, keepdims=True))
    decay = jnp.exp(g + m_st - m_new)
    wk = jnp.stack(kh) * jnp.exp(a - m_new)
    c_new = decay * c_st + jnp.stack([mm_tn(wk[h], vh[h]) for h in range(HEADS)])
    n_new = decay * n_st + jnp.sum(wk, axis=1, keepdims=True)
    hc = h_out - jnp.mean(h_out, axis=-1, keepdims=True)
    yn = hc * lax.rsqrt(jnp.mean(hc * hc, axis=-1, keepdims=True) + LN_EPS)
    y = _sigmoid(og) * (jnp.concatenate([yn[h] for h in range(HEADS)], axis=1) * nw)
    return c_new, n_new, m_new, y


def _mixer_inputs(proj_ref, lg_ref, hnw_ref, mnw_ref, qk):
    hg_in = (proj_ref[:, _grp(0)], proj_ref[:, _grp(1)], proj_ref[:, _grp(2)], proj_ref[:, _grp(3)],
             lg_ref[0:1, :], lg_ref[1:2, :], hnw_ref[...])
    ml_in = (qk[:, :D_GRP], qk[:, D_GRP:], proj_ref[:, _grp(6)], proj_ref[:, pl.ds(8 * D_GRP, LANES)],
             proj_ref[:, _grp(7)], mnw_ref[...])
    return hg_in, ml_in


def _mixer_fwd(proj, lb_logits, hg_nw, conv_w, conv_b, ml_nw):
    seq = proj.shape[0]
    n_chunks = seq // CHUNK
    proj_spec, halo_spec, small, state_specs, y_spec, _ = _mixer_specs(n_chunks, False)

    def body(proj_ref, halo_ref, lg_ref, hnw_ref, cw_ref, cb_ref, mnw_ref,
             y_ref, hst_ref, cst_ref, nst_ref, mst_ref, hs, cs, ns, ms):
        c = pl.program_id(0)

        @pl.when(c == 0)
        def _():
            hs[...] = jnp.zeros_like(hs)
            cs[...] = jnp.zeros_like(cs)
            ns[...] = jnp.zeros_like(ns)
            ms[...] = jnp.full(ms.shape, NEG_BIG, F32)

        hst_ref[0] = hs[...]
        cst_ref[0] = cs[...]
        nst_ref[0] = ns[...]
        mst_ref[0] = ms[...]
        halo = jnp.where(c > 0, halo_ref[...], 0.0)
        qk = _qk_conv(halo, proj_ref[:, pl.ds(4 * D_GRP, 2 * D_GRP)],
                      cw_ref[0:1, :], cw_ref[1:2, :], cw_ref[2:3, :], cw_ref[3:4, :], cb_ref[...])
        hg_in, ml_in = _mixer_inputs(proj_ref, lg_ref, hnw_ref, mnw_ref, qk)
        hs[...], y_hg = _hg_chunk(hs[...], *hg_in)
        cs[...], ns[...], m_new, y_ml = _ml_chunk(cs[...], ns[...], _last(ms[...], 0), *ml_in)
        ms[...] = jnp.broadcast_to(m_new, ms.shape)
        y_ref[:, pl.ds(0, D_GRP)] = y_hg.astype(BF16)
        y_ref[:, pl.ds(D_GRP, D_GRP)] = y_ml.astype(BF16)

    st = jax.ShapeDtypeStruct((n_chunks, HEADS, DK, DK), F32)
    vec = jax.ShapeDtypeStruct((n_chunks, HEADS, 1, DK), F32)
    vmem = 2 * (_nbytes((CHUNK, D_IN_PAD), F32) + _nbytes((CHUNK, 2 * D_GRP), F32) + 2 * _nbytes((HEADS, DK, DK), F32)) \
        + 2 * _nbytes((HEADS, DK, DK), F32)
    return _pcall(
        body, name="mixer_fwd", grid=(n_chunks,),
        in_specs=[proj_spec, halo_spec] + small,
        out_specs=[y_spec] + state_specs,
        out_shape=[jax.ShapeDtypeStruct((seq, 2 * D_GRP), BF16), st, st, vec, vec],
        scratch_shapes=[pltpu.VMEM((HEADS, DK, DK), F32), pltpu.VMEM((HEADS, DK, DK), F32),
                        pltpu.VMEM((HEADS, 1, DK), F32), pltpu.VMEM((HEADS, 1, DK), F32)],
        compiler_params=_params(("arbitrary",), vmem),
    )(proj, proj, lb_logits, hg_nw, conv_w, conv_b, ml_nw)


def _mixer_bwd(proj, dy, hst, cst, nst, mst, lb_logits, hg_nw, conv_w, conv_b, ml_nw):
    seq = proj.shape[0]
    n_chunks = seq // CHUNK
    proj_spec, halo_spec, small, state_specs, y_spec, _ = _mixer_specs(n_chunks, True)

    def body(proj_ref, halo_ref, dy_ref, hst_ref, cst_ref, nst_ref, mst_ref,
             lg_ref, hnw_ref, cw_ref, cb_ref, mnw_ref,
             dproj_ref, dbin_ref, dlg_ref, dhnw_ref, dcw_ref, dcb_ref, dmnw_ref,
             dhs, dcs, dns, dms, dhalo):
        c = pl.program_id(0)

        @pl.when(c == 0)
        def _():
            for r in (dhs, dcs, dns, dms, dhalo, dbin_ref, dlg_ref, dhnw_ref, dcw_ref, dcb_ref, dmnw_ref):
                r[...] = jnp.zeros_like(r)

        def put(cols, val):
            dproj_ref[:, cols] = val.astype(BF16)
            dbin_ref[:, cols] += jnp.sum(val, axis=0, keepdims=True)

        first = c == n_chunks - 1
        halo = jnp.where(first, 0.0, halo_ref[...])
        x_qk = proj_ref[:, pl.ds(4 * D_GRP, 2 * D_GRP)]
        conv_args = (halo, x_qk, cw_ref[0:1, :], cw_ref[1:2, :], cw_ref[2:3, :], cw_ref[3:4, :], cb_ref[...])
        qk, conv_vjp = jax.vjp(_qk_conv, *conv_args)
        hg_in, ml_in = _mixer_inputs(proj_ref, lg_ref, hnw_ref, mnw_ref, qk)
        _, hg_vjp = jax.vjp(_hg_chunk, hst_ref[0], *hg_in)
        _, ml_vjp = jax.vjp(_ml_chunk, cst_ref[0], nst_ref[0], _last(mst_ref[0], 0), *ml_in)
        dst, dhq, dhf, dhi, dhg, dl0, dl1, dnw = hg_vjp((dhs[...], dy_ref[:, pl.ds(0, D_GRP)]))
        dc, dn, dm, dq, dk, dv, dgates, dog, dmn = ml_vjp(
            (dcs[...], dns[...], _last(dms[...], 0), dy_ref[:, pl.ds(D_GRP, D_GRP)]))
        dhs[...] = dst
        dcs[...] = dc
        dns[...] = dn
        dms[...] = jnp.broadcast_to(dm, dms.shape)
        for i, val in ((0, dhq), (1, dhf), (2, dhi), (3, dhg), (6, dv), (7, dog)):
            put(_grp(i), val)
        put(pl.ds(8 * D_GRP, LANES), dgates)
        dlg_ref[0:1, :] += dl0
        dlg_ref[1:2, :] += dl1
        dhnw_ref[...] += dnw
        dmnw_ref[...] += dmn
        dh, dx, dw0, dw1, dw2, dw3, db = conv_vjp(jnp.concatenate([dq, dk], axis=1))
        tail = jnp.concatenate([jnp.zeros((CHUNK - SUBLANES, 2 * D_GRP), F32), dhalo[...]], axis=0)
        put(pl.ds(4 * D_GRP, 2 * D_GRP), dx + tail)
        dhalo[...] = dh
        for d, dw in enumerate((dw0, dw1, dw2, dw3)):
            dcw_ref[d:d + 1, :] += dw
        dcb_ref[...] += db

    row = pl.BlockSpec((1, D_GRP), lambda c: (0, 0))
    small_out = [pl.BlockSpec((1, D_IN_PAD), lambda c: (0, 0)), pl.BlockSpec((2, D_GRP), lambda c: (0, 0)), row,
                 pl.BlockSpec((ML_CONV, 2 * D_GRP), lambda c: (0, 0)), pl.BlockSpec((1, 2 * D_GRP), lambda c: (0, 0)), row]
    dy_spec = pl.BlockSpec((CHUNK, 2 * D_GRP), y_spec.index_map)
    vmem = 2 * (2 * _nbytes((CHUNK, D_IN_PAD), F32) + _nbytes((CHUNK, 2 * D_GRP), F32)
                + 2 * _nbytes((HEADS, DK, DK), F32)) + 2 * _nbytes((HEADS, DK, DK), F32) + 4 * 1024 * 1024
    return _pcall(
        body, name="mixer_bwd", grid=(n_chunks,),
        in_specs=[proj_spec, halo_spec, dy_spec] + state_specs + small,
        out_specs=[proj_spec] + small_out,
        out_shape=[jax.ShapeDtypeStruct((seq, D_IN_PAD), BF16), jax.ShapeDtypeStruct((1, D_IN_PAD), F32),
                   jax.ShapeDtypeStruct((2, D_GRP), F32), jax.ShapeDtypeStruct((1, D_GRP), F32),
                   jax.ShapeDtypeStruct((ML_CONV, 2 * D_GRP), F32), jax.ShapeDtypeStruct((1, 2 * D_GRP), F32),
                   jax.ShapeDtypeStruct((1, D_GRP), F32)],
        scratch_shapes=[pltpu.VMEM((HEADS, DK, DK), F32), pltpu.VMEM((HEADS, DK, DK), F32),
                        pltpu.VMEM((HEADS, 1, DK), F32), pltpu.VMEM((HEADS, 1, DK), F32),
                        pltpu.VMEM((SUBLANES, 2 * D_GRP), F32)],
        compiler_params=_params(("arbitrary",), vmem),
    )(proj, proj, dy, hst, cst, nst, mst, lb_logits, hg_nw, conv_w, conv_b, ml_nw)


def _tile(n, prefs, unit=None):
    unit = unit or n
    for p in prefs:
        if unit % p == 0 and n % p == 0:
            return p
    return unit


def _logical(arr):
    return arr.shape if arr.ndim == 2 else (arr.shape[1], arr.shape[0] * arr.shape[2])


def _group(arr):
    return arr.shape[-1]


def _split_spec(ndim, group, tr, tc, where):
    if ndim == 2:
        return pl.BlockSpec((tr, tc), where)
    per = group // tc
    assert per * tc == group, (group, tc)

    def index(*ids):
        bi, bj = where(*ids)
        return (bj // per, bi, bj % per)
    return pl.BlockSpec((None, tr, tc), index)


def _mm(name, mode, a, b, *, bias=None, res=None, res_scale=1.0, ln=None, out_dtype=F32, out_groups=None,
        copy_dtype=None, tm=None, tn=None, tk=None):
    la, lb = _logical(a), _logical(b)
    if mode == "nn":
        (m, k), n = la, lb[1]
        n_unit = _group(b) if b.ndim == 3 else n
        kc = _group(a) if a.ndim == 3 else k
    elif mode == "nt":
        (m, k), n = la, lb[0]
        n_unit = n
        kc = min(_group(a) if a.ndim == 3 else k, _group(b) if b.ndim == 3 else k)
    else:
        (k, m), n = la, lb[1]
        n_unit, kc = (_group(b) if b.ndim == 3 else n), k
        assert a.ndim == 2
    if out_groups:
        n_unit = min(n_unit, n // out_groups)
    kind = ln[0] if ln else None
    tm = tm or (256 if ln else _tile(m, (512, 256, 128)))
    tn = n if ln else (tn or _tile(n, (512, 384, 256, 128), n_unit))
    tk = (tk or _tile(k, (2048, 512, 256, 128))) if mode == "tn" else k
    gi, gj, gk = m // tm, n // tn, k // tk
    assert gi * tm == m and gj * tn == n and gk * tk == k and n_unit % tn == 0, (name, m, n, k, tm, tn, tk)
    ca, cb = {"nn": (1, 0), "nt": (1, 1), "tn": (0, 0)}[mode]
    i_outer = gk > 1 or (gi - 1) * _nbytes(b.shape, b.dtype) <= (gj - 1) * _nbytes(a.shape, a.dtype)

    def ij(where):
        return (lambda p, q, kk: where(p, q, kk)) if i_outer else (lambda p, q, kk: where(q, p, kk))
    if mode == "tn":
        a_spec = pl.BlockSpec((tk, tm), ij(lambda i, j, kk: (kk, i)))
    elif a.ndim == 3:
        a_spec = pl.BlockSpec((a.shape[0], tm, _group(a)), ij(lambda i, j, kk: (0, i, 0)))
    else:
        a_spec = pl.BlockSpec((tm, k), ij(lambda i, j, kk: (i, 0)))
    if mode != "nt":
        b_spec = _split_spec(b.ndim, _group(b), tk, tn, ij(lambda i, j, kk: (kk, j)))
    elif b.ndim == 3:
        b_spec = pl.BlockSpec((b.shape[0], tn, _group(b)), ij(lambda i, j, kk: (0, j, 0)))
    else:
        b_spec = pl.BlockSpec((tn, k), ij(lambda i, j, kk: (j, 0)))
    row_spec = pl.BlockSpec((1, tn), ij(lambda i, j, kk: (0, j)))
    blk_spec = pl.BlockSpec((tm, tn), ij(lambda i, j, kk: (i, j)))
    ins, in_specs = [a, b], [a_spec, b_spec]
    if bias is not None:
        ins.append(bias), in_specs.append(row_spec)
    if res is not None:
        ins.append(res), in_specs.append(blk_spec)
    if kind == "fwd":
        ins += [ln[1], ln[2]]
        in_specs += [row_spec, row_spec]
    elif kind == "bwd":
        ins += [ln[1], ln[2], ln[3]]
        in_specs += [blk_spec, row_spec, row_spec]
    if out_groups:
        blk_out = jax.ShapeDtypeStruct((out_groups, m, n // out_groups), out_dtype)
        out_spec = _split_spec(3, n // out_groups, tm, tn, ij(lambda i, j, kk: (i, j)))
    else:
        blk_out, out_spec = jax.ShapeDtypeStruct((m, n), out_dtype), blk_spec
    row_out = jax.ShapeDtypeStruct((1, n), F32)
    if kind is None:
        out_shape, out_specs = [blk_out], [out_spec]
    elif kind == "fwd":
        out_shape, out_specs = [blk_out, blk_out], [blk_spec, blk_spec]
    else:
        out_shape, out_specs = [blk_out, row_out, row_out], [blk_spec, row_spec, row_spec]
    if copy_dtype is not None:
        out_shape.append(jax.ShapeDtypeStruct((m, n), copy_dtype))
        out_specs.append(blk_spec)
    n_in = len(ins)

    def body(*refs):
        in_refs, out_refs, acc_ref = refs[:n_in], refs[n_in:n_in + len(out_shape)], refs[-1]
        i, kk = pl.program_id(0 if i_outer else 1), pl.program_id(2)
        a_ref, b_ref = in_refs[:2]
        extra = list(in_refs[2:])

        def epilogue(acc):
            rest = list(extra)
            if bias is not None:
                acc = acc + rest.pop(0)[...]
            if res is not None:
                acc = acc + res_scale * rest.pop(0)[...]
            if kind is None:
                out_refs[0][...] = acc.astype(out_dtype)
                return
            if kind == "fwd":
                out_refs[0][...] = acc
                y = _layer_norm(acc, rest[0][...], rest[1][...])
                out_refs[1][...] = y
                if copy_dtype is not None:
                    out_refs[-1][...] = y.astype(copy_dtype)
                return
            _, vjp = jax.vjp(_layer_norm, rest[0][...], rest[1][...], rest[2][...])
            dz, dg, db = vjp(acc)
            out_refs[0][...] = dz
            out_refs[1][...] += dg
            out_refs[2][...] += db
            if copy_dtype is not None:
                out_refs[-1][...] = dz.astype(copy_dtype)

        if kind == "bwd":
            @pl.when((i == 0) & (kk == 0))
            def _():
                out_refs[1][...] = jnp.zeros_like(out_refs[1])
                out_refs[2][...] = jnp.zeros_like(out_refs[2])

        def chunk(ref, c0, last):
            if ref.ndim == 3:
                g = ref.shape[2]
                return ref[c0 // g, :, pl.ds(c0 % g, kc)]
            return ref[:, pl.ds(c0, kc)] if last else ref[pl.ds(c0, kc), :]

        if mode == "tn" or kc == k:
            prod = _dg(a_ref[...], b_ref[...], ca, cb)
        else:
            prod = None
            for c0 in range(0, k, kc):
                part = _dg(chunk(a_ref, c0, True), chunk(b_ref, c0, mode == "nt"), ca, cb)
                prod = part if prod is None else prod + part
        if gk == 1:
            epilogue(prod)
            return

        @pl.when(kk == 0)
        def _():
            acc_ref[...] = prod

        @pl.when(kk > 0)
        def _():
            acc_ref[...] += prod

        @pl.when(kk == gk - 1)
        def _():
            epilogue(acc_ref[...])

    vmem = (2 * (_nbytes((tm, tk), a.dtype) + _nbytes((tk, tn), b.dtype))
            + (2 * len(ins) + 2 * len(out_shape) + 1) * _nbytes((tm, tn), F32))
    outs = _pcall(
        body, name=name, grid=(gi, gj, gk) if i_outer else (gj, gi, gk), in_specs=in_specs, out_specs=out_specs,
        out_shape=out_shape, scratch_shapes=[pltpu.VMEM((tm, tn) if gk > 1 else (SUBLANES, LANES), F32)],
        compiler_params=_params(("arbitrary", "arbitrary", "arbitrary"), vmem),
    )(*ins)
    return outs[0] if (kind is None and copy_dtype is None) else outs


def _attn_head(q, k, v):
    sc = mm_nt(q, k) * (CA_DH ** -0.5)
    e = jnp.exp(sc - jnp.max(sc, axis=-1, keepdims=True))
    return mm_nn(e / jnp.sum(e, axis=-1, keepdims=True), v)


def _attn_fwd(q, kv):
    seq, n_mem = q.shape[0], kv.shape[0]
    tq = _tile(seq, (512, 256, 128))

    def body(q_ref, kv_ref, o_ref):
        for h in range(HEADS):
            hd = pl.ds(h * CA_DH, CA_DH)
            o = _attn_head(q_ref[:, hd], kv_ref[:, hd], kv_ref[:, pl.ds(D_MODEL + h * CA_DH, CA_DH)])
            o_ref[:, hd] = o.astype(BF16)

    return _pcall(
        body, name="attn_fwd", grid=(seq // tq,),
        in_specs=[pl.BlockSpec((tq, D_MODEL), lambda i: (i, 0)), pl.BlockSpec((n_mem, 2 * D_MODEL), lambda i: (0, 0))],
        out_specs=pl.BlockSpec((tq, D_MODEL), lambda i: (i, 0)), out_shape=jax.ShapeDtypeStruct((seq, D_MODEL), BF16),
        compiler_params=_params(("arbitrary",), 4 * _nbytes((tq, D_MODEL), F32) + 2 * _nbytes((n_mem, 2 * D_MODEL), F32)),
    )(q, kv)


def _attn_bwd(q, kv, do):
    seq, n_mem = q.shape[0], kv.shape[0]
    tq = _tile(seq, (512, 256, 128))

    def body(q_ref, kv_ref, do_ref, dq_ref, dkv_ref):
        @pl.when(pl.program_id(0) == 0)
        def _():
            dkv_ref[...] = jnp.zeros_like(dkv_ref)

        for h in range(HEADS):
            hd = pl.ds(h * CA_DH, CA_DH)
            vd = pl.ds(D_MODEL + h * CA_DH, CA_DH)
            _, vjp = jax.vjp(_attn_head, q_ref[:, hd], kv_ref[:, hd], kv_ref[:, vd])
            dq, dk, dv = vjp(do_ref[:, hd].astype(F32))
            dq_ref[:, hd] = dq.astype(BF16)
            dkv_ref[:, hd] += dk
            dkv_ref[:, vd] += dv

    return _pcall(
        body, name="attn_bwd", grid=(seq // tq,),
        in_specs=[pl.BlockSpec((tq, D_MODEL), lambda i: (i, 0)), pl.BlockSpec((n_mem, 2 * D_MODEL), lambda i: (0, 0)),
                  pl.BlockSpec((tq, D_MODEL), lambda i: (i, 0))],
        out_specs=[pl.BlockSpec((tq, D_MODEL), lambda i: (i, 0)), pl.BlockSpec((n_mem, 2 * D_MODEL), lambda i: (0, 0))],
        out_shape=[jax.ShapeDtypeStruct((seq, D_MODEL), BF16), jax.ShapeDtypeStruct((n_mem, 2 * D_MODEL), F32)],
        compiler_params=_params(("arbitrary",), 6 * _nbytes((tq, D_MODEL), F32) + 4 * _nbytes((n_mem, 2 * D_MODEL), F32)),
    )(q, kv, do)


def _ffn_mid(hg, xg, hv, xv, wg0, wg1, wg2, bg, wv0, wv1, wv2, bv):
    return jax.nn.gelu(causal_conv(hg, xg, (wg0, wg1, wg2), bg)) * causal_conv(hv, xv, (wv0, wv1, wv2), bv)


FFN_TB = 256
FFN_W = D_FF // 2
FFN_J = D_FF // FFN_W
MXU_COLS = 256
FFN_PIECES = tuple((off, min(MXU_COLS, FFN_W - off)) for off in range(0, FFN_W, MXU_COLS))


def _ffn_common_specs(seq, row):
    tb = min(FFN_TB, seq)
    full = pl.BlockSpec((tb, D_MODEL), lambda t, j: (row(t), 0))
    vec = pl.BlockSpec((1, D_MODEL), lambda t, j: (0, 0))
    halves = []
    for off in (0, FFN_J):
        halves.append(dict(
            w_up=pl.BlockSpec((None, D_MODEL, FFN_W), lambda t, j, off=off: (j + off, 0, 0)),
            taps=pl.BlockSpec((FFN_CONV, FFN_W), lambda t, j, off=off: (0, j + off)),
            bias=pl.BlockSpec((1, FFN_W), lambda t, j, off=off: (0, j + off))))
    w_down = pl.BlockSpec((FFN_W, D_MODEL), lambda t, j: (j, 0))
    u_blk = pl.BlockSpec((2, tb, FFN_W), lambda t, j: (0, row(t), j))
    return tb, full, vec, halves, w_down, u_blk


def _ffn_vmem(tb):
    return (_nbytes((2, tb, FFN_W), F32) + _nbytes((2, tb, FFN_W), BF16) + 3 * _nbytes((D_MODEL, FFN_W), BF16)
            + 10 * _nbytes((tb, D_MODEL), F32))


def _conv_params(taps_ref, bias_ref, cols):
    return taps_ref[0:1, cols], taps_ref[1:2, cols], taps_ref[2:3, cols], bias_ref[:, cols]


def _ffn_fwd(x2b, x2, w_up, conv_w, conv_b, w_down, ln_g, ln_b, target):
    seq = x2.shape[0]
    tb, full, vec, halves, wd_spec, u_blk = _ffn_common_specs(seq, lambda t: t)
    nt = seq // tb

    def body(xb_ref, wg_ref, wv_ref, tg_ref, tv_ref, bg_ref, bv_ref, wd_ref, x_ref, g_ref, b_ref, tgt_ref,
             u_ref, h_ref, dz_ref, dg_ref, db_ref, loss_ref, dzb_ref, acc, carry):
        t, j = pl.program_id(0), pl.program_id(1)
        xb = xb_ref[...]
        pieces = [pl.ds(off, width) for off, width in FFN_PIECES]
        ug = [_dg(xb, wg_ref[:, cols], 1, 0) for cols in pieces]
        uv = [_dg(xb, wv_ref[:, cols], 1, 0) for cols in pieces]
        hs = []
        for cols, g, v in zip(pieces, ug, uv):
            u_ref[0, :, cols] = g
            u_ref[1, :, cols] = v
            halo_g = jnp.where(t == 0, 0.0, carry[j, 0, :, cols])
            halo_v = jnp.where(t == 0, 0.0, carry[j, 1, :, cols])
            h = _ffn_mid(halo_g, g, halo_v, v, *_conv_params(tg_ref, bg_ref, cols),
                         *_conv_params(tv_ref, bv_ref, cols)).astype(BF16)
            carry[j, 0, :, cols] = g[tb - SUBLANES:, :]
            carry[j, 1, :, cols] = v[tb - SUBLANES:, :]
            h_ref[:, cols] = h
            hs.append(h)
        part = None
        for cols, h in zip(pieces, hs):
            p = _dg(h, wd_ref[cols, :], 1, 0)
            part = p if part is None else part + p

        @pl.when(j == 0)
        def _():
            acc[...] = part

        @pl.when(j > 0)
        def _():
            acc[...] += part

        @pl.when(j == FFN_J - 1)
        def _():
            y, vjp = jax.vjp(_layer_norm, acc[...] + ALPHA * x_ref[...], g_ref[...], b_ref[...])
            err = y - tgt_ref[...]
            part_loss = 0.5 * jnp.sum(jnp.sum(err * err, axis=1, keepdims=True), axis=0, keepdims=True) / D_MODEL
            dz, dg, db = vjp(err / D_MODEL)

            @pl.when(t == 0)
            def _():
                for r in (dg_ref, db_ref, loss_ref):
                    r[...] = jnp.zeros_like(r)

            dz_ref[...] = dz
            dzb_ref[...] = dz.astype(BF16)
            dg_ref[...] += dg
            db_ref[...] += db
            loss_ref[...] += jnp.broadcast_to(part_loss, (1, LANES))

    h0, h1 = halves
    row = jax.ShapeDtypeStruct((1, D_MODEL), F32)
    return _pcall(
        body, name="ffn_fwd", grid=(nt, FFN_J),
        in_specs=[full, h0["w_up"], h1["w_up"], h0["taps"], h1["taps"], h0["bias"], h1["bias"], wd_spec, full, vec, vec,
                  full],
        out_specs=[u_blk, pl.BlockSpec((tb, FFN_W), lambda t, j: (t, j)), full, vec, vec,
                   pl.BlockSpec((1, LANES), lambda t, j: (0, 0)), full],
        out_shape=[jax.ShapeDtypeStruct((2, seq, D_FF), F32), jax.ShapeDtypeStruct((seq, D_FF), BF16),
                   jax.ShapeDtypeStruct((seq, D_MODEL), F32), row, row, jax.ShapeDtypeStruct((1, LANES), F32),
                   jax.ShapeDtypeStruct((seq, D_MODEL), BF16)],
        scratch_shapes=[pltpu.VMEM((tb, D_MODEL), F32), pltpu.VMEM((FFN_J, 2, SUBLANES, FFN_W), F32)],
        compiler_params=_params(("arbitrary", "arbitrary"), _ffn_vmem(tb)),
    )(x2b, w_up, w_up, conv_w, conv_w, conv_b, conv_b, w_down, x2, ln_g, ln_b, target)


def _ffn_bwd(u, conv_w, conv_b, dz3b, dz3, w_down, w_up, z2, ln_g, ln_b):
    seq = dz3.shape[0]
    tb = min(FFN_TB, seq)
    nt = seq // tb
    row8 = tb // SUBLANES
    tb, full, vec, halves, wd_spec, u_blk = _ffn_common_specs(seq, lambda t: nt - 1 - t)
    halo = pl.BlockSpec((2, SUBLANES, FFN_W), lambda t, j: (0, jnp.maximum((nt - 1 - t) * row8 - 1, 0), j))

    def body(u_ref, halo_ref, tg_ref, tv_ref, bg_ref, bv_ref, dzb_ref, wd_ref, wg_ref, wv_ref, dz3_ref, z_ref, g_ref,
             b_ref, du_ref, dw_ref, dbias_ref, dz_ref, dg_ref, db_ref, dz2b_ref, acc, carry):
        t, j = pl.program_id(0), pl.program_id(1)

        @pl.when((t == 0) & (j == 0))
        def _():
            for r in (dw_ref, dbias_ref, dg_ref, db_ref):
                r[...] = jnp.zeros_like(r)

        pieces = [pl.ds(off, width) for off, width in FFN_PIECES]
        dzb = dzb_ref[...]
        dhs = [_dg(dzb, wd_ref[cols, :], 1, 1) for cols in pieces]
        first = t == nt - 1
        dus = []
        for cols, dh in zip(pieces, dhs):
            args = (jnp.where(first, 0.0, halo_ref[0, :, cols]), u_ref[0, :, cols],
                    jnp.where(first, 0.0, halo_ref[1, :, cols]), u_ref[1, :, cols],
                    *_conv_params(tg_ref, bg_ref, cols), *_conv_params(tv_ref, bv_ref, cols))
            _, vjp = jax.vjp(_ffn_mid, *args)
            dhg, dxg, dhv, dxv, g0, g1, g2, gb, v0, v1, v2, vb = vjp(dh)
            zeros = jnp.zeros((tb - SUBLANES, dh.shape[1]), F32)
            dug = (dxg + jnp.concatenate([zeros, jnp.where(t == 0, 0.0, carry[j, 0, :, cols])], axis=0)).astype(BF16)
            duv = (dxv + jnp.concatenate([zeros, jnp.where(t == 0, 0.0, carry[j, 1, :, cols])], axis=0)).astype(BF16)
            carry[j, 0, :, cols] = dhg
            carry[j, 1, :, cols] = dhv
            du_ref[0, :, cols] = dug
            du_ref[1, :, cols] = duv
            for half, parts in enumerate(((g0, g1, g2), (v0, v1, v2))):
                for d, p in enumerate(parts):
                    dw_ref[j, half, d:d + 1, cols] += p
            dbias_ref[j, 0, :, cols] += gb
            dbias_ref[j, 1, :, cols] += vb
            dus.append((dug, duv))
        part = None
        for cols, (dug, duv) in zip(pieces, dus):
            p = _dg(dug, wg_ref[:, cols], 1, 1) + _dg(duv, wv_ref[:, cols], 1, 1)
            part = p if part is None else part + p

        @pl.when(j == 0)
        def _():
            acc[...] = part

        @pl.when(j > 0)
        def _():
            acc[...] += part

        @pl.when(j == FFN_J - 1)
        def _():
            _, ln_vjp = jax.vjp(_layer_norm, z_ref[...], g_ref[...], b_ref[...])
            dz, dg, db = ln_vjp(acc[...] + ALPHA * dz3_ref[...])
            dz_ref[...] = dz
            dz2b_ref[...] = dz.astype(BF16)
            dg_ref[...] += dg
            db_ref[...] += db

    h0, h1 = halves
    row = jax.ShapeDtypeStruct((1, D_MODEL), F32)
    whole = lambda *shape: pl.BlockSpec(shape, lambda t, j: (0,) * len(shape))
    return _pcall(
        body, name="ffn_bwd", grid=(nt, FFN_J),
        in_specs=[u_blk, halo, h0["taps"], h1["taps"], h0["bias"], h1["bias"], full, wd_spec, h0["w_up"], h1["w_up"],
                  full, full, vec, vec],
        out_specs=[u_blk, whole(FFN_J, 2, FFN_CONV, FFN_W), whole(FFN_J, 2, 1, FFN_W), full, vec, vec, full],
        out_shape=[jax.ShapeDtypeStruct((2, seq, D_FF), BF16), jax.ShapeDtypeStruct((FFN_J, 2, FFN_CONV, FFN_W), F32),
                   jax.ShapeDtypeStruct((FFN_J, 2, 1, FFN_W), F32), jax.ShapeDtypeStruct((seq, D_MODEL), F32), row, row,
                   jax.ShapeDtypeStruct((seq, D_MODEL), BF16)],
        scratch_shapes=[pltpu.VMEM((tb, D_MODEL), F32), pltpu.VMEM((FFN_J, 2, SUBLANES, FFN_W), F32)],
        compiler_params=_params(("arbitrary", "arbitrary"), _ffn_vmem(tb)),
    )(u, u, conv_w, conv_w, conv_b, conv_b, dz3b, w_down, w_up, w_up, dz3, z2, ln_g, ln_b)


def _adamw_math(w, g, m, v):
    m_new = ADAM_B1 * m + (1.0 - ADAM_B1) * g
    v_new = ADAM_B2 * v + (1.0 - ADAM_B2) * jnp.square(g)
    m_hat = m_new / (1.0 - ADAM_B1 ** ADAM_STEP)
    v_hat = v_new / (1.0 - ADAM_B2 ** ADAM_STEP)
    return -ADAM_LR * (m_hat / (jnp.sqrt(v_hat) + ADAM_EPS) + ADAM_WD * w), m_new, v_new


def _adamw(name, w, g, m, v):
    rows, cols = w.shape
    tr = _tile(rows, (256, 176, 128, 64, 40, 32, 16, 8))

    def body(w_ref, g_ref, m_ref, v_ref, d_ref, nm_ref, nv_ref):
        d_ref[...], nm_ref[...], nv_ref[...] = _adamw_math(w_ref[...], g_ref[...], m_ref[...], v_ref[...])

    spec = pl.BlockSpec((tr, cols), lambda i: (i, 0))
    sh = jax.ShapeDtypeStruct((rows, cols), F32)
    return _pcall(
        body, name=name, grid=(rows // tr,), in_specs=[spec] * 4, out_specs=[spec] * 3, out_shape=[sh] * 3,
        compiler_params=_params(("arbitrary",), 14 * _nbytes((tr, -(-cols // LANES) * LANES), F32)),
    )(w, g, m, v)


def _adamw_halves(name, core, w, mine, theirs, m, v):
    rows, cols = w.shape
    half_rows = mine.shape[0]
    tr = _tile(half_rows, (256, 176, 128))
    nbh = half_rows // tr
    assert 2 * half_rows == rows

    def body(c_ref, w_ref, a_ref, b_ref, m_ref, v_ref, g_ref, d_ref, nm_ref, nv_ref):
        g = jnp.where(pl.program_id(0) // nbh == c_ref[0], a_ref[...], b_ref[...])
        g_ref[...] = g
        d_ref[...], nm_ref[...], nv_ref[...] = _adamw_math(w_ref[...], g, m_ref[...], v_ref[...])

    spec = pl.BlockSpec((tr, cols), lambda i, c_ref: (i, 0))
    half = pl.BlockSpec((tr, cols), lambda i, c_ref: (i % nbh, 0))
    sh = jax.ShapeDtypeStruct((rows, cols), F32)
    grid_spec = pltpu.PrefetchScalarGridSpec(
        num_scalar_prefetch=1, grid=(rows // tr,), in_specs=[spec, half, half, spec, spec], out_specs=[spec] * 4)
    return _pcall(
        body, name=name, grid_spec=grid_spec, out_shape=[sh] * 4,
        compiler_params=_params(("arbitrary",), 18 * _nbytes((tr, -(-cols // LANES) * LANES), F32)),
    )(core, w, mine, theirs, m, v)


MESH = pl.DeviceIdType.MESH
ANY = pl.BlockSpec(memory_space=pl.ANY)
N_CHIPS = 4
N_DEV = 8
BF16_ROWS = 16


def _me():
    return lax.axis_index("x"), lax.axis_index("y"), lax.axis_index("c")


def _other_chips(x, y):
    return [(1 - x, y), (x, 1 - y), (1 - x, 1 - y)]


def _remote(src, dst, ssem, rsem, dev):
    return pltpu.make_async_remote_copy(src_ref=src, dst_ref=dst, send_sem=ssem, recv_sem=rsem,
                                        device_id=dev, device_id_type=MESH)


def _half_rows(ref_rows, cc):
    half = ref_rows // 2
    return pl.ds(pl.multiple_of(cc * half, BF16_ROWS), half)


def _gather_weights(shards):
    n = len(shards)
    n_ici = n * (N_CHIPS - 1)

    def body(*refs):
        ins, outs, (ssem, rsem, lsem, lrsem) = refs[:n], refs[n:2 * n], refs[2 * n:]
        x, y, c = _me()
        k_me = 2 * x + y
        sib = (x, y, 1 - c)
        chips = _other_chips(x, y)
        started = []
        for i, (w_ref, o_ref) in enumerate(zip(ins, outs)):
            cp = _remote(w_ref, o_ref.at[k_me], lsem.at[i], lrsem.at[i], sib)
            cp.start()
            started.append(cp)
        for r, (px, py) in enumerate(chips):
            for i, (w_ref, o_ref) in enumerate(zip(ins, outs)):
                rows = _half_rows(w_ref.shape[0], c)
                s = r * n + i
                cp = _remote(w_ref.at[rows], o_ref.at[k_me, rows], ssem.at[s], rsem.at[s], (px, py, c))
                cp.start()
                started.append(cp)
        for r, (px, py) in enumerate(chips):
            for i, o_ref in enumerate(outs):
                blk = o_ref.at[2 * px + py, _half_rows(o_ref.shape[1], c)]
                s = r * n + i
                _remote(blk, blk, ssem.at[s], rsem.at[s], (px, py, c)).wait_recv()
                cp = _remote(blk, blk, ssem.at[n_ici + s], rsem.at[n_ici + s], sib)
                cp.start()
                started.append(cp)
        for r, (px, py) in enumerate(chips):
            for i, o_ref in enumerate(outs):
                blk = o_ref.at[2 * px + py, _half_rows(o_ref.shape[1], 1 - c)]
                s = n_ici + r * n + i
                _remote(blk, blk, ssem.at[s], rsem.at[s], sib).wait_recv()
        for cp in started[n:]:
            cp.wait_send()
        for cp in started[:n]:
            cp.wait()

    return _pcall(
        body, name="gather_weights", in_specs=[ANY] * n, out_specs=[ANY] * n,
        out_shape=[jax.ShapeDtypeStruct((N_CHIPS,) + s.shape, s.dtype) for s in shards],
        scratch_shapes=[pltpu.SemaphoreType.DMA((2 * n_ici,)), pltpu.SemaphoreType.DMA((2 * n_ici,)),
                        pltpu.SemaphoreType.DMA((n,)), pltpu.SemaphoreType.DMA((n,))],
    )(*shards)


def _swap_halves(name, grads):
    n = len(grads)

    def body(*refs):
        ins, outs, (ssem, rsem) = refs[:n], refs[n:2 * n], refs[2 * n:]
        x, y, c = _me()
        copies = []
        for i, (g_ref, o_ref) in enumerate(zip(ins, outs)):
            for k in range(N_CHIPS):
                s = i * N_CHIPS + k
                cp = _remote(g_ref.at[k, _half_rows(g_ref.shape[1], 1 - c)], o_ref.at[k], ssem.at[s], rsem.at[s],
                             (x, y, 1 - c))
                cp.start()
                copies.append(cp)
        for cp in copies:
            cp.wait()

    return _pcall(
        body, name=name, in_specs=[ANY] * n, out_specs=[ANY] * n,
        out_shape=[jax.ShapeDtypeStruct((N_CHIPS, g.shape[1] // 2, g.shape[2]), g.dtype) for g in grads],
        scratch_shapes=[pltpu.SemaphoreType.DMA((n * N_CHIPS,)), pltpu.SemaphoreType.DMA((n * N_CHIPS,))],
    )(*grads)


SEM = pl.BlockSpec(memory_space=pltpu.SEMAPHORE)
IN_HBM = pl.BlockSpec(memory_space=pltpu.HBM)
SPLIT_PARAMS = dict(compiler_params=pltpu.CompilerParams(has_side_effects=pltpu.SideEffectType.DATAFLOW_SIDE_EFFECTING))


def _split_start(name, sources, landings, n_copies, plan):
    ns, nl = len(sources), len(landings)

    def body(*refs):
        ins, lands, (ssem, rsem), token = refs[:ns], refs[ns:ns + nl], refs[ns + nl:ns + nl + 2], refs[-1]
        for s, (src, dst, _, dev) in enumerate(plan(ins, lands)):
            _remote(src, dst, ssem.at[s], rsem.at[s], dev).start()
        token[...] = jnp.zeros_like(token)

    arrays = list(sources) + list(landings)
    outs = _call(
        body, name=name, in_specs=[IN_HBM] * (ns + nl),
        out_specs=[SEM, SEM] + [IN_HBM] * (ns + nl) + [pl.BlockSpec(memory_space=pltpu.VMEM)],
        out_shape=[pltpu.SemaphoreType.DMA((n_copies,)), pltpu.SemaphoreType.DMA((n_copies,))]
        + [pltpu.HBM(a.shape, a.dtype) for a in arrays] + [jax.ShapeDtypeStruct((SUBLANES, LANES), F32)],
        input_output_aliases={i: 2 + i for i in range(ns + nl)}, **SPLIT_PARAMS,
    )(*[pltpu.with_memory_space_constraint(a, pltpu.HBM) for a in arrays])
    return (outs[:-1], ns), outs[-1]


def _split_wait(name, handle, after, plan):
    (ssem, rsem, *thru), ns = handle
    nl = len(thru) - ns

    def body(*refs):
        ins, lands, (ssem_ref, rsem_ref) = refs[:ns], refs[ns:ns + nl], refs[ns + nl:ns + nl + 2]
        for s, (src, _, dst, dev) in enumerate(plan(ins, lands)):
            cp = _remote(src, dst, ssem_ref.at[s], rsem_ref.at[s], dev)
            cp.wait_send()
            cp.wait_recv()

    outs = _call(
        body, name=name, in_specs=[IN_HBM] * (ns + nl) + [SEM, SEM, ANY], out_specs=[IN_HBM] * (ns + nl),
        out_shape=[pltpu.HBM(t.shape, t.dtype) for t in thru],
        input_output_aliases={i: i for i in range(ns + nl)}, **SPLIT_PARAMS,
    )(*thru, ssem, rsem, after)
    return outs[:ns], outs[ns:]


def _swap_plan(ins, lands):
    x, y, c = _me()
    return [(g_ref.at[k, _half_rows(g_ref.shape[1], 1 - c)], l_ref.at[k], l_ref.at[k], (x, y, 1 - c))
            for g_ref, l_ref in zip(ins, lands) for k in range(N_CHIPS)]


def _swap_start(name, grads):
    lands = [lax.empty((N_CHIPS, g.shape[1] // 2, g.shape[2]), g.dtype) for g in grads]
    return _split_start(name, grads, lands, len(grads) * N_CHIPS, _swap_plan)


def _swap_wait(name, handle, after):
    return _split_wait(name, handle, after, _swap_plan)


def _gather_plan(ins, lands):
    x, y, c = _me()
    k_me = 2 * x + y
    plan = [(w_ref, l_ref.at[k_me], l_ref.at[k_me], (x, y, 1 - c)) for w_ref, l_ref in zip(ins, lands)]
    for px, py in _other_chips(x, y):
        for w_ref, l_ref in zip(ins, lands):
            rows = _half_rows(w_ref.shape[0], c)
            plan.append((w_ref.at[rows], l_ref.at[k_me, rows], l_ref.at[2 * px + py, rows], (px, py, c)))
    return plan


def _gather_start(name, shards):
    lands = [lax.empty((N_CHIPS,) + s.shape, s.dtype) for s in shards]
    return _split_start(name, shards, lands, len(shards) * N_CHIPS, _gather_plan)


def _gather_wait(name, handle, after):
    return _split_wait(name, handle, after, _gather_plan)[1]


def _forward_halves(name, blocks):
    n = len(blocks)
    n_sem = n * (N_CHIPS - 1)

    def body(*refs):
        outs, (ssem, rsem) = refs[n:2 * n], refs[2 * n:]
        x, y, c = _me()
        sib = (x, y, 1 - c)
        chips = _other_chips(x, y)
        sends = []
        for r, (px, py) in enumerate(chips):
            for i, o_ref in enumerate(outs):
                blk = o_ref.at[2 * px + py, _half_rows(o_ref.shape[1], c)]
                cp = _remote(blk, blk, ssem.at[r * n + i], rsem.at[r * n + i], sib)
                cp.start()
                sends.append(cp)
        for r, (px, py) in enumerate(chips):
            for i, o_ref in enumerate(outs):
                blk = o_ref.at[2 * px + py, _half_rows(o_ref.shape[1], 1 - c)]
                _remote(blk, blk, ssem.at[r * n + i], rsem.at[r * n + i], sib).wait_recv()
        for cp in sends:
            cp.wait_send()

    return _pcall(
        body, name=name, in_specs=[ANY] * n, out_specs=[ANY] * n,
        out_shape=[jax.ShapeDtypeStruct(b.shape, b.dtype) for b in blocks],
        input_output_aliases={i: i for i in range(n)},
        scratch_shapes=[pltpu.SemaphoreType.DMA((n_sem,)), pltpu.SemaphoreType.DMA((n_sem,))],
    )(*blocks)


def _scatter_plan(ins, lands):
    x, y, c = _me()
    k_me = 2 * x + y
    return [(p_ref.at[2 * px + py], l_ref.at[k_me], l_ref.at[2 * px + py], (px, py, c))
            for px, py in _other_chips(x, y) for p_ref, l_ref in zip(ins, lands)]


def _scatter_start(name, parts):
    lands = [lax.empty(p.shape, p.dtype) for p in parts]
    return _split_start(name, parts, lands, len(parts) * (N_CHIPS - 1), _scatter_plan)


def _scatter_wait(name, handle, after):
    return _split_wait(name, handle, after, _scatter_plan)[1]


def _share_halves(halves):
    n = len(halves)

    def body(*refs):
        ins, outs, (ssem, rsem) = refs[:n], refs[n:2 * n], refs[2 * n:]
        x, y, c = _me()
        copies = [_remote(r_ref, o_ref, ssem.at[i], rsem.at[i], (x, y, 1 - c))
                  for i, (r_ref, o_ref) in enumerate(zip(ins, outs))]
        for cp in copies:
            cp.start()
        for cp in copies:
            cp.wait()

    return _pcall(
        body, name="share_halves", in_specs=[ANY] * n, out_specs=[ANY] * n,
        out_shape=[jax.ShapeDtypeStruct(h.shape, h.dtype) for h in halves],
        scratch_shapes=[pltpu.SemaphoreType.DMA((n,)), pltpu.SemaphoreType.DMA((n,))],
    )(*halves)


def _exchange_small(v, reduce):
    rows = v.shape[0]

    def body(v_ref, out_ref, buf, ssem, rsem):
        x, y, c = _me()
        me = 4 * x + 2 * y + c
        peers = [((x + bx) % 2, (y + by) % 2, (c + bc) % 2)
                 for bx in (0, 1) for by in (0, 1) for bc in (0, 1) if (bx, by, bc) != (0, 0, 0)]
        dst = buf if reduce else out_ref
        dst[me] = v_ref[...]
        sends = [_remote(v_ref, dst.at[me], ssem.at[r], rsem.at[r], p) for r, p in enumerate(peers)]
        for cp in sends:
            cp.start()
        for r, (px, py, pc) in enumerate(peers):
            blk = dst.at[4 * px + 2 * py + pc]
            _remote(blk, blk, ssem.at[r], rsem.at[r], (px, py, pc)).wait_recv()
        if reduce:
            acc = buf[0]
            for d in range(1, N_DEV):
                acc = acc + buf[d]
            out_ref[...] = acc
        for cp in sends:
            cp.wait_send()

    vm = pl.BlockSpec(memory_space=pltpu.VMEM)
    out_shape = jax.ShapeDtypeStruct((rows, LANES) if reduce else (N_DEV, rows, LANES), F32)
    buf_shape = (N_DEV, rows, LANES) if reduce else (SUBLANES, LANES)
    return _pcall(
        body, pin=False, name="reduce_small" if reduce else "gather_small", in_specs=[vm], out_specs=vm, out_shape=out_shape,
        scratch_shapes=[pltpu.VMEM(buf_shape, F32), pltpu.SemaphoreType.DMA((N_DEV - 1,)),
                        pltpu.SemaphoreType.DMA((N_DEV - 1,))],
        compiler_params=pltpu.CompilerParams(vmem_limit_bytes=32 * 1024 * 1024),
    )(v)


def _add_pair(name, core, chip, g, theirs):
    _, half, cols = theirs.shape
    tr = _tile(half, (256, 176, 128))
    nb = half // tr

    def body(c_ref, k_ref, g_ref, t_ref, o32_ref, o16_ref):
        s = g_ref[...] + t_ref[...]
        o16_ref[...] = s.astype(BF16)

        @pl.when(pl.program_id(1) == k_ref[0])
        def _():
            o32_ref[...] = s

    spec = pl.BlockSpec((None, tr, cols), lambda i, k, c_ref, k_ref: (k, i, 0))
    grid_spec = pltpu.PrefetchScalarGridSpec(
        num_scalar_prefetch=2, grid=(nb, N_CHIPS),
        in_specs=[pl.BlockSpec((None, tr, cols), lambda i, k, c_ref, k_ref: (k, c_ref[0] * nb + i, 0)), spec],
        out_specs=[pl.BlockSpec((tr, cols), lambda i, k, c_ref, k_ref: (i, 0)), spec])
    return _pcall(
        body, name=name, grid_spec=grid_spec,
        out_shape=[jax.ShapeDtypeStruct((half, cols), F32), jax.ShapeDtypeStruct(theirs.shape, BF16)],
        compiler_params=_params(("arbitrary", "arbitrary"), 8 * _nbytes((tr, cols + LANES), F32)),
    )(core, chip, g, theirs)


def _add_chips(name, chip, p32, recv):
    half, cols = p32.shape
    tr = _tile(half, (256, 176, 128))

    def body(k_ref, p_ref, r0_ref, r1_ref, r2_ref, o_ref):
        o_ref[...] = ((p_ref[...] + r0_ref[...].astype(F32)) + r1_ref[...].astype(F32)) + r2_ref[...].astype(F32)

    def other(r):
        return pl.BlockSpec((None, tr, cols), lambda i, k_ref: (r + (k_ref[0] <= r).astype(jnp.int32), i, 0))
    grid_spec = pltpu.PrefetchScalarGridSpec(
        num_scalar_prefetch=1, grid=(half // tr,),
        in_specs=[pl.BlockSpec((tr, cols), lambda i, k_ref: (i, 0)), other(0), other(1), other(2)],
        out_specs=pl.BlockSpec((tr, cols), lambda i, k_ref: (i, 0)))
    return _pcall(
        body, name=name, grid_spec=grid_spec, out_shape=jax.ShapeDtypeStruct((half, cols), F32),
        compiler_params=_params(("arbitrary",), 10 * _nbytes((tr, cols + LANES), F32)),
    )(chip, p32, recv, recv, recv)


def kernel(x, mem, w_in, b_in, hg_lb_logits, hg_norm_w, ml_conv_w, ml_conv_b, ml_norm_w, w_out, ln1_g, ln1_b, ca_wq, ca_wkv, ca_wo, ln2_g, ln2_b, ffn_w_up, ffn_conv_w, ffn_conv_b, ffn_w_down, ln3_g, ln3_b, loss_target, m_w_in, m_b_in, m_hg_lb_logits, m_hg_norm_w, m_ml_conv_w, m_ml_conv_b, m_ml_norm_w, m_w_out, m_ln1_g, m_ln1_b, m_ca_wq, m_ca_wkv, m_ca_wo, m_ln2_g, m_ln2_b, m_ffn_w_up, m_ffn_conv_w, m_ffn_conv_b, m_ffn_w_down, m_ln3_g, m_ln3_b, v_w_in, v_b_in, v_hg_lb_logits, v_hg_norm_w, v_ml_conv_w, v_ml_conv_b, v_ml_norm_w, v_w_out, v_ln1_g, v_ln1_b, v_ca_wq, v_ca_wkv, v_ca_wo, v_ln2_g, v_ln2_b, v_ffn_w_up, v_ffn_conv_w, v_ffn_conv_b, v_ffn_w_down, v_ln3_g, v_ln3_b):
    return _train_step(dict(locals()))


WEIGHTS = ("w_in", "b_in", "hg_lb_logits", "hg_norm_w", "ml_conv_w", "ml_conv_b", "ml_norm_w", "w_out", "ln1_g",
           "ln1_b", "ca_wq", "ca_wkv", "ca_wo", "ln2_g", "ln2_b", "ffn_w_up", "ffn_conv_w", "ffn_conv_b",
           "ffn_w_down", "ln3_g", "ln3_b")
MATRICES = ("w_in", "w_out", "ca_wq", "ca_wkv", "ca_wo", "ffn_w_up", "ffn_w_down")
COL_SHARDED = ("w_in", "ca_wkv", "ffn_w_up", "ml_conv_w", "ffn_conv_w")
SMALL = tuple(n for n in WEIGHTS if n not in MATRICES)
PART_ROWS = 16


def _part_rows(shape):
    n = 1
    for s in shape:
        n *= s
    return -(-n // (LANES * PART_ROWS)) * PART_ROWS


def _pack(arrs, dtype):
    parts = []
    for a in arrs:
        flat = a.reshape(-1).astype(dtype)
        flat = jnp.pad(flat, (0, _part_rows(a.shape) * LANES - flat.shape[0]))
        parts.append(flat.reshape(-1, LANES))
    return jnp.concatenate(parts, axis=0)


def _unpack(buf, shapes):
    lead = buf.shape[:-2]
    outs, r = [], 0
    for sh in shapes:
        n = 1
        for s in sh:
            n *= s
        nr = _part_rows(sh)
        flat = buf[..., r:r + nr, :].reshape(lead + (nr * LANES,))
        outs.append(flat[..., :n].reshape(lead + tuple(sh)))
        r += nr
    return outs


def _cat_cols(s):
    return jnp.moveaxis(s, 0, 1).reshape(s.shape[1], -1)


def _stack_rows(s):
    return s.reshape(-1, s.shape[-1])


def _train_step(a):
    xs, mems, tgt = a["x"][0], a["mem"][0], a["loss_target"][0]
    core = lax.axis_index("c").astype(jnp.int32).reshape(1)
    chip = (2 * lax.axis_index("x") + lax.axis_index("y")).astype(jnp.int32).reshape(1)
    k_me = chip[0]
    shard = {n: a[n][0] for n in MATRICES}

    later = [n for n in MATRICES if n != "w_in"]
    taps = _exchange_small(_pack([a["ml_conv_w"][0], a["ffn_conv_w"][0]], F32), reduce=False)
    w = {"w_in": jnp.pad(_cat_cols(_gather_weights([shard["w_in"].astype(BF16)])[0]), ((0, 0), (0, D_IN_PAD - D_IN)))}
    gathering, token = _gather_start("gather_start", [shard[n].astype(BF16) for n in later])
    taps = taps.reshape((N_CHIPS, 2) + taps.shape[1:])[:, 0]
    ml_cw, ffn_cw = [_cat_cols(s) for s in _unpack(taps, [a["ml_conv_w"].shape[1:], a["ffn_conv_w"].shape[1:]])]
    b_in_p = jnp.pad(a["b_in"], ((0, 0), (0, D_IN_PAD - D_IN))) + token[0:1, 0:1]
    mixer_w = (a["hg_lb_logits"], a["hg_norm_w"], ml_cw, a["ml_conv_b"], a["ml_norm_w"])
    up_cols = a["ffn_w_up"].shape[-1]

    xb = xs.astype(BF16)
    proj = _mm("proj", "nn", xb, w["w_in"], bias=b_in_p, tm=256, tn=D_IN_PAD)
    y, hst, cst, nst, mst = _mixer_fwd(proj, *mixer_w)
    w.update(zip(later, _forward_halves("forward_halves", _gather_wait("gather_wait", gathering, y))))
    for n in ("w_out", "ca_wq", "ca_wo", "ffn_w_down"):
        w[n] = _stack_rows(w[n])
    z1, x1, x1b = _mm("mix_out", "nn", y, w["w_out"], res=xs, res_scale=ALPHA, ln=("fwd", a["ln1_g"], a["ln1_b"]),
                      copy_dtype=BF16)
    q = _mm("ca_q", "nn", x1b, w["ca_wq"], out_dtype=BF16, tn=D_MODEL)
    kv = _mm("ca_kv", "nn", mems, w["ca_wkv"])
    o = _attn_fwd(q, kv)
    z2, x2, x2b = _mm("ca_out", "nn", o, w["ca_wo"], res=x1, res_scale=ALPHA, ln=("fwd", a["ln2_g"], a["ln2_b"]),
                      copy_dtype=BF16)
    w_up = w["ffn_w_up"]
    assert w_up.shape == (2 * FFN_J, D_MODEL, FFN_W)
    u, hmid, dz3, g_ln3g, g_ln3b, loss_part, dz3b = _ffn_fwd(
        x2b, x2, w_up, ffn_cw, a["ffn_conv_b"], w["ffn_w_down"], a["ln3_g"], a["ln3_b"], tgt)

    grads = {"ln3_g": g_ln3g, "ln3_b": g_ln3b}
    grads["ffn_w_down"] = _mm("g_w_down", "tn", hmid, dz3b, tm=D_FF // 2, tn=D_MODEL)
    du, g_cw, g_cb, dz2, grads["ln2_g"], grads["ln2_b"], dz2b = _ffn_bwd(
        u, ffn_cw, a["ffn_conv_b"], dz3b, dz3, w["ffn_w_down"], w_up, z2, a["ln2_g"], a["ln2_b"])
    grads["ffn_conv_w"] = jnp.transpose(g_cw, (2, 1, 0, 3)).reshape(FFN_CONV, 2 * D_FF)
    grads["ffn_conv_b"] = jnp.transpose(g_cb, (2, 1, 0, 3)).reshape(1, 2 * D_FF)
    grads["ffn_w_up"] = _mm("g_w_up", "tn", x2b, du, out_groups=N_CHIPS, tm=D_MODEL, tn=up_cols)
    grads["ffn_w_down"] = grads["ffn_w_down"].reshape((N_CHIPS,) + shard["ffn_w_down"].shape)
    pending = {}

    def reduce_start(tag, names, swapped=None):
        group = [grads[n] for n in names]
        group, theirs = swapped or (group, _swap_halves("swap_halves_" + tag, group))
        sums = [_add_pair("add_pair_" + n, core, chip, g, t) for n, g, t in zip(names, group, theirs)]
        handle, token = _scatter_start("scatter_start_" + tag, [s16 for _, s16 in sums])
        pending[tag] = (names, [s32 for s32, _ in sums], handle)
        return token[0:1, 0:1]

    ffn = ("ffn_w_up", "ffn_w_down")
    swapping, token = _swap_start("swap_start_ffn", [grads[n] for n in ffn])
    do = _mm("d_o", "nt", dz2b, w["ca_wo"], bias=jnp.zeros((1, D_MODEL), F32) + token[0:1, 0:1], out_dtype=BF16,
             tn=D_MODEL)
    grads["ca_wo"] = _mm("g_wo", "tn", o, dz2b, tm=D_MODEL, tn=D_MODEL)
    zero = reduce_start("ffn", ffn, _swap_wait("swap_wait_ffn", swapping, grads["ca_wo"]))
    dq, dkv = _attn_bwd(q, kv + zero, do)
    grads["ca_wq"] = _mm("g_wq", "tn", x1b, dq, tm=D_MODEL, tn=D_MODEL)
    grads["ca_wkv"] = _mm("g_wkv", "tn", mems, dkv, out_groups=N_CHIPS, tm=D_MODEL)
    dz1, grads["ln1_g"], grads["ln1_b"], dz1b = _mm("d_x1", "nt", dq, w["ca_wq"], res=dz2, res_scale=ALPHA,
                                                    ln=("bwd", z1, a["ln1_g"], a["ln1_b"]), copy_dtype=BF16)
    grads["w_out"] = _mm("g_w_out", "tn", y, dz1b, tm=D_MODEL, tn=D_MODEL)
    for n in ("w_out", "ca_wq", "ca_wo"):
        grads[n] = grads[n].reshape((N_CHIPS,) + shard[n].shape)
    attn = ("w_out", "ca_wq", "ca_wkv", "ca_wo")
    swapping, token = _swap_start("swap_start_attn", [grads[n] for n in attn])
    dy = _mm("d_y", "nt", dz1b, w["w_out"], bias=jnp.zeros((1, D_MODEL), F32) + token[0:1, 0:1], tn=D_MODEL)
    zero = reduce_start("attn", attn, _swap_wait("swap_wait_attn", swapping, dy))
    (dproj, g_b_in, grads["hg_lb_logits"], grads["hg_norm_w"], grads["ml_conv_w"], grads["ml_conv_b"],
     grads["ml_norm_w"]) = _mixer_bwd(proj, dy, hst, cst, nst, mst, mixer_w[0], mixer_w[1] + zero, *mixer_w[2:])
    g_in = _mm("g_w_in", "tn", xb, dproj, tm=D_MODEL, tn=up_cols)[:, :D_IN]
    grads["w_in"] = jnp.moveaxis(g_in.reshape(D_MODEL, N_CHIPS, -1), 1, 0)
    grads["b_in"] = g_b_in[:, :D_IN]
    zero = reduce_start("in", ("w_in",))
    dx = _mm("d_x", "nt", dproj, w["w_in"], bias=jnp.zeros((1, D_MODEL), F32) + zero, res=dz1, res_scale=ALPHA,
             tm=256, tn=D_MODEL)

    halves = {}
    for tag, (names, sums32, handle) in pending.items():
        for n, s32, r in zip(names, sums32, _scatter_wait("scatter_wait_" + tag, handle, dx)):
            halves[n] = _add_chips("add_chips_" + n, chip, s32, r)
    halves = [halves[n] for n in MATRICES]
    other_halves = _share_halves(halves)

    small_shapes = [grads[n].shape for n in SMALL] + [loss_part.shape]
    summed = _unpack(_exchange_small(_pack([grads[n] for n in SMALL] + [loss_part], F32), reduce=True), small_shapes)
    loss = summed[-1][0, 0]
    for n, g in zip(SMALL, summed[:-1]):
        if n in COL_SHARDED:
            cols = a[n].shape[-1]
            g = lax.dynamic_slice_in_dim(g, k_me * cols, cols, axis=1)
        grads[n] = g

    delta, new_m, new_v = {}, {}, {}
    for n, mine, theirs in zip(MATRICES, halves, other_halves):
        grads[n], delta[n], new_m[n], new_v[n] = _adamw_halves(
            "adamw_" + n, core, shard[n], mine, theirs, a["m_" + n][0], a["v_" + n][0])
    small_w = [a[n][0] if a[n].ndim == 3 else a[n] for n in SMALL]
    small_m = [a["m_" + n][0] if a[n].ndim == 3 else a["m_" + n] for n in SMALL]
    small_v = [a["v_" + n][0] if a[n].ndim == 3 else a["v_" + n] for n in SMALL]
    shapes = [w.shape for w in small_w]
    packed = [_pack(l, F32) for l in (small_w, [grads[n] for n in SMALL], small_m, small_v)]
    for out, buf in zip((delta, new_m, new_v), _adamw("adamw_small", *packed)):
        for n, v in zip(SMALL, _unpack(buf, shapes)):
            out[n] = v

    def shaped(d):
        return [d[n].reshape(a[n].shape) for n in WEIGHTS]
    return (loss, dx[None], *shaped(grads), *shaped(delta), *shaped(new_m), *shaped(new_v))
```

```python
import functools

import jax
import jax.numpy as jnp
from jax import lax
from jax.experimental import pallas as pl
from jax.experimental.pallas import tpu as pltpu

F32 = jnp.float32
BF16 = jnp.bfloat16

D_MODEL = 1024
HEADS = 4
DK = 128
D_GRP = HEADS * DK
CHUNK = 64
ML_CONV = 4
FFN_CONV = 3
D_FF = 2816
CA_DH = D_MODEL // HEADS
DEPTH = 1
ALPHA = (2.0 * DEPTH) ** 0.25
LN_EPS = 1e-5
NEG_BIG = -1e30
D_IN = 8 * D_GRP + 2 * HEADS
D_IN_PAD = 8 * D_GRP + 128
ADAM_LR, ADAM_B1, ADAM_B2, ADAM_EPS, ADAM_WD, ADAM_STEP = 0.001, 0.9, 0.999, 1e-08, 0.01, 10

SUBLANES = 8
LANES = 128
VMEM_BYTES = 64 * 1024 * 1024


def _pcall(body, pin=True, **kw):
    if not pin:
        return _call(body, **kw)
    kw["out_shape"] = jax.tree.map(lambda s: pltpu.HBM(s.shape, s.dtype), kw["out_shape"])
    call = _call(body, **kw)

    def pinned(*args):
        return call(*[pltpu.with_memory_space_constraint(x, pltpu.HBM) if jnp.issubdtype(x.dtype, jnp.floating) else x
                      for x in args])
    return pinned


def _call(body, **kw):
    return pl.pallas_call(body, **kw)


def _params(semantics, vmem_bytes):
    limit = int(min(max(2 * vmem_bytes, 16 * 1024 * 1024), VMEM_BYTES - 8 * 1024 * 1024))
    return pltpu.CompilerParams(dimension_semantics=semantics, vmem_limit_bytes=limit)


def _nbytes(shape, dtype):
    n = 1
    for s in shape:
        n *= s
    return n * jnp.dtype(dtype).itemsize


def _dg(a, b, ca, cb):
    return lax.dot_general(a.astype(BF16), b.astype(BF16), (((ca,), (cb,)), ((), ())),
                           preferred_element_type=F32)


@jax.custom_vjp
def mm_nn(a, b):
    return _dg(a, b, 1, 0)


mm_nn.defvjp(lambda a, b: (_dg(a, b, 1, 0), (a, b)),
             lambda r, g: (_dg(g, r[1], 1, 1).astype(r[0].dtype), _dg(r[0], g, 0, 0).astype(r[1].dtype)))


@jax.custom_vjp
def mm_nt(a, b):
    return _dg(a, b, 1, 1)


mm_nt.defvjp(lambda a, b: (_dg(a, b, 1, 1), (a, b)),
             lambda r, g: (_dg(g, r[1], 1, 0).astype(r[0].dtype), _dg(g, r[0], 0, 0).astype(r[1].dtype)))


@jax.custom_vjp
def mm_tn(a, b):
    return _dg(a, b, 0, 0)


mm_tn.defvjp(lambda a, b: (_dg(a, b, 0, 0), (a, b)),
             lambda r, g: (_dg(r[1], g, 1, 1).astype(r[0].dtype), _dg(r[0], g, 1, 0).astype(r[1].dtype)))


def _tri(n, lower):
    r = lax.broadcasted_iota(jnp.int32, (n, n), 0)
    c = lax.broadcasted_iota(jnp.int32, (n, n), 1)
    return ((r >= c) if lower else (r <= c)).astype(F32)


def _tri_dot(lower, x):
    t = _tri(x.shape[0], lower).astype(BF16)
    hi = x.astype(BF16)
    rest = x - hi.astype(F32)
    mid = rest.astype(BF16)
    lo = (rest - mid.astype(F32)).astype(BF16)
    return sum(lax.dot_general(t, p, (((1,), (0,)), ((), ())), preferred_element_type=F32) for p in (hi, mid, lo))


@jax.custom_vjp
def cumsum_rows(x):
    return _tri_dot(True, x)


cumsum_rows.defvjp(lambda x: (_tri_dot(True, x), None), lambda _, g: (_tri_dot(False, g),))


def _shift_impl(halo, x, d):
    xx = jnp.concatenate([halo, x], axis=0)
    return pltpu.roll(xx, d, 0)[SUBLANES:]


@functools.partial(jax.custom_vjp, nondiff_argnums=(2,))
def shift_rows(halo, x, d):
    return _shift_impl(halo, x, d)


def _shift_bwd(d, _, g):
    n = g.shape[0] + SUBLANES
    gg = jnp.concatenate([jnp.zeros((SUBLANES, g.shape[1]), g.dtype), g], axis=0)
    r = pltpu.roll(gg, n - d, 0)
    return r[:SUBLANES], r[SUBLANES:]


shift_rows.defvjp(lambda halo, x, d: (_shift_impl(halo, x, d), None), _shift_bwd)


def causal_conv(halo, x, w_rows, b):
    k = len(w_rows)
    y = b + w_rows[k - 1] * x
    for d in range(1, k):
        y = y + w_rows[k - 1 - d] * shift_rows(halo, x, d)
    return y


def _sigmoid(x):
    return 1.0 / (1.0 + jnp.exp(-x))


def _silu(x):
    return x * _sigmoid(x)


def _log_sigmoid(x):
    return jnp.minimum(x, 0.0) - jnp.log(1.0 + jnp.exp(-jnp.abs(x)))


def _pick_row(x, i):
    row = lax.broadcasted_iota(jnp.int32, (x.shape[0], 1), 0)
    return jnp.sum(jnp.where(row == i, x, 0.0), axis=0, keepdims=True)


def _layer_norm(z, g, b):
    mu = jnp.mean(z, axis=-1, keepdims=True)
    zc = z - mu
    var = jnp.mean(zc * zc, axis=-1, keepdims=True)
    return zc * lax.rsqrt(var + LN_EPS) * g + b


def _qk_conv(halo, x, w0, w1, w2, w3, b):
    return _silu(causal_conv(halo, x, (w0, w1, w2, w3), b))


def _grp(i, h=None):
    if h is None:
        return pl.ds(i * D_GRP, D_GRP)
    return pl.ds(i * D_GRP + h * DK, DK)


def _mixer_specs(n_chunks, reverse):
    def chunk(c):
        return n_chunks - 1 - c if reverse else c
    row8 = CHUNK // SUBLANES
    proj_spec = pl.BlockSpec((CHUNK, D_IN_PAD), lambda c: (chunk(c), 0))
    halo_spec = pl.BlockSpec((SUBLANES, 2 * D_GRP), lambda c: (jnp.maximum(chunk(c) * row8 - 1, 0), 2))
    small = [pl.BlockSpec((2, D_GRP), lambda c: (0, 0)), pl.BlockSpec((1, D_GRP), lambda c: (0, 0)),
             pl.BlockSpec((ML_CONV, 2 * D_GRP), lambda c: (0, 0)), pl.BlockSpec((1, 2 * D_GRP), lambda c: (0, 0)),
             pl.BlockSpec((1, D_GRP), lambda c: (0, 0))]
    state_specs = [pl.BlockSpec((1, HEADS, DK, DK), lambda c: (chunk(c), 0, 0, 0)),
                   pl.BlockSpec((1, HEADS, DK, DK), lambda c: (chunk(c), 0, 0, 0)),
                   pl.BlockSpec((1, HEADS, 1, DK), lambda c: (chunk(c), 0, 0, 0)),
                   pl.BlockSpec((1, HEADS, 1, DK), lambda c: (chunk(c), 0, 0, 0))]
    y_spec = pl.BlockSpec((CHUNK, 2 * D_GRP), lambda c: (chunk(c), 0))
    return proj_spec, halo_spec, small, state_specs, y_spec, chunk


def _heads(x):
    return [x[:, h * DK:(h + 1) * DK] for h in range(HEADS)]


def _last(x, j):
    lane = lax.broadcasted_iota(jnp.int32, (1, x.shape[-1]), 1)
    return jnp.sum(jnp.where(lane == j, x, 0.0), axis=-1, keepdims=True)


def _hg_chunk(st_t, hq, hf, hi, hgate, l0, l1, nw):
    n = hq.shape[0]
    lb = _sigmoid(l0 - l1)
    q = _silu(hq)
    lf = jnp.log(lb + (1.0 - lb) * _sigmoid(hf))
    k = (1.0 - lb) * _sigmoid(-hf)
    b = cumsum_rows(lf)
    b_ref = _pick_row(b, n // 2 - 1)
    b_last = _pick_row(b, n - 1)
    qa, ka =_heads(q * jnp.exp(b - b_ref)), _heads(k * jnp.exp(b_ref - b))
    qe, kd, eb, v = _heads(q * jnp.exp(b)), _heads(k * jnp.exp(b_last - b)), _heads(jnp.exp(b_last)), _heads(hi)
    tri = _tri(n, True) > 0
    attn = [jnp.where(tri, mm_nt(qa[h], ka[h]), 0.0) for h in range(HEADS)]
    o = [mm_nn(attn[h], v[h]) + mm_nt(qe[h], st_t[h]) for h in range(HEADS)]
    st_new = jnp.stack([eb[h] * st_t[h] + mm_tn(v[h], kd[h]) for h in range(HEADS)])
    yn = [o[h] * lax.rsqrt(jnp.mean(o[h] * o[h], axis=-1, keepdims=True) + LN_EPS) for h in range(HEADS)]
    return st_new, jnp.concatenate(yn, axis=1) * nw * _silu(hgate)


def _ml_chunk(c_st, n_st, m_st, q, k, v, gates, og, nw):
    n = q.shape[0]
    ig = jnp.stack([_last(gates, h) for h in range(HEADS)])
    log_f = _log_sigmoid(gates)
    fl = jnp.stack([_last(log_f, HEADS + h) for h in range(HEADS)])
    bw = cumsum_rows(jnp.concatenate([jnp.broadcast_to(fl[h], (n, DK)) for h in range(HEADS)], axis=1))
    b = jnp.stack([_last(x, 0) for x in _heads(bw)])
    g = jnp.sum(fl, axis=1, keepdims=True)
    eye = lax.broadcasted_iota(jnp.int32, (n, n), 0) == lax.broadcasted_iota(jnp.int32, (n, n), 1)
    e_row = jnp.sum(jnp.where(eye, ig - b, 0.0), axis=1, keepdims=True)
    d = jnp.where(_tri(n, True) > 0, b + e_row, -jnp.inf)
    inter = b + m_st
    m_t = jnp.maximum(inter, jnp.max(d, axis=2, keepdims=True))
    qs, kh, vh = _heads(q * (DK ** -0.5)), _heads(k), _heads(v)
    s = jnp.stack([mm_nt(qs[h], kh[h]) for h in range(HEADS)]) * jnp.exp(d - m_t)
    w_inter = jnp.exp(inter - m_t)
    num = (jnp.stack([mm_nn(s[h], vh[h]) for h in range(HEADS)])
           + w_inter * jnp.stack([mm_nn(qs[h], c_st[h]) for h in range(HEADS)]))
    den = jnp.sum(s, axis=2, keepdims=True) + w_inter * jnp.sum(jnp.stack(qs) * n_st, axis=2, keepdims=True)
    h_out = num / jnp.maximum(jnp.abs(den), jnp.exp(-m_t))
    a = g - b + ig
    m_new = jnp.maximum(g + m_st, jnp.max(a, axis=1, keepdims=True))
    decay = jnp.exp(g + m_st - m_new)
    wk = jnp.stack(kh) * jnp.exp(a - m_new)
    c_new = decay * c_st + jnp.stack([mm_tn(wk[h], vh[h]) for h in range(HEADS)])
    n_new = decay * n_st + jnp.sum(wk, axis=1, keepdims=True)
    hc = h_out - jnp.mean(h_out, axis=-1, keepdims=True)
    yn = hc * lax.rsqrt(jnp.mean(hc * hc, axis=-1, keepdims=True) + LN_EPS)
    y = _sigmoid(og) * (jnp.concatenate([yn[h] for h in range(HEADS)], axis=1) * nw)
    return c_new, n_new, m_new, y


def _mixer_inputs(proj_ref, lg_ref, hnw_ref, mnw_ref, qk):
    hg_in = (proj_ref[:, _grp(0)], proj_ref[:, _grp(1)], proj_ref[:, _grp(2)], proj_ref[:, _grp(3)],
             lg_ref[0:1, :], lg_ref[1:2, :], hnw_ref[...])
    ml_in = (qk[:, :D_GRP], qk[:, D_GRP:], proj_ref[:, _grp(6)], proj_ref[:, pl.ds(8 * D_GRP, LANES)],
             proj_ref[:, _grp(7)], mnw_ref[...])
    return hg_in, ml_in


def _mixer_fwd(proj, lb_logits, hg_nw, conv_w, conv_b, ml_nw):
    seq = proj.shape[0]
    n_chunks = seq // CHUNK
    proj_spec, halo_spec, small, state_specs, y_spec, _ = _mixer_specs(n_chunks, False)

    def body(proj_ref, halo_ref, lg_ref, hnw_ref, cw_ref, cb_ref, mnw_ref,
             y_ref, hst_ref, cst_ref, nst_ref, mst_ref, hs, cs, ns, ms):
        c = pl.program_id(0)

        @pl.when(c == 0)
        def _():
            hs[...] = jnp.zeros_like(hs)
            cs[...] = jnp.zeros_like(cs)
            ns[...] = jnp.zeros_like(ns)
            ms[...] = jnp.full(ms.shape, NEG_BIG, F32)

        hst_ref[0] = hs[...]
        cst_ref[0] = cs[...]
        nst_ref[0] = ns[...]
        mst_ref[0] = ms[...]
        halo = jnp.where(c > 0, halo_ref[...], 0.0)
        qk = _qk_conv(halo, proj_ref[:, pl.ds(4 * D_GRP, 2 * D_GRP)],
                      cw_ref[0:1, :], cw_ref[1:2, :], cw_ref[2:3, :], cw_ref[3:4, :], cb_ref[...])
        hg_in, ml_in = _mixer_inputs(proj_ref, lg_ref, hnw_ref, mnw_ref, qk)
        hs[...], y_hg = _hg_chunk(hs[...], *hg_in)
        cs[...], ns[...], m_new, y_ml = _ml_chunk(cs[...], ns[...], _last(ms[...], 0), *ml_in)
        ms[...] = jnp.broadcast_to(m_new, ms.shape)
        y_ref[:, pl.ds(0, D_GRP)] = y_hg.astype(BF16)
        y_ref[:, pl.ds(D_GRP, D_GRP)] = y_ml.astype(BF16)

    st = jax.ShapeDtypeStruct((n_chunks, HEADS, DK, DK), F32)
    vec = jax.ShapeDtypeStruct((n_chunks, HEADS, 1, DK), F32)
    vmem = 2 * (_nbytes((CHUNK, D_IN_PAD), F32) + _nbytes((CHUNK, 2 * D_GRP), F32) + 2 * _nbytes((HEADS, DK, DK), F32)) \
        + 2 * _nbytes((HEADS, DK, DK), F32)
    return _pcall(
        body, name="mixer_fwd", grid=(n_chunks,),
        in_specs=[proj_spec, halo_spec] + small,
        out_specs=[y_spec] + state_specs,
        out_shape=[jax.ShapeDtypeStruct((seq, 2 * D_GRP), BF16), st, st, vec, vec],
        scratch_shapes=[pltpu.VMEM((HEADS, DK, DK), F32), pltpu.VMEM((HEADS, DK, DK), F32),
                        pltpu.VMEM((HEADS, 1, DK), F32), pltpu.VMEM((HEADS, 1, DK), F32)],
        compiler_params=_params(("arbitrary",), vmem),
    )(proj, proj, lb_logits, hg_nw, conv_w, conv_b, ml_nw)


def _mixer_bwd(proj, dy, hst, cst, nst, mst, lb_logits, hg_nw, conv_w, conv_b, ml_nw):
    seq = proj.shape[0]
    n_chunks = seq // CHUNK
    proj_spec, halo_spec, small, state_specs, y_spec, _ = _mixer_specs(n_chunks, True)

    def body(proj_ref, halo_ref, dy_ref, hst_ref, cst_ref, nst_ref, mst_ref,
             lg_ref, hnw_ref, cw_ref, cb_ref, mnw_ref,
             dproj_ref, dbin_ref, dlg_ref, dhnw_ref, dcw_ref, dcb_ref, dmnw_ref,
             dhs, dcs, dns, dms, dhalo):
        c = pl.program_id(0)

        @pl.when(c == 0)
        def _():
            for r in (dhs, dcs, dns, dms, dhalo, dbin_ref, dlg_ref, dhnw_ref, dcw_ref, dcb_ref, dmnw_ref):
                r[...] = jnp.zeros_like(r)

        def put(cols, val):
            dproj_ref[:, cols] = val.astype(BF16)
            dbin_ref[:, cols] += jnp.sum(val, axis=0, keepdims=True)

        first = c == n_chunks - 1
        halo = jnp.where(first, 0.0, halo_ref[...])
        x_qk = proj_ref[:, pl.ds(4 * D_GRP, 2 * D_GRP)]
        conv_args = (halo, x_qk, cw_ref[0:1, :], cw_ref[1:2, :], cw_ref[2:3, :], cw_ref[3:4, :], cb_ref[...])
        qk, conv_vjp = jax.vjp(_qk_conv, *conv_args)
        hg_in, ml_in = _mixer_inputs(proj_ref, lg_ref, hnw_ref, mnw_ref, qk)
        _, hg_vjp = jax.vjp(_hg_chunk, hst_ref[0], *hg_in)
        _, ml_vjp = jax.vjp(_ml_chunk, cst_ref[0], nst_ref[0], _last(mst_ref[0], 0), *ml_in)
        dst, dhq, dhf, dhi, dhg, dl0, dl1, dnw = hg_vjp((dhs[...], dy_ref[:, pl.ds(0, D_GRP)]))
        dc, dn, dm, dq, dk, dv, dgates, dog, dmn = ml_vjp(
            (dcs[...], dns[...], _last(dms[...], 0), dy_ref[:, pl.ds(D_GRP, D_GRP)]))
        dhs[...] = dst
        dcs[...] = dc
        dns[...] = dn
        dms[...] = jnp.broadcast_to(dm, dms.shape)
        for i, val in ((0, dhq), (1, dhf), (2, dhi), (3, dhg), (6, dv), (7, dog)):
            put(_grp(i), val)
        put(pl.ds(8 * D_GRP, LANES), dgates)
        dlg_ref[0:1, :] += dl0
        dlg_ref[1:2, :] += dl1
        dhnw_ref[...] += dnw
        dmnw_ref[...] += dmn
        dh, dx, dw0, dw1, dw2, dw3, db = conv_vjp(jnp.concatenate([dq, dk], axis=1))
        tail = jnp.concatenate([jnp.zeros((CHUNK - SUBLANES, 2 * D_GRP), F32), dhalo[...]], axis=0)
        put(pl.ds(4 * D_GRP, 2 * D_GRP), dx + tail)
        dhalo[...] = dh
        for d, dw in enumerate((dw0, dw1, dw2, dw3)):
            dcw_ref[d:d + 1, :] += dw
        dcb_ref[...] += db

    row = pl.BlockSpec((1, D_GRP), lambda c: (0, 0))
    small_out = [pl.BlockSpec((1, D_IN_PAD), lambda c: (0, 0)), pl.BlockSpec((2, D_GRP), lambda c: (0, 0)), row,
                 pl.BlockSpec((ML_CONV, 2 * D_GRP), lambda c: (0, 0)), pl.BlockSpec((1, 2 * D_GRP), lambda c: (0, 0)), row]
    dy_spec = pl.BlockSpec((CHUNK, 2 * D_GRP), y_spec.index_map)
    vmem = 2 * (2 * _nbytes((CHUNK, D_IN_PAD), F32) + _nbytes((CHUNK, 2 * D_GRP), F32)
                + 2 * _nbytes((HEADS, DK, DK), F32)) + 2 * _nbytes((HEADS, DK, DK), F32) + 4 * 1024 * 1024
    return _pcall(
        body, name="mixer_bwd", grid=(n_chunks,),
        in_specs=[proj_spec, halo_spec, dy_spec] + state_specs + small,
        out_specs=[proj_spec] + small_out,
        out_shape=[jax.ShapeDtypeStruct((seq, D_IN_PAD), BF16), jax.ShapeDtypeStruct((1, D_IN_PAD), F32),
                   jax.ShapeDtypeStruct((2, D_GRP), F32), jax.ShapeDtypeStruct((1, D_GRP), F32),
                   jax.ShapeDtypeStruct((ML_CONV, 2 * D_GRP), F32), jax.ShapeDtypeStruct((1, 2 * D_GRP), F32),
                   jax.ShapeDtypeStruct((1, D_GRP), F32)],
        scratch_shapes=[pltpu.VMEM((HEADS, DK, DK), F32), pltpu.VMEM((HEADS, DK, DK), F32),
                        pltpu.VMEM((HEADS, 1, DK), F32), pltpu.VMEM((HEADS, 1, DK), F32),
                        pltpu.VMEM((SUBLANES, 2 * D_GRP), F32)],
        compiler_params=_params(("arbitrary",), vmem),
    )(proj, proj, dy, hst, cst, nst, mst, lb_logits, hg_nw, conv_w, conv_b, ml_nw)


def _tile(n, prefs, unit=None):
    unit = unit or n
    for p in prefs:
        if unit % p == 0 and n % p == 0:
            return p
    return unit


def _logical(arr):
    return arr.shape if arr.ndim == 2 else (arr.shape[1], arr.shape[0] * arr.shape[2])


def _group(arr):
    return arr.shape[-1]


def _split_spec(ndim, group, tr, tc, where):
    if ndim == 2:
        return pl.BlockSpec((tr, tc), where)
    per = group // tc
    assert per * tc == group, (group, tc)

    def index(*ids):
        bi, bj = where(*ids)
        return (bj // per, bi, bj % per)
    return pl.BlockSpec((None, tr, tc), index)


def _mm(name, mode, a, b, *, bias=None, res=None, res_scale=1.0, ln=None, out_dtype=F32, out_groups=None,
        copy_dtype=None, tm=None, tn=None, tk=None):
    la, lb = _logical(a), _logical(b)
    if mode == "nn":
        (m, k), n = la, lb[1]
        n_unit = _group(b) if b.ndim == 3 else n
        kc = _group(a) if a.ndim == 3 else k
    elif mode == "nt":
        (m, k), n = la, lb[0]
        n_unit = n
        kc = min(_group(a) if a.ndim == 3 else k, _group(b) if b.ndim == 3 else k)
    else:
        (k, m), n = la, lb[1]
        n_unit, kc = (_group(b) if b.ndim == 3 else n), k
        assert a.ndim == 2
    if out_groups:
        n_unit = min(n_unit, n // out_groups)
    kind = ln[0] if ln else None
    tm = tm or (256 if ln else _tile(m, (512, 256, 128)))
    tn = n if ln else (tn or _tile(n, (512, 384, 256, 128), n_unit))
    tk = (tk or _tile(k, (2048, 512, 256, 128))) if mode == "tn" else k
    gi, gj, gk = m // tm, n // tn, k // tk
    assert gi * tm == m and gj * tn == n and gk * tk == k and n_unit % tn == 0, (name, m, n, k, tm, tn, tk)
    ca, cb = {"nn": (1, 0), "nt": (1, 1), "tn": (0, 0)}[mode]
    i_outer = gk > 1 or (gi - 1) * _nbytes(b.shape, b.dtype) <= (gj - 1) * _nbytes(a.shape, a.dtype)

    def ij(where):
        return (lambda p, q, kk: where(p, q, kk)) if i_outer else (lambda p, q, kk: where(q, p, kk))
    if mode == "tn":
        a_spec = pl.BlockSpec((tk, tm), ij(lambda i, j, kk: (kk, i)))
    elif a.ndim == 3:
        a_spec = pl.BlockSpec((a.shape[0], tm, _group(a)), ij(lambda i, j, kk: (0, i, 0)))
    else:
        a_spec = pl.BlockSpec((tm, k), ij(lambda i, j, kk: (i, 0)))
    if mode != "nt":
        b_spec = _split_spec(b.ndim, _group(b), tk, tn, ij(lambda i, j, kk: (kk, j)))
    elif b.ndim == 3:
        b_spec = pl.BlockSpec((b.shape[0], tn, _group(b)), ij(lambda i, j, kk: (0, j, 0)))
    else:
        b_spec = pl.BlockSpec((tn, k), ij(lambda i, j, kk: (j, 0)))
    row_spec = pl.BlockSpec((1, tn), ij(lambda i, j, kk: (0, j)))
    blk_spec = pl.BlockSpec((tm, tn), ij(lambda i, j, kk: (i, j)))
    ins, in_specs = [a, b], [a_spec, b_spec]
    if bias is not None:
        ins.append(bias), in_specs.append(row_spec)
    if res is not None:
        ins.append(res), in_specs.append(blk_spec)
    if kind == "fwd":
        ins += [ln[1], ln[2]]
        in_specs += [row_spec, row_spec]
    elif kind == "bwd":
        ins += [ln[1], ln[2], ln[3]]
        in_specs += [blk_spec, row_spec, row_spec]
    if out_groups:
        blk_out = jax.ShapeDtypeStruct((out_groups, m, n // out_groups), out_dtype)
        out_spec = _split_spec(3, n // out_groups, tm, tn, ij(lambda i, j, kk: (i, j)))
    else:
        blk_out, out_spec = jax.ShapeDtypeStruct((m, n), out_dtype), blk_spec
    row_out = jax.ShapeDtypeStruct((1, n), F32)
    if kind is None:
        out_shape, out_specs = [blk_out], [out_spec]
    elif kind == "fwd":
        out_shape, out_specs = [blk_out, blk_out], [blk_spec, blk_spec]
    else:
        out_shape, out_specs = [blk_out, row_out, row_out], [blk_spec, row_spec, row_spec]
    if copy_dtype is not None:
        out_shape.append(jax.ShapeDtypeStruct((m, n), copy_dtype))
        out_specs.append(blk_spec)
    n_in = len(ins)

    def body(*refs):
        in_refs, out_refs, acc_ref = refs[:n_in], refs[n_in:n_in + len(out_shape)], refs[-1]
        i, kk = pl.program_id(0 if i_outer else 1), pl.program_id(2)
        a_ref, b_ref = in_refs[:2]
        extra = list(in_refs[2:])

        def epilogue(acc):
            rest = list(extra)
            if bias is not None:
                acc = acc + rest.pop(0)[...]
            if res is not None:
                acc = acc + res_scale * rest.pop(0)[...]
            if kind is None:
                out_refs[0][...] = acc.astype(out_dtype)
                return
            if kind == "fwd":
                out_refs[0][...] = acc
                y = _layer_norm(acc, rest[0][...], rest[1][...])
                out_refs[1][...] = y
                if copy_dtype is not None:
                    out_refs[-1][...] = y.astype(copy_dtype)
                return
            _, vjp = jax.vjp(_layer_norm, rest[0][...], rest[1][...], rest[2][...])
            dz, dg, db = vjp(acc)
            out_refs[0][...] = dz
            out_refs[1][...] += dg
            out_refs[2][...] += db
            if copy_dtype is not None:
                out_refs[-1][...] = dz.astype(copy_dtype)

        if kind == "bwd":
            @pl.when((i == 0) & (kk == 0))
            def _():
                out_refs[1][...] = jnp.zeros_like(out_refs[1])
                out_refs[2][...] = jnp.zeros_like(out_refs[2])

        def chunk(ref, c0, last):
            if ref.ndim == 3:
                g = ref.shape[2]
                return ref[c0 // g, :, pl.ds(c0 % g, kc)]
            return ref[:, pl.ds(c0, kc)] if last else ref[pl.ds(c0, kc), :]

        if mode == "tn" or kc == k:
            prod = _dg(a_ref[...], b_ref[...], ca, cb)
        else:
            prod = None
            for c0 in range(0, k, kc):
                part = _dg(chunk(a_ref, c0, True), chunk(b_ref, c0, mode == "nt"), ca, cb)
                prod = part if prod is None else prod + part
        if gk == 1:
            epilogue(prod)
            return

        @pl.when(kk == 0)
        def _():
            acc_ref[...] = prod

        @pl.when(kk > 0)
        def _():
            acc_ref[...] += prod

        @pl.when(kk == gk - 1)
        def _():
            epilogue(acc_ref[...])

    vmem = (2 * (_nbytes((tm, tk), a.dtype) + _nbytes((tk, tn), b.dtype))
            + (2 * len(ins) + 2 * len(out_shape) + 1) * _nbytes((tm, tn), F32))
    outs = _pcall(
        body, name=name, grid=(gi, gj, gk) if i_outer else (gj, gi, gk), in_specs=in_specs, out_specs=out_specs,
        out_shape=out_shape, scratch_shapes=[pltpu.VMEM((tm, tn) if gk > 1 else (SUBLANES, LANES), F32)],
        compiler_params=_params(("arbitrary", "arbitrary", "arbitrary"), vmem),
    )(*ins)
    return outs[0] if (kind is None and copy_dtype is None) else outs


def _attn_head(q, k, v):
    sc = mm_nt(q, k) * (CA_DH ** -0.5)
    e = jnp.exp(sc - jnp.max(sc, axis=-1, keepdims=True))
    return mm_nn(e / jnp.sum(e, axis=-1, keepdims=True), v)


def _attn_fwd(q, kv):
    seq, n_mem = q.shape[0], kv.shape[0]
    tq = _tile(seq, (512, 256, 128))

    def body(q_ref, kv_ref, o_ref):
        for h in range(HEADS):
            hd = pl.ds(h * CA_DH, CA_DH)
            o = _attn_head(q_ref[:, hd], kv_ref[:, hd], kv_ref[:, pl.ds(D_MODEL + h * CA_DH, CA_DH)])
            o_ref[:, hd] = o.astype(BF16)

    return _pcall(
        body, name="attn_fwd", grid=(seq // tq,),
        in_specs=[pl.BlockSpec((tq, D_MODEL), lambda i: (i, 0)), pl.BlockSpec((n_mem, 2 * D_MODEL), lambda i: (0, 0))],
        out_specs=pl.BlockSpec((tq, D_MODEL), lambda i: (i, 0)), out_shape=jax.ShapeDtypeStruct((seq, D_MODEL), BF16),
        compiler_params=_params(("arbitrary",), 4 * _nbytes((tq, D_MODEL), F32) + 2 * _nbytes((n_mem, 2 * D_MODEL), F32)),
    )(q, kv)


def _attn_bwd(q, kv, do):
    seq, n_mem = q.shape[0], kv.shape[0]
    tq = _tile(seq, (512, 256, 128))

    def body(q_ref, kv_ref, do_ref, dq_ref, dkv_ref):
        @pl.when(pl.program_id(0) == 0)
        def _():
            dkv_ref[...] = jnp.zeros_like(dkv_ref)

        for h in range(HEADS):
            hd = pl.ds(h * CA_DH, CA_DH)
            vd = pl.ds(D_MODEL + h * CA_DH, CA_DH)
            _, vjp = jax.vjp(_attn_head, q_ref[:, hd], kv_ref[:, hd], kv_ref[:, vd])
            dq, dk, dv = vjp(do_ref[:, hd].astype(F32))
            dq_ref[:, hd] = dq.astype(BF16)
            dkv_ref[:, hd] += dk
            dkv_ref[:, vd] += dv

    return _pcall(
        body, name="attn_bwd", grid=(seq // tq,),
        in_specs=[pl.BlockSpec((tq, D_MODEL), lambda i: (i, 0)), pl.BlockSpec((n_mem, 2 * D_MODEL), lambda i: (0, 0)),
                  pl.BlockSpec((tq, D_MODEL), lambda i: (i, 0))],
        out_specs=[pl.BlockSpec((tq, D_MODEL), lambda i: (i, 0)), pl.BlockSpec((n_mem, 2 * D_MODEL), lambda i: (0, 0))],
        out_shape=[jax.ShapeDtypeStruct((seq, D_MODEL), BF16), jax.ShapeDtypeStruct((n_mem, 2 * D_MODEL), F32)],
        compiler_params=_params(("arbitrary",), 6 * _nbytes((tq, D_MODEL), F32) + 4 * _nbytes((n_mem, 2 * D_MODEL), F32)),
    )(q, kv, do)


def _ffn_mid(hg, xg, hv, xv, wg0, wg1, wg2, bg, wv0, wv1, wv2, bv):
    return jax.nn.gelu(causal_conv(hg, xg, (wg0, wg1, wg2), bg)) * causal_conv(hv, xv, (wv0, wv1, wv2), bv)


FFN_TB = 256
FFN_W = D_FF // 2
FFN_J = D_FF // FFN_W
MXU_COLS = 256
FFN_PIECES = tuple((off, min(MXU_COLS, FFN_W - off)) for off in range(0, FFN_W, MXU_COLS))


def _ffn_common_specs(seq, row):
    tb = min(FFN_TB, seq)
    full = pl.BlockSpec((tb, D_MODEL), lambda t, j: (row(t), 0))
    vec = pl.BlockSpec((1, D_MODEL), lambda t, j: (0, 0))
    halves = []
    for off in (0, FFN_J):
        halves.append(dict(
            w_up=pl.BlockSpec((None, D_MODEL, FFN_W), lambda t, j, off=off: (j + off, 0, 0)),
            taps=pl.BlockSpec((FFN_CONV, FFN_W), lambda t, j, off=off: (0, j + off)),
            bias=pl.BlockSpec((1, FFN_W), lambda t, j, off=off: (0, j + off))))
    w_down = pl.BlockSpec((FFN_W, D_MODEL), lambda t, j: (j, 0))
    u_blk = pl.BlockSpec((2, tb, FFN_W), lambda t, j: (0, row(t), j))
    return tb, full, vec, halves, w_down, u_blk


def _ffn_vmem(tb):
    return (_nbytes((2, tb, FFN_W), F32) + _nbytes((2, tb, FFN_W), BF16) + 3 * _nbytes((D_MODEL, FFN_W), BF16)
            + 10 * _nbytes((tb, D_MODEL), F32))


def _conv_params(taps_ref, bias_ref, cols):
    return taps_ref[0:1, cols], taps_ref[1:2, cols], taps_ref[2:3, cols], bias_ref[:, cols]


def _ffn_fwd(x2b, x2, w_up, conv_w, conv_b, w_down, ln_g, ln_b, target):
    seq = x2.shape[0]
    tb, full, vec, halves, wd_spec, u_blk = _ffn_common_specs(seq, lambda t: t)
    nt = seq // tb

    def body(xb_ref, wg_ref, wv_ref, tg_ref, tv_ref, bg_ref, bv_ref, wd_ref, x_ref, g_ref, b_ref, tgt_ref,
             u_ref, h_ref, dz_ref, dg_ref, db_ref, loss_ref, dzb_ref, acc, carry):
        t, j = pl.program_id(0), pl.program_id(1)
        xb = xb_ref[...]
        pieces = [pl.ds(off, width) for off, width in FFN_PIECES]
        ug = [_dg(xb, wg_ref[:, cols], 1, 0) for cols in pieces]
        uv = [_dg(xb, wv_ref[:, cols], 1, 0) for cols in pieces]
        hs = []
        for cols, g, v in zip(pieces, ug, uv):
            u_ref[0, :, cols] = g
            u_ref[1, :, cols] = v
            halo_g = jnp.where(t == 0, 0.0, carry[j, 0, :, cols])
            halo_v = jnp.where(t == 0, 0.0, carry[j, 1, :, cols])
            h = _ffn_mid(halo_g, g, halo_v, v, *_conv_params(tg_ref, bg_ref, cols),
                         *_conv_params(tv_ref, bv_ref, cols)).astype(BF16)
            carry[j, 0, :, cols] = g[tb - SUBLANES:, :]
            carry[j, 1, :, cols] = v[tb - SUBLANES:, :]
            h_ref[:, cols] = h
            hs.append(h)
        part = None
        for cols, h in zip(pieces, hs):
            p = _dg(h, wd_ref[cols, :], 1, 0)
            part = p if part is None else part + p

        @pl.when(j == 0)
        def _():
            acc[...] = part

        @pl.when(j > 0)
        def _():
            acc[...] += part

        @pl.when(j == FFN_J - 1)
        def _():
            y, vjp = jax.vjp(_layer_norm, acc[...] + ALPHA * x_ref[...], g_ref[...], b_ref[...])
            err = y - tgt_ref[...]
            part_loss = 0.5 * jnp.sum(jnp.sum(err * err, axis=1, keepdims=True), axis=0, keepdims=True) / D_MODEL
            dz, dg, db = vjp(err / D_MODEL)

            @pl.when(t == 0)
            def _():
                for r in (dg_ref, db_ref, loss_ref):
                    r[...] = jnp.zeros_like(r)

            dz_ref[...] = dz
            dzb_ref[...] = dz.astype(BF16)
            dg_ref[...] += dg
            db_ref[...] += db
            loss_ref[...] += jnp.broadcast_to(part_loss, (1, LANES))

    h0, h1 = halves
    row = jax.ShapeDtypeStruct((1, D_MODEL), F32)
    return _pcall(
        body, name="ffn_fwd", grid=(nt, FFN_J),
        in_specs=[full, h0["w_up"], h1["w_up"], h0["taps"], h1["taps"], h0["bias"], h1["bias"], wd_spec, full, vec, vec,
                  full],
        out_specs=[u_blk, pl.BlockSpec((tb, FFN_W), lambda t, j: (t, j)), full, vec, vec,
                   pl.BlockSpec((1, LANES), lambda t, j: (0, 0)), full],
        out_shape=[jax.ShapeDtypeStruct((2, seq, D_FF), F32), jax.ShapeDtypeStruct((seq, D_FF), BF16),
                   jax.ShapeDtypeStruct((seq, D_MODEL), F32), row, row, jax.ShapeDtypeStruct((1, LANES), F32),
                   jax.ShapeDtypeStruct((seq, D_MODEL), BF16)],
        scratch_shapes=[pltpu.VMEM((tb, D_MODEL), F32), pltpu.VMEM((FFN_J, 2, SUBLANES, FFN_W), F32)],
        compiler_params=_params(("arbitrary", "arbitrary"), _ffn_vmem(tb)),
    )(x2b, w_up, w_up, conv_w, conv_w, conv_b, conv_b, w_down, x2, ln_g, ln_b, target)


def _ffn_bwd(u, conv_w, conv_b, dz3b, dz3, w_down, w_up, z2, ln_g, ln_b):
    seq = dz3.shape[0]
    tb = min(FFN_TB, seq)
    nt = seq // tb
    row8 = tb // SUBLANES
    tb, full, vec, halves, wd_spec, u_blk = _ffn_common_specs(seq, lambda t: nt - 1 - t)
    halo = pl.BlockSpec((2, SUBLANES, FFN_W), lambda t, j: (0, jnp.maximum((nt - 1 - t) * row8 - 1, 0), j))

    def body(u_ref, halo_ref, tg_ref, tv_ref, bg_ref, bv_ref, dzb_ref, wd_ref, wg_ref, wv_ref, dz3_ref, z_ref, g_ref,
             b_ref, du_ref, dw_ref, dbias_ref, dz_ref, dg_ref, db_ref, dz2b_ref, acc, carry):
        t, j = pl.program_id(0), pl.program_id(1)

        @pl.when((t == 0) & (j == 0))
        def _():
            for r in (dw_ref, dbias_ref, dg_ref, db_ref):
                r[...] = jnp.zeros_like(r)

        pieces = [pl.ds(off, width) for off, width in FFN_PIECES]
        dzb = dzb_ref[...]
        dhs = [_dg(dzb, wd_ref[cols, :], 1, 1) for cols in pieces]
        first = t == nt - 1
        dus = []
        for cols, dh in zip(pieces, dhs):
            args = (jnp.where(first, 0.0, halo_ref[0, :, cols]), u_ref[0, :, cols],
                    jnp.where(first, 0.0, halo_ref[1, :, cols]), u_ref[1, :, cols],
                    *_conv_params(tg_ref, bg_ref, cols), *_conv_params(tv_ref, bv_ref, cols))
            _, vjp = jax.vjp(_ffn_mid, *args)
            dhg, dxg, dhv, dxv, g0, g1, g2, gb, v0, v1, v2, vb = vjp(dh)
            zeros = jnp.zeros((tb - SUBLANES, dh.shape[1]), F32)
            dug = (dxg + jnp.concatenate([zeros, jnp.where(t == 0, 0.0, carry[j, 0, :, cols])], axis=0)).astype(BF16)
            duv = (dxv + jnp.concatenate([zeros, jnp.where(t == 0, 0.0, carry[j, 1, :, cols])], axis=0)).astype(BF16)
            carry[j, 0, :, cols] = dhg
            carry[j, 1, :, cols] = dhv
            du_ref[0, :, cols] = dug
            du_ref[1, :, cols] = duv
            for half, parts in enumerate(((g0, g1, g2), (v0, v1, v2))):
                for d, p in enumerate(parts):
                    dw_ref[j, half, d:d + 1, cols] += p
            dbias_ref[j, 0, :, cols] += gb
            dbias_ref[j, 1, :, cols] += vb
            dus.append((dug, duv))
        part = None
        for cols, (dug, duv) in zip(pieces, dus):
            p = _dg(dug, wg_ref[:, cols], 1, 1) + _dg(duv, wv_ref[:, cols], 1, 1)
            part = p if part is None else part + p

        @pl.when(j == 0)
        def _():
            acc[...] = part

        @pl.when(j > 0)
        def _():
            acc[...] += part

        @pl.when(j == FFN_J - 1)
        def _():
            _, ln_vjp = jax.vjp(_layer_norm, z_ref[...], g_ref[...], b_ref[...])
            dz, dg, db = ln_vjp(acc[...] + ALPHA * dz3_ref[...])
            dz_ref[...] = dz
            dz2b_ref[...] = dz.astype(BF16)
            dg_ref[...] += dg
            db_ref[...] += db

    h0, h1 = halves
    row = jax.ShapeDtypeStruct((1, D_MODEL), F32)
    whole = lambda *shape: pl.BlockSpec(shape, lambda t, j: (0,) * len(shape))
    return _pcall(
        body, name="ffn_bwd", grid=(nt, FFN_J),
        in_specs=[u_blk, halo, h0["taps"], h1["taps"], h0["bias"], h1["bias"], full, wd_spec, h0["w_up"], h1["w_up"],
                  full, full, vec, vec],
        out_specs=[u_blk, whole(FFN_J, 2, FFN_CONV, FFN_W), whole(FFN_J, 2, 1, FFN_W), full, vec, vec, full],
        out_shape=[jax.ShapeDtypeStruct((2, seq, D_FF), BF16), jax.ShapeDtypeStruct((FFN_J, 2, FFN_CONV, FFN_W), F32),
                   jax.ShapeDtypeStruct((FFN_J, 2, 1, FFN_W), F32), jax.ShapeDtypeStruct((seq, D_MODEL), F32), row, row,
                   jax.ShapeDtypeStruct((seq, D_MODEL), BF16)],
        scratch_shapes=[pltpu.VMEM((tb, D_MODEL), F32), pltpu.VMEM((FFN_J, 2, SUBLANES, FFN_W), F32)],
        compiler_params=_params(("arbitrary", "arbitrary"), _ffn_vmem(tb)),
    )(u, u, conv_w, conv_w, conv_b, conv_b, dz3b, w_down, w_up, w_up, dz3, z2, ln_g, ln_b)


def _adamw_math(w, g, m, v):
    m_new = ADAM_B1 * m + (1.0 - ADAM_B1) * g
    v_new = ADAM_B2 * v + (1.0 - ADAM_B2) * jnp.square(g)
    m_hat = m_new / (1.0 - ADAM_B1 ** ADAM_STEP)
    v_hat = v_new / (1.0 - ADAM_B2 ** ADAM_STEP)
    return -ADAM_LR * (m_hat / (jnp.sqrt(v_hat) + ADAM_EPS) + ADAM_WD * w), m_new, v_new


def _adamw(name, w, g, m, v):
    rows, cols = w.shape
    tr = _tile(rows, (256, 176, 128, 64, 40, 32, 16, 8))

    def body(w_ref, g_ref, m_ref, v_ref, d_ref, nm_ref, nv_ref):
        d_ref[...], nm_ref[...], nv_ref[...] = _adamw_math(w_ref[...], g_ref[...], m_ref[...], v_ref[...])

    spec = pl.BlockSpec((tr, cols), lambda i: (i, 0))
    sh = jax.ShapeDtypeStruct((rows, cols), F32)
    return _pcall(
        body, name=name, grid=(rows // tr,), in_specs=[spec] * 4, out_specs=[spec] * 3, out_shape=[sh] * 3,
        compiler_params=_params(("arbitrary",), 14 * _nbytes((tr, -(-cols // LANES) * LANES), F32)),
    )(w, g, m, v)


def _adamw_halves(name, core, w, mine, theirs, m, v):
    rows, cols = w.shape
    half_rows = mine.shape[0]
    tr = _tile(half_rows, (256, 176, 128))
    nbh = half_rows // tr
    assert 2 * half_rows == rows

    def body(c_ref, w_ref, a_ref, b_ref, m_ref, v_ref, g_ref, d_ref, nm_ref, nv_ref):
        g = jnp.where(pl.program_id(0) // nbh == c_ref[0], a_ref[...], b_ref[...])
        g_ref[...] = g
        d_ref[...], nm_ref[...], nv_ref[...] = _adamw_math(w_ref[...], g, m_ref[...], v_ref[...])

    spec = pl.BlockSpec((tr, cols), lambda i, c_ref: (i, 0))
    half = pl.BlockSpec((tr, cols), lambda i, c_ref: (i % nbh, 0))
    sh = jax.ShapeDtypeStruct((rows, cols), F32)
    grid_spec = pltpu.PrefetchScalarGridSpec(
        num_scalar_prefetch=1, grid=(rows // tr,), in_specs=[spec, half, half, spec, spec], out_specs=[spec] * 4)
    return _pcall(
        body, name=name, grid_spec=grid_spec, out_shape=[sh] * 4,
        compiler_params=_params(("arbitrary",), 18 * _nbytes((tr, -(-cols // LANES) * LANES), F32)),
    )(core, w, mine, theirs, m, v)


MESH = pl.DeviceIdType.MESH
ANY = pl.BlockSpec(memory_space=pl.ANY)
N_CHIPS = 4
N_DEV = 8
BF16_ROWS = 16


def _me():
    return lax.axis_index("x"), lax.axis_index("y"), lax.axis_index("c")


def _other_chips(x, y):
    return [(1 - x, y), (x, 1 - y), (1 - x, 1 - y)]


def _remote(src, dst, ssem, rsem, dev):
    return pltpu.make_async_remote_copy(src_ref=src, dst_ref=dst, send_sem=ssem, recv_sem=rsem,
                                        device_id=dev, device_id_type=MESH)


def _half_rows(ref_rows, cc):
    half = ref_rows // 2
    return pl.ds(pl.multiple_of(cc * half, BF16_ROWS), half)


def _gather_weights(shards):
    n = len(shards)
    n_ici = n * (N_CHIPS - 1)

    def body(*refs):
        ins, outs, (ssem, rsem, lsem, lrsem) = refs[:n], refs[n:2 * n], refs[2 * n:]
        x, y, c = _me()
        k_me = 2 * x + y
        sib = (x, y, 1 - c)
        chips = _other_chips(x, y)
        started = []
        for i, (w_ref, o_ref) in enumerate(zip(ins, outs)):
            cp = _remote(w_ref, o_ref.at[k_me], lsem.at[i], lrsem.at[i], sib)
            cp.start()
            started.append(cp)
        for r, (px, py) in enumerate(chips):
            for i, (w_ref, o_ref) in enumerate(zip(ins, outs)):
                rows = _half_rows(w_ref.shape[0], c)
                s = r * n + i
                cp = _remote(w_ref.at[rows], o_ref.at[k_me, rows], ssem.at[s], rsem.at[s], (px, py, c))
                cp.start()
                started.append(cp)
        for r, (px, py) in enumerate(chips):
            for i, o_ref in enumerate(outs):
                blk = o_ref.at[2 * px + py, _half_rows(o_ref.shape[1], c)]
                s = r * n + i
                _remote(blk, blk, ssem.at[s], rsem.at[s], (px, py, c)).wait_recv()
                cp = _remote(blk, blk, ssem.at[n_ici + s], rsem.at[n_ici + s], sib)
                cp.start()
                started.append(cp)
        for r, (px, py) in enumerate(chips):
            for i, o_ref in enumerate(outs):
                blk = o_ref.at[2 * px + py, _half_rows(o_ref.shape[1], 1 - c)]
                s = n_ici + r * n + i
                _remote(blk, blk, ssem.at[s], rsem.at[s], sib).wait_recv()
        for cp in started[n:]:
            cp.wait_send()
        for cp in started[:n]:
            cp.wait()

    return _pcall(
        body, name="gather_weights", in_specs=[ANY] * n, out_specs=[ANY] * n,
        out_shape=[jax.ShapeDtypeStruct((N_CHIPS,) + s.shape, s.dtype) for s in shards],
        scratch_shapes=[pltpu.SemaphoreType.DMA((2 * n_ici,)), pltpu.SemaphoreType.DMA((2 * n_ici,)),
                        pltpu.SemaphoreType.DMA((n,)), pltpu.SemaphoreType.DMA((n,))],
    )(*shards)


def _swap_halves(name, grads):
    n = len(grads)

    def body(*refs):
        ins, outs, (ssem, rsem) = refs[:n], refs[n:2 * n], refs[2 * n:]
        x, y, c = _me()
        copies = []
        for i, (g_ref, o_ref) in enumerate(zip(ins, outs)):
            for k in range(N_CHIPS):
                s = i * N_CHIPS + k
                cp = _remote(g_ref.at[k, _half_rows(g_ref.shape[1], 1 - c)], o_ref.at[k], ssem.at[s], rsem.at[s],
                             (x, y, 1 - c))
                cp.start()
                copies.append(cp)
        for cp in copies:
            cp.wait()

    return _pcall(
        body, name=name, in_specs=[ANY] * n, out_specs=[ANY] * n,
        out_shape=[jax.ShapeDtypeStruct((N_CHIPS, g.shape[1] // 2, g.shape[2]), g.dtype) for g in grads],
        scratch_shapes=[pltpu.SemaphoreType.DMA((n * N_CHIPS,)), pltpu.SemaphoreType.DMA((n * N_CHIPS,))],
    )(*grads)


SEM = pl.BlockSpec(memory_space=pltpu.SEMAPHORE)
IN_HBM = pl.BlockSpec(memory_space=pltpu.HBM)
SPLIT_PARAMS = dict(compiler_params=pltpu.CompilerParams(has_side_effects=pltpu.SideEffectType.DATAFLOW_SIDE_EFFECTING))


def _split_start(name, sources, landings, n_copies, plan):
    ns, nl = len(sources), len(landings)

    def body(*refs):
        ins, lands, (ssem, rsem), token = refs[:ns], refs[ns:ns + nl], refs[ns + nl:ns + nl + 2], refs[-1]
        for s, (src, dst, _, dev) in enumerate(plan(ins, lands)):
            _remote(src, dst, ssem.at[s], rsem.at[s], dev).start()
        token[...] = jnp.zeros_like(token)

    arrays = list(sources) + list(landings)
    outs = _call(
        body, name=name, in_specs=[IN_HBM] * (ns + nl),
        out_specs=[SEM, SEM] + [IN_HBM] * (ns + nl) + [pl.BlockSpec(memory_space=pltpu.VMEM)],
        out_shape=[pltpu.SemaphoreType.DMA((n_copies,)), pltpu.SemaphoreType.DMA((n_copies,))]
        + [pltpu.HBM(a.shape, a.dtype) for a in arrays] + [jax.ShapeDtypeStruct((SUBLANES, LANES), F32)],
        input_output_aliases={i: 2 + i for i in range(ns + nl)}, **SPLIT_PARAMS,
    )(*[pltpu.with_memory_space_constraint(a, pltpu.HBM) for a in arrays])
    return (outs[:-1], ns), outs[-1]


def _split_wait(name, handle, after, plan):
    (ssem, rsem, *thru), ns = handle
    nl = len(thru) - ns

    def body(*refs):
        ins, lands, (ssem_ref, rsem_ref) = refs[:ns], refs[ns:ns + nl], refs[ns + nl:ns + nl + 2]
        for s, (src, _, dst, dev) in enumerate(plan(ins, lands)):
            cp = _remote(src, dst, ssem_ref.at[s], rsem_ref.at[s], dev)
            cp.wait_send()
            cp.wait_recv()

    outs = _call(
        body, name=name, in_specs=[IN_HBM] * (ns + nl) + [SEM, SEM, ANY], out_specs=[IN_HBM] * (ns + nl),
        out_shape=[pltpu.HBM(t.shape, t.dtype) for t in thru],
        input_output_aliases={i: i for i in range(ns + nl)}, **SPLIT_PARAMS,
    )(*thru, ssem, rsem, after)
    return outs[:ns], outs[ns:]


def _swap_plan(ins, lands):
    x, y, c = _me()
    return [(g_ref.at[k, _half_rows(g_ref.shape[1], 1 - c)], l_ref.at[k], l_ref.at[k], (x, y, 1 - c))
            for g_ref, l_ref in zip(ins, lands) for k in range(N_CHIPS)]


def _swap_start(name, grads):
    lands = [lax.empty((N_CHIPS, g.shape[1] // 2, g.shape[2]), g.dtype) for g in grads]
    return _split_start(name, grads, lands, len(grads) * N_CHIPS, _swap_plan)


def _swap_wait(name, handle, after):
    return _split_wait(name, handle, after, _swap_plan)


def _gather_plan(ins, lands):
    x, y, c = _me()
    k_me = 2 * x + y
    plan = [(w_ref, l_ref.at[k_me], l_ref.at[k_me], (x, y, 1 - c)) for w_ref, l_ref in zip(ins, lands)]
    for px, py in _other_chips(x, y):
        for w_ref, l_ref in zip(ins, lands):
            rows = _half_rows(w_ref.shape[0], c)
            plan.append((w_ref.at[rows], l_ref.at[k_me, rows], l_ref.at[2 * px + py, rows], (px, py, c)))
    return plan


def _gather_start(name, shards):
    lands = [lax.empty((N_CHIPS,) + s.shape, s.dtype) for s in shards]
    return _split_start(name, shards, lands, len(shards) * N_CHIPS, _gather_plan)


def _gather_wait(name, handle, after):
    return _split_wait(name, handle, after, _gather_plan)[1]


def _forward_halves(name, blocks):
    n = len(blocks)
    n_sem = n * (N_CHIPS - 1)

    def body(*refs):
        outs, (ssem, rsem) = refs[n:2 * n], refs[2 * n:]
        x, y, c = _me()
        sib = (x, y, 1 - c)
        chips = _other_chips(x, y)
        sends = []
        for r, (px, py) in enumerate(chips):
            for i, o_ref in enumerate(outs):
                blk = o_ref.at[2 * px + py, _half_rows(o_ref.shape[1], c)]
                cp = _remote(blk, blk, ssem.at[r * n + i], rsem.at[r * n + i], sib)
                cp.start()
                sends.append(cp)
        for r, (px, py) in enumerate(chips):
            for i, o_ref in enumerate(outs):
                blk = o_ref.at[2 * px + py, _half_rows(o_ref.shape[1], 1 - c)]
                _remote(blk, blk, ssem.at[r * n + i], rsem.at[r * n + i], sib).wait_recv()
        for cp in sends:
            cp.wait_send()

    return _pcall(
        body, name=name, in_specs=[ANY] * n, out_specs=[ANY] * n,
        out_shape=[jax.ShapeDtypeStruct(b.shape, b.dtype) for b in blocks],
        input_output_aliases={i: i for i in range(n)},
        scratch_shapes=[pltpu.SemaphoreType.DMA((n_sem,)), pltpu.SemaphoreType.DMA((n_sem,))],
    )(*blocks)


def _scatter_plan(ins, lands):
    x, y, c = _me()
    k_me = 2 * x + y
    return [(p_ref.at[2 * px + py], l_ref.at[k_me], l_ref.at[2 * px + py], (px, py, c))
            for px, py in _other_chips(x, y) for p_ref, l_ref in zip(ins, lands)]


def _scatter_start(name, parts):
    lands = [lax.empty(p.shape, p.dtype) for p in parts]
    return _split_start(name, parts, lands, len(parts) * (N_CHIPS - 1), _scatter_plan)


def _scatter_wait(name, handle, after):
    return _split_wait(name, handle, after, _scatter_plan)[1]


def _share_halves(halves):
    n = len(halves)

    def body(*refs):
        ins, outs, (ssem, rsem) = refs[:n], refs[n:2 * n], refs[2 * n:]
        x, y, c = _me()
        copies = [_remote(r_ref, o_ref, ssem.at[i], rsem.at[i], (x, y, 1 - c))
                  for i, (r_ref, o_ref) in enumerate(zip(ins, outs))]
        for cp in copies:
            cp.start()
        for cp in copies:
            cp.wait()

    return _pcall(
        body, name="share_halves", in_specs=[ANY] * n, out_specs=[ANY] * n,
        out_shape=[jax.ShapeDtypeStruct(h.shape, h.dtype) for h in halves],
        scratch_shapes=[pltpu.SemaphoreType.DMA((n,)), pltpu.SemaphoreType.DMA((n,))],
    )(*halves)


def _reduce_small(v):
    rows = v.shape[0]
    half = rows // 2
    assert half % SUBLANES == 0

    def body(v_ref, out_ref, pair_buf, mine, chip_buf, ssem, rsem):
        x, y, c = _me()
        k_me = 2 * x + y
        sib = (x, y, 1 - c)

        def rows_of(cc):
            return pl.ds(pl.multiple_of(cc * half, SUBLANES), half)

        swap = _remote(v_ref.at[rows_of(1 - c)], pair_buf, ssem.at[0], rsem.at[0], sib)
        swap.start()
        swap.wait()
        mine[...] = v_ref[rows_of(c), :] + pair_buf[...]
        chip_buf[k_me] = mine[...]
        sends = [_remote(mine, chip_buf.at[k_me], ssem.at[1 + r], rsem.at[1 + r], (px, py, c))
                 for r, (px, py) in enumerate(_other_chips(x, y))]
        for cp in sends:
            cp.start()
        for r, (px, py) in enumerate(_other_chips(x, y)):
            blk = chip_buf.at[2 * px + py]
            _remote(blk, blk, ssem.at[1 + r], rsem.at[1 + r], (px, py, c)).wait_recv()
        total = chip_buf[0]
        for k in range(1, N_CHIPS):
            total = total + chip_buf[k]
        out_ref[rows_of(c), :] = total
        for cp in sends:
            cp.wait_send()
        share = _remote(out_ref.at[rows_of(c)], out_ref.at[rows_of(c)], ssem.at[N_CHIPS], rsem.at[N_CHIPS], sib)
        share.start()
        got = out_ref.at[rows_of(1 - c)]
        _remote(got, got, ssem.at[N_CHIPS], rsem.at[N_CHIPS], sib).wait_recv()
        share.wait_send()

    vm = pl.BlockSpec(memory_space=pltpu.VMEM)
    return _pcall(
        body, pin=False, name="reduce_small", in_specs=[vm], out_specs=vm,
        out_shape=jax.ShapeDtypeStruct((rows, LANES), F32),
        scratch_shapes=[pltpu.VMEM((half, LANES), F32), pltpu.VMEM((half, LANES), F32),
                        pltpu.VMEM((N_CHIPS, half, LANES), F32), pltpu.SemaphoreType.DMA((N_CHIPS + 1,)),
                        pltpu.SemaphoreType.DMA((N_CHIPS + 1,))],
        compiler_params=pltpu.CompilerParams(vmem_limit_bytes=32 * 1024 * 1024),
    )(v)


def _gather_small(v):
    rows = v.shape[0]

    def body(v_ref, out_ref, ssem, rsem):
        x, y, c = _me()
        me = 4 * x + 2 * y + c
        peers = [((x + bx) % 2, (y + by) % 2, (c + bc) % 2)
                 for bx in (0, 1) for by in (0, 1) for bc in (0, 1) if (bx, by, bc) != (0, 0, 0)]
        out_ref[me] = v_ref[...]
        sends = [_remote(v_ref, out_ref.at[me], ssem.at[r], rsem.at[r], p) for r, p in enumerate(peers)]
        for cp in sends:
            cp.start()
        for r, (px, py, pc) in enumerate(peers):
            blk = out_ref.at[4 * px + 2 * py + pc]
            _remote(blk, blk, ssem.at[r], rsem.at[r], (px, py, pc)).wait_recv()
        for cp in sends:
            cp.wait_send()

    vm = pl.BlockSpec(memory_space=pltpu.VMEM)
    return _pcall(
        body, pin=False, name="gather_small", in_specs=[vm], out_specs=vm,
        out_shape=jax.ShapeDtypeStruct((N_DEV, rows, LANES), F32),
        scratch_shapes=[pltpu.SemaphoreType.DMA((N_DEV - 1,)), pltpu.SemaphoreType.DMA((N_DEV - 1,))],
        compiler_params=pltpu.CompilerParams(vmem_limit_bytes=32 * 1024 * 1024),
    )(v)


def _add_pair(name, core, chip, g, theirs):
    _, half, cols = theirs.shape
    tr = _tile(half, (256, 176, 128))
    nb = half // tr

    def body(c_ref, k_ref, g_ref, t_ref, o32_ref, o16_ref):
        s = g_ref[...] + t_ref[...]
        o16_ref[...] = s.astype(BF16)

        @pl.when(pl.program_id(1) == k_ref[0])
        def _():
            o32_ref[...] = s

    spec = pl.BlockSpec((None, tr, cols), lambda i, k, c_ref, k_ref: (k, i, 0))
    grid_spec = pltpu.PrefetchScalarGridSpec(
        num_scalar_prefetch=2, grid=(nb, N_CHIPS),
        in_specs=[pl.BlockSpec((None, tr, cols), lambda i, k, c_ref, k_ref: (k, c_ref[0] * nb + i, 0)), spec],
        out_specs=[pl.BlockSpec((tr, cols), lambda i, k, c_ref, k_ref: (i, 0)), spec])
    return _pcall(
        body, name=name, grid_spec=grid_spec,
        out_shape=[jax.ShapeDtypeStruct((half, cols), F32), jax.ShapeDtypeStruct(theirs.shape, BF16)],
        compiler_params=_params(("arbitrary", "arbitrary"), 8 * _nbytes((tr, cols + LANES), F32)),
    )(core, chip, g, theirs)


def _add_chips(name, chip, p32, recv):
    half, cols = p32.shape
    tr = _tile(half, (256, 176, 128))

    def body(k_ref, p_ref, r0_ref, r1_ref, r2_ref, o_ref):
        o_ref[...] = ((p_ref[...] + r0_ref[...].astype(F32)) + r1_ref[...].astype(F32)) + r2_ref[...].astype(F32)

    def other(r):
        return pl.BlockSpec((None, tr, cols), lambda i, k_ref: (r + (k_ref[0] <= r).astype(jnp.int32), i, 0))
    grid_spec = pltpu.PrefetchScalarGridSpec(
        num_scalar_prefetch=1, grid=(half // tr,),
        in_specs=[pl.BlockSpec((tr, cols), lambda i, k_ref: (i, 0)), other(0), other(1), other(2)],
        out_specs=pl.BlockSpec((tr, cols), lambda i, k_ref: (i, 0)))
    return _pcall(
        body, name=name, grid_spec=grid_spec, out_shape=jax.ShapeDtypeStruct((half, cols), F32),
        compiler_params=_params(("arbitrary",), 10 * _nbytes((tr, cols + LANES), F32)),
    )(chip, p32, recv, recv, recv)


def kernel(x, mem, w_in, b_in, hg_lb_logits, hg_norm_w, ml_conv_w, ml_conv_b, ml_norm_w, w_out, ln1_g, ln1_b, ca_wq, ca_wkv, ca_wo, ln2_g, ln2_b, ffn_w_up, ffn_conv_w, ffn_conv_b, ffn_w_down, ln3_g, ln3_b, loss_target, m_w_in, m_b_in, m_hg_lb_logits, m_hg_norm_w, m_ml_conv_w, m_ml_conv_b, m_ml_norm_w, m_w_out, m_ln1_g, m_ln1_b, m_ca_wq, m_ca_wkv, m_ca_wo, m_ln2_g, m_ln2_b, m_ffn_w_up, m_ffn_conv_w, m_ffn_conv_b, m_ffn_w_down, m_ln3_g, m_ln3_b, v_w_in, v_b_in, v_hg_lb_logits, v_hg_norm_w, v_ml_conv_w, v_ml_conv_b, v_ml_norm_w, v_w_out, v_ln1_g, v_ln1_b, v_ca_wq, v_ca_wkv, v_ca_wo, v_ln2_g, v_ln2_b, v_ffn_w_up, v_ffn_conv_w, v_ffn_conv_b, v_ffn_w_down, v_ln3_g, v_ln3_b):
    return _train_step(dict(locals()))


WEIGHTS = ("w_in", "b_in", "hg_lb_logits", "hg_norm_w", "ml_conv_w", "ml_conv_b", "ml_norm_w", "w_out", "ln1_g",
           "ln1_b", "ca_wq", "ca_wkv", "ca_wo", "ln2_g", "ln2_b", "ffn_w_up", "ffn_conv_w", "ffn_conv_b",
           "ffn_w_down", "ln3_g", "ln3_b")
MATRICES = ("w_in", "w_out", "ca_wq", "ca_wkv", "ca_wo", "ffn_w_up", "ffn_w_down")
COL_SHARDED = ("w_in", "ca_wkv", "ffn_w_up", "ml_conv_w", "ffn_conv_w")
SMALL = tuple(n for n in WEIGHTS if n not in MATRICES)
PART_ROWS = 16


def _part_rows(shape):
    n = 1
    for s in shape:
        n *= s
    return -(-n // (LANES * PART_ROWS)) * PART_ROWS


def _pack(arrs, dtype):
    parts = []
    for a in arrs:
        flat = a.reshape(-1).astype(dtype)
        flat = jnp.pad(flat, (0, _part_rows(a.shape) * LANES - flat.shape[0]))
        parts.append(flat.reshape(-1, LANES))
    return jnp.concatenate(parts, axis=0)


def _unpack(buf, shapes):
    lead = buf.shape[:-2]
    outs, r = [], 0
    for sh in shapes:
        n = 1
        for s in sh:
            n *= s
        nr = _part_rows(sh)
        flat = buf[..., r:r + nr, :].reshape(lead + (nr * LANES,))
        outs.append(flat[..., :n].reshape(lead + tuple(sh)))
        r += nr
    return outs


def _cat_cols(s):
    return jnp.moveaxis(s, 0, 1).reshape(s.shape[1], -1)


def _stack_rows(s):
    return s.reshape(-1, s.shape[-1])


def _train_step(a):
    xs, mems, tgt = a["x"][0], a["mem"][0], a["loss_target"][0]
    core = lax.axis_index("c").astype(jnp.int32).reshape(1)
    chip = (2 * lax.axis_index("x") + lax.axis_index("y")).astype(jnp.int32).reshape(1)
    k_me = chip[0]
    shard = {n: a[n][0] for n in MATRICES}

    later = [n for n in MATRICES if n != "w_in"]
    taps = _gather_small(_pack([a["ml_conv_w"][0], a["ffn_conv_w"][0]], F32))
    w = {"w_in": jnp.pad(_cat_cols(_gather_weights([shard["w_in"].astype(BF16)])[0]), ((0, 0), (0, D_IN_PAD - D_IN)))}
    gathering, token = _gather_start("gather_start", [shard[n].astype(BF16) for n in later])
    taps = taps.reshape((N_CHIPS, 2) + taps.shape[1:])[:, 0]
    ml_cw, ffn_cw = [_cat_cols(s) for s in _unpack(taps, [a["ml_conv_w"].shape[1:], a["ffn_conv_w"].shape[1:]])]
    b_in_p = jnp.pad(a["b_in"], ((0, 0), (0, D_IN_PAD - D_IN))) + token[0:1, 0:1]
    mixer_w = (a["hg_lb_logits"], a["hg_norm_w"], ml_cw, a["ml_conv_b"], a["ml_norm_w"])
    up_cols = a["ffn_w_up"].shape[-1]

    xb = xs.astype(BF16)
    proj = _mm("proj", "nn", xb, w["w_in"], bias=b_in_p, tm=256, tn=D_IN_PAD)
    y, hst, cst, nst, mst = _mixer_fwd(proj, *mixer_w)
    w.update(zip(later, _forward_halves("forward_halves", _gather_wait("gather_wait", gathering, y))))
    for n in ("w_out", "ca_wq", "ca_wo", "ffn_w_down"):
        w[n] = _stack_rows(w[n])
    z1, x1, x1b = _mm("mix_out", "nn", y, w["w_out"], res=xs, res_scale=ALPHA, ln=("fwd", a["ln1_g"], a["ln1_b"]),
                      copy_dtype=BF16)
    q = _mm("ca_q", "nn", x1b, w["ca_wq"], out_dtype=BF16, tn=D_MODEL)
    kv = _mm("ca_kv", "nn", mems, w["ca_wkv"])
    o = _attn_fwd(q, kv)
    z2, x2, x2b = _mm("ca_out", "nn", o, w["ca_wo"], res=x1, res_scale=ALPHA, ln=("fwd", a["ln2_g"], a["ln2_b"]),
                      copy_dtype=BF16)
    w_up = w["ffn_w_up"]
    assert w_up.shape == (2 * FFN_J, D_MODEL, FFN_W)
    u, hmid, dz3, g_ln3g, g_ln3b, loss_part, dz3b = _ffn_fwd(
        x2b, x2, w_up, ffn_cw, a["ffn_conv_b"], w["ffn_w_down"], a["ln3_g"], a["ln3_b"], tgt)

    grads = {"ln3_g": g_ln3g, "ln3_b": g_ln3b}
    grads["ffn_w_down"] = _mm("g_w_down", "tn", hmid, dz3b, tm=D_FF // 2, tn=D_MODEL)
    du, g_cw, g_cb, dz2, grads["ln2_g"], grads["ln2_b"], dz2b = _ffn_bwd(
        u, ffn_cw, a["ffn_conv_b"], dz3b, dz3, w["ffn_w_down"], w_up, z2, a["ln2_g"], a["ln2_b"])
    grads["ffn_conv_w"] = jnp.transpose(g_cw, (2, 1, 0, 3)).reshape(FFN_CONV, 2 * D_FF)
    grads["ffn_conv_b"] = jnp.transpose(g_cb, (2, 1, 0, 3)).reshape(1, 2 * D_FF)
    grads["ffn_w_up"] = _mm("g_w_up", "tn", x2b, du, out_groups=N_CHIPS, tm=D_MODEL, tn=up_cols)
    grads["ffn_w_down"] = grads["ffn_w_down"].reshape((N_CHIPS,) + shard["ffn_w_down"].shape)
    pending = {}

    def reduce_start(tag, names, swapped=None):
        group = [grads[n] for n in names]
        group, theirs = swapped or (group, _swap_halves("swap_halves_" + tag, group))
        sums = [_add_pair("add_pair_" + n, core, chip, g, t) for n, g, t in zip(names, group, theirs)]
        handle, token = _scatter_start("scatter_start_" + tag, [s16 for _, s16 in sums])
        pending[tag] = (names, [s32 for s32, _ in sums], handle)
        return token[0:1, 0:1]

    ffn = ("ffn_w_up", "ffn_w_down")
    swapping, token = _swap_start("swap_start_ffn", [grads[n] for n in ffn])
    do = _mm("d_o", "nt", dz2b, w["ca_wo"], bias=jnp.zeros((1, D_MODEL), F32) + token[0:1, 0:1], out_dtype=BF16,
             tn=D_MODEL)
    grads["ca_wo"] = _mm("g_wo", "tn", o, dz2b, tm=D_MODEL, tn=D_MODEL)
    zero = reduce_start("ffn", ffn, _swap_wait("swap_wait_ffn", swapping, grads["ca_wo"]))
    dq, dkv = _attn_bwd(q, kv + zero, do)
    grads["ca_wq"] = _mm("g_wq", "tn", x1b, dq, tm=D_MODEL, tn=D_MODEL)
    grads["ca_wkv"] = _mm("g_wkv", "tn", mems, dkv, out_groups=N_CHIPS, tm=D_MODEL)
    dz1, grads["ln1_g"], grads["ln1_b"], dz1b = _mm("d_x1", "nt", dq, w["ca_wq"], res=dz2, res_scale=ALPHA,
                                                    ln=("bwd", z1, a["ln1_g"], a["ln1_b"]), copy_dtype=BF16)
    grads["w_out"] = _mm("g_w_out", "tn", y, dz1b, tm=D_MODEL, tn=D_MODEL)
    for n in ("w_out", "ca_wq", "ca_wo"):
        grads[n] = grads[n].reshape((N_CHIPS,) + shard[n].shape)
    attn = ("w_out", "ca_wq", "ca_wkv", "ca_wo")
    swapping, token = _swap_start("swap_start_attn", [grads[n] for n in attn])
    dy = _mm("d_y", "nt", dz1b, w["w_out"], bias=jnp.zeros((1, D_MODEL), F32) + token[0:1, 0:1], tn=D_MODEL)
    zero = reduce_start("attn", attn, _swap_wait("swap_wait_attn", swapping, dy))
    (dproj, g_b_in, grads["hg_lb_logits"], grads["hg_norm_w"], grads["ml_conv_w"], grads["ml_conv_b"],
     grads["ml_norm_w"]) = _mixer_bwd(proj, dy, hst, cst, nst, mst, mixer_w[0], mixer_w[1] + zero, *mixer_w[2:])
    g_in = _mm("g_w_in", "tn", xb, dproj, tm=D_MODEL, tn=up_cols)[:, :D_IN]
    grads["w_in"] = jnp.moveaxis(g_in.reshape(D_MODEL, N_CHIPS, -1), 1, 0)
    grads["b_in"] = g_b_in[:, :D_IN]
    zero = reduce_start("in", ("w_in",))
    dx = _mm("d_x", "nt", dproj, w["w_in"], bias=jnp.zeros((1, D_MODEL), F32) + zero, res=dz1, res_scale=ALPHA,
             tm=256, tn=D_MODEL)

    halves = {}
    for tag, (names, sums32, handle) in pending.items():
        for n, s32, r in zip(names, sums32, _scatter_wait("scatter_wait_" + tag, handle, dx)):
            halves[n] = _add_chips("add_chips_" + n, chip, s32, r)
    halves = [halves[n] for n in MATRICES]
    other_halves = _share_halves(halves)

    small_shapes = [grads[n].shape for n in SMALL] + [loss_part.shape]
    summed = _unpack(_reduce_small(_pack([grads[n] for n in SMALL] + [loss_part], F32)), small_shapes)
    loss = summed[-1][0, 0]
    for n, g in zip(SMALL, summed[:-1]):
        if n in COL_SHARDED:
            cols = a[n].shape[-1]
            g = lax.dynamic_slice_in_dim(g, k_me * cols, cols, axis=1)
        grads[n] = g

    delta, new_m, new_v = {}, {}, {}
    for n, mine, theirs in zip(MATRICES, halves, other_halves):
        grads[n], delta[n], new_m[n], new_v[n] = _adamw_halves(
            "adamw_" + n, core, shard[n], mine, theirs, a["m_" + n][0], a["v_" + n][0])
    small_w = [a[n][0] if a[n].ndim == 3 else a[n] for n in SMALL]
    small_m = [a["m_" + n][0] if a[n].ndim == 3 else a["m_" + n] for n in SMALL]
    small_v = [a["v_" + n][0] if a[n].ndim == 3 else a["v_" + n] for n in SMALL]
    shapes = [w.shape for w in small_w]
    packed = [_pack(l, F32) for l in (small_w, [grads[n] for n in SMALL], small_m, small_v)]
    for out, buf in zip((delta, new_m, new_v), _adamw("adamw_small", *packed)):
        for n, v in zip(SMALL, _unpack(buf, shapes)):
            out[n] = v

    def shaped(d):
        return [d[n].reshape(a[n].shape) for n in WEIGHTS]
    return (loss, dx[None], *shaped(grads), *shaped(delta), *shaped(new_m), *shaped(new_v))
```

```python
import functools

import jax
import jax.numpy as jnp
from jax import lax
from jax.experimental import pallas as pl
from jax.experimental.pallas import tpu as pltpu

F32 = jnp.float32
BF16 = jnp.bfloat16

D_MODEL = 1024
HEADS = 4
DK = 128
D_GRP = HEADS * DK
CHUNK = 64
ML_CONV = 4
FFN_CONV = 3
D_FF = 2816
CA_DH = D_MODEL // HEADS
DEPTH = 1
ALPHA = (2.0 * DEPTH) ** 0.25
LN_EPS = 1e-5
NEG_BIG = -1e30
D_IN = 8 * D_GRP + 2 * HEADS
D_IN_PAD = 8 * D_GRP + 128
ADAM_LR, ADAM_B1, ADAM_B2, ADAM_EPS, ADAM_WD, ADAM_STEP = 0.001, 0.9, 0.999, 1e-08, 0.01, 10

SUBLANES = 8
LANES = 128
VMEM_BYTES = 64 * 1024 * 1024


def _pcall(body, pin=True, **kw):
    if not pin:
        return _call(body, **kw)
    kw["out_shape"] = jax.tree.map(lambda s: pltpu.HBM(s.shape, s.dtype), kw["out_shape"])
    call = _call(body, **kw)

    def pinned(*args):
        return call(*[pltpu.with_memory_space_constraint(x, pltpu.HBM) if jnp.issubdtype(x.dtype, jnp.floating) else x
                      for x in args])
    return pinned


def _call(body, **kw):
    return pl.pallas_call(body, **kw)


def _params(semantics, vmem_bytes):
    limit = int(min(max(2 * vmem_bytes, 16 * 1024 * 1024), VMEM_BYTES - 8 * 1024 * 1024))
    return pltpu.CompilerParams(dimension_semantics=semantics, vmem_limit_bytes=limit)


def _nbytes(shape, dtype):
    n = 1
    for s in shape:
        n *= s
    return n * jnp.dtype(dtype).itemsize


def _dg(a, b, ca, cb):
    return lax.dot_general(a.astype(BF16), b.astype(BF16), (((ca,), (cb,)), ((), ())),
                           preferred_element_type=F32)


@jax.custom_vjp
def mm_nn(a, b):
    return _dg(a, b, 1, 0)


mm_nn.defvjp(lambda a, b: (_dg(a, b, 1, 0), (a, b)),
             lambda r, g: (_dg(g, r[1], 1, 1).astype(r[0].dtype), _dg(r[0], g, 0, 0).astype(r[1].dtype)))


@jax.custom_vjp
def mm_nt(a, b):
    return _dg(a, b, 1, 1)


mm_nt.defvjp(lambda a, b: (_dg(a, b, 1, 1), (a, b)),
             lambda r, g: (_dg(g, r[1], 1, 0).astype(r[0].dtype), _dg(g, r[0], 0, 0).astype(r[1].dtype)))


@jax.custom_vjp
def mm_tn(a, b):
    return _dg(a, b, 0, 0)


mm_tn.defvjp(lambda a, b: (_dg(a, b, 0, 0), (a, b)),
             lambda r, g: (_dg(r[1], g, 1, 1).astype(r[0].dtype), _dg(r[0], g, 1, 0).astype(r[1].dtype)))


def _tri(n, lower):
    r = lax.broadcasted_iota(jnp.int32, (n, n), 0)
    c = lax.broadcasted_iota(jnp.int32, (n, n), 1)
    return ((r >= c) if lower else (r <= c)).astype(F32)


def _tri_dot(lower, x):
    t = _tri(x.shape[0], lower).astype(BF16)
    hi = x.astype(BF16)
    rest = x - hi.astype(F32)
    mid = rest.astype(BF16)
    lo = (rest - mid.astype(F32)).astype(BF16)
    return sum(lax.dot_general(t, p, (((1,), (0,)), ((), ())), preferred_element_type=F32) for p in (hi, mid, lo))


@jax.custom_vjp
def cumsum_rows(x):
    return _tri_dot(True, x)


cumsum_rows.defvjp(lambda x: (_tri_dot(True, x), None), lambda _, g: (_tri_dot(False, g),))


def _shift_impl(halo, x, d):
    xx = jnp.concatenate([halo, x], axis=0)
    return pltpu.roll(xx, d, 0)[SUBLANES:]


@functools.partial(jax.custom_vjp, nondiff_argnums=(2,))
def shift_rows(halo, x, d):
    return _shift_impl(halo, x, d)


def _shift_bwd(d, _, g):
    n = g.shape[0] + SUBLANES
    gg = jnp.concatenate([jnp.zeros((SUBLANES, g.shape[1]), g.dtype), g], axis=0)
    r = pltpu.roll(gg, n - d, 0)
    return r[:SUBLANES], r[SUBLANES:]


shift_rows.defvjp(lambda halo, x, d: (_shift_impl(halo, x, d), None), _shift_bwd)


def causal_conv(halo, x, w_rows, b):
    k = len(w_rows)
    y = b + w_rows[k - 1] * x
    for d in range(1, k):
        y = y + w_rows[k - 1 - d] * shift_rows(halo, x, d)
    return y


def _sigmoid(x):
    return 1.0 / (1.0 + jnp.exp(-x))


def _silu(x):
    return x * _sigmoid(x)


def _log_sigmoid(x):
    return jnp.minimum(x, 0.0) - jnp.log(1.0 + jnp.exp(-jnp.abs(x)))


def _pick_row(x, i):
    row = lax.broadcasted_iota(jnp.int32, (x.shape[0], 1), 0)
    return jnp.sum(jnp.where(row == i, x, 0.0), axis=0, keepdims=True)


def _layer_norm(z, g, b):
    mu = jnp.mean(z, axis=-1, keepdims=True)
    zc = z - mu
    var = jnp.mean(zc * zc, axis=-1, keepdims=True)
    return zc * lax.rsqrt(var + LN_EPS) * g + b


def _qk_conv(halo, x, w0, w1, w2, w3, b):
    return _silu(causal_conv(halo, x, (w0, w1, w2, w3), b))


def _grp(i, h=None):
    if h is None:
        return pl.ds(i * D_GRP, D_GRP)
    return pl.ds(i * D_GRP + h * DK, DK)


def _mixer_specs(n_chunks, reverse):
    def chunk(c):
        return n_chunks - 1 - c if reverse else c
    row8 = CHUNK // SUBLANES
    proj_spec = pl.BlockSpec((CHUNK, D_IN_PAD), lambda c: (chunk(c), 0))
    halo_spec = pl.BlockSpec((SUBLANES, 2 * D_GRP), lambda c: (jnp.maximum(chunk(c) * row8 - 1, 0), 2))
    small = [pl.BlockSpec((2, D_GRP), lambda c: (0, 0)), pl.BlockSpec((1, D_GRP), lambda c: (0, 0)),
             pl.BlockSpec((ML_CONV, 2 * D_GRP), lambda c: (0, 0)), pl.BlockSpec((1, 2 * D_GRP), lambda c: (0, 0)),
             pl.BlockSpec((1, D_GRP), lambda c: (0, 0))]
    state_specs = [pl.BlockSpec((1, HEADS, DK, DK), lambda c: (chunk(c), 0, 0, 0)),
                   pl.BlockSpec((1, HEADS, DK, DK), lambda c: (chunk(c), 0, 0, 0)),
                   pl.BlockSpec((1, HEADS, 1, DK), lambda c: (chunk(c), 0, 0, 0)),
                   pl.BlockSpec((1, HEADS, 1, DK), lambda c: (chunk(c), 0, 0, 0))]
    y_spec = pl.BlockSpec((CHUNK, 2 * D_GRP), lambda c: (chunk(c), 0))
    return proj_spec, halo_spec, small, state_specs, y_spec, chunk


def _heads(x):
    return [x[:, h * DK:(h + 1) * DK] for h in range(HEADS)]


def _last(x, j):
    lane = lax.broadcasted_iota(jnp.int32, (1, x.shape[-1]), 1)
    return jnp.sum(jnp.where(lane == j, x, 0.0), axis=-1, keepdims=True)


def _hg_chunk(st_t, hq, hf, hi, hgate, l0, l1, nw):
    n = hq.shape[0]
    lb = _sigmoid(l0 - l1)
    q = _silu(hq)
    lf = jnp.log(lb + (1.0 - lb) * _sigmoid(hf))
    k = (1.0 - lb) * _sigmoid(-hf)
    b = cumsum_rows(lf)
    b_ref = _pick_row(b, n // 2 - 1)
    b_last = _pick_row(b, n - 1)
    qa, ka =_heads(q * jnp.exp(b - b_ref)), _heads(k * jnp.exp(b_ref - b))
    qe, kd, eb, v = _heads(q * jnp.exp(b)), _heads(k * jnp.exp(b_last - b)), _heads(jnp.exp(b_last)), _heads(hi)
    tri = _tri(n, True) > 0
    attn = [jnp.where(tri, mm_nt(qa[h], ka[h]), 0.0) for h in range(HEADS)]
    o = [mm_nn(attn[h], v[h]) + mm_nt(qe[h], st_t[h]) for h in range(HEADS)]
    st_new = jnp.stack([eb[h] * st_t[h] + mm_tn(v[h], kd[h]) for h in range(HEADS)])
    yn = [o[h] * lax.rsqrt(jnp.mean(o[h] * o[h], axis=-1, keepdims=True) + LN_EPS) for h in range(HEADS)]
    return st_new, jnp.concatenate(yn, axis=1) * nw * _silu(hgate)


def _ml_chunk(c_st, n_st, m_st, q, k, v, gates, og, nw):
    n = q.shape[0]
    ig = jnp.stack([_last(gates, h) for h in range(HEADS)])
    log_f = _log_sigmoid(gates)
    fl = jnp.stack([_last(log_f, HEADS + h) for h in range(HEADS)])
    bw = cumsum_rows(jnp.concatenate([jnp.broadcast_to(fl[h], (n, DK)) for h in range(HEADS)], axis=1))
    b = jnp.stack([_last(x, 0) for x in _heads(bw)])
    g = jnp.sum(fl, axis=1, keepdims=True)
    eye = lax.broadcasted_iota(jnp.int32, (n, n), 0) == lax.broadcasted_iota(jnp.int32, (n, n), 1)
    e_row = jnp.sum(jnp.where(eye, ig - b, 0.0), axis=1, keepdims=True)
    d = jnp.where(_tri(n, True) > 0, b + e_row, -jnp.inf)
    inter = b + m_st
    m_t = jnp.maximum(inter, jnp.max(d, axis=2, keepdims=True))
    qs, kh, vh = _heads(q * (DK ** -0.5)), _heads(k), _heads(v)
    s = jnp.stack([mm_nt(qs[h], kh[h]) for h in range(HEADS)]) * jnp.exp(d - m_t)
    w_inter = jnp.exp(inter - m_t)
    num = (jnp.stack([mm_nn(s[h], vh[h]) for h in range(HEADS)])
           + w_inter * jnp.stack([mm_nn(qs[h], c_st[h]) for h in range(HEADS)]))
    den = jnp.sum(s, axis=2, keepdims=True) + w_inter * jnp.sum(jnp.stack(qs) * n_st, axis=2, keepdims=True)
    h_out = num / jnp.maximum(jnp.abs(den), jnp.exp(-m_t))
    a = g - b + ig
    m_new = jnp.maximum(g + m_st, jnp.max(a, axis=1, keepdims=True))
    decay = jnp.exp(g + m_st - m_new)
    wk = jnp.stack(kh) * jnp.exp(a - m_new)
    c_new = decay * c_st + jnp.stack([mm_tn(wk[h], vh[h]) for h in range(HEADS)])
    n_new = decay * n_st + jnp.sum(wk, axis=1, keepdims=True)
    hc = h_out - jnp.mean(h_out, axis=-1, keepdims=True)
    yn = hc * lax.rsqrt(jnp.mean(hc * hc, axis=-1, keepdims=True) + LN_EPS)
    y = _sigmoid(og) * (jnp.concatenate([yn[h] for h in range(HEADS)], axis=1) * nw)
    return c_new, n_new, m_new, y


def _mixer_inputs(proj_ref, lg_ref, hnw_ref, mnw_ref, qk):
    hg_in = (proj_ref[:, _grp(0)], proj_ref[:, _grp(1)], proj_ref[:, _grp(2)], proj_ref[:, _grp(3)],
             lg_ref[0:1, :], lg_ref[1:2, :], hnw_ref[...])
    ml_in = (qk[:, :D_GRP], qk[:, D_GRP:], proj_ref[:, _grp(6)], proj_ref[:, pl.ds(8 * D_GRP, LANES)],
             proj_ref[:, _grp(7)], mnw_ref[...])
    return hg_in, ml_in


def _mixer_fwd(proj, lb_logits, hg_nw, conv_w, conv_b, ml_nw):
    seq = proj.shape[0]
    n_chunks = seq // CHUNK
    proj_spec, halo_spec, small, state_specs, y_spec, _ = _mixer_specs(n_chunks, False)

    def body(proj_ref, halo_ref, lg_ref, hnw_ref, cw_ref, cb_ref, mnw_ref,
             y_ref, hst_ref, cst_ref, nst_ref, mst_ref, hs, cs, ns, ms):
        c = pl.program_id(0)

        @pl.when(c == 0)
        def _():
            hs[...] = jnp.zeros_like(hs)
            cs[...] = jnp.zeros_like(cs)
            ns[...] = jnp.zeros_like(ns)
            ms[...] = jnp.full(ms.shape, NEG_BIG, F32)

        hst_ref[0] = hs[...]
        cst_ref[0] = cs[...]
        nst_ref[0] = ns[...]
        mst_ref[0] = ms[...]
        halo = jnp.where(c > 0, halo_ref[...], 0.0)
        qk = _qk_conv(halo, proj_ref[:, pl.ds(4 * D_GRP, 2 * D_GRP)],
                      cw_ref[0:1, :], cw_ref[1:2, :], cw_ref[2:3, :], cw_ref[3:4, :], cb_ref[...])
        hg_in, ml_in = _mixer_inputs(proj_ref, lg_ref, hnw_ref, mnw_ref, qk)
        hs[...], y_hg = _hg_chunk(hs[...], *hg_in)
        cs[...], ns[...], m_new, y_ml = _ml_chunk(cs[...], ns[...], _last(ms[...], 0), *ml_in)
        ms[...] = jnp.broadcast_to(m_new, ms.shape)
        y_ref[:, pl.ds(0, D_GRP)] = y_hg.astype(BF16)
        y_ref[:, pl.ds(D_GRP, D_GRP)] = y_ml.astype(BF16)

    st = jax.ShapeDtypeStruct((n_chunks, HEADS, DK, DK), F32)
    vec = jax.ShapeDtypeStruct((n_chunks, HEADS, 1, DK), F32)
    vmem = 2 * (_nbytes((CHUNK, D_IN_PAD), F32) + _nbytes((CHUNK, 2 * D_GRP), F32) + 2 * _nbytes((HEADS, DK, DK), F32)) \
        + 2 * _nbytes((HEADS, DK, DK), F32)
    return _pcall(
        body, name="mixer_fwd", grid=(n_chunks,),
        in_specs=[proj_spec, halo_spec] + small,
        out_specs=[y_spec] + state_specs,
        out_shape=[jax.ShapeDtypeStruct((seq, 2 * D_GRP), BF16), st, st, vec, vec],
        scratch_shapes=[pltpu.VMEM((HEADS, DK, DK), F32), pltpu.VMEM((HEADS, DK, DK), F32),
                        pltpu.VMEM((HEADS, 1, DK), F32), pltpu.VMEM((HEADS, 1, DK), F32)],
        compiler_params=_params(("arbitrary",), vmem),
    )(proj, proj, lb_logits, hg_nw, conv_w, conv_b, ml_nw)


def _mixer_bwd(proj, dy, hst, cst, nst, mst, lb_logits, hg_nw, conv_w, conv_b, ml_nw):
    seq = proj.shape[0]
    n_chunks = seq // CHUNK
    proj_spec, halo_spec, small, state_specs, y_spec, _ = _mixer_specs(n_chunks, True)

    def body(proj_ref, halo_ref, dy_ref, hst_ref, cst_ref, nst_ref, mst_ref,
             lg_ref, hnw_ref, cw_ref, cb_ref, mnw_ref,
             dproj_ref, dbin_ref, dlg_ref, dhnw_ref, dcw_ref, dcb_ref, dmnw_ref,
             dhs, dcs, dns, dms, dhalo):
        c = pl.program_id(0)

        @pl.when(c == 0)
        def _():
            for r in (dhs, dcs, dns, dms, dhalo, dbin_ref, dlg_ref, dhnw_ref, dcw_ref, dcb_ref, dmnw_ref):
                r[...] = jnp.zeros_like(r)

        def put(cols, val):
            dproj_ref[:, cols] = val.astype(BF16)
            dbin_ref[:, cols] += jnp.sum(val, axis=0, keepdims=True)

        first = c == n_chunks - 1
        halo = jnp.where(first, 0.0, halo_ref[...])
        x_qk = proj_ref[:, pl.ds(4 * D_GRP, 2 * D_GRP)]
        conv_args = (halo, x_qk, cw_ref[0:1, :], cw_ref[1:2, :], cw_ref[2:3, :], cw_ref[3:4, :], cb_ref[...])
        qk, conv_vjp = jax.vjp(_qk_conv, *conv_args)
        hg_in, ml_in = _mixer_inputs(proj_ref, lg_ref, hnw_ref, mnw_ref, qk)
        _, hg_vjp = jax.vjp(_hg_chunk, hst_ref[0], *hg_in)
        _, ml_vjp = jax.vjp(_ml_chunk, cst_ref[0], nst_ref[0], _last(mst_ref[0], 0), *ml_in)
        dst, dhq, dhf, dhi, dhg, dl0, dl1, dnw = hg_vjp((dhs[...], dy_ref[:, pl.ds(0, D_GRP)]))
        dc, dn, dm, dq, dk, dv, dgates, dog, dmn = ml_vjp(
            (dcs[...], dns[...], _last(dms[...], 0), dy_ref[:, pl.ds(D_GRP, D_GRP)]))
        dhs[...] = dst
        dcs[...] = dc
        dns[...] = dn
        dms[...] = jnp.broadcast_to(dm, dms.shape)
        for i, val in ((0, dhq), (1, dhf), (2, dhi), (3, dhg), (6, dv), (7, dog)):
            put(_grp(i), val)
        put(pl.ds(8 * D_GRP, LANES), dgates)
        dlg_ref[0:1, :] += dl0
        dlg_ref[1:2, :] += dl1
        dhnw_ref[...] += dnw
        dmnw_ref[...] += dmn
        dh, dx, dw0, dw1, dw2, dw3, db = conv_vjp(jnp.concatenate([dq, dk], axis=1))
        tail = jnp.concatenate([jnp.zeros((CHUNK - SUBLANES, 2 * D_GRP), F32), dhalo[...]], axis=0)
        put(pl.ds(4 * D_GRP, 2 * D_GRP), dx + tail)
        dhalo[...] = dh
        for d, dw in enumerate((dw0, dw1, dw2, dw3)):
            dcw_ref[d:d + 1, :] += dw
        dcb_ref[...] += db

    row = pl.BlockSpec((1, D_GRP), lambda c: (0, 0))
    small_out = [pl.BlockSpec((1, D_IN_PAD), lambda c: (0, 0)), pl.BlockSpec((2, D_GRP), lambda c: (0, 0)), row,
                 pl.BlockSpec((ML_CONV, 2 * D_GRP), lambda c: (0, 0)), pl.BlockSpec((1, 2 * D_GRP), lambda c: (0, 0)), row]
    dy_spec = pl.BlockSpec((CHUNK, 2 * D_GRP), y_spec.index_map)
    vmem = 2 * (2 * _nbytes((CHUNK, D_IN_PAD), F32) + _nbytes((CHUNK, 2 * D_GRP), F32)
                + 2 * _nbytes((HEADS, DK, DK), F32)) + 2 * _nbytes((HEADS, DK, DK), F32) + 4 * 1024 * 1024
    return _pcall(
        body, name="mixer_bwd", grid=(n_chunks,),
        in_specs=[proj_spec, halo_spec, dy_spec] + state_specs + small,
        out_specs=[proj_spec] + small_out,
        out_shape=[jax.ShapeDtypeStruct((seq, D_IN_PAD), BF16), jax.ShapeDtypeStruct((1, D_IN_PAD), F32),
                   jax.ShapeDtypeStruct((2, D_GRP), F32), jax.ShapeDtypeStruct((1, D_GRP), F32),
                   jax.ShapeDtypeStruct((ML_CONV, 2 * D_GRP), F32), jax.ShapeDtypeStruct((1, 2 * D_GRP), F32),
                   jax.ShapeDtypeStruct((1, D_GRP), F32)],
        scratch_shapes=[pltpu.VMEM((HEADS, DK, DK), F32), pltpu.VMEM((HEADS, DK, DK), F32),
                        pltpu.VMEM((HEADS, 1, DK), F32), pltpu.VMEM((HEADS, 1, DK), F32),
                        pltpu.VMEM((SUBLANES, 2 * D_GRP), F32)],
        compiler_params=_params(("arbitrary",), vmem),
    )(proj, proj, dy, hst, cst, nst, mst, lb_logits, hg_nw, conv_w, conv_b, ml_nw)


def _tile(n, prefs, unit=None):
    unit = unit or n
    for p in prefs:
        if unit % p == 0 and n % p == 0:
            return p
    return unit


def _logical(arr):
    return arr.shape if arr.ndim == 2 else (arr.shape[1], arr.shape[0] * arr.shape[2])


def _group(arr):
    return arr.shape[-1]


def _split_spec(ndim, group, tr, tc, where):
    if ndim == 2:
        return pl.BlockSpec((tr, tc), where)
    per = group // tc
    assert per * tc == group, (group, tc)

    def index(*ids):
        bi, bj = where(*ids)
        return (bj // per, bi, bj % per)
    return pl.BlockSpec((None, tr, tc), index)


def _mm(name, mode, a, b, *, bias=None, res=None, res_scale=1.0, ln=None, out_dtype=F32, out_groups=None,
        copy_dtype=None, a_copy_dtype=None, tm=None, tn=None, tk=None):
    la, lb = _logical(a), _logical(b)
    if mode == "nn":
        (m, k), n = la, lb[1]
        n_unit = _group(b) if b.ndim == 3 else n
        kc = _group(a) if a.ndim == 3 else k
    elif mode == "nt":
        (m, k), n = la, lb[0]
        n_unit = n
        kc = min(_group(a) if a.ndim == 3 else k, _group(b) if b.ndim == 3 else k)
    else:
        (k, m), n = la, lb[1]
        n_unit, kc = (_group(b) if b.ndim == 3 else n), k
        assert a.ndim == 2
    if out_groups:
        n_unit = min(n_unit, n // out_groups)
    kind = ln[0] if ln else None
    tm = tm or (256 if ln else _tile(m, (512, 256, 128)))
    tn = n if ln else (tn or _tile(n, (512, 384, 256, 128), n_unit))
    tk = (tk or _tile(k, (2048, 512, 256, 128))) if mode == "tn" else k
    gi, gj, gk = m // tm, n // tn, k // tk
    assert gi * tm == m and gj * tn == n and gk * tk == k and n_unit % tn == 0, (name, m, n, k, tm, tn, tk)
    ca, cb = {"nn": (1, 0), "nt": (1, 1), "tn": (0, 0)}[mode]
    i_outer = gk > 1 or (gi - 1) * _nbytes(b.shape, b.dtype) <= (gj - 1) * _nbytes(a.shape, a.dtype)

    def ij(where):
        return (lambda p, q, kk: where(p, q, kk)) if i_outer else (lambda p, q, kk: where(q, p, kk))
    if mode == "tn":
        a_spec = pl.BlockSpec((tk, tm), ij(lambda i, j, kk: (kk, i)))
    elif a.ndim == 3:
        a_spec = pl.BlockSpec((a.shape[0], tm, _group(a)), ij(lambda i, j, kk: (0, i, 0)))
    else:
        a_spec = pl.BlockSpec((tm, k), ij(lambda i, j, kk: (i, 0)))
    if mode != "nt":
        b_spec = _split_spec(b.ndim, _group(b), tk, tn, ij(lambda i, j, kk: (kk, j)))
    elif b.ndim == 3:
        b_spec = pl.BlockSpec((b.shape[0], tn, _group(b)), ij(lambda i, j, kk: (0, j, 0)))
    else:
        b_spec = pl.BlockSpec((tn, k), ij(lambda i, j, kk: (j, 0)))
    row_spec = pl.BlockSpec((1, tn), ij(lambda i, j, kk: (0, j)))
    blk_spec = pl.BlockSpec((tm, tn), ij(lambda i, j, kk: (i, j)))
    ins, in_specs = [a, b], [a_spec, b_spec]
    if bias is not None:
        ins.append(bias), in_specs.append(row_spec)
    if res is not None:
        ins.append(res), in_specs.append(blk_spec)
    if kind == "fwd":
        ins += [ln[1], ln[2]]
        in_specs += [row_spec, row_spec]
    elif kind == "bwd":
        ins += [ln[1], ln[2], ln[3]]
        in_specs += [blk_spec, row_spec, row_spec]
    if out_groups:
        blk_out = jax.ShapeDtypeStruct((out_groups, m, n // out_groups), out_dtype)
        out_spec = _split_spec(3, n // out_groups, tm, tn, ij(lambda i, j, kk: (i, j)))
    else:
        blk_out, out_spec = jax.ShapeDtypeStruct((m, n), out_dtype), blk_spec
    row_out = jax.ShapeDtypeStruct((1, n), F32)
    if kind is None:
        out_shape, out_specs = [blk_out], [out_spec]
    elif kind == "fwd":
        out_shape, out_specs = [blk_out, blk_out], [blk_spec, blk_spec]
    else:
        out_shape, out_specs = [blk_out, row_out, row_out], [blk_spec, row_spec, row_spec]
    if copy_dtype is not None:
        out_shape.append(jax.ShapeDtypeStruct((m, n), copy_dtype))
        out_specs.append(blk_spec)
    if a_copy_dtype is not None:
        assert mode != "tn" and a.ndim == 2 and copy_dtype is None
        out_shape.append(jax.ShapeDtypeStruct((m, k), a_copy_dtype))
        out_specs.append(a_spec)
    n_in = len(ins)

    def body(*refs):
        in_refs, out_refs, acc_ref = refs[:n_in], refs[n_in:n_in + len(out_shape)], refs[-1]
        i, kk = pl.program_id(0 if i_outer else 1), pl.program_id(2)
        a_ref, b_ref = in_refs[:2]
        extra = list(in_refs[2:])
        if a_copy_dtype is not None:
            out_refs[-1][...] = a_ref[...].astype(a_copy_dtype)

        def epilogue(acc):
            rest = list(extra)
            if bias is not None:
                acc = acc + rest.pop(0)[...]
            if res is not None:
                acc = acc + res_scale * rest.pop(0)[...]
            if kind is None:
                out_refs[0][...] = acc.astype(out_dtype)
                return
            if kind == "fwd":
                out_refs[0][...] = acc
                y = _layer_norm(acc, rest[0][...], rest[1][...])
                out_refs[1][...] = y
                if copy_dtype is not None:
                    out_refs[-1][...] = y.astype(copy_dtype)
                return
            _, vjp = jax.vjp(_layer_norm, rest[0][...], rest[1][...], rest[2][...])
            dz, dg, db = vjp(acc)
            out_refs[0][...] = dz
            out_refs[1][...] += dg
            out_refs[2][...] += db
            if copy_dtype is not None:
                out_refs[-1][...] = dz.astype(copy_dtype)

        if kind == "bwd":
            @pl.when((i == 0) & (kk == 0))
            def _():
                out_refs[1][...] = jnp.zeros_like(out_refs[1])
                out_refs[2][...] = jnp.zeros_like(out_refs[2])

        def chunk(ref, c0, last):
            if ref.ndim == 3:
                g = ref.shape[2]
                return ref[c0 // g, :, pl.ds(c0 % g, kc)]
            return ref[:, pl.ds(c0, kc)] if last else ref[pl.ds(c0, kc), :]

        if mode == "tn" or kc == k:
            prod = _dg(a_ref[...], b_ref[...], ca, cb)
        else:
            prod = None
            for c0 in range(0, k, kc):
                part = _dg(chunk(a_ref, c0, True), chunk(b_ref, c0, mode == "nt"), ca, cb)
                prod = part if prod is None else prod + part
        if gk == 1:
            epilogue(prod)
            return

        @pl.when(kk == 0)
        def _():
            acc_ref[...] = prod

        @pl.when(kk > 0)
        def _():
            acc_ref[...] += prod

        @pl.when(kk == gk - 1)
        def _():
            epilogue(acc_ref[...])

    vmem = (2 * (_nbytes((tm, tk), a.dtype) + _nbytes((tk, tn), b.dtype))
            + (2 * len(ins) + 2 * len(out_shape) + 1) * _nbytes((tm, tn), F32))
    outs = _pcall(
        body, name=name, grid=(gi, gj, gk) if i_outer else (gj, gi, gk), in_specs=in_specs, out_specs=out_specs,
        out_shape=out_shape, scratch_shapes=[pltpu.VMEM((tm, tn) if gk > 1 else (SUBLANES, LANES), F32)],
        compiler_params=_params(("arbitrary", "arbitrary", "arbitrary"), vmem),
    )(*ins)
    return outs[0] if len(out_shape) == 1 else outs


def _attn_head(q, k, v):
    sc = mm_nt(q, k) * (CA_DH ** -0.5)
    e = jnp.exp(sc - jnp.max(sc, axis=-1, keepdims=True))
    return mm_nn(e / jnp.sum(e, axis=-1, keepdims=True), v)


def _attn_fwd(q, kv):
    seq, n_mem = q.shape[0], kv.shape[0]
    tq = _tile(seq, (512, 256, 128))

    def body(q_ref, kv_ref, o_ref):
        for h in range(HEADS):
            hd = pl.ds(h * CA_DH, CA_DH)
            o = _attn_head(q_ref[:, hd], kv_ref[:, hd], kv_ref[:, pl.ds(D_MODEL + h * CA_DH, CA_DH)])
            o_ref[:, hd] = o.astype(BF16)

    return _pcall(
        body, name="attn_fwd", grid=(seq // tq,),
        in_specs=[pl.BlockSpec((tq, D_MODEL), lambda i: (i, 0)), pl.BlockSpec((n_mem, 2 * D_MODEL), lambda i: (0, 0))],
        out_specs=pl.BlockSpec((tq, D_MODEL), lambda i: (i, 0)), out_shape=jax.ShapeDtypeStruct((seq, D_MODEL), BF16),
        compiler_params=_params(("arbitrary",), 4 * _nbytes((tq, D_MODEL), F32) + 2 * _nbytes((n_mem, 2 * D_MODEL), F32)),
    )(q, kv)


def _attn_bwd(q, kv, do):
    seq, n_mem = q.shape[0], kv.shape[0]
    tq = _tile(seq, (512, 256, 128))

    def body(q_ref, kv_ref, do_ref, dq_ref, dkv_ref):
        @pl.when(pl.program_id(0) == 0)
        def _():
            dkv_ref[...] = jnp.zeros_like(dkv_ref)

        for h in range(HEADS):
            hd = pl.ds(h * CA_DH, CA_DH)
            vd = pl.ds(D_MODEL + h * CA_DH, CA_DH)
            _, vjp = jax.vjp(_attn_head, q_ref[:, hd], kv_ref[:, hd], kv_ref[:, vd])
            dq, dk, dv = vjp(do_ref[:, hd].astype(F32))
            dq_ref[:, hd] = dq.astype(BF16)
            dkv_ref[:, hd] += dk
            dkv_ref[:, vd] += dv

    return _pcall(
        body, name="attn_bwd", grid=(seq // tq,),
        in_specs=[pl.BlockSpec((tq, D_MODEL), lambda i: (i, 0)), pl.BlockSpec((n_mem, 2 * D_MODEL), lambda i: (0, 0)),
                  pl.BlockSpec((tq, D_MODEL), lambda i: (i, 0))],
        out_specs=[pl.BlockSpec((tq, D_MODEL), lambda i: (i, 0)), pl.BlockSpec((n_mem, 2 * D_MODEL), lambda i: (0, 0))],
        out_shape=[jax.ShapeDtypeStruct((seq, D_MODEL), BF16), jax.ShapeDtypeStruct((n_mem, 2 * D_MODEL), F32)],
        compiler_params=_params(("arbitrary",), 6 * _nbytes((tq, D_MODEL), F32) + 4 * _nbytes((n_mem, 2 * D_MODEL), F32)),
    )(q, kv, do)


def _ffn_mid(hg, xg, hv, xv, wg0, wg1, wg2, bg, wv0, wv1, wv2, bv):
    return jax.nn.gelu(causal_conv(hg, xg, (wg0, wg1, wg2), bg)) * causal_conv(hv, xv, (wv0, wv1, wv2), bv)


FFN_TB = 256
FFN_W = D_FF // 2
FFN_J = D_FF // FFN_W
MXU_COLS = 256
FFN_PIECES = tuple((off, min(MXU_COLS, FFN_W - off)) for off in range(0, FFN_W, MXU_COLS))


def _ffn_common_specs(seq, row):
    tb = min(FFN_TB, seq)
    full = pl.BlockSpec((tb, D_MODEL), lambda t, j: (row(t), 0))
    vec = pl.BlockSpec((1, D_MODEL), lambda t, j: (0, 0))
    halves = []
    for off in (0, FFN_J):
        halves.append(dict(
            w_up=pl.BlockSpec((None, D_MODEL, FFN_W), lambda t, j, off=off: (j + off, 0, 0)),
            taps=pl.BlockSpec((FFN_CONV, FFN_W), lambda t, j, off=off: (0, j + off)),
            bias=pl.BlockSpec((1, FFN_W), lambda t, j, off=off: (0, j + off))))
    w_down = pl.BlockSpec((FFN_W, D_MODEL), lambda t, j: (j, 0))
    u_blk = pl.BlockSpec((2, tb, FFN_W), lambda t, j: (0, row(t), j))
    return tb, full, vec, halves, w_down, u_blk


def _ffn_vmem(tb):
    return (_nbytes((2, tb, FFN_W), F32) + _nbytes((2, tb, FFN_W), BF16) + 3 * _nbytes((D_MODEL, FFN_W), BF16)
            + 10 * _nbytes((tb, D_MODEL), F32))


def _conv_params(taps_ref, bias_ref, cols):
    return taps_ref[0:1, cols], taps_ref[1:2, cols], taps_ref[2:3, cols], bias_ref[:, cols]


def _ffn_fwd(x2b, x2, w_up, conv_w, conv_b, w_down, ln_g, ln_b, target):
    seq = x2.shape[0]
    tb, full, vec, halves, wd_spec, u_blk = _ffn_common_specs(seq, lambda t: t)
    nt = seq // tb

    def body(xb_ref, wg_ref, wv_ref, tg_ref, tv_ref, bg_ref, bv_ref, wd_ref, x_ref, g_ref, b_ref, tgt_ref,
             u_ref, h_ref, dz_ref, dg_ref, db_ref, loss_ref, dzb_ref, acc, carry):
        t, j = pl.program_id(0), pl.program_id(1)
        xb = xb_ref[...]
        pieces = [pl.ds(off, width) for off, width in FFN_PIECES]
        ug = [_dg(xb, wg_ref[:, cols], 1, 0) for cols in pieces]
        uv = [_dg(xb, wv_ref[:, cols], 1, 0) for cols in pieces]
        hs = []
        for cols, g, v in zip(pieces, ug, uv):
            u_ref[0, :, cols] = g
            u_ref[1, :, cols] = v
            halo_g = jnp.where(t == 0, 0.0, carry[j, 0, :, cols])
            halo_v = jnp.where(t == 0, 0.0, carry[j, 1, :, cols])
            h = _ffn_mid(halo_g, g, halo_v, v, *_conv_params(tg_ref, bg_ref, cols),
                         *_conv_params(tv_ref, bv_ref, cols)).astype(BF16)
            carry[j, 0, :, cols] = g[tb - SUBLANES:, :]
            carry[j, 1, :, cols] = v[tb - SUBLANES:, :]
            h_ref[:, cols] = h
            hs.append(h)
        part = None
        for cols, h in zip(pieces, hs):
            p = _dg(h, wd_ref[cols, :], 1, 0)
            part = p if part is None else part + p

        @pl.when(j == 0)
        def _():
            acc[...] = part

        @pl.when(j > 0)
        def _():
            acc[...] += part

        @pl.when(j == FFN_J - 1)
        def _():
            y, vjp = jax.vjp(_layer_norm, acc[...] + ALPHA * x_ref[...], g_ref[...], b_ref[...])
            err = y - tgt_ref[...]
            part_loss = 0.5 * jnp.sum(jnp.sum(err * err, axis=1, keepdims=True), axis=0, keepdims=True) / D_MODEL
            dz, dg, db = vjp(err / D_MODEL)

            @pl.when(t == 0)
            def _():
                for r in (dg_ref, db_ref, loss_ref):
                    r[...] = jnp.zeros_like(r)

            dz_ref[...] = dz
            dzb_ref[...] = dz.astype(BF16)
            dg_ref[...] += dg
            db_ref[...] += db
            loss_ref[...] += jnp.broadcast_to(part_loss, (1, LANES))

    h0, h1 = halves
    row = jax.ShapeDtypeStruct((1, D_MODEL), F32)
    return _pcall(
        body, name="ffn_fwd", grid=(nt, FFN_J),
        in_specs=[full, h0["w_up"], h1["w_up"], h0["taps"], h1["taps"], h0["bias"], h1["bias"], wd_spec, full, vec, vec,
                  full],
        out_specs=[u_blk, pl.BlockSpec((tb, FFN_W), lambda t, j: (t, j)), full, vec, vec,
                   pl.BlockSpec((1, LANES), lambda t, j: (0, 0)), full],
        out_shape=[jax.ShapeDtypeStruct((2, seq, D_FF), F32), jax.ShapeDtypeStruct((seq, D_FF), BF16),
                   jax.ShapeDtypeStruct((seq, D_MODEL), F32), row, row, jax.ShapeDtypeStruct((1, LANES), F32),
                   jax.ShapeDtypeStruct((seq, D_MODEL), BF16)],
        scratch_shapes=[pltpu.VMEM((tb, D_MODEL), F32), pltpu.VMEM((FFN_J, 2, SUBLANES, FFN_W), F32)],
        compiler_params=_params(("arbitrary", "arbitrary"), _ffn_vmem(tb)),
    )(x2b, w_up, w_up, conv_w, conv_w, conv_b, conv_b, w_down, x2, ln_g, ln_b, target)


def _ffn_bwd(u, conv_w, conv_b, dz3b, dz3, w_down, w_up, z2, ln_g, ln_b):
    seq = dz3.shape[0]
    tb = min(FFN_TB, seq)
    nt = seq // tb
    row8 = tb // SUBLANES
    tb, full, vec, halves, wd_spec, u_blk = _ffn_common_specs(seq, lambda t: nt - 1 - t)
    halo = pl.BlockSpec((2, SUBLANES, FFN_W), lambda t, j: (0, jnp.maximum((nt - 1 - t) * row8 - 1, 0), j))

    def body(u_ref, halo_ref, tg_ref, tv_ref, bg_ref, bv_ref, dzb_ref, wd_ref, wg_ref, wv_ref, dz3_ref, z_ref, g_ref,
             b_ref, du_ref, dw_ref, dbias_ref, dz_ref, dg_ref, db_ref, dz2b_ref, acc, carry):
        t, j = pl.program_id(0), pl.program_id(1)

        @pl.when((t == 0) & (j == 0))
        def _():
            for r in (dw_ref, dbias_ref, dg_ref, db_ref):
                r[...] = jnp.zeros_like(r)

        pieces = [pl.ds(off, width) for off, width in FFN_PIECES]
        dzb = dzb_ref[...]
        dhs = [_dg(dzb, wd_ref[cols, :], 1, 1) for cols in pieces]
        first = t == nt - 1
        dus = []
        for cols, dh in zip(pieces, dhs):
            args = (jnp.where(first, 0.0, halo_ref[0, :, cols]), u_ref[0, :, cols],
                    jnp.where(first, 0.0, halo_ref[1, :, cols]), u_ref[1, :, cols],
                    *_conv_params(tg_ref, bg_ref, cols), *_conv_params(tv_ref, bv_ref, cols))
            _, vjp = jax.vjp(_ffn_mid, *args)
            dhg, dxg, dhv, dxv, g0, g1, g2, gb, v0, v1, v2, vb = vjp(dh)
            zeros = jnp.zeros((tb - SUBLANES, dh.shape[1]), F32)
            dug = (dxg + jnp.concatenate([zeros, jnp.where(t == 0, 0.0, carry[j, 0, :, cols])], axis=0)).astype(BF16)
            duv = (dxv + jnp.concatenate([zeros, jnp.where(t == 0, 0.0, carry[j, 1, :, cols])], axis=0)).astype(BF16)
            carry[j, 0, :, cols] = dhg
            carry[j, 1, :, cols] = dhv
            du_ref[0, :, cols] = dug
            du_ref[1, :, cols] = duv
            for half, parts in enumerate(((g0, g1, g2), (v0, v1, v2))):
                for d, p in enumerate(parts):
                    dw_ref[j, half, d:d + 1, cols] += p
            dbias_ref[j, 0, :, cols] += gb
            dbias_ref[j, 1, :, cols] += vb
            dus.append((dug, duv))
        part = None
        for cols, (dug, duv) in zip(pieces, dus):
            p = _dg(dug, wg_ref[:, cols], 1, 1) + _dg(duv, wv_ref[:, cols], 1, 1)
            part = p if part is None else part + p

        @pl.when(j == 0)
        def _():
            acc[...] = part

        @pl.when(j > 0)
        def _():
            acc[...] += part

        @pl.when(j == FFN_J - 1)
        def _():
            _, ln_vjp = jax.vjp(_layer_norm, z_ref[...], g_ref[...], b_ref[...])
            dz, dg, db = ln_vjp(acc[...] + ALPHA * dz3_ref[...])
            dz_ref[...] = dz
            dz2b_ref[...] = dz.astype(BF16)
            dg_ref[...] += dg
            db_ref[...] += db

    h0, h1 = halves
    row = jax.ShapeDtypeStruct((1, D_MODEL), F32)
    whole = lambda *shape: pl.BlockSpec(shape, lambda t, j: (0,) * len(shape))
    return _pcall(
        body, name="ffn_bwd", grid=(nt, FFN_J),
        in_specs=[u_blk, halo, h0["taps"], h1["taps"], h0["bias"], h1["bias"], full, wd_spec, h0["w_up"], h1["w_up"],
                  full, full, vec, vec],
        out_specs=[u_blk, whole(FFN_J, 2, FFN_CONV, FFN_W), whole(FFN_J, 2, 1, FFN_W), full, vec, vec, full],
        out_shape=[jax.ShapeDtypeStruct((2, seq, D_FF), BF16), jax.ShapeDtypeStruct((FFN_J, 2, FFN_CONV, FFN_W), F32),
                   jax.ShapeDtypeStruct((FFN_J, 2, 1, FFN_W), F32), jax.ShapeDtypeStruct((seq, D_MODEL), F32), row, row,
                   jax.ShapeDtypeStruct((seq, D_MODEL), BF16)],
        scratch_shapes=[pltpu.VMEM((tb, D_MODEL), F32), pltpu.VMEM((FFN_J, 2, SUBLANES, FFN_W), F32)],
        compiler_params=_params(("arbitrary", "arbitrary"), _ffn_vmem(tb)),
    )(u, u, conv_w, conv_w, conv_b, conv_b, dz3b, w_down, w_up, w_up, dz3, z2, ln_g, ln_b)


def _adamw_math(w, g, m, v):
    m_new = ADAM_B1 * m + (1.0 - ADAM_B1) * g
    v_new = ADAM_B2 * v + (1.0 - ADAM_B2) * jnp.square(g)
    m_hat = m_new / (1.0 - ADAM_B1 ** ADAM_STEP)
    v_hat = v_new / (1.0 - ADAM_B2 ** ADAM_STEP)
    return -ADAM_LR * (m_hat / (jnp.sqrt(v_hat) + ADAM_EPS) + ADAM_WD * w), m_new, v_new


def _adamw(name, w, g, m, v):
    rows, cols = w.shape
    tr = _tile(rows, (256, 176, 128, 64, 40, 32, 16, 8))

    def body(w_ref, g_ref, m_ref, v_ref, d_ref, nm_ref, nv_ref):
        d_ref[...], nm_ref[...], nv_ref[...] = _adamw_math(w_ref[...], g_ref[...], m_ref[...], v_ref[...])

    spec = pl.BlockSpec((tr, cols), lambda i: (i, 0))
    sh = jax.ShapeDtypeStruct((rows, cols), F32)
    return _pcall(
        body, name=name, grid=(rows // tr,), in_specs=[spec] * 4, out_specs=[spec] * 3, out_shape=[sh] * 3,
        compiler_params=_params(("arbitrary",), 14 * _nbytes((tr, -(-cols // LANES) * LANES), F32)),
    )(w, g, m, v)


def _adamw_halves(name, core, w, mine, theirs, m, v):
    rows, cols = w.shape
    half_rows = mine.shape[0]
    tr = _tile(half_rows, (256, 176, 128))
    nbh = half_rows // tr
    assert 2 * half_rows == rows

    def body(c_ref, w_ref, a_ref, b_ref, m_ref, v_ref, g_ref, d_ref, nm_ref, nv_ref):
        g = jnp.where(pl.program_id(0) // nbh == c_ref[0], a_ref[...], b_ref[...])
        g_ref[...] = g
        d_ref[...], nm_ref[...], nv_ref[...] = _adamw_math(w_ref[...], g, m_ref[...], v_ref[...])

    spec = pl.BlockSpec((tr, cols), lambda i, c_ref: (i, 0))
    half = pl.BlockSpec((tr, cols), lambda i, c_ref: (i % nbh, 0))
    sh = jax.ShapeDtypeStruct((rows, cols), F32)
    grid_spec = pltpu.PrefetchScalarGridSpec(
        num_scalar_prefetch=1, grid=(rows // tr,), in_specs=[spec, half, half, spec, spec], out_specs=[spec] * 4)
    return _pcall(
        body, name=name, grid_spec=grid_spec, out_shape=[sh] * 4,
        compiler_params=_params(("arbitrary",), 18 * _nbytes((tr, -(-cols // LANES) * LANES), F32)),
    )(core, w, mine, theirs, m, v)


MESH = pl.DeviceIdType.MESH
ANY = pl.BlockSpec(memory_space=pl.ANY)
N_CHIPS = 4
BF16_ROWS = 16


def _me():
    return lax.axis_index("x"), lax.axis_index("y"), lax.axis_index("c")


def _other_chips(x, y):
    return [(1 - x, y), (x, 1 - y), (1 - x, 1 - y)]


def _remote(src, dst, ssem, rsem, dev):
    return pltpu.make_async_remote_copy(src_ref=src, dst_ref=dst, send_sem=ssem, recv_sem=rsem,
                                        device_id=dev, device_id_type=MESH)


def _half_rows(ref_rows, cc):
    half = ref_rows // 2
    return pl.ds(pl.multiple_of(cc * half, BF16_ROWS), half)


def _gather_weights(shards):
    n = len(shards)
    n_ici = n * (N_CHIPS - 1)

    def body(*refs):
        ins, outs, (ssem, rsem, lsem, lrsem) = refs[:n], refs[n:2 * n], refs[2 * n:]
        x, y, c = _me()
        k_me = 2 * x + y
        sib = (x, y, 1 - c)
        chips = _other_chips(x, y)
        started = []
        for i, (w_ref, o_ref) in enumerate(zip(ins, outs)):
            cp = _remote(w_ref, o_ref.at[k_me], lsem.at[i], lrsem.at[i], sib)
            cp.start()
            started.append(cp)
        for r, (px, py) in enumerate(chips):
            for i, (w_ref, o_ref) in enumerate(zip(ins, outs)):
                rows = _half_rows(w_ref.shape[0], c)
                s = r * n + i
                cp = _remote(w_ref.at[rows], o_ref.at[k_me, rows], ssem.at[s], rsem.at[s], (px, py, c))
                cp.start()
                started.append(cp)
        for r, (px, py) in enumerate(chips):
            for i, o_ref in enumerate(outs):
                blk = o_ref.at[2 * px + py, _half_rows(o_ref.shape[1], c)]
                s = r * n + i
                _remote(blk, blk, ssem.at[s], rsem.at[s], (px, py, c)).wait_recv()
                cp = _remote(blk, blk, ssem.at[n_ici + s], rsem.at[n_ici + s], sib)
                cp.start()
                started.append(cp)
        for r, (px, py) in enumerate(chips):
            for i, o_ref in enumerate(outs):
                blk = o_ref.at[2 * px + py, _half_rows(o_ref.shape[1], 1 - c)]
                s = n_ici + r * n + i
                _remote(blk, blk, ssem.at[s], rsem.at[s], sib).wait_recv()
        for cp in started[n:]:
            cp.wait_send()
        for cp in started[:n]:
            cp.wait()

    return _pcall(
        body, name="gather_weights", in_specs=[ANY] * n, out_specs=[ANY] * n,
        out_shape=[jax.ShapeDtypeStruct((N_CHIPS,) + s.shape, s.dtype) for s in shards],
        scratch_shapes=[pltpu.SemaphoreType.DMA((2 * n_ici,)), pltpu.SemaphoreType.DMA((2 * n_ici,)),
                        pltpu.SemaphoreType.DMA((n,)), pltpu.SemaphoreType.DMA((n,))],
    )(*shards)


def _swap_halves(name, grads):
    n = len(grads)

    def body(*refs):
        ins, outs, (ssem, rsem) = refs[:n], refs[n:2 * n], refs[2 * n:]
        x, y, c = _me()
        copies = []
        for i, (g_ref, o_ref) in enumerate(zip(ins, outs)):
            for k in range(N_CHIPS):
                s = i * N_CHIPS + k
                cp = _remote(g_ref.at[k, _half_rows(g_ref.shape[1], 1 - c)], o_ref.at[k], ssem.at[s], rsem.at[s],
                             (x, y, 1 - c))
                cp.start()
                copies.append(cp)
        for cp in copies:
            cp.wait()

    return _pcall(
        body, name=name, in_specs=[ANY] * n, out_specs=[ANY] * n,
        out_shape=[jax.ShapeDtypeStruct((N_CHIPS, g.shape[1] // 2, g.shape[2]), g.dtype) for g in grads],
        scratch_shapes=[pltpu.SemaphoreType.DMA((n * N_CHIPS,)), pltpu.SemaphoreType.DMA((n * N_CHIPS,))],
    )(*grads)


SEM = pl.BlockSpec(memory_space=pltpu.SEMAPHORE)
IN_HBM = pl.BlockSpec(memory_space=pltpu.HBM)
SPLIT_PARAMS = dict(compiler_params=pltpu.CompilerParams(has_side_effects=pltpu.SideEffectType.DATAFLOW_SIDE_EFFECTING))


def _split_start(name, sources, landings, n_copies, plan):
    ns, nl = len(sources), len(landings)

    def body(*refs):
        ins, lands, (ssem, rsem), token = refs[:ns], refs[ns:ns + nl], refs[ns + nl:ns + nl + 2], refs[-1]
        for s, (src, dst, _, dev) in enumerate(plan(ins, lands)):
            _remote(src, dst, ssem.at[s], rsem.at[s], dev).start()
        token[...] = jnp.zeros_like(token)

    arrays = list(sources) + list(landings)
    outs = _call(
        body, name=name, in_specs=[IN_HBM] * (ns + nl),
        out_specs=[SEM, SEM] + [IN_HBM] * (ns + nl) + [pl.BlockSpec(memory_space=pltpu.VMEM)],
        out_shape=[pltpu.SemaphoreType.DMA((n_copies,)), pltpu.SemaphoreType.DMA((n_copies,))]
        + [pltpu.HBM(a.shape, a.dtype) for a in arrays] + [jax.ShapeDtypeStruct((SUBLANES, LANES), F32)],
        input_output_aliases={i: 2 + i for i in range(ns + nl)}, **SPLIT_PARAMS,
    )(*[pltpu.with_memory_space_constraint(a, pltpu.HBM) for a in arrays])
    return (outs[:-1], ns), outs[-1]


def _split_wait(name, handle, after, plan):
    (ssem, rsem, *thru), ns = handle
    nl = len(thru) - ns

    def body(*refs):
        ins, lands, (ssem_ref, rsem_ref) = refs[:ns], refs[ns:ns + nl], refs[ns + nl:ns + nl + 2]
        for s, (src, _, dst, dev) in enumerate(plan(ins, lands)):
            cp = _remote(src, dst, ssem_ref.at[s], rsem_ref.at[s], dev)
            cp.wait_send()
            cp.wait_recv()

    outs = _call(
        body, name=name, in_specs=[IN_HBM] * (ns + nl) + [SEM, SEM, ANY], out_specs=[IN_HBM] * (ns + nl),
        out_shape=[pltpu.HBM(t.shape, t.dtype) for t in thru],
        input_output_aliases={i: i for i in range(ns + nl)}, **SPLIT_PARAMS,
    )(*thru, ssem, rsem, after)
    return outs[:ns], outs[ns:]


def _swap_plan(ins, lands):
    x, y, c = _me()
    return [(g_ref.at[k, _half_rows(g_ref.shape[1], 1 - c)], l_ref.at[k], l_ref.at[k], (x, y, 1 - c))
            for g_ref, l_ref in zip(ins, lands) for k in range(N_CHIPS)]


def _swap_start(name, grads):
    lands = [lax.empty((N_CHIPS, g.shape[1] // 2, g.shape[2]), g.dtype) for g in grads]
    return _split_start(name, grads, lands, len(grads) * N_CHIPS, _swap_plan)


def _swap_wait(name, handle, after):
    return _split_wait(name, handle, after, _swap_plan)


def _gather_plan(ins, lands):
    x, y, c = _me()
    k_me = 2 * x + y
    plan = [(w_ref, l_ref.at[k_me], l_ref.at[k_me], (x, y, 1 - c)) for w_ref, l_ref in zip(ins, lands)]
    for px, py in _other_chips(x, y):
        for w_ref, l_ref in zip(ins, lands):
            rows = _half_rows(w_ref.shape[0], c)
            plan.append((w_ref.at[rows], l_ref.at[k_me, rows], l_ref.at[2 * px + py, rows], (px, py, c)))
    return plan


def _gather_start(name, shards):
    lands = [lax.empty((N_CHIPS,) + s.shape, s.dtype) for s in shards]
    return _split_start(name, shards, lands, len(shards) * N_CHIPS, _gather_plan)


def _gather_wait(name, handle, after):
    return _split_wait(name, handle, after, _gather_plan)[1]


def _forward_halves(name, blocks):
    n = len(blocks)
    n_sem = n * (N_CHIPS - 1)

    def body(*refs):
        outs, (ssem, rsem) = refs[n:2 * n], refs[2 * n:]
        x, y, c = _me()
        sib = (x, y, 1 - c)
        chips = _other_chips(x, y)
        sends = []
        for r, (px, py) in enumerate(chips):
            for i, o_ref in enumerate(outs):
                blk = o_ref.at[2 * px + py, _half_rows(o_ref.shape[1], c)]
                cp = _remote(blk, blk, ssem.at[r * n + i], rsem.at[r * n + i], sib)
                cp.start()
                sends.append(cp)
        for r, (px, py) in enumerate(chips):
            for i, o_ref in enumerate(outs):
                blk = o_ref.at[2 * px + py, _half_rows(o_ref.shape[1], 1 - c)]
                _remote(blk, blk, ssem.at[r * n + i], rsem.at[r * n + i], sib).wait_recv()
        for cp in sends:
            cp.wait_send()

    return _pcall(
        body, name=name, in_specs=[ANY] * n, out_specs=[ANY] * n,
        out_shape=[jax.ShapeDtypeStruct(b.shape, b.dtype) for b in blocks],
        input_output_aliases={i: i for i in range(n)},
        scratch_shapes=[pltpu.SemaphoreType.DMA((n_sem,)), pltpu.SemaphoreType.DMA((n_sem,))],
    )(*blocks)


def _scatter_plan(ins, lands):
    x, y, c = _me()
    k_me = 2 * x + y
    return [(p_ref.at[2 * px + py], l_ref.at[k_me], l_ref.at[2 * px + py], (px, py, c))
            for px, py in _other_chips(x, y) for p_ref, l_ref in zip(ins, lands)]


def _scatter_start(name, parts):
    lands = [lax.empty(p.shape, p.dtype) for p in parts]
    return _split_start(name, parts, lands, len(parts) * (N_CHIPS - 1), _scatter_plan)


def _scatter_wait(name, handle, after):
    return _split_wait(name, handle, after, _scatter_plan)[1]


def _share_halves(halves):
    n = len(halves)

    def body(*refs):
        ins, outs, (ssem, rsem) = refs[:n], refs[n:2 * n], refs[2 * n:]
        x, y, c = _me()
        copies = [_remote(r_ref, o_ref, ssem.at[i], rsem.at[i], (x, y, 1 - c))
                  for i, (r_ref, o_ref) in enumerate(zip(ins, outs))]
        for cp in copies:
            cp.start()
        for cp in copies:
            cp.wait()

    return _pcall(
        body, name="share_halves", in_specs=[ANY] * n, out_specs=[ANY] * n,
        out_shape=[jax.ShapeDtypeStruct(h.shape, h.dtype) for h in halves],
        scratch_shapes=[pltpu.SemaphoreType.DMA((n,)), pltpu.SemaphoreType.DMA((n,))],
    )(*halves)


def _reduce_small(v):
    rows = v.shape[0]
    half = rows // 2
    assert half % SUBLANES == 0

    def body(v_ref, out_ref, pair_buf, mine, chip_buf, ssem, rsem):
        x, y, c = _me()
        k_me = 2 * x + y
        sib = (x, y, 1 - c)

        def rows_of(cc):
            return pl.ds(pl.multiple_of(cc * half, SUBLANES), half)

        swap = _remote(v_ref.at[rows_of(1 - c)], pair_buf, ssem.at[0], rsem.at[0], sib)
        swap.start()
        swap.wait()
        mine[...] = v_ref[rows_of(c), :] + pair_buf[...]
        chip_buf[k_me] = mine[...]
        sends = [_remote(mine, chip_buf.at[k_me], ssem.at[1 + r], rsem.at[1 + r], (px, py, c))
                 for r, (px, py) in enumerate(_other_chips(x, y))]
        for cp in sends:
            cp.start()
        for r, (px, py) in enumerate(_other_chips(x, y)):
            blk = chip_buf.at[2 * px + py]
            _remote(blk, blk, ssem.at[1 + r], rsem.at[1 + r], (px, py, c)).wait_recv()
        total = chip_buf[0]
        for k in range(1, N_CHIPS):
            total = total + chip_buf[k]
        out_ref[rows_of(c), :] = total
        for cp in sends:
            cp.wait_send()
        share = _remote(out_ref.at[rows_of(c)], out_ref.at[rows_of(c)], ssem.at[N_CHIPS], rsem.at[N_CHIPS], sib)
        share.start()
        got = out_ref.at[rows_of(1 - c)]
        _remote(got, got, ssem.at[N_CHIPS], rsem.at[N_CHIPS], sib).wait_recv()
        share.wait_send()

    vm = pl.BlockSpec(memory_space=pltpu.VMEM)
    return _pcall(
        body, pin=False, name="reduce_small", in_specs=[vm], out_specs=vm,
        out_shape=jax.ShapeDtypeStruct((rows, LANES), F32),
        scratch_shapes=[pltpu.VMEM((half, LANES), F32), pltpu.VMEM((half, LANES), F32),
                        pltpu.VMEM((N_CHIPS, half, LANES), F32), pltpu.SemaphoreType.DMA((N_CHIPS + 1,)),
                        pltpu.SemaphoreType.DMA((N_CHIPS + 1,))],
        compiler_params=pltpu.CompilerParams(vmem_limit_bytes=32 * 1024 * 1024),
    )(v)


def _add_pair(name, core, chip, g, theirs):
    _, half, cols = theirs.shape
    tr = _tile(half, (256, 176, 128))
    nb = half // tr

    def body(c_ref, k_ref, g_ref, t_ref, o32_ref, o16_ref):
        s = g_ref[...] + t_ref[...]
        o16_ref[...] = s.astype(BF16)

        @pl.when(pl.program_id(1) == k_ref[0])
        def _():
            o32_ref[...] = s

    spec = pl.BlockSpec((None, tr, cols), lambda i, k, c_ref, k_ref: (k, i, 0))
    grid_spec = pltpu.PrefetchScalarGridSpec(
        num_scalar_prefetch=2, grid=(nb, N_CHIPS),
        in_specs=[pl.BlockSpec((None, tr, cols), lambda i, k, c_ref, k_ref: (k, c_ref[0] * nb + i, 0)), spec],
        out_specs=[pl.BlockSpec((tr, cols), lambda i, k, c_ref, k_ref: (i, 0)), spec])
    return _pcall(
        body, name=name, grid_spec=grid_spec,
        out_shape=[jax.ShapeDtypeStruct((half, cols), F32), jax.ShapeDtypeStruct(theirs.shape, BF16)],
        compiler_params=_params(("arbitrary", "arbitrary"), 8 * _nbytes((tr, cols + LANES), F32)),
    )(core, chip, g, theirs)


def _add_chips(name, chip, p32, recv):
    half, cols = p32.shape
    tr = _tile(half, (256, 176, 128))

    def body(k_ref, p_ref, r0_ref, r1_ref, r2_ref, o_ref):
        o_ref[...] = ((p_ref[...] + r0_ref[...].astype(F32)) + r1_ref[...].astype(F32)) + r2_ref[...].astype(F32)

    def other(r):
        return pl.BlockSpec((None, tr, cols), lambda i, k_ref: (r + (k_ref[0] <= r).astype(jnp.int32), i, 0))
    grid_spec = pltpu.PrefetchScalarGridSpec(
        num_scalar_prefetch=1, grid=(half // tr,),
        in_specs=[pl.BlockSpec((tr, cols), lambda i, k_ref: (i, 0)), other(0), other(1), other(2)],
        out_specs=pl.BlockSpec((tr, cols), lambda i, k_ref: (i, 0)))
    return _pcall(
        body, name=name, grid_spec=grid_spec, out_shape=jax.ShapeDtypeStruct((half, cols), F32),
        compiler_params=_params(("arbitrary",), 10 * _nbytes((tr, cols + LANES), F32)),
    )(chip, p32, recv, recv, recv)


def kernel(x, mem, w_in, b_in, hg_lb_logits, hg_norm_w, ml_conv_w, ml_conv_b, ml_norm_w, w_out, ln1_g, ln1_b, ca_wq, ca_wkv, ca_wo, ln2_g, ln2_b, ffn_w_up, ffn_conv_w, ffn_conv_b, ffn_w_down, ln3_g, ln3_b, loss_target, m_w_in, m_b_in, m_hg_lb_logits, m_hg_norm_w, m_ml_conv_w, m_ml_conv_b, m_ml_norm_w, m_w_out, m_ln1_g, m_ln1_b, m_ca_wq, m_ca_wkv, m_ca_wo, m_ln2_g, m_ln2_b, m_ffn_w_up, m_ffn_conv_w, m_ffn_conv_b, m_ffn_w_down, m_ln3_g, m_ln3_b, v_w_in, v_b_in, v_hg_lb_logits, v_hg_norm_w, v_ml_conv_w, v_ml_conv_b, v_ml_norm_w, v_w_out, v_ln1_g, v_ln1_b, v_ca_wq, v_ca_wkv, v_ca_wo, v_ln2_g, v_ln2_b, v_ffn_w_up, v_ffn_conv_w, v_ffn_conv_b, v_ffn_w_down, v_ln3_g, v_ln3_b):
    return _train_step(dict(locals()))


WEIGHTS = ("w_in", "b_in", "hg_lb_logits", "hg_norm_w", "ml_conv_w", "ml_conv_b", "ml_norm_w", "w_out", "ln1_g",
           "ln1_b", "ca_wq", "ca_wkv", "ca_wo", "ln2_g", "ln2_b", "ffn_w_up", "ffn_conv_w", "ffn_conv_b",
           "ffn_w_down", "ln3_g", "ln3_b")
MATRICES = ("w_in", "w_out", "ca_wq", "ca_wkv", "ca_wo", "ffn_w_up", "ffn_w_down")
COL_SHARDED = ("w_in", "ca_wkv", "ffn_w_up", "ml_conv_w", "ffn_conv_w")
SMALL = tuple(n for n in WEIGHTS if n not in MATRICES)
PART_ROWS = 16


def _part_rows(shape):
    n = 1
    for s in shape:
        n *= s
    return -(-n // (LANES * PART_ROWS)) * PART_ROWS


def _pack(arrs, dtype):
    parts = []
    for a in arrs:
        flat = a.reshape(-1).astype(dtype)
        flat = jnp.pad(flat, (0, _part_rows(a.shape) * LANES - flat.shape[0]))
        parts.append(flat.reshape(-1, LANES))
    return jnp.concatenate(parts, axis=0)


def _unpack(buf, shapes):
    lead = buf.shape[:-2]
    outs, r = [], 0
    for sh in shapes:
        n = 1
        for s in sh:
            n *= s
        nr = _part_rows(sh)
        flat = buf[..., r:r + nr, :].reshape(lead + (nr * LANES,))
        outs.append(flat[..., :n].reshape(lead + tuple(sh)))
        r += nr
    return outs


def _cat_cols(s):
    return jnp.moveaxis(s, 0, 1).reshape(s.shape[1], -1)


def _stack_rows(s):
    return s.reshape(-1, s.shape[-1])


def _train_step(a):
    xs, mems, tgt = a["x"][0], a["mem"][0], a["loss_target"][0]
    core = lax.axis_index("c").astype(jnp.int32).reshape(1)
    chip = (2 * lax.axis_index("x") + lax.axis_index("y")).astype(jnp.int32).reshape(1)
    k_me = chip[0]
    shard = {n: a[n][0] for n in MATRICES}

    later = [n for n in MATRICES if n != "w_in"]
    w_in, taps = _gather_weights([shard["w_in"].astype(BF16), _pack([a["ml_conv_w"][0], a["ffn_conv_w"][0]], F32)])
    w = {"w_in": jnp.pad(_cat_cols(w_in), ((0, 0), (0, D_IN_PAD - D_IN)))}
    gathering, token = _gather_start("gather_start", [shard[n].astype(BF16) for n in later])
    ml_cw, ffn_cw = [_cat_cols(s) for s in _unpack(taps, [a["ml_conv_w"].shape[1:], a["ffn_conv_w"].shape[1:]])]
    b_in_p = jnp.pad(a["b_in"], ((0, 0), (0, D_IN_PAD - D_IN))) + token[0:1, 0:1]
    mixer_w = (a["hg_lb_logits"], a["hg_norm_w"], ml_cw, a["ml_conv_b"], a["ml_norm_w"])
    up_cols = a["ffn_w_up"].shape[-1]

    proj, xb = _mm("proj", "nn", xs, w["w_in"], bias=b_in_p, a_copy_dtype=BF16, tm=256, tn=D_IN_PAD)
    y, hst, cst, nst, mst = _mixer_fwd(proj, *mixer_w)
    w.update(zip(later, _forward_halves("forward_halves", _gather_wait("gather_wait", gathering, y))))
    for n in ("w_out", "ca_wq", "ca_wo", "ffn_w_down"):
        w[n] = _stack_rows(w[n])
    z1, x1, x1b = _mm("mix_out", "nn", y, w["w_out"], res=xs, res_scale=ALPHA, ln=("fwd", a["ln1_g"], a["ln1_b"]),
                      copy_dtype=BF16)
    q = _mm("ca_q", "nn", x1b, w["ca_wq"], out_dtype=BF16, tn=D_MODEL)
    kv = _mm("ca_kv", "nn", mems, w["ca_wkv"])
    o = _attn_fwd(q, kv)
    z2, x2, x2b = _mm("ca_out", "nn", o, w["ca_wo"], res=x1, res_scale=ALPHA, ln=("fwd", a["ln2_g"], a["ln2_b"]),
                      copy_dtype=BF16)
    w_up = w["ffn_w_up"]
    assert w_up.shape == (2 * FFN_J, D_MODEL, FFN_W)
    u, hmid, dz3, g_ln3g, g_ln3b, loss_part, dz3b = _ffn_fwd(
        x2b, x2, w_up, ffn_cw, a["ffn_conv_b"], w["ffn_w_down"], a["ln3_g"], a["ln3_b"], tgt)

    grads = {"ln3_g": g_ln3g, "ln3_b": g_ln3b}
    grads["ffn_w_down"] = _mm("g_w_down", "tn", hmid, dz3b, tm=D_FF // 2, tn=D_MODEL)
    du, g_cw, g_cb, dz2, grads["ln2_g"], grads["ln2_b"], dz2b = _ffn_bwd(
        u, ffn_cw, a["ffn_conv_b"], dz3b, dz3, w["ffn_w_down"], w_up, z2, a["ln2_g"], a["ln2_b"])
    grads["ffn_conv_w"] = jnp.transpose(g_cw, (2, 1, 0, 3)).reshape(FFN_CONV, 2 * D_FF)
    grads["ffn_conv_b"] = jnp.transpose(g_cb, (2, 1, 0, 3)).reshape(1, 2 * D_FF)
    grads["ffn_w_up"] = _mm("g_w_up", "tn", x2b, du, out_groups=N_CHIPS, tm=D_MODEL, tn=up_cols)
    grads["ffn_w_down"] = grads["ffn_w_down"].reshape((N_CHIPS,) + shard["ffn_w_down"].shape)
    pending = {}

    def reduce_start(tag, names, swapped=None):
        group = [grads[n] for n in names]
        group, theirs = swapped or (group, _swap_halves("swap_halves_" + tag, group))
        sums = [_add_pair("add_pair_" + n, core, chip, g, t) for n, g, t in zip(names, group, theirs)]
        handle, token = _scatter_start("scatter_start_" + tag, [s16 for _, s16 in sums])
        pending[tag] = (names, [s32 for s32, _ in sums], handle)
        return token[0:1, 0:1]

    ffn = ("ffn_w_up", "ffn_w_down")
    swapping, token = _swap_start("swap_start_ffn", [grads[n] for n in ffn])
    do = _mm("d_o", "nt", dz2b, w["ca_wo"], bias=jnp.zeros((1, D_MODEL), F32) + token[0:1, 0:1], out_dtype=BF16,
             tn=D_MODEL)
    grads["ca_wo"] = _mm("g_wo", "tn", o, dz2b, tm=D_MODEL, tn=D_MODEL)
    zero = reduce_start("ffn", ffn, _swap_wait("swap_wait_ffn", swapping, grads["ca_wo"]))
    dq, dkv = _attn_bwd(q, kv + zero, do)
    grads["ca_wq"] = _mm("g_wq", "tn", x1b, dq, tm=D_MODEL, tn=D_MODEL)
    grads["ca_wkv"] = _mm("g_wkv", "tn", mems, dkv, out_groups=N_CHIPS, tm=D_MODEL)
    dz1, grads["ln1_g"], grads["ln1_b"], dz1b = _mm("d_x1", "nt", dq, w["ca_wq"], res=dz2, res_scale=ALPHA,
                                                    ln=("bwd", z1, a["ln1_g"], a["ln1_b"]), copy_dtype=BF16)
    grads["w_out"] = _mm("g_w_out", "tn", y, dz1b, tm=D_MODEL, tn=D_MODEL)
    for n in ("w_out", "ca_wq", "ca_wo"):
        grads[n] = grads[n].reshape((N_CHIPS,) + shard[n].shape)
    attn = ("w_out", "ca_wq", "ca_wkv", "ca_wo")
    swapping, token = _swap_start("swap_start_attn", [grads[n] for n in attn])
    dy = _mm("d_y", "nt", dz1b, w["w_out"], bias=jnp.zeros((1, D_MODEL), F32) + token[0:1, 0:1], tn=D_MODEL)
    zero = reduce_start("attn", attn, _swap_wait("swap_wait_attn", swapping, dy))
    (dproj, g_b_in, grads["hg_lb_logits"], grads["hg_norm_w"], grads["ml_conv_w"], grads["ml_conv_b"],
     grads["ml_norm_w"]) = _mixer_bwd(proj, dy, hst, cst, nst, mst, mixer_w[0], mixer_w[1] + zero, *mixer_w[2:])
    g_in = _mm("g_w_in", "tn", xb, dproj, tm=D_MODEL, tn=up_cols)[:, :D_IN]
    grads["w_in"] = jnp.moveaxis(g_in.reshape(D_MODEL, N_CHIPS, -1), 1, 0)
    grads["b_in"] = g_b_in[:, :D_IN]
    zero = reduce_start("in", ("w_in",))
    dx = _mm("d_x", "nt", dproj, w["w_in"], bias=jnp.zeros((1, D_MODEL), F32) + zero, res=dz1, res_scale=ALPHA,
             tm=256, tn=D_MODEL)

    halves = {}
    for tag, (names, sums32, handle) in pending.items():
        for n, s32, r in zip(names, sums32, _scatter_wait("scatter_wait_" + tag, handle, dx)):
            halves[n] = _add_chips("add_chips_" + n, chip, s32, r)
    halves = [halves[n] for n in MATRICES]
    other_halves = _share_halves(halves)

    small_shapes = [grads[n].shape for n in SMALL] + [loss_part.shape]
    summed = _unpack(_reduce_small(_pack([grads[n] for n in SMALL] + [loss_part], F32)), small_shapes)
    loss = summed[-1][0, 0]
    for n, g in zip(SMALL, summed[:-1]):
        if n in COL_SHARDED:
            cols = a[n].shape[-1]
            g = lax.dynamic_slice_in_dim(g, k_me * cols, cols, axis=1)
        grads[n] = g

    delta, new_m, new_v = {}, {}, {}
    for n, mine, theirs in zip(MATRICES, halves, other_halves):
        grads[n], delta[n], new_m[n], new_v[n] = _adamw_halves(
            "adamw_" + n, core, shard[n], mine, theirs, a["m_" + n][0], a["v_" + n][0])
    small_w = [a[n][0] if a[n].ndim == 3 else a[n] for n in SMALL]
    small_m = [a["m_" + n][0] if a[n].ndim == 3 else a["m_" + n] for n in SMALL]
    small_v = [a["v_" + n][0] if a[n].ndim == 3 else a["v_" + n] for n in SMALL]
    shapes = [w.shape for w in small_w]
    packed = [_pack(l, F32) for l in (small_w, [grads[n] for n in SMALL], small_m, small_v)]
    for out, buf in zip((delta, new_m, new_v), _adamw("adamw_small", *packed)):
        for n, v in zip(SMALL, _unpack(buf, shapes)):
            out[n] = v

    def shaped(d):
        return [d[n].reshape(a[n].shape) for n in WEIGHTS]
    return (loss, dx[None], *shaped(grads), *shaped(delta), *shaped(new_m), *shaped(new_v))
```

```python
import functools

import jax
import jax.numpy as jnp
from jax import lax
from jax.experimental import pallas as pl
from jax.experimental.pallas import tpu as pltpu

F32 = jnp.float32
BF16 = jnp.bfloat16

D_MODEL = 1024
HEADS = 4
DK = 128
D_GRP = HEADS * DK
CHUNK = 64
ML_CONV = 4
FFN_CONV = 3
D_FF = 2816
CA_DH = D_MODEL // HEADS
DEPTH = 1
ALPHA = (2.0 * DEPTH) ** 0.25
LN_EPS = 1e-5
NEG_BIG = -1e30
D_IN = 8 * D_GRP + 2 * HEADS
D_IN_PAD = 8 * D_GRP + 128
ADAM_LR, ADAM_B1, ADAM_B2, ADAM_EPS, ADAM_WD, ADAM_STEP = 0.001, 0.9, 0.999, 1e-08, 0.01, 10

SUBLANES = 8
LANES = 128
VMEM_BYTES = 64 * 1024 * 1024


def _pcall(body, pin=True, **kw):
    if not pin:
        return _call(body, **kw)
    kw["out_shape"] = jax.tree.map(lambda s: pltpu.HBM(s.shape, s.dtype), kw["out_shape"])
    call = _call(body, **kw)

    def pinned(*args):
        return call(*[pltpu.with_memory_space_constraint(x, pltpu.HBM) if jnp.issubdtype(x.dtype, jnp.floating) else x
                      for x in args])
    return pinned


def _call(body, **kw):
    return pl.pallas_call(body, **kw)


def _params(semantics, vmem_bytes):
    limit = int(min(max(2 * vmem_bytes, 16 * 1024 * 1024), VMEM_BYTES - 8 * 1024 * 1024))
    return pltpu.CompilerParams(dimension_semantics=semantics, vmem_limit_bytes=limit)


def _nbytes(shape, dtype):
    n = 1
    for s in shape:
        n *= s
    return n * jnp.dtype(dtype).itemsize


def _dg(a, b, ca, cb):
    return lax.dot_general(a.astype(BF16), b.astype(BF16), (((ca,), (cb,)), ((), ())),
                           preferred_element_type=F32)


@jax.custom_vjp
def mm_nn(a, b):
    return _dg(a, b, 1, 0)


mm_nn.defvjp(lambda a, b: (_dg(a, b, 1, 0), (a, b)),
             lambda r, g: (_dg(g, r[1], 1, 1).astype(r[0].dtype), _dg(r[0], g, 0, 0).astype(r[1].dtype)))


@jax.custom_vjp
def mm_nt(a, b):
    return _dg(a, b, 1, 1)


mm_nt.defvjp(lambda a, b: (_dg(a, b, 1, 1), (a, b)),
             lambda r, g: (_dg(g, r[1], 1, 0).astype(r[0].dtype), _dg(g, r[0], 0, 0).astype(r[1].dtype)))


@jax.custom_vjp
def mm_tn(a, b):
    return _dg(a, b, 0, 0)


mm_tn.defvjp(lambda a, b: (_dg(a, b, 0, 0), (a, b)),
             lambda r, g: (_dg(r[1], g, 1, 1).astype(r[0].dtype), _dg(r[0], g, 1, 0).astype(r[1].dtype)))


def _tri(n, lower):
    r = lax.broadcasted_iota(jnp.int32, (n, n), 0)
    c = lax.broadcasted_iota(jnp.int32, (n, n), 1)
    return ((r >= c) if lower else (r <= c)).astype(F32)


def _tri_dot(lower, x):
    t = _tri(x.shape[0], lower).astype(BF16)
    hi = x.astype(BF16)
    rest = x - hi.astype(F32)
    mid = rest.astype(BF16)
    lo = (rest - mid.astype(F32)).astype(BF16)
    return sum(lax.dot_general(t, p, (((1,), (0,)), ((), ())), preferred_element_type=F32) for p in (hi, mid, lo))


@jax.custom_vjp
def cumsum_rows(x):
    return _tri_dot(True, x)


cumsum_rows.defvjp(lambda x: (_tri_dot(True, x), None), lambda _, g: (_tri_dot(False, g),))


def _shift_impl(halo, x, d):
    xx = jnp.concatenate([halo, x], axis=0)
    return pltpu.roll(xx, d, 0)[SUBLANES:]


@functools.partial(jax.custom_vjp, nondiff_argnums=(2,))
def shift_rows(halo, x, d):
    return _shift_impl(halo, x, d)


def _shift_bwd(d, _, g):
    n = g.shape[0] + SUBLANES
    gg = jnp.concatenate([jnp.zeros((SUBLANES, g.shape[1]), g.dtype), g], axis=0)
    r = pltpu.roll(gg, n - d, 0)
    return r[:SUBLANES], r[SUBLANES:]


shift_rows.defvjp(lambda halo, x, d: (_shift_impl(halo, x, d), None), _shift_bwd)


def causal_conv(halo, x, w_rows, b):
    k = len(w_rows)
    y = b + w_rows[k - 1] * x
    for d in range(1, k):
        y = y + w_rows[k - 1 - d] * shift_rows(halo, x, d)
    return y


def _sigmoid(x):
    return 1.0 / (1.0 + jnp.exp(-x))


def _silu(x):
    return x * _sigmoid(x)


def _log_sigmoid(x):
    return jnp.minimum(x, 0.0) - jnp.log(1.0 + jnp.exp(-jnp.abs(x)))


def _pick_row(x, i):
    row = lax.broadcasted_iota(jnp.int32, (x.shape[0], 1), 0)
    return jnp.sum(jnp.where(row == i, x, 0.0), axis=0, keepdims=True)


def _layer_norm(z, g, b):
    mu = jnp.mean(z, axis=-1, keepdims=True)
    zc = z - mu
    var = jnp.mean(zc * zc, axis=-1, keepdims=True)
    return zc * lax.rsqrt(var + LN_EPS) * g + b


def _qk_conv(halo, x, w0, w1, w2, w3, b):
    return _silu(causal_conv(halo, x, (w0, w1, w2, w3), b))


def _grp(i):
    return pl.ds(i * D_GRP, D_GRP)


def _mixer_specs(n_chunks, reverse):
    def chunk(c):
        return n_chunks - 1 - c if reverse else c
    row8 = CHUNK // SUBLANES
    proj_spec = pl.BlockSpec((CHUNK, D_IN_PAD), lambda c: (chunk(c), 0))
    halo_spec = pl.BlockSpec((SUBLANES, 2 * D_GRP), lambda c: (jnp.maximum(chunk(c) * row8 - 1, 0), 2))
    small = [pl.BlockSpec((2, D_GRP), lambda c: (0, 0)), pl.BlockSpec((1, D_GRP), lambda c: (0, 0)),
             pl.BlockSpec((ML_CONV, 2 * D_GRP), lambda c: (0, 0)), pl.BlockSpec((1, 2 * D_GRP), lambda c: (0, 0)),
             pl.BlockSpec((1, D_GRP), lambda c: (0, 0))]
    state_specs = [pl.BlockSpec((1, HEADS, DK, DK), lambda c: (chunk(c), 0, 0, 0)),
                   pl.BlockSpec((1, HEADS, DK, DK), lambda c: (chunk(c), 0, 0, 0)),
                   pl.BlockSpec((1, HEADS, 1, DK), lambda c: (chunk(c), 0, 0, 0)),
                   pl.BlockSpec((1, HEADS, 1, DK), lambda c: (chunk(c), 0, 0, 0))]
    y_spec = pl.BlockSpec((CHUNK, 2 * D_GRP), lambda c: (chunk(c), 0))
    return proj_spec, halo_spec, small, state_specs, y_spec, chunk


def _heads(x):
    return [x[:, h * DK:(h + 1) * DK] for h in range(HEADS)]


def _last(x, j):
    lane = lax.broadcasted_iota(jnp.int32, (1, x.shape[-1]), 1)
    return jnp.sum(jnp.where(lane == j, x, 0.0), axis=-1, keepdims=True)


def _hg_chunk(st_t, hq, hf, hi, hgate, l0, l1, nw):
    n = hq.shape[0]
    lb = _sigmoid(l0 - l1)
    q = _silu(hq)
    lf = jnp.log(lb + (1.0 - lb) * _sigmoid(hf))
    k = (1.0 - lb) * _sigmoid(-hf)
    b = cumsum_rows(lf)
    b_ref = _pick_row(b, n // 2 - 1)
    b_last = _pick_row(b, n - 1)
    qa, ka =_heads(q * jnp.exp(b - b_ref)), _heads(k * jnp.exp(b_ref - b))
    qe, kd, eb, v = _heads(q * jnp.exp(b)), _heads(k * jnp.exp(b_last - b)), _heads(jnp.exp(b_last)), _heads(hi)
    tri = _tri(n, True) > 0
    attn = [jnp.where(tri, mm_nt(qa[h], ka[h]), 0.0) for h in range(HEADS)]
    o = [mm_nn(attn[h], v[h]) + mm_nt(qe[h], st_t[h]) for h in range(HEADS)]
    st_new = jnp.stack([eb[h] * st_t[h] + mm_tn(v[h], kd[h]) for h in range(HEADS)])
    yn = [o[h] * lax.rsqrt(jnp.mean(o[h] * o[h], axis=-1, keepdims=True) + LN_EPS) for h in range(HEADS)]
    return st_new, jnp.concatenate(yn, axis=1) * nw * _silu(hgate)


def _ml_chunk(c_st, n_st, m_st, q, k, v, gates, og, nw):
    n = q.shape[0]
    ig = jnp.stack([_last(gates, h) for h in range(HEADS)])
    log_f = _log_sigmoid(gates)
    fl = jnp.stack([_last(log_f, HEADS + h) for h in range(HEADS)])
    bw = cumsum_rows(jnp.concatenate([jnp.broadcast_to(fl[h], (n, DK)) for h in range(HEADS)], axis=1))
    b = jnp.stack([_last(x, 0) for x in _heads(bw)])
    g = jnp.sum(fl, axis=1, keepdims=True)
    eye = lax.broadcasted_iota(jnp.int32, (n, n), 0) == lax.broadcasted_iota(jnp.int32, (n, n), 1)
    e_row = jnp.sum(jnp.where(eye, ig - b, 0.0), axis=1, keepdims=True)
    d = jnp.where(_tri(n, True) > 0, b + e_row, -jnp.inf)
    inter = b + m_st
    m_t = jnp.maximum(inter, jnp.max(d, axis=2, keepdims=True))
    qs, kh, vh = _heads(q * (DK ** -0.5)), _heads(k), _heads(v)
    s = jnp.stack([mm_nt(qs[h], kh[h]) for h in range(HEADS)]) * jnp.exp(d - m_t)
    w_inter = jnp.exp(inter - m_t)
    num = (jnp.stack([mm_nn(s[h], vh[h]) for h in range(HEADS)])
           + w_inter * jnp.stack([mm_nn(qs[h], c_st[h]) for h in range(HEADS)]))
    den = jnp.sum(s, axis=2, keepdims=True) + w_inter * jnp.sum(jnp.stack(qs) * n_st, axis=2, keepdims=True)
    h_out = num / jnp.maximum(jnp.abs(den), jnp.exp(-m_t))
    a = g - b + ig
    m_new = jnp.maximum(g + m_st, jnp.max(a, axis=1, keepdims=True))
    decay = jnp.exp(g + m_st - m_new)
    wk = jnp.stack(kh) * jnp.exp(a - m_new)
    c_new = decay * c_st + jnp.stack([mm_tn(wk[h], vh[h]) for h in range(HEADS)])
    n_new = decay * n_st + jnp.sum(wk, axis=1, keepdims=True)
    hc = h_out - jnp.mean(h_out, axis=-1, keepdims=True)
    yn = hc * lax.rsqrt(jnp.mean(hc * hc, axis=-1, keepdims=True) + LN_EPS)
    y = _sigmoid(og) * (jnp.concatenate([yn[h] for h in range(HEADS)], axis=1) * nw)
    return c_new, n_new, m_new, y


def _mixer_inputs(proj_ref, lg_ref, hnw_ref, mnw_ref, qk):
    hg_in = (proj_ref[:, _grp(0)], proj_ref[:, _grp(1)], proj_ref[:, _grp(2)], proj_ref[:, _grp(3)],
             lg_ref[0:1, :], lg_ref[1:2, :], hnw_ref[...])
    ml_in = (qk[:, :D_GRP], qk[:, D_GRP:], proj_ref[:, _grp(6)], proj_ref[:, pl.ds(8 * D_GRP, LANES)],
             proj_ref[:, _grp(7)], mnw_ref[...])
    return hg_in, ml_in


def _mixer_fwd(proj, lb_logits, hg_nw, conv_w, conv_b, ml_nw):
    seq = proj.shape[0]
    n_chunks = seq // CHUNK
    proj_spec, halo_spec, small, state_specs, y_spec, _ = _mixer_specs(n_chunks, False)

    def body(proj_ref, halo_ref, lg_ref, hnw_ref, cw_ref, cb_ref, mnw_ref,
             y_ref, hst_ref, cst_ref, nst_ref, mst_ref, hs, cs, ns, ms):
        c = pl.program_id(0)

        @pl.when(c == 0)
        def _():
            hs[...] = jnp.zeros_like(hs)
            cs[...] = jnp.zeros_like(cs)
            ns[...] = jnp.zeros_like(ns)
            ms[...] = jnp.full(ms.shape, NEG_BIG, F32)

        hst_ref[0] = hs[...]
        cst_ref[0] = cs[...]
        nst_ref[0] = ns[...]
        mst_ref[0] = ms[...]
        halo = jnp.where(c > 0, halo_ref[...], 0.0)
        qk = _qk_conv(halo, proj_ref[:, pl.ds(4 * D_GRP, 2 * D_GRP)],
                      cw_ref[0:1, :], cw_ref[1:2, :], cw_ref[2:3, :], cw_ref[3:4, :], cb_ref[...])
        hg_in, ml_in = _mixer_inputs(proj_ref, lg_ref, hnw_ref, mnw_ref, qk)
        hs[...], y_hg = _hg_chunk(hs[...], *hg_in)
        cs[...], ns[...], m_new, y_ml = _ml_chunk(cs[...], ns[...], _last(ms[...], 0), *ml_in)
        ms[...] = jnp.broadcast_to(m_new, ms.shape)
        y_ref[:, pl.ds(0, D_GRP)] = y_hg.astype(BF16)
        y_ref[:, pl.ds(D_GRP, D_GRP)] = y_ml.astype(BF16)

    st = jax.ShapeDtypeStruct((n_chunks, HEADS, DK, DK), F32)
    vec = jax.ShapeDtypeStruct((n_chunks, HEADS, 1, DK), F32)
    vmem = 2 * (_nbytes((CHUNK, D_IN_PAD), F32) + _nbytes((CHUNK, 2 * D_GRP), F32) + 2 * _nbytes((HEADS, DK, DK), F32)) \
        + 2 * _nbytes((HEADS, DK, DK), F32)
    return _pcall(
        body, name="mixer_fwd", grid=(n_chunks,),
        in_specs=[proj_spec, halo_spec] + small,
        out_specs=[y_spec] + state_specs,
        out_shape=[jax.ShapeDtypeStruct((seq, 2 * D_GRP), BF16), st, st, vec, vec],
        scratch_shapes=[pltpu.VMEM((HEADS, DK, DK), F32), pltpu.VMEM((HEADS, DK, DK), F32),
                        pltpu.VMEM((HEADS, 1, DK), F32), pltpu.VMEM((HEADS, 1, DK), F32)],
        compiler_params=_params(("arbitrary",), vmem),
    )(proj, proj, lb_logits, hg_nw, conv_w, conv_b, ml_nw)


def _mixer_bwd(proj, dy, hst, cst, nst, mst, lb_logits, hg_nw, conv_w, conv_b, ml_nw):
    seq = proj.shape[0]
    n_chunks = seq // CHUNK
    proj_spec, halo_spec, small, state_specs, y_spec, _ = _mixer_specs(n_chunks, True)

    def body(proj_ref, halo_ref, dy_ref, hst_ref, cst_ref, nst_ref, mst_ref,
             lg_ref, hnw_ref, cw_ref, cb_ref, mnw_ref,
             dproj_ref, dbin_ref, dlg_ref, dhnw_ref, dcw_ref, dcb_ref, dmnw_ref,
             dhs, dcs, dns, dms, dhalo):
        c = pl.program_id(0)

        @pl.when(c == 0)
        def _():
            for r in (dhs, dcs, dns, dms, dhalo, dbin_ref, dlg_ref, dhnw_ref, dcw_ref, dcb_ref, dmnw_ref):
                r[...] = jnp.zeros_like(r)

        def put(cols, val):
            dproj_ref[:, cols] = val.astype(BF16)
            dbin_ref[:, cols] += jnp.sum(val, axis=0, keepdims=True)

        first = c == n_chunks - 1
        halo = jnp.where(first, 0.0, halo_ref[...])
        x_qk = proj_ref[:, pl.ds(4 * D_GRP, 2 * D_GRP)]
        conv_args = (halo, x_qk, cw_ref[0:1, :], cw_ref[1:2, :], cw_ref[2:3, :], cw_ref[3:4, :], cb_ref[...])
        qk, conv_vjp = jax.vjp(_qk_conv, *conv_args)
        hg_in, ml_in = _mixer_inputs(proj_ref, lg_ref, hnw_ref, mnw_ref, qk)
        _, hg_vjp = jax.vjp(_hg_chunk, hst_ref[0], *hg_in)
        _, ml_vjp = jax.vjp(_ml_chunk, cst_ref[0], nst_ref[0], _last(mst_ref[0], 0), *ml_in)
        dst, dhq, dhf, dhi, dhg, dl0, dl1, dnw = hg_vjp((dhs[...], dy_ref[:, pl.ds(0, D_GRP)]))
        dc, dn, dm, dq, dk, dv, dgates, dog, dmn = ml_vjp(
            (dcs[...], dns[...], _last(dms[...], 0), dy_ref[:, pl.ds(D_GRP, D_GRP)]))
        dhs[...] = dst
        dcs[...] = dc
        dns[...] = dn
        dms[...] = jnp.broadcast_to(dm, dms.shape)
        for i, val in ((0, dhq), (1, dhf), (2, dhi), (3, dhg), (6, dv), (7, dog)):
            put(_grp(i), val)
        put(pl.ds(8 * D_GRP, LANES), dgates)
        dlg_ref[0:1, :] += dl0
        dlg_ref[1:2, :] += dl1
        dhnw_ref[...] += dnw
        dmnw_ref[...] += dmn
        dh, dx, dw0, dw1, dw2, dw3, db = conv_vjp(jnp.concatenate([dq, dk], axis=1))
        tail = jnp.concatenate([jnp.zeros((CHUNK - SUBLANES, 2 * D_GRP), F32), dhalo[...]], axis=0)
        put(pl.ds(4 * D_GRP, 2 * D_GRP), dx + tail)
        dhalo[...] = dh
        for d, dw in enumerate((dw0, dw1, dw2, dw3)):
            dcw_ref[d:d + 1, :] += dw
        dcb_ref[...] += db

    row = pl.BlockSpec((1, D_GRP), lambda c: (0, 0))
    small_out = [pl.BlockSpec((1, D_IN_PAD), lambda c: (0, 0)), pl.BlockSpec((2, D_GRP), lambda c: (0, 0)), row,
                 pl.BlockSpec((ML_CONV, 2 * D_GRP), lambda c: (0, 0)), pl.BlockSpec((1, 2 * D_GRP), lambda c: (0, 0)), row]
    dy_spec = pl.BlockSpec((CHUNK, 2 * D_GRP), y_spec.index_map)
    vmem = 2 * (2 * _nbytes((CHUNK, D_IN_PAD), F32) + _nbytes((CHUNK, 2 * D_GRP), F32)
                + 2 * _nbytes((HEADS, DK, DK), F32)) + 2 * _nbytes((HEADS, DK, DK), F32) + 4 * 1024 * 1024
    return _pcall(
        body, name="mixer_bwd", grid=(n_chunks,),
        in_specs=[proj_spec, halo_spec, dy_spec] + state_specs + small,
        out_specs=[proj_spec] + small_out,
        out_shape=[jax.ShapeDtypeStruct((seq, D_IN_PAD), BF16), jax.ShapeDtypeStruct((1, D_IN_PAD), F32),
                   jax.ShapeDtypeStruct((2, D_GRP), F32), jax.ShapeDtypeStruct((1, D_GRP), F32),
                   jax.ShapeDtypeStruct((ML_CONV, 2 * D_GRP), F32), jax.ShapeDtypeStruct((1, 2 * D_GRP), F32),
                   jax.ShapeDtypeStruct((1, D_GRP), F32)],
        scratch_shapes=[pltpu.VMEM((HEADS, DK, DK), F32), pltpu.VMEM((HEADS, DK, DK), F32),
                        pltpu.VMEM((HEADS, 1, DK), F32), pltpu.VMEM((HEADS, 1, DK), F32),
                        pltpu.VMEM((SUBLANES, 2 * D_GRP), F32)],
        compiler_params=_params(("arbitrary",), vmem),
    )(proj, proj, dy, hst, cst, nst, mst, lb_logits, hg_nw, conv_w, conv_b, ml_nw)


def _tile(n, prefs, unit=None):
    unit = unit or n
    for p in prefs:
        if unit % p == 0 and n % p == 0:
            return p
    return unit


def _logical(arr):
    return arr.shape if arr.ndim == 2 else (arr.shape[1], arr.shape[0] * arr.shape[2])


def _group(arr):
    return arr.shape[-1]


def _split_spec(ndim, group, tr, tc, where):
    if ndim == 2:
        return pl.BlockSpec((tr, tc), where)
    per = group // tc
    assert per * tc == group, (group, tc)

    def index(*ids):
        bi, bj = where(*ids)
        return (bj // per, bi, bj % per)
    return pl.BlockSpec((None, tr, tc), index)


def _mm(name, mode, a, b, *, bias=None, res=None, res_scale=1.0, ln=None, out_dtype=F32, out_groups=None,
        copy_dtype=None, a_copy_dtype=None, tm=None, tn=None, tk=None):
    la, lb = _logical(a), _logical(b)
    if mode == "nn":
        (m, k), n = la, lb[1]
        n_unit = _group(b) if b.ndim == 3 else n
        kc = _group(a) if a.ndim == 3 else k
    elif mode == "nt":
        (m, k), n = la, lb[0]
        n_unit = n
        kc = min(_group(a) if a.ndim == 3 else k, _group(b) if b.ndim == 3 else k)
    else:
        (k, m), n = la, lb[1]
        n_unit, kc = (_group(b) if b.ndim == 3 else n), k
        assert a.ndim == 2
    if out_groups:
        n_unit = min(n_unit, n // out_groups)
    kind = ln[0] if ln else None
    tm = tm or (256 if ln else _tile(m, (512, 256, 128)))
    tn = n if ln else (tn or _tile(n, (512, 384, 256, 128), n_unit))
    tk = (tk or _tile(k, (2048, 512, 256, 128))) if mode == "tn" else k
    gi, gj, gk = m // tm, n // tn, k // tk
    assert gi * tm == m and gj * tn == n and gk * tk == k and n_unit % tn == 0, (name, m, n, k, tm, tn, tk)
    ca, cb = {"nn": (1, 0), "nt": (1, 1), "tn": (0, 0)}[mode]
    i_outer = gk > 1 or (gi - 1) * _nbytes(b.shape, b.dtype) <= (gj - 1) * _nbytes(a.shape, a.dtype)

    def ij(where):
        return (lambda p, q, kk: where(p, q, kk)) if i_outer else (lambda p, q, kk: where(q, p, kk))
    if mode == "tn":
        a_spec = pl.BlockSpec((tk, tm), ij(lambda i, j, kk: (kk, i)))
    elif a.ndim == 3:
        a_spec = pl.BlockSpec((a.shape[0], tm, _group(a)), ij(lambda i, j, kk: (0, i, 0)))
    else:
        a_spec = pl.BlockSpec((tm, k), ij(lambda i, j, kk: (i, 0)))
    if mode != "nt":
        b_spec = _split_spec(b.ndim, _group(b), tk, tn, ij(lambda i, j, kk: (kk, j)))
    elif b.ndim == 3:
        b_spec = pl.BlockSpec((b.shape[0], tn, _group(b)), ij(lambda i, j, kk: (0, j, 0)))
    else:
        b_spec = pl.BlockSpec((tn, k), ij(lambda i, j, kk: (j, 0)))
    row_spec = pl.BlockSpec((1, tn), ij(lambda i, j, kk: (0, j)))
    blk_spec = pl.BlockSpec((tm, tn), ij(lambda i, j, kk: (i, j)))
    ins, in_specs = [a, b], [a_spec, b_spec]
    if bias is not None:
        ins.append(bias), in_specs.append(row_spec)
    if res is not None:
        ins.append(res), in_specs.append(blk_spec)
    if kind == "fwd":
        ins += [ln[1], ln[2]]
        in_specs += [row_spec, row_spec]
    elif kind == "bwd":
        ins += [ln[1], ln[2], ln[3]]
        in_specs += [blk_spec, row_spec, row_spec]
    if out_groups:
        blk_out = jax.ShapeDtypeStruct((out_groups, m, n // out_groups), out_dtype)
        out_spec = _split_spec(3, n // out_groups, tm, tn, ij(lambda i, j, kk: (i, j)))
    else:
        blk_out, out_spec = jax.ShapeDtypeStruct((m, n), out_dtype), blk_spec
    row_out = jax.ShapeDtypeStruct((1, n), F32)
    if kind is None:
        out_shape, out_specs = [blk_out], [out_spec]
    elif kind == "fwd":
        out_shape, out_specs = [blk_out, blk_out], [blk_spec, blk_spec]
    else:
        out_shape, out_specs = [blk_out, row_out, row_out], [blk_spec, row_spec, row_spec]
    if copy_dtype is not None:
        out_shape.append(jax.ShapeDtypeStruct((m, n), copy_dtype))
        out_specs.append(blk_spec)
    if a_copy_dtype is not None:
        assert mode != "tn" and a.ndim == 2 and copy_dtype is None
        out_shape.append(jax.ShapeDtypeStruct((m, k), a_copy_dtype))
        out_specs.append(a_spec)
    n_in = len(ins)

    def body(*refs):
        in_refs, out_refs, acc_ref = refs[:n_in], refs[n_in:n_in + len(out_shape)], refs[-1]
        i, kk = pl.program_id(0 if i_outer else 1), pl.program_id(2)
        a_ref, b_ref = in_refs[:2]
        extra = list(in_refs[2:])
        if a_copy_dtype is not None:
            out_refs[-1][...] = a_ref[...].astype(a_copy_dtype)

        def epilogue(acc):
            rest = list(extra)
            if bias is not None:
                acc = acc + rest.pop(0)[...]
            if res is not None:
                acc = acc + res_scale * rest.pop(0)[...]
            if kind is None:
                out_refs[0][...] = acc.astype(out_dtype)
                return
            if kind == "fwd":
                out_refs[0][...] = acc
                y = _layer_norm(acc, rest[0][...], rest[1][...])
                out_refs[1][...] = y
                if copy_dtype is not None:
                    out_refs[-1][...] = y.astype(copy_dtype)
                return
            _, vjp = jax.vjp(_layer_norm, rest[0][...], rest[1][...], rest[2][...])
            dz, dg, db = vjp(acc)
            out_refs[0][...] = dz
            out_refs[1][...] += dg
            out_refs[2][...] += db
            if copy_dtype is not None:
                out_refs[-1][...] = dz.astype(copy_dtype)

        if kind == "bwd":
            @pl.when((i == 0) & (kk == 0))
            def _():
                out_refs[1][...] = jnp.zeros_like(out_refs[1])
                out_refs[2][...] = jnp.zeros_like(out_refs[2])

        def chunk(ref, c0, last):
            if ref.ndim == 3:
                g = ref.shape[2]
                return ref[c0 // g, :, pl.ds(c0 % g, kc)]
            return ref[:, pl.ds(c0, kc)] if last else ref[pl.ds(c0, kc), :]

        if mode == "tn" or kc == k:
            prod = _dg(a_ref[...], b_ref[...], ca, cb)
        else:
            prod = None
            for c0 in range(0, k, kc):
                part = _dg(chunk(a_ref, c0, True), chunk(b_ref, c0, mode == "nt"), ca, cb)
                prod = part if prod is None else prod + part
        if gk == 1:
            epilogue(prod)
            return

        @pl.when(kk == 0)
        def _():
            acc_ref[...] = prod

        @pl.when(kk > 0)
        def _():
            acc_ref[...] += prod

        @pl.when(kk == gk - 1)
        def _():
            epilogue(acc_ref[...])

    vmem = (2 * (_nbytes((tm, tk), a.dtype) + _nbytes((tk, tn), b.dtype))
            + (2 * len(ins) + 2 * len(out_shape) + 1) * _nbytes((tm, tn), F32))
    outs = _pcall(
        body, name=name, grid=(gi, gj, gk) if i_outer else (gj, gi, gk), in_specs=in_specs, out_specs=out_specs,
        out_shape=out_shape, scratch_shapes=[pltpu.VMEM((tm, tn) if gk > 1 else (SUBLANES, LANES), F32)],
        compiler_params=_params(("arbitrary", "arbitrary", "arbitrary"), vmem),
    )(*ins)
    return outs[0] if len(out_shape) == 1 else outs


def _attn_head(q, k, v):
    sc = mm_nt(q, k) * (CA_DH ** -0.5)
    e = jnp.exp(sc - jnp.max(sc, axis=-1, keepdims=True))
    return mm_nn(e / jnp.sum(e, axis=-1, keepdims=True), v)


def _attn_fwd(q, kv):
    seq, n_mem = q.shape[0], kv.shape[0]
    tq = _tile(seq, (512, 256, 128))

    def body(q_ref, kv_ref, o_ref):
        for h in range(HEADS):
            hd = pl.ds(h * CA_DH, CA_DH)
            o = _attn_head(q_ref[:, hd], kv_ref[:, hd], kv_ref[:, pl.ds(D_MODEL + h * CA_DH, CA_DH)])
            o_ref[:, hd] = o.astype(BF16)

    return _pcall(
        body, name="attn_fwd", grid=(seq // tq,),
        in_specs=[pl.BlockSpec((tq, D_MODEL), lambda i: (i, 0)), pl.BlockSpec((n_mem, 2 * D_MODEL), lambda i: (0, 0))],
        out_specs=pl.BlockSpec((tq, D_MODEL), lambda i: (i, 0)), out_shape=jax.ShapeDtypeStruct((seq, D_MODEL), BF16),
        compiler_params=_params(("arbitrary",), 4 * _nbytes((tq, D_MODEL), F32) + 2 * _nbytes((n_mem, 2 * D_MODEL), F32)),
    )(q, kv)


def _attn_bwd(q, kv, do):
    seq, n_mem = q.shape[0], kv.shape[0]
    tq = _tile(seq, (512, 256, 128))

    def body(q_ref, kv_ref, do_ref, dq_ref, dkv_ref):
        @pl.when(pl.program_id(0) == 0)
        def _():
            dkv_ref[...] = jnp.zeros_like(dkv_ref)

        for h in range(HEADS):
            hd = pl.ds(h * CA_DH, CA_DH)
            vd = pl.ds(D_MODEL + h * CA_DH, CA_DH)
            _, vjp = jax.vjp(_attn_head, q_ref[:, hd], kv_ref[:, hd], kv_ref[:, vd])
            dq, dk, dv = vjp(do_ref[:, hd].astype(F32))
            dq_ref[:, hd] = dq.astype(BF16)
            dkv_ref[:, hd] += dk
            dkv_ref[:, vd] += dv

    return _pcall(
        body, name="attn_bwd", grid=(seq // tq,),
        in_specs=[pl.BlockSpec((tq, D_MODEL), lambda i: (i, 0)), pl.BlockSpec((n_mem, 2 * D_MODEL), lambda i: (0, 0)),
                  pl.BlockSpec((tq, D_MODEL), lambda i: (i, 0))],
        out_specs=[pl.BlockSpec((tq, D_MODEL), lambda i: (i, 0)), pl.BlockSpec((n_mem, 2 * D_MODEL), lambda i: (0, 0))],
        out_shape=[jax.ShapeDtypeStruct((seq, D_MODEL), BF16), jax.ShapeDtypeStruct((n_mem, 2 * D_MODEL), F32)],
        compiler_params=_params(("arbitrary",), 6 * _nbytes((tq, D_MODEL), F32) + 4 * _nbytes((n_mem, 2 * D_MODEL), F32)),
    )(q, kv, do)


def _ffn_mid(hg, xg, hv, xv, wg0, wg1, wg2, bg, wv0, wv1, wv2, bv):
    return jax.nn.gelu(causal_conv(hg, xg, (wg0, wg1, wg2), bg)) * causal_conv(hv, xv, (wv0, wv1, wv2), bv)


FFN_TB = 256
FFN_W = D_FF // 2
FFN_J = D_FF // FFN_W
MXU_COLS = 256
FFN_PIECES = tuple((off, min(MXU_COLS, FFN_W - off)) for off in range(0, FFN_W, MXU_COLS))


def _ffn_common_specs(seq, row):
    tb = min(FFN_TB, seq)
    full = pl.BlockSpec((tb, D_MODEL), lambda t, j: (row(t), 0))
    vec = pl.BlockSpec((1, D_MODEL), lambda t, j: (0, 0))
    halves = []
    for off in (0, FFN_J):
        halves.append(dict(
            w_up=pl.BlockSpec((None, D_MODEL, FFN_W), lambda t, j, off=off: (j + off, 0, 0)),
            taps=pl.BlockSpec((FFN_CONV, FFN_W), lambda t, j, off=off: (0, j + off)),
            bias=pl.BlockSpec((1, FFN_W), lambda t, j, off=off: (0, j + off))))
    w_down = pl.BlockSpec((FFN_W, D_MODEL), lambda t, j: (j, 0))
    u_blk = pl.BlockSpec((2, tb, FFN_W), lambda t, j: (0, row(t), j))
    return tb, full, vec, halves, w_down, u_blk


def _ffn_vmem(tb):
    return (_nbytes((2, tb, FFN_W), F32) + _nbytes((2, tb, FFN_W), BF16) + 3 * _nbytes((D_MODEL, FFN_W), BF16)
            + 10 * _nbytes((tb, D_MODEL), F32))


def _conv_params(taps_ref, bias_ref, cols):
    return taps_ref[0:1, cols], taps_ref[1:2, cols], taps_ref[2:3, cols], bias_ref[:, cols]


def _ffn_fwd(x2b, x2, w_up, conv_w, conv_b, w_down, ln_g, ln_b, target):
    seq = x2.shape[0]
    tb, full, vec, halves, wd_spec, u_blk = _ffn_common_specs(seq, lambda t: t)
    nt = seq // tb

    def body(xb_ref, wg_ref, wv_ref, tg_ref, tv_ref, bg_ref, bv_ref, wd_ref, x_ref, g_ref, b_ref, tgt_ref,
             u_ref, h_ref, dz_ref, dg_ref, db_ref, loss_ref, dzb_ref, acc, carry):
        t, j = pl.program_id(0), pl.program_id(1)
        xb = xb_ref[...]
        pieces = [pl.ds(off, width) for off, width in FFN_PIECES]
        ug = [_dg(xb, wg_ref[:, cols], 1, 0) for cols in pieces]
        uv = [_dg(xb, wv_ref[:, cols], 1, 0) for cols in pieces]
        hs = []
        for cols, g, v in zip(pieces, ug, uv):
            u_ref[0, :, cols] = g
            u_ref[1, :, cols] = v
            halo_g = jnp.where(t == 0, 0.0, carry[j, 0, :, cols])
            halo_v = jnp.where(t == 0, 0.0, carry[j, 1, :, cols])
            h = _ffn_mid(halo_g, g, halo_v, v, *_conv_params(tg_ref, bg_ref, cols),
                         *_conv_params(tv_ref, bv_ref, cols)).astype(BF16)
            carry[j, 0, :, cols] = g[tb - SUBLANES:, :]
            carry[j, 1, :, cols] = v[tb - SUBLANES:, :]
            h_ref[:, cols] = h
            hs.append(h)
        part = None
        for cols, h in zip(pieces, hs):
            p = _dg(h, wd_ref[cols, :], 1, 0)
            part = p if part is None else part + p

        @pl.when(j == 0)
        def _():
            acc[...] = part

        @pl.when(j > 0)
        def _():
            acc[...] += part

        @pl.when(j == FFN_J - 1)
        def _():
            y, vjp = jax.vjp(_layer_norm, acc[...] + ALPHA * x_ref[...], g_ref[...], b_ref[...])
            err = y - tgt_ref[...]
            part_loss = 0.5 * jnp.sum(jnp.sum(err * err, axis=1, keepdims=True), axis=0, keepdims=True) / D_MODEL
            dz, dg, db = vjp(err / D_MODEL)

            @pl.when(t == 0)
            def _():
                for r in (dg_ref, db_ref, loss_ref):
                    r[...] = jnp.zeros_like(r)

            dz_ref[...] = dz
            dzb_ref[...] = dz.astype(BF16)
            dg_ref[...] += dg
            db_ref[...] += db
            loss_ref[...] += jnp.broadcast_to(part_loss, (1, LANES))

    h0, h1 = halves
    row = jax.ShapeDtypeStruct((1, D_MODEL), F32)
    return _pcall(
        body, name="ffn_fwd", grid=(nt, FFN_J),
        in_specs=[full, h0["w_up"], h1["w_up"], h0["taps"], h1["taps"], h0["bias"], h1["bias"], wd_spec, full, vec, vec,
                  full],
        out_specs=[u_blk, pl.BlockSpec((tb, FFN_W), lambda t, j: (t, j)), full, vec, vec,
                   pl.BlockSpec((1, LANES), lambda t, j: (0, 0)), full],
        out_shape=[jax.ShapeDtypeStruct((2, seq, D_FF), F32), jax.ShapeDtypeStruct((seq, D_FF), BF16),
                   jax.ShapeDtypeStruct((seq, D_MODEL), F32), row, row, jax.ShapeDtypeStruct((1, LANES), F32),
                   jax.ShapeDtypeStruct((seq, D_MODEL), BF16)],
        scratch_shapes=[pltpu.VMEM((tb, D_MODEL), F32), pltpu.VMEM((FFN_J, 2, SUBLANES, FFN_W), F32)],
        compiler_params=_params(("arbitrary", "arbitrary"), _ffn_vmem(tb)),
    )(x2b, w_up, w_up, conv_w, conv_w, conv_b, conv_b, w_down, x2, ln_g, ln_b, target)


def _ffn_bwd(u, conv_w, conv_b, dz3b, dz3, w_down, w_up, z2, ln_g, ln_b):
    seq = dz3.shape[0]
    tb = min(FFN_TB, seq)
    nt = seq // tb
    row8 = tb // SUBLANES
    tb, full, vec, halves, wd_spec, u_blk = _ffn_common_specs(seq, lambda t: nt - 1 - t)
    halo = pl.BlockSpec((2, SUBLANES, FFN_W), lambda t, j: (0, jnp.maximum((nt - 1 - t) * row8 - 1, 0), j))

    def body(u_ref, halo_ref, tg_ref, tv_ref, bg_ref, bv_ref, dzb_ref, wd_ref, wg_ref, wv_ref, dz3_ref, z_ref, g_ref,
             b_ref, du_ref, dw_ref, dbias_ref, dz_ref, dg_ref, db_ref, dz2b_ref, acc, carry):
        t, j = pl.program_id(0), pl.program_id(1)

        @pl.when((t == 0) & (j == 0))
        def _():
            for r in (dw_ref, dbias_ref, dg_ref, db_ref):
                r[...] = jnp.zeros_like(r)

        pieces = [pl.ds(off, width) for off, width in FFN_PIECES]
        dzb = dzb_ref[...]
        dhs = [_dg(dzb, wd_ref[cols, :], 1, 1) for cols in pieces]
        first = t == nt - 1
        dus = []
        for cols, dh in zip(pieces, dhs):
            args = (jnp.where(first, 0.0, halo_ref[0, :, cols]), u_ref[0, :, cols],
                    jnp.where(first, 0.0, halo_ref[1, :, cols]), u_ref[1, :, cols],
                    *_conv_params(tg_ref, bg_ref, cols), *_conv_params(tv_ref, bv_ref, cols))
            _, vjp = jax.vjp(_ffn_mid, *args)
            dhg, dxg, dhv, dxv, g0, g1, g2, gb, v0, v1, v2, vb = vjp(dh)
            zeros = jnp.zeros((tb - SUBLANES, dh.shape[1]), F32)
            dug = (dxg + jnp.concatenate([zeros, jnp.where(t == 0, 0.0, carry[j, 0, :, cols])], axis=0)).astype(BF16)
            duv = (dxv + jnp.concatenate([zeros, jnp.where(t == 0, 0.0, carry[j, 1, :, cols])], axis=0)).astype(BF16)
            carry[j, 0, :, cols] = dhg
            carry[j, 1, :, cols] = dhv
            du_ref[0, :, cols] = dug
            du_ref[1, :, cols] = duv
            for half, parts in enumerate(((g0, g1, g2), (v0, v1, v2))):
                for d, p in enumerate(parts):
                    dw_ref[j, half, d:d + 1, cols] += p
            dbias_ref[j, 0, :, cols] += gb
            dbias_ref[j, 1, :, cols] += vb
            dus.append((dug, duv))
        part = None
        for cols, (dug, duv) in zip(pieces, dus):
            p = _dg(dug, wg_ref[:, cols], 1, 1) + _dg(duv, wv_ref[:, cols], 1, 1)
            part = p if part is None else part + p

        @pl.when(j == 0)
        def _():
            acc[...] = part

        @pl.when(j > 0)
        def _():
            acc[...] += part

        @pl.when(j == FFN_J - 1)
        def _():
            _, ln_vjp = jax.vjp(_layer_norm, z_ref[...], g_ref[...], b_ref[...])
            dz, dg, db = ln_vjp(acc[...] + ALPHA * dz3_ref[...])
            dz_ref[...] = dz
            dz2b_ref[...] = dz.astype(BF16)
            dg_ref[...] += dg
            db_ref[...] += db

    h0, h1 = halves
    row = jax.ShapeDtypeStruct((1, D_MODEL), F32)
    whole = lambda *shape: pl.BlockSpec(shape, lambda t, j: (0,) * len(shape))
    return _pcall(
        body, name="ffn_bwd", grid=(nt, FFN_J),
        in_specs=[u_blk, halo, h0["taps"], h1["taps"], h0["bias"], h1["bias"], full, wd_spec, h0["w_up"], h1["w_up"],
                  full, full, vec, vec],
        out_specs=[u_blk, whole(FFN_J, 2, FFN_CONV, FFN_W), whole(FFN_J, 2, 1, FFN_W), full, vec, vec, full],
        out_shape=[jax.ShapeDtypeStruct((2, seq, D_FF), BF16), jax.ShapeDtypeStruct((FFN_J, 2, FFN_CONV, FFN_W), F32),
                   jax.ShapeDtypeStruct((FFN_J, 2, 1, FFN_W), F32), jax.ShapeDtypeStruct((seq, D_MODEL), F32), row, row,
                   jax.ShapeDtypeStruct((seq, D_MODEL), BF16)],
        scratch_shapes=[pltpu.VMEM((tb, D_MODEL), F32), pltpu.VMEM((FFN_J, 2, SUBLANES, FFN_W), F32)],
        compiler_params=_params(("arbitrary", "arbitrary"), _ffn_vmem(tb)),
    )(u, u, conv_w, conv_w, conv_b, conv_b, dz3b, w_down, w_up, w_up, dz3, z2, ln_g, ln_b)


def _adamw_math(w, g, m, v):
    m_new = ADAM_B1 * m + (1.0 - ADAM_B1) * g
    v_new = ADAM_B2 * v + (1.0 - ADAM_B2) * jnp.square(g)
    m_hat = m_new / (1.0 - ADAM_B1 ** ADAM_STEP)
    v_hat = v_new / (1.0 - ADAM_B2 ** ADAM_STEP)
    return -ADAM_LR * (m_hat / (jnp.sqrt(v_hat) + ADAM_EPS) + ADAM_WD * w), m_new, v_new


def _adamw_many(name, ws, gs, ms, vs):
    n = len(ws)

    def body(*refs):
        w_refs, g_refs, m_refs, v_refs = (refs[i * n:(i + 1) * n] for i in range(4))
        d_refs, nm_refs, nv_refs = (refs[(4 + i) * n:(5 + i) * n] for i in range(3))
        for i in range(n):
            d_refs[i][...], nm_refs[i][...], nv_refs[i][...] = _adamw_math(
                w_refs[i][...], g_refs[i][...], m_refs[i][...], v_refs[i][...])

    vm = pl.BlockSpec(memory_space=pltpu.VMEM)
    outs = _pcall(
        body, pin=False, name=name, in_specs=[vm] * (4 * n), out_specs=[vm] * (3 * n),
        out_shape=[jax.ShapeDtypeStruct(w.shape, F32) for w in ws] * 3,
    )(*ws, *gs, *ms, *vs)
    return outs[:n], outs[n:2 * n], outs[2 * n:]


def _adamw_halves(name, core, w, mine, theirs, m, v):
    rows, cols = w.shape
    half_rows = mine.shape[0]
    tr = _tile(half_rows, (256, 176, 128))
    nbh = half_rows // tr
    assert 2 * half_rows == rows

    def body(c_ref, w_ref, a_ref, b_ref, m_ref, v_ref, g_ref, d_ref, nm_ref, nv_ref):
        g = jnp.where(pl.program_id(0) // nbh == c_ref[0], a_ref[...], b_ref[...])
        g_ref[...] = g
        d_ref[...], nm_ref[...], nv_ref[...] = _adamw_math(w_ref[...], g, m_ref[...], v_ref[...])

    spec = pl.BlockSpec((tr, cols), lambda i, c_ref: (i, 0))
    half = pl.BlockSpec((tr, cols), lambda i, c_ref: (i % nbh, 0))
    sh = jax.ShapeDtypeStruct((rows, cols), F32)
    grid_spec = pltpu.PrefetchScalarGridSpec(
        num_scalar_prefetch=1, grid=(rows // tr,), in_specs=[spec, half, half, spec, spec], out_specs=[spec] * 4)
    return _pcall(
        body, name=name, grid_spec=grid_spec, out_shape=[sh] * 4,
        compiler_params=_params(("arbitrary",), 18 * _nbytes((tr, -(-cols // LANES) * LANES), F32)),
    )(core, w, mine, theirs, m, v)


MESH = pl.DeviceIdType.MESH
ANY = pl.BlockSpec(memory_space=pl.ANY)
N_CHIPS = 4
BF16_ROWS = 16


def _me():
    return lax.axis_index("x"), lax.axis_index("y"), lax.axis_index("c")


def _other_chips(x, y):
    return [(1 - x, y), (x, 1 - y), (1 - x, 1 - y)]


def _remote(src, dst, ssem, rsem, dev):
    return pltpu.make_async_remote_copy(src_ref=src, dst_ref=dst, send_sem=ssem, recv_sem=rsem,
                                        device_id=dev, device_id_type=MESH)


def _half_rows(ref_rows, cc):
    half = ref_rows // 2
    return pl.ds(pl.multiple_of(cc * half, BF16_ROWS), half)


def _gather_weights(shards):
    n = len(shards)
    n_ici = n * (N_CHIPS - 1)

    def body(*refs):
        ins, outs, (ssem, rsem, lsem, lrsem) = refs[:n], refs[n:2 * n], refs[2 * n:]
        x, y, c = _me()
        k_me = 2 * x + y
        sib = (x, y, 1 - c)
        chips = _other_chips(x, y)
        started = []
        for i, (w_ref, o_ref) in enumerate(zip(ins, outs)):
            cp = _remote(w_ref, o_ref.at[k_me], lsem.at[i], lrsem.at[i], sib)
            cp.start()
            started.append(cp)
        for r, (px, py) in enumerate(chips):
            for i, (w_ref, o_ref) in enumerate(zip(ins, outs)):
                rows = _half_rows(w_ref.shape[0], c)
                s = r * n + i
                cp = _remote(w_ref.at[rows], o_ref.at[k_me, rows], ssem.at[s], rsem.at[s], (px, py, c))
                cp.start()
                started.append(cp)
        for r, (px, py) in enumerate(chips):
            for i, o_ref in enumerate(outs):
                blk = o_ref.at[2 * px + py, _half_rows(o_ref.shape[1], c)]
                s = r * n + i
                _remote(blk, blk, ssem.at[s], rsem.at[s], (px, py, c)).wait_recv()
                cp = _remote(blk, blk, ssem.at[n_ici + s], rsem.at[n_ici + s], sib)
                cp.start()
                started.append(cp)
        for r, (px, py) in enumerate(chips):
            for i, o_ref in enumerate(outs):
                blk = o_ref.at[2 * px + py, _half_rows(o_ref.shape[1], 1 - c)]
                s = n_ici + r * n + i
                _remote(blk, blk, ssem.at[s], rsem.at[s], sib).wait_recv()
        for cp in started[n:]:
            cp.wait_send()
        for cp in started[:n]:
            cp.wait()

    return _pcall(
        body, name="gather_weights", in_specs=[ANY] * n, out_specs=[ANY] * n,
        out_shape=[jax.ShapeDtypeStruct((N_CHIPS,) + s.shape, s.dtype) for s in shards],
        scratch_shapes=[pltpu.SemaphoreType.DMA((2 * n_ici,)), pltpu.SemaphoreType.DMA((2 * n_ici,)),
                        pltpu.SemaphoreType.DMA((n,)), pltpu.SemaphoreType.DMA((n,))],
    )(*shards)


def _swap_halves(name, grads):
    n = len(grads)

    def body(*refs):
        ins, outs, (ssem, rsem) = refs[:n], refs[n:2 * n], refs[2 * n:]
        x, y, c = _me()
        copies = []
        for i, (g_ref, o_ref) in enumerate(zip(ins, outs)):
            for k in range(N_CHIPS):
                s = i * N_CHIPS + k
                cp = _remote(g_ref.at[k, _half_rows(g_ref.shape[1], 1 - c)], o_ref.at[k], ssem.at[s], rsem.at[s],
                             (x, y, 1 - c))
                cp.start()
                copies.append(cp)
        for cp in copies:
            cp.wait()

    return _pcall(
        body, name=name, in_specs=[ANY] * n, out_specs=[ANY] * n,
        out_shape=[jax.ShapeDtypeStruct((N_CHIPS, g.shape[1] // 2, g.shape[2]), g.dtype) for g in grads],
        scratch_shapes=[pltpu.SemaphoreType.DMA((n * N_CHIPS,)), pltpu.SemaphoreType.DMA((n * N_CHIPS,))],
    )(*grads)


SEM = pl.BlockSpec(memory_space=pltpu.SEMAPHORE)
IN_HBM = pl.BlockSpec(memory_space=pltpu.HBM)
SPLIT_PARAMS = dict(compiler_params=pltpu.CompilerParams(has_side_effects=pltpu.SideEffectType.DATAFLOW_SIDE_EFFECTING))


def _split_start(name, sources, landings, n_copies, plan):
    ns, nl = len(sources), len(landings)

    def body(*refs):
        ins, lands, (ssem, rsem), token = refs[:ns], refs[ns:ns + nl], refs[ns + nl:ns + nl + 2], refs[-1]
        for s, (src, dst, _, dev) in enumerate(plan(ins, lands)):
            _remote(src, dst, ssem.at[s], rsem.at[s], dev).start()
        token[...] = jnp.zeros_like(token)

    arrays = list(sources) + list(landings)
    outs = _call(
        body, name=name, in_specs=[IN_HBM] * (ns + nl),
        out_specs=[SEM, SEM] + [IN_HBM] * (ns + nl) + [pl.BlockSpec(memory_space=pltpu.VMEM)],
        out_shape=[pltpu.SemaphoreType.DMA((n_copies,)), pltpu.SemaphoreType.DMA((n_copies,))]
        + [pltpu.HBM(a.shape, a.dtype) for a in arrays] + [jax.ShapeDtypeStruct((SUBLANES, LANES), F32)],
        input_output_aliases={i: 2 + i for i in range(ns + nl)}, **SPLIT_PARAMS,
    )(*[pltpu.with_memory_space_constraint(a, pltpu.HBM) for a in arrays])
    return (outs[:-1], ns), outs[-1]


def _split_wait(name, handle, after, plan):
    (ssem, rsem, *thru), ns = handle
    nl = len(thru) - ns

    def body(*refs):
        ins, lands, (ssem_ref, rsem_ref) = refs[:ns], refs[ns:ns + nl], refs[ns + nl:ns + nl + 2]
        for s, (src, _, dst, dev) in enumerate(plan(ins, lands)):
            cp = _remote(src, dst, ssem_ref.at[s], rsem_ref.at[s], dev)
            cp.wait_send()
            cp.wait_recv()

    outs = _call(
        body, name=name, in_specs=[IN_HBM] * (ns + nl) + [SEM, SEM, ANY], out_specs=[IN_HBM] * (ns + nl),
        out_shape=[pltpu.HBM(t.shape, t.dtype) for t in thru],
        input_output_aliases={i: i for i in range(ns + nl)}, **SPLIT_PARAMS,
    )(*thru, ssem, rsem, after)
    return outs[:ns], outs[ns:]


def _swap_plan(ins, lands):
    x, y, c = _me()
    return [(g_ref.at[k, _half_rows(g_ref.shape[1], 1 - c)], l_ref.at[k], l_ref.at[k], (x, y, 1 - c))
            for g_ref, l_ref in zip(ins, lands) for k in range(N_CHIPS)]


def _swap_start(name, grads):
    lands = [lax.empty((N_CHIPS, g.shape[1] // 2, g.shape[2]), g.dtype) for g in grads]
    return _split_start(name, grads, lands, len(grads) * N_CHIPS, _swap_plan)


def _swap_wait(name, handle, after):
    return _split_wait(name, handle, after, _swap_plan)


def _gather_plan(ins, lands):
    x, y, c = _me()
    k_me = 2 * x + y
    plan = [(w_ref, l_ref.at[k_me], l_ref.at[k_me], (x, y, 1 - c)) for w_ref, l_ref in zip(ins, lands)]
    for px, py in _other_chips(x, y):
        for w_ref, l_ref in zip(ins, lands):
            rows = _half_rows(w_ref.shape[0], c)
            plan.append((w_ref.at[rows], l_ref.at[k_me, rows], l_ref.at[2 * px + py, rows], (px, py, c)))
    return plan


def _gather_start(name, shards):
    lands = [lax.empty((N_CHIPS,) + s.shape, s.dtype) for s in shards]
    return _split_start(name, shards, lands, len(shards) * N_CHIPS, _gather_plan)


def _gather_wait(name, handle, after):
    return _split_wait(name, handle, after, _gather_plan)[1]


def _forward_halves(name, blocks):
    n = len(blocks)
    n_sem = n * (N_CHIPS - 1)

    def body(*refs):
        outs, (ssem, rsem) = refs[n:2 * n], refs[2 * n:]
        x, y, c = _me()
        sib = (x, y, 1 - c)
        chips = _other_chips(x, y)
        sends = []
        for r, (px, py) in enumerate(chips):
            for i, o_ref in enumerate(outs):
                blk = o_ref.at[2 * px + py, _half_rows(o_ref.shape[1], c)]
                cp = _remote(blk, blk, ssem.at[r * n + i], rsem.at[r * n + i], sib)
                cp.start()
                sends.append(cp)
        for r, (px, py) in enumerate(chips):
            for i, o_ref in enumerate(outs):
                blk = o_ref.at[2 * px + py, _half_rows(o_ref.shape[1], 1 - c)]
                _remote(blk, blk, ssem.at[r * n + i], rsem.at[r * n + i], sib).wait_recv()
        for cp in sends:
            cp.wait_send()

    return _pcall(
        body, name=name, in_specs=[ANY] * n, out_specs=[ANY] * n,
        out_shape=[jax.ShapeDtypeStruct(b.shape, b.dtype) for b in blocks],
        input_output_aliases={i: i for i in range(n)},
        scratch_shapes=[pltpu.SemaphoreType.DMA((n_sem,)), pltpu.SemaphoreType.DMA((n_sem,))],
    )(*blocks)


def _scatter_plan(ins, lands):
    x, y, c = _me()
    k_me = 2 * x + y
    return [(p_ref.at[2 * px + py], l_ref.at[k_me], l_ref.at[2 * px + py], (px, py, c))
            for px, py in _other_chips(x, y) for p_ref, l_ref in zip(ins, lands)]


def _scatter_start(name, parts):
    lands = [lax.empty(p.shape, p.dtype) for p in parts]
    return _split_start(name, parts, lands, len(parts) * (N_CHIPS - 1), _scatter_plan)


def _scatter_wait(name, handle, after):
    return _split_wait(name, handle, after, _scatter_plan)[1]


def _share_halves(halves):
    n = len(halves)

    def body(*refs):
        ins, outs, (ssem, rsem) = refs[:n], refs[n:2 * n], refs[2 * n:]
        x, y, c = _me()
        copies = [_remote(r_ref, o_ref, ssem.at[i], rsem.at[i], (x, y, 1 - c))
                  for i, (r_ref, o_ref) in enumerate(zip(ins, outs))]
        for cp in copies:
            cp.start()
        for cp in copies:
            cp.wait()

    return _pcall(
        body, name="share_halves", in_specs=[ANY] * n, out_specs=[ANY] * n,
        out_shape=[jax.ShapeDtypeStruct(h.shape, h.dtype) for h in halves],
        scratch_shapes=[pltpu.SemaphoreType.DMA((n,)), pltpu.SemaphoreType.DMA((n,))],
    )(*halves)


def _reduce_small(v):
    rows = v.shape[0]
    half = rows // 2
    assert half % SUBLANES == 0

    def body(v_ref, out_ref, pair_buf, mine, chip_buf, ssem, rsem):
        x, y, c = _me()
        k_me = 2 * x + y
        sib = (x, y, 1 - c)

        def rows_of(cc):
            return pl.ds(pl.multiple_of(cc * half, SUBLANES), half)

        swap = _remote(v_ref.at[rows_of(1 - c)], pair_buf, ssem.at[0], rsem.at[0], sib)
        swap.start()
        swap.wait()
        mine[...] = v_ref[rows_of(c), :] + pair_buf[...]
        chip_buf[k_me] = mine[...]
        sends = [_remote(mine, chip_buf.at[k_me], ssem.at[1 + r], rsem.at[1 + r], (px, py, c))
                 for r, (px, py) in enumerate(_other_chips(x, y))]
        for cp in sends:
            cp.start()
        for r, (px, py) in enumerate(_other_chips(x, y)):
            blk = chip_buf.at[2 * px + py]
            _remote(blk, blk, ssem.at[1 + r], rsem.at[1 + r], (px, py, c)).wait_recv()
        total = chip_buf[0]
        for k in range(1, N_CHIPS):
            total = total + chip_buf[k]
        out_ref[rows_of(c), :] = total
        for cp in sends:
            cp.wait_send()
        share = _remote(out_ref.at[rows_of(c)], out_ref.at[rows_of(c)], ssem.at[N_CHIPS], rsem.at[N_CHIPS], sib)
        share.start()
        got = out_ref.at[rows_of(1 - c)]
        _remote(got, got, ssem.at[N_CHIPS], rsem.at[N_CHIPS], sib).wait_recv()
        share.wait_send()

    vm = pl.BlockSpec(memory_space=pltpu.VMEM)
    return _pcall(
        body, pin=False, name="reduce_small", in_specs=[vm], out_specs=vm,
        out_shape=jax.ShapeDtypeStruct((rows, LANES), F32),
        scratch_shapes=[pltpu.VMEM((half, LANES), F32), pltpu.VMEM((half, LANES), F32),
                        pltpu.VMEM((N_CHIPS, half, LANES), F32), pltpu.SemaphoreType.DMA((N_CHIPS + 1,)),
                        pltpu.SemaphoreType.DMA((N_CHIPS + 1,))],
        compiler_params=pltpu.CompilerParams(vmem_limit_bytes=32 * 1024 * 1024),
    )(v)


def _add_pair(name, core, chip, g, theirs):
    _, half, cols = theirs.shape
    tr = _tile(half, (256, 176, 128))
    nb = half // tr

    def body(c_ref, k_ref, g_ref, t_ref, o32_ref, o16_ref):
        s = g_ref[...] + t_ref[...]
        o16_ref[...] = s.astype(BF16)

        @pl.when(pl.program_id(1) == k_ref[0])
        def _():
            o32_ref[...] = s

    spec = pl.BlockSpec((None, tr, cols), lambda i, k, c_ref, k_ref: (k, i, 0))
    grid_spec = pltpu.PrefetchScalarGridSpec(
        num_scalar_prefetch=2, grid=(nb, N_CHIPS),
        in_specs=[pl.BlockSpec((None, tr, cols), lambda i, k, c_ref, k_ref: (k, c_ref[0] * nb + i, 0)), spec],
        out_specs=[pl.BlockSpec((tr, cols), lambda i, k, c_ref, k_ref: (i, 0)), spec])
    return _pcall(
        body, name=name, grid_spec=grid_spec,
        out_shape=[jax.ShapeDtypeStruct((half, cols), F32), jax.ShapeDtypeStruct(theirs.shape, BF16)],
        compiler_params=_params(("arbitrary", "arbitrary"), 8 * _nbytes((tr, cols + LANES), F32)),
    )(core, chip, g, theirs)


def _add_chips(name, chip, p32, recv):
    half, cols = p32.shape
    tr = _tile(half, (256, 176, 128))

    def body(k_ref, p_ref, r0_ref, r1_ref, r2_ref, o_ref):
        o_ref[...] = ((p_ref[...] + r0_ref[...].astype(F32)) + r1_ref[...].astype(F32)) + r2_ref[...].astype(F32)

    def other(r):
        return pl.BlockSpec((None, tr, cols), lambda i, k_ref: (r + (k_ref[0] <= r).astype(jnp.int32), i, 0))
    grid_spec = pltpu.PrefetchScalarGridSpec(
        num_scalar_prefetch=1, grid=(half // tr,),
        in_specs=[pl.BlockSpec((tr, cols), lambda i, k_ref: (i, 0)), other(0), other(1), other(2)],
        out_specs=pl.BlockSpec((tr, cols), lambda i, k_ref: (i, 0)))
    return _pcall(
        body, name=name, grid_spec=grid_spec, out_shape=jax.ShapeDtypeStruct((half, cols), F32),
        compiler_params=_params(("arbitrary",), 10 * _nbytes((tr, cols + LANES), F32)),
    )(chip, p32, recv, recv, recv)


def kernel(x, mem, w_in, b_in, hg_lb_logits, hg_norm_w, ml_conv_w, ml_conv_b, ml_norm_w, w_out, ln1_g, ln1_b, ca_wq, ca_wkv, ca_wo, ln2_g, ln2_b, ffn_w_up, ffn_conv_w, ffn_conv_b, ffn_w_down, ln3_g, ln3_b, loss_target, m_w_in, m_b_in, m_hg_lb_logits, m_hg_norm_w, m_ml_conv_w, m_ml_conv_b, m_ml_norm_w, m_w_out, m_ln1_g, m_ln1_b, m_ca_wq, m_ca_wkv, m_ca_wo, m_ln2_g, m_ln2_b, m_ffn_w_up, m_ffn_conv_w, m_ffn_conv_b, m_ffn_w_down, m_ln3_g, m_ln3_b, v_w_in, v_b_in, v_hg_lb_logits, v_hg_norm_w, v_ml_conv_w, v_ml_conv_b, v_ml_norm_w, v_w_out, v_ln1_g, v_ln1_b, v_ca_wq, v_ca_wkv, v_ca_wo, v_ln2_g, v_ln2_b, v_ffn_w_up, v_ffn_conv_w, v_ffn_conv_b, v_ffn_w_down, v_ln3_g, v_ln3_b):
    return _train_step(dict(locals()))


WEIGHTS = ("w_in", "b_in", "hg_lb_logits", "hg_norm_w", "ml_conv_w", "ml_conv_b", "ml_norm_w", "w_out", "ln1_g",
           "ln1_b", "ca_wq", "ca_wkv", "ca_wo", "ln2_g", "ln2_b", "ffn_w_up", "ffn_conv_w", "ffn_conv_b",
           "ffn_w_down", "ln3_g", "ln3_b")
MATRICES = ("w_in", "w_out", "ca_wq", "ca_wkv", "ca_wo", "ffn_w_up", "ffn_w_down")
COL_SHARDED = ("w_in", "ca_wkv", "ffn_w_up", "ml_conv_w", "ffn_conv_w")
SMALL = tuple(n for n in WEIGHTS if n not in MATRICES)
PART_ROWS = 16


def _part_rows(shape):
    n = 1
    for s in shape:
        n *= s
    return -(-n // (LANES * PART_ROWS)) * PART_ROWS


def _pack(arrs, dtype):
    parts = []
    for a in arrs:
        flat = a.reshape(-1).astype(dtype)
        flat = jnp.pad(flat, (0, _part_rows(a.shape) * LANES - flat.shape[0]))
        parts.append(flat.reshape(-1, LANES))
    return jnp.concatenate(parts, axis=0)


def _unpack(buf, shapes):
    lead = buf.shape[:-2]
    outs, r = [], 0
    for sh in shapes:
        n = 1
        for s in sh:
            n *= s
        nr = _part_rows(sh)
        flat = buf[..., r:r + nr, :].reshape(lead + (nr * LANES,))
        outs.append(flat[..., :n].reshape(lead + tuple(sh)))
        r += nr
    return outs


def _cat_cols(s):
    return jnp.moveaxis(s, 0, 1).reshape(s.shape[1], -1)


def _stack_rows(s):
    return s.reshape(-1, s.shape[-1])


def _train_step(a):
    xs, mems, tgt = a["x"][0], a["mem"][0], a["loss_target"][0]
    core = lax.axis_index("c").astype(jnp.int32).reshape(1)
    chip = (2 * lax.axis_index("x") + lax.axis_index("y")).astype(jnp.int32).reshape(1)
    k_me = chip[0]
    shard = {n: a[n][0] for n in MATRICES}

    later = [n for n in MATRICES if n != "w_in"]
    w_in, taps = _gather_weights([shard["w_in"].astype(BF16), _pack([a["ml_conv_w"][0], a["ffn_conv_w"][0]], F32)])
    w = {"w_in": jnp.pad(_cat_cols(w_in), ((0, 0), (0, D_IN_PAD - D_IN)))}
    gathering, token = _gather_start("gather_start", [shard[n].astype(BF16) for n in later])
    ml_cw, ffn_cw = [_cat_cols(s) for s in _unpack(taps, [a["ml_conv_w"].shape[1:], a["ffn_conv_w"].shape[1:]])]
    b_in_p = jnp.pad(a["b_in"], ((0, 0), (0, D_IN_PAD - D_IN))) + token[0:1, 0:1]
    mixer_w = (a["hg_lb_logits"], a["hg_norm_w"], ml_cw, a["ml_conv_b"], a["ml_norm_w"])
    up_cols = a["ffn_w_up"].shape[-1]

    proj, xb = _mm("proj", "nn", xs, w["w_in"], bias=b_in_p, a_copy_dtype=BF16, tm=256, tn=D_IN_PAD)
    y, hst, cst, nst, mst = _mixer_fwd(proj, *mixer_w)
    w.update(zip(later, _forward_halves("forward_halves", _gather_wait("gather_wait", gathering, y))))
    for n in ("w_out", "ca_wq", "ca_wo", "ffn_w_down"):
        w[n] = _stack_rows(w[n])
    z1, x1, x1b = _mm("mix_out", "nn", y, w["w_out"], res=xs, res_scale=ALPHA, ln=("fwd", a["ln1_g"], a["ln1_b"]),
                      copy_dtype=BF16)
    q = _mm("ca_q", "nn", x1b, w["ca_wq"], out_dtype=BF16, tn=D_MODEL)
    kv = _mm("ca_kv", "nn", mems, w["ca_wkv"])
    o = _attn_fwd(q, kv)
    z2, x2, x2b = _mm("ca_out", "nn", o, w["ca_wo"], res=x1, res_scale=ALPHA, ln=("fwd", a["ln2_g"], a["ln2_b"]),
                      copy_dtype=BF16)
    w_up = w["ffn_w_up"]
    assert w_up.shape == (2 * FFN_J, D_MODEL, FFN_W)
    u, hmid, dz3, g_ln3g, g_ln3b, loss_part, dz3b = _ffn_fwd(
        x2b, x2, w_up, ffn_cw, a["ffn_conv_b"], w["ffn_w_down"], a["ln3_g"], a["ln3_b"], tgt)

    grads = {"ln3_g": g_ln3g, "ln3_b": g_ln3b}
    grads["ffn_w_down"] = _mm("g_w_down", "tn", hmid, dz3b, tm=D_FF // 2, tn=D_MODEL)
    du, g_cw, g_cb, dz2, grads["ln2_g"], grads["ln2_b"], dz2b = _ffn_bwd(
        u, ffn_cw, a["ffn_conv_b"], dz3b, dz3, w["ffn_w_down"], w_up, z2, a["ln2_g"], a["ln2_b"])
    grads["ffn_conv_w"] = jnp.transpose(g_cw, (2, 1, 0, 3)).reshape(FFN_CONV, 2 * D_FF)
    grads["ffn_conv_b"] = jnp.transpose(g_cb, (2, 1, 0, 3)).reshape(1, 2 * D_FF)
    grads["ffn_w_up"] = _mm("g_w_up", "tn", x2b, du, out_groups=N_CHIPS, tm=D_MODEL, tn=up_cols)
    grads["ffn_w_down"] = grads["ffn_w_down"].reshape((N_CHIPS,) + shard["ffn_w_down"].shape)
    pending = {}

    def reduce_start(tag, names, swapped=None):
        group = [grads[n] for n in names]
        group, theirs = swapped or (group, _swap_halves("swap_halves_" + tag, group))
        sums = [_add_pair("add_pair_" + n, core, chip, g, t) for n, g, t in zip(names, group, theirs)]
        handle, token = _scatter_start("scatter_start_" + tag, [s16 for _, s16 in sums])
        pending[tag] = (names, [s32 for s32, _ in sums], handle)
        return token[0:1, 0:1]

    ffn = ("ffn_w_up", "ffn_w_down")
    swapping, token = _swap_start("swap_start_ffn", [grads[n] for n in ffn])
    do = _mm("d_o", "nt", dz2b, w["ca_wo"], bias=jnp.zeros((1, D_MODEL), F32) + token[0:1, 0:1], out_dtype=BF16,
             tn=D_MODEL)
    grads["ca_wo"] = _mm("g_wo", "tn", o, dz2b, tm=D_MODEL, tn=D_MODEL)
    zero = reduce_start("ffn", ffn, _swap_wait("swap_wait_ffn", swapping, grads["ca_wo"]))
    dq, dkv = _attn_bwd(q, kv + zero, do)
    grads["ca_wq"] = _mm("g_wq", "tn", x1b, dq, tm=D_MODEL, tn=D_MODEL)
    grads["ca_wkv"] = _mm("g_wkv", "tn", mems, dkv, out_groups=N_CHIPS, tm=D_MODEL)
    dz1, grads["ln1_g"], grads["ln1_b"], dz1b = _mm("d_x1", "nt", dq, w["ca_wq"], res=dz2, res_scale=ALPHA,
                                                    ln=("bwd", z1, a["ln1_g"], a["ln1_b"]), copy_dtype=BF16)
    grads["w_out"] = _mm("g_w_out", "tn", y, dz1b, tm=D_MODEL, tn=D_MODEL)
    for n in ("w_out", "ca_wq", "ca_wo"):
        grads[n] = grads[n].reshape((N_CHIPS,) + shard[n].shape)
    attn = ("w_out", "ca_wq", "ca_wkv", "ca_wo")
    swapping, token = _swap_start("swap_start_attn", [grads[n] for n in attn])
    dy = _mm("d_y", "nt", dz1b, w["w_out"], bias=jnp.zeros((1, D_MODEL), F32) + token[0:1, 0:1], tn=D_MODEL)
    zero = reduce_start("attn", attn, _swap_wait("swap_wait_attn", swapping, dy))
    (dproj, g_b_in, grads["hg_lb_logits"], grads["hg_norm_w"], grads["ml_conv_w"], grads["ml_conv_b"],
     grads["ml_norm_w"]) = _mixer_bwd(proj, dy, hst, cst, nst, mst, mixer_w[0], mixer_w[1] + zero, *mixer_w[2:])
    g_in = _mm("g_w_in", "tn", xb, dproj, tm=D_MODEL, tn=up_cols)[:, :D_IN]
    grads["w_in"] = jnp.moveaxis(g_in.reshape(D_MODEL, N_CHIPS, -1), 1, 0)
    grads["b_in"] = g_b_in[:, :D_IN]
    zero = reduce_start("in", ("w_in",))
    dx = _mm("d_x", "nt", dproj, w["w_in"], bias=jnp.zeros((1, D_MODEL), F32) + zero, res=dz1, res_scale=ALPHA,
             tm=256, tn=D_MODEL)

    halves = {}
    for tag, (names, sums32, handle) in pending.items():
        for n, s32, r in zip(names, sums32, _scatter_wait("scatter_wait_" + tag, handle, dx)):
            halves[n] = _add_chips("add_chips_" + n, chip, s32, r)
    halves = [halves[n] for n in MATRICES]
    other_halves = _share_halves(halves)

    small_shapes = [grads[n].shape for n in SMALL] + [loss_part.shape]
    summed = _unpack(_reduce_small(_pack([grads[n] for n in SMALL] + [loss_part], F32)), small_shapes)
    loss = summed[-1][0, 0]
    for n, g in zip(SMALL, summed[:-1]):
        if n in COL_SHARDED:
            cols = a[n].shape[-1]
            g = lax.dynamic_slice_in_dim(g, k_me * cols, cols, axis=1)
        grads[n] = g

    delta, new_m, new_v = {}, {}, {}
    for n, mine, theirs in zip(MATRICES, halves, other_halves):
        grads[n], delta[n], new_m[n], new_v[n] = _adamw_halves(
            "adamw_" + n, core, shard[n], mine, theirs, a["m_" + n][0], a["v_" + n][0])
    small_w = [a[n][0] if a[n].ndim == 3 else a[n] for n in SMALL]
    small_m = [a["m_" + n][0] if a[n].ndim == 3 else a["m_" + n] for n in SMALL]
    small_v = [a["v_" + n][0] if a[n].ndim == 3 else a["v_" + n] for n in SMALL]
    for out, vals in zip((delta, new_m, new_v),
                         _adamw_many("adamw_small", small_w, [grads[n] for n in SMALL], small_m, small_v)):
        out.update(zip(SMALL, vals))

    def shaped(d):
        return [d[n].reshape(a[n].shape) for n in WEIGHTS]
    return (loss, dx[None], *shaped(grads), *shaped(delta), *shaped(new_m), *shaped(new_v))
```

```python
import functools

import jax
import jax.numpy as jnp
from jax import lax
from jax.experimental import pallas as pl
from jax.experimental.pallas import tpu as pltpu

F32 = jnp.float32
BF16 = jnp.bfloat16

D_MODEL = 1024
HEADS = 4
DK = 128
D_GRP = HEADS * DK
CHUNK = 64
ML_CONV = 4
FFN_CONV = 3
D_FF = 2816
CA_DH = D_MODEL // HEADS
DEPTH = 1
ALPHA = (2.0 * DEPTH) ** 0.25
LN_EPS = 1e-5
NEG_BIG = -1e30
D_IN = 8 * D_GRP + 2 * HEADS
D_IN_PAD = 8 * D_GRP + 128
ADAM_LR, ADAM_B1, ADAM_B2, ADAM_EPS, ADAM_WD, ADAM_STEP = 0.001, 0.9, 0.999, 1e-08, 0.01, 10

SUBLANES = 8
LANES = 128
VMEM_BYTES = 64 * 1024 * 1024


def _pcall(body, pin=True, **kw):
    if not pin:
        return _call(body, **kw)
    kw["out_shape"] = jax.tree.map(lambda s: pltpu.HBM(s.shape, s.dtype), kw["out_shape"])
    call = _call(body, **kw)

    def pinned(*args):
        return call(*[pltpu.with_memory_space_constraint(x, pltpu.HBM) if jnp.issubdtype(x.dtype, jnp.floating) else x
                      for x in args])
    return pinned


def _call(body, **kw):
    return pl.pallas_call(body, **kw)


def _params(semantics, vmem_bytes):
    limit = int(min(max(2 * vmem_bytes, 16 * 1024 * 1024), VMEM_BYTES - 8 * 1024 * 1024))
    return pltpu.CompilerParams(dimension_semantics=semantics, vmem_limit_bytes=limit)


def _nbytes(shape, dtype):
    n = 1
    for s in shape:
        n *= s
    return n * jnp.dtype(dtype).itemsize


def _dg(a, b, ca, cb):
    return lax.dot_general(a.astype(BF16), b.astype(BF16), (((ca,), (cb,)), ((), ())),
                           preferred_element_type=F32)


@jax.custom_vjp
def mm_nn(a, b):
    return _dg(a, b, 1, 0)


mm_nn.defvjp(lambda a, b: (_dg(a, b, 1, 0), (a, b)),
             lambda r, g: (_dg(g, r[1], 1, 1).astype(r[0].dtype), _dg(r[0], g, 0, 0).astype(r[1].dtype)))


@jax.custom_vjp
def mm_nt(a, b):
    return _dg(a, b, 1, 1)


mm_nt.defvjp(lambda a, b: (_dg(a, b, 1, 1), (a, b)),
             lambda r, g: (_dg(g, r[1], 1, 0).astype(r[0].dtype), _dg(g, r[0], 0, 0).astype(r[1].dtype)))


@jax.custom_vjp
def mm_tn(a, b):
    return _dg(a, b, 0, 0)


mm_tn.defvjp(lambda a, b: (_dg(a, b, 0, 0), (a, b)),
             lambda r, g: (_dg(r[1], g, 1, 1).astype(r[0].dtype), _dg(r[0], g, 1, 0).astype(r[1].dtype)))


def _tri(n, lower):
    r = lax.broadcasted_iota(jnp.int32, (n, n), 0)
    c = lax.broadcasted_iota(jnp.int32, (n, n), 1)
    return ((r >= c) if lower else (r <= c)).astype(F32)


def _tri_dot(lower, x):
    t = _tri(x.shape[0], lower).astype(BF16)
    hi = x.astype(BF16)
    rest = x - hi.astype(F32)
    mid = rest.astype(BF16)
    lo = (rest - mid.astype(F32)).astype(BF16)
    return sum(lax.dot_general(t, p, (((1,), (0,)), ((), ())), preferred_element_type=F32) for p in (hi, mid, lo))


@jax.custom_vjp
def cumsum_rows(x):
    return _tri_dot(True, x)


cumsum_rows.defvjp(lambda x: (_tri_dot(True, x), None), lambda _, g: (_tri_dot(False, g),))


def _shift_impl(halo, x, d):
    xx = jnp.concatenate([halo, x], axis=0)
    return pltpu.roll(xx, d, 0)[SUBLANES:]


@functools.partial(jax.custom_vjp, nondiff_argnums=(2,))
def shift_rows(halo, x, d):
    return _shift_impl(halo, x, d)


def _shift_bwd(d, _, g):
    n = g.shape[0] + SUBLANES
    gg = jnp.concatenate([jnp.zeros((SUBLANES, g.shape[1]), g.dtype), g], axis=0)
    r = pltpu.roll(gg, n - d, 0)
    return r[:SUBLANES], r[SUBLANES:]


shift_rows.defvjp(lambda halo, x, d: (_shift_impl(halo, x, d), None), _shift_bwd)


def causal_conv(halo, x, w_rows, b):
    k = len(w_rows)
    y = b + w_rows[k - 1] * x
    for d in range(1, k):
        y = y + w_rows[k - 1 - d] * shift_rows(halo, x, d)
    return y


def _sigmoid(x):
    return 1.0 / (1.0 + jnp.exp(-x))


def _silu(x):
    return x * _sigmoid(x)


def _log_sigmoid(x):
    return jnp.minimum(x, 0.0) - jnp.log(1.0 + jnp.exp(-jnp.abs(x)))


def _pick_row(x, i):
    row = lax.broadcasted_iota(jnp.int32, (x.shape[0], 1), 0)
    return jnp.sum(jnp.where(row == i, x, 0.0), axis=0, keepdims=True)


def _layer_norm(z, g, b):
    mu = jnp.mean(z, axis=-1, keepdims=True)
    zc = z - mu
    var = jnp.mean(zc * zc, axis=-1, keepdims=True)
    return zc * lax.rsqrt(var + LN_EPS) * g + b


def _qk_conv(halo, x, w0, w1, w2, w3, b):
    return _silu(causal_conv(halo, x, (w0, w1, w2, w3), b))


def _grp(i):
    return pl.ds(i * D_GRP, D_GRP)


def _mixer_specs(n_chunks, reverse):
    def chunk(c):
        return n_chunks - 1 - c if reverse else c
    row8 = CHUNK // SUBLANES
    proj_spec = pl.BlockSpec((CHUNK, D_IN_PAD), lambda c: (chunk(c), 0))
    halo_spec = pl.BlockSpec((SUBLANES, 2 * D_GRP), lambda c: (jnp.maximum(chunk(c) * row8 - 1, 0), 2))
    small = [pl.BlockSpec((2, D_GRP), lambda c: (0, 0)), pl.BlockSpec((1, D_GRP), lambda c: (0, 0)),
             pl.BlockSpec((ML_CONV, 2 * D_GRP), lambda c: (0, 0)), pl.BlockSpec((1, 2 * D_GRP), lambda c: (0, 0)),
             pl.BlockSpec((1, D_GRP), lambda c: (0, 0))]
    state_specs = [pl.BlockSpec((1, HEADS, DK, DK), lambda c: (chunk(c), 0, 0, 0)),
                   pl.BlockSpec((1, HEADS, DK, DK), lambda c: (chunk(c), 0, 0, 0)),
                   pl.BlockSpec((1, HEADS, 1, DK), lambda c: (chunk(c), 0, 0, 0)),
                   pl.BlockSpec((1, HEADS, 1, DK), lambda c: (chunk(c), 0, 0, 0))]
    y_spec = pl.BlockSpec((CHUNK, 2 * D_GRP), lambda c: (chunk(c), 0))
    return proj_spec, halo_spec, small, state_specs, y_spec, chunk


def _heads(x):
    return [x[:, h * DK:(h + 1) * DK] for h in range(HEADS)]


def _last(x, j):
    lane = lax.broadcasted_iota(jnp.int32, (1, x.shape[-1]), 1)
    return jnp.sum(jnp.where(lane == j, x, 0.0), axis=-1, keepdims=True)


def _hg_chunk(st_t, hq, hf, hi, hgate, l0, l1, nw):
    n = hq.shape[0]
    lb = _sigmoid(l0 - l1)
    q = _silu(hq)
    lf = jnp.log(lb + (1.0 - lb) * _sigmoid(hf))
    k = (1.0 - lb) * _sigmoid(-hf)
    b = cumsum_rows(lf)
    b_ref = _pick_row(b, n // 2 - 1)
    b_last = _pick_row(b, n - 1)
    qa, ka =_heads(q * jnp.exp(b - b_ref)), _heads(k * jnp.exp(b_ref - b))
    qe, kd, eb, v = _heads(q * jnp.exp(b)), _heads(k * jnp.exp(b_last - b)), _heads(jnp.exp(b_last)), _heads(hi)
    tri = _tri(n, True) > 0
    attn = [jnp.where(tri, mm_nt(qa[h], ka[h]), 0.0) for h in range(HEADS)]
    o = [mm_nn(attn[h], v[h]) + mm_nt(qe[h], st_t[h]) for h in range(HEADS)]
    st_new = jnp.stack([eb[h] * st_t[h] + mm_tn(v[h], kd[h]) for h in range(HEADS)])
    yn = [o[h] * lax.rsqrt(jnp.mean(o[h] * o[h], axis=-1, keepdims=True) + LN_EPS) for h in range(HEADS)]
    return st_new, jnp.concatenate(yn, axis=1) * nw * _silu(hgate)


def _ml_chunk(c_st, n_st, m_st, q, k, v, gates, og, nw):
    n = q.shape[0]
    ig = jnp.stack([_last(gates, h) for h in range(HEADS)])
    log_f = _log_sigmoid(gates)
    fl = jnp.stack([_last(log_f, HEADS + h) for h in range(HEADS)])
    bw = cumsum_rows(jnp.concatenate([jnp.broadcast_to(fl[h], (n, DK)) for h in range(HEADS)], axis=1))
    b = jnp.stack([_last(x, 0) for x in _heads(bw)])
    g = jnp.sum(fl, axis=1, keepdims=True)
    eye = lax.broadcasted_iota(jnp.int32, (n, n), 0) == lax.broadcasted_iota(jnp.int32, (n, n), 1)
    e_row = jnp.sum(jnp.where(eye, ig - b, 0.0), axis=1, keepdims=True)
    d = jnp.where(_tri(n, True) > 0, b + e_row, -jnp.inf)
    inter = b + m_st
    m_t = jnp.maximum(inter, jnp.max(d, axis=2, keepdims=True))
    qs, kh, vh = _heads(q * (DK ** -0.5)), _heads(k), _heads(v)
    s = jnp.stack([mm_nt(qs[h], kh[h]) for h in range(HEADS)]) * jnp.exp(d - m_t)
    w_inter = jnp.exp(inter - m_t)
    num = (jnp.stack([mm_nn(s[h], vh[h]) for h in range(HEADS)])
           + w_inter * jnp.stack([mm_nn(qs[h], c_st[h]) for h in range(HEADS)]))
    den = jnp.sum(s, axis=2, keepdims=True) + w_inter * jnp.sum(jnp.stack(qs) * n_st, axis=2, keepdims=True)
    h_out = num / jnp.maximum(jnp.abs(den), jnp.exp(-m_t))
    a = g - b + ig
    m_new = jnp.maximum(g + m_st, jnp.max(a, axis=1, keepdims=True))
    decay = jnp.exp(g + m_st - m_new)
    wk = jnp.stack(kh) * jnp.exp(a - m_new)
    c_new = decay * c_st + jnp.stack([mm_tn(wk[h], vh[h]) for h in range(HEADS)])
    n_new = decay * n_st + jnp.sum(wk, axis=1, keepdims=True)
    hc = h_out - jnp.mean(h_out, axis=-1, keepdims=True)
    yn = hc * lax.rsqrt(jnp.mean(hc * hc, axis=-1, keepdims=True) + LN_EPS)
    y = _sigmoid(og) * (jnp.concatenate([yn[h] for h in range(HEADS)], axis=1) * nw)
    return c_new, n_new, m_new, y


def _mixer_inputs(proj_ref, lg_ref, hnw_ref, mnw_ref, qk):
    hg_in = (proj_ref[:, _grp(0)], proj_ref[:, _grp(1)], proj_ref[:, _grp(2)], proj_ref[:, _grp(3)],
             lg_ref[0:1, :], lg_ref[1:2, :], hnw_ref[...])
    ml_in = (qk[:, :D_GRP], qk[:, D_GRP:], proj_ref[:, _grp(6)], proj_ref[:, pl.ds(8 * D_GRP, LANES)],
             proj_ref[:, _grp(7)], mnw_ref[...])
    return hg_in, ml_in


def _mixer_fwd(proj, lb_logits, hg_nw, conv_w, conv_b, ml_nw):
    seq = proj.shape[0]
    n_chunks = seq // CHUNK
    proj_spec, halo_spec, small, state_specs, y_spec, _ = _mixer_specs(n_chunks, False)

    def body(proj_ref, halo_ref, lg_ref, hnw_ref, cw_ref, cb_ref, mnw_ref,
             y_ref, hst_ref, cst_ref, nst_ref, mst_ref, hs, cs, ns, ms):
        c = pl.program_id(0)

        @pl.when(c == 0)
        def _():
            hs[...] = jnp.zeros_like(hs)
            cs[...] = jnp.zeros_like(cs)
            ns[...] = jnp.zeros_like(ns)
            ms[...] = jnp.full(ms.shape, NEG_BIG, F32)

        hst_ref[0] = hs[...]
        cst_ref[0] = cs[...]
        nst_ref[0] = ns[...]
        mst_ref[0] = ms[...]
        halo = jnp.where(c > 0, halo_ref[...], 0.0)
        qk = _qk_conv(halo, proj_ref[:, pl.ds(4 * D_GRP, 2 * D_GRP)],
                      cw_ref[0:1, :], cw_ref[1:2, :], cw_ref[2:3, :], cw_ref[3:4, :], cb_ref[...])
        hg_in, ml_in = _mixer_inputs(proj_ref, lg_ref, hnw_ref, mnw_ref, qk)
        hs[...], y_hg = _hg_chunk(hs[...], *hg_in)
        cs[...], ns[...], m_new, y_ml = _ml_chunk(cs[...], ns[...], _last(ms[...], 0), *ml_in)
        ms[...] = jnp.broadcast_to(m_new, ms.shape)
        y_ref[:, pl.ds(0, D_GRP)] = y_hg.astype(BF16)
        y_ref[:, pl.ds(D_GRP, D_GRP)] = y_ml.astype(BF16)

    st = jax.ShapeDtypeStruct((n_chunks, HEADS, DK, DK), F32)
    vec = jax.ShapeDtypeStruct((n_chunks, HEADS, 1, DK), F32)
    vmem = 2 * (_nbytes((CHUNK, D_IN_PAD), F32) + _nbytes((CHUNK, 2 * D_GRP), F32) + 2 * _nbytes((HEADS, DK, DK), F32)) \
        + 2 * _nbytes((HEADS, DK, DK), F32)
    return _pcall(
        body, name="mixer_fwd", grid=(n_chunks,),
        in_specs=[proj_spec, halo_spec] + small,
        out_specs=[y_spec] + state_specs,
        out_shape=[jax.ShapeDtypeStruct((seq, 2 * D_GRP), BF16), st, st, vec, vec],
        scratch_shapes=[pltpu.VMEM((HEADS, DK, DK), F32), pltpu.VMEM((HEADS, DK, DK), F32),
                        pltpu.VMEM((HEADS, 1, DK), F32), pltpu.VMEM((HEADS, 1, DK), F32)],
        compiler_params=_params(("arbitrary",), vmem),
    )(proj, proj, lb_logits, hg_nw, conv_w, conv_b, ml_nw)


def _mixer_bwd(proj, dy, hst, cst, nst, mst, lb_logits, hg_nw, conv_w, conv_b, ml_nw):
    seq = proj.shape[0]
    n_chunks = seq // CHUNK
    proj_spec, halo_spec, small, state_specs, y_spec, _ = _mixer_specs(n_chunks, True)

    def body(proj_ref, halo_ref, dy_ref, hst_ref, cst_ref, nst_ref, mst_ref,
             lg_ref, hnw_ref, cw_ref, cb_ref, mnw_ref,
             dproj_ref, dbin_ref, dlg_ref, dhnw_ref, dcw_ref, dcb_ref, dmnw_ref,
             dhs, dcs, dns, dms, dhalo):
        c = pl.program_id(0)

        @pl.when(c == 0)
        def _():
            for r in (dhs, dcs, dns, dms, dhalo, dbin_ref, dlg_ref, dhnw_ref, dcw_ref, dcb_ref, dmnw_ref):
                r[...] = jnp.zeros_like(r)

        def put(cols, val):
            dproj_ref[:, cols] = val.astype(BF16)
            dbin_ref[:, cols] += jnp.sum(val, axis=0, keepdims=True)

        first = c == n_chunks - 1
        halo = jnp.where(first, 0.0, halo_ref[...])
        x_qk = proj_ref[:, pl.ds(4 * D_GRP, 2 * D_GRP)]
        conv_args = (halo, x_qk, cw_ref[0:1, :], cw_ref[1:2, :], cw_ref[2:3, :], cw_ref[3:4, :], cb_ref[...])
        qk, conv_vjp = jax.vjp(_qk_conv, *conv_args)
        hg_in, ml_in = _mixer_inputs(proj_ref, lg_ref, hnw_ref, mnw_ref, qk)
        _, hg_vjp = jax.vjp(_hg_chunk, hst_ref[0], *hg_in)
        _, ml_vjp = jax.vjp(_ml_chunk, cst_ref[0], nst_ref[0], _last(mst_ref[0], 0), *ml_in)
        dst, dhq, dhf, dhi, dhg, dl0, dl1, dnw = hg_vjp((dhs[...], dy_ref[:, pl.ds(0, D_GRP)]))
        dc, dn, dm, dq, dk, dv, dgates, dog, dmn = ml_vjp(
            (dcs[...], dns[...], _last(dms[...], 0), dy_ref[:, pl.ds(D_GRP, D_GRP)]))
        dhs[...] = dst
        dcs[...] = dc
        dns[...] = dn
        dms[...] = jnp.broadcast_to(dm, dms.shape)
        for i, val in ((0, dhq), (1, dhf), (2, dhi), (3, dhg), (6, dv), (7, dog)):
            put(_grp(i), val)
        put(pl.ds(8 * D_GRP, LANES), dgates)
        dlg_ref[0:1, :] += dl0
        dlg_ref[1:2, :] += dl1
        dhnw_ref[...] += dnw
        dmnw_ref[...] += dmn
        dh, dx, dw0, dw1, dw2, dw3, db = conv_vjp(jnp.concatenate([dq, dk], axis=1))
        tail = jnp.concatenate([jnp.zeros((CHUNK - SUBLANES, 2 * D_GRP), F32), dhalo[...]], axis=0)
        put(pl.ds(4 * D_GRP, 2 * D_GRP), dx + tail)
        dhalo[...] = dh
        for d, dw in enumerate((dw0, dw1, dw2, dw3)):
            dcw_ref[d:d + 1, :] += dw
        dcb_ref[...] += db

    row = pl.BlockSpec((1, D_GRP), lambda c: (0, 0))
    small_out = [pl.BlockSpec((1, D_IN_PAD), lambda c: (0, 0)), pl.BlockSpec((2, D_GRP), lambda c: (0, 0)), row,
                 pl.BlockSpec((ML_CONV, 2 * D_GRP), lambda c: (0, 0)), pl.BlockSpec((1, 2 * D_GRP), lambda c: (0, 0)), row]
    dy_spec = pl.BlockSpec((CHUNK, 2 * D_GRP), y_spec.index_map)
    vmem = 2 * (2 * _nbytes((CHUNK, D_IN_PAD), F32) + _nbytes((CHUNK, 2 * D_GRP), F32)
                + 2 * _nbytes((HEADS, DK, DK), F32)) + 2 * _nbytes((HEADS, DK, DK), F32) + 4 * 1024 * 1024
    return _pcall(
        body, name="mixer_bwd", grid=(n_chunks,),
        in_specs=[proj_spec, halo_spec, dy_spec] + state_specs + small,
        out_specs=[proj_spec] + small_out,
        out_shape=[jax.ShapeDtypeStruct((seq, D_IN_PAD), BF16), jax.ShapeDtypeStruct((1, D_IN_PAD), F32),
                   jax.ShapeDtypeStruct((2, D_GRP), F32), jax.ShapeDtypeStruct((1, D_GRP), F32),
                   jax.ShapeDtypeStruct((ML_CONV, 2 * D_GRP), F32), jax.ShapeDtypeStruct((1, 2 * D_GRP), F32),
                   jax.ShapeDtypeStruct((1, D_GRP), F32)],
        scratch_shapes=[pltpu.VMEM((HEADS, DK, DK), F32), pltpu.VMEM((HEADS, DK, DK), F32),
                        pltpu.VMEM((HEADS, 1, DK), F32), pltpu.VMEM((HEADS, 1, DK), F32),
                        pltpu.VMEM((SUBLANES, 2 * D_GRP), F32)],
        compiler_params=_params(("arbitrary",), vmem),
    )(proj, proj, dy, hst, cst, nst, mst, lb_logits, hg_nw, conv_w, conv_b, ml_nw)


def _tile(n, prefs, unit=None):
    unit = unit or n
    for p in prefs:
        if unit % p == 0 and n % p == 0:
            return p
    return unit


def _logical(arr):
    return arr.shape if arr.ndim == 2 else (arr.shape[1], arr.shape[0] * arr.shape[2])


def _group(arr):
    return arr.shape[-1]


def _split_spec(ndim, group, tr, tc, where):
    if ndim == 2:
        return pl.BlockSpec((tr, tc), where)
    per = group // tc
    assert per * tc == group, (group, tc)

    def index(*ids):
        bi, bj = where(*ids)
        return (bj // per, bi, bj % per)
    return pl.BlockSpec((None, tr, tc), index)


def _mm(name, mode, a, b, *, bias=None, res=None, res_scale=1.0, ln=None, out_dtype=F32, out_groups=None,
        copy_dtype=None, a_copy_dtype=None, tm=None, tn=None, tk=None):
    la, lb = _logical(a), _logical(b)
    if mode == "nn":
        (m, k), n = la, lb[1]
        n_unit = _group(b) if b.ndim == 3 else n
        kc = _group(a) if a.ndim == 3 else k
    elif mode == "nt":
        (m, k), n = la, lb[0]
        n_unit = n
        kc = min(_group(a) if a.ndim == 3 else k, _group(b) if b.ndim == 3 else k)
    else:
        (k, m), n = la, lb[1]
        n_unit, kc = (_group(b) if b.ndim == 3 else n), k
        assert a.ndim == 2
    if out_groups:
        n_unit = min(n_unit, n // out_groups)
    kind = ln[0] if ln else None
    tm = tm or (256 if ln else _tile(m, (512, 256, 128)))
    tn = n if ln else (tn or _tile(n, (512, 384, 256, 128), n_unit))
    tk = (tk or _tile(k, (4096, 2048, 512, 256, 128))) if mode == "tn" else k
    gi, gj, gk = m // tm, n // tn, k // tk
    assert gi * tm == m and gj * tn == n and gk * tk == k and n_unit % tn == 0, (name, m, n, k, tm, tn, tk)
    ca, cb = {"nn": (1, 0), "nt": (1, 1), "tn": (0, 0)}[mode]
    i_outer = gk > 1 or (gi - 1) * _nbytes(b.shape, b.dtype) <= (gj - 1) * _nbytes(a.shape, a.dtype)

    def ij(where):
        return (lambda p, q, kk: where(p, q, kk)) if i_outer else (lambda p, q, kk: where(q, p, kk))
    if mode == "tn":
        a_spec = pl.BlockSpec((tk, tm), ij(lambda i, j, kk: (kk, i)))
    elif a.ndim == 3:
        a_spec = pl.BlockSpec((a.shape[0], tm, _group(a)), ij(lambda i, j, kk: (0, i, 0)))
    else:
        a_spec = pl.BlockSpec((tm, k), ij(lambda i, j, kk: (i, 0)))
    if mode != "nt":
        b_spec = _split_spec(b.ndim, _group(b), tk, tn, ij(lambda i, j, kk: (kk, j)))
    elif b.ndim == 3:
        b_spec = pl.BlockSpec((b.shape[0], tn, _group(b)), ij(lambda i, j, kk: (0, j, 0)))
    else:
        b_spec = pl.BlockSpec((tn, k), ij(lambda i, j, kk: (j, 0)))
    row_spec = pl.BlockSpec((1, tn), ij(lambda i, j, kk: (0, j)))
    blk_spec = pl.BlockSpec((tm, tn), ij(lambda i, j, kk: (i, j)))
    ins, in_specs = [a, b], [a_spec, b_spec]
    if bias is not None:
        ins.append(bias), in_specs.append(row_spec)
    if res is not None:
        ins.append(res), in_specs.append(blk_spec)
    if kind == "fwd":
        ins += [ln[1], ln[2]]
        in_specs += [row_spec, row_spec]
    elif kind == "bwd":
        ins += [ln[1], ln[2], ln[3]]
        in_specs += [blk_spec, row_spec, row_spec]
    if out_groups:
        blk_out = jax.ShapeDtypeStruct((out_groups, m, n // out_groups), out_dtype)
        out_spec = _split_spec(3, n // out_groups, tm, tn, ij(lambda i, j, kk: (i, j)))
    else:
        blk_out, out_spec = jax.ShapeDtypeStruct((m, n), out_dtype), blk_spec
    row_out = jax.ShapeDtypeStruct((1, n), F32)
    if kind is None:
        out_shape, out_specs = [blk_out], [out_spec]
    elif kind == "fwd":
        out_shape, out_specs = [blk_out, blk_out], [blk_spec, blk_spec]
    else:
        out_shape, out_specs = [blk_out, row_out, row_out], [blk_spec, row_spec, row_spec]
    if copy_dtype is not None:
        out_shape.append(jax.ShapeDtypeStruct((m, n), copy_dtype))
        out_specs.append(blk_spec)
    if a_copy_dtype is not None:
        assert mode != "tn" and a.ndim == 2 and copy_dtype is None
        out_shape.append(jax.ShapeDtypeStruct((m, k), a_copy_dtype))
        out_specs.append(a_spec)
    n_in = len(ins)

    def body(*refs):
        in_refs, out_refs, acc_ref = refs[:n_in], refs[n_in:n_in + len(out_shape)], refs[-1]
        i, kk = pl.program_id(0 if i_outer else 1), pl.program_id(2)
        a_ref, b_ref = in_refs[:2]
        extra = list(in_refs[2:])
        if a_copy_dtype is not None:
            out_refs[-1][...] = a_ref[...].astype(a_copy_dtype)

        def epilogue(acc):
            rest = list(extra)
            if bias is not None:
                acc = acc + rest.pop(0)[...]
            if res is not None:
                acc = acc + res_scale * rest.pop(0)[...]
            if kind is None:
                out_refs[0][...] = acc.astype(out_dtype)
                return
            if kind == "fwd":
                out_refs[0][...] = acc
                y = _layer_norm(acc, rest[0][...], rest[1][...])
                out_refs[1][...] = y
                if copy_dtype is not None:
                    out_refs[-1][...] = y.astype(copy_dtype)
                return
            _, vjp = jax.vjp(_layer_norm, rest[0][...], rest[1][...], rest[2][...])
            dz, dg, db = vjp(acc)
            out_refs[0][...] = dz
            out_refs[1][...] += dg
            out_refs[2][...] += db
            if copy_dtype is not None:
                out_refs[-1][...] = dz.astype(copy_dtype)

        if kind == "bwd":
            @pl.when((i == 0) & (kk == 0))
            def _():
                out_refs[1][...] = jnp.zeros_like(out_refs[1])
                out_refs[2][...] = jnp.zeros_like(out_refs[2])

        def chunk(ref, c0, last):
            if ref.ndim == 3:
                g = ref.shape[2]
                return ref[c0 // g, :, pl.ds(c0 % g, kc)]
            return ref[:, pl.ds(c0, kc)] if last else ref[pl.ds(c0, kc), :]

        if mode == "tn" or kc == k:
            prod = _dg(a_ref[...], b_ref[...], ca, cb)
        else:
            prod = None
            for c0 in range(0, k, kc):
                part = _dg(chunk(a_ref, c0, True), chunk(b_ref, c0, mode == "nt"), ca, cb)
                prod = part if prod is None else prod + part
        if gk == 1:
            epilogue(prod)
            return

        @pl.when(kk == 0)
        def _():
            acc_ref[...] = prod

        @pl.when(kk > 0)
        def _():
            acc_ref[...] += prod

        @pl.when(kk == gk - 1)
        def _():
            epilogue(acc_ref[...])

    vmem = (2 * (_nbytes((tm, tk), a.dtype) + _nbytes((tk, tn), b.dtype))
            + (2 * len(ins) + 2 * len(out_shape) + 1) * _nbytes((tm, tn), F32))
    outs = _pcall(
        body, name=name, grid=(gi, gj, gk) if i_outer else (gj, gi, gk), in_specs=in_specs, out_specs=out_specs,
        out_shape=out_shape, scratch_shapes=[pltpu.VMEM((tm, tn) if gk > 1 else (SUBLANES, LANES), F32)],
        compiler_params=_params(("arbitrary", "arbitrary", "arbitrary"), vmem),
    )(*ins)
    return outs[0] if len(out_shape) == 1 else outs


def _attn_head(q, k, v):
    sc = mm_nt(q, k) * (CA_DH ** -0.5)
    e = jnp.exp(sc - jnp.max(sc, axis=-1, keepdims=True))
    return mm_nn(e / jnp.sum(e, axis=-1, keepdims=True), v)


def _attn_fwd(q, kv):
    seq, n_mem = q.shape[0], kv.shape[0]
    tq = _tile(seq, (512, 256, 128))

    def body(q_ref, kv_ref, o_ref):
        for h in range(HEADS):
            hd = pl.ds(h * CA_DH, CA_DH)
            o = _attn_head(q_ref[:, hd], kv_ref[:, hd], kv_ref[:, pl.ds(D_MODEL + h * CA_DH, CA_DH)])
            o_ref[:, hd] = o.astype(BF16)

    return _pcall(
        body, name="attn_fwd", grid=(seq // tq,),
        in_specs=[pl.BlockSpec((tq, D_MODEL), lambda i: (i, 0)), pl.BlockSpec((n_mem, 2 * D_MODEL), lambda i: (0, 0))],
        out_specs=pl.BlockSpec((tq, D_MODEL), lambda i: (i, 0)), out_shape=jax.ShapeDtypeStruct((seq, D_MODEL), BF16),
        compiler_params=_params(("arbitrary",), 4 * _nbytes((tq, D_MODEL), F32) + 2 * _nbytes((n_mem, 2 * D_MODEL), F32)),
    )(q, kv)


def _attn_bwd(q, kv, do):
    seq, n_mem = q.shape[0], kv.shape[0]
    tq = _tile(seq, (512, 256, 128))

    def body(q_ref, kv_ref, do_ref, dq_ref, dkv_ref):
        @pl.when(pl.program_id(0) == 0)
        def _():
            dkv_ref[...] = jnp.zeros_like(dkv_ref)

        for h in range(HEADS):
            hd = pl.ds(h * CA_DH, CA_DH)
            vd = pl.ds(D_MODEL + h * CA_DH, CA_DH)
            _, vjp = jax.vjp(_attn_head, q_ref[:, hd], kv_ref[:, hd], kv_ref[:, vd])
            dq, dk, dv = vjp(do_ref[:, hd].astype(F32))
            dq_ref[:, hd] = dq.astype(BF16)
            dkv_ref[:, hd] += dk
            dkv_ref[:, vd] += dv

    return _pcall(
        body, name="attn_bwd", grid=(seq // tq,),
        in_specs=[pl.BlockSpec((tq, D_MODEL), lambda i: (i, 0)), pl.BlockSpec((n_mem, 2 * D_MODEL), lambda i: (0, 0)),
                  pl.BlockSpec((tq, D_MODEL), lambda i: (i, 0))],
        out_specs=[pl.BlockSpec((tq, D_MODEL), lambda i: (i, 0)), pl.BlockSpec((n_mem, 2 * D_MODEL), lambda i: (0, 0))],
        out_shape=[jax.ShapeDtypeStruct((seq, D_MODEL), BF16), jax.ShapeDtypeStruct((n_mem, 2 * D_MODEL), F32)],
        compiler_params=_params(("arbitrary",), 6 * _nbytes((tq, D_MODEL), F32) + 4 * _nbytes((n_mem, 2 * D_MODEL), F32)),
    )(q, kv, do)


def _ffn_mid(hg, xg, hv, xv, wg0, wg1, wg2, bg, wv0, wv1, wv2, bv):
    return jax.nn.gelu(causal_conv(hg, xg, (wg0, wg1, wg2), bg)) * causal_conv(hv, xv, (wv0, wv1, wv2), bv)


FFN_TB = 256
FFN_W = D_FF // 2
FFN_J = D_FF // FFN_W
MXU_COLS = 256
FFN_PIECES = tuple((off, min(MXU_COLS, FFN_W - off)) for off in range(0, FFN_W, MXU_COLS))


def _ffn_common_specs(seq, row):
    tb = min(FFN_TB, seq)
    full = pl.BlockSpec((tb, D_MODEL), lambda t, j: (row(t), 0))
    vec = pl.BlockSpec((1, D_MODEL), lambda t, j: (0, 0))
    halves = []
    for off in (0, FFN_J):
        halves.append(dict(
            w_up=pl.BlockSpec((None, D_MODEL, FFN_W), lambda t, j, off=off: (j + off, 0, 0)),
            taps=pl.BlockSpec((FFN_CONV, FFN_W), lambda t, j, off=off: (0, j + off)),
            bias=pl.BlockSpec((1, FFN_W), lambda t, j, off=off: (0, j + off))))
    w_down = pl.BlockSpec((FFN_W, D_MODEL), lambda t, j: (j, 0))
    u_blk = pl.BlockSpec((2, tb, FFN_W), lambda t, j: (0, row(t), j))
    return tb, full, vec, halves, w_down, u_blk


def _ffn_vmem(tb):
    return (_nbytes((2, tb, FFN_W), F32) + _nbytes((2, tb, FFN_W), BF16) + 3 * _nbytes((D_MODEL, FFN_W), BF16)
            + 10 * _nbytes((tb, D_MODEL), F32))


def _conv_params(taps_ref, bias_ref, cols):
    return taps_ref[0:1, cols], taps_ref[1:2, cols], taps_ref[2:3, cols], bias_ref[:, cols]


def _ffn_fwd(x2b, x2, w_up, conv_w, conv_b, w_down, ln_g, ln_b, target):
    seq = x2.shape[0]
    tb, full, vec, halves, wd_spec, u_blk = _ffn_common_specs(seq, lambda t: t)
    nt = seq // tb

    def body(xb_ref, wg_ref, wv_ref, tg_ref, tv_ref, bg_ref, bv_ref, wd_ref, x_ref, g_ref, b_ref, tgt_ref,
             u_ref, h_ref, dz_ref, dg_ref, db_ref, loss_ref, dzb_ref, acc, carry):
        t, j = pl.program_id(0), pl.program_id(1)
        xb = xb_ref[...]
        pieces = [pl.ds(off, width) for off, width in FFN_PIECES]
        ug = [_dg(xb, wg_ref[:, cols], 1, 0) for cols in pieces]
        uv = [_dg(xb, wv_ref[:, cols], 1, 0) for cols in pieces]
        hs = []
        for cols, g, v in zip(pieces, ug, uv):
            u_ref[0, :, cols] = g
            u_ref[1, :, cols] = v
            halo_g = jnp.where(t == 0, 0.0, carry[j, 0, :, cols])
            halo_v = jnp.where(t == 0, 0.0, carry[j, 1, :, cols])
            h = _ffn_mid(halo_g, g, halo_v, v, *_conv_params(tg_ref, bg_ref, cols),
                         *_conv_params(tv_ref, bv_ref, cols)).astype(BF16)
            carry[j, 0, :, cols] = g[tb - SUBLANES:, :]
            carry[j, 1, :, cols] = v[tb - SUBLANES:, :]
            h_ref[:, cols] = h
            hs.append(h)
        part = None
        for cols, h in zip(pieces, hs):
            p = _dg(h, wd_ref[cols, :], 1, 0)
            part = p if part is None else part + p

        @pl.when(j == 0)
        def _():
            acc[...] = part

        @pl.when(j > 0)
        def _():
            acc[...] += part

        @pl.when(j == FFN_J - 1)
        def _():
            y, vjp = jax.vjp(_layer_norm, acc[...] + ALPHA * x_ref[...], g_ref[...], b_ref[...])
            err = y - tgt_ref[...]
            part_loss = 0.5 * jnp.sum(jnp.sum(err * err, axis=1, keepdims=True), axis=0, keepdims=True) / D_MODEL
            dz, dg, db = vjp(err / D_MODEL)

            @pl.when(t == 0)
            def _():
                for r in (dg_ref, db_ref, loss_ref):
                    r[...] = jnp.zeros_like(r)

            dz_ref[...] = dz
            dzb_ref[...] = dz.astype(BF16)
            dg_ref[...] += dg
            db_ref[...] += db
            loss_ref[...] += jnp.broadcast_to(part_loss, (1, LANES))

    h0, h1 = halves
    row = jax.ShapeDtypeStruct((1, D_MODEL), F32)
    return _pcall(
        body, name="ffn_fwd", grid=(nt, FFN_J),
        in_specs=[full, h0["w_up"], h1["w_up"], h0["taps"], h1["taps"], h0["bias"], h1["bias"], wd_spec, full, vec, vec,
                  full],
        out_specs=[u_blk, pl.BlockSpec((tb, FFN_W), lambda t, j: (t, j)), full, vec, vec,
                   pl.BlockSpec((1, LANES), lambda t, j: (0, 0)), full],
        out_shape=[jax.ShapeDtypeStruct((2, seq, D_FF), F32), jax.ShapeDtypeStruct((seq, D_FF), BF16),
                   jax.ShapeDtypeStruct((seq, D_MODEL), F32), row, row, jax.ShapeDtypeStruct((1, LANES), F32),
                   jax.ShapeDtypeStruct((seq, D_MODEL), BF16)],
        scratch_shapes=[pltpu.VMEM((tb, D_MODEL), F32), pltpu.VMEM((FFN_J, 2, SUBLANES, FFN_W), F32)],
        compiler_params=_params(("arbitrary", "arbitrary"), _ffn_vmem(tb)),
    )(x2b, w_up, w_up, conv_w, conv_w, conv_b, conv_b, w_down, x2, ln_g, ln_b, target)


def _ffn_bwd(u, conv_w, conv_b, dz3b, dz3, w_down, w_up, z2, ln_g, ln_b):
    seq = dz3.shape[0]
    tb = min(FFN_TB, seq)
    nt = seq // tb
    row8 = tb // SUBLANES
    tb, full, vec, halves, wd_spec, u_blk = _ffn_common_specs(seq, lambda t: nt - 1 - t)
    halo = pl.BlockSpec((2, SUBLANES, FFN_W), lambda t, j: (0, jnp.maximum((nt - 1 - t) * row8 - 1, 0), j))

    def body(u_ref, halo_ref, tg_ref, tv_ref, bg_ref, bv_ref, dzb_ref, wd_ref, wg_ref, wv_ref, dz3_ref, z_ref, g_ref,
             b_ref, du_ref, dw_ref, dbias_ref, dz_ref, dg_ref, db_ref, dz2b_ref, acc, carry):
        t, j = pl.program_id(0), pl.program_id(1)

        @pl.when((t == 0) & (j == 0))
        def _():
            for r in (dw_ref, dbias_ref, dg_ref, db_ref):
                r[...] = jnp.zeros_like(r)

        pieces = [pl.ds(off, width) for off, width in FFN_PIECES]
        dzb = dzb_ref[...]
        dhs = [_dg(dzb, wd_ref[cols, :], 1, 1) for cols in pieces]
        first = t == nt - 1
        dus = []
        for cols, dh in zip(pieces, dhs):
            args = (jnp.where(first, 0.0, halo_ref[0, :, cols]), u_ref[0, :, cols],
                    jnp.where(first, 0.0, halo_ref[1, :, cols]), u_ref[1, :, cols],
                    *_conv_params(tg_ref, bg_ref, cols), *_conv_params(tv_ref, bv_ref, cols))
            _, vjp = jax.vjp(_ffn_mid, *args)
            dhg, dxg, dhv, dxv, g0, g1, g2, gb, v0, v1, v2, vb = vjp(dh)
            zeros = jnp.zeros((tb - SUBLANES, dh.shape[1]), F32)
            dug = (dxg + jnp.concatenate([zeros, jnp.where(t == 0, 0.0, carry[j, 0, :, cols])], axis=0)).astype(BF16)
            duv = (dxv + jnp.concatenate([zeros, jnp.where(t == 0, 0.0, carry[j, 1, :, cols])], axis=0)).astype(BF16)
            carry[j, 0, :, cols] = dhg
            carry[j, 1, :, cols] = dhv
            du_ref[0, :, cols] = dug
            du_ref[1, :, cols] = duv
            for half, parts in enumerate(((g0, g1, g2), (v0, v1, v2))):
                for d, p in enumerate(parts):
                    dw_ref[j, half, d:d + 1, cols] += p
            dbias_ref[j, 0, :, cols] += gb
            dbias_ref[j, 1, :, cols] += vb
            dus.append((dug, duv))
        part = None
        for cols, (dug, duv) in zip(pieces, dus):
            p = _dg(dug, wg_ref[:, cols], 1, 1) + _dg(duv, wv_ref[:, cols], 1, 1)
            part = p if part is None else part + p

        @pl.when(j == 0)
        def _():
            acc[...] = part

        @pl.when(j > 0)
        def _():
            acc[...] += part

        @pl.when(j == FFN_J - 1)
        def _():
            _, ln_vjp = jax.vjp(_layer_norm, z_ref[...], g_ref[...], b_ref[...])
            dz, dg, db = ln_vjp(acc[...] + ALPHA * dz3_ref[...])
            dz_ref[...] = dz
            dz2b_ref[...] = dz.astype(BF16)
            dg_ref[...] += dg
            db_ref[...] += db

    h0, h1 = halves
    row = jax.ShapeDtypeStruct((1, D_MODEL), F32)
    whole = lambda *shape: pl.BlockSpec(shape, lambda t, j: (0,) * len(shape))
    return _pcall(
        body, name="ffn_bwd", grid=(nt, FFN_J),
        in_specs=[u_blk, halo, h0["taps"], h1["taps"], h0["bias"], h1["bias"], full, wd_spec, h0["w_up"], h1["w_up"],
                  full, full, vec, vec],
        out_specs=[u_blk, whole(FFN_J, 2, FFN_CONV, FFN_W), whole(FFN_J, 2, 1, FFN_W), full, vec, vec, full],
        out_shape=[jax.ShapeDtypeStruct((2, seq, D_FF), BF16), jax.ShapeDtypeStruct((FFN_J, 2, FFN_CONV, FFN_W), F32),
                   jax.ShapeDtypeStruct((FFN_J, 2, 1, FFN_W), F32), jax.ShapeDtypeStruct((seq, D_MODEL), F32), row, row,
                   jax.ShapeDtypeStruct((seq, D_MODEL), BF16)],
        scratch_shapes=[pltpu.VMEM((tb, D_MODEL), F32), pltpu.VMEM((FFN_J, 2, SUBLANES, FFN_W), F32)],
        compiler_params=_params(("arbitrary", "arbitrary"), _ffn_vmem(tb)),
    )(u, u, conv_w, conv_w, conv_b, conv_b, dz3b, w_down, w_up, w_up, dz3, z2, ln_g, ln_b)


def _adamw_math(w, g, m, v):
    m_new = ADAM_B1 * m + (1.0 - ADAM_B1) * g
    v_new = ADAM_B2 * v + (1.0 - ADAM_B2) * jnp.square(g)
    m_hat = m_new / (1.0 - ADAM_B1 ** ADAM_STEP)
    v_hat = v_new / (1.0 - ADAM_B2 ** ADAM_STEP)
    return -ADAM_LR * (m_hat / (jnp.sqrt(v_hat) + ADAM_EPS) + ADAM_WD * w), m_new, v_new


def _adamw_many(name, ws, gs, ms, vs):
    n = len(ws)

    def body(*refs):
        w_refs, g_refs, m_refs, v_refs = (refs[i * n:(i + 1) * n] for i in range(4))
        d_refs, nm_refs, nv_refs = (refs[(4 + i) * n:(5 + i) * n] for i in range(3))
        for i in range(n):
            d_refs[i][...], nm_refs[i][...], nv_refs[i][...] = _adamw_math(
                w_refs[i][...], g_refs[i][...], m_refs[i][...], v_refs[i][...])

    vm = pl.BlockSpec(memory_space=pltpu.VMEM)
    outs = _pcall(
        body, pin=False, name=name, in_specs=[vm] * (4 * n), out_specs=[vm] * (3 * n),
        out_shape=[jax.ShapeDtypeStruct(w.shape, F32) for w in ws] * 3,
    )(*ws, *gs, *ms, *vs)
    return outs[:n], outs[n:2 * n], outs[2 * n:]


def _adamw_halves(name, core, w, mine, theirs, m, v):
    rows, cols = w.shape
    half_rows = mine.shape[0]
    tr = _tile(half_rows, (256, 176, 128))
    nbh = half_rows // tr
    assert 2 * half_rows == rows

    def body(c_ref, w_ref, a_ref, b_ref, m_ref, v_ref, g_ref, d_ref, nm_ref, nv_ref):
        g = jnp.where(pl.program_id(0) // nbh == c_ref[0], a_ref[...], b_ref[...])
        g_ref[...] = g
        d_ref[...], nm_ref[...], nv_ref[...] = _adamw_math(w_ref[...], g, m_ref[...], v_ref[...])

    spec = pl.BlockSpec((tr, cols), lambda i, c_ref: (i, 0))
    half = pl.BlockSpec((tr, cols), lambda i, c_ref: (i % nbh, 0))
    sh = jax.ShapeDtypeStruct((rows, cols), F32)
    grid_spec = pltpu.PrefetchScalarGridSpec(
        num_scalar_prefetch=1, grid=(rows // tr,), in_specs=[spec, half, half, spec, spec], out_specs=[spec] * 4)
    return _pcall(
        body, name=name, grid_spec=grid_spec, out_shape=[sh] * 4,
        compiler_params=_params(("arbitrary",), 18 * _nbytes((tr, -(-cols // LANES) * LANES), F32)),
    )(core, w, mine, theirs, m, v)


MESH = pl.DeviceIdType.MESH
ANY = pl.BlockSpec(memory_space=pl.ANY)
N_CHIPS = 4
BF16_ROWS = 16


def _me():
    return lax.axis_index("x"), lax.axis_index("y"), lax.axis_index("c")


def _other_chips(x, y):
    return [(1 - x, y), (x, 1 - y), (1 - x, 1 - y)]


def _remote(src, dst, ssem, rsem, dev):
    return pltpu.make_async_remote_copy(src_ref=src, dst_ref=dst, send_sem=ssem, recv_sem=rsem,
                                        device_id=dev, device_id_type=MESH)


def _half_rows(ref_rows, cc):
    half = ref_rows // 2
    return pl.ds(pl.multiple_of(cc * half, BF16_ROWS), half)


def _gather_weights(shards):
    n = len(shards)
    n_ici = n * (N_CHIPS - 1)

    def body(*refs):
        ins, outs, (ssem, rsem, lsem, lrsem) = refs[:n], refs[n:2 * n], refs[2 * n:]
        x, y, c = _me()
        k_me = 2 * x + y
        sib = (x, y, 1 - c)
        chips = _other_chips(x, y)
        started = []
        for i, (w_ref, o_ref) in enumerate(zip(ins, outs)):
            cp = _remote(w_ref, o_ref.at[k_me], lsem.at[i], lrsem.at[i], sib)
            cp.start()
            started.append(cp)
        for r, (px, py) in enumerate(chips):
            for i, (w_ref, o_ref) in enumerate(zip(ins, outs)):
                rows = _half_rows(w_ref.shape[0], c)
                s = r * n + i
                cp = _remote(w_ref.at[rows], o_ref.at[k_me, rows], ssem.at[s], rsem.at[s], (px, py, c))
                cp.start()
                started.append(cp)
        for r, (px, py) in enumerate(chips):
            for i, o_ref in enumerate(outs):
                blk = o_ref.at[2 * px + py, _half_rows(o_ref.shape[1], c)]
                s = r * n + i
                _remote(blk, blk, ssem.at[s], rsem.at[s], (px, py, c)).wait_recv()
                cp = _remote(blk, blk, ssem.at[n_ici + s], rsem.at[n_ici + s], sib)
                cp.start()
                started.append(cp)
        for r, (px, py) in enumerate(chips):
            for i, o_ref in enumerate(outs):
                blk = o_ref.at[2 * px + py, _half_rows(o_ref.shape[1], 1 - c)]
                s = n_ici + r * n + i
                _remote(blk, blk, ssem.at[s], rsem.at[s], sib).wait_recv()
        for cp in started[n:]:
            cp.wait_send()
        for cp in started[:n]:
            cp.wait()

    return _pcall(
        body, name="gather_weights", in_specs=[ANY] * n, out_specs=[ANY] * n,
        out_shape=[jax.ShapeDtypeStruct((N_CHIPS,) + s.shape, s.dtype) for s in shards],
        scratch_shapes=[pltpu.SemaphoreType.DMA((2 * n_ici,)), pltpu.SemaphoreType.DMA((2 * n_ici,)),
                        pltpu.SemaphoreType.DMA((n,)), pltpu.SemaphoreType.DMA((n,))],
    )(*shards)


def _swap_halves(name, grads):
    n = len(grads)

    def body(*refs):
        ins, outs, (ssem, rsem) = refs[:n], refs[n:2 * n], refs[2 * n:]
        x, y, c = _me()
        copies = []
        for i, (g_ref, o_ref) in enumerate(zip(ins, outs)):
            for k in range(N_CHIPS):
                s = i * N_CHIPS + k
                cp = _remote(g_ref.at[k, _half_rows(g_ref.shape[1], 1 - c)], o_ref.at[k], ssem.at[s], rsem.at[s],
                             (x, y, 1 - c))
                cp.start()
                copies.append(cp)
        for cp in copies:
            cp.wait()

    return _pcall(
        body, name=name, in_specs=[ANY] * n, out_specs=[ANY] * n,
        out_shape=[jax.ShapeDtypeStruct((N_CHIPS, g.shape[1] // 2, g.shape[2]), g.dtype) for g in grads],
        scratch_shapes=[pltpu.SemaphoreType.DMA((n * N_CHIPS,)), pltpu.SemaphoreType.DMA((n * N_CHIPS,))],
    )(*grads)


SEM = pl.BlockSpec(memory_space=pltpu.SEMAPHORE)
IN_HBM = pl.BlockSpec(memory_space=pltpu.HBM)
SPLIT_PARAMS = dict(compiler_params=pltpu.CompilerParams(has_side_effects=pltpu.SideEffectType.DATAFLOW_SIDE_EFFECTING))


def _split_start(name, sources, landings, n_copies, plan):
    ns, nl = len(sources), len(landings)

    def body(*refs):
        ins, lands, (ssem, rsem), token = refs[:ns], refs[ns:ns + nl], refs[ns + nl:ns + nl + 2], refs[-1]
        for s, (src, dst, _, dev) in enumerate(plan(ins, lands)):
            _remote(src, dst, ssem.at[s], rsem.at[s], dev).start()
        token[...] = jnp.zeros_like(token)

    arrays = list(sources) + list(landings)
    outs = _call(
        body, name=name, in_specs=[IN_HBM] * (ns + nl),
        out_specs=[SEM, SEM] + [IN_HBM] * (ns + nl) + [pl.BlockSpec(memory_space=pltpu.VMEM)],
        out_shape=[pltpu.SemaphoreType.DMA((n_copies,)), pltpu.SemaphoreType.DMA((n_copies,))]
        + [pltpu.HBM(a.shape, a.dtype) for a in arrays] + [jax.ShapeDtypeStruct((SUBLANES, LANES), F32)],
        input_output_aliases={i: 2 + i for i in range(ns + nl)}, **SPLIT_PARAMS,
    )(*[pltpu.with_memory_space_constraint(a, pltpu.HBM) for a in arrays])
    return (outs[:-1], ns), outs[-1]


def _split_wait(name, handle, after, plan):
    (ssem, rsem, *thru), ns = handle
    nl = len(thru) - ns

    def body(*refs):
        ins, lands, (ssem_ref, rsem_ref) = refs[:ns], refs[ns:ns + nl], refs[ns + nl:ns + nl + 2]
        for s, (src, _, dst, dev) in enumerate(plan(ins, lands)):
            cp = _remote(src, dst, ssem_ref.at[s], rsem_ref.at[s], dev)
            cp.wait_send()
            cp.wait_recv()

    outs = _call(
        body, name=name, in_specs=[IN_HBM] * (ns + nl) + [SEM, SEM, ANY], out_specs=[IN_HBM] * (ns + nl),
        out_shape=[pltpu.HBM(t.shape, t.dtype) for t in thru],
        input_output_aliases={i: i for i in range(ns + nl)}, **SPLIT_PARAMS,
    )(*thru, ssem, rsem, after)
    return outs[:ns], outs[ns:]


def _swap_plan(ins, lands):
    x, y, c = _me()
    return [(g_ref.at[k, _half_rows(g_ref.shape[1], 1 - c)], l_ref.at[k], l_ref.at[k], (x, y, 1 - c))
            for g_ref, l_ref in zip(ins, lands) for k in range(N_CHIPS)]


def _swap_start(name, grads):
    lands = [lax.empty((N_CHIPS, g.shape[1] // 2, g.shape[2]), g.dtype) for g in grads]
    return _split_start(name, grads, lands, len(grads) * N_CHIPS, _swap_plan)


def _swap_wait(name, handle, after):
    return _split_wait(name, handle, after, _swap_plan)


def _gather_plan(ins, lands):
    x, y, c = _me()
    k_me = 2 * x + y
    plan = [(w_ref, l_ref.at[k_me], l_ref.at[k_me], (x, y, 1 - c)) for w_ref, l_ref in zip(ins, lands)]
    for px, py in _other_chips(x, y):
        for w_ref, l_ref in zip(ins, lands):
            rows = _half_rows(w_ref.shape[0], c)
            plan.append((w_ref.at[rows], l_ref.at[k_me, rows], l_ref.at[2 * px + py, rows], (px, py, c)))
    return plan


def _gather_start(name, shards):
    lands = [lax.empty((N_CHIPS,) + s.shape, s.dtype) for s in shards]
    return _split_start(name, shards, lands, len(shards) * N_CHIPS, _gather_plan)


def _gather_wait(name, handle, after):
    return _split_wait(name, handle, after, _gather_plan)[1]


def _forward_halves(name, blocks):
    n = len(blocks)
    n_sem = n * (N_CHIPS - 1)

    def body(*refs):
        outs, (ssem, rsem) = refs[n:2 * n], refs[2 * n:]
        x, y, c = _me()
        sib = (x, y, 1 - c)
        chips = _other_chips(x, y)
        sends = []
        for r, (px, py) in enumerate(chips):
            for i, o_ref in enumerate(outs):
                blk = o_ref.at[2 * px + py, _half_rows(o_ref.shape[1], c)]
                cp = _remote(blk, blk, ssem.at[r * n + i], rsem.at[r * n + i], sib)
                cp.start()
                sends.append(cp)
        for r, (px, py) in enumerate(chips):
            for i, o_ref in enumerate(outs):
                blk = o_ref.at[2 * px + py, _half_rows(o_ref.shape[1], 1 - c)]
                _remote(blk, blk, ssem.at[r * n + i], rsem.at[r * n + i], sib).wait_recv()
        for cp in sends:
            cp.wait_send()

    return _pcall(
        body, name=name, in_specs=[ANY] * n, out_specs=[ANY] * n,
        out_shape=[jax.ShapeDtypeStruct(b.shape, b.dtype) for b in blocks],
        input_output_aliases={i: i for i in range(n)},
        scratch_shapes=[pltpu.SemaphoreType.DMA((n_sem,)), pltpu.SemaphoreType.DMA((n_sem,))],
    )(*blocks)


def _scatter_plan(ins, lands):
    x, y, c = _me()
    k_me = 2 * x + y
    return [(p_ref.at[2 * px + py], l_ref.at[k_me], l_ref.at[2 * px + py], (px, py, c))
            for px, py in _other_chips(x, y) for p_ref, l_ref in zip(ins, lands)]


def _scatter_start(name, parts):
    lands = [lax.empty(p.shape, p.dtype) for p in parts]
    return _split_start(name, parts, lands, len(parts) * (N_CHIPS - 1), _scatter_plan)


def _scatter_wait(name, handle, after):
    return _split_wait(name, handle, after, _scatter_plan)[1]


def _share_halves(halves):
    n = len(halves)

    def body(*refs):
        ins, outs, (ssem, rsem) = refs[:n], refs[n:2 * n], refs[2 * n:]
        x, y, c = _me()
        copies = [_remote(r_ref, o_ref, ssem.at[i], rsem.at[i], (x, y, 1 - c))
                  for i, (r_ref, o_ref) in enumerate(zip(ins, outs))]
        for cp in copies:
            cp.start()
        for cp in copies:
            cp.wait()

    return _pcall(
        body, name="share_halves", in_specs=[ANY] * n, out_specs=[ANY] * n,
        out_shape=[jax.ShapeDtypeStruct(h.shape, h.dtype) for h in halves],
        scratch_shapes=[pltpu.SemaphoreType.DMA((n,)), pltpu.SemaphoreType.DMA((n,))],
    )(*halves)


def _reduce_small(v):
    rows = v.shape[0]
    half = rows // 2
    assert half % SUBLANES == 0

    def body(v_ref, out_ref, pair_buf, mine, chip_buf, ssem, rsem):
        x, y, c = _me()
        k_me = 2 * x + y
        sib = (x, y, 1 - c)

        def rows_of(cc):
            return pl.ds(pl.multiple_of(cc * half, SUBLANES), half)

        swap = _remote(v_ref.at[rows_of(1 - c)], pair_buf, ssem.at[0], rsem.at[0], sib)
        swap.start()
        swap.wait()
        mine[...] = v_ref[rows_of(c), :] + pair_buf[...]
        chip_buf[k_me] = mine[...]
        sends = [_remote(mine, chip_buf.at[k_me], ssem.at[1 + r], rsem.at[1 + r], (px, py, c))
                 for r, (px, py) in enumerate(_other_chips(x, y))]
        for cp in sends:
            cp.start()
        for r, (px, py) in enumerate(_other_chips(x, y)):
            blk = chip_buf.at[2 * px + py]
            _remote(blk, blk, ssem.at[1 + r], rsem.at[1 + r], (px, py, c)).wait_recv()
        total = chip_buf[0]
        for k in range(1, N_CHIPS):
            total = total + chip_buf[k]
        out_ref[rows_of(c), :] = total
        for cp in sends:
            cp.wait_send()
        share = _remote(out_ref.at[rows_of(c)], out_ref.at[rows_of(c)], ssem.at[N_CHIPS], rsem.at[N_CHIPS], sib)
        share.start()
        got = out_ref.at[rows_of(1 - c)]
        _remote(got, got, ssem.at[N_CHIPS], rsem.at[N_CHIPS], sib).wait_recv()
        share.wait_send()

    vm = pl.BlockSpec(memory_space=pltpu.VMEM)
    return _pcall(
        body, pin=False, name="reduce_small", in_specs=[vm], out_specs=vm,
        out_shape=jax.ShapeDtypeStruct((rows, LANES), F32),
        scratch_shapes=[pltpu.VMEM((half, LANES), F32), pltpu.VMEM((half, LANES), F32),
                        pltpu.VMEM((N_CHIPS, half, LANES), F32), pltpu.SemaphoreType.DMA((N_CHIPS + 1,)),
                        pltpu.SemaphoreType.DMA((N_CHIPS + 1,))],
        compiler_params=pltpu.CompilerParams(vmem_limit_bytes=32 * 1024 * 1024),
    )(v)


def _add_pair(name, core, chip, g, theirs):
    _, half, cols = theirs.shape
    tr = _tile(half, (256, 176, 128))
    nb = half // tr

    def body(c_ref, k_ref, g_ref, t_ref, o32_ref, o16_ref):
        s = g_ref[...] + t_ref[...]
        o16_ref[...] = s.astype(BF16)

        @pl.when(pl.program_id(1) == k_ref[0])
        def _():
            o32_ref[...] = s

    spec = pl.BlockSpec((None, tr, cols), lambda i, k, c_ref, k_ref: (k, i, 0))
    grid_spec = pltpu.PrefetchScalarGridSpec(
        num_scalar_prefetch=2, grid=(nb, N_CHIPS),
        in_specs=[pl.BlockSpec((None, tr, cols), lambda i, k, c_ref, k_ref: (k, c_ref[0] * nb + i, 0)), spec],
        out_specs=[pl.BlockSpec((tr, cols), lambda i, k, c_ref, k_ref: (i, 0)), spec])
    return _pcall(
        body, name=name, grid_spec=grid_spec,
        out_shape=[jax.ShapeDtypeStruct((half, cols), F32), jax.ShapeDtypeStruct(theirs.shape, BF16)],
        compiler_params=_params(("arbitrary", "arbitrary"), 8 * _nbytes((tr, cols + LANES), F32)),
    )(core, chip, g, theirs)


def _add_chips(name, chip, p32, recv):
    half, cols = p32.shape
    tr = _tile(half, (256, 176, 128))

    def body(k_ref, p_ref, r0_ref, r1_ref, r2_ref, o_ref):
        o_ref[...] = ((p_ref[...] + r0_ref[...].astype(F32)) + r1_ref[...].astype(F32)) + r2_ref[...].astype(F32)

    def other(r):
        return pl.BlockSpec((None, tr, cols), lambda i, k_ref: (r + (k_ref[0] <= r).astype(jnp.int32), i, 0))
    grid_spec = pltpu.PrefetchScalarGridSpec(
        num_scalar_prefetch=1, grid=(half // tr,),
        in_specs=[pl.BlockSpec((tr, cols), lambda i, k_ref: (i, 0)), other(0), other(1), other(2)],
        out_specs=pl.BlockSpec((tr, cols), lambda i, k_ref: (i, 0)))
    return _pcall(
        body, name=name, grid_spec=grid_spec, out_shape=jax.ShapeDtypeStruct((half, cols), F32),
        compiler_params=_params(("arbitrary",), 10 * _nbytes((tr, cols + LANES), F32)),
    )(chip, p32, recv, recv, recv)


def kernel(x, mem, w_in, b_in, hg_lb_logits, hg_norm_w, ml_conv_w, ml_conv_b, ml_norm_w, w_out, ln1_g, ln1_b, ca_wq, ca_wkv, ca_wo, ln2_g, ln2_b, ffn_w_up, ffn_conv_w, ffn_conv_b, ffn_w_down, ln3_g, ln3_b, loss_target, m_w_in, m_b_in, m_hg_lb_logits, m_hg_norm_w, m_ml_conv_w, m_ml_conv_b, m_ml_norm_w, m_w_out, m_ln1_g, m_ln1_b, m_ca_wq, m_ca_wkv, m_ca_wo, m_ln2_g, m_ln2_b, m_ffn_w_up, m_ffn_conv_w, m_ffn_conv_b, m_ffn_w_down, m_ln3_g, m_ln3_b, v_w_in, v_b_in, v_hg_lb_logits, v_hg_norm_w, v_ml_conv_w, v_ml_conv_b, v_ml_norm_w, v_w_out, v_ln1_g, v_ln1_b, v_ca_wq, v_ca_wkv, v_ca_wo, v_ln2_g, v_ln2_b, v_ffn_w_up, v_ffn_conv_w, v_ffn_conv_b, v_ffn_w_down, v_ln3_g, v_ln3_b):
    return _train_step(dict(locals()))


WEIGHTS = ("w_in", "b_in", "hg_lb_logits", "hg_norm_w", "ml_conv_w", "ml_conv_b", "ml_norm_w", "w_out", "ln1_g",
           "ln1_b", "ca_wq", "ca_wkv", "ca_wo", "ln2_g", "ln2_b", "ffn_w_up", "ffn_conv_w", "ffn_conv_b",
           "ffn_w_down", "ln3_g", "ln3_b")
MATRICES = ("w_in", "w_out", "ca_wq", "ca_wkv", "ca_wo", "ffn_w_up", "ffn_w_down")
COL_SHARDED = ("w_in", "ca_wkv", "ffn_w_up", "ml_conv_w", "ffn_conv_w")
SMALL = tuple(n for n in WEIGHTS if n not in MATRICES)
PART_ROWS = 16


def _part_rows(shape):
    n = 1
    for s in shape:
        n *= s
    return -(-n // (LANES * PART_ROWS)) * PART_ROWS


def _pack(arrs, dtype):
    parts = []
    for a in arrs:
        flat = a.reshape(-1).astype(dtype)
        flat = jnp.pad(flat, (0, _part_rows(a.shape) * LANES - flat.shape[0]))
        parts.append(flat.reshape(-1, LANES))
    return jnp.concatenate(parts, axis=0)


def _unpack(buf, shapes):
    lead = buf.shape[:-2]
    outs, r = [], 0
    for sh in shapes:
        n = 1
        for s in sh:
            n *= s
        nr = _part_rows(sh)
        flat = buf[..., r:r + nr, :].reshape(lead + (nr * LANES,))
        outs.append(flat[..., :n].reshape(lead + tuple(sh)))
        r += nr
    return outs


def _cat_cols(s):
    return jnp.moveaxis(s, 0, 1).reshape(s.shape[1], -1)


def _stack_rows(s):
    return s.reshape(-1, s.shape[-1])


def _train_step(a):
    xs, mems, tgt = a["x"][0], a["mem"][0], a["loss_target"][0]
    core = lax.axis_index("c").astype(jnp.int32).reshape(1)
    chip = (2 * lax.axis_index("x") + lax.axis_index("y")).astype(jnp.int32).reshape(1)
    k_me = chip[0]
    shard = {n: a[n][0] for n in MATRICES}

    later = [n for n in MATRICES if n != "w_in"]
    w_in, taps = _gather_weights([shard["w_in"].astype(BF16), _pack([a["ml_conv_w"][0], a["ffn_conv_w"][0]], F32)])
    w = {"w_in": jnp.pad(_cat_cols(w_in), ((0, 0), (0, D_IN_PAD - D_IN)))}
    gathering, token = _gather_start("gather_start", [shard[n].astype(BF16) for n in later])
    ml_cw, ffn_cw = [_cat_cols(s) for s in _unpack(taps, [a["ml_conv_w"].shape[1:], a["ffn_conv_w"].shape[1:]])]
    b_in_p = jnp.pad(a["b_in"], ((0, 0), (0, D_IN_PAD - D_IN))) + token[0:1, 0:1]
    mixer_w = (a["hg_lb_logits"], a["hg_norm_w"], ml_cw, a["ml_conv_b"], a["ml_norm_w"])
    up_cols = a["ffn_w_up"].shape[-1]

    proj, xb = _mm("proj", "nn", xs, w["w_in"], bias=b_in_p, a_copy_dtype=BF16, tm=256, tn=D_IN_PAD)
    y, hst, cst, nst, mst = _mixer_fwd(proj, *mixer_w)
    w.update(zip(later, _forward_halves("forward_halves", _gather_wait("gather_wait", gathering, y))))
    for n in ("w_out", "ca_wq", "ca_wo", "ffn_w_down"):
        w[n] = _stack_rows(w[n])
    z1, x1, x1b = _mm("mix_out", "nn", y, w["w_out"], res=xs, res_scale=ALPHA, ln=("fwd", a["ln1_g"], a["ln1_b"]),
                      copy_dtype=BF16)
    q = _mm("ca_q", "nn", x1b, w["ca_wq"], out_dtype=BF16, tn=D_MODEL)
    kv = _mm("ca_kv", "nn", mems, w["ca_wkv"])
    o = _attn_fwd(q, kv)
    z2, x2, x2b = _mm("ca_out", "nn", o, w["ca_wo"], res=x1, res_scale=ALPHA, ln=("fwd", a["ln2_g"], a["ln2_b"]),
                      copy_dtype=BF16)
    w_up = w["ffn_w_up"]
    assert w_up.shape == (2 * FFN_J, D_MODEL, FFN_W)
    u, hmid, dz3, g_ln3g, g_ln3b, loss_part, dz3b = _ffn_fwd(
        x2b, x2, w_up, ffn_cw, a["ffn_conv_b"], w["ffn_w_down"], a["ln3_g"], a["ln3_b"], tgt)

    grads = {"ln3_g": g_ln3g, "ln3_b": g_ln3b}
    grads["ffn_w_down"] = _mm("g_w_down", "tn", hmid, dz3b, tm=D_FF // 2, tn=D_MODEL)
    du, g_cw, g_cb, dz2, grads["ln2_g"], grads["ln2_b"], dz2b = _ffn_bwd(
        u, ffn_cw, a["ffn_conv_b"], dz3b, dz3, w["ffn_w_down"], w_up, z2, a["ln2_g"], a["ln2_b"])
    grads["ffn_conv_w"] = jnp.transpose(g_cw, (2, 1, 0, 3)).reshape(FFN_CONV, 2 * D_FF)
    grads["ffn_conv_b"] = jnp.transpose(g_cb, (2, 1, 0, 3)).reshape(1, 2 * D_FF)
    grads["ffn_w_up"] = _mm("g_w_up", "tn", x2b, du, out_groups=N_CHIPS, tm=D_MODEL, tn=up_cols)
    grads["ffn_w_down"] = grads["ffn_w_down"].reshape((N_CHIPS,) + shard["ffn_w_down"].shape)
    pending = {}

    def reduce_start(tag, names, swapped=None):
        group = [grads[n] for n in names]
        group, theirs = swapped or (group, _swap_halves("swap_halves_" + tag, group))
        sums = [_add_pair("add_pair_" + n, core, chip, g, t) for n, g, t in zip(names, group, theirs)]
        handle, token = _scatter_start("scatter_start_" + tag, [s16 for _, s16 in sums])
        pending[tag] = (names, [s32 for s32, _ in sums], handle)
        return token[0:1, 0:1]

    ffn = ("ffn_w_up", "ffn_w_down")
    swapping, token = _swap_start("swap_start_ffn", [grads[n] for n in ffn])
    do = _mm("d_o", "nt", dz2b, w["ca_wo"], bias=jnp.zeros((1, D_MODEL), F32) + token[0:1, 0:1], out_dtype=BF16,
             tn=D_MODEL)
    grads["ca_wo"] = _mm("g_wo", "tn", o, dz2b, tm=D_MODEL, tn=D_MODEL)
    zero = reduce_start("ffn", ffn, _swap_wait("swap_wait_ffn", swapping, grads["ca_wo"]))
    dq, dkv = _attn_bwd(q, kv + zero, do)
    grads["ca_wq"] = _mm("g_wq", "tn", x1b, dq, tm=D_MODEL, tn=D_MODEL)
    grads["ca_wkv"] = _mm("g_wkv", "tn", mems, dkv, out_groups=N_CHIPS, tm=D_MODEL)
    dz1, grads["ln1_g"], grads["ln1_b"], dz1b = _mm("d_x1", "nt", dq, w["ca_wq"], res=dz2, res_scale=ALPHA,
                                                    ln=("bwd", z1, a["ln1_g"], a["ln1_b"]), copy_dtype=BF16)
    grads["w_out"] = _mm("g_w_out", "tn", y, dz1b, tm=D_MODEL, tn=D_MODEL)
    for n in ("w_out", "ca_wq", "ca_wo"):
        grads[n] = grads[n].reshape((N_CHIPS,) + shard[n].shape)
    attn = ("w_out", "ca_wq", "ca_wkv", "ca_wo")
    swapping, token = _swap_start("swap_start_attn", [grads[n] for n in attn])
    dy = _mm("d_y", "nt", dz1b, w["w_out"], bias=jnp.zeros((1, D_MODEL), F32) + token[0:1, 0:1], tn=D_MODEL)
    zero = reduce_start("attn", attn, _swap_wait("swap_wait_attn", swapping, dy))
    (dproj, g_b_in, grads["hg_lb_logits"], grads["hg_norm_w"], grads["ml_conv_w"], grads["ml_conv_b"],
     grads["ml_norm_w"]) = _mixer_bwd(proj, dy, hst, cst, nst, mst, mixer_w[0], mixer_w[1] + zero, *mixer_w[2:])
    g_in = _mm("g_w_in", "tn", xb, dproj, tm=D_MODEL, tn=up_cols)[:, :D_IN]
    grads["w_in"] = jnp.moveaxis(g_in.reshape(D_MODEL, N_CHIPS, -1), 1, 0)
    grads["b_in"] = g_b_in[:, :D_IN]
    zero = reduce_start("in", ("w_in",))
    dx = _mm("d_x", "nt", dproj, w["w_in"], bias=jnp.zeros((1, D_MODEL), F32) + zero, res=dz1, res_scale=ALPHA,
             tm=256, tn=D_MODEL)

    halves = {}
    for tag, (names, sums32, handle) in pending.items():
        for n, s32, r in zip(names, sums32, _scatter_wait("scatter_wait_" + tag, handle, dx)):
            halves[n] = _add_chips("add_chips_" + n, chip, s32, r)
    halves = [halves[n] for n in MATRICES]
    other_halves = _share_halves(halves)

    small_shapes = [grads[n].shape for n in SMALL] + [loss_part.shape]
    summed = _unpack(_reduce_small(_pack([grads[n] for n in SMALL] + [loss_part], F32)), small_shapes)
    loss = summed[-1][0, 0]
    for n, g in zip(SMALL, summed[:-1]):
        if n in COL_SHARDED:
            cols = a[n].shape[-1]
            g = lax.dynamic_slice_in_dim(g, k_me * cols, cols, axis=1)
        grads[n] = g

    delta, new_m, new_v = {}, {}, {}
    for n, mine, theirs in zip(MATRICES, halves, other_halves):
        grads[n], delta[n], new_m[n], new_v[n] = _adamw_halves(
            "adamw_" + n, core, shard[n], mine, theirs, a["m_" + n][0], a["v_" + n][0])
    small_w = [a[n][0] if a[n].ndim == 3 else a[n] for n in SMALL]
    small_m = [a["m_" + n][0] if a[n].ndim == 3 else a["m_" + n] for n in SMALL]
    small_v = [a["v_" + n][0] if a[n].ndim == 3 else a["v_" + n] for n in SMALL]
    for out, vals in zip((delta, new_m, new_v),
                         _adamw_many("adamw_small", small_w, [grads[n] for n in SMALL], small_m, small_v)):
        out.update(zip(SMALL, vals))

    def shaped(d):
        return [d[n].reshape(a[n].shape) for n in WEIGHTS]
    return (loss, dx[None], *shaped(grads), *shaped(delta), *shaped(new_m), *shaped(new_v))
```

```python
import functools

import jax
import jax.numpy as jnp
from jax import lax
from jax.experimental import pallas as pl
from jax.experimental.pallas import tpu as pltpu

F32 = jnp.float32
BF16 = jnp.bfloat16

D_MODEL = 1024
HEADS = 4
DK = 128
D_GRP = HEADS * DK
CHUNK = 64
ML_CONV = 4
FFN_CONV = 3
D_FF = 2816
CA_DH = D_MODEL // HEADS
DEPTH = 1
ALPHA = (2.0 * DEPTH) ** 0.25
LN_EPS = 1e-5
NEG_BIG = -1e30
D_IN = 8 * D_GRP + 2 * HEADS
D_IN_PAD = 8 * D_GRP + 128
ADAM_LR, ADAM_B1, ADAM_B2, ADAM_EPS, ADAM_WD, ADAM_STEP = 0.001, 0.9, 0.999, 1e-08, 0.01, 10

SUBLANES = 8
LANES = 128
VMEM_BYTES = 64 * 1024 * 1024


def _pcall(body, pin=True, **kw):
    if not pin:
        return _call(body, **kw)
    kw["out_shape"] = jax.tree.map(lambda s: pltpu.HBM(s.shape, s.dtype), kw["out_shape"])
    call = _call(body, **kw)

    def pinned(*args):
        return call(*[pltpu.with_memory_space_constraint(x, pltpu.HBM) if jnp.issubdtype(x.dtype, jnp.floating) else x
                      for x in args])
    return pinned


def _call(body, **kw):
    return pl.pallas_call(body, **kw)


def _params(semantics, vmem_bytes):
    limit = int(min(max(2 * vmem_bytes, 16 * 1024 * 1024), VMEM_BYTES - 8 * 1024 * 1024))
    return pltpu.CompilerParams(dimension_semantics=semantics, vmem_limit_bytes=limit)


def _nbytes(shape, dtype):
    n = 1
    for s in shape:
        n *= s
    return n * jnp.dtype(dtype).itemsize


def _dg(a, b, ca, cb):
    return lax.dot_general(a.astype(BF16), b.astype(BF16), (((ca,), (cb,)), ((), ())),
                           preferred_element_type=F32)


@jax.custom_vjp
def mm_nn(a, b):
    return _dg(a, b, 1, 0)


mm_nn.defvjp(lambda a, b: (_dg(a, b, 1, 0), (a, b)),
             lambda r, g: (_dg(g, r[1], 1, 1).astype(r[0].dtype), _dg(r[0], g, 0, 0).astype(r[1].dtype)))


@jax.custom_vjp
def mm_nt(a, b):
    return _dg(a, b, 1, 1)


mm_nt.defvjp(lambda a, b: (_dg(a, b, 1, 1), (a, b)),
             lambda r, g: (_dg(g, r[1], 1, 0).astype(r[0].dtype), _dg(g, r[0], 0, 0).astype(r[1].dtype)))


@jax.custom_vjp
def mm_tn(a, b):
    return _dg(a, b, 0, 0)


mm_tn.defvjp(lambda a, b: (_dg(a, b, 0, 0), (a, b)),
             lambda r, g: (_dg(r[1], g, 1, 1).astype(r[0].dtype), _dg(r[0], g, 1, 0).astype(r[1].dtype)))


def _tri(n, lower):
    r = lax.broadcasted_iota(jnp.int32, (n, n), 0)
    c = lax.broadcasted_iota(jnp.int32, (n, n), 1)
    return ((r >= c) if lower else (r <= c)).astype(F32)


def _tri_dot(lower, x):
    t = _tri(x.shape[0], lower).astype(BF16)
    hi = x.astype(BF16)
    rest = x - hi.astype(F32)
    mid = rest.astype(BF16)
    lo = (rest - mid.astype(F32)).astype(BF16)
    return sum(lax.dot_general(t, p, (((1,), (0,)), ((), ())), preferred_element_type=F32) for p in (hi, mid, lo))


@jax.custom_vjp
def cumsum_rows(x):
    return _tri_dot(True, x)


cumsum_rows.defvjp(lambda x: (_tri_dot(True, x), None), lambda _, g: (_tri_dot(False, g),))


def _shift_impl(halo, x, d):
    xx = jnp.concatenate([halo, x], axis=0)
    return pltpu.roll(xx, d, 0)[SUBLANES:]


@functools.partial(jax.custom_vjp, nondiff_argnums=(2,))
def shift_rows(halo, x, d):
    return _shift_impl(halo, x, d)


def _shift_bwd(d, _, g):
    n = g.shape[0] + SUBLANES
    gg = jnp.concatenate([jnp.zeros((SUBLANES, g.shape[1]), g.dtype), g], axis=0)
    r = pltpu.roll(gg, n - d, 0)
    return r[:SUBLANES], r[SUBLANES:]


shift_rows.defvjp(lambda halo, x, d: (_shift_impl(halo, x, d), None), _shift_bwd)


def causal_conv(halo, x, w_rows, b):
    k = len(w_rows)
    y = b + w_rows[k - 1] * x
    for d in range(1, k):
        y = y + w_rows[k - 1 - d] * shift_rows(halo, x, d)
    return y


def _sigmoid(x):
    return 1.0 / (1.0 + jnp.exp(-x))


def _silu(x):
    return x * _sigmoid(x)


def _log_sigmoid(x):
    return jnp.minimum(x, 0.0) - jnp.log(1.0 + jnp.exp(-jnp.abs(x)))


def _pick_row(x, i):
    row = lax.broadcasted_iota(jnp.int32, (x.shape[0], 1), 0)
    return jnp.sum(jnp.where(row == i, x, 0.0), axis=0, keepdims=True)


def _layer_norm(z, g, b):
    mu = jnp.mean(z, axis=-1, keepdims=True)
    zc = z - mu
    var = jnp.mean(zc * zc, axis=-1, keepdims=True)
    return zc * lax.rsqrt(var + LN_EPS) * g + b


def _qk_conv(halo, x, w0, w1, w2, w3, b):
    return _silu(causal_conv(halo, x, (w0, w1, w2, w3), b))


def _grp(i):
    return pl.ds(i * D_GRP, D_GRP)


def _mixer_specs(n_chunks, reverse):
    def chunk(c):
        return n_chunks - 1 - c if reverse else c
    row8 = CHUNK // SUBLANES
    proj_spec = pl.BlockSpec((CHUNK, D_IN_PAD), lambda c: (chunk(c), 0))
    halo_spec = pl.BlockSpec((SUBLANES, 2 * D_GRP), lambda c: (jnp.maximum(chunk(c) * row8 - 1, 0), 2))
    small = [pl.BlockSpec((2, D_GRP), lambda c: (0, 0)), pl.BlockSpec((1, D_GRP), lambda c: (0, 0)),
             pl.BlockSpec((ML_CONV, 2 * D_GRP), lambda c: (0, 0)), pl.BlockSpec((1, 2 * D_GRP), lambda c: (0, 0)),
             pl.BlockSpec((1, D_GRP), lambda c: (0, 0))]
    state_specs = [pl.BlockSpec((1, HEADS, DK, DK), lambda c: (chunk(c), 0, 0, 0)),
                   pl.BlockSpec((1, HEADS, DK, DK), lambda c: (chunk(c), 0, 0, 0)),
                   pl.BlockSpec((1, HEADS, 1, DK), lambda c: (chunk(c), 0, 0, 0)),
                   pl.BlockSpec((1, HEADS, 1, DK), lambda c: (chunk(c), 0, 0, 0))]
    y_spec = pl.BlockSpec((CHUNK, 2 * D_GRP), lambda c: (chunk(c), 0))
    return proj_spec, halo_spec, small, state_specs, y_spec, chunk


def _heads(x):
    return [x[:, h * DK:(h + 1) * DK] for h in range(HEADS)]


def _last(x, j):
    lane = lax.broadcasted_iota(jnp.int32, (1, x.shape[-1]), 1)
    return jnp.sum(jnp.where(lane == j, x, 0.0), axis=-1, keepdims=True)


def _hg_chunk(st_t, hq, hf, hi, hgate, l0, l1, nw):
    n = hq.shape[0]
    lb = _sigmoid(l0 - l1)
    q = _silu(hq)
    lf = jnp.log(lb + (1.0 - lb) * _sigmoid(hf))
    k = (1.0 - lb) * _sigmoid(-hf)
    b = cumsum_rows(lf)
    b_ref = _pick_row(b, n // 2 - 1)
    b_last = _pick_row(b, n - 1)
    qa, ka =_heads(q * jnp.exp(b - b_ref)), _heads(k * jnp.exp(b_ref - b))
    qe, kd, eb, v = _heads(q * jnp.exp(b)), _heads(k * jnp.exp(b_last - b)), _heads(jnp.exp(b_last)), _heads(hi)
    tri = _tri(n, True) > 0
    attn = [jnp.where(tri, mm_nt(qa[h], ka[h]), 0.0) for h in range(HEADS)]
    o = [mm_nn(attn[h], v[h]) + mm_nt(qe[h], st_t[h]) for h in range(HEADS)]
    st_new = jnp.stack([eb[h] * st_t[h] + mm_tn(v[h], kd[h]) for h in range(HEADS)])
    yn = [o[h] * lax.rsqrt(jnp.mean(o[h] * o[h], axis=-1, keepdims=True) + LN_EPS) for h in range(HEADS)]
    return st_new, jnp.concatenate(yn, axis=1) * nw * _silu(hgate)


def _ml_chunk(c_st, n_st, m_st, q, k, v, gates, og, nw):
    n = q.shape[0]
    ig = jnp.stack([_last(gates, h) for h in range(HEADS)])
    log_f = _log_sigmoid(gates)
    fl = jnp.stack([_last(log_f, HEADS + h) for h in range(HEADS)])
    bw = cumsum_rows(jnp.concatenate([jnp.broadcast_to(fl[h], (n, DK)) for h in range(HEADS)], axis=1))
    b = jnp.stack([_last(x, 0) for x in _heads(bw)])
    g = jnp.sum(fl, axis=1, keepdims=True)
    eye = lax.broadcasted_iota(jnp.int32, (n, n), 0) == lax.broadcasted_iota(jnp.int32, (n, n), 1)
    e_row = jnp.sum(jnp.where(eye, ig - b, 0.0), axis=1, keepdims=True)
    d = jnp.where(_tri(n, True) > 0, b + e_row, -jnp.inf)
    inter = b + m_st
    m_t = jnp.maximum(inter, jnp.max(d, axis=2, keepdims=True))
    qs, kh, vh = _heads(q * (DK ** -0.5)), _heads(k), _heads(v)
    s = jnp.stack([mm_nt(qs[h], kh[h]) for h in range(HEADS)]) * jnp.exp(d - m_t)
    w_inter = jnp.exp(inter - m_t)
    num = (jnp.stack([mm_nn(s[h], vh[h]) for h in range(HEADS)])
           + w_inter * jnp.stack([mm_nn(qs[h], c_st[h]) for h in range(HEADS)]))
    den = jnp.sum(s, axis=2, keepdims=True) + w_inter * jnp.sum(jnp.stack(qs) * n_st, axis=2, keepdims=True)
    h_out = num / jnp.maximum(jnp.abs(den), jnp.exp(-m_t))
    a = g - b + ig
    m_new = jnp.maximum(g + m_st, jnp.max(a, axis=1, keepdims=True))
    decay = jnp.exp(g + m_st - m_new)
    wk = jnp.stack(kh) * jnp.exp(a - m_new)
    c_new = decay * c_st + jnp.stack([mm_tn(wk[h], vh[h]) for h in range(HEADS)])
    n_new = decay * n_st + jnp.sum(wk, axis=1, keepdims=True)
    hc = h_out - jnp.mean(h_out, axis=-1, keepdims=True)
    yn = hc * lax.rsqrt(jnp.mean(hc * hc, axis=-1, keepdims=True) + LN_EPS)
    y = _sigmoid(og) * (jnp.concatenate([yn[h] for h in range(HEADS)], axis=1) * nw)
    return c_new, n_new, m_new, y


def _mixer_inputs(proj_ref, lg_ref, hnw_ref, mnw_ref, qk):
    hg_in = (proj_ref[:, _grp(0)], proj_ref[:, _grp(1)], proj_ref[:, _grp(2)], proj_ref[:, _grp(3)],
             lg_ref[0:1, :], lg_ref[1:2, :], hnw_ref[...])
    ml_in = (qk[:, :D_GRP], qk[:, D_GRP:], proj_ref[:, _grp(6)], proj_ref[:, pl.ds(8 * D_GRP, LANES)],
             proj_ref[:, _grp(7)], mnw_ref[...])
    return hg_in, ml_in


def _mixer_fwd(proj, lb_logits, hg_nw, conv_w, conv_b, ml_nw):
    seq = proj.shape[0]
    n_chunks = seq // CHUNK
    proj_spec, halo_spec, small, state_specs, y_spec, _ = _mixer_specs(n_chunks, False)

    def body(proj_ref, halo_ref, lg_ref, hnw_ref, cw_ref, cb_ref, mnw_ref,
             y_ref, hst_ref, cst_ref, nst_ref, mst_ref, hs, cs, ns, ms):
        c = pl.program_id(0)

        @pl.when(c == 0)
        def _():
            hs[...] = jnp.zeros_like(hs)
            cs[...] = jnp.zeros_like(cs)
            ns[...] = jnp.zeros_like(ns)
            ms[...] = jnp.full(ms.shape, NEG_BIG, F32)

        hst_ref[0] = hs[...]
        cst_ref[0] = cs[...]
        nst_ref[0] = ns[...]
        mst_ref[0] = ms[...]
        halo = jnp.where(c > 0, halo_ref[...], 0.0)
        qk = _qk_conv(halo, proj_ref[:, pl.ds(4 * D_GRP, 2 * D_GRP)],
                      cw_ref[0:1, :], cw_ref[1:2, :], cw_ref[2:3, :], cw_ref[3:4, :], cb_ref[...])
        hg_in, ml_in = _mixer_inputs(proj_ref, lg_ref, hnw_ref, mnw_ref, qk)
        hs[...], y_hg = _hg_chunk(hs[...], *hg_in)
        cs[...], ns[...], m_new, y_ml = _ml_chunk(cs[...], ns[...], _last(ms[...], 0), *ml_in)
        ms[...] = jnp.broadcast_to(m_new, ms.shape)
        y_ref[:, pl.ds(0, D_GRP)] = y_hg.astype(BF16)
        y_ref[:, pl.ds(D_GRP, D_GRP)] = y_ml.astype(BF16)

    st = jax.ShapeDtypeStruct((n_chunks, HEADS, DK, DK), F32)
    vec = jax.ShapeDtypeStruct((n_chunks, HEADS, 1, DK), F32)
    vmem = 2 * (_nbytes((CHUNK, D_IN_PAD), F32) + _nbytes((CHUNK, 2 * D_GRP), F32) + 2 * _nbytes((HEADS, DK, DK), F32)) \
        + 2 * _nbytes((HEADS, DK, DK), F32)
    return _pcall(
        body, name="mixer_fwd", grid=(n_chunks,),
        in_specs=[proj_spec, halo_spec] + small,
        out_specs=[y_spec] + state_specs,
        out_shape=[jax.ShapeDtypeStruct((seq, 2 * D_GRP), BF16), st, st, vec, vec],
        scratch_shapes=[pltpu.VMEM((HEADS, DK, DK), F32), pltpu.VMEM((HEADS, DK, DK), F32),
                        pltpu.VMEM((HEADS, 1, DK), F32), pltpu.VMEM((HEADS, 1, DK), F32)],
        compiler_params=_params(("arbitrary",), vmem),
    )(proj, proj, lb_logits, hg_nw, conv_w, conv_b, ml_nw)


def _mixer_bwd(proj, dy, hst, cst, nst, mst, lb_logits, hg_nw, conv_w, conv_b, ml_nw):
    seq = proj.shape[0]
    n_chunks = seq // CHUNK
    proj_spec, halo_spec, small, state_specs, y_spec, _ = _mixer_specs(n_chunks, True)

    def body(proj_ref, halo_ref, dy_ref, hst_ref, cst_ref, nst_ref, mst_ref,
             lg_ref, hnw_ref, cw_ref, cb_ref, mnw_ref,
             dproj_ref, dbin_ref, dlg_ref, dhnw_ref, dcw_ref, dcb_ref, dmnw_ref,
             dhs, dcs, dns, dms, dhalo):
        c = pl.program_id(0)

        @pl.when(c == 0)
        def _():
            for r in (dhs, dcs, dns, dms, dhalo, dbin_ref, dlg_ref, dhnw_ref, dcw_ref, dcb_ref, dmnw_ref):
                r[...] = jnp.zeros_like(r)

        def put(cols, val):
            dproj_ref[:, cols] = val.astype(BF16)
            dbin_ref[:, cols] += jnp.sum(val, axis=0, keepdims=True)

        first = c == n_chunks - 1
        halo = jnp.where(first, 0.0, halo_ref[...])
        x_qk = proj_ref[:, pl.ds(4 * D_GRP, 2 * D_GRP)]
        conv_args = (halo, x_qk, cw_ref[0:1, :], cw_ref[1:2, :], cw_ref[2:3, :], cw_ref[3:4, :], cb_ref[...])
        qk, conv_vjp = jax.vjp(_qk_conv, *conv_args)
        hg_in, ml_in = _mixer_inputs(proj_ref, lg_ref, hnw_ref, mnw_ref, qk)
        _, hg_vjp = jax.vjp(_hg_chunk, hst_ref[0], *hg_in)
        _, ml_vjp = jax.vjp(_ml_chunk, cst_ref[0], nst_ref[0], _last(mst_ref[0], 0), *ml_in)
        dst, dhq, dhf, dhi, dhg, dl0, dl1, dnw = hg_vjp((dhs[...], dy_ref[:, pl.ds(0, D_GRP)]))
        dc, dn, dm, dq, dk, dv, dgates, dog, dmn = ml_vjp(
            (dcs[...], dns[...], _last(dms[...], 0), dy_ref[:, pl.ds(D_GRP, D_GRP)]))
        dhs[...] = dst
        dcs[...] = dc
        dns[...] = dn
        dms[...] = jnp.broadcast_to(dm, dms.shape)
        for i, val in ((0, dhq), (1, dhf), (2, dhi), (3, dhg), (6, dv), (7, dog)):
            put(_grp(i), val)
        put(pl.ds(8 * D_GRP, LANES), dgates)
        dlg_ref[0:1, :] += dl0
        dlg_ref[1:2, :] += dl1
        dhnw_ref[...] += dnw
        dmnw_ref[...] += dmn
        dh, dx, dw0, dw1, dw2, dw3, db = conv_vjp(jnp.concatenate([dq, dk], axis=1))
        tail = jnp.concatenate([jnp.zeros((CHUNK - SUBLANES, 2 * D_GRP), F32), dhalo[...]], axis=0)
        put(pl.ds(4 * D_GRP, 2 * D_GRP), dx + tail)
        dhalo[...] = dh
        for d, dw in enumerate((dw0, dw1, dw2, dw3)):
            dcw_ref[d:d + 1, :] += dw
        dcb_ref[...] += db

    row = pl.BlockSpec((1, D_GRP), lambda c: (0, 0))
    small_out = [pl.BlockSpec((1, D_IN_PAD), lambda c: (0, 0)), pl.BlockSpec((2, D_GRP), lambda c: (0, 0)), row,
                 pl.BlockSpec((ML_CONV, 2 * D_GRP), lambda c: (0, 0)), pl.BlockSpec((1, 2 * D_GRP), lambda c: (0, 0)), row]
    dy_spec = pl.BlockSpec((CHUNK, 2 * D_GRP), y_spec.index_map)
    vmem = 2 * (2 * _nbytes((CHUNK, D_IN_PAD), F32) + _nbytes((CHUNK, 2 * D_GRP), F32)
                + 2 * _nbytes((HEADS, DK, DK), F32)) + 2 * _nbytes((HEADS, DK, DK), F32) + 4 * 1024 * 1024
    return _pcall(
        body, name="mixer_bwd", grid=(n_chunks,),
        in_specs=[proj_spec, halo_spec, dy_spec] + state_specs + small,
        out_specs=[proj_spec] + small_out,
        out_shape=[jax.ShapeDtypeStruct((seq, D_IN_PAD), BF16), jax.ShapeDtypeStruct((1, D_IN_PAD), F32),
                   jax.ShapeDtypeStruct((2, D_GRP), F32), jax.ShapeDtypeStruct((1, D_GRP), F32),
                   jax.ShapeDtypeStruct((ML_CONV, 2 * D_GRP), F32), jax.ShapeDtypeStruct((1, 2 * D_GRP), F32),
                   jax.ShapeDtypeStruct((1, D_GRP), F32)],
        scratch_shapes=[pltpu.VMEM((HEADS, DK, DK), F32), pltpu.VMEM((HEADS, DK, DK), F32),
                        pltpu.VMEM((HEADS, 1, DK), F32), pltpu.VMEM((HEADS, 1, DK), F32),
                        pltpu.VMEM((SUBLANES, 2 * D_GRP), F32)],
        compiler_params=_params(("arbitrary",), vmem),
    )(proj, proj, dy, hst, cst, nst, mst, lb_logits, hg_nw, conv_w, conv_b, ml_nw)


def _tile(n, prefs, unit=None):
    unit = unit or n
    for p in prefs:
        if unit % p == 0 and n % p == 0:
            return p
    return unit


def _logical(arr):
    return arr.shape if arr.ndim == 2 else (arr.shape[1], arr.shape[0] * arr.shape[2])


def _group(arr):
    return arr.shape[-1]


def _split_spec(ndim, group, tr, tc, where):
    if ndim == 2:
        return pl.BlockSpec((tr, tc), where)
    per = group // tc
    assert per * tc == group, (group, tc)

    def index(*ids):
        bi, bj = where(*ids)
        return (bj // per, bi, bj % per)
    return pl.BlockSpec((None, tr, tc), index)


def _mm(name, mode, a, b, *, bias=None, res=None, res_scale=1.0, ln=None, out_dtype=F32, out_groups=None,
        copy_dtype=None, a_copy_dtype=None, tm=None, tn=None, tk=None):
    la, lb = _logical(a), _logical(b)
    if mode == "nn":
        (m, k), n = la, lb[1]
        n_unit = _group(b) if b.ndim == 3 else n
        kc = _group(a) if a.ndim == 3 else k
    elif mode == "nt":
        (m, k), n = la, lb[0]
        n_unit = n
        kc = min(_group(a) if a.ndim == 3 else k, _group(b) if b.ndim == 3 else k)
    else:
        (k, m), n = la, lb[1]
        n_unit, kc = (_group(b) if b.ndim == 3 else n), k
        assert a.ndim == 2
    if out_groups:
        n_unit = min(n_unit, n // out_groups)
    kind = ln[0] if ln else None
    tm = tm or (256 if ln else _tile(m, (512, 256, 128)))
    tn = n if ln else (tn or _tile(n, (512, 384, 256, 128), n_unit))
    if mode != "tn":
        tk = k
    elif tk is None:
        tk = _tile(k, (4096, 2048, 512, 256, 128) if (m // tm) * (n // tn) > 1 else (2048, 512, 256, 128))
    gi, gj, gk = m // tm, n // tn, k // tk
    assert gi * tm == m and gj * tn == n and gk * tk == k and n_unit % tn == 0, (name, m, n, k, tm, tn, tk)
    ca, cb = {"nn": (1, 0), "nt": (1, 1), "tn": (0, 0)}[mode]
    i_outer = gk > 1 or (gi - 1) * _nbytes(b.shape, b.dtype) <= (gj - 1) * _nbytes(a.shape, a.dtype)

    def ij(where):
        return (lambda p, q, kk: where(p, q, kk)) if i_outer else (lambda p, q, kk: where(q, p, kk))
    if mode == "tn":
        a_spec = pl.BlockSpec((tk, tm), ij(lambda i, j, kk: (kk, i)))
    elif a.ndim == 3:
        a_spec = pl.BlockSpec((a.shape[0], tm, _group(a)), ij(lambda i, j, kk: (0, i, 0)))
    else:
        a_spec = pl.BlockSpec((tm, k), ij(lambda i, j, kk: (i, 0)))
    if mode != "nt":
        b_spec = _split_spec(b.ndim, _group(b), tk, tn, ij(lambda i, j, kk: (kk, j)))
    elif b.ndim == 3:
        b_spec = pl.BlockSpec((b.shape[0], tn, _group(b)), ij(lambda i, j, kk: (0, j, 0)))
    else:
        b_spec = pl.BlockSpec((tn, k), ij(lambda i, j, kk: (j, 0)))
    row_spec = pl.BlockSpec((1, tn), ij(lambda i, j, kk: (0, j)))
    blk_spec = pl.BlockSpec((tm, tn), ij(lambda i, j, kk: (i, j)))
    ins, in_specs = [a, b], [a_spec, b_spec]
    if bias is not None:
        ins.append(bias), in_specs.append(row_spec)
    if res is not None:
        ins.append(res), in_specs.append(blk_spec)
    if kind == "fwd":
        ins += [ln[1], ln[2]]
        in_specs += [row_spec, row_spec]
    elif kind == "bwd":
        ins += [ln[1], ln[2], ln[3]]
        in_specs += [blk_spec, row_spec, row_spec]
    if out_groups:
        blk_out = jax.ShapeDtypeStruct((out_groups, m, n // out_groups), out_dtype)
        out_spec = _split_spec(3, n // out_groups, tm, tn, ij(lambda i, j, kk: (i, j)))
    else:
        blk_out, out_spec = jax.ShapeDtypeStruct((m, n), out_dtype), blk_spec
    row_out = jax.ShapeDtypeStruct((1, n), F32)
    if kind is None:
        out_shape, out_specs = [blk_out], [out_spec]
    elif kind == "fwd":
        out_shape, out_specs = [blk_out, blk_out], [blk_spec, blk_spec]
    else:
        out_shape, out_specs = [blk_out, row_out, row_out], [blk_spec, row_spec, row_spec]
    if copy_dtype is not None:
        out_shape.append(jax.ShapeDtypeStruct((m, n), copy_dtype))
        out_specs.append(blk_spec)
    if a_copy_dtype is not None:
        assert mode != "tn" and a.ndim == 2 and copy_dtype is None
        out_shape.append(jax.ShapeDtypeStruct((m, k), a_copy_dtype))
        out_specs.append(a_spec)
    n_in = len(ins)

    def body(*refs):
        in_refs, out_refs, acc_ref = refs[:n_in], refs[n_in:n_in + len(out_shape)], refs[-1]
        i, kk = pl.program_id(0 if i_outer else 1), pl.program_id(2)
        a_ref, b_ref = in_refs[:2]
        extra = list(in_refs[2:])
        if a_copy_dtype is not None:
            out_refs[-1][...] = a_ref[...].astype(a_copy_dtype)

        def epilogue(acc):
            rest = list(extra)
            if bias is not None:
                acc = acc + rest.pop(0)[...]
            if res is not None:
                acc = acc + res_scale * rest.pop(0)[...]
            if kind is None:
                out_refs[0][...] = acc.astype(out_dtype)
                return
            if kind == "fwd":
                out_refs[0][...] = acc
                y = _layer_norm(acc, rest[0][...], rest[1][...])
                out_refs[1][...] = y
                if copy_dtype is not None:
                    out_refs[-1][...] = y.astype(copy_dtype)
                return
            _, vjp = jax.vjp(_layer_norm, rest[0][...], rest[1][...], rest[2][...])
            dz, dg, db = vjp(acc)
            out_refs[0][...] = dz
            out_refs[1][...] += dg
            out_refs[2][...] += db
            if copy_dtype is not None:
                out_refs[-1][...] = dz.astype(copy_dtype)

        if kind == "bwd":
            @pl.when((i == 0) & (kk == 0))
            def _():
                out_refs[1][...] = jnp.zeros_like(out_refs[1])
                out_refs[2][...] = jnp.zeros_like(out_refs[2])

        def chunk(ref, c0, last):
            if ref.ndim == 3:
                g = ref.shape[2]
                return ref[c0 // g, :, pl.ds(c0 % g, kc)]
            return ref[:, pl.ds(c0, kc)] if last else ref[pl.ds(c0, kc), :]

        if mode == "tn" or kc == k:
            prod = _dg(a_ref[...], b_ref[...], ca, cb)
        else:
            prod = None
            for c0 in range(0, k, kc):
                part = _dg(chunk(a_ref, c0, True), chunk(b_ref, c0, mode == "nt"), ca, cb)
                prod = part if prod is None else prod + part
        if gk == 1:
            epilogue(prod)
            return

        @pl.when(kk == 0)
        def _():
            acc_ref[...] = prod

        @pl.when(kk > 0)
        def _():
            acc_ref[...] += prod

        @pl.when(kk == gk - 1)
        def _():
            epilogue(acc_ref[...])

    vmem = (2 * (_nbytes((tm, tk), a.dtype) + _nbytes((tk, tn), b.dtype))
            + (2 * len(ins) + 2 * len(out_shape) + 1) * _nbytes((tm, tn), F32))
    outs = _pcall(
        body, name=name, grid=(gi, gj, gk) if i_outer else (gj, gi, gk), in_specs=in_specs, out_specs=out_specs,
        out_shape=out_shape, scratch_shapes=[pltpu.VMEM((tm, tn) if gk > 1 else (SUBLANES, LANES), F32)],
        compiler_params=_params(("arbitrary", "arbitrary", "arbitrary"), vmem),
    )(*ins)
    return outs[0] if len(out_shape) == 1 else outs


def _attn_head(q, k, v):
    sc = mm_nt(q, k) * (CA_DH ** -0.5)
    e = jnp.exp(sc - jnp.max(sc, axis=-1, keepdims=True))
    return mm_nn(e / jnp.sum(e, axis=-1, keepdims=True), v)


def _attn_fwd(q, kv):
    seq, n_mem = q.shape[0], kv.shape[0]
    tq = _tile(seq, (512, 256, 128))

    def body(q_ref, kv_ref, o_ref):
        for h in range(HEADS):
            hd = pl.ds(h * CA_DH, CA_DH)
            o = _attn_head(q_ref[:, hd], kv_ref[:, hd], kv_ref[:, pl.ds(D_MODEL + h * CA_DH, CA_DH)])
            o_ref[:, hd] = o.astype(BF16)

    return _pcall(
        body, name="attn_fwd", grid=(seq // tq,),
        in_specs=[pl.BlockSpec((tq, D_MODEL), lambda i: (i, 0)), pl.BlockSpec((n_mem, 2 * D_MODEL), lambda i: (0, 0))],
        out_specs=pl.BlockSpec((tq, D_MODEL), lambda i: (i, 0)), out_shape=jax.ShapeDtypeStruct((seq, D_MODEL), BF16),
        compiler_params=_params(("arbitrary",), 4 * _nbytes((tq, D_MODEL), F32) + 2 * _nbytes((n_mem, 2 * D_MODEL), F32)),
    )(q, kv)


def _attn_bwd(q, kv, do):
    seq, n_mem = q.shape[0], kv.shape[0]
    tq = _tile(seq, (512, 256, 128))

    def body(q_ref, kv_ref, do_ref, dq_ref, dkv_ref):
        @pl.when(pl.program_id(0) == 0)
        def _():
            dkv_ref[...] = jnp.zeros_like(dkv_ref)

        for h in range(HEADS):
            hd = pl.ds(h * CA_DH, CA_DH)
            vd = pl.ds(D_MODEL + h * CA_DH, CA_DH)
            _, vjp = jax.vjp(_attn_head, q_ref[:, hd], kv_ref[:, hd], kv_ref[:, vd])
            dq, dk, dv = vjp(do_ref[:, hd].astype(F32))
            dq_ref[:, hd] = dq.astype(BF16)
            dkv_ref[:, hd] += dk
            dkv_ref[:, vd] += dv

    return _pcall(
        body, name="attn_bwd", grid=(seq // tq,),
        in_specs=[pl.BlockSpec((tq, D_MODEL), lambda i: (i, 0)), pl.BlockSpec((n_mem, 2 * D_MODEL), lambda i: (0, 0)),
                  pl.BlockSpec((tq, D_MODEL), lambda i: (i, 0))],
        out_specs=[pl.BlockSpec((tq, D_MODEL), lambda i: (i, 0)), pl.BlockSpec((n_mem, 2 * D_MODEL), lambda i: (0, 0))],
        out_shape=[jax.ShapeDtypeStruct((seq, D_MODEL), BF16), jax.ShapeDtypeStruct((n_mem, 2 * D_MODEL), F32)],
        compiler_params=_params(("arbitrary",), 6 * _nbytes((tq, D_MODEL), F32) + 4 * _nbytes((n_mem, 2 * D_MODEL), F32)),
    )(q, kv, do)


def _ffn_mid(hg, xg, hv, xv, wg0, wg1, wg2, bg, wv0, wv1, wv2, bv):
    return jax.nn.gelu(causal_conv(hg, xg, (wg0, wg1, wg2), bg)) * causal_conv(hv, xv, (wv0, wv1, wv2), bv)


FFN_TB = 256
FFN_W = D_FF // 2
FFN_J = D_FF // FFN_W
MXU_COLS = 256
FFN_PIECES = tuple((off, min(MXU_COLS, FFN_W - off)) for off in range(0, FFN_W, MXU_COLS))


def _ffn_common_specs(seq, row):
    tb = min(FFN_TB, seq)
    full = pl.BlockSpec((tb, D_MODEL), lambda t, j: (row(t), 0))
    vec = pl.BlockSpec((1, D_MODEL), lambda t, j: (0, 0))
    halves = []
    for off in (0, FFN_J):
        halves.append(dict(
            w_up=pl.BlockSpec((None, D_MODEL, FFN_W), lambda t, j, off=off: (j + off, 0, 0)),
            taps=pl.BlockSpec((FFN_CONV, FFN_W), lambda t, j, off=off: (0, j + off)),
            bias=pl.BlockSpec((1, FFN_W), lambda t, j, off=off: (0, j + off))))
    w_down = pl.BlockSpec((FFN_W, D_MODEL), lambda t, j: (j, 0))
    u_blk = pl.BlockSpec((2, tb, FFN_W), lambda t, j: (0, row(t), j))
    return tb, full, vec, halves, w_down, u_blk


def _ffn_vmem(tb):
    return (_nbytes((2, tb, FFN_W), F32) + _nbytes((2, tb, FFN_W), BF16) + 3 * _nbytes((D_MODEL, FFN_W), BF16)
            + 10 * _nbytes((tb, D_MODEL), F32))


def _conv_params(taps_ref, bias_ref, cols):
    return taps_ref[0:1, cols], taps_ref[1:2, cols], taps_ref[2:3, cols], bias_ref[:, cols]


def _ffn_fwd(x2b, x2, w_up, conv_w, conv_b, w_down, ln_g, ln_b, target):
    seq = x2.shape[0]
    tb, full, vec, halves, wd_spec, u_blk = _ffn_common_specs(seq, lambda t: t)
    nt = seq // tb

    def body(xb_ref, wg_ref, wv_ref, tg_ref, tv_ref, bg_ref, bv_ref, wd_ref, x_ref, g_ref, b_ref, tgt_ref,
             u_ref, h_ref, dz_ref, dg_ref, db_ref, loss_ref, dzb_ref, acc, carry):
        t, j = pl.program_id(0), pl.program_id(1)
        xb = xb_ref[...]
        pieces = [pl.ds(off, width) for off, width in FFN_PIECES]
        ug = [_dg(xb, wg_ref[:, cols], 1, 0) for cols in pieces]
        uv = [_dg(xb, wv_ref[:, cols], 1, 0) for cols in pieces]
        hs = []
        for cols, g, v in zip(pieces, ug, uv):
            u_ref[0, :, cols] = g
            u_ref[1, :, cols] = v
            halo_g = jnp.where(t == 0, 0.0, carry[j, 0, :, cols])
            halo_v = jnp.where(t == 0, 0.0, carry[j, 1, :, cols])
            h = _ffn_mid(halo_g, g, halo_v, v, *_conv_params(tg_ref, bg_ref, cols),
                         *_conv_params(tv_ref, bv_ref, cols)).astype(BF16)
            carry[j, 0, :, cols] = g[tb - SUBLANES:, :]
            carry[j, 1, :, cols] = v[tb - SUBLANES:, :]
            h_ref[:, cols] = h
            hs.append(h)
        part = None
        for cols, h in zip(pieces, hs):
            p = _dg(h, wd_ref[cols, :], 1, 0)
            part = p if part is None else part + p

        @pl.when(j == 0)
        def _():
            acc[...] = part

        @pl.when(j > 0)
        def _():
            acc[...] += part

        @pl.when(j == FFN_J - 1)
        def _():
            y, vjp = jax.vjp(_layer_norm, acc[...] + ALPHA * x_ref[...], g_ref[...], b_ref[...])
            err = y - tgt_ref[...]
            part_loss = 0.5 * jnp.sum(jnp.sum(err * err, axis=1, keepdims=True), axis=0, keepdims=True) / D_MODEL
            dz, dg, db = vjp(err / D_MODEL)

            @pl.when(t == 0)
            def _():
                for r in (dg_ref, db_ref, loss_ref):
                    r[...] = jnp.zeros_like(r)

            dz_ref[...] = dz
            dzb_ref[...] = dz.astype(BF16)
            dg_ref[...] += dg
            db_ref[...] += db
            loss_ref[...] += jnp.broadcast_to(part_loss, (1, LANES))

    h0, h1 = halves
    row = jax.ShapeDtypeStruct((1, D_MODEL), F32)
    return _pcall(
        body, name="ffn_fwd", grid=(nt, FFN_J),
        in_specs=[full, h0["w_up"], h1["w_up"], h0["taps"], h1["taps"], h0["bias"], h1["bias"], wd_spec, full, vec, vec,
                  full],
        out_specs=[u_blk, pl.BlockSpec((tb, FFN_W), lambda t, j: (t, j)), full, vec, vec,
                   pl.BlockSpec((1, LANES), lambda t, j: (0, 0)), full],
        out_shape=[jax.ShapeDtypeStruct((2, seq, D_FF), F32), jax.ShapeDtypeStruct((seq, D_FF), BF16),
                   jax.ShapeDtypeStruct((seq, D_MODEL), F32), row, row, jax.ShapeDtypeStruct((1, LANES), F32),
                   jax.ShapeDtypeStruct((seq, D_MODEL), BF16)],
        scratch_shapes=[pltpu.VMEM((tb, D_MODEL), F32), pltpu.VMEM((FFN_J, 2, SUBLANES, FFN_W), F32)],
        compiler_params=_params(("arbitrary", "arbitrary"), _ffn_vmem(tb)),
    )(x2b, w_up, w_up, conv_w, conv_w, conv_b, conv_b, w_down, x2, ln_g, ln_b, target)


def _ffn_bwd(u, conv_w, conv_b, dz3b, dz3, w_down, w_up, z2, ln_g, ln_b):
    seq = dz3.shape[0]
    tb = min(FFN_TB, seq)
    nt = seq // tb
    row8 = tb // SUBLANES
    tb, full, vec, halves, wd_spec, u_blk = _ffn_common_specs(seq, lambda t: nt - 1 - t)
    halo = pl.BlockSpec((2, SUBLANES, FFN_W), lambda t, j: (0, jnp.maximum((nt - 1 - t) * row8 - 1, 0), j))

    def body(u_ref, halo_ref, tg_ref, tv_ref, bg_ref, bv_ref, dzb_ref, wd_ref, wg_ref, wv_ref, dz3_ref, z_ref, g_ref,
             b_ref, du_ref, dw_ref, dbias_ref, dz_ref, dg_ref, db_ref, dz2b_ref, acc, carry):
        t, j = pl.program_id(0), pl.program_id(1)

        @pl.when((t == 0) & (j == 0))
        def _():
            for r in (dw_ref, dbias_ref, dg_ref, db_ref):
                r[...] = jnp.zeros_like(r)

        pieces = [pl.ds(off, width) for off, width in FFN_PIECES]
        dzb = dzb_ref[...]
        dhs = [_dg(dzb, wd_ref[cols, :], 1, 1) for cols in pieces]
        first = t == nt - 1
        dus = []
        for cols, dh in zip(pieces, dhs):
            args = (jnp.where(first, 0.0, halo_ref[0, :, cols]), u_ref[0, :, cols],
                    jnp.where(first, 0.0, halo_ref[1, :, cols]), u_ref[1, :, cols],
                    *_conv_params(tg_ref, bg_ref, cols), *_conv_params(tv_ref, bv_ref, cols))
            _, vjp = jax.vjp(_ffn_mid, *args)
            dhg, dxg, dhv, dxv, g0, g1, g2, gb, v0, v1, v2, vb = vjp(dh)
            zeros = jnp.zeros((tb - SUBLANES, dh.shape[1]), F32)
            dug = (dxg + jnp.concatenate([zeros, jnp.where(t == 0, 0.0, carry[j, 0, :, cols])], axis=0)).astype(BF16)
            duv = (dxv + jnp.concatenate([zeros, jnp.where(t == 0, 0.0, carry[j, 1, :, cols])], axis=0)).astype(BF16)
            carry[j, 0, :, cols] = dhg
            carry[j, 1, :, cols] = dhv
            du_ref[0, :, cols] = dug
            du_ref[1, :, cols] = duv
            for half, parts in enumerate(((g0, g1, g2), (v0, v1, v2))):
                for d, p in enumerate(parts):
                    dw_ref[j, half, d:d + 1, cols] += p
            dbias_ref[j, 0, :, cols] += gb
            dbias_ref[j, 1, :, cols] += vb
            dus.append((dug, duv))
        part = None
        for cols, (dug, duv) in zip(pieces, dus):
            p = _dg(dug, wg_ref[:, cols], 1, 1) + _dg(duv, wv_ref[:, cols], 1, 1)
            part = p if part is None else part + p

        @pl.when(j == 0)
        def _():
            acc[...] = part

        @pl.when(j > 0)
        def _():
            acc[...] += part

        @pl.when(j == FFN_J - 1)
        def _():
            _, ln_vjp = jax.vjp(_layer_norm, z_ref[...], g_ref[...], b_ref[...])
            dz, dg, db = ln_vjp(acc[...] + ALPHA * dz3_ref[...])
            dz_ref[...] = dz
            dz2b_ref[...] = dz.astype(BF16)
            dg_ref[...] += dg
            db_ref[...] += db

    h0, h1 = halves
    row = jax.ShapeDtypeStruct((1, D_MODEL), F32)
    whole = lambda *shape: pl.BlockSpec(shape, lambda t, j: (0,) * len(shape))
    return _pcall(
        body, name="ffn_bwd", grid=(nt, FFN_J),
        in_specs=[u_blk, halo, h0["taps"], h1["taps"], h0["bias"], h1["bias"], full, wd_spec, h0["w_up"], h1["w_up"],
                  full, full, vec, vec],
        out_specs=[u_blk, whole(FFN_J, 2, FFN_CONV, FFN_W), whole(FFN_J, 2, 1, FFN_W), full, vec, vec, full],
        out_shape=[jax.ShapeDtypeStruct((2, seq, D_FF), BF16), jax.ShapeDtypeStruct((FFN_J, 2, FFN_CONV, FFN_W), F32),
                   jax.ShapeDtypeStruct((FFN_J, 2, 1, FFN_W), F32), jax.ShapeDtypeStruct((seq, D_MODEL), F32), row, row,
                   jax.ShapeDtypeStruct((seq, D_MODEL), BF16)],
        scratch_shapes=[pltpu.VMEM((tb, D_MODEL), F32), pltpu.VMEM((FFN_J, 2, SUBLANES, FFN_W), F32)],
        compiler_params=_params(("arbitrary", "arbitrary"), _ffn_vmem(tb)),
    )(u, u, conv_w, conv_w, conv_b, conv_b, dz3b, w_down, w_up, w_up, dz3, z2, ln_g, ln_b)


def _adamw_math(w, g, m, v):
    m_new = ADAM_B1 * m + (1.0 - ADAM_B1) * g
    v_new = ADAM_B2 * v + (1.0 - ADAM_B2) * jnp.square(g)
    m_hat = m_new / (1.0 - ADAM_B1 ** ADAM_STEP)
    v_hat = v_new / (1.0 - ADAM_B2 ** ADAM_STEP)
    return -ADAM_LR * (m_hat / (jnp.sqrt(v_hat) + ADAM_EPS) + ADAM_WD * w), m_new, v_new


def _adamw_many(name, ws, gs, ms, vs):
    n = len(ws)

    def body(*refs):
        w_refs, g_refs, m_refs, v_refs = (refs[i * n:(i + 1) * n] for i in range(4))
        d_refs, nm_refs, nv_refs = (refs[(4 + i) * n:(5 + i) * n] for i in range(3))
        for i in range(n):
            d_refs[i][...], nm_refs[i][...], nv_refs[i][...] = _adamw_math(
                w_refs[i][...], g_refs[i][...], m_refs[i][...], v_refs[i][...])

    vm = pl.BlockSpec(memory_space=pltpu.VMEM)
    outs = _pcall(
        body, pin=False, name=name, in_specs=[vm] * (4 * n), out_specs=[vm] * (3 * n),
        out_shape=[jax.ShapeDtypeStruct(w.shape, F32) for w in ws] * 3,
    )(*ws, *gs, *ms, *vs)
    return outs[:n], outs[n:2 * n], outs[2 * n:]


def _adamw_halves(name, core, w, mine, theirs, m, v):
    rows, cols = w.shape
    half_rows = mine.shape[0]
    tr = _tile(half_rows, (256, 176, 128))
    nbh = half_rows // tr
    assert 2 * half_rows == rows

    def body(c_ref, w_ref, a_ref, b_ref, m_ref, v_ref, g_ref, d_ref, nm_ref, nv_ref):
        g = jnp.where(pl.program_id(0) // nbh == c_ref[0], a_ref[...], b_ref[...])
        g_ref[...] = g
        d_ref[...], nm_ref[...], nv_ref[...] = _adamw_math(w_ref[...], g, m_ref[...], v_ref[...])

    spec = pl.BlockSpec((tr, cols), lambda i, c_ref: (i, 0))
    half = pl.BlockSpec((tr, cols), lambda i, c_ref: (i % nbh, 0))
    sh = jax.ShapeDtypeStruct((rows, cols), F32)
    grid_spec = pltpu.PrefetchScalarGridSpec(
        num_scalar_prefetch=1, grid=(rows // tr,), in_specs=[spec, half, half, spec, spec], out_specs=[spec] * 4)
    return _pcall(
        body, name=name, grid_spec=grid_spec, out_shape=[sh] * 4,
        compiler_params=_params(("arbitrary",), 18 * _nbytes((tr, -(-cols // LANES) * LANES), F32)),
    )(core, w, mine, theirs, m, v)


MESH = pl.DeviceIdType.MESH
ANY = pl.BlockSpec(memory_space=pl.ANY)
N_CHIPS = 4
BF16_ROWS = 16


def _me():
    return lax.axis_index("x"), lax.axis_index("y"), lax.axis_index("c")


def _other_chips(x, y):
    return [(1 - x, y), (x, 1 - y), (1 - x, 1 - y)]


def _remote(src, dst, ssem, rsem, dev):
    return pltpu.make_async_remote_copy(src_ref=src, dst_ref=dst, send_sem=ssem, recv_sem=rsem,
                                        device_id=dev, device_id_type=MESH)


def _half_rows(ref_rows, cc):
    half = ref_rows // 2
    return pl.ds(pl.multiple_of(cc * half, BF16_ROWS), half)


def _gather_weights(shards):
    n = len(shards)
    n_ici = n * (N_CHIPS - 1)

    def body(*refs):
        ins, outs, (ssem, rsem, lsem, lrsem) = refs[:n], refs[n:2 * n], refs[2 * n:]
        x, y, c = _me()
        k_me = 2 * x + y
        sib = (x, y, 1 - c)
        chips = _other_chips(x, y)
        started = []
        for i, (w_ref, o_ref) in enumerate(zip(ins, outs)):
            cp = _remote(w_ref, o_ref.at[k_me], lsem.at[i], lrsem.at[i], sib)
            cp.start()
            started.append(cp)
        for r, (px, py) in enumerate(chips):
            for i, (w_ref, o_ref) in enumerate(zip(ins, outs)):
                rows = _half_rows(w_ref.shape[0], c)
                s = r * n + i
                cp = _remote(w_ref.at[rows], o_ref.at[k_me, rows], ssem.at[s], rsem.at[s], (px, py, c))
                cp.start()
                started.append(cp)
        for r, (px, py) in enumerate(chips):
            for i, o_ref in enumerate(outs):
                blk = o_ref.at[2 * px + py, _half_rows(o_ref.shape[1], c)]
                s = r * n + i
                _remote(blk, blk, ssem.at[s], rsem.at[s], (px, py, c)).wait_recv()
                cp = _remote(blk, blk, ssem.at[n_ici + s], rsem.at[n_ici + s], sib)
                cp.start()
                started.append(cp)
        for r, (px, py) in enumerate(chips):
            for i, o_ref in enumerate(outs):
                blk = o_ref.at[2 * px + py, _half_rows(o_ref.shape[1], 1 - c)]
                s = n_ici + r * n + i
                _remote(blk, blk, ssem.at[s], rsem.at[s], sib).wait_recv()
        for cp in started[n:]:
            cp.wait_send()
        for cp in started[:n]:
            cp.wait()

    return _pcall(
        body, name="gather_weights", in_specs=[ANY] * n, out_specs=[ANY] * n,
        out_shape=[jax.ShapeDtypeStruct((N_CHIPS,) + s.shape, s.dtype) for s in shards],
        scratch_shapes=[pltpu.SemaphoreType.DMA((2 * n_ici,)), pltpu.SemaphoreType.DMA((2 * n_ici,)),
                        pltpu.SemaphoreType.DMA((n,)), pltpu.SemaphoreType.DMA((n,))],
    )(*shards)


def _swap_halves(name, grads):
    n = len(grads)

    def body(*refs):
        ins, outs, (ssem, rsem) = refs[:n], refs[n:2 * n], refs[2 * n:]
        x, y, c = _me()
        copies = []
        for i, (g_ref, o_ref) in enumerate(zip(ins, outs)):
            for k in range(N_CHIPS):
                s = i * N_CHIPS + k
                cp = _remote(g_ref.at[k, _half_rows(g_ref.shape[1], 1 - c)], o_ref.at[k], ssem.at[s], rsem.at[s],
                             (x, y, 1 - c))
                cp.start()
                copies.append(cp)
        for cp in copies:
            cp.wait()

    return _pcall(
        body, name=name, in_specs=[ANY] * n, out_specs=[ANY] * n,
        out_shape=[jax.ShapeDtypeStruct((N_CHIPS, g.shape[1] // 2, g.shape[2]), g.dtype) for g in grads],
        scratch_shapes=[pltpu.SemaphoreType.DMA((n * N_CHIPS,)), pltpu.SemaphoreType.DMA((n * N_CHIPS,))],
    )(*grads)


SEM = pl.BlockSpec(memory_space=pltpu.SEMAPHORE)
IN_HBM = pl.BlockSpec(memory_space=pltpu.HBM)
SPLIT_PARAMS = dict(compiler_params=pltpu.CompilerParams(has_side_effects=pltpu.SideEffectType.DATAFLOW_SIDE_EFFECTING))


def _split_start(name, sources, landings, n_copies, plan):
    ns, nl = len(sources), len(landings)

    def body(*refs):
        ins, lands, (ssem, rsem), token = refs[:ns], refs[ns:ns + nl], refs[ns + nl:ns + nl + 2], refs[-1]
        for s, (src, dst, _, dev) in enumerate(plan(ins, lands)):
            _remote(src, dst, ssem.at[s], rsem.at[s], dev).start()
        token[...] = jnp.zeros_like(token)

    arrays = list(sources) + list(landings)
    outs = _call(
        body, name=name, in_specs=[IN_HBM] * (ns + nl),
        out_specs=[SEM, SEM] + [IN_HBM] * (ns + nl) + [pl.BlockSpec(memory_space=pltpu.VMEM)],
        out_shape=[pltpu.SemaphoreType.DMA((n_copies,)), pltpu.SemaphoreType.DMA((n_copies,))]
        + [pltpu.HBM(a.shape, a.dtype) for a in arrays] + [jax.ShapeDtypeStruct((SUBLANES, LANES), F32)],
        input_output_aliases={i: 2 + i for i in range(ns + nl)}, **SPLIT_PARAMS,
    )(*[pltpu.with_memory_space_constraint(a, pltpu.HBM) for a in arrays])
    return (outs[:-1], ns), outs[-1]


def _split_wait(name, handle, after, plan):
    (ssem, rsem, *thru), ns = handle
    nl = len(thru) - ns

    def body(*refs):
        ins, lands, (ssem_ref, rsem_ref) = refs[:ns], refs[ns:ns + nl], refs[ns + nl:ns + nl + 2]
        for s, (src, _, dst, dev) in enumerate(plan(ins, lands)):
            cp = _remote(src, dst, ssem_ref.at[s], rsem_ref.at[s], dev)
            cp.wait_send()
            cp.wait_recv()

    outs = _call(
        body, name=name, in_specs=[IN_HBM] * (ns + nl) + [SEM, SEM, ANY], out_specs=[IN_HBM] * (ns + nl),
        out_shape=[pltpu.HBM(t.shape, t.dtype) for t in thru],
        input_output_aliases={i: i for i in range(ns + nl)}, **SPLIT_PARAMS,
    )(*thru, ssem, rsem, after)
    return outs[:ns], outs[ns:]


def _swap_plan(ins, lands):
    x, y, c = _me()
    return [(g_ref.at[k, _half_rows(g_ref.shape[1], 1 - c)], l_ref.at[k], l_ref.at[k], (x, y, 1 - c))
            for g_ref, l_ref in zip(ins, lands) for k in range(N_CHIPS)]


def _swap_start(name, grads):
    lands = [lax.empty((N_CHIPS, g.shape[1] // 2, g.shape[2]), g.dtype) for g in grads]
    return _split_start(name, grads, lands, len(grads) * N_CHIPS, _swap_plan)


def _swap_wait(name, handle, after):
    return _split_wait(name, handle, after, _swap_plan)


def _gather_plan(ins, lands):
    x, y, c = _me()
    k_me = 2 * x + y
    plan = [(w_ref, l_ref.at[k_me], l_ref.at[k_me], (x, y, 1 - c)) for w_ref, l_ref in zip(ins, lands)]
    for px, py in _other_chips(x, y):
        for w_ref, l_ref in zip(ins, lands):
            rows = _half_rows(w_ref.shape[0], c)
            plan.append((w_ref.at[rows], l_ref.at[k_me, rows], l_ref.at[2 * px + py, rows], (px, py, c)))
    return plan


def _gather_start(name, shards):
    lands = [lax.empty((N_CHIPS,) + s.shape, s.dtype) for s in shards]
    return _split_start(name, shards, lands, len(shards) * N_CHIPS, _gather_plan)


def _gather_wait(name, handle, after):
    return _split_wait(name, handle, after, _gather_plan)[1]


def _forward_halves(name, blocks):
    n = len(blocks)
    n_sem = n * (N_CHIPS - 1)

    def body(*refs):
        outs, (ssem, rsem) = refs[n:2 * n], refs[2 * n:]
        x, y, c = _me()
        sib = (x, y, 1 - c)
        chips = _other_chips(x, y)
        sends = []
        for r, (px, py) in enumerate(chips):
            for i, o_ref in enumerate(outs):
                blk = o_ref.at[2 * px + py, _half_rows(o_ref.shape[1], c)]
                cp = _remote(blk, blk, ssem.at[r * n + i], rsem.at[r * n + i], sib)
                cp.start()
                sends.append(cp)
        for r, (px, py) in enumerate(chips):
            for i, o_ref in enumerate(outs):
                blk = o_ref.at[2 * px + py, _half_rows(o_ref.shape[1], 1 - c)]
                _remote(blk, blk, ssem.at[r * n + i], rsem.at[r * n + i], sib).wait_recv()
        for cp in sends:
            cp.wait_send()

    return _pcall(
        body, name=name, in_specs=[ANY] * n, out_specs=[ANY] * n,
        out_shape=[jax.ShapeDtypeStruct(b.shape, b.dtype) for b in blocks],
        input_output_aliases={i: i for i in range(n)},
        scratch_shapes=[pltpu.SemaphoreType.DMA((n_sem,)), pltpu.SemaphoreType.DMA((n_sem,))],
    )(*blocks)


def _scatter_plan(ins, lands):
    x, y, c = _me()
    k_me = 2 * x + y
    return [(p_ref.at[2 * px + py], l_ref.at[k_me], l_ref.at[2 * px + py], (px, py, c))
            for px, py in _other_chips(x, y) for p_ref, l_ref in zip(ins, lands)]


def _scatter_start(name, parts):
    lands = [lax.empty(p.shape, p.dtype) for p in parts]
    return _split_start(name, parts, lands, len(parts) * (N_CHIPS - 1), _scatter_plan)


def _scatter_wait(name, handle, after):
    return _split_wait(name, handle, after, _scatter_plan)[1]


def _share_halves(halves):
    n = len(halves)

    def body(*refs):
        ins, outs, (ssem, rsem) = refs[:n], refs[n:2 * n], refs[2 * n:]
        x, y, c = _me()
        copies = [_remote(r_ref, o_ref, ssem.at[i], rsem.at[i], (x, y, 1 - c))
                  for i, (r_ref, o_ref) in enumerate(zip(ins, outs))]
        for cp in copies:
            cp.start()
        for cp in copies:
            cp.wait()

    return _pcall(
        body, name="share_halves", in_specs=[ANY] * n, out_specs=[ANY] * n,
        out_shape=[jax.ShapeDtypeStruct(h.shape, h.dtype) for h in halves],
        scratch_shapes=[pltpu.SemaphoreType.DMA((n,)), pltpu.SemaphoreType.DMA((n,))],
    )(*halves)


def _reduce_small(v):
    rows = v.shape[0]
    half = rows // 2
    assert half % SUBLANES == 0

    def body(v_ref, out_ref, pair_buf, mine, chip_buf, ssem, rsem):
        x, y, c = _me()
        k_me = 2 * x + y
        sib = (x, y, 1 - c)

        def rows_of(cc):
            return pl.ds(pl.multiple_of(cc * half, SUBLANES), half)

        swap = _remote(v_ref.at[rows_of(1 - c)], pair_buf, ssem.at[0], rsem.at[0], sib)
        swap.start()
        swap.wait()
        mine[...] = v_ref[rows_of(c), :] + pair_buf[...]
        chip_buf[k_me] = mine[...]
        sends = [_remote(mine, chip_buf.at[k_me], ssem.at[1 + r], rsem.at[1 + r], (px, py, c))
                 for r, (px, py) in enumerate(_other_chips(x, y))]
        for cp in sends:
            cp.start()
        for r, (px, py) in enumerate(_other_chips(x, y)):
            blk = chip_buf.at[2 * px + py]
            _remote(blk, blk, ssem.at[1 + r], rsem.at[1 + r], (px, py, c)).wait_recv()
        total = chip_buf[0]
        for k in range(1, N_CHIPS):
            total = total + chip_buf[k]
        out_ref[rows_of(c), :] = total
        for cp in sends:
            cp.wait_send()
        share = _remote(out_ref.at[rows_of(c)], out_ref.at[rows_of(c)], ssem.at[N_CHIPS], rsem.at[N_CHIPS], sib)
        share.start()
        got = out_ref.at[rows_of(1 - c)]
        _remote(got, got, ssem.at[N_CHIPS], rsem.at[N_CHIPS], sib).wait_recv()
        share.wait_send()

    vm = pl.BlockSpec(memory_space=pltpu.VMEM)
    return _pcall(
        body, pin=False, name="reduce_small", in_specs=[vm], out_specs=vm,
        out_shape=jax.ShapeDtypeStruct((rows, LANES), F32),
        scratch_shapes=[pltpu.VMEM((half, LANES), F32), pltpu.VMEM((half, LANES), F32),
                        pltpu.VMEM((N_CHIPS, half, LANES), F32), pltpu.SemaphoreType.DMA((N_CHIPS + 1,)),
                        pltpu.SemaphoreType.DMA((N_CHIPS + 1,))],
        compiler_params=pltpu.CompilerParams(vmem_limit_bytes=32 * 1024 * 1024),
    )(v)


def _add_pair(name, core, chip, g, theirs):
    _, half, cols = theirs.shape
    tr = _tile(half, (256, 176, 128))
    nb = half // tr

    def body(c_ref, k_ref, g_ref, t_ref, o32_ref, o16_ref):
        s = g_ref[...] + t_ref[...]
        o16_ref[...] = s.astype(BF16)

        @pl.when(pl.program_id(1) == k_ref[0])
        def _():
            o32_ref[...] = s

    spec = pl.BlockSpec((None, tr, cols), lambda i, k, c_ref, k_ref: (k, i, 0))
    grid_spec = pltpu.PrefetchScalarGridSpec(
        num_scalar_prefetch=2, grid=(nb, N_CHIPS),
        in_specs=[pl.BlockSpec((None, tr, cols), lambda i, k, c_ref, k_ref: (k, c_ref[0] * nb + i, 0)), spec],
        out_specs=[pl.BlockSpec((tr, cols), lambda i, k, c_ref, k_ref: (i, 0)), spec])
    return _pcall(
        body, name=name, grid_spec=grid_spec,
        out_shape=[jax.ShapeDtypeStruct((half, cols), F32), jax.ShapeDtypeStruct(theirs.shape, BF16)],
        compiler_params=_params(("arbitrary", "arbitrary"), 8 * _nbytes((tr, cols + LANES), F32)),
    )(core, chip, g, theirs)


def _add_chips(name, chip, p32, recv):
    half, cols = p32.shape
    tr = _tile(half, (256, 176, 128))

    def body(k_ref, p_ref, r0_ref, r1_ref, r2_ref, o_ref):
        o_ref[...] = ((p_ref[...] + r0_ref[...].astype(F32)) + r1_ref[...].astype(F32)) + r2_ref[...].astype(F32)

    def other(r):
        return pl.BlockSpec((None, tr, cols), lambda i, k_ref: (r + (k_ref[0] <= r).astype(jnp.int32), i, 0))
    grid_spec = pltpu.PrefetchScalarGridSpec(
        num_scalar_prefetch=1, grid=(half // tr,),
        in_specs=[pl.BlockSpec((tr, cols), lambda i, k_ref: (i, 0)), other(0), other(1), other(2)],
        out_specs=pl.BlockSpec((tr, cols), lambda i, k_ref: (i, 0)))
    return _pcall(
        body, name=name, grid_spec=grid_spec, out_shape=jax.ShapeDtypeStruct((half, cols), F32),
        compiler_params=_params(("arbitrary",), 10 * _nbytes((tr, cols + LANES), F32)),
    )(chip, p32, recv, recv, recv)


def kernel(x, mem, w_in, b_in, hg_lb_logits, hg_norm_w, ml_conv_w, ml_conv_b, ml_norm_w, w_out, ln1_g, ln1_b, ca_wq, ca_wkv, ca_wo, ln2_g, ln2_b, ffn_w_up, ffn_conv_w, ffn_conv_b, ffn_w_down, ln3_g, ln3_b, loss_target, m_w_in, m_b_in, m_hg_lb_logits, m_hg_norm_w, m_ml_conv_w, m_ml_conv_b, m_ml_norm_w, m_w_out, m_ln1_g, m_ln1_b, m_ca_wq, m_ca_wkv, m_ca_wo, m_ln2_g, m_ln2_b, m_ffn_w_up, m_ffn_conv_w, m_ffn_conv_b, m_ffn_w_down, m_ln3_g, m_ln3_b, v_w_in, v_b_in, v_hg_lb_logits, v_hg_norm_w, v_ml_conv_w, v_ml_conv_b, v_ml_norm_w, v_w_out, v_ln1_g, v_ln1_b, v_ca_wq, v_ca_wkv, v_ca_wo, v_ln2_g, v_ln2_b, v_ffn_w_up, v_ffn_conv_w, v_ffn_conv_b, v_ffn_w_down, v_ln3_g, v_ln3_b):
    return _train_step(dict(locals()))


WEIGHTS = ("w_in", "b_in", "hg_lb_logits", "hg_norm_w", "ml_conv_w", "ml_conv_b", "ml_norm_w", "w_out", "ln1_g",
           "ln1_b", "ca_wq", "ca_wkv", "ca_wo", "ln2_g", "ln2_b", "ffn_w_up", "ffn_conv_w", "ffn_conv_b",
           "ffn_w_down", "ln3_g", "ln3_b")
MATRICES = ("w_in", "w_out", "ca_wq", "ca_wkv", "ca_wo", "ffn_w_up", "ffn_w_down")
COL_SHARDED = ("w_in", "ca_wkv", "ffn_w_up", "ml_conv_w", "ffn_conv_w")
SMALL = tuple(n for n in WEIGHTS if n not in MATRICES)
PART_ROWS = 16


def _part_rows(shape):
    n = 1
    for s in shape:
        n *= s
    return -(-n // (LANES * PART_ROWS)) * PART_ROWS


def _pack(arrs, dtype):
    parts = []
    for a in arrs:
        flat = a.reshape(-1).astype(dtype)
        flat = jnp.pad(flat, (0, _part_rows(a.shape) * LANES - flat.shape[0]))
        parts.append(flat.reshape(-1, LANES))
    return jnp.concatenate(parts, axis=0)


def _unpack(buf, shapes):
    lead = buf.shape[:-2]
    outs, r = [], 0
    for sh in shapes:
        n = 1
        for s in sh:
            n *= s
        nr = _part_rows(sh)
        flat = buf[..., r:r + nr, :].reshape(lead + (nr * LANES,))
        outs.append(flat[..., :n].reshape(lead + tuple(sh)))
        r += nr
    return outs


def _cat_cols(s):
    return jnp.moveaxis(s, 0, 1).reshape(s.shape[1], -1)


def _stack_rows(s):
    return s.reshape(-1, s.shape[-1])


def _train_step(a):
    xs, mems, tgt = a["x"][0], a["mem"][0], a["loss_target"][0]
    core = lax.axis_index("c").astype(jnp.int32).reshape(1)
    chip = (2 * lax.axis_index("x") + lax.axis_index("y")).astype(jnp.int32).reshape(1)
    k_me = chip[0]
    shard = {n: a[n][0] for n in MATRICES}

    later = [n for n in MATRICES if n != "w_in"]
    w_in, taps = _gather_weights([shard["w_in"].astype(BF16), _pack([a["ml_conv_w"][0], a["ffn_conv_w"][0]], F32)])
    w = {"w_in": jnp.pad(_cat_cols(w_in), ((0, 0), (0, D_IN_PAD - D_IN)))}
    gathering, token = _gather_start("gather_start", [shard[n].astype(BF16) for n in later])
    ml_cw, ffn_cw = [_cat_cols(s) for s in _unpack(taps, [a["ml_conv_w"].shape[1:], a["ffn_conv_w"].shape[1:]])]
    b_in_p = jnp.pad(a["b_in"], ((0, 0), (0, D_IN_PAD - D_IN))) + token[0:1, 0:1]
    mixer_w = (a["hg_lb_logits"], a["hg_norm_w"], ml_cw, a["ml_conv_b"], a["ml_norm_w"])
    up_cols = a["ffn_w_up"].shape[-1]

    proj, xb = _mm("proj", "nn", xs, w["w_in"], bias=b_in_p, a_copy_dtype=BF16, tm=256, tn=D_IN_PAD)
    y, hst, cst, nst, mst = _mixer_fwd(proj, *mixer_w)
    w.update(zip(later, _forward_halves("forward_halves", _gather_wait("gather_wait", gathering, y))))
    for n in ("w_out", "ca_wq", "ca_wo", "ffn_w_down"):
        w[n] = _stack_rows(w[n])
    z1, x1, x1b = _mm("mix_out", "nn", y, w["w_out"], res=xs, res_scale=ALPHA, ln=("fwd", a["ln1_g"], a["ln1_b"]),
                      copy_dtype=BF16)
    q = _mm("ca_q", "nn", x1b, w["ca_wq"], out_dtype=BF16, tn=D_MODEL)
    kv = _mm("ca_kv", "nn", mems, w["ca_wkv"])
    o = _attn_fwd(q, kv)
    z2, x2, x2b = _mm("ca_out", "nn", o, w["ca_wo"], res=x1, res_scale=ALPHA, ln=("fwd", a["ln2_g"], a["ln2_b"]),
                      copy_dtype=BF16)
    w_up = w["ffn_w_up"]
    assert w_up.shape == (2 * FFN_J, D_MODEL, FFN_W)
    u, hmid, dz3, g_ln3g, g_ln3b, loss_part, dz3b = _ffn_fwd(
        x2b, x2, w_up, ffn_cw, a["ffn_conv_b"], w["ffn_w_down"], a["ln3_g"], a["ln3_b"], tgt)

    grads = {"ln3_g": g_ln3g, "ln3_b": g_ln3b}
    grads["ffn_w_down"] = _mm("g_w_down", "tn", hmid, dz3b, tm=D_FF // 2, tn=D_MODEL)
    du, g_cw, g_cb, dz2, grads["ln2_g"], grads["ln2_b"], dz2b = _ffn_bwd(
        u, ffn_cw, a["ffn_conv_b"], dz3b, dz3, w["ffn_w_down"], w_up, z2, a["ln2_g"], a["ln2_b"])
    grads["ffn_conv_w"] = jnp.transpose(g_cw, (2, 1, 0, 3)).reshape(FFN_CONV, 2 * D_FF)
    grads["ffn_conv_b"] = jnp.transpose(g_cb, (2, 1, 0, 3)).reshape(1, 2 * D_FF)
    grads["ffn_w_up"] = _mm("g_w_up", "tn", x2b, du, out_groups=N_CHIPS, tm=D_MODEL, tn=up_cols)
    grads["ffn_w_down"] = grads["ffn_w_down"].reshape((N_CHIPS,) + shard["ffn_w_down"].shape)
    pending = {}

    def reduce_start(tag, names, swapped=None):
        group = [grads[n] for n in names]
        group, theirs = swapped or (group, _swap_halves("swap_halves_" + tag, group))
        sums = [_add_pair("add_pair_" + n, core, chip, g, t) for n, g, t in zip(names, group, theirs)]
        handle, token = _scatter_start("scatter_start_" + tag, [s16 for _, s16 in sums])
        pending[tag] = (names, [s32 for s32, _ in sums], handle)
        return token[0:1, 0:1]

    ffn = ("ffn_w_up", "ffn_w_down")
    swapping, token = _swap_start("swap_start_ffn", [grads[n] for n in ffn])
    do = _mm("d_o", "nt", dz2b, w["ca_wo"], bias=jnp.zeros((1, D_MODEL), F32) + token[0:1, 0:1], out_dtype=BF16,
             tn=D_MODEL)
    grads["ca_wo"] = _mm("g_wo", "tn", o, dz2b, tm=D_MODEL, tn=D_MODEL)
    zero = reduce_start("ffn", ffn, _swap_wait("swap_wait_ffn", swapping, grads["ca_wo"]))
    dq, dkv = _attn_bwd(q, kv + zero, do)
    grads["ca_wq"] = _mm("g_wq", "tn", x1b, dq, tm=D_MODEL, tn=D_MODEL)
    grads["ca_wkv"] = _mm("g_wkv", "tn", mems, dkv, out_groups=N_CHIPS, tm=D_MODEL)
    dz1, grads["ln1_g"], grads["ln1_b"], dz1b = _mm("d_x1", "nt", dq, w["ca_wq"], res=dz2, res_scale=ALPHA,
                                                    ln=("bwd", z1, a["ln1_g"], a["ln1_b"]), copy_dtype=BF16)
    grads["w_out"] = _mm("g_w_out", "tn", y, dz1b, tm=D_MODEL, tn=D_MODEL)
    for n in ("w_out", "ca_wq", "ca_wo"):
        grads[n] = grads[n].reshape((N_CHIPS,) + shard[n].shape)
    attn = ("w_out", "ca_wq", "ca_wkv", "ca_wo")
    swapping, token = _swap_start("swap_start_attn", [grads[n] for n in attn])
    dy = _mm("d_y", "nt", dz1b, w["w_out"], bias=jnp.zeros((1, D_MODEL), F32) + token[0:1, 0:1], tn=D_MODEL)
    zero = reduce_start("attn", attn, _swap_wait("swap_wait_attn", swapping, dy))
    (dproj, g_b_in, grads["hg_lb_logits"], grads["hg_norm_w"], grads["ml_conv_w"], grads["ml_conv_b"],
     grads["ml_norm_w"]) = _mixer_bwd(proj, dy, hst, cst, nst, mst, mixer_w[0], mixer_w[1] + zero, *mixer_w[2:])
    g_in = _mm("g_w_in", "tn", xb, dproj, tm=D_MODEL, tn=up_cols)[:, :D_IN]
    grads["w_in"] = jnp.moveaxis(g_in.reshape(D_MODEL, N_CHIPS, -1), 1, 0)
    grads["b_in"] = g_b_in[:, :D_IN]
    zero = reduce_start("in", ("w_in",))
    dx = _mm("d_x", "nt", dproj, w["w_in"], bias=jnp.zeros((1, D_MODEL), F32) + zero, res=dz1, res_scale=ALPHA,
             tm=256, tn=D_MODEL)

    halves = {}
    for tag, (names, sums32, handle) in pending.items():
        for n, s32, r in zip(names, sums32, _scatter_wait("scatter_wait_" + tag, handle, dx)):
            halves[n] = _add_chips("add_chips_" + n, chip, s32, r)
    halves = [halves[n] for n in MATRICES]
    other_halves = _share_halves(halves)

    small_shapes = [grads[n].shape for n in SMALL] + [loss_part.shape]
    summed = _unpack(_reduce_small(_pack([grads[n] for n in SMALL] + [loss_part], F32)), small_shapes)
    loss = summed[-1][0, 0]
    for n, g in zip(SMALL, summed[:-1]):
        if n in COL_SHARDED:
            cols = a[n].shape[-1]
            g = lax.dynamic_slice_in_dim(g, k_me * cols, cols, axis=1)
        grads[n] = g

    delta, new_m, new_v = {}, {}, {}
    for n, mine, theirs in zip(MATRICES, halves, other_halves):
        grads[n], delta[n], new_m[n], new_v[n] = _adamw_halves(
            "adamw_" + n, core, shard[n], mine, theirs, a["m_" + n][0], a["v_" + n][0])
    small_w = [a[n][0] if a[n].ndim == 3 else a[n] for n in SMALL]
    small_m = [a["m_" + n][0] if a[n].ndim == 3 else a["m_" + n] for n in SMALL]
    small_v = [a["v_" + n][0] if a[n].ndim == 3 else a["v_" + n] for n in SMALL]
    for out, vals in zip((delta, new_m, new_v),
                         _adamw_many("adamw_small", small_w, [grads[n] for n in SMALL], small_m, small_v)):
        out.update(zip(SMALL, vals))

    def shaped(d):
        return [d[n].reshape(a[n].shape) for n in WEIGHTS]
    return (loss, dx[None], *shaped(grads), *shaped(delta), *shaped(new_m), *shaped(new_v))
```

```python
import functools

import jax
import jax.numpy as jnp
from jax import lax
from jax.experimental import pallas as pl
from jax.experimental.pallas import tpu as pltpu

F32 = jnp.float32
BF16 = jnp.bfloat16

D_MODEL = 1024
HEADS = 4
DK = 128
D_GRP = HEADS * DK
CHUNK = 64
ML_CONV = 4
FFN_CONV = 3
D_FF = 2816
CA_DH = D_MODEL // HEADS
DEPTH = 1
ALPHA = (2.0 * DEPTH) ** 0.25
LN_EPS = 1e-5
NEG_BIG = -1e30
D_IN = 8 * D_GRP + 2 * HEADS
D_IN_PAD = 8 * D_GRP + 128
ADAM_LR, ADAM_B1, ADAM_B2, ADAM_EPS, ADAM_WD, ADAM_STEP = 0.001, 0.9, 0.999, 1e-08, 0.01, 10

SUBLANES = 8
LANES = 128
VMEM_BYTES = 64 * 1024 * 1024


def _pcall(body, pin=True, **kw):
    if not pin:
        return _call(body, **kw)
    kw["out_shape"] = jax.tree.map(lambda s: pltpu.HBM(s.shape, s.dtype), kw["out_shape"])
    call = _call(body, **kw)

    def pinned(*args):
        return call(*[pltpu.with_memory_space_constraint(x, pltpu.HBM) if jnp.issubdtype(x.dtype, jnp.floating) else x
                      for x in args])
    return pinned


def _call(body, **kw):
    return pl.pallas_call(body, **kw)


def _params(semantics, vmem_bytes):
    limit = int(min(max(2 * vmem_bytes, 16 * 1024 * 1024), VMEM_BYTES - 8 * 1024 * 1024))
    return pltpu.CompilerParams(dimension_semantics=semantics, vmem_limit_bytes=limit)


def _nbytes(shape, dtype):
    n = 1
    for s in shape:
        n *= s
    return n * jnp.dtype(dtype).itemsize


def _dg(a, b, ca, cb):
    return lax.dot_general(a.astype(BF16), b.astype(BF16), (((ca,), (cb,)), ((), ())),
                           preferred_element_type=F32)


@jax.custom_vjp
def mm_nn(a, b):
    return _dg(a, b, 1, 0)


mm_nn.defvjp(lambda a, b: (_dg(a, b, 1, 0), (a, b)),
             lambda r, g: (_dg(g, r[1], 1, 1).astype(r[0].dtype), _dg(r[0], g, 0, 0).astype(r[1].dtype)))


@jax.custom_vjp
def mm_nt(a, b):
    return _dg(a, b, 1, 1)


mm_nt.defvjp(lambda a, b: (_dg(a, b, 1, 1), (a, b)),
             lambda r, g: (_dg(g, r[1], 1, 0).astype(r[0].dtype), _dg(g, r[0], 0, 0).astype(r[1].dtype)))


@jax.custom_vjp
def mm_tn(a, b):
    return _dg(a, b, 0, 0)


mm_tn.defvjp(lambda a, b: (_dg(a, b, 0, 0), (a, b)),
             lambda r, g: (_dg(r[1], g, 1, 1).astype(r[0].dtype), _dg(r[0], g, 1, 0).astype(r[1].dtype)))


def _tri(n, lower):
    r = lax.broadcasted_iota(jnp.int32, (n, n), 0)
    c = lax.broadcasted_iota(jnp.int32, (n, n), 1)
    return ((r >= c) if lower else (r <= c)).astype(F32)


def _tri_dot(lower, x):
    t = _tri(x.shape[0], lower).astype(BF16)
    hi = x.astype(BF16)
    rest = x - hi.astype(F32)
    mid = rest.astype(BF16)
    lo = (rest - mid.astype(F32)).astype(BF16)
    return sum(lax.dot_general(t, p, (((1,), (0,)), ((), ())), preferred_element_type=F32) for p in (hi, mid, lo))


@jax.custom_vjp
def cumsum_rows(x):
    return _tri_dot(True, x)


cumsum_rows.defvjp(lambda x: (_tri_dot(True, x), None), lambda _, g: (_tri_dot(False, g),))


def _shift_impl(halo, x, d):
    xx = jnp.concatenate([halo, x], axis=0)
    return pltpu.roll(xx, d, 0)[SUBLANES:]


@functools.partial(jax.custom_vjp, nondiff_argnums=(2,))
def shift_rows(halo, x, d):
    return _shift_impl(halo, x, d)


def _shift_bwd(d, _, g):
    n = g.shape[0] + SUBLANES
    gg = jnp.concatenate([jnp.zeros((SUBLANES, g.shape[1]), g.dtype), g], axis=0)
    r = pltpu.roll(gg, n - d, 0)
    return r[:SUBLANES], r[SUBLANES:]


shift_rows.defvjp(lambda halo, x, d: (_shift_impl(halo, x, d), None), _shift_bwd)


def causal_conv(halo, x, w_rows, b):
    k = len(w_rows)
    y = b + w_rows[k - 1] * x
    for d in range(1, k):
        y = y + w_rows[k - 1 - d] * shift_rows(halo, x, d)
    return y


def _sigmoid(x):
    return 1.0 / (1.0 + jnp.exp(-x))


def _silu(x):
    return x * _sigmoid(x)


def _log_sigmoid(x):
    return jnp.minimum(x, 0.0) - jnp.log(1.0 + jnp.exp(-jnp.abs(x)))


def _pick_row(x, i):
    row = lax.broadcasted_iota(jnp.int32, (x.shape[0], 1), 0)
    return jnp.sum(jnp.where(row == i, x, 0.0), axis=0, keepdims=True)


def _layer_norm(z, g, b):
    mu = jnp.mean(z, axis=-1, keepdims=True)
    zc = z - mu
    var = jnp.mean(zc * zc, axis=-1, keepdims=True)
    return zc * lax.rsqrt(var + LN_EPS) * g + b


def _qk_conv(halo, x, w0, w1, w2, w3, b):
    return _silu(causal_conv(halo, x, (w0, w1, w2, w3), b))


def _grp(i):
    return pl.ds(i * D_GRP, D_GRP)


def _mixer_specs(n_chunks, reverse):
    def chunk(c):
        return n_chunks - 1 - c if reverse else c
    row8 = CHUNK // SUBLANES
    proj_spec = pl.BlockSpec((CHUNK, D_IN_PAD), lambda c: (chunk(c), 0))
    halo_spec = pl.BlockSpec((SUBLANES, 2 * D_GRP), lambda c: (jnp.maximum(chunk(c) * row8 - 1, 0), 2))
    small = [pl.BlockSpec((2, D_GRP), lambda c: (0, 0)), pl.BlockSpec((1, D_GRP), lambda c: (0, 0)),
             pl.BlockSpec((ML_CONV, 2 * D_GRP), lambda c: (0, 0)), pl.BlockSpec((1, 2 * D_GRP), lambda c: (0, 0)),
             pl.BlockSpec((1, D_GRP), lambda c: (0, 0))]
    state_specs = [pl.BlockSpec((1, HEADS, DK, DK), lambda c: (chunk(c), 0, 0, 0)),
                   pl.BlockSpec((1, HEADS, DK, DK), lambda c: (chunk(c), 0, 0, 0)),
                   pl.BlockSpec((1, HEADS, 1, DK), lambda c: (chunk(c), 0, 0, 0)),
                   pl.BlockSpec((1, HEADS, 1, DK), lambda c: (chunk(c), 0, 0, 0))]
    y_spec = pl.BlockSpec((CHUNK, 2 * D_GRP), lambda c: (chunk(c), 0))
    return proj_spec, halo_spec, small, state_specs, y_spec, chunk


def _heads(x):
    return [x[:, h * DK:(h + 1) * DK] for h in range(HEADS)]


def _last(x, j):
    lane = lax.broadcasted_iota(jnp.int32, (1, x.shape[-1]), 1)
    return jnp.sum(jnp.where(lane == j, x, 0.0), axis=-1, keepdims=True)


def _hg_chunk(st_t, hq, hf, hi, hgate, l0, l1, nw):
    n = hq.shape[0]
    lb = _sigmoid(l0 - l1)
    q = _silu(hq)
    lf = jnp.log(lb + (1.0 - lb) * _sigmoid(hf))
    k = (1.0 - lb) * _sigmoid(-hf)
    b = cumsum_rows(lf)
    b_ref = _pick_row(b, n // 2 - 1)
    b_last = _pick_row(b, n - 1)
    qa, ka =_heads(q * jnp.exp(b - b_ref)), _heads(k * jnp.exp(b_ref - b))
    qe, kd, eb, v = _heads(q * jnp.exp(b)), _heads(k * jnp.exp(b_last - b)), _heads(jnp.exp(b_last)), _heads(hi)
    tri = _tri(n, True) > 0
    attn = [jnp.where(tri, mm_nt(qa[h], ka[h]), 0.0) for h in range(HEADS)]
    o = [mm_nn(attn[h], v[h]) + mm_nt(qe[h], st_t[h]) for h in range(HEADS)]
    st_new = jnp.stack([eb[h] * st_t[h] + mm_tn(v[h], kd[h]) for h in range(HEADS)])
    yn = [o[h] * lax.rsqrt(jnp.mean(o[h] * o[h], axis=-1, keepdims=True) + LN_EPS) for h in range(HEADS)]
    return st_new, jnp.concatenate(yn, axis=1) * nw * _silu(hgate)


def _ml_chunk(c_st, n_st, m_st, q, k, v, gates, og, nw):
    n = q.shape[0]
    ig = jnp.stack([_last(gates, h) for h in range(HEADS)])
    log_f = _log_sigmoid(gates)
    fl = jnp.stack([_last(log_f, HEADS + h) for h in range(HEADS)])
    bw = cumsum_rows(jnp.concatenate([jnp.broadcast_to(fl[h], (n, DK)) for h in range(HEADS)], axis=1))
    b = jnp.stack([_last(x, 0) for x in _heads(bw)])
    g = jnp.sum(fl, axis=1, keepdims=True)
    eye = lax.broadcasted_iota(jnp.int32, (n, n), 0) == lax.broadcasted_iota(jnp.int32, (n, n), 1)
    e_row = jnp.sum(jnp.where(eye, ig - b, 0.0), axis=1, keepdims=True)
    d = jnp.where(_tri(n, True) > 0, b + e_row, -jnp.inf)
    inter = b + m_st
    m_t = jnp.maximum(inter, jnp.max(d, axis=2, keepdims=True))
    qs, kh, vh = _heads(q * (DK ** -0.5)), _heads(k), _heads(v)
    s = jnp.stack([mm_nt(qs[h], kh[h]) for h in range(HEADS)]) * jnp.exp(d - m_t)
    w_inter = jnp.exp(inter - m_t)
    num = (jnp.stack([mm_nn(s[h], vh[h]) for h in range(HEADS)])
           + w_inter * jnp.stack([mm_nn(qs[h], c_st[h]) for h in range(HEADS)]))
    den = jnp.sum(s, axis=2, keepdims=True) + w_inter * jnp.sum(jnp.stack(qs) * n_st, axis=2, keepdims=True)
    h_out = num / jnp.maximum(jnp.abs(den), jnp.exp(-m_t))
    a = g - b + ig
    m_new = jnp.maximum(g + m_st, jnp.max(a, axis=1, keepdims=True))
    decay = jnp.exp(g + m_st - m_new)
    wk = jnp.stack(kh) * jnp.exp(a - m_new)
    c_new = decay * c_st + jnp.stack([mm_tn(wk[h], vh[h]) for h in range(HEADS)])
    n_new = decay * n_st + jnp.sum(wk, axis=1, keepdims=True)
    hc = h_out - jnp.mean(h_out, axis=-1, keepdims=True)
    yn = hc * lax.rsqrt(jnp.mean(hc * hc, axis=-1, keepdims=True) + LN_EPS)
    y = _sigmoid(og) * (jnp.concatenate([yn[h] for h in range(HEADS)], axis=1) * nw)
    return c_new, n_new, m_new, y


def _mixer_inputs(proj_ref, lg_ref, hnw_ref, mnw_ref, qk):
    hg_in = (proj_ref[:, _grp(0)], proj_ref[:, _grp(1)], proj_ref[:, _grp(2)], proj_ref[:, _grp(3)],
             lg_ref[0:1, :], lg_ref[1:2, :], hnw_ref[...])
    ml_in = (qk[:, :D_GRP], qk[:, D_GRP:], proj_ref[:, _grp(6)], proj_ref[:, pl.ds(8 * D_GRP, LANES)],
             proj_ref[:, _grp(7)], mnw_ref[...])
    return hg_in, ml_in


def _mixer_fwd(proj, lb_logits, hg_nw, conv_w, conv_b, ml_nw):
    seq = proj.shape[0]
    n_chunks = seq // CHUNK
    proj_spec, halo_spec, small, state_specs, y_spec, _ = _mixer_specs(n_chunks, False)

    def body(proj_ref, halo_ref, lg_ref, hnw_ref, cw_ref, cb_ref, mnw_ref,
             y_ref, hst_ref, cst_ref, nst_ref, mst_ref, hs, cs, ns, ms):
        c = pl.program_id(0)

        @pl.when(c == 0)
        def _():
            hs[...] = jnp.zeros_like(hs)
            cs[...] = jnp.zeros_like(cs)
            ns[...] = jnp.zeros_like(ns)
            ms[...] = jnp.full(ms.shape, NEG_BIG, F32)

        hst_ref[0] = hs[...]
        cst_ref[0] = cs[...]
        nst_ref[0] = ns[...]
        mst_ref[0] = ms[...]
        halo = jnp.where(c > 0, halo_ref[...], 0.0)
        qk = _qk_conv(halo, proj_ref[:, pl.ds(4 * D_GRP, 2 * D_GRP)],
                      cw_ref[0:1, :], cw_ref[1:2, :], cw_ref[2:3, :], cw_ref[3:4, :], cb_ref[...])
        hg_in, ml_in = _mixer_inputs(proj_ref, lg_ref, hnw_ref, mnw_ref, qk)
        hs[...], y_hg = _hg_chunk(hs[...], *hg_in)
        cs[...], ns[...], m_new, y_ml = _ml_chunk(cs[...], ns[...], _last(ms[...], 0), *ml_in)
        ms[...] = jnp.broadcast_to(m_new, ms.shape)
        y_ref[:, pl.ds(0, D_GRP)] = y_hg.astype(BF16)
        y_ref[:, pl.ds(D_GRP, D_GRP)] = y_ml.astype(BF16)

    st = jax.ShapeDtypeStruct((n_chunks, HEADS, DK, DK), F32)
    vec = jax.ShapeDtypeStruct((n_chunks, HEADS, 1, DK), F32)
    vmem = 2 * (_nbytes((CHUNK, D_IN_PAD), F32) + _nbytes((CHUNK, 2 * D_GRP), F32) + 2 * _nbytes((HEADS, DK, DK), F32)) \
        + 2 * _nbytes((HEADS, DK, DK), F32)
    return _pcall(
        body, name="mixer_fwd", grid=(n_chunks,),
        in_specs=[proj_spec, halo_spec] + small,
        out_specs=[y_spec] + state_specs,
        out_shape=[jax.ShapeDtypeStruct((seq, 2 * D_GRP), BF16), st, st, vec, vec],
        scratch_shapes=[pltpu.VMEM((HEADS, DK, DK), F32), pltpu.VMEM((HEADS, DK, DK), F32),
                        pltpu.VMEM((HEADS, 1, DK), F32), pltpu.VMEM((HEADS, 1, DK), F32)],
        compiler_params=_params(("arbitrary",), vmem),
    )(proj, proj, lb_logits, hg_nw, conv_w, conv_b, ml_nw)


def _mixer_bwd(proj, dy, hst, cst, nst, mst, lb_logits, hg_nw, conv_w, conv_b, ml_nw):
    seq = proj.shape[0]
    n_chunks = seq // CHUNK
    proj_spec, halo_spec, small, state_specs, y_spec, _ = _mixer_specs(n_chunks, True)

    def body(proj_ref, halo_ref, dy_ref, hst_ref, cst_ref, nst_ref, mst_ref,
             lg_ref, hnw_ref, cw_ref, cb_ref, mnw_ref,
             dproj_ref, dbin_ref, dlg_ref, dhnw_ref, dcw_ref, dcb_ref, dmnw_ref,
             dhs, dcs, dns, dms, dhalo):
        c = pl.program_id(0)

        @pl.when(c == 0)
        def _():
            for r in (dhs, dcs, dns, dms, dhalo, dbin_ref, dlg_ref, dhnw_ref, dcw_ref, dcb_ref, dmnw_ref):
                r[...] = jnp.zeros_like(r)

        def put(cols, val):
            dproj_ref[:, cols] = val.astype(BF16)
            dbin_ref[:, cols] += jnp.sum(val, axis=0, keepdims=True)

        first = c == n_chunks - 1
        halo = jnp.where(first, 0.0, halo_ref[...])
        x_qk = proj_ref[:, pl.ds(4 * D_GRP, 2 * D_GRP)]
        conv_args = (halo, x_qk, cw_ref[0:1, :], cw_ref[1:2, :], cw_ref[2:3, :], cw_ref[3:4, :], cb_ref[...])
        qk, conv_vjp = jax.vjp(_qk_conv, *conv_args)
        hg_in, ml_in = _mixer_inputs(proj_ref, lg_ref, hnw_ref, mnw_ref, qk)
        _, hg_vjp = jax.vjp(_hg_chunk, hst_ref[0], *hg_in)
        _, ml_vjp = jax.vjp(_ml_chunk, cst_ref[0], nst_ref[0], _last(mst_ref[0], 0), *ml_in)
        dst, dhq, dhf, dhi, dhg, dl0, dl1, dnw = hg_vjp((dhs[...], dy_ref[:, pl.ds(0, D_GRP)]))
        dc, dn, dm, dq, dk, dv, dgates, dog, dmn = ml_vjp(
            (dcs[...], dns[...], _last(dms[...], 0), dy_ref[:, pl.ds(D_GRP, D_GRP)]))
        dhs[...] = dst
        dcs[...] = dc
        dns[...] = dn
        dms[...] = jnp.broadcast_to(dm, dms.shape)
        for i, val in ((0, dhq), (1, dhf), (2, dhi), (3, dhg), (6, dv), (7, dog)):
            put(_grp(i), val)
        put(pl.ds(8 * D_GRP, LANES), dgates)
        dlg_ref[0:1, :] += dl0
        dlg_ref[1:2, :] += dl1
        dhnw_ref[...] += dnw
        dmnw_ref[...] += dmn
        dh, dx, dw0, dw1, dw2, dw3, db = conv_vjp(jnp.concatenate([dq, dk], axis=1))
        tail = jnp.concatenate([jnp.zeros((CHUNK - SUBLANES, 2 * D_GRP), F32), dhalo[...]], axis=0)
        put(pl.ds(4 * D_GRP, 2 * D_GRP), dx + tail)
        dhalo[...] = dh
        for d, dw in enumerate((dw0, dw1, dw2, dw3)):
            dcw_ref[d:d + 1, :] += dw
        dcb_ref[...] += db

    row = pl.BlockSpec((1, D_GRP), lambda c: (0, 0))
    small_out = [pl.BlockSpec((1, D_IN_PAD), lambda c: (0, 0)), pl.BlockSpec((2, D_GRP), lambda c: (0, 0)), row,
                 pl.BlockSpec((ML_CONV, 2 * D_GRP), lambda c: (0, 0)), pl.BlockSpec((1, 2 * D_GRP), lambda c: (0, 0)), row]
    dy_spec = pl.BlockSpec((CHUNK, 2 * D_GRP), y_spec.index_map)
    vmem = 2 * (2 * _nbytes((CHUNK, D_IN_PAD), F32) + _nbytes((CHUNK, 2 * D_GRP), F32)
                + 2 * _nbytes((HEADS, DK, DK), F32)) + 2 * _nbytes((HEADS, DK, DK), F32) + 4 * 1024 * 1024
    return _pcall(
        body, name="mixer_bwd", grid=(n_chunks,),
        in_specs=[proj_spec, halo_spec, dy_spec] + state_specs + small,
        out_specs=[proj_spec] + small_out,
        out_shape=[jax.ShapeDtypeStruct((seq, D_IN_PAD), BF16), jax.ShapeDtypeStruct((1, D_IN_PAD), F32),
                   jax.ShapeDtypeStruct((2, D_GRP), F32), jax.ShapeDtypeStruct((1, D_GRP), F32),
                   jax.ShapeDtypeStruct((ML_CONV, 2 * D_GRP), F32), jax.ShapeDtypeStruct((1, 2 * D_GRP), F32),
                   jax.ShapeDtypeStruct((1, D_GRP), F32)],
        scratch_shapes=[pltpu.VMEM((HEADS, DK, DK), F32), pltpu.VMEM((HEADS, DK, DK), F32),
                        pltpu.VMEM((HEADS, 1, DK), F32), pltpu.VMEM((HEADS, 1, DK), F32),
                        pltpu.VMEM((SUBLANES, 2 * D_GRP), F32)],
        compiler_params=_params(("arbitrary",), vmem),
    )(proj, proj, dy, hst, cst, nst, mst, lb_logits, hg_nw, conv_w, conv_b, ml_nw)


def _tile(n, prefs, unit=None):
    unit = unit or n
    for p in prefs:
        if unit % p == 0 and n % p == 0:
            return p
    return unit


def _logical(arr):
    return arr.shape if arr.ndim == 2 else (arr.shape[1], arr.shape[0] * arr.shape[2])


def _group(arr):
    return arr.shape[-1]


def _split_spec(ndim, group, tr, tc, where):
    if ndim == 2:
        return pl.BlockSpec((tr, tc), where)
    per = group // tc
    assert per * tc == group, (group, tc)

    def index(*ids):
        bi, bj = where(*ids)
        return (bj // per, bi, bj % per)
    return pl.BlockSpec((None, tr, tc), index)


def _mm(name, mode, a, b, *, bias=None, res=None, res_scale=1.0, ln=None, out_dtype=F32, out_groups=None,
        copy_dtype=None, a_copy_dtype=None, tm=None, tn=None, tk=None):
    la, lb = _logical(a), _logical(b)
    if mode == "nn":
        (m, k), n = la, lb[1]
        n_unit = _group(b) if b.ndim == 3 else n
        kc = _group(a) if a.ndim == 3 else k
    elif mode == "nt":
        (m, k), n = la, lb[0]
        n_unit = n
        kc = min(_group(a) if a.ndim == 3 else k, _group(b) if b.ndim == 3 else k)
    else:
        (k, m), n = la, lb[1]
        n_unit, kc = (_group(b) if b.ndim == 3 else n), k
        assert a.ndim == 2
    if out_groups:
        n_unit = min(n_unit, n // out_groups)
    kind = ln[0] if ln else None
    tm = tm or (256 if ln else _tile(m, (512, 256, 128)))
    tn = n if ln else (tn or _tile(n, (512, 384, 256, 128), n_unit))
    if mode != "tn":
        tk = k
    elif tk is None:
        tk = _tile(k, (4096, 2048, 512, 256, 128) if (m // tm) * (n // tn) > 1 else (2048, 512, 256, 128))
    gi, gj, gk = m // tm, n // tn, k // tk
    assert gi * tm == m and gj * tn == n and gk * tk == k and n_unit % tn == 0, (name, m, n, k, tm, tn, tk)
    ca, cb = {"nn": (1, 0), "nt": (1, 1), "tn": (0, 0)}[mode]
    i_outer = gk > 1 or (gi - 1) * _nbytes(b.shape, b.dtype) <= (gj - 1) * _nbytes(a.shape, a.dtype)

    def ij(where):
        return (lambda p, q, kk: where(p, q, kk)) if i_outer else (lambda p, q, kk: where(q, p, kk))
    if mode == "tn":
        a_spec = pl.BlockSpec((tk, tm), ij(lambda i, j, kk: (kk, i)))
    elif a.ndim == 3:
        a_spec = pl.BlockSpec((a.shape[0], tm, _group(a)), ij(lambda i, j, kk: (0, i, 0)))
    else:
        a_spec = pl.BlockSpec((tm, k), ij(lambda i, j, kk: (i, 0)))
    if mode != "nt":
        b_spec = _split_spec(b.ndim, _group(b), tk, tn, ij(lambda i, j, kk: (kk, j)))
    elif b.ndim == 3:
        b_spec = pl.BlockSpec((b.shape[0], tn, _group(b)), ij(lambda i, j, kk: (0, j, 0)))
    else:
        b_spec = pl.BlockSpec((tn, k), ij(lambda i, j, kk: (j, 0)))
    row_spec = pl.BlockSpec((1, tn), ij(lambda i, j, kk: (0, j)))
    blk_spec = pl.BlockSpec((tm, tn), ij(lambda i, j, kk: (i, j)))
    ins, in_specs = [a, b], [a_spec, b_spec]
    if bias is not None:
        ins.append(bias), in_specs.append(row_spec)
    if res is not None:
        ins.append(res), in_specs.append(blk_spec)
    if kind == "fwd":
        ins += [ln[1], ln[2]]
        in_specs += [row_spec, row_spec]
    elif kind == "bwd":
        ins += [ln[1], ln[2], ln[3]]
        in_specs += [blk_spec, row_spec, row_spec]
    if out_groups:
        blk_out = jax.ShapeDtypeStruct((out_groups, m, n // out_groups), out_dtype)
        out_spec = _split_spec(3, n // out_groups, tm, tn, ij(lambda i, j, kk: (i, j)))
    else:
        blk_out, out_spec = jax.ShapeDtypeStruct((m, n), out_dtype), blk_spec
    row_out = jax.ShapeDtypeStruct((1, n), F32)
    if kind is None:
        out_shape, out_specs = [blk_out], [out_spec]
    elif kind == "fwd":
        out_shape, out_specs = [blk_out, blk_out], [blk_spec, blk_spec]
    else:
        out_shape, out_specs = [blk_out, row_out, row_out], [blk_spec, row_spec, row_spec]
    if copy_dtype is not None:
        out_shape.append(jax.ShapeDtypeStruct((m, n), copy_dtype))
        out_specs.append(blk_spec)
    if a_copy_dtype is not None:
        assert mode != "tn" and a.ndim == 2 and copy_dtype is None
        out_shape.append(jax.ShapeDtypeStruct((m, k), a_copy_dtype))
        out_specs.append(a_spec)
    n_in = len(ins)

    def body(*refs):
        in_refs, out_refs, acc_ref = refs[:n_in], refs[n_in:n_in + len(out_shape)], refs[-1]
        i, kk = pl.program_id(0 if i_outer else 1), pl.program_id(2)
        a_ref, b_ref = in_refs[:2]
        extra = list(in_refs[2:])
        if a_copy_dtype is not None:
            out_refs[-1][...] = a_ref[...].astype(a_copy_dtype)

        def epilogue(acc):
            rest = list(extra)
            if bias is not None:
                acc = acc + rest.pop(0)[...]
            if res is not None:
                acc = acc + res_scale * rest.pop(0)[...]
            if kind is None:
                out_refs[0][...] = acc.astype(out_dtype)
                return
            if kind == "fwd":
                out_refs[0][...] = acc
                y = _layer_norm(acc, rest[0][...], rest[1][...])
                out_refs[1][...] = y
                if copy_dtype is not None:
                    out_refs[-1][...] = y.astype(copy_dtype)
                return
            _, vjp = jax.vjp(_layer_norm, rest[0][...], rest[1][...], rest[2][...])
            dz, dg, db = vjp(acc)
            out_refs[0][...] = dz
            out_refs[1][...] += dg
            out_refs[2][...] += db
            if copy_dtype is not None:
                out_refs[-1][...] = dz.astype(copy_dtype)

        if kind == "bwd":
            @pl.when((i == 0) & (kk == 0))
            def _():
                out_refs[1][...] = jnp.zeros_like(out_refs[1])
                out_refs[2][...] = jnp.zeros_like(out_refs[2])

        def chunk(ref, c0, last):
            if ref.ndim == 3:
                g = ref.shape[2]
                return ref[c0 // g, :, pl.ds(c0 % g, kc)]
            return ref[:, pl.ds(c0, kc)] if last else ref[pl.ds(c0, kc), :]

        if mode == "tn" or kc == k:
            prod = _dg(a_ref[...], b_ref[...], ca, cb)
        else:
            prod = None
            for c0 in range(0, k, kc):
                part = _dg(chunk(a_ref, c0, True), chunk(b_ref, c0, mode == "nt"), ca, cb)
                prod = part if prod is None else prod + part
        if gk == 1:
            epilogue(prod)
            return

        @pl.when(kk == 0)
        def _():
            acc_ref[...] = prod

        @pl.when(kk > 0)
        def _():
            acc_ref[...] += prod

        @pl.when(kk == gk - 1)
        def _():
            epilogue(acc_ref[...])

    vmem = (2 * (_nbytes((tm, tk), a.dtype) + _nbytes((tk, tn), b.dtype))
            + (2 * len(ins) + 2 * len(out_shape) + 1) * _nbytes((tm, tn), F32))
    outs = _pcall(
        body, name=name, grid=(gi, gj, gk) if i_outer else (gj, gi, gk), in_specs=in_specs, out_specs=out_specs,
        out_shape=out_shape, scratch_shapes=[pltpu.VMEM((tm, tn) if gk > 1 else (SUBLANES, LANES), F32)],
        compiler_params=_params(("arbitrary", "arbitrary", "arbitrary"), vmem),
    )(*ins)
    return outs[0] if len(out_shape) == 1 else outs


def _attn_head(q, k, v):
    sc = mm_nt(q, k) * (CA_DH ** -0.5)
    e = jnp.exp(sc - jnp.max(sc, axis=-1, keepdims=True))
    return mm_nn(e / jnp.sum(e, axis=-1, keepdims=True), v)


def _attn_fwd(q, kv):
    seq, n_mem = q.shape[0], kv.shape[0]
    tq = _tile(seq, (512, 256, 128))

    def body(q_ref, kv_ref, o_ref):
        for h in range(HEADS):
            hd = pl.ds(h * CA_DH, CA_DH)
            o = _attn_head(q_ref[:, hd], kv_ref[:, hd], kv_ref[:, pl.ds(D_MODEL + h * CA_DH, CA_DH)])
            o_ref[:, hd] = o.astype(BF16)

    return _pcall(
        body, name="attn_fwd", grid=(seq // tq,),
        in_specs=[pl.BlockSpec((tq, D_MODEL), lambda i: (i, 0)), pl.BlockSpec((n_mem, 2 * D_MODEL), lambda i: (0, 0))],
        out_specs=pl.BlockSpec((tq, D_MODEL), lambda i: (i, 0)), out_shape=jax.ShapeDtypeStruct((seq, D_MODEL), BF16),
        compiler_params=_params(("arbitrary",), 4 * _nbytes((tq, D_MODEL), F32) + 2 * _nbytes((n_mem, 2 * D_MODEL), F32)),
    )(q, kv)


def _attn_bwd(q, kv, do):
    seq, n_mem = q.shape[0], kv.shape[0]
    tq = _tile(seq, (512, 256, 128))

    def body(q_ref, kv_ref, do_ref, dq_ref, dkv_ref):
        @pl.when(pl.program_id(0) == 0)
        def _():
            dkv_ref[...] = jnp.zeros_like(dkv_ref)

        for h in range(HEADS):
            hd = pl.ds(h * CA_DH, CA_DH)
            vd = pl.ds(D_MODEL + h * CA_DH, CA_DH)
            _, vjp = jax.vjp(_attn_head, q_ref[:, hd], kv_ref[:, hd], kv_ref[:, vd])
            dq, dk, dv = vjp(do_ref[:, hd].astype(F32))
            dq_ref[:, hd] = dq.astype(BF16)
            dkv_ref[:, hd] += dk
            dkv_ref[:, vd] += dv

    return _pcall(
        body, name="attn_bwd", grid=(seq // tq,),
        in_specs=[pl.BlockSpec((tq, D_MODEL), lambda i: (i, 0)), pl.BlockSpec((n_mem, 2 * D_MODEL), lambda i: (0, 0)),
                  pl.BlockSpec((tq, D_MODEL), lambda i: (i, 0))],
        out_specs=[pl.BlockSpec((tq, D_MODEL), lambda i: (i, 0)), pl.BlockSpec((n_mem, 2 * D_MODEL), lambda i: (0, 0))],
        out_shape=[jax.ShapeDtypeStruct((seq, D_MODEL), BF16), jax.ShapeDtypeStruct((n_mem, 2 * D_MODEL), F32)],
        compiler_params=_params(("arbitrary",), 6 * _nbytes((tq, D_MODEL), F32) + 4 * _nbytes((n_mem, 2 * D_MODEL), F32)),
    )(q, kv, do)


def _ffn_mid(hg, xg, hv, xv, wg0, wg1, wg2, bg, wv0, wv1, wv2, bv):
    return jax.nn.gelu(causal_conv(hg, xg, (wg0, wg1, wg2), bg)) * causal_conv(hv, xv, (wv0, wv1, wv2), bv)


FFN_TB = 256
FFN_W = D_FF // 2
FFN_J = D_FF // FFN_W
MXU_COLS = 256
FFN_PIECES = tuple((off, min(MXU_COLS, FFN_W - off)) for off in range(0, FFN_W, MXU_COLS))


def _ffn_common_specs(seq, row):
    tb = min(FFN_TB, seq)
    full = pl.BlockSpec((tb, D_MODEL), lambda t, j: (row(t), 0))
    vec = pl.BlockSpec((1, D_MODEL), lambda t, j: (0, 0))
    halves = []
    for off in (0, FFN_J):
        halves.append(dict(
            w_up=pl.BlockSpec((None, D_MODEL, FFN_W), lambda t, j, off=off: (j + off, 0, 0)),
            taps=pl.BlockSpec((FFN_CONV, FFN_W), lambda t, j, off=off: (0, j + off)),
            bias=pl.BlockSpec((1, FFN_W), lambda t, j, off=off: (0, j + off))))
    w_down = pl.BlockSpec((FFN_W, D_MODEL), lambda t, j: (j, 0))
    u_blk = pl.BlockSpec((2, tb, FFN_W), lambda t, j: (0, row(t), j))
    return tb, full, vec, halves, w_down, u_blk


def _ffn_vmem(tb):
    return (_nbytes((2, tb, FFN_W), F32) + _nbytes((2, tb, FFN_W), BF16) + 3 * _nbytes((D_MODEL, FFN_W), BF16)
            + 10 * _nbytes((tb, D_MODEL), F32))


def _conv_params(taps_ref, bias_ref, cols):
    return taps_ref[0:1, cols], taps_ref[1:2, cols], taps_ref[2:3, cols], bias_ref[:, cols]


def _ffn_fwd(x2b, x2, w_up, conv_w, conv_b, w_down, ln_g, ln_b, target):
    seq = x2.shape[0]
    tb, full, vec, halves, wd_spec, u_blk = _ffn_common_specs(seq, lambda t: t)
    nt = seq // tb

    def body(xb_ref, wg_ref, wv_ref, tg_ref, tv_ref, bg_ref, bv_ref, wd_ref, x_ref, g_ref, b_ref, tgt_ref,
             u_ref, h_ref, dz_ref, dg_ref, db_ref, loss_ref, dzb_ref, acc, carry):
        t, j = pl.program_id(0), pl.program_id(1)
        xb = xb_ref[...]
        pieces = [pl.ds(off, width) for off, width in FFN_PIECES]
        ug = [_dg(xb, wg_ref[:, cols], 1, 0) for cols in pieces]
        uv = [_dg(xb, wv_ref[:, cols], 1, 0) for cols in pieces]
        hs = []
        for cols, g, v in zip(pieces, ug, uv):
            u_ref[0, :, cols] = g
            u_ref[1, :, cols] = v
            halo_g = jnp.where(t == 0, 0.0, carry[j, 0, :, cols])
            halo_v = jnp.where(t == 0, 0.0, carry[j, 1, :, cols])
            h = _ffn_mid(halo_g, g, halo_v, v, *_conv_params(tg_ref, bg_ref, cols),
                         *_conv_params(tv_ref, bv_ref, cols)).astype(BF16)
            carry[j, 0, :, cols] = g[tb - SUBLANES:, :]
            carry[j, 1, :, cols] = v[tb - SUBLANES:, :]
            h_ref[:, cols] = h
            hs.append(h)
        part = None
        for cols, h in zip(pieces, hs):
            p = _dg(h, wd_ref[cols, :], 1, 0)
            part = p if part is None else part + p

        @pl.when(j == 0)
        def _():
            acc[...] = part

        @pl.when(j > 0)
        def _():
            acc[...] += part

        @pl.when(j == FFN_J - 1)
        def _():
            y, vjp = jax.vjp(_layer_norm, acc[...] + ALPHA * x_ref[...], g_ref[...], b_ref[...])
            err = y - tgt_ref[...]
            part_loss = 0.5 * jnp.sum(jnp.sum(err * err, axis=1, keepdims=True), axis=0, keepdims=True) / D_MODEL
            dz, dg, db = vjp(err / D_MODEL)

            @pl.when(t == 0)
            def _():
                for r in (dg_ref, db_ref, loss_ref):
                    r[...] = jnp.zeros_like(r)

            dz_ref[...] = dz
            dzb_ref[...] = dz.astype(BF16)
            dg_ref[...] += dg
            db_ref[...] += db
            loss_ref[...] += jnp.broadcast_to(part_loss, (1, LANES))

    h0, h1 = halves
    row = jax.ShapeDtypeStruct((1, D_MODEL), F32)
    return _pcall(
        body, name="ffn_fwd", grid=(nt, FFN_J),
        in_specs=[full, h0["w_up"], h1["w_up"], h0["taps"], h1["taps"], h0["bias"], h1["bias"], wd_spec, full, vec, vec,
                  full],
        out_specs=[u_blk, pl.BlockSpec((tb, FFN_W), lambda t, j: (t, j)), full, vec, vec,
                   pl.BlockSpec((1, LANES), lambda t, j: (0, 0)), full],
        out_shape=[jax.ShapeDtypeStruct((2, seq, D_FF), F32), jax.ShapeDtypeStruct((seq, D_FF), BF16),
                   jax.ShapeDtypeStruct((seq, D_MODEL), F32), row, row, jax.ShapeDtypeStruct((1, LANES), F32),
                   jax.ShapeDtypeStruct((seq, D_MODEL), BF16)],
        scratch_shapes=[pltpu.VMEM((tb, D_MODEL), F32), pltpu.VMEM((FFN_J, 2, SUBLANES, FFN_W), F32)],
        compiler_params=_params(("arbitrary", "arbitrary"), _ffn_vmem(tb)),
    )(x2b, w_up, w_up, conv_w, conv_w, conv_b, conv_b, w_down, x2, ln_g, ln_b, target)


def _ffn_bwd(u, conv_w, conv_b, dz3b, dz3, w_down, w_up, z2, ln_g, ln_b):
    seq = dz3.shape[0]
    tb = min(FFN_TB, seq)
    nt = seq // tb
    row8 = tb // SUBLANES
    tb, full, vec, halves, wd_spec, u_blk = _ffn_common_specs(seq, lambda t: nt - 1 - t)
    halo = pl.BlockSpec((2, SUBLANES, FFN_W), lambda t, j: (0, jnp.maximum((nt - 1 - t) * row8 - 1, 0), j))

    def body(u_ref, halo_ref, tg_ref, tv_ref, bg_ref, bv_ref, dzb_ref, wd_ref, wg_ref, wv_ref, dz3_ref, z_ref, g_ref,
             b_ref, du_ref, dw_ref, dbias_ref, dz_ref, dg_ref, db_ref, dz2b_ref, acc, carry):
        t, j = pl.program_id(0), pl.program_id(1)

        @pl.when((t == 0) & (j == 0))
        def _():
            for r in (dw_ref, dbias_ref, dg_ref, db_ref):
                r[...] = jnp.zeros_like(r)

        pieces = [pl.ds(off, width) for off, width in FFN_PIECES]
        dzb = dzb_ref[...]
        dhs = [_dg(dzb, wd_ref[cols, :], 1, 1) for cols in pieces]
        first = t == nt - 1
        dus = []
        for cols, dh in zip(pieces, dhs):
            args = (jnp.where(first, 0.0, halo_ref[0, :, cols]), u_ref[0, :, cols],
                    jnp.where(first, 0.0, halo_ref[1, :, cols]), u_ref[1, :, cols],
                    *_conv_params(tg_ref, bg_ref, cols), *_conv_params(tv_ref, bv_ref, cols))
            _, vjp = jax.vjp(_ffn_mid, *args)
            dhg, dxg, dhv, dxv, g0, g1, g2, gb, v0, v1, v2, vb = vjp(dh)
            zeros = jnp.zeros((tb - SUBLANES, dh.shape[1]), F32)
            dug = (dxg + jnp.concatenate([zeros, jnp.where(t == 0, 0.0, carry[j, 0, :, cols])], axis=0)).astype(BF16)
            duv = (dxv + jnp.concatenate([zeros, jnp.where(t == 0, 0.0, carry[j, 1, :, cols])], axis=0)).astype(BF16)
            carry[j, 0, :, cols] = dhg
            carry[j, 1, :, cols] = dhv
            du_ref[0, :, cols] = dug
            du_ref[1, :, cols] = duv
            for half, parts in enumerate(((g0, g1, g2), (v0, v1, v2))):
                for d, p in enumerate(parts):
                    dw_ref[j, half, d:d + 1, cols] += p
            dbias_ref[j, 0, :, cols] += gb
            dbias_ref[j, 1, :, cols] += vb
            dus.append((dug, duv))
        part = None
        for cols, (dug, duv) in zip(pieces, dus):
            p = _dg(dug, wg_ref[:, cols], 1, 1) + _dg(duv, wv_ref[:, cols], 1, 1)
            part = p if part is None else part + p

        @pl.when(j == 0)
        def _():
            acc[...] = part

        @pl.when(j > 0)
        def _():
            acc[...] += part

        @pl.when(j == FFN_J - 1)
        def _():
            _, ln_vjp = jax.vjp(_layer_norm, z_ref[...], g_ref[...], b_ref[...])
            dz, dg, db = ln_vjp(acc[...] + ALPHA * dz3_ref[...])
            dz_ref[...] = dz
            dz2b_ref[...] = dz.astype(BF16)
            dg_ref[...] += dg
            db_ref[...] += db

    h0, h1 = halves
    row = jax.ShapeDtypeStruct((1, D_MODEL), F32)
    whole = lambda *shape: pl.BlockSpec(shape, lambda t, j: (0,) * len(shape))
    return _pcall(
        body, name="ffn_bwd", grid=(nt, FFN_J),
        in_specs=[u_blk, halo, h0["taps"], h1["taps"], h0["bias"], h1["bias"], full, wd_spec, h0["w_up"], h1["w_up"],
                  full, full, vec, vec],
        out_specs=[u_blk, whole(FFN_J, 2, FFN_CONV, FFN_W), whole(FFN_J, 2, 1, FFN_W), full, vec, vec, full],
        out_shape=[jax.ShapeDtypeStruct((2, seq, D_FF), BF16), jax.ShapeDtypeStruct((FFN_J, 2, FFN_CONV, FFN_W), F32),
                   jax.ShapeDtypeStruct((FFN_J, 2, 1, FFN_W), F32), jax.ShapeDtypeStruct((seq, D_MODEL), F32), row, row,
                   jax.ShapeDtypeStruct((seq, D_MODEL), BF16)],
        scratch_shapes=[pltpu.VMEM((tb, D_MODEL), F32), pltpu.VMEM((FFN_J, 2, SUBLANES, FFN_W), F32)],
        compiler_params=_params(("arbitrary", "arbitrary"), _ffn_vmem(tb)),
    )(u, u, conv_w, conv_w, conv_b, conv_b, dz3b, w_down, w_up, w_up, dz3, z2, ln_g, ln_b)


def _adamw_math(w, g, m, v):
    m_new = ADAM_B1 * m + (1.0 - ADAM_B1) * g
    v_new = ADAM_B2 * v + (1.0 - ADAM_B2) * jnp.square(g)
    m_hat = m_new / (1.0 - ADAM_B1 ** ADAM_STEP)
    v_hat = v_new / (1.0 - ADAM_B2 ** ADAM_STEP)
    return -ADAM_LR * (m_hat / (jnp.sqrt(v_hat) + ADAM_EPS) + ADAM_WD * w), m_new, v_new


def _adamw_many(name, ws, gs, ms, vs):
    n = len(ws)

    def body(*refs):
        w_refs, g_refs, m_refs, v_refs = (refs[i * n:(i + 1) * n] for i in range(4))
        d_refs, nm_refs, nv_refs = (refs[(4 + i) * n:(5 + i) * n] for i in range(3))
        for i in range(n):
            d_refs[i][...], nm_refs[i][...], nv_refs[i][...] = _adamw_math(
                w_refs[i][...], g_refs[i][...], m_refs[i][...], v_refs[i][...])

    vm = pl.BlockSpec(memory_space=pltpu.VMEM)
    outs = _pcall(
        body, pin=False, name=name, in_specs=[vm] * (4 * n), out_specs=[vm] * (3 * n),
        out_shape=[jax.ShapeDtypeStruct(w.shape, F32) for w in ws] * 3,
    )(*ws, *gs, *ms, *vs)
    return outs[:n], outs[n:2 * n], outs[2 * n:]


def _adamw_halves(name, core, w, mine, theirs, m, v):
    rows, cols = w.shape
    half_rows = mine.shape[0]
    tr = _tile(half_rows, (256, 176, 128))
    nbh = half_rows // tr
    assert 2 * half_rows == rows

    def body(c_ref, w_ref, a_ref, b_ref, m_ref, v_ref, g_ref, d_ref, nm_ref, nv_ref):
        g = jnp.where(pl.program_id(0) // nbh == c_ref[0], a_ref[...], b_ref[...])
        g_ref[...] = g
        d_ref[...], nm_ref[...], nv_ref[...] = _adamw_math(w_ref[...], g, m_ref[...], v_ref[...])

    spec = pl.BlockSpec((tr, cols), lambda i, c_ref: (i, 0))
    half = pl.BlockSpec((tr, cols), lambda i, c_ref: (i % nbh, 0))
    sh = jax.ShapeDtypeStruct((rows, cols), F32)
    grid_spec = pltpu.PrefetchScalarGridSpec(
        num_scalar_prefetch=1, grid=(rows // tr,), in_specs=[spec, half, half, spec, spec], out_specs=[spec] * 4)
    return _pcall(
        body, name=name, grid_spec=grid_spec, out_shape=[sh] * 4,
        compiler_params=_params(("arbitrary",), 18 * _nbytes((tr, -(-cols // LANES) * LANES), F32)),
    )(core, w, mine, theirs, m, v)


MESH = pl.DeviceIdType.MESH
ANY = pl.BlockSpec(memory_space=pl.ANY)
N_CHIPS = 4
BF16_ROWS = 16


def _me():
    return lax.axis_index("x"), lax.axis_index("y"), lax.axis_index("c")


def _other_chips(x, y):
    return [(1 - x, y), (x, 1 - y), (1 - x, 1 - y)]


def _remote(src, dst, ssem, rsem, dev):
    return pltpu.make_async_remote_copy(src_ref=src, dst_ref=dst, send_sem=ssem, recv_sem=rsem,
                                        device_id=dev, device_id_type=MESH)


def _half_rows(ref_rows, cc):
    half = ref_rows // 2
    return pl.ds(pl.multiple_of(cc * half, BF16_ROWS), half)


def _gather_weights(shards):
    n = len(shards)
    n_ici = n * (N_CHIPS - 1)

    def body(*refs):
        ins, outs, (ssem, rsem, lsem, lrsem) = refs[:n], refs[n:2 * n], refs[2 * n:]
        x, y, c = _me()
        k_me = 2 * x + y
        sib = (x, y, 1 - c)
        chips = _other_chips(x, y)
        started = []
        for i, (w_ref, o_ref) in enumerate(zip(ins, outs)):
            cp = _remote(w_ref, o_ref.at[k_me], lsem.at[i], lrsem.at[i], sib)
            cp.start()
            started.append(cp)
        for r, (px, py) in enumerate(chips):
            for i, (w_ref, o_ref) in enumerate(zip(ins, outs)):
                rows = _half_rows(w_ref.shape[0], c)
                s = r * n + i
                cp = _remote(w_ref.at[rows], o_ref.at[k_me, rows], ssem.at[s], rsem.at[s], (px, py, c))
                cp.start()
                started.append(cp)
        for r, (px, py) in enumerate(chips):
            for i, o_ref in enumerate(outs):
                blk = o_ref.at[2 * px + py, _half_rows(o_ref.shape[1], c)]
                s = r * n + i
                _remote(blk, blk, ssem.at[s], rsem.at[s], (px, py, c)).wait_recv()
                cp = _remote(blk, blk, ssem.at[n_ici + s], rsem.at[n_ici + s], sib)
                cp.start()
                started.append(cp)
        for r, (px, py) in enumerate(chips):
            for i, o_ref in enumerate(outs):
                blk = o_ref.at[2 * px + py, _half_rows(o_ref.shape[1], 1 - c)]
                s = n_ici + r * n + i
                _remote(blk, blk, ssem.at[s], rsem.at[s], sib).wait_recv()
        for cp in started[n:]:
            cp.wait_send()
        for cp in started[:n]:
            cp.wait()

    return _pcall(
        body, name="gather_weights", in_specs=[ANY] * n, out_specs=[ANY] * n,
        out_shape=[jax.ShapeDtypeStruct((N_CHIPS,) + s.shape, s.dtype) for s in shards],
        scratch_shapes=[pltpu.SemaphoreType.DMA((2 * n_ici,)), pltpu.SemaphoreType.DMA((2 * n_ici,)),
                        pltpu.SemaphoreType.DMA((n,)), pltpu.SemaphoreType.DMA((n,))],
    )(*shards)


def _swap_halves(name, grads):
    n = len(grads)

    def body(*refs):
        ins, outs, (ssem, rsem) = refs[:n], refs[n:2 * n], refs[2 * n:]
        x, y, c = _me()
        copies = []
        for i, (g_ref, o_ref) in enumerate(zip(ins, outs)):
            for k in range(N_CHIPS):
                s = i * N_CHIPS + k
                cp = _remote(g_ref.at[k, _half_rows(g_ref.shape[1], 1 - c)], o_ref.at[k], ssem.at[s], rsem.at[s],
                             (x, y, 1 - c))
                cp.start()
                copies.append(cp)
        for cp in copies:
            cp.wait()

    return _pcall(
        body, name=name, in_specs=[ANY] * n, out_specs=[ANY] * n,
        out_shape=[jax.ShapeDtypeStruct((N_CHIPS, g.shape[1] // 2, g.shape[2]), g.dtype) for g in grads],
        scratch_shapes=[pltpu.SemaphoreType.DMA((n * N_CHIPS,)), pltpu.SemaphoreType.DMA((n * N_CHIPS,))],
    )(*grads)


SEM = pl.BlockSpec(memory_space=pltpu.SEMAPHORE)
IN_HBM = pl.BlockSpec(memory_space=pltpu.HBM)
SPLIT_PARAMS = dict(compiler_params=pltpu.CompilerParams(has_side_effects=pltpu.SideEffectType.DATAFLOW_SIDE_EFFECTING))


def _split_start(name, sources, landings, n_copies, plan):
    ns, nl = len(sources), len(landings)

    def body(*refs):
        ins, lands, (ssem, rsem), token = refs[:ns], refs[ns:ns + nl], refs[ns + nl:ns + nl + 2], refs[-1]
        for s, (src, dst, _, dev) in enumerate(plan(ins, lands)):
            _remote(src, dst, ssem.at[s], rsem.at[s], dev).start()
        token[...] = jnp.zeros_like(token)

    arrays = list(sources) + list(landings)
    outs = _call(
        body, name=name, in_specs=[IN_HBM] * (ns + nl),
        out_specs=[SEM, SEM] + [IN_HBM] * (ns + nl) + [pl.BlockSpec(memory_space=pltpu.VMEM)],
        out_shape=[pltpu.SemaphoreType.DMA((n_copies,)), pltpu.SemaphoreType.DMA((n_copies,))]
        + [pltpu.HBM(a.shape, a.dtype) for a in arrays] + [jax.ShapeDtypeStruct((SUBLANES, LANES), F32)],
        input_output_aliases={i: 2 + i for i in range(ns + nl)}, **SPLIT_PARAMS,
    )(*[pltpu.with_memory_space_constraint(a, pltpu.HBM) for a in arrays])
    return (outs[:-1], ns), outs[-1]


def _split_wait(name, handle, after, plan):
    (ssem, rsem, *thru), ns = handle
    nl = len(thru) - ns

    def body(*refs):
        ins, lands, (ssem_ref, rsem_ref) = refs[:ns], refs[ns:ns + nl], refs[ns + nl:ns + nl + 2]
        for s, (src, _, dst, dev) in enumerate(plan(ins, lands)):
            cp = _remote(src, dst, ssem_ref.at[s], rsem_ref.at[s], dev)
            cp.wait_send()
            cp.wait_recv()

    outs = _call(
        body, name=name, in_specs=[IN_HBM] * (ns + nl) + [SEM, SEM, ANY], out_specs=[IN_HBM] * (ns + nl),
        out_shape=[pltpu.HBM(t.shape, t.dtype) for t in thru],
        input_output_aliases={i: i for i in range(ns + nl)}, **SPLIT_PARAMS,
    )(*thru, ssem, rsem, after)
    return outs[:ns], outs[ns:]


def _swap_plan(ins, lands):
    x, y, c = _me()
    return [(g_ref.at[k, _half_rows(g_ref.shape[1], 1 - c)], l_ref.at[k], l_ref.at[k], (x, y, 1 - c))
            for g_ref, l_ref in zip(ins, lands) for k in range(N_CHIPS)]


def _swap_start(name, grads):
    lands = [lax.empty((N_CHIPS, g.shape[1] // 2, g.shape[2]), g.dtype) for g in grads]
    return _split_start(name, grads, lands, len(grads) * N_CHIPS, _swap_plan)


def _swap_wait(name, handle, after):
    return _split_wait(name, handle, after, _swap_plan)


def _gather_plan(ins, lands):
    x, y, c = _me()
    k_me = 2 * x + y
    plan = [(w_ref, l_ref.at[k_me], l_ref.at[k_me], (x, y, 1 - c)) for w_ref, l_ref in zip(ins, lands)]
    for px, py in _other_chips(x, y):
        for w_ref, l_ref in zip(ins, lands):
            rows = _half_rows(w_ref.shape[0], c)
            plan.append((w_ref.at[rows], l_ref.at[k_me, rows], l_ref.at[2 * px + py, rows], (px, py, c)))
    return plan


def _gather_start(name, shards):
    lands = [lax.empty((N_CHIPS,) + s.shape, s.dtype) for s in shards]
    return _split_start(name, shards, lands, len(shards) * N_CHIPS, _gather_plan)


def _gather_wait(name, handle, after):
    return _split_wait(name, handle, after, _gather_plan)[1]


def _forward_halves(name, blocks):
    n = len(blocks)
    n_sem = n * (N_CHIPS - 1)

    def body(*refs):
        outs, (ssem, rsem) = refs[n:2 * n], refs[2 * n:]
        x, y, c = _me()
        sib = (x, y, 1 - c)
        chips = _other_chips(x, y)
        sends = []
        for r, (px, py) in enumerate(chips):
            for i, o_ref in enumerate(outs):
                blk = o_ref.at[2 * px + py, _half_rows(o_ref.shape[1], c)]
                cp = _remote(blk, blk, ssem.at[r * n + i], rsem.at[r * n + i], sib)
                cp.start()
                sends.append(cp)
        for r, (px, py) in enumerate(chips):
            for i, o_ref in enumerate(outs):
                blk = o_ref.at[2 * px + py, _half_rows(o_ref.shape[1], 1 - c)]
                _remote(blk, blk, ssem.at[r * n + i], rsem.at[r * n + i], sib).wait_recv()
        for cp in sends:
            cp.wait_send()

    return _pcall(
        body, name=name, in_specs=[ANY] * n, out_specs=[ANY] * n,
        out_shape=[jax.ShapeDtypeStruct(b.shape, b.dtype) for b in blocks],
        input_output_aliases={i: i for i in range(n)},
        scratch_shapes=[pltpu.SemaphoreType.DMA((n_sem,)), pltpu.SemaphoreType.DMA((n_sem,))],
    )(*blocks)


def _scatter_plan(ins, lands):
    x, y, c = _me()
    k_me = 2 * x + y
    return [(p_ref.at[2 * px + py], l_ref.at[k_me], l_ref.at[2 * px + py], (px, py, c))
            for px, py in _other_chips(x, y) for p_ref, l_ref in zip(ins, lands)]


def _scatter_start(name, parts):
    lands = [lax.empty(p.shape, p.dtype) for p in parts]
    return _split_start(name, parts, lands, len(parts) * (N_CHIPS - 1), _scatter_plan)


def _scatter_wait(name, handle, after):
    return _split_wait(name, handle, after, _scatter_plan)[1]


def _share_halves(halves):
    n = len(halves)

    def body(*refs):
        ins, outs, (ssem, rsem) = refs[:n], refs[n:2 * n], refs[2 * n:]
        x, y, c = _me()
        copies = [_remote(r_ref, o_ref, ssem.at[i], rsem.at[i], (x, y, 1 - c))
                  for i, (r_ref, o_ref) in enumerate(zip(ins, outs))]
        for cp in copies:
            cp.start()
        for cp in copies:
            cp.wait()

    return _pcall(
        body, name="share_halves", in_specs=[ANY] * n, out_specs=[ANY] * n,
        out_shape=[jax.ShapeDtypeStruct(h.shape, h.dtype) for h in halves],
        scratch_shapes=[pltpu.SemaphoreType.DMA((n,)), pltpu.SemaphoreType.DMA((n,))],
    )(*halves)


def _reduce_small(v):
    rows = v.shape[0]
    half = rows // 2
    assert half % SUBLANES == 0

    def body(v_ref, out_ref, pair_buf, mine, chip_buf, ssem, rsem):
        x, y, c = _me()
        k_me = 2 * x + y
        sib = (x, y, 1 - c)

        def rows_of(cc):
            return pl.ds(pl.multiple_of(cc * half, SUBLANES), half)

        swap = _remote(v_ref.at[rows_of(1 - c)], pair_buf, ssem.at[0], rsem.at[0], sib)
        swap.start()
        swap.wait()
        mine[...] = v_ref[rows_of(c), :] + pair_buf[...]
        chip_buf[k_me] = mine[...]
        sends = [_remote(mine, chip_buf.at[k_me], ssem.at[1 + r], rsem.at[1 + r], (px, py, c))
                 for r, (px, py) in enumerate(_other_chips(x, y))]
        for cp in sends:
            cp.start()
        for r, (px, py) in enumerate(_other_chips(x, y)):
            blk = chip_buf.at[2 * px + py]
            _remote(blk, blk, ssem.at[1 + r], rsem.at[1 + r], (px, py, c)).wait_recv()
        total = chip_buf[0]
        for k in range(1, N_CHIPS):
            total = total + chip_buf[k]
        out_ref[rows_of(c), :] = total
        for cp in sends:
            cp.wait_send()
        share = _remote(out_ref.at[rows_of(c)], out_ref.at[rows_of(c)], ssem.at[N_CHIPS], rsem.at[N_CHIPS], sib)
        share.start()
        got = out_ref.at[rows_of(1 - c)]
        _remote(got, got, ssem.at[N_CHIPS], rsem.at[N_CHIPS], sib).wait_recv()
        share.wait_send()

    vm = pl.BlockSpec(memory_space=pltpu.VMEM)
    return _pcall(
        body, pin=False, name="reduce_small", in_specs=[vm], out_specs=vm,
        out_shape=jax.ShapeDtypeStruct((rows, LANES), F32),
        scratch_shapes=[pltpu.VMEM((half, LANES), F32), pltpu.VMEM((half, LANES), F32),
                        pltpu.VMEM((N_CHIPS, half, LANES), F32), pltpu.SemaphoreType.DMA((N_CHIPS + 1,)),
                        pltpu.SemaphoreType.DMA((N_CHIPS + 1,))],
        compiler_params=pltpu.CompilerParams(vmem_limit_bytes=32 * 1024 * 1024),
    )(v)


def _add_pair(name, core, chip, g, theirs):
    _, half, cols = theirs.shape
    tr = _tile(half, (256, 176, 128))
    nb = half // tr

    def body(c_ref, k_ref, g_ref, t_ref, o32_ref, o16_ref):
        s = g_ref[...] + t_ref[...]
        o16_ref[...] = s.astype(BF16)

        @pl.when(pl.program_id(1) == k_ref[0])
        def _():
            o32_ref[...] = s

    spec = pl.BlockSpec((None, tr, cols), lambda i, k, c_ref, k_ref: (k, i, 0))
    grid_spec = pltpu.PrefetchScalarGridSpec(
        num_scalar_prefetch=2, grid=(nb, N_CHIPS),
        in_specs=[pl.BlockSpec((None, tr, cols), lambda i, k, c_ref, k_ref: (k, c_ref[0] * nb + i, 0)), spec],
        out_specs=[pl.BlockSpec((tr, cols), lambda i, k, c_ref, k_ref: (i, 0)), spec])
    return _pcall(
        body, name=name, grid_spec=grid_spec,
        out_shape=[jax.ShapeDtypeStruct((half, cols), F32), jax.ShapeDtypeStruct(theirs.shape, BF16)],
        compiler_params=_params(("arbitrary", "arbitrary"), 8 * _nbytes((tr, cols + LANES), F32)),
    )(core, chip, g, theirs)


def _add_chips(name, chip, p32, recv):
    half, cols = p32.shape
    tr = _tile(half, (256, 176, 128))

    def body(k_ref, p_ref, r0_ref, r1_ref, r2_ref, o_ref):
        o_ref[...] = ((p_ref[...] + r0_ref[...].astype(F32)) + r1_ref[...].astype(F32)) + r2_ref[...].astype(F32)

    def other(r):
        return pl.BlockSpec((None, tr, cols), lambda i, k_ref: (r + (k_ref[0] <= r).astype(jnp.int32), i, 0))
    grid_spec = pltpu.PrefetchScalarGridSpec(
        num_scalar_prefetch=1, grid=(half // tr,),
        in_specs=[pl.BlockSpec((tr, cols), lambda i, k_ref: (i, 0)), other(0), other(1), other(2)],
        out_specs=pl.BlockSpec((tr, cols), lambda i, k_ref: (i, 0)))
    return _pcall(
        body, name=name, grid_spec=grid_spec, out_shape=jax.ShapeDtypeStruct((half, cols), F32),
        compiler_params=_params(("arbitrary",), 10 * _nbytes((tr, cols + LANES), F32)),
    )(chip, p32, recv, recv, recv)


def kernel(x, mem, w_in, b_in, hg_lb_logits, hg_norm_w, ml_conv_w, ml_conv_b, ml_norm_w, w_out, ln1_g, ln1_b, ca_wq, ca_wkv, ca_wo, ln2_g, ln2_b, ffn_w_up, ffn_conv_w, ffn_conv_b, ffn_w_down, ln3_g, ln3_b, loss_target, m_w_in, m_b_in, m_hg_lb_logits, m_hg_norm_w, m_ml_conv_w, m_ml_conv_b, m_ml_norm_w, m_w_out, m_ln1_g, m_ln1_b, m_ca_wq, m_ca_wkv, m_ca_wo, m_ln2_g, m_ln2_b, m_ffn_w_up, m_ffn_conv_w, m_ffn_conv_b, m_ffn_w_down, m_ln3_g, m_ln3_b, v_w_in, v_b_in, v_hg_lb_logits, v_hg_norm_w, v_ml_conv_w, v_ml_conv_b, v_ml_norm_w, v_w_out, v_ln1_g, v_ln1_b, v_ca_wq, v_ca_wkv, v_ca_wo, v_ln2_g, v_ln2_b, v_ffn_w_up, v_ffn_conv_w, v_ffn_conv_b, v_ffn_w_down, v_ln3_g, v_ln3_b):
    return _train_step(dict(locals()))


WEIGHTS = ("w_in", "b_in", "hg_lb_logits", "hg_norm_w", "ml_conv_w", "ml_conv_b", "ml_norm_w", "w_out", "ln1_g",
           "ln1_b", "ca_wq", "ca_wkv", "ca_wo", "ln2_g", "ln2_b", "ffn_w_up", "ffn_conv_w", "ffn_conv_b",
           "ffn_w_down", "ln3_g", "ln3_b")
MATRICES = ("w_in", "w_out", "ca_wq", "ca_wkv", "ca_wo", "ffn_w_up", "ffn_w_down")
COL_SHARDED = ("w_in", "ca_wkv", "ffn_w_up", "ml_conv_w", "ffn_conv_w")
SMALL = tuple(n for n in WEIGHTS if n not in MATRICES)
PART_ROWS = 16


def _part_rows(shape):
    n = 1
    for s in shape:
        n *= s
    return -(-n // (LANES * PART_ROWS)) * PART_ROWS


def _pack(arrs, dtype):
    parts = []
    for a in arrs:
        flat = a.reshape(-1).astype(dtype)
        flat = jnp.pad(flat, (0, _part_rows(a.shape) * LANES - flat.shape[0]))
        parts.append(flat.reshape(-1, LANES))
    return jnp.concatenate(parts, axis=0)


def _unpack(buf, shapes):
    lead = buf.shape[:-2]
    outs, r = [], 0
    for sh in shapes:
        n = 1
        for s in sh:
            n *= s
        nr = _part_rows(sh)
        flat = buf[..., r:r + nr, :].reshape(lead + (nr * LANES,))
        outs.append(flat[..., :n].reshape(lead + tuple(sh)))
        r += nr
    return outs


def _cat_cols(s):
    return jnp.moveaxis(s, 0, 1).reshape(s.shape[1], -1)


def _stack_rows(s):
    return s.reshape(-1, s.shape[-1])


def _train_step(a):
    xs, mems, tgt = a["x"][0], a["mem"][0], a["loss_target"][0]
    core = lax.axis_index("c").astype(jnp.int32).reshape(1)
    chip = (2 * lax.axis_index("x") + lax.axis_index("y")).astype(jnp.int32).reshape(1)
    k_me = chip[0]
    shard = {n: a[n][0] for n in MATRICES}

    later = [n for n in MATRICES if n != "w_in"]
    w_in, taps = _gather_weights([shard["w_in"].astype(BF16), _pack([a["ml_conv_w"][0], a["ffn_conv_w"][0]], F32)])
    w = {"w_in": jnp.pad(_cat_cols(w_in), ((0, 0), (0, D_IN_PAD - D_IN)))}
    gathering, token = _gather_start("gather_start", [shard[n].astype(BF16) for n in later])
    ml_cw, ffn_cw = [_cat_cols(s) for s in _unpack(taps, [a["ml_conv_w"].shape[1:], a["ffn_conv_w"].shape[1:]])]
    b_in_p = jnp.pad(a["b_in"], ((0, 0), (0, D_IN_PAD - D_IN))) + token[0:1, 0:1]
    mixer_w = (a["hg_lb_logits"], a["hg_norm_w"], ml_cw, a["ml_conv_b"], a["ml_norm_w"])
    up_cols = a["ffn_w_up"].shape[-1]

    proj, xb = _mm("proj", "nn", xs, w["w_in"], bias=b_in_p, a_copy_dtype=BF16, tm=256, tn=D_IN_PAD)
    y, hst, cst, nst, mst = _mixer_fwd(proj, *mixer_w)
    w.update(zip(later, _forward_halves("forward_halves", _gather_wait("gather_wait", gathering, y))))
    for n in ("w_out", "ca_wq", "ca_wo", "ffn_w_down"):
        w[n] = _stack_rows(w[n])
    z1, x1, x1b = _mm("mix_out", "nn", y, w["w_out"], res=xs, res_scale=ALPHA, ln=("fwd", a["ln1_g"], a["ln1_b"]),
                      copy_dtype=BF16)
    q = _mm("ca_q", "nn", x1b, w["ca_wq"], out_dtype=BF16, tn=D_MODEL)
    kv = _mm("ca_kv", "nn", mems, w["ca_wkv"])
    o = _attn_fwd(q, kv)
    z2, x2, x2b = _mm("ca_out", "nn", o, w["ca_wo"], res=x1, res_scale=ALPHA, ln=("fwd", a["ln2_g"], a["ln2_b"]),
                      copy_dtype=BF16)
    w_up = w["ffn_w_up"]
    assert w_up.shape == (2 * FFN_J, D_MODEL, FFN_W)
    u, hmid, dz3, g_ln3g, g_ln3b, loss_part, dz3b = _ffn_fwd(
        x2b, x2, w_up, ffn_cw, a["ffn_conv_b"], w["ffn_w_down"], a["ln3_g"], a["ln3_b"], tgt)

    grads = {"ln3_g": g_ln3g, "ln3_b": g_ln3b}
    grads["ffn_w_down"] = _mm("g_w_down", "tn", hmid, dz3b, tm=D_FF // 2, tn=D_MODEL)
    du, g_cw, g_cb, dz2, grads["ln2_g"], grads["ln2_b"], dz2b = _ffn_bwd(
        u, ffn_cw, a["ffn_conv_b"], dz3b, dz3, w["ffn_w_down"], w_up, z2, a["ln2_g"], a["ln2_b"])
    grads["ffn_conv_w"] = jnp.transpose(g_cw, (2, 1, 0, 3)).reshape(FFN_CONV, 2 * D_FF)
    grads["ffn_conv_b"] = jnp.transpose(g_cb, (2, 1, 0, 3)).reshape(1, 2 * D_FF)
    grads["ffn_w_up"] = _mm("g_w_up", "tn", x2b, du, out_groups=N_CHIPS, tm=D_MODEL, tn=up_cols)
    grads["ffn_w_down"] = grads["ffn_w_down"].reshape((N_CHIPS,) + shard["ffn_w_down"].shape)
    pending = {}

    def reduce_start(tag, names, swapped=None):
        group = [grads[n] for n in names]
        group, theirs = swapped or (group, _swap_halves("swap_halves_" + tag, group))
        sums = [_add_pair("add_pair_" + n, core, chip, g, t) for n, g, t in zip(names, group, theirs)]
        handle, token = _scatter_start("scatter_start_" + tag, [s16 for _, s16 in sums])
        pending[tag] = (names, [s32 for s32, _ in sums], handle)
        return token[0:1, 0:1]

    ffn = ("ffn_w_up", "ffn_w_down")
    swapping, token = _swap_start("swap_start_ffn", [grads[n] for n in ffn])
    do = _mm("d_o", "nt", dz2b, w["ca_wo"], bias=jnp.zeros((1, D_MODEL), F32) + token[0:1, 0:1], out_dtype=BF16,
             tn=D_MODEL)
    grads["ca_wo"] = _mm("g_wo", "tn", o, dz2b, tm=D_MODEL // 2, tn=D_MODEL)
    zero = reduce_start("ffn", ffn, _swap_wait("swap_wait_ffn", swapping, grads["ca_wo"]))
    dq, dkv = _attn_bwd(q, kv + zero, do)
    grads["ca_wq"] = _mm("g_wq", "tn", x1b, dq, tm=D_MODEL // 2, tn=D_MODEL)
    grads["ca_wkv"] = _mm("g_wkv", "tn", mems, dkv, out_groups=N_CHIPS, tm=D_MODEL)
    dz1, grads["ln1_g"], grads["ln1_b"], dz1b = _mm("d_x1", "nt", dq, w["ca_wq"], res=dz2, res_scale=ALPHA,
                                                    ln=("bwd", z1, a["ln1_g"], a["ln1_b"]), copy_dtype=BF16)
    grads["w_out"] = _mm("g_w_out", "tn", y, dz1b, tm=D_MODEL // 2, tn=D_MODEL)
    for n in ("w_out", "ca_wq", "ca_wo"):
        grads[n] = grads[n].reshape((N_CHIPS,) + shard[n].shape)
    attn = ("w_out", "ca_wq", "ca_wkv", "ca_wo")
    swapping, token = _swap_start("swap_start_attn", [grads[n] for n in attn])
    dy = _mm("d_y", "nt", dz1b, w["w_out"], bias=jnp.zeros((1, D_MODEL), F32) + token[0:1, 0:1], tn=D_MODEL)
    zero = reduce_start("attn", attn, _swap_wait("swap_wait_attn", swapping, dy))
    (dproj, g_b_in, grads["hg_lb_logits"], grads["hg_norm_w"], grads["ml_conv_w"], grads["ml_conv_b"],
     grads["ml_norm_w"]) = _mixer_bwd(proj, dy, hst, cst, nst, mst, mixer_w[0], mixer_w[1] + zero, *mixer_w[2:])
    g_in = _mm("g_w_in", "tn", xb, dproj, tm=D_MODEL, tn=up_cols)[:, :D_IN]
    grads["w_in"] = jnp.moveaxis(g_in.reshape(D_MODEL, N_CHIPS, -1), 1, 0)
    grads["b_in"] = g_b_in[:, :D_IN]
    zero = reduce_start("in", ("w_in",))
    dx = _mm("d_x", "nt", dproj, w["w_in"], bias=jnp.zeros((1, D_MODEL), F32) + zero, res=dz1, res_scale=ALPHA,
             tm=256, tn=D_MODEL)

    halves = {}
    for tag, (names, sums32, handle) in pending.items():
        for n, s32, r in zip(names, sums32, _scatter_wait("scatter_wait_" + tag, handle, dx)):
            halves[n] = _add_chips("add_chips_" + n, chip, s32, r)
    halves = [halves[n] for n in MATRICES]
    other_halves = _share_halves(halves)

    small_shapes = [grads[n].shape for n in SMALL] + [loss_part.shape]
    summed = _unpack(_reduce_small(_pack([grads[n] for n in SMALL] + [loss_part], F32)), small_shapes)
    loss = summed[-1][0, 0]
    for n, g in zip(SMALL, summed[:-1]):
        if n in COL_SHARDED:
            cols = a[n].shape[-1]
            g = lax.dynamic_slice_in_dim(g, k_me * cols, cols, axis=1)
        grads[n] = g

    delta, new_m, new_v = {}, {}, {}
    for n, mine, theirs in zip(MATRICES, halves, other_halves):
        grads[n], delta[n], new_m[n], new_v[n] = _adamw_halves(
            "adamw_" + n, core, shard[n], mine, theirs, a["m_" + n][0], a["v_" + n][0])
    small_w = [a[n][0] if a[n].ndim == 3 else a[n] for n in SMALL]
    small_m = [a["m_" + n][0] if a[n].ndim == 3 else a["m_" + n] for n in SMALL]
    small_v = [a["v_" + n][0] if a[n].ndim == 3 else a["v_" + n] for n in SMALL]
    for out, vals in zip((delta, new_m, new_v),
                         _adamw_many("adamw_small", small_w, [grads[n] for n in SMALL], small_m, small_v)):
        out.update(zip(SMALL, vals))

    def shaped(d):
        return [d[n].reshape(a[n].shape) for n in WEIGHTS]
    return (loss, dx[None], *shaped(grads), *shaped(delta), *shaped(new_m), *shaped(new_v))
```

```python
import functools

import jax
import jax.numpy as jnp
from jax import lax
from jax.experimental import pallas as pl
from jax.experimental.pallas import tpu as pltpu

F32 = jnp.float32
BF16 = jnp.bfloat16

D_MODEL = 1024
HEADS = 4
DK = 128
D_GRP = HEADS * DK
CHUNK = 64
ML_CONV = 4
FFN_CONV = 3
D_FF = 2816
CA_DH = D_MODEL // HEADS
DEPTH = 1
ALPHA = (2.0 * DEPTH) ** 0.25
LN_EPS = 1e-5
NEG_BIG = -1e30
D_IN = 8 * D_GRP + 2 * HEADS
D_IN_PAD = 8 * D_GRP + 128
ADAM_LR, ADAM_B1, ADAM_B2, ADAM_EPS, ADAM_WD, ADAM_STEP = 0.001, 0.9, 0.999, 1e-08, 0.01, 10

SUBLANES = 8
LANES = 128
VMEM_BYTES = 64 * 1024 * 1024


def _pcall(body, pin=True, **kw):
    if not pin:
        return _call(body, **kw)
    kw["out_shape"] = jax.tree.map(lambda s: pltpu.HBM(s.shape, s.dtype), kw["out_shape"])
    call = _call(body, **kw)

    def pinned(*args):
        return call(*[pltpu.with_memory_space_constraint(x, pltpu.HBM) if jnp.issubdtype(x.dtype, jnp.floating) else x
                      for x in args])
    return pinned


def _call(body, **kw):
    return pl.pallas_call(body, **kw)


def _params(semantics, vmem_bytes):
    limit = int(min(max(2 * vmem_bytes, 16 * 1024 * 1024), VMEM_BYTES - 8 * 1024 * 1024))
    return pltpu.CompilerParams(dimension_semantics=semantics, vmem_limit_bytes=limit)


def _nbytes(shape, dtype):
    n = 1
    for s in shape:
        n *= s
    return n * jnp.dtype(dtype).itemsize


def _dg(a, b, ca, cb):
    return lax.dot_general(a.astype(BF16), b.astype(BF16), (((ca,), (cb,)), ((), ())),
                           preferred_element_type=F32)


@jax.custom_vjp
def mm_nn(a, b):
    return _dg(a, b, 1, 0)


mm_nn.defvjp(lambda a, b: (_dg(a, b, 1, 0), (a, b)),
             lambda r, g: (_dg(g, r[1], 1, 1).astype(r[0].dtype), _dg(r[0], g, 0, 0).astype(r[1].dtype)))


@jax.custom_vjp
def mm_nt(a, b):
    return _dg(a, b, 1, 1)


mm_nt.defvjp(lambda a, b: (_dg(a, b, 1, 1), (a, b)),
             lambda r, g: (_dg(g, r[1], 1, 0).astype(r[0].dtype), _dg(g, r[0], 0, 0).astype(r[1].dtype)))


@jax.custom_vjp
def mm_tn(a, b):
    return _dg(a, b, 0, 0)


mm_tn.defvjp(lambda a, b: (_dg(a, b, 0, 0), (a, b)),
             lambda r, g: (_dg(r[1], g, 1, 1).astype(r[0].dtype), _dg(r[0], g, 1, 0).astype(r[1].dtype)))


def _tri(n, lower):
    r = lax.broadcasted_iota(jnp.int32, (n, n), 0)
    c = lax.broadcasted_iota(jnp.int32, (n, n), 1)
    return ((r >= c) if lower else (r <= c)).astype(F32)


def _tri_dot(lower, x):
    t = _tri(x.shape[0], lower).astype(BF16)
    hi = x.astype(BF16)
    rest = x - hi.astype(F32)
    mid = rest.astype(BF16)
    lo = (rest - mid.astype(F32)).astype(BF16)
    return sum(lax.dot_general(t, p, (((1,), (0,)), ((), ())), preferred_element_type=F32) for p in (hi, mid, lo))


@jax.custom_vjp
def cumsum_rows(x):
    return _tri_dot(True, x)


cumsum_rows.defvjp(lambda x: (_tri_dot(True, x), None), lambda _, g: (_tri_dot(False, g),))


def _shift_impl(halo, x, d):
    xx = jnp.concatenate([halo, x], axis=0)
    return pltpu.roll(xx, d, 0)[SUBLANES:]


@functools.partial(jax.custom_vjp, nondiff_argnums=(2,))
def shift_rows(halo, x, d):
    return _shift_impl(halo, x, d)


def _shift_bwd(d, _, g):
    n = g.shape[0] + SUBLANES
    gg = jnp.concatenate([jnp.zeros((SUBLANES, g.shape[1]), g.dtype), g], axis=0)
    r = pltpu.roll(gg, n - d, 0)
    return r[:SUBLANES], r[SUBLANES:]


shift_rows.defvjp(lambda halo, x, d: (_shift_impl(halo, x, d), None), _shift_bwd)


def causal_conv(halo, x, w_rows, b):
    k = len(w_rows)
    y = b + w_rows[k - 1] * x
    for d in range(1, k):
        y = y + w_rows[k - 1 - d] * shift_rows(halo, x, d)
    return y


def _sigmoid(x):
    return 1.0 / (1.0 + jnp.exp(-x))


def _silu(x):
    return x * _sigmoid(x)


def _log_sigmoid(x):
    return jnp.minimum(x, 0.0) - jnp.log(1.0 + jnp.exp(-jnp.abs(x)))


def _pick_row(x, i):
    row = lax.broadcasted_iota(jnp.int32, (x.shape[0], 1), 0)
    return jnp.sum(jnp.where(row == i, x, 0.0), axis=0, keepdims=True)


def _layer_norm(z, g, b):
    mu = jnp.mean(z, axis=-1, keepdims=True)
    zc = z - mu
    var = jnp.mean(zc * zc, axis=-1, keepdims=True)
    return zc * lax.rsqrt(var + LN_EPS) * g + b


def _qk_conv(halo, x, w0, w1, w2, w3, b):
    return _silu(causal_conv(halo, x, (w0, w1, w2, w3), b))


def _grp(i):
    return pl.ds(i * D_GRP, D_GRP)


def _mixer_specs(n_chunks, reverse):
    def chunk(c):
        return n_chunks - 1 - c if reverse else c
    row8 = CHUNK // SUBLANES
    proj_spec = pl.BlockSpec((CHUNK, D_IN_PAD), lambda c: (chunk(c), 0))
    halo_spec = pl.BlockSpec((SUBLANES, 2 * D_GRP), lambda c: (jnp.maximum(chunk(c) * row8 - 1, 0), 2))
    small = [pl.BlockSpec((2, D_GRP), lambda c: (0, 0)), pl.BlockSpec((1, D_GRP), lambda c: (0, 0)),
             pl.BlockSpec((ML_CONV, 2 * D_GRP), lambda c: (0, 0)), pl.BlockSpec((1, 2 * D_GRP), lambda c: (0, 0)),
             pl.BlockSpec((1, D_GRP), lambda c: (0, 0))]
    state_specs = [pl.BlockSpec((1, HEADS, DK, DK), lambda c: (chunk(c), 0, 0, 0)),
                   pl.BlockSpec((1, HEADS, DK, DK), lambda c: (chunk(c), 0, 0, 0)),
                   pl.BlockSpec((1, HEADS, 1, DK), lambda c: (chunk(c), 0, 0, 0)),
                   pl.BlockSpec((1, HEADS, 1, DK), lambda c: (chunk(c), 0, 0, 0))]
    y_spec = pl.BlockSpec((CHUNK, 2 * D_GRP), lambda c: (chunk(c), 0))
    return proj_spec, halo_spec, small, state_specs, y_spec, chunk


def _heads(x):
    return [x[:, h * DK:(h + 1) * DK] for h in range(HEADS)]


def _last(x, j):
    lane = lax.broadcasted_iota(jnp.int32, (1, x.shape[-1]), 1)
    return jnp.sum(jnp.where(lane == j, x, 0.0), axis=-1, keepdims=True)


def _hg_chunk(st_t, hq, hf, hi, hgate, l0, l1, nw):
    n = hq.shape[0]
    lb = _sigmoid(l0 - l1)
    q = _silu(hq)
    lf = jnp.log(lb + (1.0 - lb) * _sigmoid(hf))
    k = (1.0 - lb) * _sigmoid(-hf)
    b = cumsum_rows(lf)
    b_ref = _pick_row(b, n // 2 - 1)
    b_last = _pick_row(b, n - 1)
    qa, ka =_heads(q * jnp.exp(b - b_ref)), _heads(k * jnp.exp(b_ref - b))
    qe, kd, eb, v = _heads(q * jnp.exp(b)), _heads(k * jnp.exp(b_last - b)), _heads(jnp.exp(b_last)), _heads(hi)
    tri = _tri(n, True) > 0
    attn = [jnp.where(tri, mm_nt(qa[h], ka[h]), 0.0) for h in range(HEADS)]
    o = [mm_nn(attn[h], v[h]) + mm_nt(qe[h], st_t[h]) for h in range(HEADS)]
    st_new = jnp.stack([eb[h] * st_t[h] + mm_tn(v[h], kd[h]) for h in range(HEADS)])
    yn = [o[h] * lax.rsqrt(jnp.mean(o[h] * o[h], axis=-1, keepdims=True) + LN_EPS) for h in range(HEADS)]
    return st_new, jnp.concatenate(yn, axis=1) * nw * _silu(hgate)


def _ml_chunk(c_st, n_st, m_st, q, k, v, gates, og, nw):
    n = q.shape[0]
    ig = jnp.stack([_last(gates, h) for h in range(HEADS)])
    log_f = _log_sigmoid(gates)
    fl = jnp.stack([_last(log_f, HEADS + h) for h in range(HEADS)])
    bw = cumsum_rows(jnp.concatenate([jnp.broadcast_to(fl[h], (n, DK)) for h in range(HEADS)], axis=1))
    b = jnp.stack([_last(x, 0) for x in _heads(bw)])
    g = jnp.sum(fl, axis=1, keepdims=True)
    eye = lax.broadcasted_iota(jnp.int32, (n, n), 0) == lax.broadcasted_iota(jnp.int32, (n, n), 1)
    e_row = jnp.sum(jnp.where(eye, ig - b, 0.0), axis=1, keepdims=True)
    d = jnp.where(_tri(n, True) > 0, b + e_row, -jnp.inf)
    inter = b + m_st
    m_t = jnp.maximum(inter, jnp.max(d, axis=2, keepdims=True))
    qs, kh, vh = _heads(q * (DK ** -0.5)), _heads(k), _heads(v)
    s = jnp.stack([mm_nt(qs[h], kh[h]) for h in range(HEADS)]) * jnp.exp(d - m_t)
    w_inter = jnp.exp(inter - m_t)
    num = (jnp.stack([mm_nn(s[h], vh[h]) for h in range(HEADS)])
           + w_inter * jnp.stack([mm_nn(qs[h], c_st[h]) for h in range(HEADS)]))
    den = jnp.sum(s, axis=2, keepdims=True) + w_inter * jnp.sum(jnp.stack(qs) * n_st, axis=2, keepdims=True)
    h_out = num / jnp.maximum(jnp.abs(den), jnp.exp(-m_t))
    a = g - b + ig
    m_new = jnp.maximum(g + m_st, jnp.max(a, axis=1, keepdims=True))
    decay = jnp.exp(g + m_st - m_new)
    wk = jnp.stack(kh) * jnp.exp(a - m_new)
    c_new = decay * c_st + jnp.stack([mm_tn(wk[h], vh[h]) for h in range(HEADS)])
    n_new = decay * n_st + jnp.sum(wk, axis=1, keepdims=True)
    hc = h_out - jnp.mean(h_out, axis=-1, keepdims=True)
    yn = hc * lax.rsqrt(jnp.mean(hc * hc, axis=-1, keepdims=True) + LN_EPS)
    y = _sigmoid(og) * (jnp.concatenate([yn[h] for h in range(HEADS)], axis=1) * nw)
    return c_new, n_new, m_new, y


def _mixer_inputs(proj_ref, lg_ref, hnw_ref, mnw_ref, qk):
    hg_in = (proj_ref[:, _grp(0)], proj_ref[:, _grp(1)], proj_ref[:, _grp(2)], proj_ref[:, _grp(3)],
             lg_ref[0:1, :], lg_ref[1:2, :], hnw_ref[...])
    ml_in = (qk[:, :D_GRP], qk[:, D_GRP:], proj_ref[:, _grp(6)], proj_ref[:, pl.ds(8 * D_GRP, LANES)],
             proj_ref[:, _grp(7)], mnw_ref[...])
    return hg_in, ml_in


def _mixer_fwd(proj, lb_logits, hg_nw, conv_w, conv_b, ml_nw):
    seq = proj.shape[0]
    n_chunks = seq // CHUNK
    proj_spec, halo_spec, small, state_specs, y_spec, _ = _mixer_specs(n_chunks, False)

    def body(proj_ref, halo_ref, lg_ref, hnw_ref, cw_ref, cb_ref, mnw_ref,
             y_ref, hst_ref, cst_ref, nst_ref, mst_ref, hs, cs, ns, ms):
        c = pl.program_id(0)

        @pl.when(c == 0)
        def _():
            hs[...] = jnp.zeros_like(hs)
            cs[...] = jnp.zeros_like(cs)
            ns[...] = jnp.zeros_like(ns)
            ms[...] = jnp.full(ms.shape, NEG_BIG, F32)

        hst_ref[0] = hs[...]
        cst_ref[0] = cs[...]
        nst_ref[0] = ns[...]
        mst_ref[0] = ms[...]
        halo = jnp.where(c > 0, halo_ref[...], 0.0)
        qk = _qk_conv(halo, proj_ref[:, pl.ds(4 * D_GRP, 2 * D_GRP)],
                      cw_ref[0:1, :], cw_ref[1:2, :], cw_ref[2:3, :], cw_ref[3:4, :], cb_ref[...])
        hg_in, ml_in = _mixer_inputs(proj_ref, lg_ref, hnw_ref, mnw_ref, qk)
        hs[...], y_hg = _hg_chunk(hs[...], *hg_in)
        cs[...], ns[...], m_new, y_ml = _ml_chunk(cs[...], ns[...], _last(ms[...], 0), *ml_in)
        ms[...] = jnp.broadcast_to(m_new, ms.shape)
        y_ref[:, pl.ds(0, D_GRP)] = y_hg.astype(BF16)
        y_ref[:, pl.ds(D_GRP, D_GRP)] = y_ml.astype(BF16)

    st = jax.ShapeDtypeStruct((n_chunks, HEADS, DK, DK), F32)
    vec = jax.ShapeDtypeStruct((n_chunks, HEADS, 1, DK), F32)
    vmem = 2 * (_nbytes((CHUNK, D_IN_PAD), F32) + _nbytes((CHUNK, 2 * D_GRP), F32) + 2 * _nbytes((HEADS, DK, DK), F32)) \
        + 2 * _nbytes((HEADS, DK, DK), F32)
    return _pcall(
        body, name="mixer_fwd", grid=(n_chunks,),
        in_specs=[proj_spec, halo_spec] + small,
        out_specs=[y_spec] + state_specs,
        out_shape=[jax.ShapeDtypeStruct((seq, 2 * D_GRP), BF16), st, st, vec, vec],
        scratch_shapes=[pltpu.VMEM((HEADS, DK, DK), F32), pltpu.VMEM((HEADS, DK, DK), F32),
                        pltpu.VMEM((HEADS, 1, DK), F32), pltpu.VMEM((HEADS, 1, DK), F32)],
        compiler_params=_params(("arbitrary",), vmem),
    )(proj, proj, lb_logits, hg_nw, conv_w, conv_b, ml_nw)


def _mixer_bwd(proj, dy, hst, cst, nst, mst, lb_logits, hg_nw, conv_w, conv_b, ml_nw):
    seq = proj.shape[0]
    n_chunks = seq // CHUNK
    proj_spec, halo_spec, small, state_specs, y_spec, _ = _mixer_specs(n_chunks, True)

    def body(proj_ref, halo_ref, dy_ref, hst_ref, cst_ref, nst_ref, mst_ref,
             lg_ref, hnw_ref, cw_ref, cb_ref, mnw_ref,
             dproj_ref, dbin_ref, dlg_ref, dhnw_ref, dcw_ref, dcb_ref, dmnw_ref,
             dhs, dcs, dns, dms, dhalo):
        c = pl.program_id(0)

        @pl.when(c == 0)
        def _():
            for r in (dhs, dcs, dns, dms, dhalo, dbin_ref, dlg_ref, dhnw_ref, dcw_ref, dcb_ref, dmnw_ref):
                r[...] = jnp.zeros_like(r)

        def put(cols, val):
            dproj_ref[:, cols] = val.astype(BF16)
            dbin_ref[:, cols] += jnp.sum(val, axis=0, keepdims=True)

        first = c == n_chunks - 1
        halo = jnp.where(first, 0.0, halo_ref[...])
        x_qk = proj_ref[:, pl.ds(4 * D_GRP, 2 * D_GRP)]
        conv_args = (halo, x_qk, cw_ref[0:1, :], cw_ref[1:2, :], cw_ref[2:3, :], cw_ref[3:4, :], cb_ref[...])
        qk, conv_vjp = jax.vjp(_qk_conv, *conv_args)
        hg_in, ml_in = _mixer_inputs(proj_ref, lg_ref, hnw_ref, mnw_ref, qk)
        _, hg_vjp = jax.vjp(_hg_chunk, hst_ref[0], *hg_in)
        _, ml_vjp = jax.vjp(_ml_chunk, cst_ref[0], nst_ref[0], _last(mst_ref[0], 0), *ml_in)
        dst, dhq, dhf, dhi, dhg, dl0, dl1, dnw = hg_vjp((dhs[...], dy_ref[:, pl.ds(0, D_GRP)]))
        dc, dn, dm, dq, dk, dv, dgates, dog, dmn = ml_vjp(
            (dcs[...], dns[...], _last(dms[...], 0), dy_ref[:, pl.ds(D_GRP, D_GRP)]))
        dhs[...] = dst
        dcs[...] = dc
        dns[...] = dn
        dms[...] = jnp.broadcast_to(dm, dms.shape)
        for i, val in ((0, dhq), (1, dhf), (2, dhi), (3, dhg), (6, dv), (7, dog)):
            put(_grp(i), val)
        put(pl.ds(8 * D_GRP, LANES), dgates)
        dlg_ref[0:1, :] += dl0
        dlg_ref[1:2, :] += dl1
        dhnw_ref[...] += dnw
        dmnw_ref[...] += dmn
        dh, dx, dw0, dw1, dw2, dw3, db = conv_vjp(jnp.concatenate([dq, dk], axis=1))
        tail = jnp.concatenate([jnp.zeros((CHUNK - SUBLANES, 2 * D_GRP), F32), dhalo[...]], axis=0)
        put(pl.ds(4 * D_GRP, 2 * D_GRP), dx + tail)
        dhalo[...] = dh
        for d, dw in enumerate((dw0, dw1, dw2, dw3)):
            dcw_ref[d:d + 1, :] += dw
        dcb_ref[...] += db

    row = pl.BlockSpec((1, D_GRP), lambda c: (0, 0))
    small_out = [pl.BlockSpec((1, D_IN_PAD), lambda c: (0, 0)), pl.BlockSpec((2, D_GRP), lambda c: (0, 0)), row,
                 pl.BlockSpec((ML_CONV, 2 * D_GRP), lambda c: (0, 0)), pl.BlockSpec((1, 2 * D_GRP), lambda c: (0, 0)), row]
    dy_spec = pl.BlockSpec((CHUNK, 2 * D_GRP), y_spec.index_map)
    vmem = 2 * (2 * _nbytes((CHUNK, D_IN_PAD), F32) + _nbytes((CHUNK, 2 * D_GRP), F32)
                + 2 * _nbytes((HEADS, DK, DK), F32)) + 2 * _nbytes((HEADS, DK, DK), F32) + 4 * 1024 * 1024
    return _pcall(
        body, name="mixer_bwd", grid=(n_chunks,),
        in_specs=[proj_spec, halo_spec, dy_spec] + state_specs + small,
        out_specs=[proj_spec] + small_out,
        out_shape=[jax.ShapeDtypeStruct((seq, D_IN_PAD), BF16), jax.ShapeDtypeStruct((1, D_IN_PAD), F32),
                   jax.ShapeDtypeStruct((2, D_GRP), F32), jax.ShapeDtypeStruct((1, D_GRP), F32),
                   jax.ShapeDtypeStruct((ML_CONV, 2 * D_GRP), F32), jax.ShapeDtypeStruct((1, 2 * D_GRP), F32),
                   jax.ShapeDtypeStruct((1, D_GRP), F32)],
        scratch_shapes=[pltpu.VMEM((HEADS, DK, DK), F32), pltpu.VMEM((HEADS, DK, DK), F32),
                        pltpu.VMEM((HEADS, 1, DK), F32), pltpu.VMEM((HEADS, 1, DK), F32),
                        pltpu.VMEM((SUBLANES, 2 * D_GRP), F32)],
        compiler_params=_params(("arbitrary",), vmem),
    )(proj, proj, dy, hst, cst, nst, mst, lb_logits, hg_nw, conv_w, conv_b, ml_nw)


def _tile(n, prefs, unit=None):
    unit = unit or n
    for p in prefs:
        if unit % p == 0 and n % p == 0:
            return p
    return unit


def _logical(arr):
    return arr.shape if arr.ndim == 2 else (arr.shape[1], arr.shape[0] * arr.shape[2])


def _group(arr):
    return arr.shape[-1]


def _split_spec(ndim, group, tr, tc, where):
    if ndim == 2:
        return pl.BlockSpec((tr, tc), where)
    per = group // tc
    assert per * tc == group, (group, tc)

    def index(*ids):
        bi, bj = where(*ids)
        return (bj // per, bi, bj % per)
    return pl.BlockSpec((None, tr, tc), index)


def _mm(name, mode, a, b, *, bias=None, res=None, res_scale=1.0, ln=None, out_dtype=F32, out_groups=None,
        copy_dtype=None, a_copy_dtype=None, tm=None, tn=None, tk=None):
    la, lb = _logical(a), _logical(b)
    if mode == "nn":
        (m, k), n = la, lb[1]
        n_unit = _group(b) if b.ndim == 3 else n
        kc = _group(a) if a.ndim == 3 else k
    elif mode == "nt":
        (m, k), n = la, lb[0]
        n_unit = n
        kc = min(_group(a) if a.ndim == 3 else k, _group(b) if b.ndim == 3 else k)
    else:
        (k, m), n = la, lb[1]
        n_unit, kc = (_group(b) if b.ndim == 3 else n), k
        assert a.ndim == 2
    if out_groups:
        n_unit = min(n_unit, n // out_groups)
    kind = ln[0] if ln else None
    tm = tm or (256 if ln else _tile(m, (512, 256, 128)))
    tn = n if ln else (tn or _tile(n, (512, 384, 256, 128), n_unit))
    if mode != "tn":
        tk = k
    elif tk is None:
        tk = _tile(k, (4096, 2048, 512, 256, 128) if (m // tm) * (n // tn) > 1 else (2048, 512, 256, 128))
    gi, gj, gk = m // tm, n // tn, k // tk
    assert gi * tm == m and gj * tn == n and gk * tk == k and n_unit % tn == 0, (name, m, n, k, tm, tn, tk)
    ca, cb = {"nn": (1, 0), "nt": (1, 1), "tn": (0, 0)}[mode]
    i_outer = gk > 1 or (gi - 1) * _nbytes(b.shape, b.dtype) <= (gj - 1) * _nbytes(a.shape, a.dtype)

    def ij(where):
        return (lambda p, q, kk: where(p, q, kk)) if i_outer else (lambda p, q, kk: where(q, p, kk))
    if mode == "tn":
        a_spec = pl.BlockSpec((tk, tm), ij(lambda i, j, kk: (kk, i)))
    elif a.ndim == 3:
        a_spec = pl.BlockSpec((a.shape[0], tm, _group(a)), ij(lambda i, j, kk: (0, i, 0)))
    else:
        a_spec = pl.BlockSpec((tm, k), ij(lambda i, j, kk: (i, 0)))
    if mode != "nt":
        b_spec = _split_spec(b.ndim, _group(b), tk, tn, ij(lambda i, j, kk: (kk, j)))
    elif b.ndim == 3:
        b_spec = pl.BlockSpec((b.shape[0], tn, _group(b)), ij(lambda i, j, kk: (0, j, 0)))
    else:
        b_spec = pl.BlockSpec((tn, k), ij(lambda i, j, kk: (j, 0)))
    row_spec = pl.BlockSpec((1, tn), ij(lambda i, j, kk: (0, j)))
    blk_spec = pl.BlockSpec((tm, tn), ij(lambda i, j, kk: (i, j)))
    ins, in_specs = [a, b], [a_spec, b_spec]
    if bias is not None:
        ins.append(bias), in_specs.append(row_spec)
    if res is not None:
        ins.append(res), in_specs.append(blk_spec)
    if kind == "fwd":
        ins += [ln[1], ln[2]]
        in_specs += [row_spec, row_spec]
    elif kind == "bwd":
        ins += [ln[1], ln[2], ln[3]]
        in_specs += [blk_spec, row_spec, row_spec]
    if out_groups:
        blk_out = jax.ShapeDtypeStruct((out_groups, m, n // out_groups), out_dtype)
        out_spec = _split_spec(3, n // out_groups, tm, tn, ij(lambda i, j, kk: (i, j)))
    else:
        blk_out, out_spec = jax.ShapeDtypeStruct((m, n), out_dtype), blk_spec
    row_out = jax.ShapeDtypeStruct((1, n), F32)
    if kind is None:
        out_shape, out_specs = [blk_out], [out_spec]
    elif kind == "fwd":
        out_shape, out_specs = [blk_out, blk_out], [blk_spec, blk_spec]
    else:
        out_shape, out_specs = [blk_out, row_out, row_out], [blk_spec, row_spec, row_spec]
    if copy_dtype is not None:
        out_shape.append(jax.ShapeDtypeStruct((m, n), copy_dtype))
        out_specs.append(blk_spec)
    if a_copy_dtype is not None:
        assert mode != "tn" and a.ndim == 2 and copy_dtype is None
        out_shape.append(jax.ShapeDtypeStruct((m, k), a_copy_dtype))
        out_specs.append(a_spec)
    n_in = len(ins)

    def body(*refs):
        in_refs, out_refs, acc_ref = refs[:n_in], refs[n_in:n_in + len(out_shape)], refs[-1]
        i, kk = pl.program_id(0 if i_outer else 1), pl.program_id(2)
        a_ref, b_ref = in_refs[:2]
        extra = list(in_refs[2:])
        if a_copy_dtype is not None:
            out_refs[-1][...] = a_ref[...].astype(a_copy_dtype)

        def epilogue(acc):
            rest = list(extra)
            if bias is not None:
                acc = acc + rest.pop(0)[...]
            if res is not None:
                acc = acc + res_scale * rest.pop(0)[...]
            if kind is None:
                out_refs[0][...] = acc.astype(out_dtype)
                return
            if kind == "fwd":
                out_refs[0][...] = acc
                y = _layer_norm(acc, rest[0][...], rest[1][...])
                out_refs[1][...] = y
                if copy_dtype is not None:
                    out_refs[-1][...] = y.astype(copy_dtype)
                return
            _, vjp = jax.vjp(_layer_norm, rest[0][...], rest[1][...], rest[2][...])
            dz, dg, db = vjp(acc)
            out_refs[0][...] = dz
            out_refs[1][...] += dg
            out_refs[2][...] += db
            if copy_dtype is not None:
                out_refs[-1][...] = dz.astype(copy_dtype)

        if kind == "bwd":
            @pl.when((i == 0) & (kk == 0))
            def _():
                out_refs[1][...] = jnp.zeros_like(out_refs[1])
                out_refs[2][...] = jnp.zeros_like(out_refs[2])

        def chunk(ref, c0, last):
            if ref.ndim == 3:
                g = ref.shape[2]
                return ref[c0 // g, :, pl.ds(c0 % g, kc)]
            return ref[:, pl.ds(c0, kc)] if last else ref[pl.ds(c0, kc), :]

        if mode == "tn" or kc == k:
            prod = _dg(a_ref[...], b_ref[...], ca, cb)
        else:
            prod = None
            for c0 in range(0, k, kc):
                part = _dg(chunk(a_ref, c0, True), chunk(b_ref, c0, mode == "nt"), ca, cb)
                prod = part if prod is None else prod + part
        if gk == 1:
            epilogue(prod)
            return

        @pl.when(kk == 0)
        def _():
            acc_ref[...] = prod

        @pl.when(kk > 0)
        def _():
            acc_ref[...] += prod

        @pl.when(kk == gk - 1)
        def _():
            epilogue(acc_ref[...])

    vmem = (2 * (_nbytes((tm, tk), a.dtype) + _nbytes((tk, tn), b.dtype))
            + (2 * len(ins) + 2 * len(out_shape) + 1) * _nbytes((tm, tn), F32))
    outs = _pcall(
        body, name=name, grid=(gi, gj, gk) if i_outer else (gj, gi, gk), in_specs=in_specs, out_specs=out_specs,
        out_shape=out_shape, scratch_shapes=[pltpu.VMEM((tm, tn) if gk > 1 else (SUBLANES, LANES), F32)],
        compiler_params=_params(("arbitrary", "arbitrary", "arbitrary"), vmem),
    )(*ins)
    return outs[0] if len(out_shape) == 1 else outs


def _attn_head(q, k, v):
    sc = mm_nt(q, k) * (CA_DH ** -0.5)
    e = jnp.exp(sc - jnp.max(sc, axis=-1, keepdims=True))
    return mm_nn(e / jnp.sum(e, axis=-1, keepdims=True), v)


def _attn_fwd(q, kv):
    seq, n_mem = q.shape[0], kv.shape[0]
    tq = _tile(seq, (512, 256, 128))

    def body(q_ref, kv_ref, o_ref):
        for h in range(HEADS):
            hd = pl.ds(h * CA_DH, CA_DH)
            o = _attn_head(q_ref[:, hd], kv_ref[:, hd], kv_ref[:, pl.ds(D_MODEL + h * CA_DH, CA_DH)])
            o_ref[:, hd] = o.astype(BF16)

    return _pcall(
        body, name="attn_fwd", grid=(seq // tq,),
        in_specs=[pl.BlockSpec((tq, D_MODEL), lambda i: (i, 0)), pl.BlockSpec((n_mem, 2 * D_MODEL), lambda i: (0, 0))],
        out_specs=pl.BlockSpec((tq, D_MODEL), lambda i: (i, 0)), out_shape=jax.ShapeDtypeStruct((seq, D_MODEL), BF16),
        compiler_params=_params(("arbitrary",), 4 * _nbytes((tq, D_MODEL), F32) + 2 * _nbytes((n_mem, 2 * D_MODEL), F32)),
    )(q, kv)


def _attn_bwd(q, kv, do):
    seq, n_mem = q.shape[0], kv.shape[0]
    tq = _tile(seq, (512, 256, 128))

    def body(q_ref, kv_ref, do_ref, dq_ref, dkv_ref):
        @pl.when(pl.program_id(0) == 0)
        def _():
            dkv_ref[...] = jnp.zeros_like(dkv_ref)

        for h in range(HEADS):
            hd = pl.ds(h * CA_DH, CA_DH)
            vd = pl.ds(D_MODEL + h * CA_DH, CA_DH)
            _, vjp = jax.vjp(_attn_head, q_ref[:, hd], kv_ref[:, hd], kv_ref[:, vd])
            dq, dk, dv = vjp(do_ref[:, hd].astype(F32))
            dq_ref[:, hd] = dq.astype(BF16)
            dkv_ref[:, hd] += dk
            dkv_ref[:, vd] += dv

    return _pcall(
        body, name="attn_bwd", grid=(seq // tq,),
        in_specs=[pl.BlockSpec((tq, D_MODEL), lambda i: (i, 0)), pl.BlockSpec((n_mem, 2 * D_MODEL), lambda i: (0, 0)),
                  pl.BlockSpec((tq, D_MODEL), lambda i: (i, 0))],
        out_specs=[pl.BlockSpec((tq, D_MODEL), lambda i: (i, 0)), pl.BlockSpec((n_mem, 2 * D_MODEL), lambda i: (0, 0))],
        out_shape=[jax.ShapeDtypeStruct((seq, D_MODEL), BF16), jax.ShapeDtypeStruct((n_mem, 2 * D_MODEL), F32)],
        compiler_params=_params(("arbitrary",), 6 * _nbytes((tq, D_MODEL), F32) + 4 * _nbytes((n_mem, 2 * D_MODEL), F32)),
    )(q, kv, do)


def _ffn_mid(hg, xg, hv, xv, wg0, wg1, wg2, bg, wv0, wv1, wv2, bv):
    return jax.nn.gelu(causal_conv(hg, xg, (wg0, wg1, wg2), bg)) * causal_conv(hv, xv, (wv0, wv1, wv2), bv)


FFN_TB = 256
FFN_W = D_FF // 2
FFN_J = D_FF // FFN_W
MXU_COLS = 256
FFN_PIECES = tuple((off, min(MXU_COLS, FFN_W - off)) for off in range(0, FFN_W, MXU_COLS))


def _ffn_common_specs(seq, row):
    tb = min(FFN_TB, seq)
    full = pl.BlockSpec((tb, D_MODEL), lambda t, j: (row(t), 0))
    vec = pl.BlockSpec((1, D_MODEL), lambda t, j: (0, 0))
    halves = []
    for off in (0, FFN_J):
        halves.append(dict(
            w_up=pl.BlockSpec((None, D_MODEL, FFN_W), lambda t, j, off=off: (j + off, 0, 0)),
            taps=pl.BlockSpec((FFN_CONV, FFN_W), lambda t, j, off=off: (0, j + off)),
            bias=pl.BlockSpec((1, FFN_W), lambda t, j, off=off: (0, j + off))))
    w_down = pl.BlockSpec((FFN_W, D_MODEL), lambda t, j: (j, 0))
    u_blk = pl.BlockSpec((2, tb, FFN_W), lambda t, j: (0, row(t), j))
    return tb, full, vec, halves, w_down, u_blk


def _ffn_vmem(tb):
    return (_nbytes((2, tb, FFN_W), F32) + _nbytes((2, tb, FFN_W), BF16) + 3 * _nbytes((D_MODEL, FFN_W), BF16)
            + 10 * _nbytes((tb, D_MODEL), F32))


def _conv_params(taps_ref, bias_ref, cols):
    return taps_ref[0:1, cols], taps_ref[1:2, cols], taps_ref[2:3, cols], bias_ref[:, cols]


def _ffn_fwd(x2b, x2, w_up, conv_w, conv_b, w_down, ln_g, ln_b, target):
    seq = x2.shape[0]
    tb, full, vec, halves, wd_spec, u_blk = _ffn_common_specs(seq, lambda t: t)
    nt = seq // tb

    def body(xb_ref, wg_ref, wv_ref, tg_ref, tv_ref, bg_ref, bv_ref, wd_ref, x_ref, g_ref, b_ref, tgt_ref,
             u_ref, h_ref, dz_ref, dg_ref, db_ref, loss_ref, dzb_ref, acc, carry):
        t, j = pl.program_id(0), pl.program_id(1)
        xb = xb_ref[...]
        pieces = [pl.ds(off, width) for off, width in FFN_PIECES]
        ug = [_dg(xb, wg_ref[:, cols], 1, 0) for cols in pieces]
        uv = [_dg(xb, wv_ref[:, cols], 1, 0) for cols in pieces]
        hs = []
        for cols, g, v in zip(pieces, ug, uv):
            u_ref[0, :, cols] = g
            u_ref[1, :, cols] = v
            halo_g = jnp.where(t == 0, 0.0, carry[j, 0, :, cols])
            halo_v = jnp.where(t == 0, 0.0, carry[j, 1, :, cols])
            h = _ffn_mid(halo_g, g, halo_v, v, *_conv_params(tg_ref, bg_ref, cols),
                         *_conv_params(tv_ref, bv_ref, cols)).astype(BF16)
            carry[j, 0, :, cols] = g[tb - SUBLANES:, :]
            carry[j, 1, :, cols] = v[tb - SUBLANES:, :]
            h_ref[:, cols] = h
            hs.append(h)
        part = None
        for cols, h in zip(pieces, hs):
            p = _dg(h, wd_ref[cols, :], 1, 0)
            part = p if part is None else part + p

        @pl.when(j == 0)
        def _():
            acc[...] = part

        @pl.when(j > 0)
        def _():
            acc[...] += part

        @pl.when(j == FFN_J - 1)
        def _():
            y, vjp = jax.vjp(_layer_norm, acc[...] + ALPHA * x_ref[...], g_ref[...], b_ref[...])
            err = y - tgt_ref[...]
            part_loss = 0.5 * jnp.sum(jnp.sum(err * err, axis=1, keepdims=True), axis=0, keepdims=True) / D_MODEL
            dz, dg, db = vjp(err / D_MODEL)

            @pl.when(t == 0)
            def _():
                for r in (dg_ref, db_ref, loss_ref):
                    r[...] = jnp.zeros_like(r)

            dz_ref[...] = dz
            dzb_ref[...] = dz.astype(BF16)
            dg_ref[...] += dg
            db_ref[...] += db
            loss_ref[...] += jnp.broadcast_to(part_loss, (1, LANES))

    h0, h1 = halves
    row = jax.ShapeDtypeStruct((1, D_MODEL), F32)
    return _pcall(
        body, name="ffn_fwd", grid=(nt, FFN_J),
        in_specs=[full, h0["w_up"], h1["w_up"], h0["taps"], h1["taps"], h0["bias"], h1["bias"], wd_spec, full, vec, vec,
                  full],
        out_specs=[u_blk, pl.BlockSpec((tb, FFN_W), lambda t, j: (t, j)), full, vec, vec,
                   pl.BlockSpec((1, LANES), lambda t, j: (0, 0)), full],
        out_shape=[jax.ShapeDtypeStruct((2, seq, D_FF), F32), jax.ShapeDtypeStruct((seq, D_FF), BF16),
                   jax.ShapeDtypeStruct((seq, D_MODEL), F32), row, row, jax.ShapeDtypeStruct((1, LANES), F32),
                   jax.ShapeDtypeStruct((seq, D_MODEL), BF16)],
        scratch_shapes=[pltpu.VMEM((tb, D_MODEL), F32), pltpu.VMEM((FFN_J, 2, SUBLANES, FFN_W), F32)],
        compiler_params=_params(("arbitrary", "arbitrary"), _ffn_vmem(tb)),
    )(x2b, w_up, w_up, conv_w, conv_w, conv_b, conv_b, w_down, x2, ln_g, ln_b, target)


def _ffn_bwd(u, conv_w, conv_b, dz3b, dz3, w_down, w_up, z2, ln_g, ln_b):
    seq = dz3.shape[0]
    tb = min(FFN_TB, seq)
    nt = seq // tb
    row8 = tb // SUBLANES
    tb, full, vec, halves, wd_spec, u_blk = _ffn_common_specs(seq, lambda t: nt - 1 - t)
    halo = pl.BlockSpec((2, SUBLANES, FFN_W), lambda t, j: (0, jnp.maximum((nt - 1 - t) * row8 - 1, 0), j))

    def body(u_ref, halo_ref, tg_ref, tv_ref, bg_ref, bv_ref, dzb_ref, wd_ref, wg_ref, wv_ref, dz3_ref, z_ref, g_ref,
             b_ref, du_ref, dw_ref, dbias_ref, dz_ref, dg_ref, db_ref, dz2b_ref, acc, carry):
        t, j = pl.program_id(0), pl.program_id(1)

        @pl.when((t == 0) & (j == 0))
        def _():
            for r in (dw_ref, dbias_ref, dg_ref, db_ref):
                r[...] = jnp.zeros_like(r)

        pieces = [pl.ds(off, width) for off, width in FFN_PIECES]
        dzb = dzb_ref[...]
        dhs = [_dg(dzb, wd_ref[cols, :], 1, 1) for cols in pieces]
        first = t == nt - 1
        dus = []
        for cols, dh in zip(pieces, dhs):
            args = (jnp.where(first, 0.0, halo_ref[0, :, cols]), u_ref[0, :, cols],
                    jnp.where(first, 0.0, halo_ref[1, :, cols]), u_ref[1, :, cols],
                    *_conv_params(tg_ref, bg_ref, cols), *_conv_params(tv_ref, bv_ref, cols))
            _, vjp = jax.vjp(_ffn_mid, *args)
            dhg, dxg, dhv, dxv, g0, g1, g2, gb, v0, v1, v2, vb = vjp(dh)
            zeros = jnp.zeros((tb - SUBLANES, dh.shape[1]), F32)
            dug = (dxg + jnp.concatenate([zeros, jnp.where(t == 0, 0.0, carry[j, 0, :, cols])], axis=0)).astype(BF16)
            duv = (dxv + jnp.concatenate([zeros, jnp.where(t == 0, 0.0, carry[j, 1, :, cols])], axis=0)).astype(BF16)
            carry[j, 0, :, cols] = dhg
            carry[j, 1, :, cols] = dhv
            du_ref[0, :, cols] = dug
            du_ref[1, :, cols] = duv
            for half, parts in enumerate(((g0, g1, g2), (v0, v1, v2))):
                for d, p in enumerate(parts):
                    dw_ref[j, half, d:d + 1, cols] += p
            dbias_ref[j, 0, :, cols] += gb
            dbias_ref[j, 1, :, cols] += vb
            dus.append((dug, duv))
        part = None
        for cols, (dug, duv) in zip(pieces, dus):
            p = _dg(dug, wg_ref[:, cols], 1, 1) + _dg(duv, wv_ref[:, cols], 1, 1)
            part = p if part is None else part + p

        @pl.when(j == 0)
        def _():
            acc[...] = part

        @pl.when(j > 0)
        def _():
            acc[...] += part

        @pl.when(j == FFN_J - 1)
        def _():
            _, ln_vjp = jax.vjp(_layer_norm, z_ref[...], g_ref[...], b_ref[...])
            dz, dg, db = ln_vjp(acc[...] + ALPHA * dz3_ref[...])
            dz_ref[...] = dz
            dz2b_ref[...] = dz.astype(BF16)
            dg_ref[...] += dg
            db_ref[...] += db

    h0, h1 = halves
    row = jax.ShapeDtypeStruct((1, D_MODEL), F32)
    whole = lambda *shape: pl.BlockSpec(shape, lambda t, j: (0,) * len(shape))
    return _pcall(
        body, name="ffn_bwd", grid=(nt, FFN_J),
        in_specs=[u_blk, halo, h0["taps"], h1["taps"], h0["bias"], h1["bias"], full, wd_spec, h0["w_up"], h1["w_up"],
                  full, full, vec, vec],
        out_specs=[u_blk, whole(FFN_J, 2, FFN_CONV, FFN_W), whole(FFN_J, 2, 1, FFN_W), full, vec, vec, full],
        out_shape=[jax.ShapeDtypeStruct((2, seq, D_FF), BF16), jax.ShapeDtypeStruct((FFN_J, 2, FFN_CONV, FFN_W), F32),
                   jax.ShapeDtypeStruct((FFN_J, 2, 1, FFN_W), F32), jax.ShapeDtypeStruct((seq, D_MODEL), F32), row, row,
                   jax.ShapeDtypeStruct((seq, D_MODEL), BF16)],
        scratch_shapes=[pltpu.VMEM((tb, D_MODEL), F32), pltpu.VMEM((FFN_J, 2, SUBLANES, FFN_W), F32)],
        compiler_params=_params(("arbitrary", "arbitrary"), _ffn_vmem(tb)),
    )(u, u, conv_w, conv_w, conv_b, conv_b, dz3b, w_down, w_up, w_up, dz3, z2, ln_g, ln_b)


def _adamw_math(w, g, m, v):
    m_new = ADAM_B1 * m + (1.0 - ADAM_B1) * g
    v_new = ADAM_B2 * v + (1.0 - ADAM_B2) * jnp.square(g)
    m_hat = m_new / (1.0 - ADAM_B1 ** ADAM_STEP)
    v_hat = v_new / (1.0 - ADAM_B2 ** ADAM_STEP)
    return -ADAM_LR * (m_hat / (jnp.sqrt(v_hat) + ADAM_EPS) + ADAM_WD * w), m_new, v_new


def _adamw_many(name, ws, gs, ms, vs):
    n = len(ws)

    def body(*refs):
        w_refs, g_refs, m_refs, v_refs = (refs[i * n:(i + 1) * n] for i in range(4))
        d_refs, nm_refs, nv_refs = (refs[(4 + i) * n:(5 + i) * n] for i in range(3))
        for i in range(n):
            d_refs[i][...], nm_refs[i][...], nv_refs[i][...] = _adamw_math(
                w_refs[i][...], g_refs[i][...], m_refs[i][...], v_refs[i][...])

    vm = pl.BlockSpec(memory_space=pltpu.VMEM)
    outs = _pcall(
        body, pin=False, name=name, in_specs=[vm] * (4 * n), out_specs=[vm] * (3 * n),
        out_shape=[jax.ShapeDtypeStruct(w.shape, F32) for w in ws] * 3,
    )(*ws, *gs, *ms, *vs)
    return outs[:n], outs[n:2 * n], outs[2 * n:]


def _adamw_halves(name, core, w, mine, theirs, m, v):
    rows, cols = w.shape
    half_rows = mine.shape[0]
    tr = _tile(half_rows, (256, 176, 128))
    nbh = half_rows // tr
    assert 2 * half_rows == rows

    def body(c_ref, w_ref, a_ref, b_ref, m_ref, v_ref, g_ref, d_ref, nm_ref, nv_ref):
        g = jnp.where(pl.program_id(0) // nbh == c_ref[0], a_ref[...], b_ref[...])
        g_ref[...] = g
        d_ref[...], nm_ref[...], nv_ref[...] = _adamw_math(w_ref[...], g, m_ref[...], v_ref[...])

    spec = pl.BlockSpec((tr, cols), lambda i, c_ref: (i, 0))
    half = pl.BlockSpec((tr, cols), lambda i, c_ref: (i % nbh, 0))
    sh = jax.ShapeDtypeStruct((rows, cols), F32)
    grid_spec = pltpu.PrefetchScalarGridSpec(
        num_scalar_prefetch=1, grid=(rows // tr,), in_specs=[spec, half, half, spec, spec], out_specs=[spec] * 4)
    return _pcall(
        body, name=name, grid_spec=grid_spec, out_shape=[sh] * 4,
        compiler_params=_params(("arbitrary",), 18 * _nbytes((tr, -(-cols // LANES) * LANES), F32)),
    )(core, w, mine, theirs, m, v)


MESH = pl.DeviceIdType.MESH
ANY = pl.BlockSpec(memory_space=pl.ANY)
N_CHIPS = 4
BF16_ROWS = 16


def _me():
    return lax.axis_index("x"), lax.axis_index("y"), lax.axis_index("c")


def _other_chips(x, y):
    return [(1 - x, y), (x, 1 - y), (1 - x, 1 - y)]


def _remote(src, dst, ssem, rsem, dev):
    return pltpu.make_async_remote_copy(src_ref=src, dst_ref=dst, send_sem=ssem, recv_sem=rsem,
                                        device_id=dev, device_id_type=MESH)


def _half_rows(ref_rows, cc):
    half = ref_rows // 2
    return pl.ds(pl.multiple_of(cc * half, BF16_ROWS), half)


def _gather_weights(shards):
    n = len(shards)
    n_ici = n * (N_CHIPS - 1)

    def body(*refs):
        ins, outs, (ssem, rsem, lsem, lrsem) = refs[:n], refs[n:2 * n], refs[2 * n:]
        x, y, c = _me()
        k_me = 2 * x + y
        sib = (x, y, 1 - c)
        chips = _other_chips(x, y)
        started = []
        for i, (w_ref, o_ref) in enumerate(zip(ins, outs)):
            cp = _remote(w_ref, o_ref.at[k_me], lsem.at[i], lrsem.at[i], sib)
            cp.start()
            started.append(cp)
        for r, (px, py) in enumerate(chips):
            for i, (w_ref, o_ref) in enumerate(zip(ins, outs)):
                rows = _half_rows(w_ref.shape[0], c)
                s = r * n + i
                cp = _remote(w_ref.at[rows], o_ref.at[k_me, rows], ssem.at[s], rsem.at[s], (px, py, c))
                cp.start()
                started.append(cp)
        for r, (px, py) in enumerate(chips):
            for i, o_ref in enumerate(outs):
                blk = o_ref.at[2 * px + py, _half_rows(o_ref.shape[1], c)]
                s = r * n + i
                _remote(blk, blk, ssem.at[s], rsem.at[s], (px, py, c)).wait_recv()
                cp = _remote(blk, blk, ssem.at[n_ici + s], rsem.at[n_ici + s], sib)
                cp.start()
                started.append(cp)
        for r, (px, py) in enumerate(chips):
            for i, o_ref in enumerate(outs):
                blk = o_ref.at[2 * px + py, _half_rows(o_ref.shape[1], 1 - c)]
                s = n_ici + r * n + i
                _remote(blk, blk, ssem.at[s], rsem.at[s], sib).wait_recv()
        for cp in started[n:]:
            cp.wait_send()
        for cp in started[:n]:
            cp.wait()

    return _pcall(
        body, name="gather_weights", in_specs=[ANY] * n, out_specs=[ANY] * n,
        out_shape=[jax.ShapeDtypeStruct((N_CHIPS,) + s.shape, s.dtype) for s in shards],
        scratch_shapes=[pltpu.SemaphoreType.DMA((2 * n_ici,)), pltpu.SemaphoreType.DMA((2 * n_ici,)),
                        pltpu.SemaphoreType.DMA((n,)), pltpu.SemaphoreType.DMA((n,))],
    )(*shards)


def _swap_halves(name, grads):
    n = len(grads)

    def body(*refs):
        ins, outs, (ssem, rsem) = refs[:n], refs[n:2 * n], refs[2 * n:]
        x, y, c = _me()
        copies = []
        for i, (g_ref, o_ref) in enumerate(zip(ins, outs)):
            for k in range(N_CHIPS):
                s = i * N_CHIPS + k
                cp = _remote(g_ref.at[k, _half_rows(g_ref.shape[1], 1 - c)], o_ref.at[k], ssem.at[s], rsem.at[s],
                             (x, y, 1 - c))
                cp.start()
                copies.append(cp)
        for cp in copies:
            cp.wait()

    return _pcall(
        body, name=name, in_specs=[ANY] * n, out_specs=[ANY] * n,
        out_shape=[jax.ShapeDtypeStruct((N_CHIPS, g.shape[1] // 2, g.shape[2]), g.dtype) for g in grads],
        scratch_shapes=[pltpu.SemaphoreType.DMA((n * N_CHIPS,)), pltpu.SemaphoreType.DMA((n * N_CHIPS,))],
    )(*grads)


SEM = pl.BlockSpec(memory_space=pltpu.SEMAPHORE)
IN_HBM = pl.BlockSpec(memory_space=pltpu.HBM)
SPLIT_PARAMS = dict(compiler_params=pltpu.CompilerParams(has_side_effects=pltpu.SideEffectType.DATAFLOW_SIDE_EFFECTING))


def _split_start(name, sources, landings, n_copies, plan):
    ns, nl = len(sources), len(landings)

    def body(*refs):
        ins, lands, (ssem, rsem), token = refs[:ns], refs[ns:ns + nl], refs[ns + nl:ns + nl + 2], refs[-1]
        for s, (src, dst, _, dev) in enumerate(plan(ins, lands)):
            _remote(src, dst, ssem.at[s], rsem.at[s], dev).start()
        token[...] = jnp.zeros_like(token)

    arrays = list(sources) + list(landings)
    outs = _call(
        body, name=name, in_specs=[IN_HBM] * (ns + nl),
        out_specs=[SEM, SEM] + [IN_HBM] * (ns + nl) + [pl.BlockSpec(memory_space=pltpu.VMEM)],
        out_shape=[pltpu.SemaphoreType.DMA((n_copies,)), pltpu.SemaphoreType.DMA((n_copies,))]
        + [pltpu.HBM(a.shape, a.dtype) for a in arrays] + [jax.ShapeDtypeStruct((SUBLANES, LANES), F32)],
        input_output_aliases={i: 2 + i for i in range(ns + nl)}, **SPLIT_PARAMS,
    )(*[pltpu.with_memory_space_constraint(a, pltpu.HBM) for a in arrays])
    return (outs[:-1], ns), outs[-1]


def _split_wait(name, handle, after, plan):
    (ssem, rsem, *thru), ns = handle
    nl = len(thru) - ns

    def body(*refs):
        ins, lands, (ssem_ref, rsem_ref) = refs[:ns], refs[ns:ns + nl], refs[ns + nl:ns + nl + 2]
        for s, (src, _, dst, dev) in enumerate(plan(ins, lands)):
            cp = _remote(src, dst, ssem_ref.at[s], rsem_ref.at[s], dev)
            cp.wait_send()
            cp.wait_recv()

    outs = _call(
        body, name=name, in_specs=[IN_HBM] * (ns + nl) + [SEM, SEM, ANY], out_specs=[IN_HBM] * (ns + nl),
        out_shape=[pltpu.HBM(t.shape, t.dtype) for t in thru],
        input_output_aliases={i: i for i in range(ns + nl)}, **SPLIT_PARAMS,
    )(*thru, ssem, rsem, after)
    return outs[:ns], outs[ns:]


def _swap_plan(ins, lands):
    x, y, c = _me()
    return [(g_ref.at[k, _half_rows(g_ref.shape[1], 1 - c)], l_ref.at[k], l_ref.at[k], (x, y, 1 - c))
            for g_ref, l_ref in zip(ins, lands) for k in range(N_CHIPS)]


def _swap_start(name, grads):
    lands = [lax.empty((N_CHIPS, g.shape[1] // 2, g.shape[2]), g.dtype) for g in grads]
    return _split_start(name, grads, lands, len(grads) * N_CHIPS, _swap_plan)


def _swap_wait(name, handle, after):
    return _split_wait(name, handle, after, _swap_plan)


def _gather_plan(ins, lands):
    x, y, c = _me()
    k_me = 2 * x + y
    plan = [(w_ref, l_ref.at[k_me], l_ref.at[k_me], (x, y, 1 - c)) for w_ref, l_ref in zip(ins, lands)]
    for px, py in _other_chips(x, y):
        for w_ref, l_ref in zip(ins, lands):
            rows = _half_rows(w_ref.shape[0], c)
            plan.append((w_ref.at[rows], l_ref.at[k_me, rows], l_ref.at[2 * px + py, rows], (px, py, c)))
    return plan


def _gather_start(name, shards):
    lands = [lax.empty((N_CHIPS,) + s.shape, s.dtype) for s in shards]
    return _split_start(name, shards, lands, len(shards) * N_CHIPS, _gather_plan)


def _gather_wait(name, handle, after):
    return _split_wait(name, handle, after, _gather_plan)[1]


def _forward_halves(name, blocks):
    n = len(blocks)
    n_sem = n * (N_CHIPS - 1)

    def body(*refs):
        outs, (ssem, rsem) = refs[n:2 * n], refs[2 * n:]
        x, y, c = _me()
        sib = (x, y, 1 - c)
        chips = _other_chips(x, y)
        sends = []
        for r, (px, py) in enumerate(chips):
            for i, o_ref in enumerate(outs):
                blk = o_ref.at[2 * px + py, _half_rows(o_ref.shape[1], c)]
                cp = _remote(blk, blk, ssem.at[r * n + i], rsem.at[r * n + i], sib)
                cp.start()
                sends.append(cp)
        for r, (px, py) in enumerate(chips):
            for i, o_ref in enumerate(outs):
                blk = o_ref.at[2 * px + py, _half_rows(o_ref.shape[1], 1 - c)]
                _remote(blk, blk, ssem.at[r * n + i], rsem.at[r * n + i], sib).wait_recv()
        for cp in sends:
            cp.wait_send()

    return _pcall(
        body, name=name, in_specs=[ANY] * n, out_specs=[ANY] * n,
        out_shape=[jax.ShapeDtypeStruct(b.shape, b.dtype) for b in blocks],
        input_output_aliases={i: i for i in range(n)},
        scratch_shapes=[pltpu.SemaphoreType.DMA((n_sem,)), pltpu.SemaphoreType.DMA((n_sem,))],
    )(*blocks)


def _scatter_plan(ins, lands):
    x, y, c = _me()
    k_me = 2 * x + y
    return [(p_ref.at[2 * px + py], l_ref.at[k_me], l_ref.at[2 * px + py], (px, py, c))
            for px, py in _other_chips(x, y) for p_ref, l_ref in zip(ins, lands)]


def _scatter_start(name, parts):
    lands = [lax.empty(p.shape, p.dtype) for p in parts]
    return _split_start(name, parts, lands, len(parts) * (N_CHIPS - 1), _scatter_plan)


def _scatter_wait(name, handle, after):
    return _split_wait(name, handle, after, _scatter_plan)[1]


def _share_and_reduce(halves, v):
    n = len(halves)
    rows = v.shape[0]
    half = rows // 2
    assert half % SUBLANES == 0

    def body(*refs):
        ins, v_ref, outs, out_ref = refs[:n], refs[n], refs[n + 1:2 * n + 1], refs[2 * n + 1]
        pair_buf, mine, chip_buf, ssem, rsem, half_ssem, half_rsem = refs[2 * n + 2:]
        x, y, c = _me()
        k_me = 2 * x + y
        sib = (x, y, 1 - c)
        copies = [_remote(r_ref, o_ref, half_ssem.at[i], half_rsem.at[i], sib)
                  for i, (r_ref, o_ref) in enumerate(zip(ins, outs))]
        for cp in copies:
            cp.start()

        def rows_of(cc):
            return pl.ds(pl.multiple_of(cc * half, SUBLANES), half)

        swap = _remote(v_ref.at[rows_of(1 - c)], pair_buf, ssem.at[0], rsem.at[0], sib)
        swap.start()
        swap.wait()
        mine[...] = v_ref[rows_of(c), :] + pair_buf[...]
        chip_buf[k_me] = mine[...]
        sends = [_remote(mine, chip_buf.at[k_me], ssem.at[1 + r], rsem.at[1 + r], (px, py, c))
                 for r, (px, py) in enumerate(_other_chips(x, y))]
        for cp in sends:
            cp.start()
        for r, (px, py) in enumerate(_other_chips(x, y)):
            blk = chip_buf.at[2 * px + py]
            _remote(blk, blk, ssem.at[1 + r], rsem.at[1 + r], (px, py, c)).wait_recv()
        total = chip_buf[0]
        for k in range(1, N_CHIPS):
            total = total + chip_buf[k]
        out_ref[rows_of(c), :] = total
        for cp in sends:
            cp.wait_send()
        share = _remote(out_ref.at[rows_of(c)], out_ref.at[rows_of(c)], ssem.at[N_CHIPS], rsem.at[N_CHIPS], sib)
        share.start()
        got = out_ref.at[rows_of(1 - c)]
        _remote(got, got, ssem.at[N_CHIPS], rsem.at[N_CHIPS], sib).wait_recv()
        share.wait_send()
        for cp in copies:
            cp.wait()

    vm = pl.BlockSpec(memory_space=pltpu.VMEM)
    outs = _call(
        body, name="share_and_reduce", in_specs=[ANY] * n + [vm], out_specs=[ANY] * n + [vm],
        out_shape=[pltpu.HBM(h.shape, h.dtype) for h in halves] + [jax.ShapeDtypeStruct((rows, LANES), F32)],
        scratch_shapes=[pltpu.VMEM((half, LANES), F32), pltpu.VMEM((half, LANES), F32),
                        pltpu.VMEM((N_CHIPS, half, LANES), F32), pltpu.SemaphoreType.DMA((N_CHIPS + 1,)),
                        pltpu.SemaphoreType.DMA((N_CHIPS + 1,)), pltpu.SemaphoreType.DMA((n,)),
                        pltpu.SemaphoreType.DMA((n,))],
        compiler_params=pltpu.CompilerParams(vmem_limit_bytes=32 * 1024 * 1024),
    )(*[pltpu.with_memory_space_constraint(h, pltpu.HBM) for h in halves], v)
    return outs[:n], outs[n]


def _add_pair(name, core, chip, g, theirs):
    _, half, cols = theirs.shape
    tr = _tile(half, (256, 176, 128))
    nb = half // tr

    def body(c_ref, k_ref, g_ref, t_ref, o32_ref, o16_ref):
        s = g_ref[...] + t_ref[...]
        o16_ref[...] = s.astype(BF16)

        @pl.when(pl.program_id(1) == k_ref[0])
        def _():
            o32_ref[...] = s

    spec = pl.BlockSpec((None, tr, cols), lambda i, k, c_ref, k_ref: (k, i, 0))
    grid_spec = pltpu.PrefetchScalarGridSpec(
        num_scalar_prefetch=2, grid=(nb, N_CHIPS),
        in_specs=[pl.BlockSpec((None, tr, cols), lambda i, k, c_ref, k_ref: (k, c_ref[0] * nb + i, 0)), spec],
        out_specs=[pl.BlockSpec((tr, cols), lambda i, k, c_ref, k_ref: (i, 0)), spec])
    return _pcall(
        body, name=name, grid_spec=grid_spec,
        out_shape=[jax.ShapeDtypeStruct((half, cols), F32), jax.ShapeDtypeStruct(theirs.shape, BF16)],
        compiler_params=_params(("arbitrary", "arbitrary"), 8 * _nbytes((tr, cols + LANES), F32)),
    )(core, chip, g, theirs)


def _add_chips(name, chip, p32, recv):
    half, cols = p32.shape
    tr = _tile(half, (256, 176, 128))

    def body(k_ref, p_ref, r0_ref, r1_ref, r2_ref, o_ref):
        o_ref[...] = ((p_ref[...] + r0_ref[...].astype(F32)) + r1_ref[...].astype(F32)) + r2_ref[...].astype(F32)

    def other(r):
        return pl.BlockSpec((None, tr, cols), lambda i, k_ref: (r + (k_ref[0] <= r).astype(jnp.int32), i, 0))
    grid_spec = pltpu.PrefetchScalarGridSpec(
        num_scalar_prefetch=1, grid=(half // tr,),
        in_specs=[pl.BlockSpec((tr, cols), lambda i, k_ref: (i, 0)), other(0), other(1), other(2)],
        out_specs=pl.BlockSpec((tr, cols), lambda i, k_ref: (i, 0)))
    return _pcall(
        body, name=name, grid_spec=grid_spec, out_shape=jax.ShapeDtypeStruct((half, cols), F32),
        compiler_params=_params(("arbitrary",), 10 * _nbytes((tr, cols + LANES), F32)),
    )(chip, p32, recv, recv, recv)


def kernel(x, mem, w_in, b_in, hg_lb_logits, hg_norm_w, ml_conv_w, ml_conv_b, ml_norm_w, w_out, ln1_g, ln1_b, ca_wq, ca_wkv, ca_wo, ln2_g, ln2_b, ffn_w_up, ffn_conv_w, ffn_conv_b, ffn_w_down, ln3_g, ln3_b, loss_target, m_w_in, m_b_in, m_hg_lb_logits, m_hg_norm_w, m_ml_conv_w, m_ml_conv_b, m_ml_norm_w, m_w_out, m_ln1_g, m_ln1_b, m_ca_wq, m_ca_wkv, m_ca_wo, m_ln2_g, m_ln2_b, m_ffn_w_up, m_ffn_conv_w, m_ffn_conv_b, m_ffn_w_down, m_ln3_g, m_ln3_b, v_w_in, v_b_in, v_hg_lb_logits, v_hg_norm_w, v_ml_conv_w, v_ml_conv_b, v_ml_norm_w, v_w_out, v_ln1_g, v_ln1_b, v_ca_wq, v_ca_wkv, v_ca_wo, v_ln2_g, v_ln2_b, v_ffn_w_up, v_ffn_conv_w, v_ffn_conv_b, v_ffn_w_down, v_ln3_g, v_ln3_b):
    return _train_step(dict(locals()))


WEIGHTS = ("w_in", "b_in", "hg_lb_logits", "hg_norm_w", "ml_conv_w", "ml_conv_b", "ml_norm_w", "w_out", "ln1_g",
           "ln1_b", "ca_wq", "ca_wkv", "ca_wo", "ln2_g", "ln2_b", "ffn_w_up", "ffn_conv_w", "ffn_conv_b",
           "ffn_w_down", "ln3_g", "ln3_b")
MATRICES = ("w_in", "w_out", "ca_wq", "ca_wkv", "ca_wo", "ffn_w_up", "ffn_w_down")
COL_SHARDED = ("w_in", "ca_wkv", "ffn_w_up", "ml_conv_w", "ffn_conv_w")
SMALL = tuple(n for n in WEIGHTS if n not in MATRICES)
PART_ROWS = 16


def _part_rows(shape):
    n = 1
    for s in shape:
        n *= s
    return -(-n // (LANES * PART_ROWS)) * PART_ROWS


def _pack(arrs, dtype):
    parts = []
    for a in arrs:
        flat = a.reshape(-1).astype(dtype)
        flat = jnp.pad(flat, (0, _part_rows(a.shape) * LANES - flat.shape[0]))
        parts.append(flat.reshape(-1, LANES))
    return jnp.concatenate(parts, axis=0)


def _unpack(buf, shapes):
    lead = buf.shape[:-2]
    outs, r = [], 0
    for sh in shapes:
        n = 1
        for s in sh:
            n *= s
        nr = _part_rows(sh)
        flat = buf[..., r:r + nr, :].reshape(lead + (nr * LANES,))
        outs.append(flat[..., :n].reshape(lead + tuple(sh)))
        r += nr
    return outs


def _cat_cols(s):
    return jnp.moveaxis(s, 0, 1).reshape(s.shape[1], -1)


def _stack_rows(s):
    return s.reshape(-1, s.shape[-1])


def _train_step(a):
    xs, mems, tgt = a["x"][0], a["mem"][0], a["loss_target"][0]
    core = lax.axis_index("c").astype(jnp.int32).reshape(1)
    chip = (2 * lax.axis_index("x") + lax.axis_index("y")).astype(jnp.int32).reshape(1)
    k_me = chip[0]
    shard = {n: a[n][0] for n in MATRICES}

    later = [n for n in MATRICES if n != "w_in"]
    w_in, taps = _gather_weights([shard["w_in"].astype(BF16), _pack([a["ml_conv_w"][0], a["ffn_conv_w"][0]], F32)])
    w = {"w_in": jnp.pad(_cat_cols(w_in), ((0, 0), (0, D_IN_PAD - D_IN)))}
    gathering, token = _gather_start("gather_start", [shard[n].astype(BF16) for n in later])
    ml_cw, ffn_cw = [_cat_cols(s) for s in _unpack(taps, [a["ml_conv_w"].shape[1:], a["ffn_conv_w"].shape[1:]])]
    b_in_p = jnp.pad(a["b_in"], ((0, 0), (0, D_IN_PAD - D_IN))) + token[0:1, 0:1]
    mixer_w = (a["hg_lb_logits"], a["hg_norm_w"], ml_cw, a["ml_conv_b"], a["ml_norm_w"])
    up_cols = a["ffn_w_up"].shape[-1]

    proj, xb = _mm("proj", "nn", xs, w["w_in"], bias=b_in_p, a_copy_dtype=BF16, tm=256, tn=D_IN_PAD)
    y, hst, cst, nst, mst = _mixer_fwd(proj, *mixer_w)
    w.update(zip(later, _forward_halves("forward_halves", _gather_wait("gather_wait", gathering, y))))
    for n in ("w_out", "ca_wq", "ca_wo", "ffn_w_down"):
        w[n] = _stack_rows(w[n])
    z1, x1, x1b = _mm("mix_out", "nn", y, w["w_out"], res=xs, res_scale=ALPHA, ln=("fwd", a["ln1_g"], a["ln1_b"]),
                      copy_dtype=BF16)
    q = _mm("ca_q", "nn", x1b, w["ca_wq"], out_dtype=BF16, tn=D_MODEL)
    kv = _mm("ca_kv", "nn", mems, w["ca_wkv"])
    o = _attn_fwd(q, kv)
    z2, x2, x2b = _mm("ca_out", "nn", o, w["ca_wo"], res=x1, res_scale=ALPHA, ln=("fwd", a["ln2_g"], a["ln2_b"]),
                      copy_dtype=BF16)
    w_up = w["ffn_w_up"]
    assert w_up.shape == (2 * FFN_J, D_MODEL, FFN_W)
    u, hmid, dz3, g_ln3g, g_ln3b, loss_part, dz3b = _ffn_fwd(
        x2b, x2, w_up, ffn_cw, a["ffn_conv_b"], w["ffn_w_down"], a["ln3_g"], a["ln3_b"], tgt)

    grads = {"ln3_g": g_ln3g, "ln3_b": g_ln3b}
    grads["ffn_w_down"] = _mm("g_w_down", "tn", hmid, dz3b, tm=D_FF // 2, tn=D_MODEL)
    du, g_cw, g_cb, dz2, grads["ln2_g"], grads["ln2_b"], dz2b = _ffn_bwd(
        u, ffn_cw, a["ffn_conv_b"], dz3b, dz3, w["ffn_w_down"], w_up, z2, a["ln2_g"], a["ln2_b"])
    grads["ffn_conv_w"] = jnp.transpose(g_cw, (2, 1, 0, 3)).reshape(FFN_CONV, 2 * D_FF)
    grads["ffn_conv_b"] = jnp.transpose(g_cb, (2, 1, 0, 3)).reshape(1, 2 * D_FF)
    grads["ffn_w_up"] = _mm("g_w_up", "tn", x2b, du, out_groups=N_CHIPS, tm=D_MODEL, tn=up_cols)
    grads["ffn_w_down"] = grads["ffn_w_down"].reshape((N_CHIPS,) + shard["ffn_w_down"].shape)
    pending = {}

    def reduce_start(tag, names, swapped=None):
        group = [grads[n] for n in names]
        group, theirs = swapped or (group, _swap_halves("swap_halves_" + tag, group))
        sums = [_add_pair("add_pair_" + n, core, chip, g, t) for n, g, t in zip(names, group, theirs)]
        handle, token = _scatter_start("scatter_start_" + tag, [s16 for _, s16 in sums])
        pending[tag] = (names, [s32 for s32, _ in sums], handle)
        return token[0:1, 0:1]

    ffn = ("ffn_w_up", "ffn_w_down")
    swapping, token = _swap_start("swap_start_ffn", [grads[n] for n in ffn])
    do = _mm("d_o", "nt", dz2b, w["ca_wo"], bias=jnp.zeros((1, D_MODEL), F32) + token[0:1, 0:1], out_dtype=BF16,
             tn=D_MODEL)
    grads["ca_wo"] = _mm("g_wo", "tn", o, dz2b, tm=D_MODEL // 2, tn=D_MODEL)
    zero = reduce_start("ffn", ffn, _swap_wait("swap_wait_ffn", swapping, grads["ca_wo"]))
    dq, dkv = _attn_bwd(q, kv + zero, do)
    grads["ca_wq"] = _mm("g_wq", "tn", x1b, dq, tm=D_MODEL // 2, tn=D_MODEL)
    grads["ca_wkv"] = _mm("g_wkv", "tn", mems, dkv, out_groups=N_CHIPS, tm=D_MODEL)
    dz1, grads["ln1_g"], grads["ln1_b"], dz1b = _mm("d_x1", "nt", dq, w["ca_wq"], res=dz2, res_scale=ALPHA,
                                                    ln=("bwd", z1, a["ln1_g"], a["ln1_b"]), copy_dtype=BF16)
    grads["w_out"] = _mm("g_w_out", "tn", y, dz1b, tm=D_MODEL // 2, tn=D_MODEL)
    for n in ("w_out", "ca_wq", "ca_wo"):
        grads[n] = grads[n].reshape((N_CHIPS,) + shard[n].shape)
    attn = ("w_out", "ca_wq", "ca_wkv", "ca_wo")
    swapping, token = _swap_start("swap_start_attn", [grads[n] for n in attn])
    dy = _mm("d_y", "nt", dz1b, w["w_out"], bias=jnp.zeros((1, D_MODEL), F32) + token[0:1, 0:1], tn=D_MODEL)
    zero = reduce_start("attn", attn, _swap_wait("swap_wait_attn", swapping, dy))
    (dproj, g_b_in, grads["hg_lb_logits"], grads["hg_norm_w"], grads["ml_conv_w"], grads["ml_conv_b"],
     grads["ml_norm_w"]) = _mixer_bwd(proj, dy, hst, cst, nst, mst, mixer_w[0], mixer_w[1] + zero, *mixer_w[2:])
    g_in = _mm("g_w_in", "tn", xb, dproj, tm=D_MODEL, tn=up_cols)[:, :D_IN]
    grads["w_in"] = jnp.moveaxis(g_in.reshape(D_MODEL, N_CHIPS, -1), 1, 0)
    grads["b_in"] = g_b_in[:, :D_IN]
    zero = reduce_start("in", ("w_in",))
    dx = _mm("d_x", "nt", dproj, w["w_in"], bias=jnp.zeros((1, D_MODEL), F32) + zero, res=dz1, res_scale=ALPHA,
             tm=256, tn=D_MODEL)

    halves = {}
    for tag, (names, sums32, handle) in pending.items():
        for n, s32, r in zip(names, sums32, _scatter_wait("scatter_wait_" + tag, handle, dx)):
            halves[n] = _add_chips("add_chips_" + n, chip, s32, r)
    halves = [halves[n] for n in MATRICES]

    small_shapes = [grads[n].shape for n in SMALL] + [loss_part.shape]
    other_halves, summed = _share_and_reduce(halves, _pack([grads[n] for n in SMALL] + [loss_part], F32))
    summed = _unpack(summed, small_shapes)
    loss = summed[-1][0, 0]
    for n, g in zip(SMALL, summed[:-1]):
        if n in COL_SHARDED:
            cols = a[n].shape[-1]
            g = lax.dynamic_slice_in_dim(g, k_me * cols, cols, axis=1)
        grads[n] = g

    delta, new_m, new_v = {}, {}, {}
    for n, mine, theirs in zip(MATRICES, halves, other_halves):
        grads[n], delta[n], new_m[n], new_v[n] = _adamw_halves(
            "adamw_" + n, core, shard[n], mine, theirs, a["m_" + n][0], a["v_" + n][0])
    small_w = [a[n][0] if a[n].ndim == 3 else a[n] for n in SMALL]
    small_m = [a["m_" + n][0] if a[n].ndim == 3 else a["m_" + n] for n in SMALL]
    small_v = [a["v_" + n][0] if a[n].ndim == 3 else a["v_" + n] for n in SMALL]
    for out, vals in zip((delta, new_m, new_v),
                         _adamw_many("adamw_small", small_w, [grads[n] for n in SMALL], small_m, small_v)):
        out.update(zip(SMALL, vals))

    def shaped(d):
        return [d[n].reshape(a[n].shape) for n in WEIGHTS]
    return (loss, dx[None], *shaped(grads), *shaped(delta), *shaped(new_m), *shaped(new_v))
```

```python
import functools

import jax
import jax.numpy as jnp
from jax import lax
from jax.experimental import pallas as pl
from jax.experimental.pallas import tpu as pltpu

F32 = jnp.float32
BF16 = jnp.bfloat16

D_MODEL = 1024
HEADS = 4
DK = 128
D_GRP = HEADS * DK
CHUNK = 64
ML_CONV = 4
FFN_CONV = 3
D_FF = 2816
CA_DH = D_MODEL // HEADS
DEPTH = 1
ALPHA = (2.0 * DEPTH) ** 0.25
LN_EPS = 1e-5
NEG_BIG = -1e30
D_IN = 8 * D_GRP + 2 * HEADS
D_IN_PAD = 8 * D_GRP + 128
ADAM_LR, ADAM_B1, ADAM_B2, ADAM_EPS, ADAM_WD, ADAM_STEP = 0.001, 0.9, 0.999, 1e-08, 0.01, 10

SUBLANES = 8
LANES = 128
VMEM_BYTES = 64 * 1024 * 1024


def _pcall(body, pin=True, **kw):
    if not pin:
        return _call(body, **kw)
    kw["out_shape"] = jax.tree.map(lambda s: pltpu.HBM(s.shape, s.dtype), kw["out_shape"])
    call = _call(body, **kw)

    def pinned(*args):
        return call(*[pltpu.with_memory_space_constraint(x, pltpu.HBM) if jnp.issubdtype(x.dtype, jnp.floating) else x
                      for x in args])
    return pinned


def _call(body, **kw):
    return pl.pallas_call(body, **kw)


def _params(semantics, vmem_bytes):
    limit = int(min(max(2 * vmem_bytes, 16 * 1024 * 1024), VMEM_BYTES - 8 * 1024 * 1024))
    return pltpu.CompilerParams(dimension_semantics=semantics, vmem_limit_bytes=limit)


def _nbytes(shape, dtype):
    n = 1
    for s in shape:
        n *= s
    return n * jnp.dtype(dtype).itemsize


def _dg(a, b, ca, cb):
    return lax.dot_general(a.astype(BF16), b.astype(BF16), (((ca,), (cb,)), ((), ())),
                           preferred_element_type=F32)


@jax.custom_vjp
def mm_nn(a, b):
    return _dg(a, b, 1, 0)


mm_nn.defvjp(lambda a, b: (_dg(a, b, 1, 0), (a, b)),
             lambda r, g: (_dg(g, r[1], 1, 1).astype(r[0].dtype), _dg(r[0], g, 0, 0).astype(r[1].dtype)))


@jax.custom_vjp
def mm_nt(a, b):
    return _dg(a, b, 1, 1)


mm_nt.defvjp(lambda a, b: (_dg(a, b, 1, 1), (a, b)),
             lambda r, g: (_dg(g, r[1], 1, 0).astype(r[0].dtype), _dg(g, r[0], 0, 0).astype(r[1].dtype)))


@jax.custom_vjp
def mm_tn(a, b):
    return _dg(a, b, 0, 0)


mm_tn.defvjp(lambda a, b: (_dg(a, b, 0, 0), (a, b)),
             lambda r, g: (_dg(r[1], g, 1, 1).astype(r[0].dtype), _dg(r[0], g, 1, 0).astype(r[1].dtype)))


def _tri(n, lower):
    r = lax.broadcasted_iota(jnp.int32, (n, n), 0)
    c = lax.broadcasted_iota(jnp.int32, (n, n), 1)
    return ((r >= c) if lower else (r <= c)).astype(F32)


def _tri_dot(lower, x):
    t = _tri(x.shape[0], lower).astype(BF16)
    hi = x.astype(BF16)
    rest = x - hi.astype(F32)
    mid = rest.astype(BF16)
    lo = (rest - mid.astype(F32)).astype(BF16)
    return sum(lax.dot_general(t, p, (((1,), (0,)), ((), ())), preferred_element_type=F32) for p in (hi, mid, lo))


@jax.custom_vjp
def cumsum_rows(x):
    return _tri_dot(True, x)


cumsum_rows.defvjp(lambda x: (_tri_dot(True, x), None), lambda _, g: (_tri_dot(False, g),))


def _shift_impl(halo, x, d):
    xx = jnp.concatenate([halo, x], axis=0)
    return pltpu.roll(xx, d, 0)[SUBLANES:]


@functools.partial(jax.custom_vjp, nondiff_argnums=(2,))
def shift_rows(halo, x, d):
    return _shift_impl(halo, x, d)


def _shift_bwd(d, _, g):
    n = g.shape[0] + SUBLANES
    gg = jnp.concatenate([jnp.zeros((SUBLANES, g.shape[1]), g.dtype), g], axis=0)
    r = pltpu.roll(gg, n - d, 0)
    return r[:SUBLANES], r[SUBLANES:]


shift_rows.defvjp(lambda halo, x, d: (_shift_impl(halo, x, d), None), _shift_bwd)


def causal_conv(halo, x, w_rows, b):
    k = len(w_rows)
    y = b + w_rows[k - 1] * x
    for d in range(1, k):
        y = y + w_rows[k - 1 - d] * shift_rows(halo, x, d)
    return y


def _sigmoid(x):
    return 1.0 / (1.0 + jnp.exp(-x))


def _silu(x):
    return x * _sigmoid(x)


def _log_sigmoid(x):
    return jnp.minimum(x, 0.0) - jnp.log(1.0 + jnp.exp(-jnp.abs(x)))


def _pick_row(x, i):
    row = lax.broadcasted_iota(jnp.int32, (x.shape[0], 1), 0)
    return jnp.sum(jnp.where(row == i, x, 0.0), axis=0, keepdims=True)


def _layer_norm(z, g, b):
    mu = jnp.mean(z, axis=-1, keepdims=True)
    zc = z - mu
    var = jnp.mean(zc * zc, axis=-1, keepdims=True)
    return zc * lax.rsqrt(var + LN_EPS) * g + b


def _qk_conv(halo, x, w0, w1, w2, w3, b):
    return _silu(causal_conv(halo, x, (w0, w1, w2, w3), b))


def _grp(i):
    return pl.ds(i * D_GRP, D_GRP)


def _mixer_specs(n_chunks, reverse):
    def chunk(c):
        return n_chunks - 1 - c if reverse else c
    row8 = CHUNK // SUBLANES
    proj_spec = pl.BlockSpec((CHUNK, D_IN_PAD), lambda c: (chunk(c), 0))
    halo_spec = pl.BlockSpec((SUBLANES, 2 * D_GRP), lambda c: (jnp.maximum(chunk(c) * row8 - 1, 0), 2))
    small = [pl.BlockSpec((2, D_GRP), lambda c: (0, 0)), pl.BlockSpec((1, D_GRP), lambda c: (0, 0)),
             pl.BlockSpec((ML_CONV, 2 * D_GRP), lambda c: (0, 0)), pl.BlockSpec((1, 2 * D_GRP), lambda c: (0, 0)),
             pl.BlockSpec((1, D_GRP), lambda c: (0, 0))]
    state_specs = [pl.BlockSpec((1, HEADS, DK, DK), lambda c: (chunk(c), 0, 0, 0)),
                   pl.BlockSpec((1, HEADS, DK, DK), lambda c: (chunk(c), 0, 0, 0)),
                   pl.BlockSpec((1, HEADS, 1, DK), lambda c: (chunk(c), 0, 0, 0)),
                   pl.BlockSpec((1, HEADS, 1, DK), lambda c: (chunk(c), 0, 0, 0))]
    y_spec = pl.BlockSpec((CHUNK, 2 * D_GRP), lambda c: (chunk(c), 0))
    return proj_spec, halo_spec, small, state_specs, y_spec, chunk


def _heads(x):
    return [x[:, h * DK:(h + 1) * DK] for h in range(HEADS)]


def _last(x, j):
    lane = lax.broadcasted_iota(jnp.int32, (1, x.shape[-1]), 1)
    return jnp.sum(jnp.where(lane == j, x, 0.0), axis=-1, keepdims=True)


def _hg_chunk(st_t, hq, hf, hi, hgate, l0, l1, nw):
    n = hq.shape[0]
    lb = _sigmoid(l0 - l1)
    q = _silu(hq)
    lf = jnp.log(lb + (1.0 - lb) * _sigmoid(hf))
    k = (1.0 - lb) * _sigmoid(-hf)
    b = cumsum_rows(lf)
    b_ref = _pick_row(b, n // 2 - 1)
    b_last = _pick_row(b, n - 1)
    qa, ka =_heads(q * jnp.exp(b - b_ref)), _heads(k * jnp.exp(b_ref - b))
    qe, kd, eb, v = _heads(q * jnp.exp(b)), _heads(k * jnp.exp(b_last - b)), _heads(jnp.exp(b_last)), _heads(hi)
    tri = _tri(n, True) > 0
    attn = [jnp.where(tri, mm_nt(qa[h], ka[h]), 0.0) for h in range(HEADS)]
    o = [mm_nn(attn[h], v[h]) + mm_nt(qe[h], st_t[h]) for h in range(HEADS)]
    st_new = jnp.stack([eb[h] * st_t[h] + mm_tn(v[h], kd[h]) for h in range(HEADS)])
    yn = [o[h] * lax.rsqrt(jnp.mean(o[h] * o[h], axis=-1, keepdims=True) + LN_EPS) for h in range(HEADS)]
    return st_new, jnp.concatenate(yn, axis=1) * nw * _silu(hgate)


def _ml_chunk(c_st, n_st, m_st, q, k, v, gates, og, nw):
    n = q.shape[0]
    ig = jnp.stack([_last(gates, h) for h in range(HEADS)])
    log_f = _log_sigmoid(gates)
    fl = jnp.stack([_last(log_f, HEADS + h) for h in range(HEADS)])
    bw = cumsum_rows(jnp.concatenate([jnp.broadcast_to(fl[h], (n, DK)) for h in range(HEADS)], axis=1))
    b = jnp.stack([_last(x, 0) for x in _heads(bw)])
    g = jnp.sum(fl, axis=1, keepdims=True)
    eye = lax.broadcasted_iota(jnp.int32, (n, n), 0) == lax.broadcasted_iota(jnp.int32, (n, n), 1)
    e_row = jnp.sum(jnp.where(eye, ig - b, 0.0), axis=1, keepdims=True)
    d = jnp.where(_tri(n, True) > 0, b + e_row, -jnp.inf)
    inter = b + m_st
    m_t = jnp.maximum(inter, jnp.max(d, axis=2, keepdims=True))
    qs, kh, vh = _heads(q * (DK ** -0.5)), _heads(k), _heads(v)
    s = jnp.stack([mm_nt(qs[h], kh[h]) for h in range(HEADS)]) * jnp.exp(d - m_t)
    w_inter = jnp.exp(inter - m_t)
    num = (jnp.stack([mm_nn(s[h], vh[h]) for h in range(HEADS)])
           + w_inter * jnp.stack([mm_nn(qs[h], c_st[h]) for h in range(HEADS)]))
    den = jnp.sum(s, axis=2, keepdims=True) + w_inter * jnp.sum(jnp.stack(qs) * n_st, axis=2, keepdims=True)
    h_out = num / jnp.maximum(jnp.abs(den), jnp.exp(-m_t))
    a = g - b + ig
    m_new = jnp.maximum(g + m_st, jnp.max(a, axis=1, keepdims=True))
    decay = jnp.exp(g + m_st - m_new)
    wk = jnp.stack(kh) * jnp.exp(a - m_new)
    c_new = decay * c_st + jnp.stack([mm_tn(wk[h], vh[h]) for h in range(HEADS)])
    n_new = decay * n_st + jnp.sum(wk, axis=1, keepdims=True)
    hc = h_out - jnp.mean(h_out, axis=-1, keepdims=True)
    yn = hc * lax.rsqrt(jnp.mean(hc * hc, axis=-1, keepdims=True) + LN_EPS)
    y = _sigmoid(og) * (jnp.concatenate([yn[h] for h in range(HEADS)], axis=1) * nw)
    return c_new, n_new, m_new, y


def _mixer_inputs(proj_ref, lg_ref, hnw_ref, mnw_ref, qk):
    hg_in = (proj_ref[:, _grp(0)], proj_ref[:, _grp(1)], proj_ref[:, _grp(2)], proj_ref[:, _grp(3)],
             lg_ref[0:1, :], lg_ref[1:2, :], hnw_ref[...])
    ml_in = (qk[:, :D_GRP], qk[:, D_GRP:], proj_ref[:, _grp(6)], proj_ref[:, pl.ds(8 * D_GRP, LANES)],
             proj_ref[:, _grp(7)], mnw_ref[...])
    return hg_in, ml_in


def _mixer_fwd(proj, lb_logits, hg_nw, conv_w, conv_b, ml_nw):
    seq = proj.shape[0]
    n_chunks = seq // CHUNK
    proj_spec, halo_spec, small, state_specs, y_spec, _ = _mixer_specs(n_chunks, False)

    def body(proj_ref, halo_ref, lg_ref, hnw_ref, cw_ref, cb_ref, mnw_ref,
             y_ref, hst_ref, cst_ref, nst_ref, mst_ref, hs, cs, ns, ms):
        c = pl.program_id(0)

        @pl.when(c == 0)
        def _():
            hs[...] = jnp.zeros_like(hs)
            cs[...] = jnp.zeros_like(cs)
            ns[...] = jnp.zeros_like(ns)
            ms[...] = jnp.full(ms.shape, NEG_BIG, F32)

        hst_ref[0] = hs[...]
        cst_ref[0] = cs[...]
        nst_ref[0] = ns[...]
        mst_ref[0] = ms[...]
        halo = jnp.where(c > 0, halo_ref[...], 0.0)
        qk = _qk_conv(halo, proj_ref[:, pl.ds(4 * D_GRP, 2 * D_GRP)],
                      cw_ref[0:1, :], cw_ref[1:2, :], cw_ref[2:3, :], cw_ref[3:4, :], cb_ref[...])
        hg_in, ml_in = _mixer_inputs(proj_ref, lg_ref, hnw_ref, mnw_ref, qk)
        hs[...], y_hg = _hg_chunk(hs[...], *hg_in)
        cs[...], ns[...], m_new, y_ml = _ml_chunk(cs[...], ns[...], _last(ms[...], 0), *ml_in)
        ms[...] = jnp.broadcast_to(m_new, ms.shape)
        y_ref[:, pl.ds(0, D_GRP)] = y_hg.astype(BF16)
        y_ref[:, pl.ds(D_GRP, D_GRP)] = y_ml.astype(BF16)

    st = jax.ShapeDtypeStruct((n_chunks, HEADS, DK, DK), F32)
    vec = jax.ShapeDtypeStruct((n_chunks, HEADS, 1, DK), F32)
    vmem = 2 * (_nbytes((CHUNK, D_IN_PAD), F32) + _nbytes((CHUNK, 2 * D_GRP), F32) + 2 * _nbytes((HEADS, DK, DK), F32)) \
        + 2 * _nbytes((HEADS, DK, DK), F32)
    return _pcall(
        body, name="mixer_fwd", grid=(n_chunks,),
        in_specs=[proj_spec, halo_spec] + small,
        out_specs=[y_spec] + state_specs,
        out_shape=[jax.ShapeDtypeStruct((seq, 2 * D_GRP), BF16), st, st, vec, vec],
        scratch_shapes=[pltpu.VMEM((HEADS, DK, DK), F32), pltpu.VMEM((HEADS, DK, DK), F32),
                        pltpu.VMEM((HEADS, 1, DK), F32), pltpu.VMEM((HEADS, 1, DK), F32)],
        compiler_params=_params(("arbitrary",), vmem),
    )(proj, proj, lb_logits, hg_nw, conv_w, conv_b, ml_nw)


def _mixer_bwd(proj, dy, hst, cst, nst, mst, lb_logits, hg_nw, conv_w, conv_b, ml_nw):
    seq = proj.shape[0]
    n_chunks = seq // CHUNK
    proj_spec, halo_spec, small, state_specs, y_spec, _ = _mixer_specs(n_chunks, True)

    def body(proj_ref, halo_ref, dy_ref, hst_ref, cst_ref, nst_ref, mst_ref,
             lg_ref, hnw_ref, cw_ref, cb_ref, mnw_ref,
             dproj_ref, dbin_ref, dlg_ref, dhnw_ref, dcw_ref, dcb_ref, dmnw_ref,
             dhs, dcs, dns, dms, dhalo):
        c = pl.program_id(0)

        @pl.when(c == 0)
        def _():
            for r in (dhs, dcs, dns, dms, dhalo, dbin_ref, dlg_ref, dhnw_ref, dcw_ref, dcb_ref, dmnw_ref):
                r[...] = jnp.zeros_like(r)

        def put(cols, val):
            dproj_ref[:, cols] = val.astype(BF16)
            dbin_ref[:, cols] += jnp.sum(val, axis=0, keepdims=True)

        first = c == n_chunks - 1
        halo = jnp.where(first, 0.0, halo_ref[...])
        x_qk = proj_ref[:, pl.ds(4 * D_GRP, 2 * D_GRP)]
        conv_args = (halo, x_qk, cw_ref[0:1, :], cw_ref[1:2, :], cw_ref[2:3, :], cw_ref[3:4, :], cb_ref[...])
        qk, conv_vjp = jax.vjp(_qk_conv, *conv_args)
        hg_in, ml_in = _mixer_inputs(proj_ref, lg_ref, hnw_ref, mnw_ref, qk)
        _, hg_vjp = jax.vjp(_hg_chunk, hst_ref[0], *hg_in)
        _, ml_vjp = jax.vjp(_ml_chunk, cst_ref[0], nst_ref[0], _last(mst_ref[0], 0), *ml_in)
        dst, dhq, dhf, dhi, dhg, dl0, dl1, dnw = hg_vjp((dhs[...], dy_ref[:, pl.ds(0, D_GRP)]))
        dc, dn, dm, dq, dk, dv, dgates, dog, dmn = ml_vjp(
            (dcs[...], dns[...], _last(dms[...], 0), dy_ref[:, pl.ds(D_GRP, D_GRP)]))
        dhs[...] = dst
        dcs[...] = dc
        dns[...] = dn
        dms[...] = jnp.broadcast_to(dm, dms.shape)
        for i, val in ((0, dhq), (1, dhf), (2, dhi), (3, dhg), (6, dv), (7, dog)):
            put(_grp(i), val)
        put(pl.ds(8 * D_GRP, LANES), dgates)
        dlg_ref[0:1, :] += dl0
        dlg_ref[1:2, :] += dl1
        dhnw_ref[...] += dnw
        dmnw_ref[...] += dmn
        dh, dx, dw0, dw1, dw2, dw3, db = conv_vjp(jnp.concatenate([dq, dk], axis=1))
        tail = jnp.concatenate([jnp.zeros((CHUNK - SUBLANES, 2 * D_GRP), F32), dhalo[...]], axis=0)
        put(pl.ds(4 * D_GRP, 2 * D_GRP), dx + tail)
        dhalo[...] = dh
        for d, dw in enumerate((dw0, dw1, dw2, dw3)):
            dcw_ref[d:d + 1, :] += dw
        dcb_ref[...] += db

    row = pl.BlockSpec((1, D_GRP), lambda c: (0, 0))
    small_out = [pl.BlockSpec((1, D_IN_PAD), lambda c: (0, 0)), pl.BlockSpec((2, D_GRP), lambda c: (0, 0)), row,
                 pl.BlockSpec((ML_CONV, 2 * D_GRP), lambda c: (0, 0)), pl.BlockSpec((1, 2 * D_GRP), lambda c: (0, 0)), row]
    dy_spec = pl.BlockSpec((CHUNK, 2 * D_GRP), y_spec.index_map)
    vmem = 2 * (2 * _nbytes((CHUNK, D_IN_PAD), F32) + _nbytes((CHUNK, 2 * D_GRP), F32)
                + 2 * _nbytes((HEADS, DK, DK), F32)) + 2 * _nbytes((HEADS, DK, DK), F32) + 4 * 1024 * 1024
    return _pcall(
        body, name="mixer_bwd", grid=(n_chunks,),
        in_specs=[proj_spec, halo_spec, dy_spec] + state_specs + small,
        out_specs=[proj_spec] + small_out,
        out_shape=[jax.ShapeDtypeStruct((seq, D_IN_PAD), BF16), jax.ShapeDtypeStruct((1, D_IN_PAD), F32),
                   jax.ShapeDtypeStruct((2, D_GRP), F32), jax.ShapeDtypeStruct((1, D_GRP), F32),
                   jax.ShapeDtypeStruct((ML_CONV, 2 * D_GRP), F32), jax.ShapeDtypeStruct((1, 2 * D_GRP), F32),
                   jax.ShapeDtypeStruct((1, D_GRP), F32)],
        scratch_shapes=[pltpu.VMEM((HEADS, DK, DK), F32), pltpu.VMEM((HEADS, DK, DK), F32),
                        pltpu.VMEM((HEADS, 1, DK), F32), pltpu.VMEM((HEADS, 1, DK), F32),
                        pltpu.VMEM((SUBLANES, 2 * D_GRP), F32)],
        compiler_params=_params(("arbitrary",), vmem),
    )(proj, proj, dy, hst, cst, nst, mst, lb_logits, hg_nw, conv_w, conv_b, ml_nw)


def _tile(n, prefs, unit=None):
    unit = unit or n
    for p in prefs:
        if unit % p == 0 and n % p == 0:
            return p
    return unit


def _logical(arr):
    return arr.shape if arr.ndim == 2 else (arr.shape[1], arr.shape[0] * arr.shape[2])


def _group(arr):
    return arr.shape[-1]


def _split_spec(ndim, group, tr, tc, where):
    if ndim == 2:
        return pl.BlockSpec((tr, tc), where)
    per = group // tc
    assert per * tc == group, (group, tc)

    def index(*ids):
        bi, bj = where(*ids)
        return (bj // per, bi, bj % per)
    return pl.BlockSpec((None, tr, tc), index)


def _mm(name, mode, a, b, *, bias=None, res=None, res_scale=1.0, ln=None, out_dtype=F32, out_groups=None,
        copy_dtype=None, a_copy_dtype=None, tm=None, tn=None, tk=None):
    la, lb = _logical(a), _logical(b)
    if mode == "nn":
        (m, k), n = la, lb[1]
        n_unit = _group(b) if b.ndim == 3 else n
        kc = _group(a) if a.ndim == 3 else k
    elif mode == "nt":
        (m, k), n = la, lb[0]
        n_unit = n
        kc = min(_group(a) if a.ndim == 3 else k, _group(b) if b.ndim == 3 else k)
    else:
        (k, m), n = la, lb[1]
        n_unit, kc = (_group(b) if b.ndim == 3 else n), k
        assert a.ndim == 2
    if out_groups:
        n_unit = min(n_unit, n // out_groups)
    kind = ln[0] if ln else None
    tm = tm or (256 if ln else _tile(m, (512, 256, 128)))
    tn = n if ln else (tn or _tile(n, (512, 384, 256, 128), n_unit))
    if mode != "tn":
        tk = k
    elif tk is None:
        tk = _tile(k, (4096, 2048, 512, 256, 128) if (m // tm) * (n // tn) > 1 else (2048, 512, 256, 128))
    gi, gj, gk = m // tm, n // tn, k // tk
    assert gi * tm == m and gj * tn == n and gk * tk == k and n_unit % tn == 0, (name, m, n, k, tm, tn, tk)
    ca, cb = {"nn": (1, 0), "nt": (1, 1), "tn": (0, 0)}[mode]
    i_outer = gk > 1 or (gi - 1) * _nbytes(b.shape, b.dtype) <= (gj - 1) * _nbytes(a.shape, a.dtype)

    def ij(where):
        return (lambda p, q, kk: where(p, q, kk)) if i_outer else (lambda p, q, kk: where(q, p, kk))
    if mode == "tn":
        a_spec = pl.BlockSpec((tk, tm), ij(lambda i, j, kk: (kk, i)))
    elif a.ndim == 3:
        a_spec = pl.BlockSpec((a.shape[0], tm, _group(a)), ij(lambda i, j, kk: (0, i, 0)))
    else:
        a_spec = pl.BlockSpec((tm, k), ij(lambda i, j, kk: (i, 0)))
    if mode != "nt":
        b_spec = _split_spec(b.ndim, _group(b), tk, tn, ij(lambda i, j, kk: (kk, j)))
    elif b.ndim == 3:
        b_spec = pl.BlockSpec((b.shape[0], tn, _group(b)), ij(lambda i, j, kk: (0, j, 0)))
    else:
        b_spec = pl.BlockSpec((tn, k), ij(lambda i, j, kk: (j, 0)))
    row_spec = pl.BlockSpec((1, tn), ij(lambda i, j, kk: (0, j)))
    blk_spec = pl.BlockSpec((tm, tn), ij(lambda i, j, kk: (i, j)))
    ins, in_specs = [a, b], [a_spec, b_spec]
    if bias is not None:
        ins.append(bias), in_specs.append(row_spec)
    if res is not None:
        ins.append(res), in_specs.append(blk_spec)
    if kind == "fwd":
        ins += [ln[1], ln[2]]
        in_specs += [row_spec, row_spec]
    elif kind == "bwd":
        ins += [ln[1], ln[2], ln[3]]
        in_specs += [blk_spec, row_spec, row_spec]
    if out_groups:
        blk_out = jax.ShapeDtypeStruct((out_groups, m, n // out_groups), out_dtype)
        out_spec = _split_spec(3, n // out_groups, tm, tn, ij(lambda i, j, kk: (i, j)))
    else:
        blk_out, out_spec = jax.ShapeDtypeStruct((m, n), out_dtype), blk_spec
    row_out = jax.ShapeDtypeStruct((1, n), F32)
    if kind is None:
        out_shape, out_specs = [blk_out], [out_spec]
    elif kind == "fwd":
        out_shape, out_specs = [blk_out, blk_out], [blk_spec, blk_spec]
    else:
        out_shape, out_specs = [blk_out, row_out, row_out], [blk_spec, row_spec, row_spec]
    if copy_dtype is not None:
        out_shape.append(jax.ShapeDtypeStruct((m, n), copy_dtype))
        out_specs.append(blk_spec)
    if a_copy_dtype is not None:
        assert mode != "tn" and a.ndim == 2 and copy_dtype is None
        out_shape.append(jax.ShapeDtypeStruct((m, k), a_copy_dtype))
        out_specs.append(a_spec)
    n_in = len(ins)

    def body(*refs):
        in_refs, out_refs, acc_ref = refs[:n_in], refs[n_in:n_in + len(out_shape)], refs[-1]
        i, kk = pl.program_id(0 if i_outer else 1), pl.program_id(2)
        a_ref, b_ref = in_refs[:2]
        extra = list(in_refs[2:])
        if a_copy_dtype is not None:
            out_refs[-1][...] = a_ref[...].astype(a_copy_dtype)

        def epilogue(acc):
            rest = list(extra)
            if bias is not None:
                acc = acc + rest.pop(0)[...]
            if res is not None:
                acc = acc + res_scale * rest.pop(0)[...]
            if kind is None:
                out_refs[0][...] = acc.astype(out_dtype)
                return
            if kind == "fwd":
                out_refs[0][...] = acc
                y = _layer_norm(acc, rest[0][...], rest[1][...])
                out_refs[1][...] = y
                if copy_dtype is not None:
                    out_refs[-1][...] = y.astype(copy_dtype)
                return
            _, vjp = jax.vjp(_layer_norm, rest[0][...], rest[1][...], rest[2][...])
            dz, dg, db = vjp(acc)
            out_refs[0][...] = dz
            out_refs[1][...] += dg
            out_refs[2][...] += db
            if copy_dtype is not None:
                out_refs[-1][...] = dz.astype(copy_dtype)

        if kind == "bwd":
            @pl.when((i == 0) & (kk == 0))
            def _():
                out_refs[1][...] = jnp.zeros_like(out_refs[1])
                out_refs[2][...] = jnp.zeros_like(out_refs[2])

        def chunk(ref, c0, last):
            if ref.ndim == 3:
                g = ref.shape[2]
                return ref[c0 // g, :, pl.ds(c0 % g, kc)]
            return ref[:, pl.ds(c0, kc)] if last else ref[pl.ds(c0, kc), :]

        if mode == "tn" or kc == k:
            prod = _dg(a_ref[...], b_ref[...], ca, cb)
        else:
            prod = None
            for c0 in range(0, k, kc):
                part = _dg(chunk(a_ref, c0, True), chunk(b_ref, c0, mode == "nt"), ca, cb)
                prod = part if prod is None else prod + part
        if gk == 1:
            epilogue(prod)
            return

        @pl.when(kk == 0)
        def _():
            acc_ref[...] = prod

        @pl.when(kk > 0)
        def _():
            acc_ref[...] += prod

        @pl.when(kk == gk - 1)
        def _():
            epilogue(acc_ref[...])

    vmem = (2 * (_nbytes((tm, tk), a.dtype) + _nbytes((tk, tn), b.dtype))
            + (2 * len(ins) + 2 * len(out_shape) + 1) * _nbytes((tm, tn), F32))
    outs = _pcall(
        body, name=name, grid=(gi, gj, gk) if i_outer else (gj, gi, gk), in_specs=in_specs, out_specs=out_specs,
        out_shape=out_shape, scratch_shapes=[pltpu.VMEM((tm, tn) if gk > 1 else (SUBLANES, LANES), F32)],
        compiler_params=_params(("arbitrary", "arbitrary", "arbitrary"), vmem),
    )(*ins)
    return outs[0] if len(out_shape) == 1 else outs


def _attn_head(q, k, v):
    sc = mm_nt(q, k) * (CA_DH ** -0.5)
    e = jnp.exp(sc - jnp.max(sc, axis=-1, keepdims=True))
    return mm_nn(e / jnp.sum(e, axis=-1, keepdims=True), v)


def _attn_fwd(q, kv):
    seq, n_mem = q.shape[0], kv.shape[0]
    tq = _tile(seq, (512, 256, 128))

    def body(q_ref, kv_ref, o_ref):
        for h in range(HEADS):
            hd = pl.ds(h * CA_DH, CA_DH)
            o = _attn_head(q_ref[:, hd], kv_ref[:, hd], kv_ref[:, pl.ds(D_MODEL + h * CA_DH, CA_DH)])
            o_ref[:, hd] = o.astype(BF16)

    return _pcall(
        body, name="attn_fwd", grid=(seq // tq,),
        in_specs=[pl.BlockSpec((tq, D_MODEL), lambda i: (i, 0)), pl.BlockSpec((n_mem, 2 * D_MODEL), lambda i: (0, 0))],
        out_specs=pl.BlockSpec((tq, D_MODEL), lambda i: (i, 0)), out_shape=jax.ShapeDtypeStruct((seq, D_MODEL), BF16),
        compiler_params=_params(("arbitrary",), 4 * _nbytes((tq, D_MODEL), F32) + 2 * _nbytes((n_mem, 2 * D_MODEL), F32)),
    )(q, kv)


def _attn_bwd(q, kv, do):
    seq, n_mem = q.shape[0], kv.shape[0]
    tq = _tile(seq, (512, 256, 128))

    def body(q_ref, kv_ref, do_ref, dq_ref, dkv_ref):
        @pl.when(pl.program_id(0) == 0)
        def _():
            dkv_ref[...] = jnp.zeros_like(dkv_ref)

        for h in range(HEADS):
            hd = pl.ds(h * CA_DH, CA_DH)
            vd = pl.ds(D_MODEL + h * CA_DH, CA_DH)
            _, vjp = jax.vjp(_attn_head, q_ref[:, hd], kv_ref[:, hd], kv_ref[:, vd])
            dq, dk, dv = vjp(do_ref[:, hd].astype(F32))
            dq_ref[:, hd] = dq.astype(BF16)
            dkv_ref[:, hd] += dk
            dkv_ref[:, vd] += dv

    return _pcall(
        body, name="attn_bwd", grid=(seq // tq,),
        in_specs=[pl.BlockSpec((tq, D_MODEL), lambda i: (i, 0)), pl.BlockSpec((n_mem, 2 * D_MODEL), lambda i: (0, 0)),
                  pl.BlockSpec((tq, D_MODEL), lambda i: (i, 0))],
        out_specs=[pl.BlockSpec((tq, D_MODEL), lambda i: (i, 0)), pl.BlockSpec((n_mem, 2 * D_MODEL), lambda i: (0, 0))],
        out_shape=[jax.ShapeDtypeStruct((seq, D_MODEL), BF16), jax.ShapeDtypeStruct((n_mem, 2 * D_MODEL), F32)],
        compiler_params=_params(("arbitrary",), 6 * _nbytes((tq, D_MODEL), F32) + 4 * _nbytes((n_mem, 2 * D_MODEL), F32)),
    )(q, kv, do)


def _ffn_mid(hg, xg, hv, xv, wg0, wg1, wg2, bg, wv0, wv1, wv2, bv):
    return jax.nn.gelu(causal_conv(hg, xg, (wg0, wg1, wg2), bg)) * causal_conv(hv, xv, (wv0, wv1, wv2), bv)


FFN_TB = 256
FFN_W = D_FF // 2
FFN_J = D_FF // FFN_W
MXU_COLS = 256
FFN_PIECES = tuple((off, min(MXU_COLS, FFN_W - off)) for off in range(0, FFN_W, MXU_COLS))


def _ffn_common_specs(seq, row):
    tb = min(FFN_TB, seq)
    full = pl.BlockSpec((tb, D_MODEL), lambda t, j: (row(t), 0))
    vec = pl.BlockSpec((1, D_MODEL), lambda t, j: (0, 0))
    halves = []
    for off in (0, FFN_J):
        halves.append(dict(
            w_up=pl.BlockSpec((None, D_MODEL, FFN_W), lambda t, j, off=off: (j + off, 0, 0)),
            taps=pl.BlockSpec((FFN_CONV, FFN_W), lambda t, j, off=off: (0, j + off)),
            bias=pl.BlockSpec((1, FFN_W), lambda t, j, off=off: (0, j + off))))
    w_down = pl.BlockSpec((FFN_W, D_MODEL), lambda t, j: (j, 0))
    u_blk = pl.BlockSpec((2, tb, FFN_W), lambda t, j: (0, row(t), j))
    return tb, full, vec, halves, w_down, u_blk


def _ffn_vmem(tb):
    return (_nbytes((2, tb, FFN_W), F32) + _nbytes((2, tb, FFN_W), BF16) + 3 * _nbytes((D_MODEL, FFN_W), BF16)
            + 10 * _nbytes((tb, D_MODEL), F32))


def _conv_params(taps_ref, bias_ref, cols):
    return taps_ref[0:1, cols], taps_ref[1:2, cols], taps_ref[2:3, cols], bias_ref[:, cols]


def _ffn_fwd(x2b, x2, w_up, conv_w, conv_b, w_down, ln_g, ln_b, target):
    seq = x2.shape[0]
    tb, full, vec, halves, wd_spec, u_blk = _ffn_common_specs(seq, lambda t: t)
    nt = seq // tb

    def body(xb_ref, wg_ref, wv_ref, tg_ref, tv_ref, bg_ref, bv_ref, wd_ref, x_ref, g_ref, b_ref, tgt_ref,
             u_ref, h_ref, dz_ref, dg_ref, db_ref, loss_ref, dzb_ref, acc, carry):
        t, j = pl.program_id(0), pl.program_id(1)
        xb = xb_ref[...]
        pieces = [pl.ds(off, width) for off, width in FFN_PIECES]
        ug = [_dg(xb, wg_ref[:, cols], 1, 0) for cols in pieces]
        uv = [_dg(xb, wv_ref[:, cols], 1, 0) for cols in pieces]
        hs = []
        for cols, g, v in zip(pieces, ug, uv):
            u_ref[0, :, cols] = g
            u_ref[1, :, cols] = v
            halo_g = jnp.where(t == 0, 0.0, carry[j, 0, :, cols])
            halo_v = jnp.where(t == 0, 0.0, carry[j, 1, :, cols])
            h = _ffn_mid(halo_g, g, halo_v, v, *_conv_params(tg_ref, bg_ref, cols),
                         *_conv_params(tv_ref, bv_ref, cols)).astype(BF16)
            carry[j, 0, :, cols] = g[tb - SUBLANES:, :]
            carry[j, 1, :, cols] = v[tb - SUBLANES:, :]
            h_ref[:, cols] = h
            hs.append(h)
        part = None
        for cols, h in zip(pieces, hs):
            p = _dg(h, wd_ref[cols, :], 1, 0)
            part = p if part is None else part + p

        @pl.when(j == 0)
        def _():
            acc[...] = part

        @pl.when(j > 0)
        def _():
            acc[...] += part

        @pl.when(j == FFN_J - 1)
        def _():
            y, vjp = jax.vjp(_layer_norm, acc[...] + ALPHA * x_ref[...], g_ref[...], b_ref[...])
            err = y - tgt_ref[...]
            part_loss = 0.5 * jnp.sum(jnp.sum(err * err, axis=1, keepdims=True), axis=0, keepdims=True) / D_MODEL
            dz, dg, db = vjp(err / D_MODEL)

            @pl.when(t == 0)
            def _():
                for r in (dg_ref, db_ref, loss_ref):
                    r[...] = jnp.zeros_like(r)

            dz_ref[...] = dz
            dzb_ref[...] = dz.astype(BF16)
            dg_ref[...] += dg
            db_ref[...] += db
            loss_ref[...] += jnp.broadcast_to(part_loss, (1, LANES))

    h0, h1 = halves
    row = jax.ShapeDtypeStruct((1, D_MODEL), F32)
    return _pcall(
        body, name="ffn_fwd", grid=(nt, FFN_J),
        in_specs=[full, h0["w_up"], h1["w_up"], h0["taps"], h1["taps"], h0["bias"], h1["bias"], wd_spec, full, vec, vec,
                  full],
        out_specs=[u_blk, pl.BlockSpec((tb, FFN_W), lambda t, j: (t, j)), full, vec, vec,
                   pl.BlockSpec((1, LANES), lambda t, j: (0, 0)), full],
        out_shape=[jax.ShapeDtypeStruct((2, seq, D_FF), F32), jax.ShapeDtypeStruct((seq, D_FF), BF16),
                   jax.ShapeDtypeStruct((seq, D_MODEL), F32), row, row, jax.ShapeDtypeStruct((1, LANES), F32),
                   jax.ShapeDtypeStruct((seq, D_MODEL), BF16)],
        scratch_shapes=[pltpu.VMEM((tb, D_MODEL), F32), pltpu.VMEM((FFN_J, 2, SUBLANES, FFN_W), F32)],
        compiler_params=_params(("arbitrary", "arbitrary"), _ffn_vmem(tb)),
    )(x2b, w_up, w_up, conv_w, conv_w, conv_b, conv_b, w_down, x2, ln_g, ln_b, target)


def _ffn_bwd(u, conv_w, conv_b, dz3b, dz3, w_down, w_up, z2, ln_g, ln_b):
    seq = dz3.shape[0]
    tb = min(FFN_TB, seq)
    nt = seq // tb
    row8 = tb // SUBLANES
    tb, full, vec, halves, wd_spec, u_blk = _ffn_common_specs(seq, lambda t: nt - 1 - t)
    halo = pl.BlockSpec((2, SUBLANES, FFN_W), lambda t, j: (0, jnp.maximum((nt - 1 - t) * row8 - 1, 0), j))

    def body(u_ref, halo_ref, tg_ref, tv_ref, bg_ref, bv_ref, dzb_ref, wd_ref, wg_ref, wv_ref, dz3_ref, z_ref, g_ref,
             b_ref, du_ref, dw_ref, dbias_ref, dz_ref, dg_ref, db_ref, dz2b_ref, acc, carry):
        t, j = pl.program_id(0), pl.program_id(1)

        @pl.when((t == 0) & (j == 0))
        def _():
            for r in (dw_ref, dbias_ref, dg_ref, db_ref):
                r[...] = jnp.zeros_like(r)

        pieces = [pl.ds(off, width) for off, width in FFN_PIECES]
        dzb = dzb_ref[...]
        dhs = [_dg(dzb, wd_ref[cols, :], 1, 1) for cols in pieces]
        first = t == nt - 1
        dus = []
        for cols, dh in zip(pieces, dhs):
            args = (jnp.where(first, 0.0, halo_ref[0, :, cols]), u_ref[0, :, cols],
                    jnp.where(first, 0.0, halo_ref[1, :, cols]), u_ref[1, :, cols],
                    *_conv_params(tg_ref, bg_ref, cols), *_conv_params(tv_ref, bv_ref, cols))
            _, vjp = jax.vjp(_ffn_mid, *args)
            dhg, dxg, dhv, dxv, g0, g1, g2, gb, v0, v1, v2, vb = vjp(dh)
            zeros = jnp.zeros((tb - SUBLANES, dh.shape[1]), F32)
            dug = (dxg + jnp.concatenate([zeros, jnp.where(t == 0, 0.0, carry[j, 0, :, cols])], axis=0)).astype(BF16)
            duv = (dxv + jnp.concatenate([zeros, jnp.where(t == 0, 0.0, carry[j, 1, :, cols])], axis=0)).astype(BF16)
            carry[j, 0, :, cols] = dhg
            carry[j, 1, :, cols] = dhv
            du_ref[0, :, cols] = dug
            du_ref[1, :, cols] = duv
            for half, parts in enumerate(((g0, g1, g2), (v0, v1, v2))):
                for d, p in enumerate(parts):
                    dw_ref[j, half, d:d + 1, cols] += p
            dbias_ref[j, 0, :, cols] += gb
            dbias_ref[j, 1, :, cols] += vb
            dus.append((dug, duv))
        part = None
        for cols, (dug, duv) in zip(pieces, dus):
            p = _dg(dug, wg_ref[:, cols], 1, 1) + _dg(duv, wv_ref[:, cols], 1, 1)
            part = p if part is None else part + p

        @pl.when(j == 0)
        def _():
            acc[...] = part

        @pl.when(j > 0)
        def _():
            acc[...] += part

        @pl.when(j == FFN_J - 1)
        def _():
            _, ln_vjp = jax.vjp(_layer_norm, z_ref[...], g_ref[...], b_ref[...])
            dz, dg, db = ln_vjp(acc[...] + ALPHA * dz3_ref[...])
            dz_ref[...] = dz
            dz2b_ref[...] = dz.astype(BF16)
            dg_ref[...] += dg
            db_ref[...] += db

    h0, h1 = halves
    row = jax.ShapeDtypeStruct((1, D_MODEL), F32)
    whole = lambda *shape: pl.BlockSpec(shape, lambda t, j: (0,) * len(shape))
    return _pcall(
        body, name="ffn_bwd", grid=(nt, FFN_J),
        in_specs=[u_blk, halo, h0["taps"], h1["taps"], h0["bias"], h1["bias"], full, wd_spec, h0["w_up"], h1["w_up"],
                  full, full, vec, vec],
        out_specs=[u_blk, whole(FFN_J, 2, FFN_CONV, FFN_W), whole(FFN_J, 2, 1, FFN_W), full, vec, vec, full],
        out_shape=[jax.ShapeDtypeStruct((2, seq, D_FF), BF16), jax.ShapeDtypeStruct((FFN_J, 2, FFN_CONV, FFN_W), F32),
                   jax.ShapeDtypeStruct((FFN_J, 2, 1, FFN_W), F32), jax.ShapeDtypeStruct((seq, D_MODEL), F32), row, row,
                   jax.ShapeDtypeStruct((seq, D_MODEL), BF16)],
        scratch_shapes=[pltpu.VMEM((tb, D_MODEL), F32), pltpu.VMEM((FFN_J, 2, SUBLANES, FFN_W), F32)],
        compiler_params=_params(("arbitrary", "arbitrary"), _ffn_vmem(tb)),
    )(u, u, conv_w, conv_w, conv_b, conv_b, dz3b, w_down, w_up, w_up, dz3, z2, ln_g, ln_b)


def _adamw_math(w, g, m, v):
    m_new = ADAM_B1 * m + (1.0 - ADAM_B1) * g
    v_new = ADAM_B2 * v + (1.0 - ADAM_B2) * jnp.square(g)
    m_hat = m_new / (1.0 - ADAM_B1 ** ADAM_STEP)
    v_hat = v_new / (1.0 - ADAM_B2 ** ADAM_STEP)
    return -ADAM_LR * (m_hat / (jnp.sqrt(v_hat) + ADAM_EPS) + ADAM_WD * w), m_new, v_new


def _adamw_many(name, ws, gs, ms, vs):
    n = len(ws)

    def body(*refs):
        w_refs, g_refs, m_refs, v_refs = (refs[i * n:(i + 1) * n] for i in range(4))
        d_refs, nm_refs, nv_refs = (refs[(4 + i) * n:(5 + i) * n] for i in range(3))
        for i in range(n):
            d_refs[i][...], nm_refs[i][...], nv_refs[i][...] = _adamw_math(
                w_refs[i][...], g_refs[i][...], m_refs[i][...], v_refs[i][...])

    vm = pl.BlockSpec(memory_space=pltpu.VMEM)
    outs = _pcall(
        body, pin=False, name=name, in_specs=[vm] * (4 * n), out_specs=[vm] * (3 * n),
        out_shape=[jax.ShapeDtypeStruct(w.shape, F32) for w in ws] * 3,
    )(*ws, *gs, *ms, *vs)
    return outs[:n], outs[n:2 * n], outs[2 * n:]


def _adamw_halves(name, core, w, mine, theirs, m, v):
    rows, cols = w.shape
    half_rows = mine.shape[0]
    tr = _tile(half_rows, (256, 176, 128))
    nbh = half_rows // tr
    assert 2 * half_rows == rows

    def body(c_ref, w_ref, a_ref, b_ref, m_ref, v_ref, g_ref, d_ref, nm_ref, nv_ref):
        g = jnp.where(pl.program_id(0) // nbh == c_ref[0], a_ref[...], b_ref[...])
        g_ref[...] = g
        d_ref[...], nm_ref[...], nv_ref[...] = _adamw_math(w_ref[...], g, m_ref[...], v_ref[...])

    spec = pl.BlockSpec((tr, cols), lambda i, c_ref: (i, 0))
    half = pl.BlockSpec((tr, cols), lambda i, c_ref: (i % nbh, 0))
    sh = jax.ShapeDtypeStruct((rows, cols), F32)
    grid_spec = pltpu.PrefetchScalarGridSpec(
        num_scalar_prefetch=1, grid=(rows // tr,), in_specs=[spec, half, half, spec, spec], out_specs=[spec] * 4)
    return _pcall(
        body, name=name, grid_spec=grid_spec, out_shape=[sh] * 4,
        compiler_params=_params(("arbitrary",), 18 * _nbytes((tr, -(-cols // LANES) * LANES), F32)),
    )(core, w, mine, theirs, m, v)


MESH = pl.DeviceIdType.MESH
ANY = pl.BlockSpec(memory_space=pl.ANY)
N_CHIPS = 4
BF16_ROWS = 16


def _me():
    return lax.axis_index("x"), lax.axis_index("y"), lax.axis_index("c")


def _other_chips(x, y):
    return [(1 - x, y), (x, 1 - y), (1 - x, 1 - y)]


def _remote(src, dst, ssem, rsem, dev):
    return pltpu.make_async_remote_copy(src_ref=src, dst_ref=dst, send_sem=ssem, recv_sem=rsem,
                                        device_id=dev, device_id_type=MESH)


def _half_rows(ref_rows, cc):
    half = ref_rows // 2
    return pl.ds(pl.multiple_of(cc * half, BF16_ROWS), half)


def _gather_weights(shards):
    n = len(shards)
    n_ici = n * (N_CHIPS - 1)

    def body(*refs):
        ins, outs, (ssem, rsem, lsem, lrsem) = refs[:n], refs[n:2 * n], refs[2 * n:]
        x, y, c = _me()
        k_me = 2 * x + y
        sib = (x, y, 1 - c)
        chips = _other_chips(x, y)
        started = []
        for i, (w_ref, o_ref) in enumerate(zip(ins, outs)):
            cp = _remote(w_ref, o_ref.at[k_me], lsem.at[i], lrsem.at[i], sib)
            cp.start()
            started.append(cp)
        for r, (px, py) in enumerate(chips):
            for i, (w_ref, o_ref) in enumerate(zip(ins, outs)):
                rows = _half_rows(w_ref.shape[0], c)
                s = r * n + i
                cp = _remote(w_ref.at[rows], o_ref.at[k_me, rows], ssem.at[s], rsem.at[s], (px, py, c))
                cp.start()
                started.append(cp)
        for r, (px, py) in enumerate(chips):
            for i, o_ref in enumerate(outs):
                blk = o_ref.at[2 * px + py, _half_rows(o_ref.shape[1], c)]
                s = r * n + i
                _remote(blk, blk, ssem.at[s], rsem.at[s], (px, py, c)).wait_recv()
                cp = _remote(blk, blk, ssem.at[n_ici + s], rsem.at[n_ici + s], sib)
                cp.start()
                started.append(cp)
        for r, (px, py) in enumerate(chips):
            for i, o_ref in enumerate(outs):
                blk = o_ref.at[2 * px + py, _half_rows(o_ref.shape[1], 1 - c)]
                s = n_ici + r * n + i
                _remote(blk, blk, ssem.at[s], rsem.at[s], sib).wait_recv()
        for cp in started[n:]:
            cp.wait_send()
        for cp in started[:n]:
            cp.wait()

    return _pcall(
        body, name="gather_weights", in_specs=[ANY] * n, out_specs=[ANY] * n,
        out_shape=[jax.ShapeDtypeStruct((N_CHIPS,) + s.shape, s.dtype) for s in shards],
        scratch_shapes=[pltpu.SemaphoreType.DMA((2 * n_ici,)), pltpu.SemaphoreType.DMA((2 * n_ici,)),
                        pltpu.SemaphoreType.DMA((n,)), pltpu.SemaphoreType.DMA((n,))],
    )(*shards)


def _swap_halves(name, grads):
    n = len(grads)

    def body(*refs):
        ins, outs, (ssem, rsem) = refs[:n], refs[n:2 * n], refs[2 * n:]
        x, y, c = _me()
        copies = []
        for i, (g_ref, o_ref) in enumerate(zip(ins, outs)):
            for k in range(N_CHIPS):
                s = i * N_CHIPS + k
                cp = _remote(g_ref.at[k, _half_rows(g_ref.shape[1], 1 - c)], o_ref.at[k], ssem.at[s], rsem.at[s],
                             (x, y, 1 - c))
                cp.start()
                copies.append(cp)
        for cp in copies:
            cp.wait()

    return _pcall(
        body, name=name, in_specs=[ANY] * n, out_specs=[ANY] * n,
        out_shape=[jax.ShapeDtypeStruct((N_CHIPS, g.shape[1] // 2, g.shape[2]), g.dtype) for g in grads],
        scratch_shapes=[pltpu.SemaphoreType.DMA((n * N_CHIPS,)), pltpu.SemaphoreType.DMA((n * N_CHIPS,))],
    )(*grads)


SEM = pl.BlockSpec(memory_space=pltpu.SEMAPHORE)
IN_HBM = pl.BlockSpec(memory_space=pltpu.HBM)
SPLIT_PARAMS = dict(compiler_params=pltpu.CompilerParams(has_side_effects=pltpu.SideEffectType.DATAFLOW_SIDE_EFFECTING))


def _split_start(name, sources, landings, n_copies, plan):
    ns, nl = len(sources), len(landings)

    def body(*refs):
        ins, lands, (ssem, rsem), token = refs[:ns], refs[ns:ns + nl], refs[ns + nl:ns + nl + 2], refs[-1]
        for s, (src, dst, _, dev) in enumerate(plan(ins, lands)):
            _remote(src, dst, ssem.at[s], rsem.at[s], dev).start()
        token[...] = jnp.zeros_like(token)

    arrays = list(sources) + list(landings)
    outs = _call(
        body, name=name, in_specs=[IN_HBM] * (ns + nl),
        out_specs=[SEM, SEM] + [IN_HBM] * (ns + nl) + [pl.BlockSpec(memory_space=pltpu.VMEM)],
        out_shape=[pltpu.SemaphoreType.DMA((n_copies,)), pltpu.SemaphoreType.DMA((n_copies,))]
        + [pltpu.HBM(a.shape, a.dtype) for a in arrays] + [jax.ShapeDtypeStruct((SUBLANES, LANES), F32)],
        input_output_aliases={i: 2 + i for i in range(ns + nl)}, **SPLIT_PARAMS,
    )(*[pltpu.with_memory_space_constraint(a, pltpu.HBM) for a in arrays])
    return (outs[:-1], ns), outs[-1]


def _split_wait(name, handle, after, plan):
    (ssem, rsem, *thru), ns = handle
    nl = len(thru) - ns

    def body(*refs):
        ins, lands, (ssem_ref, rsem_ref) = refs[:ns], refs[ns:ns + nl], refs[ns + nl:ns + nl + 2]
        for s, (src, _, dst, dev) in enumerate(plan(ins, lands)):
            cp = _remote(src, dst, ssem_ref.at[s], rsem_ref.at[s], dev)
            cp.wait_send()
            cp.wait_recv()

    outs = _call(
        body, name=name, in_specs=[IN_HBM] * (ns + nl) + [SEM, SEM, ANY], out_specs=[IN_HBM] * (ns + nl),
        out_shape=[pltpu.HBM(t.shape, t.dtype) for t in thru],
        input_output_aliases={i: i for i in range(ns + nl)}, **SPLIT_PARAMS,
    )(*thru, ssem, rsem, after)
    return outs[:ns], outs[ns:]


def _swap_plan(ins, lands):
    x, y, c = _me()
    return [(g_ref.at[k, _half_rows(g_ref.shape[1], 1 - c)], l_ref.at[k], l_ref.at[k], (x, y, 1 - c))
            for g_ref, l_ref in zip(ins, lands) for k in range(N_CHIPS)]


def _swap_start(name, grads):
    lands = [lax.empty((N_CHIPS, g.shape[1] // 2, g.shape[2]), g.dtype) for g in grads]
    return _split_start(name, grads, lands, len(grads) * N_CHIPS, _swap_plan)


def _swap_wait(name, handle, after):
    return _split_wait(name, handle, after, _swap_plan)


def _gather_plan(ins, lands):
    x, y, c = _me()
    k_me = 2 * x + y
    plan = [(w_ref, l_ref.at[k_me], l_ref.at[k_me], (x, y, 1 - c)) for w_ref, l_ref in zip(ins, lands)]
    for px, py in _other_chips(x, y):
        for w_ref, l_ref in zip(ins, lands):
            rows = _half_rows(w_ref.shape[0], c)
            plan.append((w_ref.at[rows], l_ref.at[k_me, rows], l_ref.at[2 * px + py, rows], (px, py, c)))
    return plan


def _gather_start(name, shards):
    lands = [lax.empty((N_CHIPS,) + s.shape, s.dtype) for s in shards]
    return _split_start(name, shards, lands, len(shards) * N_CHIPS, _gather_plan)


def _gather_wait(name, handle, after):
    return _split_wait(name, handle, after, _gather_plan)[1]


def _forward_halves(name, blocks):
    n = len(blocks)
    n_sem = n * (N_CHIPS - 1)

    def body(*refs):
        outs, (ssem, rsem) = refs[n:2 * n], refs[2 * n:]
        x, y, c = _me()
        sib = (x, y, 1 - c)
        chips = _other_chips(x, y)
        sends = []
        for r, (px, py) in enumerate(chips):
            for i, o_ref in enumerate(outs):
                blk = o_ref.at[2 * px + py, _half_rows(o_ref.shape[1], c)]
                cp = _remote(blk, blk, ssem.at[r * n + i], rsem.at[r * n + i], sib)
                cp.start()
                sends.append(cp)
        for r, (px, py) in enumerate(chips):
            for i, o_ref in enumerate(outs):
                blk = o_ref.at[2 * px + py, _half_rows(o_ref.shape[1], 1 - c)]
                _remote(blk, blk, ssem.at[r * n + i], rsem.at[r * n + i], sib).wait_recv()
        for cp in sends:
            cp.wait_send()

    return _pcall(
        body, name=name, in_specs=[ANY] * n, out_specs=[ANY] * n,
        out_shape=[jax.ShapeDtypeStruct(b.shape, b.dtype) for b in blocks],
        input_output_aliases={i: i for i in range(n)},
        scratch_shapes=[pltpu.SemaphoreType.DMA((n_sem,)), pltpu.SemaphoreType.DMA((n_sem,))],
    )(*blocks)


def _scatter_plan(ins, lands):
    x, y, c = _me()
    k_me = 2 * x + y
    return [(p_ref.at[2 * px + py], l_ref.at[k_me], l_ref.at[2 * px + py], (px, py, c))
            for px, py in _other_chips(x, y) for p_ref, l_ref in zip(ins, lands)]


def _scatter_start(name, parts):
    lands = [lax.empty(p.shape, p.dtype) for p in parts]
    return _split_start(name, parts, lands, len(parts) * (N_CHIPS - 1), _scatter_plan)


def _scatter_wait(name, handle, after):
    return _split_wait(name, handle, after, _scatter_plan)[1]


def _share_and_reduce(halves, v):
    n = len(halves)
    rows = v.shape[0]
    half = rows // 2
    assert half % SUBLANES == 0

    def body(*refs):
        ins, v_ref, outs, out_ref = refs[:n], refs[n], refs[n + 1:2 * n + 1], refs[2 * n + 1]
        pair_buf, mine, chip_buf, ssem, rsem, half_ssem, half_rsem = refs[2 * n + 2:]
        x, y, c = _me()
        k_me = 2 * x + y
        sib = (x, y, 1 - c)
        copies = [_remote(r_ref, o_ref, half_ssem.at[i], half_rsem.at[i], sib)
                  for i, (r_ref, o_ref) in enumerate(zip(ins, outs))]
        for cp in copies:
            cp.start()

        def rows_of(cc):
            return pl.ds(pl.multiple_of(cc * half, SUBLANES), half)

        swap = _remote(v_ref.at[rows_of(1 - c)], pair_buf, ssem.at[0], rsem.at[0], sib)
        swap.start()
        swap.wait()
        mine[...] = v_ref[rows_of(c), :] + pair_buf[...]
        chip_buf[k_me] = mine[...]
        sends = [_remote(mine, chip_buf.at[k_me], ssem.at[1 + r], rsem.at[1 + r], (px, py, c))
                 for r, (px, py) in enumerate(_other_chips(x, y))]
        for cp in sends:
            cp.start()
        for r, (px, py) in enumerate(_other_chips(x, y)):
            blk = chip_buf.at[2 * px + py]
            _remote(blk, blk, ssem.at[1 + r], rsem.at[1 + r], (px, py, c)).wait_recv()
        total = chip_buf[0]
        for k in range(1, N_CHIPS):
            total = total + chip_buf[k]
        out_ref[rows_of(c), :] = total
        for cp in sends:
            cp.wait_send()
        share = _remote(out_ref.at[rows_of(c)], out_ref.at[rows_of(c)], ssem.at[N_CHIPS], rsem.at[N_CHIPS], sib)
        share.start()
        got = out_ref.at[rows_of(1 - c)]
        _remote(got, got, ssem.at[N_CHIPS], rsem.at[N_CHIPS], sib).wait_recv()
        share.wait_send()
        for cp in copies:
            cp.wait()

    vm = pl.BlockSpec(memory_space=pltpu.VMEM)
    outs = _call(
        body, name="share_and_reduce", in_specs=[ANY] * n + [vm], out_specs=[ANY] * n + [vm],
        out_shape=[pltpu.HBM(h.shape, h.dtype) for h in halves] + [jax.ShapeDtypeStruct((rows, LANES), F32)],
        scratch_shapes=[pltpu.VMEM((half, LANES), F32), pltpu.VMEM((half, LANES), F32),
                        pltpu.VMEM((N_CHIPS, half, LANES), F32), pltpu.SemaphoreType.DMA((N_CHIPS + 1,)),
                        pltpu.SemaphoreType.DMA((N_CHIPS + 1,)), pltpu.SemaphoreType.DMA((n,)),
                        pltpu.SemaphoreType.DMA((n,))],
        compiler_params=pltpu.CompilerParams(vmem_limit_bytes=32 * 1024 * 1024),
    )(*[pltpu.with_memory_space_constraint(h, pltpu.HBM) for h in halves], v)
    return outs[:n], outs[n]


def _add_pair(name, core, chip, g, theirs):
    _, half, cols = theirs.shape
    tr = _tile(half, (256, 176, 128))
    nb = half // tr

    def body(c_ref, k_ref, g_ref, t_ref, o32_ref, o16_ref):
        s = g_ref[...] + t_ref[...]
        o16_ref[...] = s.astype(BF16)

        @pl.when(pl.program_id(1) == k_ref[0])
        def _():
            o32_ref[...] = s

    spec = pl.BlockSpec((None, tr, cols), lambda i, k, c_ref, k_ref: (k, i, 0))
    grid_spec = pltpu.PrefetchScalarGridSpec(
        num_scalar_prefetch=2, grid=(nb, N_CHIPS),
        in_specs=[pl.BlockSpec((None, tr, cols), lambda i, k, c_ref, k_ref: (k, c_ref[0] * nb + i, 0)), spec],
        out_specs=[pl.BlockSpec((tr, cols), lambda i, k, c_ref, k_ref: (i, 0)), spec])
    return _pcall(
        body, name=name, grid_spec=grid_spec,
        out_shape=[jax.ShapeDtypeStruct((half, cols), F32), jax.ShapeDtypeStruct(theirs.shape, BF16)],
        compiler_params=_params(("arbitrary", "arbitrary"), 8 * _nbytes((tr, cols + LANES), F32)),
    )(core, chip, g, theirs)


def _add_chips(name, chip, p32, recv):
    half, cols = p32.shape
    tr = _tile(half, (256, 176, 128))

    def body(k_ref, p_ref, r0_ref, r1_ref, r2_ref, o_ref):
        o_ref[...] = ((p_ref[...] + r0_ref[...].astype(F32)) + r1_ref[...].astype(F32)) + r2_ref[...].astype(F32)

    def other(r):
        return pl.BlockSpec((None, tr, cols), lambda i, k_ref: (r + (k_ref[0] <= r).astype(jnp.int32), i, 0))
    grid_spec = pltpu.PrefetchScalarGridSpec(
        num_scalar_prefetch=1, grid=(half // tr,),
        in_specs=[pl.BlockSpec((tr, cols), lambda i, k_ref: (i, 0)), other(0), other(1), other(2)],
        out_specs=pl.BlockSpec((tr, cols), lambda i, k_ref: (i, 0)))
    return _pcall(
        body, name=name, grid_spec=grid_spec, out_shape=jax.ShapeDtypeStruct((half, cols), F32),
        compiler_params=_params(("arbitrary",), 10 * _nbytes((tr, cols + LANES), F32)),
    )(chip, p32, recv, recv, recv)


def kernel(x, mem, w_in, b_in, hg_lb_logits, hg_norm_w, ml_conv_w, ml_conv_b, ml_norm_w, w_out, ln1_g, ln1_b, ca_wq, ca_wkv, ca_wo, ln2_g, ln2_b, ffn_w_up, ffn_conv_w, ffn_conv_b, ffn_w_down, ln3_g, ln3_b, loss_target, m_w_in, m_b_in, m_hg_lb_logits, m_hg_norm_w, m_ml_conv_w, m_ml_conv_b, m_ml_norm_w, m_w_out, m_ln1_g, m_ln1_b, m_ca_wq, m_ca_wkv, m_ca_wo, m_ln2_g, m_ln2_b, m_ffn_w_up, m_ffn_conv_w, m_ffn_conv_b, m_ffn_w_down, m_ln3_g, m_ln3_b, v_w_in, v_b_in, v_hg_lb_logits, v_hg_norm_w, v_ml_conv_w, v_ml_conv_b, v_ml_norm_w, v_w_out, v_ln1_g, v_ln1_b, v_ca_wq, v_ca_wkv, v_ca_wo, v_ln2_g, v_ln2_b, v_ffn_w_up, v_ffn_conv_w, v_ffn_conv_b, v_ffn_w_down, v_ln3_g, v_ln3_b):
    return _train_step(dict(locals()))


WEIGHTS = ("w_in", "b_in", "hg_lb_logits", "hg_norm_w", "ml_conv_w", "ml_conv_b", "ml_norm_w", "w_out", "ln1_g",
           "ln1_b", "ca_wq", "ca_wkv", "ca_wo", "ln2_g", "ln2_b", "ffn_w_up", "ffn_conv_w", "ffn_conv_b",
           "ffn_w_down", "ln3_g", "ln3_b")
MATRICES = ("w_in", "w_out", "ca_wq", "ca_wkv", "ca_wo", "ffn_w_up", "ffn_w_down")
COL_SHARDED = ("w_in", "ca_wkv", "ffn_w_up", "ml_conv_w", "ffn_conv_w")
SMALL = tuple(n for n in WEIGHTS if n not in MATRICES)
PART_ROWS = 16


def _part_rows(shape):
    n = 1
    for s in shape:
        n *= s
    return -(-n // (LANES * PART_ROWS)) * PART_ROWS


def _pack(arrs, dtype):
    parts = []
    for a in arrs:
        flat = a.reshape(-1).astype(dtype)
        flat = jnp.pad(flat, (0, _part_rows(a.shape) * LANES - flat.shape[0]))
        parts.append(flat.reshape(-1, LANES))
    return jnp.concatenate(parts, axis=0)


def _unpack(buf, shapes):
    lead = buf.shape[:-2]
    outs, r = [], 0
    for sh in shapes:
        n = 1
        for s in sh:
            n *= s
        nr = _part_rows(sh)
        flat = buf[..., r:r + nr, :].reshape(lead + (nr * LANES,))
        outs.append(flat[..., :n].reshape(lead + tuple(sh)))
        r += nr
    return outs


def _cat_cols(s):
    return jnp.moveaxis(s, 0, 1).reshape(s.shape[1], -1)


def _stack_rows(s):
    return s.reshape(-1, s.shape[-1])


def _train_step(a):
    xs, mems, tgt = a["x"][0], a["mem"][0], a["loss_target"][0]
    core = lax.axis_index("c").astype(jnp.int32).reshape(1)
    chip = (2 * lax.axis_index("x") + lax.axis_index("y")).astype(jnp.int32).reshape(1)
    k_me = chip[0]
    shard = {n: a[n][0] for n in MATRICES}

    later = [n for n in MATRICES if n != "w_in"]
    w_in, taps = _gather_weights([shard["w_in"].astype(BF16), _pack([a["ml_conv_w"][0], a["ffn_conv_w"][0]], F32)])
    w = {"w_in": jnp.pad(_cat_cols(w_in), ((0, 0), (0, D_IN_PAD - D_IN)))}
    gathering, token = _gather_start("gather_start", [shard[n].astype(BF16) for n in later])
    ml_cw, ffn_cw = [_cat_cols(s) for s in _unpack(taps, [a["ml_conv_w"].shape[1:], a["ffn_conv_w"].shape[1:]])]
    b_in_p = jnp.pad(a["b_in"], ((0, 0), (0, D_IN_PAD - D_IN))) + token[0:1, 0:1]
    mixer_w = (a["hg_lb_logits"], a["hg_norm_w"], ml_cw, a["ml_conv_b"], a["ml_norm_w"])
    up_cols = a["ffn_w_up"].shape[-1]

    proj, xb = _mm("proj", "nn", xs, w["w_in"], bias=b_in_p, a_copy_dtype=BF16, tm=256, tn=D_IN_PAD)
    y, hst, cst, nst, mst = _mixer_fwd(proj, *mixer_w)
    w.update(zip(later, _forward_halves("forward_halves", _gather_wait("gather_wait", gathering, y))))
    for n in ("w_out", "ca_wq", "ca_wo", "ffn_w_down"):
        w[n] = _stack_rows(w[n])
    z1, x1, x1b = _mm("mix_out", "nn", y, w["w_out"], res=xs, res_scale=ALPHA, ln=("fwd", a["ln1_g"], a["ln1_b"]),
                      copy_dtype=BF16)
    q = _mm("ca_q", "nn", x1b, w["ca_wq"], out_dtype=BF16, tn=D_MODEL)
    kv = _mm("ca_kv", "nn", mems, w["ca_wkv"])
    o = _attn_fwd(q, kv)
    z2, x2, x2b = _mm("ca_out", "nn", o, w["ca_wo"], res=x1, res_scale=ALPHA, ln=("fwd", a["ln2_g"], a["ln2_b"]),
                      copy_dtype=BF16)
    w_up = w["ffn_w_up"]
    assert w_up.shape == (2 * FFN_J, D_MODEL, FFN_W)
    u, hmid, dz3, g_ln3g, g_ln3b, loss_part, dz3b = _ffn_fwd(
        x2b, x2, w_up, ffn_cw, a["ffn_conv_b"], w["ffn_w_down"], a["ln3_g"], a["ln3_b"], tgt)

    grads = {"ln3_g": g_ln3g, "ln3_b": g_ln3b}
    grads["ffn_w_down"] = _mm("g_w_down", "tn", hmid, dz3b, tm=D_FF // 2, tn=D_MODEL)
    du, g_cw, g_cb, dz2, grads["ln2_g"], grads["ln2_b"], dz2b = _ffn_bwd(
        u, ffn_cw, a["ffn_conv_b"], dz3b, dz3, w["ffn_w_down"], w_up, z2, a["ln2_g"], a["ln2_b"])
    grads["ffn_conv_w"] = jnp.transpose(g_cw, (2, 1, 0, 3)).reshape(FFN_CONV, 2 * D_FF)
    grads["ffn_conv_b"] = jnp.transpose(g_cb, (2, 1, 0, 3)).reshape(1, 2 * D_FF)
    grads["ffn_w_up"] = _mm("g_w_up", "tn", x2b, du, out_groups=N_CHIPS, tm=D_MODEL, tn=up_cols)
    grads["ffn_w_down"] = grads["ffn_w_down"].reshape((N_CHIPS,) + shard["ffn_w_down"].shape)
    pending = {}

    def reduce_start(tag, names, swapped=None):
        group = [grads[n] for n in names]
        group, theirs = swapped or (group, _swap_halves("swap_halves_" + tag, group))
        sums = [_add_pair("add_pair_" + n, core, chip, g, t) for n, g, t in zip(names, group, theirs)]
        handle, token = _scatter_start("scatter_start_" + tag, [s16 for _, s16 in sums])
        pending[tag] = (names, [s32 for s32, _ in sums], handle)
        return token[0:1, 0:1]

    ffn = ("ffn_w_up", "ffn_w_down")
    swapping, token = _swap_start("swap_start_ffn", [grads[n] for n in ffn])
    do = _mm("d_o", "nt", dz2b, w["ca_wo"], bias=jnp.zeros((1, D_MODEL), F32) + token[0:1, 0:1], out_dtype=BF16,
             tn=D_MODEL)
    grads["ca_wo"] = _mm("g_wo", "tn", o, dz2b, tm=D_MODEL // 2, tn=D_MODEL)
    zero = reduce_start("ffn", ffn, _swap_wait("swap_wait_ffn", swapping, grads["ca_wo"]))
    dq, dkv = _attn_bwd(q, kv + zero, do)
    grads["ca_wq"] = _mm("g_wq", "tn", x1b, dq, tm=D_MODEL // 2, tn=D_MODEL)
    grads["ca_wkv"] = _mm("g_wkv", "tn", mems, dkv, out_groups=N_CHIPS, tm=D_MODEL)
    dz1, grads["ln1_g"], grads["ln1_b"], dz1b = _mm("d_x1", "nt", dq, w["ca_wq"], res=dz2, res_scale=ALPHA,
                                                    ln=("bwd", z1, a["ln1_g"], a["ln1_b"]), copy_dtype=BF16)
    grads["w_out"] = _mm("g_w_out", "tn", y, dz1b, tm=D_MODEL // 2, tn=D_MODEL)
    for n in ("w_out", "ca_wq", "ca_wo"):
        grads[n] = grads[n].reshape((N_CHIPS,) + shard[n].shape)
    attn = ("w_out", "ca_wq", "ca_wkv", "ca_wo")
    swapping, token = _swap_start("swap_start_attn", [grads[n] for n in attn])
    dy = _mm("d_y", "nt", dz1b, w["w_out"], bias=jnp.zeros((1, D_MODEL), F32) + token[0:1, 0:1], tn=D_MODEL)
    zero = reduce_start("attn", attn, _swap_wait("swap_wait_attn", swapping, dy))
    (dproj, g_b_in, grads["hg_lb_logits"], grads["hg_norm_w"], grads["ml_conv_w"], grads["ml_conv_b"],
     grads["ml_norm_w"]) = _mixer_bwd(proj, dy, hst, cst, nst, mst, mixer_w[0], mixer_w[1] + zero, *mixer_w[2:])
    g_in = _mm("g_w_in", "tn", xb, dproj, tm=D_MODEL, tn=up_cols)

    halves = {}

    def reduce_end(tag, after):
        names, sums32, handle = pending[tag]
        for n, s32, r in zip(names, sums32, _scatter_wait("scatter_wait_" + tag, handle, after)):
            halves[n] = _add_chips("add_chips_" + n, chip, s32, r)

    reduce_end("ffn", g_in)
    reduce_end("attn", g_in)
    grads["w_in"] = jnp.moveaxis(g_in[:, :D_IN].reshape(D_MODEL, N_CHIPS, -1), 1, 0)
    grads["b_in"] = g_b_in[:, :D_IN]
    zero = reduce_start("in", ("w_in",))
    dx = _mm("d_x", "nt", dproj, w["w_in"], bias=jnp.zeros((1, D_MODEL), F32) + zero, res=dz1, res_scale=ALPHA,
             tm=256, tn=D_MODEL)
    reduce_end("in", dx)
    halves = [halves[n] for n in MATRICES]

    small_shapes = [grads[n].shape for n in SMALL] + [loss_part.shape]
    other_halves, summed = _share_and_reduce(halves, _pack([grads[n] for n in SMALL] + [loss_part], F32))
    summed = _unpack(summed, small_shapes)
    loss = summed[-1][0, 0]
    for n, g in zip(SMALL, summed[:-1]):
        if n in COL_SHARDED:
            cols = a[n].shape[-1]
            g = lax.dynamic_slice_in_dim(g, k_me * cols, cols, axis=1)
        grads[n] = g

    delta, new_m, new_v = {}, {}, {}
    for n, mine, theirs in zip(MATRICES, halves, other_halves):
        grads[n], delta[n], new_m[n], new_v[n] = _adamw_halves(
            "adamw_" + n, core, shard[n], mine, theirs, a["m_" + n][0], a["v_" + n][0])
    small_w = [a[n][0] if a[n].ndim == 3 else a[n] for n in SMALL]
    small_m = [a["m_" + n][0] if a[n].ndim == 3 else a["m_" + n] for n in SMALL]
    small_v = [a["v_" + n][0] if a[n].ndim == 3 else a["v_" + n] for n in SMALL]
    for out, vals in zip((delta, new_m, new_v),
                         _adamw_many("adamw_small", small_w, [grads[n] for n in SMALL], small_m, small_v)):
        out.update(zip(SMALL, vals))

    def shaped(d):
        return [d[n].reshape(a[n].shape) for n in WEIGHTS]
    return (loss, dx[None], *shaped(grads), *shaped(delta), *shaped(new_m), *shaped(new_v))
```

```python
import functools

import jax
import jax.numpy as jnp
from jax import lax
from jax.experimental import pallas as pl
from jax.experimental.pallas import tpu as pltpu

F32 = jnp.float32
BF16 = jnp.bfloat16

D_MODEL = 1024
HEADS = 4
DK = 128
D_GRP = HEADS * DK
CHUNK = 64
ML_CONV = 4
FFN_CONV = 3
D_FF = 2816
CA_DH = D_MODEL // HEADS
DEPTH = 1
ALPHA = (2.0 * DEPTH) ** 0.25
LN_EPS = 1e-5
NEG_BIG = -1e30
D_IN = 8 * D_GRP + 2 * HEADS
D_IN_PAD = 8 * D_GRP + 128
ADAM_LR, ADAM_B1, ADAM_B2, ADAM_EPS, ADAM_WD, ADAM_STEP = 0.001, 0.9, 0.999, 1e-08, 0.01, 10

SUBLANES = 8
LANES = 128
VMEM_BYTES = 64 * 1024 * 1024


def _pcall(body, pin=True, **kw):
    if not pin:
        return _call(body, **kw)
    kw["out_shape"] = jax.tree.map(lambda s: pltpu.HBM(s.shape, s.dtype), kw["out_shape"])
    call = _call(body, **kw)

    def pinned(*args):
        return call(*[pltpu.with_memory_space_constraint(x, pltpu.HBM) if jnp.issubdtype(x.dtype, jnp.floating) else x
                      for x in args])
    return pinned


def _call(body, **kw):
    return pl.pallas_call(body, **kw)


def _params(semantics, vmem_bytes):
    limit = int(min(max(2 * vmem_bytes, 16 * 1024 * 1024), VMEM_BYTES - 8 * 1024 * 1024))
    return pltpu.CompilerParams(dimension_semantics=semantics, vmem_limit_bytes=limit)


def _nbytes(shape, dtype):
    n = 1
    for s in shape:
        n *= s
    return n * jnp.dtype(dtype).itemsize


def _dg(a, b, ca, cb):
    return lax.dot_general(a.astype(BF16), b.astype(BF16), (((ca,), (cb,)), ((), ())),
                           preferred_element_type=F32)


@jax.custom_vjp
def mm_nn(a, b):
    return _dg(a, b, 1, 0)


mm_nn.defvjp(lambda a, b: (_dg(a, b, 1, 0), (a, b)),
             lambda r, g: (_dg(g, r[1], 1, 1).astype(r[0].dtype), _dg(r[0], g, 0, 0).astype(r[1].dtype)))


@jax.custom_vjp
def mm_nt(a, b):
    return _dg(a, b, 1, 1)


mm_nt.defvjp(lambda a, b: (_dg(a, b, 1, 1), (a, b)),
             lambda r, g: (_dg(g, r[1], 1, 0).astype(r[0].dtype), _dg(g, r[0], 0, 0).astype(r[1].dtype)))


@jax.custom_vjp
def mm_tn(a, b):
    return _dg(a, b, 0, 0)


mm_tn.defvjp(lambda a, b: (_dg(a, b, 0, 0), (a, b)),
             lambda r, g: (_dg(r[1], g, 1, 1).astype(r[0].dtype), _dg(r[0], g, 1, 0).astype(r[1].dtype)))


def _tri(n, lower):
    r = lax.broadcasted_iota(jnp.int32, (n, n), 0)
    c = lax.broadcasted_iota(jnp.int32, (n, n), 1)
    return ((r >= c) if lower else (r <= c)).astype(F32)


def _tri_dot(lower, x):
    t = _tri(x.shape[0], lower).astype(BF16)
    hi = x.astype(BF16)
    rest = x - hi.astype(F32)
    mid = rest.astype(BF16)
    lo = (rest - mid.astype(F32)).astype(BF16)
    return sum(lax.dot_general(t, p, (((1,), (0,)), ((), ())), preferred_element_type=F32) for p in (hi, mid, lo))


@jax.custom_vjp
def cumsum_rows(x):
    return _tri_dot(True, x)


cumsum_rows.defvjp(lambda x: (_tri_dot(True, x), None), lambda _, g: (_tri_dot(False, g),))


def _shift_impl(halo, x, d):
    xx = jnp.concatenate([halo, x], axis=0)
    return pltpu.roll(xx, d, 0)[SUBLANES:]


@functools.partial(jax.custom_vjp, nondiff_argnums=(2,))
def shift_rows(halo, x, d):
    return _shift_impl(halo, x, d)


def _shift_bwd(d, _, g):
    n = g.shape[0] + SUBLANES
    gg = jnp.concatenate([jnp.zeros((SUBLANES, g.shape[1]), g.dtype), g], axis=0)
    r = pltpu.roll(gg, n - d, 0)
    return r[:SUBLANES], r[SUBLANES:]


shift_rows.defvjp(lambda halo, x, d: (_shift_impl(halo, x, d), None), _shift_bwd)


def causal_conv(halo, x, w_rows, b):
    k = len(w_rows)
    y = b + w_rows[k - 1] * x
    for d in range(1, k):
        y = y + w_rows[k - 1 - d] * shift_rows(halo, x, d)
    return y


def _sigmoid(x):
    return 1.0 / (1.0 + jnp.exp(-x))


def _silu(x):
    return x * _sigmoid(x)


def _log_sigmoid(x):
    return jnp.minimum(x, 0.0) - jnp.log(1.0 + jnp.exp(-jnp.abs(x)))


def _pick_row(x, i):
    row = lax.broadcasted_iota(jnp.int32, (x.shape[0], 1), 0)
    return jnp.sum(jnp.where(row == i, x, 0.0), axis=0, keepdims=True)


def _layer_norm(z, g, b):
    mu = jnp.mean(z, axis=-1, keepdims=True)
    zc = z - mu
    var = jnp.mean(zc * zc, axis=-1, keepdims=True)
    return zc * lax.rsqrt(var + LN_EPS) * g + b


def _qk_conv(halo, x, w0, w1, w2, w3, b):
    return _silu(causal_conv(halo, x, (w0, w1, w2, w3), b))


def _grp(i):
    return pl.ds(i * D_GRP, D_GRP)


def _mixer_specs(n_chunks, reverse):
    def chunk(c):
        return n_chunks - 1 - c if reverse else c
    row8 = CHUNK // SUBLANES
    proj_spec = pl.BlockSpec((CHUNK, D_IN_PAD), lambda c: (chunk(c), 0))
    halo_spec = pl.BlockSpec((SUBLANES, 2 * D_GRP), lambda c: (jnp.maximum(chunk(c) * row8 - 1, 0), 2))
    small = [pl.BlockSpec((2, D_GRP), lambda c: (0, 0)), pl.BlockSpec((1, D_GRP), lambda c: (0, 0)),
             pl.BlockSpec((ML_CONV, 2 * D_GRP), lambda c: (0, 0)), pl.BlockSpec((1, 2 * D_GRP), lambda c: (0, 0)),
             pl.BlockSpec((1, D_GRP), lambda c: (0, 0))]
    state_specs = [pl.BlockSpec((1, HEADS, DK, DK), lambda c: (chunk(c), 0, 0, 0)),
                   pl.BlockSpec((1, HEADS, DK, DK), lambda c: (chunk(c), 0, 0, 0)),
                   pl.BlockSpec((1, HEADS, 1, DK), lambda c: (chunk(c), 0, 0, 0)),
                   pl.BlockSpec((1, HEADS, 1, DK), lambda c: (chunk(c), 0, 0, 0))]
    y_spec = pl.BlockSpec((CHUNK, 2 * D_GRP), lambda c: (chunk(c), 0))
    return proj_spec, halo_spec, small, state_specs, y_spec, chunk


def _heads(x):
    return [x[:, h * DK:(h + 1) * DK] for h in range(HEADS)]


def _last(x, j):
    lane = lax.broadcasted_iota(jnp.int32, (1, x.shape[-1]), 1)
    return jnp.sum(jnp.where(lane == j, x, 0.0), axis=-1, keepdims=True)


def _hg_chunk(st_t, hq, hf, hi, hgate, l0, l1, nw):
    n = hq.shape[0]
    lb = _sigmoid(l0 - l1)
    q = _silu(hq)
    lf = jnp.log(lb + (1.0 - lb) * _sigmoid(hf))
    k = (1.0 - lb) * _sigmoid(-hf)
    b = cumsum_rows(lf)
    b_ref = _pick_row(b, n // 2 - 1)
    b_last = _pick_row(b, n - 1)
    qa, ka =_heads(q * jnp.exp(b - b_ref)), _heads(k * jnp.exp(b_ref - b))
    qe, kd, eb, v = _heads(q * jnp.exp(b)), _heads(k * jnp.exp(b_last - b)), _heads(jnp.exp(b_last)), _heads(hi)
    tri = _tri(n, True) > 0
    attn = [jnp.where(tri, mm_nt(qa[h], ka[h]), 0.0) for h in range(HEADS)]
    o = [mm_nn(attn[h], v[h]) + mm_nt(qe[h], st_t[h]) for h in range(HEADS)]
    st_new = jnp.stack([eb[h] * st_t[h] + mm_tn(v[h], kd[h]) for h in range(HEADS)])
    yn = [o[h] * lax.rsqrt(jnp.mean(o[h] * o[h], axis=-1, keepdims=True) + LN_EPS) for h in range(HEADS)]
    return st_new, jnp.concatenate(yn, axis=1) * nw * _silu(hgate)


def _ml_chunk(c_st, n_st, m_st, q, k, v, gates, og, nw):
    n = q.shape[0]
    ig = jnp.stack([_last(gates, h) for h in range(HEADS)])
    log_f = _log_sigmoid(gates)
    fl = jnp.stack([_last(log_f, HEADS + h) for h in range(HEADS)])
    bw = cumsum_rows(jnp.concatenate([jnp.broadcast_to(fl[h], (n, DK)) for h in range(HEADS)], axis=1))
    b = jnp.stack([_last(x, 0) for x in _heads(bw)])
    g = jnp.sum(fl, axis=1, keepdims=True)
    eye = lax.broadcasted_iota(jnp.int32, (n, n), 0) == lax.broadcasted_iota(jnp.int32, (n, n), 1)
    e_row = jnp.sum(jnp.where(eye, ig - b, 0.0), axis=1, keepdims=True)
    d = jnp.where(_tri(n, True) > 0, b + e_row, -jnp.inf)
    inter = b + m_st
    m_t = jnp.maximum(inter, jnp.max(d, axis=2, keepdims=True))
    qs, kh, vh = _heads(q * (DK ** -0.5)), _heads(k), _heads(v)
    s = jnp.stack([mm_nt(qs[h], kh[h]) for h in range(HEADS)]) * jnp.exp(d - m_t)
    w_inter = jnp.exp(inter - m_t)
    num = (jnp.stack([mm_nn(s[h], vh[h]) for h in range(HEADS)])
           + w_inter * jnp.stack([mm_nn(qs[h], c_st[h]) for h in range(HEADS)]))
    den = jnp.sum(s, axis=2, keepdims=True) + w_inter * jnp.sum(jnp.stack(qs) * n_st, axis=2, keepdims=True)
    h_out = num / jnp.maximum(jnp.abs(den), jnp.exp(-m_t))
    a = g - b + ig
    m_new = jnp.maximum(g + m_st, jnp.max(a, axis=1, keepdims=True))
    decay = jnp.exp(g + m_st - m_new)
    wk = jnp.stack(kh) * jnp.exp(a - m_new)
    c_new = decay * c_st + jnp.stack([mm_tn(wk[h], vh[h]) for h in range(HEADS)])
    n_new = decay * n_st + jnp.sum(wk, axis=1, keepdims=True)
    hc = h_out - jnp.mean(h_out, axis=-1, keepdims=True)
    yn = hc * lax.rsqrt(jnp.mean(hc * hc, axis=-1, keepdims=True) + LN_EPS)
    y = _sigmoid(og) * (jnp.concatenate([yn[h] for h in range(HEADS)], axis=1) * nw)
    return c_new, n_new, m_new, y


def _mixer_inputs(proj_ref, lg_ref, hnw_ref, mnw_ref, qk):
    hg_in = (proj_ref[:, _grp(0)], proj_ref[:, _grp(1)], proj_ref[:, _grp(2)], proj_ref[:, _grp(3)],
             lg_ref[0:1, :], lg_ref[1:2, :], hnw_ref[...])
    ml_in = (qk[:, :D_GRP], qk[:, D_GRP:], proj_ref[:, _grp(6)], proj_ref[:, pl.ds(8 * D_GRP, LANES)],
             proj_ref[:, _grp(7)], mnw_ref[...])
    return hg_in, ml_in


def _mixer_fwd(proj, lb_logits, hg_nw, conv_w, conv_b, ml_nw):
    seq = proj.shape[0]
    n_chunks = seq // CHUNK
    proj_spec, halo_spec, small, state_specs, y_spec, _ = _mixer_specs(n_chunks, False)

    def body(proj_ref, halo_ref, lg_ref, hnw_ref, cw_ref, cb_ref, mnw_ref,
             y_ref, hst_ref, cst_ref, nst_ref, mst_ref, hs, cs, ns, ms):
        c = pl.program_id(0)

        @pl.when(c == 0)
        def _():
            hs[...] = jnp.zeros_like(hs)
            cs[...] = jnp.zeros_like(cs)
            ns[...] = jnp.zeros_like(ns)
            ms[...] = jnp.full(ms.shape, NEG_BIG, F32)

        hst_ref[0] = hs[...]
        cst_ref[0] = cs[...]
        nst_ref[0] = ns[...]
        mst_ref[0] = ms[...]
        halo = jnp.where(c > 0, halo_ref[...], 0.0)
        qk = _qk_conv(halo, proj_ref[:, pl.ds(4 * D_GRP, 2 * D_GRP)],
                      cw_ref[0:1, :], cw_ref[1:2, :], cw_ref[2:3, :], cw_ref[3:4, :], cb_ref[...])
        hg_in, ml_in = _mixer_inputs(proj_ref, lg_ref, hnw_ref, mnw_ref, qk)
        hs[...], y_hg = _hg_chunk(hs[...], *hg_in)
        cs[...], ns[...], m_new, y_ml = _ml_chunk(cs[...], ns[...], _last(ms[...], 0), *ml_in)
        ms[...] = jnp.broadcast_to(m_new, ms.shape)
        y_ref[:, pl.ds(0, D_GRP)] = y_hg.astype(BF16)
        y_ref[:, pl.ds(D_GRP, D_GRP)] = y_ml.astype(BF16)

    st = jax.ShapeDtypeStruct((n_chunks, HEADS, DK, DK), F32)
    vec = jax.ShapeDtypeStruct((n_chunks, HEADS, 1, DK), F32)
    vmem = 2 * (_nbytes((CHUNK, D_IN_PAD), F32) + _nbytes((CHUNK, 2 * D_GRP), F32) + 2 * _nbytes((HEADS, DK, DK), F32)) \
        + 2 * _nbytes((HEADS, DK, DK), F32)
    return _pcall(
        body, name="mixer_fwd", grid=(n_chunks,),
        in_specs=[proj_spec, halo_spec] + small,
        out_specs=[y_spec] + state_specs,
        out_shape=[jax.ShapeDtypeStruct((seq, 2 * D_GRP), BF16), st, st, vec, vec],
        scratch_shapes=[pltpu.VMEM((HEADS, DK, DK), F32), pltpu.VMEM((HEADS, DK, DK), F32),
                        pltpu.VMEM((HEADS, 1, DK), F32), pltpu.VMEM((HEADS, 1, DK), F32)],
        compiler_params=_params(("arbitrary",), vmem),
    )(proj, proj, lb_logits, hg_nw, conv_w, conv_b, ml_nw)


def _mixer_bwd(proj, dy, hst, cst, nst, mst, lb_logits, hg_nw, conv_w, conv_b, ml_nw):
    seq = proj.shape[0]
    n_chunks = seq // CHUNK
    proj_spec, halo_spec, small, state_specs, y_spec, _ = _mixer_specs(n_chunks, True)

    def body(proj_ref, halo_ref, dy_ref, hst_ref, cst_ref, nst_ref, mst_ref,
             lg_ref, hnw_ref, cw_ref, cb_ref, mnw_ref,
             dproj_ref, dbin_ref, dlg_ref, dhnw_ref, dcw_ref, dcb_ref, dmnw_ref,
             dhs, dcs, dns, dms, dhalo):
        c = pl.program_id(0)

        @pl.when(c == 0)
        def _():
            for r in (dhs, dcs, dns, dms, dhalo, dbin_ref, dlg_ref, dhnw_ref, dcw_ref, dcb_ref, dmnw_ref):
                r[...] = jnp.zeros_like(r)

        def put(cols, val):
            dproj_ref[:, cols] = val.astype(BF16)
            dbin_ref[:, cols] += jnp.sum(val, axis=0, keepdims=True)

        first = c == n_chunks - 1
        halo = jnp.where(first, 0.0, halo_ref[...])
        x_qk = proj_ref[:, pl.ds(4 * D_GRP, 2 * D_GRP)]
        conv_args = (halo, x_qk, cw_ref[0:1, :], cw_ref[1:2, :], cw_ref[2:3, :], cw_ref[3:4, :], cb_ref[...])
        qk, conv_vjp = jax.vjp(_qk_conv, *conv_args)
        hg_in, ml_in = _mixer_inputs(proj_ref, lg_ref, hnw_ref, mnw_ref, qk)
        _, hg_vjp = jax.vjp(_hg_chunk, hst_ref[0], *hg_in)
        _, ml_vjp = jax.vjp(_ml_chunk, cst_ref[0], nst_ref[0], _last(mst_ref[0], 0), *ml_in)
        dst, dhq, dhf, dhi, dhg, dl0, dl1, dnw = hg_vjp((dhs[...], dy_ref[:, pl.ds(0, D_GRP)]))
        dc, dn, dm, dq, dk, dv, dgates, dog, dmn = ml_vjp(
            (dcs[...], dns[...], _last(dms[...], 0), dy_ref[:, pl.ds(D_GRP, D_GRP)]))
        dhs[...] = dst
        dcs[...] = dc
        dns[...] = dn
        dms[...] = jnp.broadcast_to(dm, dms.shape)
        for i, val in ((0, dhq), (1, dhf), (2, dhi), (3, dhg), (6, dv), (7, dog)):
            put(_grp(i), val)
        put(pl.ds(8 * D_GRP, LANES), dgates)
        dlg_ref[0:1, :] += dl0
        dlg_ref[1:2, :] += dl1
        dhnw_ref[...] += dnw
        dmnw_ref[...] += dmn
        dh, dx, dw0, dw1, dw2, dw3, db = conv_vjp(jnp.concatenate([dq, dk], axis=1))
        tail = jnp.concatenate([jnp.zeros((CHUNK - SUBLANES, 2 * D_GRP), F32), dhalo[...]], axis=0)
        put(pl.ds(4 * D_GRP, 2 * D_GRP), dx + tail)
        dhalo[...] = dh
        for d, dw in enumerate((dw0, dw1, dw2, dw3)):
            dcw_ref[d:d + 1, :] += dw
        dcb_ref[...] += db

    row = pl.BlockSpec((1, D_GRP), lambda c: (0, 0))
    small_out = [pl.BlockSpec((1, D_IN_PAD), lambda c: (0, 0)), pl.BlockSpec((2, D_GRP), lambda c: (0, 0)), row,
                 pl.BlockSpec((ML_CONV, 2 * D_GRP), lambda c: (0, 0)), pl.BlockSpec((1, 2 * D_GRP), lambda c: (0, 0)), row]
    dy_spec = pl.BlockSpec((CHUNK, 2 * D_GRP), y_spec.index_map)
    vmem = 2 * (2 * _nbytes((CHUNK, D_IN_PAD), F32) + _nbytes((CHUNK, 2 * D_GRP), F32)
                + 2 * _nbytes((HEADS, DK, DK), F32)) + 2 * _nbytes((HEADS, DK, DK), F32) + 4 * 1024 * 1024
    return _pcall(
        body, name="mixer_bwd", grid=(n_chunks,),
        in_specs=[proj_spec, halo_spec, dy_spec] + state_specs + small,
        out_specs=[proj_spec] + small_out,
        out_shape=[jax.ShapeDtypeStruct((seq, D_IN_PAD), BF16), jax.ShapeDtypeStruct((1, D_IN_PAD), F32),
                   jax.ShapeDtypeStruct((2, D_GRP), F32), jax.ShapeDtypeStruct((1, D_GRP), F32),
                   jax.ShapeDtypeStruct((ML_CONV, 2 * D_GRP), F32), jax.ShapeDtypeStruct((1, 2 * D_GRP), F32),
                   jax.ShapeDtypeStruct((1, D_GRP), F32)],
        scratch_shapes=[pltpu.VMEM((HEADS, DK, DK), F32), pltpu.VMEM((HEADS, DK, DK), F32),
                        pltpu.VMEM((HEADS, 1, DK), F32), pltpu.VMEM((HEADS, 1, DK), F32),
                        pltpu.VMEM((SUBLANES, 2 * D_GRP), F32)],
        compiler_params=_params(("arbitrary",), vmem),
    )(proj, proj, dy, hst, cst, nst, mst, lb_logits, hg_nw, conv_w, conv_b, ml_nw)


def _tile(n, prefs, unit=None):
    unit = unit or n
    for p in prefs:
        if unit % p == 0 and n % p == 0:
            return p
    return unit


def _logical(arr):
    return arr.shape if arr.ndim == 2 else (arr.shape[1], arr.shape[0] * arr.shape[2])


def _group(arr):
    return arr.shape[-1]


def _split_spec(ndim, group, tr, tc, where):
    if ndim == 2:
        return pl.BlockSpec((tr, tc), where)
    per = group // tc
    assert per * tc == group, (group, tc)

    def index(*ids):
        bi, bj = where(*ids)
        return (bj // per, bi, bj % per)
    return pl.BlockSpec((None, tr, tc), index)


def _mm(name, mode, a, b, *, bias=None, res=None, res_scale=1.0, ln=None, out_dtype=F32, out_groups=None,
        copy_dtype=None, a_copy_dtype=None, tm=None, tn=None, tk=None):
    la, lb = _logical(a), _logical(b)
    if mode == "nn":
        (m, k), n = la, lb[1]
        n_unit = _group(b) if b.ndim == 3 else n
        kc = _group(a) if a.ndim == 3 else k
    elif mode == "nt":
        (m, k), n = la, lb[0]
        n_unit = n
        kc = min(_group(a) if a.ndim == 3 else k, _group(b) if b.ndim == 3 else k)
    else:
        (k, m), n = la, lb[1]
        n_unit, kc = (_group(b) if b.ndim == 3 else n), k
        assert a.ndim == 2
    if out_groups:
        n_unit = min(n_unit, n // out_groups)
    kind = ln[0] if ln else None
    tm = tm or (256 if ln else _tile(m, (512, 256, 128)))
    tn = n if ln else (tn or _tile(n, (512, 384, 256, 128), n_unit))
    if mode != "tn":
        tk = k
    elif tk is None:
        tk = _tile(k, (4096, 2048, 512, 256, 128) if (m // tm) * (n // tn) > 1 else (2048, 512, 256, 128))
    gi, gj, gk = m // tm, n // tn, k // tk
    assert gi * tm == m and gj * tn == n and gk * tk == k and n_unit % tn == 0, (name, m, n, k, tm, tn, tk)
    ca, cb = {"nn": (1, 0), "nt": (1, 1), "tn": (0, 0)}[mode]
    i_outer = gk > 1 or (gi - 1) * _nbytes(b.shape, b.dtype) <= (gj - 1) * _nbytes(a.shape, a.dtype)

    def ij(where):
        return (lambda p, q, kk: where(p, q, kk)) if i_outer else (lambda p, q, kk: where(q, p, kk))
    if mode == "tn":
        a_spec = pl.BlockSpec((tk, tm), ij(lambda i, j, kk: (kk, i)))
    elif a.ndim == 3:
        a_spec = pl.BlockSpec((a.shape[0], tm, _group(a)), ij(lambda i, j, kk: (0, i, 0)))
    else:
        a_spec = pl.BlockSpec((tm, k), ij(lambda i, j, kk: (i, 0)))
    if mode != "nt":
        b_spec = _split_spec(b.ndim, _group(b), tk, tn, ij(lambda i, j, kk: (kk, j)))
    elif b.ndim == 3:
        b_spec = pl.BlockSpec((b.shape[0], tn, _group(b)), ij(lambda i, j, kk: (0, j, 0)))
    else:
        b_spec = pl.BlockSpec((tn, k), ij(lambda i, j, kk: (j, 0)))
    row_spec = pl.BlockSpec((1, tn), ij(lambda i, j, kk: (0, j)))
    blk_spec = pl.BlockSpec((tm, tn), ij(lambda i, j, kk: (i, j)))
    ins, in_specs = [a, b], [a_spec, b_spec]
    if bias is not None:
        ins.append(bias), in_specs.append(row_spec)
    if res is not None:
        ins.append(res), in_specs.append(blk_spec)
    if kind == "fwd":
        ins += [ln[1], ln[2]]
        in_specs += [row_spec, row_spec]
    elif kind == "bwd":
        ins += [ln[1], ln[2], ln[3]]
        in_specs += [blk_spec, row_spec, row_spec]
    if out_groups:
        blk_out = jax.ShapeDtypeStruct((out_groups, m, n // out_groups), out_dtype)
        out_spec = _split_spec(3, n // out_groups, tm, tn, ij(lambda i, j, kk: (i, j)))
    else:
        blk_out, out_spec = jax.ShapeDtypeStruct((m, n), out_dtype), blk_spec
    row_out = jax.ShapeDtypeStruct((1, n), F32)
    if kind is None:
        out_shape, out_specs = [blk_out], [out_spec]
    elif kind == "fwd":
        out_shape, out_specs = [blk_out, blk_out], [blk_spec, blk_spec]
    else:
        out_shape, out_specs = [blk_out, row_out, row_out], [blk_spec, row_spec, row_spec]
    if copy_dtype is not None:
        out_shape.append(jax.ShapeDtypeStruct((m, n), copy_dtype))
        out_specs.append(blk_spec)
    if a_copy_dtype is not None:
        assert mode != "tn" and a.ndim == 2 and copy_dtype is None
        out_shape.append(jax.ShapeDtypeStruct((m, k), a_copy_dtype))
        out_specs.append(a_spec)
    n_in = len(ins)

    def body(*refs):
        in_refs, out_refs, acc_ref = refs[:n_in], refs[n_in:n_in + len(out_shape)], refs[-1]
        i, kk = pl.program_id(0 if i_outer else 1), pl.program_id(2)
        a_ref, b_ref = in_refs[:2]
        extra = list(in_refs[2:])
        if a_copy_dtype is not None:
            out_refs[-1][...] = a_ref[...].astype(a_copy_dtype)

        def epilogue(acc):
            rest = list(extra)
            if bias is not None:
                acc = acc + rest.pop(0)[...]
            if res is not None:
                acc = acc + res_scale * rest.pop(0)[...]
            if kind is None:
                out_refs[0][...] = acc.astype(out_dtype)
                return
            if kind == "fwd":
                out_refs[0][...] = acc
                y = _layer_norm(acc, rest[0][...], rest[1][...])
                out_refs[1][...] = y
                if copy_dtype is not None:
                    out_refs[-1][...] = y.astype(copy_dtype)
                return
            _, vjp = jax.vjp(_layer_norm, rest[0][...], rest[1][...], rest[2][...])
            dz, dg, db = vjp(acc)
            out_refs[0][...] = dz
            out_refs[1][...] += dg
            out_refs[2][...] += db
            if copy_dtype is not None:
                out_refs[-1][...] = dz.astype(copy_dtype)

        if kind == "bwd":
            @pl.when((i == 0) & (kk == 0))
            def _():
                out_refs[1][...] = jnp.zeros_like(out_refs[1])
                out_refs[2][...] = jnp.zeros_like(out_refs[2])

        def chunk(ref, c0, last):
            if ref.ndim == 3:
                g = ref.shape[2]
                return ref[c0 // g, :, pl.ds(c0 % g, kc)]
            return ref[:, pl.ds(c0, kc)] if last else ref[pl.ds(c0, kc), :]

        if mode == "tn" or kc == k:
            prod = _dg(a_ref[...], b_ref[...], ca, cb)
        else:
            prod = None
            for c0 in range(0, k, kc):
                part = _dg(chunk(a_ref, c0, True), chunk(b_ref, c0, mode == "nt"), ca, cb)
                prod = part if prod is None else prod + part
        if gk == 1:
            epilogue(prod)
            return

        @pl.when(kk == 0)
        def _():
            acc_ref[...] = prod

        @pl.when(kk > 0)
        def _():
            acc_ref[...] += prod

        @pl.when(kk == gk - 1)
        def _():
            epilogue(acc_ref[...])

    vmem = (2 * (_nbytes((tm, tk), a.dtype) + _nbytes((tk, tn), b.dtype))
            + (2 * len(ins) + 2 * len(out_shape) + 1) * _nbytes((tm, tn), F32))
    outs = _pcall(
        body, name=name, grid=(gi, gj, gk) if i_outer else (gj, gi, gk), in_specs=in_specs, out_specs=out_specs,
        out_shape=out_shape, scratch_shapes=[pltpu.VMEM((tm, tn) if gk > 1 else (SUBLANES, LANES), F32)],
        compiler_params=_params(("arbitrary", "arbitrary", "arbitrary"), vmem),
    )(*ins)
    return outs[0] if len(out_shape) == 1 else outs


def _attn_head(q, k, v):
    sc = mm_nt(q, k) * (CA_DH ** -0.5)
    e = jnp.exp(sc - jnp.max(sc, axis=-1, keepdims=True))
    return mm_nn(e / jnp.sum(e, axis=-1, keepdims=True), v)


def _attn_fwd(q, kv):
    seq, n_mem = q.shape[0], kv.shape[0]
    tq = _tile(seq, (512, 256, 128))

    def body(q_ref, kv_ref, o_ref):
        for h in range(HEADS):
            hd = pl.ds(h * CA_DH, CA_DH)
            o = _attn_head(q_ref[:, hd], kv_ref[:, hd], kv_ref[:, pl.ds(D_MODEL + h * CA_DH, CA_DH)])
            o_ref[:, hd] = o.astype(BF16)

    return _pcall(
        body, name="attn_fwd", grid=(seq // tq,),
        in_specs=[pl.BlockSpec((tq, D_MODEL), lambda i: (i, 0)), pl.BlockSpec((n_mem, 2 * D_MODEL), lambda i: (0, 0))],
        out_specs=pl.BlockSpec((tq, D_MODEL), lambda i: (i, 0)), out_shape=jax.ShapeDtypeStruct((seq, D_MODEL), BF16),
        compiler_params=_params(("arbitrary",), 4 * _nbytes((tq, D_MODEL), F32) + 2 * _nbytes((n_mem, 2 * D_MODEL), F32)),
    )(q, kv)


def _attn_bwd(q, kv, do):
    seq, n_mem = q.shape[0], kv.shape[0]
    tq = _tile(seq, (512, 256, 128))

    def body(q_ref, kv_ref, do_ref, dq_ref, dkv_ref):
        @pl.when(pl.program_id(0) == 0)
        def _():
            dkv_ref[...] = jnp.zeros_like(dkv_ref)

        for h in range(HEADS):
            hd = pl.ds(h * CA_DH, CA_DH)
            vd = pl.ds(D_MODEL + h * CA_DH, CA_DH)
            _, vjp = jax.vjp(_attn_head, q_ref[:, hd], kv_ref[:, hd], kv_ref[:, vd])
            dq, dk, dv = vjp(do_ref[:, hd].astype(F32))
            dq_ref[:, hd] = dq.astype(BF16)
            dkv_ref[:, hd] += dk
            dkv_ref[:, vd] += dv

    return _pcall(
        body, name="attn_bwd", grid=(seq // tq,),
        in_specs=[pl.BlockSpec((tq, D_MODEL), lambda i: (i, 0)), pl.BlockSpec((n_mem, 2 * D_MODEL), lambda i: (0, 0)),
                  pl.BlockSpec((tq, D_MODEL), lambda i: (i, 0))],
        out_specs=[pl.BlockSpec((tq, D_MODEL), lambda i: (i, 0)), pl.BlockSpec((n_mem, 2 * D_MODEL), lambda i: (0, 0))],
        out_shape=[jax.ShapeDtypeStruct((seq, D_MODEL), BF16), jax.ShapeDtypeStruct((n_mem, 2 * D_MODEL), F32)],
        compiler_params=_params(("arbitrary",), 6 * _nbytes((tq, D_MODEL), F32) + 4 * _nbytes((n_mem, 2 * D_MODEL), F32)),
    )(q, kv, do)


def _ffn_mid(hg, xg, hv, xv, wg0, wg1, wg2, bg, wv0, wv1, wv2, bv):
    return jax.nn.gelu(causal_conv(hg, xg, (wg0, wg1, wg2), bg)) * causal_conv(hv, xv, (wv0, wv1, wv2), bv)


FFN_TB = 256
FFN_W = D_FF // 2
FFN_J = D_FF // FFN_W
MXU_COLS = 256
FFN_PIECES = tuple((off, min(MXU_COLS, FFN_W - off)) for off in range(0, FFN_W, MXU_COLS))


def _ffn_common_specs(seq, row):
    tb = min(FFN_TB, seq)
    full = pl.BlockSpec((tb, D_MODEL), lambda t, j: (row(t), 0))
    vec = pl.BlockSpec((1, D_MODEL), lambda t, j: (0, 0))
    halves = []
    for off in (0, FFN_J):
        halves.append(dict(
            w_up=pl.BlockSpec((None, D_MODEL, FFN_W), lambda t, j, off=off: (j + off, 0, 0)),
            taps=pl.BlockSpec((FFN_CONV, FFN_W), lambda t, j, off=off: (0, j + off)),
            bias=pl.BlockSpec((1, FFN_W), lambda t, j, off=off: (0, j + off))))
    w_down = pl.BlockSpec((FFN_W, D_MODEL), lambda t, j: (j, 0))
    u_blk = pl.BlockSpec((2, tb, FFN_W), lambda t, j: (0, row(t), j))
    return tb, full, vec, halves, w_down, u_blk


def _ffn_vmem(tb):
    return (_nbytes((2, tb, FFN_W), F32) + _nbytes((2, tb, FFN_W), BF16) + 3 * _nbytes((D_MODEL, FFN_W), BF16)
            + 10 * _nbytes((tb, D_MODEL), F32))


def _conv_params(taps_ref, bias_ref, cols):
    return taps_ref[0:1, cols], taps_ref[1:2, cols], taps_ref[2:3, cols], bias_ref[:, cols]


def _ffn_fwd(x2b, x2, w_up, conv_w, conv_b, w_down, ln_g, ln_b, target):
    seq = x2.shape[0]
    tb, full, vec, halves, wd_spec, u_blk = _ffn_common_specs(seq, lambda t: t)
    nt = seq // tb

    def body(xb_ref, wg_ref, wv_ref, tg_ref, tv_ref, bg_ref, bv_ref, wd_ref, x_ref, g_ref, b_ref, tgt_ref,
             u_ref, h_ref, dz_ref, dg_ref, db_ref, loss_ref, dzb_ref, acc, carry):
        t, j = pl.program_id(0), pl.program_id(1)
        xb = xb_ref[...]
        pieces = [pl.ds(off, width) for off, width in FFN_PIECES]
        ug = [_dg(xb, wg_ref[:, cols], 1, 0) for cols in pieces]
        uv = [_dg(xb, wv_ref[:, cols], 1, 0) for cols in pieces]
        hs = []
        for cols, g, v in zip(pieces, ug, uv):
            u_ref[0, :, cols] = g
            u_ref[1, :, cols] = v
            halo_g = jnp.where(t == 0, 0.0, carry[j, 0, :, cols])
            halo_v = jnp.where(t == 0, 0.0, carry[j, 1, :, cols])
            h = _ffn_mid(halo_g, g, halo_v, v, *_conv_params(tg_ref, bg_ref, cols),
                         *_conv_params(tv_ref, bv_ref, cols)).astype(BF16)
            carry[j, 0, :, cols] = g[tb - SUBLANES:, :]
            carry[j, 1, :, cols] = v[tb - SUBLANES:, :]
            h_ref[:, cols] = h
            hs.append(h)
        part = None
        for cols, h in zip(pieces, hs):
            p = _dg(h, wd_ref[cols, :], 1, 0)
            part = p if part is None else part + p

        @pl.when(j == 0)
        def _():
            acc[...] = part

        @pl.when(j > 0)
        def _():
            acc[...] += part

        @pl.when(j == FFN_J - 1)
        def _():
            y, vjp = jax.vjp(_layer_norm, acc[...] + ALPHA * x_ref[...], g_ref[...], b_ref[...])
            err = y - tgt_ref[...]
            part_loss = 0.5 * jnp.sum(jnp.sum(err * err, axis=1, keepdims=True), axis=0, keepdims=True) / D_MODEL
            dz, dg, db = vjp(err / D_MODEL)

            @pl.when(t == 0)
            def _():
                for r in (dg_ref, db_ref, loss_ref):
                    r[...] = jnp.zeros_like(r)

            dz_ref[...] = dz
            dzb_ref[...] = dz.astype(BF16)
            dg_ref[...] += dg
            db_ref[...] += db
            loss_ref[...] += jnp.broadcast_to(part_loss, (1, LANES))

    h0, h1 = halves
    row = jax.ShapeDtypeStruct((1, D_MODEL), F32)
    return _pcall(
        body, name="ffn_fwd", grid=(nt, FFN_J),
        in_specs=[full, h0["w_up"], h1["w_up"], h0["taps"], h1["taps"], h0["bias"], h1["bias"], wd_spec, full, vec, vec,
                  full],
        out_specs=[u_blk, pl.BlockSpec((tb, FFN_W), lambda t, j: (t, j)), full, vec, vec,
                   pl.BlockSpec((1, LANES), lambda t, j: (0, 0)), full],
        out_shape=[jax.ShapeDtypeStruct((2, seq, D_FF), F32), jax.ShapeDtypeStruct((seq, D_FF), BF16),
                   jax.ShapeDtypeStruct((seq, D_MODEL), F32), row, row, jax.ShapeDtypeStruct((1, LANES), F32),
                   jax.ShapeDtypeStruct((seq, D_MODEL), BF16)],
        scratch_shapes=[pltpu.VMEM((tb, D_MODEL), F32), pltpu.VMEM((FFN_J, 2, SUBLANES, FFN_W), F32)],
        compiler_params=_params(("arbitrary", "arbitrary"), _ffn_vmem(tb)),
    )(x2b, w_up, w_up, conv_w, conv_w, conv_b, conv_b, w_down, x2, ln_g, ln_b, target)


def _ffn_bwd(u, conv_w, conv_b, dz3b, dz3, w_down, w_up, z2, ln_g, ln_b):
    seq = dz3.shape[0]
    tb = min(FFN_TB, seq)
    nt = seq // tb
    row8 = tb // SUBLANES
    tb, full, vec, halves, wd_spec, u_blk = _ffn_common_specs(seq, lambda t: nt - 1 - t)
    halo = pl.BlockSpec((2, SUBLANES, FFN_W), lambda t, j: (0, jnp.maximum((nt - 1 - t) * row8 - 1, 0), j))

    def body(u_ref, halo_ref, tg_ref, tv_ref, bg_ref, bv_ref, dzb_ref, wd_ref, wg_ref, wv_ref, dz3_ref, z_ref, g_ref,
             b_ref, du_ref, dw_ref, dbias_ref, dz_ref, dg_ref, db_ref, dz2b_ref, acc, carry):
        t, j = pl.program_id(0), pl.program_id(1)

        @pl.when((t == 0) & (j == 0))
        def _():
            for r in (dw_ref, dbias_ref, dg_ref, db_ref):
                r[...] = jnp.zeros_like(r)

        pieces = [pl.ds(off, width) for off, width in FFN_PIECES]
        dzb = dzb_ref[...]
        dhs = [_dg(dzb, wd_ref[cols, :], 1, 1) for cols in pieces]
        first = t == nt - 1
        dus = []
        for cols, dh in zip(pieces, dhs):
            args = (jnp.where(first, 0.0, halo_ref[0, :, cols]), u_ref[0, :, cols],
                    jnp.where(first, 0.0, halo_ref[1, :, cols]), u_ref[1, :, cols],
                    *_conv_params(tg_ref, bg_ref, cols), *_conv_params(tv_ref, bv_ref, cols))
            _, vjp = jax.vjp(_ffn_mid, *args)
            dhg, dxg, dhv, dxv, g0, g1, g2, gb, v0, v1, v2, vb = vjp(dh)
            zeros = jnp.zeros((tb - SUBLANES, dh.shape[1]), F32)
            dug = (dxg + jnp.concatenate([zeros, jnp.where(t == 0, 0.0, carry[j, 0, :, cols])], axis=0)).astype(BF16)
            duv = (dxv + jnp.concatenate([zeros, jnp.where(t == 0, 0.0, carry[j, 1, :, cols])], axis=0)).astype(BF16)
            carry[j, 0, :, cols] = dhg
            carry[j, 1, :, cols] = dhv
            du_ref[0, :, cols] = dug
            du_ref[1, :, cols] = duv
            for half, parts in enumerate(((g0, g1, g2), (v0, v1, v2))):
                for d, p in enumerate(parts):
                    dw_ref[j, half, d:d + 1, cols] += p
            dbias_ref[j, 0, :, cols] += gb
            dbias_ref[j, 1, :, cols] += vb
            dus.append((dug, duv))
        part = None
        for cols, (dug, duv) in zip(pieces, dus):
            p = _dg(dug, wg_ref[:, cols], 1, 1) + _dg(duv, wv_ref[:, cols], 1, 1)
            part = p if part is None else part + p

        @pl.when(j == 0)
        def _():
            acc[...] = part

        @pl.when(j > 0)
        def _():
            acc[...] += part

        @pl.when(j == FFN_J - 1)
        def _():
            _, ln_vjp = jax.vjp(_layer_norm, z_ref[...], g_ref[...], b_ref[...])
            dz, dg, db = ln_vjp(acc[...] + ALPHA * dz3_ref[...])
            dz_ref[...] = dz
            dz2b_ref[...] = dz.astype(BF16)
            dg_ref[...] += dg
            db_ref[...] += db

    h0, h1 = halves
    row = jax.ShapeDtypeStruct((1, D_MODEL), F32)
    whole = lambda *shape: pl.BlockSpec(shape, lambda t, j: (0,) * len(shape))
    return _pcall(
        body, name="ffn_bwd", grid=(nt, FFN_J),
        in_specs=[u_blk, halo, h0["taps"], h1["taps"], h0["bias"], h1["bias"], full, wd_spec, h0["w_up"], h1["w_up"],
                  full, full, vec, vec],
        out_specs=[u_blk, whole(FFN_J, 2, FFN_CONV, FFN_W), whole(FFN_J, 2, 1, FFN_W), full, vec, vec, full],
        out_shape=[jax.ShapeDtypeStruct((2, seq, D_FF), BF16), jax.ShapeDtypeStruct((FFN_J, 2, FFN_CONV, FFN_W), F32),
                   jax.ShapeDtypeStruct((FFN_J, 2, 1, FFN_W), F32), jax.ShapeDtypeStruct((seq, D_MODEL), F32), row, row,
                   jax.ShapeDtypeStruct((seq, D_MODEL), BF16)],
        scratch_shapes=[pltpu.VMEM((tb, D_MODEL), F32), pltpu.VMEM((FFN_J, 2, SUBLANES, FFN_W), F32)],
        compiler_params=_params(("arbitrary", "arbitrary"), _ffn_vmem(tb)),
    )(u, u, conv_w, conv_w, conv_b, conv_b, dz3b, w_down, w_up, w_up, dz3, z2, ln_g, ln_b)


def _adamw_math(w, g, m, v):
    m_new = ADAM_B1 * m + (1.0 - ADAM_B1) * g
    v_new = ADAM_B2 * v + (1.0 - ADAM_B2) * jnp.square(g)
    m_hat = m_new / (1.0 - ADAM_B1 ** ADAM_STEP)
    v_hat = v_new / (1.0 - ADAM_B2 ** ADAM_STEP)
    return -ADAM_LR * (m_hat / (jnp.sqrt(v_hat) + ADAM_EPS) + ADAM_WD * w), m_new, v_new


def _adamw_many(name, ws, gs, ms, vs):
    n = len(ws)

    def body(*refs):
        w_refs, g_refs, m_refs, v_refs = (refs[i * n:(i + 1) * n] for i in range(4))
        d_refs, nm_refs, nv_refs = (refs[(4 + i) * n:(5 + i) * n] for i in range(3))
        for i in range(n):
            d_refs[i][...], nm_refs[i][...], nv_refs[i][...] = _adamw_math(
                w_refs[i][...], g_refs[i][...], m_refs[i][...], v_refs[i][...])

    vm = pl.BlockSpec(memory_space=pltpu.VMEM)
    outs = _pcall(
        body, pin=False, name=name, in_specs=[vm] * (4 * n), out_specs=[vm] * (3 * n),
        out_shape=[jax.ShapeDtypeStruct(w.shape, F32) for w in ws] * 3,
    )(*ws, *gs, *ms, *vs)
    return outs[:n], outs[n:2 * n], outs[2 * n:]


def _adamw_halves(name, core, w, mine, theirs, m, v):
    rows, cols = w.shape
    half_rows = mine.shape[0]
    tr = _tile(half_rows, (256, 176, 128))
    nbh = half_rows // tr
    assert 2 * half_rows == rows

    def body(c_ref, w_ref, a_ref, b_ref, m_ref, v_ref, g_ref, d_ref, nm_ref, nv_ref):
        g = jnp.where(pl.program_id(0) // nbh == c_ref[0], a_ref[...], b_ref[...])
        g_ref[...] = g
        d_ref[...], nm_ref[...], nv_ref[...] = _adamw_math(w_ref[...], g, m_ref[...], v_ref[...])

    spec = pl.BlockSpec((tr, cols), lambda i, c_ref: (i, 0))
    half = pl.BlockSpec((tr, cols), lambda i, c_ref: (i % nbh, 0))
    sh = jax.ShapeDtypeStruct((rows, cols), F32)
    grid_spec = pltpu.PrefetchScalarGridSpec(
        num_scalar_prefetch=1, grid=(rows // tr,), in_specs=[spec, half, half, spec, spec], out_specs=[spec] * 4)
    return _pcall(
        body, name=name, grid_spec=grid_spec, out_shape=[sh] * 4,
        compiler_params=_params(("arbitrary",), 18 * _nbytes((tr, -(-cols // LANES) * LANES), F32)),
    )(core, w, mine, theirs, m, v)


MESH = pl.DeviceIdType.MESH
ANY = pl.BlockSpec(memory_space=pl.ANY)
N_CHIPS = 4
BF16_ROWS = 16


def _me():
    return lax.axis_index("x"), lax.axis_index("y"), lax.axis_index("c")


def _other_chips(x, y):
    return [(1 - x, y), (x, 1 - y), (1 - x, 1 - y)]


def _remote(src, dst, ssem, rsem, dev):
    return pltpu.make_async_remote_copy(src_ref=src, dst_ref=dst, send_sem=ssem, recv_sem=rsem,
                                        device_id=dev, device_id_type=MESH)


def _half_rows(ref_rows, cc):
    half = ref_rows // 2
    return pl.ds(pl.multiple_of(cc * half, BF16_ROWS), half)


def _gather_weights(shards):
    n = len(shards)
    n_ici = n * (N_CHIPS - 1)

    def body(*refs):
        ins, outs, (ssem, rsem, lsem, lrsem) = refs[:n], refs[n:2 * n], refs[2 * n:]
        x, y, c = _me()
        k_me = 2 * x + y
        sib = (x, y, 1 - c)
        chips = _other_chips(x, y)
        started = []
        for i, (w_ref, o_ref) in enumerate(zip(ins, outs)):
            cp = _remote(w_ref, o_ref.at[k_me], lsem.at[i], lrsem.at[i], sib)
            cp.start()
            started.append(cp)
        for r, (px, py) in enumerate(chips):
            for i, (w_ref, o_ref) in enumerate(zip(ins, outs)):
                rows = _half_rows(w_ref.shape[0], c)
                s = r * n + i
                cp = _remote(w_ref.at[rows], o_ref.at[k_me, rows], ssem.at[s], rsem.at[s], (px, py, c))
                cp.start()
                started.append(cp)
        for r, (px, py) in enumerate(chips):
            for i, o_ref in enumerate(outs):
                blk = o_ref.at[2 * px + py, _half_rows(o_ref.shape[1], c)]
                s = r * n + i
                _remote(blk, blk, ssem.at[s], rsem.at[s], (px, py, c)).wait_recv()
                cp = _remote(blk, blk, ssem.at[n_ici + s], rsem.at[n_ici + s], sib)
                cp.start()
                started.append(cp)
        for r, (px, py) in enumerate(chips):
            for i, o_ref in enumerate(outs):
                blk = o_ref.at[2 * px + py, _half_rows(o_ref.shape[1], 1 - c)]
                s = n_ici + r * n + i
                _remote(blk, blk, ssem.at[s], rsem.at[s], sib).wait_recv()
        for cp in started[n:]:
            cp.wait_send()
        for cp in started[:n]:
            cp.wait()

    return _pcall(
        body, name="gather_weights", in_specs=[ANY] * n, out_specs=[ANY] * n,
        out_shape=[jax.ShapeDtypeStruct((N_CHIPS,) + s.shape, s.dtype) for s in shards],
        scratch_shapes=[pltpu.SemaphoreType.DMA((2 * n_ici,)), pltpu.SemaphoreType.DMA((2 * n_ici,)),
                        pltpu.SemaphoreType.DMA((n,)), pltpu.SemaphoreType.DMA((n,))],
    )(*shards)


def _swap_halves(name, grads):
    n = len(grads)

    def body(*refs):
        ins, outs, (ssem, rsem) = refs[:n], refs[n:2 * n], refs[2 * n:]
        x, y, c = _me()
        copies = []
        for i, (g_ref, o_ref) in enumerate(zip(ins, outs)):
            for k in range(N_CHIPS):
                s = i * N_CHIPS + k
                cp = _remote(g_ref.at[k, _half_rows(g_ref.shape[1], 1 - c)], o_ref.at[k], ssem.at[s], rsem.at[s],
                             (x, y, 1 - c))
                cp.start()
                copies.append(cp)
        for cp in copies:
            cp.wait()

    return _pcall(
        body, name=name, in_specs=[ANY] * n, out_specs=[ANY] * n,
        out_shape=[jax.ShapeDtypeStruct((N_CHIPS, g.shape[1] // 2, g.shape[2]), g.dtype) for g in grads],
        scratch_shapes=[pltpu.SemaphoreType.DMA((n * N_CHIPS,)), pltpu.SemaphoreType.DMA((n * N_CHIPS,))],
    )(*grads)


SEM = pl.BlockSpec(memory_space=pltpu.SEMAPHORE)
IN_HBM = pl.BlockSpec(memory_space=pltpu.HBM)
SPLIT_PARAMS = dict(compiler_params=pltpu.CompilerParams(has_side_effects=pltpu.SideEffectType.DATAFLOW_SIDE_EFFECTING))


def _split_start(name, sources, landings, n_copies, plan):
    ns, nl = len(sources), len(landings)

    def body(*refs):
        ins, lands, (ssem, rsem), token = refs[:ns], refs[ns:ns + nl], refs[ns + nl:ns + nl + 2], refs[-1]
        for s, (src, dst, _, dev) in enumerate(plan(ins, lands)):
            _remote(src, dst, ssem.at[s], rsem.at[s], dev).start()
        token[...] = jnp.zeros_like(token)

    arrays = list(sources) + list(landings)
    outs = _call(
        body, name=name, in_specs=[IN_HBM] * (ns + nl),
        out_specs=[SEM, SEM] + [IN_HBM] * (ns + nl) + [pl.BlockSpec(memory_space=pltpu.VMEM)],
        out_shape=[pltpu.SemaphoreType.DMA((n_copies,)), pltpu.SemaphoreType.DMA((n_copies,))]
        + [pltpu.HBM(a.shape, a.dtype) for a in arrays] + [jax.ShapeDtypeStruct((SUBLANES, LANES), F32)],
        input_output_aliases={i: 2 + i for i in range(ns + nl)}, **SPLIT_PARAMS,
    )(*[pltpu.with_memory_space_constraint(a, pltpu.HBM) for a in arrays])
    return (outs[:-1], ns), outs[-1]


def _split_wait(name, handle, after, plan):
    (ssem, rsem, *thru), ns = handle
    nl = len(thru) - ns

    def body(*refs):
        ins, lands, (ssem_ref, rsem_ref) = refs[:ns], refs[ns:ns + nl], refs[ns + nl:ns + nl + 2]
        for s, (src, _, dst, dev) in enumerate(plan(ins, lands)):
            cp = _remote(src, dst, ssem_ref.at[s], rsem_ref.at[s], dev)
            cp.wait_send()
            cp.wait_recv()

    outs = _call(
        body, name=name, in_specs=[IN_HBM] * (ns + nl) + [SEM, SEM, ANY], out_specs=[IN_HBM] * (ns + nl),
        out_shape=[pltpu.HBM(t.shape, t.dtype) for t in thru],
        input_output_aliases={i: i for i in range(ns + nl)}, **SPLIT_PARAMS,
    )(*thru, ssem, rsem, after)
    return outs[:ns], outs[ns:]


def _swap_plan(ins, lands):
    x, y, c = _me()
    return [(g_ref.at[k, _half_rows(g_ref.shape[1], 1 - c)], l_ref.at[k], l_ref.at[k], (x, y, 1 - c))
            for g_ref, l_ref in zip(ins, lands) for k in range(N_CHIPS)]


def _swap_start(name, grads):
    lands = [lax.empty((N_CHIPS, g.shape[1] // 2, g.shape[2]), g.dtype) for g in grads]
    return _split_start(name, grads, lands, len(grads) * N_CHIPS, _swap_plan)


def _swap_wait(name, handle, after):
    return _split_wait(name, handle, after, _swap_plan)


def _gather_plan(ins, lands):
    x, y, c = _me()
    k_me = 2 * x + y
    plan = [(w_ref, l_ref.at[k_me], l_ref.at[k_me], (x, y, 1 - c)) for w_ref, l_ref in zip(ins, lands)]
    for px, py in _other_chips(x, y):
        for w_ref, l_ref in zip(ins, lands):
            rows = _half_rows(w_ref.shape[0], c)
            plan.append((w_ref.at[rows], l_ref.at[k_me, rows], l_ref.at[2 * px + py, rows], (px, py, c)))
    return plan


def _gather_start(name, shards):
    lands = [lax.empty((N_CHIPS,) + s.shape, s.dtype) for s in shards]
    return _split_start(name, shards, lands, len(shards) * N_CHIPS, _gather_plan)


def _gather_wait(name, handle, after):
    return _split_wait(name, handle, after, _gather_plan)[1]


def _forward_halves(name, blocks):
    n = len(blocks)
    n_sem = n * (N_CHIPS - 1)

    def body(*refs):
        outs, (ssem, rsem) = refs[n:2 * n], refs[2 * n:]
        x, y, c = _me()
        sib = (x, y, 1 - c)
        chips = _other_chips(x, y)
        sends = []
        for r, (px, py) in enumerate(chips):
            for i, o_ref in enumerate(outs):
                blk = o_ref.at[2 * px + py, _half_rows(o_ref.shape[1], c)]
                cp = _remote(blk, blk, ssem.at[r * n + i], rsem.at[r * n + i], sib)
                cp.start()
                sends.append(cp)
        for r, (px, py) in enumerate(chips):
            for i, o_ref in enumerate(outs):
                blk = o_ref.at[2 * px + py, _half_rows(o_ref.shape[1], 1 - c)]
                _remote(blk, blk, ssem.at[r * n + i], rsem.at[r * n + i], sib).wait_recv()
        for cp in sends:
            cp.wait_send()

    return _pcall(
        body, name=name, in_specs=[ANY] * n, out_specs=[ANY] * n,
        out_shape=[jax.ShapeDtypeStruct(b.shape, b.dtype) for b in blocks],
        input_output_aliases={i: i for i in range(n)},
        scratch_shapes=[pltpu.SemaphoreType.DMA((n_sem,)), pltpu.SemaphoreType.DMA((n_sem,))],
    )(*blocks)


def _scatter_plan(ins, lands):
    x, y, c = _me()
    k_me = 2 * x + y
    return [(p_ref.at[2 * px + py], l_ref.at[k_me], l_ref.at[2 * px + py], (px, py, c))
            for px, py in _other_chips(x, y) for p_ref, l_ref in zip(ins, lands)]


def _scatter_start(name, parts):
    lands = [lax.empty(p.shape, p.dtype) for p in parts]
    return _split_start(name, parts, lands, len(parts) * (N_CHIPS - 1), _scatter_plan)


def _scatter_wait(name, handle, after):
    return _split_wait(name, handle, after, _scatter_plan)[1]


def _share_and_reduce(halves, v):
    n = len(halves)
    rows = v.shape[0]
    half = rows // 2
    assert half % SUBLANES == 0

    def body(*refs):
        ins, v_ref, outs, out_ref = refs[:n], refs[n], refs[n + 1:2 * n + 1], refs[2 * n + 1]
        pair_buf, mine, chip_buf, ssem, rsem, half_ssem, half_rsem = refs[2 * n + 2:]
        x, y, c = _me()
        k_me = 2 * x + y
        sib = (x, y, 1 - c)
        copies = [_remote(r_ref, o_ref, half_ssem.at[i], half_rsem.at[i], sib)
                  for i, (r_ref, o_ref) in enumerate(zip(ins, outs))]
        for cp in copies:
            cp.start()

        def rows_of(cc):
            return pl.ds(pl.multiple_of(cc * half, SUBLANES), half)

        swap = _remote(v_ref.at[rows_of(1 - c)], pair_buf, ssem.at[0], rsem.at[0], sib)
        swap.start()
        swap.wait()
        mine[...] = v_ref[rows_of(c), :] + pair_buf[...]
        chip_buf[k_me] = mine[...]
        sends = [_remote(mine, chip_buf.at[k_me], ssem.at[1 + r], rsem.at[1 + r], (px, py, c))
                 for r, (px, py) in enumerate(_other_chips(x, y))]
        for cp in sends:
            cp.start()
        for r, (px, py) in enumerate(_other_chips(x, y)):
            blk = chip_buf.at[2 * px + py]
            _remote(blk, blk, ssem.at[1 + r], rsem.at[1 + r], (px, py, c)).wait_recv()
        total = chip_buf[0]
        for k in range(1, N_CHIPS):
            total = total + chip_buf[k]
        out_ref[rows_of(c), :] = total
        for cp in sends:
            cp.wait_send()
        share = _remote(out_ref.at[rows_of(c)], out_ref.at[rows_of(c)], ssem.at[N_CHIPS], rsem.at[N_CHIPS], sib)
        share.start()
        got = out_ref.at[rows_of(1 - c)]
        _remote(got, got, ssem.at[N_CHIPS], rsem.at[N_CHIPS], sib).wait_recv()
        share.wait_send()
        for cp in copies:
            cp.wait()

    vm = pl.BlockSpec(memory_space=pltpu.VMEM)
    outs = _call(
        body, name="share_and_reduce", in_specs=[ANY] * n + [vm], out_specs=[ANY] * n + [vm],
        out_shape=[pltpu.HBM(h.shape, h.dtype) for h in halves] + [jax.ShapeDtypeStruct((rows, LANES), F32)],
        scratch_shapes=[pltpu.VMEM((half, LANES), F32), pltpu.VMEM((half, LANES), F32),
                        pltpu.VMEM((N_CHIPS, half, LANES), F32), pltpu.SemaphoreType.DMA((N_CHIPS + 1,)),
                        pltpu.SemaphoreType.DMA((N_CHIPS + 1,)), pltpu.SemaphoreType.DMA((n,)),
                        pltpu.SemaphoreType.DMA((n,))],
        compiler_params=pltpu.CompilerParams(vmem_limit_bytes=32 * 1024 * 1024),
    )(*[pltpu.with_memory_space_constraint(h, pltpu.HBM) for h in halves], v)
    return outs[:n], outs[n]


def _add_pair(name, core, chip, g, theirs):
    _, half, cols = theirs.shape
    tr = _tile(half, (256, 176, 128))
    nb = half // tr

    def body(c_ref, k_ref, g_ref, t_ref, o32_ref, o16_ref):
        s = g_ref[...] + t_ref[...]
        o16_ref[...] = s.astype(BF16)

        @pl.when(pl.program_id(1) == k_ref[0])
        def _():
            o32_ref[...] = s

    spec = pl.BlockSpec((None, tr, cols), lambda i, k, c_ref, k_ref: (k, i, 0))
    grid_spec = pltpu.PrefetchScalarGridSpec(
        num_scalar_prefetch=2, grid=(nb, N_CHIPS),
        in_specs=[pl.BlockSpec((None, tr, cols), lambda i, k, c_ref, k_ref: (k, c_ref[0] * nb + i, 0)), spec],
        out_specs=[pl.BlockSpec((tr, cols), lambda i, k, c_ref, k_ref: (i, 0)), spec])
    return _pcall(
        body, name=name, grid_spec=grid_spec,
        out_shape=[jax.ShapeDtypeStruct((half, cols), F32), jax.ShapeDtypeStruct(theirs.shape, BF16)],
        compiler_params=_params(("arbitrary", "arbitrary"), 8 * _nbytes((tr, cols + LANES), F32)),
    )(core, chip, g, theirs)


def _add_chips(name, chip, p32, recv):
    half, cols = p32.shape
    tr = _tile(half, (256, 176, 128))

    def body(k_ref, p_ref, r0_ref, r1_ref, r2_ref, o_ref):
        o_ref[...] = ((p_ref[...] + r0_ref[...].astype(F32)) + r1_ref[...].astype(F32)) + r2_ref[...].astype(F32)

    def other(r):
        return pl.BlockSpec((None, tr, cols), lambda i, k_ref: (r + (k_ref[0] <= r).astype(jnp.int32), i, 0))
    grid_spec = pltpu.PrefetchScalarGridSpec(
        num_scalar_prefetch=1, grid=(half // tr,),
        in_specs=[pl.BlockSpec((tr, cols), lambda i, k_ref: (i, 0)), other(0), other(1), other(2)],
        out_specs=pl.BlockSpec((tr, cols), lambda i, k_ref: (i, 0)))
    return _pcall(
        body, name=name, grid_spec=grid_spec, out_shape=jax.ShapeDtypeStruct((half, cols), F32),
        compiler_params=_params(("arbitrary",), 10 * _nbytes((tr, cols + LANES), F32)),
    )(chip, p32, recv, recv, recv)


def kernel(x, mem, w_in, b_in, hg_lb_logits, hg_norm_w, ml_conv_w, ml_conv_b, ml_norm_w, w_out, ln1_g, ln1_b, ca_wq, ca_wkv, ca_wo, ln2_g, ln2_b, ffn_w_up, ffn_conv_w, ffn_conv_b, ffn_w_down, ln3_g, ln3_b, loss_target, m_w_in, m_b_in, m_hg_lb_logits, m_hg_norm_w, m_ml_conv_w, m_ml_conv_b, m_ml_norm_w, m_w_out, m_ln1_g, m_ln1_b, m_ca_wq, m_ca_wkv, m_ca_wo, m_ln2_g, m_ln2_b, m_ffn_w_up, m_ffn_conv_w, m_ffn_conv_b, m_ffn_w_down, m_ln3_g, m_ln3_b, v_w_in, v_b_in, v_hg_lb_logits, v_hg_norm_w, v_ml_conv_w, v_ml_conv_b, v_ml_norm_w, v_w_out, v_ln1_g, v_ln1_b, v_ca_wq, v_ca_wkv, v_ca_wo, v_ln2_g, v_ln2_b, v_ffn_w_up, v_ffn_conv_w, v_ffn_conv_b, v_ffn_w_down, v_ln3_g, v_ln3_b):
    return _train_step(dict(locals()))


WEIGHTS = ("w_in", "b_in", "hg_lb_logits", "hg_norm_w", "ml_conv_w", "ml_conv_b", "ml_norm_w", "w_out", "ln1_g",
           "ln1_b", "ca_wq", "ca_wkv", "ca_wo", "ln2_g", "ln2_b", "ffn_w_up", "ffn_conv_w", "ffn_conv_b",
           "ffn_w_down", "ln3_g", "ln3_b")
MATRICES = ("w_in", "w_out", "ca_wq", "ca_wkv", "ca_wo", "ffn_w_up", "ffn_w_down")
COL_SHARDED = ("w_in", "ca_wkv", "ffn_w_up", "ml_conv_w", "ffn_conv_w")
SMALL = tuple(n for n in WEIGHTS if n not in MATRICES)
PART_ROWS = 16


def _part_rows(shape):
    n = 1
    for s in shape:
        n *= s
    return -(-n // (LANES * PART_ROWS)) * PART_ROWS


def _pack(arrs, dtype):
    parts = []
    for a in arrs:
        flat = a.reshape(-1).astype(dtype)
        flat = jnp.pad(flat, (0, _part_rows(a.shape) * LANES - flat.shape[0]))
        parts.append(flat.reshape(-1, LANES))
    return jnp.concatenate(parts, axis=0)


def _unpack(buf, shapes):
    lead = buf.shape[:-2]
    outs, r = [], 0
    for sh in shapes:
        n = 1
        for s in sh:
            n *= s
        nr = _part_rows(sh)
        flat = buf[..., r:r + nr, :].reshape(lead + (nr * LANES,))
        outs.append(flat[..., :n].reshape(lead + tuple(sh)))
        r += nr
    return outs


def _cat_cols(s):
    return jnp.moveaxis(s, 0, 1).reshape(s.shape[1], -1)


def _stack_rows(s):
    return s.reshape(-1, s.shape[-1])


def _train_step(a):
    xs, mems, tgt = a["x"][0], a["mem"][0], a["loss_target"][0]
    core = lax.axis_index("c").astype(jnp.int32).reshape(1)
    chip = (2 * lax.axis_index("x") + lax.axis_index("y")).astype(jnp.int32).reshape(1)
    k_me = chip[0]
    shard = {n: a[n][0] for n in MATRICES}

    later = [n for n in MATRICES if n != "w_in"]
    w_in, taps = _gather_weights([shard["w_in"].astype(BF16), _pack([a["ml_conv_w"][0], a["ffn_conv_w"][0]], F32)])
    w = {"w_in": jnp.concatenate([*w_in, jnp.zeros((D_MODEL, D_IN_PAD - D_IN), BF16)], axis=1)}
    gathering, token = _gather_start("gather_start", [shard[n].astype(BF16) for n in later])
    ml_cw, ffn_cw = [_cat_cols(s) for s in _unpack(taps, [a["ml_conv_w"].shape[1:], a["ffn_conv_w"].shape[1:]])]
    b_in_p = jnp.pad(a["b_in"], ((0, 0), (0, D_IN_PAD - D_IN))) + token[0:1, 0:1]
    mixer_w = (a["hg_lb_logits"], a["hg_norm_w"], ml_cw, a["ml_conv_b"], a["ml_norm_w"])
    up_cols = a["ffn_w_up"].shape[-1]

    proj, xb = _mm("proj", "nn", xs, w["w_in"], bias=b_in_p, a_copy_dtype=BF16, tm=256, tn=D_IN_PAD)
    y, hst, cst, nst, mst = _mixer_fwd(proj, *mixer_w)
    w.update(zip(later, _forward_halves("forward_halves", _gather_wait("gather_wait", gathering, y))))
    for n in ("w_out", "ca_wq", "ca_wo", "ffn_w_down"):
        w[n] = _stack_rows(w[n])
    z1, x1, x1b = _mm("mix_out", "nn", y, w["w_out"], res=xs, res_scale=ALPHA, ln=("fwd", a["ln1_g"], a["ln1_b"]),
                      copy_dtype=BF16)
    q = _mm("ca_q", "nn", x1b, w["ca_wq"], out_dtype=BF16, tn=D_MODEL)
    kv = _mm("ca_kv", "nn", mems, w["ca_wkv"])
    o = _attn_fwd(q, kv)
    z2, x2, x2b = _mm("ca_out", "nn", o, w["ca_wo"], res=x1, res_scale=ALPHA, ln=("fwd", a["ln2_g"], a["ln2_b"]),
                      copy_dtype=BF16)
    w_up = w["ffn_w_up"]
    assert w_up.shape == (2 * FFN_J, D_MODEL, FFN_W)
    u, hmid, dz3, g_ln3g, g_ln3b, loss_part, dz3b = _ffn_fwd(
        x2b, x2, w_up, ffn_cw, a["ffn_conv_b"], w["ffn_w_down"], a["ln3_g"], a["ln3_b"], tgt)

    grads = {"ln3_g": g_ln3g, "ln3_b": g_ln3b}
    grads["ffn_w_down"] = _mm("g_w_down", "tn", hmid, dz3b, tm=D_FF // 2, tn=D_MODEL)
    du, g_cw, g_cb, dz2, grads["ln2_g"], grads["ln2_b"], dz2b = _ffn_bwd(
        u, ffn_cw, a["ffn_conv_b"], dz3b, dz3, w["ffn_w_down"], w_up, z2, a["ln2_g"], a["ln2_b"])
    grads["ffn_conv_w"] = jnp.transpose(g_cw, (2, 1, 0, 3)).reshape(FFN_CONV, 2 * D_FF)
    grads["ffn_conv_b"] = jnp.transpose(g_cb, (2, 1, 0, 3)).reshape(1, 2 * D_FF)
    grads["ffn_w_up"] = _mm("g_w_up", "tn", x2b, du, out_groups=N_CHIPS, tm=D_MODEL, tn=up_cols)
    grads["ffn_w_down"] = grads["ffn_w_down"].reshape((N_CHIPS,) + shard["ffn_w_down"].shape)
    pending = {}

    def reduce_start(tag, names, swapped=None):
        group = [grads[n] for n in names]
        group, theirs = swapped or (group, _swap_halves("swap_halves_" + tag, group))
        sums = [_add_pair("add_pair_" + n, core, chip, g, t) for n, g, t in zip(names, group, theirs)]
        handle, token = _scatter_start("scatter_start_" + tag, [s16 for _, s16 in sums])
        pending[tag] = (names, [s32 for s32, _ in sums], handle)
        return token[0:1, 0:1]

    ffn = ("ffn_w_up", "ffn_w_down")
    swapping, token = _swap_start("swap_start_ffn", [grads[n] for n in ffn])
    do = _mm("d_o", "nt", dz2b, w["ca_wo"], bias=jnp.zeros((1, D_MODEL), F32) + token[0:1, 0:1], out_dtype=BF16,
             tn=D_MODEL)
    grads["ca_wo"] = _mm("g_wo", "tn", o, dz2b, tm=D_MODEL // 2, tn=D_MODEL)
    zero = reduce_start("ffn", ffn, _swap_wait("swap_wait_ffn", swapping, grads["ca_wo"]))
    dq, dkv = _attn_bwd(q, kv + zero, do)
    grads["ca_wq"] = _mm("g_wq", "tn", x1b, dq, tm=D_MODEL // 2, tn=D_MODEL)
    grads["ca_wkv"] = _mm("g_wkv", "tn", mems, dkv, out_groups=N_CHIPS, tm=D_MODEL)
    dz1, grads["ln1_g"], grads["ln1_b"], dz1b = _mm("d_x1", "nt", dq, w["ca_wq"], res=dz2, res_scale=ALPHA,
                                                    ln=("bwd", z1, a["ln1_g"], a["ln1_b"]), copy_dtype=BF16)
    grads["w_out"] = _mm("g_w_out", "tn", y, dz1b, tm=D_MODEL // 2, tn=D_MODEL)
    for n in ("w_out", "ca_wq", "ca_wo"):
        grads[n] = grads[n].reshape((N_CHIPS,) + shard[n].shape)
    attn = ("w_out", "ca_wq", "ca_wkv", "ca_wo")
    swapping, token = _swap_start("swap_start_attn", [grads[n] for n in attn])
    dy = _mm("d_y", "nt", dz1b, w["w_out"], bias=jnp.zeros((1, D_MODEL), F32) + token[0:1, 0:1], tn=D_MODEL)
    zero = reduce_start("attn", attn, _swap_wait("swap_wait_attn", swapping, dy))
    (dproj, g_b_in, grads["hg_lb_logits"], grads["hg_norm_w"], grads["ml_conv_w"], grads["ml_conv_b"],
     grads["ml_norm_w"]) = _mixer_bwd(proj, dy, hst, cst, nst, mst, mixer_w[0], mixer_w[1] + zero, *mixer_w[2:])
    g_in = _mm("g_w_in", "tn", xb, dproj, tm=D_MODEL, tn=up_cols)
    in_cols = D_IN // N_CHIPS
    grads["w_in"] = jnp.stack([g_in[:, k * in_cols:(k + 1) * in_cols] for k in range(N_CHIPS)])
    grads["b_in"] = g_b_in[:, :D_IN]
    zero = reduce_start("in", ("w_in",))
    dx = _mm("d_x", "nt", dproj, w["w_in"], bias=jnp.zeros((1, D_MODEL), F32) + zero, res=dz1, res_scale=ALPHA,
             tm=256, tn=D_MODEL)

    halves = {}
    for tag, (names, sums32, handle) in pending.items():
        for n, s32, r in zip(names, sums32, _scatter_wait("scatter_wait_" + tag, handle, dx)):
            halves[n] = _add_chips("add_chips_" + n, chip, s32, r)
    halves = [halves[n] for n in MATRICES]

    small_shapes = [grads[n].shape for n in SMALL] + [loss_part.shape]
    other_halves, summed = _share_and_reduce(halves, _pack([grads[n] for n in SMALL] + [loss_part], F32))
    summed = _unpack(summed, small_shapes)
    loss = summed[-1][0, 0]
    for n, g in zip(SMALL, summed[:-1]):
        if n in COL_SHARDED:
            cols = a[n].shape[-1]
            g = lax.dynamic_slice_in_dim(g, k_me * cols, cols, axis=1)
        grads[n] = g

    delta, new_m, new_v = {}, {}, {}
    for n, mine, theirs in zip(MATRICES, halves, other_halves):
        grads[n], delta[n], new_m[n], new_v[n] = _adamw_halves(
            "adamw_" + n, core, shard[n], mine, theirs, a["m_" + n][0], a["v_" + n][0])
    small_w = [a[n][0] if a[n].ndim == 3 else a[n] for n in SMALL]
    small_m = [a["m_" + n][0] if a[n].ndim == 3 else a["m_" + n] for n in SMALL]
    small_v = [a["v_" + n][0] if a[n].ndim == 3 else a["v_" + n] for n in SMALL]
    for out, vals in zip((delta, new_m, new_v),
                         _adamw_many("adamw_small", small_w, [grads[n] for n in SMALL], small_m, small_v)):
        out.update(zip(SMALL, vals))

    def shaped(d):
        return [d[n].reshape(a[n].shape) for n in WEIGHTS]
    return (loss, dx[None], *shaped(grads), *shaped(delta), *shaped(new_m), *shaped(new_v))
```

```python
import functools

import jax
import jax.numpy as jnp
from jax import lax
from jax.experimental import pallas as pl
from jax.experimental.pallas import tpu as pltpu

F32 = jnp.float32
BF16 = jnp.bfloat16

D_MODEL = 1024
HEADS = 4
DK = 128
D_GRP = HEADS * DK
CHUNK = 64
ML_CONV = 4
FFN_CONV = 3
D_FF = 2816
CA_DH = D_MODEL // HEADS
DEPTH = 1
ALPHA = (2.0 * DEPTH) ** 0.25
LN_EPS = 1e-5
NEG_BIG = -1e30
D_IN = 8 * D_GRP + 2 * HEADS
D_IN_PAD = 8 * D_GRP + 128
ADAM_LR, ADAM_B1, ADAM_B2, ADAM_EPS, ADAM_WD, ADAM_STEP = 0.001, 0.9, 0.999, 1e-08, 0.01, 10

SUBLANES = 8
LANES = 128
VMEM_BYTES = 64 * 1024 * 1024


def _pcall(body, pin=True, **kw):
    if not pin:
        return _call(body, **kw)
    kw["out_shape"] = jax.tree.map(lambda s: pltpu.HBM(s.shape, s.dtype), kw["out_shape"])
    call = _call(body, **kw)

    def pinned(*args):
        return call(*[pltpu.with_memory_space_constraint(x, pltpu.HBM) if jnp.issubdtype(x.dtype, jnp.floating) else x
                      for x in args])
    return pinned


def _call(body, **kw):
    return pl.pallas_call(body, **kw)


def _params(semantics, vmem_bytes):
    limit = int(min(max(2 * vmem_bytes, 16 * 1024 * 1024), VMEM_BYTES - 8 * 1024 * 1024))
    return pltpu.CompilerParams(dimension_semantics=semantics, vmem_limit_bytes=limit)


def _nbytes(shape, dtype):
    n = 1
    for s in shape:
        n *= s
    return n * jnp.dtype(dtype).itemsize


def _dg(a, b, ca, cb):
    return lax.dot_general(a.astype(BF16), b.astype(BF16), (((ca,), (cb,)), ((), ())),
                           preferred_element_type=F32)


@jax.custom_vjp
def mm_nn(a, b):
    return _dg(a, b, 1, 0)


mm_nn.defvjp(lambda a, b: (_dg(a, b, 1, 0), (a, b)),
             lambda r, g: (_dg(g, r[1], 1, 1).astype(r[0].dtype), _dg(r[0], g, 0, 0).astype(r[1].dtype)))


@jax.custom_vjp
def mm_nt(a, b):
    return _dg(a, b, 1, 1)


mm_nt.defvjp(lambda a, b: (_dg(a, b, 1, 1), (a, b)),
             lambda r, g: (_dg(g, r[1], 1, 0).astype(r[0].dtype), _dg(g, r[0], 0, 0).astype(r[1].dtype)))


@jax.custom_vjp
def mm_tn(a, b):
    return _dg(a, b, 0, 0)


mm_tn.defvjp(lambda a, b: (_dg(a, b, 0, 0), (a, b)),
             lambda r, g: (_dg(r[1], g, 1, 1).astype(r[0].dtype), _dg(r[0], g, 1, 0).astype(r[1].dtype)))


def _tri(n, lower):
    r = lax.broadcasted_iota(jnp.int32, (n, n), 0)
    c = lax.broadcasted_iota(jnp.int32, (n, n), 1)
    return ((r >= c) if lower else (r <= c)).astype(F32)


def _tri_dot(lower, x):
    t = _tri(x.shape[0], lower).astype(BF16)
    hi = x.astype(BF16)
    rest = x - hi.astype(F32)
    mid = rest.astype(BF16)
    lo = (rest - mid.astype(F32)).astype(BF16)
    return sum(lax.dot_general(t, p, (((1,), (0,)), ((), ())), preferred_element_type=F32) for p in (hi, mid, lo))


@jax.custom_vjp
def cumsum_rows(x):
    return _tri_dot(True, x)


cumsum_rows.defvjp(lambda x: (_tri_dot(True, x), None), lambda _, g: (_tri_dot(False, g),))


def _shift_impl(halo, x, d):
    xx = jnp.concatenate([halo, x], axis=0)
    return pltpu.roll(xx, d, 0)[SUBLANES:]


@functools.partial(jax.custom_vjp, nondiff_argnums=(2,))
def shift_rows(halo, x, d):
    return _shift_impl(halo, x, d)


def _shift_bwd(d, _, g):
    n = g.shape[0] + SUBLANES
    gg = jnp.concatenate([jnp.zeros((SUBLANES, g.shape[1]), g.dtype), g], axis=0)
    r = pltpu.roll(gg, n - d, 0)
    return r[:SUBLANES], r[SUBLANES:]


shift_rows.defvjp(lambda halo, x, d: (_shift_impl(halo, x, d), None), _shift_bwd)


def causal_conv(halo, x, w_rows, b):
    k = len(w_rows)
    y = b + w_rows[k - 1] * x
    for d in range(1, k):
        y = y + w_rows[k - 1 - d] * shift_rows(halo, x, d)
    return y


def _sigmoid(x):
    return 1.0 / (1.0 + jnp.exp(-x))


def _silu(x):
    return x * _sigmoid(x)


def _log_sigmoid(x):
    return jnp.minimum(x, 0.0) - jnp.log(1.0 + jnp.exp(-jnp.abs(x)))


def _pick_row(x, i):
    row = lax.broadcasted_iota(jnp.int32, (x.shape[0], 1), 0)
    return jnp.sum(jnp.where(row == i, x, 0.0), axis=0, keepdims=True)


def _layer_norm(z, g, b):
    mu = jnp.mean(z, axis=-1, keepdims=True)
    zc = z - mu
    var = jnp.mean(zc * zc, axis=-1, keepdims=True)
    return zc * lax.rsqrt(var + LN_EPS) * g + b


def _qk_conv(halo, x, w0, w1, w2, w3, b):
    return _silu(causal_conv(halo, x, (w0, w1, w2, w3), b))


def _grp(i):
    return pl.ds(i * D_GRP, D_GRP)


def _mixer_specs(n_chunks, reverse):
    def chunk(c):
        return n_chunks - 1 - c if reverse else c
    row8 = CHUNK // SUBLANES
    proj_spec = pl.BlockSpec((CHUNK, D_IN_PAD), lambda c: (chunk(c), 0))
    halo_spec = pl.BlockSpec((SUBLANES, 2 * D_GRP), lambda c: (jnp.maximum(chunk(c) * row8 - 1, 0), 2))
    small = [pl.BlockSpec((2, D_GRP), lambda c: (0, 0)), pl.BlockSpec((1, D_GRP), lambda c: (0, 0)),
             pl.BlockSpec((ML_CONV, 2 * D_GRP), lambda c: (0, 0)), pl.BlockSpec((1, 2 * D_GRP), lambda c: (0, 0)),
             pl.BlockSpec((1, D_GRP), lambda c: (0, 0))]
    state_specs = [pl.BlockSpec((1, HEADS, DK, DK), lambda c: (chunk(c), 0, 0, 0)),
                   pl.BlockSpec((1, HEADS, DK, DK), lambda c: (chunk(c), 0, 0, 0)),
                   pl.BlockSpec((1, HEADS, 1, DK), lambda c: (chunk(c), 0, 0, 0)),
                   pl.BlockSpec((1, HEADS, 1, DK), lambda c: (chunk(c), 0, 0, 0))]
    y_spec = pl.BlockSpec((CHUNK, 2 * D_GRP), lambda c: (chunk(c), 0))
    return proj_spec, halo_spec, small, state_specs, y_spec, chunk


def _heads(x):
    return [x[:, h * DK:(h + 1) * DK] for h in range(HEADS)]


def _last(x, j):
    lane = lax.broadcasted_iota(jnp.int32, (1, x.shape[-1]), 1)
    return jnp.sum(jnp.where(lane == j, x, 0.0), axis=-1, keepdims=True)


def _hg_chunk(st_t, hq, hf, hi, hgate, l0, l1, nw):
    n = hq.shape[0]
    lb = _sigmoid(l0 - l1)
    q = _silu(hq)
    lf = jnp.log(lb + (1.0 - lb) * _sigmoid(hf))
    k = (1.0 - lb) * _sigmoid(-hf)
    b = cumsum_rows(lf)
    b_ref = _pick_row(b, n // 2 - 1)
    b_last = _pick_row(b, n - 1)
    qa, ka =_heads(q * jnp.exp(b - b_ref)), _heads(k * jnp.exp(b_ref - b))
    qe, kd, eb, v = _heads(q * jnp.exp(b)), _heads(k * jnp.exp(b_last - b)), _heads(jnp.exp(b_last)), _heads(hi)
    tri = _tri(n, True) > 0
    attn = [jnp.where(tri, mm_nt(qa[h], ka[h]), 0.0) for h in range(HEADS)]
    o = [mm_nn(attn[h], v[h]) + mm_nt(qe[h], st_t[h]) for h in range(HEADS)]
    st_new = jnp.stack([eb[h] * st_t[h] + mm_tn(v[h], kd[h]) for h in range(HEADS)])
    yn = [o[h] * lax.rsqrt(jnp.mean(o[h] * o[h], axis=-1, keepdims=True) + LN_EPS) for h in range(HEADS)]
    return st_new, jnp.concatenate(yn, axis=1) * nw * _silu(hgate)


def _ml_chunk(c_st, n_st, m_st, q, k, v, gates, og, nw):
    n = q.shape[0]
    ig = jnp.stack([_last(gates, h) for h in range(HEADS)])
    log_f = _log_sigmoid(gates)
    fl = jnp.stack([_last(log_f, HEADS + h) for h in range(HEADS)])
    bw = cumsum_rows(jnp.concatenate([jnp.broadcast_to(fl[h], (n, DK)) for h in range(HEADS)], axis=1))
    b = jnp.stack([_last(x, 0) for x in _heads(bw)])
    g = jnp.sum(fl, axis=1, keepdims=True)
    eye = lax.broadcasted_iota(jnp.int32, (n, n), 0) == lax.broadcasted_iota(jnp.int32, (n, n), 1)
    e_row = jnp.sum(jnp.where(eye, ig - b, 0.0), axis=1, keepdims=True)
    d = jnp.where(_tri(n, True) > 0, b + e_row, -jnp.inf)
    inter = b + m_st
    m_t = jnp.maximum(inter, jnp.max(d, axis=2, keepdims=True))
    qs, kh, vh = _heads(q * (DK ** -0.5)), _heads(k), _heads(v)
    s = jnp.stack([mm_nt(qs[h], kh[h]) for h in range(HEADS)]) * jnp.exp(d - m_t)
    w_inter = jnp.exp(inter - m_t)
    num = (jnp.stack([mm_nn(s[h], vh[h]) for h in range(HEADS)])
           + w_inter * jnp.stack([mm_nn(qs[h], c_st[h]) for h in range(HEADS)]))
    den = jnp.sum(s, axis=2, keepdims=True) + w_inter * jnp.sum(jnp.stack(qs) * n_st, axis=2, keepdims=True)
    h_out = num / jnp.maximum(jnp.abs(den), jnp.exp(-m_t))
    a = g - b + ig
    m_new = jnp.maximum(g + m_st, jnp.max(a, axis=1, keepdims=True))
    decay = jnp.exp(g + m_st - m_new)
    wk = jnp.stack(kh) * jnp.exp(a - m_new)
    c_new = decay * c_st + jnp.stack([mm_tn(wk[h], vh[h]) for h in range(HEADS)])
    n_new = decay * n_st + jnp.sum(wk, axis=1, keepdims=True)
    hc = h_out - jnp.mean(h_out, axis=-1, keepdims=True)
    yn = hc * lax.rsqrt(jnp.mean(hc * hc, axis=-1, keepdims=True) + LN_EPS)
    y = _sigmoid(og) * (jnp.concatenate([yn[h] for h in range(HEADS)], axis=1) * nw)
    return c_new, n_new, m_new, y


def _mixer_inputs(proj_ref, lg_ref, hnw_ref, mnw_ref, qk):
    hg_in = (proj_ref[:, _grp(0)], proj_ref[:, _grp(1)], proj_ref[:, _grp(2)], proj_ref[:, _grp(3)],
             lg_ref[0:1, :], lg_ref[1:2, :], hnw_ref[...])
    ml_in = (qk[:, :D_GRP], qk[:, D_GRP:], proj_ref[:, _grp(6)], proj_ref[:, pl.ds(8 * D_GRP, LANES)],
             proj_ref[:, _grp(7)], mnw_ref[...])
    return hg_in, ml_in


def _mixer_fwd(proj, lb_logits, hg_nw, conv_w, conv_b, ml_nw):
    seq = proj.shape[0]
    n_chunks = seq // CHUNK
    proj_spec, halo_spec, small, state_specs, y_spec, _ = _mixer_specs(n_chunks, False)

    def body(proj_ref, halo_ref, lg_ref, hnw_ref, cw_ref, cb_ref, mnw_ref,
             y_ref, hst_ref, cst_ref, nst_ref, mst_ref, hs, cs, ns, ms):
        c = pl.program_id(0)

        @pl.when(c == 0)
        def _():
            hs[...] = jnp.zeros_like(hs)
            cs[...] = jnp.zeros_like(cs)
            ns[...] = jnp.zeros_like(ns)
            ms[...] = jnp.full(ms.shape, NEG_BIG, F32)

        hst_ref[0] = hs[...]
        cst_ref[0] = cs[...]
        nst_ref[0] = ns[...]
        mst_ref[0] = ms[...]
        halo = jnp.where(c > 0, halo_ref[...], 0.0)
        qk = _qk_conv(halo, proj_ref[:, pl.ds(4 * D_GRP, 2 * D_GRP)],
                      cw_ref[0:1, :], cw_ref[1:2, :], cw_ref[2:3, :], cw_ref[3:4, :], cb_ref[...])
        hg_in, ml_in = _mixer_inputs(proj_ref, lg_ref, hnw_ref, mnw_ref, qk)
        hs[...], y_hg = _hg_chunk(hs[...], *hg_in)
        cs[...], ns[...], m_new, y_ml = _ml_chunk(cs[...], ns[...], _last(ms[...], 0), *ml_in)
        ms[...] = jnp.broadcast_to(m_new, ms.shape)
        y_ref[:, pl.ds(0, D_GRP)] = y_hg.astype(BF16)
        y_ref[:, pl.ds(D_GRP, D_GRP)] = y_ml.astype(BF16)

    st = jax.ShapeDtypeStruct((n_chunks, HEADS, DK, DK), F32)
    vec = jax.ShapeDtypeStruct((n_chunks, HEADS, 1, DK), F32)
    vmem = 2 * (_nbytes((CHUNK, D_IN_PAD), F32) + _nbytes((CHUNK, 2 * D_GRP), F32) + 2 * _nbytes((HEADS, DK, DK), F32)) \
        + 2 * _nbytes((HEADS, DK, DK), F32)
    return _pcall(
        body, name="mixer_fwd", grid=(n_chunks,),
        in_specs=[proj_spec, halo_spec] + small,
        out_specs=[y_spec] + state_specs,
        out_shape=[jax.ShapeDtypeStruct((seq, 2 * D_GRP), BF16), st, st, vec, vec],
        scratch_shapes=[pltpu.VMEM((HEADS, DK, DK), F32), pltpu.VMEM((HEADS, DK, DK), F32),
                        pltpu.VMEM((HEADS, 1, DK), F32), pltpu.VMEM((HEADS, 1, DK), F32)],
        compiler_params=_params(("arbitrary",), vmem),
    )(proj, proj, lb_logits, hg_nw, conv_w, conv_b, ml_nw)


def _mixer_bwd(proj, dy, hst, cst, nst, mst, lb_logits, hg_nw, conv_w, conv_b, ml_nw):
    seq = proj.shape[0]
    n_chunks = seq // CHUNK
    proj_spec, halo_spec, small, state_specs, y_spec, _ = _mixer_specs(n_chunks, True)

    def body(proj_ref, halo_ref, dy_ref, hst_ref, cst_ref, nst_ref, mst_ref,
             lg_ref, hnw_ref, cw_ref, cb_ref, mnw_ref,
             dproj_ref, dbin_ref, dlg_ref, dhnw_ref, dcw_ref, dcb_ref, dmnw_ref,
             dhs, dcs, dns, dms, dhalo):
        c = pl.program_id(0)

        @pl.when(c == 0)
        def _():
            for r in (dhs, dcs, dns, dms, dhalo, dbin_ref, dlg_ref, dhnw_ref, dcw_ref, dcb_ref, dmnw_ref):
                r[...] = jnp.zeros_like(r)

        def put(cols, val):
            dproj_ref[:, cols] = val.astype(BF16)
            dbin_ref[:, cols] += jnp.sum(val, axis=0, keepdims=True)

        first = c == n_chunks - 1
        halo = jnp.where(first, 0.0, halo_ref[...])
        x_qk = proj_ref[:, pl.ds(4 * D_GRP, 2 * D_GRP)]
        conv_args = (halo, x_qk, cw_ref[0:1, :], cw_ref[1:2, :], cw_ref[2:3, :], cw_ref[3:4, :], cb_ref[...])
        qk, conv_vjp = jax.vjp(_qk_conv, *conv_args)
        hg_in, ml_in = _mixer_inputs(proj_ref, lg_ref, hnw_ref, mnw_ref, qk)
        _, hg_vjp = jax.vjp(_hg_chunk, hst_ref[0], *hg_in)
        _, ml_vjp = jax.vjp(_ml_chunk, cst_ref[0], nst_ref[0], _last(mst_ref[0], 0), *ml_in)
        dst, dhq, dhf, dhi, dhg, dl0, dl1, dnw = hg_vjp((dhs[...], dy_ref[:, pl.ds(0, D_GRP)]))
        dc, dn, dm, dq, dk, dv, dgates, dog, dmn = ml_vjp(
            (dcs[...], dns[...], _last(dms[...], 0), dy_ref[:, pl.ds(D_GRP, D_GRP)]))
        dhs[...] = dst
        dcs[...] = dc
        dns[...] = dn
        dms[...] = jnp.broadcast_to(dm, dms.shape)
        for i, val in ((0, dhq), (1, dhf), (2, dhi), (3, dhg), (6, dv), (7, dog)):
            put(_grp(i), val)
        put(pl.ds(8 * D_GRP, LANES), dgates)
        dlg_ref[0:1, :] += dl0
        dlg_ref[1:2, :] += dl1
        dhnw_ref[...] += dnw
        dmnw_ref[...] += dmn
        dh, dx, dw0, dw1, dw2, dw3, db = conv_vjp(jnp.concatenate([dq, dk], axis=1))
        tail = jnp.concatenate([jnp.zeros((CHUNK - SUBLANES, 2 * D_GRP), F32), dhalo[...]], axis=0)
        put(pl.ds(4 * D_GRP, 2 * D_GRP), dx + tail)
        dhalo[...] = dh
        for d, dw in enumerate((dw0, dw1, dw2, dw3)):
            dcw_ref[d:d + 1, :] += dw
        dcb_ref[...] += db

    row = pl.BlockSpec((1, D_GRP), lambda c: (0, 0))
    small_out = [pl.BlockSpec((1, D_IN_PAD), lambda c: (0, 0)), pl.BlockSpec((2, D_GRP), lambda c: (0, 0)), row,
                 pl.BlockSpec((ML_CONV, 2 * D_GRP), lambda c: (0, 0)), pl.BlockSpec((1, 2 * D_GRP), lambda c: (0, 0)), row]
    dy_spec = pl.BlockSpec((CHUNK, 2 * D_GRP), y_spec.index_map)
    vmem = 2 * (2 * _nbytes((CHUNK, D_IN_PAD), F32) + _nbytes((CHUNK, 2 * D_GRP), F32)
                + 2 * _nbytes((HEADS, DK, DK), F32)) + 2 * _nbytes((HEADS, DK, DK), F32) + 4 * 1024 * 1024
    return _pcall(
        body, name="mixer_bwd", grid=(n_chunks,),
        in_specs=[proj_spec, halo_spec, dy_spec] + state_specs + small,
        out_specs=[proj_spec] + small_out,
        out_shape=[jax.ShapeDtypeStruct((seq, D_IN_PAD), BF16), jax.ShapeDtypeStruct((1, D_IN_PAD), F32),
                   jax.ShapeDtypeStruct((2, D_GRP), F32), jax.ShapeDtypeStruct((1, D_GRP), F32),
                   jax.ShapeDtypeStruct((ML_CONV, 2 * D_GRP), F32), jax.ShapeDtypeStruct((1, 2 * D_GRP), F32),
                   jax.ShapeDtypeStruct((1, D_GRP), F32)],
        scratch_shapes=[pltpu.VMEM((HEADS, DK, DK), F32), pltpu.VMEM((HEADS, DK, DK), F32),
                        pltpu.VMEM((HEADS, 1, DK), F32), pltpu.VMEM((HEADS, 1, DK), F32),
                        pltpu.VMEM((SUBLANES, 2 * D_GRP), F32)],
        compiler_params=_params(("arbitrary",), vmem),
    )(proj, proj, dy, hst, cst, nst, mst, lb_logits, hg_nw, conv_w, conv_b, ml_nw)


def _tile(n, prefs, unit=None):
    unit = unit or n
    for p in prefs:
        if unit % p == 0 and n % p == 0:
            return p
    return unit


def _logical(arr):
    return arr.shape if arr.ndim == 2 else (arr.shape[1], arr.shape[0] * arr.shape[2])


def _group(arr):
    return arr.shape[-1]


def _split_spec(ndim, group, tr, tc, where):
    if ndim == 2:
        return pl.BlockSpec((tr, tc), where)
    per = group // tc
    assert per * tc == group, (group, tc)

    def index(*ids):
        bi, bj = where(*ids)
        return (bj // per, bi, bj % per)
    return pl.BlockSpec((None, tr, tc), index)


def _mm(name, mode, a, b, *, bias=None, res=None, res_scale=1.0, ln=None, out_dtype=F32, out_groups=None,
        copy_dtype=None, a_copy_dtype=None, tm=None, tn=None, tk=None):
    la, lb = _logical(a), _logical(b)
    if mode == "nn":
        (m, k), n = la, lb[1]
        n_unit = _group(b) if b.ndim == 3 else n
        kc = _group(a) if a.ndim == 3 else k
    elif mode == "nt":
        (m, k), n = la, lb[0]
        n_unit = n
        kc = min(_group(a) if a.ndim == 3 else k, _group(b) if b.ndim == 3 else k)
    else:
        (k, m), n = la, lb[1]
        n_unit, kc = (_group(b) if b.ndim == 3 else n), k
        assert a.ndim == 2
    if out_groups:
        n_unit = min(n_unit, n // out_groups)
    kind = ln[0] if ln else None
    tm = tm or (256 if ln else _tile(m, (512, 256, 128)))
    tn = n if ln else (tn or _tile(n, (512, 384, 256, 128), n_unit))
    if mode != "tn":
        tk = k
    elif tk is None:
        tk = _tile(k, (4096, 2048, 512, 256, 128) if (m // tm) * (n // tn) > 1 else (2048, 512, 256, 128))
    gi, gj, gk = m // tm, n // tn, k // tk
    assert gi * tm == m and gj * tn == n and gk * tk == k and n_unit % tn == 0, (name, m, n, k, tm, tn, tk)
    ca, cb = {"nn": (1, 0), "nt": (1, 1), "tn": (0, 0)}[mode]
    i_outer = gk > 1 or (gi - 1) * _nbytes(b.shape, b.dtype) <= (gj - 1) * _nbytes(a.shape, a.dtype)

    def ij(where):
        return (lambda p, q, kk: where(p, q, kk)) if i_outer else (lambda p, q, kk: where(q, p, kk))
    if mode == "tn":
        a_spec = pl.BlockSpec((tk, tm), ij(lambda i, j, kk: (kk, i)))
    elif a.ndim == 3:
        a_spec = pl.BlockSpec((a.shape[0], tm, _group(a)), ij(lambda i, j, kk: (0, i, 0)))
    else:
        a_spec = pl.BlockSpec((tm, k), ij(lambda i, j, kk: (i, 0)))
    if mode != "nt":
        b_spec = _split_spec(b.ndim, _group(b), tk, tn, ij(lambda i, j, kk: (kk, j)))
    elif b.ndim == 3:
        b_spec = pl.BlockSpec((b.shape[0], tn, _group(b)), ij(lambda i, j, kk: (0, j, 0)))
    else:
        b_spec = pl.BlockSpec((tn, k), ij(lambda i, j, kk: (j, 0)))
    row_spec = pl.BlockSpec((1, tn), ij(lambda i, j, kk: (0, j)))
    blk_spec = pl.BlockSpec((tm, tn), ij(lambda i, j, kk: (i, j)))
    ins, in_specs = [a, b], [a_spec, b_spec]
    if bias is not None:
        ins.append(bias), in_specs.append(row_spec)
    if res is not None:
        ins.append(res), in_specs.append(blk_spec)
    if kind == "fwd":
        ins += [ln[1], ln[2]]
        in_specs += [row_spec, row_spec]
    elif kind == "bwd":
        ins += [ln[1], ln[2], ln[3]]
        in_specs += [blk_spec, row_spec, row_spec]
    if out_groups:
        blk_out = jax.ShapeDtypeStruct((out_groups, m, n // out_groups), out_dtype)
        out_spec = _split_spec(3, n // out_groups, tm, tn, ij(lambda i, j, kk: (i, j)))
    else:
        blk_out, out_spec = jax.ShapeDtypeStruct((m, n), out_dtype), blk_spec
    row_out = jax.ShapeDtypeStruct((1, n), F32)
    if kind is None:
        out_shape, out_specs = [blk_out], [out_spec]
    elif kind == "fwd":
        out_shape, out_specs = [blk_out, blk_out], [blk_spec, blk_spec]
    else:
        out_shape, out_specs = [blk_out, row_out, row_out], [blk_spec, row_spec, row_spec]
    if copy_dtype is not None:
        out_shape.append(jax.ShapeDtypeStruct((m, n), copy_dtype))
        out_specs.append(blk_spec)
    if a_copy_dtype is not None:
        assert mode != "tn" and a.ndim == 2 and copy_dtype is None
        out_shape.append(jax.ShapeDtypeStruct((m, k), a_copy_dtype))
        out_specs.append(a_spec)
    n_in = len(ins)

    def body(*refs):
        in_refs, out_refs, acc_ref = refs[:n_in], refs[n_in:n_in + len(out_shape)], refs[-1]
        i, kk = pl.program_id(0 if i_outer else 1), pl.program_id(2)
        a_ref, b_ref = in_refs[:2]
        extra = list(in_refs[2:])
        if a_copy_dtype is not None:
            out_refs[-1][...] = a_ref[...].astype(a_copy_dtype)

        def epilogue(acc):
            rest = list(extra)
            if bias is not None:
                acc = acc + rest.pop(0)[...]
            if res is not None:
                acc = acc + res_scale * rest.pop(0)[...]
            if kind is None:
                out_refs[0][...] = acc.astype(out_dtype)
                return
            if kind == "fwd":
                out_refs[0][...] = acc
                y = _layer_norm(acc, rest[0][...], rest[1][...])
                out_refs[1][...] = y
                if copy_dtype is not None:
                    out_refs[-1][...] = y.astype(copy_dtype)
                return
            _, vjp = jax.vjp(_layer_norm, rest[0][...], rest[1][...], rest[2][...])
            dz, dg, db = vjp(acc)
            out_refs[0][...] = dz
            out_refs[1][...] += dg
            out_refs[2][...] += db
            if copy_dtype is not None:
                out_refs[-1][...] = dz.astype(copy_dtype)

        if kind == "bwd":
            @pl.when((i == 0) & (kk == 0))
            def _():
                out_refs[1][...] = jnp.zeros_like(out_refs[1])
                out_refs[2][...] = jnp.zeros_like(out_refs[2])

        def chunk(ref, c0, last):
            if ref.ndim == 3:
                g = ref.shape[2]
                return ref[c0 // g, :, pl.ds(c0 % g, kc)]
            return ref[:, pl.ds(c0, kc)] if last else ref[pl.ds(c0, kc), :]

        if mode == "tn" or kc == k:
            prod = _dg(a_ref[...], b_ref[...], ca, cb)
        else:
            prod = None
            for c0 in range(0, k, kc):
                part = _dg(chunk(a_ref, c0, True), chunk(b_ref, c0, mode == "nt"), ca, cb)
                prod = part if prod is None else prod + part
        if gk == 1:
            epilogue(prod)
            return

        @pl.when(kk == 0)
        def _():
            acc_ref[...] = prod

        @pl.when(kk > 0)
        def _():
            acc_ref[...] += prod

        @pl.when(kk == gk - 1)
        def _():
            epilogue(acc_ref[...])

    vmem = (2 * (_nbytes((tm, tk), a.dtype) + _nbytes((tk, tn), b.dtype))
            + (2 * len(ins) + 2 * len(out_shape) + 1) * _nbytes((tm, tn), F32))
    outs = _pcall(
        body, name=name, grid=(gi, gj, gk) if i_outer else (gj, gi, gk), in_specs=in_specs, out_specs=out_specs,
        out_shape=out_shape, scratch_shapes=[pltpu.VMEM((tm, tn) if gk > 1 else (SUBLANES, LANES), F32)],
        compiler_params=_params(("arbitrary", "arbitrary", "arbitrary"), vmem),
    )(*ins)
    return outs[0] if len(out_shape) == 1 else outs


def _attn_head(q, k, v):
    sc = mm_nt(q, k) * (CA_DH ** -0.5)
    e = jnp.exp(sc - jnp.max(sc, axis=-1, keepdims=True))
    return mm_nn(e / jnp.sum(e, axis=-1, keepdims=True), v)


def _attn_fwd(q, kv):
    seq, n_mem = q.shape[0], kv.shape[0]
    tq = _tile(seq, (512, 256, 128))

    def body(q_ref, kv_ref, o_ref):
        for h in range(HEADS):
            hd = pl.ds(h * CA_DH, CA_DH)
            o = _attn_head(q_ref[:, hd], kv_ref[:, hd], kv_ref[:, pl.ds(D_MODEL + h * CA_DH, CA_DH)])
            o_ref[:, hd] = o.astype(BF16)

    return _pcall(
        body, name="attn_fwd", grid=(seq // tq,),
        in_specs=[pl.BlockSpec((tq, D_MODEL), lambda i: (i, 0)), pl.BlockSpec((n_mem, 2 * D_MODEL), lambda i: (0, 0))],
        out_specs=pl.BlockSpec((tq, D_MODEL), lambda i: (i, 0)), out_shape=jax.ShapeDtypeStruct((seq, D_MODEL), BF16),
        compiler_params=_params(("arbitrary",), 4 * _nbytes((tq, D_MODEL), F32) + 2 * _nbytes((n_mem, 2 * D_MODEL), F32)),
    )(q, kv)


def _attn_bwd(q, kv, do):
    seq, n_mem = q.shape[0], kv.shape[0]
    tq = _tile(seq, (512, 256, 128))

    def body(q_ref, kv_ref, do_ref, dq_ref, dkv_ref):
        @pl.when(pl.program_id(0) == 0)
        def _():
            dkv_ref[...] = jnp.zeros_like(dkv_ref)

        for h in range(HEADS):
            hd = pl.ds(h * CA_DH, CA_DH)
            vd = pl.ds(D_MODEL + h * CA_DH, CA_DH)
            _, vjp = jax.vjp(_attn_head, q_ref[:, hd], kv_ref[:, hd], kv_ref[:, vd])
            dq, dk, dv = vjp(do_ref[:, hd].astype(F32))
            dq_ref[:, hd] = dq.astype(BF16)
            dkv_ref[:, hd] += dk
            dkv_ref[:, vd] += dv

    return _pcall(
        body, name="attn_bwd", grid=(seq // tq,),
        in_specs=[pl.BlockSpec((tq, D_MODEL), lambda i: (i, 0)), pl.BlockSpec((n_mem, 2 * D_MODEL), lambda i: (0, 0)),
                  pl.BlockSpec((tq, D_MODEL), lambda i: (i, 0))],
        out_specs=[pl.BlockSpec((tq, D_MODEL), lambda i: (i, 0)), pl.BlockSpec((n_mem, 2 * D_MODEL), lambda i: (0, 0))],
        out_shape=[jax.ShapeDtypeStruct((seq, D_MODEL), BF16), jax.ShapeDtypeStruct((n_mem, 2 * D_MODEL), F32)],
        compiler_params=_params(("arbitrary",), 6 * _nbytes((tq, D_MODEL), F32) + 4 * _nbytes((n_mem, 2 * D_MODEL), F32)),
    )(q, kv, do)


def _ffn_mid(hg, xg, hv, xv, wg0, wg1, wg2, bg, wv0, wv1, wv2, bv):
    return jax.nn.gelu(causal_conv(hg, xg, (wg0, wg1, wg2), bg)) * causal_conv(hv, xv, (wv0, wv1, wv2), bv)


FFN_TB = 256
FFN_W = D_FF // 2
FFN_J = D_FF // FFN_W
MXU_COLS = 256
FFN_PIECES = tuple((off, min(MXU_COLS, FFN_W - off)) for off in range(0, FFN_W, MXU_COLS))


def _ffn_common_specs(seq, row):
    tb = min(FFN_TB, seq)
    full = pl.BlockSpec((tb, D_MODEL), lambda t, j: (row(t), 0))
    vec = pl.BlockSpec((1, D_MODEL), lambda t, j: (0, 0))
    halves = []
    for off in (0, FFN_J):
        halves.append(dict(
            w_up=pl.BlockSpec((None, D_MODEL, FFN_W), lambda t, j, off=off: (j + off, 0, 0)),
            taps=pl.BlockSpec((FFN_CONV, FFN_W), lambda t, j, off=off: (0, j + off)),
            bias=pl.BlockSpec((1, FFN_W), lambda t, j, off=off: (0, j + off))))
    w_down = pl.BlockSpec((FFN_W, D_MODEL), lambda t, j: (j, 0))
    u_blk = pl.BlockSpec((2, tb, FFN_W), lambda t, j: (0, row(t), j))
    return tb, full, vec, halves, w_down, u_blk


def _ffn_vmem(tb):
    return (_nbytes((2, tb, FFN_W), F32) + _nbytes((2, tb, FFN_W), BF16) + 3 * _nbytes((D_MODEL, FFN_W), BF16)
            + 10 * _nbytes((tb, D_MODEL), F32))


def _conv_params(taps_ref, bias_ref, cols):
    return taps_ref[0:1, cols], taps_ref[1:2, cols], taps_ref[2:3, cols], bias_ref[:, cols]


def _ffn_fwd(x2b, x2, w_up, conv_w, conv_b, w_down, ln_g, ln_b, target):
    seq = x2.shape[0]
    tb, full, vec, halves, wd_spec, u_blk = _ffn_common_specs(seq, lambda t: t)
    nt = seq // tb

    def body(xb_ref, wg_ref, wv_ref, tg_ref, tv_ref, bg_ref, bv_ref, wd_ref, x_ref, g_ref, b_ref, tgt_ref,
             u_ref, h_ref, dz_ref, dg_ref, db_ref, loss_ref, dzb_ref, acc, carry):
        t, j = pl.program_id(0), pl.program_id(1)
        xb = xb_ref[...]
        pieces = [pl.ds(off, width) for off, width in FFN_PIECES]
        ug = [_dg(xb, wg_ref[:, cols], 1, 0) for cols in pieces]
        uv = [_dg(xb, wv_ref[:, cols], 1, 0) for cols in pieces]
        hs = []
        for cols, g, v in zip(pieces, ug, uv):
            u_ref[0, :, cols] = g
            u_ref[1, :, cols] = v
            halo_g = jnp.where(t == 0, 0.0, carry[j, 0, :, cols])
            halo_v = jnp.where(t == 0, 0.0, carry[j, 1, :, cols])
            h = _ffn_mid(halo_g, g, halo_v, v, *_conv_params(tg_ref, bg_ref, cols),
                         *_conv_params(tv_ref, bv_ref, cols)).astype(BF16)
            carry[j, 0, :, cols] = g[tb - SUBLANES:, :]
            carry[j, 1, :, cols] = v[tb - SUBLANES:, :]
            h_ref[:, cols] = h
            hs.append(h)
        part = None
        for cols, h in zip(pieces, hs):
            p = _dg(h, wd_ref[cols, :], 1, 0)
            part = p if part is None else part + p

        @pl.when(j == 0)
        def _():
            acc[...] = part

        @pl.when(j > 0)
        def _():
            acc[...] += part

        @pl.when(j == FFN_J - 1)
        def _():
            y, vjp = jax.vjp(_layer_norm, acc[...] + ALPHA * x_ref[...], g_ref[...], b_ref[...])
            err = y - tgt_ref[...]
            part_loss = 0.5 * jnp.sum(jnp.sum(err * err, axis=1, keepdims=True), axis=0, keepdims=True) / D_MODEL
            dz, dg, db = vjp(err / D_MODEL)

            @pl.when(t == 0)
            def _():
                for r in (dg_ref, db_ref, loss_ref):
                    r[...] = jnp.zeros_like(r)

            dz_ref[...] = dz
            dzb_ref[...] = dz.astype(BF16)
            dg_ref[...] += dg
            db_ref[...] += db
            loss_ref[...] += jnp.broadcast_to(part_loss, (1, LANES))

    h0, h1 = halves
    row = jax.ShapeDtypeStruct((1, D_MODEL), F32)
    return _pcall(
        body, name="ffn_fwd", grid=(nt, FFN_J),
        in_specs=[full, h0["w_up"], h1["w_up"], h0["taps"], h1["taps"], h0["bias"], h1["bias"], wd_spec, full, vec, vec,
                  full],
        out_specs=[u_blk, pl.BlockSpec((tb, FFN_W), lambda t, j: (t, j)), full, vec, vec,
                   pl.BlockSpec((1, LANES), lambda t, j: (0, 0)), full],
        out_shape=[jax.ShapeDtypeStruct((2, seq, D_FF), F32), jax.ShapeDtypeStruct((seq, D_FF), BF16),
                   jax.ShapeDtypeStruct((seq, D_MODEL), F32), row, row, jax.ShapeDtypeStruct((1, LANES), F32),
                   jax.ShapeDtypeStruct((seq, D_MODEL), BF16)],
        scratch_shapes=[pltpu.VMEM((tb, D_MODEL), F32), pltpu.VMEM((FFN_J, 2, SUBLANES, FFN_W), F32)],
        compiler_params=_params(("arbitrary", "arbitrary"), _ffn_vmem(tb)),
    )(x2b, w_up, w_up, conv_w, conv_w, conv_b, conv_b, w_down, x2, ln_g, ln_b, target)


def _ffn_bwd(u, conv_w, conv_b, dz3b, dz3, w_down, w_up, z2, ln_g, ln_b):
    seq = dz3.shape[0]
    tb = min(FFN_TB, seq)
    nt = seq // tb
    row8 = tb // SUBLANES
    tb, full, vec, halves, wd_spec, u_blk = _ffn_common_specs(seq, lambda t: nt - 1 - t)
    halo = pl.BlockSpec((2, SUBLANES, FFN_W), lambda t, j: (0, jnp.maximum((nt - 1 - t) * row8 - 1, 0), j))

    def body(u_ref, halo_ref, tg_ref, tv_ref, bg_ref, bv_ref, dzb_ref, wd_ref, wg_ref, wv_ref, dz3_ref, z_ref, g_ref,
             b_ref, du_ref, dw_ref, dbias_ref, dz_ref, dg_ref, db_ref, dz2b_ref, acc, carry):
        t, j = pl.program_id(0), pl.program_id(1)

        @pl.when((t == 0) & (j == 0))
        def _():
            for r in (dw_ref, dbias_ref, dg_ref, db_ref):
                r[...] = jnp.zeros_like(r)

        pieces = [pl.ds(off, width) for off, width in FFN_PIECES]
        dzb = dzb_ref[...]
        dhs = [_dg(dzb, wd_ref[cols, :], 1, 1) for cols in pieces]
        first = t == nt - 1
        dus = []
        for cols, dh in zip(pieces, dhs):
            args = (jnp.where(first, 0.0, halo_ref[0, :, cols]), u_ref[0, :, cols],
                    jnp.where(first, 0.0, halo_ref[1, :, cols]), u_ref[1, :, cols],
                    *_conv_params(tg_ref, bg_ref, cols), *_conv_params(tv_ref, bv_ref, cols))
            _, vjp = jax.vjp(_ffn_mid, *args)
            dhg, dxg, dhv, dxv, g0, g1, g2, gb, v0, v1, v2, vb = vjp(dh)
            zeros = jnp.zeros((tb - SUBLANES, dh.shape[1]), F32)
            dug = (dxg + jnp.concatenate([zeros, jnp.where(t == 0, 0.0, carry[j, 0, :, cols])], axis=0)).astype(BF16)
            duv = (dxv + jnp.concatenate([zeros, jnp.where(t == 0, 0.0, carry[j, 1, :, cols])], axis=0)).astype(BF16)
            carry[j, 0, :, cols] = dhg
            carry[j, 1, :, cols] = dhv
            du_ref[0, :, cols] = dug
            du_ref[1, :, cols] = duv
            for half, parts in enumerate(((g0, g1, g2), (v0, v1, v2))):
                for d, p in enumerate(parts):
                    dw_ref[j, half, d:d + 1, cols] += p
            dbias_ref[j, 0, :, cols] += gb
            dbias_ref[j, 1, :, cols] += vb
            dus.append((dug, duv))
        part = None
        for cols, (dug, duv) in zip(pieces, dus):
            p = _dg(dug, wg_ref[:, cols], 1, 1) + _dg(duv, wv_ref[:, cols], 1, 1)
            part = p if part is None else part + p

        @pl.when(j == 0)
        def _():
            acc[...] = part

        @pl.when(j > 0)
        def _():
            acc[...] += part

        @pl.when(j == FFN_J - 1)
        def _():
            _, ln_vjp = jax.vjp(_layer_norm, z_ref[...], g_ref[...], b_ref[...])
            dz, dg, db = ln_vjp(acc[...] + ALPHA * dz3_ref[...])
            dz_ref[...] = dz
            dz2b_ref[...] = dz.astype(BF16)
            dg_ref[...] += dg
            db_ref[...] += db

    h0, h1 = halves
    row = jax.ShapeDtypeStruct((1, D_MODEL), F32)
    whole = lambda *shape: pl.BlockSpec(shape, lambda t, j: (0,) * len(shape))
    return _pcall(
        body, name="ffn_bwd", grid=(nt, FFN_J),
        in_specs=[u_blk, halo, h0["taps"], h1["taps"], h0["bias"], h1["bias"], full, wd_spec, h0["w_up"], h1["w_up"],
                  full, full, vec, vec],
        out_specs=[u_blk, whole(FFN_J, 2, FFN_CONV, FFN_W), whole(FFN_J, 2, 1, FFN_W), full, vec, vec, full],
        out_shape=[jax.ShapeDtypeStruct((2, seq, D_FF), BF16), jax.ShapeDtypeStruct((FFN_J, 2, FFN_CONV, FFN_W), F32),
                   jax.ShapeDtypeStruct((FFN_J, 2, 1, FFN_W), F32), jax.ShapeDtypeStruct((seq, D_MODEL), F32), row, row,
                   jax.ShapeDtypeStruct((seq, D_MODEL), BF16)],
        scratch_shapes=[pltpu.VMEM((tb, D_MODEL), F32), pltpu.VMEM((FFN_J, 2, SUBLANES, FFN_W), F32)],
        compiler_params=_params(("arbitrary", "arbitrary"), _ffn_vmem(tb)),
    )(u, u, conv_w, conv_w, conv_b, conv_b, dz3b, w_down, w_up, w_up, dz3, z2, ln_g, ln_b)


def _adamw_math(w, g, m, v):
    m_new = ADAM_B1 * m + (1.0 - ADAM_B1) * g
    v_new = ADAM_B2 * v + (1.0 - ADAM_B2) * jnp.square(g)
    m_hat = m_new / (1.0 - ADAM_B1 ** ADAM_STEP)
    v_hat = v_new / (1.0 - ADAM_B2 ** ADAM_STEP)
    return -ADAM_LR * (m_hat / (jnp.sqrt(v_hat) + ADAM_EPS) + ADAM_WD * w), m_new, v_new


def _adamw_many(name, ws, gs, ms, vs):
    n = len(ws)

    def body(*refs):
        w_refs, g_refs, m_refs, v_refs = (refs[i * n:(i + 1) * n] for i in range(4))
        d_refs, nm_refs, nv_refs = (refs[(4 + i) * n:(5 + i) * n] for i in range(3))
        for i in range(n):
            d_refs[i][...], nm_refs[i][...], nv_refs[i][...] = _adamw_math(
                w_refs[i][...], g_refs[i][...], m_refs[i][...], v_refs[i][...])

    vm = pl.BlockSpec(memory_space=pltpu.VMEM)
    outs = _pcall(
        body, pin=False, name=name, in_specs=[vm] * (4 * n), out_specs=[vm] * (3 * n),
        out_shape=[jax.ShapeDtypeStruct(w.shape, F32) for w in ws] * 3,
    )(*ws, *gs, *ms, *vs)
    return outs[:n], outs[n:2 * n], outs[2 * n:]


def _adamw_halves(name, core, w, mine, theirs, m, v):
    rows, cols = w.shape
    half_rows = mine.shape[0]
    tr = _tile(half_rows, (256, 176, 128))
    nbh = half_rows // tr
    assert 2 * half_rows == rows

    def body(c_ref, w_ref, a_ref, b_ref, m_ref, v_ref, g_ref, d_ref, nm_ref, nv_ref):
        g = jnp.where(pl.program_id(0) // nbh == c_ref[0], a_ref[...], b_ref[...])
        g_ref[...] = g
        d_ref[...], nm_ref[...], nv_ref[...] = _adamw_math(w_ref[...], g, m_ref[...], v_ref[...])

    spec = pl.BlockSpec((tr, cols), lambda i, c_ref: (i, 0))
    half = pl.BlockSpec((tr, cols), lambda i, c_ref: (i % nbh, 0))
    sh = jax.ShapeDtypeStruct((rows, cols), F32)
    grid_spec = pltpu.PrefetchScalarGridSpec(
        num_scalar_prefetch=1, grid=(rows // tr,), in_specs=[spec, half, half, spec, spec], out_specs=[spec] * 4)
    return _pcall(
        body, name=name, grid_spec=grid_spec, out_shape=[sh] * 4,
        compiler_params=_params(("arbitrary",), 18 * _nbytes((tr, -(-cols // LANES) * LANES), F32)),
    )(core, w, mine, theirs, m, v)


MESH = pl.DeviceIdType.MESH
ANY = pl.BlockSpec(memory_space=pl.ANY)
N_CHIPS = 4
BF16_ROWS = 16


def _me():
    return lax.axis_index("x"), lax.axis_index("y"), lax.axis_index("c")


def _other_chips(x, y):
    return [(1 - x, y), (x, 1 - y), (1 - x, 1 - y)]


def _remote(src, dst, ssem, rsem, dev):
    return pltpu.make_async_remote_copy(src_ref=src, dst_ref=dst, send_sem=ssem, recv_sem=rsem,
                                        device_id=dev, device_id_type=MESH)


def _half_rows(ref_rows, cc):
    half = ref_rows // 2
    return pl.ds(pl.multiple_of(cc * half, BF16_ROWS), half)


def _gather_weights(shards):
    n = len(shards)
    n_ici = n * (N_CHIPS - 1)

    def body(*refs):
        ins, outs, (ssem, rsem, lsem, lrsem) = refs[:n], refs[n:2 * n], refs[2 * n:]
        x, y, c = _me()
        k_me = 2 * x + y
        sib = (x, y, 1 - c)
        chips = _other_chips(x, y)
        started = []
        for i, (w_ref, o_ref) in enumerate(zip(ins, outs)):
            cp = _remote(w_ref, o_ref.at[k_me], lsem.at[i], lrsem.at[i], sib)
            cp.start()
            started.append(cp)
        for r, (px, py) in enumerate(chips):
            for i, (w_ref, o_ref) in enumerate(zip(ins, outs)):
                rows = _half_rows(w_ref.shape[0], c)
                s = r * n + i
                cp = _remote(w_ref.at[rows], o_ref.at[k_me, rows], ssem.at[s], rsem.at[s], (px, py, c))
                cp.start()
                started.append(cp)
        for r, (px, py) in enumerate(chips):
            for i, o_ref in enumerate(outs):
                blk = o_ref.at[2 * px + py, _half_rows(o_ref.shape[1], c)]
                s = r * n + i
                _remote(blk, blk, ssem.at[s], rsem.at[s], (px, py, c)).wait_recv()
                cp = _remote(blk, blk, ssem.at[n_ici + s], rsem.at[n_ici + s], sib)
                cp.start()
                started.append(cp)
        for r, (px, py) in enumerate(chips):
            for i, o_ref in enumerate(outs):
                blk = o_ref.at[2 * px + py, _half_rows(o_ref.shape[1], 1 - c)]
                s = n_ici + r * n + i
                _remote(blk, blk, ssem.at[s], rsem.at[s], sib).wait_recv()
        for cp in started[n:]:
            cp.wait_send()
        for cp in started[:n]:
            cp.wait()

    return _pcall(
        body, name="gather_weights", in_specs=[ANY] * n, out_specs=[ANY] * n,
        out_shape=[jax.ShapeDtypeStruct((N_CHIPS,) + s.shape, s.dtype) for s in shards],
        scratch_shapes=[pltpu.SemaphoreType.DMA((2 * n_ici,)), pltpu.SemaphoreType.DMA((2 * n_ici,)),
                        pltpu.SemaphoreType.DMA((n,)), pltpu.SemaphoreType.DMA((n,))],
    )(*shards)


def _swap_halves(name, grads):
    n = len(grads)

    def body(*refs):
        ins, outs, (ssem, rsem) = refs[:n], refs[n:2 * n], refs[2 * n:]
        x, y, c = _me()
        copies = []
        for i, (g_ref, o_ref) in enumerate(zip(ins, outs)):
            for k in range(N_CHIPS):
                s = i * N_CHIPS + k
                cp = _remote(g_ref.at[k, _half_rows(g_ref.shape[1], 1 - c)], o_ref.at[k], ssem.at[s], rsem.at[s],
                             (x, y, 1 - c))
                cp.start()
                copies.append(cp)
        for cp in copies:
            cp.wait()

    return _pcall(
        body, name=name, in_specs=[ANY] * n, out_specs=[ANY] * n,
        out_shape=[jax.ShapeDtypeStruct((N_CHIPS, g.shape[1] // 2, g.shape[2]), g.dtype) for g in grads],
        scratch_shapes=[pltpu.SemaphoreType.DMA((n * N_CHIPS,)), pltpu.SemaphoreType.DMA((n * N_CHIPS,))],
    )(*grads)


SEM = pl.BlockSpec(memory_space=pltpu.SEMAPHORE)
IN_HBM = pl.BlockSpec(memory_space=pltpu.HBM)
SPLIT_PARAMS = dict(compiler_params=pltpu.CompilerParams(has_side_effects=pltpu.SideEffectType.DATAFLOW_SIDE_EFFECTING))


def _split_start(name, sources, landings, n_copies, plan):
    ns, nl = len(sources), len(landings)

    def body(*refs):
        ins, lands, (ssem, rsem), token = refs[:ns], refs[ns:ns + nl], refs[ns + nl:ns + nl + 2], refs[-1]
        for s, (src, dst, _, dev) in enumerate(plan(ins, lands)):
            _remote(src, dst, ssem.at[s], rsem.at[s], dev).start()
        token[...] = jnp.zeros_like(token)

    arrays = list(sources) + list(landings)
    outs = _call(
        body, name=name, in_specs=[IN_HBM] * (ns + nl),
        out_specs=[SEM, SEM] + [IN_HBM] * (ns + nl) + [pl.BlockSpec(memory_space=pltpu.VMEM)],
        out_shape=[pltpu.SemaphoreType.DMA((n_copies,)), pltpu.SemaphoreType.DMA((n_copies,))]
        + [pltpu.HBM(a.shape, a.dtype) for a in arrays] + [jax.ShapeDtypeStruct((1, 1), F32)],
        input_output_aliases={i: 2 + i for i in range(ns + nl)}, **SPLIT_PARAMS,
    )(*[pltpu.with_memory_space_constraint(a, pltpu.HBM) for a in arrays])
    return (outs[:-1], ns), outs[-1]


def _split_wait(name, handle, after, plan):
    (ssem, rsem, *thru), ns = handle
    nl = len(thru) - ns

    def body(*refs):
        ins, lands, (ssem_ref, rsem_ref) = refs[:ns], refs[ns:ns + nl], refs[ns + nl:ns + nl + 2]
        for s, (src, _, dst, dev) in enumerate(plan(ins, lands)):
            cp = _remote(src, dst, ssem_ref.at[s], rsem_ref.at[s], dev)
            cp.wait_send()
            cp.wait_recv()

    outs = _call(
        body, name=name, in_specs=[IN_HBM] * (ns + nl) + [SEM, SEM, ANY], out_specs=[IN_HBM] * (ns + nl),
        out_shape=[pltpu.HBM(t.shape, t.dtype) for t in thru],
        input_output_aliases={i: i for i in range(ns + nl)}, **SPLIT_PARAMS,
    )(*thru, ssem, rsem, after)
    return outs[:ns], outs[ns:]


def _swap_plan(ins, lands):
    x, y, c = _me()
    return [(g_ref.at[k, _half_rows(g_ref.shape[1], 1 - c)], l_ref.at[k], l_ref.at[k], (x, y, 1 - c))
            for g_ref, l_ref in zip(ins, lands) for k in range(N_CHIPS)]


def _swap_start(name, grads):
    lands = [lax.empty((N_CHIPS, g.shape[1] // 2, g.shape[2]), g.dtype) for g in grads]
    return _split_start(name, grads, lands, len(grads) * N_CHIPS, _swap_plan)


def _swap_wait(name, handle, after):
    return _split_wait(name, handle, after, _swap_plan)


def _gather_plan(ins, lands):
    x, y, c = _me()
    k_me = 2 * x + y
    plan = [(w_ref, l_ref.at[k_me], l_ref.at[k_me], (x, y, 1 - c)) for w_ref, l_ref in zip(ins, lands)]
    for px, py in _other_chips(x, y):
        for w_ref, l_ref in zip(ins, lands):
            rows = _half_rows(w_ref.shape[0], c)
            plan.append((w_ref.at[rows], l_ref.at[k_me, rows], l_ref.at[2 * px + py, rows], (px, py, c)))
    return plan


def _gather_start(name, shards):
    lands = [lax.empty((N_CHIPS,) + s.shape, s.dtype) for s in shards]
    return _split_start(name, shards, lands, len(shards) * N_CHIPS, _gather_plan)


def _gather_wait(name, handle, after):
    return _split_wait(name, handle, after, _gather_plan)[1]


def _forward_halves(name, blocks):
    n = len(blocks)
    n_sem = n * (N_CHIPS - 1)

    def body(*refs):
        outs, (ssem, rsem) = refs[n:2 * n], refs[2 * n:]
        x, y, c = _me()
        sib = (x, y, 1 - c)
        chips = _other_chips(x, y)
        sends = []
        for r, (px, py) in enumerate(chips):
            for i, o_ref in enumerate(outs):
                blk = o_ref.at[2 * px + py, _half_rows(o_ref.shape[1], c)]
                cp = _remote(blk, blk, ssem.at[r * n + i], rsem.at[r * n + i], sib)
                cp.start()
                sends.append(cp)
        for r, (px, py) in enumerate(chips):
            for i, o_ref in enumerate(outs):
                blk = o_ref.at[2 * px + py, _half_rows(o_ref.shape[1], 1 - c)]
                _remote(blk, blk, ssem.at[r * n + i], rsem.at[r * n + i], sib).wait_recv()
        for cp in sends:
            cp.wait_send()

    return _pcall(
        body, name=name, in_specs=[ANY] * n, out_specs=[ANY] * n,
        out_shape=[jax.ShapeDtypeStruct(b.shape, b.dtype) for b in blocks],
        input_output_aliases={i: i for i in range(n)},
        scratch_shapes=[pltpu.SemaphoreType.DMA((n_sem,)), pltpu.SemaphoreType.DMA((n_sem,))],
    )(*blocks)


def _scatter_plan(ins, lands):
    x, y, c = _me()
    k_me = 2 * x + y
    return [(p_ref.at[2 * px + py], l_ref.at[k_me], l_ref.at[2 * px + py], (px, py, c))
            for px, py in _other_chips(x, y) for p_ref, l_ref in zip(ins, lands)]


def _scatter_start(name, parts):
    lands = [lax.empty(p.shape, p.dtype) for p in parts]
    return _split_start(name, parts, lands, len(parts) * (N_CHIPS - 1), _scatter_plan)


def _scatter_wait(name, handle, after):
    return _split_wait(name, handle, after, _scatter_plan)[1]


def _share_and_reduce(halves, v):
    n = len(halves)
    rows = v.shape[0]
    half = rows // 2
    assert half % SUBLANES == 0

    def body(*refs):
        ins, v_ref, outs, out_ref = refs[:n], refs[n], refs[n + 1:2 * n + 1], refs[2 * n + 1]
        pair_buf, mine, chip_buf, ssem, rsem, half_ssem, half_rsem = refs[2 * n + 2:]
        x, y, c = _me()
        k_me = 2 * x + y
        sib = (x, y, 1 - c)
        copies = [_remote(r_ref, o_ref, half_ssem.at[i], half_rsem.at[i], sib)
                  for i, (r_ref, o_ref) in enumerate(zip(ins, outs))]
        for cp in copies:
            cp.start()

        def rows_of(cc):
            return pl.ds(pl.multiple_of(cc * half, SUBLANES), half)

        swap = _remote(v_ref.at[rows_of(1 - c)], pair_buf, ssem.at[0], rsem.at[0], sib)
        swap.start()
        swap.wait()
        mine[...] = v_ref[rows_of(c), :] + pair_buf[...]
        chip_buf[k_me] = mine[...]
        sends = [_remote(mine, chip_buf.at[k_me], ssem.at[1 + r], rsem.at[1 + r], (px, py, c))
                 for r, (px, py) in enumerate(_other_chips(x, y))]
        for cp in sends:
            cp.start()
        for r, (px, py) in enumerate(_other_chips(x, y)):
            blk = chip_buf.at[2 * px + py]
            _remote(blk, blk, ssem.at[1 + r], rsem.at[1 + r], (px, py, c)).wait_recv()
        total = chip_buf[0]
        for k in range(1, N_CHIPS):
            total = total + chip_buf[k]
        out_ref[rows_of(c), :] = total
        for cp in sends:
            cp.wait_send()
        share = _remote(out_ref.at[rows_of(c)], out_ref.at[rows_of(c)], ssem.at[N_CHIPS], rsem.at[N_CHIPS], sib)
        share.start()
        got = out_ref.at[rows_of(1 - c)]
        _remote(got, got, ssem.at[N_CHIPS], rsem.at[N_CHIPS], sib).wait_recv()
        share.wait_send()
        for cp in copies:
            cp.wait()

    vm = pl.BlockSpec(memory_space=pltpu.VMEM)
    outs = _call(
        body, name="share_and_reduce", in_specs=[ANY] * n + [vm], out_specs=[ANY] * n + [vm],
        out_shape=[pltpu.HBM(h.shape, h.dtype) for h in halves] + [jax.ShapeDtypeStruct((rows, LANES), F32)],
        scratch_shapes=[pltpu.VMEM((half, LANES), F32), pltpu.VMEM((half, LANES), F32),
                        pltpu.VMEM((N_CHIPS, half, LANES), F32), pltpu.SemaphoreType.DMA((N_CHIPS + 1,)),
                        pltpu.SemaphoreType.DMA((N_CHIPS + 1,)), pltpu.SemaphoreType.DMA((n,)),
                        pltpu.SemaphoreType.DMA((n,))],
        compiler_params=pltpu.CompilerParams(vmem_limit_bytes=32 * 1024 * 1024),
    )(*[pltpu.with_memory_space_constraint(h, pltpu.HBM) for h in halves], v)
    return outs[:n], outs[n]


def _add_pair(name, core, chip, g, theirs):
    _, half, cols = theirs.shape
    tr = _tile(half, (256, 176, 128))
    nb = half // tr

    def body(c_ref, k_ref, g_ref, t_ref, o32_ref, o16_ref):
        s = g_ref[...] + t_ref[...]
        o16_ref[...] = s.astype(BF16)

        @pl.when(pl.program_id(1) == k_ref[0])
        def _():
            o32_ref[...] = s

    spec = pl.BlockSpec((None, tr, cols), lambda i, k, c_ref, k_ref: (k, i, 0))
    grid_spec = pltpu.PrefetchScalarGridSpec(
        num_scalar_prefetch=2, grid=(nb, N_CHIPS),
        in_specs=[pl.BlockSpec((None, tr, cols), lambda i, k, c_ref, k_ref: (k, c_ref[0] * nb + i, 0)), spec],
        out_specs=[pl.BlockSpec((tr, cols), lambda i, k, c_ref, k_ref: (i, 0)), spec])
    return _pcall(
        body, name=name, grid_spec=grid_spec,
        out_shape=[jax.ShapeDtypeStruct((half, cols), F32), jax.ShapeDtypeStruct(theirs.shape, BF16)],
        compiler_params=_params(("arbitrary", "arbitrary"), 8 * _nbytes((tr, cols + LANES), F32)),
    )(core, chip, g, theirs)


def _add_chips(name, chip, p32, recv):
    half, cols = p32.shape
    tr = _tile(half, (256, 176, 128))

    def body(k_ref, p_ref, r0_ref, r1_ref, r2_ref, o_ref):
        o_ref[...] = ((p_ref[...] + r0_ref[...].astype(F32)) + r1_ref[...].astype(F32)) + r2_ref[...].astype(F32)

    def other(r):
        return pl.BlockSpec((None, tr, cols), lambda i, k_ref: (r + (k_ref[0] <= r).astype(jnp.int32), i, 0))
    grid_spec = pltpu.PrefetchScalarGridSpec(
        num_scalar_prefetch=1, grid=(half // tr,),
        in_specs=[pl.BlockSpec((tr, cols), lambda i, k_ref: (i, 0)), other(0), other(1), other(2)],
        out_specs=pl.BlockSpec((tr, cols), lambda i, k_ref: (i, 0)))
    return _pcall(
        body, name=name, grid_spec=grid_spec, out_shape=jax.ShapeDtypeStruct((half, cols), F32),
        compiler_params=_params(("arbitrary",), 10 * _nbytes((tr, cols + LANES), F32)),
    )(chip, p32, recv, recv, recv)


def kernel(x, mem, w_in, b_in, hg_lb_logits, hg_norm_w, ml_conv_w, ml_conv_b, ml_norm_w, w_out, ln1_g, ln1_b, ca_wq, ca_wkv, ca_wo, ln2_g, ln2_b, ffn_w_up, ffn_conv_w, ffn_conv_b, ffn_w_down, ln3_g, ln3_b, loss_target, m_w_in, m_b_in, m_hg_lb_logits, m_hg_norm_w, m_ml_conv_w, m_ml_conv_b, m_ml_norm_w, m_w_out, m_ln1_g, m_ln1_b, m_ca_wq, m_ca_wkv, m_ca_wo, m_ln2_g, m_ln2_b, m_ffn_w_up, m_ffn_conv_w, m_ffn_conv_b, m_ffn_w_down, m_ln3_g, m_ln3_b, v_w_in, v_b_in, v_hg_lb_logits, v_hg_norm_w, v_ml_conv_w, v_ml_conv_b, v_ml_norm_w, v_w_out, v_ln1_g, v_ln1_b, v_ca_wq, v_ca_wkv, v_ca_wo, v_ln2_g, v_ln2_b, v_ffn_w_up, v_ffn_conv_w, v_ffn_conv_b, v_ffn_w_down, v_ln3_g, v_ln3_b):
    return _train_step(dict(locals()))


WEIGHTS = ("w_in", "b_in", "hg_lb_logits", "hg_norm_w", "ml_conv_w", "ml_conv_b", "ml_norm_w", "w_out", "ln1_g",
           "ln1_b", "ca_wq", "ca_wkv", "ca_wo", "ln2_g", "ln2_b", "ffn_w_up", "ffn_conv_w", "ffn_conv_b",
           "ffn_w_down", "ln3_g", "ln3_b")
MATRICES = ("w_in", "w_out", "ca_wq", "ca_wkv", "ca_wo", "ffn_w_up", "ffn_w_down")
COL_SHARDED = ("w_in", "ca_wkv", "ffn_w_up", "ml_conv_w", "ffn_conv_w")
SMALL = tuple(n for n in WEIGHTS if n not in MATRICES)
PART_ROWS = 16


def _part_rows(shape):
    n = 1
    for s in shape:
        n *= s
    return -(-n // (LANES * PART_ROWS)) * PART_ROWS


def _pack(arrs, dtype):
    parts = []
    for a in arrs:
        flat = a.reshape(-1).astype(dtype)
        flat = jnp.pad(flat, (0, _part_rows(a.shape) * LANES - flat.shape[0]))
        parts.append(flat.reshape(-1, LANES))
    return jnp.concatenate(parts, axis=0)


def _unpack(buf, shapes):
    lead = buf.shape[:-2]
    outs, r = [], 0
    for sh in shapes:
        n = 1
        for s in sh:
            n *= s
        nr = _part_rows(sh)
        flat = buf[..., r:r + nr, :].reshape(lead + (nr * LANES,))
        outs.append(flat[..., :n].reshape(lead + tuple(sh)))
        r += nr
    return outs


def _cat_cols(s):
    return jnp.moveaxis(s, 0, 1).reshape(s.shape[1], -1)


def _stack_rows(s):
    return s.reshape(-1, s.shape[-1])


def _train_step(a):
    xs, mems, tgt = a["x"][0], a["mem"][0], a["loss_target"][0]
    core = lax.axis_index("c").astype(jnp.int32).reshape(1)
    chip = (2 * lax.axis_index("x") + lax.axis_index("y")).astype(jnp.int32).reshape(1)
    k_me = chip[0]
    shard = {n: a[n][0] for n in MATRICES}

    later = [n for n in MATRICES if n != "w_in"]
    w_in, taps = _gather_weights([shard["w_in"].astype(BF16), _pack([a["ml_conv_w"][0], a["ffn_conv_w"][0]], F32)])
    w = {"w_in": jnp.concatenate([*w_in, jnp.zeros((D_MODEL, D_IN_PAD - D_IN), BF16)], axis=1)}
    gathering, token = _gather_start("gather_start", [shard[n].astype(BF16) for n in later])
    ml_cw, ffn_cw = [_cat_cols(s) for s in _unpack(taps, [a["ml_conv_w"].shape[1:], a["ffn_conv_w"].shape[1:]])]
    b_in_p = jnp.pad(a["b_in"], ((0, 0), (0, D_IN_PAD - D_IN))) + token
    mixer_w = (a["hg_lb_logits"], a["hg_norm_w"], ml_cw, a["ml_conv_b"], a["ml_norm_w"])
    up_cols = a["ffn_w_up"].shape[-1]

    proj, xb = _mm("proj", "nn", xs, w["w_in"], bias=b_in_p, a_copy_dtype=BF16, tm=256, tn=D_IN_PAD)
    y, hst, cst, nst, mst = _mixer_fwd(proj, *mixer_w)
    w.update(zip(later, _forward_halves("forward_halves", _gather_wait("gather_wait", gathering, y))))
    for n in ("w_out", "ca_wq", "ca_wo", "ffn_w_down"):
        w[n] = _stack_rows(w[n])
    z1, x1, x1b = _mm("mix_out", "nn", y, w["w_out"], res=xs, res_scale=ALPHA, ln=("fwd", a["ln1_g"], a["ln1_b"]),
                      copy_dtype=BF16)
    q = _mm("ca_q", "nn", x1b, w["ca_wq"], out_dtype=BF16, tn=D_MODEL)
    kv = _mm("ca_kv", "nn", mems, w["ca_wkv"])
    o = _attn_fwd(q, kv)
    z2, x2, x2b = _mm("ca_out", "nn", o, w["ca_wo"], res=x1, res_scale=ALPHA, ln=("fwd", a["ln2_g"], a["ln2_b"]),
                      copy_dtype=BF16)
    w_up = w["ffn_w_up"]
    assert w_up.shape == (2 * FFN_J, D_MODEL, FFN_W)
    u, hmid, dz3, g_ln3g, g_ln3b, loss_part, dz3b = _ffn_fwd(
        x2b, x2, w_up, ffn_cw, a["ffn_conv_b"], w["ffn_w_down"], a["ln3_g"], a["ln3_b"], tgt)

    grads = {"ln3_g": g_ln3g, "ln3_b": g_ln3b}
    grads["ffn_w_down"] = _mm("g_w_down", "tn", hmid, dz3b, tm=D_FF // 2, tn=D_MODEL)
    du, g_cw, g_cb, dz2, grads["ln2_g"], grads["ln2_b"], dz2b = _ffn_bwd(
        u, ffn_cw, a["ffn_conv_b"], dz3b, dz3, w["ffn_w_down"], w_up, z2, a["ln2_g"], a["ln2_b"])
    grads["ffn_conv_w"] = jnp.transpose(g_cw, (2, 1, 0, 3)).reshape(FFN_CONV, 2 * D_FF)
    grads["ffn_conv_b"] = jnp.transpose(g_cb, (2, 1, 0, 3)).reshape(1, 2 * D_FF)
    grads["ffn_w_up"] = _mm("g_w_up", "tn", x2b, du, out_groups=N_CHIPS, tm=D_MODEL, tn=up_cols)
    grads["ffn_w_down"] = grads["ffn_w_down"].reshape((N_CHIPS,) + shard["ffn_w_down"].shape)
    pending = {}

    def reduce_start(tag, names, swapped=None):
        group = [grads[n] for n in names]
        group, theirs = swapped or (group, _swap_halves("swap_halves_" + tag, group))
        sums = [_add_pair("add_pair_" + n, core, chip, g, t) for n, g, t in zip(names, group, theirs)]
        handle, token = _scatter_start("scatter_start_" + tag, [s16 for _, s16 in sums])
        pending[tag] = (names, [s32 for s32, _ in sums], handle)
        return token

    ffn = ("ffn_w_up", "ffn_w_down")
    swapping, token = _swap_start("swap_start_ffn", [grads[n] for n in ffn])
    do = _mm("d_o", "nt", dz2b, w["ca_wo"], bias=jnp.zeros((1, D_MODEL), F32) + token, out_dtype=BF16,
             tn=D_MODEL)
    grads["ca_wo"] = _mm("g_wo", "tn", o, dz2b, tm=D_MODEL // 2, tn=D_MODEL)
    zero = reduce_start("ffn", ffn, _swap_wait("swap_wait_ffn", swapping, grads["ca_wo"]))
    dq, dkv = _attn_bwd(q, kv + zero, do)
    grads["ca_wq"] = _mm("g_wq", "tn", x1b, dq, tm=D_MODEL // 2, tn=D_MODEL)
    grads["ca_wkv"] = _mm("g_wkv", "tn", mems, dkv, out_groups=N_CHIPS, tm=D_MODEL)
    dz1, grads["ln1_g"], grads["ln1_b"], dz1b = _mm("d_x1", "nt", dq, w["ca_wq"], res=dz2, res_scale=ALPHA,
                                                    ln=("bwd", z1, a["ln1_g"], a["ln1_b"]), copy_dtype=BF16)
    grads["w_out"] = _mm("g_w_out", "tn", y, dz1b, tm=D_MODEL // 2, tn=D_MODEL)
    for n in ("w_out", "ca_wq", "ca_wo"):
        grads[n] = grads[n].reshape((N_CHIPS,) + shard[n].shape)
    attn = ("w_out", "ca_wq", "ca_wkv", "ca_wo")
    swapping, token = _swap_start("swap_start_attn", [grads[n] for n in attn])
    dy = _mm("d_y", "nt", dz1b, w["w_out"], bias=jnp.zeros((1, D_MODEL), F32) + token, tn=D_MODEL)
    zero = reduce_start("attn", attn, _swap_wait("swap_wait_attn", swapping, dy))
    (dproj, g_b_in, grads["hg_lb_logits"], grads["hg_norm_w"], grads["ml_conv_w"], grads["ml_conv_b"],
     grads["ml_norm_w"]) = _mixer_bwd(proj, dy, hst, cst, nst, mst, mixer_w[0], mixer_w[1] + zero, *mixer_w[2:])
    g_in = _mm("g_w_in", "tn", xb, dproj, tm=D_MODEL, tn=up_cols)
    in_cols = D_IN // N_CHIPS
    grads["w_in"] = jnp.stack([g_in[:, k * in_cols:(k + 1) * in_cols] for k in range(N_CHIPS)])
    grads["b_in"] = g_b_in[:, :D_IN]
    zero = reduce_start("in", ("w_in",))
    dx = _mm("d_x", "nt", dproj, w["w_in"], bias=jnp.zeros((1, D_MODEL), F32) + zero, res=dz1, res_scale=ALPHA,
             tm=256, tn=D_MODEL)

    halves = {}
    for tag, (names, sums32, handle) in pending.items():
        for n, s32, r in zip(names, sums32, _scatter_wait("scatter_wait_" + tag, handle, dx)):
            halves[n] = _add_chips("add_chips_" + n, chip, s32, r)
    halves = [halves[n] for n in MATRICES]

    small_shapes = [grads[n].shape for n in SMALL] + [loss_part.shape]
    other_halves, summed = _share_and_reduce(halves, _pack([grads[n] for n in SMALL] + [loss_part], F32))
    summed = _unpack(summed, small_shapes)
    loss = summed[-1][0, 0]
    for n, g in zip(SMALL, summed[:-1]):
        if n in COL_SHARDED:
            cols = a[n].shape[-1]
            g = lax.dynamic_slice_in_dim(g, k_me * cols, cols, axis=1)
        grads[n] = g

    delta, new_m, new_v = {}, {}, {}
    for n, mine, theirs in zip(MATRICES, halves, other_halves):
        grads[n], delta[n], new_m[n], new_v[n] = _adamw_halves(
            "adamw_" + n, core, shard[n], mine, theirs, a["m_" + n][0], a["v_" + n][0])
    small_w = [a[n][0] if a[n].ndim == 3 else a[n] for n in SMALL]
    small_m = [a["m_" + n][0] if a[n].ndim == 3 else a["m_" + n] for n in SMALL]
    small_v = [a["v_" + n][0] if a[n].ndim == 3 else a["v_" + n] for n in SMALL]
    for out, vals in zip((delta, new_m, new_v),
                         _adamw_many("adamw_small", small_w, [grads[n] for n in SMALL], small_m, small_v)):
        out.update(zip(SMALL, vals))

    def shaped(d):
        return [d[n].reshape(a[n].shape) for n in WEIGHTS]
    return (loss, dx[None], *shaped(grads), *shaped(delta), *shaped(new_m), *shaped(new_v))
```

```python
import functools

import jax
import jax.numpy as jnp
from jax import lax
from jax.experimental import pallas as pl
from jax.experimental.pallas import tpu as pltpu

F32 = jnp.float32
BF16 = jnp.bfloat16

D_MODEL = 1024
HEADS = 4
DK = 128
D_GRP = HEADS * DK
CHUNK = 64
ML_CONV = 4
FFN_CONV = 3
D_FF = 2816
CA_DH = D_MODEL // HEADS
DEPTH = 1
ALPHA = (2.0 * DEPTH) ** 0.25
LN_EPS = 1e-5
NEG_BIG = -1e30
D_IN = 8 * D_GRP + 2 * HEADS
D_IN_PAD = 8 * D_GRP + 128
ADAM_LR, ADAM_B1, ADAM_B2, ADAM_EPS, ADAM_WD, ADAM_STEP = 0.001, 0.9, 0.999, 1e-08, 0.01, 10

SUBLANES = 8
LANES = 128
VMEM_BYTES = 64 * 1024 * 1024


def _pcall(body, pin=True, **kw):
    if not pin:
        return _call(body, **kw)
    kw["out_shape"] = jax.tree.map(lambda s: pltpu.HBM(s.shape, s.dtype), kw["out_shape"])
    call = _call(body, **kw)

    def pinned(*args):
        return call(*[pltpu.with_memory_space_constraint(x, pltpu.HBM) if jnp.issubdtype(x.dtype, jnp.floating) else x
                      for x in args])
    return pinned


def _call(body, **kw):
    return pl.pallas_call(body, **kw)


def _params(semantics, vmem_bytes):
    limit = int(min(max(2 * vmem_bytes, 16 * 1024 * 1024), VMEM_BYTES - 8 * 1024 * 1024))
    return pltpu.CompilerParams(dimension_semantics=semantics, vmem_limit_bytes=limit)


def _nbytes(shape, dtype):
    n = 1
    for s in shape:
        n *= s
    return n * jnp.dtype(dtype).itemsize


def _dg(a, b, ca, cb):
    return lax.dot_general(a.astype(BF16), b.astype(BF16), (((ca,), (cb,)), ((), ())),
                           preferred_element_type=F32)


@jax.custom_vjp
def mm_nn(a, b):
    return _dg(a, b, 1, 0)


mm_nn.defvjp(lambda a, b: (_dg(a, b, 1, 0), (a, b)),
             lambda r, g: (_dg(g, r[1], 1, 1).astype(r[0].dtype), _dg(r[0], g, 0, 0).astype(r[1].dtype)))


@jax.custom_vjp
def mm_nt(a, b):
    return _dg(a, b, 1, 1)


mm_nt.defvjp(lambda a, b: (_dg(a, b, 1, 1), (a, b)),
             lambda r, g: (_dg(g, r[1], 1, 0).astype(r[0].dtype), _dg(g, r[0], 0, 0).astype(r[1].dtype)))


@jax.custom_vjp
def mm_tn(a, b):
    return _dg(a, b, 0, 0)


mm_tn.defvjp(lambda a, b: (_dg(a, b, 0, 0), (a, b)),
             lambda r, g: (_dg(r[1], g, 1, 1).astype(r[0].dtype), _dg(r[0], g, 1, 0).astype(r[1].dtype)))


def _tri(n, lower):
    r = lax.broadcasted_iota(jnp.int32, (n, n), 0)
    c = lax.broadcasted_iota(jnp.int32, (n, n), 1)
    return ((r >= c) if lower else (r <= c)).astype(F32)


def _tri_dot(lower, x):
    t = _tri(x.shape[0], lower).astype(BF16)
    hi = x.astype(BF16)
    rest = x - hi.astype(F32)
    mid = rest.astype(BF16)
    lo = (rest - mid.astype(F32)).astype(BF16)
    return sum(lax.dot_general(t, p, (((1,), (0,)), ((), ())), preferred_element_type=F32) for p in (hi, mid, lo))


@jax.custom_vjp
def cumsum_rows(x):
    return _tri_dot(True, x)


cumsum_rows.defvjp(lambda x: (_tri_dot(True, x), None), lambda _, g: (_tri_dot(False, g),))


def _shift_impl(halo, x, d):
    xx = jnp.concatenate([halo, x], axis=0)
    return pltpu.roll(xx, d, 0)[SUBLANES:]


@functools.partial(jax.custom_vjp, nondiff_argnums=(2,))
def shift_rows(halo, x, d):
    return _shift_impl(halo, x, d)


def _shift_bwd(d, _, g):
    n = g.shape[0] + SUBLANES
    gg = jnp.concatenate([jnp.zeros((SUBLANES, g.shape[1]), g.dtype), g], axis=0)
    r = pltpu.roll(gg, n - d, 0)
    return r[:SUBLANES], r[SUBLANES:]


shift_rows.defvjp(lambda halo, x, d: (_shift_impl(halo, x, d), None), _shift_bwd)


def causal_conv(halo, x, w_rows, b):
    k = len(w_rows)
    y = b + w_rows[k - 1] * x
    for d in range(1, k):
        y = y + w_rows[k - 1 - d] * shift_rows(halo, x, d)
    return y


def _sigmoid(x):
    return 1.0 / (1.0 + jnp.exp(-x))


def _silu(x):
    return x * _sigmoid(x)


def _log_sigmoid(x):
    return jnp.minimum(x, 0.0) - jnp.log(1.0 + jnp.exp(-jnp.abs(x)))


def _pick_row(x, i):
    row = lax.broadcasted_iota(jnp.int32, (x.shape[0], 1), 0)
    return jnp.sum(jnp.where(row == i, x, 0.0), axis=0, keepdims=True)


def _layer_norm(z, g, b):
    mu = jnp.mean(z, axis=-1, keepdims=True)
    zc = z - mu
    var = jnp.mean(zc * zc, axis=-1, keepdims=True)
    return zc * lax.rsqrt(var + LN_EPS) * g + b


def _qk_conv(halo, x, w0, w1, w2, w3, b):
    return _silu(causal_conv(halo, x, (w0, w1, w2, w3), b))


def _grp(i):
    return pl.ds(i * D_GRP, D_GRP)


def _mixer_specs(n_chunks, reverse):
    def chunk(c):
        return n_chunks - 1 - c if reverse else c
    row8 = CHUNK // SUBLANES
    proj_spec = pl.BlockSpec((CHUNK, D_IN_PAD), lambda c: (chunk(c), 0))
    halo_spec = pl.BlockSpec((SUBLANES, 2 * D_GRP), lambda c: (jnp.maximum(chunk(c) * row8 - 1, 0), 2))
    small = [pl.BlockSpec((2, D_GRP), lambda c: (0, 0)), pl.BlockSpec((1, D_GRP), lambda c: (0, 0)),
             pl.BlockSpec((ML_CONV, 2 * D_GRP), lambda c: (0, 0)), pl.BlockSpec((1, 2 * D_GRP), lambda c: (0, 0)),
             pl.BlockSpec((1, D_GRP), lambda c: (0, 0))]
    state_specs = [pl.BlockSpec((1, HEADS, DK, DK), lambda c: (chunk(c), 0, 0, 0)),
                   pl.BlockSpec((1, HEADS, DK, DK), lambda c: (chunk(c), 0, 0, 0)),
                   pl.BlockSpec((1, HEADS, 1, DK), lambda c: (chunk(c), 0, 0, 0)),
                   pl.BlockSpec((1, HEADS, 1, DK), lambda c: (chunk(c), 0, 0, 0))]
    y_spec = pl.BlockSpec((CHUNK, 2 * D_GRP), lambda c: (chunk(c), 0))
    return proj_spec, halo_spec, small, state_specs, y_spec, chunk


def _heads(x):
    return [x[:, h * DK:(h + 1) * DK] for h in range(HEADS)]


def _last(x, j):
    lane = lax.broadcasted_iota(jnp.int32, (1, x.shape[-1]), 1)
    return jnp.sum(jnp.where(lane == j, x, 0.0), axis=-1, keepdims=True)


def _hg_chunk(st_t, hq, hf, hi, hgate, l0, l1, nw):
    n = hq.shape[0]
    lb = _sigmoid(l0 - l1)
    q = _silu(hq)
    lf = jnp.log(lb + (1.0 - lb) * _sigmoid(hf))
    k = (1.0 - lb) * _sigmoid(-hf)
    b = cumsum_rows(lf)
    b_ref = _pick_row(b, n // 2 - 1)
    b_last = _pick_row(b, n - 1)
    qa, ka =_heads(q * jnp.exp(b - b_ref)), _heads(k * jnp.exp(b_ref - b))
    qe, kd, eb, v = _heads(q * jnp.exp(b)), _heads(k * jnp.exp(b_last - b)), _heads(jnp.exp(b_last)), _heads(hi)
    tri = _tri(n, True) > 0
    attn = [jnp.where(tri, mm_nt(qa[h], ka[h]), 0.0) for h in range(HEADS)]
    o = [mm_nn(attn[h], v[h]) + mm_nt(qe[h], st_t[h]) for h in range(HEADS)]
    st_new = jnp.stack([eb[h] * st_t[h] + mm_tn(v[h], kd[h]) for h in range(HEADS)])
    yn = [o[h] * lax.rsqrt(jnp.mean(o[h] * o[h], axis=-1, keepdims=True) + LN_EPS) for h in range(HEADS)]
    return st_new, jnp.concatenate(yn, axis=1) * nw * _silu(hgate)


def _ml_chunk(c_st, n_st, m_st, q, k, v, gates, og, nw):
    n = q.shape[0]
    ig = jnp.stack([_last(gates, h) for h in range(HEADS)])
    log_f = _log_sigmoid(gates)
    fl = jnp.stack([_last(log_f, HEADS + h) for h in range(HEADS)])
    bw = cumsum_rows(jnp.concatenate([jnp.broadcast_to(fl[h], (n, DK)) for h in range(HEADS)], axis=1))
    b = jnp.stack([_last(x, 0) for x in _heads(bw)])
    g = jnp.sum(fl, axis=1, keepdims=True)
    eye = lax.broadcasted_iota(jnp.int32, (n, n), 0) == lax.broadcasted_iota(jnp.int32, (n, n), 1)
    e_row = jnp.sum(jnp.where(eye, ig - b, 0.0), axis=1, keepdims=True)
    d = jnp.where(_tri(n, True) > 0, b + e_row, -jnp.inf)
    inter = b + m_st
    m_t = jnp.maximum(inter, jnp.max(d, axis=2, keepdims=True))
    qs, kh, vh = _heads(q * (DK ** -0.5)), _heads(k), _heads(v)
    s = jnp.stack([mm_nt(qs[h], kh[h]) for h in range(HEADS)]) * jnp.exp(d - m_t)
    w_inter = jnp.exp(inter - m_t)
    num = (jnp.stack([mm_nn(s[h], vh[h]) for h in range(HEADS)])
           + w_inter * jnp.stack([mm_nn(qs[h], c_st[h]) for h in range(HEADS)]))
    den = jnp.sum(s, axis=2, keepdims=True) + w_inter * jnp.sum(jnp.stack(qs) * n_st, axis=2, keepdims=True)
    h_out = num / jnp.maximum(jnp.abs(den), jnp.exp(-m_t))
    a = g - b + ig
    m_new = jnp.maximum(g + m_st, jnp.max(a, axis=1, keepdims=True))
    decay = jnp.exp(g + m_st - m_new)
    wk = jnp.stack(kh) * jnp.exp(a - m_new)
    c_new = decay * c_st + jnp.stack([mm_tn(wk[h], vh[h]) for h in range(HEADS)])
    n_new = decay * n_st + jnp.sum(wk, axis=1, keepdims=True)
    hc = h_out - jnp.mean(h_out, axis=-1, keepdims=True)
    yn = hc * lax.rsqrt(jnp.mean(hc * hc, axis=-1, keepdims=True) + LN_EPS)
    y = _sigmoid(og) * (jnp.concatenate([yn[h] for h in range(HEADS)], axis=1) * nw)
    return c_new, n_new, m_new, y


def _mixer_inputs(proj_ref, lg_ref, hnw_ref, mnw_ref, qk):
    hg_in = (proj_ref[:, _grp(0)], proj_ref[:, _grp(1)], proj_ref[:, _grp(2)], proj_ref[:, _grp(3)],
             lg_ref[0:1, :], lg_ref[1:2, :], hnw_ref[...])
    ml_in = (qk[:, :D_GRP], qk[:, D_GRP:], proj_ref[:, _grp(6)], proj_ref[:, pl.ds(8 * D_GRP, LANES)],
             proj_ref[:, _grp(7)], mnw_ref[...])
    return hg_in, ml_in


def _mixer_fwd(proj, lb_logits, hg_nw, conv_w, conv_b, ml_nw):
    seq = proj.shape[0]
    n_chunks = seq // CHUNK
    proj_spec, halo_spec, small, state_specs, y_spec, _ = _mixer_specs(n_chunks, False)

    def body(proj_ref, halo_ref, lg_ref, hnw_ref, cw_ref, cb_ref, mnw_ref,
             y_ref, hst_ref, cst_ref, nst_ref, mst_ref, hs, cs, ns, ms):
        c = pl.program_id(0)

        @pl.when(c == 0)
        def _():
            hs[...] = jnp.zeros_like(hs)
            cs[...] = jnp.zeros_like(cs)
            ns[...] = jnp.zeros_like(ns)
            ms[...] = jnp.full(ms.shape, NEG_BIG, F32)

        hst_ref[0] = hs[...]
        cst_ref[0] = cs[...]
        nst_ref[0] = ns[...]
        mst_ref[0] = ms[...]
        halo = jnp.where(c > 0, halo_ref[...], 0.0)
        qk = _qk_conv(halo, proj_ref[:, pl.ds(4 * D_GRP, 2 * D_GRP)],
                      cw_ref[0:1, :], cw_ref[1:2, :], cw_ref[2:3, :], cw_ref[3:4, :], cb_ref[...])
        hg_in, ml_in = _mixer_inputs(proj_ref, lg_ref, hnw_ref, mnw_ref, qk)
        hs[...], y_hg = _hg_chunk(hs[...], *hg_in)
        cs[...], ns[...], m_new, y_ml = _ml_chunk(cs[...], ns[...], _last(ms[...], 0), *ml_in)
        ms[...] = jnp.broadcast_to(m_new, ms.shape)
        y_ref[:, pl.ds(0, D_GRP)] = y_hg.astype(BF16)
        y_ref[:, pl.ds(D_GRP, D_GRP)] = y_ml.astype(BF16)

    st = jax.ShapeDtypeStruct((n_chunks, HEADS, DK, DK), F32)
    vec = jax.ShapeDtypeStruct((n_chunks, HEADS, 1, DK), F32)
    vmem = 2 * (_nbytes((CHUNK, D_IN_PAD), F32) + _nbytes((CHUNK, 2 * D_GRP), F32) + 2 * _nbytes((HEADS, DK, DK), F32)) \
        + 2 * _nbytes((HEADS, DK, DK), F32)
    return _pcall(
        body, name="mixer_fwd", grid=(n_chunks,),
        in_specs=[proj_spec, halo_spec] + small,
        out_specs=[y_spec] + state_specs,
        out_shape=[jax.ShapeDtypeStruct((seq, 2 * D_GRP), BF16), st, st, vec, vec],
        scratch_shapes=[pltpu.VMEM((HEADS, DK, DK), F32), pltpu.VMEM((HEADS, DK, DK), F32),
                        pltpu.VMEM((HEADS, 1, DK), F32), pltpu.VMEM((HEADS, 1, DK), F32)],
        compiler_params=_params(("arbitrary",), vmem),
    )(proj, proj, lb_logits, hg_nw, conv_w, conv_b, ml_nw)


def _mixer_bwd(proj, dy, hst, cst, nst, mst, lb_logits, hg_nw, conv_w, conv_b, ml_nw):
    seq = proj.shape[0]
    n_chunks = seq // CHUNK
    proj_spec, halo_spec, small, state_specs, y_spec, _ = _mixer_specs(n_chunks, True)

    def body(proj_ref, halo_ref, dy_ref, hst_ref, cst_ref, nst_ref, mst_ref,
             lg_ref, hnw_ref, cw_ref, cb_ref, mnw_ref,
             dproj_ref, dbin_ref, dlg_ref, dhnw_ref, dcw_ref, dcb_ref, dmnw_ref,
             dhs, dcs, dns, dms, dhalo):
        c = pl.program_id(0)

        @pl.when(c == 0)
        def _():
            for r in (dhs, dcs, dns, dms, dhalo, dbin_ref, dlg_ref, dhnw_ref, dcw_ref, dcb_ref, dmnw_ref):
                r[...] = jnp.zeros_like(r)

        def put(cols, val):
            dproj_ref[:, cols] = val.astype(BF16)
            dbin_ref[:, cols] += jnp.sum(val, axis=0, keepdims=True)

        first = c == n_chunks - 1
        halo = jnp.where(first, 0.0, halo_ref[...])
        x_qk = proj_ref[:, pl.ds(4 * D_GRP, 2 * D_GRP)]
        conv_args = (halo, x_qk, cw_ref[0:1, :], cw_ref[1:2, :], cw_ref[2:3, :], cw_ref[3:4, :], cb_ref[...])
        qk, conv_vjp = jax.vjp(_qk_conv, *conv_args)
        hg_in, ml_in = _mixer_inputs(proj_ref, lg_ref, hnw_ref, mnw_ref, qk)
        _, hg_vjp = jax.vjp(_hg_chunk, hst_ref[0], *hg_in)
        _, ml_vjp = jax.vjp(_ml_chunk, cst_ref[0], nst_ref[0], _last(mst_ref[0], 0), *ml_in)
        dst, dhq, dhf, dhi, dhg, dl0, dl1, dnw = hg_vjp((dhs[...], dy_ref[:, pl.ds(0, D_GRP)]))
        dc, dn, dm, dq, dk, dv, dgates, dog, dmn = ml_vjp(
            (dcs[...], dns[...], _last(dms[...], 0), dy_ref[:, pl.ds(D_GRP, D_GRP)]))
        dhs[...] = dst
        dcs[...] = dc
        dns[...] = dn
        dms[...] = jnp.broadcast_to(dm, dms.shape)
        for i, val in ((0, dhq), (1, dhf), (2, dhi), (3, dhg), (6, dv), (7, dog)):
            put(_grp(i), val)
        put(pl.ds(8 * D_GRP, LANES), dgates)
        dlg_ref[0:1, :] += dl0
        dlg_ref[1:2, :] += dl1
        dhnw_ref[...] += dnw
        dmnw_ref[...] += dmn
        dh, dx, dw0, dw1, dw2, dw3, db = conv_vjp(jnp.concatenate([dq, dk], axis=1))
        tail = jnp.concatenate([jnp.zeros((CHUNK - SUBLANES, 2 * D_GRP), F32), dhalo[...]], axis=0)
        put(pl.ds(4 * D_GRP, 2 * D_GRP), dx + tail)
        dhalo[...] = dh
        for d, dw in enumerate((dw0, dw1, dw2, dw3)):
            dcw_ref[d:d + 1, :] += dw
        dcb_ref[...] += db

    row = pl.BlockSpec((1, D_GRP), lambda c: (0, 0))
    small_out = [pl.BlockSpec((1, D_IN_PAD), lambda c: (0, 0)), pl.BlockSpec((2, D_GRP), lambda c: (0, 0)), row,
                 pl.BlockSpec((ML_CONV, 2 * D_GRP), lambda c: (0, 0)), pl.BlockSpec((1, 2 * D_GRP), lambda c: (0, 0)), row]
    dy_spec = pl.BlockSpec((CHUNK, 2 * D_GRP), y_spec.index_map)
    vmem = 2 * (2 * _nbytes((CHUNK, D_IN_PAD), F32) + _nbytes((CHUNK, 2 * D_GRP), F32)
                + 2 * _nbytes((HEADS, DK, DK), F32)) + 2 * _nbytes((HEADS, DK, DK), F32) + 4 * 1024 * 1024
    return _pcall(
        body, name="mixer_bwd", grid=(n_chunks,),
        in_specs=[proj_spec, halo_spec, dy_spec] + state_specs + small,
        out_specs=[proj_spec] + small_out,
        out_shape=[jax.ShapeDtypeStruct((seq, D_IN_PAD), BF16), jax.ShapeDtypeStruct((1, D_IN_PAD), F32),
                   jax.ShapeDtypeStruct((2, D_GRP), F32), jax.ShapeDtypeStruct((1, D_GRP), F32),
                   jax.ShapeDtypeStruct((ML_CONV, 2 * D_GRP), F32), jax.ShapeDtypeStruct((1, 2 * D_GRP), F32),
                   jax.ShapeDtypeStruct((1, D_GRP), F32)],
        scratch_shapes=[pltpu.VMEM((HEADS, DK, DK), F32), pltpu.VMEM((HEADS, DK, DK), F32),
                        pltpu.VMEM((HEADS, 1, DK), F32), pltpu.VMEM((HEADS, 1, DK), F32),
                        pltpu.VMEM((SUBLANES, 2 * D_GRP), F32)],
        compiler_params=_params(("arbitrary",), vmem),
    )(proj, proj, dy, hst, cst, nst, mst, lb_logits, hg_nw, conv_w, conv_b, ml_nw)


def _tile(n, prefs, unit=None):
    unit = unit or n
    for p in prefs:
        if unit % p == 0 and n % p == 0:
            return p
    return unit


def _logical(arr):
    return arr.shape if arr.ndim == 2 else (arr.shape[1], arr.shape[0] * arr.shape[2])


def _group(arr):
    return arr.shape[-1]


def _split_spec(ndim, group, tr, tc, where):
    if ndim == 2:
        return pl.BlockSpec((tr, tc), where)
    per = group // tc
    assert per * tc == group, (group, tc)

    def index(*ids):
        bi, bj = where(*ids)
        return (bj // per, bi, bj % per)
    return pl.BlockSpec((None, tr, tc), index)


def _mm(name, mode, a, b, *, bias=None, res=None, res_scale=1.0, ln=None, out_dtype=F32, out_groups=None,
        copy_dtype=None, a_copy_dtype=None, tm=None, tn=None, tk=None):
    la, lb = _logical(a), _logical(b)
    if mode == "nn":
        (m, k), n = la, lb[1]
        n_unit = _group(b) if b.ndim == 3 else n
        kc = _group(a) if a.ndim == 3 else k
    elif mode == "nt":
        (m, k), n = la, lb[0]
        n_unit = n
        kc = min(_group(a) if a.ndim == 3 else k, _group(b) if b.ndim == 3 else k)
    else:
        (k, m), n = la, lb[1]
        n_unit, kc = (_group(b) if b.ndim == 3 else n), k
        assert a.ndim == 2
    if out_groups:
        n_unit = min(n_unit, n // out_groups)
    kind = ln[0] if ln else None
    tm = tm or (256 if ln else _tile(m, (512, 256, 128)))
    tn = n if ln else (tn or _tile(n, (512, 384, 256, 128), n_unit))
    if mode != "tn":
        tk = k
    elif tk is None:
        tk = _tile(k, (4096, 2048, 512, 256, 128) if (m // tm) * (n // tn) > 1 else (2048, 512, 256, 128))
    gi, gj, gk = m // tm, n // tn, k // tk
    assert gi * tm == m and gj * tn == n and gk * tk == k and n_unit % tn == 0, (name, m, n, k, tm, tn, tk)
    ca, cb = {"nn": (1, 0), "nt": (1, 1), "tn": (0, 0)}[mode]
    i_outer = gk > 1 or (gi - 1) * _nbytes(b.shape, b.dtype) <= (gj - 1) * _nbytes(a.shape, a.dtype)

    def ij(where):
        return (lambda p, q, kk: where(p, q, kk)) if i_outer else (lambda p, q, kk: where(q, p, kk))
    if mode == "tn":
        a_spec = pl.BlockSpec((tk, tm), ij(lambda i, j, kk: (kk, i)))
    elif a.ndim == 3:
        a_spec = pl.BlockSpec((a.shape[0], tm, _group(a)), ij(lambda i, j, kk: (0, i, 0)))
    else:
        a_spec = pl.BlockSpec((tm, k), ij(lambda i, j, kk: (i, 0)))
    if mode != "nt":
        b_spec = _split_spec(b.ndim, _group(b), tk, tn, ij(lambda i, j, kk: (kk, j)))
    elif b.ndim == 3:
        b_spec = pl.BlockSpec((b.shape[0], tn, _group(b)), ij(lambda i, j, kk: (0, j, 0)))
    else:
        b_spec = pl.BlockSpec((tn, k), ij(lambda i, j, kk: (j, 0)))
    row_spec = pl.BlockSpec((1, tn), ij(lambda i, j, kk: (0, j)))
    blk_spec = pl.BlockSpec((tm, tn), ij(lambda i, j, kk: (i, j)))
    ins, in_specs = [a, b], [a_spec, b_spec]
    if bias is not None:
        ins.append(bias), in_specs.append(row_spec)
    if res is not None:
        ins.append(res), in_specs.append(blk_spec)
    if kind == "fwd":
        ins += [ln[1], ln[2]]
        in_specs += [row_spec, row_spec]
    elif kind == "bwd":
        ins += [ln[1], ln[2], ln[3]]
        in_specs += [blk_spec, row_spec, row_spec]
    if out_groups:
        blk_out = jax.ShapeDtypeStruct((out_groups, m, n // out_groups), out_dtype)
        out_spec = _split_spec(3, n // out_groups, tm, tn, ij(lambda i, j, kk: (i, j)))
    else:
        blk_out, out_spec = jax.ShapeDtypeStruct((m, n), out_dtype), blk_spec
    row_out = jax.ShapeDtypeStruct((1, n), F32)
    if kind is None:
        out_shape, out_specs = [blk_out], [out_spec]
    elif kind == "fwd":
        out_shape, out_specs = [blk_out, blk_out], [blk_spec, blk_spec]
    else:
        out_shape, out_specs = [blk_out, row_out, row_out], [blk_spec, row_spec, row_spec]
    if copy_dtype is not None:
        out_shape.append(jax.ShapeDtypeStruct((m, n), copy_dtype))
        out_specs.append(blk_spec)
    if a_copy_dtype is not None:
        assert mode != "tn" and a.ndim == 2 and copy_dtype is None
        out_shape.append(jax.ShapeDtypeStruct((m, k), a_copy_dtype))
        out_specs.append(a_spec)
    n_in = len(ins)

    def body(*refs):
        in_refs, out_refs, acc_ref = refs[:n_in], refs[n_in:n_in + len(out_shape)], refs[-1]
        i, kk = pl.program_id(0 if i_outer else 1), pl.program_id(2)
        a_ref, b_ref = in_refs[:2]
        extra = list(in_refs[2:])
        if a_copy_dtype is not None:
            out_refs[-1][...] = a_ref[...].astype(a_copy_dtype)

        def epilogue(acc):
            rest = list(extra)
            if bias is not None:
                acc = acc + rest.pop(0)[...]
            if res is not None:
                acc = acc + res_scale * rest.pop(0)[...]
            if kind is None:
                out_refs[0][...] = acc.astype(out_dtype)
                return
            if kind == "fwd":
                out_refs[0][...] = acc
                y = _layer_norm(acc, rest[0][...], rest[1][...])
                out_refs[1][...] = y
                if copy_dtype is not None:
                    out_refs[-1][...] = y.astype(copy_dtype)
                return
            _, vjp = jax.vjp(_layer_norm, rest[0][...], rest[1][...], rest[2][...])
            dz, dg, db = vjp(acc)
            out_refs[0][...] = dz
            out_refs[1][...] += dg
            out_refs[2][...] += db
            if copy_dtype is not None:
                out_refs[-1][...] = dz.astype(copy_dtype)

        if kind == "bwd":
            @pl.when((i == 0) & (kk == 0))
            def _():
                out_refs[1][...] = jnp.zeros_like(out_refs[1])
                out_refs[2][...] = jnp.zeros_like(out_refs[2])

        def chunk(ref, c0, last):
            if ref.ndim == 3:
                g = ref.shape[2]
                return ref[c0 // g, :, pl.ds(c0 % g, kc)]
            return ref[:, pl.ds(c0, kc)] if last else ref[pl.ds(c0, kc), :]

        if mode == "tn" or kc == k:
            prod = _dg(a_ref[...], b_ref[...], ca, cb)
        else:
            prod = None
            for c0 in range(0, k, kc):
                part = _dg(chunk(a_ref, c0, True), chunk(b_ref, c0, mode == "nt"), ca, cb)
                prod = part if prod is None else prod + part
        if gk == 1:
            epilogue(prod)
            return

        @pl.when(kk == 0)
        def _():
            acc_ref[...] = prod

        @pl.when(kk > 0)
        def _():
            acc_ref[...] += prod

        @pl.when(kk == gk - 1)
        def _():
            epilogue(acc_ref[...])

    vmem = (2 * (_nbytes((tm, tk), a.dtype) + _nbytes((tk, tn), b.dtype))
            + (2 * len(ins) + 2 * len(out_shape) + 1) * _nbytes((tm, tn), F32))
    outs = _pcall(
        body, name=name, grid=(gi, gj, gk) if i_outer else (gj, gi, gk), in_specs=in_specs, out_specs=out_specs,
        out_shape=out_shape, scratch_shapes=[pltpu.VMEM((tm, tn) if gk > 1 else (SUBLANES, LANES), F32)],
        compiler_params=_params(("arbitrary", "arbitrary", "arbitrary"), vmem),
    )(*ins)
    return outs[0] if len(out_shape) == 1 else outs


STREAM_ROWS = 512


def _mm_tn_streamed(name, a, b):
    k, m = a.shape
    n = b.shape[1]
    nk = k // STREAM_ROWS
    assert nk * STREAM_ROWS == k and b.shape[0] == k

    def body(a_hbm, b_hbm, o_hbm, a_vm, b_vm, o_vm, sem, out_sem):
        def copies(i):
            rows = pl.ds(i * STREAM_ROWS, STREAM_ROWS)
            return (pltpu.make_async_copy(a_hbm.at[rows], a_vm.at[rows], sem.at[0, i]),
                    pltpu.make_async_copy(b_hbm.at[rows], b_vm.at[rows], sem.at[1, i]))

        for i in range(nk):
            for cp in copies(i):
                cp.start()
        for i in range(nk):
            for cp in copies(i):
                cp.wait()
            rows = pl.ds(i * STREAM_ROWS, STREAM_ROWS)
            prod = _dg(a_vm[rows, :], b_vm[rows, :], 0, 0)
            if i == 0:
                o_vm[...] = prod
            else:
                o_vm[...] += prod
        out = pltpu.make_async_copy(o_vm, o_hbm, out_sem.at[0])
        out.start()
        out.wait()

    vmem = _nbytes(a.shape, a.dtype) + _nbytes(b.shape, b.dtype) + 2 * _nbytes((m, n), F32)
    return _pcall(
        body, name=name, in_specs=[ANY, ANY], out_specs=ANY, out_shape=jax.ShapeDtypeStruct((m, n), F32),
        scratch_shapes=[pltpu.VMEM(a.shape, a.dtype), pltpu.VMEM(b.shape, b.dtype), pltpu.VMEM((m, n), F32),
                        pltpu.SemaphoreType.DMA((2, nk)), pltpu.SemaphoreType.DMA((1,))],
        compiler_params=pltpu.CompilerParams(vmem_limit_bytes=int(vmem + 8 * 1024 * 1024)),
    )(a, b)


def _attn_head(q, k, v):
    sc = mm_nt(q, k) * (CA_DH ** -0.5)
    e = jnp.exp(sc - jnp.max(sc, axis=-1, keepdims=True))
    return mm_nn(e / jnp.sum(e, axis=-1, keepdims=True), v)


def _attn_fwd(q, kv):
    seq, n_mem = q.shape[0], kv.shape[0]
    tq = _tile(seq, (512, 256, 128))

    def body(q_ref, kv_ref, o_ref):
        for h in range(HEADS):
            hd = pl.ds(h * CA_DH, CA_DH)
            o = _attn_head(q_ref[:, hd], kv_ref[:, hd], kv_ref[:, pl.ds(D_MODEL + h * CA_DH, CA_DH)])
            o_ref[:, hd] = o.astype(BF16)

    return _pcall(
        body, name="attn_fwd", grid=(seq // tq,),
        in_specs=[pl.BlockSpec((tq, D_MODEL), lambda i: (i, 0)), pl.BlockSpec((n_mem, 2 * D_MODEL), lambda i: (0, 0))],
        out_specs=pl.BlockSpec((tq, D_MODEL), lambda i: (i, 0)), out_shape=jax.ShapeDtypeStruct((seq, D_MODEL), BF16),
        compiler_params=_params(("arbitrary",), 4 * _nbytes((tq, D_MODEL), F32) + 2 * _nbytes((n_mem, 2 * D_MODEL), F32)),
    )(q, kv)


def _attn_bwd(q, kv, do):
    seq, n_mem = q.shape[0], kv.shape[0]
    tq = _tile(seq, (512, 256, 128))

    def body(q_ref, kv_ref, do_ref, dq_ref, dkv_ref):
        @pl.when(pl.program_id(0) == 0)
        def _():
            dkv_ref[...] = jnp.zeros_like(dkv_ref)

        for h in range(HEADS):
            hd = pl.ds(h * CA_DH, CA_DH)
            vd = pl.ds(D_MODEL + h * CA_DH, CA_DH)
            _, vjp = jax.vjp(_attn_head, q_ref[:, hd], kv_ref[:, hd], kv_ref[:, vd])
            dq, dk, dv = vjp(do_ref[:, hd].astype(F32))
            dq_ref[:, hd] = dq.astype(BF16)
            dkv_ref[:, hd] += dk
            dkv_ref[:, vd] += dv

    return _pcall(
        body, name="attn_bwd", grid=(seq // tq,),
        in_specs=[pl.BlockSpec((tq, D_MODEL), lambda i: (i, 0)), pl.BlockSpec((n_mem, 2 * D_MODEL), lambda i: (0, 0)),
                  pl.BlockSpec((tq, D_MODEL), lambda i: (i, 0))],
        out_specs=[pl.BlockSpec((tq, D_MODEL), lambda i: (i, 0)), pl.BlockSpec((n_mem, 2 * D_MODEL), lambda i: (0, 0))],
        out_shape=[jax.ShapeDtypeStruct((seq, D_MODEL), BF16), jax.ShapeDtypeStruct((n_mem, 2 * D_MODEL), F32)],
        compiler_params=_params(("arbitrary",), 6 * _nbytes((tq, D_MODEL), F32) + 4 * _nbytes((n_mem, 2 * D_MODEL), F32)),
    )(q, kv, do)


def _ffn_mid(hg, xg, hv, xv, wg0, wg1, wg2, bg, wv0, wv1, wv2, bv):
    return jax.nn.gelu(causal_conv(hg, xg, (wg0, wg1, wg2), bg)) * causal_conv(hv, xv, (wv0, wv1, wv2), bv)


FFN_TB = 256
FFN_W = D_FF // 2
FFN_J = D_FF // FFN_W
MXU_COLS = 256
FFN_PIECES = tuple((off, min(MXU_COLS, FFN_W - off)) for off in range(0, FFN_W, MXU_COLS))


def _ffn_common_specs(seq, row):
    tb = min(FFN_TB, seq)
    full = pl.BlockSpec((tb, D_MODEL), lambda t, j: (row(t), 0))
    vec = pl.BlockSpec((1, D_MODEL), lambda t, j: (0, 0))
    halves = []
    for off in (0, FFN_J):
        halves.append(dict(
            w_up=pl.BlockSpec((None, D_MODEL, FFN_W), lambda t, j, off=off: (j + off, 0, 0)),
            taps=pl.BlockSpec((FFN_CONV, FFN_W), lambda t, j, off=off: (0, j + off)),
            bias=pl.BlockSpec((1, FFN_W), lambda t, j, off=off: (0, j + off))))
    w_down = pl.BlockSpec((FFN_W, D_MODEL), lambda t, j: (j, 0))
    u_blk = pl.BlockSpec((2, tb, FFN_W), lambda t, j: (0, row(t), j))
    return tb, full, vec, halves, w_down, u_blk


def _ffn_vmem(tb):
    return (_nbytes((2, tb, FFN_W), F32) + _nbytes((2, tb, FFN_W), BF16) + 3 * _nbytes((D_MODEL, FFN_W), BF16)
            + 10 * _nbytes((tb, D_MODEL), F32))


def _conv_params(taps_ref, bias_ref, cols):
    return taps_ref[0:1, cols], taps_ref[1:2, cols], taps_ref[2:3, cols], bias_ref[:, cols]


def _ffn_fwd(x2b, x2, w_up, conv_w, conv_b, w_down, ln_g, ln_b, target):
    seq = x2.shape[0]
    tb, full, vec, halves, wd_spec, u_blk = _ffn_common_specs(seq, lambda t: t)
    nt = seq // tb

    def body(xb_ref, wg_ref, wv_ref, tg_ref, tv_ref, bg_ref, bv_ref, wd_ref, x_ref, g_ref, b_ref, tgt_ref,
             u_ref, h_ref, dz_ref, dg_ref, db_ref, loss_ref, dzb_ref, acc, carry):
        t, j = pl.program_id(0), pl.program_id(1)
        xb = xb_ref[...]
        pieces = [pl.ds(off, width) for off, width in FFN_PIECES]
        ug = [_dg(xb, wg_ref[:, cols], 1, 0) for cols in pieces]
        uv = [_dg(xb, wv_ref[:, cols], 1, 0) for cols in pieces]
        hs = []
        for cols, g, v in zip(pieces, ug, uv):
            u_ref[0, :, cols] = g
            u_ref[1, :, cols] = v
            halo_g = jnp.where(t == 0, 0.0, carry[j, 0, :, cols])
            halo_v = jnp.where(t == 0, 0.0, carry[j, 1, :, cols])
            h = _ffn_mid(halo_g, g, halo_v, v, *_conv_params(tg_ref, bg_ref, cols),
                         *_conv_params(tv_ref, bv_ref, cols)).astype(BF16)
            carry[j, 0, :, cols] = g[tb - SUBLANES:, :]
            carry[j, 1, :, cols] = v[tb - SUBLANES:, :]
            h_ref[:, cols] = h
            hs.append(h)
        part = None
        for cols, h in zip(pieces, hs):
            p = _dg(h, wd_ref[cols, :], 1, 0)
            part = p if part is None else part + p

        @pl.when(j == 0)
        def _():
            acc[...] = part

        @pl.when(j > 0)
        def _():
            acc[...] += part

        @pl.when(j == FFN_J - 1)
        def _():
            y, vjp = jax.vjp(_layer_norm, acc[...] + ALPHA * x_ref[...], g_ref[...], b_ref[...])
            err = y - tgt_ref[...]
            part_loss = 0.5 * jnp.sum(jnp.sum(err * err, axis=1, keepdims=True), axis=0, keepdims=True) / D_MODEL
            dz, dg, db = vjp(err / D_MODEL)

            @pl.when(t == 0)
            def _():
                for r in (dg_ref, db_ref, loss_ref):
                    r[...] = jnp.zeros_like(r)

            dz_ref[...] = dz
            dzb_ref[...] = dz.astype(BF16)
            dg_ref[...] += dg
            db_ref[...] += db
            loss_ref[...] += jnp.broadcast_to(part_loss, (1, LANES))

    h0, h1 = halves
    row = jax.ShapeDtypeStruct((1, D_MODEL), F32)
    return _pcall(
        body, name="ffn_fwd", grid=(nt, FFN_J),
        in_specs=[full, h0["w_up"], h1["w_up"], h0["taps"], h1["taps"], h0["bias"], h1["bias"], wd_spec, full, vec, vec,
                  full],
        out_specs=[u_blk, pl.BlockSpec((tb, FFN_W), lambda t, j: (t, j)), full, vec, vec,
                   pl.BlockSpec((1, LANES), lambda t, j: (0, 0)), full],
        out_shape=[jax.ShapeDtypeStruct((2, seq, D_FF), F32), jax.ShapeDtypeStruct((seq, D_FF), BF16),
                   jax.ShapeDtypeStruct((seq, D_MODEL), F32), row, row, jax.ShapeDtypeStruct((1, LANES), F32),
                   jax.ShapeDtypeStruct((seq, D_MODEL), BF16)],
        scratch_shapes=[pltpu.VMEM((tb, D_MODEL), F32), pltpu.VMEM((FFN_J, 2, SUBLANES, FFN_W), F32)],
        compiler_params=_params(("arbitrary", "arbitrary"), _ffn_vmem(tb)),
    )(x2b, w_up, w_up, conv_w, conv_w, conv_b, conv_b, w_down, x2, ln_g, ln_b, target)


def _ffn_bwd(u, conv_w, conv_b, dz3b, dz3, w_down, w_up, z2, ln_g, ln_b):
    seq = dz3.shape[0]
    tb = min(FFN_TB, seq)
    nt = seq // tb
    row8 = tb // SUBLANES
    tb, full, vec, halves, wd_spec, u_blk = _ffn_common_specs(seq, lambda t: nt - 1 - t)
    halo = pl.BlockSpec((2, SUBLANES, FFN_W), lambda t, j: (0, jnp.maximum((nt - 1 - t) * row8 - 1, 0), j))

    def body(u_ref, halo_ref, tg_ref, tv_ref, bg_ref, bv_ref, dzb_ref, wd_ref, wg_ref, wv_ref, dz3_ref, z_ref, g_ref,
             b_ref, du_ref, dw_ref, dbias_ref, dz_ref, dg_ref, db_ref, dz2b_ref, acc, carry):
        t, j = pl.program_id(0), pl.program_id(1)

        @pl.when((t == 0) & (j == 0))
        def _():
            for r in (dw_ref, dbias_ref, dg_ref, db_ref):
                r[...] = jnp.zeros_like(r)

        pieces = [pl.ds(off, width) for off, width in FFN_PIECES]
        dzb = dzb_ref[...]
        dhs = [_dg(dzb, wd_ref[cols, :], 1, 1) for cols in pieces]
        first = t == nt - 1
        dus = []
        for cols, dh in zip(pieces, dhs):
            args = (jnp.where(first, 0.0, halo_ref[0, :, cols]), u_ref[0, :, cols],
                    jnp.where(first, 0.0, halo_ref[1, :, cols]), u_ref[1, :, cols],
                    *_conv_params(tg_ref, bg_ref, cols), *_conv_params(tv_ref, bv_ref, cols))
            _, vjp = jax.vjp(_ffn_mid, *args)
            dhg, dxg, dhv, dxv, g0, g1, g2, gb, v0, v1, v2, vb = vjp(dh)
            zeros = jnp.zeros((tb - SUBLANES, dh.shape[1]), F32)
            dug = (dxg + jnp.concatenate([zeros, jnp.where(t == 0, 0.0, carry[j, 0, :, cols])], axis=0)).astype(BF16)
            duv = (dxv + jnp.concatenate([zeros, jnp.where(t == 0, 0.0, carry[j, 1, :, cols])], axis=0)).astype(BF16)
            carry[j, 0, :, cols] = dhg
            carry[j, 1, :, cols] = dhv
            du_ref[0, :, cols] = dug
            du_ref[1, :, cols] = duv
            for half, parts in enumerate(((g0, g1, g2), (v0, v1, v2))):
                for d, p in enumerate(parts):
                    dw_ref[j, half, d:d + 1, cols] += p
            dbias_ref[j, 0, :, cols] += gb
            dbias_ref[j, 1, :, cols] += vb
            dus.append((dug, duv))
        part = None
        for cols, (dug, duv) in zip(pieces, dus):
            p = _dg(dug, wg_ref[:, cols], 1, 1) + _dg(duv, wv_ref[:, cols], 1, 1)
            part = p if part is None else part + p

        @pl.when(j == 0)
        def _():
            acc[...] = part

        @pl.when(j > 0)
        def _():
            acc[...] += part

        @pl.when(j == FFN_J - 1)
        def _():
            _, ln_vjp = jax.vjp(_layer_norm, z_ref[...], g_ref[...], b_ref[...])
            dz, dg, db = ln_vjp(acc[...] + ALPHA * dz3_ref[...])
            dz_ref[...] = dz
            dz2b_ref[...] = dz.astype(BF16)
            dg_ref[...] += dg
            db_ref[...] += db

    h0, h1 = halves
    row = jax.ShapeDtypeStruct((1, D_MODEL), F32)
    whole = lambda *shape: pl.BlockSpec(shape, lambda t, j: (0,) * len(shape))
    return _pcall(
        body, name="ffn_bwd", grid=(nt, FFN_J),
        in_specs=[u_blk, halo, h0["taps"], h1["taps"], h0["bias"], h1["bias"], full, wd_spec, h0["w_up"], h1["w_up"],
                  full, full, vec, vec],
        out_specs=[u_blk, whole(FFN_J, 2, FFN_CONV, FFN_W), whole(FFN_J, 2, 1, FFN_W), full, vec, vec, full],
        out_shape=[jax.ShapeDtypeStruct((2, seq, D_FF), BF16), jax.ShapeDtypeStruct((FFN_J, 2, FFN_CONV, FFN_W), F32),
                   jax.ShapeDtypeStruct((FFN_J, 2, 1, FFN_W), F32), jax.ShapeDtypeStruct((seq, D_MODEL), F32), row, row,
                   jax.ShapeDtypeStruct((seq, D_MODEL), BF16)],
        scratch_shapes=[pltpu.VMEM((tb, D_MODEL), F32), pltpu.VMEM((FFN_J, 2, SUBLANES, FFN_W), F32)],
        compiler_params=_params(("arbitrary", "arbitrary"), _ffn_vmem(tb)),
    )(u, u, conv_w, conv_w, conv_b, conv_b, dz3b, w_down, w_up, w_up, dz3, z2, ln_g, ln_b)


def _adamw_math(w, g, m, v):
    m_new = ADAM_B1 * m + (1.0 - ADAM_B1) * g
    v_new = ADAM_B2 * v + (1.0 - ADAM_B2) * jnp.square(g)
    m_hat = m_new / (1.0 - ADAM_B1 ** ADAM_STEP)
    v_hat = v_new / (1.0 - ADAM_B2 ** ADAM_STEP)
    return -ADAM_LR * (m_hat / (jnp.sqrt(v_hat) + ADAM_EPS) + ADAM_WD * w), m_new, v_new


def _adamw_many(name, ws, gs, ms, vs):
    n = len(ws)

    def body(*refs):
        w_refs, g_refs, m_refs, v_refs = (refs[i * n:(i + 1) * n] for i in range(4))
        d_refs, nm_refs, nv_refs = (refs[(4 + i) * n:(5 + i) * n] for i in range(3))
        for i in range(n):
            d_refs[i][...], nm_refs[i][...], nv_refs[i][...] = _adamw_math(
                w_refs[i][...], g_refs[i][...], m_refs[i][...], v_refs[i][...])

    vm = pl.BlockSpec(memory_space=pltpu.VMEM)
    outs = _pcall(
        body, pin=False, name=name, in_specs=[vm] * (4 * n), out_specs=[vm] * (3 * n),
        out_shape=[jax.ShapeDtypeStruct(w.shape, F32) for w in ws] * 3,
    )(*ws, *gs, *ms, *vs)
    return outs[:n], outs[n:2 * n], outs[2 * n:]


def _adamw_halves(name, core, w, mine, theirs, m, v):
    rows, cols = w.shape
    half_rows = mine.shape[0]
    tr = _tile(half_rows, (256, 176, 128))
    nbh = half_rows // tr
    assert 2 * half_rows == rows

    def body(c_ref, w_ref, a_ref, b_ref, m_ref, v_ref, g_ref, d_ref, nm_ref, nv_ref):
        g = jnp.where(pl.program_id(0) // nbh == c_ref[0], a_ref[...], b_ref[...])
        g_ref[...] = g
        d_ref[...], nm_ref[...], nv_ref[...] = _adamw_math(w_ref[...], g, m_ref[...], v_ref[...])

    spec = pl.BlockSpec((tr, cols), lambda i, c_ref: (i, 0))
    half = pl.BlockSpec((tr, cols), lambda i, c_ref: (i % nbh, 0))
    sh = jax.ShapeDtypeStruct((rows, cols), F32)
    grid_spec = pltpu.PrefetchScalarGridSpec(
        num_scalar_prefetch=1, grid=(rows // tr,), in_specs=[spec, half, half, spec, spec], out_specs=[spec] * 4)
    return _pcall(
        body, name=name, grid_spec=grid_spec, out_shape=[sh] * 4,
        compiler_params=_params(("arbitrary",), 18 * _nbytes((tr, -(-cols // LANES) * LANES), F32)),
    )(core, w, mine, theirs, m, v)


MESH = pl.DeviceIdType.MESH
ANY = pl.BlockSpec(memory_space=pl.ANY)
N_CHIPS = 4
BF16_ROWS = 16


def _me():
    return lax.axis_index("x"), lax.axis_index("y"), lax.axis_index("c")


def _other_chips(x, y):
    return [(1 - x, y), (x, 1 - y), (1 - x, 1 - y)]


def _remote(src, dst, ssem, rsem, dev):
    return pltpu.make_async_remote_copy(src_ref=src, dst_ref=dst, send_sem=ssem, recv_sem=rsem,
                                        device_id=dev, device_id_type=MESH)


def _half_rows(ref_rows, cc):
    half = ref_rows // 2
    return pl.ds(pl.multiple_of(cc * half, BF16_ROWS), half)


def _gather_weights(shards):
    n = len(shards)
    n_ici = n * (N_CHIPS - 1)

    def body(*refs):
        ins, outs, (ssem, rsem, lsem, lrsem) = refs[:n], refs[n:2 * n], refs[2 * n:]
        x, y, c = _me()
        k_me = 2 * x + y
        sib = (x, y, 1 - c)
        chips = _other_chips(x, y)
        started = []
        for i, (w_ref, o_ref) in enumerate(zip(ins, outs)):
            cp = _remote(w_ref, o_ref.at[k_me], lsem.at[i], lrsem.at[i], sib)
            cp.start()
            started.append(cp)
        for r, (px, py) in enumerate(chips):
            for i, (w_ref, o_ref) in enumerate(zip(ins, outs)):
                rows = _half_rows(w_ref.shape[0], c)
                s = r * n + i
                cp = _remote(w_ref.at[rows], o_ref.at[k_me, rows], ssem.at[s], rsem.at[s], (px, py, c))
                cp.start()
                started.append(cp)
        for r, (px, py) in enumerate(chips):
            for i, o_ref in enumerate(outs):
                blk = o_ref.at[2 * px + py, _half_rows(o_ref.shape[1], c)]
                s = r * n + i
                _remote(blk, blk, ssem.at[s], rsem.at[s], (px, py, c)).wait_recv()
                cp = _remote(blk, blk, ssem.at[n_ici + s], rsem.at[n_ici + s], sib)
                cp.start()
                started.append(cp)
        for r, (px, py) in enumerate(chips):
            for i, o_ref in enumerate(outs):
                blk = o_ref.at[2 * px + py, _half_rows(o_ref.shape[1], 1 - c)]
                s = n_ici + r * n + i
                _remote(blk, blk, ssem.at[s], rsem.at[s], sib).wait_recv()
        for cp in started[n:]:
            cp.wait_send()
        for cp in started[:n]:
            cp.wait()

    return _pcall(
        body, name="gather_weights", in_specs=[ANY] * n, out_specs=[ANY] * n,
        out_shape=[jax.ShapeDtypeStruct((N_CHIPS,) + s.shape, s.dtype) for s in shards],
        scratch_shapes=[pltpu.SemaphoreType.DMA((2 * n_ici,)), pltpu.SemaphoreType.DMA((2 * n_ici,)),
                        pltpu.SemaphoreType.DMA((n,)), pltpu.SemaphoreType.DMA((n,))],
    )(*shards)


def _swap_halves(name, grads):
    n = len(grads)

    def body(*refs):
        ins, outs, (ssem, rsem) = refs[:n], refs[n:2 * n], refs[2 * n:]
        x, y, c = _me()
        copies = []
        for i, (g_ref, o_ref) in enumerate(zip(ins, outs)):
            for k in range(N_CHIPS):
                s = i * N_CHIPS + k
                cp = _remote(g_ref.at[k, _half_rows(g_ref.shape[1], 1 - c)], o_ref.at[k], ssem.at[s], rsem.at[s],
                             (x, y, 1 - c))
                cp.start()
                copies.append(cp)
        for cp in copies:
            cp.wait()

    return _pcall(
        body, name=name, in_specs=[ANY] * n, out_specs=[ANY] * n,
        out_shape=[jax.ShapeDtypeStruct((N_CHIPS, g.shape[1] // 2, g.shape[2]), g.dtype) for g in grads],
        scratch_shapes=[pltpu.SemaphoreType.DMA((n * N_CHIPS,)), pltpu.SemaphoreType.DMA((n * N_CHIPS,))],
    )(*grads)


SEM = pl.BlockSpec(memory_space=pltpu.SEMAPHORE)
IN_HBM = pl.BlockSpec(memory_space=pltpu.HBM)
SPLIT_PARAMS = dict(compiler_params=pltpu.CompilerParams(has_side_effects=pltpu.SideEffectType.DATAFLOW_SIDE_EFFECTING))


def _split_start(name, sources, landings, n_copies, plan):
    ns, nl = len(sources), len(landings)

    def body(*refs):
        ins, lands, (ssem, rsem), token = refs[:ns], refs[ns:ns + nl], refs[ns + nl:ns + nl + 2], refs[-1]
        for s, (src, dst, _, dev) in enumerate(plan(ins, lands)):
            _remote(src, dst, ssem.at[s], rsem.at[s], dev).start()
        token[...] = jnp.zeros_like(token)

    arrays = list(sources) + list(landings)
    outs = _call(
        body, name=name, in_specs=[IN_HBM] * (ns + nl),
        out_specs=[SEM, SEM] + [IN_HBM] * (ns + nl) + [pl.BlockSpec(memory_space=pltpu.VMEM)],
        out_shape=[pltpu.SemaphoreType.DMA((n_copies,)), pltpu.SemaphoreType.DMA((n_copies,))]
        + [pltpu.HBM(a.shape, a.dtype) for a in arrays] + [jax.ShapeDtypeStruct((SUBLANES, LANES), F32)],
        input_output_aliases={i: 2 + i for i in range(ns + nl)}, **SPLIT_PARAMS,
    )(*[pltpu.with_memory_space_constraint(a, pltpu.HBM) for a in arrays])
    return (outs[:-1], ns), outs[-1]


def _split_wait(name, handle, after, plan):
    (ssem, rsem, *thru), ns = handle
    nl = len(thru) - ns

    def body(*refs):
        ins, lands, (ssem_ref, rsem_ref) = refs[:ns], refs[ns:ns + nl], refs[ns + nl:ns + nl + 2]
        for s, (src, _, dst, dev) in enumerate(plan(ins, lands)):
            cp = _remote(src, dst, ssem_ref.at[s], rsem_ref.at[s], dev)
            cp.wait_send()
            cp.wait_recv()

    outs = _call(
        body, name=name, in_specs=[IN_HBM] * (ns + nl) + [SEM, SEM, ANY], out_specs=[IN_HBM] * (ns + nl),
        out_shape=[pltpu.HBM(t.shape, t.dtype) for t in thru],
        input_output_aliases={i: i for i in range(ns + nl)}, **SPLIT_PARAMS,
    )(*thru, ssem, rsem, after)
    return outs[:ns], outs[ns:]


def _swap_plan(ins, lands):
    x, y, c = _me()
    return [(g_ref.at[k, _half_rows(g_ref.shape[1], 1 - c)], l_ref.at[k], l_ref.at[k], (x, y, 1 - c))
            for g_ref, l_ref in zip(ins, lands) for k in range(N_CHIPS)]


def _swap_start(name, grads):
    lands = [lax.empty((N_CHIPS, g.shape[1] // 2, g.shape[2]), g.dtype) for g in grads]
    return _split_start(name, grads, lands, len(grads) * N_CHIPS, _swap_plan)


def _swap_wait(name, handle, after):
    return _split_wait(name, handle, after, _swap_plan)


def _gather_plan(ins, lands):
    x, y, c = _me()
    k_me = 2 * x + y
    plan = [(w_ref, l_ref.at[k_me], l_ref.at[k_me], (x, y, 1 - c)) for w_ref, l_ref in zip(ins, lands)]
    for px, py in _other_chips(x, y):
        for w_ref, l_ref in zip(ins, lands):
            rows = _half_rows(w_ref.shape[0], c)
            plan.append((w_ref.at[rows], l_ref.at[k_me, rows], l_ref.at[2 * px + py, rows], (px, py, c)))
    return plan


def _gather_start(name, shards):
    lands = [lax.empty((N_CHIPS,) + s.shape, s.dtype) for s in shards]
    return _split_start(name, shards, lands, len(shards) * N_CHIPS, _gather_plan)


def _gather_wait(name, handle, after):
    return _split_wait(name, handle, after, _gather_plan)[1]


def _forward_halves(name, blocks):
    n = len(blocks)
    n_sem = n * (N_CHIPS - 1)

    def body(*refs):
        outs, (ssem, rsem) = refs[n:2 * n], refs[2 * n:]
        x, y, c = _me()
        sib = (x, y, 1 - c)
        chips = _other_chips(x, y)
        sends = []
        for r, (px, py) in enumerate(chips):
            for i, o_ref in enumerate(outs):
                blk = o_ref.at[2 * px + py, _half_rows(o_ref.shape[1], c)]
                cp = _remote(blk, blk, ssem.at[r * n + i], rsem.at[r * n + i], sib)
                cp.start()
                sends.append(cp)
        for r, (px, py) in enumerate(chips):
            for i, o_ref in enumerate(outs):
                blk = o_ref.at[2 * px + py, _half_rows(o_ref.shape[1], 1 - c)]
                _remote(blk, blk, ssem.at[r * n + i], rsem.at[r * n + i], sib).wait_recv()
        for cp in sends:
            cp.wait_send()

    return _pcall(
        body, name=name, in_specs=[ANY] * n, out_specs=[ANY] * n,
        out_shape=[jax.ShapeDtypeStruct(b.shape, b.dtype) for b in blocks],
        input_output_aliases={i: i for i in range(n)},
        scratch_shapes=[pltpu.SemaphoreType.DMA((n_sem,)), pltpu.SemaphoreType.DMA((n_sem,))],
    )(*blocks)


def _scatter_plan(ins, lands):
    x, y, c = _me()
    k_me = 2 * x + y
    return [(p_ref.at[2 * px + py], l_ref.at[k_me], l_ref.at[2 * px + py], (px, py, c))
            for px, py in _other_chips(x, y) for p_ref, l_ref in zip(ins, lands)]


def _scatter_start(name, parts):
    lands = [lax.empty(p.shape, p.dtype) for p in parts]
    return _split_start(name, parts, lands, len(parts) * (N_CHIPS - 1), _scatter_plan)


def _scatter_wait(name, handle, after):
    return _split_wait(name, handle, after, _scatter_plan)[1]


def _share_and_reduce(halves, v):
    n = len(halves)
    rows = v.shape[0]
    half = rows // 2
    assert half % SUBLANES == 0

    def body(*refs):
        ins, v_ref, outs, out_ref = refs[:n], refs[n], refs[n + 1:2 * n + 1], refs[2 * n + 1]
        pair_buf, mine, chip_buf, ssem, rsem, half_ssem, half_rsem = refs[2 * n + 2:]
        x, y, c = _me()
        k_me = 2 * x + y
        sib = (x, y, 1 - c)
        copies = [_remote(r_ref, o_ref, half_ssem.at[i], half_rsem.at[i], sib)
                  for i, (r_ref, o_ref) in enumerate(zip(ins, outs))]
        for cp in copies:
            cp.start()

        def rows_of(cc):
            return pl.ds(pl.multiple_of(cc * half, SUBLANES), half)

        swap = _remote(v_ref.at[rows_of(1 - c)], pair_buf, ssem.at[0], rsem.at[0], sib)
        swap.start()
        swap.wait()
        mine[...] = v_ref[rows_of(c), :] + pair_buf[...]
        chip_buf[k_me] = mine[...]
        sends = [_remote(mine, chip_buf.at[k_me], ssem.at[1 + r], rsem.at[1 + r], (px, py, c))
                 for r, (px, py) in enumerate(_other_chips(x, y))]
        for cp in sends:
            cp.start()
        for r, (px, py) in enumerate(_other_chips(x, y)):
            blk = chip_buf.at[2 * px + py]
            _remote(blk, blk, ssem.at[1 + r], rsem.at[1 + r], (px, py, c)).wait_recv()
        total = chip_buf[0]
        for k in range(1, N_CHIPS):
            total = total + chip_buf[k]
        out_ref[rows_of(c), :] = total
        for cp in sends:
            cp.wait_send()
        share = _remote(out_ref.at[rows_of(c)], out_ref.at[rows_of(c)], ssem.at[N_CHIPS], rsem.at[N_CHIPS], sib)
        share.start()
        got = out_ref.at[rows_of(1 - c)]
        _remote(got, got, ssem.at[N_CHIPS], rsem.at[N_CHIPS], sib).wait_recv()
        share.wait_send()
        for cp in copies:
            cp.wait()

    vm = pl.BlockSpec(memory_space=pltpu.VMEM)
    outs = _call(
        body, name="share_and_reduce", in_specs=[ANY] * n + [vm], out_specs=[ANY] * n + [vm],
        out_shape=[pltpu.HBM(h.shape, h.dtype) for h in halves] + [jax.ShapeDtypeStruct((rows, LANES), F32)],
        scratch_shapes=[pltpu.VMEM((half, LANES), F32), pltpu.VMEM((half, LANES), F32),
                        pltpu.VMEM((N_CHIPS, half, LANES), F32), pltpu.SemaphoreType.DMA((N_CHIPS + 1,)),
                        pltpu.SemaphoreType.DMA((N_CHIPS + 1,)), pltpu.SemaphoreType.DMA((n,)),
                        pltpu.SemaphoreType.DMA((n,))],
        compiler_params=pltpu.CompilerParams(vmem_limit_bytes=32 * 1024 * 1024),
    )(*[pltpu.with_memory_space_constraint(h, pltpu.HBM) for h in halves], v)
    return outs[:n], outs[n]


def _add_pair(name, core, chip, g, theirs):
    _, half, cols = theirs.shape
    tr = _tile(half, (256, 176, 128))
    nb = half // tr

    def body(c_ref, k_ref, g_ref, t_ref, o32_ref, o16_ref):
        s = g_ref[...] + t_ref[...]
        o16_ref[...] = s.astype(BF16)

        @pl.when(pl.program_id(1) == k_ref[0])
        def _():
            o32_ref[...] = s

    spec = pl.BlockSpec((None, tr, cols), lambda i, k, c_ref, k_ref: (k, i, 0))
    grid_spec = pltpu.PrefetchScalarGridSpec(
        num_scalar_prefetch=2, grid=(nb, N_CHIPS),
        in_specs=[pl.BlockSpec((None, tr, cols), lambda i, k, c_ref, k_ref: (k, c_ref[0] * nb + i, 0)), spec],
        out_specs=[pl.BlockSpec((tr, cols), lambda i, k, c_ref, k_ref: (i, 0)), spec])
    return _pcall(
        body, name=name, grid_spec=grid_spec,
        out_shape=[jax.ShapeDtypeStruct((half, cols), F32), jax.ShapeDtypeStruct(theirs.shape, BF16)],
        compiler_params=_params(("arbitrary", "arbitrary"), 8 * _nbytes((tr, cols + LANES), F32)),
    )(core, chip, g, theirs)


def _add_chips(name, chip, p32, recv):
    half, cols = p32.shape
    tr = _tile(half, (256, 176, 128))

    def body(k_ref, p_ref, r0_ref, r1_ref, r2_ref, o_ref):
        o_ref[...] = ((p_ref[...] + r0_ref[...].astype(F32)) + r1_ref[...].astype(F32)) + r2_ref[...].astype(F32)

    def other(r):
        return pl.BlockSpec((None, tr, cols), lambda i, k_ref: (r + (k_ref[0] <= r).astype(jnp.int32), i, 0))
    grid_spec = pltpu.PrefetchScalarGridSpec(
        num_scalar_prefetch=1, grid=(half // tr,),
        in_specs=[pl.BlockSpec((tr, cols), lambda i, k_ref: (i, 0)), other(0), other(1), other(2)],
        out_specs=pl.BlockSpec((tr, cols), lambda i, k_ref: (i, 0)))
    return _pcall(
        body, name=name, grid_spec=grid_spec, out_shape=jax.ShapeDtypeStruct((half, cols), F32),
        compiler_params=_params(("arbitrary",), 10 * _nbytes((tr, cols + LANES), F32)),
    )(chip, p32, recv, recv, recv)


def kernel(x, mem, w_in, b_in, hg_lb_logits, hg_norm_w, ml_conv_w, ml_conv_b, ml_norm_w, w_out, ln1_g, ln1_b, ca_wq, ca_wkv, ca_wo, ln2_g, ln2_b, ffn_w_up, ffn_conv_w, ffn_conv_b, ffn_w_down, ln3_g, ln3_b, loss_target, m_w_in, m_b_in, m_hg_lb_logits, m_hg_norm_w, m_ml_conv_w, m_ml_conv_b, m_ml_norm_w, m_w_out, m_ln1_g, m_ln1_b, m_ca_wq, m_ca_wkv, m_ca_wo, m_ln2_g, m_ln2_b, m_ffn_w_up, m_ffn_conv_w, m_ffn_conv_b, m_ffn_w_down, m_ln3_g, m_ln3_b, v_w_in, v_b_in, v_hg_lb_logits, v_hg_norm_w, v_ml_conv_w, v_ml_conv_b, v_ml_norm_w, v_w_out, v_ln1_g, v_ln1_b, v_ca_wq, v_ca_wkv, v_ca_wo, v_ln2_g, v_ln2_b, v_ffn_w_up, v_ffn_conv_w, v_ffn_conv_b, v_ffn_w_down, v_ln3_g, v_ln3_b):
    return _train_step(dict(locals()))


WEIGHTS = ("w_in", "b_in", "hg_lb_logits", "hg_norm_w", "ml_conv_w", "ml_conv_b", "ml_norm_w", "w_out", "ln1_g",
           "ln1_b", "ca_wq", "ca_wkv", "ca_wo", "ln2_g", "ln2_b", "ffn_w_up", "ffn_conv_w", "ffn_conv_b",
           "ffn_w_down", "ln3_g", "ln3_b")
MATRICES = ("w_in", "w_out", "ca_wq", "ca_wkv", "ca_wo", "ffn_w_up", "ffn_w_down")
COL_SHARDED = ("w_in", "ca_wkv", "ffn_w_up", "ml_conv_w", "ffn_conv_w")
SMALL = tuple(n for n in WEIGHTS if n not in MATRICES)
PART_ROWS = 16


def _part_rows(shape):
    n = 1
    for s in shape:
        n *= s
    return -(-n // (LANES * PART_ROWS)) * PART_ROWS


def _pack(arrs, dtype):
    parts = []
    for a in arrs:
        flat = a.reshape(-1).astype(dtype)
        flat = jnp.pad(flat, (0, _part_rows(a.shape) * LANES - flat.shape[0]))
        parts.append(flat.reshape(-1, LANES))
    return jnp.concatenate(parts, axis=0)


def _unpack(buf, shapes):
    lead = buf.shape[:-2]
    outs, r = [], 0
    for sh in shapes:
        n = 1
        for s in sh:
            n *= s
        nr = _part_rows(sh)
        flat = buf[..., r:r + nr, :].reshape(lead + (nr * LANES,))
        outs.append(flat[..., :n].reshape(lead + tuple(sh)))
        r += nr
    return outs


def _cat_cols(s):
    return jnp.moveaxis(s, 0, 1).reshape(s.shape[1], -1)


def _stack_rows(s):
    return s.reshape(-1, s.shape[-1])


def _train_step(a):
    xs, mems, tgt = a["x"][0], a["mem"][0], a["loss_target"][0]
    core = lax.axis_index("c").astype(jnp.int32).reshape(1)
    chip = (2 * lax.axis_index("x") + lax.axis_index("y")).astype(jnp.int32).reshape(1)
    k_me = chip[0]
    shard = {n: a[n][0] for n in MATRICES}

    later = [n for n in MATRICES if n != "w_in"]
    w_in, taps = _gather_weights([shard["w_in"].astype(BF16), _pack([a["ml_conv_w"][0], a["ffn_conv_w"][0]], F32)])
    w = {"w_in": jnp.concatenate([*w_in, jnp.zeros((D_MODEL, D_IN_PAD - D_IN), BF16)], axis=1)}
    gathering, token = _gather_start("gather_start", [shard[n].astype(BF16) for n in later])
    ml_cw, ffn_cw = [_cat_cols(s) for s in _unpack(taps, [a["ml_conv_w"].shape[1:], a["ffn_conv_w"].shape[1:]])]
    b_in_p = jnp.pad(a["b_in"], ((0, 0), (0, D_IN_PAD - D_IN))) + token[0:1, 0:1]
    mixer_w = (a["hg_lb_logits"], a["hg_norm_w"], ml_cw, a["ml_conv_b"], a["ml_norm_w"])
    up_cols = a["ffn_w_up"].shape[-1]

    proj, xb = _mm("proj", "nn", xs, w["w_in"], bias=b_in_p, a_copy_dtype=BF16, tm=256, tn=D_IN_PAD)
    y, hst, cst, nst, mst = _mixer_fwd(proj, *mixer_w)
    w.update(zip(later, _forward_halves("forward_halves", _gather_wait("gather_wait", gathering, y))))
    for n in ("w_out", "ca_wq", "ca_wo", "ffn_w_down"):
        w[n] = _stack_rows(w[n])
    z1, x1, x1b = _mm("mix_out", "nn", y, w["w_out"], res=xs, res_scale=ALPHA, ln=("fwd", a["ln1_g"], a["ln1_b"]),
                      copy_dtype=BF16)
    q = _mm("ca_q", "nn", x1b, w["ca_wq"], out_dtype=BF16, tn=D_MODEL)
    kv = _mm("ca_kv", "nn", mems, w["ca_wkv"])
    o = _attn_fwd(q, kv)
    z2, x2, x2b = _mm("ca_out", "nn", o, w["ca_wo"], res=x1, res_scale=ALPHA, ln=("fwd", a["ln2_g"], a["ln2_b"]),
                      copy_dtype=BF16)
    w_up = w["ffn_w_up"]
    assert w_up.shape == (2 * FFN_J, D_MODEL, FFN_W)
    u, hmid, dz3, g_ln3g, g_ln3b, loss_part, dz3b = _ffn_fwd(
        x2b, x2, w_up, ffn_cw, a["ffn_conv_b"], w["ffn_w_down"], a["ln3_g"], a["ln3_b"], tgt)

    grads = {"ln3_g": g_ln3g, "ln3_b": g_ln3b}
    grads["ffn_w_down"] = _mm("g_w_down", "tn", hmid, dz3b, tm=D_FF // 2, tn=D_MODEL)
    du, g_cw, g_cb, dz2, grads["ln2_g"], grads["ln2_b"], dz2b = _ffn_bwd(
        u, ffn_cw, a["ffn_conv_b"], dz3b, dz3, w["ffn_w_down"], w_up, z2, a["ln2_g"], a["ln2_b"])
    grads["ffn_conv_w"] = jnp.transpose(g_cw, (2, 1, 0, 3)).reshape(FFN_CONV, 2 * D_FF)
    grads["ffn_conv_b"] = jnp.transpose(g_cb, (2, 1, 0, 3)).reshape(1, 2 * D_FF)
    grads["ffn_w_up"] = _mm("g_w_up", "tn", x2b, du, out_groups=N_CHIPS, tm=D_MODEL, tn=up_cols)
    grads["ffn_w_down"] = grads["ffn_w_down"].reshape((N_CHIPS,) + shard["ffn_w_down"].shape)
    pending = {}

    def reduce_start(tag, names, swapped=None):
        group = [grads[n] for n in names]
        group, theirs = swapped or (group, _swap_halves("swap_halves_" + tag, group))
        sums = [_add_pair("add_pair_" + n, core, chip, g, t) for n, g, t in zip(names, group, theirs)]
        handle, token = _scatter_start("scatter_start_" + tag, [s16 for _, s16 in sums])
        pending[tag] = (names, [s32 for s32, _ in sums], handle)
        return token[0:1, 0:1]

    ffn = ("ffn_w_up", "ffn_w_down")
    swapping, token = _swap_start("swap_start_ffn", [grads[n] for n in ffn])
    do = _mm("d_o", "nt", dz2b, w["ca_wo"], bias=jnp.zeros((1, D_MODEL), F32) + token[0:1, 0:1], out_dtype=BF16,
             tn=D_MODEL)
    grads["ca_wo"] = _mm_tn_streamed("g_wo", o, dz2b)
    zero = reduce_start("ffn", ffn, _swap_wait("swap_wait_ffn", swapping, grads["ca_wo"]))
    dq, dkv = _attn_bwd(q, kv + zero, do)
    grads["ca_wq"] = _mm_tn_streamed("g_wq", x1b, dq)
    grads["ca_wkv"] = _mm("g_wkv", "tn", mems, dkv, out_groups=N_CHIPS, tm=D_MODEL)
    dz1, grads["ln1_g"], grads["ln1_b"], dz1b = _mm("d_x1", "nt", dq, w["ca_wq"], res=dz2, res_scale=ALPHA,
                                                    ln=("bwd", z1, a["ln1_g"], a["ln1_b"]), copy_dtype=BF16)
    grads["w_out"] = _mm_tn_streamed("g_w_out", y, dz1b)
    for n in ("w_out", "ca_wq", "ca_wo"):
        grads[n] = grads[n].reshape((N_CHIPS,) + shard[n].shape)
    attn = ("w_out", "ca_wq", "ca_wkv", "ca_wo")
    swapping, token = _swap_start("swap_start_attn", [grads[n] for n in attn])
    dy = _mm("d_y", "nt", dz1b, w["w_out"], bias=jnp.zeros((1, D_MODEL), F32) + token[0:1, 0:1], tn=D_MODEL)
    zero = reduce_start("attn", attn, _swap_wait("swap_wait_attn", swapping, dy))
    (dproj, g_b_in, grads["hg_lb_logits"], grads["hg_norm_w"], grads["ml_conv_w"], grads["ml_conv_b"],
     grads["ml_norm_w"]) = _mixer_bwd(proj, dy, hst, cst, nst, mst, mixer_w[0], mixer_w[1] + zero, *mixer_w[2:])
    g_in = _mm("g_w_in", "tn", xb, dproj, tm=D_MODEL, tn=up_cols)
    in_cols = D_IN // N_CHIPS
    grads["w_in"] = jnp.stack([g_in[:, k * in_cols:(k + 1) * in_cols] for k in range(N_CHIPS)])
    grads["b_in"] = g_b_in[:, :D_IN]
    zero = reduce_start("in", ("w_in",))
    dx = _mm("d_x", "nt", dproj, w["w_in"], bias=jnp.zeros((1, D_MODEL), F32) + zero, res=dz1, res_scale=ALPHA,
             tm=256, tn=D_MODEL)

    halves = {}
    for tag, (names, sums32, handle) in pending.items():
        for n, s32, r in zip(names, sums32, _scatter_wait("scatter_wait_" + tag, handle, dx)):
            halves[n] = _add_chips("add_chips_" + n, chip, s32, r)
    halves = [halves[n] for n in MATRICES]

    small_shapes = [grads[n].shape for n in SMALL] + [loss_part.shape]
    other_halves, summed = _share_and_reduce(halves, _pack([grads[n] for n in SMALL] + [loss_part], F32))
    summed = _unpack(summed, small_shapes)
    loss = summed[-1][0, 0]
    for n, g in zip(SMALL, summed[:-1]):
        if n in COL_SHARDED:
            cols = a[n].shape[-1]
            g = lax.dynamic_slice_in_dim(g, k_me * cols, cols, axis=1)
        grads[n] = g

    delta, new_m, new_v = {}, {}, {}
    for n, mine, theirs in zip(MATRICES, halves, other_halves):
        grads[n], delta[n], new_m[n], new_v[n] = _adamw_halves(
            "adamw_" + n, core, shard[n], mine, theirs, a["m_" + n][0], a["v_" + n][0])
    small_w = [a[n][0] if a[n].ndim == 3 else a[n] for n in SMALL]
    small_m = [a["m_" + n][0] if a[n].ndim == 3 else a["m_" + n] for n in SMALL]
    small_v = [a["v_" + n][0] if a[n].ndim == 3 else a["v_" + n] for n in SMALL]
    for out, vals in zip((delta, new_m, new_v),
                         _adamw_many("adamw_small", small_w, [grads[n] for n in SMALL], small_m, small_v)):
        out.update(zip(SMALL, vals))

    def shaped(d):
        return [d[n].reshape(a[n].shape) for n in WEIGHTS]
    return (loss, dx[None], *shaped(grads), *shaped(delta), *shaped(new_m), *shaped(new_v))
```

```python
import functools

import jax
import jax.numpy as jnp
from jax import lax
from jax.experimental import pallas as pl
from jax.experimental.pallas import tpu as pltpu

F32 = jnp.float32
BF16 = jnp.bfloat16

D_MODEL = 1024
HEADS = 4
DK = 128
D_GRP = HEADS * DK
CHUNK = 64
ML_CONV = 4
FFN_CONV = 3
D_FF = 2816
CA_DH = D_MODEL // HEADS
DEPTH = 1
ALPHA = (2.0 * DEPTH) ** 0.25
LN_EPS = 1e-5
NEG_BIG = -1e30
D_IN = 8 * D_GRP + 2 * HEADS
D_IN_PAD = 8 * D_GRP + 128
ADAM_LR, ADAM_B1, ADAM_B2, ADAM_EPS, ADAM_WD, ADAM_STEP = 0.001, 0.9, 0.999, 1e-08, 0.01, 10

SUBLANES = 8
LANES = 128
VMEM_BYTES = 64 * 1024 * 1024


def _pcall(body, pin=True, **kw):
    if not pin:
        return _call(body, **kw)
    kw["out_shape"] = jax.tree.map(lambda s: pltpu.HBM(s.shape, s.dtype), kw["out_shape"])
    call = _call(body, **kw)

    def pinned(*args):
        return call(*[pltpu.with_memory_space_constraint(x, pltpu.HBM) if jnp.issubdtype(x.dtype, jnp.floating) else x
                      for x in args])
    return pinned


def _call(body, **kw):
    return pl.pallas_call(body, **kw)


def _params(semantics, vmem_bytes):
    limit = int(min(max(2 * vmem_bytes, 16 * 1024 * 1024), VMEM_BYTES - 8 * 1024 * 1024))
    return pltpu.CompilerParams(dimension_semantics=semantics, vmem_limit_bytes=limit)


def _nbytes(shape, dtype):
    n = 1
    for s in shape:
        n *= s
    return n * jnp.dtype(dtype).itemsize


def _dg(a, b, ca, cb):
    return lax.dot_general(a.astype(BF16), b.astype(BF16), (((ca,), (cb,)), ((), ())),
                           preferred_element_type=F32)


@jax.custom_vjp
def mm_nn(a, b):
    return _dg(a, b, 1, 0)


mm_nn.defvjp(lambda a, b: (_dg(a, b, 1, 0), (a, b)),
             lambda r, g: (_dg(g, r[1], 1, 1).astype(r[0].dtype), _dg(r[0], g, 0, 0).astype(r[1].dtype)))


@jax.custom_vjp
def mm_nt(a, b):
    return _dg(a, b, 1, 1)


mm_nt.defvjp(lambda a, b: (_dg(a, b, 1, 1), (a, b)),
             lambda r, g: (_dg(g, r[1], 1, 0).astype(r[0].dtype), _dg(g, r[0], 0, 0).astype(r[1].dtype)))


@jax.custom_vjp
def mm_tn(a, b):
    return _dg(a, b, 0, 0)


mm_tn.defvjp(lambda a, b: (_dg(a, b, 0, 0), (a, b)),
             lambda r, g: (_dg(r[1], g, 1, 1).astype(r[0].dtype), _dg(r[0], g, 1, 0).astype(r[1].dtype)))


def _tri(n, lower):
    r = lax.broadcasted_iota(jnp.int32, (n, n), 0)
    c = lax.broadcasted_iota(jnp.int32, (n, n), 1)
    return ((r >= c) if lower else (r <= c)).astype(F32)


def _tri_dot(lower, x):
    t = _tri(x.shape[0], lower).astype(BF16)
    hi = x.astype(BF16)
    rest = x - hi.astype(F32)
    mid = rest.astype(BF16)
    lo = (rest - mid.astype(F32)).astype(BF16)
    return sum(lax.dot_general(t, p, (((1,), (0,)), ((), ())), preferred_element_type=F32) for p in (hi, mid, lo))


@jax.custom_vjp
def cumsum_rows(x):
    return _tri_dot(True, x)


cumsum_rows.defvjp(lambda x: (_tri_dot(True, x), None), lambda _, g: (_tri_dot(False, g),))


def _shift_impl(halo, x, d):
    xx = jnp.concatenate([halo, x], axis=0)
    return pltpu.roll(xx, d, 0)[SUBLANES:]


@functools.partial(jax.custom_vjp, nondiff_argnums=(2,))
def shift_rows(halo, x, d):
    return _shift_impl(halo, x, d)


def _shift_bwd(d, _, g):
    n = g.shape[0] + SUBLANES
    gg = jnp.concatenate([jnp.zeros((SUBLANES, g.shape[1]), g.dtype), g], axis=0)
    r = pltpu.roll(gg, n - d, 0)
    return r[:SUBLANES], r[SUBLANES:]


shift_rows.defvjp(lambda halo, x, d: (_shift_impl(halo, x, d), None), _shift_bwd)


def causal_conv(halo, x, w_rows, b):
    k = len(w_rows)
    y = b + w_rows[k - 1] * x
    for d in range(1, k):
        y = y + w_rows[k - 1 - d] * shift_rows(halo, x, d)
    return y


def _sigmoid(x):
    return 1.0 / (1.0 + jnp.exp(-x))


def _silu(x):
    return x * _sigmoid(x)


def _log_sigmoid(x):
    return jnp.minimum(x, 0.0) - jnp.log(1.0 + jnp.exp(-jnp.abs(x)))


def _pick_row(x, i):
    row = lax.broadcasted_iota(jnp.int32, (x.shape[0], 1), 0)
    return jnp.sum(jnp.where(row == i, x, 0.0), axis=0, keepdims=True)


def _layer_norm(z, g, b):
    mu = jnp.mean(z, axis=-1, keepdims=True)
    zc = z - mu
    var = jnp.mean(zc * zc, axis=-1, keepdims=True)
    return zc * lax.rsqrt(var + LN_EPS) * g + b


def _qk_conv(halo, x, w0, w1, w2, w3, b):
    return _silu(causal_conv(halo, x, (w0, w1, w2, w3), b))


def _grp(i):
    return pl.ds(i * D_GRP, D_GRP)


def _mixer_specs(n_chunks, reverse):
    def chunk(c):
        return n_chunks - 1 - c if reverse else c
    row8 = CHUNK // SUBLANES
    proj_spec = pl.BlockSpec((CHUNK, D_IN_PAD), lambda c: (chunk(c), 0))
    halo_spec = pl.BlockSpec((SUBLANES, 2 * D_GRP), lambda c: (jnp.maximum(chunk(c) * row8 - 1, 0), 2))
    small = [pl.BlockSpec((2, D_GRP), lambda c: (0, 0)), pl.BlockSpec((1, D_GRP), lambda c: (0, 0)),
             pl.BlockSpec((ML_CONV, 2 * D_GRP), lambda c: (0, 0)), pl.BlockSpec((1, 2 * D_GRP), lambda c: (0, 0)),
             pl.BlockSpec((1, D_GRP), lambda c: (0, 0))]
    state_specs = [pl.BlockSpec((1, HEADS, DK, DK), lambda c: (chunk(c), 0, 0, 0)),
                   pl.BlockSpec((1, HEADS, DK, DK), lambda c: (chunk(c), 0, 0, 0)),
                   pl.BlockSpec((1, HEADS, 1, DK), lambda c: (chunk(c), 0, 0, 0)),
                   pl.BlockSpec((1, HEADS, 1, DK), lambda c: (chunk(c), 0, 0, 0))]
    y_spec = pl.BlockSpec((CHUNK, 2 * D_GRP), lambda c: (chunk(c), 0))
    return proj_spec, halo_spec, small, state_specs, y_spec, chunk


def _heads(x):
    return [x[:, h * DK:(h + 1) * DK] for h in range(HEADS)]


def _last(x, j):
    lane = lax.broadcasted_iota(jnp.int32, (1, x.shape[-1]), 1)
    return jnp.sum(jnp.where(lane == j, x, 0.0), axis=-1, keepdims=True)


def _hg_chunk(st_t, hq, hf, hi, hgate, l0, l1, nw):
    n = hq.shape[0]
    lb = _sigmoid(l0 - l1)
    q = _silu(hq)
    lf = jnp.log(lb + (1.0 - lb) * _sigmoid(hf))
    k = (1.0 - lb) * _sigmoid(-hf)
    b = cumsum_rows(lf)
    b_ref = _pick_row(b, n // 2 - 1)
    b_last = _pick_row(b, n - 1)
    qa, ka =_heads(q * jnp.exp(b - b_ref)), _heads(k * jnp.exp(b_ref - b))
    qe, kd, eb, v = _heads(q * jnp.exp(b)), _heads(k * jnp.exp(b_last - b)), _heads(jnp.exp(b_last)), _heads(hi)
    tri = _tri(n, True) > 0
    attn = [jnp.where(tri, mm_nt(qa[h], ka[h]), 0.0) for h in range(HEADS)]
    o = [mm_nn(attn[h], v[h]) + mm_nt(qe[h], st_t[h]) for h in range(HEADS)]
    st_new = jnp.stack([eb[h] * st_t[h] + mm_tn(v[h], kd[h]) for h in range(HEADS)])
    yn = [o[h] * lax.rsqrt(jnp.mean(o[h] * o[h], axis=-1, keepdims=True) + LN_EPS) for h in range(HEADS)]
    return st_new, jnp.concatenate(yn, axis=1) * nw * _silu(hgate)


def _ml_chunk(c_st, n_st, m_st, q, k, v, gates, og, nw):
    n = q.shape[0]
    ig = jnp.stack([_last(gates, h) for h in range(HEADS)])
    log_f = _log_sigmoid(gates)
    fl = jnp.stack([_last(log_f, HEADS + h) for h in range(HEADS)])
    bw = cumsum_rows(jnp.concatenate([jnp.broadcast_to(fl[h], (n, DK)) for h in range(HEADS)], axis=1))
    b = jnp.stack([_last(x, 0) for x in _heads(bw)])
    g = jnp.sum(fl, axis=1, keepdims=True)
    eye = lax.broadcasted_iota(jnp.int32, (n, n), 0) == lax.broadcasted_iota(jnp.int32, (n, n), 1)
    e_row = jnp.sum(jnp.where(eye, ig - b, 0.0), axis=1, keepdims=True)
    d = jnp.where(_tri(n, True) > 0, b + e_row, -jnp.inf)
    inter = b + m_st
    m_t = jnp.maximum(inter, jnp.max(d, axis=2, keepdims=True))
    qs, kh, vh = _heads(q * (DK ** -0.5)), _heads(k), _heads(v)
    s = jnp.stack([mm_nt(qs[h], kh[h]) for h in range(HEADS)]) * jnp.exp(d - m_t)
    w_inter = jnp.exp(inter - m_t)
    num = (jnp.stack([mm_nn(s[h], vh[h]) for h in range(HEADS)])
           + w_inter * jnp.stack([mm_nn(qs[h], c_st[h]) for h in range(HEADS)]))
    den = jnp.sum(s, axis=2, keepdims=True) + w_inter * jnp.sum(jnp.stack(qs) * n_st, axis=2, keepdims=True)
    h_out = num / jnp.maximum(jnp.abs(den), jnp.exp(-m_t))
    a = g - b + ig
    m_new = jnp.maximum(g + m_st, jnp.max(a, axis=1, keepdims=True))
    decay = jnp.exp(g + m_st - m_new)
    wk = jnp.stack(kh) * jnp.exp(a - m_new)
    c_new = decay * c_st + jnp.stack([mm_tn(wk[h], vh[h]) for h in range(HEADS)])
    n_new = decay * n_st + jnp.sum(wk, axis=1, keepdims=True)
    hc = h_out - jnp.mean(h_out, axis=-1, keepdims=True)
    yn = hc * lax.rsqrt(jnp.mean(hc * hc, axis=-1, keepdims=True) + LN_EPS)
    y = _sigmoid(og) * (jnp.concatenate([yn[h] for h in range(HEADS)], axis=1) * nw)
    return c_new, n_new, m_new, y


def _mixer_inputs(proj_ref, lg_ref, hnw_ref, mnw_ref, qk):
    hg_in = (proj_ref[:, _grp(0)], proj_ref[:, _grp(1)], proj_ref[:, _grp(2)], proj_ref[:, _grp(3)],
             lg_ref[0:1, :], lg_ref[1:2, :], hnw_ref[...])
    ml_in = (qk[:, :D_GRP], qk[:, D_GRP:], proj_ref[:, _grp(6)], proj_ref[:, pl.ds(8 * D_GRP, LANES)],
             proj_ref[:, _grp(7)], mnw_ref[...])
    return hg_in, ml_in


def _mixer_fwd(proj, lb_logits, hg_nw, conv_w, conv_b, ml_nw):
    seq = proj.shape[0]
    n_chunks = seq // CHUNK
    proj_spec, halo_spec, small, state_specs, y_spec, _ = _mixer_specs(n_chunks, False)

    def body(proj_ref, halo_ref, lg_ref, hnw_ref, cw_ref, cb_ref, mnw_ref,
             y_ref, hst_ref, cst_ref, nst_ref, mst_ref, hs, cs, ns, ms):
        c = pl.program_id(0)

        @pl.when(c == 0)
        def _():
            hs[...] = jnp.zeros_like(hs)
            cs[...] = jnp.zeros_like(cs)
            ns[...] = jnp.zeros_like(ns)
            ms[...] = jnp.full(ms.shape, NEG_BIG, F32)

        hst_ref[0] = hs[...]
        cst_ref[0] = cs[...]
        nst_ref[0] = ns[...]
        mst_ref[0] = ms[...]
        halo = jnp.where(c > 0, halo_ref[...], 0.0)
        qk = _qk_conv(halo, proj_ref[:, pl.ds(4 * D_GRP, 2 * D_GRP)],
                      cw_ref[0:1, :], cw_ref[1:2, :], cw_ref[2:3, :], cw_ref[3:4, :], cb_ref[...])
        hg_in, ml_in = _mixer_inputs(proj_ref, lg_ref, hnw_ref, mnw_ref, qk)
        hs[...], y_hg = _hg_chunk(hs[...], *hg_in)
        cs[...], ns[...], m_new, y_ml = _ml_chunk(cs[...], ns[...], _last(ms[...], 0), *ml_in)
        ms[...] = jnp.broadcast_to(m_new, ms.shape)
        y_ref[:, pl.ds(0, D_GRP)] = y_hg.astype(BF16)
        y_ref[:, pl.ds(D_GRP, D_GRP)] = y_ml.astype(BF16)

    st = jax.ShapeDtypeStruct((n_chunks, HEADS, DK, DK), F32)
    vec = jax.ShapeDtypeStruct((n_chunks, HEADS, 1, DK), F32)
    vmem = 2 * (_nbytes((CHUNK, D_IN_PAD), F32) + _nbytes((CHUNK, 2 * D_GRP), F32) + 2 * _nbytes((HEADS, DK, DK), F32)) \
        + 2 * _nbytes((HEADS, DK, DK), F32)
    return _pcall(
        body, name="mixer_fwd", grid=(n_chunks,),
        in_specs=[proj_spec, halo_spec] + small,
        out_specs=[y_spec] + state_specs,
        out_shape=[jax.ShapeDtypeStruct((seq, 2 * D_GRP), BF16), st, st, vec, vec],
        scratch_shapes=[pltpu.VMEM((HEADS, DK, DK), F32), pltpu.VMEM((HEADS, DK, DK), F32),
                        pltpu.VMEM((HEADS, 1, DK), F32), pltpu.VMEM((HEADS, 1, DK), F32)],
        compiler_params=_params(("arbitrary",), vmem),
    )(proj, proj, lb_logits, hg_nw, conv_w, conv_b, ml_nw)


def _mixer_bwd(proj, dy, hst, cst, nst, mst, lb_logits, hg_nw, conv_w, conv_b, ml_nw):
    seq = proj.shape[0]
    n_chunks = seq // CHUNK
    proj_spec, halo_spec, small, state_specs, y_spec, _ = _mixer_specs(n_chunks, True)

    def body(proj_ref, halo_ref, dy_ref, hst_ref, cst_ref, nst_ref, mst_ref,
             lg_ref, hnw_ref, cw_ref, cb_ref, mnw_ref,
             dproj_ref, dbin_ref, dlg_ref, dhnw_ref, dcw_ref, dcb_ref, dmnw_ref,
             dhs, dcs, dns, dms, dhalo):
        c = pl.program_id(0)

        @pl.when(c == 0)
        def _():
            for r in (dhs, dcs, dns, dms, dhalo, dbin_ref, dlg_ref, dhnw_ref, dcw_ref, dcb_ref, dmnw_ref):
                r[...] = jnp.zeros_like(r)

        def put(cols, val):
            dproj_ref[:, cols] = val.astype(BF16)
            dbin_ref[:, cols] += jnp.sum(val, axis=0, keepdims=True)

        first = c == n_chunks - 1
        halo = jnp.where(first, 0.0, halo_ref[...])
        x_qk = proj_ref[:, pl.ds(4 * D_GRP, 2 * D_GRP)]
        conv_args = (halo, x_qk, cw_ref[0:1, :], cw_ref[1:2, :], cw_ref[2:3, :], cw_ref[3:4, :], cb_ref[...])
        qk, conv_vjp = jax.vjp(_qk_conv, *conv_args)
        hg_in, ml_in = _mixer_inputs(proj_ref, lg_ref, hnw_ref, mnw_ref, qk)
        _, hg_vjp = jax.vjp(_hg_chunk, hst_ref[0], *hg_in)
        _, ml_vjp = jax.vjp(_ml_chunk, cst_ref[0], nst_ref[0], _last(mst_ref[0], 0), *ml_in)
        dst, dhq, dhf, dhi, dhg, dl0, dl1, dnw = hg_vjp((dhs[...], dy_ref[:, pl.ds(0, D_GRP)]))
        dc, dn, dm, dq, dk, dv, dgates, dog, dmn = ml_vjp(
            (dcs[...], dns[...], _last(dms[...], 0), dy_ref[:, pl.ds(D_GRP, D_GRP)]))
        dhs[...] = dst
        dcs[...] = dc
        dns[...] = dn
        dms[...] = jnp.broadcast_to(dm, dms.shape)
        for i, val in ((0, dhq), (1, dhf), (2, dhi), (3, dhg), (6, dv), (7, dog)):
            put(_grp(i), val)
        put(pl.ds(8 * D_GRP, LANES), dgates)
        dlg_ref[0:1, :] += dl0
        dlg_ref[1:2, :] += dl1
        dhnw_ref[...] += dnw
        dmnw_ref[...] += dmn
        dh, dx, dw0, dw1, dw2, dw3, db = conv_vjp(jnp.concatenate([dq, dk], axis=1))
        tail = jnp.concatenate([jnp.zeros((CHUNK - SUBLANES, 2 * D_GRP), F32), dhalo[...]], axis=0)
        put(pl.ds(4 * D_GRP, 2 * D_GRP), dx + tail)
        dhalo[...] = dh
        for d, dw in enumerate((dw0, dw1, dw2, dw3)):
            dcw_ref[d:d + 1, :] += dw
        dcb_ref[...] += db

    row = pl.BlockSpec((1, D_GRP), lambda c: (0, 0))
    small_out = [pl.BlockSpec((1, D_IN_PAD), lambda c: (0, 0)), pl.BlockSpec((2, D_GRP), lambda c: (0, 0)), row,
                 pl.BlockSpec((ML_CONV, 2 * D_GRP), lambda c: (0, 0)), pl.BlockSpec((1, 2 * D_GRP), lambda c: (0, 0)), row]
    dy_spec = pl.BlockSpec((CHUNK, 2 * D_GRP), y_spec.index_map)
    vmem = 2 * (2 * _nbytes((CHUNK, D_IN_PAD), F32) + _nbytes((CHUNK, 2 * D_GRP), F32)
                + 2 * _nbytes((HEADS, DK, DK), F32)) + 2 * _nbytes((HEADS, DK, DK), F32) + 4 * 1024 * 1024
    return _pcall(
        body, name="mixer_bwd", grid=(n_chunks,),
        in_specs=[proj_spec, halo_spec, dy_spec] + state_specs + small,
        out_specs=[proj_spec] + small_out,
        out_shape=[jax.ShapeDtypeStruct((seq, D_IN_PAD), BF16), jax.ShapeDtypeStruct((1, D_IN_PAD), F32),
                   jax.ShapeDtypeStruct((2, D_GRP), F32), jax.ShapeDtypeStruct((1, D_GRP), F32),
                   jax.ShapeDtypeStruct((ML_CONV, 2 * D_GRP), F32), jax.ShapeDtypeStruct((1, 2 * D_GRP), F32),
                   jax.ShapeDtypeStruct((1, D_GRP), F32)],
        scratch_shapes=[pltpu.VMEM((HEADS, DK, DK), F32), pltpu.VMEM((HEADS, DK, DK), F32),
                        pltpu.VMEM((HEADS, 1, DK), F32), pltpu.VMEM((HEADS, 1, DK), F32),
                        pltpu.VMEM((SUBLANES, 2 * D_GRP), F32)],
        compiler_params=_params(("arbitrary",), vmem),
    )(proj, proj, dy, hst, cst, nst, mst, lb_logits, hg_nw, conv_w, conv_b, ml_nw)


def _tile(n, prefs, unit=None):
    unit = unit or n
    for p in prefs:
        if unit % p == 0 and n % p == 0:
            return p
    return unit


def _logical(arr):
    return arr.shape if arr.ndim == 2 else (arr.shape[1], arr.shape[0] * arr.shape[2])


def _group(arr):
    return arr.shape[-1]


def _split_spec(ndim, group, tr, tc, where):
    if ndim == 2:
        return pl.BlockSpec((tr, tc), where)
    per = group // tc
    assert per * tc == group, (group, tc)

    def index(*ids):
        bi, bj = where(*ids)
        return (bj // per, bi, bj % per)
    return pl.BlockSpec((None, tr, tc), index)


def _mm(name, mode, a, b, *, bias=None, res=None, res_scale=1.0, ln=None, out_dtype=F32, out_groups=None,
        copy_dtype=None, a_copy_dtype=None, tm=None, tn=None, tk=None):
    la, lb = _logical(a), _logical(b)
    if mode == "nn":
        (m, k), n = la, lb[1]
        n_unit = _group(b) if b.ndim == 3 else n
        kc = _group(a) if a.ndim == 3 else k
    elif mode == "nt":
        (m, k), n = la, lb[0]
        n_unit = n
        kc = min(_group(a) if a.ndim == 3 else k, _group(b) if b.ndim == 3 else k)
    else:
        (k, m), n = la, lb[1]
        n_unit, kc = (_group(b) if b.ndim == 3 else n), k
        assert a.ndim == 2
    if out_groups:
        n_unit = min(n_unit, n // out_groups)
    kind = ln[0] if ln else None
    tm = tm or (256 if ln else _tile(m, (512, 256, 128)))
    tn = n if ln else (tn or _tile(n, (512, 384, 256, 128), n_unit))
    if mode != "tn":
        tk = k
    elif tk is None:
        tk = _tile(k, (4096, 2048, 512, 256, 128) if (m // tm) * (n // tn) > 1 else (2048, 512, 256, 128))
    gi, gj, gk = m // tm, n // tn, k // tk
    assert gi * tm == m and gj * tn == n and gk * tk == k and n_unit % tn == 0, (name, m, n, k, tm, tn, tk)
    ca, cb = {"nn": (1, 0), "nt": (1, 1), "tn": (0, 0)}[mode]
    i_outer = gk > 1 or (gi - 1) * _nbytes(b.shape, b.dtype) <= (gj - 1) * _nbytes(a.shape, a.dtype)

    def ij(where):
        return (lambda p, q, kk: where(p, q, kk)) if i_outer else (lambda p, q, kk: where(q, p, kk))
    if mode == "tn":
        a_spec = pl.BlockSpec((tk, tm), ij(lambda i, j, kk: (kk, i)))
    elif a.ndim == 3:
        a_spec = pl.BlockSpec((a.shape[0], tm, _group(a)), ij(lambda i, j, kk: (0, i, 0)))
    else:
        a_spec = pl.BlockSpec((tm, k), ij(lambda i, j, kk: (i, 0)))
    if mode != "nt":
        b_spec = _split_spec(b.ndim, _group(b), tk, tn, ij(lambda i, j, kk: (kk, j)))
    elif b.ndim == 3:
        b_spec = pl.BlockSpec((b.shape[0], tn, _group(b)), ij(lambda i, j, kk: (0, j, 0)))
    else:
        b_spec = pl.BlockSpec((tn, k), ij(lambda i, j, kk: (j, 0)))
    row_spec = pl.BlockSpec((1, tn), ij(lambda i, j, kk: (0, j)))
    blk_spec = pl.BlockSpec((tm, tn), ij(lambda i, j, kk: (i, j)))
    ins, in_specs = [a, b], [a_spec, b_spec]
    if bias is not None:
        ins.append(bias), in_specs.append(row_spec)
    if res is not None:
        ins.append(res), in_specs.append(blk_spec)
    if kind == "fwd":
        ins += [ln[1], ln[2]]
        in_specs += [row_spec, row_spec]
    elif kind == "bwd":
        ins += [ln[1], ln[2], ln[3]]
        in_specs += [blk_spec, row_spec, row_spec]
    if out_groups:
        blk_out = jax.ShapeDtypeStruct((out_groups, m, n // out_groups), out_dtype)
        out_spec = _split_spec(3, n // out_groups, tm, tn, ij(lambda i, j, kk: (i, j)))
    else:
        blk_out, out_spec = jax.ShapeDtypeStruct((m, n), out_dtype), blk_spec
    row_out = jax.ShapeDtypeStruct((1, n), F32)
    if kind is None:
        out_shape, out_specs = [blk_out], [out_spec]
    elif kind == "fwd":
        out_shape, out_specs = [blk_out, blk_out], [blk_spec, blk_spec]
    else:
        out_shape, out_specs = [blk_out, row_out, row_out], [blk_spec, row_spec, row_spec]
    if copy_dtype is not None:
        out_shape.append(jax.ShapeDtypeStruct((m, n), copy_dtype))
        out_specs.append(blk_spec)
    if a_copy_dtype is not None:
        assert mode != "tn" and a.ndim == 2 and copy_dtype is None
        out_shape.append(jax.ShapeDtypeStruct((m, k), a_copy_dtype))
        out_specs.append(a_spec)
    n_in = len(ins)

    def body(*refs):
        in_refs, out_refs, acc_ref = refs[:n_in], refs[n_in:n_in + len(out_shape)], refs[-1]
        i, kk = pl.program_id(0 if i_outer else 1), pl.program_id(2)
        a_ref, b_ref = in_refs[:2]
        extra = list(in_refs[2:])
        if a_copy_dtype is not None:
            out_refs[-1][...] = a_ref[...].astype(a_copy_dtype)

        def epilogue(acc):
            rest = list(extra)
            if bias is not None:
                acc = acc + rest.pop(0)[...]
            if res is not None:
                acc = acc + res_scale * rest.pop(0)[...]
            if kind is None:
                out_refs[0][...] = acc.astype(out_dtype)
                return
            if kind == "fwd":
                out_refs[0][...] = acc
                y = _layer_norm(acc, rest[0][...], rest[1][...])
                out_refs[1][...] = y
                if copy_dtype is not None:
                    out_refs[-1][...] = y.astype(copy_dtype)
                return
            _, vjp = jax.vjp(_layer_norm, rest[0][...], rest[1][...], rest[2][...])
            dz, dg, db = vjp(acc)
            out_refs[0][...] = dz
            out_refs[1][...] += dg
            out_refs[2][...] += db
            if copy_dtype is not None:
                out_refs[-1][...] = dz.astype(copy_dtype)

        if kind == "bwd":
            @pl.when((i == 0) & (kk == 0))
            def _():
                out_refs[1][...] = jnp.zeros_like(out_refs[1])
                out_refs[2][...] = jnp.zeros_like(out_refs[2])

        def chunk(ref, c0, last):
            if ref.ndim == 3:
                g = ref.shape[2]
                return ref[c0 // g, :, pl.ds(c0 % g, kc)]
            return ref[:, pl.ds(c0, kc)] if last else ref[pl.ds(c0, kc), :]

        if mode == "tn" or kc == k:
            prod = _dg(a_ref[...], b_ref[...], ca, cb)
        else:
            prod = None
            for c0 in range(0, k, kc):
                part = _dg(chunk(a_ref, c0, True), chunk(b_ref, c0, mode == "nt"), ca, cb)
                prod = part if prod is None else prod + part
        if gk == 1:
            epilogue(prod)
            return

        @pl.when(kk == 0)
        def _():
            acc_ref[...] = prod

        @pl.when(kk > 0)
        def _():
            acc_ref[...] += prod

        @pl.when(kk == gk - 1)
        def _():
            epilogue(acc_ref[...])

    vmem = (2 * (_nbytes((tm, tk), a.dtype) + _nbytes((tk, tn), b.dtype))
            + (2 * len(ins) + 2 * len(out_shape) + 1) * _nbytes((tm, tn), F32))
    outs = _pcall(
        body, name=name, grid=(gi, gj, gk) if i_outer else (gj, gi, gk), in_specs=in_specs, out_specs=out_specs,
        out_shape=out_shape, scratch_shapes=[pltpu.VMEM((tm, tn) if gk > 1 else (SUBLANES, LANES), F32)],
        compiler_params=_params(("arbitrary", "arbitrary", "arbitrary"), vmem),
    )(*ins)
    return outs[0] if len(out_shape) == 1 else outs


STREAM_ROWS = 512
STREAM_AHEAD = 3


def _mm_tn_streamed(name, a, b):
    k, m = a.shape
    n = b.shape[1]
    nk = k // STREAM_ROWS
    assert nk * STREAM_ROWS == k and b.shape[0] == k

    def body(a_hbm, b_hbm, o_hbm, a_vm, b_vm, o_vm, sem, out_sem):
        def copies(i):
            rows = pl.ds(i * STREAM_ROWS, STREAM_ROWS)
            return (pltpu.make_async_copy(a_hbm.at[rows], a_vm.at[rows], sem.at[0, i]),
                    pltpu.make_async_copy(b_hbm.at[rows], b_vm.at[rows], sem.at[1, i]))

        for i in range(min(STREAM_AHEAD, nk)):
            for cp in copies(i):
                cp.start()
        for i in range(nk):
            for cp in copies(i):
                cp.wait()
            if i + STREAM_AHEAD < nk:
                for cp in copies(i + STREAM_AHEAD):
                    cp.start()
            rows = pl.ds(i * STREAM_ROWS, STREAM_ROWS)
            prod = _dg(a_vm[rows, :], b_vm[rows, :], 0, 0)
            if i == 0:
                o_vm[...] = prod
            else:
                o_vm[...] += prod
        out = pltpu.make_async_copy(o_vm, o_hbm, out_sem.at[0])
        out.start()
        out.wait()

    vmem = _nbytes(a.shape, a.dtype) + _nbytes(b.shape, b.dtype) + 2 * _nbytes((m, n), F32)
    return _pcall(
        body, name=name, in_specs=[ANY, ANY], out_specs=ANY, out_shape=jax.ShapeDtypeStruct((m, n), F32),
        scratch_shapes=[pltpu.VMEM(a.shape, a.dtype), pltpu.VMEM(b.shape, b.dtype), pltpu.VMEM((m, n), F32),
                        pltpu.SemaphoreType.DMA((2, nk)), pltpu.SemaphoreType.DMA((1,))],
        compiler_params=pltpu.CompilerParams(vmem_limit_bytes=int(vmem + 8 * 1024 * 1024)),
    )(a, b)


def _attn_head(q, k, v):
    sc = mm_nt(q, k) * (CA_DH ** -0.5)
    e = jnp.exp(sc - jnp.max(sc, axis=-1, keepdims=True))
    return mm_nn(e / jnp.sum(e, axis=-1, keepdims=True), v)


def _attn_fwd(q, kv):
    seq, n_mem = q.shape[0], kv.shape[0]
    tq = _tile(seq, (512, 256, 128))

    def body(q_ref, kv_ref, o_ref):
        for h in range(HEADS):
            hd = pl.ds(h * CA_DH, CA_DH)
            o = _attn_head(q_ref[:, hd], kv_ref[:, hd], kv_ref[:, pl.ds(D_MODEL + h * CA_DH, CA_DH)])
            o_ref[:, hd] = o.astype(BF16)

    return _pcall(
        body, name="attn_fwd", grid=(seq // tq,),
        in_specs=[pl.BlockSpec((tq, D_MODEL), lambda i: (i, 0)), pl.BlockSpec((n_mem, 2 * D_MODEL), lambda i: (0, 0))],
        out_specs=pl.BlockSpec((tq, D_MODEL), lambda i: (i, 0)), out_shape=jax.ShapeDtypeStruct((seq, D_MODEL), BF16),
        compiler_params=_params(("arbitrary",), 4 * _nbytes((tq, D_MODEL), F32) + 2 * _nbytes((n_mem, 2 * D_MODEL), F32)),
    )(q, kv)


def _attn_bwd(q, kv, do):
    seq, n_mem = q.shape[0], kv.shape[0]
    tq = _tile(seq, (512, 256, 128))

    def body(q_ref, kv_ref, do_ref, dq_ref, dkv_ref):
        @pl.when(pl.program_id(0) == 0)
        def _():
            dkv_ref[...] = jnp.zeros_like(dkv_ref)

        for h in range(HEADS):
            hd = pl.ds(h * CA_DH, CA_DH)
            vd = pl.ds(D_MODEL + h * CA_DH, CA_DH)
            _, vjp = jax.vjp(_attn_head, q_ref[:, hd], kv_ref[:, hd], kv_ref[:, vd])
            dq, dk, dv = vjp(do_ref[:, hd].astype(F32))
            dq_ref[:, hd] = dq.astype(BF16)
            dkv_ref[:, hd] += dk
            dkv_ref[:, vd] += dv

    return _pcall(
        body, name="attn_bwd", grid=(seq // tq,),
        in_specs=[pl.BlockSpec((tq, D_MODEL), lambda i: (i, 0)), pl.BlockSpec((n_mem, 2 * D_MODEL), lambda i: (0, 0)),
                  pl.BlockSpec((tq, D_MODEL), lambda i: (i, 0))],
        out_specs=[pl.BlockSpec((tq, D_MODEL), lambda i: (i, 0)), pl.BlockSpec((n_mem, 2 * D_MODEL), lambda i: (0, 0))],
        out_shape=[jax.ShapeDtypeStruct((seq, D_MODEL), BF16), jax.ShapeDtypeStruct((n_mem, 2 * D_MODEL), F32)],
        compiler_params=_params(("arbitrary",), 6 * _nbytes((tq, D_MODEL), F32) + 4 * _nbytes((n_mem, 2 * D_MODEL), F32)),
    )(q, kv, do)


def _ffn_mid(hg, xg, hv, xv, wg0, wg1, wg2, bg, wv0, wv1, wv2, bv):
    return jax.nn.gelu(causal_conv(hg, xg, (wg0, wg1, wg2), bg)) * causal_conv(hv, xv, (wv0, wv1, wv2), bv)


FFN_TB = 256
FFN_W = D_FF // 2
FFN_J = D_FF // FFN_W
MXU_COLS = 256
FFN_PIECES = tuple((off, min(MXU_COLS, FFN_W - off)) for off in range(0, FFN_W, MXU_COLS))


def _ffn_common_specs(seq, row):
    tb = min(FFN_TB, seq)
    full = pl.BlockSpec((tb, D_MODEL), lambda t, j: (row(t), 0))
    vec = pl.BlockSpec((1, D_MODEL), lambda t, j: (0, 0))
    halves = []
    for off in (0, FFN_J):
        halves.append(dict(
            w_up=pl.BlockSpec((None, D_MODEL, FFN_W), lambda t, j, off=off: (j + off, 0, 0)),
            taps=pl.BlockSpec((FFN_CONV, FFN_W), lambda t, j, off=off: (0, j + off)),
            bias=pl.BlockSpec((1, FFN_W), lambda t, j, off=off: (0, j + off))))
    w_down = pl.BlockSpec((FFN_W, D_MODEL), lambda t, j: (j, 0))
    u_blk = pl.BlockSpec((2, tb, FFN_W), lambda t, j: (0, row(t), j))
    return tb, full, vec, halves, w_down, u_blk


def _ffn_vmem(tb):
    return (_nbytes((2, tb, FFN_W), F32) + _nbytes((2, tb, FFN_W), BF16) + 3 * _nbytes((D_MODEL, FFN_W), BF16)
            + 10 * _nbytes((tb, D_MODEL), F32))


def _conv_params(taps_ref, bias_ref, cols):
    return taps_ref[0:1, cols], taps_ref[1:2, cols], taps_ref[2:3, cols], bias_ref[:, cols]


def _ffn_fwd(x2b, x2, w_up, conv_w, conv_b, w_down, ln_g, ln_b, target):
    seq = x2.shape[0]
    tb, full, vec, halves, wd_spec, u_blk = _ffn_common_specs(seq, lambda t: t)
    nt = seq // tb

    def body(xb_ref, wg_ref, wv_ref, tg_ref, tv_ref, bg_ref, bv_ref, wd_ref, x_ref, g_ref, b_ref, tgt_ref,
             u_ref, h_ref, dz_ref, dg_ref, db_ref, loss_ref, dzb_ref, acc, carry):
        t, j = pl.program_id(0), pl.program_id(1)
        xb = xb_ref[...]
        pieces = [pl.ds(off, width) for off, width in FFN_PIECES]
        ug = [_dg(xb, wg_ref[:, cols], 1, 0) for cols in pieces]
        uv = [_dg(xb, wv_ref[:, cols], 1, 0) for cols in pieces]
        hs = []
        for cols, g, v in zip(pieces, ug, uv):
            u_ref[0, :, cols] = g
            u_ref[1, :, cols] = v
            halo_g = jnp.where(t == 0, 0.0, carry[j, 0, :, cols])
            halo_v = jnp.where(t == 0, 0.0, carry[j, 1, :, cols])
            h = _ffn_mid(halo_g, g, halo_v, v, *_conv_params(tg_ref, bg_ref, cols),
                         *_conv_params(tv_ref, bv_ref, cols)).astype(BF16)
            carry[j, 0, :, cols] = g[tb - SUBLANES:, :]
            carry[j, 1, :, cols] = v[tb - SUBLANES:, :]
            h_ref[:, cols] = h
            hs.append(h)
        part = None
        for cols, h in zip(pieces, hs):
            p = _dg(h, wd_ref[cols, :], 1, 0)
            part = p if part is None else part + p

        @pl.when(j == 0)
        def _():
            acc[...] = part

        @pl.when(j > 0)
        def _():
            acc[...] += part

        @pl.when(j == FFN_J - 1)
        def _():
            y, vjp = jax.vjp(_layer_norm, acc[...] + ALPHA * x_ref[...], g_ref[...], b_ref[...])
            err = y - tgt_ref[...]
            part_loss = 0.5 * jnp.sum(jnp.sum(err * err, axis=1, keepdims=True), axis=0, keepdims=True) / D_MODEL
            dz, dg, db = vjp(err / D_MODEL)

            @pl.when(t == 0)
            def _():
                for r in (dg_ref, db_ref, loss_ref):
                    r[...] = jnp.zeros_like(r)

            dz_ref[...] = dz
            dzb_ref[...] = dz.astype(BF16)
            dg_ref[...] += dg
            db_ref[...] += db
            loss_ref[...] += jnp.broadcast_to(part_loss, (1, LANES))

    h0, h1 = halves
    row = jax.ShapeDtypeStruct((1, D_MODEL), F32)
    return _pcall(
        body, name="ffn_fwd", grid=(nt, FFN_J),
        in_specs=[full, h0["w_up"], h1["w_up"], h0["taps"], h1["taps"], h0["bias"], h1["bias"], wd_spec, full, vec, vec,
                  full],
        out_specs=[u_blk, pl.BlockSpec((tb, FFN_W), lambda t, j: (t, j)), full, vec, vec,
                   pl.BlockSpec((1, LANES), lambda t, j: (0, 0)), full],
        out_shape=[jax.ShapeDtypeStruct((2, seq, D_FF), F32), jax.ShapeDtypeStruct((seq, D_FF), BF16),
                   jax.ShapeDtypeStruct((seq, D_MODEL), F32), row, row, jax.ShapeDtypeStruct((1, LANES), F32),
                   jax.ShapeDtypeStruct((seq, D_MODEL), BF16)],
        scratch_shapes=[pltpu.VMEM((tb, D_MODEL), F32), pltpu.VMEM((FFN_J, 2, SUBLANES, FFN_W), F32)],
        compiler_params=_params(("arbitrary", "arbitrary"), _ffn_vmem(tb)),
    )(x2b, w_up, w_up, conv_w, conv_w, conv_b, conv_b, w_down, x2, ln_g, ln_b, target)


def _ffn_bwd(u, conv_w, conv_b, dz3b, dz3, w_down, w_up, z2, ln_g, ln_b):
    seq = dz3.shape[0]
    tb = min(FFN_TB, seq)
    nt = seq // tb
    row8 = tb // SUBLANES
    tb, full, vec, halves, wd_spec, u_blk = _ffn_common_specs(seq, lambda t: nt - 1 - t)
    halo = pl.BlockSpec((2, SUBLANES, FFN_W), lambda t, j: (0, jnp.maximum((nt - 1 - t) * row8 - 1, 0), j))

    def body(u_ref, halo_ref, tg_ref, tv_ref, bg_ref, bv_ref, dzb_ref, wd_ref, wg_ref, wv_ref, dz3_ref, z_ref, g_ref,
             b_ref, du_ref, dw_ref, dbias_ref, dz_ref, dg_ref, db_ref, dz2b_ref, acc, carry):
        t, j = pl.program_id(0), pl.program_id(1)

        @pl.when((t == 0) & (j == 0))
        def _():
            for r in (dw_ref, dbias_ref, dg_ref, db_ref):
                r[...] = jnp.zeros_like(r)

        pieces = [pl.ds(off, width) for off, width in FFN_PIECES]
        dzb = dzb_ref[...]
        dhs = [_dg(dzb, wd_ref[cols, :], 1, 1) for cols in pieces]
        first = t == nt - 1
        dus = []
        for cols, dh in zip(pieces, dhs):
            args = (jnp.where(first, 0.0, halo_ref[0, :, cols]), u_ref[0, :, cols],
                    jnp.where(first, 0.0, halo_ref[1, :, cols]), u_ref[1, :, cols],
                    *_conv_params(tg_ref, bg_ref, cols), *_conv_params(tv_ref, bv_ref, cols))
            _, vjp = jax.vjp(_ffn_mid, *args)
            dhg, dxg, dhv, dxv, g0, g1, g2, gb, v0, v1, v2, vb = vjp(dh)
            zeros = jnp.zeros((tb - SUBLANES, dh.shape[1]), F32)
            dug = (dxg + jnp.concatenate([zeros, jnp.where(t == 0, 0.0, carry[j, 0, :, cols])], axis=0)).astype(BF16)
            duv = (dxv + jnp.concatenate([zeros, jnp.where(t == 0, 0.0, carry[j, 1, :, cols])], axis=0)).astype(BF16)
            carry[j, 0, :, cols] = dhg
            carry[j, 1, :, cols] = dhv
            du_ref[0, :, cols] = dug
            du_ref[1, :, cols] = duv
            for half, parts in enumerate(((g0, g1, g2), (v0, v1, v2))):
                for d, p in enumerate(parts):
                    dw_ref[j, half, d:d + 1, cols] += p
            dbias_ref[j, 0, :, cols] += gb
            dbias_ref[j, 1, :, cols] += vb
            dus.append((dug, duv))
        part = None
        for cols, (dug, duv) in zip(pieces, dus):
            p = _dg(dug, wg_ref[:, cols], 1, 1) + _dg(duv, wv_ref[:, cols], 1, 1)
            part = p if part is None else part + p

        @pl.when(j == 0)
        def _():
            acc[...] = part

        @pl.when(j > 0)
        def _():
            acc[...] += part

        @pl.when(j == FFN_J - 1)
        def _():
            _, ln_vjp = jax.vjp(_layer_norm, z_ref[...], g_ref[...], b_ref[...])
            dz, dg, db = ln_vjp(acc[...] + ALPHA * dz3_ref[...])
            dz_ref[...] = dz
            dz2b_ref[...] = dz.astype(BF16)
            dg_ref[...] += dg
            db_ref[...] += db

    h0, h1 = halves
    row = jax.ShapeDtypeStruct((1, D_MODEL), F32)
    whole = lambda *shape: pl.BlockSpec(shape, lambda t, j: (0,) * len(shape))
    return _pcall(
        body, name="ffn_bwd", grid=(nt, FFN_J),
        in_specs=[u_blk, halo, h0["taps"], h1["taps"], h0["bias"], h1["bias"], full, wd_spec, h0["w_up"], h1["w_up"],
                  full, full, vec, vec],
        out_specs=[u_blk, whole(FFN_J, 2, FFN_CONV, FFN_W), whole(FFN_J, 2, 1, FFN_W), full, vec, vec, full],
        out_shape=[jax.ShapeDtypeStruct((2, seq, D_FF), BF16), jax.ShapeDtypeStruct((FFN_J, 2, FFN_CONV, FFN_W), F32),
                   jax.ShapeDtypeStruct((FFN_J, 2, 1, FFN_W), F32), jax.ShapeDtypeStruct((seq, D_MODEL), F32), row, row,
                   jax.ShapeDtypeStruct((seq, D_MODEL), BF16)],
        scratch_shapes=[pltpu.VMEM((tb, D_MODEL), F32), pltpu.VMEM((FFN_J, 2, SUBLANES, FFN_W), F32)],
        compiler_params=_params(("arbitrary", "arbitrary"), _ffn_vmem(tb)),
    )(u, u, conv_w, conv_w, conv_b, conv_b, dz3b, w_down, w_up, w_up, dz3, z2, ln_g, ln_b)


def _adamw_math(w, g, m, v):
    m_new = ADAM_B1 * m + (1.0 - ADAM_B1) * g
    v_new = ADAM_B2 * v + (1.0 - ADAM_B2) * jnp.square(g)
    m_hat = m_new / (1.0 - ADAM_B1 ** ADAM_STEP)
    v_hat = v_new / (1.0 - ADAM_B2 ** ADAM_STEP)
    return -ADAM_LR * (m_hat / (jnp.sqrt(v_hat) + ADAM_EPS) + ADAM_WD * w), m_new, v_new


def _adamw_many(name, ws, gs, ms, vs):
    n = len(ws)

    def body(*refs):
        w_refs, g_refs, m_refs, v_refs = (refs[i * n:(i + 1) * n] for i in range(4))
        d_refs, nm_refs, nv_refs = (refs[(4 + i) * n:(5 + i) * n] for i in range(3))
        for i in range(n):
            d_refs[i][...], nm_refs[i][...], nv_refs[i][...] = _adamw_math(
                w_refs[i][...], g_refs[i][...], m_refs[i][...], v_refs[i][...])

    vm = pl.BlockSpec(memory_space=pltpu.VMEM)
    outs = _pcall(
        body, pin=False, name=name, in_specs=[vm] * (4 * n), out_specs=[vm] * (3 * n),
        out_shape=[jax.ShapeDtypeStruct(w.shape, F32) for w in ws] * 3,
    )(*ws, *gs, *ms, *vs)
    return outs[:n], outs[n:2 * n], outs[2 * n:]


def _adamw_halves(name, core, w, mine, theirs, m, v):
    rows, cols = w.shape
    half_rows = mine.shape[0]
    tr = _tile(half_rows, (256, 176, 128))
    nbh = half_rows // tr
    assert 2 * half_rows == rows

    def body(c_ref, w_ref, a_ref, b_ref, m_ref, v_ref, g_ref, d_ref, nm_ref, nv_ref):
        g = jnp.where(pl.program_id(0) // nbh == c_ref[0], a_ref[...], b_ref[...])
        g_ref[...] = g
        d_ref[...], nm_ref[...], nv_ref[...] = _adamw_math(w_ref[...], g, m_ref[...], v_ref[...])

    spec = pl.BlockSpec((tr, cols), lambda i, c_ref: (i, 0))
    half = pl.BlockSpec((tr, cols), lambda i, c_ref: (i % nbh, 0))
    sh = jax.ShapeDtypeStruct((rows, cols), F32)
    grid_spec = pltpu.PrefetchScalarGridSpec(
        num_scalar_prefetch=1, grid=(rows // tr,), in_specs=[spec, half, half, spec, spec], out_specs=[spec] * 4)
    return _pcall(
        body, name=name, grid_spec=grid_spec, out_shape=[sh] * 4,
        compiler_params=_params(("arbitrary",), 18 * _nbytes((tr, -(-cols // LANES) * LANES), F32)),
    )(core, w, mine, theirs, m, v)


MESH = pl.DeviceIdType.MESH
ANY = pl.BlockSpec(memory_space=pl.ANY)
N_CHIPS = 4
BF16_ROWS = 16


def _me():
    return lax.axis_index("x"), lax.axis_index("y"), lax.axis_index("c")


def _other_chips(x, y):
    return [(1 - x, y), (x, 1 - y), (1 - x, 1 - y)]


def _remote(src, dst, ssem, rsem, dev):
    return pltpu.make_async_remote_copy(src_ref=src, dst_ref=dst, send_sem=ssem, recv_sem=rsem,
                                        device_id=dev, device_id_type=MESH)


def _half_rows(ref_rows, cc):
    half = ref_rows // 2
    return pl.ds(pl.multiple_of(cc * half, BF16_ROWS), half)


def _gather_weights(shards):
    n = len(shards)
    n_ici = n * (N_CHIPS - 1)

    def body(*refs):
        ins, outs, (ssem, rsem, lsem, lrsem) = refs[:n], refs[n:2 * n], refs[2 * n:]
        x, y, c = _me()
        k_me = 2 * x + y
        sib = (x, y, 1 - c)
        chips = _other_chips(x, y)
        started = []
        for i, (w_ref, o_ref) in enumerate(zip(ins, outs)):
            cp = _remote(w_ref, o_ref.at[k_me], lsem.at[i], lrsem.at[i], sib)
            cp.start()
            started.append(cp)
        for r, (px, py) in enumerate(chips):
            for i, (w_ref, o_ref) in enumerate(zip(ins, outs)):
                rows = _half_rows(w_ref.shape[0], c)
                s = r * n + i
                cp = _remote(w_ref.at[rows], o_ref.at[k_me, rows], ssem.at[s], rsem.at[s], (px, py, c))
                cp.start()
                started.append(cp)
        for r, (px, py) in enumerate(chips):
            for i, o_ref in enumerate(outs):
                blk = o_ref.at[2 * px + py, _half_rows(o_ref.shape[1], c)]
                s = r * n + i
                _remote(blk, blk, ssem.at[s], rsem.at[s], (px, py, c)).wait_recv()
                cp = _remote(blk, blk, ssem.at[n_ici + s], rsem.at[n_ici + s], sib)
                cp.start()
                started.append(cp)
        for r, (px, py) in enumerate(chips):
            for i, o_ref in enumerate(outs):
                blk = o_ref.at[2 * px + py, _half_rows(o_ref.shape[1], 1 - c)]
                s = n_ici + r * n + i
                _remote(blk, blk, ssem.at[s], rsem.at[s], sib).wait_recv()
        for cp in started[n:]:
            cp.wait_send()
        for cp in started[:n]:
            cp.wait()

    return _pcall(
        body, name="gather_weights", in_specs=[ANY] * n, out_specs=[ANY] * n,
        out_shape=[jax.ShapeDtypeStruct((N_CHIPS,) + s.shape, s.dtype) for s in shards],
        scratch_shapes=[pltpu.SemaphoreType.DMA((2 * n_ici,)), pltpu.SemaphoreType.DMA((2 * n_ici,)),
                        pltpu.SemaphoreType.DMA((n,)), pltpu.SemaphoreType.DMA((n,))],
    )(*shards)


def _swap_halves(name, grads):
    n = len(grads)

    def body(*refs):
        ins, outs, (ssem, rsem) = refs[:n], refs[n:2 * n], refs[2 * n:]
        x, y, c = _me()
        copies = []
        for i, (g_ref, o_ref) in enumerate(zip(ins, outs)):
            for k in range(N_CHIPS):
                s = i * N_CHIPS + k
                cp = _remote(g_ref.at[k, _half_rows(g_ref.shape[1], 1 - c)], o_ref.at[k], ssem.at[s], rsem.at[s],
                             (x, y, 1 - c))
                cp.start()
                copies.append(cp)
        for cp in copies:
            cp.wait()

    return _pcall(
        body, name=name, in_specs=[ANY] * n, out_specs=[ANY] * n,
        out_shape=[jax.ShapeDtypeStruct((N_CHIPS, g.shape[1] // 2, g.shape[2]), g.dtype) for g in grads],
        scratch_shapes=[pltpu.SemaphoreType.DMA((n * N_CHIPS,)), pltpu.SemaphoreType.DMA((n * N_CHIPS,))],
    )(*grads)


SEM = pl.BlockSpec(memory_space=pltpu.SEMAPHORE)
IN_HBM = pl.BlockSpec(memory_space=pltpu.HBM)
SPLIT_PARAMS = dict(compiler_params=pltpu.CompilerParams(has_side_effects=pltpu.SideEffectType.DATAFLOW_SIDE_EFFECTING))


def _split_start(name, sources, landings, n_copies, plan):
    ns, nl = len(sources), len(landings)

    def body(*refs):
        ins, lands, (ssem, rsem), token = refs[:ns], refs[ns:ns + nl], refs[ns + nl:ns + nl + 2], refs[-1]
        for s, (src, dst, _, dev) in enumerate(plan(ins, lands)):
            _remote(src, dst, ssem.at[s], rsem.at[s], dev).start()
        token[...] = jnp.zeros_like(token)

    arrays = list(sources) + list(landings)
    outs = _call(
        body, name=name, in_specs=[IN_HBM] * (ns + nl),
        out_specs=[SEM, SEM] + [IN_HBM] * (ns + nl) + [pl.BlockSpec(memory_space=pltpu.VMEM)],
        out_shape=[pltpu.SemaphoreType.DMA((n_copies,)), pltpu.SemaphoreType.DMA((n_copies,))]
        + [pltpu.HBM(a.shape, a.dtype) for a in arrays] + [jax.ShapeDtypeStruct((SUBLANES, LANES), F32)],
        input_output_aliases={i: 2 + i for i in range(ns + nl)}, **SPLIT_PARAMS,
    )(*[pltpu.with_memory_space_constraint(a, pltpu.HBM) for a in arrays])
    return (outs[:-1], ns), outs[-1]


def _split_wait(name, handle, after, plan):
    (ssem, rsem, *thru), ns = handle
    nl = len(thru) - ns

    def body(*refs):
        ins, lands, (ssem_ref, rsem_ref) = refs[:ns], refs[ns:ns + nl], refs[ns + nl:ns + nl + 2]
        for s, (src, _, dst, dev) in enumerate(plan(ins, lands)):
            cp = _remote(src, dst, ssem_ref.at[s], rsem_ref.at[s], dev)
            cp.wait_send()
            cp.wait_recv()

    outs = _call(
        body, name=name, in_specs=[IN_HBM] * (ns + nl) + [SEM, SEM, ANY], out_specs=[IN_HBM] * (ns + nl),
        out_shape=[pltpu.HBM(t.shape, t.dtype) for t in thru],
        input_output_aliases={i: i for i in range(ns + nl)}, **SPLIT_PARAMS,
    )(*thru, ssem, rsem, after)
    return outs[:ns], outs[ns:]


def _swap_plan(ins, lands):
    x, y, c = _me()
    return [(g_ref.at[k, _half_rows(g_ref.shape[1], 1 - c)], l_ref.at[k], l_ref.at[k], (x, y, 1 - c))
            for g_ref, l_ref in zip(ins, lands) for k in range(N_CHIPS)]


def _swap_start(name, grads):
    lands = [lax.empty((N_CHIPS, g.shape[1] // 2, g.shape[2]), g.dtype) for g in grads]
    return _split_start(name, grads, lands, len(grads) * N_CHIPS, _swap_plan)


def _swap_wait(name, handle, after):
    return _split_wait(name, handle, after, _swap_plan)


def _gather_plan(ins, lands):
    x, y, c = _me()
    k_me = 2 * x + y
    plan = [(w_ref, l_ref.at[k_me], l_ref.at[k_me], (x, y, 1 - c)) for w_ref, l_ref in zip(ins, lands)]
    for px, py in _other_chips(x, y):
        for w_ref, l_ref in zip(ins, lands):
            rows = _half_rows(w_ref.shape[0], c)
            plan.append((w_ref.at[rows], l_ref.at[k_me, rows], l_ref.at[2 * px + py, rows], (px, py, c)))
    return plan


def _gather_start(name, shards):
    lands = [lax.empty((N_CHIPS,) + s.shape, s.dtype) for s in shards]
    return _split_start(name, shards, lands, len(shards) * N_CHIPS, _gather_plan)


def _gather_wait(name, handle, after):
    return _split_wait(name, handle, after, _gather_plan)[1]


def _forward_halves(name, blocks):
    n = len(blocks)
    n_sem = n * (N_CHIPS - 1)

    def body(*refs):
        outs, (ssem, rsem) = refs[n:2 * n], refs[2 * n:]
        x, y, c = _me()
        sib = (x, y, 1 - c)
        chips = _other_chips(x, y)
        sends = []
        for r, (px, py) in enumerate(chips):
            for i, o_ref in enumerate(outs):
                blk = o_ref.at[2 * px + py, _half_rows(o_ref.shape[1], c)]
                cp = _remote(blk, blk, ssem.at[r * n + i], rsem.at[r * n + i], sib)
                cp.start()
                sends.append(cp)
        for r, (px, py) in enumerate(chips):
            for i, o_ref in enumerate(outs):
                blk = o_ref.at[2 * px + py, _half_rows(o_ref.shape[1], 1 - c)]
                _remote(blk, blk, ssem.at[r * n + i], rsem.at[r * n + i], sib).wait_recv()
        for cp in sends:
            cp.wait_send()

    return _pcall(
        body, name=name, in_specs=[ANY] * n, out_specs=[ANY] * n,
        out_shape=[jax.ShapeDtypeStruct(b.shape, b.dtype) for b in blocks],
        input_output_aliases={i: i for i in range(n)},
        scratch_shapes=[pltpu.SemaphoreType.DMA((n_sem,)), pltpu.SemaphoreType.DMA((n_sem,))],
    )(*blocks)


def _scatter_plan(ins, lands):
    x, y, c = _me()
    k_me = 2 * x + y
    return [(p_ref.at[2 * px + py], l_ref.at[k_me], l_ref.at[2 * px + py], (px, py, c))
            for px, py in _other_chips(x, y) for p_ref, l_ref in zip(ins, lands)]


def _scatter_start(name, parts):
    lands = [lax.empty(p.shape, p.dtype) for p in parts]
    return _split_start(name, parts, lands, len(parts) * (N_CHIPS - 1), _scatter_plan)


def _scatter_wait(name, handle, after):
    return _split_wait(name, handle, after, _scatter_plan)[1]


def _share_and_reduce(halves, v):
    n = len(halves)
    rows = v.shape[0]
    half = rows // 2
    assert half % SUBLANES == 0

    def body(*refs):
        ins, v_ref, outs, out_ref = refs[:n], refs[n], refs[n + 1:2 * n + 1], refs[2 * n + 1]
        pair_buf, mine, chip_buf, ssem, rsem, half_ssem, half_rsem = refs[2 * n + 2:]
        x, y, c = _me()
        k_me = 2 * x + y
        sib = (x, y, 1 - c)
        copies = [_remote(r_ref, o_ref, half_ssem.at[i], half_rsem.at[i], sib)
                  for i, (r_ref, o_ref) in enumerate(zip(ins, outs))]
        for cp in copies:
            cp.start()

        def rows_of(cc):
            return pl.ds(pl.multiple_of(cc * half, SUBLANES), half)

        swap = _remote(v_ref.at[rows_of(1 - c)], pair_buf, ssem.at[0], rsem.at[0], sib)
        swap.start()
        swap.wait()
        mine[...] = v_ref[rows_of(c), :] + pair_buf[...]
        chip_buf[k_me] = mine[...]
        sends = [_remote(mine, chip_buf.at[k_me], ssem.at[1 + r], rsem.at[1 + r], (px, py, c))
                 for r, (px, py) in enumerate(_other_chips(x, y))]
        for cp in sends:
            cp.start()
        for r, (px, py) in enumerate(_other_chips(x, y)):
            blk = chip_buf.at[2 * px + py]
            _remote(blk, blk, ssem.at[1 + r], rsem.at[1 + r], (px, py, c)).wait_recv()
        total = chip_buf[0]
        for k in range(1, N_CHIPS):
            total = total + chip_buf[k]
        out_ref[rows_of(c), :] = total
        for cp in sends:
            cp.wait_send()
        share = _remote(out_ref.at[rows_of(c)], out_ref.at[rows_of(c)], ssem.at[N_CHIPS], rsem.at[N_CHIPS], sib)
        share.start()
        got = out_ref.at[rows_of(1 - c)]
        _remote(got, got, ssem.at[N_CHIPS], rsem.at[N_CHIPS], sib).wait_recv()
        share.wait_send()
        for cp in copies:
            cp.wait()

    vm = pl.BlockSpec(memory_space=pltpu.VMEM)
    outs = _call(
        body, name="share_and_reduce", in_specs=[ANY] * n + [vm], out_specs=[ANY] * n + [vm],
        out_shape=[pltpu.HBM(h.shape, h.dtype) for h in halves] + [jax.ShapeDtypeStruct((rows, LANES), F32)],
        scratch_shapes=[pltpu.VMEM((half, LANES), F32), pltpu.VMEM((half, LANES), F32),
                        pltpu.VMEM((N_CHIPS, half, LANES), F32), pltpu.SemaphoreType.DMA((N_CHIPS + 1,)),
                        pltpu.SemaphoreType.DMA((N_CHIPS + 1,)), pltpu.SemaphoreType.DMA((n,)),
                        pltpu.SemaphoreType.DMA((n,))],
        compiler_params=pltpu.CompilerParams(vmem_limit_bytes=32 * 1024 * 1024),
    )(*[pltpu.with_memory_space_constraint(h, pltpu.HBM) for h in halves], v)
    return outs[:n], outs[n]


def _add_pair(name, core, chip, g, theirs):
    _, half, cols = theirs.shape
    tr = _tile(half, (256, 176, 128))
    nb = half // tr

    def body(c_ref, k_ref, g_ref, t_ref, o32_ref, o16_ref):
        s = g_ref[...] + t_ref[...]
        o16_ref[...] = s.astype(BF16)

        @pl.when(pl.program_id(1) == k_ref[0])
        def _():
            o32_ref[...] = s

    spec = pl.BlockSpec((None, tr, cols), lambda i, k, c_ref, k_ref: (k, i, 0))
    grid_spec = pltpu.PrefetchScalarGridSpec(
        num_scalar_prefetch=2, grid=(nb, N_CHIPS),
        in_specs=[pl.BlockSpec((None, tr, cols), lambda i, k, c_ref, k_ref: (k, c_ref[0] * nb + i, 0)), spec],
        out_specs=[pl.BlockSpec((tr, cols), lambda i, k, c_ref, k_ref: (i, 0)), spec])
    return _pcall(
        body, name=name, grid_spec=grid_spec,
        out_shape=[jax.ShapeDtypeStruct((half, cols), F32), jax.ShapeDtypeStruct(theirs.shape, BF16)],
        compiler_params=_params(("arbitrary", "arbitrary"), 8 * _nbytes((tr, cols + LANES), F32)),
    )(core, chip, g, theirs)


def _add_chips(name, chip, p32, recv):
    half, cols = p32.shape
    tr = _tile(half, (256, 176, 128))

    def body(k_ref, p_ref, r0_ref, r1_ref, r2_ref, o_ref):
        o_ref[...] = ((p_ref[...] + r0_ref[...].astype(F32)) + r1_ref[...].astype(F32)) + r2_ref[...].astype(F32)

    def other(r):
        return pl.BlockSpec((None, tr, cols), lambda i, k_ref: (r + (k_ref[0] <= r).astype(jnp.int32), i, 0))
    grid_spec = pltpu.PrefetchScalarGridSpec(
        num_scalar_prefetch=1, grid=(half // tr,),
        in_specs=[pl.BlockSpec((tr, cols), lambda i, k_ref: (i, 0)), other(0), other(1), other(2)],
        out_specs=pl.BlockSpec((tr, cols), lambda i, k_ref: (i, 0)))
    return _pcall(
        body, name=name, grid_spec=grid_spec, out_shape=jax.ShapeDtypeStruct((half, cols), F32),
        compiler_params=_params(("arbitrary",), 10 * _nbytes((tr, cols + LANES), F32)),
    )(chip, p32, recv, recv, recv)


def kernel(x, mem, w_in, b_in, hg_lb_logits, hg_norm_w, ml_conv_w, ml_conv_b, ml_norm_w, w_out, ln1_g, ln1_b, ca_wq, ca_wkv, ca_wo, ln2_g, ln2_b, ffn_w_up, ffn_conv_w, ffn_conv_b, ffn_w_down, ln3_g, ln3_b, loss_target, m_w_in, m_b_in, m_hg_lb_logits, m_hg_norm_w, m_ml_conv_w, m_ml_conv_b, m_ml_norm_w, m_w_out, m_ln1_g, m_ln1_b, m_ca_wq, m_ca_wkv, m_ca_wo, m_ln2_g, m_ln2_b, m_ffn_w_up, m_ffn_conv_w, m_ffn_conv_b, m_ffn_w_down, m_ln3_g, m_ln3_b, v_w_in, v_b_in, v_hg_lb_logits, v_hg_norm_w, v_ml_conv_w, v_ml_conv_b, v_ml_norm_w, v_w_out, v_ln1_g, v_ln1_b, v_ca_wq, v_ca_wkv, v_ca_wo, v_ln2_g, v_ln2_b, v_ffn_w_up, v_ffn_conv_w, v_ffn_conv_b, v_ffn_w_down, v_ln3_g, v_ln3_b):
    return _train_step(dict(locals()))


WEIGHTS = ("w_in", "b_in", "hg_lb_logits", "hg_norm_w", "ml_conv_w", "ml_conv_b", "ml_norm_w", "w_out", "ln1_g",
           "ln1_b", "ca_wq", "ca_wkv", "ca_wo", "ln2_g", "ln2_b", "ffn_w_up", "ffn_conv_w", "ffn_conv_b",
           "ffn_w_down", "ln3_g", "ln3_b")
MATRICES = ("w_in", "w_out", "ca_wq", "ca_wkv", "ca_wo", "ffn_w_up", "ffn_w_down")
COL_SHARDED = ("w_in", "ca_wkv", "ffn_w_up", "ml_conv_w", "ffn_conv_w")
SMALL = tuple(n for n in WEIGHTS if n not in MATRICES)
PART_ROWS = 16


def _part_rows(shape):
    n = 1
    for s in shape:
        n *= s
    return -(-n // (LANES * PART_ROWS)) * PART_ROWS


def _pack(arrs, dtype):
    parts = []
    for a in arrs:
        flat = a.reshape(-1).astype(dtype)
        flat = jnp.pad(flat, (0, _part_rows(a.shape) * LANES - flat.shape[0]))
        parts.append(flat.reshape(-1, LANES))
    return jnp.concatenate(parts, axis=0)


def _unpack(buf, shapes):
    lead = buf.shape[:-2]
    outs, r = [], 0
    for sh in shapes:
        n = 1
        for s in sh:
            n *= s
        nr = _part_rows(sh)
        flat = buf[..., r:r + nr, :].reshape(lead + (nr * LANES,))
        outs.append(flat[..., :n].reshape(lead + tuple(sh)))
        r += nr
    return outs


def _cat_cols(s):
    return jnp.moveaxis(s, 0, 1).reshape(s.shape[1], -1)


def _stack_rows(s):
    return s.reshape(-1, s.shape[-1])


def _train_step(a):
    xs, mems, tgt = a["x"][0], a["mem"][0], a["loss_target"][0]
    core = lax.axis_index("c").astype(jnp.int32).reshape(1)
    chip = (2 * lax.axis_index("x") + lax.axis_index("y")).astype(jnp.int32).reshape(1)
    k_me = chip[0]
    shard = {n: a[n][0] for n in MATRICES}

    later = [n for n in MATRICES if n != "w_in"]
    w_in, taps = _gather_weights([shard["w_in"].astype(BF16), _pack([a["ml_conv_w"][0], a["ffn_conv_w"][0]], F32)])
    w = {"w_in": jnp.concatenate([*w_in, jnp.zeros((D_MODEL, D_IN_PAD - D_IN), BF16)], axis=1)}
    gathering, token = _gather_start("gather_start", [shard[n].astype(BF16) for n in later])
    ml_cw, ffn_cw = [_cat_cols(s) for s in _unpack(taps, [a["ml_conv_w"].shape[1:], a["ffn_conv_w"].shape[1:]])]
    b_in_p = jnp.pad(a["b_in"], ((0, 0), (0, D_IN_PAD - D_IN))) + token[0:1, 0:1]
    mixer_w = (a["hg_lb_logits"], a["hg_norm_w"], ml_cw, a["ml_conv_b"], a["ml_norm_w"])
    up_cols = a["ffn_w_up"].shape[-1]

    proj, xb = _mm("proj", "nn", xs, w["w_in"], bias=b_in_p, a_copy_dtype=BF16, tm=256, tn=D_IN_PAD)
    y, hst, cst, nst, mst = _mixer_fwd(proj, *mixer_w)
    w.update(zip(later, _forward_halves("forward_halves", _gather_wait("gather_wait", gathering, y))))
    for n in ("w_out", "ca_wq", "ca_wo", "ffn_w_down"):
        w[n] = _stack_rows(w[n])
    z1, x1, x1b = _mm("mix_out", "nn", y, w["w_out"], res=xs, res_scale=ALPHA, ln=("fwd", a["ln1_g"], a["ln1_b"]),
                      copy_dtype=BF16)
    q = _mm("ca_q", "nn", x1b, w["ca_wq"], out_dtype=BF16, tn=D_MODEL)
    kv = _mm("ca_kv", "nn", mems, w["ca_wkv"])
    o = _attn_fwd(q, kv)
    z2, x2, x2b = _mm("ca_out", "nn", o, w["ca_wo"], res=x1, res_scale=ALPHA, ln=("fwd", a["ln2_g"], a["ln2_b"]),
                      copy_dtype=BF16)
    w_up = w["ffn_w_up"]
    assert w_up.shape == (2 * FFN_J, D_MODEL, FFN_W)
    u, hmid, dz3, g_ln3g, g_ln3b, loss_part, dz3b = _ffn_fwd(
        x2b, x2, w_up, ffn_cw, a["ffn_conv_b"], w["ffn_w_down"], a["ln3_g"], a["ln3_b"], tgt)

    grads = {"ln3_g": g_ln3g, "ln3_b": g_ln3b}
    grads["ffn_w_down"] = _mm("g_w_down", "tn", hmid, dz3b, tm=D_FF // 2, tn=D_MODEL)
    du, g_cw, g_cb, dz2, grads["ln2_g"], grads["ln2_b"], dz2b = _ffn_bwd(
        u, ffn_cw, a["ffn_conv_b"], dz3b, dz3, w["ffn_w_down"], w_up, z2, a["ln2_g"], a["ln2_b"])
    grads["ffn_conv_w"] = jnp.transpose(g_cw, (2, 1, 0, 3)).reshape(FFN_CONV, 2 * D_FF)
    grads["ffn_conv_b"] = jnp.transpose(g_cb, (2, 1, 0, 3)).reshape(1, 2 * D_FF)
    grads["ffn_w_up"] = _mm("g_w_up", "tn", x2b, du, out_groups=N_CHIPS, tm=D_MODEL, tn=up_cols)
    grads["ffn_w_down"] = grads["ffn_w_down"].reshape((N_CHIPS,) + shard["ffn_w_down"].shape)
    pending = {}

    def reduce_start(tag, names, swapped=None):
        group = [grads[n] for n in names]
        group, theirs = swapped or (group, _swap_halves("swap_halves_" + tag, group))
        sums = [_add_pair("add_pair_" + n, core, chip, g, t) for n, g, t in zip(names, group, theirs)]
        handle, token = _scatter_start("scatter_start_" + tag, [s16 for _, s16 in sums])
        pending[tag] = (names, [s32 for s32, _ in sums], handle)
        return token[0:1, 0:1]

    ffn = ("ffn_w_up", "ffn_w_down")
    swapping, token = _swap_start("swap_start_ffn", [grads[n] for n in ffn])
    do = _mm("d_o", "nt", dz2b, w["ca_wo"], bias=jnp.zeros((1, D_MODEL), F32) + token[0:1, 0:1], out_dtype=BF16,
             tn=D_MODEL)
    grads["ca_wo"] = _mm_tn_streamed("g_wo", o, dz2b)
    zero = reduce_start("ffn", ffn, _swap_wait("swap_wait_ffn", swapping, grads["ca_wo"]))
    dq, dkv = _attn_bwd(q, kv + zero, do)
    grads["ca_wq"] = _mm_tn_streamed("g_wq", x1b, dq)
    grads["ca_wkv"] = _mm("g_wkv", "tn", mems, dkv, out_groups=N_CHIPS, tm=D_MODEL)
    dz1, grads["ln1_g"], grads["ln1_b"], dz1b = _mm("d_x1", "nt", dq, w["ca_wq"], res=dz2, res_scale=ALPHA,
                                                    ln=("bwd", z1, a["ln1_g"], a["ln1_b"]), copy_dtype=BF16)
    grads["w_out"] = _mm_tn_streamed("g_w_out", y, dz1b)
    for n in ("w_out", "ca_wq", "ca_wo"):
        grads[n] = grads[n].reshape((N_CHIPS,) + shard[n].shape)
    attn = ("w_out", "ca_wq", "ca_wkv", "ca_wo")
    swapping, token = _swap_start("swap_start_attn", [grads[n] for n in attn])
    dy = _mm("d_y", "nt", dz1b, w["w_out"], bias=jnp.zeros((1, D_MODEL), F32) + token[0:1, 0:1], tn=D_MODEL)
    zero = reduce_start("attn", attn, _swap_wait("swap_wait_attn", swapping, dy))
    (dproj, g_b_in, grads["hg_lb_logits"], grads["hg_norm_w"], grads["ml_conv_w"], grads["ml_conv_b"],
     grads["ml_norm_w"]) = _mixer_bwd(proj, dy, hst, cst, nst, mst, mixer_w[0], mixer_w[1] + zero, *mixer_w[2:])
    g_in = _mm("g_w_in", "tn", xb, dproj, tm=D_MODEL, tn=up_cols)
    in_cols = D_IN // N_CHIPS
    grads["w_in"] = jnp.stack([g_in[:, k * in_cols:(k + 1) * in_cols] for k in range(N_CHIPS)])
    grads["b_in"] = g_b_in[:, :D_IN]
    zero = reduce_start("in", ("w_in",))
    dx = _mm("d_x", "nt", dproj, w["w_in"], bias=jnp.zeros((1, D_MODEL), F32) + zero, res=dz1, res_scale=ALPHA,
             tm=256, tn=D_MODEL)

    halves = {}
    for tag, (names, sums32, handle) in pending.items():
        for n, s32, r in zip(names, sums32, _scatter_wait("scatter_wait_" + tag, handle, dx)):
            halves[n] = _add_chips("add_chips_" + n, chip, s32, r)
    halves = [halves[n] for n in MATRICES]

    small_shapes = [grads[n].shape for n in SMALL] + [loss_part.shape]
    other_halves, summed = _share_and_reduce(halves, _pack([grads[n] for n in SMALL] + [loss_part], F32))
    summed = _unpack(summed, small_shapes)
    loss = summed[-1][0, 0]
    for n, g in zip(SMALL, summed[:-1]):
        if n in COL_SHARDED:
            cols = a[n].shape[-1]
            g = lax.dynamic_slice_in_dim(g, k_me * cols, cols, axis=1)
        grads[n] = g

    delta, new_m, new_v = {}, {}, {}
    for n, mine, theirs in zip(MATRICES, halves, other_halves):
        grads[n], delta[n], new_m[n], new_v[n] = _adamw_halves(
            "adamw_" + n, core, shard[n], mine, theirs, a["m_" + n][0], a["v_" + n][0])
    small_w = [a[n][0] if a[n].ndim == 3 else a[n] for n in SMALL]
    small_m = [a["m_" + n][0] if a[n].ndim == 3 else a["m_" + n] for n in SMALL]
    small_v = [a["v_" + n][0] if a[n].ndim == 3 else a["v_" + n] for n in SMALL]
    for out, vals in zip((delta, new_m, new_v),
                         _adamw_many("adamw_small", small_w, [grads[n] for n in SMALL], small_m, small_v)):
        out.update(zip(SMALL, vals))

    def shaped(d):
        return [d[n].reshape(a[n].shape) for n in WEIGHTS]
    return (loss, dx[None], *shaped(grads), *shaped(delta), *shaped(new_m), *shaped(new_v))
```

```python
import functools

import jax
import jax.numpy as jnp
from jax import lax
from jax.experimental import pallas as pl
from jax.experimental.pallas import tpu as pltpu

F32 = jnp.float32
BF16 = jnp.bfloat16

D_MODEL = 1024
HEADS = 4
DK = 128
D_GRP = HEADS * DK
CHUNK = 64
ML_CONV = 4
FFN_CONV = 3
D_FF = 2816
CA_DH = D_MODEL // HEADS
DEPTH = 1
ALPHA = (2.0 * DEPTH) ** 0.25
LN_EPS = 1e-5
NEG_BIG = -1e30
D_IN = 8 * D_GRP + 2 * HEADS
D_IN_PAD = 8 * D_GRP + 128
ADAM_LR, ADAM_B1, ADAM_B2, ADAM_EPS, ADAM_WD, ADAM_STEP = 0.001, 0.9, 0.999, 1e-08, 0.01, 10

SUBLANES = 8
LANES = 128
VMEM_BYTES = 64 * 1024 * 1024


def _pcall(body, pin=True, fused=(), **kw):
    if not pin:
        return _call(body, **kw)
    kw["out_shape"] = jax.tree.map(lambda s: pltpu.HBM(s.shape, s.dtype), kw["out_shape"])
    call = _call(body, **kw)

    def pinned(*args):
        return call(*[pltpu.with_memory_space_constraint(x, pltpu.HBM)
                      if jnp.issubdtype(x.dtype, jnp.floating) and i not in fused else x
                      for i, x in enumerate(args)])
    return pinned


def _call(body, **kw):
    return pl.pallas_call(body, **kw)


def _params(semantics, vmem_bytes, fuse_inputs=None):
    limit = int(min(max(2 * vmem_bytes, 16 * 1024 * 1024), VMEM_BYTES - 8 * 1024 * 1024))
    return pltpu.CompilerParams(dimension_semantics=semantics, vmem_limit_bytes=limit, allow_input_fusion=fuse_inputs)


def _nbytes(shape, dtype):
    n = 1
    for s in shape:
        n *= s
    return n * jnp.dtype(dtype).itemsize


def _dg(a, b, ca, cb):
    return lax.dot_general(a.astype(BF16), b.astype(BF16), (((ca,), (cb,)), ((), ())),
                           preferred_element_type=F32)


@jax.custom_vjp
def mm_nn(a, b):
    return _dg(a, b, 1, 0)


mm_nn.defvjp(lambda a, b: (_dg(a, b, 1, 0), (a, b)),
             lambda r, g: (_dg(g, r[1], 1, 1).astype(r[0].dtype), _dg(r[0], g, 0, 0).astype(r[1].dtype)))


@jax.custom_vjp
def mm_nt(a, b):
    return _dg(a, b, 1, 1)


mm_nt.defvjp(lambda a, b: (_dg(a, b, 1, 1), (a, b)),
             lambda r, g: (_dg(g, r[1], 1, 0).astype(r[0].dtype), _dg(g, r[0], 0, 0).astype(r[1].dtype)))


@jax.custom_vjp
def mm_tn(a, b):
    return _dg(a, b, 0, 0)


mm_tn.defvjp(lambda a, b: (_dg(a, b, 0, 0), (a, b)),
             lambda r, g: (_dg(r[1], g, 1, 1).astype(r[0].dtype), _dg(r[0], g, 1, 0).astype(r[1].dtype)))


def _tri(n, lower):
    r = lax.broadcasted_iota(jnp.int32, (n, n), 0)
    c = lax.broadcasted_iota(jnp.int32, (n, n), 1)
    return ((r >= c) if lower else (r <= c)).astype(F32)


def _tri_dot(lower, x):
    t = _tri(x.shape[0], lower).astype(BF16)
    hi = x.astype(BF16)
    rest = x - hi.astype(F32)
    mid = rest.astype(BF16)
    lo = (rest - mid.astype(F32)).astype(BF16)
    return sum(lax.dot_general(t, p, (((1,), (0,)), ((), ())), preferred_element_type=F32) for p in (hi, mid, lo))


@jax.custom_vjp
def cumsum_rows(x):
    return _tri_dot(True, x)


cumsum_rows.defvjp(lambda x: (_tri_dot(True, x), None), lambda _, g: (_tri_dot(False, g),))


def _shift_impl(halo, x, d):
    xx = jnp.concatenate([halo, x], axis=0)
    return pltpu.roll(xx, d, 0)[SUBLANES:]


@functools.partial(jax.custom_vjp, nondiff_argnums=(2,))
def shift_rows(halo, x, d):
    return _shift_impl(halo, x, d)


def _shift_bwd(d, _, g):
    n = g.shape[0] + SUBLANES
    gg = jnp.concatenate([jnp.zeros((SUBLANES, g.shape[1]), g.dtype), g], axis=0)
    r = pltpu.roll(gg, n - d, 0)
    return r[:SUBLANES], r[SUBLANES:]


shift_rows.defvjp(lambda halo, x, d: (_shift_impl(halo, x, d), None), _shift_bwd)


def causal_conv(halo, x, w_rows, b):
    k = len(w_rows)
    y = b + w_rows[k - 1] * x
    for d in range(1, k):
        y = y + w_rows[k - 1 - d] * shift_rows(halo, x, d)
    return y


def _sigmoid(x):
    return 1.0 / (1.0 + jnp.exp(-x))


def _silu(x):
    return x * _sigmoid(x)


def _log_sigmoid(x):
    return jnp.minimum(x, 0.0) - jnp.log(1.0 + jnp.exp(-jnp.abs(x)))


def _pick_row(x, i):
    row = lax.broadcasted_iota(jnp.int32, (x.shape[0], 1), 0)
    return jnp.sum(jnp.where(row == i, x, 0.0), axis=0, keepdims=True)


def _layer_norm(z, g, b):
    mu = jnp.mean(z, axis=-1, keepdims=True)
    zc = z - mu
    var = jnp.mean(zc * zc, axis=-1, keepdims=True)
    return zc * lax.rsqrt(var + LN_EPS) * g + b


def _qk_conv(halo, x, w0, w1, w2, w3, b):
    return _silu(causal_conv(halo, x, (w0, w1, w2, w3), b))


def _grp(i):
    return pl.ds(i * D_GRP, D_GRP)


def _mixer_specs(n_chunks, reverse):
    def chunk(c):
        return n_chunks - 1 - c if reverse else c
    row8 = CHUNK // SUBLANES
    proj_spec = pl.BlockSpec((CHUNK, D_IN_PAD), lambda c: (chunk(c), 0))
    halo_spec = pl.BlockSpec((SUBLANES, 2 * D_GRP), lambda c: (jnp.maximum(chunk(c) * row8 - 1, 0), 2))
    small = [pl.BlockSpec((2, D_GRP), lambda c: (0, 0)), pl.BlockSpec((1, D_GRP), lambda c: (0, 0)),
             pl.BlockSpec((ML_CONV, 2 * D_GRP), lambda c: (0, 0)), pl.BlockSpec((1, 2 * D_GRP), lambda c: (0, 0)),
             pl.BlockSpec((1, D_GRP), lambda c: (0, 0))]
    state_specs = [pl.BlockSpec((1, HEADS, DK, DK), lambda c: (chunk(c), 0, 0, 0)),
                   pl.BlockSpec((1, HEADS, DK, DK), lambda c: (chunk(c), 0, 0, 0)),
                   pl.BlockSpec((1, HEADS, 1, DK), lambda c: (chunk(c), 0, 0, 0)),
                   pl.BlockSpec((1, HEADS, 1, DK), lambda c: (chunk(c), 0, 0, 0))]
    y_spec = pl.BlockSpec((CHUNK, 2 * D_GRP), lambda c: (chunk(c), 0))
    return proj_spec, halo_spec, small, state_specs, y_spec, chunk


def _heads(x):
    return [x[:, h * DK:(h + 1) * DK] for h in range(HEADS)]


def _last(x, j):
    lane = lax.broadcasted_iota(jnp.int32, (1, x.shape[-1]), 1)
    return jnp.sum(jnp.where(lane == j, x, 0.0), axis=-1, keepdims=True)


def _hg_chunk(st_t, hq, hf, hi, hgate, l0, l1, nw):
    n = hq.shape[0]
    lb = _sigmoid(l0 - l1)
    q = _silu(hq)
    lf = jnp.log(lb + (1.0 - lb) * _sigmoid(hf))
    k = (1.0 - lb) * _sigmoid(-hf)
    b = cumsum_rows(lf)
    b_ref = _pick_row(b, n // 2 - 1)
    b_last = _pick_row(b, n - 1)
    qa, ka =_heads(q * jnp.exp(b - b_ref)), _heads(k * jnp.exp(b_ref - b))
    qe, kd, eb, v = _heads(q * jnp.exp(b)), _heads(k * jnp.exp(b_last - b)), _heads(jnp.exp(b_last)), _heads(hi)
    tri = _tri(n, True) > 0
    attn = [jnp.where(tri, mm_nt(qa[h], ka[h]), 0.0) for h in range(HEADS)]
    o = [mm_nn(attn[h], v[h]) + mm_nt(qe[h], st_t[h]) for h in range(HEADS)]
    st_new = jnp.stack([eb[h] * st_t[h] + mm_tn(v[h], kd[h]) for h in range(HEADS)])
    yn = [o[h] * lax.rsqrt(jnp.mean(o[h] * o[h], axis=-1, keepdims=True) + LN_EPS) for h in range(HEADS)]
    return st_new, jnp.concatenate(yn, axis=1) * nw * _silu(hgate)


def _ml_chunk(c_st, n_st, m_st, q, k, v, gates, og, nw):
    n = q.shape[0]
    ig = jnp.stack([_last(gates, h) for h in range(HEADS)])
    log_f = _log_sigmoid(gates)
    fl = jnp.stack([_last(log_f, HEADS + h) for h in range(HEADS)])
    bw = cumsum_rows(jnp.concatenate([jnp.broadcast_to(fl[h], (n, DK)) for h in range(HEADS)], axis=1))
    b = jnp.stack([_last(x, 0) for x in _heads(bw)])
    g = jnp.sum(fl, axis=1, keepdims=True)
    eye = lax.broadcasted_iota(jnp.int32, (n, n), 0) == lax.broadcasted_iota(jnp.int32, (n, n), 1)
    e_row = jnp.sum(jnp.where(eye, ig - b, 0.0), axis=1, keepdims=True)
    d = jnp.where(_tri(n, True) > 0, b + e_row, -jnp.inf)
    inter = b + m_st
    m_t = jnp.maximum(inter, jnp.max(d, axis=2, keepdims=True))
    qs, kh, vh = _heads(q * (DK ** -0.5)), _heads(k), _heads(v)
    s = jnp.stack([mm_nt(qs[h], kh[h]) for h in range(HEADS)]) * jnp.exp(d - m_t)
    w_inter = jnp.exp(inter - m_t)
    num = (jnp.stack([mm_nn(s[h], vh[h]) for h in range(HEADS)])
           + w_inter * jnp.stack([mm_nn(qs[h], c_st[h]) for h in range(HEADS)]))
    den = jnp.sum(s, axis=2, keepdims=True) + w_inter * jnp.sum(jnp.stack(qs) * n_st, axis=2, keepdims=True)
    h_out = num / jnp.maximum(jnp.abs(den), jnp.exp(-m_t))
    a = g - b + ig
    m_new = jnp.maximum(g + m_st, jnp.max(a, axis=1, keepdims=True))
    decay = jnp.exp(g + m_st - m_new)
    wk = jnp.stack(kh) * jnp.exp(a - m_new)
    c_new = decay * c_st + jnp.stack([mm_tn(wk[h], vh[h]) for h in range(HEADS)])
    n_new = decay * n_st + jnp.sum(wk, axis=1, keepdims=True)
    hc = h_out - jnp.mean(h_out, axis=-1, keepdims=True)
    yn = hc * lax.rsqrt(jnp.mean(hc * hc, axis=-1, keepdims=True) + LN_EPS)
    y = _sigmoid(og) * (jnp.concatenate([yn[h] for h in range(HEADS)], axis=1) * nw)
    return c_new, n_new, m_new, y


def _mixer_inputs(proj_ref, lg_ref, hnw_ref, mnw_ref, qk):
    hg_in = (proj_ref[:, _grp(0)], proj_ref[:, _grp(1)], proj_ref[:, _grp(2)], proj_ref[:, _grp(3)],
             lg_ref[0:1, :], lg_ref[1:2, :], hnw_ref[...])
    ml_in = (qk[:, :D_GRP], qk[:, D_GRP:], proj_ref[:, _grp(6)], proj_ref[:, pl.ds(8 * D_GRP, LANES)],
             proj_ref[:, _grp(7)], mnw_ref[...])
    return hg_in, ml_in


def _mixer_fwd(proj, lb_logits, hg_nw, conv_w, conv_b, ml_nw):
    seq = proj.shape[0]
    n_chunks = seq // CHUNK
    proj_spec, halo_spec, small, state_specs, y_spec, _ = _mixer_specs(n_chunks, False)

    def body(proj_ref, halo_ref, lg_ref, hnw_ref, cw_ref, cb_ref, mnw_ref,
             y_ref, hst_ref, cst_ref, nst_ref, mst_ref, hs, cs, ns, ms):
        c = pl.program_id(0)

        @pl.when(c == 0)
        def _():
            hs[...] = jnp.zeros_like(hs)
            cs[...] = jnp.zeros_like(cs)
            ns[...] = jnp.zeros_like(ns)
            ms[...] = jnp.full(ms.shape, NEG_BIG, F32)

        hst_ref[0] = hs[...]
        cst_ref[0] = cs[...]
        nst_ref[0] = ns[...]
        mst_ref[0] = ms[...]
        halo = jnp.where(c > 0, halo_ref[...], 0.0)
        qk = _qk_conv(halo, proj_ref[:, pl.ds(4 * D_GRP, 2 * D_GRP)],
                      cw_ref[0:1, :], cw_ref[1:2, :], cw_ref[2:3, :], cw_ref[3:4, :], cb_ref[...])
        hg_in, ml_in = _mixer_inputs(proj_ref, lg_ref, hnw_ref, mnw_ref, qk)
        hs[...], y_hg = _hg_chunk(hs[...], *hg_in)
        cs[...], ns[...], m_new, y_ml = _ml_chunk(cs[...], ns[...], _last(ms[...], 0), *ml_in)
        ms[...] = jnp.broadcast_to(m_new, ms.shape)
        y_ref[:, pl.ds(0, D_GRP)] = y_hg.astype(BF16)
        y_ref[:, pl.ds(D_GRP, D_GRP)] = y_ml.astype(BF16)

    st = jax.ShapeDtypeStruct((n_chunks, HEADS, DK, DK), F32)
    vec = jax.ShapeDtypeStruct((n_chunks, HEADS, 1, DK), F32)
    vmem = 2 * (_nbytes((CHUNK, D_IN_PAD), F32) + _nbytes((CHUNK, 2 * D_GRP), F32) + 2 * _nbytes((HEADS, DK, DK), F32)) \
        + 2 * _nbytes((HEADS, DK, DK), F32)
    return _pcall(
        body, name="mixer_fwd", grid=(n_chunks,),
        in_specs=[proj_spec, halo_spec] + small,
        out_specs=[y_spec] + state_specs,
        out_shape=[jax.ShapeDtypeStruct((seq, 2 * D_GRP), BF16), st, st, vec, vec],
        scratch_shapes=[pltpu.VMEM((HEADS, DK, DK), F32), pltpu.VMEM((HEADS, DK, DK), F32),
                        pltpu.VMEM((HEADS, 1, DK), F32), pltpu.VMEM((HEADS, 1, DK), F32)],
        compiler_params=_params(("arbitrary",), vmem),
    )(proj, proj, lb_logits, hg_nw, conv_w, conv_b, ml_nw)


def _mixer_bwd(proj, dy, hst, cst, nst, mst, lb_logits, hg_nw, conv_w, conv_b, ml_nw):
    seq = proj.shape[0]
    n_chunks = seq // CHUNK
    proj_spec, halo_spec, small, state_specs, y_spec, _ = _mixer_specs(n_chunks, True)

    def body(proj_ref, halo_ref, dy_ref, hst_ref, cst_ref, nst_ref, mst_ref,
             lg_ref, hnw_ref, cw_ref, cb_ref, mnw_ref,
             dproj_ref, dbin_ref, dlg_ref, dhnw_ref, dcw_ref, dcb_ref, dmnw_ref,
             dhs, dcs, dns, dms, dhalo):
        c = pl.program_id(0)

        @pl.when(c == 0)
        def _():
            for r in (dhs, dcs, dns, dms, dhalo, dbin_ref, dlg_ref, dhnw_ref, dcw_ref, dcb_ref, dmnw_ref):
                r[...] = jnp.zeros_like(r)

        def put(cols, val):
            dproj_ref[:, cols] = val.astype(BF16)
            dbin_ref[:, cols] += jnp.sum(val, axis=0, keepdims=True)

        first = c == n_chunks - 1
        halo = jnp.where(first, 0.0, halo_ref[...])
        x_qk = proj_ref[:, pl.ds(4 * D_GRP, 2 * D_GRP)]
        conv_args = (halo, x_qk, cw_ref[0:1, :], cw_ref[1:2, :], cw_ref[2:3, :], cw_ref[3:4, :], cb_ref[...])
        qk, conv_vjp = jax.vjp(_qk_conv, *conv_args)
        hg_in, ml_in = _mixer_inputs(proj_ref, lg_ref, hnw_ref, mnw_ref, qk)
        _, hg_vjp = jax.vjp(_hg_chunk, hst_ref[0], *hg_in)
        _, ml_vjp = jax.vjp(_ml_chunk, cst_ref[0], nst_ref[0], _last(mst_ref[0], 0), *ml_in)
        dst, dhq, dhf, dhi, dhg, dl0, dl1, dnw = hg_vjp((dhs[...], dy_ref[:, pl.ds(0, D_GRP)]))
        dc, dn, dm, dq, dk, dv, dgates, dog, dmn = ml_vjp(
            (dcs[...], dns[...], _last(dms[...], 0), dy_ref[:, pl.ds(D_GRP, D_GRP)]))
        dhs[...] = dst
        dcs[...] = dc
        dns[...] = dn
        dms[...] = jnp.broadcast_to(dm, dms.shape)
        for i, val in ((0, dhq), (1, dhf), (2, dhi), (3, dhg), (6, dv), (7, dog)):
            put(_grp(i), val)
        put(pl.ds(8 * D_GRP, LANES), dgates)
        dlg_ref[0:1, :] += dl0
        dlg_ref[1:2, :] += dl1
        dhnw_ref[...] += dnw
        dmnw_ref[...] += dmn
        dh, dx, dw0, dw1, dw2, dw3, db = conv_vjp(jnp.concatenate([dq, dk], axis=1))
        tail = jnp.concatenate([jnp.zeros((CHUNK - SUBLANES, 2 * D_GRP), F32), dhalo[...]], axis=0)
        put(pl.ds(4 * D_GRP, 2 * D_GRP), dx + tail)
        dhalo[...] = dh
        for d, dw in enumerate((dw0, dw1, dw2, dw3)):
            dcw_ref[d:d + 1, :] += dw
        dcb_ref[...] += db

    row = pl.BlockSpec((1, D_GRP), lambda c: (0, 0))
    small_out = [pl.BlockSpec((1, D_IN_PAD), lambda c: (0, 0)), pl.BlockSpec((2, D_GRP), lambda c: (0, 0)), row,
                 pl.BlockSpec((ML_CONV, 2 * D_GRP), lambda c: (0, 0)), pl.BlockSpec((1, 2 * D_GRP), lambda c: (0, 0)), row]
    dy_spec = pl.BlockSpec((CHUNK, 2 * D_GRP), y_spec.index_map)
    vmem = 2 * (2 * _nbytes((CHUNK, D_IN_PAD), F32) + _nbytes((CHUNK, 2 * D_GRP), F32)
                + 2 * _nbytes((HEADS, DK, DK), F32)) + 2 * _nbytes((HEADS, DK, DK), F32) + 4 * 1024 * 1024
    return _pcall(
        body, name="mixer_bwd", grid=(n_chunks,),
        in_specs=[proj_spec, halo_spec, dy_spec] + state_specs + small,
        out_specs=[proj_spec] + small_out,
        out_shape=[jax.ShapeDtypeStruct((seq, D_IN_PAD), BF16), jax.ShapeDtypeStruct((1, D_IN_PAD), F32),
                   jax.ShapeDtypeStruct((2, D_GRP), F32), jax.ShapeDtypeStruct((1, D_GRP), F32),
                   jax.ShapeDtypeStruct((ML_CONV, 2 * D_GRP), F32), jax.ShapeDtypeStruct((1, 2 * D_GRP), F32),
                   jax.ShapeDtypeStruct((1, D_GRP), F32)],
        scratch_shapes=[pltpu.VMEM((HEADS, DK, DK), F32), pltpu.VMEM((HEADS, DK, DK), F32),
                        pltpu.VMEM((HEADS, 1, DK), F32), pltpu.VMEM((HEADS, 1, DK), F32),
                        pltpu.VMEM((SUBLANES, 2 * D_GRP), F32)],
        compiler_params=_params(("arbitrary",), vmem),
    )(proj, proj, dy, hst, cst, nst, mst, lb_logits, hg_nw, conv_w, conv_b, ml_nw)


def _tile(n, prefs, unit=None):
    unit = unit or n
    for p in prefs:
        if unit % p == 0 and n % p == 0:
            return p
    return unit


def _logical(arr):
    return arr.shape if arr.ndim == 2 else (arr.shape[1], arr.shape[0] * arr.shape[2])


def _group(arr):
    return arr.shape[-1]


def _split_spec(ndim, group, tr, tc, where):
    if ndim == 2:
        return pl.BlockSpec((tr, tc), where)
    per = group // tc
    assert per * tc == group, (group, tc)

    def index(*ids):
        bi, bj = where(*ids)
        return (bj // per, bi, bj % per)
    return pl.BlockSpec((None, tr, tc), index)


def _mm(name, mode, a, b, *, bias=None, res=None, res_scale=1.0, ln=None, out_dtype=F32, out_groups=None,
        copy_dtype=None, a_copy_dtype=None, tm=None, tn=None, tk=None, fuse_b=False):
    la, lb = _logical(a), _logical(b)
    if mode == "nn":
        (m, k), n = la, lb[1]
        n_unit = _group(b) if b.ndim == 3 else n
        kc = _group(a) if a.ndim == 3 else k
    elif mode == "nt":
        (m, k), n = la, lb[0]
        n_unit = n
        kc = min(_group(a) if a.ndim == 3 else k, _group(b) if b.ndim == 3 else k)
    else:
        (k, m), n = la, lb[1]
        n_unit, kc = (_group(b) if b.ndim == 3 else n), k
        assert a.ndim == 2
    if out_groups:
        n_unit = min(n_unit, n // out_groups)
    kind = ln[0] if ln else None
    tm = tm or (256 if ln else _tile(m, (512, 256, 128)))
    tn = n if ln else (tn or _tile(n, (512, 384, 256, 128), n_unit))
    if mode != "tn":
        tk = k
    elif tk is None:
        tk = _tile(k, (4096, 2048, 512, 256, 128) if (m // tm) * (n // tn) > 1 else (2048, 512, 256, 128))
    gi, gj, gk = m // tm, n // tn, k // tk
    assert gi * tm == m and gj * tn == n and gk * tk == k and n_unit % tn == 0, (name, m, n, k, tm, tn, tk)
    ca, cb = {"nn": (1, 0), "nt": (1, 1), "tn": (0, 0)}[mode]
    i_outer = gk > 1 or (gi - 1) * _nbytes(b.shape, b.dtype) <= (gj - 1) * _nbytes(a.shape, a.dtype)

    def ij(where):
        return (lambda p, q, kk: where(p, q, kk)) if i_outer else (lambda p, q, kk: where(q, p, kk))
    if mode == "tn":
        a_spec = pl.BlockSpec((tk, tm), ij(lambda i, j, kk: (kk, i)))
    elif a.ndim == 3:
        a_spec = pl.BlockSpec((a.shape[0], tm, _group(a)), ij(lambda i, j, kk: (0, i, 0)))
    else:
        a_spec = pl.BlockSpec((tm, k), ij(lambda i, j, kk: (i, 0)))
    if mode != "nt":
        b_spec = _split_spec(b.ndim, _group(b), tk, tn, ij(lambda i, j, kk: (kk, j)))
    elif b.ndim == 3:
        b_spec = pl.BlockSpec((b.shape[0], tn, _group(b)), ij(lambda i, j, kk: (0, j, 0)))
    else:
        b_spec = pl.BlockSpec((tn, k), ij(lambda i, j, kk: (j, 0)))
    row_spec = pl.BlockSpec((1, tn), ij(lambda i, j, kk: (0, j)))
    blk_spec = pl.BlockSpec((tm, tn), ij(lambda i, j, kk: (i, j)))
    ins, in_specs = [a, b], [a_spec, b_spec]
    if bias is not None:
        ins.append(bias), in_specs.append(row_spec)
    if res is not None:
        ins.append(res), in_specs.append(blk_spec)
    if kind == "fwd":
        ins += [ln[1], ln[2]]
        in_specs += [row_spec, row_spec]
    elif kind == "bwd":
        ins += [ln[1], ln[2], ln[3]]
        in_specs += [blk_spec, row_spec, row_spec]
    if out_groups:
        blk_out = jax.ShapeDtypeStruct((out_groups, m, n // out_groups), out_dtype)
        out_spec = _split_spec(3, n // out_groups, tm, tn, ij(lambda i, j, kk: (i, j)))
    else:
        blk_out, out_spec = jax.ShapeDtypeStruct((m, n), out_dtype), blk_spec
    row_out = jax.ShapeDtypeStruct((1, n), F32)
    if kind is None:
        out_shape, out_specs = [blk_out], [out_spec]
    elif kind == "fwd":
        out_shape, out_specs = [blk_out, blk_out], [blk_spec, blk_spec]
    else:
        out_shape, out_specs = [blk_out, row_out, row_out], [blk_spec, row_spec, row_spec]
    if copy_dtype is not None:
        out_shape.append(jax.ShapeDtypeStruct((m, n), copy_dtype))
        out_specs.append(blk_spec)
    if a_copy_dtype is not None:
        assert mode != "tn" and a.ndim == 2 and copy_dtype is None
        out_shape.append(jax.ShapeDtypeStruct((m, k), a_copy_dtype))
        out_specs.append(a_spec)
    n_in = len(ins)

    def body(*refs):
        in_refs, out_refs, acc_ref = refs[:n_in], refs[n_in:n_in + len(out_shape)], refs[-1]
        i, kk = pl.program_id(0 if i_outer else 1), pl.program_id(2)
        a_ref, b_ref = in_refs[:2]
        extra = list(in_refs[2:])
        if a_copy_dtype is not None:
            out_refs[-1][...] = a_ref[...].astype(a_copy_dtype)

        def epilogue(acc):
            rest = list(extra)
            if bias is not None:
                acc = acc + rest.pop(0)[...]
            if res is not None:
                acc = acc + res_scale * rest.pop(0)[...]
            if kind is None:
                out_refs[0][...] = acc.astype(out_dtype)
                return
            if kind == "fwd":
                out_refs[0][...] = acc
                y = _layer_norm(acc, rest[0][...], rest[1][...])
                out_refs[1][...] = y
                if copy_dtype is not None:
                    out_refs[-1][...] = y.astype(copy_dtype)
                return
            _, vjp = jax.vjp(_layer_norm, rest[0][...], rest[1][...], rest[2][...])
            dz, dg, db = vjp(acc)
            out_refs[0][...] = dz
            out_refs[1][...] += dg
            out_refs[2][...] += db
            if copy_dtype is not None:
                out_refs[-1][...] = dz.astype(copy_dtype)

        if kind == "bwd":
            @pl.when((i == 0) & (kk == 0))
            def _():
                out_refs[1][...] = jnp.zeros_like(out_refs[1])
                out_refs[2][...] = jnp.zeros_like(out_refs[2])

        def chunk(ref, c0, last):
            if ref.ndim == 3:
                g = ref.shape[2]
                return ref[c0 // g, :, pl.ds(c0 % g, kc)]
            return ref[:, pl.ds(c0, kc)] if last else ref[pl.ds(c0, kc), :]

        if mode == "tn" or kc == k:
            prod = _dg(a_ref[...], b_ref[...], ca, cb)
        else:
            prod = None
            for c0 in range(0, k, kc):
                part = _dg(chunk(a_ref, c0, True), chunk(b_ref, c0, mode == "nt"), ca, cb)
                prod = part if prod is None else prod + part
        if gk == 1:
            epilogue(prod)
            return

        @pl.when(kk == 0)
        def _():
            acc_ref[...] = prod

        @pl.when(kk > 0)
        def _():
            acc_ref[...] += prod

        @pl.when(kk == gk - 1)
        def _():
            epilogue(acc_ref[...])

    vmem = (2 * (_nbytes((tm, tk), a.dtype) + _nbytes((tk, tn), b.dtype))
            + (2 * len(ins) + 2 * len(out_shape) + 1) * _nbytes((tm, tn), F32))
    outs = _pcall(
        body, name=name, grid=(gi, gj, gk) if i_outer else (gj, gi, gk), in_specs=in_specs, out_specs=out_specs,
        out_shape=out_shape, scratch_shapes=[pltpu.VMEM((tm, tn) if gk > 1 else (SUBLANES, LANES), F32)],
        fused=(1,) if fuse_b else (),
        compiler_params=_params(("arbitrary", "arbitrary", "arbitrary"), vmem,
                                [i == 1 for i in range(len(ins))] if fuse_b else None),
    )(*ins)
    return outs[0] if len(out_shape) == 1 else outs


STREAM_ROWS = 512
STREAM_AHEAD = 3


def _mm_tn_streamed(name, a, b):
    k, m = a.shape
    n = b.shape[1]
    nk = k // STREAM_ROWS
    assert nk * STREAM_ROWS == k and b.shape[0] == k

    def body(a_hbm, b_hbm, o_hbm, a_vm, b_vm, o_vm, sem, out_sem):
        def copies(i):
            rows = pl.ds(i * STREAM_ROWS, STREAM_ROWS)
            return (pltpu.make_async_copy(a_hbm.at[rows], a_vm.at[rows], sem.at[0, i]),
                    pltpu.make_async_copy(b_hbm.at[rows], b_vm.at[rows], sem.at[1, i]))

        for i in range(min(STREAM_AHEAD, nk)):
            for cp in copies(i):
                cp.start()
        for i in range(nk):
            for cp in copies(i):
                cp.wait()
            if i + STREAM_AHEAD < nk:
                for cp in copies(i + STREAM_AHEAD):
                    cp.start()
            rows = pl.ds(i * STREAM_ROWS, STREAM_ROWS)
            prod = _dg(a_vm[rows, :], b_vm[rows, :], 0, 0)
            if i == 0:
                o_vm[...] = prod
            else:
                o_vm[...] += prod
        out = pltpu.make_async_copy(o_vm, o_hbm, out_sem.at[0])
        out.start()
        out.wait()

    vmem = _nbytes(a.shape, a.dtype) + _nbytes(b.shape, b.dtype) + 2 * _nbytes((m, n), F32)
    return _pcall(
        body, name=name, in_specs=[ANY, ANY], out_specs=ANY, out_shape=jax.ShapeDtypeStruct((m, n), F32),
        scratch_shapes=[pltpu.VMEM(a.shape, a.dtype), pltpu.VMEM(b.shape, b.dtype), pltpu.VMEM((m, n), F32),
                        pltpu.SemaphoreType.DMA((2, nk)), pltpu.SemaphoreType.DMA((1,))],
        compiler_params=pltpu.CompilerParams(vmem_limit_bytes=int(vmem + 8 * 1024 * 1024)),
    )(a, b)


def _attn_head(q, k, v):
    sc = mm_nt(q, k) * (CA_DH ** -0.5)
    e = jnp.exp(sc - jnp.max(sc, axis=-1, keepdims=True))
    return mm_nn(e / jnp.sum(e, axis=-1, keepdims=True), v)


def _attn_fwd(q, kv):
    seq, n_mem = q.shape[0], kv.shape[0]
    tq = _tile(seq, (512, 256, 128))

    def body(q_ref, kv_ref, o_ref):
        for h in range(HEADS):
            hd = pl.ds(h * CA_DH, CA_DH)
            o = _attn_head(q_ref[:, hd], kv_ref[:, hd], kv_ref[:, pl.ds(D_MODEL + h * CA_DH, CA_DH)])
            o_ref[:, hd] = o.astype(BF16)

    return _pcall(
        body, name="attn_fwd", grid=(seq // tq,),
        in_specs=[pl.BlockSpec((tq, D_MODEL), lambda i: (i, 0)), pl.BlockSpec((n_mem, 2 * D_MODEL), lambda i: (0, 0))],
        out_specs=pl.BlockSpec((tq, D_MODEL), lambda i: (i, 0)), out_shape=jax.ShapeDtypeStruct((seq, D_MODEL), BF16),
        compiler_params=_params(("arbitrary",), 4 * _nbytes((tq, D_MODEL), F32) + 2 * _nbytes((n_mem, 2 * D_MODEL), F32)),
    )(q, kv)


def _attn_bwd(q, kv, do):
    seq, n_mem = q.shape[0], kv.shape[0]
    tq = _tile(seq, (512, 256, 128))

    def body(q_ref, kv_ref, do_ref, dq_ref, dkv_ref):
        @pl.when(pl.program_id(0) == 0)
        def _():
            dkv_ref[...] = jnp.zeros_like(dkv_ref)

        for h in range(HEADS):
            hd = pl.ds(h * CA_DH, CA_DH)
            vd = pl.ds(D_MODEL + h * CA_DH, CA_DH)
            _, vjp = jax.vjp(_attn_head, q_ref[:, hd], kv_ref[:, hd], kv_ref[:, vd])
            dq, dk, dv = vjp(do_ref[:, hd].astype(F32))
            dq_ref[:, hd] = dq.astype(BF16)
            dkv_ref[:, hd] += dk
            dkv_ref[:, vd] += dv

    return _pcall(
        body, name="attn_bwd", grid=(seq // tq,),
        in_specs=[pl.BlockSpec((tq, D_MODEL), lambda i: (i, 0)), pl.BlockSpec((n_mem, 2 * D_MODEL), lambda i: (0, 0)),
                  pl.BlockSpec((tq, D_MODEL), lambda i: (i, 0))],
        out_specs=[pl.BlockSpec((tq, D_MODEL), lambda i: (i, 0)), pl.BlockSpec((n_mem, 2 * D_MODEL), lambda i: (0, 0))],
        out_shape=[jax.ShapeDtypeStruct((seq, D_MODEL), BF16), jax.ShapeDtypeStruct((n_mem, 2 * D_MODEL), F32)],
        compiler_params=_params(("arbitrary",), 6 * _nbytes((tq, D_MODEL), F32) + 4 * _nbytes((n_mem, 2 * D_MODEL), F32)),
    )(q, kv, do)


def _ffn_mid(hg, xg, hv, xv, wg0, wg1, wg2, bg, wv0, wv1, wv2, bv):
    return jax.nn.gelu(causal_conv(hg, xg, (wg0, wg1, wg2), bg)) * causal_conv(hv, xv, (wv0, wv1, wv2), bv)


FFN_TB = 256
FFN_W = D_FF // 2
FFN_J = D_FF // FFN_W
MXU_COLS = 256
FFN_PIECES = tuple((off, min(MXU_COLS, FFN_W - off)) for off in range(0, FFN_W, MXU_COLS))


def _ffn_common_specs(seq, row):
    tb = min(FFN_TB, seq)
    full = pl.BlockSpec((tb, D_MODEL), lambda t, j: (row(t), 0))
    vec = pl.BlockSpec((1, D_MODEL), lambda t, j: (0, 0))
    halves = []
    for off in (0, FFN_J):
        halves.append(dict(
            w_up=pl.BlockSpec((None, D_MODEL, FFN_W), lambda t, j, off=off: (j + off, 0, 0)),
            taps=pl.BlockSpec((FFN_CONV, FFN_W), lambda t, j, off=off: (0, j + off)),
            bias=pl.BlockSpec((1, FFN_W), lambda t, j, off=off: (0, j + off))))
    w_down = pl.BlockSpec((FFN_W, D_MODEL), lambda t, j: (j, 0))
    u_blk = pl.BlockSpec((2, tb, FFN_W), lambda t, j: (0, row(t), j))
    return tb, full, vec, halves, w_down, u_blk


def _ffn_vmem(tb):
    return (_nbytes((2, tb, FFN_W), F32) + _nbytes((2, tb, FFN_W), BF16) + 3 * _nbytes((D_MODEL, FFN_W), BF16)
            + 10 * _nbytes((tb, D_MODEL), F32))


def _conv_params(taps_ref, bias_ref, cols):
    return taps_ref[0:1, cols], taps_ref[1:2, cols], taps_ref[2:3, cols], bias_ref[:, cols]


def _ffn_fwd(x2b, x2, w_up, conv_w, conv_b, w_down, ln_g, ln_b, target):
    seq = x2.shape[0]
    tb, full, vec, halves, wd_spec, u_blk = _ffn_common_specs(seq, lambda t: t)
    nt = seq // tb

    def body(xb_ref, wg_ref, wv_ref, tg_ref, tv_ref, bg_ref, bv_ref, wd_ref, x_ref, g_ref, b_ref, tgt_ref,
             u_ref, h_ref, dz_ref, dg_ref, db_ref, loss_ref, dzb_ref, acc, carry):
        t, j = pl.program_id(0), pl.program_id(1)
        xb = xb_ref[...]
        pieces = [pl.ds(off, width) for off, width in FFN_PIECES]
        ug = [_dg(xb, wg_ref[:, cols], 1, 0) for cols in pieces]
        uv = [_dg(xb, wv_ref[:, cols], 1, 0) for cols in pieces]
        hs = []
        for cols, g, v in zip(pieces, ug, uv):
            u_ref[0, :, cols] = g
            u_ref[1, :, cols] = v
            halo_g = jnp.where(t == 0, 0.0, carry[j, 0, :, cols])
            halo_v = jnp.where(t == 0, 0.0, carry[j, 1, :, cols])
            h = _ffn_mid(halo_g, g, halo_v, v, *_conv_params(tg_ref, bg_ref, cols),
                         *_conv_params(tv_ref, bv_ref, cols)).astype(BF16)
            carry[j, 0, :, cols] = g[tb - SUBLANES:, :]
            carry[j, 1, :, cols] = v[tb - SUBLANES:, :]
            h_ref[:, cols] = h
            hs.append(h)
        part = None
        for cols, h in zip(pieces, hs):
            p = _dg(h, wd_ref[cols, :], 1, 0)
            part = p if part is None else part + p

        @pl.when(j == 0)
        def _():
            acc[...] = part

        @pl.when(j > 0)
        def _():
            acc[...] += part

        @pl.when(j == FFN_J - 1)
        def _():
            y, vjp = jax.vjp(_layer_norm, acc[...] + ALPHA * x_ref[...], g_ref[...], b_ref[...])
            err = y - tgt_ref[...]
            part_loss = 0.5 * jnp.sum(jnp.sum(err * err, axis=1, keepdims=True), axis=0, keepdims=True) / D_MODEL
            dz, dg, db = vjp(err / D_MODEL)

            @pl.when(t == 0)
            def _():
                for r in (dg_ref, db_ref, loss_ref):
                    r[...] = jnp.zeros_like(r)

            dz_ref[...] = dz
            dzb_ref[...] = dz.astype(BF16)
            dg_ref[...] += dg
            db_ref[...] += db
            loss_ref[...] += jnp.broadcast_to(part_loss, (1, LANES))

    h0, h1 = halves
    row = jax.ShapeDtypeStruct((1, D_MODEL), F32)
    return _pcall(
        body, name="ffn_fwd", grid=(nt, FFN_J),
        in_specs=[full, h0["w_up"], h1["w_up"], h0["taps"], h1["taps"], h0["bias"], h1["bias"], wd_spec, full, vec, vec,
                  full],
        out_specs=[u_blk, pl.BlockSpec((tb, FFN_W), lambda t, j: (t, j)), full, vec, vec,
                   pl.BlockSpec((1, LANES), lambda t, j: (0, 0)), full],
        out_shape=[jax.ShapeDtypeStruct((2, seq, D_FF), F32), jax.ShapeDtypeStruct((seq, D_FF), BF16),
                   jax.ShapeDtypeStruct((seq, D_MODEL), F32), row, row, jax.ShapeDtypeStruct((1, LANES), F32),
                   jax.ShapeDtypeStruct((seq, D_MODEL), BF16)],
        scratch_shapes=[pltpu.VMEM((tb, D_MODEL), F32), pltpu.VMEM((FFN_J, 2, SUBLANES, FFN_W), F32)],
        compiler_params=_params(("arbitrary", "arbitrary"), _ffn_vmem(tb)),
    )(x2b, w_up, w_up, conv_w, conv_w, conv_b, conv_b, w_down, x2, ln_g, ln_b, target)


def _ffn_bwd(u, conv_w, conv_b, dz3b, dz3, w_down, w_up, z2, ln_g, ln_b):
    seq = dz3.shape[0]
    tb = min(FFN_TB, seq)
    nt = seq // tb
    row8 = tb // SUBLANES
    tb, full, vec, halves, wd_spec, u_blk = _ffn_common_specs(seq, lambda t: nt - 1 - t)
    halo = pl.BlockSpec((2, SUBLANES, FFN_W), lambda t, j: (0, jnp.maximum((nt - 1 - t) * row8 - 1, 0), j))

    def body(u_ref, halo_ref, tg_ref, tv_ref, bg_ref, bv_ref, dzb_ref, wd_ref, wg_ref, wv_ref, dz3_ref, z_ref, g_ref,
             b_ref, du_ref, dw_ref, dbias_ref, dz_ref, dg_ref, db_ref, dz2b_ref, acc, carry):
        t, j = pl.program_id(0), pl.program_id(1)

        @pl.when((t == 0) & (j == 0))
        def _():
            for r in (dw_ref, dbias_ref, dg_ref, db_ref):
                r[...] = jnp.zeros_like(r)

        pieces = [pl.ds(off, width) for off, width in FFN_PIECES]
        dzb = dzb_ref[...]
        dhs = [_dg(dzb, wd_ref[cols, :], 1, 1) for cols in pieces]
        first = t == nt - 1
        dus = []
        for cols, dh in zip(pieces, dhs):
            args = (jnp.where(first, 0.0, halo_ref[0, :, cols]), u_ref[0, :, cols],
                    jnp.where(first, 0.0, halo_ref[1, :, cols]), u_ref[1, :, cols],
                    *_conv_params(tg_ref, bg_ref, cols), *_conv_params(tv_ref, bv_ref, cols))
            _, vjp = jax.vjp(_ffn_mid, *args)
            dhg, dxg, dhv, dxv, g0, g1, g2, gb, v0, v1, v2, vb = vjp(dh)
            zeros = jnp.zeros((tb - SUBLANES, dh.shape[1]), F32)
            dug = (dxg + jnp.concatenate([zeros, jnp.where(t == 0, 0.0, carry[j, 0, :, cols])], axis=0)).astype(BF16)
            duv = (dxv + jnp.concatenate([zeros, jnp.where(t == 0, 0.0, carry[j, 1, :, cols])], axis=0)).astype(BF16)
            carry[j, 0, :, cols] = dhg
            carry[j, 1, :, cols] = dhv
            du_ref[0, :, cols] = dug
            du_ref[1, :, cols] = duv
            for half, parts in enumerate(((g0, g1, g2), (v0, v1, v2))):
                for d, p in enumerate(parts):
                    dw_ref[j, half, d:d + 1, cols] += p
            dbias_ref[j, 0, :, cols] += gb
            dbias_ref[j, 1, :, cols] += vb
            dus.append((dug, duv))
        part = None
        for cols, (dug, duv) in zip(pieces, dus):
            p = _dg(dug, wg_ref[:, cols], 1, 1) + _dg(duv, wv_ref[:, cols], 1, 1)
            part = p if part is None else part + p

        @pl.when(j == 0)
        def _():
            acc[...] = part

        @pl.when(j > 0)
        def _():
            acc[...] += part

        @pl.when(j == FFN_J - 1)
        def _():
            _, ln_vjp = jax.vjp(_layer_norm, z_ref[...], g_ref[...], b_ref[...])
            dz, dg, db = ln_vjp(acc[...] + ALPHA * dz3_ref[...])
            dz_ref[...] = dz
            dz2b_ref[...] = dz.astype(BF16)
            dg_ref[...] += dg
            db_ref[...] += db

    h0, h1 = halves
    row = jax.ShapeDtypeStruct((1, D_MODEL), F32)
    whole = lambda *shape: pl.BlockSpec(shape, lambda t, j: (0,) * len(shape))
    return _pcall(
        body, name="ffn_bwd", grid=(nt, FFN_J),
        in_specs=[u_blk, halo, h0["taps"], h1["taps"], h0["bias"], h1["bias"], full, wd_spec, h0["w_up"], h1["w_up"],
                  full, full, vec, vec],
        out_specs=[u_blk, whole(FFN_J, 2, FFN_CONV, FFN_W), whole(FFN_J, 2, 1, FFN_W), full, vec, vec, full],
        out_shape=[jax.ShapeDtypeStruct((2, seq, D_FF), BF16), jax.ShapeDtypeStruct((FFN_J, 2, FFN_CONV, FFN_W), F32),
                   jax.ShapeDtypeStruct((FFN_J, 2, 1, FFN_W), F32), jax.ShapeDtypeStruct((seq, D_MODEL), F32), row, row,
                   jax.ShapeDtypeStruct((seq, D_MODEL), BF16)],
        scratch_shapes=[pltpu.VMEM((tb, D_MODEL), F32), pltpu.VMEM((FFN_J, 2, SUBLANES, FFN_W), F32)],
        compiler_params=_params(("arbitrary", "arbitrary"), _ffn_vmem(tb)),
    )(u, u, conv_w, conv_w, conv_b, conv_b, dz3b, w_down, w_up, w_up, dz3, z2, ln_g, ln_b)


def _adamw_math(w, g, m, v):
    m_new = ADAM_B1 * m + (1.0 - ADAM_B1) * g
    v_new = ADAM_B2 * v + (1.0 - ADAM_B2) * jnp.square(g)
    m_hat = m_new / (1.0 - ADAM_B1 ** ADAM_STEP)
    v_hat = v_new / (1.0 - ADAM_B2 ** ADAM_STEP)
    return -ADAM_LR * (m_hat / (jnp.sqrt(v_hat) + ADAM_EPS) + ADAM_WD * w), m_new, v_new


def _adamw_many(name, ws, gs, ms, vs):
    n = len(ws)

    def body(*refs):
        w_refs, g_refs, m_refs, v_refs = (refs[i * n:(i + 1) * n] for i in range(4))
        d_refs, nm_refs, nv_refs = (refs[(4 + i) * n:(5 + i) * n] for i in range(3))
        for i in range(n):
            d_refs[i][...], nm_refs[i][...], nv_refs[i][...] = _adamw_math(
                w_refs[i][...], g_refs[i][...], m_refs[i][...], v_refs[i][...])

    vm = pl.BlockSpec(memory_space=pltpu.VMEM)
    outs = _pcall(
        body, pin=False, name=name, in_specs=[vm] * (4 * n), out_specs=[vm] * (3 * n),
        out_shape=[jax.ShapeDtypeStruct(w.shape, F32) for w in ws] * 3,
    )(*ws, *gs, *ms, *vs)
    return outs[:n], outs[n:2 * n], outs[2 * n:]


def _adamw_halves(name, core, w, mine, theirs, m, v):
    rows, cols = w.shape
    half_rows = mine.shape[0]
    tr = _tile(half_rows, (256, 176, 128))
    nbh = half_rows // tr
    assert 2 * half_rows == rows

    def body(c_ref, w_ref, a_ref, b_ref, m_ref, v_ref, g_ref, d_ref, nm_ref, nv_ref):
        g = jnp.where(pl.program_id(0) // nbh == c_ref[0], a_ref[...], b_ref[...])
        g_ref[...] = g
        d_ref[...], nm_ref[...], nv_ref[...] = _adamw_math(w_ref[...], g, m_ref[...], v_ref[...])

    spec = pl.BlockSpec((tr, cols), lambda i, c_ref: (i, 0))
    half = pl.BlockSpec((tr, cols), lambda i, c_ref: (i % nbh, 0))
    sh = jax.ShapeDtypeStruct((rows, cols), F32)
    grid_spec = pltpu.PrefetchScalarGridSpec(
        num_scalar_prefetch=1, grid=(rows // tr,), in_specs=[spec, half, half, spec, spec], out_specs=[spec] * 4)
    return _pcall(
        body, name=name, grid_spec=grid_spec, out_shape=[sh] * 4,
        compiler_params=_params(("arbitrary",), 18 * _nbytes((tr, -(-cols // LANES) * LANES), F32)),
    )(core, w, mine, theirs, m, v)


MESH = pl.DeviceIdType.MESH
ANY = pl.BlockSpec(memory_space=pl.ANY)
N_CHIPS = 4
BF16_ROWS = 16


def _me():
    return lax.axis_index("x"), lax.axis_index("y"), lax.axis_index("c")


def _other_chips(x, y):
    return [(1 - x, y), (x, 1 - y), (1 - x, 1 - y)]


def _remote(src, dst, ssem, rsem, dev):
    return pltpu.make_async_remote_copy(src_ref=src, dst_ref=dst, send_sem=ssem, recv_sem=rsem,
                                        device_id=dev, device_id_type=MESH)


def _half_rows(ref_rows, cc):
    half = ref_rows // 2
    return pl.ds(pl.multiple_of(cc * half, BF16_ROWS), half)


def _gather_weights(shards):
    n = len(shards)
    n_ici = n * (N_CHIPS - 1)

    def body(*refs):
        ins, outs, (ssem, rsem, lsem, lrsem) = refs[:n], refs[n:2 * n], refs[2 * n:]
        x, y, c = _me()
        k_me = 2 * x + y
        sib = (x, y, 1 - c)
        chips = _other_chips(x, y)
        started = []
        for i, (w_ref, o_ref) in enumerate(zip(ins, outs)):
            cp = _remote(w_ref, o_ref.at[k_me], lsem.at[i], lrsem.at[i], sib)
            cp.start()
            started.append(cp)
        for r, (px, py) in enumerate(chips):
            for i, (w_ref, o_ref) in enumerate(zip(ins, outs)):
                rows = _half_rows(w_ref.shape[0], c)
                s = r * n + i
                cp = _remote(w_ref.at[rows], o_ref.at[k_me, rows], ssem.at[s], rsem.at[s], (px, py, c))
                cp.start()
                started.append(cp)
        for r, (px, py) in enumerate(chips):
            for i, o_ref in enumerate(outs):
                blk = o_ref.at[2 * px + py, _half_rows(o_ref.shape[1], c)]
                s = r * n + i
                _remote(blk, blk, ssem.at[s], rsem.at[s], (px, py, c)).wait_recv()
                cp = _remote(blk, blk, ssem.at[n_ici + s], rsem.at[n_ici + s], sib)
                cp.start()
                started.append(cp)
        for r, (px, py) in enumerate(chips):
            for i, o_ref in enumerate(outs):
                blk = o_ref.at[2 * px + py, _half_rows(o_ref.shape[1], 1 - c)]
                s = n_ici + r * n + i
                _remote(blk, blk, ssem.at[s], rsem.at[s], sib).wait_recv()
        for cp in started[n:]:
            cp.wait_send()
        for cp in started[:n]:
            cp.wait()

    return _pcall(
        body, name="gather_weights", in_specs=[ANY] * n, out_specs=[ANY] * n,
        out_shape=[jax.ShapeDtypeStruct((N_CHIPS,) + s.shape, s.dtype) for s in shards],
        scratch_shapes=[pltpu.SemaphoreType.DMA((2 * n_ici,)), pltpu.SemaphoreType.DMA((2 * n_ici,)),
                        pltpu.SemaphoreType.DMA((n,)), pltpu.SemaphoreType.DMA((n,))],
    )(*shards)


def _swap_halves(name, grads):
    n = len(grads)

    def body(*refs):
        ins, outs, (ssem, rsem) = refs[:n], refs[n:2 * n], refs[2 * n:]
        x, y, c = _me()
        copies = []
        for i, (g_ref, o_ref) in enumerate(zip(ins, outs)):
            for k in range(N_CHIPS):
                s = i * N_CHIPS + k
                cp = _remote(g_ref.at[k, _half_rows(g_ref.shape[1], 1 - c)], o_ref.at[k], ssem.at[s], rsem.at[s],
                             (x, y, 1 - c))
                cp.start()
                copies.append(cp)
        for cp in copies:
            cp.wait()

    return _pcall(
        body, name=name, in_specs=[ANY] * n, out_specs=[ANY] * n,
        out_shape=[jax.ShapeDtypeStruct((N_CHIPS, g.shape[1] // 2, g.shape[2]), g.dtype) for g in grads],
        scratch_shapes=[pltpu.SemaphoreType.DMA((n * N_CHIPS,)), pltpu.SemaphoreType.DMA((n * N_CHIPS,))],
    )(*grads)


SEM = pl.BlockSpec(memory_space=pltpu.SEMAPHORE)
IN_HBM = pl.BlockSpec(memory_space=pltpu.HBM)
SPLIT_PARAMS = dict(compiler_params=pltpu.CompilerParams(has_side_effects=pltpu.SideEffectType.DATAFLOW_SIDE_EFFECTING))


def _split_start(name, sources, landings, n_copies, plan):
    ns, nl = len(sources), len(landings)

    def body(*refs):
        ins, lands, (ssem, rsem), token = refs[:ns], refs[ns:ns + nl], refs[ns + nl:ns + nl + 2], refs[-1]
        for s, (src, dst, _, dev) in enumerate(plan(ins, lands)):
            _remote(src, dst, ssem.at[s], rsem.at[s], dev).start()
        token[...] = jnp.zeros_like(token)

    arrays = list(sources) + list(landings)
    outs = _call(
        body, name=name, in_specs=[IN_HBM] * (ns + nl),
        out_specs=[SEM, SEM] + [IN_HBM] * (ns + nl) + [pl.BlockSpec(memory_space=pltpu.VMEM)],
        out_shape=[pltpu.SemaphoreType.DMA((n_copies,)), pltpu.SemaphoreType.DMA((n_copies,))]
        + [pltpu.HBM(a.shape, a.dtype) for a in arrays] + [jax.ShapeDtypeStruct((SUBLANES, LANES), F32)],
        input_output_aliases={i: 2 + i for i in range(ns + nl)}, **SPLIT_PARAMS,
    )(*[pltpu.with_memory_space_constraint(a, pltpu.HBM) for a in arrays])
    return (outs[:-1], ns), outs[-1]


def _split_wait(name, handle, after, plan):
    (ssem, rsem, *thru), ns = handle
    nl = len(thru) - ns

    def body(*refs):
        ins, lands, (ssem_ref, rsem_ref) = refs[:ns], refs[ns:ns + nl], refs[ns + nl:ns + nl + 2]
        for s, (src, _, dst, dev) in enumerate(plan(ins, lands)):
            cp = _remote(src, dst, ssem_ref.at[s], rsem_ref.at[s], dev)
            cp.wait_send()
            cp.wait_recv()

    outs = _call(
        body, name=name, in_specs=[IN_HBM] * (ns + nl) + [SEM, SEM, ANY], out_specs=[IN_HBM] * (ns + nl),
        out_shape=[pltpu.HBM(t.shape, t.dtype) for t in thru],
        input_output_aliases={i: i for i in range(ns + nl)}, **SPLIT_PARAMS,
    )(*thru, ssem, rsem, after)
    return outs[:ns], outs[ns:]


def _swap_plan(ins, lands):
    x, y, c = _me()
    return [(g_ref.at[k, _half_rows(g_ref.shape[1], 1 - c)], l_ref.at[k], l_ref.at[k], (x, y, 1 - c))
            for g_ref, l_ref in zip(ins, lands) for k in range(N_CHIPS)]


def _swap_start(name, grads):
    lands = [lax.empty((N_CHIPS, g.shape[1] // 2, g.shape[2]), g.dtype) for g in grads]
    return _split_start(name, grads, lands, len(grads) * N_CHIPS, _swap_plan)


def _swap_wait(name, handle, after):
    return _split_wait(name, handle, after, _swap_plan)


def _gather_plan(ins, lands):
    x, y, c = _me()
    k_me = 2 * x + y
    plan = [(w_ref, l_ref.at[k_me], l_ref.at[k_me], (x, y, 1 - c)) for w_ref, l_ref in zip(ins, lands)]
    for px, py in _other_chips(x, y):
        for w_ref, l_ref in zip(ins, lands):
            rows = _half_rows(w_ref.shape[0], c)
            plan.append((w_ref.at[rows], l_ref.at[k_me, rows], l_ref.at[2 * px + py, rows], (px, py, c)))
    return plan


def _gather_start(name, shards):
    lands = [lax.empty((N_CHIPS,) + s.shape, s.dtype) for s in shards]
    return _split_start(name, shards, lands, len(shards) * N_CHIPS, _gather_plan)


def _gather_wait(name, handle, after):
    return _split_wait(name, handle, after, _gather_plan)[1]


def _forward_halves(name, blocks):
    n = len(blocks)
    n_sem = n * (N_CHIPS - 1)

    def body(*refs):
        outs, (ssem, rsem) = refs[n:2 * n], refs[2 * n:]
        x, y, c = _me()
        sib = (x, y, 1 - c)
        chips = _other_chips(x, y)
        sends = []
        for r, (px, py) in enumerate(chips):
            for i, o_ref in enumerate(outs):
                blk = o_ref.at[2 * px + py, _half_rows(o_ref.shape[1], c)]
                cp = _remote(blk, blk, ssem.at[r * n + i], rsem.at[r * n + i], sib)
                cp.start()
                sends.append(cp)
        for r, (px, py) in enumerate(chips):
            for i, o_ref in enumerate(outs):
                blk = o_ref.at[2 * px + py, _half_rows(o_ref.shape[1], 1 - c)]
                _remote(blk, blk, ssem.at[r * n + i], rsem.at[r * n + i], sib).wait_recv()
        for cp in sends:
            cp.wait_send()

    return _pcall(
        body, name=name, in_specs=[ANY] * n, out_specs=[ANY] * n,
        out_shape=[jax.ShapeDtypeStruct(b.shape, b.dtype) for b in blocks],
        input_output_aliases={i: i for i in range(n)},
        scratch_shapes=[pltpu.SemaphoreType.DMA((n_sem,)), pltpu.SemaphoreType.DMA((n_sem,))],
    )(*blocks)


def _scatter_plan(ins, lands):
    x, y, c = _me()
    k_me = 2 * x + y
    return [(p_ref.at[2 * px + py], l_ref.at[k_me], l_ref.at[2 * px + py], (px, py, c))
            for px, py in _other_chips(x, y) for p_ref, l_ref in zip(ins, lands)]


def _scatter_start(name, parts):
    lands = [lax.empty(p.shape, p.dtype) for p in parts]
    return _split_start(name, parts, lands, len(parts) * (N_CHIPS - 1), _scatter_plan)


def _scatter_wait(name, handle, after):
    return _split_wait(name, handle, after, _scatter_plan)[1]


def _share_and_reduce(halves, v):
    n = len(halves)
    rows = v.shape[0]
    half = rows // 2
    assert half % SUBLANES == 0

    def body(*refs):
        ins, v_ref, outs, out_ref = refs[:n], refs[n], refs[n + 1:2 * n + 1], refs[2 * n + 1]
        pair_buf, mine, chip_buf, ssem, rsem, half_ssem, half_rsem = refs[2 * n + 2:]
        x, y, c = _me()
        k_me = 2 * x + y
        sib = (x, y, 1 - c)
        copies = [_remote(r_ref, o_ref, half_ssem.at[i], half_rsem.at[i], sib)
                  for i, (r_ref, o_ref) in enumerate(zip(ins, outs))]
        for cp in copies:
            cp.start()

        def rows_of(cc):
            return pl.ds(pl.multiple_of(cc * half, SUBLANES), half)

        swap = _remote(v_ref.at[rows_of(1 - c)], pair_buf, ssem.at[0], rsem.at[0], sib)
        swap.start()
        swap.wait()
        mine[...] = v_ref[rows_of(c), :] + pair_buf[...]
        chip_buf[k_me] = mine[...]
        sends = [_remote(mine, chip_buf.at[k_me], ssem.at[1 + r], rsem.at[1 + r], (px, py, c))
                 for r, (px, py) in enumerate(_other_chips(x, y))]
        for cp in sends:
            cp.start()
        for r, (px, py) in enumerate(_other_chips(x, y)):
            blk = chip_buf.at[2 * px + py]
            _remote(blk, blk, ssem.at[1 + r], rsem.at[1 + r], (px, py, c)).wait_recv()
        total = chip_buf[0]
        for k in range(1, N_CHIPS):
            total = total + chip_buf[k]
        out_ref[rows_of(c), :] = total
        for cp in sends:
            cp.wait_send()
        share = _remote(out_ref.at[rows_of(c)], out_ref.at[rows_of(c)], ssem.at[N_CHIPS], rsem.at[N_CHIPS], sib)
        share.start()
        got = out_ref.at[rows_of(1 - c)]
        _remote(got, got, ssem.at[N_CHIPS], rsem.at[N_CHIPS], sib).wait_recv()
        share.wait_send()
        for cp in copies:
            cp.wait()

    vm = pl.BlockSpec(memory_space=pltpu.VMEM)
    outs = _call(
        body, name="share_and_reduce", in_specs=[ANY] * n + [vm], out_specs=[ANY] * n + [vm],
        out_shape=[pltpu.HBM(h.shape, h.dtype) for h in halves] + [jax.ShapeDtypeStruct((rows, LANES), F32)],
        scratch_shapes=[pltpu.VMEM((half, LANES), F32), pltpu.VMEM((half, LANES), F32),
                        pltpu.VMEM((N_CHIPS, half, LANES), F32), pltpu.SemaphoreType.DMA((N_CHIPS + 1,)),
                        pltpu.SemaphoreType.DMA((N_CHIPS + 1,)), pltpu.SemaphoreType.DMA((n,)),
                        pltpu.SemaphoreType.DMA((n,))],
        compiler_params=pltpu.CompilerParams(vmem_limit_bytes=32 * 1024 * 1024),
    )(*[pltpu.with_memory_space_constraint(h, pltpu.HBM) for h in halves], v)
    return outs[:n], outs[n]


def _add_pair(name, core, chip, g, theirs):
    _, half, cols = theirs.shape
    tr = _tile(half, (256, 176, 128))
    nb = half // tr

    def body(c_ref, k_ref, g_ref, t_ref, o32_ref, o16_ref):
        s = g_ref[...] + t_ref[...]
        o16_ref[...] = s.astype(BF16)

        @pl.when(pl.program_id(1) == k_ref[0])
        def _():
            o32_ref[...] = s

    spec = pl.BlockSpec((None, tr, cols), lambda i, k, c_ref, k_ref: (k, i, 0))
    grid_spec = pltpu.PrefetchScalarGridSpec(
        num_scalar_prefetch=2, grid=(nb, N_CHIPS),
        in_specs=[pl.BlockSpec((None, tr, cols), lambda i, k, c_ref, k_ref: (k, c_ref[0] * nb + i, 0)), spec],
        out_specs=[pl.BlockSpec((tr, cols), lambda i, k, c_ref, k_ref: (i, 0)), spec])
    return _pcall(
        body, name=name, grid_spec=grid_spec,
        out_shape=[jax.ShapeDtypeStruct((half, cols), F32), jax.ShapeDtypeStruct(theirs.shape, BF16)],
        compiler_params=_params(("arbitrary", "arbitrary"), 8 * _nbytes((tr, cols + LANES), F32)),
    )(core, chip, g, theirs)


def _add_chips(name, chip, p32, recv):
    half, cols = p32.shape
    tr = _tile(half, (256, 176, 128))

    def body(k_ref, p_ref, r0_ref, r1_ref, r2_ref, o_ref):
        o_ref[...] = ((p_ref[...] + r0_ref[...].astype(F32)) + r1_ref[...].astype(F32)) + r2_ref[...].astype(F32)

    def other(r):
        return pl.BlockSpec((None, tr, cols), lambda i, k_ref: (r + (k_ref[0] <= r).astype(jnp.int32), i, 0))
    grid_spec = pltpu.PrefetchScalarGridSpec(
        num_scalar_prefetch=1, grid=(half // tr,),
        in_specs=[pl.BlockSpec((tr, cols), lambda i, k_ref: (i, 0)), other(0), other(1), other(2)],
        out_specs=pl.BlockSpec((tr, cols), lambda i, k_ref: (i, 0)))
    return _pcall(
        body, name=name, grid_spec=grid_spec, out_shape=jax.ShapeDtypeStruct((half, cols), F32),
        compiler_params=_params(("arbitrary",), 10 * _nbytes((tr, cols + LANES), F32)),
    )(chip, p32, recv, recv, recv)


def kernel(x, mem, w_in, b_in, hg_lb_logits, hg_norm_w, ml_conv_w, ml_conv_b, ml_norm_w, w_out, ln1_g, ln1_b, ca_wq, ca_wkv, ca_wo, ln2_g, ln2_b, ffn_w_up, ffn_conv_w, ffn_conv_b, ffn_w_down, ln3_g, ln3_b, loss_target, m_w_in, m_b_in, m_hg_lb_logits, m_hg_norm_w, m_ml_conv_w, m_ml_conv_b, m_ml_norm_w, m_w_out, m_ln1_g, m_ln1_b, m_ca_wq, m_ca_wkv, m_ca_wo, m_ln2_g, m_ln2_b, m_ffn_w_up, m_ffn_conv_w, m_ffn_conv_b, m_ffn_w_down, m_ln3_g, m_ln3_b, v_w_in, v_b_in, v_hg_lb_logits, v_hg_norm_w, v_ml_conv_w, v_ml_conv_b, v_ml_norm_w, v_w_out, v_ln1_g, v_ln1_b, v_ca_wq, v_ca_wkv, v_ca_wo, v_ln2_g, v_ln2_b, v_ffn_w_up, v_ffn_conv_w, v_ffn_conv_b, v_ffn_w_down, v_ln3_g, v_ln3_b):
    return _train_step(dict(locals()))


WEIGHTS = ("w_in", "b_in", "hg_lb_logits", "hg_norm_w", "ml_conv_w", "ml_conv_b", "ml_norm_w", "w_out", "ln1_g",
           "ln1_b", "ca_wq", "ca_wkv", "ca_wo", "ln2_g", "ln2_b", "ffn_w_up", "ffn_conv_w", "ffn_conv_b",
           "ffn_w_down", "ln3_g", "ln3_b")
MATRICES = ("w_in", "w_out", "ca_wq", "ca_wkv", "ca_wo", "ffn_w_up", "ffn_w_down")
COL_SHARDED = ("w_in", "ca_wkv", "ffn_w_up", "ml_conv_w", "ffn_conv_w")
SMALL = tuple(n for n in WEIGHTS if n not in MATRICES)
PART_ROWS = 16


def _part_rows(shape):
    n = 1
    for s in shape:
        n *= s
    return -(-n // (LANES * PART_ROWS)) * PART_ROWS


def _pack(arrs, dtype):
    parts = []
    for a in arrs:
        flat = a.reshape(-1).astype(dtype)
        flat = jnp.pad(flat, (0, _part_rows(a.shape) * LANES - flat.shape[0]))
        parts.append(flat.reshape(-1, LANES))
    return jnp.concatenate(parts, axis=0)


def _unpack(buf, shapes):
    lead = buf.shape[:-2]
    outs, r = [], 0
    for sh in shapes:
        n = 1
        for s in sh:
            n *= s
        nr = _part_rows(sh)
        flat = buf[..., r:r + nr, :].reshape(lead + (nr * LANES,))
        outs.append(flat[..., :n].reshape(lead + tuple(sh)))
        r += nr
    return outs


def _cat_cols(s):
    return jnp.moveaxis(s, 0, 1).reshape(s.shape[1], -1)


def _stack_rows(s):
    return s.reshape(-1, s.shape[-1])


def _train_step(a):
    xs, mems, tgt = a["x"][0], a["mem"][0], a["loss_target"][0]
    core = lax.axis_index("c").astype(jnp.int32).reshape(1)
    chip = (2 * lax.axis_index("x") + lax.axis_index("y")).astype(jnp.int32).reshape(1)
    k_me = chip[0]
    shard = {n: a[n][0] for n in MATRICES}

    later = [n for n in MATRICES if n != "w_in"]
    w_in, taps = _gather_weights([shard["w_in"].astype(BF16), _pack([a["ml_conv_w"][0], a["ffn_conv_w"][0]], F32)])
    w = {"w_in": jnp.concatenate([*w_in, jnp.zeros((D_MODEL, D_IN_PAD - D_IN), BF16)], axis=1)}
    gathering, token = _gather_start("gather_start", [shard[n].astype(BF16) for n in later])
    ml_cw, ffn_cw = [_cat_cols(s) for s in _unpack(taps, [a["ml_conv_w"].shape[1:], a["ffn_conv_w"].shape[1:]])]
    b_in_p = jnp.pad(a["b_in"], ((0, 0), (0, D_IN_PAD - D_IN))) + token[0:1, 0:1]
    mixer_w = (a["hg_lb_logits"], a["hg_norm_w"], ml_cw, a["ml_conv_b"], a["ml_norm_w"])
    up_cols = a["ffn_w_up"].shape[-1]

    proj, xb = _mm("proj", "nn", xs, w["w_in"], bias=b_in_p, a_copy_dtype=BF16, tm=256, tn=D_IN_PAD, fuse_b=True)
    y, hst, cst, nst, mst = _mixer_fwd(proj, *mixer_w)
    w.update(zip(later, _forward_halves("forward_halves", _gather_wait("gather_wait", gathering, y))))
    for n in ("w_out", "ca_wq", "ca_wo", "ffn_w_down"):
        w[n] = _stack_rows(w[n])
    z1, x1, x1b = _mm("mix_out", "nn", y, w["w_out"], res=xs, res_scale=ALPHA, ln=("fwd", a["ln1_g"], a["ln1_b"]),
                      copy_dtype=BF16)
    q = _mm("ca_q", "nn", x1b, w["ca_wq"], out_dtype=BF16, tn=D_MODEL)
    kv = _mm("ca_kv", "nn", mems, w["ca_wkv"])
    o = _attn_fwd(q, kv)
    z2, x2, x2b = _mm("ca_out", "nn", o, w["ca_wo"], res=x1, res_scale=ALPHA, ln=("fwd", a["ln2_g"], a["ln2_b"]),
                      copy_dtype=BF16)
    w_up = w["ffn_w_up"]
    assert w_up.shape == (2 * FFN_J, D_MODEL, FFN_W)
    u, hmid, dz3, g_ln3g, g_ln3b, loss_part, dz3b = _ffn_fwd(
        x2b, x2, w_up, ffn_cw, a["ffn_conv_b"], w["ffn_w_down"], a["ln3_g"], a["ln3_b"], tgt)

    grads = {"ln3_g": g_ln3g, "ln3_b": g_ln3b}
    grads["ffn_w_down"] = _mm("g_w_down", "tn", hmid, dz3b, tm=D_FF // 2, tn=D_MODEL)
    du, g_cw, g_cb, dz2, grads["ln2_g"], grads["ln2_b"], dz2b = _ffn_bwd(
        u, ffn_cw, a["ffn_conv_b"], dz3b, dz3, w["ffn_w_down"], w_up, z2, a["ln2_g"], a["ln2_b"])
    grads["ffn_conv_w"] = jnp.transpose(g_cw, (2, 1, 0, 3)).reshape(FFN_CONV, 2 * D_FF)
    grads["ffn_conv_b"] = jnp.transpose(g_cb, (2, 1, 0, 3)).reshape(1, 2 * D_FF)
    grads["ffn_w_up"] = _mm("g_w_up", "tn", x2b, du, out_groups=N_CHIPS, tm=D_MODEL, tn=up_cols)
    grads["ffn_w_down"] = grads["ffn_w_down"].reshape((N_CHIPS,) + shard["ffn_w_down"].shape)
    pending = {}

    def reduce_start(tag, names, swapped=None):
        group = [grads[n] for n in names]
        group, theirs = swapped or (group, _swap_halves("swap_halves_" + tag, group))
        sums = [_add_pair("add_pair_" + n, core, chip, g, t) for n, g, t in zip(names, group, theirs)]
        handle, token = _scatter_start("scatter_start_" + tag, [s16 for _, s16 in sums])
        pending[tag] = (names, [s32 for s32, _ in sums], handle)
        return token[0:1, 0:1]

    ffn = ("ffn_w_up", "ffn_w_down")
    swapping, token = _swap_start("swap_start_ffn", [grads[n] for n in ffn])
    do = _mm("d_o", "nt", dz2b, w["ca_wo"], bias=jnp.zeros((1, D_MODEL), F32) + token[0:1, 0:1], out_dtype=BF16,
             tn=D_MODEL)
    grads["ca_wo"] = _mm_tn_streamed("g_wo", o, dz2b)
    zero = reduce_start("ffn", ffn, _swap_wait("swap_wait_ffn", swapping, grads["ca_wo"]))
    dq, dkv = _attn_bwd(q, kv + zero, do)
    grads["ca_wq"] = _mm_tn_streamed("g_wq", x1b, dq)
    grads["ca_wkv"] = _mm("g_wkv", "tn", mems, dkv, out_groups=N_CHIPS, tm=D_MODEL)
    dz1, grads["ln1_g"], grads["ln1_b"], dz1b = _mm("d_x1", "nt", dq, w["ca_wq"], res=dz2, res_scale=ALPHA,
                                                    ln=("bwd", z1, a["ln1_g"], a["ln1_b"]), copy_dtype=BF16)
    grads["w_out"] = _mm_tn_streamed("g_w_out", y, dz1b)
    for n in ("w_out", "ca_wq", "ca_wo"):
        grads[n] = grads[n].reshape((N_CHIPS,) + shard[n].shape)
    attn = ("w_out", "ca_wq", "ca_wkv", "ca_wo")
    swapping, token = _swap_start("swap_start_attn", [grads[n] for n in attn])
    dy = _mm("d_y", "nt", dz1b, w["w_out"], bias=jnp.zeros((1, D_MODEL), F32) + token[0:1, 0:1], tn=D_MODEL)
    zero = reduce_start("attn", attn, _swap_wait("swap_wait_attn", swapping, dy))
    (dproj, g_b_in, grads["hg_lb_logits"], grads["hg_norm_w"], grads["ml_conv_w"], grads["ml_conv_b"],
     grads["ml_norm_w"]) = _mixer_bwd(proj, dy, hst, cst, nst, mst, mixer_w[0], mixer_w[1] + zero, *mixer_w[2:])
    g_in = _mm("g_w_in", "tn", xb, dproj, tm=D_MODEL, tn=up_cols)
    in_cols = D_IN // N_CHIPS
    grads["w_in"] = jnp.stack([g_in[:, k * in_cols:(k + 1) * in_cols] for k in range(N_CHIPS)])
    grads["b_in"] = g_b_in[:, :D_IN]
    zero = reduce_start("in", ("w_in",))
    dx = _mm("d_x", "nt", dproj, w["w_in"], bias=jnp.zeros((1, D_MODEL), F32) + zero, res=dz1, res_scale=ALPHA,
             tm=256, tn=D_MODEL, fuse_b=True)

    halves = {}
    for tag, (names, sums32, handle) in pending.items():
        for n, s32, r in zip(names, sums32, _scatter_wait("scatter_wait_" + tag, handle, dx)):
            halves[n] = _add_chips("add_chips_" + n, chip, s32, r)
    halves = [halves[n] for n in MATRICES]

    small_shapes = [grads[n].shape for n in SMALL] + [loss_part.shape]
    other_halves, summed = _share_and_reduce(halves, _pack([grads[n] for n in SMALL] + [loss_part], F32))
    summed = _unpack(summed, small_shapes)
    loss = summed[-1][0, 0]
    for n, g in zip(SMALL, summed[:-1]):
        if n in COL_SHARDED:
            cols = a[n].shape[-1]
            g = lax.dynamic_slice_in_dim(g, k_me * cols, cols, axis=1)
        grads[n] = g

    delta, new_m, new_v = {}, {}, {}
    for n, mine, theirs in zip(MATRICES, halves, other_halves):
        grads[n], delta[n], new_m[n], new_v[n] = _adamw_halves(
            "adamw_" + n, core, shard[n], mine, theirs, a["m_" + n][0], a["v_" + n][0])
    small_w = [a[n][0] if a[n].ndim == 3 else a[n] for n in SMALL]
    small_m = [a["m_" + n][0] if a[n].ndim == 3 else a["m_" + n] for n in SMALL]
    small_v = [a["v_" + n][0] if a[n].ndim == 3 else a["v_" + n] for n in SMALL]
    for out, vals in zip((delta, new_m, new_v),
                         _adamw_many("adamw_small", small_w, [grads[n] for n in SMALL], small_m, small_v)):
        out.update(zip(SMALL, vals))

    def shaped(d):
        return [d[n].reshape(a[n].shape) for n in WEIGHTS]
    return (loss, dx[None], *shaped(grads), *shaped(delta), *shaped(new_m), *shaped(new_v))
```

```python
import functools

import jax
import jax.numpy as jnp
from jax import lax
from jax.experimental import pallas as pl
from jax.experimental.pallas import tpu as pltpu

F32 = jnp.float32
BF16 = jnp.bfloat16

D_MODEL = 1024
HEADS = 4
DK = 128
D_GRP = HEADS * DK
CHUNK = 64
ML_CONV = 4
FFN_CONV = 3
D_FF = 2816
CA_DH = D_MODEL // HEADS
DEPTH = 1
ALPHA = (2.0 * DEPTH) ** 0.25
LN_EPS = 1e-5
NEG_BIG = -1e30
D_IN = 8 * D_GRP + 2 * HEADS
D_IN_PAD = 8 * D_GRP + 128
ADAM_LR, ADAM_B1, ADAM_B2, ADAM_EPS, ADAM_WD, ADAM_STEP = 0.001, 0.9, 0.999, 1e-08, 0.01, 10

SUBLANES = 8
LANES = 128
VMEM_BYTES = 64 * 1024 * 1024


def _pcall(body, pin=True, **kw):
    if not pin:
        return _call(body, **kw)
    kw["out_shape"] = jax.tree.map(lambda s: pltpu.HBM(s.shape, s.dtype), kw["out_shape"])
    call = _call(body, **kw)

    def pinned(*args):
        return call(*[pltpu.with_memory_space_constraint(x, pltpu.HBM) if jnp.issubdtype(x.dtype, jnp.floating) else x
                      for x in args])
    return pinned


def _call(body, **kw):
    return pl.pallas_call(body, **kw)


def _params(semantics, vmem_bytes):
    limit = int(min(max(2 * vmem_bytes, 16 * 1024 * 1024), VMEM_BYTES - 8 * 1024 * 1024))
    return pltpu.CompilerParams(dimension_semantics=semantics, vmem_limit_bytes=limit)


def _nbytes(shape, dtype):
    n = 1
    for s in shape:
        n *= s
    return n * jnp.dtype(dtype).itemsize


def _dg(a, b, ca, cb):
    return lax.dot_general(a.astype(BF16), b.astype(BF16), (((ca,), (cb,)), ((), ())),
                           preferred_element_type=F32)


@jax.custom_vjp
def mm_nn(a, b):
    return _dg(a, b, 1, 0)


mm_nn.defvjp(lambda a, b: (_dg(a, b, 1, 0), (a, b)),
             lambda r, g: (_dg(g, r[1], 1, 1).astype(r[0].dtype), _dg(r[0], g, 0, 0).astype(r[1].dtype)))


@jax.custom_vjp
def mm_nt(a, b):
    return _dg(a, b, 1, 1)


mm_nt.defvjp(lambda a, b: (_dg(a, b, 1, 1), (a, b)),
             lambda r, g: (_dg(g, r[1], 1, 0).astype(r[0].dtype), _dg(g, r[0], 0, 0).astype(r[1].dtype)))


@jax.custom_vjp
def mm_tn(a, b):
    return _dg(a, b, 0, 0)


mm_tn.defvjp(lambda a, b: (_dg(a, b, 0, 0), (a, b)),
             lambda r, g: (_dg(r[1], g, 1, 1).astype(r[0].dtype), _dg(r[0], g, 1, 0).astype(r[1].dtype)))


def _tri(n, lower):
    r = lax.broadcasted_iota(jnp.int32, (n, n), 0)
    c = lax.broadcasted_iota(jnp.int32, (n, n), 1)
    return ((r >= c) if lower else (r <= c)).astype(F32)


def _tri_dot(lower, x):
    t = _tri(x.shape[0], lower).astype(BF16)
    hi = x.astype(BF16)
    rest = x - hi.astype(F32)
    mid = rest.astype(BF16)
    lo = (rest - mid.astype(F32)).astype(BF16)
    return sum(lax.dot_general(t, p, (((1,), (0,)), ((), ())), preferred_element_type=F32) for p in (hi, mid, lo))


@jax.custom_vjp
def cumsum_rows(x):
    return _tri_dot(True, x)


cumsum_rows.defvjp(lambda x: (_tri_dot(True, x), None), lambda _, g: (_tri_dot(False, g),))


def _shift_impl(halo, x, d):
    xx = jnp.concatenate([halo, x], axis=0)
    return pltpu.roll(xx, d, 0)[SUBLANES:]


@functools.partial(jax.custom_vjp, nondiff_argnums=(2,))
def shift_rows(halo, x, d):
    return _shift_impl(halo, x, d)


def _shift_bwd(d, _, g):
    n = g.shape[0] + SUBLANES
    gg = jnp.concatenate([jnp.zeros((SUBLANES, g.shape[1]), g.dtype), g], axis=0)
    r = pltpu.roll(gg, n - d, 0)
    return r[:SUBLANES], r[SUBLANES:]


shift_rows.defvjp(lambda halo, x, d: (_shift_impl(halo, x, d), None), _shift_bwd)


def causal_conv(halo, x, w_rows, b):
    k = len(w_rows)
    y = b + w_rows[k - 1] * x
    for d in range(1, k):
        y = y + w_rows[k - 1 - d] * shift_rows(halo, x, d)
    return y


def _sigmoid(x):
    return 1.0 / (1.0 + jnp.exp(-x))


def _silu(x):
    return x * _sigmoid(x)


def _log_sigmoid(x):
    return jnp.minimum(x, 0.0) - jnp.log(1.0 + jnp.exp(-jnp.abs(x)))


def _pick_row(x, i):
    row = lax.broadcasted_iota(jnp.int32, (x.shape[0], 1), 0)
    return jnp.sum(jnp.where(row == i, x, 0.0), axis=0, keepdims=True)


def _layer_norm(z, g, b):
    mu = jnp.mean(z, axis=-1, keepdims=True)
    zc = z - mu
    var = jnp.mean(zc * zc, axis=-1, keepdims=True)
    return zc * lax.rsqrt(var + LN_EPS) * g + b


def _qk_conv(halo, x, w0, w1, w2, w3, b):
    return _silu(causal_conv(halo, x, (w0, w1, w2, w3), b))


def _grp(i):
    return pl.ds(i * D_GRP, D_GRP)


def _mixer_specs(n_chunks, reverse):
    def chunk(c):
        return n_chunks - 1 - c if reverse else c
    row8 = CHUNK // SUBLANES
    proj_spec = pl.BlockSpec((CHUNK, D_IN_PAD), lambda c: (chunk(c), 0))
    halo_spec = pl.BlockSpec((SUBLANES, 2 * D_GRP), lambda c: (jnp.maximum(chunk(c) * row8 - 1, 0), 2))
    small = [pl.BlockSpec((2, D_GRP), lambda c: (0, 0)), pl.BlockSpec((1, D_GRP), lambda c: (0, 0)),
             pl.BlockSpec((ML_CONV, 2 * D_GRP), lambda c: (0, 0)), pl.BlockSpec((1, 2 * D_GRP), lambda c: (0, 0)),
             pl.BlockSpec((1, D_GRP), lambda c: (0, 0))]
    state_specs = [pl.BlockSpec((1, HEADS, DK, DK), lambda c: (chunk(c), 0, 0, 0)),
                   pl.BlockSpec((1, HEADS, DK, DK), lambda c: (chunk(c), 0, 0, 0)),
                   pl.BlockSpec((1, HEADS, 1, DK), lambda c: (chunk(c), 0, 0, 0)),
                   pl.BlockSpec((1, HEADS, 1, DK), lambda c: (chunk(c), 0, 0, 0))]
    y_spec = pl.BlockSpec((CHUNK, 2 * D_GRP), lambda c: (chunk(c), 0))
    return proj_spec, halo_spec, small, state_specs, y_spec, chunk


def _heads(x):
    return [x[:, h * DK:(h + 1) * DK] for h in range(HEADS)]


def _last(x, j):
    lane = lax.broadcasted_iota(jnp.int32, (1, x.shape[-1]), 1)
    return jnp.sum(jnp.where(lane == j, x, 0.0), axis=-1, keepdims=True)


def _hg_chunk(st_t, hq, hf, hi, hgate, l0, l1, nw):
    n = hq.shape[0]
    lb = _sigmoid(l0 - l1)
    q = _silu(hq)
    lf = jnp.log(lb + (1.0 - lb) * _sigmoid(hf))
    k = (1.0 - lb) * _sigmoid(-hf)
    b = cumsum_rows(lf)
    b_ref = _pick_row(b, n // 2 - 1)
    b_last = _pick_row(b, n - 1)
    qa, ka =_heads(q * jnp.exp(b - b_ref)), _heads(k * jnp.exp(b_ref - b))
    qe, kd, eb, v = _heads(q * jnp.exp(b)), _heads(k * jnp.exp(b_last - b)), _heads(jnp.exp(b_last)), _heads(hi)
    tri = _tri(n, True) > 0
    attn = [jnp.where(tri, mm_nt(qa[h], ka[h]), 0.0) for h in range(HEADS)]
    o = [mm_nn(attn[h], v[h]) + mm_nt(qe[h], st_t[h]) for h in range(HEADS)]
    st_new = jnp.stack([eb[h] * st_t[h] + mm_tn(v[h], kd[h]) for h in range(HEADS)])
    yn = [o[h] * lax.rsqrt(jnp.mean(o[h] * o[h], axis=-1, keepdims=True) + LN_EPS) for h in range(HEADS)]
    return st_new, jnp.concatenate(yn, axis=1) * nw * _silu(hgate)


def _ml_chunk(c_st, n_st, m_st, q, k, v, gates, og, nw):
    n = q.shape[0]
    ig = jnp.stack([_last(gates, h) for h in range(HEADS)])
    log_f = _log_sigmoid(gates)
    fl = jnp.stack([_last(log_f, HEADS + h) for h in range(HEADS)])
    bw = cumsum_rows(jnp.concatenate([jnp.broadcast_to(fl[h], (n, DK)) for h in range(HEADS)], axis=1))
    b = jnp.stack([_last(x, 0) for x in _heads(bw)])
    g = jnp.sum(fl, axis=1, keepdims=True)
    eye = lax.broadcasted_iota(jnp.int32, (n, n), 0) == lax.broadcasted_iota(jnp.int32, (n, n), 1)
    e_row = jnp.sum(jnp.where(eye, ig - b, 0.0), axis=1, keepdims=True)
    d = jnp.where(_tri(n, True) > 0, b + e_row, -jnp.inf)
    inter = b + m_st
    m_t = jnp.maximum(inter, jnp.max(d, axis=2, keepdims=True))
    qs, kh, vh = _heads(q * (DK ** -0.5)), _heads(k), _heads(v)
    s = jnp.stack([mm_nt(qs[h], kh[h]) for h in range(HEADS)]) * jnp.exp(d - m_t)
    w_inter = jnp.exp(inter - m_t)
    num = (jnp.stack([mm_nn(s[h], vh[h]) for h in range(HEADS)])
           + w_inter * jnp.stack([mm_nn(qs[h], c_st[h]) for h in range(HEADS)]))
    den = jnp.sum(s, axis=2, keepdims=True) + w_inter * jnp.sum(jnp.stack(qs) * n_st, axis=2, keepdims=True)
    h_out = num / jnp.maximum(jnp.abs(den), jnp.exp(-m_t))
    a = g - b + ig
    m_new = jnp.maximum(g + m_st, jnp.max(a, axis=1, keepdims=True))
    decay = jnp.exp(g + m_st - m_new)
    wk = jnp.stack(kh) * jnp.exp(a - m_new)
    c_new = decay * c_st + jnp.stack([mm_tn(wk[h], vh[h]) for h in range(HEADS)])
    n_new = decay * n_st + jnp.sum(wk, axis=1, keepdims=True)
    hc = h_out - jnp.mean(h_out, axis=-1, keepdims=True)
    yn = hc * lax.rsqrt(jnp.mean(hc * hc, axis=-1, keepdims=True) + LN_EPS)
    y = _sigmoid(og) * (jnp.concatenate([yn[h] for h in range(HEADS)], axis=1) * nw)
    return c_new, n_new, m_new, y


def _mixer_inputs(proj_ref, lg_ref, hnw_ref, mnw_ref, qk):
    hg_in = (proj_ref[:, _grp(0)], proj_ref[:, _grp(1)], proj_ref[:, _grp(2)], proj_ref[:, _grp(3)],
             lg_ref[0:1, :], lg_ref[1:2, :], hnw_ref[...])
    ml_in = (qk[:, :D_GRP], qk[:, D_GRP:], proj_ref[:, _grp(6)], proj_ref[:, pl.ds(8 * D_GRP, LANES)],
             proj_ref[:, _grp(7)], mnw_ref[...])
    return hg_in, ml_in


def _mixer_fwd(proj, lb_logits, hg_nw, conv_w, conv_b, ml_nw):
    seq = proj.shape[0]
    n_chunks = seq // CHUNK
    proj_spec, halo_spec, small, state_specs, y_spec, _ = _mixer_specs(n_chunks, False)

    def body(proj_ref, halo_ref, lg_ref, hnw_ref, cw_ref, cb_ref, mnw_ref,
             y_ref, hst_ref, cst_ref, nst_ref, mst_ref, hs, cs, ns, ms):
        c = pl.program_id(0)

        @pl.when(c == 0)
        def _():
            hs[...] = jnp.zeros_like(hs)
            cs[...] = jnp.zeros_like(cs)
            ns[...] = jnp.zeros_like(ns)
            ms[...] = jnp.full(ms.shape, NEG_BIG, F32)

        hst_ref[0] = hs[...]
        cst_ref[0] = cs[...]
        nst_ref[0] = ns[...]
        mst_ref[0] = ms[...]
        halo = jnp.where(c > 0, halo_ref[...], 0.0)
        qk = _qk_conv(halo, proj_ref[:, pl.ds(4 * D_GRP, 2 * D_GRP)],
                      cw_ref[0:1, :], cw_ref[1:2, :], cw_ref[2:3, :], cw_ref[3:4, :], cb_ref[...])
        hg_in, ml_in = _mixer_inputs(proj_ref, lg_ref, hnw_ref, mnw_ref, qk)
        hs[...], y_hg = _hg_chunk(hs[...], *hg_in)
        cs[...], ns[...], m_new, y_ml = _ml_chunk(cs[...], ns[...], _last(ms[...], 0), *ml_in)
        ms[...] = jnp.broadcast_to(m_new, ms.shape)
        y_ref[:, pl.ds(0, D_GRP)] = y_hg.astype(BF16)
        y_ref[:, pl.ds(D_GRP, D_GRP)] = y_ml.astype(BF16)

    st = jax.ShapeDtypeStruct((n_chunks, HEADS, DK, DK), F32)
    vec = jax.ShapeDtypeStruct((n_chunks, HEADS, 1, DK), F32)
    vmem = 2 * (_nbytes((CHUNK, D_IN_PAD), F32) + _nbytes((CHUNK, 2 * D_GRP), F32) + 2 * _nbytes((HEADS, DK, DK), F32)) \
        + 2 * _nbytes((HEADS, DK, DK), F32)
    return _pcall(
        body, name="mixer_fwd", grid=(n_chunks,),
        in_specs=[proj_spec, halo_spec] + small,
        out_specs=[y_spec] + state_specs,
        out_shape=[jax.ShapeDtypeStruct((seq, 2 * D_GRP), BF16), st, st, vec, vec],
        scratch_shapes=[pltpu.VMEM((HEADS, DK, DK), F32), pltpu.VMEM((HEADS, DK, DK), F32),
                        pltpu.VMEM((HEADS, 1, DK), F32), pltpu.VMEM((HEADS, 1, DK), F32)],
        compiler_params=_params(("arbitrary",), vmem),
    )(proj, proj, lb_logits, hg_nw, conv_w, conv_b, ml_nw)


def _mixer_bwd(proj, dy, hst, cst, nst, mst, lb_logits, hg_nw, conv_w, conv_b, ml_nw):
    seq = proj.shape[0]
    n_chunks = seq // CHUNK
    proj_spec, halo_spec, small, state_specs, y_spec, _ = _mixer_specs(n_chunks, True)

    def body(proj_ref, halo_ref, dy_ref, hst_ref, cst_ref, nst_ref, mst_ref,
             lg_ref, hnw_ref, cw_ref, cb_ref, mnw_ref,
             dproj_ref, dbin_ref, dlg_ref, dhnw_ref, dcw_ref, dcb_ref, dmnw_ref,
             dhs, dcs, dns, dms, dhalo):
        c = pl.program_id(0)

        @pl.when(c == 0)
        def _():
            for r in (dhs, dcs, dns, dms, dhalo, dbin_ref, dlg_ref, dhnw_ref, dcw_ref, dcb_ref, dmnw_ref):
                r[...] = jnp.zeros_like(r)

        def put(cols, val):
            dproj_ref[:, cols] = val.astype(BF16)
            dbin_ref[:, cols] += jnp.sum(val, axis=0, keepdims=True)

        first = c == n_chunks - 1
        halo = jnp.where(first, 0.0, halo_ref[...])
        x_qk = proj_ref[:, pl.ds(4 * D_GRP, 2 * D_GRP)]
        conv_args = (halo, x_qk, cw_ref[0:1, :], cw_ref[1:2, :], cw_ref[2:3, :], cw_ref[3:4, :], cb_ref[...])
        qk, conv_vjp = jax.vjp(_qk_conv, *conv_args)
        hg_in, ml_in = _mixer_inputs(proj_ref, lg_ref, hnw_ref, mnw_ref, qk)
        _, hg_vjp = jax.vjp(_hg_chunk, hst_ref[0], *hg_in)
        _, ml_vjp = jax.vjp(_ml_chunk, cst_ref[0], nst_ref[0], _last(mst_ref[0], 0), *ml_in)
        dst, dhq, dhf, dhi, dhg, dl0, dl1, dnw = hg_vjp((dhs[...], dy_ref[:, pl.ds(0, D_GRP)]))
        dc, dn, dm, dq, dk, dv, dgates, dog, dmn = ml_vjp(
            (dcs[...], dns[...], _last(dms[...], 0), dy_ref[:, pl.ds(D_GRP, D_GRP)]))
        dhs[...] = dst
        dcs[...] = dc
        dns[...] = dn
        dms[...] = jnp.broadcast_to(dm, dms.shape)
        for i, val in ((0, dhq), (1, dhf), (2, dhi), (3, dhg), (6, dv), (7, dog)):
            put(_grp(i), val)
        put(pl.ds(8 * D_GRP, LANES), dgates)
        dlg_ref[0:1, :] += dl0
        dlg_ref[1:2, :] += dl1
        dhnw_ref[...] += dnw
        dmnw_ref[...] += dmn
        dh, dx, dw0, dw1, dw2, dw3, db = conv_vjp(jnp.concatenate([dq, dk], axis=1))
        tail = jnp.concatenate([jnp.zeros((CHUNK - SUBLANES, 2 * D_GRP), F32), dhalo[...]], axis=0)
        put(pl.ds(4 * D_GRP, 2 * D_GRP), dx + tail)
        dhalo[...] = dh
        for d, dw in enumerate((dw0, dw1, dw2, dw3)):
            dcw_ref[d:d + 1, :] += dw
        dcb_ref[...] += db

    row = pl.BlockSpec((1, D_GRP), lambda c: (0, 0))
    small_out = [pl.BlockSpec((1, D_IN_PAD), lambda c: (0, 0)), pl.BlockSpec((2, D_GRP), lambda c: (0, 0)), row,
                 pl.BlockSpec((ML_CONV, 2 * D_GRP), lambda c: (0, 0)), pl.BlockSpec((1, 2 * D_GRP), lambda c: (0, 0)), row]
    dy_spec = pl.BlockSpec((CHUNK, 2 * D_GRP), y_spec.index_map)
    vmem = 2 * (2 * _nbytes((CHUNK, D_IN_PAD), F32) + _nbytes((CHUNK, 2 * D_GRP), F32)
                + 2 * _nbytes((HEADS, DK, DK), F32)) + 2 * _nbytes((HEADS, DK, DK), F32) + 4 * 1024 * 1024
    return _pcall(
        body, name="mixer_bwd", grid=(n_chunks,),
        in_specs=[proj_spec, halo_spec, dy_spec] + state_specs + small,
        out_specs=[proj_spec] + small_out,
        out_shape=[jax.ShapeDtypeStruct((seq, D_IN_PAD), BF16), jax.ShapeDtypeStruct((1, D_IN_PAD), F32),
                   jax.ShapeDtypeStruct((2, D_GRP), F32), jax.ShapeDtypeStruct((1, D_GRP), F32),
                   jax.ShapeDtypeStruct((ML_CONV, 2 * D_GRP), F32), jax.ShapeDtypeStruct((1, 2 * D_GRP), F32),
                   jax.ShapeDtypeStruct((1, D_GRP), F32)],
        scratch_shapes=[pltpu.VMEM((HEADS, DK, DK), F32), pltpu.VMEM((HEADS, DK, DK), F32),
                        pltpu.VMEM((HEADS, 1, DK), F32), pltpu.VMEM((HEADS, 1, DK), F32),
                        pltpu.VMEM((SUBLANES, 2 * D_GRP), F32)],
        compiler_params=_params(("arbitrary",), vmem),
    )(proj, proj, dy, hst, cst, nst, mst, lb_logits, hg_nw, conv_w, conv_b, ml_nw)


def _tile(n, prefs, unit=None):
    unit = unit or n
    for p in prefs:
        if unit % p == 0 and n % p == 0:
            return p
    return unit


def _logical(arr):
    return arr.shape if arr.ndim == 2 else (arr.shape[1], arr.shape[0] * arr.shape[2])


def _group(arr):
    return arr.shape[-1]


def _split_spec(ndim, group, tr, tc, where):
    if ndim == 2:
        return pl.BlockSpec((tr, tc), where)
    per = group // tc
    assert per * tc == group, (group, tc)

    def index(*ids):
        bi, bj = where(*ids)
        return (bj // per, bi, bj % per)
    return pl.BlockSpec((None, tr, tc), index)


def _mm(name, mode, a, b, *, bias=None, res=None, res_scale=1.0, ln=None, out_dtype=F32, out_groups=None,
        copy_dtype=None, a_copy_dtype=None, tm=None, tn=None, tk=None):
    la, lb = _logical(a), _logical(b)
    if mode == "nn":
        (m, k), n = la, lb[1]
        n_unit = _group(b) if b.ndim == 3 else n
        kc = _group(a) if a.ndim == 3 else k
    elif mode == "nt":
        (m, k), n = la, lb[0]
        n_unit = n
        kc = min(_group(a) if a.ndim == 3 else k, _group(b) if b.ndim == 3 else k)
    else:
        (k, m), n = la, lb[1]
        n_unit, kc = (_group(b) if b.ndim == 3 else n), k
        assert a.ndim == 2
    if out_groups:
        n_unit = min(n_unit, n // out_groups)
    kind = ln[0] if ln else None
    tm = tm or (256 if ln else _tile(m, (512, 256, 128)))
    tn = n if ln else (tn or _tile(n, (512, 384, 256, 128), n_unit))
    if mode != "tn":
        tk = k
    elif tk is None:
        tk = _tile(k, (4096, 2048, 512, 256, 128) if (m // tm) * (n // tn) > 1 else (2048, 512, 256, 128))
    gi, gj, gk = m // tm, n // tn, k // tk
    assert gi * tm == m and gj * tn == n and gk * tk == k and n_unit % tn == 0, (name, m, n, k, tm, tn, tk)
    ca, cb = {"nn": (1, 0), "nt": (1, 1), "tn": (0, 0)}[mode]
    i_outer = gk > 1 or (gi - 1) * _nbytes(b.shape, b.dtype) <= (gj - 1) * _nbytes(a.shape, a.dtype)

    def ij(where):
        return (lambda p, q, kk: where(p, q, kk)) if i_outer else (lambda p, q, kk: where(q, p, kk))
    if mode == "tn":
        a_spec = pl.BlockSpec((tk, tm), ij(lambda i, j, kk: (kk, i)))
    elif a.ndim == 3:
        a_spec = pl.BlockSpec((a.shape[0], tm, _group(a)), ij(lambda i, j, kk: (0, i, 0)))
    else:
        a_spec = pl.BlockSpec((tm, k), ij(lambda i, j, kk: (i, 0)))
    if mode != "nt":
        b_spec = _split_spec(b.ndim, _group(b), tk, tn, ij(lambda i, j, kk: (kk, j)))
    elif b.ndim == 3:
        b_spec = pl.BlockSpec((b.shape[0], tn, _group(b)), ij(lambda i, j, kk: (0, j, 0)))
    else:
        b_spec = pl.BlockSpec((tn, k), ij(lambda i, j, kk: (j, 0)))
    row_spec = pl.BlockSpec((1, tn), ij(lambda i, j, kk: (0, j)))
    blk_spec = pl.BlockSpec((tm, tn), ij(lambda i, j, kk: (i, j)))
    ins, in_specs = [a, b], [a_spec, b_spec]
    if bias is not None:
        ins.append(bias), in_specs.append(row_spec)
    if res is not None:
        ins.append(res), in_specs.append(blk_spec)
    if kind == "fwd":
        ins += [ln[1], ln[2]]
        in_specs += [row_spec, row_spec]
    elif kind == "bwd":
        ins += [ln[1], ln[2], ln[3]]
        in_specs += [blk_spec, row_spec, row_spec]
    if out_groups:
        blk_out = jax.ShapeDtypeStruct((out_groups, m, n // out_groups), out_dtype)
        out_spec = _split_spec(3, n // out_groups, tm, tn, ij(lambda i, j, kk: (i, j)))
    else:
        blk_out, out_spec = jax.ShapeDtypeStruct((m, n), out_dtype), blk_spec
    row_out = jax.ShapeDtypeStruct((1, n), F32)
    if kind is None:
        out_shape, out_specs = [blk_out], [out_spec]
    elif kind == "fwd":
        out_shape, out_specs = [blk_out, blk_out], [blk_spec, blk_spec]
    else:
        out_shape, out_specs = [blk_out, row_out, row_out], [blk_spec, row_spec, row_spec]
    if copy_dtype is not None:
        out_shape.append(jax.ShapeDtypeStruct((m, n), copy_dtype))
        out_specs.append(blk_spec)
    if a_copy_dtype is not None:
        assert mode != "tn" and a.ndim == 2 and copy_dtype is None
        out_shape.append(jax.ShapeDtypeStruct((m, k), a_copy_dtype))
        out_specs.append(a_spec)
    n_in = len(ins)

    def body(*refs):
        in_refs, out_refs, acc_ref = refs[:n_in], refs[n_in:n_in + len(out_shape)], refs[-1]
        i, kk = pl.program_id(0 if i_outer else 1), pl.program_id(2)
        a_ref, b_ref = in_refs[:2]
        extra = list(in_refs[2:])
        if a_copy_dtype is not None:
            out_refs[-1][...] = a_ref[...].astype(a_copy_dtype)

        def epilogue(acc):
            rest = list(extra)
            if bias is not None:
                acc = acc + rest.pop(0)[...]
            if res is not None:
                acc = acc + res_scale * rest.pop(0)[...]
            if kind is None:
                out_refs[0][...] = acc.astype(out_dtype)
                return
            if kind == "fwd":
                out_refs[0][...] = acc
                y = _layer_norm(acc, rest[0][...], rest[1][...])
                out_refs[1][...] = y
                if copy_dtype is not None:
                    out_refs[-1][...] = y.astype(copy_dtype)
                return
            _, vjp = jax.vjp(_layer_norm, rest[0][...], rest[1][...], rest[2][...])
            dz, dg, db = vjp(acc)
            out_refs[0][...] = dz
            out_refs[1][...] += dg
            out_refs[2][...] += db
            if copy_dtype is not None:
                out_refs[-1][...] = dz.astype(copy_dtype)

        if kind == "bwd":
            @pl.when((i == 0) & (kk == 0))
            def _():
                out_refs[1][...] = jnp.zeros_like(out_refs[1])
                out_refs[2][...] = jnp.zeros_like(out_refs[2])

        def chunk(ref, c0, last):
            if ref.ndim == 3:
                g = ref.shape[2]
                return ref[c0 // g, :, pl.ds(c0 % g, kc)]
            return ref[:, pl.ds(c0, kc)] if last else ref[pl.ds(c0, kc), :]

        if mode == "tn" or kc == k:
            prod = _dg(a_ref[...], b_ref[...], ca, cb)
        else:
            prod = None
            for c0 in range(0, k, kc):
                part = _dg(chunk(a_ref, c0, True), chunk(b_ref, c0, mode == "nt"), ca, cb)
                prod = part if prod is None else prod + part
        if gk == 1:
            epilogue(prod)
            return

        @pl.when(kk == 0)
        def _():
            acc_ref[...] = prod

        @pl.when(kk > 0)
        def _():
            acc_ref[...] += prod

        @pl.when(kk == gk - 1)
        def _():
            epilogue(acc_ref[...])

    vmem = (2 * (_nbytes((tm, tk), a.dtype) + _nbytes((tk, tn), b.dtype))
            + (2 * len(ins) + 2 * len(out_shape) + 1) * _nbytes((tm, tn), F32))
    outs = _pcall(
        body, name=name, grid=(gi, gj, gk) if i_outer else (gj, gi, gk), in_specs=in_specs, out_specs=out_specs,
        out_shape=out_shape, scratch_shapes=[pltpu.VMEM((tm, tn) if gk > 1 else (SUBLANES, LANES), F32)],
        compiler_params=_params(("arbitrary", "arbitrary", "arbitrary"), vmem),
    )(*ins)
    return outs[0] if len(out_shape) == 1 else outs


STREAM_ROWS = 512
STREAM_AHEAD = 3


def _mm_tn_streamed(name, a, b):
    k, m = a.shape
    n = b.shape[1]
    nk = k // STREAM_ROWS
    assert nk * STREAM_ROWS == k and b.shape[0] == k

    def body(a_hbm, b_hbm, o_hbm, a_vm, b_vm, o_vm, sem, out_sem):
        def copies(i):
            rows = pl.ds(i * STREAM_ROWS, STREAM_ROWS)
            return (pltpu.make_async_copy(a_hbm.at[rows], a_vm.at[rows], sem.at[0, i]),
                    pltpu.make_async_copy(b_hbm.at[rows], b_vm.at[rows], sem.at[1, i]))

        def start(i):
            for queue, cp in enumerate(copies(i)):
                cp.start(priority=queue)

        for i in range(min(STREAM_AHEAD, nk)):
            start(i)
        for i in range(nk):
            for cp in copies(i):
                cp.wait()
            if i + STREAM_AHEAD < nk:
                start(i + STREAM_AHEAD)
            rows = pl.ds(i * STREAM_ROWS, STREAM_ROWS)
            prod = _dg(a_vm[rows, :], b_vm[rows, :], 0, 0)
            if i == 0:
                o_vm[...] = prod
            else:
                o_vm[...] += prod
        out = pltpu.make_async_copy(o_vm, o_hbm, out_sem.at[0])
        out.start()
        out.wait()

    vmem = _nbytes(a.shape, a.dtype) + _nbytes(b.shape, b.dtype) + 2 * _nbytes((m, n), F32)
    return _pcall(
        body, name=name, in_specs=[ANY, ANY], out_specs=ANY, out_shape=jax.ShapeDtypeStruct((m, n), F32),
        scratch_shapes=[pltpu.VMEM(a.shape, a.dtype), pltpu.VMEM(b.shape, b.dtype), pltpu.VMEM((m, n), F32),
                        pltpu.SemaphoreType.DMA((2, nk)), pltpu.SemaphoreType.DMA((1,))],
        compiler_params=pltpu.CompilerParams(vmem_limit_bytes=int(vmem + 8 * 1024 * 1024)),
    )(a, b)


def _attn_head(q, k, v):
    sc = mm_nt(q, k) * (CA_DH ** -0.5)
    e = jnp.exp(sc - jnp.max(sc, axis=-1, keepdims=True))
    return mm_nn(e / jnp.sum(e, axis=-1, keepdims=True), v)


def _attn_fwd(q, kv):
    seq, n_mem = q.shape[0], kv.shape[0]
    tq = _tile(seq, (512, 256, 128))

    def body(q_ref, kv_ref, o_ref):
        for h in range(HEADS):
            hd = pl.ds(h * CA_DH, CA_DH)
            o = _attn_head(q_ref[:, hd], kv_ref[:, hd], kv_ref[:, pl.ds(D_MODEL + h * CA_DH, CA_DH)])
            o_ref[:, hd] = o.astype(BF16)

    return _pcall(
        body, name="attn_fwd", grid=(seq // tq,),
        in_specs=[pl.BlockSpec((tq, D_MODEL), lambda i: (i, 0)), pl.BlockSpec((n_mem, 2 * D_MODEL), lambda i: (0, 0))],
        out_specs=pl.BlockSpec((tq, D_MODEL), lambda i: (i, 0)), out_shape=jax.ShapeDtypeStruct((seq, D_MODEL), BF16),
        compiler_params=_params(("arbitrary",), 4 * _nbytes((tq, D_MODEL), F32) + 2 * _nbytes((n_mem, 2 * D_MODEL), F32)),
    )(q, kv)


def _attn_bwd(q, kv, do):
    seq, n_mem = q.shape[0], kv.shape[0]
    tq = _tile(seq, (512, 256, 128))

    def body(q_ref, kv_ref, do_ref, dq_ref, dkv_ref):
        @pl.when(pl.program_id(0) == 0)
        def _():
            dkv_ref[...] = jnp.zeros_like(dkv_ref)

        for h in range(HEADS):
            hd = pl.ds(h * CA_DH, CA_DH)
            vd = pl.ds(D_MODEL + h * CA_DH, CA_DH)
            _, vjp = jax.vjp(_attn_head, q_ref[:, hd], kv_ref[:, hd], kv_ref[:, vd])
            dq, dk, dv = vjp(do_ref[:, hd].astype(F32))
            dq_ref[:, hd] = dq.astype(BF16)
            dkv_ref[:, hd] += dk
            dkv_ref[:, vd] += dv

    return _pcall(
        body, name="attn_bwd", grid=(seq // tq,),
        in_specs=[pl.BlockSpec((tq, D_MODEL), lambda i: (i, 0)), pl.BlockSpec((n_mem, 2 * D_MODEL), lambda i: (0, 0)),
                  pl.BlockSpec((tq, D_MODEL), lambda i: (i, 0))],
        out_specs=[pl.BlockSpec((tq, D_MODEL), lambda i: (i, 0)), pl.BlockSpec((n_mem, 2 * D_MODEL), lambda i: (0, 0))],
        out_shape=[jax.ShapeDtypeStruct((seq, D_MODEL), BF16), jax.ShapeDtypeStruct((n_mem, 2 * D_MODEL), F32)],
        compiler_params=_params(("arbitrary",), 6 * _nbytes((tq, D_MODEL), F32) + 4 * _nbytes((n_mem, 2 * D_MODEL), F32)),
    )(q, kv, do)


def _ffn_mid(hg, xg, hv, xv, wg0, wg1, wg2, bg, wv0, wv1, wv2, bv):
    return jax.nn.gelu(causal_conv(hg, xg, (wg0, wg1, wg2), bg)) * causal_conv(hv, xv, (wv0, wv1, wv2), bv)


FFN_TB = 256
FFN_W = D_FF // 2
FFN_J = D_FF // FFN_W
MXU_COLS = 256
FFN_PIECES = tuple((off, min(MXU_COLS, FFN_W - off)) for off in range(0, FFN_W, MXU_COLS))


def _ffn_common_specs(seq, row):
    tb = min(FFN_TB, seq)
    full = pl.BlockSpec((tb, D_MODEL), lambda t, j: (row(t), 0))
    vec = pl.BlockSpec((1, D_MODEL), lambda t, j: (0, 0))
    halves = []
    for off in (0, FFN_J):
        halves.append(dict(
            w_up=pl.BlockSpec((None, D_MODEL, FFN_W), lambda t, j, off=off: (j + off, 0, 0)),
            taps=pl.BlockSpec((FFN_CONV, FFN_W), lambda t, j, off=off: (0, j + off)),
            bias=pl.BlockSpec((1, FFN_W), lambda t, j, off=off: (0, j + off))))
    w_down = pl.BlockSpec((FFN_W, D_MODEL), lambda t, j: (j, 0))
    u_blk = pl.BlockSpec((2, tb, FFN_W), lambda t, j: (0, row(t), j))
    return tb, full, vec, halves, w_down, u_blk


def _ffn_vmem(tb):
    return (_nbytes((2, tb, FFN_W), F32) + _nbytes((2, tb, FFN_W), BF16) + 3 * _nbytes((D_MODEL, FFN_W), BF16)
            + 10 * _nbytes((tb, D_MODEL), F32))


def _conv_params(taps_ref, bias_ref, cols):
    return taps_ref[0:1, cols], taps_ref[1:2, cols], taps_ref[2:3, cols], bias_ref[:, cols]


def _ffn_fwd(x2b, x2, w_up, conv_w, conv_b, w_down, ln_g, ln_b, target):
    seq = x2.shape[0]
    tb, full, vec, halves, wd_spec, u_blk = _ffn_common_specs(seq, lambda t: t)
    nt = seq // tb

    def body(xb_ref, wg_ref, wv_ref, tg_ref, tv_ref, bg_ref, bv_ref, wd_ref, x_ref, g_ref, b_ref, tgt_ref,
             u_ref, h_ref, dz_ref, dg_ref, db_ref, loss_ref, dzb_ref, acc, carry):
        t, j = pl.program_id(0), pl.program_id(1)
        xb = xb_ref[...]
        pieces = [pl.ds(off, width) for off, width in FFN_PIECES]
        ug = [_dg(xb, wg_ref[:, cols], 1, 0) for cols in pieces]
        uv = [_dg(xb, wv_ref[:, cols], 1, 0) for cols in pieces]
        hs = []
        for cols, g, v in zip(pieces, ug, uv):
            u_ref[0, :, cols] = g
            u_ref[1, :, cols] = v
            halo_g = jnp.where(t == 0, 0.0, carry[j, 0, :, cols])
            halo_v = jnp.where(t == 0, 0.0, carry[j, 1, :, cols])
            h = _ffn_mid(halo_g, g, halo_v, v, *_conv_params(tg_ref, bg_ref, cols),
                         *_conv_params(tv_ref, bv_ref, cols)).astype(BF16)
            carry[j, 0, :, cols] = g[tb - SUBLANES:, :]
            carry[j, 1, :, cols] = v[tb - SUBLANES:, :]
            h_ref[:, cols] = h
            hs.append(h)
        part = None
        for cols, h in zip(pieces, hs):
            p = _dg(h, wd_ref[cols, :], 1, 0)
            part = p if part is None else part + p

        @pl.when(j == 0)
        def _():
            acc[...] = part

        @pl.when(j > 0)
        def _():
            acc[...] += part

        @pl.when(j == FFN_J - 1)
        def _():
            y, vjp = jax.vjp(_layer_norm, acc[...] + ALPHA * x_ref[...], g_ref[...], b_ref[...])
            err = y - tgt_ref[...]
            part_loss = 0.5 * jnp.sum(jnp.sum(err * err, axis=1, keepdims=True), axis=0, keepdims=True) / D_MODEL
            dz, dg, db = vjp(err / D_MODEL)

            @pl.when(t == 0)
            def _():
                for r in (dg_ref, db_ref, loss_ref):
                    r[...] = jnp.zeros_like(r)

            dz_ref[...] = dz
            dzb_ref[...] = dz.astype(BF16)
            dg_ref[...] += dg
            db_ref[...] += db
            loss_ref[...] += jnp.broadcast_to(part_loss, (1, LANES))

    h0, h1 = halves
    row = jax.ShapeDtypeStruct((1, D_MODEL), F32)
    return _pcall(
        body, name="ffn_fwd", grid=(nt, FFN_J),
        in_specs=[full, h0["w_up"], h1["w_up"], h0["taps"], h1["taps"], h0["bias"], h1["bias"], wd_spec, full, vec, vec,
                  full],
        out_specs=[u_blk, pl.BlockSpec((tb, FFN_W), lambda t, j: (t, j)), full, vec, vec,
                   pl.BlockSpec((1, LANES), lambda t, j: (0, 0)), full],
        out_shape=[jax.ShapeDtypeStruct((2, seq, D_FF), F32), jax.ShapeDtypeStruct((seq, D_FF), BF16),
                   jax.ShapeDtypeStruct((seq, D_MODEL), F32), row, row, jax.ShapeDtypeStruct((1, LANES), F32),
                   jax.ShapeDtypeStruct((seq, D_MODEL), BF16)],
        scratch_shapes=[pltpu.VMEM((tb, D_MODEL), F32), pltpu.VMEM((FFN_J, 2, SUBLANES, FFN_W), F32)],
        compiler_params=_params(("arbitrary", "arbitrary"), _ffn_vmem(tb)),
    )(x2b, w_up, w_up, conv_w, conv_w, conv_b, conv_b, w_down, x2, ln_g, ln_b, target)


def _ffn_bwd(u, conv_w, conv_b, dz3b, dz3, w_down, w_up, z2, ln_g, ln_b):
    seq = dz3.shape[0]
    tb = min(FFN_TB, seq)
    nt = seq // tb
    row8 = tb // SUBLANES
    tb, full, vec, halves, wd_spec, u_blk = _ffn_common_specs(seq, lambda t: nt - 1 - t)
    halo = pl.BlockSpec((2, SUBLANES, FFN_W), lambda t, j: (0, jnp.maximum((nt - 1 - t) * row8 - 1, 0), j))

    def body(u_ref, halo_ref, tg_ref, tv_ref, bg_ref, bv_ref, dzb_ref, wd_ref, wg_ref, wv_ref, dz3_ref, z_ref, g_ref,
             b_ref, du_ref, dw_ref, dbias_ref, dz_ref, dg_ref, db_ref, dz2b_ref, acc, carry):
        t, j = pl.program_id(0), pl.program_id(1)

        @pl.when((t == 0) & (j == 0))
        def _():
            for r in (dw_ref, dbias_ref, dg_ref, db_ref):
                r[...] = jnp.zeros_like(r)

        pieces = [pl.ds(off, width) for off, width in FFN_PIECES]
        dzb = dzb_ref[...]
        dhs = [_dg(dzb, wd_ref[cols, :], 1, 1) for cols in pieces]
        first = t == nt - 1
        dus = []
        for cols, dh in zip(pieces, dhs):
            args = (jnp.where(first, 0.0, halo_ref[0, :, cols]), u_ref[0, :, cols],
                    jnp.where(first, 0.0, halo_ref[1, :, cols]), u_ref[1, :, cols],
                    *_conv_params(tg_ref, bg_ref, cols), *_conv_params(tv_ref, bv_ref, cols))
            _, vjp = jax.vjp(_ffn_mid, *args)
            dhg, dxg, dhv, dxv, g0, g1, g2, gb, v0, v1, v2, vb = vjp(dh)
            zeros = jnp.zeros((tb - SUBLANES, dh.shape[1]), F32)
            dug = (dxg + jnp.concatenate([zeros, jnp.where(t == 0, 0.0, carry[j, 0, :, cols])], axis=0)).astype(BF16)
            duv = (dxv + jnp.concatenate([zeros, jnp.where(t == 0, 0.0, carry[j, 1, :, cols])], axis=0)).astype(BF16)
            carry[j, 0, :, cols] = dhg
            carry[j, 1, :, cols] = dhv
            du_ref[0, :, cols] = dug
            du_ref[1, :, cols] = duv
            for half, parts in enumerate(((g0, g1, g2), (v0, v1, v2))):
                for d, p in enumerate(parts):
                    dw_ref[j, half, d:d + 1, cols] += p
            dbias_ref[j, 0, :, cols] += gb
            dbias_ref[j, 1, :, cols] += vb
            dus.append((dug, duv))
        part = None
        for cols, (dug, duv) in zip(pieces, dus):
            p = _dg(dug, wg_ref[:, cols], 1, 1) + _dg(duv, wv_ref[:, cols], 1, 1)
            part = p if part is None else part + p

        @pl.when(j == 0)
        def _():
            acc[...] = part

        @pl.when(j > 0)
        def _():
            acc[...] += part

        @pl.when(j == FFN_J - 1)
        def _():
            _, ln_vjp = jax.vjp(_layer_norm, z_ref[...], g_ref[...], b_ref[...])
            dz, dg, db = ln_vjp(acc[...] + ALPHA * dz3_ref[...])
            dz_ref[...] = dz
            dz2b_ref[...] = dz.astype(BF16)
            dg_ref[...] += dg
            db_ref[...] += db

    h0, h1 = halves
    row = jax.ShapeDtypeStruct((1, D_MODEL), F32)
    whole = lambda *shape: pl.BlockSpec(shape, lambda t, j: (0,) * len(shape))
    return _pcall(
        body, name="ffn_bwd", grid=(nt, FFN_J),
        in_specs=[u_blk, halo, h0["taps"], h1["taps"], h0["bias"], h1["bias"], full, wd_spec, h0["w_up"], h1["w_up"],
                  full, full, vec, vec],
        out_specs=[u_blk, whole(FFN_J, 2, FFN_CONV, FFN_W), whole(FFN_J, 2, 1, FFN_W), full, vec, vec, full],
        out_shape=[jax.ShapeDtypeStruct((2, seq, D_FF), BF16), jax.ShapeDtypeStruct((FFN_J, 2, FFN_CONV, FFN_W), F32),
                   jax.ShapeDtypeStruct((FFN_J, 2, 1, FFN_W), F32), jax.ShapeDtypeStruct((seq, D_MODEL), F32), row, row,
                   jax.ShapeDtypeStruct((seq, D_MODEL), BF16)],
        scratch_shapes=[pltpu.VMEM((tb, D_MODEL), F32), pltpu.VMEM((FFN_J, 2, SUBLANES, FFN_W), F32)],
        compiler_params=_params(("arbitrary", "arbitrary"), _ffn_vmem(tb)),
    )(u, u, conv_w, conv_w, conv_b, conv_b, dz3b, w_down, w_up, w_up, dz3, z2, ln_g, ln_b)


def _adamw_math(w, g, m, v):
    m_new = ADAM_B1 * m + (1.0 - ADAM_B1) * g
    v_new = ADAM_B2 * v + (1.0 - ADAM_B2) * jnp.square(g)
    m_hat = m_new / (1.0 - ADAM_B1 ** ADAM_STEP)
    v_hat = v_new / (1.0 - ADAM_B2 ** ADAM_STEP)
    return -ADAM_LR * (m_hat / (jnp.sqrt(v_hat) + ADAM_EPS) + ADAM_WD * w), m_new, v_new


def _adamw_many(name, ws, gs, ms, vs):
    n = len(ws)

    def body(*refs):
        w_refs, g_refs, m_refs, v_refs = (refs[i * n:(i + 1) * n] for i in range(4))
        d_refs, nm_refs, nv_refs = (refs[(4 + i) * n:(5 + i) * n] for i in range(3))
        for i in range(n):
            d_refs[i][...], nm_refs[i][...], nv_refs[i][...] = _adamw_math(
                w_refs[i][...], g_refs[i][...], m_refs[i][...], v_refs[i][...])

    vm = pl.BlockSpec(memory_space=pltpu.VMEM)
    outs = _pcall(
        body, pin=False, name=name, in_specs=[vm] * (4 * n), out_specs=[vm] * (3 * n),
        out_shape=[jax.ShapeDtypeStruct(w.shape, F32) for w in ws] * 3,
    )(*ws, *gs, *ms, *vs)
    return outs[:n], outs[n:2 * n], outs[2 * n:]


def _adamw_halves(name, core, w, mine, theirs, m, v):
    rows, cols = w.shape
    half_rows = mine.shape[0]
    tr = _tile(half_rows, (256, 176, 128))
    nbh = half_rows // tr
    assert 2 * half_rows == rows

    def body(c_ref, w_ref, a_ref, b_ref, m_ref, v_ref, g_ref, d_ref, nm_ref, nv_ref):
        g = jnp.where(pl.program_id(0) // nbh == c_ref[0], a_ref[...], b_ref[...])
        g_ref[...] = g
        d_ref[...], nm_ref[...], nv_ref[...] = _adamw_math(w_ref[...], g, m_ref[...], v_ref[...])

    spec = pl.BlockSpec((tr, cols), lambda i, c_ref: (i, 0))
    half = pl.BlockSpec((tr, cols), lambda i, c_ref: (i % nbh, 0))
    sh = jax.ShapeDtypeStruct((rows, cols), F32)
    grid_spec = pltpu.PrefetchScalarGridSpec(
        num_scalar_prefetch=1, grid=(rows // tr,), in_specs=[spec, half, half, spec, spec], out_specs=[spec] * 4)
    return _pcall(
        body, name=name, grid_spec=grid_spec, out_shape=[sh] * 4,
        compiler_params=_params(("arbitrary",), 18 * _nbytes((tr, -(-cols // LANES) * LANES), F32)),
    )(core, w, mine, theirs, m, v)


MESH = pl.DeviceIdType.MESH
ANY = pl.BlockSpec(memory_space=pl.ANY)
N_CHIPS = 4
BF16_ROWS = 16


def _me():
    return lax.axis_index("x"), lax.axis_index("y"), lax.axis_index("c")


def _other_chips(x, y):
    return [(1 - x, y), (x, 1 - y), (1 - x, 1 - y)]


def _remote(src, dst, ssem, rsem, dev):
    return pltpu.make_async_remote_copy(src_ref=src, dst_ref=dst, send_sem=ssem, recv_sem=rsem,
                                        device_id=dev, device_id_type=MESH)


def _half_rows(ref_rows, cc):
    half = ref_rows // 2
    return pl.ds(pl.multiple_of(cc * half, BF16_ROWS), half)


def _gather_weights(shards):
    n = len(shards)
    n_ici = n * (N_CHIPS - 1)

    def body(*refs):
        ins, outs, (ssem, rsem, lsem, lrsem) = refs[:n], refs[n:2 * n], refs[2 * n:]
        x, y, c = _me()
        k_me = 2 * x + y
        sib = (x, y, 1 - c)
        chips = _other_chips(x, y)
        started = []
        for i, (w_ref, o_ref) in enumerate(zip(ins, outs)):
            cp = _remote(w_ref, o_ref.at[k_me], lsem.at[i], lrsem.at[i], sib)
            cp.start()
            started.append(cp)
        for r, (px, py) in enumerate(chips):
            for i, (w_ref, o_ref) in enumerate(zip(ins, outs)):
                rows = _half_rows(w_ref.shape[0], c)
                s = r * n + i
                cp = _remote(w_ref.at[rows], o_ref.at[k_me, rows], ssem.at[s], rsem.at[s], (px, py, c))
                cp.start()
                started.append(cp)
        for r, (px, py) in enumerate(chips):
            for i, o_ref in enumerate(outs):
                blk = o_ref.at[2 * px + py, _half_rows(o_ref.shape[1], c)]
                s = r * n + i
                _remote(blk, blk, ssem.at[s], rsem.at[s], (px, py, c)).wait_recv()
                cp = _remote(blk, blk, ssem.at[n_ici + s], rsem.at[n_ici + s], sib)
                cp.start()
                started.append(cp)
        for r, (px, py) in enumerate(chips):
            for i, o_ref in enumerate(outs):
                blk = o_ref.at[2 * px + py, _half_rows(o_ref.shape[1], 1 - c)]
                s = n_ici + r * n + i
                _remote(blk, blk, ssem.at[s], rsem.at[s], sib).wait_recv()
        for cp in started[n:]:
            cp.wait_send()
        for cp in started[:n]:
            cp.wait()

    return _pcall(
        body, name="gather_weights", in_specs=[ANY] * n, out_specs=[ANY] * n,
        out_shape=[jax.ShapeDtypeStruct((N_CHIPS,) + s.shape, s.dtype) for s in shards],
        scratch_shapes=[pltpu.SemaphoreType.DMA((2 * n_ici,)), pltpu.SemaphoreType.DMA((2 * n_ici,)),
                        pltpu.SemaphoreType.DMA((n,)), pltpu.SemaphoreType.DMA((n,))],
    )(*shards)


def _swap_halves(name, grads):
    n = len(grads)

    def body(*refs):
        ins, outs, (ssem, rsem) = refs[:n], refs[n:2 * n], refs[2 * n:]
        x, y, c = _me()
        copies = []
        for i, (g_ref, o_ref) in enumerate(zip(ins, outs)):
            for k in range(N_CHIPS):
                s = i * N_CHIPS + k
                cp = _remote(g_ref.at[k, _half_rows(g_ref.shape[1], 1 - c)], o_ref.at[k], ssem.at[s], rsem.at[s],
                             (x, y, 1 - c))
                cp.start()
                copies.append(cp)
        for cp in copies:
            cp.wait()

    return _pcall(
        body, name=name, in_specs=[ANY] * n, out_specs=[ANY] * n,
        out_shape=[jax.ShapeDtypeStruct((N_CHIPS, g.shape[1] // 2, g.shape[2]), g.dtype) for g in grads],
        scratch_shapes=[pltpu.SemaphoreType.DMA((n * N_CHIPS,)), pltpu.SemaphoreType.DMA((n * N_CHIPS,))],
    )(*grads)


SEM = pl.BlockSpec(memory_space=pltpu.SEMAPHORE)
IN_HBM = pl.BlockSpec(memory_space=pltpu.HBM)
SPLIT_PARAMS = dict(compiler_params=pltpu.CompilerParams(has_side_effects=pltpu.SideEffectType.DATAFLOW_SIDE_EFFECTING))


def _split_start(name, sources, landings, n_copies, plan):
    ns, nl = len(sources), len(landings)

    def body(*refs):
        ins, lands, (ssem, rsem), token = refs[:ns], refs[ns:ns + nl], refs[ns + nl:ns + nl + 2], refs[-1]
        for s, (src, dst, _, dev) in enumerate(plan(ins, lands)):
            _remote(src, dst, ssem.at[s], rsem.at[s], dev).start()
        token[...] = jnp.zeros_like(token)

    arrays = list(sources) + list(landings)
    outs = _call(
        body, name=name, in_specs=[IN_HBM] * (ns + nl),
        out_specs=[SEM, SEM] + [IN_HBM] * (ns + nl) + [pl.BlockSpec(memory_space=pltpu.VMEM)],
        out_shape=[pltpu.SemaphoreType.DMA((n_copies,)), pltpu.SemaphoreType.DMA((n_copies,))]
        + [pltpu.HBM(a.shape, a.dtype) for a in arrays] + [jax.ShapeDtypeStruct((SUBLANES, LANES), F32)],
        input_output_aliases={i: 2 + i for i in range(ns + nl)}, **SPLIT_PARAMS,
    )(*[pltpu.with_memory_space_constraint(a, pltpu.HBM) for a in arrays])
    return (outs[:-1], ns), outs[-1]


def _split_wait(name, handle, after, plan):
    (ssem, rsem, *thru), ns = handle
    nl = len(thru) - ns

    def body(*refs):
        ins, lands, (ssem_ref, rsem_ref) = refs[:ns], refs[ns:ns + nl], refs[ns + nl:ns + nl + 2]
        for s, (src, _, dst, dev) in enumerate(plan(ins, lands)):
            cp = _remote(src, dst, ssem_ref.at[s], rsem_ref.at[s], dev)
            cp.wait_send()
            cp.wait_recv()

    outs = _call(
        body, name=name, in_specs=[IN_HBM] * (ns + nl) + [SEM, SEM, ANY], out_specs=[IN_HBM] * (ns + nl),
        out_shape=[pltpu.HBM(t.shape, t.dtype) for t in thru],
        input_output_aliases={i: i for i in range(ns + nl)}, **SPLIT_PARAMS,
    )(*thru, ssem, rsem, after)
    return outs[:ns], outs[ns:]


def _swap_plan(ins, lands):
    x, y, c = _me()
    return [(g_ref.at[k, _half_rows(g_ref.shape[1], 1 - c)], l_ref.at[k], l_ref.at[k], (x, y, 1 - c))
            for g_ref, l_ref in zip(ins, lands) for k in range(N_CHIPS)]


def _swap_start(name, grads):
    lands = [lax.empty((N_CHIPS, g.shape[1] // 2, g.shape[2]), g.dtype) for g in grads]
    return _split_start(name, grads, lands, len(grads) * N_CHIPS, _swap_plan)


def _swap_wait(name, handle, after):
    return _split_wait(name, handle, after, _swap_plan)


def _gather_plan(ins, lands):
    x, y, c = _me()
    k_me = 2 * x + y
    plan = [(w_ref, l_ref.at[k_me], l_ref.at[k_me], (x, y, 1 - c)) for w_ref, l_ref in zip(ins, lands)]
    for px, py in _other_chips(x, y):
        for w_ref, l_ref in zip(ins, lands):
            rows = _half_rows(w_ref.shape[0], c)
            plan.append((w_ref.at[rows], l_ref.at[k_me, rows], l_ref.at[2 * px + py, rows], (px, py, c)))
    return plan


def _gather_start(name, shards):
    lands = [lax.empty((N_CHIPS,) + s.shape, s.dtype) for s in shards]
    return _split_start(name, shards, lands, len(shards) * N_CHIPS, _gather_plan)


def _gather_wait(name, handle, after):
    return _split_wait(name, handle, after, _gather_plan)[1]


def _forward_halves(name, blocks):
    n = len(blocks)
    n_sem = n * (N_CHIPS - 1)

    def body(*refs):
        outs, (ssem, rsem) = refs[n:2 * n], refs[2 * n:]
        x, y, c = _me()
        sib = (x, y, 1 - c)
        chips = _other_chips(x, y)
        sends = []
        for r, (px, py) in enumerate(chips):
            for i, o_ref in enumerate(outs):
                blk = o_ref.at[2 * px + py, _half_rows(o_ref.shape[1], c)]
                cp = _remote(blk, blk, ssem.at[r * n + i], rsem.at[r * n + i], sib)
                cp.start()
                sends.append(cp)
        for r, (px, py) in enumerate(chips):
            for i, o_ref in enumerate(outs):
                blk = o_ref.at[2 * px + py, _half_rows(o_ref.shape[1], 1 - c)]
                _remote(blk, blk, ssem.at[r * n + i], rsem.at[r * n + i], sib).wait_recv()
        for cp in sends:
            cp.wait_send()

    return _pcall(
        body, name=name, in_specs=[ANY] * n, out_specs=[ANY] * n,
        out_shape=[jax.ShapeDtypeStruct(b.shape, b.dtype) for b in blocks],
        input_output_aliases={i: i for i in range(n)},
        scratch_shapes=[pltpu.SemaphoreType.DMA((n_sem,)), pltpu.SemaphoreType.DMA((n_sem,))],
    )(*blocks)


def _scatter_plan(ins, lands):
    x, y, c = _me()
    k_me = 2 * x + y
    return [(p_ref.at[2 * px + py], l_ref.at[k_me], l_ref.at[2 * px + py], (px, py, c))
            for px, py in _other_chips(x, y) for p_ref, l_ref in zip(ins, lands)]


def _scatter_start(name, parts):
    lands = [lax.empty(p.shape, p.dtype) for p in parts]
    return _split_start(name, parts, lands, len(parts) * (N_CHIPS - 1), _scatter_plan)


def _scatter_wait(name, handle, after):
    return _split_wait(name, handle, after, _scatter_plan)[1]


def _share_and_reduce(halves, v):
    n = len(halves)
    rows = v.shape[0]
    half = rows // 2
    assert half % SUBLANES == 0

    def body(*refs):
        ins, v_ref, outs, out_ref = refs[:n], refs[n], refs[n + 1:2 * n + 1], refs[2 * n + 1]
        pair_buf, mine, chip_buf, ssem, rsem, half_ssem, half_rsem = refs[2 * n + 2:]
        x, y, c = _me()
        k_me = 2 * x + y
        sib = (x, y, 1 - c)
        copies = [_remote(r_ref, o_ref, half_ssem.at[i], half_rsem.at[i], sib)
                  for i, (r_ref, o_ref) in enumerate(zip(ins, outs))]
        for cp in copies:
            cp.start()

        def rows_of(cc):
            return pl.ds(pl.multiple_of(cc * half, SUBLANES), half)

        swap = _remote(v_ref.at[rows_of(1 - c)], pair_buf, ssem.at[0], rsem.at[0], sib)
        swap.start()
        swap.wait()
        mine[...] = v_ref[rows_of(c), :] + pair_buf[...]
        chip_buf[k_me] = mine[...]
        sends = [_remote(mine, chip_buf.at[k_me], ssem.at[1 + r], rsem.at[1 + r], (px, py, c))
                 for r, (px, py) in enumerate(_other_chips(x, y))]
        for cp in sends:
            cp.start()
        for r, (px, py) in enumerate(_other_chips(x, y)):
            blk = chip_buf.at[2 * px + py]
            _remote(blk, blk, ssem.at[1 + r], rsem.at[1 + r], (px, py, c)).wait_recv()
        total = chip_buf[0]
        for k in range(1, N_CHIPS):
            total = total + chip_buf[k]
        out_ref[rows_of(c), :] = total
        for cp in sends:
            cp.wait_send()
        share = _remote(out_ref.at[rows_of(c)], out_ref.at[rows_of(c)], ssem.at[N_CHIPS], rsem.at[N_CHIPS], sib)
        share.start()
        got = out_ref.at[rows_of(1 - c)]
        _remote(got, got, ssem.at[N_CHIPS], rsem.at[N_CHIPS], sib).wait_recv()
        share.wait_send()
        for cp in copies:
            cp.wait()

    vm = pl.BlockSpec(memory_space=pltpu.VMEM)
    outs = _call(
        body, name="share_and_reduce", in_specs=[ANY] * n + [vm], out_specs=[ANY] * n + [vm],
        out_shape=[pltpu.HBM(h.shape, h.dtype) for h in halves] + [jax.ShapeDtypeStruct((rows, LANES), F32)],
        scratch_shapes=[pltpu.VMEM((half, LANES), F32), pltpu.VMEM((half, LANES), F32),
                        pltpu.VMEM((N_CHIPS, half, LANES), F32), pltpu.SemaphoreType.DMA((N_CHIPS + 1,)),
                        pltpu.SemaphoreType.DMA((N_CHIPS + 1,)), pltpu.SemaphoreType.DMA((n,)),
                        pltpu.SemaphoreType.DMA((n,))],
        compiler_params=pltpu.CompilerParams(vmem_limit_bytes=32 * 1024 * 1024),
    )(*[pltpu.with_memory_space_constraint(h, pltpu.HBM) for h in halves], v)
    return outs[:n], outs[n]


def _add_pair(name, core, chip, g, theirs):
    _, half, cols = theirs.shape
    tr = _tile(half, (256, 176, 128))
    nb = half // tr

    def body(c_ref, k_ref, g_ref, t_ref, o32_ref, o16_ref):
        s = g_ref[...] + t_ref[...]
        o16_ref[...] = s.astype(BF16)

        @pl.when(pl.program_id(1) == k_ref[0])
        def _():
            o32_ref[...] = s

    spec = pl.BlockSpec((None, tr, cols), lambda i, k, c_ref, k_ref: (k, i, 0))
    grid_spec = pltpu.PrefetchScalarGridSpec(
        num_scalar_prefetch=2, grid=(nb, N_CHIPS),
        in_specs=[pl.BlockSpec((None, tr, cols), lambda i, k, c_ref, k_ref: (k, c_ref[0] * nb + i, 0)), spec],
        out_specs=[pl.BlockSpec((tr, cols), lambda i, k, c_ref, k_ref: (i, 0)), spec])
    return _pcall(
        body, name=name, grid_spec=grid_spec,
        out_shape=[jax.ShapeDtypeStruct((half, cols), F32), jax.ShapeDtypeStruct(theirs.shape, BF16)],
        compiler_params=_params(("arbitrary", "arbitrary"), 8 * _nbytes((tr, cols + LANES), F32)),
    )(core, chip, g, theirs)


def _add_chips(name, chip, p32, recv):
    half, cols = p32.shape
    tr = _tile(half, (256, 176, 128))

    def body(k_ref, p_ref, r0_ref, r1_ref, r2_ref, o_ref):
        o_ref[...] = ((p_ref[...] + r0_ref[...].astype(F32)) + r1_ref[...].astype(F32)) + r2_ref[...].astype(F32)

    def other(r):
        return pl.BlockSpec((None, tr, cols), lambda i, k_ref: (r + (k_ref[0] <= r).astype(jnp.int32), i, 0))
    grid_spec = pltpu.PrefetchScalarGridSpec(
        num_scalar_prefetch=1, grid=(half // tr,),
        in_specs=[pl.BlockSpec((tr, cols), lambda i, k_ref: (i, 0)), other(0), other(1), other(2)],
        out_specs=pl.BlockSpec((tr, cols), lambda i, k_ref: (i, 0)))
    return _pcall(
        body, name=name, grid_spec=grid_spec, out_shape=jax.ShapeDtypeStruct((half, cols), F32),
        compiler_params=_params(("arbitrary",), 10 * _nbytes((tr, cols + LANES), F32)),
    )(chip, p32, recv, recv, recv)


def kernel(x, mem, w_in, b_in, hg_lb_logits, hg_norm_w, ml_conv_w, ml_conv_b, ml_norm_w, w_out, ln1_g, ln1_b, ca_wq, ca_wkv, ca_wo, ln2_g, ln2_b, ffn_w_up, ffn_conv_w, ffn_conv_b, ffn_w_down, ln3_g, ln3_b, loss_target, m_w_in, m_b_in, m_hg_lb_logits, m_hg_norm_w, m_ml_conv_w, m_ml_conv_b, m_ml_norm_w, m_w_out, m_ln1_g, m_ln1_b, m_ca_wq, m_ca_wkv, m_ca_wo, m_ln2_g, m_ln2_b, m_ffn_w_up, m_ffn_conv_w, m_ffn_conv_b, m_ffn_w_down, m_ln3_g, m_ln3_b, v_w_in, v_b_in, v_hg_lb_logits, v_hg_norm_w, v_ml_conv_w, v_ml_conv_b, v_ml_norm_w, v_w_out, v_ln1_g, v_ln1_b, v_ca_wq, v_ca_wkv, v_ca_wo, v_ln2_g, v_ln2_b, v_ffn_w_up, v_ffn_conv_w, v_ffn_conv_b, v_ffn_w_down, v_ln3_g, v_ln3_b):
    return _train_step(dict(locals()))


WEIGHTS = ("w_in", "b_in", "hg_lb_logits", "hg_norm_w", "ml_conv_w", "ml_conv_b", "ml_norm_w", "w_out", "ln1_g",
           "ln1_b", "ca_wq", "ca_wkv", "ca_wo", "ln2_g", "ln2_b", "ffn_w_up", "ffn_conv_w", "ffn_conv_b",
           "ffn_w_down", "ln3_g", "ln3_b")
MATRICES = ("w_in", "w_out", "ca_wq", "ca_wkv", "ca_wo", "ffn_w_up", "ffn_w_down")
COL_SHARDED = ("w_in", "ca_wkv", "ffn_w_up", "ml_conv_w", "ffn_conv_w")
SMALL = tuple(n for n in WEIGHTS if n not in MATRICES)
PART_ROWS = 16


def _part_rows(shape):
    n = 1
    for s in shape:
        n *= s
    return -(-n // (LANES * PART_ROWS)) * PART_ROWS


def _pack(arrs, dtype):
    parts = []
    for a in arrs:
        flat = a.reshape(-1).astype(dtype)
        flat = jnp.pad(flat, (0, _part_rows(a.shape) * LANES - flat.shape[0]))
        parts.append(flat.reshape(-1, LANES))
    return jnp.concatenate(parts, axis=0)


def _unpack(buf, shapes):
    lead = buf.shape[:-2]
    outs, r = [], 0
    for sh in shapes:
        n = 1
        for s in sh:
            n *= s
        nr = _part_rows(sh)
        flat = buf[..., r:r + nr, :].reshape(lead + (nr * LANES,))
        outs.append(flat[..., :n].reshape(lead + tuple(sh)))
        r += nr
    return outs


def _cat_cols(s):
    return jnp.moveaxis(s, 0, 1).reshape(s.shape[1], -1)


def _stack_rows(s):
    return s.reshape(-1, s.shape[-1])


def _train_step(a):
    xs, mems, tgt = a["x"][0], a["mem"][0], a["loss_target"][0]
    core = lax.axis_index("c").astype(jnp.int32).reshape(1)
    chip = (2 * lax.axis_index("x") + lax.axis_index("y")).astype(jnp.int32).reshape(1)
    k_me = chip[0]
    shard = {n: a[n][0] for n in MATRICES}

    later = [n for n in MATRICES if n != "w_in"]
    w_in, taps = _gather_weights([shard["w_in"].astype(BF16), _pack([a["ml_conv_w"][0], a["ffn_conv_w"][0]], F32)])
    w = {"w_in": jnp.concatenate([*w_in, jnp.zeros((D_MODEL, D_IN_PAD - D_IN), BF16)], axis=1)}
    gathering, token = _gather_start("gather_start", [shard[n].astype(BF16) for n in later])
    ml_cw, ffn_cw = [_cat_cols(s) for s in _unpack(taps, [a["ml_conv_w"].shape[1:], a["ffn_conv_w"].shape[1:]])]
    b_in_p = jnp.pad(a["b_in"], ((0, 0), (0, D_IN_PAD - D_IN))) + token[0:1, 0:1]
    mixer_w = (a["hg_lb_logits"], a["hg_norm_w"], ml_cw, a["ml_conv_b"], a["ml_norm_w"])
    up_cols = a["ffn_w_up"].shape[-1]

    proj, xb = _mm("proj", "nn", xs, w["w_in"], bias=b_in_p, a_copy_dtype=BF16, tm=256, tn=D_IN_PAD)
    y, hst, cst, nst, mst = _mixer_fwd(proj, *mixer_w)
    w.update(zip(later, _forward_halves("forward_halves", _gather_wait("gather_wait", gathering, y))))
    for n in ("w_out", "ca_wq", "ca_wo", "ffn_w_down"):
        w[n] = _stack_rows(w[n])
    z1, x1, x1b = _mm("mix_out", "nn", y, w["w_out"], res=xs, res_scale=ALPHA, ln=("fwd", a["ln1_g"], a["ln1_b"]),
                      copy_dtype=BF16)
    q = _mm("ca_q", "nn", x1b, w["ca_wq"], out_dtype=BF16, tn=D_MODEL)
    kv = _mm("ca_kv", "nn", mems, w["ca_wkv"])
    o = _attn_fwd(q, kv)
    z2, x2, x2b = _mm("ca_out", "nn", o, w["ca_wo"], res=x1, res_scale=ALPHA, ln=("fwd", a["ln2_g"], a["ln2_b"]),
                      copy_dtype=BF16)
    w_up = w["ffn_w_up"]
    assert w_up.shape == (2 * FFN_J, D_MODEL, FFN_W)
    u, hmid, dz3, g_ln3g, g_ln3b, loss_part, dz3b = _ffn_fwd(
        x2b, x2, w_up, ffn_cw, a["ffn_conv_b"], w["ffn_w_down"], a["ln3_g"], a["ln3_b"], tgt)

    grads = {"ln3_g": g_ln3g, "ln3_b": g_ln3b}
    grads["ffn_w_down"] = _mm("g_w_down", "tn", hmid, dz3b, tm=D_FF // 2, tn=D_MODEL)
    du, g_cw, g_cb, dz2, grads["ln2_g"], grads["ln2_b"], dz2b = _ffn_bwd(
        u, ffn_cw, a["ffn_conv_b"], dz3b, dz3, w["ffn_w_down"], w_up, z2, a["ln2_g"], a["ln2_b"])
    grads["ffn_conv_w"] = jnp.transpose(g_cw, (2, 1, 0, 3)).reshape(FFN_CONV, 2 * D_FF)
    grads["ffn_conv_b"] = jnp.transpose(g_cb, (2, 1, 0, 3)).reshape(1, 2 * D_FF)
    grads["ffn_w_up"] = _mm("g_w_up", "tn", x2b, du, out_groups=N_CHIPS, tm=D_MODEL, tn=up_cols)
    grads["ffn_w_down"] = grads["ffn_w_down"].reshape((N_CHIPS,) + shard["ffn_w_down"].shape)
    pending = {}

    def reduce_start(tag, names, swapped=None):
        group = [grads[n] for n in names]
        group, theirs = swapped or (group, _swap_halves("swap_halves_" + tag, group))
        sums = [_add_pair("add_pair_" + n, core, chip, g, t) for n, g, t in zip(names, group, theirs)]
        handle, token = _scatter_start("scatter_start_" + tag, [s16 for _, s16 in sums])
        pending[tag] = (names, [s32 for s32, _ in sums], handle)
        return token[0:1, 0:1]

    ffn = ("ffn_w_up", "ffn_w_down")
    swapping, token = _swap_start("swap_start_ffn", [grads[n] for n in ffn])
    do = _mm("d_o", "nt", dz2b, w["ca_wo"], bias=jnp.zeros((1, D_MODEL), F32) + token[0:1, 0:1], out_dtype=BF16,
             tn=D_MODEL)
    grads["ca_wo"] = _mm_tn_streamed("g_wo", o, dz2b)
    zero = reduce_start("ffn", ffn, _swap_wait("swap_wait_ffn", swapping, grads["ca_wo"]))
    dq, dkv = _attn_bwd(q, kv + zero, do)
    grads["ca_wq"] = _mm_tn_streamed("g_wq", x1b, dq)
    grads["ca_wkv"] = _mm("g_wkv", "tn", mems, dkv, out_groups=N_CHIPS, tm=D_MODEL)
    dz1, grads["ln1_g"], grads["ln1_b"], dz1b = _mm("d_x1", "nt", dq, w["ca_wq"], res=dz2, res_scale=ALPHA,
                                                    ln=("bwd", z1, a["ln1_g"], a["ln1_b"]), copy_dtype=BF16)
    grads["w_out"] = _mm_tn_streamed("g_w_out", y, dz1b)
    for n in ("w_out", "ca_wq", "ca_wo"):
        grads[n] = grads[n].reshape((N_CHIPS,) + shard[n].shape)
    attn = ("w_out", "ca_wq", "ca_wkv", "ca_wo")
    swapping, token = _swap_start("swap_start_attn", [grads[n] for n in attn])
    dy = _mm("d_y", "nt", dz1b, w["w_out"], bias=jnp.zeros((1, D_MODEL), F32) + token[0:1, 0:1], tn=D_MODEL)
    zero = reduce_start("attn", attn, _swap_wait("swap_wait_attn", swapping, dy))
    (dproj, g_b_in, grads["hg_lb_logits"], grads["hg_norm_w"], grads["ml_conv_w"], grads["ml_conv_b"],
     grads["ml_norm_w"]) = _mixer_bwd(proj, dy, hst, cst, nst, mst, mixer_w[0], mixer_w[1] + zero, *mixer_w[2:])
    g_in = _mm("g_w_in", "tn", xb, dproj, tm=D_MODEL, tn=up_cols)
    in_cols = D_IN // N_CHIPS
    grads["w_in"] = jnp.stack([g_in[:, k * in_cols:(k + 1) * in_cols] for k in range(N_CHIPS)])
    grads["b_in"] = g_b_in[:, :D_IN]
    zero = reduce_start("in", ("w_in",))
    dx = _mm("d_x", "nt", dproj, w["w_in"], bias=jnp.zeros((1, D_MODEL), F32) + zero, res=dz1, res_scale=ALPHA,
             tm=256, tn=D_MODEL)

    halves = {}
    for tag, (names, sums32, handle) in pending.items():
        for n, s32, r in zip(names, sums32, _scatter_wait("scatter_wait_" + tag, handle, dx)):
            halves[n] = _add_chips("add_chips_" + n, chip, s32, r)
    halves = [halves[n] for n in MATRICES]

    small_shapes = [grads[n].shape for n in SMALL] + [loss_part.shape]
    other_halves, summed = _share_and_reduce(halves, _pack([grads[n] for n in SMALL] + [loss_part], F32))
    summed = _unpack(summed, small_shapes)
    loss = summed[-1][0, 0]
    for n, g in zip(SMALL, summed[:-1]):
        if n in COL_SHARDED:
            cols = a[n].shape[-1]
            g = lax.dynamic_slice_in_dim(g, k_me * cols, cols, axis=1)
        grads[n] = g

    delta, new_m, new_v = {}, {}, {}
    for n, mine, theirs in zip(MATRICES, halves, other_halves):
        grads[n], delta[n], new_m[n], new_v[n] = _adamw_halves(
            "adamw_" + n, core, shard[n], mine, theirs, a["m_" + n][0], a["v_" + n][0])
    small_w = [a[n][0] if a[n].ndim == 3 else a[n] for n in SMALL]
    small_m = [a["m_" + n][0] if a[n].ndim == 3 else a["m_" + n] for n in SMALL]
    small_v = [a["v_" + n][0] if a[n].ndim == 3 else a["v_" + n] for n in SMALL]
    for out, vals in zip((delta, new_m, new_v),
                         _adamw_many("adamw_small", small_w, [grads[n] for n in SMALL], small_m, small_v)):
        out.update(zip(SMALL, vals))

    def shaped(d):
        return [d[n].reshape(a[n].shape) for n in WEIGHTS]
    return (loss, dx[None], *shaped(grads), *shaped(delta), *shaped(new_m), *shaped(new_v))
```
